```python
import jax, jax.numpy as jnp
from jax import lax
import numpy as np

D_MODEL = 1024
BATCH = 8
SEQ = 2048
DEPTH = 1

CHUNK = 64
GLA_HEADS = 4
GLA_DK = 64
GLA_DV = 128
GLA_QK_W = GLA_HEADS * GLA_DK
GLA_V_W = GLA_HEADS * GLA_DV
GLA_LOWRANK = 16
GLA_GATE_NORM = 16.0
ML_HEADS = 4
ML_DH = 128
ML_W = ML_HEADS * ML_DH
ML_QKV_BLOCK = 4
CONV_K = 4
D_FF = 4 * D_MODEL
EPS = 1e-6
IN_SPLITS = (GLA_QK_W, GLA_QK_W, GLA_V_W, GLA_V_W, GLA_LOWRANK, ML_W, ML_W, D_MODEL, D_MODEL)
D_IN = sum(IN_SPLITS)

kernel_name = "hybrid_gla_mlstm_sandwich_block"


def _rmsnorm(x, g):
    xf = x.astype(jnp.float32)
    y = xf * lax.rsqrt(jnp.mean(xf * xf, axis=-1, keepdims=True) + EPS)
    return (y * g.astype(jnp.float32)).astype(x.dtype)


def _headnorm(x, g):
    xh = x.reshape(*x.shape[:-1], ML_HEADS, ML_DH)
    mu = jnp.mean(xh, axis=-1, keepdims=True)
    var = jnp.mean(jnp.square(xh - mu), axis=-1, keepdims=True)
    y = ((xh - mu) * lax.rsqrt(var + EPS)).reshape(x.shape)
    return y * g.astype(jnp.float32)


def _to_chunks(x, n_heads):
    b, t, w = x.shape
    return x.astype(jnp.float32).reshape(b, t // CHUNK, CHUNK, n_heads, w // n_heads).transpose(0, 3, 1, 2, 4)


def _from_chunks(x):
    b, h, nc, c, d = x.shape
    return x.transpose(0, 2, 3, 1, 4).reshape(b, nc * c, h * d)


def _causal_conv(x, w, bias):
    y = lax.conv_general_dilated(x, w[:, None, :].astype(x.dtype), window_strides=(1,),
                                 padding=[(CONV_K - 1, 0)],
                                 dimension_numbers=('NWC', 'WIO', 'NWC'),
                                 feature_group_count=x.shape[-1])
    return y + bias


def _blockdiag(x, w):
    b, t, n = x.shape
    xb = x.reshape(b, t, n // ML_QKV_BLOCK, ML_QKV_BLOCK)
    return jnp.einsum('btgj,gji->btgi', xb, w).reshape(b, t, n)


def _gla(q, k, v, g, a_low, w_a_up, b_a_up, g_norm):
    b = q.shape[0]
    log_a = jax.nn.log_sigmoid((a_low @ w_a_up + b_a_up).astype(jnp.float32)) / GLA_GATE_NORM
    qc = _to_chunks(q, GLA_HEADS) * GLA_DK ** -0.5
    kc = _to_chunks(k, GLA_HEADS)
    vc = _to_chunks(v, GLA_HEADS)
    cum = jnp.cumsum(_to_chunks(log_a, GLA_HEADS), axis=3)
    e_pos = jnp.exp(cum)
    e_neg = jnp.exp(-cum)
    a_fwd = jnp.einsum('bhntk,bhnsk->bhnts', qc * e_pos, kc * e_neg)
    a_bwd = jnp.einsum('bhntk,bhnsk->bhnts', qc * e_neg, kc * e_pos)
    idx = jnp.arange(CHUNK)
    scores = jnp.where(idx[:, None] >= idx[None, :], a_fwd, a_bwd)
    o = jnp.einsum('bhnts,bhnsv->bhntv', scores, vc)
    cum_last = cum[..., -1:, :]
    s_chunk = jnp.einsum('bhnsk,bhnsv->bhnkv', kc * jnp.exp(cum_last - cum), vc)
    decay = jnp.exp(cum_last[..., 0, :])

    def step(s, inp):
        d, sc = inp
        return d[..., None] * s + sc, s

    s0 = jnp.zeros((b, GLA_HEADS, GLA_DK, GLA_DV), jnp.float32)
    _, s_prev = lax.scan(step, s0, (jnp.moveaxis(decay, 2, 0), jnp.moveaxis(s_chunk, 2, 0)))
    o = o + jnp.einsum('bhntk,bhnkv->bhntv', qc * e_pos, jnp.moveaxis(s_prev, 0, 2))
    o = _rmsnorm(o, g_norm)
    return (_from_chunks(o) * jax.nn.silu(g.astype(jnp.float32))).astype(q.dtype)


def _mlstm(x_m, o_pre, conv_w, conv_b, w_q, w_k, w_v, w_if, b_if, skip, g_norm):
    b, t, _ = x_m.shape
    nc = t // CHUNK
    xc = jax.nn.silu(_causal_conv(x_m, conv_w, conv_b))
    q = _blockdiag(xc, w_q)
    k = _blockdiag(xc, w_k)
    v = _blockdiag(x_m, w_v)
    gates = (jnp.concatenate([q, k, v], axis=-1) @ w_if + b_if).astype(jnp.float32)
    gates = gates.reshape(b, nc, CHUNK, 2 * ML_HEADS).transpose(0, 3, 1, 2)
    log_i = gates[:, :ML_HEADS]
    log_f = jax.nn.log_sigmoid(gates[:, ML_HEADS:])
    f_cum = jnp.cumsum(log_f, axis=-1)
    qc = _to_chunks(q, ML_HEADS)
    kc = _to_chunks(k, ML_HEADS) * ML_DH ** -0.5
    vc = _to_chunks(v, ML_HEADS)
    f_last = f_cum[..., -1]
    a = f_last[..., None] - f_cum + log_i
    m_loc = jnp.max(a, axis=-1)
    kw = kc * jnp.exp(a - m_loc[..., None])[..., None]
    c_chunk = jnp.einsum('bhnsk,bhnsv->bhnkv', kw, vc)
    n_chunk = jnp.sum(kw, axis=3)

    def step(carry, inp):
        c, n, m = carry
        fl, ml, cc, ncnk = inp
        m_new = jnp.maximum(fl + m, ml)
        sp = jnp.exp(fl + m - m_new)
        sl = jnp.exp(ml - m_new)
        c_new = sp[..., None, None] * c + sl[..., None, None] * cc
        n_new = sp[..., None] * n + sl[..., None] * ncnk
        return (c_new, n_new, m_new), (c, n, m)

    init = (jnp.zeros((b, ML_HEADS, ML_DH, ML_DH), jnp.float32),
            jnp.zeros((b, ML_HEADS, ML_DH), jnp.float32),
            jnp.zeros((b, ML_HEADS), jnp.float32))
    xs = (jnp.moveaxis(f_last, 2, 0), jnp.moveaxis(m_loc, 2, 0),
          jnp.moveaxis(c_chunk, 2, 0), jnp.moveaxis(n_chunk, 2, 0))
    _, (c_prev, n_prev, m_prev) = lax.scan(step, init, xs)
    c_prev = jnp.moveaxis(c_prev, 0, 2)
    n_prev = jnp.moveaxis(n_prev, 0, 2)
    m_prev = jnp.moveaxis(m_prev, 0, 2)
    log_d = log_i[..., None, :] - jnp.abs(f_cum[..., :, None] - f_cum[..., None, :])
    g_inter = f_cum + m_prev[..., None]
    m_t = jnp.maximum(g_inter, jnp.max(log_d, axis=-1))
    s = jnp.einsum('bhntk,bhnsk->bhnts', qc, kc) * jnp.exp(log_d - m_t[..., None])
    sc = jnp.exp(g_inter - m_t)
    num = jnp.einsum('bhnts,bhnsv->bhntv', s, vc) + sc[..., None] * jnp.einsum('bhntk,bhnkv->bhntv', qc, c_prev)
    den = jnp.sum(s, axis=-1) + sc * jnp.einsum('bhntk,bhnk->bhnt', qc, n_prev)
    den = jnp.maximum(jnp.abs(den), jnp.exp(-m_t))
    h_cell = _from_chunks(num / den[..., None]) * jax.nn.sigmoid(o_pre.astype(jnp.float32))
    h = _headnorm(h_cell, g_norm) + skip.astype(jnp.float32) * xc.astype(jnp.float32)
    return h.astype(x_m.dtype)


def _fwd_setup_inputs(seed: int = 0) -> dict:
    key = jax.random.key(seed)
    ks = jax.random.split(key, 24)
    L = DEPTH

    def nrm(k, shape, scale):
        return jax.random.normal(k, shape, jnp.float32) * scale

    def gain(k, n):
        return 1.0 + nrm(k, (L, n), 0.02)

    bi = nrm(ks[12], (L, 2 * ML_HEADS), 0.1)
    b_if = jnp.concatenate([bi[:, :ML_HEADS], jnp.linspace(3.0, 6.0, ML_HEADS)[None, :] + bi[:, ML_HEADS:]], axis=-1)
    return {
        "x": nrm(ks[0], (BATCH, SEQ, D_MODEL), 1.0),
        "g_pre_mix": gain(ks[1], D_MODEL),
        "w_in": nrm(ks[2], (L, D_MODEL, D_IN), D_MODEL ** -0.5),
        "w_a_up": nrm(ks[3], (L, GLA_LOWRANK, GLA_QK_W), GLA_LOWRANK ** -0.5),
        "b_a_up": nrm(ks[4], (L, GLA_QK_W), 0.1),
        "g_gla_norm": gain(ks[5], GLA_DV),
        "conv_w": nrm(ks[6], (L, CONV_K, ML_W), CONV_K ** -0.5),
        "conv_b": nrm(ks[7], (L, ML_W), 0.02),
        "w_q_ml": nrm(ks[8], (L, ML_W // ML_QKV_BLOCK, ML_QKV_BLOCK, ML_QKV_BLOCK), ML_QKV_BLOCK ** -0.5),
        "w_k_ml": nrm(ks[9], (L, ML_W // ML_QKV_BLOCK, ML_QKV_BLOCK, ML_QKV_BLOCK), ML_QKV_BLOCK ** -0.5),
        "w_v_ml": nrm(ks[10], (L, ML_W // ML_QKV_BLOCK, ML_QKV_BLOCK, ML_QKV_BLOCK), ML_QKV_BLOCK ** -0.5),
        "w_if": nrm(ks[11], (L, 3 * ML_W, 2 * ML_HEADS), (3 * ML_W) ** -0.5),
        "b_if": b_if,
        "ml_skip": gain(ks[13], ML_W),
        "g_ml_norm": gain(ks[14], ML_W),
        "w_pa": nrm(ks[15], (L, GLA_V_W, D_MODEL), GLA_V_W ** -0.5),
        "w_pb": nrm(ks[16], (L, ML_W, D_MODEL), ML_W ** -0.5),
        "w_o": nrm(ks[17], (L, D_MODEL, D_MODEL), D_MODEL ** -0.5),
        "g_post_mix": gain(ks[18], D_MODEL),
        "g_pre_mlp": gain(ks[19], D_MODEL),
        "w_up": nrm(ks[20], (L, D_MODEL, D_FF), D_MODEL ** -0.5),
        "w_down": nrm(ks[21], (L, D_FF, D_MODEL), D_FF ** -0.5),
        "g_post_mlp": gain(ks[22], D_MODEL),
    }


def _fwd_reference(x, g_pre_mix, w_in, w_a_up, b_a_up, g_gla_norm, conv_w, conv_b, w_q_ml, w_k_ml, w_v_ml,
              w_if, b_if, ml_skip, g_ml_norm, w_pa, w_pb, w_o, g_post_mix, g_pre_mlp, w_up, w_down,
              g_post_mlp):
    split_at = [int(s) for s in np.cumsum(IN_SPLITS)[:-1]]
    for l in range(DEPTH):
        h = _rmsnorm(x, g_pre_mix[l])
        proj = h @ w_in[l]
        q_a, k_a, v_a, g_a, a_low, x_m, o_pre, gate_a, gate_b = jnp.split(proj, split_at, axis=-1)
        y_a = _gla(q_a, k_a, v_a, g_a, a_low, w_a_up[l], b_a_up[l], g_gla_norm[l]) @ w_pa[l]
        y_b = _mlstm(x_m, o_pre, conv_w[l], conv_b[l], w_q_ml[l], w_k_ml[l], w_v_ml[l],
                     w_if[l], b_if[l], ml_skip[l], g_ml_norm[l]) @ w_pb[l]
        merged = jax.nn.sigmoid(gate_a) * y_a + jax.nn.sigmoid(gate_b) * y_b
        x = x + _rmsnorm(merged @ w_o[l], g_post_mix[l])
        h2 = _rmsnorm(x, g_pre_mlp[l])
        u = jnp.square(jax.nn.relu(h2 @ w_up[l]))
        x = x + _rmsnorm(u @ w_down[l], g_post_mlp[l])
    return x


import jax as _jax
import jax.numpy as _jnp

TWIN_FORMAT = 'train_step'
FWD_PARAMS = ['x', 'g_pre_mix', 'w_in', 'w_a_up', 'b_a_up', 'g_gla_norm', 'conv_w', 'conv_b', 'w_q_ml', 'w_k_ml', 'w_v_ml', 'w_if', 'b_if', 'ml_skip', 'g_ml_norm', 'w_pa', 'w_pb', 'w_o', 'g_post_mix', 'g_pre_mlp', 'w_up', 'w_down', 'g_post_mlp']
TWIN_WEIGHTS = ['g_pre_mix', 'w_in', 'w_a_up', 'b_a_up', 'g_gla_norm', 'conv_w', 'conv_b', 'w_q_ml', 'w_k_ml', 'w_v_ml', 'w_if', 'b_if', 'ml_skip', 'g_ml_norm', 'w_pa', 'w_pb', 'w_o', 'g_post_mix', 'g_pre_mlp', 'w_up', 'w_down', 'g_post_mlp']
TWIN_DIFF_INPUT = 'x'
TWIN_INPUTS = ['x', 'g_pre_mix', 'w_in', 'w_a_up', 'b_a_up', 'g_gla_norm', 'conv_w', 'conv_b', 'w_q_ml', 'w_k_ml', 'w_v_ml', 'w_if', 'b_if', 'ml_skip', 'g_ml_norm', 'w_pa', 'w_pb', 'w_o', 'g_post_mix', 'g_pre_mlp', 'w_up', 'w_down', 'g_post_mlp', 'loss_target', 'm_g_pre_mix', 'm_w_in', 'm_w_a_up', 'm_b_a_up', 'm_g_gla_norm', 'm_conv_w', 'm_conv_b', 'm_w_q_ml', 'm_w_k_ml', 'm_w_v_ml', 'm_w_if', 'm_b_if', 'm_ml_skip', 'm_g_ml_norm', 'm_w_pa', 'm_w_pb', 'm_w_o', 'm_g_post_mix', 'm_g_pre_mlp', 'm_w_up', 'm_w_down', 'm_g_post_mlp', 'v_g_pre_mix', 'v_w_in', 'v_w_a_up', 'v_b_a_up', 'v_g_gla_norm', 'v_conv_w', 'v_conv_b', 'v_w_q_ml', 'v_w_k_ml', 'v_w_v_ml', 'v_w_if', 'v_b_if', 'v_ml_skip', 'v_g_ml_norm', 'v_w_pa', 'v_w_pb', 'v_w_o', 'v_g_post_mix', 'v_g_pre_mlp', 'v_w_up', 'v_w_down', 'v_g_post_mlp']
TWIN_OUTPUTS = ['loss', 'grad_x', 'grad_g_pre_mix', 'grad_w_in', 'grad_w_a_up', 'grad_b_a_up', 'grad_g_gla_norm', 'grad_conv_w', 'grad_conv_b', 'grad_w_q_ml', 'grad_w_k_ml', 'grad_w_v_ml', 'grad_w_if', 'grad_b_if', 'grad_ml_skip', 'grad_g_ml_norm', 'grad_w_pa', 'grad_w_pb', 'grad_w_o', 'grad_g_post_mix', 'grad_g_pre_mlp', 'grad_w_up', 'grad_w_down', 'grad_g_post_mlp', 'delta_g_pre_mix', 'delta_w_in', 'delta_w_a_up', 'delta_b_a_up', 'delta_g_gla_norm', 'delta_conv_w', 'delta_conv_b', 'delta_w_q_ml', 'delta_w_k_ml', 'delta_w_v_ml', 'delta_w_if', 'delta_b_if', 'delta_ml_skip', 'delta_g_ml_norm', 'delta_w_pa', 'delta_w_pb', 'delta_w_o', 'delta_g_post_mix', 'delta_g_pre_mlp', 'delta_w_up', 'delta_w_down', 'delta_g_post_mlp', 'new_m_g_pre_mix', 'new_m_w_in', 'new_m_w_a_up', 'new_m_b_a_up', 'new_m_g_gla_norm', 'new_m_conv_w', 'new_m_conv_b', 'new_m_w_q_ml', 'new_m_w_k_ml', 'new_m_w_v_ml', 'new_m_w_if', 'new_m_b_if', 'new_m_ml_skip', 'new_m_g_ml_norm', 'new_m_w_pa', 'new_m_w_pb', 'new_m_w_o', 'new_m_g_post_mix', 'new_m_g_pre_mlp', 'new_m_w_up', 'new_m_w_down', 'new_m_g_post_mlp', 'new_v_g_pre_mix', 'new_v_w_in', 'new_v_w_a_up', 'new_v_b_a_up', 'new_v_g_gla_norm', 'new_v_conv_w', 'new_v_conv_b', 'new_v_w_q_ml', 'new_v_w_k_ml', 'new_v_w_v_ml', 'new_v_w_if', 'new_v_b_if', 'new_v_ml_skip', 'new_v_g_ml_norm', 'new_v_w_pa', 'new_v_w_pb', 'new_v_w_o', 'new_v_g_post_mix', 'new_v_g_pre_mlp', 'new_v_w_up', 'new_v_w_down', 'new_v_g_post_mlp']
TWIN_LEAF_KINDS = {'loss': 'loss', 'grad_x': 'grad_x', 'grad_g_pre_mix': 'grad_w', 'grad_w_in': 'grad_w', 'grad_w_a_up': 'grad_w', 'grad_b_a_up': 'grad_w', 'grad_g_gla_norm': 'grad_w', 'grad_conv_w': 'grad_w', 'grad_conv_b': 'grad_w', 'grad_w_q_ml': 'grad_w', 'grad_w_k_ml': 'grad_w', 'grad_w_v_ml': 'grad_w', 'grad_w_if': 'grad_w', 'grad_b_if': 'grad_w', 'grad_ml_skip': 'grad_w', 'grad_g_ml_norm': 'grad_w', 'grad_w_pa': 'grad_w', 'grad_w_pb': 'grad_w', 'grad_w_o': 'grad_w', 'grad_g_post_mix': 'grad_w', 'grad_g_pre_mlp': 'grad_w', 'grad_w_up': 'grad_w', 'grad_w_down': 'grad_w', 'grad_g_post_mlp': 'grad_w', 'delta_g_pre_mix': 'delta_w', 'delta_w_in': 'delta_w', 'delta_w_a_up': 'delta_w', 'delta_b_a_up': 'delta_w', 'delta_g_gla_norm': 'delta_w', 'delta_conv_w': 'delta_w', 'delta_conv_b': 'delta_w', 'delta_w_q_ml': 'delta_w', 'delta_w_k_ml': 'delta_w', 'delta_w_v_ml': 'delta_w', 'delta_w_if': 'delta_w', 'delta_b_if': 'delta_w', 'delta_ml_skip': 'delta_w', 'delta_g_ml_norm': 'delta_w', 'delta_w_pa': 'delta_w', 'delta_w_pb': 'delta_w', 'delta_w_o': 'delta_w', 'delta_g_post_mix': 'delta_w', 'delta_g_pre_mlp': 'delta_w', 'delta_w_up': 'delta_w', 'delta_w_down': 'delta_w', 'delta_g_post_mlp': 'delta_w', 'new_m_g_pre_mix': 'new_m', 'new_m_w_in': 'new_m', 'new_m_w_a_up': 'new_m', 'new_m_b_a_up': 'new_m', 'new_m_g_gla_norm': 'new_m', 'new_m_conv_w': 'new_m', 'new_m_conv_b': 'new_m', 'new_m_w_q_ml': 'new_m', 'new_m_w_k_ml': 'new_m', 'new_m_w_v_ml': 'new_m', 'new_m_w_if': 'new_m', 'new_m_b_if': 'new_m', 'new_m_ml_skip': 'new_m', 'new_m_g_ml_norm': 'new_m', 'new_m_w_pa': 'new_m', 'new_m_w_pb': 'new_m', 'new_m_w_o': 'new_m', 'new_m_g_post_mix': 'new_m', 'new_m_g_pre_mlp': 'new_m', 'new_m_w_up': 'new_m', 'new_m_w_down': 'new_m', 'new_m_g_post_mlp': 'new_m', 'new_v_g_pre_mix': 'new_v', 'new_v_w_in': 'new_v', 'new_v_w_a_up': 'new_v', 'new_v_b_a_up': 'new_v', 'new_v_g_gla_norm': 'new_v', 'new_v_conv_w': 'new_v', 'new_v_conv_b': 'new_v', 'new_v_w_q_ml': 'new_v', 'new_v_w_k_ml': 'new_v', 'new_v_w_v_ml': 'new_v', 'new_v_w_if': 'new_v', 'new_v_b_if': 'new_v', 'new_v_ml_skip': 'new_v', 'new_v_g_ml_norm': 'new_v', 'new_v_w_pa': 'new_v', 'new_v_w_pb': 'new_v', 'new_v_w_o': 'new_v', 'new_v_g_post_mix': 'new_v', 'new_v_g_pre_mlp': 'new_v', 'new_v_w_up': 'new_v', 'new_v_w_down': 'new_v', 'new_v_g_post_mlp': 'new_v'}


def _forward(args):
    return _fwd_reference(*[args[k] for k in FWD_PARAMS])


def _output_shape():
    out = _jax.eval_shape(lambda: _forward(_fwd_setup_inputs(0)))
    return out.shape, out.dtype

N_MICROBATCH = 1
ADAM_LR = 0.001
ADAM_B1 = 0.9
ADAM_B2 = 0.999
ADAM_EPS = 1e-08
ADAM_WD = 0.01
ADAM_STEP = 10
PER_EXAMPLE_BATCH_AXIS = {'x': 0, 'loss_target': 0}
SHARED_INPUTS = []
_WEIGHT_DTYPES = {'g_pre_mix': _jnp.float32, 'w_in': _jnp.float32, 'w_a_up': _jnp.float32, 'b_a_up': _jnp.float32, 'g_gla_norm': _jnp.float32, 'conv_w': _jnp.float32, 'conv_b': _jnp.float32, 'w_q_ml': _jnp.float32, 'w_k_ml': _jnp.float32, 'w_v_ml': _jnp.float32, 'w_if': _jnp.float32, 'b_if': _jnp.float32, 'ml_skip': _jnp.float32, 'g_ml_norm': _jnp.float32, 'w_pa': _jnp.float32, 'w_pb': _jnp.float32, 'w_o': _jnp.float32, 'g_post_mix': _jnp.float32, 'g_pre_mlp': _jnp.float32, 'w_up': _jnp.float32, 'w_down': _jnp.float32, 'g_post_mlp': _jnp.float32}
MOMENT_SCALE = {'g_pre_mix': 6.443981e-01, 'w_in': 2.957974e-01, 'w_a_up': 2.706971e-02, 'b_a_up': 1.195863e-01, 'g_gla_norm': 3.797250e-01, 'conv_w': 1.381663e+00, 'conv_b': 3.602771e+00, 'w_q_ml': 8.028217e-01, 'w_k_ml': 8.348701e-01, 'w_v_ml': 7.909540e-01, 'w_if': 2.505301e+00, 'b_if': 2.845735e+00, 'ml_skip': 1.449109e+00, 'g_ml_norm': 7.717447e-01, 'w_pa': 1.382815e-01, 'w_pb': 1.206441e+00, 'w_o': 1.295852e+00, 'g_post_mix': 1.614633e+01, 'g_pre_mlp': 6.171582e-01, 'w_up': 2.996120e-01, 'w_down': 1.303421e+00, 'g_post_mlp': 1.654943e+01}


def _to_microbatches(a, axis):
    t = _jnp.moveaxis(a, axis, 0)
    t = t.reshape((N_MICROBATCH, t.shape[0] // N_MICROBATCH) + t.shape[1:])
    return _jnp.moveaxis(t, 1, axis + 1)


def setup_inputs(seed: int = 0) -> dict:
    inp = _fwd_setup_inputs(seed)
    key = _jax.random.fold_in(_jax.random.key(seed), 7919)
    shape, _ = _output_shape()
    out = dict(inp)
    out["loss_target"] = _jax.random.normal(_jax.random.fold_in(key, 0), shape, _jnp.float32)
    for i, name in enumerate(TWIN_WEIGHTS):
        w = inp[name].astype(_jnp.float32)
        if MOMENT_SCALE is None:
            s = _jnp.sqrt(_jnp.mean(_jnp.square(w)) + 1e-30)
        else:
            s = MOMENT_SCALE[name]
        km, kv = _jax.random.split(_jax.random.fold_in(key, i + 1))
        out[name] = w
        out["m_" + name] = s * _jax.random.normal(km, w.shape, _jnp.float32)
        out["v_" + name] = (s * s) * _jax.random.uniform(kv, w.shape, _jnp.float32, 0.5, 1.5)
    if N_MICROBATCH > 1:
        for name, axis in PER_EXAMPLE_BATCH_AXIS.items():
            out[name] = _to_microbatches(out[name], axis)
    return {'x': out['x'], 'g_pre_mix': out['g_pre_mix'], 'w_in': out['w_in'], 'w_a_up': out['w_a_up'], 'b_a_up': out['b_a_up'], 'g_gla_norm': out['g_gla_norm'], 'conv_w': out['conv_w'], 'conv_b': out['conv_b'], 'w_q_ml': out['w_q_ml'], 'w_k_ml': out['w_k_ml'], 'w_v_ml': out['w_v_ml'], 'w_if': out['w_if'], 'b_if': out['b_if'], 'ml_skip': out['ml_skip'], 'g_ml_norm': out['g_ml_norm'], 'w_pa': out['w_pa'], 'w_pb': out['w_pb'], 'w_o': out['w_o'], 'g_post_mix': out['g_post_mix'], 'g_pre_mlp': out['g_pre_mlp'], 'w_up': out['w_up'], 'w_down': out['w_down'], 'g_post_mlp': out['g_post_mlp'], 'loss_target': out['loss_target'], 'm_g_pre_mix': out['m_g_pre_mix'], 'm_w_in': out['m_w_in'], 'm_w_a_up': out['m_w_a_up'], 'm_b_a_up': out['m_b_a_up'], 'm_g_gla_norm': out['m_g_gla_norm'], 'm_conv_w': out['m_conv_w'], 'm_conv_b': out['m_conv_b'], 'm_w_q_ml': out['m_w_q_ml'], 'm_w_k_ml': out['m_w_k_ml'], 'm_w_v_ml': out['m_w_v_ml'], 'm_w_if': out['m_w_if'], 'm_b_if': out['m_b_if'], 'm_ml_skip': out['m_ml_skip'], 'm_g_ml_norm': out['m_g_ml_norm'], 'm_w_pa': out['m_w_pa'], 'm_w_pb': out['m_w_pb'], 'm_w_o': out['m_w_o'], 'm_g_post_mix': out['m_g_post_mix'], 'm_g_pre_mlp': out['m_g_pre_mlp'], 'm_w_up': out['m_w_up'], 'm_w_down': out['m_w_down'], 'm_g_post_mlp': out['m_g_post_mlp'], 'v_g_pre_mix': out['v_g_pre_mix'], 'v_w_in': out['v_w_in'], 'v_w_a_up': out['v_w_a_up'], 'v_b_a_up': out['v_b_a_up'], 'v_g_gla_norm': out['v_g_gla_norm'], 'v_conv_w': out['v_conv_w'], 'v_conv_b': out['v_conv_b'], 'v_w_q_ml': out['v_w_q_ml'], 'v_w_k_ml': out['v_w_k_ml'], 'v_w_v_ml': out['v_w_v_ml'], 'v_w_if': out['v_w_if'], 'v_b_if': out['v_b_if'], 'v_ml_skip': out['v_ml_skip'], 'v_g_ml_norm': out['v_g_ml_norm'], 'v_w_pa': out['v_w_pa'], 'v_w_pb': out['v_w_pb'], 'v_w_o': out['v_w_o'], 'v_g_post_mix': out['v_g_post_mix'], 'v_g_pre_mlp': out['v_g_pre_mlp'], 'v_w_up': out['v_w_up'], 'v_w_down': out['v_w_down'], 'v_g_post_mlp': out['v_g_post_mlp']}


def _loss(weights, diff, rest, loss_target):
    with _jax.named_scope("forward"):
        args = {**rest, TWIN_DIFF_INPUT: diff, **{k: w.astype(_WEIGHT_DTYPES[k]) for k, w in weights.items()}}
        y = _forward(args)
    with _jax.named_scope("loss_head"):
        err = _jnp.square(y.astype(_jnp.float32) - loss_target)
        return 0.5 * _jnp.sum(_jnp.mean(err, axis=-1)) if err.ndim else 0.5 * err


def _adamw(w, g, m, v):
    m = ADAM_B1 * m + (1.0 - ADAM_B1) * g
    v = ADAM_B2 * v + (1.0 - ADAM_B2) * _jnp.square(g)
    m_hat = m / (1.0 - ADAM_B1 ** ADAM_STEP)
    v_hat = v / (1.0 - ADAM_B2 ** ADAM_STEP)
    delta = -ADAM_LR * (m_hat / (_jnp.sqrt(v_hat) + ADAM_EPS) + ADAM_WD * w)
    return delta, m, v


def reference(x, g_pre_mix, w_in, w_a_up, b_a_up, g_gla_norm, conv_w, conv_b, w_q_ml, w_k_ml, w_v_ml, w_if, b_if, ml_skip, g_ml_norm, w_pa, w_pb, w_o, g_post_mix, g_pre_mlp, w_up, w_down, g_post_mlp, loss_target, m_g_pre_mix, m_w_in, m_w_a_up, m_b_a_up, m_g_gla_norm, m_conv_w, m_conv_b, m_w_q_ml, m_w_k_ml, m_w_v_ml, m_w_if, m_b_if, m_ml_skip, m_g_ml_norm, m_w_pa, m_w_pb, m_w_o, m_g_post_mix, m_g_pre_mlp, m_w_up, m_w_down, m_g_post_mlp, v_g_pre_mix, v_w_in, v_w_a_up, v_b_a_up, v_g_gla_norm, v_conv_w, v_conv_b, v_w_q_ml, v_w_k_ml, v_w_v_ml, v_w_if, v_b_if, v_ml_skip, v_g_ml_norm, v_w_pa, v_w_pb, v_w_o, v_g_post_mix, v_g_pre_mlp, v_w_up, v_w_down, v_g_post_mlp):
    given = dict(x=x, g_pre_mix=g_pre_mix, w_in=w_in, w_a_up=w_a_up, b_a_up=b_a_up, g_gla_norm=g_gla_norm, conv_w=conv_w, conv_b=conv_b, w_q_ml=w_q_ml, w_k_ml=w_k_ml, w_v_ml=w_v_ml, w_if=w_if, b_if=b_if, ml_skip=ml_skip, g_ml_norm=g_ml_norm, w_pa=w_pa, w_pb=w_pb, w_o=w_o, g_post_mix=g_post_mix, g_pre_mlp=g_pre_mlp, w_up=w_up, w_down=w_down, g_post_mlp=g_post_mlp, loss_target=loss_target, m_g_pre_mix=m_g_pre_mix, m_w_in=m_w_in, m_w_a_up=m_w_a_up, m_b_a_up=m_b_a_up, m_g_gla_norm=m_g_gla_norm, m_conv_w=m_conv_w, m_conv_b=m_conv_b, m_w_q_ml=m_w_q_ml, m_w_k_ml=m_w_k_ml, m_w_v_ml=m_w_v_ml, m_w_if=m_w_if, m_b_if=m_b_if, m_ml_skip=m_ml_skip, m_g_ml_norm=m_g_ml_norm, m_w_pa=m_w_pa, m_w_pb=m_w_pb, m_w_o=m_w_o, m_g_post_mix=m_g_post_mix, m_g_pre_mlp=m_g_pre_mlp, m_w_up=m_w_up, m_w_down=m_w_down, m_g_post_mlp=m_g_post_mlp, v_g_pre_mix=v_g_pre_mix, v_w_in=v_w_in, v_w_a_up=v_w_a_up, v_b_a_up=v_b_a_up, v_g_gla_norm=v_g_gla_norm, v_conv_w=v_conv_w, v_conv_b=v_conv_b, v_w_q_ml=v_w_q_ml, v_w_k_ml=v_w_k_ml, v_w_v_ml=v_w_v_ml, v_w_if=v_w_if, v_b_if=v_b_if, v_ml_skip=v_ml_skip, v_g_ml_norm=v_g_ml_norm, v_w_pa=v_w_pa, v_w_pb=v_w_pb, v_w_o=v_w_o, v_g_post_mix=v_g_post_mix, v_g_pre_mlp=v_g_pre_mlp, v_w_up=v_w_up, v_w_down=v_w_down, v_g_post_mlp=v_g_post_mlp)
    weights = {n: given[n] for n in TWIN_WEIGHTS}
    shared = {n: given[n] for n in SHARED_INPUTS}
    per_example = {n: given[n] for n in ['x']}
    grad_fn = _jax.value_and_grad(_loss, argnums=(0, 1))

    def one_microbatch(ex, loss_target):
        ex = dict(ex)
        diff = ex.pop(TWIN_DIFF_INPUT)
        return grad_fn(weights, diff, {**shared, **ex}, loss_target)

    if N_MICROBATCH == 1:
        loss, (grad_w, grad_x) = one_microbatch(per_example, given["loss_target"])
    else:
        def body(carry, xs):
            loss_sum, grad_sum = carry
            l_k, (gw_k, gx_k) = one_microbatch(xs[0], xs[1])
            with _jax.named_scope("update"):
                return (loss_sum + l_k, _jax.tree.map(_jnp.add, grad_sum, gw_k)), gx_k

        init = (_jnp.zeros((), _jnp.float32), _jax.tree.map(_jnp.zeros_like, weights))
        (loss, grad_w), grad_x = _jax.lax.scan(body, init, (per_example, given["loss_target"]))
    with _jax.named_scope("update"):
        delta_w, new_m, new_v = {}, {}, {}
        for n in TWIN_WEIGHTS:
            delta_w[n], new_m[n], new_v[n] = _adamw(weights[n], grad_w[n], given["m_" + n], given["v_" + n])
    return (loss, grad_x, *[grad_w[n] for n in TWIN_WEIGHTS], *[delta_w[n] for n in TWIN_WEIGHTS],
            *[new_m[n] for n in TWIN_WEIGHTS], *[new_v[n] for n in TWIN_WEIGHTS])
```

```python
import functools

import jax
import jax.numpy as jnp
from jax import lax
from jax.experimental import pallas as pl
from jax.experimental.pallas import tpu as pltpu

F32 = jnp.float32
BF16 = jnp.bfloat16
MESH = pl.DeviceIdType.MESH

N_DEV = 8
EPS = 1e-6
CHUNK = 64
HEADS = 4
GLA_DK = 64
HEAD_W = 128
GLA_GATE_NORM = 16.0
LOWRANK = 16
CONV_K = 4
QKV_BLOCK = 4
LANES = 128
HALO = 8
IN_SPLITS = (256, 256, 512, 512, 16, 512, 512, 1024, 1024)

ADAM_LR = 0.001
ADAM_B1 = 0.9
ADAM_B2 = 0.999
ADAM_EPS = 1e-08
ADAM_WD = 0.01
ADAM_STEP = 10

VMEM_LIMIT = 56 * 1024 * 1024


def _cparams(*sem):
    return pltpu.CompilerParams(dimension_semantics=sem, vmem_limit_bytes=VMEM_LIMIT)


_DN = {"nn": (((1,), (0,)), ((), ())), "nt": (((1,), (1,)), ((), ())), "tn": (((0,), (0,)), ((), ()))}


def _raw_dot(a, b, mode):
    return lax.dot_general(a.astype(BF16), b.astype(BF16), _DN[mode], preferred_element_type=F32)


@functools.partial(jax.custom_vjp, nondiff_argnums=(2,))
def _bdot(a, b, mode):
    return _raw_dot(a, b, mode)


def _bdot_fwd(a, b, mode):
    return _raw_dot(a, b, mode), (a, b)


def _bdot_bwd(mode, res, ct):
    a, b = res
    if mode == "nn":
        da, db = _raw_dot(ct, b, "nt"), _raw_dot(a, ct, "tn")
    elif mode == "nt":
        da, db = _raw_dot(ct, b, "nn"), _raw_dot(ct, a, "tn")
    else:
        da, db = _raw_dot(b, ct, "nt"), _raw_dot(a, ct, "nn")
    return da.astype(a.dtype), db.astype(b.dtype)


_bdot.defvjp(_bdot_fwd, _bdot_bwd)


def _split_dot(tri, x):
    hi = x.astype(BF16)
    r1 = x - hi.astype(F32)
    mid = r1.astype(BF16)
    lo = (r1 - mid.astype(F32)).astype(BF16)
    dn = _DN["nn"]
    return (lax.dot_general(tri, hi, dn, preferred_element_type=F32)
            + lax.dot_general(tri, mid, dn, preferred_element_type=F32)
            + lax.dot_general(tri, lo, dn, preferred_element_type=F32))


def _tri(n, lower):
    r = lax.broadcasted_iota(jnp.int32, (n, n), 0)
    c = lax.broadcasted_iota(jnp.int32, (n, n), 1)
    return ((c <= r) if lower else (c >= r)).astype(BF16)


@jax.custom_vjp
def _cumsum_rows(x):
    return _split_dot(_tri(x.shape[0], True), x)


def _cumsum_rows_fwd(x):
    return _cumsum_rows(x), None


def _cumsum_rows_bwd(_, ct):
    return (_split_dot(_tri(ct.shape[0], False), ct),)


_cumsum_rows.defvjp(_cumsum_rows_fwd, _cumsum_rows_bwd)


def _abs(x):
    return jnp.where(x >= 0, x, -x)


def _sigmoid(x):
    return lax.logistic(x)


def _log_sigmoid(x):
    return jnp.minimum(x, 0.0) - jnp.log(1.0 + jnp.exp(-_abs(x)))


def _rms(x, g):
    return x * lax.rsqrt(jnp.mean(x * x, axis=-1, keepdims=True) + EPS) * g


def _head_slices(w):
    return [slice(h * w, (h + 1) * w) for h in range(HEADS)]


def _tile(dim, want):
    if dim <= want or dim % LANES:
        return dim
    t = want
    while dim % t:
        t -= LANES
    return t


def _mm(a, b, mode, out_dtype, name, tm=512, tn=512, tk=512):
    if mode == "nn":
        (m, k), (k2, n) = a.shape, b.shape
    elif mode == "nt":
        (m, k), (n, k2) = a.shape, b.shape
    else:
        (k, m), (k2, n) = a.shape, b.shape
    assert k == k2, (name, a.shape, b.shape)
    tm, tn, tk = _tile(m, tm), _tile(n, tn), _tile(k, tk)
    nk = k // tk

    def body(a_ref, b_ref, o_ref, acc_ref):
        kk = pl.program_id(2)
        p = _raw_dot(a_ref[...], b_ref[...], mode)

        @pl.when(kk == 0)
        def _():
            acc_ref[...] = p

        @pl.when(kk > 0)
        def _():
            acc_ref[...] += p

        @pl.when(kk == nk - 1)
        def _():
            o_ref[...] = acc_ref[...].astype(out_dtype)

    a_spec = pl.BlockSpec((tk, tm), lambda i, j, kk: (kk, i)) if mode == "tn" else pl.BlockSpec((tm, tk), lambda i, j, kk: (i, kk))
    b_spec = pl.BlockSpec((tn, tk), lambda i, j, kk: (j, kk)) if mode == "nt" else pl.BlockSpec((tk, tn), lambda i, j, kk: (kk, j))
    return pl.pallas_call(
        body, name=name, grid=(m // tm, n // tn, nk),
        in_specs=[a_spec, b_spec], out_specs=pl.BlockSpec((tm, tn), lambda i, j, kk: (i, j)),
        out_shape=jax.ShapeDtypeStruct((m, n), out_dtype),
        scratch_shapes=[pltpu.VMEM((tm, tn), F32)],
        compiler_params=_cparams("parallel", "parallel", "arbitrary"),
    )(a, b)


def _rowwise(name, fn, rows, params, out_rows, out_accs=(), tile=256):
    t = rows[0].shape[0]
    r = min(tile, t)
    assert t % r == 0
    n_in, n_or = len(rows) + len(params), len(out_rows)

    def body(*refs):
        vals = [ref[...] for ref in refs[:n_in]]
        outs = refs[n_in:]
        ro, ao = fn(*vals)
        for ref, v in zip(outs[:n_or], ro):
            ref[...] = v.astype(ref.dtype)
        if out_accs:
            _accumulate(pl.program_id(0), outs[n_or:], ao)

    def full(shape):
        return pl.BlockSpec(shape, lambda i, nd=len(shape): (0,) * nd)

    return pl.pallas_call(
        body, name=name, grid=(t // r,),
        in_specs=[pl.BlockSpec((r, a.shape[1]), lambda i: (i, 0)) for a in rows] + [full(p.shape) for p in params],
        out_specs=[pl.BlockSpec((r, w), lambda i: (i, 0)) for w, _ in out_rows] + [full(s) for s, _ in out_accs],
        out_shape=[jax.ShapeDtypeStruct((t, w), dt) for w, dt in out_rows] + [jax.ShapeDtypeStruct(s, dt) for s, dt in out_accs],
        compiler_params=_cparams("arbitrary"),
    )(*rows, *params)


def _accumulate(step, refs, vals):
    for ref, v in zip(refs, vals):
        @pl.when(step == 0)
        def _(ref=ref, v=v):
            ref[...] = v.astype(ref.dtype)

        @pl.when(step > 0)
        def _(ref=ref, v=v):
            ref[...] += v.astype(ref.dtype)


def _gla_chunk(q, k, v, la, st):
    c = q.shape[0]
    row = lax.broadcasted_iota(jnp.int32, (c, c), 0)
    col = lax.broadcasted_iota(jnp.int32, (c, c), 1)
    cum = _cumsum_rows(la)
    cl = jnp.sum(la, axis=0, keepdims=True)
    ep = jnp.exp(cum)
    en = jnp.exp(-cum)
    qs = q * (GLA_DK ** -0.5)
    qp = qs * ep
    a_f = _bdot(qp, k * en, "nt")
    a_b = _bdot(qs * en, k * ep, "nt")
    sc = jnp.where(row >= col, a_f, a_b)
    o = _bdot(sc, v, "nn") + _bdot(qp, st, "nt")
    kd = k * jnp.exp(cl - cum)
    st_new = st * jnp.exp(cl) + _bdot(v, kd, "tn")
    return o, st_new


def _gla_specs(nc, rev):
    def ch(n):
        return (nc - 1 - n) if rev else n
    hm = pl.BlockSpec((None, CHUNK, GLA_DK), lambda h, n: (h, ch(n), 0))
    tm = pl.BlockSpec((CHUNK, HEAD_W), lambda h, n: (ch(n), h))
    st = pl.BlockSpec((None, None, HEAD_W, GLA_DK), lambda h, n: (h, ch(n), 0, 0))
    return hm, tm, st


def _gla_fwd(q, k, v, la):
    t = v.shape[0]
    nc = t // CHUNK
    hm, tm, st = _gla_specs(nc, False)

    def body(q_ref, k_ref, v_ref, la_ref, o_ref, sp_ref, st_ref):
        @pl.when(pl.program_id(1) == 0)
        def _():
            st_ref[...] = jnp.zeros_like(st_ref)

        s = st_ref[...]
        sp_ref[...] = s
        o, s_new = _gla_chunk(q_ref[...], k_ref[...], v_ref[...], la_ref[...], s)
        o_ref[...] = o
        st_ref[...] = s_new

    return pl.pallas_call(
        body, name="gla_fwd", grid=(HEADS, nc),
        in_specs=[hm, hm, tm, hm], out_specs=[tm, st],
        out_shape=[jax.ShapeDtypeStruct((t, HEADS * HEAD_W), F32), jax.ShapeDtypeStruct((HEADS, nc, HEAD_W, GLA_DK), F32)],
        scratch_shapes=[pltpu.VMEM((HEAD_W, GLA_DK), F32)],
        compiler_params=_cparams("arbitrary", "arbitrary"),
    )(q, k, v, la)


def _gla_bwd(q, k, v, la, sp, do):
    t = v.shape[0]
    nc = t // CHUNK
    hm, tm, st = _gla_specs(nc, True)

    def body(q_ref, k_ref, v_ref, la_ref, sp_ref, do_ref, dq_ref, dk_ref, dv_ref, dla_ref, ds_ref):
        @pl.when(pl.program_id(1) == 0)
        def _():
            ds_ref[...] = jnp.zeros_like(ds_ref)

        _, vjp = jax.vjp(_gla_chunk, q_ref[...], k_ref[...], v_ref[...], la_ref[...], sp_ref[...])
        dq, dk, dv, dla, ds = vjp((do_ref[...], ds_ref[...]))
        dq_ref[...] = dq
        dk_ref[...] = dk
        dv_ref[...] = dv
        dla_ref[...] = dla
        ds_ref[...] = ds

    hm_shape = jax.ShapeDtypeStruct((HEADS, t, GLA_DK), F32)
    return pl.pallas_call(
        body, name="gla_bwd", grid=(HEADS, nc),
        in_specs=[hm, hm, tm, hm, st, tm], out_specs=[hm, hm, tm, hm],
        out_shape=[hm_shape, hm_shape, jax.ShapeDtypeStruct((t, HEADS * HEAD_W), F32), hm_shape],
        scratch_shapes=[pltpu.VMEM((HEAD_W, GLA_DK), F32)],
        compiler_params=_cparams("arbitrary", "arbitrary"),
    )(q, k, v, la, sp, do)


def _ml_chunk(q, k, v, li_r, lf_r, cm, nv, m):
    c = q.shape[0]
    row = lax.broadcasted_iota(jnp.int32, (c, c), 0)
    col = lax.broadcasted_iota(jnp.int32, (c, c), 1)
    eye = (row == col).astype(F32)
    li_c = jnp.sum(eye * li_r, axis=1, keepdims=True)
    lf_c = jnp.sum(eye * lf_r, axis=1, keepdims=True)
    fc_c = jnp.sum((col <= row).astype(F32) * lf_r, axis=1, keepdims=True)
    fc_r = jnp.sum((row <= col).astype(F32) * lf_c, axis=0, keepdims=True)
    f_last = jnp.sum(lf_r, axis=1, keepdims=True)
    kc = k * (HEAD_W ** -0.5)
    a_c = f_last - fc_c + li_c
    m_loc = jnp.max(a_c, axis=0, keepdims=True)
    kw = kc * jnp.exp(a_c - m_loc)
    c_chunk = _bdot(kw, v, "tn")
    n_chunk = jnp.sum(kw, axis=0, keepdims=True)
    m_new = jnp.maximum(f_last + m, m_loc)
    sp = jnp.exp(f_last + m - m_new)
    sl = jnp.exp(m_loc - m_new)
    cm_new = sp * cm + sl * c_chunk
    nv_new = sp * nv + sl * n_chunk
    log_d = li_r - _abs(fc_c - fc_r)
    g_inter = fc_c + m
    m_t = jnp.maximum(g_inter, jnp.max(log_d, axis=1, keepdims=True))
    s = _bdot(q, kc, "nt") * jnp.exp(log_d - m_t)
    sc = jnp.exp(g_inter - m_t)
    num = _bdot(s, v, "nn") + sc * _bdot(q, cm, "nn")
    den = jnp.sum(s, axis=1, keepdims=True) + sc * jnp.sum(q * nv, axis=1, keepdims=True)
    den = jnp.maximum(_abs(den), jnp.exp(-m_t))
    return num / den, cm_new, nv_new, m_new


def _ml_specs(nc, rev):
    def ch(n):
        return (nc - 1 - n) if rev else n
    tm = pl.BlockSpec((CHUNK, HEAD_W), lambda h, n: (ch(n), h))
    gate = pl.BlockSpec((None, None, 1, CHUNK), lambda h, n: (h, ch(n), 0, 0))
    cm = pl.BlockSpec((None, None, HEAD_W, HEAD_W), lambda h, n: (h, ch(n), 0, 0))
    vec = pl.BlockSpec((None, None, 1, HEAD_W), lambda h, n: (h, ch(n), 0, 0))
    return tm, gate, cm, vec


def _ml_fwd(q, k, v, li, lf):
    t = q.shape[0]
    nc = t // CHUNK
    tm, gate, cm, vec = _ml_specs(nc, False)

    def body(q_ref, k_ref, v_ref, li_ref, lf_ref, hc_ref, cp_ref, np_ref, mp_ref, c_ref, n_ref, m_ref):
        @pl.when(pl.program_id(1) == 0)
        def _():
            c_ref[...] = jnp.zeros_like(c_ref)
            n_ref[...] = jnp.zeros_like(n_ref)
            m_ref[...] = jnp.zeros_like(m_ref)

        c0, n0, m0 = c_ref[...], n_ref[...], m_ref[...]
        cp_ref[...] = c0
        np_ref[...] = n0
        mp_ref[...] = m0
        hc, c1, n1, m1 = _ml_chunk(q_ref[...], k_ref[...], v_ref[...], li_ref[...], lf_ref[...], c0, n0, m0[:, 0:1])
        hc_ref[...] = hc
        c_ref[...] = c1
        n_ref[...] = n1
        m_ref[...] = jnp.broadcast_to(m1, m_ref.shape)

    return pl.pallas_call(
        body, name="mlstm_fwd", grid=(HEADS, nc),
        in_specs=[tm, tm, tm, gate, gate], out_specs=[tm, cm, vec, vec],
        out_shape=[jax.ShapeDtypeStruct((t, HEADS * HEAD_W), F32), jax.ShapeDtypeStruct((HEADS, nc, HEAD_W, HEAD_W), F32),
                   jax.ShapeDtypeStruct((HEADS, nc, 1, HEAD_W), F32), jax.ShapeDtypeStruct((HEADS, nc, 1, HEAD_W), F32)],
        scratch_shapes=[pltpu.VMEM((HEAD_W, HEAD_W), F32), pltpu.VMEM((1, HEAD_W), F32), pltpu.VMEM((1, HEAD_W), F32)],
        compiler_params=_cparams("arbitrary", "arbitrary"),
    )(q, k, v, li, lf)


def _ml_bwd(q, k, v, li, lf, cp, npv, mp, dhc):
    t = q.shape[0]
    nc = t // CHUNK
    tm, gate, cm, vec = _ml_specs(nc, True)

    def body(q_ref, k_ref, v_ref, li_ref, lf_ref, cp_ref, np_ref, mp_ref, dhc_ref,
             dq_ref, dk_ref, dv_ref, dli_ref, dlf_ref, dc_ref, dn_ref, dm_ref):
        @pl.when(pl.program_id(1) == 0)
        def _():
            dc_ref[...] = jnp.zeros_like(dc_ref)
            dn_ref[...] = jnp.zeros_like(dn_ref)
            dm_ref[...] = jnp.zeros_like(dm_ref)

        _, vjp = jax.vjp(_ml_chunk, q_ref[...], k_ref[...], v_ref[...], li_ref[...], lf_ref[...],
                         cp_ref[...], np_ref[...], mp_ref[...][:, 0:1])
        dq, dk, dv, dli, dlf, dc, dn, dm = vjp((dhc_ref[...], dc_ref[...], dn_ref[...], dm_ref[...][:, 0:1]))
        dq_ref[...] = dq
        dk_ref[...] = dk
        dv_ref[...] = dv
        dli_ref[...] = dli
        dlf_ref[...] = dlf
        dc_ref[...] = dc
        dn_ref[...] = dn
        dm_ref[...] = jnp.broadcast_to(dm, dm_ref.shape)

    tm_shape = jax.ShapeDtypeStruct((t, HEADS * HEAD_W), F32)
    gate_shape = jax.ShapeDtypeStruct((HEADS, nc, 1, CHUNK), F32)
    return pl.pallas_call(
        body, name="mlstm_bwd", grid=(HEADS, nc),
        in_specs=[tm, tm, tm, gate, gate, cm, vec, vec, tm], out_specs=[tm, tm, tm, gate, gate],
        out_shape=[tm_shape, tm_shape, tm_shape, gate_shape, gate_shape],
        scratch_shapes=[pltpu.VMEM((HEAD_W, HEAD_W), F32), pltpu.VMEM((1, HEAD_W), F32), pltpu.VMEM((1, HEAD_W), F32)],
        compiler_params=_cparams("arbitrary", "arbitrary"),
    )(q, k, v, li, lf, cp, npv, mp, dhc)


def _ml_pre(s0, s1, s2, s3, cw0, cw1, cw2, cw3, cb, wq, wk, wv, wiq, wik, wiv, bif):
    pre = cb + cw0 * s0 + cw1 * s1 + cw2 * s2 + cw3 * s3
    xc = pre * _sigmoid(pre)
    q = _bdot(xc, wq, "nn")
    k = _bdot(xc, wk, "nn")
    v = _bdot(s3, wv, "nn")
    gates = _bdot(q, wiq, "nn") + _bdot(k, wik, "nn") + _bdot(v, wiv, "nn") + bif
    lane = lax.broadcasted_iota(jnp.int32, gates.shape, 1)
    gl = jnp.where(lane < HEADS, gates, _log_sigmoid(gates))
    return xc, q, k, v, gl


def _delayed(xs_ref, x_ref, halo_ref, r):
    xs_ref[0:HALO, :] = halo_ref[...]
    xs_ref[HALO:HALO + r, :] = x_ref[...]
    return [xs_ref[pl.ds(HALO - (CONV_K - 1) + j, r), :] for j in range(CONV_K)]


def _full_spec(shape):
    return pl.BlockSpec(shape, lambda i, nd=len(shape): (0,) * nd)


def _ml_pre_fwd(x_m, x_pad, params, tile=256):
    t, w = x_m.shape
    r = min(tile, t)

    def body(*refs):
        x_ref, halo_ref = refs[:2]
        p = [ref[...] for ref in refs[2:2 + len(params)]]
        outs = refs[2 + len(params):-1]
        res = _ml_pre(*_delayed(refs[-1], x_ref, halo_ref, r), *p)
        for ref, val in zip(outs, res):
            ref[...] = val

    row = pl.BlockSpec((r, w), lambda i: (i, 0))
    return pl.pallas_call(
        body, name="ml_pre_fwd", grid=(t // r,),
        in_specs=[row, pl.BlockSpec((HALO, w), lambda i: (i * (r // HALO), 0))] + [_full_spec(p.shape) for p in params],
        out_specs=[row] * 4 + [pl.BlockSpec((r, LANES), lambda i: (i, 0))],
        out_shape=[jax.ShapeDtypeStruct((t, w), F32)] * 4 + [jax.ShapeDtypeStruct((t, LANES), F32)],
        scratch_shapes=[pltpu.VMEM((r + HALO, w), F32)],
        compiler_params=_cparams("arbitrary"),
    )(x_m, x_pad, *params)


def _ml_pre_bwd(x_m, x_pad, params, cts, tile=256):
    t, w = x_m.shape
    r = min(tile, t)
    nt = t // r
    n_p = len(params)

    def body(*refs):
        x_ref, halo_ref = refs[:2]
        p = [ref[...] for ref in refs[2:2 + n_p]]
        ct = [ref[...] for ref in refs[2 + n_p:7 + n_p]]
        dx_ref = refs[7 + n_p]
        dp_refs = refs[8 + n_p:8 + 2 * n_p]
        xs_ref, ds_ref, carry_ref = refs[8 + 2 * n_p:]
        step = pl.program_id(0)

        @pl.when(step == 0)
        def _():
            ds_ref[...] = jnp.zeros_like(ds_ref)
            carry_ref[...] = jnp.zeros_like(carry_ref)

        _, vjp = jax.vjp(_ml_pre, *_delayed(xs_ref, x_ref, halo_ref, r), *p)
        grads = vjp(tuple(ct))
        for j in range(CONV_K):
            ds_ref[j, HALO:HALO + r, :] = grads[j]
        lead = HALO + CONV_K - 1
        d_tile = sum(ds_ref[j, pl.ds(lead - j, r), :] for j in range(CONV_K))
        d_halo = sum(ds_ref[j, pl.ds(CONV_K - 1 - j, HALO), :] for j in range(CONV_K))
        dx_ref[...] = d_tile
        dx_ref[r - HALO:r, :] += carry_ref[...]
        carry_ref[...] = d_halo
        _accumulate(step, dp_refs, grads[CONV_K:])

    row = pl.BlockSpec((r, w), lambda i: (nt - 1 - i, 0))
    return pl.pallas_call(
        body, name="ml_pre_bwd", grid=(nt,),
        in_specs=[row, pl.BlockSpec((HALO, w), lambda i: ((nt - 1 - i) * (r // HALO), 0))] + [_full_spec(p.shape) for p in params]
        + [row] * 4 + [pl.BlockSpec((r, LANES), lambda i: (nt - 1 - i, 0))],
        out_specs=[row] + [_full_spec(p.shape) for p in params],
        out_shape=[jax.ShapeDtypeStruct((t, w), F32)] + [jax.ShapeDtypeStruct(p.shape, F32) for p in params],
        scratch_shapes=[pltpu.VMEM((r + HALO, w), F32), pltpu.VMEM((CONV_K, r + 2 * HALO, w), F32), pltpu.VMEM((HALO, w), F32)],
        compiler_params=_cparams("arbitrary"),
    )(x_m, x_pad, *params, *cts)


def _per_head(fn, row_vals, head_params, shared_params=()):
    return [fn(*[a[:, hs] for a in row_vals], *[p[:, hs] for p in head_params], *shared_params) for hs in _head_slices(HEAD_W)]


def _gla_out(o, g, gn):
    return _rms(o, gn) * (g * _sigmoid(g))


def _ml_out(hc, op, xc, g, sk):
    hcell = hc * _sigmoid(op)
    mu = jnp.mean(hcell, axis=-1, keepdims=True)
    d = hcell - mu
    var = jnp.mean(d * d, axis=-1, keepdims=True)
    return d * lax.rsqrt(var + EPS) * g + sk * xc


def _log_decay(al, w, b):
    return _log_sigmoid(_bdot(al, w, "nn") + b) * (1.0 / GLA_GATE_NORM)


def _merge(ga, gb, ya, yb):
    return _sigmoid(ga) * ya + _sigmoid(gb) * yb


def _post_mix(x, z, gpm, gpl):
    x1 = x + _rms(z, gpm)
    return x1, _rms(x1, gpl)


def _loss_rows(x1, dn, tgt, g):
    e = x1 + _rms(dn, g) - tgt
    return 0.5 * jnp.sum(jnp.mean(e * e, axis=-1, keepdims=True), axis=0, keepdims=True)


def _lin(p):
    return 4 * p[0] + 2 * p[1] + p[2]


def _me():
    return lax.axis_index("x"), lax.axis_index("y"), lax.axis_index("c")


def _flip(p, k):
    return tuple((1 - v) if (k >> (2 - i)) & 1 else v for i, v in enumerate(p))


ANY = pl.BlockSpec(memory_space=pl.ANY)


def _allgather_big(shards, name):
    n = len(shards)

    def body(*refs):
        ins, outs = refs[:n], refs[n:2 * n]
        send_sems, recv_sems, local_sems = refs[2 * n:]
        me = _me()
        x, y, c = me
        sib = (x, y, 1 - c)
        chips = [(1 - x, y), (x, 1 - y), (1 - x, 1 - y)]

        def cp(a, k, block, to, src=None):
            dst = outs[a].at[_lin(block)]
            return pltpu.make_async_remote_copy(src_ref=dst if src is None else src, dst_ref=dst,
                                                send_sem=send_sems.at[a * 7 + k], recv_sem=recv_sems.at[a * 7 + k],
                                                device_id=to, device_id_type=MESH)

        mine = [pltpu.make_async_copy(ins[a], outs[a].at[_lin(me)], local_sems.at[a]) for a in range(n)]
        for m in mine:
            m.start()
        first = []
        for a in range(n):
            first.append(cp(a, 0, me, sib, src=ins[a]))
            first += [cp(a, 1 + j, me, (*chip, c), src=ins[a]) for j, chip in enumerate(chips)]
        for f in first:
            f.start()
        passed = []
        for j, chip in enumerate(chips):
            for a in range(n):
                cp(a, 1 + j, (*chip, c), me).wait_recv()
                fwd = cp(a, 4 + j, (*chip, c), sib)
                fwd.start()
                passed.append(fwd)
        for a in range(n):
            cp(a, 0, sib, me).wait_recv()
            for j, chip in enumerate(chips):
                cp(a, 4 + j, (*chip, 1 - c), me).wait_recv()
        for f in first + passed:
            f.wait_send()
        for m in mine:
            m.wait()

    return pl.pallas_call(
        body, name=name,
        in_specs=[ANY] * n, out_specs=[ANY] * n,
        out_shape=[jax.ShapeDtypeStruct((N_DEV, *s.shape), s.dtype) for s in shards],
        scratch_shapes=[pltpu.SemaphoreType.DMA((7 * n,)), pltpu.SemaphoreType.DMA((7 * n,)), pltpu.SemaphoreType.DMA((n,))],
    )(*shards)


def _exchange_big(parts, name):
    n = len(parts)

    def body(*refs):
        ins, outs = refs[:n], refs[n:2 * n]
        send_sems, recv_sems, local_sems = refs[2 * n:]
        me = _me()
        copies = []
        for a in range(n):
            loc = pltpu.make_async_copy(ins[a].at[_lin(me)], outs[a].at[_lin(me)], local_sems.at[a])
            loc.start()
            copies.append(loc)
            for k in range(1, N_DEV):
                peer = _flip(me, k)
                rc = pltpu.make_async_remote_copy(src_ref=ins[a].at[_lin(peer)], dst_ref=outs[a].at[_lin(me)],
                                                  send_sem=send_sems.at[a * 7 + k - 1], recv_sem=recv_sems.at[a * 7 + k - 1],
                                                  device_id=peer, device_id_type=MESH)
                rc.start()
                copies.append(rc)
        for cpy in copies:
            cpy.wait()

    return pl.pallas_call(
        body, name=name,
        in_specs=[ANY] * n, out_specs=[ANY] * n,
        out_shape=[jax.ShapeDtypeStruct(p.shape, p.dtype) for p in parts],
        scratch_shapes=[pltpu.SemaphoreType.DMA((7 * n,)), pltpu.SemaphoreType.DMA((7 * n,)), pltpu.SemaphoreType.DMA((n,))],
    )(*parts)


def _allgather_small(vec, name):
    r, w = vec.shape

    def body(x_ref, out_ref, send_sems, recv_sems):
        me = _me()
        out_ref[_lin(me)] = x_ref[...]
        copies = []
        for k in range(1, N_DEV):
            rc = pltpu.make_async_remote_copy(src_ref=x_ref, dst_ref=out_ref.at[_lin(me)],
                                              send_sem=send_sems.at[k - 1], recv_sem=recv_sems.at[k - 1],
                                              device_id=_flip(me, k), device_id_type=MESH)
            rc.start()
            copies.append(rc)
        for rc in copies:
            rc.wait()

    return pl.pallas_call(
        body, name=name,
        in_specs=[pl.BlockSpec(memory_space=pltpu.VMEM)], out_specs=pl.BlockSpec(memory_space=pltpu.VMEM),
        out_shape=jax.ShapeDtypeStruct((N_DEV, r, w), vec.dtype),
        scratch_shapes=[pltpu.SemaphoreType.DMA((7,)), pltpu.SemaphoreType.DMA((7,))],
    )(vec)


def _adamw(w, g, m, v):
    m2 = ADAM_B1 * m + (1.0 - ADAM_B1) * g
    v2 = ADAM_B2 * v + (1.0 - ADAM_B2) * (g * g)
    m_hat = m2 / (1.0 - ADAM_B1 ** ADAM_STEP)
    v_hat = v2 / (1.0 - ADAM_B2 ** ADAM_STEP)
    delta = -ADAM_LR * (m_hat / (jnp.sqrt(v_hat) + ADAM_EPS) + ADAM_WD * w)
    return delta, m2, v2


def _sum_adamw(recv, w, m, v, name, tile=256):
    r, c = w.shape
    tr = min(tile, r)

    def body(p_ref, w_ref, m_ref, v_ref, g_ref, d_ref, m2_ref, v2_ref):
        g = p_ref[0].astype(F32)
        for s in range(1, N_DEV):
            g = g + p_ref[s].astype(F32)
        d, m2, v2 = _adamw(w_ref[...], g, m_ref[...], v_ref[...])
        g_ref[...] = g
        d_ref[...] = d
        m2_ref[...] = m2
        v2_ref[...] = v2

    row = pl.BlockSpec((tr, c), lambda i: (i, 0))
    return pl.pallas_call(
        body, name=name, grid=(r // tr,),
        in_specs=[pl.BlockSpec((N_DEV, tr, c), lambda i: (0, i, 0)), row, row, row], out_specs=[row] * 4,
        out_shape=[jax.ShapeDtypeStruct((r, c), F32)] * 4,
        compiler_params=_cparams("parallel"),
    )(recv, w, m, v)


def _sum_slots(gathered, name):
    _, r, w = gathered.shape

    def body(p_ref, o_ref):
        g = p_ref[0]
        for s in range(1, N_DEV):
            g = g + p_ref[s]
        o_ref[...] = g

    return pl.pallas_call(body, name=name, out_shape=jax.ShapeDtypeStruct((r, w), F32))(gathered)


def _pack(arrs):
    flat = jnp.concatenate([a.reshape(-1).astype(F32) for a in arrs])
    rows = -(-flat.shape[0] // (8 * LANES)) * 8
    return jnp.pad(flat, (0, rows * LANES - flat.shape[0])).reshape(rows, LANES)


def _unpack(packed, shapes):
    flat = packed.reshape(-1)
    out, off = [], 0
    for s in shapes:
        size = 1
        for d in s:
            size *= d
        out.append(flat[off:off + size].reshape(s))
        off += size
    return out


def _to_hm(a, d):
    t = a.shape[0]
    return a.reshape(t, HEADS, d).transpose(1, 0, 2)


def _from_hm(a):
    h, t, d = a.shape
    return a.transpose(1, 0, 2).reshape(t, h * d)


def _gate_rows(g):
    t = g.shape[0]
    return g.T.reshape(HEADS, t // CHUNK, 1, CHUNK)


def _gate_cols(g):
    h, nc, _, c = g.shape
    return g.reshape(h, nc * c).T


def _blockdiag_dense(w):
    g = w.shape[0]
    eye = jnp.eye(g, dtype=w.dtype)
    return (w[:, :, None, :] * eye[:, None, :, None]).reshape(g * QKV_BLOCK, g * QKV_BLOCK)


def _blockdiag_blocks(d):
    g = d.shape[0] // QKV_BLOCK
    d4 = d.reshape(g, QKV_BLOCK, g, QKV_BLOCK)
    eye = jnp.eye(g, dtype=d.dtype)
    return (d4 * eye[:, None, :, None]).sum(axis=2)


def _col_blocks(w):
    k, n = w.shape
    return w.reshape(k, N_DEV, n // N_DEV).transpose(1, 0, 2)


def _from_col_blocks(g):
    d, k, n = g.shape
    return g.transpose(1, 0, 2).reshape(k, d * n)


def _local_step(x, tgt, wb, ws):
    t, d = x.shape
    g1 = ws["g_pre_mix"]
    row2 = lambda a: a.reshape(1, -1)

    (h,) = _rowwise("pre_mix_norm", lambda xv, g: ((_rms(xv, g),), ()), [x], [g1], [(d, BF16)])
    proj = _mm(h, wb["w_in"], "nn", F32, "proj_in", tm=256)
    offs = [0]
    for s in IN_SPLITS:
        offs.append(offs[-1] + s)
    q_a, k_a, v_a, g_a, a_low, x_m, o_pre, gate_a, gate_b = [proj[:, offs[i]:offs[i + 1]] for i in range(9)]

    a_low_p = jnp.pad(a_low, ((0, 0), (0, LANES - LOWRANK)))
    w_a_up_p = jnp.pad(ws["w_a_up"], ((0, LANES - LOWRANK), (0, 0)))
    b_a_up = ws["b_a_up"]
    (la,) = _rowwise("gla_decay", lambda al, w, b: ((_log_decay(al, w, b),), ()), [a_low_p], [w_a_up_p, b_a_up],
                     [(HEADS * GLA_DK, F32)])
    q_hm, k_hm, la_hm = _to_hm(q_a, GLA_DK), _to_hm(k_a, GLA_DK), _to_hm(la, GLA_DK)
    o_gla, s_prev = _gla_fwd(q_hm, k_hm, v_a, la_hm)
    gn = ws["g_gla_norm"]
    (ya_in,) = _rowwise("gla_out", lambda o, g, n_: ((jnp.concatenate(_per_head(_gla_out, [o, g], [], [n_]), axis=1),), ()),
                        [o_gla, g_a], [gn], [(HEADS * HEAD_W, BF16)])
    y_a = _mm(ya_in, wb["w_pa"], "nn", F32, "proj_a")

    cw = ws["conv_w"]
    w_if_p = jnp.pad(ws["w_if"], ((0, 0), (0, LANES - 2 * HEADS)))
    ml_w = HEADS * HEAD_W
    pre_params = [cw[0:1], cw[1:2], cw[2:3], cw[3:4], ws["conv_b"],
                  _blockdiag_dense(ws["w_q_ml"]), _blockdiag_dense(ws["w_k_ml"]), _blockdiag_dense(ws["w_v_ml"]),
                  w_if_p[0:ml_w], w_if_p[ml_w:2 * ml_w], w_if_p[2 * ml_w:3 * ml_w],
                  jnp.pad(ws["b_if"], ((0, 0), (0, LANES - 2 * HEADS)))]
    x_pad = jnp.pad(x_m, ((HALO, 0), (0, 0)))
    xc, q_m, k_m, v_m, gl = _ml_pre_fwd(x_m, x_pad, pre_params)
    li, lf = _gate_rows(gl[:, 0:HEADS]), _gate_rows(gl[:, HEADS:2 * HEADS])
    hc, c_prev, n_prev, m_prev = _ml_fwd(q_m, k_m, v_m, li, lf)
    g_ml, skip = ws["g_ml_norm"], ws["ml_skip"]
    (h_b,) = _rowwise("mlstm_out", lambda a, b, c_, g, s: ((jnp.concatenate(_per_head(_ml_out, [a, b, c_], [g, s]), axis=1),), ()),
                      [hc, o_pre, xc], [g_ml, skip], [(ml_w, BF16)])
    y_b = _mm(h_b, wb["w_pb"], "nn", F32, "proj_b")

    (merged,) = _rowwise("merge", lambda ga, gb, ya, yb: ((_merge(ga, gb, ya, yb),), ()), [gate_a, gate_b, y_a, y_b], [],
                         [(d, BF16)])
    z = _mm(merged, wb["w_o"], "nn", F32, "proj_o")
    gpm, gpl, gpo = ws["g_post_mix"], ws["g_pre_mlp"], ws["g_post_mlp"]
    x1, h2 = _rowwise("post_mix", lambda xv, zv, a, b: (_post_mix(xv, zv, a, b), ()), [x, z], [gpm, gpl], [(d, F32), (d, BF16)])
    up = _mm(h2, wb["w_up"], "nn", F32, "mlp_up", tm=256)
    (u,) = _rowwise("mlp_act", lambda a: ((jnp.square(jnp.maximum(a, 0.0)),), ()), [up], [], [(up.shape[1], BF16)])
    dn = _mm(u, wb["w_down"], "nn", F32, "mlp_down")

    def loss_and_grads(x1v, dnv, tgtv, g):
        loss, vjp = jax.vjp(lambda a, b, c_: _loss_rows(a, b, tgtv, c_), x1v, dnv, g)
        dx1, ddn, dg = vjp(jnp.ones((1, 1), F32))
        return (dx1, ddn), (jnp.broadcast_to(loss, (1, LANES)), dg)

    dx1_y, d_dn, loss, d_gpo = _rowwise("loss", loss_and_grads, [x1, dn, tgt], [gpo], [(d, F32), (d, BF16)],
                                        [((1, LANES), F32), ((1, d), F32)])

    d_u = _mm(d_dn, wb["w_down"], "nt", F32, "mlp_down_dx", tm=256)
    dw_down = _mm(u, d_dn, "tn", BF16, "mlp_down_dw")
    (d_up,) = _rowwise("mlp_act_bwd", lambda du, a: ((du * (2.0 * jnp.maximum(a, 0.0)),), ()), [d_u, up], [], [(up.shape[1], BF16)])
    d_h2 = _mm(d_up, wb["w_up"], "nt", F32, "mlp_up_dx")
    dw_up = _mm(h2, d_up, "tn", BF16, "mlp_up_dw")
    def post_mix_bwd(xv, zv, dx1, dh2, a, b):
        _, vjp = jax.vjp(_post_mix, xv, zv, a, b)
        dx, dz, da, db = vjp((dx1, dh2))
        return (dx, dz), (da, db)

    dx_res, d_z, d_gpm, d_gpl = _rowwise("post_mix_bwd", post_mix_bwd, [x, z, dx1_y, d_h2], [gpm, gpl],
                                         [(d, F32), (d, BF16)], [((1, d), F32), ((1, d), F32)])
    d_merged = _mm(d_z, wb["w_o"], "nt", F32, "proj_o_dx")
    dw_o = _mm(merged, d_z, "tn", BF16, "proj_o_dw")
    d_ga, d_gb, d_ya, d_yb = _rowwise("merge_bwd", lambda *v: (jax.vjp(_merge, *v[:4])[1](v[4]), ()),
                                      [gate_a, gate_b, y_a, y_b, d_merged], [], [(d, F32), (d, F32), (d, BF16), (d, BF16)])
    d_ya_in = _mm(d_ya, wb["w_pa"], "nt", F32, "proj_a_dx")
    dw_pa = _mm(ya_in, d_ya, "tn", BF16, "proj_a_dw")
    d_hb = _mm(d_yb, wb["w_pb"], "nt", F32, "proj_b_dx")
    dw_pb = _mm(h_b, d_yb, "tn", BF16, "proj_b_dw")

    def ml_out_bwd(a, b, c_, ct, g, s):
        parts = []
        for hs in _head_slices(HEAD_W):
            _, vjp = jax.vjp(_ml_out, a[:, hs], b[:, hs], c_[:, hs], g[:, hs], s[:, hs])
            parts.append(vjp(ct[:, hs]))
        cat = lambda i: jnp.concatenate([p[i] for p in parts], axis=1)
        return (cat(0), cat(1), cat(2)), (cat(3), cat(4))

    d_hc, d_opre, d_xc, d_gml, d_skip = _rowwise("mlstm_out_bwd", ml_out_bwd, [hc, o_pre, xc, d_hb], [g_ml, skip],
                                                 [(ml_w, F32)] * 3, [((1, ml_w), F32)] * 2)
    d_qm, d_km, d_vm, d_li, d_lf = _ml_bwd(q_m, k_m, v_m, li, lf, c_prev, n_prev, m_prev, d_hc)
    d_gl = jnp.concatenate([_gate_cols(d_li), _gate_cols(d_lf), jnp.zeros((t, LANES - 2 * HEADS), F32)], axis=1)
    pre_grads = _ml_pre_bwd(x_m, x_pad, pre_params, [d_xc, d_qm, d_km, d_vm, d_gl])
    d_xm = pre_grads[0]
    d_cw = jnp.concatenate(pre_grads[1:5], axis=0)
    d_cb = pre_grads[5]
    d_wq, d_wk, d_wv = [_blockdiag_blocks(g) for g in pre_grads[6:9]]
    d_wif = jnp.concatenate(pre_grads[9:12], axis=0)[:, 0:2 * HEADS]
    d_bif = pre_grads[12][:, 0:2 * HEADS]

    def gla_out_bwd(o, g, ct, n_):
        parts = []
        for hs in _head_slices(HEAD_W):
            _, vjp = jax.vjp(_gla_out, o[:, hs], g[:, hs], n_)
            parts.append(vjp(ct[:, hs]))
        cat = lambda i: jnp.concatenate([p[i] for p in parts], axis=1)
        return (cat(0), cat(1)), (sum(p[2] for p in parts),)

    d_o, d_g_a, d_gn = _rowwise("gla_out_bwd", gla_out_bwd, [o_gla, g_a, d_ya_in], [gn], [(ml_w, F32)] * 2, [((1, HEAD_W), F32)])
    dq_hm, dk_hm, d_va, dla_hm = _gla_bwd(q_hm, k_hm, v_a, la_hm, s_prev, d_o)

    def decay_bwd(al, ct, w, b):
        _, vjp = jax.vjp(_log_decay, al, w, b)
        dal, dw, db = vjp(ct)
        return (dal,), (dw, db)

    d_alow_p, d_wa_p, d_ba = _rowwise("gla_decay_bwd", decay_bwd, [a_low_p, _from_hm(dla_hm)], [w_a_up_p, b_a_up],
                                      [(LANES, F32)], [(w_a_up_p.shape, F32), (b_a_up.shape, F32)])
    d_proj = jnp.concatenate([_from_hm(dq_hm), _from_hm(dk_hm), d_va, d_g_a, d_alow_p[:, 0:LOWRANK], d_xm, d_opre, d_ga, d_gb],
                             axis=1).astype(BF16)
    d_h = _mm(d_proj, wb["w_in"], "nt", F32, "proj_in_dx", tm=256)
    dw_in = _mm(h, d_proj, "tn", BF16, "proj_in_dw", tm=256, tk=256)

    def pre_mix_bwd(xv, dh, dres, g):
        _, vjp = jax.vjp(_rms, xv, g)
        dx, dg = vjp(dh)
        return (dx + dres,), (dg,)

    grad_x, d_g1 = _rowwise("pre_mix_norm_bwd", pre_mix_bwd, [x, d_h, dx_res], [g1], [(d, F32)], [((1, d), F32)])

    big = dict(w_in=dw_in, w_pa=dw_pa, w_pb=dw_pb, w_o=dw_o, w_up=dw_up, w_down=dw_down)
    small = dict(g_pre_mix=d_g1, w_a_up=d_wa_p[0:LOWRANK], b_a_up=d_ba, g_gla_norm=d_gn, conv_w=d_cw, conv_b=d_cb,
                 w_q_ml=d_wq, w_k_ml=d_wk, w_v_ml=d_wv, w_if=d_wif, b_if=d_bif, ml_skip=d_skip, g_ml_norm=d_gml,
                 g_post_mix=d_gpm, g_pre_mlp=d_gpl, g_post_mlp=d_gpo)
    return loss[:, 0:1], grad_x, big, small


BIG = ("w_in", "w_pa", "w_pb", "w_o", "w_up", "w_down")
BIG_COL_SHARDED = ("w_in", "w_pa", "w_pb", "w_up")
SMALL_SHARDED = {"w_a_up": 1, "conv_w": 1, "w_if": 0}
SMALL = ("g_pre_mix", "w_a_up", "b_a_up", "g_gla_norm", "conv_w", "conv_b", "w_q_ml", "w_k_ml", "w_v_ml", "w_if", "b_if",
         "ml_skip", "g_ml_norm", "g_post_mix", "g_pre_mlp", "g_post_mlp")
WEIGHTS = ("g_pre_mix", "w_in", "w_a_up", "b_a_up", "g_gla_norm", "conv_w", "conv_b", "w_q_ml", "w_k_ml", "w_v_ml", "w_if", "b_if",
           "ml_skip", "g_ml_norm", "w_pa", "w_pb", "w_o", "g_post_mix", "g_pre_mlp", "w_up", "w_down", "g_post_mlp")


def _my_slice(a, axis):
    n = a.shape[axis] // N_DEV
    return lax.dynamic_slice_in_dim(a, _lin(_me()) * n, n, axis)


def kernel(x, g_pre_mix, w_in, w_a_up, b_a_up, g_gla_norm, conv_w, conv_b, w_q_ml, w_k_ml, w_v_ml, w_if, b_if, ml_skip, g_ml_norm, w_pa, w_pb, w_o, g_post_mix, g_pre_mlp, w_up, w_down, g_post_mlp, loss_target, m_g_pre_mix, m_w_in, m_w_a_up, m_b_a_up, m_g_gla_norm, m_conv_w, m_conv_b, m_w_q_ml, m_w_k_ml, m_w_v_ml, m_w_if, m_b_if, m_ml_skip, m_g_ml_norm, m_w_pa, m_w_pb, m_w_o, m_g_post_mix, m_g_pre_mlp, m_w_up, m_w_down, m_g_post_mlp, v_g_pre_mix, v_w_in, v_w_a_up, v_b_a_up, v_g_gla_norm, v_conv_w, v_conv_b, v_w_q_ml, v_w_k_ml, v_w_v_ml, v_w_if, v_b_if, v_ml_skip, v_g_ml_norm, v_w_pa, v_w_pb, v_w_o, v_g_post_mix, v_g_pre_mlp, v_w_up, v_w_down, v_g_post_mlp):
    args = dict(locals())
    w = {n: args[n][0] for n in WEIGHTS}
    m = {n: args["m_" + n][0] for n in WEIGHTS}
    v = {n: args["v_" + n][0] for n in WEIGHTS}

    gathered = _allgather_big([w[n].astype(BF16) for n in BIG], "allgather_weights")
    wb = {}
    for n, g in zip(BIG, gathered):
        wb[n] = _from_col_blocks(g) if n in BIG_COL_SHARDED else g.reshape(-1, g.shape[-1])
    sharded_names = tuple(SMALL_SHARDED)
    small_g = _allgather_small(_pack([w[n] for n in sharded_names]), "allgather_small_weights")
    ws = {n: (w[n].reshape(1, -1) if w[n].ndim == 1 else w[n]) for n in SMALL if n not in SMALL_SHARDED}
    per_dev = [_unpack(small_g[dev], [w[n].shape for n in sharded_names]) for dev in range(N_DEV)]
    for i, n in enumerate(sharded_names):
        ws[n] = jnp.concatenate([per_dev[dev][i] for dev in range(N_DEV)], axis=SMALL_SHARDED[n])

    loss, grad_x, big, small = _local_step(x[0], loss_target[0], wb, ws)

    parts = [_col_blocks(big[n]) if n in BIG_COL_SHARDED else big[n].reshape(N_DEV, -1, big[n].shape[-1]) for n in BIG]
    recv = _exchange_big(parts, "exchange_weight_grads")
    out = {}
    for n, r in zip(BIG, recv):
        out[n] = _sum_adamw(r, w[n], m[n], v[n], "adamw_" + n)

    small_shapes = [small[n].shape for n in SMALL]
    vec = _sum_slots(_allgather_small(_pack([small[n] for n in SMALL] + [loss]), "allgather_small_grads"), "sum_small_grads")
    summed = _unpack(vec, small_shapes + [(1, 1)])
    g_small = {}
    for n, g in zip(SMALL, summed[:-1]):
        g_small[n] = _my_slice(g, SMALL_SHARDED[n]) if n in SMALL_SHARDED else g
    shapes = [g_small[n].shape for n in SMALL]
    packed = [_pack([d[n].reshape(g_small[n].shape) for n in SMALL]) for d in (w, g_small, m, v)]
    upd = _rowwise("adamw_small", lambda a, b, c_, d_: (_adamw(a, b, c_, d_), ()), packed, [], [(LANES, F32)] * 3, tile=8 * 1024)
    deltas, new_ms, new_vs = [_unpack(p, shapes) for p in upd]
    for i, n in enumerate(SMALL):
        out[n] = (g_small[n], deltas[i], new_ms[i], new_vs[i])

    shaped = lambda a, n: a.reshape(args[n].shape)
    return (summed[-1].reshape(()), grad_x[None],
            *[shaped(out[n][0], n) for n in WEIGHTS], *[shaped(out[n][1], n) for n in WEIGHTS],
            *[shaped(out[n][2], n) for n in WEIGHTS], *[shaped(out[n][3], n) for n in WEIGHTS])
```

```python
import functools

import jax
import jax.numpy as jnp
from jax import lax
from jax.experimental import pallas as pl
from jax.experimental.pallas import tpu as pltpu

F32 = jnp.float32
BF16 = jnp.bfloat16
MESH = pl.DeviceIdType.MESH

N_DEV = 8
EPS = 1e-6
CHUNK = 64
HEADS = 4
GLA_DK = 64
HEAD_W = 128
GLA_GATE_NORM = 16.0
LOWRANK = 16
CONV_K = 4
QKV_BLOCK = 4
LANES = 128
HALO = 8
IN_SPLITS = (256, 256, 512, 512, 16, 512, 512, 1024, 1024)

ADAM_LR = 0.001
ADAM_B1 = 0.9
ADAM_B2 = 0.999
ADAM_EPS = 1e-08
ADAM_WD = 0.01
ADAM_STEP = 10

VMEM_LIMIT = 56 * 1024 * 1024


def _cparams(*sem):
    return pltpu.CompilerParams(dimension_semantics=sem, vmem_limit_bytes=VMEM_LIMIT)


_DN = {"nn": (((1,), (0,)), ((), ())), "nt": (((1,), (1,)), ((), ())), "tn": (((0,), (0,)), ((), ()))}


def _raw_dot(a, b, mode):
    return lax.dot_general(a.astype(BF16), b.astype(BF16), _DN[mode], preferred_element_type=F32)


@functools.partial(jax.custom_vjp, nondiff_argnums=(2,))
def _bdot(a, b, mode):
    return _raw_dot(a, b, mode)


def _bdot_fwd(a, b, mode):
    return _raw_dot(a, b, mode), (a, b)


def _bdot_bwd(mode, res, ct):
    a, b = res
    if mode == "nn":
        da, db = _raw_dot(ct, b, "nt"), _raw_dot(a, ct, "tn")
    elif mode == "nt":
        da, db = _raw_dot(ct, b, "nn"), _raw_dot(ct, a, "tn")
    else:
        da, db = _raw_dot(b, ct, "nt"), _raw_dot(a, ct, "nn")
    return da.astype(a.dtype), db.astype(b.dtype)


_bdot.defvjp(_bdot_fwd, _bdot_bwd)


def _split3(x):
    hi = x.astype(BF16)
    r1 = x - hi.astype(F32)
    mid = r1.astype(BF16)
    return hi, mid, (r1 - mid.astype(F32)).astype(BF16)


def _split_dot(tri, x):
    return sum(lax.dot_general(tri, t, _DN["nn"], preferred_element_type=F32) for t in _split3(x))


def _tri(n, lower):
    r = lax.broadcasted_iota(jnp.int32, (n, n), 0)
    c = lax.broadcasted_iota(jnp.int32, (n, n), 1)
    return ((c <= r) if lower else (c >= r)).astype(BF16)


@jax.custom_vjp
def _cumsum_rows(x):
    return _split_dot(_tri(x.shape[0], True), x)


def _cumsum_rows_fwd(x):
    return _cumsum_rows(x), None


def _cumsum_rows_bwd(_, ct):
    return (_split_dot(_tri(ct.shape[0], False), ct),)


_cumsum_rows.defvjp(_cumsum_rows_fwd, _cumsum_rows_bwd)


def _abs(x):
    return jnp.where(x >= 0, x, -x)


def _sigmoid(x):
    return lax.logistic(x)


def _log_sigmoid(x):
    return jnp.minimum(x, 0.0) - jnp.log(1.0 + jnp.exp(-_abs(x)))


def _rms(x, g):
    return x * lax.rsqrt(jnp.mean(x * x, axis=-1, keepdims=True) + EPS) * g


def _head_slices(w):
    return [slice(h * w, (h + 1) * w) for h in range(HEADS)]


def _tile(dim, want):
    if dim <= want or dim % LANES:
        return dim
    t = want
    while dim % t:
        t -= LANES
    return t


def _mm(a, b, mode, out_dtype, name, tm=1024, tn=1024, tk=4096, epilogue=None, extra=()):
    if mode == "nn":
        (m, k), (k2, n) = a.shape, b.shape
    elif mode == "nt":
        (m, k), (n, k2) = a.shape, b.shape
    else:
        (k, m), (k2, n) = a.shape, b.shape
    assert k == k2, (name, a.shape, b.shape)
    tm, tn, tk = _tile(m, tm), _tile(n, tn), _tile(k, tk)
    nk = k // tk
    out_dtypes = out_dtype if epilogue else (out_dtype,)
    assert nk == 1 or (out_dtype == F32 and not epilogue), name
    n_in = 2 + len(extra)

    def body(*refs):
        p = _raw_dot(refs[0][...], refs[1][...], mode)
        if nk > 1:
            _accumulate(pl.program_id(2), [refs[2]], [p])
            return
        outs = epilogue(p, *[r[...] for r in refs[2:n_in]]) if epilogue else (p,)
        for ref, val in zip(refs[n_in:], outs):
            ref[...] = val.astype(ref.dtype)

    a_spec = pl.BlockSpec((tk, tm), lambda i, j, kk: (kk, i)) if mode == "tn" else pl.BlockSpec((tm, tk), lambda i, j, kk: (i, kk))
    b_spec = pl.BlockSpec((tn, tk), lambda i, j, kk: (j, kk)) if mode == "nt" else pl.BlockSpec((tk, tn), lambda i, j, kk: (kk, j))
    o_spec = pl.BlockSpec((tm, tn), lambda i, j, kk: (i, j))
    res = pl.pallas_call(
        body, name=name, grid=(m // tm, n // tn, nk),
        in_specs=[a_spec, b_spec] + [o_spec] * len(extra), out_specs=[o_spec] * len(out_dtypes),
        out_shape=[jax.ShapeDtypeStruct((m, n), dt) for dt in out_dtypes],
        compiler_params=_cparams("parallel", "parallel", "arbitrary"),
    )(a, b, *extra)
    return res if epilogue else res[0]


def _rowwise(name, fn, rows, params, out_rows, out_accs=(), tile=256):
    t = rows[0].shape[0]
    r = min(tile, t)
    assert t % r == 0
    n_in, n_or = len(rows) + len(params), len(out_rows)

    def body(*refs):
        vals = [ref[...] for ref in refs[:n_in]]
        outs = refs[n_in:]
        ro, ao = fn(*vals)
        for ref, v in zip(outs[:n_or], ro):
            ref[...] = v.astype(ref.dtype)
        if out_accs:
            _accumulate(pl.program_id(0), outs[n_or:], ao)

    def full(shape):
        return pl.BlockSpec(shape, lambda i, nd=len(shape): (0,) * nd)

    return pl.pallas_call(
        body, name=name, grid=(t // r,),
        in_specs=[pl.BlockSpec((r, a.shape[1]), lambda i: (i, 0)) for a in rows] + [full(p.shape) for p in params],
        out_specs=[pl.BlockSpec((r, w), lambda i: (i, 0)) for w, _ in out_rows] + [full(s) for s, _ in out_accs],
        out_shape=[jax.ShapeDtypeStruct((t, w), dt) for w, dt in out_rows] + [jax.ShapeDtypeStruct(s, dt) for s, dt in out_accs],
        compiler_params=_cparams("arbitrary"),
    )(*rows, *params)


def _accumulate(step, refs, vals):
    for ref, v in zip(refs, vals):
        @pl.when(step == 0)
        def _(ref=ref, v=v):
            ref[...] = v.astype(ref.dtype)

        @pl.when(step > 0)
        def _(ref=ref, v=v):
            ref[...] += v.astype(ref.dtype)


def _gla_chunk(q, k, v, la, st):
    c = q.shape[0]
    row = lax.broadcasted_iota(jnp.int32, (c, c), 0)
    col = lax.broadcasted_iota(jnp.int32, (c, c), 1)
    cum = _cumsum_rows(la)
    cl = jnp.sum(la, axis=0, keepdims=True)
    ep = jnp.exp(cum)
    en = jnp.exp(-cum)
    qs = q * (GLA_DK ** -0.5)
    qp = qs * ep
    a_f = _bdot(qp, k * en, "nt")
    a_b = _bdot(qs * en, k * ep, "nt")
    sc = jnp.where(row >= col, a_f, a_b)
    o = _bdot(sc, v, "nn") + _bdot(qp, st, "nt")
    kd = k * jnp.exp(cl - cum)
    st_new = st * jnp.exp(cl) + _bdot(v, kd, "tn")
    return o, st_new


def _gla_specs(nc, rev):
    def ch(n):
        return (nc - 1 - n) if rev else n
    hm = pl.BlockSpec((None, CHUNK, GLA_DK), lambda h, n: (h, ch(n), 0))
    tm = pl.BlockSpec((CHUNK, HEAD_W), lambda h, n: (ch(n), h))
    st = pl.BlockSpec((None, None, HEAD_W, GLA_DK), lambda h, n: (h, ch(n), 0, 0))
    return hm, tm, st


def _gla_fwd(q, k, v, la):
    t = v.shape[0]
    nc = t // CHUNK
    hm, tm, st = _gla_specs(nc, False)

    def body(q_ref, k_ref, v_ref, la_ref, o_ref, sp_ref, st_ref):
        @pl.when(pl.program_id(1) == 0)
        def _():
            st_ref[...] = jnp.zeros_like(st_ref)

        s = st_ref[...]
        sp_ref[...] = s
        o, s_new = _gla_chunk(q_ref[...], k_ref[...], v_ref[...], la_ref[...], s)
        o_ref[...] = o
        st_ref[...] = s_new

    return pl.pallas_call(
        body, name="gla_fwd", grid=(HEADS, nc),
        in_specs=[hm, hm, tm, hm], out_specs=[tm, st],
        out_shape=[jax.ShapeDtypeStruct((t, HEADS * HEAD_W), F32), jax.ShapeDtypeStruct((HEADS, nc, HEAD_W, GLA_DK), F32)],
        scratch_shapes=[pltpu.VMEM((HEAD_W, GLA_DK), F32)],
        compiler_params=_cparams("arbitrary", "arbitrary"),
    )(q, k, v, la)


def _gla_bwd(q, k, v, la, sp, do):
    t = v.shape[0]
    nc = t // CHUNK
    hm, tm, st = _gla_specs(nc, True)

    def body(q_ref, k_ref, v_ref, la_ref, sp_ref, do_ref, dq_ref, dk_ref, dv_ref, dla_ref, ds_ref):
        @pl.when(pl.program_id(1) == 0)
        def _():
            ds_ref[...] = jnp.zeros_like(ds_ref)

        _, vjp = jax.vjp(_gla_chunk, q_ref[...], k_ref[...], v_ref[...], la_ref[...], sp_ref[...])
        dq, dk, dv, dla, ds = vjp((do_ref[...], ds_ref[...]))
        dq_ref[...] = dq
        dk_ref[...] = dk
        dv_ref[...] = dv
        dla_ref[...] = dla
        ds_ref[...] = ds

    hm_shape = jax.ShapeDtypeStruct((HEADS, t, GLA_DK), F32)
    return pl.pallas_call(
        body, name="gla_bwd", grid=(HEADS, nc),
        in_specs=[hm, hm, tm, hm, st, tm], out_specs=[hm, hm, tm, hm],
        out_shape=[hm_shape, hm_shape, jax.ShapeDtypeStruct((t, HEADS * HEAD_W), F32), hm_shape],
        scratch_shapes=[pltpu.VMEM((HEAD_W, GLA_DK), F32)],
        compiler_params=_cparams("arbitrary", "arbitrary"),
    )(q, k, v, la, sp, do)


def _ml_chunk(q, k, v, li_r, lf_r, cm, nv, m):
    c = q.shape[0]
    row = lax.broadcasted_iota(jnp.int32, (c, c), 0)
    col = lax.broadcasted_iota(jnp.int32, (c, c), 1)
    eye = (row == col).astype(F32)
    li_c = jnp.sum(eye * li_r, axis=1, keepdims=True)
    lf_c = jnp.sum(eye * lf_r, axis=1, keepdims=True)
    fc_c = jnp.sum((col <= row).astype(F32) * lf_r, axis=1, keepdims=True)
    fc_r = jnp.sum((row <= col).astype(F32) * lf_c, axis=0, keepdims=True)
    f_last = jnp.sum(lf_r, axis=1, keepdims=True)
    kc = k * (HEAD_W ** -0.5)
    a_c = f_last - fc_c + li_c
    m_loc = jnp.max(a_c, axis=0, keepdims=True)
    kw = kc * jnp.exp(a_c - m_loc)
    c_chunk = _bdot(kw, v, "tn")
    n_chunk = jnp.sum(kw, axis=0, keepdims=True)
    m_new = jnp.maximum(f_last + m, m_loc)
    sp = jnp.exp(f_last + m - m_new)
    sl = jnp.exp(m_loc - m_new)
    cm_new = sp * cm + sl * c_chunk
    nv_new = sp * nv + sl * n_chunk
    log_d = li_r - _abs(fc_c - fc_r)
    g_inter = fc_c + m
    m_t = jnp.maximum(g_inter, jnp.max(log_d, axis=1, keepdims=True))
    s = _bdot(q, kc, "nt") * jnp.exp(log_d - m_t)
    sc = jnp.exp(g_inter - m_t)
    num = _bdot(s, v, "nn") + sc * _bdot(q, cm, "nn")
    den = jnp.sum(s, axis=1, keepdims=True) + sc * jnp.sum(q * nv, axis=1, keepdims=True)
    den = jnp.maximum(_abs(den), jnp.exp(-m_t))
    return num / den, cm_new, nv_new, m_new


def _ml_specs(nc, rev):
    def ch(n):
        return (nc - 1 - n) if rev else n
    tm = pl.BlockSpec((CHUNK, HEAD_W), lambda h, n: (ch(n), h))
    gate = pl.BlockSpec((None, None, 1, CHUNK), lambda h, n: (h, ch(n), 0, 0))
    cm = pl.BlockSpec((None, None, HEAD_W, HEAD_W), lambda h, n: (h, ch(n), 0, 0))
    vec = pl.BlockSpec((None, None, 1, HEAD_W), lambda h, n: (h, ch(n), 0, 0))
    return tm, gate, cm, vec


def _ml_fwd(q, k, v, li, lf):
    t = q.shape[0]
    nc = t // CHUNK
    tm, gate, cm, vec = _ml_specs(nc, False)

    def body(q_ref, k_ref, v_ref, li_ref, lf_ref, hc_ref, cp_ref, np_ref, mp_ref, c_ref, n_ref, m_ref):
        @pl.when(pl.program_id(1) == 0)
        def _():
            c_ref[...] = jnp.zeros_like(c_ref)
            n_ref[...] = jnp.zeros_like(n_ref)
            m_ref[...] = jnp.zeros_like(m_ref)

        c0, n0, m0 = c_ref[...], n_ref[...], m_ref[...]
        cp_ref[...] = c0
        np_ref[...] = n0
        mp_ref[...] = m0
        hc, c1, n1, m1 = _ml_chunk(q_ref[...], k_ref[...], v_ref[...], li_ref[...], lf_ref[...], c0, n0, m0[:, 0:1])
        hc_ref[...] = hc
        c_ref[...] = c1
        n_ref[...] = n1
        m_ref[...] = jnp.broadcast_to(m1, m_ref.shape)

    return pl.pallas_call(
        body, name="mlstm_fwd", grid=(HEADS, nc),
        in_specs=[tm, tm, tm, gate, gate], out_specs=[tm, cm, vec, vec],
        out_shape=[jax.ShapeDtypeStruct((t, HEADS * HEAD_W), F32), jax.ShapeDtypeStruct((HEADS, nc, HEAD_W, HEAD_W), F32),
                   jax.ShapeDtypeStruct((HEADS, nc, 1, HEAD_W), F32), jax.ShapeDtypeStruct((HEADS, nc, 1, HEAD_W), F32)],
        scratch_shapes=[pltpu.VMEM((HEAD_W, HEAD_W), F32), pltpu.VMEM((1, HEAD_W), F32), pltpu.VMEM((1, HEAD_W), F32)],
        compiler_params=_cparams("arbitrary", "arbitrary"),
    )(q, k, v, li, lf)


def _ml_bwd(q, k, v, li, lf, cp, npv, mp, dhc):
    t = q.shape[0]
    nc = t // CHUNK
    tm, gate, cm, vec = _ml_specs(nc, True)

    def body(q_ref, k_ref, v_ref, li_ref, lf_ref, cp_ref, np_ref, mp_ref, dhc_ref,
             dq_ref, dk_ref, dv_ref, dli_ref, dlf_ref, dc_ref, dn_ref, dm_ref):
        @pl.when(pl.program_id(1) == 0)
        def _():
            dc_ref[...] = jnp.zeros_like(dc_ref)
            dn_ref[...] = jnp.zeros_like(dn_ref)
            dm_ref[...] = jnp.zeros_like(dm_ref)

        _, vjp = jax.vjp(_ml_chunk, q_ref[...], k_ref[...], v_ref[...], li_ref[...], lf_ref[...],
                         cp_ref[...], np_ref[...], mp_ref[...][:, 0:1])
        dq, dk, dv, dli, dlf, dc, dn, dm = vjp((dhc_ref[...], dc_ref[...], dn_ref[...], dm_ref[...][:, 0:1]))
        dq_ref[...] = dq
        dk_ref[...] = dk
        dv_ref[...] = dv
        dli_ref[...] = dli
        dlf_ref[...] = dlf
        dc_ref[...] = dc
        dn_ref[...] = dn
        dm_ref[...] = jnp.broadcast_to(dm, dm_ref.shape)

    tm_shape = jax.ShapeDtypeStruct((t, HEADS * HEAD_W), F32)
    gate_shape = jax.ShapeDtypeStruct((HEADS, nc, 1, CHUNK), F32)
    return pl.pallas_call(
        body, name="mlstm_bwd", grid=(HEADS, nc),
        in_specs=[tm, tm, tm, gate, gate, cm, vec, vec, tm], out_specs=[tm, tm, tm, gate, gate],
        out_shape=[tm_shape, tm_shape, tm_shape, gate_shape, gate_shape],
        scratch_shapes=[pltpu.VMEM((HEAD_W, HEAD_W), F32), pltpu.VMEM((1, HEAD_W), F32), pltpu.VMEM((1, HEAD_W), F32)],
        compiler_params=_cparams("arbitrary", "arbitrary"),
    )(q, k, v, li, lf, cp, npv, mp, dhc)


def _ml_pre(s0, s1, s2, s3, cw0, cw1, cw2, cw3, cb, wq, wk, wv, wiq, wik, wiv, bif):
    pre = cb + cw0 * s0 + cw1 * s1 + cw2 * s2 + cw3 * s3
    xc = pre * _sigmoid(pre)
    q = _bdot(xc, wq, "nn")
    k = _bdot(xc, wk, "nn")
    v = _bdot(s3, wv, "nn")
    gates = _bdot(q, wiq, "nn") + _bdot(k, wik, "nn") + _bdot(v, wiv, "nn") + bif
    lane = lax.broadcasted_iota(jnp.int32, gates.shape, 1)
    gl = jnp.where(lane < HEADS, gates, _log_sigmoid(gates))
    return xc, q, k, v, gl


def _delayed(xs_ref, x_ref, halo_ref, r):
    xs_ref[0:HALO, :] = halo_ref[...]
    xs_ref[HALO:HALO + r, :] = x_ref[...]
    return [xs_ref[pl.ds(HALO - (CONV_K - 1) + j, r), :] for j in range(CONV_K)]


def _full_spec(shape):
    return pl.BlockSpec(shape, lambda i, nd=len(shape): (0,) * nd)


def _ml_pre_fwd(x_m, x_pad, params, tile=256):
    t, w = x_m.shape
    r = min(tile, t)

    def body(*refs):
        x_ref, halo_ref = refs[:2]
        p = [ref[...] for ref in refs[2:2 + len(params)]]
        outs = refs[2 + len(params):-1]
        res = _ml_pre(*_delayed(refs[-1], x_ref, halo_ref, r), *p)
        for ref, val in zip(outs, res):
            ref[...] = val

    row = pl.BlockSpec((r, w), lambda i: (i, 0))
    return pl.pallas_call(
        body, name="ml_pre_fwd", grid=(t // r,),
        in_specs=[row, pl.BlockSpec((HALO, w), lambda i: (i * (r // HALO), 0))] + [_full_spec(p.shape) for p in params],
        out_specs=[row] * 4 + [pl.BlockSpec((r, LANES), lambda i: (i, 0))],
        out_shape=[jax.ShapeDtypeStruct((t, w), F32)] * 4 + [jax.ShapeDtypeStruct((t, LANES), F32)],
        scratch_shapes=[pltpu.VMEM((r + HALO, w), F32)],
        compiler_params=_cparams("arbitrary"),
    )(x_m, x_pad, *params)


def _ml_pre_bwd(x_m, x_pad, params, cts, tile=256):
    t, w = x_m.shape
    r = min(tile, t)
    nt = t // r
    n_p = len(params)

    def body(*refs):
        x_ref, halo_ref = refs[:2]
        p = [ref[...] for ref in refs[2:2 + n_p]]
        ct = [ref[...] for ref in refs[2 + n_p:7 + n_p]]
        dx_ref = refs[7 + n_p]
        dp_refs = refs[8 + n_p:8 + 2 * n_p]
        xs_ref, ds_ref, carry_ref = refs[8 + 2 * n_p:]
        step = pl.program_id(0)

        @pl.when(step == 0)
        def _():
            ds_ref[...] = jnp.zeros_like(ds_ref)
            carry_ref[...] = jnp.zeros_like(carry_ref)

        _, vjp = jax.vjp(_ml_pre, *_delayed(xs_ref, x_ref, halo_ref, r), *p)
        grads = vjp(tuple(ct))
        for j in range(CONV_K):
            ds_ref[j, HALO:HALO + r, :] = grads[j]
        lead = HALO + CONV_K - 1
        d_tile = sum(ds_ref[j, pl.ds(lead - j, r), :] for j in range(CONV_K))
        d_halo = sum(ds_ref[j, pl.ds(CONV_K - 1 - j, HALO), :] for j in range(CONV_K))
        dx_ref[...] = d_tile
        dx_ref[r - HALO:r, :] += carry_ref[...]
        carry_ref[...] = d_halo
        _accumulate(step, dp_refs, grads[CONV_K:])

    row = pl.BlockSpec((r, w), lambda i: (nt - 1 - i, 0))
    return pl.pallas_call(
        body, name="ml_pre_bwd", grid=(nt,),
        in_specs=[row, pl.BlockSpec((HALO, w), lambda i: ((nt - 1 - i) * (r // HALO), 0))] + [_full_spec(p.shape) for p in params]
        + [row] * 4 + [pl.BlockSpec((r, LANES), lambda i: (nt - 1 - i, 0))],
        out_specs=[row] + [_full_spec(p.shape) for p in params],
        out_shape=[jax.ShapeDtypeStruct((t, w), F32)] + [jax.ShapeDtypeStruct(p.shape, F32) for p in params],
        scratch_shapes=[pltpu.VMEM((r + HALO, w), F32), pltpu.VMEM((CONV_K, r + 2 * HALO, w), F32), pltpu.VMEM((HALO, w), F32)],
        compiler_params=_cparams("arbitrary"),
    )(x_m, x_pad, *params, *cts)


def _per_head(fn, row_vals, head_params, shared_params=()):
    return [fn(*[a[:, hs] for a in row_vals], *[p[:, hs] for p in head_params], *shared_params) for hs in _head_slices(HEAD_W)]


def _gla_out(o, g, gn):
    return _rms(o, gn) * (g * _sigmoid(g))


def _ml_out(hc, op, xc, g, sk):
    hcell = hc * _sigmoid(op)
    mu = jnp.mean(hcell, axis=-1, keepdims=True)
    d = hcell - mu
    var = jnp.mean(d * d, axis=-1, keepdims=True)
    return d * lax.rsqrt(var + EPS) * g + sk * xc


def _log_decay(al, w, b):
    return _log_sigmoid(_bdot(al, w, "nn") + b) * (1.0 / GLA_GATE_NORM)


def _merge(ga, gb, ya, yb):
    return _sigmoid(ga) * ya + _sigmoid(gb) * yb


def _post_mix(x, z, gpm, gpl):
    x1 = x + _rms(z, gpm)
    return x1, _rms(x1, gpl)


def _loss_rows(x1, dn, tgt, g):
    e = x1 + _rms(dn, g) - tgt
    return 0.5 * jnp.sum(jnp.mean(e * e, axis=-1, keepdims=True), axis=0, keepdims=True)


def _lin(p):
    return 4 * p[0] + 2 * p[1] + p[2]


def _me():
    return lax.axis_index("x"), lax.axis_index("y"), lax.axis_index("c")


def _flip(p, k):
    return tuple((1 - v) if (k >> (2 - i)) & 1 else v for i, v in enumerate(p))


ANY = pl.BlockSpec(memory_space=pl.ANY)


def _allgather_big(shards, name):
    n = len(shards)

    def body(*refs):
        ins, outs = refs[:n], refs[n:2 * n]
        send_sems, recv_sems, local_sems = refs[2 * n:]
        me = _me()
        x, y, c = me
        sib = (x, y, 1 - c)
        chips = [(1 - x, y), (x, 1 - y), (1 - x, 1 - y)]

        def cp(a, k, block, to, src=None):
            dst = outs[a].at[_lin(block)]
            return pltpu.make_async_remote_copy(src_ref=dst if src is None else src, dst_ref=dst,
                                                send_sem=send_sems.at[a * 7 + k], recv_sem=recv_sems.at[a * 7 + k],
                                                device_id=to, device_id_type=MESH)

        mine = [pltpu.make_async_copy(ins[a], outs[a].at[_lin(me)], local_sems.at[a]) for a in range(n)]
        for m in mine:
            m.start()
        first = []
        for a in range(n):
            first.append(cp(a, 0, me, sib, src=ins[a]))
            first += [cp(a, 1 + j, me, (*chip, c), src=ins[a]) for j, chip in enumerate(chips)]
        for f in first:
            f.start()
        passed = []
        for j, chip in enumerate(chips):
            for a in range(n):
                cp(a, 1 + j, (*chip, c), me).wait_recv()
                fwd = cp(a, 4 + j, (*chip, c), sib)
                fwd.start()
                passed.append(fwd)
        for a in range(n):
            cp(a, 0, sib, me).wait_recv()
            for j, chip in enumerate(chips):
                cp(a, 4 + j, (*chip, 1 - c), me).wait_recv()
        for f in first + passed:
            f.wait_send()
        for m in mine:
            m.wait()

    return pl.pallas_call(
        body, name=name,
        in_specs=[ANY] * n, out_specs=[ANY] * n,
        out_shape=[jax.ShapeDtypeStruct((N_DEV, *s.shape), s.dtype) for s in shards],
        scratch_shapes=[pltpu.SemaphoreType.DMA((7 * n,)), pltpu.SemaphoreType.DMA((7 * n,)), pltpu.SemaphoreType.DMA((n,))],
    )(*shards)


def _exchange_big(parts, name):
    n = len(parts)

    def body(*refs):
        ins, outs = refs[:n], refs[n:2 * n]
        send_sems, recv_sems, local_sems = refs[2 * n:]
        me = _me()
        copies = []
        for a in range(n):
            loc = pltpu.make_async_copy(ins[a].at[_lin(me)], outs[a].at[_lin(me)], local_sems.at[a])
            loc.start()
            copies.append(loc)
            for k in range(1, N_DEV):
                peer = _flip(me, k)
                rc = pltpu.make_async_remote_copy(src_ref=ins[a].at[_lin(peer)], dst_ref=outs[a].at[_lin(me)],
                                                  send_sem=send_sems.at[a * 7 + k - 1], recv_sem=recv_sems.at[a * 7 + k - 1],
                                                  device_id=peer, device_id_type=MESH)
                rc.start()
                copies.append(rc)
        for cpy in copies:
            cpy.wait()

    return pl.pallas_call(
        body, name=name,
        in_specs=[ANY] * n, out_specs=[ANY] * n,
        out_shape=[jax.ShapeDtypeStruct(p.shape, p.dtype) for p in parts],
        scratch_shapes=[pltpu.SemaphoreType.DMA((7 * n,)), pltpu.SemaphoreType.DMA((7 * n,)), pltpu.SemaphoreType.DMA((n,))],
    )(*parts)


def _allgather_small(vec, name):
    r, w = vec.shape

    def body(x_ref, out_ref, send_sems, recv_sems):
        me = _me()
        out_ref[_lin(me)] = x_ref[...]
        copies = []
        for k in range(1, N_DEV):
            rc = pltpu.make_async_remote_copy(src_ref=x_ref, dst_ref=out_ref.at[_lin(me)],
                                              send_sem=send_sems.at[k - 1], recv_sem=recv_sems.at[k - 1],
                                              device_id=_flip(me, k), device_id_type=MESH)
            rc.start()
            copies.append(rc)
        for rc in copies:
            rc.wait()

    return pl.pallas_call(
        body, name=name,
        in_specs=[pl.BlockSpec(memory_space=pltpu.VMEM)], out_specs=pl.BlockSpec(memory_space=pltpu.VMEM),
        out_shape=jax.ShapeDtypeStruct((N_DEV, r, w), vec.dtype),
        scratch_shapes=[pltpu.SemaphoreType.DMA((7,)), pltpu.SemaphoreType.DMA((7,))],
    )(vec)


def _adamw(w, g, m, v):
    m2 = ADAM_B1 * m + (1.0 - ADAM_B1) * g
    v2 = ADAM_B2 * v + (1.0 - ADAM_B2) * (g * g)
    m_hat = m2 / (1.0 - ADAM_B1 ** ADAM_STEP)
    v_hat = v2 / (1.0 - ADAM_B2 ** ADAM_STEP)
    delta = -ADAM_LR * (m_hat / (jnp.sqrt(v_hat) + ADAM_EPS) + ADAM_WD * w)
    return delta, m2, v2


def _sum_adamw(recv, w, m, v, name, tile=256):
    r, c = w.shape
    tr = min(tile, r)

    def body(p_ref, w_ref, m_ref, v_ref, g_ref, d_ref, m2_ref, v2_ref):
        g = p_ref[0].astype(F32)
        for s in range(1, N_DEV):
            g = g + p_ref[s].astype(F32)
        d, m2, v2 = _adamw(w_ref[...], g, m_ref[...], v_ref[...])
        g_ref[...] = g
        d_ref[...] = d
        m2_ref[...] = m2
        v2_ref[...] = v2

    row = pl.BlockSpec((tr, c), lambda i: (i, 0))
    return pl.pallas_call(
        body, name=name, grid=(r // tr,),
        in_specs=[pl.BlockSpec((N_DEV, tr, c), lambda i: (0, i, 0)), row, row, row], out_specs=[row] * 4,
        out_shape=[jax.ShapeDtypeStruct((r, c), F32)] * 4,
        compiler_params=_cparams("parallel"),
    )(recv, w, m, v)


def _sum_slots(gathered, name):
    _, r, w = gathered.shape

    def body(p_ref, o_ref):
        g = p_ref[0]
        for s in range(1, N_DEV):
            g = g + p_ref[s]
        o_ref[...] = g

    return pl.pallas_call(body, name=name, out_shape=jax.ShapeDtypeStruct((r, w), F32))(gathered)


def _pack(arrs):
    flat = jnp.concatenate([a.reshape(-1).astype(F32) for a in arrs])
    rows = -(-flat.shape[0] // (8 * LANES)) * 8
    return jnp.pad(flat, (0, rows * LANES - flat.shape[0])).reshape(rows, LANES)


def _unpack(packed, shapes):
    flat = packed.reshape(-1)
    out, off = [], 0
    for s in shapes:
        size = 1
        for d in s:
            size *= d
        out.append(flat[off:off + size].reshape(s))
        off += size
    return out


def _to_hm(a, d):
    t = a.shape[0]
    return a.reshape(t, HEADS, d).transpose(1, 0, 2)


def _from_hm(a):
    h, t, d = a.shape
    return a.transpose(1, 0, 2).reshape(t, h * d)


def _gate_rows(g):
    t = g.shape[0]
    return g.T.reshape(HEADS, t // CHUNK, 1, CHUNK)


def _gate_cols(g):
    h, nc, _, c = g.shape
    return g.reshape(h, nc * c).T


def _blockdiag_dense(w):
    n = w.shape[0] * QKV_BLOCK
    tiled = jnp.tile(w.reshape(n, QKV_BLOCK), (1, n // QKV_BLOCK))
    r = lax.broadcasted_iota(jnp.int32, (n, n), 0)
    c = lax.broadcasted_iota(jnp.int32, (n, n), 1)
    return jnp.where(r // QKV_BLOCK == c // QKV_BLOCK, tiled, 0.0)


def _blockdiag_blocks(dense):
    n = dense[0].shape[0]
    k = len(dense)

    def body(*refs):
        r = lax.broadcasted_iota(jnp.int32, (n, n), 0)
        c = lax.broadcasted_iota(jnp.int32, (n, n), 1)
        fr = lax.broadcasted_iota(jnp.int32, (n, LANES), 0)
        fc = lax.broadcasted_iota(jnp.int32, (n, LANES), 1)
        fold = ((fr & (QKV_BLOCK - 1)) == fc).astype(BF16)
        for i in range(k):
            kept = jnp.where((r >> 2) == (c >> 2), refs[i][...], 0.0)
            refs[k + i][...] = sum(lax.dot_general(t, fold, _DN["nn"], preferred_element_type=F32) for t in _split3(kept))

    out = pl.pallas_call(body, name="blockdiag_blocks", out_shape=[jax.ShapeDtypeStruct((n, LANES), F32)] * k)(*dense)
    return [o[:, 0:QKV_BLOCK].reshape(n // QKV_BLOCK, QKV_BLOCK, QKV_BLOCK) for o in out]


def _col_blocks(w):
    k, n = w.shape
    return w.reshape(k, N_DEV, n // N_DEV).transpose(1, 0, 2)


def _from_col_blocks(g):
    d, k, n = g.shape
    return g.transpose(1, 0, 2).reshape(k, d * n)


def _local_step(x, tgt, wb, ws):
    t, d = x.shape
    g1 = ws["g_pre_mix"]
    row2 = lambda a: a.reshape(1, -1)

    (h,) = _rowwise("pre_mix_norm", lambda xv, g: ((_rms(xv, g),), ()), [x], [g1], [(d, BF16)])
    proj = _mm(h, wb["w_in"], "nn", F32, "proj_in", tm=512)
    offs = [0]
    for s in IN_SPLITS:
        offs.append(offs[-1] + s)
    q_a, k_a, v_a, g_a, a_low, x_m, o_pre, gate_a, gate_b = [proj[:, offs[i]:offs[i + 1]] for i in range(9)]

    a_low_p = jnp.pad(a_low, ((0, 0), (0, LANES - LOWRANK)))
    w_a_up_p = jnp.pad(ws["w_a_up"], ((0, LANES - LOWRANK), (0, 0)))
    b_a_up = ws["b_a_up"]
    (la,) = _rowwise("gla_decay", lambda al, w, b: ((_log_decay(al, w, b),), ()), [a_low_p], [w_a_up_p, b_a_up],
                     [(HEADS * GLA_DK, F32)])
    q_hm, k_hm, la_hm = _to_hm(q_a, GLA_DK), _to_hm(k_a, GLA_DK), _to_hm(la, GLA_DK)
    o_gla, s_prev = _gla_fwd(q_hm, k_hm, v_a, la_hm)
    gn = ws["g_gla_norm"]
    (ya_in,) = _rowwise("gla_out", lambda o, g, n_: ((jnp.concatenate(_per_head(_gla_out, [o, g], [], [n_]), axis=1),), ()),
                        [o_gla, g_a], [gn], [(HEADS * HEAD_W, BF16)])
    y_a = _mm(ya_in, wb["w_pa"], "nn", F32, "proj_a")

    cw = ws["conv_w"]
    w_if_p = jnp.pad(ws["w_if"], ((0, 0), (0, LANES - 2 * HEADS)))
    ml_w = HEADS * HEAD_W
    pre_params = [cw[0:1], cw[1:2], cw[2:3], cw[3:4], ws["conv_b"],
                  _blockdiag_dense(ws["w_q_ml"]), _blockdiag_dense(ws["w_k_ml"]), _blockdiag_dense(ws["w_v_ml"]),
                  w_if_p[0:ml_w], w_if_p[ml_w:2 * ml_w], w_if_p[2 * ml_w:3 * ml_w],
                  jnp.pad(ws["b_if"], ((0, 0), (0, LANES - 2 * HEADS)))]
    x_pad = jnp.pad(x_m, ((HALO, 0), (0, 0)))
    xc, q_m, k_m, v_m, gl = _ml_pre_fwd(x_m, x_pad, pre_params)
    li, lf = _gate_rows(gl[:, 0:HEADS]), _gate_rows(gl[:, HEADS:2 * HEADS])
    hc, c_prev, n_prev, m_prev = _ml_fwd(q_m, k_m, v_m, li, lf)
    g_ml, skip = ws["g_ml_norm"], ws["ml_skip"]
    (h_b,) = _rowwise("mlstm_out", lambda a, b, c_, g, s: ((jnp.concatenate(_per_head(_ml_out, [a, b, c_], [g, s]), axis=1),), ()),
                      [hc, o_pre, xc], [g_ml, skip], [(ml_w, BF16)])
    y_b = _mm(h_b, wb["w_pb"], "nn", F32, "proj_b")

    (merged,) = _rowwise("merge", lambda ga, gb, ya, yb: ((_merge(ga, gb, ya, yb),), ()), [gate_a, gate_b, y_a, y_b], [],
                         [(d, BF16)])
    z = _mm(merged, wb["w_o"], "nn", F32, "proj_o")
    gpm, gpl, gpo = ws["g_post_mix"], ws["g_pre_mlp"], ws["g_post_mlp"]
    x1, h2 = _rowwise("post_mix", lambda xv, zv, a, b: (_post_mix(xv, zv, a, b), ()), [x, z], [gpm, gpl], [(d, F32), (d, BF16)])
    up, u = _mm(h2, wb["w_up"], "nn", (F32, BF16), "mlp_up", epilogue=lambda p: (p, jnp.square(jnp.maximum(p, 0.0))))
    dn = _mm(u, wb["w_down"], "nn", F32, "mlp_down", tm=512)

    def loss_and_grads(x1v, dnv, tgtv, g):
        loss, vjp = jax.vjp(lambda a, b, c_: _loss_rows(a, b, tgtv, c_), x1v, dnv, g)
        dx1, ddn, dg = vjp(jnp.ones((1, 1), F32))
        return (dx1, ddn), (jnp.broadcast_to(loss, (1, LANES)), dg)

    dx1_y, d_dn, loss, d_gpo = _rowwise("loss", loss_and_grads, [x1, dn, tgt], [gpo], [(d, F32), (d, BF16)],
                                        [((1, LANES), F32), ((1, d), F32)])

    (d_up,) = _mm(d_dn, wb["w_down"], "nt", (BF16,), "mlp_down_dx", extra=[up],
                  epilogue=lambda p, a: (p * (2.0 * jnp.maximum(a, 0.0)),))
    dw_down = _mm(u, d_dn, "tn", BF16, "mlp_down_dw", tm=512)
    d_h2 = _mm(d_up, wb["w_up"], "nt", F32, "mlp_up_dx", tm=512)
    dw_up = _mm(h2, d_up, "tn", BF16, "mlp_up_dw")

    def post_mix_bwd(xv, zv, dx1, dh2, a, b):
        _, vjp = jax.vjp(_post_mix, xv, zv, a, b)
        dx, dz, da, db = vjp((dx1, dh2))
        return (dx, dz), (da, db)

    dx_res, d_z, d_gpm, d_gpl = _rowwise("post_mix_bwd", post_mix_bwd, [x, z, dx1_y, d_h2], [gpm, gpl],
                                         [(d, F32), (d, BF16)], [((1, d), F32), ((1, d), F32)])
    d_merged = _mm(d_z, wb["w_o"], "nt", F32, "proj_o_dx")
    dw_o = _mm(merged, d_z, "tn", BF16, "proj_o_dw")
    d_ga, d_gb, d_ya, d_yb = _rowwise("merge_bwd", lambda *v: (jax.vjp(_merge, *v[:4])[1](v[4]), ()),
                                      [gate_a, gate_b, y_a, y_b, d_merged], [], [(d, F32), (d, F32), (d, BF16), (d, BF16)])
    d_ya_in = _mm(d_ya, wb["w_pa"], "nt", F32, "proj_a_dx")
    dw_pa = _mm(ya_in, d_ya, "tn", BF16, "proj_a_dw")
    d_hb = _mm(d_yb, wb["w_pb"], "nt", F32, "proj_b_dx")
    dw_pb = _mm(h_b, d_yb, "tn", BF16, "proj_b_dw")

    def ml_out_bwd(a, b, c_, ct, g, s):
        parts = []
        for hs in _head_slices(HEAD_W):
            _, vjp = jax.vjp(_ml_out, a[:, hs], b[:, hs], c_[:, hs], g[:, hs], s[:, hs])
            parts.append(vjp(ct[:, hs]))
        cat = lambda i: jnp.concatenate([p[i] for p in parts], axis=1)
        return (cat(0), cat(1), cat(2)), (cat(3), cat(4))

    d_hc, d_opre, d_xc, d_gml, d_skip = _rowwise("mlstm_out_bwd", ml_out_bwd, [hc, o_pre, xc, d_hb], [g_ml, skip],
                                                 [(ml_w, F32)] * 3, [((1, ml_w), F32)] * 2)
    d_qm, d_km, d_vm, d_li, d_lf = _ml_bwd(q_m, k_m, v_m, li, lf, c_prev, n_prev, m_prev, d_hc)
    d_gl = jnp.concatenate([_gate_cols(d_li), _gate_cols(d_lf), jnp.zeros((t, LANES - 2 * HEADS), F32)], axis=1)
    pre_grads = _ml_pre_bwd(x_m, x_pad, pre_params, [d_xc, d_qm, d_km, d_vm, d_gl])
    d_xm = pre_grads[0]
    d_cw = jnp.concatenate(pre_grads[1:5], axis=0)
    d_cb = pre_grads[5]
    d_wq, d_wk, d_wv = _blockdiag_blocks(pre_grads[6:9])
    d_wif = jnp.concatenate(pre_grads[9:12], axis=0)[:, 0:2 * HEADS]
    d_bif = pre_grads[12][:, 0:2 * HEADS]

    def gla_out_bwd(o, g, ct, n_):
        parts = []
        for hs in _head_slices(HEAD_W):
            _, vjp = jax.vjp(_gla_out, o[:, hs], g[:, hs], n_)
            parts.append(vjp(ct[:, hs]))
        cat = lambda i: jnp.concatenate([p[i] for p in parts], axis=1)
        return (cat(0), cat(1)), (sum(p[2] for p in parts),)

    d_o, d_g_a, d_gn = _rowwise("gla_out_bwd", gla_out_bwd, [o_gla, g_a, d_ya_in], [gn], [(ml_w, F32)] * 2, [((1, HEAD_W), F32)])
    dq_hm, dk_hm, d_va, dla_hm = _gla_bwd(q_hm, k_hm, v_a, la_hm, s_prev, d_o)

    def decay_bwd(al, ct, w, b):
        _, vjp = jax.vjp(_log_decay, al, w, b)
        dal, dw, db = vjp(ct)
        return (dal,), (dw, db)

    d_alow_p, d_wa_p, d_ba = _rowwise("gla_decay_bwd", decay_bwd, [a_low_p, _from_hm(dla_hm)], [w_a_up_p, b_a_up],
                                      [(LANES, F32)], [(w_a_up_p.shape, F32), (b_a_up.shape, F32)])
    d_proj = jnp.concatenate([_from_hm(dq_hm), _from_hm(dk_hm), d_va, d_g_a, d_alow_p[:, 0:LOWRANK], d_xm, d_opre, d_ga, d_gb],
                             axis=1).astype(BF16)
    d_h = _mm(d_proj, wb["w_in"], "nt", F32, "proj_in_dx", tm=512)
    dw_in = _mm(h, d_proj, "tn", F32, "proj_in_dw", tm=512, tk=1024)

    def pre_mix_bwd(xv, dh, dres, g):
        _, vjp = jax.vjp(_rms, xv, g)
        dx, dg = vjp(dh)
        return (dx + dres,), (dg,)

    grad_x, d_g1 = _rowwise("pre_mix_norm_bwd", pre_mix_bwd, [x, d_h, dx_res], [g1], [(d, F32)], [((1, d), F32)])

    big = dict(w_in=dw_in, w_pa=dw_pa, w_pb=dw_pb, w_o=dw_o, w_up=dw_up, w_down=dw_down)
    small = dict(g_pre_mix=d_g1, w_a_up=d_wa_p[0:LOWRANK], b_a_up=d_ba, g_gla_norm=d_gn, conv_w=d_cw, conv_b=d_cb,
                 w_q_ml=d_wq, w_k_ml=d_wk, w_v_ml=d_wv, w_if=d_wif, b_if=d_bif, ml_skip=d_skip, g_ml_norm=d_gml,
                 g_post_mix=d_gpm, g_pre_mlp=d_gpl, g_post_mlp=d_gpo)
    return loss[:, 0:1], grad_x, big, small


BIG = ("w_in", "w_pa", "w_pb", "w_o", "w_up", "w_down")
BIG_COL_SHARDED = ("w_in", "w_pa", "w_pb", "w_up")
SMALL_SHARDED = {"w_a_up": 1, "conv_w": 1, "w_if": 0}
SMALL = ("g_pre_mix", "w_a_up", "b_a_up", "g_gla_norm", "conv_w", "conv_b", "w_q_ml", "w_k_ml", "w_v_ml", "w_if", "b_if",
         "ml_skip", "g_ml_norm", "g_post_mix", "g_pre_mlp", "g_post_mlp")
WEIGHTS = ("g_pre_mix", "w_in", "w_a_up", "b_a_up", "g_gla_norm", "conv_w", "conv_b", "w_q_ml", "w_k_ml", "w_v_ml", "w_if", "b_if",
           "ml_skip", "g_ml_norm", "w_pa", "w_pb", "w_o", "g_post_mix", "g_pre_mlp", "w_up", "w_down", "g_post_mlp")


def _my_slice(a, axis):
    n = a.shape[axis] // N_DEV
    return lax.dynamic_slice_in_dim(a, _lin(_me()) * n, n, axis)


def kernel(x, g_pre_mix, w_in, w_a_up, b_a_up, g_gla_norm, conv_w, conv_b, w_q_ml, w_k_ml, w_v_ml, w_if, b_if, ml_skip, g_ml_norm, w_pa, w_pb, w_o, g_post_mix, g_pre_mlp, w_up, w_down, g_post_mlp, loss_target, m_g_pre_mix, m_w_in, m_w_a_up, m_b_a_up, m_g_gla_norm, m_conv_w, m_conv_b, m_w_q_ml, m_w_k_ml, m_w_v_ml, m_w_if, m_b_if, m_ml_skip, m_g_ml_norm, m_w_pa, m_w_pb, m_w_o, m_g_post_mix, m_g_pre_mlp, m_w_up, m_w_down, m_g_post_mlp, v_g_pre_mix, v_w_in, v_w_a_up, v_b_a_up, v_g_gla_norm, v_conv_w, v_conv_b, v_w_q_ml, v_w_k_ml, v_w_v_ml, v_w_if, v_b_if, v_ml_skip, v_g_ml_norm, v_w_pa, v_w_pb, v_w_o, v_g_post_mix, v_g_pre_mlp, v_w_up, v_w_down, v_g_post_mlp):
    args = dict(locals())
    w = {n: args[n][0] for n in WEIGHTS}
    m = {n: args["m_" + n][0] for n in WEIGHTS}
    v = {n: args["v_" + n][0] for n in WEIGHTS}

    gathered = _allgather_big([w[n].astype(BF16) for n in BIG], "allgather_weights")
    wb = {}
    for n, g in zip(BIG, gathered):
        wb[n] = _from_col_blocks(g) if n in BIG_COL_SHARDED else g.reshape(-1, g.shape[-1])
    sharded_names = tuple(SMALL_SHARDED)
    small_g = _allgather_small(_pack([w[n] for n in sharded_names]), "allgather_small_weights")
    ws = {n: (w[n].reshape(1, -1) if w[n].ndim == 1 else w[n]) for n in SMALL if n not in SMALL_SHARDED}
    per_dev = [_unpack(small_g[dev], [w[n].shape for n in sharded_names]) for dev in range(N_DEV)]
    for i, n in enumerate(sharded_names):
        ws[n] = jnp.concatenate([per_dev[dev][i] for dev in range(N_DEV)], axis=SMALL_SHARDED[n])

    loss, grad_x, big, small = _local_step(x[0], loss_target[0], wb, ws)

    parts = [(_col_blocks(big[n]) if n in BIG_COL_SHARDED else big[n].reshape(N_DEV, -1, big[n].shape[-1])).astype(BF16)
             for n in BIG]
    recv = _exchange_big(parts, "exchange_weight_grads")
    out = {}
    for n, r in zip(BIG, recv):
        out[n] = _sum_adamw(r, w[n], m[n], v[n], "adamw_" + n)

    small_shapes = [small[n].shape for n in SMALL]
    vec = _sum_slots(_allgather_small(_pack([small[n] for n in SMALL] + [loss]), "allgather_small_grads"), "sum_small_grads")
    summed = _unpack(vec, small_shapes + [(1, 1)])
    g_small = {}
    for n, g in zip(SMALL, summed[:-1]):
        g_small[n] = _my_slice(g, SMALL_SHARDED[n]) if n in SMALL_SHARDED else g
    shapes = [g_small[n].shape for n in SMALL]
    packed = [_pack([d[n].reshape(g_small[n].shape) for n in SMALL]) for d in (w, g_small, m, v)]
    upd = _rowwise("adamw_small", lambda a, b, c_, d_: (_adamw(a, b, c_, d_), ()), packed, [], [(LANES, F32)] * 3, tile=8 * 1024)
    deltas, new_ms, new_vs = [_unpack(p, shapes) for p in upd]
    for i, n in enumerate(SMALL):
        out[n] = (g_small[n], deltas[i], new_ms[i], new_vs[i])

    shaped = lambda a, n: a.reshape(args[n].shape)
    return (summed[-1].reshape(()), grad_x[None],
            *[shaped(out[n][0], n) for n in WEIGHTS], *[shaped(out[n][1], n) for n in WEIGHTS],
            *[shaped(out[n][2], n) for n in WEIGHTS], *[shaped(out[n][3], n) for n in WEIGHTS])
```

```python
import functools

import jax
import jax.numpy as jnp
from jax import lax
from jax.experimental import pallas as pl
from jax.experimental.pallas import tpu as pltpu

F32 = jnp.float32
BF16 = jnp.bfloat16
MESH = pl.DeviceIdType.MESH

N_DEV = 8
EPS = 1e-6
CHUNK = 64
HEADS = 4
GLA_DK = 64
HEAD_W = 128
GLA_GATE_NORM = 16.0
LOWRANK = 16
CONV_K = 4
QKV_BLOCK = 4
LANES = 128
HALO = 8
IN_SPLITS = (256, 256, 512, 512, 16, 512, 512, 1024, 1024)

ADAM_LR = 0.001
ADAM_B1 = 0.9
ADAM_B2 = 0.999
ADAM_EPS = 1e-08
ADAM_WD = 0.01
ADAM_STEP = 10

VMEM_LIMIT = 56 * 1024 * 1024


def _cparams(*sem):
    return pltpu.CompilerParams(dimension_semantics=sem, vmem_limit_bytes=VMEM_LIMIT)


_DN = {"nn": (((1,), (0,)), ((), ())), "nt": (((1,), (1,)), ((), ())), "tn": (((0,), (0,)), ((), ()))}


def _raw_dot(a, b, mode):
    return lax.dot_general(a.astype(BF16), b.astype(BF16), _DN[mode], preferred_element_type=F32)


@functools.partial(jax.custom_vjp, nondiff_argnums=(2,))
def _bdot(a, b, mode):
    return _raw_dot(a, b, mode)


def _bdot_fwd(a, b, mode):
    return _raw_dot(a, b, mode), (a, b)


def _bdot_bwd(mode, res, ct):
    a, b = res
    if mode == "nn":
        da, db = _raw_dot(ct, b, "nt"), _raw_dot(a, ct, "tn")
    elif mode == "nt":
        da, db = _raw_dot(ct, b, "nn"), _raw_dot(ct, a, "tn")
    else:
        da, db = _raw_dot(b, ct, "nt"), _raw_dot(a, ct, "nn")
    return da.astype(a.dtype), db.astype(b.dtype)


_bdot.defvjp(_bdot_fwd, _bdot_bwd)


def _split3(x):
    hi = x.astype(BF16)
    r1 = x - hi.astype(F32)
    mid = r1.astype(BF16)
    return hi, mid, (r1 - mid.astype(F32)).astype(BF16)


def _split_dot(tri, x):
    return sum(lax.dot_general(tri, t, _DN["nn"], preferred_element_type=F32) for t in _split3(x))


def _tri(n, lower):
    r = lax.broadcasted_iota(jnp.int32, (n, n), 0)
    c = lax.broadcasted_iota(jnp.int32, (n, n), 1)
    return ((c <= r) if lower else (c >= r)).astype(BF16)


@jax.custom_vjp
def _cumsum_rows(x):
    return _split_dot(_tri(x.shape[0], True), x)


def _cumsum_rows_fwd(x):
    return _cumsum_rows(x), None


def _cumsum_rows_bwd(_, ct):
    return (_split_dot(_tri(ct.shape[0], False), ct),)


_cumsum_rows.defvjp(_cumsum_rows_fwd, _cumsum_rows_bwd)


def _abs(x):
    return jnp.where(x >= 0, x, -x)


def _sigmoid(x):
    return lax.logistic(x)


def _log_sigmoid(x):
    return jnp.minimum(x, 0.0) - jnp.log(1.0 + jnp.exp(-_abs(x)))


def _rms(x, g):
    return x * lax.rsqrt(jnp.mean(x * x, axis=-1, keepdims=True) + EPS) * g


def _head_slices(w):
    return [slice(h * w, (h + 1) * w) for h in range(HEADS)]


def _tile(dim, want):
    if dim <= want or dim % LANES:
        return dim
    t = want
    while dim % t:
        t -= LANES
    return t


def _mm(a, b, mode, out_dtype, name, tm=1024, tn=1024, tk=4096, epilogue=None, extra=()):
    if mode == "nn":
        (m, k), (k2, n) = a.shape, b.shape
    elif mode == "nt":
        (m, k), (n, k2) = a.shape, b.shape
    else:
        (k, m), (k2, n) = a.shape, b.shape
    assert k == k2, (name, a.shape, b.shape)
    tm, tn, tk = _tile(m, tm), _tile(n, tn), _tile(k, tk)
    nk = k // tk
    out_dtypes = out_dtype if epilogue else (out_dtype,)
    assert nk == 1 or (out_dtype == F32 and not epilogue), name
    n_in = 2 + len(extra)

    def body(*refs):
        p = _raw_dot(refs[0][...], refs[1][...], mode)
        if nk > 1:
            _accumulate(pl.program_id(2), [refs[2]], [p])
            return
        outs = epilogue(p, *[r[...] for r in refs[2:n_in]]) if epilogue else (p,)
        for ref, val in zip(refs[n_in:], outs):
            ref[...] = val.astype(ref.dtype)

    a_spec = pl.BlockSpec((tk, tm), lambda i, j, kk: (kk, i)) if mode == "tn" else pl.BlockSpec((tm, tk), lambda i, j, kk: (i, kk))
    b_spec = pl.BlockSpec((tn, tk), lambda i, j, kk: (j, kk)) if mode == "nt" else pl.BlockSpec((tk, tn), lambda i, j, kk: (kk, j))
    o_spec = pl.BlockSpec((tm, tn), lambda i, j, kk: (i, j))
    res = pl.pallas_call(
        body, name=name, grid=(m // tm, n // tn, nk),
        in_specs=[a_spec, b_spec] + [o_spec] * len(extra), out_specs=[o_spec] * len(out_dtypes),
        out_shape=[jax.ShapeDtypeStruct((m, n), dt) for dt in out_dtypes],
        compiler_params=_cparams("parallel", "parallel", "arbitrary"),
    )(a, b, *extra)
    return res if epilogue else res[0]


def _rowwise(name, fn, rows, params, out_rows, out_accs=(), tile=256, deps=()):
    t = rows[0].shape[0]
    r = min(tile, t)
    assert t % r == 0
    n_in, n_or = len(rows) + len(params), len(out_rows)
    n_all = n_in + len(deps)
    params = list(params) + list(deps)

    def body(*refs):
        vals = [ref[...] for ref in refs[:n_in]]
        outs = refs[n_all:]
        ro, ao = fn(*vals)
        for ref, v in zip(outs[:n_or], ro):
            ref[...] = v.astype(ref.dtype)
        if out_accs:
            _accumulate(pl.program_id(0), outs[n_or:], ao)

    def full(shape):
        return pl.BlockSpec(shape, lambda i, nd=len(shape): (0,) * nd)

    return pl.pallas_call(
        body, name=name, grid=(t // r,),
        in_specs=[pl.BlockSpec((r, a.shape[1]), lambda i: (i, 0)) for a in rows] + [full(p.shape) for p in params],
        out_specs=[pl.BlockSpec((r, w), lambda i: (i, 0)) for w, _ in out_rows] + [full(s) for s, _ in out_accs],
        out_shape=[jax.ShapeDtypeStruct((t, w), dt) for w, dt in out_rows] + [jax.ShapeDtypeStruct(s, dt) for s, dt in out_accs],
        compiler_params=_cparams("arbitrary"),
    )(*rows, *params)


def _accumulate(step, refs, vals):
    for ref, v in zip(refs, vals):
        @pl.when(step == 0)
        def _(ref=ref, v=v):
            ref[...] = v.astype(ref.dtype)

        @pl.when(step > 0)
        def _(ref=ref, v=v):
            ref[...] += v.astype(ref.dtype)


def _gla_chunk(q, k, v, la, st):
    c = q.shape[0]
    row = lax.broadcasted_iota(jnp.int32, (c, c), 0)
    col = lax.broadcasted_iota(jnp.int32, (c, c), 1)
    cum = _cumsum_rows(la)
    cl = jnp.sum(la, axis=0, keepdims=True)
    ep = jnp.exp(cum)
    en = jnp.exp(-cum)
    qs = q * (GLA_DK ** -0.5)
    qp = qs * ep
    a_f = _bdot(qp, k * en, "nt")
    a_b = _bdot(qs * en, k * ep, "nt")
    sc = jnp.where(row >= col, a_f, a_b)
    o = _bdot(sc, v, "nn") + _bdot(qp, st, "nt")
    kd = k * jnp.exp(cl - cum)
    st_new = st * jnp.exp(cl) + _bdot(v, kd, "tn")
    return o, st_new


def _gla_specs(nc, rev):
    def ch(n):
        return (nc - 1 - n) if rev else n
    hm = pl.BlockSpec((None, CHUNK, GLA_DK), lambda h, n: (h, ch(n), 0))
    tm = pl.BlockSpec((CHUNK, HEAD_W), lambda h, n: (ch(n), h))
    st = pl.BlockSpec((None, None, HEAD_W, GLA_DK), lambda h, n: (h, ch(n), 0, 0))
    return hm, tm, st


def _gla_fwd(q, k, v, la):
    t = v.shape[0]
    nc = t // CHUNK
    hm, tm, st = _gla_specs(nc, False)

    def body(q_ref, k_ref, v_ref, la_ref, o_ref, sp_ref, st_ref):
        @pl.when(pl.program_id(1) == 0)
        def _():
            st_ref[...] = jnp.zeros_like(st_ref)

        s = st_ref[...]
        sp_ref[...] = s
        o, s_new = _gla_chunk(q_ref[...], k_ref[...], v_ref[...], la_ref[...], s)
        o_ref[...] = o
        st_ref[...] = s_new

    return pl.pallas_call(
        body, name="gla_fwd", grid=(HEADS, nc),
        in_specs=[hm, hm, tm, hm], out_specs=[tm, st],
        out_shape=[jax.ShapeDtypeStruct((t, HEADS * HEAD_W), F32), jax.ShapeDtypeStruct((HEADS, nc, HEAD_W, GLA_DK), F32)],
        scratch_shapes=[pltpu.VMEM((HEAD_W, GLA_DK), F32)],
        compiler_params=_cparams("arbitrary", "arbitrary"),
    )(q, k, v, la)


def _gla_bwd(q, k, v, la, sp, do):
    t = v.shape[0]
    nc = t // CHUNK
    hm, tm, st = _gla_specs(nc, True)

    def body(q_ref, k_ref, v_ref, la_ref, sp_ref, do_ref, dq_ref, dk_ref, dv_ref, dla_ref, ds_ref):
        @pl.when(pl.program_id(1) == 0)
        def _():
            ds_ref[...] = jnp.zeros_like(ds_ref)

        _, vjp = jax.vjp(_gla_chunk, q_ref[...], k_ref[...], v_ref[...], la_ref[...], sp_ref[...])
        dq, dk, dv, dla, ds = vjp((do_ref[...], ds_ref[...]))
        dq_ref[...] = dq
        dk_ref[...] = dk
        dv_ref[...] = dv
        dla_ref[...] = dla
        ds_ref[...] = ds

    hm_shape = jax.ShapeDtypeStruct((HEADS, t, GLA_DK), F32)
    return pl.pallas_call(
        body, name="gla_bwd", grid=(HEADS, nc),
        in_specs=[hm, hm, tm, hm, st, tm], out_specs=[hm, hm, tm, hm],
        out_shape=[hm_shape, hm_shape, jax.ShapeDtypeStruct((t, HEADS * HEAD_W), F32), hm_shape],
        scratch_shapes=[pltpu.VMEM((HEAD_W, GLA_DK), F32)],
        compiler_params=_cparams("arbitrary", "arbitrary"),
    )(q, k, v, la, sp, do)


def _ml_chunk(q, k, v, li_r, lf_r, cm, nv, m):
    c = q.shape[0]
    row = lax.broadcasted_iota(jnp.int32, (c, c), 0)
    col = lax.broadcasted_iota(jnp.int32, (c, c), 1)
    eye = (row == col).astype(F32)
    li_c = jnp.sum(eye * li_r, axis=1, keepdims=True)
    lf_c = jnp.sum(eye * lf_r, axis=1, keepdims=True)
    fc_c = jnp.sum((col <= row).astype(F32) * lf_r, axis=1, keepdims=True)
    fc_r = jnp.sum((row <= col).astype(F32) * lf_c, axis=0, keepdims=True)
    f_last = jnp.sum(lf_r, axis=1, keepdims=True)
    kc = k * (HEAD_W ** -0.5)
    a_c = f_last - fc_c + li_c
    m_loc = jnp.max(a_c, axis=0, keepdims=True)
    kw = kc * jnp.exp(a_c - m_loc)
    c_chunk = _bdot(kw, v, "tn")
    n_chunk = jnp.sum(kw, axis=0, keepdims=True)
    m_new = jnp.maximum(f_last + m, m_loc)
    sp = jnp.exp(f_last + m - m_new)
    sl = jnp.exp(m_loc - m_new)
    cm_new = sp * cm + sl * c_chunk
    nv_new = sp * nv + sl * n_chunk
    log_d = li_r - _abs(fc_c - fc_r)
    g_inter = fc_c + m
    m_t = jnp.maximum(g_inter, jnp.max(log_d, axis=1, keepdims=True))
    s = _bdot(q, kc, "nt") * jnp.exp(log_d - m_t)
    sc = jnp.exp(g_inter - m_t)
    num = _bdot(s, v, "nn") + sc * _bdot(q, cm, "nn")
    den = jnp.sum(s, axis=1, keepdims=True) + sc * jnp.sum(q * nv, axis=1, keepdims=True)
    den = jnp.maximum(_abs(den), jnp.exp(-m_t))
    return num / den, cm_new, nv_new, m_new


def _ml_specs(nc, rev):
    def ch(n):
        return (nc - 1 - n) if rev else n
    tm = pl.BlockSpec((CHUNK, HEAD_W), lambda h, n: (ch(n), h))
    gate = pl.BlockSpec((None, None, 1, CHUNK), lambda h, n: (h, ch(n), 0, 0))
    cm = pl.BlockSpec((None, None, HEAD_W, HEAD_W), lambda h, n: (h, ch(n), 0, 0))
    vec = pl.BlockSpec((None, None, 1, HEAD_W), lambda h, n: (h, ch(n), 0, 0))
    return tm, gate, cm, vec


def _ml_fwd(q, k, v, li, lf):
    t = q.shape[0]
    nc = t // CHUNK
    tm, gate, cm, vec = _ml_specs(nc, False)

    def body(q_ref, k_ref, v_ref, li_ref, lf_ref, hc_ref, cp_ref, np_ref, mp_ref, c_ref, n_ref, m_ref):
        @pl.when(pl.program_id(1) == 0)
        def _():
            c_ref[...] = jnp.zeros_like(c_ref)
            n_ref[...] = jnp.zeros_like(n_ref)
            m_ref[...] = jnp.zeros_like(m_ref)

        c0, n0, m0 = c_ref[...], n_ref[...], m_ref[...]
        cp_ref[...] = c0
        np_ref[...] = n0
        mp_ref[...] = m0
        hc, c1, n1, m1 = _ml_chunk(q_ref[...], k_ref[...], v_ref[...], li_ref[...], lf_ref[...], c0, n0, m0[:, 0:1])
        hc_ref[...] = hc
        c_ref[...] = c1
        n_ref[...] = n1
        m_ref[...] = jnp.broadcast_to(m1, m_ref.shape)

    return pl.pallas_call(
        body, name="mlstm_fwd", grid=(HEADS, nc),
        in_specs=[tm, tm, tm, gate, gate], out_specs=[tm, cm, vec, vec],
        out_shape=[jax.ShapeDtypeStruct((t, HEADS * HEAD_W), F32), jax.ShapeDtypeStruct((HEADS, nc, HEAD_W, HEAD_W), F32),
                   jax.ShapeDtypeStruct((HEADS, nc, 1, HEAD_W), F32), jax.ShapeDtypeStruct((HEADS, nc, 1, HEAD_W), F32)],
        scratch_shapes=[pltpu.VMEM((HEAD_W, HEAD_W), F32), pltpu.VMEM((1, HEAD_W), F32), pltpu.VMEM((1, HEAD_W), F32)],
        compiler_params=_cparams("arbitrary", "arbitrary"),
    )(q, k, v, li, lf)


def _ml_bwd(q, k, v, li, lf, cp, npv, mp, dhc):
    t = q.shape[0]
    nc = t // CHUNK
    tm, gate, cm, vec = _ml_specs(nc, True)

    def body(q_ref, k_ref, v_ref, li_ref, lf_ref, cp_ref, np_ref, mp_ref, dhc_ref,
             dq_ref, dk_ref, dv_ref, dli_ref, dlf_ref, dc_ref, dn_ref, dm_ref):
        @pl.when(pl.program_id(1) == 0)
        def _():
            dc_ref[...] = jnp.zeros_like(dc_ref)
            dn_ref[...] = jnp.zeros_like(dn_ref)
            dm_ref[...] = jnp.zeros_like(dm_ref)

        _, vjp = jax.vjp(_ml_chunk, q_ref[...], k_ref[...], v_ref[...], li_ref[...], lf_ref[...],
                         cp_ref[...], np_ref[...], mp_ref[...][:, 0:1])
        dq, dk, dv, dli, dlf, dc, dn, dm = vjp((dhc_ref[...], dc_ref[...], dn_ref[...], dm_ref[...][:, 0:1]))
        dq_ref[...] = dq
        dk_ref[...] = dk
        dv_ref[...] = dv
        dli_ref[...] = dli
        dlf_ref[...] = dlf
        dc_ref[...] = dc
        dn_ref[...] = dn
        dm_ref[...] = jnp.broadcast_to(dm, dm_ref.shape)

    tm_shape = jax.ShapeDtypeStruct((t, HEADS * HEAD_W), F32)
    gate_shape = jax.ShapeDtypeStruct((HEADS, nc, 1, CHUNK), F32)
    return pl.pallas_call(
        body, name="mlstm_bwd", grid=(HEADS, nc),
        in_specs=[tm, tm, tm, gate, gate, cm, vec, vec, tm], out_specs=[tm, tm, tm, gate, gate],
        out_shape=[tm_shape, tm_shape, tm_shape, gate_shape, gate_shape],
        scratch_shapes=[pltpu.VMEM((HEAD_W, HEAD_W), F32), pltpu.VMEM((1, HEAD_W), F32), pltpu.VMEM((1, HEAD_W), F32)],
        compiler_params=_cparams("arbitrary", "arbitrary"),
    )(q, k, v, li, lf, cp, npv, mp, dhc)


def _ml_pre(s0, s1, s2, s3, cw0, cw1, cw2, cw3, cb, wq, wk, wv, wiq, wik, wiv, bif):
    pre = cb + cw0 * s0 + cw1 * s1 + cw2 * s2 + cw3 * s3
    xc = pre * _sigmoid(pre)
    q = _bdot(xc, wq, "nn")
    k = _bdot(xc, wk, "nn")
    v = _bdot(s3, wv, "nn")
    gates = _bdot(q, wiq, "nn") + _bdot(k, wik, "nn") + _bdot(v, wiv, "nn") + bif
    lane = lax.broadcasted_iota(jnp.int32, gates.shape, 1)
    gl = jnp.where(lane < HEADS, gates, _log_sigmoid(gates))
    return xc, q, k, v, gl


def _delayed(xs_ref, x_ref, halo_ref, r):
    xs_ref[0:HALO, :] = halo_ref[...]
    xs_ref[HALO:HALO + r, :] = x_ref[...]
    return [xs_ref[pl.ds(HALO - (CONV_K - 1) + j, r), :] for j in range(CONV_K)]


def _full_spec(shape):
    return pl.BlockSpec(shape, lambda i, nd=len(shape): (0,) * nd)


def _ml_pre_fwd(x_m, x_pad, params, tile=256):
    t, w = x_m.shape
    r = min(tile, t)

    def body(*refs):
        x_ref, halo_ref = refs[:2]
        p = [ref[...] for ref in refs[2:2 + len(params)]]
        outs = refs[2 + len(params):-1]
        res = _ml_pre(*_delayed(refs[-1], x_ref, halo_ref, r), *p)
        for ref, val in zip(outs, res):
            ref[...] = val

    row = pl.BlockSpec((r, w), lambda i: (i, 0))
    return pl.pallas_call(
        body, name="ml_pre_fwd", grid=(t // r,),
        in_specs=[row, pl.BlockSpec((HALO, w), lambda i: (i * (r // HALO), 0))] + [_full_spec(p.shape) for p in params],
        out_specs=[row] * 4 + [pl.BlockSpec((r, LANES), lambda i: (i, 0))],
        out_shape=[jax.ShapeDtypeStruct((t, w), F32)] * 4 + [jax.ShapeDtypeStruct((t, LANES), F32)],
        scratch_shapes=[pltpu.VMEM((r + HALO, w), F32)],
        compiler_params=_cparams("arbitrary"),
    )(x_m, x_pad, *params)


def _ml_pre_bwd(x_m, x_pad, params, cts, tile=256):
    t, w = x_m.shape
    r = min(tile, t)
    nt = t // r
    n_p = len(params)

    def body(*refs):
        x_ref, halo_ref = refs[:2]
        p = [ref[...] for ref in refs[2:2 + n_p]]
        ct = [ref[...] for ref in refs[2 + n_p:7 + n_p]]
        dx_ref = refs[7 + n_p]
        dp_refs = refs[8 + n_p:8 + 2 * n_p]
        xs_ref, ds_ref, carry_ref = refs[8 + 2 * n_p:]
        step = pl.program_id(0)

        @pl.when(step == 0)
        def _():
            ds_ref[...] = jnp.zeros_like(ds_ref)
            carry_ref[...] = jnp.zeros_like(carry_ref)

        _, vjp = jax.vjp(_ml_pre, *_delayed(xs_ref, x_ref, halo_ref, r), *p)
        grads = vjp(tuple(ct))
        for j in range(CONV_K):
            ds_ref[j, HALO:HALO + r, :] = grads[j]
        lead = HALO + CONV_K - 1
        d_tile = sum(ds_ref[j, pl.ds(lead - j, r), :] for j in range(CONV_K))
        d_halo = sum(ds_ref[j, pl.ds(CONV_K - 1 - j, HALO), :] for j in range(CONV_K))
        dx_ref[...] = d_tile
        dx_ref[r - HALO:r, :] += carry_ref[...]
        carry_ref[...] = d_halo
        _accumulate(step, dp_refs, grads[CONV_K:])

    row = pl.BlockSpec((r, w), lambda i: (nt - 1 - i, 0))
    return pl.pallas_call(
        body, name="ml_pre_bwd", grid=(nt,),
        in_specs=[row, pl.BlockSpec((HALO, w), lambda i: ((nt - 1 - i) * (r // HALO), 0))] + [_full_spec(p.shape) for p in params]
        + [row] * 4 + [pl.BlockSpec((r, LANES), lambda i: (nt - 1 - i, 0))],
        out_specs=[row] + [_full_spec(p.shape) for p in params],
        out_shape=[jax.ShapeDtypeStruct((t, w), F32)] + [jax.ShapeDtypeStruct(p.shape, F32) for p in params],
        scratch_shapes=[pltpu.VMEM((r + HALO, w), F32), pltpu.VMEM((CONV_K, r + 2 * HALO, w), F32), pltpu.VMEM((HALO, w), F32)],
        compiler_params=_cparams("arbitrary"),
    )(x_m, x_pad, *params, *cts)


def _per_head(fn, row_vals, head_params, shared_params=()):
    return [fn(*[a[:, hs] for a in row_vals], *[p[:, hs] for p in head_params], *shared_params) for hs in _head_slices(HEAD_W)]


def _gla_out(o, g, gn):
    return _rms(o, gn) * (g * _sigmoid(g))


def _ml_out(hc, op, xc, g, sk):
    hcell = hc * _sigmoid(op)
    mu = jnp.mean(hcell, axis=-1, keepdims=True)
    d = hcell - mu
    var = jnp.mean(d * d, axis=-1, keepdims=True)
    return d * lax.rsqrt(var + EPS) * g + sk * xc


def _log_decay(al, w, b):
    return _log_sigmoid(_bdot(al, w, "nn") + b) * (1.0 / GLA_GATE_NORM)


def _merge(ga, gb, ya, yb):
    return _sigmoid(ga) * ya + _sigmoid(gb) * yb


def _post_mix(x, z, gpm, gpl):
    x1 = x + _rms(z, gpm)
    return x1, _rms(x1, gpl)


def _loss_rows(x1, dn, tgt, g):
    e = x1 + _rms(dn, g) - tgt
    return 0.5 * jnp.sum(jnp.mean(e * e, axis=-1, keepdims=True), axis=0, keepdims=True)


def _lin(p):
    return 4 * p[0] + 2 * p[1] + p[2]


def _me():
    return lax.axis_index("x"), lax.axis_index("y"), lax.axis_index("c")


def _flip(p, k):
    return tuple((1 - v) if (k >> (2 - i)) & 1 else v for i, v in enumerate(p))


ANY = pl.BlockSpec(memory_space=pl.ANY)


def _allgather_big(shards, name):
    n = len(shards)

    def body(*refs):
        ins, outs = refs[:n], refs[n:2 * n]
        send_sems, recv_sems, local_sems = refs[2 * n:]
        me = _me()
        x, y, c = me
        sib = (x, y, 1 - c)
        chips = [(1 - x, y), (x, 1 - y), (1 - x, 1 - y)]

        def cp(a, k, block, to, src=None):
            dst = outs[a].at[_lin(block)]
            return pltpu.make_async_remote_copy(src_ref=dst if src is None else src, dst_ref=dst,
                                                send_sem=send_sems.at[a * 7 + k], recv_sem=recv_sems.at[a * 7 + k],
                                                device_id=to, device_id_type=MESH)

        mine = [pltpu.make_async_copy(ins[a], outs[a].at[_lin(me)], local_sems.at[a]) for a in range(n)]
        for m in mine:
            m.start()
        first = []
        for a in range(n):
            first.append(cp(a, 0, me, sib, src=ins[a]))
            first += [cp(a, 1 + j, me, (*chip, c), src=ins[a]) for j, chip in enumerate(chips)]
        for f in first:
            f.start()
        passed = []
        for j, chip in enumerate(chips):
            for a in range(n):
                cp(a, 1 + j, (*chip, c), me).wait_recv()
                fwd = cp(a, 4 + j, (*chip, c), sib)
                fwd.start()
                passed.append(fwd)
        for a in range(n):
            cp(a, 0, sib, me).wait_recv()
            for j, chip in enumerate(chips):
                cp(a, 4 + j, (*chip, 1 - c), me).wait_recv()
        for f in first + passed:
            f.wait_send()
        for m in mine:
            m.wait()

    return pl.pallas_call(
        body, name=name,
        in_specs=[ANY] * n, out_specs=[ANY] * n,
        out_shape=[jax.ShapeDtypeStruct((N_DEV, *s.shape), s.dtype) for s in shards],
        scratch_shapes=[pltpu.SemaphoreType.DMA((7 * n,)), pltpu.SemaphoreType.DMA((7 * n,)), pltpu.SemaphoreType.DMA((n,))],
    )(*shards)


def _exchange_big(parts, name):
    n = len(parts)

    def body(*refs):
        ins, outs = refs[:n], refs[n:2 * n]
        send_sems, recv_sems, local_sems = refs[2 * n:]
        me = _me()
        copies = []
        for a in range(n):
            loc = pltpu.make_async_copy(ins[a].at[_lin(me)], outs[a].at[_lin(me)], local_sems.at[a])
            loc.start()
            copies.append(loc)
            for k in range(1, N_DEV):
                peer = _flip(me, k)
                rc = pltpu.make_async_remote_copy(src_ref=ins[a].at[_lin(peer)], dst_ref=outs[a].at[_lin(me)],
                                                  send_sem=send_sems.at[a * 7 + k - 1], recv_sem=recv_sems.at[a * 7 + k - 1],
                                                  device_id=peer, device_id_type=MESH)
                rc.start()
                copies.append(rc)
        for cpy in copies:
            cpy.wait()

    return pl.pallas_call(
        body, name=name,
        in_specs=[ANY] * n, out_specs=[ANY] * n,
        out_shape=[jax.ShapeDtypeStruct(p.shape, p.dtype) for p in parts],
        scratch_shapes=[pltpu.SemaphoreType.DMA((7 * n,)), pltpu.SemaphoreType.DMA((7 * n,)), pltpu.SemaphoreType.DMA((n,))],
    )(*parts)


HBM = pl.BlockSpec(memory_space=pltpu.HBM)
SEM = pl.BlockSpec(memory_space=pltpu.SEMAPHORE)
DATAFLOW = pltpu.SideEffectType.DATAFLOW_SIDE_EFFECTING


def _peer_copies(kind, srcs, lands, send_sems, recv_sems):
    me = _me()
    copies = []
    for a, (src, land) in enumerate(zip(srcs, lands)):
        for k in range(1, N_DEV):
            peer = _flip(me, k)
            copies.append(pltpu.make_async_remote_copy(
                src_ref=src if kind == "gather" else src.at[_lin(peer)], dst_ref=land.at[_lin(me)],
                send_sem=send_sems.at[a * 7 + k - 1], recv_sem=recv_sems.at[a * 7 + k - 1],
                device_id=peer, device_id_type=MESH))
    return copies


def _copies_start(kind, srcs, name):
    n = len(srcs)
    land_shapes = [((N_DEV, *s.shape) if kind == "gather" else s.shape) for s in srcs]

    def body(*refs):
        for cp in _peer_copies(kind, refs[:n], refs[n:2 * n], refs[2 * n], refs[2 * n + 1]):
            cp.start()
        refs[-1][...] = jnp.zeros_like(refs[-1])

    def hbm(a):
        return pltpu.with_memory_space_constraint(a, pltpu.HBM)

    out = pl.pallas_call(
        body, name=name,
        out_shape=(pltpu.SemaphoreType.DMA((7 * n,)), pltpu.SemaphoreType.DMA((7 * n,)),
                   *[pltpu.HBM(s.shape, s.dtype) for s in srcs],
                   *[pltpu.HBM(ls, s.dtype) for ls, s in zip(land_shapes, srcs)],
                   jax.ShapeDtypeStruct((8, LANES), F32)),
        in_specs=[HBM] * (2 * n), out_specs=(SEM, SEM, *[HBM] * (2 * n), pl.BlockSpec(memory_space=pltpu.VMEM)),
        input_output_aliases={i: 2 + i for i in range(2 * n)},
        compiler_params=pltpu.CompilerParams(has_side_effects=DATAFLOW),
    )(*[hbm(s) for s in srcs], *[hbm(lax.empty(ls, s.dtype)) for ls, s in zip(land_shapes, srcs)])
    return (kind, n, out[:-1]), out[-1]


def _copies_wait(state, after, name):
    kind, n, (send_sems, recv_sems, *thru) = state

    def body(*refs):
        for cp in _peer_copies(kind, refs[:n], refs[n:2 * n], refs[2 * n], refs[2 * n + 1]):
            cp.wait_send()
            cp.wait_recv()

    out = pl.pallas_call(
        body, name=name,
        out_shape=tuple(pltpu.HBM(t.shape, t.dtype) for t in thru),
        in_specs=[HBM] * (2 * n) + [SEM, SEM, ANY], out_specs=tuple([HBM] * (2 * n)),
        input_output_aliases={i: i for i in range(2 * n)},
        compiler_params=pltpu.CompilerParams(has_side_effects=DATAFLOW),
    )(*thru, send_sems, recv_sems, after)
    return out[:n], out[n:]


def _allgather_small(vec, name):
    r, w = vec.shape

    def body(x_ref, out_ref, send_sems, recv_sems):
        me = _me()
        out_ref[_lin(me)] = x_ref[...]
        copies = []
        for k in range(1, N_DEV):
            rc = pltpu.make_async_remote_copy(src_ref=x_ref, dst_ref=out_ref.at[_lin(me)],
                                              send_sem=send_sems.at[k - 1], recv_sem=recv_sems.at[k - 1],
                                              device_id=_flip(me, k), device_id_type=MESH)
            rc.start()
            copies.append(rc)
        for rc in copies:
            rc.wait()

    return pl.pallas_call(
        body, name=name,
        in_specs=[pl.BlockSpec(memory_space=pltpu.VMEM)], out_specs=pl.BlockSpec(memory_space=pltpu.VMEM),
        out_shape=jax.ShapeDtypeStruct((N_DEV, r, w), vec.dtype),
        scratch_shapes=[pltpu.SemaphoreType.DMA((7,)), pltpu.SemaphoreType.DMA((7,))],
    )(vec)


def _adamw(w, g, m, v):
    m2 = ADAM_B1 * m + (1.0 - ADAM_B1) * g
    v2 = ADAM_B2 * v + (1.0 - ADAM_B2) * (g * g)
    m_hat = m2 / (1.0 - ADAM_B1 ** ADAM_STEP)
    v_hat = v2 / (1.0 - ADAM_B2 ** ADAM_STEP)
    delta = -ADAM_LR * (m_hat / (jnp.sqrt(v_hat) + ADAM_EPS) + ADAM_WD * w)
    return delta, m2, v2


def _sum_adamw(land, part, me_idx, w, m, v, name, tile=256):
    r, c = w.shape
    tr = min(tile, r)

    def body(me_ref, own_ref, *refs):
        slots = refs[:N_DEV]
        w_ref, m_ref, v_ref, g_ref, d_ref, m2_ref, v2_ref = refs[N_DEV:]
        own = own_ref[...].astype(F32)
        g = None
        for s in range(N_DEV):
            term = jnp.where(me_ref[0] == s, own, slots[s][...].astype(F32))
            g = term if g is None else g + term
        d, m2, v2 = _adamw(w_ref[...], g, m_ref[...], v_ref[...])
        g_ref[...] = g
        d_ref[...] = d
        m2_ref[...] = m2
        v2_ref[...] = v2

    def slot_spec(s):
        return pl.BlockSpec((None, tr, c), lambda i, me: (jnp.where(me[0] == s, (s + 1) % N_DEV, s), i, 0))

    row = pl.BlockSpec((tr, c), lambda i, me: (i, 0))
    return pl.pallas_call(
        body, name=name,
        grid_spec=pltpu.PrefetchScalarGridSpec(
            num_scalar_prefetch=1, grid=(r // tr,),
            in_specs=[pl.BlockSpec((None, tr, c), lambda i, me: (me[0], i, 0))] + [slot_spec(s) for s in range(N_DEV)] + [row] * 3,
            out_specs=[row] * 4),
        out_shape=[jax.ShapeDtypeStruct((r, c), F32)] * 4,
        compiler_params=_cparams("parallel"),
    )(me_idx, part, *[land] * N_DEV, w, m, v)


def _sum_slots(gathered, name):
    _, r, w = gathered.shape

    def body(p_ref, o_ref):
        g = p_ref[0]
        for s in range(1, N_DEV):
            g = g + p_ref[s]
        o_ref[...] = g

    return pl.pallas_call(body, name=name, out_shape=jax.ShapeDtypeStruct((r, w), F32))(gathered)


def _pack(arrs):
    flat = jnp.concatenate([a.reshape(-1).astype(F32) for a in arrs])
    rows = -(-flat.shape[0] // (8 * LANES)) * 8
    return jnp.pad(flat, (0, rows * LANES - flat.shape[0])).reshape(rows, LANES)


def _unpack(packed, shapes):
    flat = packed.reshape(-1)
    out, off = [], 0
    for s in shapes:
        size = 1
        for d in s:
            size *= d
        out.append(flat[off:off + size].reshape(s))
        off += size
    return out


def _to_hm(a, d):
    t = a.shape[0]
    return a.reshape(t, HEADS, d).transpose(1, 0, 2)


def _from_hm(a):
    h, t, d = a.shape
    return a.transpose(1, 0, 2).reshape(t, h * d)


def _gate_rows(g):
    t = g.shape[0]
    return g.T.reshape(HEADS, t // CHUNK, 1, CHUNK)


def _gate_cols(g):
    h, nc, _, c = g.shape
    return g.reshape(h, nc * c).T


def _blockdiag_dense(w):
    n = w.shape[0] * QKV_BLOCK
    tiled = jnp.tile(w.reshape(n, QKV_BLOCK), (1, n // QKV_BLOCK))
    r = lax.broadcasted_iota(jnp.int32, (n, n), 0)
    c = lax.broadcasted_iota(jnp.int32, (n, n), 1)
    return jnp.where(r // QKV_BLOCK == c // QKV_BLOCK, tiled, 0.0)


def _blockdiag_blocks(dense):
    n = dense[0].shape[0]
    k = len(dense)

    def body(*refs):
        r = lax.broadcasted_iota(jnp.int32, (n, n), 0)
        c = lax.broadcasted_iota(jnp.int32, (n, n), 1)
        fr = lax.broadcasted_iota(jnp.int32, (n, LANES), 0)
        fc = lax.broadcasted_iota(jnp.int32, (n, LANES), 1)
        fold = ((fr & (QKV_BLOCK - 1)) == fc).astype(BF16)
        for i in range(k):
            kept = jnp.where((r >> 2) == (c >> 2), refs[i][...], 0.0)
            refs[k + i][...] = sum(lax.dot_general(t, fold, _DN["nn"], preferred_element_type=F32) for t in _split3(kept))

    out = pl.pallas_call(body, name="blockdiag_blocks", out_shape=[jax.ShapeDtypeStruct((n, LANES), F32)] * k)(*dense)
    return [o[:, 0:QKV_BLOCK].reshape(n // QKV_BLOCK, QKV_BLOCK, QKV_BLOCK) for o in out]


def _col_blocks(w):
    k, n = w.shape
    return w.reshape(k, N_DEV, n // N_DEV).transpose(1, 0, 2)


def _from_col_blocks(g):
    d, k, n = g.shape
    return g.transpose(1, 0, 2).reshape(k, d * n)


def _local_step(x, tgt, weight, ws, tokens=(), on_grads=None):
    t, d = x.shape
    g1 = ws["g_pre_mix"]
    on_grads = on_grads or (lambda grads: None)

    def dep(token):
        return () if token is None else (token,)

    (h,) = _rowwise("pre_mix_norm", lambda xv, g: ((_rms(xv, g),), ()), [x], [g1], [(d, BF16)], deps=tokens)
    proj = _mm(h, weight("w_in", h), "nn", F32, "proj_in", tm=512)
    offs = [0]
    for s in IN_SPLITS:
        offs.append(offs[-1] + s)
    q_a, k_a, v_a, g_a, a_low, x_m, o_pre, gate_a, gate_b = [proj[:, offs[i]:offs[i + 1]] for i in range(9)]

    a_low_p = jnp.pad(a_low, ((0, 0), (0, LANES - LOWRANK)))
    w_a_up_p = jnp.pad(ws["w_a_up"], ((0, LANES - LOWRANK), (0, 0)))
    b_a_up = ws["b_a_up"]
    (la,) = _rowwise("gla_decay", lambda al, w, b: ((_log_decay(al, w, b),), ()), [a_low_p], [w_a_up_p, b_a_up],
                     [(HEADS * GLA_DK, F32)])
    q_hm, k_hm, la_hm = _to_hm(q_a, GLA_DK), _to_hm(k_a, GLA_DK), _to_hm(la, GLA_DK)
    o_gla, s_prev = _gla_fwd(q_hm, k_hm, v_a, la_hm)
    gn = ws["g_gla_norm"]
    (ya_in,) = _rowwise("gla_out", lambda o, g, n_: ((jnp.concatenate(_per_head(_gla_out, [o, g], [], [n_]), axis=1),), ()),
                        [o_gla, g_a], [gn], [(HEADS * HEAD_W, BF16)])
    y_a = _mm(ya_in, weight("w_pa", ya_in), "nn", F32, "proj_a")

    cw = ws["conv_w"]
    w_if_p = jnp.pad(ws["w_if"], ((0, 0), (0, LANES - 2 * HEADS)))
    ml_w = HEADS * HEAD_W
    pre_params = [cw[0:1], cw[1:2], cw[2:3], cw[3:4], ws["conv_b"],
                  _blockdiag_dense(ws["w_q_ml"]), _blockdiag_dense(ws["w_k_ml"]), _blockdiag_dense(ws["w_v_ml"]),
                  w_if_p[0:ml_w], w_if_p[ml_w:2 * ml_w], w_if_p[2 * ml_w:3 * ml_w],
                  jnp.pad(ws["b_if"], ((0, 0), (0, LANES - 2 * HEADS)))]
    x_pad = jnp.pad(x_m, ((HALO, 0), (0, 0)))
    xc, q_m, k_m, v_m, gl = _ml_pre_fwd(x_m, x_pad, pre_params)
    li, lf = _gate_rows(gl[:, 0:HEADS]), _gate_rows(gl[:, HEADS:2 * HEADS])
    hc, c_prev, n_prev, m_prev = _ml_fwd(q_m, k_m, v_m, li, lf)
    g_ml, skip = ws["g_ml_norm"], ws["ml_skip"]
    (h_b,) = _rowwise("mlstm_out", lambda a, b, c_, g, s: ((jnp.concatenate(_per_head(_ml_out, [a, b, c_], [g, s]), axis=1),), ()),
                      [hc, o_pre, xc], [g_ml, skip], [(ml_w, BF16)])
    y_b = _mm(h_b, weight("w_pb", h_b), "nn", F32, "proj_b")

    (merged,) = _rowwise("merge", lambda ga, gb, ya, yb: ((_merge(ga, gb, ya, yb),), ()), [gate_a, gate_b, y_a, y_b], [],
                         [(d, BF16)])
    z = _mm(merged, weight("w_o", merged), "nn", F32, "proj_o")
    gpm, gpl, gpo = ws["g_post_mix"], ws["g_pre_mlp"], ws["g_post_mlp"]
    x1, h2 = _rowwise("post_mix", lambda xv, zv, a, b: (_post_mix(xv, zv, a, b), ()), [x, z], [gpm, gpl], [(d, F32), (d, BF16)])
    up, u = _mm(h2, weight("w_up", h2), "nn", (F32, BF16), "mlp_up", epilogue=lambda p: (p, jnp.square(jnp.maximum(p, 0.0))))
    dn = _mm(u, weight("w_down", u), "nn", F32, "mlp_down", tm=512)

    def loss_and_grads(x1v, dnv, tgtv, g):
        loss, vjp = jax.vjp(lambda a, b, c_: _loss_rows(a, b, tgtv, c_), x1v, dnv, g)
        dx1, ddn, dg = vjp(jnp.ones((1, 1), F32))
        return (dx1, ddn), (jnp.broadcast_to(loss, (1, LANES)), dg)

    dx1_y, d_dn, loss, d_gpo = _rowwise("loss", loss_and_grads, [x1, dn, tgt], [gpo], [(d, F32), (d, BF16)],
                                        [((1, LANES), F32), ((1, d), F32)])

    (d_up,) = _mm(d_dn, weight("w_down", u), "nt", (BF16,), "mlp_down_dx", extra=[up],
                  epilogue=lambda p, a: (p * (2.0 * jnp.maximum(a, 0.0)),))
    dw_down = _mm(u, d_dn, "tn", BF16, "mlp_down_dw", tm=512)
    d_h2 = _mm(d_up, weight("w_up", h2), "nt", F32, "mlp_up_dx", tm=512)
    dw_up = _mm(h2, d_up, "tn", BF16, "mlp_up_dw")
    sent_mlp = on_grads(dict(w_down=dw_down, w_up=dw_up))

    def post_mix_bwd(xv, zv, dx1, dh2, a, b):
        _, vjp = jax.vjp(_post_mix, xv, zv, a, b)
        dx, dz, da, db = vjp((dx1, dh2))
        return (dx, dz), (da, db)

    dx_res, d_z, d_gpm, d_gpl = _rowwise("post_mix_bwd", post_mix_bwd, [x, z, dx1_y, d_h2], [gpm, gpl],
                                         [(d, F32), (d, BF16)], [((1, d), F32), ((1, d), F32)], deps=dep(sent_mlp))
    d_merged = _mm(d_z, weight("w_o", merged), "nt", F32, "proj_o_dx")
    dw_o = _mm(merged, d_z, "tn", BF16, "proj_o_dw")
    d_ga, d_gb, d_ya, d_yb = _rowwise("merge_bwd", lambda *v: (jax.vjp(_merge, *v[:4])[1](v[4]), ()),
                                      [gate_a, gate_b, y_a, y_b, d_merged], [], [(d, F32), (d, F32), (d, BF16), (d, BF16)])
    d_ya_in = _mm(d_ya, weight("w_pa", ya_in), "nt", F32, "proj_a_dx")
    dw_pa = _mm(ya_in, d_ya, "tn", BF16, "proj_a_dw")
    d_hb = _mm(d_yb, weight("w_pb", h_b), "nt", F32, "proj_b_dx")
    dw_pb = _mm(h_b, d_yb, "tn", BF16, "proj_b_dw")
    sent_mix = on_grads(dict(w_o=dw_o, w_pa=dw_pa, w_pb=dw_pb))

    def ml_out_bwd(a, b, c_, ct, g, s):
        parts = []
        for hs in _head_slices(HEAD_W):
            _, vjp = jax.vjp(_ml_out, a[:, hs], b[:, hs], c_[:, hs], g[:, hs], s[:, hs])
            parts.append(vjp(ct[:, hs]))
        cat = lambda i: jnp.concatenate([p[i] for p in parts], axis=1)
        return (cat(0), cat(1), cat(2)), (cat(3), cat(4))

    d_hc, d_opre, d_xc, d_gml, d_skip = _rowwise("mlstm_out_bwd", ml_out_bwd, [hc, o_pre, xc, d_hb], [g_ml, skip],
                                                 [(ml_w, F32)] * 3, [((1, ml_w), F32)] * 2, deps=dep(sent_mix))
    d_qm, d_km, d_vm, d_li, d_lf = _ml_bwd(q_m, k_m, v_m, li, lf, c_prev, n_prev, m_prev, d_hc)
    d_gl = jnp.concatenate([_gate_cols(d_li), _gate_cols(d_lf), jnp.zeros((t, LANES - 2 * HEADS), F32)], axis=1)
    pre_grads = _ml_pre_bwd(x_m, x_pad, pre_params, [d_xc, d_qm, d_km, d_vm, d_gl])
    d_xm = pre_grads[0]
    d_cw = jnp.concatenate(pre_grads[1:5], axis=0)
    d_cb = pre_grads[5]
    d_wq, d_wk, d_wv = _blockdiag_blocks(pre_grads[6:9])
    d_wif = jnp.concatenate(pre_grads[9:12], axis=0)[:, 0:2 * HEADS]
    d_bif = pre_grads[12][:, 0:2 * HEADS]

    def gla_out_bwd(o, g, ct, n_):
        parts = []
        for hs in _head_slices(HEAD_W):
            _, vjp = jax.vjp(_gla_out, o[:, hs], g[:, hs], n_)
            parts.append(vjp(ct[:, hs]))
        cat = lambda i: jnp.concatenate([p[i] for p in parts], axis=1)
        return (cat(0), cat(1)), (sum(p[2] for p in parts),)

    d_o, d_g_a, d_gn = _rowwise("gla_out_bwd", gla_out_bwd, [o_gla, g_a, d_ya_in], [gn], [(ml_w, F32)] * 2, [((1, HEAD_W), F32)])
    dq_hm, dk_hm, d_va, dla_hm = _gla_bwd(q_hm, k_hm, v_a, la_hm, s_prev, d_o)

    def decay_bwd(al, ct, w, b):
        _, vjp = jax.vjp(_log_decay, al, w, b)
        dal, dw, db = vjp(ct)
        return (dal,), (dw, db)

    d_alow_p, d_wa_p, d_ba = _rowwise("gla_decay_bwd", decay_bwd, [a_low_p, _from_hm(dla_hm)], [w_a_up_p, b_a_up],
                                      [(LANES, F32)], [(w_a_up_p.shape, F32), (b_a_up.shape, F32)])
    d_proj = jnp.concatenate([_from_hm(dq_hm), _from_hm(dk_hm), d_va, d_g_a, d_alow_p[:, 0:LOWRANK], d_xm, d_opre, d_ga, d_gb],
                             axis=1).astype(BF16)
    d_h = _mm(d_proj, weight("w_in", h), "nt", F32, "proj_in_dx", tm=512)
    dw_in = _mm(h, d_proj, "tn", F32, "proj_in_dw", tm=512, tk=1024)

    def pre_mix_bwd(xv, dh, dres, g):
        _, vjp = jax.vjp(_rms, xv, g)
        dx, dg = vjp(dh)
        return (dx + dres,), (dg,)

    grad_x, d_g1 = _rowwise("pre_mix_norm_bwd", pre_mix_bwd, [x, d_h, dx_res], [g1], [(d, F32)], [((1, d), F32)])

    big = dict(w_in=dw_in)
    small = dict(g_pre_mix=d_g1, w_a_up=d_wa_p[0:LOWRANK], b_a_up=d_ba, g_gla_norm=d_gn, conv_w=d_cw, conv_b=d_cb,
                 w_q_ml=d_wq, w_k_ml=d_wk, w_v_ml=d_wv, w_if=d_wif, b_if=d_bif, ml_skip=d_skip, g_ml_norm=d_gml,
                 g_post_mix=d_gpm, g_pre_mlp=d_gpl, g_post_mlp=d_gpo)
    return loss[:, 0:1], grad_x, big, small


BIG = ("w_in", "w_pa", "w_pb", "w_o", "w_up", "w_down")
BIG_COL_SHARDED = ("w_in", "w_pa", "w_pb", "w_up")
SMALL_SHARDED = {"w_a_up": 1, "conv_w": 1, "w_if": 0}
SMALL = ("g_pre_mix", "w_a_up", "b_a_up", "g_gla_norm", "conv_w", "conv_b", "w_q_ml", "w_k_ml", "w_v_ml", "w_if", "b_if",
         "ml_skip", "g_ml_norm", "g_post_mix", "g_pre_mlp", "g_post_mlp")
WEIGHTS = ("g_pre_mix", "w_in", "w_a_up", "b_a_up", "g_gla_norm", "conv_w", "conv_b", "w_q_ml", "w_k_ml", "w_v_ml", "w_if", "b_if",
           "ml_skip", "g_ml_norm", "w_pa", "w_pb", "w_o", "g_post_mix", "g_pre_mlp", "w_up", "w_down", "g_post_mlp")


def _my_slice(a, axis):
    n = a.shape[axis] // N_DEV
    return lax.dynamic_slice_in_dim(a, _lin(_me()) * n, n, axis)


def kernel(x, g_pre_mix, w_in, w_a_up, b_a_up, g_gla_norm, conv_w, conv_b, w_q_ml, w_k_ml, w_v_ml, w_if, b_if, ml_skip, g_ml_norm, w_pa, w_pb, w_o, g_post_mix, g_pre_mlp, w_up, w_down, g_post_mlp, loss_target, m_g_pre_mix, m_w_in, m_w_a_up, m_b_a_up, m_g_gla_norm, m_conv_w, m_conv_b, m_w_q_ml, m_w_k_ml, m_w_v_ml, m_w_if, m_b_if, m_ml_skip, m_g_ml_norm, m_w_pa, m_w_pb, m_w_o, m_g_post_mix, m_g_pre_mlp, m_w_up, m_w_down, m_g_post_mlp, v_g_pre_mix, v_w_in, v_w_a_up, v_b_a_up, v_g_gla_norm, v_conv_w, v_conv_b, v_w_q_ml, v_w_k_ml, v_w_v_ml, v_w_if, v_b_if, v_ml_skip, v_g_ml_norm, v_w_pa, v_w_pb, v_w_o, v_g_post_mix, v_g_pre_mlp, v_w_up, v_w_down, v_g_post_mlp):
    args = dict(locals())
    w = {n: args[n][0] for n in WEIGHTS}
    m = {n: args["m_" + n][0] for n in WEIGHTS}
    v = {n: args["v_" + n][0] for n in WEIGHTS}

    me_lin = _lin(_me())
    me_idx = jnp.reshape(me_lin, (1,)).astype(jnp.int32)

    def full_weight(n, g):
        return _from_col_blocks(g) if n in BIG_COL_SHARDED else g.reshape(-1, g.shape[-1])

    def grad_parts(n, g):
        return (_col_blocks(g) if n in BIG_COL_SHARDED else g.reshape(N_DEV, -1, g.shape[-1])).astype(BF16)

    ready = {"w_in": full_weight("w_in", _allgather_big([w["w_in"].astype(BF16)], "allgather_w_in")[0])}
    pending, tokens = {}, []
    for group in (("w_pa", "w_pb", "w_o"), ("w_up", "w_down")):
        shards = [w[n].astype(BF16) for n in group]
        state, token = _copies_start("gather", shards, "allgather_start_" + group[0])
        tokens.append(token)
        for n in group:
            pending[n] = (group, state)

    def weight(n, after):
        if n not in ready:
            group, state = pending[n]
            shards, lands = _copies_wait(state, after, "allgather_wait_" + group[0])
            for gn, shard, land in zip(group, shards, lands):
                ready[gn] = full_weight(gn, lax.dynamic_update_slice(land, shard[None], (me_lin, 0, 0)))
        return ready[n]

    sharded_names = tuple(SMALL_SHARDED)
    small_g = _allgather_small(_pack([w[n] for n in sharded_names]), "allgather_small_weights")
    ws = {n: (w[n].reshape(1, -1) if w[n].ndim == 1 else w[n]) for n in SMALL if n not in SMALL_SHARDED}
    per_dev = [_unpack(small_g[dev], [w[n].shape for n in sharded_names]) for dev in range(N_DEV)]
    for i, n in enumerate(sharded_names):
        ws[n] = jnp.concatenate([per_dev[dev][i] for dev in range(N_DEV)], axis=SMALL_SHARDED[n])

    sent = []

    def on_grads(grads):
        names = tuple(grads)
        state, token = _copies_start("exchange", [grad_parts(n, grads[n]) for n in names], "exchange_start_" + names[0])
        sent.append((names, state))
        return token

    loss, grad_x, big, small = _local_step(x[0], loss_target[0], weight, ws, tokens, on_grads)

    out = {}
    part_in = grad_parts("w_in", big["w_in"])
    (land_in,) = _exchange_big([part_in], "exchange_w_in")
    out["w_in"] = _sum_adamw(land_in, part_in, me_idx, w["w_in"], m["w_in"], v["w_in"], "adamw_w_in")
    for names, state in sent:
        parts, lands = _copies_wait(state, land_in, "exchange_wait_" + names[0])
        for n, part, land in zip(names, parts, lands):
            out[n] = _sum_adamw(land, part, me_idx, w[n], m[n], v[n], "adamw_" + n)

    small_shapes = [small[n].shape for n in SMALL]
    vec = _sum_slots(_allgather_small(_pack([small[n] for n in SMALL] + [loss]), "allgather_small_grads"), "sum_small_grads")
    summed = _unpack(vec, small_shapes + [(1, 1)])
    g_small = {}
    for n, g in zip(SMALL, summed[:-1]):
        g_small[n] = _my_slice(g, SMALL_SHARDED[n]) if n in SMALL_SHARDED else g
    shapes = [g_small[n].shape for n in SMALL]
    packed = [_pack([d[n].reshape(g_small[n].shape) for n in SMALL]) for d in (w, g_small, m, v)]
    upd = _rowwise("adamw_small", lambda a, b, c_, d_: (_adamw(a, b, c_, d_), ()), packed, [], [(LANES, F32)] * 3, tile=8 * 1024)
    deltas, new_ms, new_vs = [_unpack(p, shapes) for p in upd]
    for i, n in enumerate(SMALL):
        out[n] = (g_small[n], deltas[i], new_ms[i], new_vs[i])

    shaped = lambda a, n: a.reshape(args[n].shape)
    return (summed[-1].reshape(()), grad_x[None],
            *[shaped(out[n][0], n) for n in WEIGHTS], *[shaped(out[n][1], n) for n in WEIGHTS],
            *[shaped(out[n][2], n) for n in WEIGHTS], *[shaped(out[n][3], n) for n in WEIGHTS])
```

```python
import functools

import jax
import jax.numpy as jnp
from jax import lax
from jax.experimental import pallas as pl
from jax.experimental.pallas import tpu as pltpu

F32 = jnp.float32
BF16 = jnp.bfloat16
MESH = pl.DeviceIdType.MESH

N_DEV = 8
EPS = 1e-6
CHUNK = 64
HEADS = 4
GLA_DK = 64
HEAD_W = 128
GLA_GATE_NORM = 16.0
LOWRANK = 16
CONV_K = 4
QKV_BLOCK = 4
LANES = 128
HALO = 8
IN_SPLITS = (256, 256, 512, 512, 16, 512, 512, 1024, 1024)

ADAM_LR = 0.001
ADAM_B1 = 0.9
ADAM_B2 = 0.999
ADAM_EPS = 1e-08
ADAM_WD = 0.01
ADAM_STEP = 10

VMEM_LIMIT = 56 * 1024 * 1024


def _cparams(*sem):
    return pltpu.CompilerParams(dimension_semantics=sem, vmem_limit_bytes=VMEM_LIMIT)


_DN = {"nn": (((1,), (0,)), ((), ())), "nt": (((1,), (1,)), ((), ())), "tn": (((0,), (0,)), ((), ()))}


def _raw_dot(a, b, mode):
    return lax.dot_general(a.astype(BF16), b.astype(BF16), _DN[mode], preferred_element_type=F32)


@functools.partial(jax.custom_vjp, nondiff_argnums=(2,))
def _bdot(a, b, mode):
    return _raw_dot(a, b, mode)


def _bdot_fwd(a, b, mode):
    return _raw_dot(a, b, mode), (a, b)


def _bdot_bwd(mode, res, ct):
    a, b = res
    if mode == "nn":
        da, db = _raw_dot(ct, b, "nt"), _raw_dot(a, ct, "tn")
    elif mode == "nt":
        da, db = _raw_dot(ct, b, "nn"), _raw_dot(ct, a, "tn")
    else:
        da, db = _raw_dot(b, ct, "nt"), _raw_dot(a, ct, "nn")
    return da.astype(a.dtype), db.astype(b.dtype)


_bdot.defvjp(_bdot_fwd, _bdot_bwd)


def _split3(x):
    hi = x.astype(BF16)
    r1 = x - hi.astype(F32)
    mid = r1.astype(BF16)
    return hi, mid, (r1 - mid.astype(F32)).astype(BF16)


def _split_dot(tri, x):
    return sum(lax.dot_general(tri, t, _DN["nn"], preferred_element_type=F32) for t in _split3(x))


def _tri(n, lower):
    r = lax.broadcasted_iota(jnp.int32, (n, n), 0)
    c = lax.broadcasted_iota(jnp.int32, (n, n), 1)
    return ((c <= r) if lower else (c >= r)).astype(BF16)


@jax.custom_vjp
def _cumsum_rows(x):
    return _split_dot(_tri(x.shape[0], True), x)


def _cumsum_rows_fwd(x):
    return _cumsum_rows(x), None


def _cumsum_rows_bwd(_, ct):
    return (_split_dot(_tri(ct.shape[0], False), ct),)


_cumsum_rows.defvjp(_cumsum_rows_fwd, _cumsum_rows_bwd)


def _abs(x):
    return jnp.where(x >= 0, x, -x)


def _sigmoid(x):
    return lax.logistic(x)


def _log_sigmoid(x):
    return jnp.minimum(x, 0.0) - jnp.log(1.0 + jnp.exp(-_abs(x)))


def _rms(x, g):
    return x * lax.rsqrt(jnp.mean(x * x, axis=-1, keepdims=True) + EPS) * g


def _head_slices(w):
    return [slice(h * w, (h + 1) * w) for h in range(HEADS)]


def _tile(dim, want):
    if dim <= want or dim % LANES:
        return dim
    t = want
    while dim % t:
        t -= LANES
    return t


def _mm(a, b, mode, out_dtype, name, tm=1024, tn=1024, tk=4096, epilogue=None, extra=()):
    if mode == "nn":
        (m, k), (k2, n) = a.shape, b.shape
    elif mode == "nt":
        (m, k), (n, k2) = a.shape, b.shape
    else:
        (k, m), (k2, n) = a.shape, b.shape
    assert k == k2, (name, a.shape, b.shape)
    tm, tn, tk = _tile(m, tm), _tile(n, tn), _tile(k, tk)
    nk = k // tk
    out_dtypes = out_dtype if epilogue else (out_dtype,)
    assert nk == 1 or (out_dtype == F32 and not epilogue), name
    n_in = 2 + len(extra)

    def body(*refs):
        p = _raw_dot(refs[0][...], refs[1][...], mode)
        if nk > 1:
            _accumulate(pl.program_id(2), [refs[2]], [p])
            return
        outs = epilogue(p, *[r[...] for r in refs[2:n_in]]) if epilogue else (p,)
        for ref, val in zip(refs[n_in:], outs):
            ref[...] = val.astype(ref.dtype)

    a_spec = pl.BlockSpec((tk, tm), lambda i, j, kk: (kk, i)) if mode == "tn" else pl.BlockSpec((tm, tk), lambda i, j, kk: (i, kk))
    b_spec = pl.BlockSpec((tn, tk), lambda i, j, kk: (j, kk)) if mode == "nt" else pl.BlockSpec((tk, tn), lambda i, j, kk: (kk, j))
    o_spec = pl.BlockSpec((tm, tn), lambda i, j, kk: (i, j))
    res = pl.pallas_call(
        body, name=name, grid=(m // tm, n // tn, nk),
        in_specs=[a_spec, b_spec] + [o_spec] * len(extra), out_specs=[o_spec] * len(out_dtypes),
        out_shape=[jax.ShapeDtypeStruct((m, n), dt) for dt in out_dtypes],
        compiler_params=_cparams("parallel", "parallel", "arbitrary"),
    )(a, b, *extra)
    return res if epilogue else res[0]


def _rowwise(name, fn, rows, params, out_rows, out_accs=(), tile=256, deps=()):
    t = rows[0].shape[0]
    r = min(tile, t)
    assert t % r == 0
    n_in, n_or = len(rows) + len(params), len(out_rows)
    n_all = n_in + len(deps)
    params = list(params) + list(deps)

    def body(*refs):
        vals = [ref[...] for ref in refs[:n_in]]
        outs = refs[n_all:]
        ro, ao = fn(*vals)
        for ref, v in zip(outs[:n_or], ro):
            ref[...] = v.astype(ref.dtype)
        if out_accs:
            _accumulate(pl.program_id(0), outs[n_or:], ao)

    def full(shape):
        return pl.BlockSpec(shape, lambda i, nd=len(shape): (0,) * nd)

    return pl.pallas_call(
        body, name=name, grid=(t // r,),
        in_specs=[pl.BlockSpec((r, a.shape[1]), lambda i: (i, 0)) for a in rows] + [full(p.shape) for p in params],
        out_specs=[pl.BlockSpec((r, w), lambda i: (i, 0)) for w, _ in out_rows] + [full(s) for s, _ in out_accs],
        out_shape=[jax.ShapeDtypeStruct((t, w), dt) for w, dt in out_rows] + [jax.ShapeDtypeStruct(s, dt) for s, dt in out_accs],
        compiler_params=_cparams("arbitrary"),
    )(*rows, *params)


def _accumulate(step, refs, vals):
    for ref, v in zip(refs, vals):
        @pl.when(step == 0)
        def _(ref=ref, v=v):
            ref[...] = v.astype(ref.dtype)

        @pl.when(step > 0)
        def _(ref=ref, v=v):
            ref[...] += v.astype(ref.dtype)


def _gla_chunk(q, k, v, la, st):
    c = q.shape[0]
    row = lax.broadcasted_iota(jnp.int32, (c, c), 0)
    col = lax.broadcasted_iota(jnp.int32, (c, c), 1)
    cum = _cumsum_rows(la)
    cl = jnp.sum(la, axis=0, keepdims=True)
    ep = jnp.exp(cum)
    en = jnp.exp(-cum)
    qs = q * (GLA_DK ** -0.5)
    qp = qs * ep
    a_f = _bdot(qp, k * en, "nt")
    a_b = _bdot(qs * en, k * ep, "nt")
    sc = jnp.where(row >= col, a_f, a_b)
    o = _bdot(sc, v, "nn") + _bdot(qp, st, "nt")
    kd = k * jnp.exp(cl - cum)
    st_new = st * jnp.exp(cl) + _bdot(v, kd, "tn")
    return o, st_new


def _gla_specs(nc, rev):
    def ch(n):
        return (nc - 1 - n) if rev else n
    hm = pl.BlockSpec((HEADS, CHUNK, GLA_DK), lambda n: (0, ch(n), 0))
    tm = pl.BlockSpec((CHUNK, HEADS * HEAD_W), lambda n: (ch(n), 0))
    st = pl.BlockSpec((HEADS, None, HEAD_W, GLA_DK), lambda n: (0, ch(n), 0, 0))
    return hm, tm, st


def _gla_fwd(q, k, v, la):
    t = v.shape[0]
    nc = t // CHUNK
    hm, tm, st = _gla_specs(nc, False)

    def body(q_ref, k_ref, v_ref, la_ref, o_ref, sp_ref, st_ref):
        @pl.when(pl.program_id(0) == 0)
        def _():
            st_ref[...] = jnp.zeros_like(st_ref)

        for h, hs in enumerate(_head_slices(HEAD_W)):
            s = st_ref[h]
            sp_ref[h] = s
            o, s_new = _gla_chunk(q_ref[h], k_ref[h], v_ref[:, hs], la_ref[h], s)
            o_ref[:, hs] = o
            st_ref[h] = s_new

    return pl.pallas_call(
        body, name="gla_fwd", grid=(nc,),
        in_specs=[hm, hm, tm, hm], out_specs=[tm, st],
        out_shape=[jax.ShapeDtypeStruct((t, HEADS * HEAD_W), F32), jax.ShapeDtypeStruct((HEADS, nc, HEAD_W, GLA_DK), F32)],
        scratch_shapes=[pltpu.VMEM((HEADS, HEAD_W, GLA_DK), F32)],
        compiler_params=_cparams("arbitrary"),
    )(q, k, v, la)


def _gla_bwd(q, k, v, la, sp, do):
    t = v.shape[0]
    nc = t // CHUNK
    hm, tm, st = _gla_specs(nc, True)

    def body(q_ref, k_ref, v_ref, la_ref, sp_ref, do_ref, dq_ref, dk_ref, dv_ref, dla_ref, ds_ref):
        @pl.when(pl.program_id(0) == 0)
        def _():
            ds_ref[...] = jnp.zeros_like(ds_ref)

        for h, hs in enumerate(_head_slices(HEAD_W)):
            _, vjp = jax.vjp(_gla_chunk, q_ref[h], k_ref[h], v_ref[:, hs], la_ref[h], sp_ref[h])
            dq, dk, dv, dla, ds = vjp((do_ref[:, hs], ds_ref[h]))
            dq_ref[h] = dq
            dk_ref[h] = dk
            dv_ref[:, hs] = dv
            dla_ref[h] = dla
            ds_ref[h] = ds

    hm_shape = jax.ShapeDtypeStruct((HEADS, t, GLA_DK), F32)
    return pl.pallas_call(
        body, name="gla_bwd", grid=(nc,),
        in_specs=[hm, hm, tm, hm, st, tm], out_specs=[hm, hm, tm, hm],
        out_shape=[hm_shape, hm_shape, jax.ShapeDtypeStruct((t, HEADS * HEAD_W), F32), hm_shape],
        scratch_shapes=[pltpu.VMEM((HEADS, HEAD_W, GLA_DK), F32)],
        compiler_params=_cparams("arbitrary"),
    )(q, k, v, la, sp, do)


def _ml_chunk(q, k, v, li_r, lf_r, cm, nv, m):
    c = q.shape[0]
    row = lax.broadcasted_iota(jnp.int32, (c, c), 0)
    col = lax.broadcasted_iota(jnp.int32, (c, c), 1)
    eye = (row == col).astype(F32)
    li_c = jnp.sum(eye * li_r, axis=1, keepdims=True)
    lf_c = jnp.sum(eye * lf_r, axis=1, keepdims=True)
    fc_c = jnp.sum((col <= row).astype(F32) * lf_r, axis=1, keepdims=True)
    fc_r = jnp.sum((row <= col).astype(F32) * lf_c, axis=0, keepdims=True)
    f_last = jnp.sum(lf_r, axis=1, keepdims=True)
    kc = k * (HEAD_W ** -0.5)
    a_c = f_last - fc_c + li_c
    m_loc = jnp.max(a_c, axis=0, keepdims=True)
    kw = kc * jnp.exp(a_c - m_loc)
    c_chunk = _bdot(kw, v, "tn")
    n_chunk = jnp.sum(kw, axis=0, keepdims=True)
    m_new = jnp.maximum(f_last + m, m_loc)
    sp = jnp.exp(f_last + m - m_new)
    sl = jnp.exp(m_loc - m_new)
    cm_new = sp * cm + sl * c_chunk
    nv_new = sp * nv + sl * n_chunk
    log_d = li_r - _abs(fc_c - fc_r)
    g_inter = fc_c + m
    m_t = jnp.maximum(g_inter, jnp.max(log_d, axis=1, keepdims=True))
    s = _bdot(q, kc, "nt") * jnp.exp(log_d - m_t)
    sc = jnp.exp(g_inter - m_t)
    num = _bdot(s, v, "nn") + sc * _bdot(q, cm, "nn")
    den = jnp.sum(s, axis=1, keepdims=True) + sc * jnp.sum(q * nv, axis=1, keepdims=True)
    den = jnp.maximum(_abs(den), jnp.exp(-m_t))
    return num / den, cm_new, nv_new, m_new


def _ml_specs(nc, rev):
    def ch(n):
        return (nc - 1 - n) if rev else n
    tm = pl.BlockSpec((CHUNK, HEADS * HEAD_W), lambda n: (ch(n), 0))
    gate = pl.BlockSpec((HEADS, None, 1, CHUNK), lambda n: (0, ch(n), 0, 0))
    cm = pl.BlockSpec((HEADS, None, HEAD_W, HEAD_W), lambda n: (0, ch(n), 0, 0))
    vec = pl.BlockSpec((HEADS, None, 1, HEAD_W), lambda n: (0, ch(n), 0, 0))
    return tm, gate, cm, vec


_ML_STATE = [pltpu.VMEM((HEADS, HEAD_W, HEAD_W), F32), pltpu.VMEM((HEADS, 1, HEAD_W), F32), pltpu.VMEM((HEADS, 1, HEAD_W), F32)]


def _ml_fwd(q, k, v, li, lf):
    t = q.shape[0]
    nc = t // CHUNK
    tm, gate, cm, vec = _ml_specs(nc, False)

    def body(q_ref, k_ref, v_ref, li_ref, lf_ref, hc_ref, cp_ref, np_ref, mp_ref, c_ref, n_ref, m_ref):
        @pl.when(pl.program_id(0) == 0)
        def _():
            c_ref[...] = jnp.zeros_like(c_ref)
            n_ref[...] = jnp.zeros_like(n_ref)
            m_ref[...] = jnp.zeros_like(m_ref)

        for h, hs in enumerate(_head_slices(HEAD_W)):
            c0, n0, m0 = c_ref[h], n_ref[h], m_ref[h]
            cp_ref[h] = c0
            np_ref[h] = n0
            mp_ref[h] = m0
            hc, c1, n1, m1 = _ml_chunk(q_ref[:, hs], k_ref[:, hs], v_ref[:, hs], li_ref[h], lf_ref[h], c0, n0, m0[:, 0:1])
            hc_ref[:, hs] = hc
            c_ref[h] = c1
            n_ref[h] = n1
            m_ref[h] = jnp.broadcast_to(m1, (1, HEAD_W))

    return pl.pallas_call(
        body, name="mlstm_fwd", grid=(nc,),
        in_specs=[tm, tm, tm, gate, gate], out_specs=[tm, cm, vec, vec],
        out_shape=[jax.ShapeDtypeStruct((t, HEADS * HEAD_W), F32), jax.ShapeDtypeStruct((HEADS, nc, HEAD_W, HEAD_W), F32),
                   jax.ShapeDtypeStruct((HEADS, nc, 1, HEAD_W), F32), jax.ShapeDtypeStruct((HEADS, nc, 1, HEAD_W), F32)],
        scratch_shapes=_ML_STATE,
        compiler_params=_cparams("arbitrary"),
    )(q, k, v, li, lf)


def _ml_bwd(q, k, v, li, lf, cp, npv, mp, dhc):
    t = q.shape[0]
    nc = t // CHUNK
    tm, gate, cm, vec = _ml_specs(nc, True)

    def body(q_ref, k_ref, v_ref, li_ref, lf_ref, cp_ref, np_ref, mp_ref, dhc_ref,
             dq_ref, dk_ref, dv_ref, dli_ref, dlf_ref, dc_ref, dn_ref, dm_ref):
        @pl.when(pl.program_id(0) == 0)
        def _():
            dc_ref[...] = jnp.zeros_like(dc_ref)
            dn_ref[...] = jnp.zeros_like(dn_ref)
            dm_ref[...] = jnp.zeros_like(dm_ref)

        for h, hs in enumerate(_head_slices(HEAD_W)):
            _, vjp = jax.vjp(_ml_chunk, q_ref[:, hs], k_ref[:, hs], v_ref[:, hs], li_ref[h], lf_ref[h],
                             cp_ref[h], np_ref[h], mp_ref[h][:, 0:1])
            dq, dk, dv, dli, dlf, dc, dn, dm = vjp((dhc_ref[:, hs], dc_ref[h], dn_ref[h], dm_ref[h][:, 0:1]))
            dq_ref[:, hs] = dq
            dk_ref[:, hs] = dk
            dv_ref[:, hs] = dv
            dli_ref[h] = dli
            dlf_ref[h] = dlf
            dc_ref[h] = dc
            dn_ref[h] = dn
            dm_ref[h] = jnp.broadcast_to(dm, (1, HEAD_W))

    tm_shape = jax.ShapeDtypeStruct((t, HEADS * HEAD_W), F32)
    gate_shape = jax.ShapeDtypeStruct((HEADS, nc, 1, CHUNK), F32)
    return pl.pallas_call(
        body, name="mlstm_bwd", grid=(nc,),
        in_specs=[tm, tm, tm, gate, gate, cm, vec, vec, tm], out_specs=[tm, tm, tm, gate, gate],
        out_shape=[tm_shape, tm_shape, tm_shape, gate_shape, gate_shape],
        scratch_shapes=_ML_STATE,
        compiler_params=_cparams("arbitrary"),
    )(q, k, v, li, lf, cp, npv, mp, dhc)


def _ml_pre(s0, s1, s2, s3, cw0, cw1, cw2, cw3, cb, wq, wk, wv, wiq, wik, wiv, bif):
    pre = cb + cw0 * s0 + cw1 * s1 + cw2 * s2 + cw3 * s3
    xc = pre * _sigmoid(pre)
    q = _bdot(xc, wq, "nn")
    k = _bdot(xc, wk, "nn")
    v = _bdot(s3, wv, "nn")
    gates = _bdot(q, wiq, "nn") + _bdot(k, wik, "nn") + _bdot(v, wiv, "nn") + bif
    lane = lax.broadcasted_iota(jnp.int32, gates.shape, 1)
    gl = jnp.where(lane < HEADS, gates, _log_sigmoid(gates))
    return xc, q, k, v, gl


def _delayed(xs_ref, x_ref, halo_ref, r):
    xs_ref[0:HALO, :] = halo_ref[...]
    xs_ref[HALO:HALO + r, :] = x_ref[...]
    return [xs_ref[pl.ds(HALO - (CONV_K - 1) + j, r), :] for j in range(CONV_K)]


def _full_spec(shape):
    return pl.BlockSpec(shape, lambda i, nd=len(shape): (0,) * nd)


def _ml_pre_fwd(x_m, x_pad, params, tile=256):
    t, w = x_m.shape
    r = min(tile, t)

    def body(*refs):
        x_ref, halo_ref = refs[:2]
        p = [ref[...] for ref in refs[2:2 + len(params)]]
        outs = refs[2 + len(params):-1]
        res = _ml_pre(*_delayed(refs[-1], x_ref, halo_ref, r), *p)
        for ref, val in zip(outs, res):
            ref[...] = val

    row = pl.BlockSpec((r, w), lambda i: (i, 0))
    return pl.pallas_call(
        body, name="ml_pre_fwd", grid=(t // r,),
        in_specs=[row, pl.BlockSpec((HALO, w), lambda i: (i * (r // HALO), 0))] + [_full_spec(p.shape) for p in params],
        out_specs=[row] * 4 + [pl.BlockSpec((r, LANES), lambda i: (i, 0))],
        out_shape=[jax.ShapeDtypeStruct((t, w), F32)] * 4 + [jax.ShapeDtypeStruct((t, LANES), F32)],
        scratch_shapes=[pltpu.VMEM((r + HALO, w), F32)],
        compiler_params=_cparams("arbitrary"),
    )(x_m, x_pad, *params)


def _ml_pre_bwd(x_m, x_pad, params, cts, tile=256):
    t, w = x_m.shape
    r = min(tile, t)
    nt = t // r
    n_p = len(params)

    def body(*refs):
        x_ref, halo_ref = refs[:2]
        p = [ref[...] for ref in refs[2:2 + n_p]]
        ct = [ref[...] for ref in refs[2 + n_p:7 + n_p]]
        dx_ref = refs[7 + n_p]
        dp_refs = refs[8 + n_p:8 + 2 * n_p]
        xs_ref, ds_ref, carry_ref = refs[8 + 2 * n_p:]
        step = pl.program_id(0)

        @pl.when(step == 0)
        def _():
            ds_ref[...] = jnp.zeros_like(ds_ref)
            carry_ref[...] = jnp.zeros_like(carry_ref)

        _, vjp = jax.vjp(_ml_pre, *_delayed(xs_ref, x_ref, halo_ref, r), *p)
        grads = vjp(tuple(ct))
        for j in range(CONV_K):
            ds_ref[j, HALO:HALO + r, :] = grads[j]
        lead = HALO + CONV_K - 1
        d_tile = sum(ds_ref[j, pl.ds(lead - j, r), :] for j in range(CONV_K))
        d_halo = sum(ds_ref[j, pl.ds(CONV_K - 1 - j, HALO), :] for j in range(CONV_K))
        dx_ref[...] = d_tile
        dx_ref[r - HALO:r, :] += carry_ref[...]
        carry_ref[...] = d_halo
        _accumulate(step, dp_refs, grads[CONV_K:])

    row = pl.BlockSpec((r, w), lambda i: (nt - 1 - i, 0))
    return pl.pallas_call(
        body, name="ml_pre_bwd", grid=(nt,),
        in_specs=[row, pl.BlockSpec((HALO, w), lambda i: ((nt - 1 - i) * (r // HALO), 0))] + [_full_spec(p.shape) for p in params]
        + [row] * 4 + [pl.BlockSpec((r, LANES), lambda i: (nt - 1 - i, 0))],
        out_specs=[row] + [_full_spec(p.shape) for p in params],
        out_shape=[jax.ShapeDtypeStruct((t, w), F32)] + [jax.ShapeDtypeStruct(p.shape, F32) for p in params],
        scratch_shapes=[pltpu.VMEM((r + HALO, w), F32), pltpu.VMEM((CONV_K, r + 2 * HALO, w), F32), pltpu.VMEM((HALO, w), F32)],
        compiler_params=_cparams("arbitrary"),
    )(x_m, x_pad, *params, *cts)


def _per_head(fn, row_vals, head_params, shared_params=()):
    return [fn(*[a[:, hs] for a in row_vals], *[p[:, hs] for p in head_params], *shared_params) for hs in _head_slices(HEAD_W)]


def _gla_out(o, g, gn):
    return _rms(o, gn) * (g * _sigmoid(g))


def _ml_out(hc, op, xc, g, sk):
    hcell = hc * _sigmoid(op)
    mu = jnp.mean(hcell, axis=-1, keepdims=True)
    d = hcell - mu
    var = jnp.mean(d * d, axis=-1, keepdims=True)
    return d * lax.rsqrt(var + EPS) * g + sk * xc


def _log_decay(al, w, b):
    return _log_sigmoid(_bdot(al, w, "nn") + b) * (1.0 / GLA_GATE_NORM)


def _merge(ga, gb, ya, yb):
    return _sigmoid(ga) * ya + _sigmoid(gb) * yb


def _post_mix(x, z, gpm, gpl):
    x1 = x + _rms(z, gpm)
    return x1, _rms(x1, gpl)


def _loss_rows(x1, dn, tgt, g):
    e = x1 + _rms(dn, g) - tgt
    return 0.5 * jnp.sum(jnp.mean(e * e, axis=-1, keepdims=True), axis=0, keepdims=True)


def _lin(p):
    return 4 * p[0] + 2 * p[1] + p[2]


def _me():
    return lax.axis_index("x"), lax.axis_index("y"), lax.axis_index("c")


def _flip(p, k):
    return tuple((1 - v) if (k >> (2 - i)) & 1 else v for i, v in enumerate(p))


ANY = pl.BlockSpec(memory_space=pl.ANY)


def _allgather_big(shards, name):
    n = len(shards)

    def body(*refs):
        ins, outs = refs[:n], refs[n:2 * n]
        send_sems, recv_sems, local_sems = refs[2 * n:]
        me = _me()
        x, y, c = me
        sib = (x, y, 1 - c)
        chips = [(1 - x, y), (x, 1 - y), (1 - x, 1 - y)]

        def cp(a, k, block, to, src=None):
            dst = outs[a].at[_lin(block)]
            return pltpu.make_async_remote_copy(src_ref=dst if src is None else src, dst_ref=dst,
                                                send_sem=send_sems.at[a * 7 + k], recv_sem=recv_sems.at[a * 7 + k],
                                                device_id=to, device_id_type=MESH)

        mine = [pltpu.make_async_copy(ins[a], outs[a].at[_lin(me)], local_sems.at[a]) for a in range(n)]
        for m in mine:
            m.start()
        first = []
        for a in range(n):
            first.append(cp(a, 0, me, sib, src=ins[a]))
            first += [cp(a, 1 + j, me, (*chip, c), src=ins[a]) for j, chip in enumerate(chips)]
        for f in first:
            f.start()
        passed = []
        for j, chip in enumerate(chips):
            for a in range(n):
                cp(a, 1 + j, (*chip, c), me).wait_recv()
                fwd = cp(a, 4 + j, (*chip, c), sib)
                fwd.start()
                passed.append(fwd)
        for a in range(n):
            cp(a, 0, sib, me).wait_recv()
            for j, chip in enumerate(chips):
                cp(a, 4 + j, (*chip, 1 - c), me).wait_recv()
        for f in first + passed:
            f.wait_send()
        for m in mine:
            m.wait()

    return pl.pallas_call(
        body, name=name,
        in_specs=[ANY] * n, out_specs=[ANY] * n,
        out_shape=[jax.ShapeDtypeStruct((N_DEV, *s.shape), s.dtype) for s in shards],
        scratch_shapes=[pltpu.SemaphoreType.DMA((7 * n,)), pltpu.SemaphoreType.DMA((7 * n,)), pltpu.SemaphoreType.DMA((n,))],
    )(*shards)


def _exchange_big(parts, name):
    n = len(parts)

    def body(*refs):
        ins, outs = refs[:n], refs[n:2 * n]
        send_sems, recv_sems, local_sems = refs[2 * n:]
        me = _me()
        copies = []
        for a in range(n):
            loc = pltpu.make_async_copy(ins[a].at[_lin(me)], outs[a].at[_lin(me)], local_sems.at[a])
            loc.start()
            copies.append(loc)
            for k in range(1, N_DEV):
                peer = _flip(me, k)
                rc = pltpu.make_async_remote_copy(src_ref=ins[a].at[_lin(peer)], dst_ref=outs[a].at[_lin(me)],
                                                  send_sem=send_sems.at[a * 7 + k - 1], recv_sem=recv_sems.at[a * 7 + k - 1],
                                                  device_id=peer, device_id_type=MESH)
                rc.start()
                copies.append(rc)
        for cpy in copies:
            cpy.wait()

    return pl.pallas_call(
        body, name=name,
        in_specs=[ANY] * n, out_specs=[ANY] * n,
        out_shape=[jax.ShapeDtypeStruct(p.shape, p.dtype) for p in parts],
        scratch_shapes=[pltpu.SemaphoreType.DMA((7 * n,)), pltpu.SemaphoreType.DMA((7 * n,)), pltpu.SemaphoreType.DMA((n,))],
    )(*parts)


HBM = pl.BlockSpec(memory_space=pltpu.HBM)
SEM = pl.BlockSpec(memory_space=pltpu.SEMAPHORE)
DATAFLOW = pltpu.SideEffectType.DATAFLOW_SIDE_EFFECTING


def _peer_copies(kind, srcs, lands, send_sems, recv_sems):
    me = _me()
    copies = []
    for a, (src, land) in enumerate(zip(srcs, lands)):
        for k in range(1, N_DEV):
            peer = _flip(me, k)
            copies.append(pltpu.make_async_remote_copy(
                src_ref=src if kind == "gather" else src.at[_lin(peer)], dst_ref=land.at[_lin(me)],
                send_sem=send_sems.at[a * 7 + k - 1], recv_sem=recv_sems.at[a * 7 + k - 1],
                device_id=peer, device_id_type=MESH))
    return copies


def _copies_start(kind, srcs, name):
    n = len(srcs)
    land_shapes = [((N_DEV, *s.shape) if kind == "gather" else s.shape) for s in srcs]

    def body(*refs):
        for cp in _peer_copies(kind, refs[:n], refs[n:2 * n], refs[2 * n], refs[2 * n + 1]):
            cp.start()
        refs[-1][...] = jnp.zeros_like(refs[-1])

    def hbm(a):
        return pltpu.with_memory_space_constraint(a, pltpu.HBM)

    out = pl.pallas_call(
        body, name=name,
        out_shape=(pltpu.SemaphoreType.DMA((7 * n,)), pltpu.SemaphoreType.DMA((7 * n,)),
                   *[pltpu.HBM(s.shape, s.dtype) for s in srcs],
                   *[pltpu.HBM(ls, s.dtype) for ls, s in zip(land_shapes, srcs)],
                   jax.ShapeDtypeStruct((8, LANES), F32)),
        in_specs=[HBM] * (2 * n), out_specs=(SEM, SEM, *[HBM] * (2 * n), pl.BlockSpec(memory_space=pltpu.VMEM)),
        input_output_aliases={i: 2 + i for i in range(2 * n)},
        compiler_params=pltpu.CompilerParams(has_side_effects=DATAFLOW),
    )(*[hbm(s) for s in srcs], *[hbm(lax.empty(ls, s.dtype)) for ls, s in zip(land_shapes, srcs)])
    return (kind, n, out[:-1]), out[-1]


def _copies_wait(state, after, name):
    kind, n, (send_sems, recv_sems, *thru) = state

    def body(*refs):
        for cp in _peer_copies(kind, refs[:n], refs[n:2 * n], refs[2 * n], refs[2 * n + 1]):
            cp.wait_send()
            cp.wait_recv()

    out = pl.pallas_call(
        body, name=name,
        out_shape=tuple(pltpu.HBM(t.shape, t.dtype) for t in thru),
        in_specs=[HBM] * (2 * n) + [SEM, SEM, ANY], out_specs=tuple([HBM] * (2 * n)),
        input_output_aliases={i: i for i in range(2 * n)},
        compiler_params=pltpu.CompilerParams(has_side_effects=DATAFLOW),
    )(*thru, send_sems, recv_sems, after)
    return out[:n], out[n:]


def _allgather_small(vec, name):
    r, w = vec.shape

    def body(x_ref, out_ref, send_sems, recv_sems):
        me = _me()
        out_ref[_lin(me)] = x_ref[...]
        copies = []
        for k in range(1, N_DEV):
            rc = pltpu.make_async_remote_copy(src_ref=x_ref, dst_ref=out_ref.at[_lin(me)],
                                              send_sem=send_sems.at[k - 1], recv_sem=recv_sems.at[k - 1],
                                              device_id=_flip(me, k), device_id_type=MESH)
            rc.start()
            copies.append(rc)
        for rc in copies:
            rc.wait()

    return pl.pallas_call(
        body, name=name,
        in_specs=[pl.BlockSpec(memory_space=pltpu.VMEM)], out_specs=pl.BlockSpec(memory_space=pltpu.VMEM),
        out_shape=jax.ShapeDtypeStruct((N_DEV, r, w), vec.dtype),
        scratch_shapes=[pltpu.SemaphoreType.DMA((7,)), pltpu.SemaphoreType.DMA((7,))],
    )(vec)


def _adamw(w, g, m, v):
    m2 = ADAM_B1 * m + (1.0 - ADAM_B1) * g
    v2 = ADAM_B2 * v + (1.0 - ADAM_B2) * (g * g)
    m_hat = m2 / (1.0 - ADAM_B1 ** ADAM_STEP)
    v_hat = v2 / (1.0 - ADAM_B2 ** ADAM_STEP)
    delta = -ADAM_LR * (m_hat / (jnp.sqrt(v_hat) + ADAM_EPS) + ADAM_WD * w)
    return delta, m2, v2


def _sum_adamw(land, part, me_idx, w, m, v, name, tile=256):
    r, c = w.shape
    tr = min(tile, r)

    def body(me_ref, own_ref, *refs):
        slots = refs[:N_DEV]
        w_ref, m_ref, v_ref, g_ref, d_ref, m2_ref, v2_ref = refs[N_DEV:]
        own = own_ref[...].astype(F32)
        g = None
        for s in range(N_DEV):
            term = jnp.where(me_ref[0] == s, own, slots[s][...].astype(F32))
            g = term if g is None else g + term
        d, m2, v2 = _adamw(w_ref[...], g, m_ref[...], v_ref[...])
        g_ref[...] = g
        d_ref[...] = d
        m2_ref[...] = m2
        v2_ref[...] = v2

    def slot_spec(s):
        return pl.BlockSpec((None, tr, c), lambda i, me: (jnp.where(me[0] == s, (s + 1) % N_DEV, s), i, 0))

    row = pl.BlockSpec((tr, c), lambda i, me: (i, 0))
    return pl.pallas_call(
        body, name=name,
        grid_spec=pltpu.PrefetchScalarGridSpec(
            num_scalar_prefetch=1, grid=(r // tr,),
            in_specs=[pl.BlockSpec((None, tr, c), lambda i, me: (me[0], i, 0))] + [slot_spec(s) for s in range(N_DEV)] + [row] * 3,
            out_specs=[row] * 4),
        out_shape=[jax.ShapeDtypeStruct((r, c), F32)] * 4,
        compiler_params=_cparams("parallel"),
    )(me_idx, part, *[land] * N_DEV, w, m, v)


def _sum_slots(gathered, name):
    _, r, w = gathered.shape

    def body(p_ref, o_ref):
        g = p_ref[0]
        for s in range(1, N_DEV):
            g = g + p_ref[s]
        o_ref[...] = g

    return pl.pallas_call(body, name=name, out_shape=jax.ShapeDtypeStruct((r, w), F32))(gathered)


def _pack(arrs):
    flat = jnp.concatenate([a.reshape(-1).astype(F32) for a in arrs])
    rows = -(-flat.shape[0] // (8 * LANES)) * 8
    return jnp.pad(flat, (0, rows * LANES - flat.shape[0])).reshape(rows, LANES)


def _unpack(packed, shapes):
    flat = packed.reshape(-1)
    out, off = [], 0
    for s in shapes:
        size = 1
        for d in s:
            size *= d
        out.append(flat[off:off + size].reshape(s))
        off += size
    return out


def _to_hm(a, d):
    t = a.shape[0]
    return a.reshape(t, HEADS, d).transpose(1, 0, 2)


def _from_hm(a):
    h, t, d = a.shape
    return a.transpose(1, 0, 2).reshape(t, h * d)


def _gate_rows(g):
    t = g.shape[0]
    return g.T.reshape(HEADS, t // CHUNK, 1, CHUNK)


def _gate_cols(g):
    h, nc, _, c = g.shape
    return g.reshape(h, nc * c).T


def _blockdiag_dense(w):
    n = w.shape[0] * QKV_BLOCK
    tiled = jnp.tile(w.reshape(n, QKV_BLOCK), (1, n // QKV_BLOCK))
    r = lax.broadcasted_iota(jnp.int32, (n, n), 0)
    c = lax.broadcasted_iota(jnp.int32, (n, n), 1)
    return jnp.where(r // QKV_BLOCK == c // QKV_BLOCK, tiled, 0.0)


def _blockdiag_blocks(dense):
    n = dense[0].shape[0]
    k = len(dense)

    def body(*refs):
        r = lax.broadcasted_iota(jnp.int32, (n, n), 0)
        c = lax.broadcasted_iota(jnp.int32, (n, n), 1)
        fr = lax.broadcasted_iota(jnp.int32, (n, LANES), 0)
        fc = lax.broadcasted_iota(jnp.int32, (n, LANES), 1)
        fold = ((fr & (QKV_BLOCK - 1)) == fc).astype(BF16)
        for i in range(k):
            kept = jnp.where((r >> 2) == (c >> 2), refs[i][...], 0.0)
            refs[k + i][...] = sum(lax.dot_general(t, fold, _DN["nn"], preferred_element_type=F32) for t in _split3(kept))

    out = pl.pallas_call(body, name="blockdiag_blocks", out_shape=[jax.ShapeDtypeStruct((n, LANES), F32)] * k)(*dense)
    return [o[:, 0:QKV_BLOCK].reshape(n // QKV_BLOCK, QKV_BLOCK, QKV_BLOCK) for o in out]


def _col_blocks(w):
    k, n = w.shape
    return w.reshape(k, N_DEV, n // N_DEV).transpose(1, 0, 2)


def _from_col_blocks(g):
    d, k, n = g.shape
    return g.transpose(1, 0, 2).reshape(k, d * n)


def _local_step(x, tgt, weight, ws, tokens=(), on_grads=None):
    t, d = x.shape
    g1 = ws["g_pre_mix"]
    on_grads = on_grads or (lambda grads: None)

    def dep(token):
        return () if token is None else (token,)

    (h,) = _rowwise("pre_mix_norm", lambda xv, g: ((_rms(xv, g),), ()), [x], [g1], [(d, BF16)], deps=tokens)
    proj = _mm(h, weight("w_in", h), "nn", F32, "proj_in", tm=512)
    offs = [0]
    for s in IN_SPLITS:
        offs.append(offs[-1] + s)
    q_a, k_a, v_a, g_a, a_low, x_m, o_pre, gate_a, gate_b = [proj[:, offs[i]:offs[i + 1]] for i in range(9)]

    a_low_p = jnp.pad(a_low, ((0, 0), (0, LANES - LOWRANK)))
    w_a_up_p = jnp.pad(ws["w_a_up"], ((0, LANES - LOWRANK), (0, 0)))
    b_a_up = ws["b_a_up"]
    (la,) = _rowwise("gla_decay", lambda al, w, b: ((_log_decay(al, w, b),), ()), [a_low_p], [w_a_up_p, b_a_up],
                     [(HEADS * GLA_DK, F32)])
    q_hm, k_hm, la_hm = _to_hm(q_a, GLA_DK), _to_hm(k_a, GLA_DK), _to_hm(la, GLA_DK)
    o_gla, s_prev = _gla_fwd(q_hm, k_hm, v_a, la_hm)
    gn = ws["g_gla_norm"]
    (ya_in,) = _rowwise("gla_out", lambda o, g, n_: ((jnp.concatenate(_per_head(_gla_out, [o, g], [], [n_]), axis=1),), ()),
                        [o_gla, g_a], [gn], [(HEADS * HEAD_W, BF16)])
    y_a = _mm(ya_in, weight("w_pa", ya_in), "nn", F32, "proj_a")

    cw = ws["conv_w"]
    w_if_p = jnp.pad(ws["w_if"], ((0, 0), (0, LANES - 2 * HEADS)))
    ml_w = HEADS * HEAD_W
    pre_params = [cw[0:1], cw[1:2], cw[2:3], cw[3:4], ws["conv_b"],
                  _blockdiag_dense(ws["w_q_ml"]), _blockdiag_dense(ws["w_k_ml"]), _blockdiag_dense(ws["w_v_ml"]),
                  w_if_p[0:ml_w], w_if_p[ml_w:2 * ml_w], w_if_p[2 * ml_w:3 * ml_w],
                  jnp.pad(ws["b_if"], ((0, 0), (0, LANES - 2 * HEADS)))]
    x_pad = jnp.pad(x_m, ((HALO, 0), (0, 0)))
    xc, q_m, k_m, v_m, gl = _ml_pre_fwd(x_m, x_pad, pre_params)
    li, lf = _gate_rows(gl[:, 0:HEADS]), _gate_rows(gl[:, HEADS:2 * HEADS])
    hc, c_prev, n_prev, m_prev = _ml_fwd(q_m, k_m, v_m, li, lf)
    g_ml, skip = ws["g_ml_norm"], ws["ml_skip"]
    (h_b,) = _rowwise("mlstm_out", lambda a, b, c_, g, s: ((jnp.concatenate(_per_head(_ml_out, [a, b, c_], [g, s]), axis=1),), ()),
                      [hc, o_pre, xc], [g_ml, skip], [(ml_w, BF16)])
    y_b = _mm(h_b, weight("w_pb", h_b), "nn", F32, "proj_b")

    (merged,) = _rowwise("merge", lambda ga, gb, ya, yb: ((_merge(ga, gb, ya, yb),), ()), [gate_a, gate_b, y_a, y_b], [],
                         [(d, BF16)])
    z = _mm(merged, weight("w_o", merged), "nn", F32, "proj_o")
    gpm, gpl, gpo = ws["g_post_mix"], ws["g_pre_mlp"], ws["g_post_mlp"]
    x1, h2 = _rowwise("post_mix", lambda xv, zv, a, b: (_post_mix(xv, zv, a, b), ()), [x, z], [gpm, gpl], [(d, F32), (d, BF16)])
    up, u = _mm(h2, weight("w_up", h2), "nn", (F32, BF16), "mlp_up", epilogue=lambda p: (p, jnp.square(jnp.maximum(p, 0.0))))
    dn = _mm(u, weight("w_down", u), "nn", F32, "mlp_down", tm=512)

    def loss_and_grads(x1v, dnv, tgtv, g):
        loss, vjp = jax.vjp(lambda a, b, c_: _loss_rows(a, b, tgtv, c_), x1v, dnv, g)
        dx1, ddn, dg = vjp(jnp.ones((1, 1), F32))
        return (dx1, ddn), (jnp.broadcast_to(loss, (1, LANES)), dg)

    dx1_y, d_dn, loss, d_gpo = _rowwise("loss", loss_and_grads, [x1, dn, tgt], [gpo], [(d, F32), (d, BF16)],
                                        [((1, LANES), F32), ((1, d), F32)])

    (d_up,) = _mm(d_dn, weight("w_down", u), "nt", (BF16,), "mlp_down_dx", extra=[up],
                  epilogue=lambda p, a: (p * (2.0 * jnp.maximum(a, 0.0)),))
    dw_down = _mm(u, d_dn, "tn", BF16, "mlp_down_dw", tm=512)
    d_h2 = _mm(d_up, weight("w_up", h2), "nt", F32, "mlp_up_dx", tm=512)
    dw_up = _mm(h2, d_up, "tn", BF16, "mlp_up_dw")
    sent_mlp = on_grads(dict(w_down=dw_down, w_up=dw_up))

    def post_mix_bwd(xv, zv, dx1, dh2, a, b):
        _, vjp = jax.vjp(_post_mix, xv, zv, a, b)
        dx, dz, da, db = vjp((dx1, dh2))
        return (dx, dz), (da, db)

    dx_res, d_z, d_gpm, d_gpl = _rowwise("post_mix_bwd", post_mix_bwd, [x, z, dx1_y, d_h2], [gpm, gpl],
                                         [(d, F32), (d, BF16)], [((1, d), F32), ((1, d), F32)], deps=dep(sent_mlp))
    d_merged = _mm(d_z, weight("w_o", merged), "nt", F32, "proj_o_dx")
    dw_o = _mm(merged, d_z, "tn", BF16, "proj_o_dw")
    d_ga, d_gb, d_ya, d_yb = _rowwise("merge_bwd", lambda *v: (jax.vjp(_merge, *v[:4])[1](v[4]), ()),
                                      [gate_a, gate_b, y_a, y_b, d_merged], [], [(d, F32), (d, F32), (d, BF16), (d, BF16)])
    d_ya_in = _mm(d_ya, weight("w_pa", ya_in), "nt", F32, "proj_a_dx")
    dw_pa = _mm(ya_in, d_ya, "tn", BF16, "proj_a_dw")
    d_hb = _mm(d_yb, weight("w_pb", h_b), "nt", F32, "proj_b_dx")
    dw_pb = _mm(h_b, d_yb, "tn", BF16, "proj_b_dw")
    sent_mix = on_grads(dict(w_o=dw_o, w_pa=dw_pa, w_pb=dw_pb))

    def ml_out_bwd(a, b, c_, ct, g, s):
        parts = []
        for hs in _head_slices(HEAD_W):
            _, vjp = jax.vjp(_ml_out, a[:, hs], b[:, hs], c_[:, hs], g[:, hs], s[:, hs])
            parts.append(vjp(ct[:, hs]))
        cat = lambda i: jnp.concatenate([p[i] for p in parts], axis=1)
        return (cat(0), cat(1), cat(2)), (cat(3), cat(4))

    d_hc, d_opre, d_xc, d_gml, d_skip = _rowwise("mlstm_out_bwd", ml_out_bwd, [hc, o_pre, xc, d_hb], [g_ml, skip],
                                                 [(ml_w, F32)] * 3, [((1, ml_w), F32)] * 2, deps=dep(sent_mix))
    d_qm, d_km, d_vm, d_li, d_lf = _ml_bwd(q_m, k_m, v_m, li, lf, c_prev, n_prev, m_prev, d_hc)
    d_gl = jnp.concatenate([_gate_cols(d_li), _gate_cols(d_lf), jnp.zeros((t, LANES - 2 * HEADS), F32)], axis=1)
    pre_grads = _ml_pre_bwd(x_m, x_pad, pre_params, [d_xc, d_qm, d_km, d_vm, d_gl])
    d_xm = pre_grads[0]
    d_cw = jnp.concatenate(pre_grads[1:5], axis=0)
    d_cb = pre_grads[5]
    d_wq, d_wk, d_wv = _blockdiag_blocks(pre_grads[6:9])
    d_wif = jnp.concatenate(pre_grads[9:12], axis=0)[:, 0:2 * HEADS]
    d_bif = pre_grads[12][:, 0:2 * HEADS]

    def gla_out_bwd(o, g, ct, n_):
        parts = []
        for hs in _head_slices(HEAD_W):
            _, vjp = jax.vjp(_gla_out, o[:, hs], g[:, hs], n_)
            parts.append(vjp(ct[:, hs]))
        cat = lambda i: jnp.concatenate([p[i] for p in parts], axis=1)
        return (cat(0), cat(1)), (sum(p[2] for p in parts),)

    d_o, d_g_a, d_gn = _rowwise("gla_out_bwd", gla_out_bwd, [o_gla, g_a, d_ya_in], [gn], [(ml_w, F32)] * 2, [((1, HEAD_W), F32)])
    dq_hm, dk_hm, d_va, dla_hm = _gla_bwd(q_hm, k_hm, v_a, la_hm, s_prev, d_o)

    def decay_bwd(al, ct, w, b):
        _, vjp = jax.vjp(_log_decay, al, w, b)
        dal, dw, db = vjp(ct)
        return (dal,), (dw, db)

    d_alow_p, d_wa_p, d_ba = _rowwise("gla_decay_bwd", decay_bwd, [a_low_p, _from_hm(dla_hm)], [w_a_up_p, b_a_up],
                                      [(LANES, F32)], [(w_a_up_p.shape, F32), (b_a_up.shape, F32)])
    d_proj = jnp.concatenate([_from_hm(dq_hm), _from_hm(dk_hm), d_va, d_g_a, d_alow_p[:, 0:LOWRANK], d_xm, d_opre, d_ga, d_gb],
                             axis=1).astype(BF16)
    d_h = _mm(d_proj, weight("w_in", h), "nt", F32, "proj_in_dx", tm=512)
    dw_in = _mm(h, d_proj, "tn", F32, "proj_in_dw", tm=512, tk=1024)

    def pre_mix_bwd(xv, dh, dres, g):
        _, vjp = jax.vjp(_rms, xv, g)
        dx, dg = vjp(dh)
        return (dx + dres,), (dg,)

    grad_x, d_g1 = _rowwise("pre_mix_norm_bwd", pre_mix_bwd, [x, d_h, dx_res], [g1], [(d, F32)], [((1, d), F32)])

    big = dict(w_in=dw_in)
    small = dict(g_pre_mix=d_g1, w_a_up=d_wa_p[0:LOWRANK], b_a_up=d_ba, g_gla_norm=d_gn, conv_w=d_cw, conv_b=d_cb,
                 w_q_ml=d_wq, w_k_ml=d_wk, w_v_ml=d_wv, w_if=d_wif, b_if=d_bif, ml_skip=d_skip, g_ml_norm=d_gml,
                 g_post_mix=d_gpm, g_pre_mlp=d_gpl, g_post_mlp=d_gpo)
    return loss[:, 0:1], grad_x, big, small


BIG = ("w_in", "w_pa", "w_pb", "w_o", "w_up", "w_down")
BIG_COL_SHARDED = ("w_in", "w_pa", "w_pb", "w_up")
SMALL_SHARDED = {"w_a_up": 1, "conv_w": 1, "w_if": 0}
SMALL = ("g_pre_mix", "w_a_up", "b_a_up", "g_gla_norm", "conv_w", "conv_b", "w_q_ml", "w_k_ml", "w_v_ml", "w_if", "b_if",
         "ml_skip", "g_ml_norm", "g_post_mix", "g_pre_mlp", "g_post_mlp")
WEIGHTS = ("g_pre_mix", "w_in", "w_a_up", "b_a_up", "g_gla_norm", "conv_w", "conv_b", "w_q_ml", "w_k_ml", "w_v_ml", "w_if", "b_if",
           "ml_skip", "g_ml_norm", "w_pa", "w_pb", "w_o", "g_post_mix", "g_pre_mlp", "w_up", "w_down", "g_post_mlp")


def _my_slice(a, axis):
    n = a.shape[axis] // N_DEV
    return lax.dynamic_slice_in_dim(a, _lin(_me()) * n, n, axis)


def kernel(x, g_pre_mix, w_in, w_a_up, b_a_up, g_gla_norm, conv_w, conv_b, w_q_ml, w_k_ml, w_v_ml, w_if, b_if, ml_skip, g_ml_norm, w_pa, w_pb, w_o, g_post_mix, g_pre_mlp, w_up, w_down, g_post_mlp, loss_target, m_g_pre_mix, m_w_in, m_w_a_up, m_b_a_up, m_g_gla_norm, m_conv_w, m_conv_b, m_w_q_ml, m_w_k_ml, m_w_v_ml, m_w_if, m_b_if, m_ml_skip, m_g_ml_norm, m_w_pa, m_w_pb, m_w_o, m_g_post_mix, m_g_pre_mlp, m_w_up, m_w_down, m_g_post_mlp, v_g_pre_mix, v_w_in, v_w_a_up, v_b_a_up, v_g_gla_norm, v_conv_w, v_conv_b, v_w_q_ml, v_w_k_ml, v_w_v_ml, v_w_if, v_b_if, v_ml_skip, v_g_ml_norm, v_w_pa, v_w_pb, v_w_o, v_g_post_mix, v_g_pre_mlp, v_w_up, v_w_down, v_g_post_mlp):
    args = dict(locals())
    w = {n: args[n][0] for n in WEIGHTS}
    m = {n: args["m_" + n][0] for n in WEIGHTS}
    v = {n: args["v_" + n][0] for n in WEIGHTS}

    me_lin = _lin(_me())
    me_idx = jnp.reshape(me_lin, (1,)).astype(jnp.int32)

    def full_weight(n, g):
        return _from_col_blocks(g) if n in BIG_COL_SHARDED else g.reshape(-1, g.shape[-1])

    def grad_parts(n, g):
        return (_col_blocks(g) if n in BIG_COL_SHARDED else g.reshape(N_DEV, -1, g.shape[-1])).astype(BF16)

    ready = {"w_in": full_weight("w_in", _allgather_big([w["w_in"].astype(BF16)], "allgather_w_in")[0])}
    pending, tokens = {}, []
    for group in (("w_pa", "w_pb", "w_o"), ("w_up", "w_down")):
        shards = [w[n].astype(BF16) for n in group]
        state, token = _copies_start("gather", shards, "allgather_start_" + group[0])
        tokens.append(token)
        for n in group:
            pending[n] = (group, state)

    def weight(n, after):
        if n not in ready:
            group, state = pending[n]
            shards, lands = _copies_wait(state, after, "allgather_wait_" + group[0])
            for gn, shard, land in zip(group, shards, lands):
                ready[gn] = full_weight(gn, lax.dynamic_update_slice(land, shard[None], (me_lin, 0, 0)))
        return ready[n]

    sharded_names = tuple(SMALL_SHARDED)
    small_g = _allgather_small(_pack([w[n] for n in sharded_names]), "allgather_small_weights")
    ws = {n: (w[n].reshape(1, -1) if w[n].ndim == 1 else w[n]) for n in SMALL if n not in SMALL_SHARDED}
    per_dev = [_unpack(small_g[dev], [w[n].shape for n in sharded_names]) for dev in range(N_DEV)]
    for i, n in enumerate(sharded_names):
        ws[n] = jnp.concatenate([per_dev[dev][i] for dev in range(N_DEV)], axis=SMALL_SHARDED[n])

    sent = []

    def on_grads(grads):
        names = tuple(grads)
        state, token = _copies_start("exchange", [grad_parts(n, grads[n]) for n in names], "exchange_start_" + names[0])
        sent.append((names, state))
        return token

    loss, grad_x, big, small = _local_step(x[0], loss_target[0], weight, ws, tokens, on_grads)

    out = {}
    part_in = grad_parts("w_in", big["w_in"])
    (land_in,) = _exchange_big([part_in], "exchange_w_in")
    out["w_in"] = _sum_adamw(land_in, part_in, me_idx, w["w_in"], m["w_in"], v["w_in"], "adamw_w_in")
    for names, state in sent:
        parts, lands = _copies_wait(state, land_in, "exchange_wait_" + names[0])
        for n, part, land in zip(names, parts, lands):
            out[n] = _sum_adamw(land, part, me_idx, w[n], m[n], v[n], "adamw_" + n)

    small_shapes = [small[n].shape for n in SMALL]
    vec = _sum_slots(_allgather_small(_pack([small[n] for n in SMALL] + [loss]), "allgather_small_grads"), "sum_small_grads")
    summed = _unpack(vec, small_shapes + [(1, 1)])
    g_small = {}
    for n, g in zip(SMALL, summed[:-1]):
        g_small[n] = _my_slice(g, SMALL_SHARDED[n]) if n in SMALL_SHARDED else g
    shapes = [g_small[n].shape for n in SMALL]
    packed = [_pack([d[n].reshape(g_small[n].shape) for n in SMALL]) for d in (w, g_small, m, v)]
    upd = _rowwise("adamw_small", lambda a, b, c_, d_: (_adamw(a, b, c_, d_), ()), packed, [], [(LANES, F32)] * 3, tile=8 * 1024)
    deltas, new_ms, new_vs = [_unpack(p, shapes) for p in upd]
    for i, n in enumerate(SMALL):
        out[n] = (g_small[n], deltas[i], new_ms[i], new_vs[i])

    shaped = lambda a, n: a.reshape(args[n].shape)
    return (summed[-1].reshape(()), grad_x[None],
            *[shaped(out[n][0], n) for n in WEIGHTS], *[shaped(out[n][1], n) for n in WEIGHTS],
            *[shaped(out[n][2], n) for n in WEIGHTS], *[shaped(out[n][3], n) for n in WEIGHTS])
```

```python
import functools

import jax
import jax.numpy as jnp
from jax import lax
from jax.experimental import pallas as pl
from jax.experimental.pallas import tpu as pltpu

F32 = jnp.float32
BF16 = jnp.bfloat16
MESH = pl.DeviceIdType.MESH

N_DEV = 8
EPS = 1e-6
CHUNK = 64
HEADS = 4
GLA_DK = 64
HEAD_W = 128
GLA_GATE_NORM = 16.0
LOWRANK = 16
CONV_K = 4
QKV_BLOCK = 4
LANES = 128
HALO = 8
IN_SPLITS = (256, 256, 512, 512, 16, 512, 512, 1024, 1024)

ADAM_LR = 0.001
ADAM_B1 = 0.9
ADAM_B2 = 0.999
ADAM_EPS = 1e-08
ADAM_WD = 0.01
ADAM_STEP = 10

VMEM_LIMIT = 56 * 1024 * 1024


def _cparams(*sem):
    return pltpu.CompilerParams(dimension_semantics=sem, vmem_limit_bytes=VMEM_LIMIT)


_DN = {"nn": (((1,), (0,)), ((), ())), "nt": (((1,), (1,)), ((), ())), "tn": (((0,), (0,)), ((), ()))}


def _raw_dot(a, b, mode):
    return lax.dot_general(a.astype(BF16), b.astype(BF16), _DN[mode], preferred_element_type=F32)


@functools.partial(jax.custom_vjp, nondiff_argnums=(2,))
def _bdot(a, b, mode):
    return _raw_dot(a, b, mode)


def _bdot_fwd(a, b, mode):
    return _raw_dot(a, b, mode), (a, b)


def _bdot_bwd(mode, res, ct):
    a, b = res
    if mode == "nn":
        da, db = _raw_dot(ct, b, "nt"), _raw_dot(a, ct, "tn")
    elif mode == "nt":
        da, db = _raw_dot(ct, b, "nn"), _raw_dot(ct, a, "tn")
    else:
        da, db = _raw_dot(b, ct, "nt"), _raw_dot(a, ct, "nn")
    return da.astype(a.dtype), db.astype(b.dtype)


_bdot.defvjp(_bdot_fwd, _bdot_bwd)


def _split3(x):
    hi = x.astype(BF16)
    r1 = x - hi.astype(F32)
    mid = r1.astype(BF16)
    return hi, mid, (r1 - mid.astype(F32)).astype(BF16)


def _split_dot(tri, x):
    return sum(lax.dot_general(tri, t, _DN["nn"], preferred_element_type=F32) for t in _split3(x))


def _tri(n, lower):
    r = lax.broadcasted_iota(jnp.int32, (n, n), 0)
    c = lax.broadcasted_iota(jnp.int32, (n, n), 1)
    return ((c <= r) if lower else (c >= r)).astype(BF16)


@jax.custom_vjp
def _cumsum_rows(x):
    return _split_dot(_tri(x.shape[0], True), x)


def _cumsum_rows_fwd(x):
    return _cumsum_rows(x), None


def _cumsum_rows_bwd(_, ct):
    return (_split_dot(_tri(ct.shape[0], False), ct),)


_cumsum_rows.defvjp(_cumsum_rows_fwd, _cumsum_rows_bwd)


def _abs(x):
    return jnp.where(x >= 0, x, -x)


def _sigmoid(x):
    return lax.logistic(x)


def _log_sigmoid(x):
    return jnp.minimum(x, 0.0) - jnp.log(1.0 + jnp.exp(-_abs(x)))


def _rms(x, g):
    return x * lax.rsqrt(jnp.mean(x * x, axis=-1, keepdims=True) + EPS) * g


def _head_slices(w):
    return [slice(h * w, (h + 1) * w) for h in range(HEADS)]


def _tile(dim, want):
    if dim <= want or dim % LANES:
        return dim
    t = want
    while dim % t:
        t -= LANES
    return t


def _mm(a, b, mode, out_dtype, name, tm=1024, tn=1024, tk=4096, epilogue=None, extra=()):
    if mode == "nn":
        (m, k), (k2, n) = a.shape, b.shape
    elif mode == "nt":
        (m, k), (n, k2) = a.shape, b.shape
    else:
        (k, m), (k2, n) = a.shape, b.shape
    assert k == k2, (name, a.shape, b.shape)
    tm, tn, tk = _tile(m, tm), _tile(n, tn), _tile(k, tk)
    nk = k // tk
    out_dtypes = out_dtype if epilogue else (out_dtype,)
    assert nk == 1 or (out_dtype == F32 and not epilogue), name
    n_in = 2 + len(extra)

    def body(*refs):
        p = _raw_dot(refs[0][...], refs[1][...], mode)
        if nk > 1:
            _accumulate(pl.program_id(2), [refs[2]], [p])
            return
        outs = epilogue(p, *[r[...] for r in refs[2:n_in]]) if epilogue else (p,)
        for ref, val in zip(refs[n_in:], outs):
            ref[...] = val.astype(ref.dtype)

    a_spec = pl.BlockSpec((tk, tm), lambda i, j, kk: (kk, i)) if mode == "tn" else pl.BlockSpec((tm, tk), lambda i, j, kk: (i, kk))
    b_spec = pl.BlockSpec((tn, tk), lambda i, j, kk: (j, kk)) if mode == "nt" else pl.BlockSpec((tk, tn), lambda i, j, kk: (kk, j))
    o_spec = pl.BlockSpec((tm, tn), lambda i, j, kk: (i, j))
    res = pl.pallas_call(
        body, name=name, grid=(m // tm, n // tn, nk),
        in_specs=[a_spec, b_spec] + [o_spec] * len(extra), out_specs=[o_spec] * len(out_dtypes),
        out_shape=[jax.ShapeDtypeStruct((m, n), dt) for dt in out_dtypes],
        compiler_params=_cparams("parallel", "parallel", "arbitrary"),
    )(a, b, *extra)
    return res if epilogue else res[0]


def _rowwise(name, fn, rows, params, out_rows, out_accs=(), tile=256, deps=()):
    t = rows[0].shape[0]
    r = min(tile, t)
    assert t % r == 0
    n_in, n_or = len(rows) + len(params), len(out_rows)
    n_all = n_in + len(deps)
    params = list(params) + list(deps)

    def body(*refs):
        vals = [ref[...] for ref in refs[:n_in]]
        outs = refs[n_all:]
        ro, ao = fn(*vals)
        for ref, v in zip(outs[:n_or], ro):
            ref[...] = v.astype(ref.dtype)
        if out_accs:
            _accumulate(pl.program_id(0), outs[n_or:], ao)

    def full(shape):
        return pl.BlockSpec(shape, lambda i, nd=len(shape): (0,) * nd)

    return pl.pallas_call(
        body, name=name, grid=(t // r,),
        in_specs=[pl.BlockSpec((r, a.shape[1]), lambda i: (i, 0)) for a in rows] + [full(p.shape) for p in params],
        out_specs=[pl.BlockSpec((r, w), lambda i: (i, 0)) for w, _ in out_rows] + [full(s) for s, _ in out_accs],
        out_shape=[jax.ShapeDtypeStruct((t, w), dt) for w, dt in out_rows] + [jax.ShapeDtypeStruct(s, dt) for s, dt in out_accs],
        compiler_params=_cparams("arbitrary"),
    )(*rows, *params)


def _accumulate(step, refs, vals):
    for ref, v in zip(refs, vals):
        @pl.when(step == 0)
        def _(ref=ref, v=v):
            ref[...] = v.astype(ref.dtype)

        @pl.when(step > 0)
        def _(ref=ref, v=v):
            ref[...] += v.astype(ref.dtype)


def _gla_chunk(q, k, v, la, st):
    c = q.shape[0]
    row = lax.broadcasted_iota(jnp.int32, (c, c), 0)
    col = lax.broadcasted_iota(jnp.int32, (c, c), 1)
    cum = _cumsum_rows(la)
    cl = jnp.sum(la, axis=0, keepdims=True)
    ep = jnp.exp(cum)
    en = jnp.exp(-cum)
    qs = q * (GLA_DK ** -0.5)
    qp = qs * ep
    a_f = _bdot(qp, k * en, "nt")
    a_b = _bdot(qs * en, k * ep, "nt")
    sc = jnp.where(row >= col, a_f, a_b)
    o = _bdot(sc, v, "nn") + _bdot(qp, st, "nt")
    kd = k * jnp.exp(cl - cum)
    st_new = st * jnp.exp(cl) + _bdot(v, kd, "tn")
    return o, st_new


def _gla_specs(nc, rev):
    def ch(n):
        return (nc - 1 - n) if rev else n
    hm = pl.BlockSpec((HEADS, CHUNK, GLA_DK), lambda n: (0, ch(n), 0))
    tm = pl.BlockSpec((CHUNK, HEADS * HEAD_W), lambda n: (ch(n), 0))
    st = pl.BlockSpec((HEADS, None, HEAD_W, GLA_DK), lambda n: (0, ch(n), 0, 0))
    return hm, tm, st


def _gla_fwd(q, k, v, la):
    t = v.shape[0]
    nc = t // CHUNK
    hm, tm, st = _gla_specs(nc, False)

    def body(q_ref, k_ref, v_ref, la_ref, o_ref, sp_ref, st_ref):
        @pl.when(pl.program_id(0) == 0)
        def _():
            st_ref[...] = jnp.zeros_like(st_ref)

        for h, hs in enumerate(_head_slices(HEAD_W)):
            s = st_ref[h]
            sp_ref[h] = s
            o, s_new = _gla_chunk(q_ref[h], k_ref[h], v_ref[:, hs], la_ref[h], s)
            o_ref[:, hs] = o
            st_ref[h] = s_new

    return pl.pallas_call(
        body, name="gla_fwd", grid=(nc,),
        in_specs=[hm, hm, tm, hm], out_specs=[tm, st],
        out_shape=[jax.ShapeDtypeStruct((t, HEADS * HEAD_W), F32), jax.ShapeDtypeStruct((HEADS, nc, HEAD_W, GLA_DK), F32)],
        scratch_shapes=[pltpu.VMEM((HEADS, HEAD_W, GLA_DK), F32)],
        compiler_params=_cparams("arbitrary"),
    )(q, k, v, la)


def _gla_bwd(q, k, v, la, sp, do):
    t = v.shape[0]
    nc = t // CHUNK
    hm, tm, st = _gla_specs(nc, True)

    def body(q_ref, k_ref, v_ref, la_ref, sp_ref, do_ref, dq_ref, dk_ref, dv_ref, dla_ref, ds_ref):
        @pl.when(pl.program_id(0) == 0)
        def _():
            ds_ref[...] = jnp.zeros_like(ds_ref)

        for h, hs in enumerate(_head_slices(HEAD_W)):
            _, vjp = jax.vjp(_gla_chunk, q_ref[h], k_ref[h], v_ref[:, hs], la_ref[h], sp_ref[h])
            dq, dk, dv, dla, ds = vjp((do_ref[:, hs], ds_ref[h]))
            dq_ref[h] = dq
            dk_ref[h] = dk
            dv_ref[:, hs] = dv
            dla_ref[h] = dla
            ds_ref[h] = ds

    hm_shape = jax.ShapeDtypeStruct((HEADS, t, GLA_DK), F32)
    return pl.pallas_call(
        body, name="gla_bwd", grid=(nc,),
        in_specs=[hm, hm, tm, hm, st, tm], out_specs=[hm, hm, tm, hm],
        out_shape=[hm_shape, hm_shape, jax.ShapeDtypeStruct((t, HEADS * HEAD_W), F32), hm_shape],
        scratch_shapes=[pltpu.VMEM((HEADS, HEAD_W, GLA_DK), F32)],
        compiler_params=_cparams("arbitrary"),
    )(q, k, v, la, sp, do)


def _ml_chunk(q, k, v, li_r, lf_r, cm, nv, m):
    c = q.shape[0]
    row = lax.broadcasted_iota(jnp.int32, (c, c), 0)
    col = lax.broadcasted_iota(jnp.int32, (c, c), 1)
    eye = (row == col).astype(F32)
    li_c = jnp.sum(eye * li_r, axis=1, keepdims=True)
    lf_c = jnp.sum(eye * lf_r, axis=1, keepdims=True)
    fc_c = jnp.sum((col <= row).astype(F32) * lf_r, axis=1, keepdims=True)
    fc_r = jnp.sum((row <= col).astype(F32) * lf_c, axis=0, keepdims=True)
    f_last = jnp.sum(lf_r, axis=1, keepdims=True)
    kc = k * (HEAD_W ** -0.5)
    a_c = f_last - fc_c + li_c
    m_loc = jnp.max(a_c, axis=0, keepdims=True)
    kw = kc * jnp.exp(a_c - m_loc)
    c_chunk = _bdot(kw, v, "tn")
    n_chunk = jnp.sum(kw, axis=0, keepdims=True)
    m_new = jnp.maximum(f_last + m, m_loc)
    sp = jnp.exp(f_last + m - m_new)
    sl = jnp.exp(m_loc - m_new)
    cm_new = sp * cm + sl * c_chunk
    nv_new = sp * nv + sl * n_chunk
    log_d = li_r - _abs(fc_c - fc_r)
    g_inter = fc_c + m
    m_t = jnp.maximum(g_inter, jnp.max(log_d, axis=1, keepdims=True))
    s = _bdot(q, kc, "nt") * jnp.exp(log_d - m_t)
    sc = jnp.exp(g_inter - m_t)
    num = _bdot(s, v, "nn") + sc * _bdot(q, cm, "nn")
    den = jnp.sum(s, axis=1, keepdims=True) + sc * jnp.sum(q * nv, axis=1, keepdims=True)
    den = jnp.maximum(_abs(den), jnp.exp(-m_t))
    return num / den, cm_new, nv_new, m_new


def _ml_specs(nc, rev):
    def ch(n):
        return (nc - 1 - n) if rev else n
    tm = pl.BlockSpec((CHUNK, HEADS * HEAD_W), lambda n: (ch(n), 0))
    gate = pl.BlockSpec((HEADS, None, 1, CHUNK), lambda n: (0, ch(n), 0, 0))
    cm = pl.BlockSpec((HEADS, None, HEAD_W, HEAD_W), lambda n: (0, ch(n), 0, 0))
    vec = pl.BlockSpec((HEADS, None, 1, HEAD_W), lambda n: (0, ch(n), 0, 0))
    return tm, gate, cm, vec


_ML_STATE = [pltpu.VMEM((HEADS, HEAD_W, HEAD_W), F32), pltpu.VMEM((HEADS, 1, HEAD_W), F32), pltpu.VMEM((HEADS, 1, HEAD_W), F32)]


def _ml_fwd(q, k, v, li, lf):
    t = q.shape[0]
    nc = t // CHUNK
    tm, gate, cm, vec = _ml_specs(nc, False)

    def body(q_ref, k_ref, v_ref, li_ref, lf_ref, hc_ref, cp_ref, np_ref, mp_ref, c_ref, n_ref, m_ref):
        @pl.when(pl.program_id(0) == 0)
        def _():
            c_ref[...] = jnp.zeros_like(c_ref)
            n_ref[...] = jnp.zeros_like(n_ref)
            m_ref[...] = jnp.zeros_like(m_ref)

        for h, hs in enumerate(_head_slices(HEAD_W)):
            c0, n0, m0 = c_ref[h], n_ref[h], m_ref[h]
            cp_ref[h] = c0
            np_ref[h] = n0
            mp_ref[h] = m0
            hc, c1, n1, m1 = _ml_chunk(q_ref[:, hs], k_ref[:, hs], v_ref[:, hs], li_ref[h], lf_ref[h], c0, n0, m0[:, 0:1])
            hc_ref[:, hs] = hc
            c_ref[h] = c1
            n_ref[h] = n1
            m_ref[h] = jnp.broadcast_to(m1, (1, HEAD_W))

    return pl.pallas_call(
        body, name="mlstm_fwd", grid=(nc,),
        in_specs=[tm, tm, tm, gate, gate], out_specs=[tm, cm, vec, vec],
        out_shape=[jax.ShapeDtypeStruct((t, HEADS * HEAD_W), F32), jax.ShapeDtypeStruct((HEADS, nc, HEAD_W, HEAD_W), F32),
                   jax.ShapeDtypeStruct((HEADS, nc, 1, HEAD_W), F32), jax.ShapeDtypeStruct((HEADS, nc, 1, HEAD_W), F32)],
        scratch_shapes=_ML_STATE,
        compiler_params=_cparams("arbitrary"),
    )(q, k, v, li, lf)


def _ml_bwd(q, k, v, li, lf, cp, npv, mp, dhc):
    t = q.shape[0]
    nc = t // CHUNK
    tm, gate, cm, vec = _ml_specs(nc, True)

    def body(q_ref, k_ref, v_ref, li_ref, lf_ref, cp_ref, np_ref, mp_ref, dhc_ref,
             dq_ref, dk_ref, dv_ref, dli_ref, dlf_ref, dc_ref, dn_ref, dm_ref):
        @pl.when(pl.program_id(0) == 0)
        def _():
            dc_ref[...] = jnp.zeros_like(dc_ref)
            dn_ref[...] = jnp.zeros_like(dn_ref)
            dm_ref[...] = jnp.zeros_like(dm_ref)

        for h, hs in enumerate(_head_slices(HEAD_W)):
            _, vjp = jax.vjp(_ml_chunk, q_ref[:, hs], k_ref[:, hs], v_ref[:, hs], li_ref[h], lf_ref[h],
                             cp_ref[h], np_ref[h], mp_ref[h][:, 0:1])
            dq, dk, dv, dli, dlf, dc, dn, dm = vjp((dhc_ref[:, hs], dc_ref[h], dn_ref[h], dm_ref[h][:, 0:1]))
            dq_ref[:, hs] = dq
            dk_ref[:, hs] = dk
            dv_ref[:, hs] = dv
            dli_ref[h] = dli
            dlf_ref[h] = dlf
            dc_ref[h] = dc
            dn_ref[h] = dn
            dm_ref[h] = jnp.broadcast_to(dm, (1, HEAD_W))

    tm_shape = jax.ShapeDtypeStruct((t, HEADS * HEAD_W), F32)
    gate_shape = jax.ShapeDtypeStruct((HEADS, nc, 1, CHUNK), F32)
    return pl.pallas_call(
        body, name="mlstm_bwd", grid=(nc,),
        in_specs=[tm, tm, tm, gate, gate, cm, vec, vec, tm], out_specs=[tm, tm, tm, gate, gate],
        out_shape=[tm_shape, tm_shape, tm_shape, gate_shape, gate_shape],
        scratch_shapes=_ML_STATE,
        compiler_params=_cparams("arbitrary"),
    )(q, k, v, li, lf, cp, npv, mp, dhc)


def _ml_pre(s0, s1, s2, s3, cw0, cw1, cw2, cw3, cb, wq, wk, wv, wiq, wik, wiv, bif):
    pre = cb + cw0 * s0 + cw1 * s1 + cw2 * s2 + cw3 * s3
    xc = pre * _sigmoid(pre)
    q = _bdot(xc, wq, "nn")
    k = _bdot(xc, wk, "nn")
    v = _bdot(s3, wv, "nn")
    gates = _bdot(q, wiq, "nn") + _bdot(k, wik, "nn") + _bdot(v, wiv, "nn") + bif
    lane = lax.broadcasted_iota(jnp.int32, gates.shape, 1)
    gl = jnp.where(lane < HEADS, gates, _log_sigmoid(gates))
    return xc, q, k, v, gl


def _delayed(xs_ref, x_ref, halo_ref, r):
    xs_ref[0:HALO, :] = halo_ref[...]
    xs_ref[HALO:HALO + r, :] = x_ref[...]
    return [xs_ref[pl.ds(HALO - (CONV_K - 1) + j, r), :] for j in range(CONV_K)]


def _full_spec(shape):
    return pl.BlockSpec(shape, lambda i, nd=len(shape): (0,) * nd)


def _ml_pre_fwd(x_m, x_pad, params, tile=256):
    t, w = x_m.shape
    r = min(tile, t)

    def body(*refs):
        x_ref, halo_ref = refs[:2]
        p = [ref[...] for ref in refs[2:2 + len(params)]]
        outs = refs[2 + len(params):-1]
        res = _ml_pre(*_delayed(refs[-1], x_ref, halo_ref, r), *p)
        for ref, val in zip(outs, res):
            ref[...] = val

    row = pl.BlockSpec((r, w), lambda i: (i, 0))
    return pl.pallas_call(
        body, name="ml_pre_fwd", grid=(t // r,),
        in_specs=[row, pl.BlockSpec((HALO, w), lambda i: (i * (r // HALO), 0))] + [_full_spec(p.shape) for p in params],
        out_specs=[row] * 4 + [pl.BlockSpec((r, LANES), lambda i: (i, 0))],
        out_shape=[jax.ShapeDtypeStruct((t, w), F32)] * 4 + [jax.ShapeDtypeStruct((t, LANES), F32)],
        scratch_shapes=[pltpu.VMEM((r + HALO, w), F32)],
        compiler_params=_cparams("arbitrary"),
    )(x_m, x_pad, *params)


def _ml_pre_bwd(x_m, x_pad, params, cts, tile=256):
    t, w = x_m.shape
    r = min(tile, t)
    nt = t // r
    n_p = len(params)

    def body(*refs):
        x_ref, halo_ref = refs[:2]
        p = [ref[...] for ref in refs[2:2 + n_p]]
        ct = [ref[...] for ref in refs[2 + n_p:7 + n_p]]
        dx_ref = refs[7 + n_p]
        dp_refs = refs[8 + n_p:8 + 2 * n_p]
        xs_ref, ds_ref, carry_ref = refs[8 + 2 * n_p:]
        step = pl.program_id(0)

        @pl.when(step == 0)
        def _():
            ds_ref[...] = jnp.zeros_like(ds_ref)
            carry_ref[...] = jnp.zeros_like(carry_ref)

        _, vjp = jax.vjp(_ml_pre, *_delayed(xs_ref, x_ref, halo_ref, r), *p)
        grads = vjp(tuple(ct))
        for j in range(CONV_K):
            ds_ref[j, HALO:HALO + r, :] = grads[j]
        lead = HALO + CONV_K - 1
        d_tile = sum(ds_ref[j, pl.ds(lead - j, r), :] for j in range(CONV_K))
        d_halo = sum(ds_ref[j, pl.ds(CONV_K - 1 - j, HALO), :] for j in range(CONV_K))
        dx_ref[...] = d_tile
        dx_ref[r - HALO:r, :] += carry_ref[...]
        carry_ref[...] = d_halo
        _accumulate(step, dp_refs, grads[CONV_K:])

    row = pl.BlockSpec((r, w), lambda i: (nt - 1 - i, 0))
    return pl.pallas_call(
        body, name="ml_pre_bwd", grid=(nt,),
        in_specs=[row, pl.BlockSpec((HALO, w), lambda i: ((nt - 1 - i) * (r // HALO), 0))] + [_full_spec(p.shape) for p in params]
        + [row] * 4 + [pl.BlockSpec((r, LANES), lambda i: (nt - 1 - i, 0))],
        out_specs=[row] + [_full_spec(p.shape) for p in params],
        out_shape=[jax.ShapeDtypeStruct((t, w), F32)] + [jax.ShapeDtypeStruct(p.shape, F32) for p in params],
        scratch_shapes=[pltpu.VMEM((r + HALO, w), F32), pltpu.VMEM((CONV_K, r + 2 * HALO, w), F32), pltpu.VMEM((HALO, w), F32)],
        compiler_params=_cparams("arbitrary"),
    )(x_m, x_pad, *params, *cts)


def _per_head(fn, row_vals, head_params, shared_params=()):
    return [fn(*[a[:, hs] for a in row_vals], *[p[:, hs] for p in head_params], *shared_params) for hs in _head_slices(HEAD_W)]


def _gla_out(o, g, gn):
    return _rms(o, gn) * (g * _sigmoid(g))


def _ml_out(hc, op, xc, g, sk):
    hcell = hc * _sigmoid(op)
    mu = jnp.mean(hcell, axis=-1, keepdims=True)
    d = hcell - mu
    var = jnp.mean(d * d, axis=-1, keepdims=True)
    return d * lax.rsqrt(var + EPS) * g + sk * xc


def _log_decay(al, w, b):
    return _log_sigmoid(_bdot(al, w, "nn") + b) * (1.0 / GLA_GATE_NORM)


def _merge(ga, gb, ya, yb):
    return _sigmoid(ga) * ya + _sigmoid(gb) * yb


def _post_mix(x, z, gpm, gpl):
    x1 = x + _rms(z, gpm)
    return x1, _rms(x1, gpl)


def _loss_rows(x1, dn, tgt, g):
    e = x1 + _rms(dn, g) - tgt
    return 0.5 * jnp.sum(jnp.mean(e * e, axis=-1, keepdims=True), axis=0, keepdims=True)


def _lin(p):
    return 4 * p[0] + 2 * p[1] + p[2]


def _me():
    return lax.axis_index("x"), lax.axis_index("y"), lax.axis_index("c")


def _flip(p, k):
    return tuple((1 - v) if (k >> (2 - i)) & 1 else v for i, v in enumerate(p))


ANY = pl.BlockSpec(memory_space=pl.ANY)


def _allgather_big(shards, name):
    n = len(shards)

    def body(*refs):
        ins, outs = refs[:n], refs[n:2 * n]
        send_sems, recv_sems, local_sems = refs[2 * n:]
        me = _me()
        x, y, c = me
        sib = (x, y, 1 - c)
        chips = [(1 - x, y), (x, 1 - y), (1 - x, 1 - y)]

        def cp(a, k, block, to, src=None):
            dst = outs[a].at[_lin(block)]
            return pltpu.make_async_remote_copy(src_ref=dst if src is None else src, dst_ref=dst,
                                                send_sem=send_sems.at[a * 7 + k], recv_sem=recv_sems.at[a * 7 + k],
                                                device_id=to, device_id_type=MESH)

        mine = [pltpu.make_async_copy(ins[a], outs[a].at[_lin(me)], local_sems.at[a]) for a in range(n)]
        for m in mine:
            m.start()
        first = []
        for a in range(n):
            first.append(cp(a, 0, me, sib, src=ins[a]))
            first += [cp(a, 1 + j, me, (*chip, c), src=ins[a]) for j, chip in enumerate(chips)]
        for f in first:
            f.start()
        passed = []
        for j, chip in enumerate(chips):
            for a in range(n):
                cp(a, 1 + j, (*chip, c), me).wait_recv()
                fwd = cp(a, 4 + j, (*chip, c), sib)
                fwd.start()
                passed.append(fwd)
        for a in range(n):
            cp(a, 0, sib, me).wait_recv()
            for j, chip in enumerate(chips):
                cp(a, 4 + j, (*chip, 1 - c), me).wait_recv()
        for f in first + passed:
            f.wait_send()
        for m in mine:
            m.wait()

    return pl.pallas_call(
        body, name=name,
        in_specs=[ANY] * n, out_specs=[ANY] * n,
        out_shape=[jax.ShapeDtypeStruct((N_DEV, *s.shape), s.dtype) for s in shards],
        scratch_shapes=[pltpu.SemaphoreType.DMA((7 * n,)), pltpu.SemaphoreType.DMA((7 * n,)), pltpu.SemaphoreType.DMA((n,))],
    )(*shards)


HBM =pl.BlockSpec(memory_space=pltpu.HBM)
SEM = pl.BlockSpec(memory_space=pltpu.SEMAPHORE)
DATAFLOW = pltpu.SideEffectType.DATAFLOW_SIDE_EFFECTING


def _peer_copies(kind, srcs, lands, send_sems, recv_sems):
    me = _me()
    copies = []
    for a, (src, land) in enumerate(zip(srcs, lands)):
        for k in range(1, N_DEV):
            peer = _flip(me, k)
            copies.append(pltpu.make_async_remote_copy(
                src_ref=src if kind == "gather" else src.at[_lin(peer)], dst_ref=land.at[_lin(me)],
                send_sem=send_sems.at[a * 7 + k - 1], recv_sem=recv_sems.at[a * 7 + k - 1],
                device_id=peer, device_id_type=MESH))
    return copies


def _copies_start(kind, srcs, name, after=None):
    n = len(srcs)
    extra = [] if after is None else [after]
    land_shapes = [((N_DEV, *s.shape) if kind == "gather" else s.shape) for s in srcs]

    def body(*refs):
        sems = refs[2 * n + len(extra):]
        for cp in _peer_copies(kind, refs[:n], refs[n:2 * n], sems[0], sems[1]):
            cp.start()
        refs[-1][...] = jnp.zeros_like(refs[-1])

    def hbm(a):
        return pltpu.with_memory_space_constraint(a, pltpu.HBM)

    out = pl.pallas_call(
        body, name=name,
        out_shape=(pltpu.SemaphoreType.DMA((7 * n,)), pltpu.SemaphoreType.DMA((7 * n,)),
                   *[pltpu.HBM(s.shape, s.dtype) for s in srcs],
                   *[pltpu.HBM(ls, s.dtype) for ls, s in zip(land_shapes, srcs)],
                   jax.ShapeDtypeStruct((8, LANES), F32)),
        in_specs=[HBM] * (2 * n) + [ANY] * len(extra),
        out_specs=(SEM, SEM, *[HBM] * (2 * n), pl.BlockSpec(memory_space=pltpu.VMEM)),
        input_output_aliases={i: 2 + i for i in range(2 * n)},
        compiler_params=pltpu.CompilerParams(has_side_effects=DATAFLOW),
    )(*[hbm(s) for s in srcs], *[hbm(lax.empty(ls, s.dtype)) for ls, s in zip(land_shapes, srcs)], *extra)
    return (kind, n, out[:-1]), out[-1]


def _copies_wait(state, after, name):
    kind, n, (send_sems, recv_sems, *thru) = state
    after = list(after) if isinstance(after, (list, tuple)) else [after]

    def body(*refs):
        for cp in _peer_copies(kind, refs[:n], refs[n:2 * n], refs[2 * n], refs[2 * n + 1]):
            cp.wait_send()
            cp.wait_recv()

    out = pl.pallas_call(
        body, name=name,
        out_shape=tuple(pltpu.HBM(t.shape, t.dtype) for t in thru),
        in_specs=[HBM] * (2 * n) + [SEM, SEM] + [ANY] * len(after), out_specs=tuple([HBM] * (2 * n)),
        input_output_aliases={i: i for i in range(2 * n)},
        compiler_params=pltpu.CompilerParams(has_side_effects=DATAFLOW),
    )(*thru, send_sems, recv_sems, *after)
    return out[:n], out[n:]


def _allgather_small(vec, name):
    r, w = vec.shape

    def body(x_ref, out_ref, send_sems, recv_sems):
        me = _me()
        out_ref[_lin(me)] = x_ref[...]
        copies = []
        for k in range(1, N_DEV):
            rc = pltpu.make_async_remote_copy(src_ref=x_ref, dst_ref=out_ref.at[_lin(me)],
                                              send_sem=send_sems.at[k - 1], recv_sem=recv_sems.at[k - 1],
                                              device_id=_flip(me, k), device_id_type=MESH)
            rc.start()
            copies.append(rc)
        for rc in copies:
            rc.wait()

    return pl.pallas_call(
        body, name=name,
        in_specs=[pl.BlockSpec(memory_space=pltpu.VMEM)], out_specs=pl.BlockSpec(memory_space=pltpu.VMEM),
        out_shape=jax.ShapeDtypeStruct((N_DEV, r, w), vec.dtype),
        scratch_shapes=[pltpu.SemaphoreType.DMA((7,)), pltpu.SemaphoreType.DMA((7,))],
    )(vec)


def _adamw(w, g, m, v):
    m2 = ADAM_B1 * m + (1.0 - ADAM_B1) * g
    v2 = ADAM_B2 * v + (1.0 - ADAM_B2) * (g * g)
    m_hat = m2 / (1.0 - ADAM_B1 ** ADAM_STEP)
    v_hat = v2 / (1.0 - ADAM_B2 ** ADAM_STEP)
    delta = -ADAM_LR * (m_hat / (jnp.sqrt(v_hat) + ADAM_EPS) + ADAM_WD * w)
    return delta, m2, v2


def _sum_adamw(land, part, me_idx, w, m, v, name, tile=256):
    r, c = w.shape
    tr = min(tile, r)

    def body(me_ref, own_ref, *refs):
        slots = refs[:N_DEV]
        w_ref, m_ref, v_ref, g_ref, d_ref, m2_ref, v2_ref = refs[N_DEV:]
        own = own_ref[...].astype(F32)
        g = None
        for s in range(N_DEV):
            term = jnp.where(me_ref[0] == s, own, slots[s][...].astype(F32))
            g = term if g is None else g + term
        d, m2, v2 = _adamw(w_ref[...], g, m_ref[...], v_ref[...])
        g_ref[...] = g
        d_ref[...] = d
        m2_ref[...] = m2
        v2_ref[...] = v2

    def slot_spec(s):
        return pl.BlockSpec((None, tr, c), lambda i, me: (jnp.where(me[0] == s, (s + 1) % N_DEV, s), i, 0))

    row = pl.BlockSpec((tr, c), lambda i, me: (i, 0))
    return pl.pallas_call(
        body, name=name,
        grid_spec=pltpu.PrefetchScalarGridSpec(
            num_scalar_prefetch=1, grid=(r // tr,),
            in_specs=[pl.BlockSpec((None, tr, c), lambda i, me: (me[0], i, 0))] + [slot_spec(s) for s in range(N_DEV)] + [row] * 3,
            out_specs=[row] * 4),
        out_shape=[jax.ShapeDtypeStruct((r, c), F32)] * 4,
        compiler_params=_cparams("parallel"),
    )(me_idx, part, *[land] * N_DEV, w, m, v)


def _sum_slots(gathered, name):
    _, r, w = gathered.shape

    def body(p_ref, o_ref):
        g = p_ref[0]
        for s in range(1, N_DEV):
            g = g + p_ref[s]
        o_ref[...] = g

    return pl.pallas_call(body, name=name, out_shape=jax.ShapeDtypeStruct((r, w), F32))(gathered)


def _pack(arrs):
    flat = jnp.concatenate([a.reshape(-1).astype(F32) for a in arrs])
    rows = -(-flat.shape[0] // (8 * LANES)) * 8
    return jnp.pad(flat, (0, rows * LANES - flat.shape[0])).reshape(rows, LANES)


def _unpack(packed, shapes):
    flat = packed.reshape(-1)
    out, off = [], 0
    for s in shapes:
        size = 1
        for d in s:
            size *= d
        out.append(flat[off:off + size].reshape(s))
        off += size
    return out


def _to_hm(a, d):
    t = a.shape[0]
    return a.reshape(t, HEADS, d).transpose(1, 0, 2)


def _from_hm(a):
    h, t, d = a.shape
    return a.transpose(1, 0, 2).reshape(t, h * d)


def _gate_rows(g):
    t = g.shape[0]
    return g.T.reshape(HEADS, t // CHUNK, 1, CHUNK)


def _gate_cols(g):
    h, nc, _, c = g.shape
    return g.reshape(h, nc * c).T


def _blockdiag_dense(w):
    n = w.shape[0] * QKV_BLOCK
    tiled = jnp.tile(w.reshape(n, QKV_BLOCK), (1, n // QKV_BLOCK))
    r = lax.broadcasted_iota(jnp.int32, (n, n), 0)
    c = lax.broadcasted_iota(jnp.int32, (n, n), 1)
    return jnp.where(r // QKV_BLOCK == c // QKV_BLOCK, tiled, 0.0)


def _blockdiag_blocks(dense):
    n = dense[0].shape[0]
    k = len(dense)

    def body(*refs):
        r = lax.broadcasted_iota(jnp.int32, (n, n), 0)
        c = lax.broadcasted_iota(jnp.int32, (n, n), 1)
        fr = lax.broadcasted_iota(jnp.int32, (n, LANES), 0)
        fc = lax.broadcasted_iota(jnp.int32, (n, LANES), 1)
        fold = ((fr & (QKV_BLOCK - 1)) == fc).astype(BF16)
        for i in range(k):
            kept = jnp.where((r >> 2) == (c >> 2), refs[i][...], 0.0)
            refs[k + i][...] = sum(lax.dot_general(t, fold, _DN["nn"], preferred_element_type=F32) for t in _split3(kept))

    out = pl.pallas_call(body, name="blockdiag_blocks", out_shape=[jax.ShapeDtypeStruct((n, LANES), F32)] * k)(*dense)
    return [o[:, 0:QKV_BLOCK].reshape(n // QKV_BLOCK, QKV_BLOCK, QKV_BLOCK) for o in out]


def _col_blocks(w):
    k, n = w.shape
    return w.reshape(k, N_DEV, n // N_DEV).transpose(1, 0, 2)


def _from_col_blocks(g):
    d, k, n = g.shape
    return g.transpose(1, 0, 2).reshape(k, d * n)


def _local_step(x, tgt, weight, ws, prefetch=None, on_grads=None):
    t, d = x.shape
    g1 = ws["g_pre_mix"]
    prefetch = prefetch or (lambda names, after: None)
    on_grads = on_grads or (lambda grads: None)

    def dep(token):
        return () if token is None else (token,)

    (h,) = _rowwise("pre_mix_norm", lambda xv, g: ((_rms(xv, g),), ()), [x], [g1], [(d, BF16)])
    fetch_mix = prefetch(("w_pa", "w_pb", "w_o"), h)
    proj = _mm(h, weight("w_in", h), "nn", F32, "proj_in", tm=512)
    fetch_up = prefetch(("w_up",), proj)
    offs = [0]
    for s in IN_SPLITS:
        offs.append(offs[-1] + s)
    q_a, k_a, v_a, g_a, a_low, x_m, o_pre, gate_a, gate_b = [proj[:, offs[i]:offs[i + 1]] for i in range(9)]

    a_low_p = jnp.pad(a_low, ((0, 0), (0, LANES - LOWRANK)))
    w_a_up_p = jnp.pad(ws["w_a_up"], ((0, LANES - LOWRANK), (0, 0)))
    b_a_up = ws["b_a_up"]
    (la,) = _rowwise("gla_decay", lambda al, w, b: ((_log_decay(al, w, b),), ()), [a_low_p], [w_a_up_p, b_a_up],
                     [(HEADS * GLA_DK, F32)], deps=dep(fetch_mix) + dep(fetch_up))
    q_hm, k_hm, la_hm = _to_hm(q_a, GLA_DK), _to_hm(k_a, GLA_DK), _to_hm(la, GLA_DK)
    o_gla, s_prev = _gla_fwd(q_hm, k_hm, v_a, la_hm)
    fetch_down = prefetch(("w_down",), o_gla)
    gn = ws["g_gla_norm"]
    (ya_in,) = _rowwise("gla_out", lambda o, g, n_: ((jnp.concatenate(_per_head(_gla_out, [o, g], [], [n_]), axis=1),), ()),
                        [o_gla, g_a], [gn], [(HEADS * HEAD_W, BF16)], deps=dep(fetch_down))
    y_a = _mm(ya_in, weight("w_pa", ya_in), "nn", F32, "proj_a")

    cw = ws["conv_w"]
    w_if_p = jnp.pad(ws["w_if"], ((0, 0), (0, LANES - 2 * HEADS)))
    ml_w = HEADS * HEAD_W
    pre_params = [cw[0:1], cw[1:2], cw[2:3], cw[3:4], ws["conv_b"],
                  _blockdiag_dense(ws["w_q_ml"]), _blockdiag_dense(ws["w_k_ml"]), _blockdiag_dense(ws["w_v_ml"]),
                  w_if_p[0:ml_w], w_if_p[ml_w:2 * ml_w], w_if_p[2 * ml_w:3 * ml_w],
                  jnp.pad(ws["b_if"], ((0, 0), (0, LANES - 2 * HEADS)))]
    x_pad = jnp.pad(x_m, ((HALO, 0), (0, 0)))
    xc, q_m, k_m, v_m, gl = _ml_pre_fwd(x_m, x_pad, pre_params)
    li, lf = _gate_rows(gl[:, 0:HEADS]), _gate_rows(gl[:, HEADS:2 * HEADS])
    hc, c_prev, n_prev, m_prev = _ml_fwd(q_m, k_m, v_m, li, lf)
    g_ml, skip = ws["g_ml_norm"], ws["ml_skip"]
    (h_b,) = _rowwise("mlstm_out", lambda a, b, c_, g, s: ((jnp.concatenate(_per_head(_ml_out, [a, b, c_], [g, s]), axis=1),), ()),
                      [hc, o_pre, xc], [g_ml, skip], [(ml_w, BF16)])
    y_b = _mm(h_b, weight("w_pb", h_b), "nn", F32, "proj_b")

    (merged,) = _rowwise("merge", lambda ga, gb, ya, yb: ((_merge(ga, gb, ya, yb),), ()), [gate_a, gate_b, y_a, y_b], [],
                         [(d, BF16)])
    z = _mm(merged, weight("w_o", merged), "nn", F32, "proj_o")
    gpm, gpl, gpo = ws["g_post_mix"], ws["g_pre_mlp"], ws["g_post_mlp"]
    x1, h2 = _rowwise("post_mix", lambda xv, zv, a, b: (_post_mix(xv, zv, a, b), ()), [x, z], [gpm, gpl], [(d, F32), (d, BF16)])
    up, u = _mm(h2, weight("w_up", h2), "nn", (F32, BF16), "mlp_up", epilogue=lambda p: (p, jnp.square(jnp.maximum(p, 0.0))))
    dn = _mm(u, weight("w_down", u), "nn", F32, "mlp_down", tm=512)

    def loss_and_grads(x1v, dnv, tgtv, g):
        loss, vjp = jax.vjp(lambda a, b, c_: _loss_rows(a, b, tgtv, c_), x1v, dnv, g)
        dx1, ddn, dg = vjp(jnp.ones((1, 1), F32))
        return (dx1, ddn), (jnp.broadcast_to(loss, (1, LANES)), dg)

    dx1_y, d_dn, loss, d_gpo = _rowwise("loss", loss_and_grads, [x1, dn, tgt], [gpo], [(d, F32), (d, BF16)],
                                        [((1, LANES), F32), ((1, d), F32)])

    (d_up,) = _mm(d_dn, weight("w_down", u), "nt", (BF16,), "mlp_down_dx", extra=[up],
                  epilogue=lambda p, a: (p * (2.0 * jnp.maximum(a, 0.0)),))
    dw_down = _mm(u, d_dn, "tn", BF16, "mlp_down_dw", tm=512)
    d_h2 = _mm(d_up, weight("w_up", h2), "nt", F32, "mlp_up_dx", tm=512)
    dw_up = _mm(h2, d_up, "tn", BF16, "mlp_up_dw")
    sent_mlp = on_grads(dict(w_down=dw_down, w_up=dw_up))

    def post_mix_bwd(xv, zv, dx1, dh2, a, b):
        _, vjp = jax.vjp(_post_mix, xv, zv, a, b)
        dx, dz, da, db = vjp((dx1, dh2))
        return (dx, dz), (da, db)

    dx_res, d_z, d_gpm, d_gpl = _rowwise("post_mix_bwd", post_mix_bwd, [x, z, dx1_y, d_h2], [gpm, gpl],
                                         [(d, F32), (d, BF16)], [((1, d), F32), ((1, d), F32)], deps=dep(sent_mlp))
    d_merged = _mm(d_z, weight("w_o", merged), "nt", F32, "proj_o_dx")
    dw_o = _mm(merged, d_z, "tn", BF16, "proj_o_dw")
    d_ga, d_gb, d_ya, d_yb = _rowwise("merge_bwd", lambda *v: (jax.vjp(_merge, *v[:4])[1](v[4]), ()),
                                      [gate_a, gate_b, y_a, y_b, d_merged], [], [(d, F32), (d, F32), (d, BF16), (d, BF16)])
    d_ya_in = _mm(d_ya, weight("w_pa", ya_in), "nt", F32, "proj_a_dx")
    dw_pa = _mm(ya_in, d_ya, "tn", BF16, "proj_a_dw")
    d_hb = _mm(d_yb, weight("w_pb", h_b), "nt", F32, "proj_b_dx")
    dw_pb = _mm(h_b, d_yb, "tn", BF16, "proj_b_dw")
    sent_mix = on_grads(dict(w_o=dw_o, w_pa=dw_pa, w_pb=dw_pb))

    def ml_out_bwd(a, b, c_, ct, g, s):
        parts = []
        for hs in _head_slices(HEAD_W):
            _, vjp = jax.vjp(_ml_out, a[:, hs], b[:, hs], c_[:, hs], g[:, hs], s[:, hs])
            parts.append(vjp(ct[:, hs]))
        cat = lambda i: jnp.concatenate([p[i] for p in parts], axis=1)
        return (cat(0), cat(1), cat(2)), (cat(3), cat(4))

    d_hc, d_opre, d_xc, d_gml, d_skip = _rowwise("mlstm_out_bwd", ml_out_bwd, [hc, o_pre, xc, d_hb], [g_ml, skip],
                                                 [(ml_w, F32)] * 3, [((1, ml_w), F32)] * 2, deps=dep(sent_mix))
    d_qm, d_km, d_vm, d_li, d_lf = _ml_bwd(q_m, k_m, v_m, li, lf, c_prev, n_prev, m_prev, d_hc)
    d_gl = jnp.concatenate([_gate_cols(d_li), _gate_cols(d_lf), jnp.zeros((t, LANES - 2 * HEADS), F32)], axis=1)
    pre_grads = _ml_pre_bwd(x_m, x_pad, pre_params, [d_xc, d_qm, d_km, d_vm, d_gl])
    d_xm = pre_grads[0]
    d_cw = jnp.concatenate(pre_grads[1:5], axis=0)
    d_cb = pre_grads[5]
    d_wq, d_wk, d_wv = _blockdiag_blocks(pre_grads[6:9])
    d_wif = jnp.concatenate(pre_grads[9:12], axis=0)[:, 0:2 * HEADS]
    d_bif = pre_grads[12][:, 0:2 * HEADS]

    def gla_out_bwd(o, g, ct, n_):
        parts = []
        for hs in _head_slices(HEAD_W):
            _, vjp = jax.vjp(_gla_out, o[:, hs], g[:, hs], n_)
            parts.append(vjp(ct[:, hs]))
        cat = lambda i: jnp.concatenate([p[i] for p in parts], axis=1)
        return (cat(0), cat(1)), (sum(p[2] for p in parts),)

    d_o, d_g_a, d_gn = _rowwise("gla_out_bwd", gla_out_bwd, [o_gla, g_a, d_ya_in], [gn], [(ml_w, F32)] * 2, [((1, HEAD_W), F32)])
    dq_hm, dk_hm, d_va, dla_hm = _gla_bwd(q_hm, k_hm, v_a, la_hm, s_prev, d_o)

    def decay_bwd(al, ct, w, b):
        _, vjp = jax.vjp(_log_decay, al, w, b)
        dal, dw, db = vjp(ct)
        return (dal,), (dw, db)

    d_alow_p, d_wa_p, d_ba = _rowwise("gla_decay_bwd", decay_bwd, [a_low_p, _from_hm(dla_hm)], [w_a_up_p, b_a_up],
                                      [(LANES, F32)], [(w_a_up_p.shape, F32), (b_a_up.shape, F32)])
    d_proj = jnp.concatenate([_from_hm(dq_hm), _from_hm(dk_hm), d_va, d_g_a, d_alow_p[:, 0:LOWRANK], d_xm, d_opre, d_ga, d_gb],
                             axis=1).astype(BF16)
    d_h = _mm(d_proj, weight("w_in", h), "nt", F32, "proj_in_dx", tm=512)
    dw_in = _mm(h, d_proj, "tn", F32, "proj_in_dw", tm=512, tk=1024)
    sent_in = on_grads(dict(w_in=dw_in))

    def pre_mix_bwd(xv, dh, dres, g):
        _, vjp = jax.vjp(_rms, xv, g)
        dx, dg = vjp(dh)
        return (dx + dres,), (dg,)

    grad_x, d_g1 = _rowwise("pre_mix_norm_bwd", pre_mix_bwd, [x, d_h, dx_res], [g1], [(d, F32)], [((1, d), F32)],
                            deps=dep(sent_in))

    small = dict(g_pre_mix=d_g1, w_a_up=d_wa_p[0:LOWRANK], b_a_up=d_ba, g_gla_norm=d_gn, conv_w=d_cw, conv_b=d_cb,
                 w_q_ml=d_wq, w_k_ml=d_wk, w_v_ml=d_wv, w_if=d_wif, b_if=d_bif, ml_skip=d_skip, g_ml_norm=d_gml,
                 g_post_mix=d_gpm, g_pre_mlp=d_gpl, g_post_mlp=d_gpo)
    return loss[:, 0:1], grad_x, small


BIG = ("w_in", "w_pa", "w_pb", "w_o", "w_up", "w_down")
BIG_COL_SHARDED = ("w_in", "w_pa", "w_pb", "w_up")
SMALL_SHARDED = {"w_a_up": 1, "conv_w": 1, "w_if": 0}
SMALL = ("g_pre_mix", "w_a_up", "b_a_up", "g_gla_norm", "conv_w", "conv_b", "w_q_ml", "w_k_ml", "w_v_ml", "w_if", "b_if",
         "ml_skip", "g_ml_norm", "g_post_mix", "g_pre_mlp", "g_post_mlp")
WEIGHTS = ("g_pre_mix", "w_in", "w_a_up", "b_a_up", "g_gla_norm", "conv_w", "conv_b", "w_q_ml", "w_k_ml", "w_v_ml", "w_if", "b_if",
           "ml_skip", "g_ml_norm", "w_pa", "w_pb", "w_o", "g_post_mix", "g_pre_mlp", "w_up", "w_down", "g_post_mlp")


def _my_slice(a, axis):
    n = a.shape[axis] // N_DEV
    return lax.dynamic_slice_in_dim(a, _lin(_me()) * n, n, axis)


def kernel(x, g_pre_mix, w_in, w_a_up, b_a_up, g_gla_norm, conv_w, conv_b, w_q_ml, w_k_ml, w_v_ml, w_if, b_if, ml_skip, g_ml_norm, w_pa, w_pb, w_o, g_post_mix, g_pre_mlp, w_up, w_down, g_post_mlp, loss_target, m_g_pre_mix, m_w_in, m_w_a_up, m_b_a_up, m_g_gla_norm, m_conv_w, m_conv_b, m_w_q_ml, m_w_k_ml, m_w_v_ml, m_w_if, m_b_if, m_ml_skip, m_g_ml_norm, m_w_pa, m_w_pb, m_w_o, m_g_post_mix, m_g_pre_mlp, m_w_up, m_w_down, m_g_post_mlp, v_g_pre_mix, v_w_in, v_w_a_up, v_b_a_up, v_g_gla_norm, v_conv_w, v_conv_b, v_w_q_ml, v_w_k_ml, v_w_v_ml, v_w_if, v_b_if, v_ml_skip, v_g_ml_norm, v_w_pa, v_w_pb, v_w_o, v_g_post_mix, v_g_pre_mlp, v_w_up, v_w_down, v_g_post_mlp):
    args = dict(locals())
    w = {n: args[n][0] for n in WEIGHTS}
    m = {n: args["m_" + n][0] for n in WEIGHTS}
    v = {n: args["v_" + n][0] for n in WEIGHTS}

    me_lin = _lin(_me())
    me_idx = jnp.reshape(me_lin, (1,)).astype(jnp.int32)

    def full_weight(n, g):
        return _from_col_blocks(g) if n in BIG_COL_SHARDED else g.reshape(-1, g.shape[-1])

    def grad_parts(n, g):
        return (_col_blocks(g) if n in BIG_COL_SHARDED else g.reshape(N_DEV, -1, g.shape[-1])).astype(BF16)

    ready = {"w_in": full_weight("w_in", _allgather_big([w["w_in"].astype(BF16)], "allgather_w_in")[0])}
    pending = {}

    def prefetch(group, after):
        state, token = _copies_start("gather", [w[n].astype(BF16) for n in group], "allgather_start_" + group[0], after)
        for n in group:
            pending[n] = (group, state)
        return token

    def weight(n, after):
        if n not in ready:
            group, state = pending[n]
            shards, lands = _copies_wait(state, after, "allgather_wait_" + group[0])
            for gn, shard, land in zip(group, shards, lands):
                ready[gn] = full_weight(gn, lax.dynamic_update_slice(land, shard[None], (me_lin, 0, 0)))
        return ready[n]

    sharded_names = tuple(SMALL_SHARDED)
    small_g = _allgather_small(_pack([w[n] for n in sharded_names]), "allgather_small_weights")
    ws = {n: (w[n].reshape(1, -1) if w[n].ndim == 1 else w[n]) for n in SMALL if n not in SMALL_SHARDED}
    per_dev = [_unpack(small_g[dev], [w[n].shape for n in sharded_names]) for dev in range(N_DEV)]
    for i, n in enumerate(sharded_names):
        ws[n] = jnp.concatenate([per_dev[dev][i] for dev in range(N_DEV)], axis=SMALL_SHARDED[n])

    sent = []

    def on_grads(grads):
        names = tuple(grads)
        state, token = _copies_start("exchange", [grad_parts(n, grads[n]) for n in names], "exchange_start_" + names[0])
        sent.append((names, state))
        return token

    loss, grad_x, small = _local_step(x[0], loss_target[0], weight, ws, prefetch, on_grads)

    out = {}

    def finish(names, state, after):
        parts, lands = _copies_wait(state, after, "exchange_wait_" + names[0])
        for n, part, land in zip(names, parts, lands):
            out[n] = _sum_adamw(land, part, me_idx, w[n], m[n], v[n], "adamw_" + n)

    for names, state in sent[:-1]:
        finish(names, state, [grad_x])

    small_shapes = [small[n].shape for n in SMALL]
    vec = _sum_slots(_allgather_small(_pack([small[n] for n in SMALL] + [loss]), "allgather_small_grads"), "sum_small_grads")
    summed = _unpack(vec, small_shapes + [(1, 1)])
    g_small = {}
    for n, g in zip(SMALL, summed[:-1]):
        g_small[n] = _my_slice(g, SMALL_SHARDED[n]) if n in SMALL_SHARDED else g
    shapes = [g_small[n].shape for n in SMALL]
    packed = [_pack([d[n].reshape(g_small[n].shape) for n in SMALL]) for d in (w, g_small, m, v)]
    upd = _rowwise("adamw_small", lambda a, b, c_, d_: (_adamw(a, b, c_, d_), ()), packed, [], [(LANES, F32)] * 3, tile=8 * 1024)
    deltas, new_ms, new_vs = [_unpack(p, shapes) for p in upd]
    for i, n in enumerate(SMALL):
        out[n] = (g_small[n], deltas[i], new_ms[i], new_vs[i])
    finish(*sent[-1], [upd[0]] + [out[n][1] for n in BIG if n in out])

    shaped = lambda a, n: a.reshape(args[n].shape)
    return (summed[-1].reshape(()), grad_x[None],
            *[shaped(out[n][0], n) for n in WEIGHTS], *[shaped(out[n][1], n) for n in WEIGHTS],
            *[shaped(out[n][2], n) for n in WEIGHTS], *[shaped(out[n][3], n) for n in WEIGHTS])
```

```python
import functools

import jax
import jax.numpy as jnp
from jax import lax
from jax.experimental import pallas as pl
from jax.experimental.pallas import tpu as pltpu

F32 = jnp.float32
BF16 = jnp.bfloat16
MESH = pl.DeviceIdType.MESH

N_DEV = 8
EPS = 1e-6
CHUNK = 64
HEADS = 4
GLA_DK = 64
HEAD_W = 128
GLA_GATE_NORM = 16.0
LOWRANK = 16
CONV_K = 4
QKV_BLOCK = 4
LANES = 128
HALO = 8
IN_SPLITS = (256, 256, 512, 512, 16, 512, 512, 1024, 1024)

ADAM_LR = 0.001
ADAM_B1 = 0.9
ADAM_B2 = 0.999
ADAM_EPS = 1e-08
ADAM_WD = 0.01
ADAM_STEP = 10

VMEM_LIMIT = 56 * 1024 * 1024


def _cparams(*sem):
    return pltpu.CompilerParams(dimension_semantics=sem, vmem_limit_bytes=VMEM_LIMIT)


_DN = {"nn": (((1,), (0,)), ((), ())), "nt": (((1,), (1,)), ((), ())), "tn": (((0,), (0,)), ((), ()))}


def _raw_dot(a, b, mode):
    return lax.dot_general(a.astype(BF16), b.astype(BF16), _DN[mode], preferred_element_type=F32)


@functools.partial(jax.custom_vjp, nondiff_argnums=(2,))
def _bdot(a, b, mode):
    return _raw_dot(a, b, mode)


def _bdot_fwd(a, b, mode):
    return _raw_dot(a, b, mode), (a, b)


def _bdot_bwd(mode, res, ct):
    a, b = res
    if mode == "nn":
        da, db = _raw_dot(ct, b, "nt"), _raw_dot(a, ct, "tn")
    elif mode == "nt":
        da, db = _raw_dot(ct, b, "nn"), _raw_dot(ct, a, "tn")
    else:
        da, db = _raw_dot(b, ct, "nt"), _raw_dot(a, ct, "nn")
    return da.astype(a.dtype), db.astype(b.dtype)


_bdot.defvjp(_bdot_fwd, _bdot_bwd)


def _split3(x):
    hi = x.astype(BF16)
    r1 = x - hi.astype(F32)
    mid = r1.astype(BF16)
    return hi, mid, (r1 - mid.astype(F32)).astype(BF16)


def _split_dot(tri, x):
    return sum(lax.dot_general(tri, t, _DN["nn"], preferred_element_type=F32) for t in _split3(x))


def _tri(n, lower):
    r = lax.broadcasted_iota(jnp.int32, (n, n), 0)
    c = lax.broadcasted_iota(jnp.int32, (n, n), 1)
    return ((c <= r) if lower else (c >= r)).astype(BF16)


@jax.custom_vjp
def _cumsum_rows(x):
    return _split_dot(_tri(x.shape[0], True), x)


def _cumsum_rows_fwd(x):
    return _cumsum_rows(x), None


def _cumsum_rows_bwd(_, ct):
    return (_split_dot(_tri(ct.shape[0], False), ct),)


_cumsum_rows.defvjp(_cumsum_rows_fwd, _cumsum_rows_bwd)


def _abs(x):
    return jnp.where(x >= 0, x, -x)


def _sigmoid(x):
    return lax.logistic(x)


def _log_sigmoid(x):
    return jnp.minimum(x, 0.0) - jnp.log(1.0 + jnp.exp(-_abs(x)))


def _rms(x, g):
    return x * lax.rsqrt(jnp.mean(x * x, axis=-1, keepdims=True) + EPS) * g


def _head_slices(w):
    return [slice(h * w, (h + 1) * w) for h in range(HEADS)]


def _tile(dim, want):
    if dim <= want or dim % LANES:
        return dim
    t = want
    while dim % t:
        t -= LANES
    return t


def _mm(a, b, mode, out_dtype, name, tm=1024, tn=1024, tk=4096, epilogue=None, extra=(), deps=()):
    if mode == "nn":
        (m, k), (k2, n) = a.shape, b.shape
    elif mode == "nt":
        (m, k), (n, k2) = a.shape, b.shape
    else:
        (k, m), (k2, n) = a.shape, b.shape
    assert k == k2, (name, a.shape, b.shape)
    tm, tn, tk = _tile(m, tm), _tile(n, tn), _tile(k, tk)
    nk = k // tk
    out_dtypes = out_dtype if epilogue else (out_dtype,)
    assert nk == 1 or (out_dtype == F32 and not epilogue), name
    n_in = 2 + len(extra)

    def body(*refs):
        p = _raw_dot(refs[0][...], refs[1][...], mode)
        if nk > 1:
            _accumulate(pl.program_id(2), [refs[n_in + len(deps)]], [p])
            return
        outs = epilogue(p, *[r[...] for r in refs[2:n_in]]) if epilogue else (p,)
        for ref, val in zip(refs[n_in + len(deps):], outs):
            ref[...] = val.astype(ref.dtype)

    a_spec = pl.BlockSpec((tk, tm), lambda i, j, kk: (kk, i)) if mode == "tn" else pl.BlockSpec((tm, tk), lambda i, j, kk: (i, kk))
    b_spec = pl.BlockSpec((tn, tk), lambda i, j, kk: (j, kk)) if mode == "nt" else pl.BlockSpec((tk, tn), lambda i, j, kk: (kk, j))
    o_spec = pl.BlockSpec((tm, tn), lambda i, j, kk: (i, j))
    res = pl.pallas_call(
        body, name=name, grid=(m // tm, n // tn, nk),
        in_specs=[a_spec, b_spec] + [o_spec] * len(extra) + [ANY] * len(deps), out_specs=[o_spec] * len(out_dtypes),
        out_shape=[jax.ShapeDtypeStruct((m, n), dt) for dt in out_dtypes],
        compiler_params=_cparams("parallel", "parallel", "arbitrary"),
    )(a, b, *extra, *deps)
    return res if epilogue else res[0]


def _rowwise(name, fn, rows, params, out_rows, out_accs=(), tile=256, deps=()):
    t = rows[0].shape[0]
    r = min(tile, t)
    assert t % r == 0
    n_in, n_or = len(rows) + len(params), len(out_rows)
    n_all = n_in + len(deps)
    params = list(params) + list(deps)

    def body(*refs):
        vals = [ref[...] for ref in refs[:n_in]]
        outs = refs[n_all:]
        ro, ao = fn(*vals)
        for ref, v in zip(outs[:n_or], ro):
            ref[...] = v.astype(ref.dtype)
        if out_accs:
            _accumulate(pl.program_id(0), outs[n_or:], ao)

    def full(shape):
        return pl.BlockSpec(shape, lambda i, nd=len(shape): (0,) * nd)

    return pl.pallas_call(
        body, name=name, grid=(t // r,),
        in_specs=[pl.BlockSpec((r, a.shape[1]), lambda i: (i, 0)) for a in rows] + [full(p.shape) for p in params],
        out_specs=[pl.BlockSpec((r, w), lambda i: (i, 0)) for w, _ in out_rows] + [full(s) for s, _ in out_accs],
        out_shape=[jax.ShapeDtypeStruct((t, w), dt) for w, dt in out_rows] + [jax.ShapeDtypeStruct(s, dt) for s, dt in out_accs],
        compiler_params=_cparams("arbitrary"),
    )(*rows, *params)


def _accumulate(step, refs, vals):
    for ref, v in zip(refs, vals):
        @pl.when(step == 0)
        def _(ref=ref, v=v):
            ref[...] = v.astype(ref.dtype)

        @pl.when(step > 0)
        def _(ref=ref, v=v):
            ref[...] += v.astype(ref.dtype)


def _gla_chunk(q, k, v, la, st):
    c = q.shape[0]
    row = lax.broadcasted_iota(jnp.int32, (c, c), 0)
    col = lax.broadcasted_iota(jnp.int32, (c, c), 1)
    cum = _cumsum_rows(la)
    cl = jnp.sum(la, axis=0, keepdims=True)
    ep = jnp.exp(cum)
    en = jnp.exp(-cum)
    qs = q * (GLA_DK ** -0.5)
    qp = qs * ep
    a_f = _bdot(qp, k * en, "nt")
    a_b = _bdot(qs * en, k * ep, "nt")
    sc = jnp.where(row >= col, a_f, a_b)
    o = _bdot(sc, v, "nn") + _bdot(qp, st, "nt")
    kd = k * jnp.exp(cl - cum)
    st_new = st * jnp.exp(cl) + _bdot(v, kd, "tn")
    return o, st_new


def _gla_specs(nc, rev):
    def ch(n):
        return (nc - 1 - n) if rev else n
    hm = pl.BlockSpec((HEADS, CHUNK, GLA_DK), lambda n: (0, ch(n), 0))
    tm = pl.BlockSpec((CHUNK, HEADS * HEAD_W), lambda n: (ch(n), 0))
    st = pl.BlockSpec((HEADS, None, HEAD_W, GLA_DK), lambda n: (0, ch(n), 0, 0))
    return hm, tm, st


def _gla_fwd(q, k, v, la):
    t = v.shape[0]
    nc = t // CHUNK
    hm, tm, st = _gla_specs(nc, False)

    def body(q_ref, k_ref, v_ref, la_ref, o_ref, sp_ref, st_ref):
        @pl.when(pl.program_id(0) == 0)
        def _():
            st_ref[...] = jnp.zeros_like(st_ref)

        for h, hs in enumerate(_head_slices(HEAD_W)):
            s = st_ref[h]
            sp_ref[h] = s
            o, s_new = _gla_chunk(q_ref[h], k_ref[h], v_ref[:, hs], la_ref[h], s)
            o_ref[:, hs] = o
            st_ref[h] = s_new

    return pl.pallas_call(
        body, name="gla_fwd", grid=(nc,),
        in_specs=[hm, hm, tm, hm], out_specs=[tm, st],
        out_shape=[jax.ShapeDtypeStruct((t, HEADS * HEAD_W), F32), jax.ShapeDtypeStruct((HEADS, nc, HEAD_W, GLA_DK), F32)],
        scratch_shapes=[pltpu.VMEM((HEADS, HEAD_W, GLA_DK), F32)],
        compiler_params=_cparams("arbitrary"),
    )(q, k, v, la)


def _gla_bwd(q, k, v, la, sp, do):
    t = v.shape[0]
    nc = t // CHUNK
    hm, tm, st = _gla_specs(nc, True)

    def body(q_ref, k_ref, v_ref, la_ref, sp_ref, do_ref, dq_ref, dk_ref, dv_ref, dla_ref, ds_ref):
        @pl.when(pl.program_id(0) == 0)
        def _():
            ds_ref[...] = jnp.zeros_like(ds_ref)

        for h, hs in enumerate(_head_slices(HEAD_W)):
            _, vjp = jax.vjp(_gla_chunk, q_ref[h], k_ref[h], v_ref[:, hs], la_ref[h], sp_ref[h])
            dq, dk, dv, dla, ds = vjp((do_ref[:, hs], ds_ref[h]))
            dq_ref[h] = dq
            dk_ref[h] = dk
            dv_ref[:, hs] = dv
            dla_ref[h] = dla
            ds_ref[h] = ds

    hm_shape = jax.ShapeDtypeStruct((HEADS, t, GLA_DK), F32)
    return pl.pallas_call(
        body, name="gla_bwd", grid=(nc,),
        in_specs=[hm, hm, tm, hm, st, tm], out_specs=[hm, hm, tm, hm],
        out_shape=[hm_shape, hm_shape, jax.ShapeDtypeStruct((t, HEADS * HEAD_W), F32), hm_shape],
        scratch_shapes=[pltpu.VMEM((HEADS, HEAD_W, GLA_DK), F32)],
        compiler_params=_cparams("arbitrary"),
    )(q, k, v, la, sp, do)


def _ml_chunk(q, k, v, li_r, lf_r, cm, nv, m):
    c = q.shape[0]
    row = lax.broadcasted_iota(jnp.int32, (c, c), 0)
    col = lax.broadcasted_iota(jnp.int32, (c, c), 1)
    eye = (row == col).astype(F32)
    li_c = jnp.sum(eye * li_r, axis=1, keepdims=True)
    lf_c = jnp.sum(eye * lf_r, axis=1, keepdims=True)
    fc_c = jnp.sum((col <= row).astype(F32) * lf_r, axis=1, keepdims=True)
    fc_r = jnp.sum((row <= col).astype(F32) * lf_c, axis=0, keepdims=True)
    f_last = jnp.sum(lf_r, axis=1, keepdims=True)
    kc = k * (HEAD_W ** -0.5)
    a_c = f_last - fc_c + li_c
    m_loc = jnp.max(a_c, axis=0, keepdims=True)
    kw = kc * jnp.exp(a_c - m_loc)
    c_chunk = _bdot(kw, v, "tn")
    n_chunk = jnp.sum(kw, axis=0, keepdims=True)
    m_new = jnp.maximum(f_last + m, m_loc)
    sp = jnp.exp(f_last + m - m_new)
    sl = jnp.exp(m_loc - m_new)
    cm_new = sp * cm + sl * c_chunk
    nv_new = sp * nv + sl * n_chunk
    log_d = li_r - _abs(fc_c - fc_r)
    g_inter = fc_c + m
    m_t = jnp.maximum(g_inter, jnp.max(log_d, axis=1, keepdims=True))
    s = _bdot(q, kc, "nt") * jnp.exp(log_d - m_t)
    sc = jnp.exp(g_inter - m_t)
    num = _bdot(s, v, "nn") + sc * _bdot(q, cm, "nn")
    den = jnp.sum(s, axis=1, keepdims=True) + sc * jnp.sum(q * nv, axis=1, keepdims=True)
    den = jnp.maximum(_abs(den), jnp.exp(-m_t))
    return num / den, cm_new, nv_new, m_new


def _ml_specs(nc, rev):
    def ch(n):
        return (nc - 1 - n) if rev else n
    tm = pl.BlockSpec((CHUNK, HEADS * HEAD_W), lambda n: (ch(n), 0))
    gate = pl.BlockSpec((HEADS, None, 1, CHUNK), lambda n: (0, ch(n), 0, 0))
    cm = pl.BlockSpec((HEADS, None, HEAD_W, HEAD_W), lambda n: (0, ch(n), 0, 0))
    vec = pl.BlockSpec((HEADS, None, 1, HEAD_W), lambda n: (0, ch(n), 0, 0))
    return tm, gate, cm, vec


_ML_STATE = [pltpu.VMEM((HEADS, HEAD_W, HEAD_W), F32), pltpu.VMEM((HEADS, 1, HEAD_W), F32), pltpu.VMEM((HEADS, 1, HEAD_W), F32)]


def _ml_fwd(q, k, v, li, lf):
    t = q.shape[0]
    nc = t // CHUNK
    tm, gate, cm, vec = _ml_specs(nc, False)

    def body(q_ref, k_ref, v_ref, li_ref, lf_ref, hc_ref, cp_ref, np_ref, mp_ref, c_ref, n_ref, m_ref):
        @pl.when(pl.program_id(0) == 0)
        def _():
            c_ref[...] = jnp.zeros_like(c_ref)
            n_ref[...] = jnp.zeros_like(n_ref)
            m_ref[...] = jnp.zeros_like(m_ref)

        for h, hs in enumerate(_head_slices(HEAD_W)):
            c0, n0, m0 = c_ref[h], n_ref[h], m_ref[h]
            cp_ref[h] = c0
            np_ref[h] = n0
            mp_ref[h] = m0
            hc, c1, n1, m1 = _ml_chunk(q_ref[:, hs], k_ref[:, hs], v_ref[:, hs], li_ref[h], lf_ref[h], c0, n0, m0[:, 0:1])
            hc_ref[:, hs] = hc
            c_ref[h] = c1
            n_ref[h] = n1
            m_ref[h] = jnp.broadcast_to(m1, (1, HEAD_W))

    return pl.pallas_call(
        body, name="mlstm_fwd", grid=(nc,),
        in_specs=[tm, tm, tm, gate, gate], out_specs=[tm, cm, vec, vec],
        out_shape=[jax.ShapeDtypeStruct((t, HEADS * HEAD_W), F32), jax.ShapeDtypeStruct((HEADS, nc, HEAD_W, HEAD_W), F32),
                   jax.ShapeDtypeStruct((HEADS, nc, 1, HEAD_W), F32), jax.ShapeDtypeStruct((HEADS, nc, 1, HEAD_W), F32)],
        scratch_shapes=_ML_STATE,
        compiler_params=_cparams("arbitrary"),
    )(q, k, v, li, lf)


def _ml_bwd(q, k, v, li, lf, cp, npv, mp, dhc):
    t = q.shape[0]
    nc = t // CHUNK
    tm, gate, cm, vec = _ml_specs(nc, True)

    def body(q_ref, k_ref, v_ref, li_ref, lf_ref, cp_ref, np_ref, mp_ref, dhc_ref,
             dq_ref, dk_ref, dv_ref, dli_ref, dlf_ref, dc_ref, dn_ref, dm_ref):
        @pl.when(pl.program_id(0) == 0)
        def _():
            dc_ref[...] = jnp.zeros_like(dc_ref)
            dn_ref[...] = jnp.zeros_like(dn_ref)
            dm_ref[...] = jnp.zeros_like(dm_ref)

        for h, hs in enumerate(_head_slices(HEAD_W)):
            _, vjp = jax.vjp(_ml_chunk, q_ref[:, hs], k_ref[:, hs], v_ref[:, hs], li_ref[h], lf_ref[h],
                             cp_ref[h], np_ref[h], mp_ref[h][:, 0:1])
            dq, dk, dv, dli, dlf, dc, dn, dm = vjp((dhc_ref[:, hs], dc_ref[h], dn_ref[h], dm_ref[h][:, 0:1]))
            dq_ref[:, hs] = dq
            dk_ref[:, hs] = dk
            dv_ref[:, hs] = dv
            dli_ref[h] = dli
            dlf_ref[h] = dlf
            dc_ref[h] = dc
            dn_ref[h] = dn
            dm_ref[h] = jnp.broadcast_to(dm, (1, HEAD_W))

    tm_shape = jax.ShapeDtypeStruct((t, HEADS * HEAD_W), F32)
    gate_shape = jax.ShapeDtypeStruct((HEADS, nc, 1, CHUNK), F32)
    return pl.pallas_call(
        body, name="mlstm_bwd", grid=(nc,),
        in_specs=[tm, tm, tm, gate, gate, cm, vec, vec, tm], out_specs=[tm, tm, tm, gate, gate],
        out_shape=[tm_shape, tm_shape, tm_shape, gate_shape, gate_shape],
        scratch_shapes=_ML_STATE,
        compiler_params=_cparams("arbitrary"),
    )(q, k, v, li, lf, cp, npv, mp, dhc)


def _ml_pre(s0, s1, s2, s3, cw0, cw1, cw2, cw3, cb, wq, wk, wv, wiq, wik, wiv, bif):
    pre = cb + cw0 * s0 + cw1 * s1 + cw2 * s2 + cw3 * s3
    xc = pre * _sigmoid(pre)
    q = _bdot(xc, wq, "nn")
    k = _bdot(xc, wk, "nn")
    v = _bdot(s3, wv, "nn")
    gates = _bdot(q, wiq, "nn") + _bdot(k, wik, "nn") + _bdot(v, wiv, "nn") + bif
    lane = lax.broadcasted_iota(jnp.int32, gates.shape, 1)
    gl = jnp.where(lane < HEADS, gates, _log_sigmoid(gates))
    return xc, q, k, v, gl


def _delayed(xs_ref, x_ref, halo_ref, r):
    xs_ref[0:HALO, :] = halo_ref[...]
    xs_ref[HALO:HALO + r, :] = x_ref[...]
    return [xs_ref[pl.ds(HALO - (CONV_K - 1) + j, r), :] for j in range(CONV_K)]


def _full_spec(shape):
    return pl.BlockSpec(shape, lambda i, nd=len(shape): (0,) * nd)


def _ml_pre_fwd(x_m, x_pad, params, tile=256):
    t, w = x_m.shape
    r = min(tile, t)

    def body(*refs):
        x_ref, halo_ref = refs[:2]
        p = [ref[...] for ref in refs[2:2 + len(params)]]
        outs = refs[2 + len(params):-1]
        res = _ml_pre(*_delayed(refs[-1], x_ref, halo_ref, r), *p)
        for ref, val in zip(outs, res):
            ref[...] = val

    row = pl.BlockSpec((r, w), lambda i: (i, 0))
    return pl.pallas_call(
        body, name="ml_pre_fwd", grid=(t // r,),
        in_specs=[row, pl.BlockSpec((HALO, w), lambda i: (i * (r // HALO), 0))] + [_full_spec(p.shape) for p in params],
        out_specs=[row] * 4 + [pl.BlockSpec((r, LANES), lambda i: (i, 0))],
        out_shape=[jax.ShapeDtypeStruct((t, w), F32)] * 4 + [jax.ShapeDtypeStruct((t, LANES), F32)],
        scratch_shapes=[pltpu.VMEM((r + HALO, w), F32)],
        compiler_params=_cparams("arbitrary"),
    )(x_m, x_pad, *params)


def _ml_pre_bwd(x_m, x_pad, params, cts, tile=256):
    t, w = x_m.shape
    r = min(tile, t)
    nt = t // r
    n_p = len(params)

    def body(*refs):
        x_ref, halo_ref = refs[:2]
        p = [ref[...] for ref in refs[2:2 + n_p]]
        ct = [ref[...] for ref in refs[2 + n_p:7 + n_p]]
        dx_ref = refs[7 + n_p]
        dp_refs = refs[8 + n_p:8 + 2 * n_p]
        xs_ref, ds_ref, carry_ref = refs[8 + 2 * n_p:]
        step = pl.program_id(0)

        @pl.when(step == 0)
        def _():
            ds_ref[...] = jnp.zeros_like(ds_ref)
            carry_ref[...] = jnp.zeros_like(carry_ref)

        _, vjp = jax.vjp(_ml_pre, *_delayed(xs_ref, x_ref, halo_ref, r), *p)
        grads = vjp(tuple(ct))
        for j in range(CONV_K):
            ds_ref[j, HALO:HALO + r, :] = grads[j]
        lead = HALO + CONV_K - 1
        d_tile = sum(ds_ref[j, pl.ds(lead - j, r), :] for j in range(CONV_K))
        d_halo = sum(ds_ref[j, pl.ds(CONV_K - 1 - j, HALO), :] for j in range(CONV_K))
        dx_ref[...] = d_tile
        dx_ref[r - HALO:r, :] += carry_ref[...]
        carry_ref[...] = d_halo
        _accumulate(step, dp_refs, grads[CONV_K:])

    row = pl.BlockSpec((r, w), lambda i: (nt - 1 - i, 0))
    return pl.pallas_call(
        body, name="ml_pre_bwd", grid=(nt,),
        in_specs=[row, pl.BlockSpec((HALO, w), lambda i: ((nt - 1 - i) * (r // HALO), 0))] + [_full_spec(p.shape) for p in params]
        + [row] * 4 + [pl.BlockSpec((r, LANES), lambda i: (nt - 1 - i, 0))],
        out_specs=[row] + [_full_spec(p.shape) for p in params],
        out_shape=[jax.ShapeDtypeStruct((t, w), F32)] + [jax.ShapeDtypeStruct(p.shape, F32) for p in params],
        scratch_shapes=[pltpu.VMEM((r + HALO, w), F32), pltpu.VMEM((CONV_K, r + 2 * HALO, w), F32), pltpu.VMEM((HALO, w), F32)],
        compiler_params=_cparams("arbitrary"),
    )(x_m, x_pad, *params, *cts)


def _per_head(fn, row_vals, head_params, shared_params=()):
    return [fn(*[a[:, hs] for a in row_vals], *[p[:, hs] for p in head_params], *shared_params) for hs in _head_slices(HEAD_W)]


def _gla_out(o, g, gn):
    return _rms(o, gn) * (g * _sigmoid(g))


def _ml_out(hc, op, xc, g, sk):
    hcell = hc * _sigmoid(op)
    mu = jnp.mean(hcell, axis=-1, keepdims=True)
    d = hcell - mu
    var = jnp.mean(d * d, axis=-1, keepdims=True)
    return d * lax.rsqrt(var + EPS) * g + sk * xc


def _log_decay(al, w, b):
    return _log_sigmoid(_bdot(al, w, "nn") + b) * (1.0 / GLA_GATE_NORM)


def _merge(ga, gb, ya, yb):
    return _sigmoid(ga) * ya + _sigmoid(gb) * yb


def _post_mix(x, z, gpm, gpl):
    x1 = x + _rms(z, gpm)
    return x1, _rms(x1, gpl)


def _loss_rows(x1, dn, tgt, g):
    e = x1 + _rms(dn, g) - tgt
    return 0.5 * jnp.sum(jnp.mean(e * e, axis=-1, keepdims=True), axis=0, keepdims=True)


def _lin(p):
    return 4 * p[0] + 2 * p[1] + p[2]


def _me():
    return lax.axis_index("x"), lax.axis_index("y"), lax.axis_index("c")


def _flip(p, k):
    return tuple((1 - v) if (k >> (2 - i)) & 1 else v for i, v in enumerate(p))


ANY = pl.BlockSpec(memory_space=pl.ANY)


def _allgather_big(shards, name, deps=()):
    n = len(shards)
    n_in = n + len(deps)

    def body(*refs):
        ins, outs = refs[:n], refs[n_in:n_in + n]
        send_sems, recv_sems, local_sems = refs[n_in + n:]
        me = _me()
        x, y, c = me
        sib = (x, y, 1 - c)
        chips = [(1 - x, y), (x, 1 - y), (1 - x, 1 - y)]

        def cp(a, k, block, to, src=None):
            dst = outs[a].at[_lin(block)]
            return pltpu.make_async_remote_copy(src_ref=dst if src is None else src, dst_ref=dst,
                                                send_sem=send_sems.at[a * 7 + k], recv_sem=recv_sems.at[a * 7 + k],
                                                device_id=to, device_id_type=MESH)

        mine = [pltpu.make_async_copy(ins[a], outs[a].at[_lin(me)], local_sems.at[a]) for a in range(n)]
        for m in mine:
            m.start()
        first = []
        for a in range(n):
            first.append(cp(a, 0, me, sib, src=ins[a]))
            first += [cp(a, 1 + j, me, (*chip, c), src=ins[a]) for j, chip in enumerate(chips)]
        for f in first:
            f.start()
        passed = []
        for j, chip in enumerate(chips):
            for a in range(n):
                cp(a, 1 + j, (*chip, c), me).wait_recv()
                fwd = cp(a, 4 + j, (*chip, c), sib)
                fwd.start()
                passed.append(fwd)
        for a in range(n):
            cp(a, 0, sib, me).wait_recv()
            for j, chip in enumerate(chips):
                cp(a, 4 + j, (*chip, 1 - c), me).wait_recv()
        for f in first + passed:
            f.wait_send()
        for m in mine:
            m.wait()

    return pl.pallas_call(
        body, name=name,
        in_specs=[ANY] * n_in, out_specs=[ANY] * n,
        out_shape=[jax.ShapeDtypeStruct((N_DEV, *s.shape), s.dtype) for s in shards],
        scratch_shapes=[pltpu.SemaphoreType.DMA((7 * n,)), pltpu.SemaphoreType.DMA((7 * n,)), pltpu.SemaphoreType.DMA((n,))],
    )(*shards, *deps)


HBM =pl.BlockSpec(memory_space=pltpu.HBM)
SEM = pl.BlockSpec(memory_space=pltpu.SEMAPHORE)
DATAFLOW = pltpu.SideEffectType.DATAFLOW_SIDE_EFFECTING


def _peer_copies(kind, srcs, lands, send_sems, recv_sems):
    me = _me()
    copies = []
    for a, (src, land) in enumerate(zip(srcs, lands)):
        for k in range(1, N_DEV):
            peer = _flip(me, k)
            copies.append(pltpu.make_async_remote_copy(
                src_ref=src if kind == "gather" else src.at[_lin(peer)], dst_ref=land.at[_lin(me)],
                send_sem=send_sems.at[a * 7 + k - 1], recv_sem=recv_sems.at[a * 7 + k - 1],
                device_id=peer, device_id_type=MESH))
    return copies


def _copies_start(kind, srcs, name, after=None):
    n = len(srcs)
    extra = [] if after is None else [after]
    land_shapes = [((N_DEV, *s.shape) if kind == "gather" else s.shape) for s in srcs]

    def body(*refs):
        sems = refs[2 * n + len(extra):]
        for cp in _peer_copies(kind, refs[:n], refs[n:2 * n], sems[0], sems[1]):
            cp.start()
        refs[-1][...] = jnp.zeros_like(refs[-1])

    def hbm(a):
        return pltpu.with_memory_space_constraint(a, pltpu.HBM)

    out = pl.pallas_call(
        body, name=name,
        out_shape=(pltpu.SemaphoreType.DMA((7 * n,)), pltpu.SemaphoreType.DMA((7 * n,)),
                   *[pltpu.HBM(s.shape, s.dtype) for s in srcs],
                   *[pltpu.HBM(ls, s.dtype) for ls, s in zip(land_shapes, srcs)],
                   jax.ShapeDtypeStruct((8, LANES), F32)),
        in_specs=[HBM] * (2 * n) + [ANY] * len(extra),
        out_specs=(SEM, SEM, *[HBM] * (2 * n), pl.BlockSpec(memory_space=pltpu.VMEM)),
        input_output_aliases={i: 2 + i for i in range(2 * n)},
        compiler_params=pltpu.CompilerParams(has_side_effects=DATAFLOW),
    )(*[hbm(s) for s in srcs], *[hbm(lax.empty(ls, s.dtype)) for ls, s in zip(land_shapes, srcs)], *extra)
    return (kind, n, out[:-1]), out[-1]


def _copies_wait(state, after, name):
    kind, n, (send_sems, recv_sems, *thru) = state
    after = list(after) if isinstance(after, (list, tuple)) else [after]

    def body(*refs):
        for cp in _peer_copies(kind, refs[:n], refs[n:2 * n], refs[2 * n], refs[2 * n + 1]):
            cp.wait_send()
            cp.wait_recv()

    out = pl.pallas_call(
        body, name=name,
        out_shape=tuple(pltpu.HBM(t.shape, t.dtype) for t in thru),
        in_specs=[HBM] * (2 * n) + [SEM, SEM] + [ANY] * len(after), out_specs=tuple([HBM] * (2 * n)),
        input_output_aliases={i: i for i in range(2 * n)},
        compiler_params=pltpu.CompilerParams(has_side_effects=DATAFLOW),
    )(*thru, send_sems, recv_sems, *after)
    return out[:n], out[n:]


def _adamw(w, g, m, v):
    m2 = ADAM_B1 * m + (1.0 - ADAM_B1) * g
    v2 = ADAM_B2 * v + (1.0 - ADAM_B2) * (g * g)
    m_hat = m2 / (1.0 - ADAM_B1 ** ADAM_STEP)
    v_hat = v2 / (1.0 - ADAM_B2 ** ADAM_STEP)
    delta = -ADAM_LR * (m_hat / (jnp.sqrt(v_hat) + ADAM_EPS) + ADAM_WD * w)
    return delta, m2, v2


def _sum_adamw(land, part, me_idx, w, m, v, name, tile=256):
    r, c = w.shape
    tr = min(tile, r)

    def body(me_ref, own_ref, *refs):
        slots = refs[:N_DEV]
        w_ref, m_ref, v_ref, g_ref, d_ref, m2_ref, v2_ref = refs[N_DEV:]
        own = own_ref[...].astype(F32)
        g = None
        for s in range(N_DEV):
            term = jnp.where(me_ref[0] == s, own, slots[s][...].astype(F32))
            g = term if g is None else g + term
        d, m2, v2 = _adamw(w_ref[...], g, m_ref[...], v_ref[...])
        g_ref[...] = g
        d_ref[...] = d
        m2_ref[...] = m2
        v2_ref[...] = v2

    def slot_spec(s):
        return pl.BlockSpec((None, tr, c), lambda i, me: (jnp.where(me[0] == s, (s + 1) % N_DEV, s), i, 0))

    row = pl.BlockSpec((tr, c), lambda i, me: (i, 0))
    return pl.pallas_call(
        body, name=name,
        grid_spec=pltpu.PrefetchScalarGridSpec(
            num_scalar_prefetch=1, grid=(r // tr,),
            in_specs=[pl.BlockSpec((None, tr, c), lambda i, me: (me[0], i, 0))] + [slot_spec(s) for s in range(N_DEV)] + [row] * 3,
            out_specs=[row] * 4),
        out_shape=[jax.ShapeDtypeStruct((r, c), F32)] * 4,
        compiler_params=_cparams("parallel"),
    )(me_idx, part, *[land] * N_DEV, w, m, v)


def _sum_slots(gathered, name):
    _, r, w = gathered.shape

    def body(p_ref, o_ref):
        g = p_ref[0]
        for s in range(1, N_DEV):
            g = g + p_ref[s]
        o_ref[...] = g

    return pl.pallas_call(body, name=name, out_shape=jax.ShapeDtypeStruct((r, w), F32))(gathered)


def _pack(arrs):
    flat = jnp.concatenate([a.reshape(-1).astype(F32) for a in arrs])
    rows = -(-flat.shape[0] // (8 * LANES)) * 8
    return jnp.pad(flat, (0, rows * LANES - flat.shape[0])).reshape(rows, LANES)


def _unpack(packed, shapes):
    flat = packed.reshape(-1)
    out, off = [], 0
    for s in shapes:
        size = 1
        for d in s:
            size *= d
        out.append(flat[off:off + size].reshape(s))
        off += size
    return out


def _to_hm(a, d):
    t = a.shape[0]
    return a.reshape(t, HEADS, d).transpose(1, 0, 2)


def _from_hm(a):
    h, t, d = a.shape
    return a.transpose(1, 0, 2).reshape(t, h * d)


def _gate_rows(g):
    t = g.shape[0]
    return g.T.reshape(HEADS, t // CHUNK, 1, CHUNK)


def _gate_cols(g):
    h, nc, _, c = g.shape
    return g.reshape(h, nc * c).T


def _blockdiag_dense(w):
    n = w.shape[0] * QKV_BLOCK
    tiled = jnp.tile(w.reshape(n, QKV_BLOCK), (1, n // QKV_BLOCK))
    r = lax.broadcasted_iota(jnp.int32, (n, n), 0)
    c = lax.broadcasted_iota(jnp.int32, (n, n), 1)
    return jnp.where(r // QKV_BLOCK == c // QKV_BLOCK, tiled, 0.0)


def _blockdiag_blocks(dense):
    n = dense[0].shape[0]
    k = len(dense)

    def body(*refs):
        r = lax.broadcasted_iota(jnp.int32, (n, n), 0)
        c = lax.broadcasted_iota(jnp.int32, (n, n), 1)
        fr = lax.broadcasted_iota(jnp.int32, (n, LANES), 0)
        fc = lax.broadcasted_iota(jnp.int32, (n, LANES), 1)
        fold = ((fr & (QKV_BLOCK - 1)) == fc).astype(BF16)
        for i in range(k):
            kept = jnp.where((r >> 2) == (c >> 2), refs[i][...], 0.0)
            refs[k + i][...] = sum(lax.dot_general(t, fold, _DN["nn"], preferred_element_type=F32) for t in _split3(kept))

    out = pl.pallas_call(body, name="blockdiag_blocks", out_shape=[jax.ShapeDtypeStruct((n, LANES), F32)] * k)(*dense)
    return [o[:, 0:QKV_BLOCK].reshape(n // QKV_BLOCK, QKV_BLOCK, QKV_BLOCK) for o in out]


def _col_blocks(w):
    k, n = w.shape
    return w.reshape(k, N_DEV, n // N_DEV).transpose(1, 0, 2)


def _from_col_blocks(g):
    d, k, n = g.shape
    return g.transpose(1, 0, 2).reshape(k, d * n)


def _local_step(x, tgt, weight, ws, prefetch, on_grads, on_small):
    t, d = x.shape
    g1 = ws["g_pre_mix"]

    def dep(token):
        return () if token is None else (token,)

    (h,) = _rowwise("pre_mix_norm", lambda xv, g: ((_rms(xv, g),), ()), [x], [g1], [(d, BF16)])
    fetch_mix = prefetch(("w_pa", "w_pb", "w_o"), h)
    proj = _mm(h, weight("w_in", h), "nn", F32, "proj_in", tm=512)
    fetch_up = prefetch(("w_up",), proj)
    offs = [0]
    for s in IN_SPLITS:
        offs.append(offs[-1] + s)
    q_a, k_a, v_a, g_a, a_low, x_m, o_pre, gate_a, gate_b = [proj[:, offs[i]:offs[i + 1]] for i in range(9)]

    a_low_p = jnp.pad(a_low, ((0, 0), (0, LANES - LOWRANK)))
    w_a_up_p = jnp.pad(ws["w_a_up"], ((0, LANES - LOWRANK), (0, 0)))
    b_a_up = ws["b_a_up"]
    (la,) = _rowwise("gla_decay", lambda al, w, b: ((_log_decay(al, w, b),), ()), [a_low_p], [w_a_up_p, b_a_up],
                     [(HEADS * GLA_DK, F32)], deps=dep(fetch_mix) + dep(fetch_up))
    q_hm, k_hm, la_hm = _to_hm(q_a, GLA_DK), _to_hm(k_a, GLA_DK), _to_hm(la, GLA_DK)
    o_gla, s_prev = _gla_fwd(q_hm, k_hm, v_a, la_hm)
    fetch_down = prefetch(("w_down",), o_gla)
    gn = ws["g_gla_norm"]
    (ya_in,) = _rowwise("gla_out", lambda o, g, n_: ((jnp.concatenate(_per_head(_gla_out, [o, g], [], [n_]), axis=1),), ()),
                        [o_gla, g_a], [gn], [(HEADS * HEAD_W, BF16)], deps=dep(fetch_down))
    y_a = _mm(ya_in, weight("w_pa", ya_in), "nn", F32, "proj_a")

    cw = ws["conv_w"]
    w_if_p = jnp.pad(ws["w_if"], ((0, 0), (0, LANES - 2 * HEADS)))
    ml_w = HEADS * HEAD_W
    pre_params = [cw[0:1], cw[1:2], cw[2:3], cw[3:4], ws["conv_b"],
                  _blockdiag_dense(ws["w_q_ml"]), _blockdiag_dense(ws["w_k_ml"]), _blockdiag_dense(ws["w_v_ml"]),
                  w_if_p[0:ml_w], w_if_p[ml_w:2 * ml_w], w_if_p[2 * ml_w:3 * ml_w],
                  jnp.pad(ws["b_if"], ((0, 0), (0, LANES - 2 * HEADS)))]
    x_pad = jnp.pad(x_m, ((HALO, 0), (0, 0)))
    xc, q_m, k_m, v_m, gl = _ml_pre_fwd(x_m, x_pad, pre_params)
    li, lf = _gate_rows(gl[:, 0:HEADS]), _gate_rows(gl[:, HEADS:2 * HEADS])
    hc, c_prev, n_prev, m_prev = _ml_fwd(q_m, k_m, v_m, li, lf)
    g_ml, skip = ws["g_ml_norm"], ws["ml_skip"]
    (h_b,) = _rowwise("mlstm_out", lambda a, b, c_, g, s: ((jnp.concatenate(_per_head(_ml_out, [a, b, c_], [g, s]), axis=1),), ()),
                      [hc, o_pre, xc], [g_ml, skip], [(ml_w, BF16)])
    y_b = _mm(h_b, weight("w_pb", h_b), "nn", F32, "proj_b")

    (merged,) = _rowwise("merge", lambda ga, gb, ya, yb: ((_merge(ga, gb, ya, yb),), ()), [gate_a, gate_b, y_a, y_b], [],
                         [(d, BF16)])
    z = _mm(merged, weight("w_o", merged), "nn", F32, "proj_o")
    gpm, gpl, gpo = ws["g_post_mix"], ws["g_pre_mlp"], ws["g_post_mlp"]
    x1, h2 = _rowwise("post_mix", lambda xv, zv, a, b: (_post_mix(xv, zv, a, b), ()), [x, z], [gpm, gpl], [(d, F32), (d, BF16)])
    up, u = _mm(h2, weight("w_up", h2), "nn", (F32, BF16), "mlp_up", epilogue=lambda p: (p, jnp.square(jnp.maximum(p, 0.0))))
    dn = _mm(u, weight("w_down", u), "nn", F32, "mlp_down", tm=512)

    def loss_and_grads(x1v, dnv, tgtv, g):
        loss, vjp = jax.vjp(lambda a, b, c_: _loss_rows(a, b, tgtv, c_), x1v, dnv, g)
        dx1, ddn, dg = vjp(jnp.ones((1, 1), F32))
        return (dx1, ddn), (jnp.broadcast_to(loss, (1, LANES)), dg)

    dx1_y, d_dn, loss, d_gpo = _rowwise("loss", loss_and_grads, [x1, dn, tgt], [gpo], [(d, F32), (d, BF16)],
                                        [((1, LANES), F32), ((1, d), F32)])

    (d_up,) = _mm(d_dn, weight("w_down", u), "nt", (BF16,), "mlp_down_dx", extra=[up],
                  epilogue=lambda p, a: (p * (2.0 * jnp.maximum(a, 0.0)),))
    dw_down = _mm(u, d_dn, "tn", BF16, "mlp_down_dw", tm=512)
    d_h2 = _mm(d_up, weight("w_up", h2), "nt", F32, "mlp_up_dx", tm=512)
    dw_up = _mm(h2, d_up, "tn", BF16, "mlp_up_dw")
    sent_mlp = on_grads(dict(w_down=dw_down, w_up=dw_up))

    def post_mix_bwd(xv, zv, dx1, dh2, a, b):
        _, vjp = jax.vjp(_post_mix, xv, zv, a, b)
        dx, dz, da, db = vjp((dx1, dh2))
        return (dx, dz), (da, db)

    dx_res, d_z, d_gpm, d_gpl = _rowwise("post_mix_bwd", post_mix_bwd, [x, z, dx1_y, d_h2], [gpm, gpl],
                                         [(d, F32), (d, BF16)], [((1, d), F32), ((1, d), F32)], deps=dep(sent_mlp))
    d_merged = _mm(d_z, weight("w_o", merged), "nt", F32, "proj_o_dx")
    dw_o = _mm(merged, d_z, "tn", BF16, "proj_o_dw")
    d_ga, d_gb, d_ya, d_yb = _rowwise("merge_bwd", lambda *v: (jax.vjp(_merge, *v[:4])[1](v[4]), ()),
                                      [gate_a, gate_b, y_a, y_b, d_merged], [], [(d, F32), (d, F32), (d, BF16), (d, BF16)])
    d_ya_in = _mm(d_ya, weight("w_pa", ya_in), "nt", F32, "proj_a_dx")
    dw_pa = _mm(ya_in, d_ya, "tn", BF16, "proj_a_dw")
    d_hb = _mm(d_yb, weight("w_pb", h_b), "nt", F32, "proj_b_dx")
    dw_pb = _mm(h_b, d_yb, "tn", BF16, "proj_b_dw")
    sent_mix = on_grads(dict(w_o=dw_o, w_pa=dw_pa, w_pb=dw_pb))

    def ml_out_bwd(a, b, c_, ct, g, s):
        parts = []
        for hs in _head_slices(HEAD_W):
            _, vjp = jax.vjp(_ml_out, a[:, hs], b[:, hs], c_[:, hs], g[:, hs], s[:, hs])
            parts.append(vjp(ct[:, hs]))
        cat = lambda i: jnp.concatenate([p[i] for p in parts], axis=1)
        return (cat(0), cat(1), cat(2)), (cat(3), cat(4))

    d_hc, d_opre, d_xc, d_gml, d_skip = _rowwise("mlstm_out_bwd", ml_out_bwd, [hc, o_pre, xc, d_hb], [g_ml, skip],
                                                 [(ml_w, F32)] * 3, [((1, ml_w), F32)] * 2, deps=dep(sent_mix))
    d_qm, d_km, d_vm, d_li, d_lf = _ml_bwd(q_m, k_m, v_m, li, lf, c_prev, n_prev, m_prev, d_hc)
    d_gl = jnp.concatenate([_gate_cols(d_li), _gate_cols(d_lf), jnp.zeros((t, LANES - 2 * HEADS), F32)], axis=1)
    pre_grads = _ml_pre_bwd(x_m, x_pad, pre_params, [d_xc, d_qm, d_km, d_vm, d_gl])
    d_xm = pre_grads[0]
    d_cw = jnp.concatenate(pre_grads[1:5], axis=0)
    d_cb = pre_grads[5]
    d_wq, d_wk, d_wv = _blockdiag_blocks(pre_grads[6:9])
    d_wif = jnp.concatenate(pre_grads[9:12], axis=0)[:, 0:2 * HEADS]
    d_bif = pre_grads[12][:, 0:2 * HEADS]

    def gla_out_bwd(o, g, ct, n_):
        parts = []
        for hs in _head_slices(HEAD_W):
            _, vjp = jax.vjp(_gla_out, o[:, hs], g[:, hs], n_)
            parts.append(vjp(ct[:, hs]))
        cat = lambda i: jnp.concatenate([p[i] for p in parts], axis=1)
        return (cat(0), cat(1)), (sum(p[2] for p in parts),)

    d_o, d_g_a, d_gn = _rowwise("gla_out_bwd", gla_out_bwd, [o_gla, g_a, d_ya_in], [gn], [(ml_w, F32)] * 2, [((1, HEAD_W), F32)])
    dq_hm, dk_hm, d_va, dla_hm = _gla_bwd(q_hm, k_hm, v_a, la_hm, s_prev, d_o)

    def decay_bwd(al, ct, w, b):
        _, vjp = jax.vjp(_log_decay, al, w, b)
        dal, dw, db = vjp(ct)
        return (dal,), (dw, db)

    d_alow_p, d_wa_p, d_ba = _rowwise("gla_decay_bwd", decay_bwd, [a_low_p, _from_hm(dla_hm)], [w_a_up_p, b_a_up],
                                      [(LANES, F32)], [(w_a_up_p.shape, F32), (b_a_up.shape, F32)])
    d_proj = jnp.concatenate([_from_hm(dq_hm), _from_hm(dk_hm), d_va, d_g_a, d_alow_p[:, 0:LOWRANK], d_xm, d_opre, d_ga, d_gb],
                             axis=1).astype(BF16)
    d_h = _mm(d_proj, weight("w_in", h), "nt", F32, "proj_in_dx", tm=512)

    def pre_mix_bwd(xv, dh, dres, g):
        _, vjp = jax.vjp(_rms, xv, g)
        dx, dg = vjp(dh)
        return (dx + dres,), (dg,)

    grad_x, d_g1 = _rowwise("pre_mix_norm_bwd", pre_mix_bwd, [x, d_h, dx_res], [g1], [(d, F32)], [((1, d), F32)])
    small = dict(g_pre_mix=d_g1, w_a_up=d_wa_p[0:LOWRANK], b_a_up=d_ba, g_gla_norm=d_gn, conv_w=d_cw, conv_b=d_cb,
                 w_q_ml=d_wq, w_k_ml=d_wk, w_v_ml=d_wv, w_if=d_wif, b_if=d_bif, ml_skip=d_skip, g_ml_norm=d_gml,
                 g_post_mix=d_gpm, g_pre_mlp=d_gpl, g_post_mlp=d_gpo)
    sent_small = on_small(small, loss[:, 0:1])
    dw_in = _mm(h, d_proj, "tn", F32, "proj_in_dw", tm=512, tk=1024, deps=dep(sent_small))
    return grad_x, on_grads(dict(w_in=dw_in))


BIG = ("w_in", "w_pa", "w_pb", "w_o", "w_up", "w_down")
BIG_COL_SHARDED = ("w_in", "w_pa", "w_pb", "w_up")
SMALL_SHARDED = {"w_a_up": 1, "conv_w": 1, "w_if": 0}
SMALL = ("g_pre_mix", "w_a_up", "b_a_up", "g_gla_norm", "conv_w", "conv_b", "w_q_ml", "w_k_ml", "w_v_ml", "w_if", "b_if",
         "ml_skip", "g_ml_norm", "g_post_mix", "g_pre_mlp", "g_post_mlp")
WEIGHTS = ("g_pre_mix", "w_in", "w_a_up", "b_a_up", "g_gla_norm", "conv_w", "conv_b", "w_q_ml", "w_k_ml", "w_v_ml", "w_if", "b_if",
           "ml_skip", "g_ml_norm", "w_pa", "w_pb", "w_o", "g_post_mix", "g_pre_mlp", "w_up", "w_down", "g_post_mlp")


def _my_slice(a, axis):
    n = a.shape[axis] // N_DEV
    return lax.dynamic_slice_in_dim(a, _lin(_me()) * n, n, axis)


def kernel(x, g_pre_mix, w_in, w_a_up, b_a_up, g_gla_norm, conv_w, conv_b, w_q_ml, w_k_ml, w_v_ml, w_if, b_if, ml_skip, g_ml_norm, w_pa, w_pb, w_o, g_post_mix, g_pre_mlp, w_up, w_down, g_post_mlp, loss_target, m_g_pre_mix, m_w_in, m_w_a_up, m_b_a_up, m_g_gla_norm, m_conv_w, m_conv_b, m_w_q_ml, m_w_k_ml, m_w_v_ml, m_w_if, m_b_if, m_ml_skip, m_g_ml_norm, m_w_pa, m_w_pb, m_w_o, m_g_post_mix, m_g_pre_mlp, m_w_up, m_w_down, m_g_post_mlp, v_g_pre_mix, v_w_in, v_w_a_up, v_b_a_up, v_g_gla_norm, v_conv_w, v_conv_b, v_w_q_ml, v_w_k_ml, v_w_v_ml, v_w_if, v_b_if, v_ml_skip, v_g_ml_norm, v_w_pa, v_w_pb, v_w_o, v_g_post_mix, v_g_pre_mlp, v_w_up, v_w_down, v_g_post_mlp):
    args = dict(locals())
    w = {n: args[n][0] for n in WEIGHTS}
    m = {n: args["m_" + n][0] for n in WEIGHTS}
    v = {n: args["v_" + n][0] for n in WEIGHTS}

    me_lin = _lin(_me())
    me_idx = jnp.reshape(me_lin, (1,)).astype(jnp.int32)

    def full_weight(n, g):
        return _from_col_blocks(g) if n in BIG_COL_SHARDED else g.reshape(-1, g.shape[-1])

    def grad_parts(n, g):
        return (_col_blocks(g) if n in BIG_COL_SHARDED else g.reshape(N_DEV, -1, g.shape[-1])).astype(BF16)

    sharded_names = tuple(SMALL_SHARDED)
    small_w = _pack([w[n] for n in sharded_names])
    small_w_state, small_w_token = _copies_start("gather", [small_w], "allgather_start_small_weights")
    ready = {"w_in": full_weight("w_in", _allgather_big([w["w_in"].astype(BF16)], "allgather_w_in", [small_w_token])[0])}
    pending = {}

    def prefetch(group, after):
        state, token = _copies_start("gather", [w[n].astype(BF16) for n in group], "allgather_start_" + group[0], after)
        for n in group:
            pending[n] = (group, state)
        return token

    def weight(n, after):
        if n not in ready:
            group, state = pending[n]
            shards, lands = _copies_wait(state, after, "allgather_wait_" + group[0])
            for gn, shard, land in zip(group, shards, lands):
                ready[gn] = full_weight(gn, lax.dynamic_update_slice(land, shard[None], (me_lin, 0, 0)))
        return ready[n]

    (small_w,), (small_w_land,) = _copies_wait(small_w_state, ready["w_in"], "allgather_wait_small_weights")
    small_g = lax.dynamic_update_slice(small_w_land, small_w[None], (me_lin, 0, 0))
    ws = {n: (w[n].reshape(1, -1) if w[n].ndim == 1 else w[n]) for n in SMALL if n not in SMALL_SHARDED}
    per_dev = [_unpack(small_g[dev], [w[n].shape for n in sharded_names]) for dev in range(N_DEV)]
    for i, n in enumerate(sharded_names):
        ws[n] = jnp.concatenate([per_dev[dev][i] for dev in range(N_DEV)], axis=SMALL_SHARDED[n])

    sent = []

    def on_grads(grads):
        names = tuple(grads)
        state, token = _copies_start("exchange", [grad_parts(n, grads[n]) for n in names], "exchange_start_" + names[0])
        sent.append((names, state))
        return token

    small_sent = {}

    def on_small(small, loss):
        small_sent["shapes"] = [small[n].shape for n in SMALL] + [(1, 1)]
        small_sent["state"], token = _copies_start("gather", [_pack([small[n] for n in SMALL] + [loss])],
                                                   "allgather_start_small_grads")
        return token

    grad_x, last_token = _local_step(x[0], loss_target[0], weight, ws, prefetch, on_grads, on_small)

    out = {}

    def finish(names, state, after):
        parts, lands = _copies_wait(state, after, "exchange_wait_" + names[0])
        for n, part, land in zip(names, parts, lands):
            out[n] = _sum_adamw(land, part, me_idx, w[n], m[n], v[n], "adamw_" + n)

    for names, state in sent[:-1]:
        finish(names, state, [grad_x, last_token])

    (small_vec,), (small_land,) = _copies_wait(small_sent["state"], [grad_x, last_token], "allgather_wait_small_grads")
    vec = _sum_slots(lax.dynamic_update_slice(small_land, small_vec[None], (me_lin, 0, 0)), "sum_small_grads")
    summed = _unpack(vec, small_sent["shapes"])
    g_small = {}
    for n, g in zip(SMALL, summed[:-1]):
        g_small[n] = _my_slice(g, SMALL_SHARDED[n]) if n in SMALL_SHARDED else g
    shapes = [g_small[n].shape for n in SMALL]
    packed = [_pack([d[n].reshape(g_small[n].shape) for n in SMALL]) for d in (w, g_small, m, v)]
    upd = _rowwise("adamw_small", lambda a, b, c_, d_: (_adamw(a, b, c_, d_), ()), packed, [], [(LANES, F32)] * 3, tile=8 * 1024)
    deltas, new_ms, new_vs = [_unpack(p, shapes) for p in upd]
    for i, n in enumerate(SMALL):
        out[n] = (g_small[n], deltas[i], new_ms[i], new_vs[i])
    finish(*sent[-1], [upd[0]] + [out[n][1] for n in BIG if n in out])

    shaped = lambda a, n: a.reshape(args[n].shape)
    return (summed[-1].reshape(()), grad_x[None],
            *[shaped(out[n][0], n) for n in WEIGHTS], *[shaped(out[n][1], n) for n in WEIGHTS],
            *[shaped(out[n][2], n) for n in WEIGHTS], *[shaped(out[n][3], n) for n in WEIGHTS])
```

```python
import functools

import jax
import jax.numpy as jnp
from jax import lax
from jax.experimental import pallas as pl
from jax.experimental.pallas import tpu as pltpu

F32 = jnp.float32
BF16 = jnp.bfloat16
MESH = pl.DeviceIdType.MESH

N_DEV = 8
EPS = 1e-6
CHUNK = 64
HEADS = 4
GLA_DK = 64
HEAD_W = 128
GLA_GATE_NORM = 16.0
LOWRANK = 16
CONV_K = 4
QKV_BLOCK = 4
LANES = 128
HALO = 8
IN_SPLITS = (256, 256, 512, 512, 16, 512, 512, 1024, 1024)

ADAM_LR = 0.001
ADAM_B1 = 0.9
ADAM_B2 = 0.999
ADAM_EPS = 1e-08
ADAM_WD = 0.01
ADAM_STEP = 10

VMEM_LIMIT = 56 * 1024 * 1024


def _cparams(*sem):
    return pltpu.CompilerParams(dimension_semantics=sem, vmem_limit_bytes=VMEM_LIMIT)


_DN = {"nn": (((1,), (0,)), ((), ())), "nt": (((1,), (1,)), ((), ())), "tn": (((0,), (0,)), ((), ()))}


def _raw_dot(a, b, mode):
    return lax.dot_general(a.astype(BF16), b.astype(BF16), _DN[mode], preferred_element_type=F32)


@functools.partial(jax.custom_vjp, nondiff_argnums=(2,))
def _bdot(a, b, mode):
    return _raw_dot(a, b, mode)


def _bdot_fwd(a, b, mode):
    return _raw_dot(a, b, mode), (a, b)


def _bdot_bwd(mode, res, ct):
    a, b = res
    if mode == "nn":
        da, db = _raw_dot(ct, b, "nt"), _raw_dot(a, ct, "tn")
    elif mode == "nt":
        da, db = _raw_dot(ct, b, "nn"), _raw_dot(ct, a, "tn")
    else:
        da, db = _raw_dot(b, ct, "nt"), _raw_dot(a, ct, "nn")
    return da.astype(a.dtype), db.astype(b.dtype)


_bdot.defvjp(_bdot_fwd, _bdot_bwd)


def _split3(x):
    hi = x.astype(BF16)
    r1 = x - hi.astype(F32)
    mid = r1.astype(BF16)
    return hi, mid, (r1 - mid.astype(F32)).astype(BF16)


def _split_dot(tri, x):
    return sum(lax.dot_general(tri, t, _DN["nn"], preferred_element_type=F32) for t in _split3(x))


def _tri(n, lower):
    r = lax.broadcasted_iota(jnp.int32, (n, n), 0)
    c = lax.broadcasted_iota(jnp.int32, (n, n), 1)
    return ((c <= r) if lower else (c >= r)).astype(BF16)


@jax.custom_vjp
def _cumsum_rows(x):
    return _split_dot(_tri(x.shape[0], True), x)


def _cumsum_rows_fwd(x):
    return _cumsum_rows(x), None


def _cumsum_rows_bwd(_, ct):
    return (_split_dot(_tri(ct.shape[0], False), ct),)


_cumsum_rows.defvjp(_cumsum_rows_fwd, _cumsum_rows_bwd)


def _abs(x):
    return jnp.where(x >= 0, x, -x)


def _sigmoid(x):
    return lax.logistic(x)


def _log_sigmoid(x):
    return jnp.minimum(x, 0.0) - jnp.log(1.0 + jnp.exp(-_abs(x)))


def _rms(x, g):
    return x * lax.rsqrt(jnp.mean(x * x, axis=-1, keepdims=True) + EPS) * g


def _head_slices(w):
    return [slice(h * w, (h + 1) * w) for h in range(HEADS)]


def _tile(dim, want):
    if dim <= want or dim % LANES:
        return dim
    t = want
    while dim % t:
        t -= LANES
    return t


def _mm(a, b, mode, out_dtype, name, tm=1024, tn=1024, tk=4096, epilogue=None, extra=(), deps=()):
    if mode == "nn":
        (m, k), (k2, n) = a.shape, b.shape
    elif mode == "nt":
        (m, k), (n, k2) = a.shape, b.shape
    else:
        (k, m), (k2, n) = a.shape, b.shape
    assert k == k2, (name, a.shape, b.shape)
    tm, tn, tk = _tile(m, tm), _tile(n, tn), _tile(k, tk)
    nk = k // tk
    out_dtypes = out_dtype if epilogue else (out_dtype,)
    assert nk == 1 or (out_dtype == F32 and not epilogue), name
    n_in = 2 + len(extra)

    def body(*refs):
        p = _raw_dot(refs[0][...], refs[1][...], mode)
        if nk > 1:
            _accumulate(pl.program_id(2), [refs[n_in + len(deps)]], [p])
            return
        outs = epilogue(p, *[r[...] for r in refs[2:n_in]]) if epilogue else (p,)
        for ref, val in zip(refs[n_in + len(deps):], outs):
            ref[...] = val.astype(ref.dtype)

    a_spec = pl.BlockSpec((tk, tm), lambda i, j, kk: (kk, i)) if mode == "tn" else pl.BlockSpec((tm, tk), lambda i, j, kk: (i, kk))
    b_spec = pl.BlockSpec((tn, tk), lambda i, j, kk: (j, kk)) if mode == "nt" else pl.BlockSpec((tk, tn), lambda i, j, kk: (kk, j))
    o_spec = pl.BlockSpec((tm, tn), lambda i, j, kk: (i, j))
    res = pl.pallas_call(
        body, name=name, grid=(m // tm, n // tn, nk),
        in_specs=[a_spec, b_spec] + [o_spec] * len(extra) + [ANY] * len(deps), out_specs=[o_spec] * len(out_dtypes),
        out_shape=[jax.ShapeDtypeStruct((m, n), dt) for dt in out_dtypes],
        compiler_params=_cparams("parallel", "parallel", "arbitrary"),
    )(a, b, *extra, *deps)
    return res if epilogue else res[0]


def _mm_shards(a, wg, name, tm=512):
    t, k = a.shape
    nb, _, n = wg.shape
    tm = _tile(t, tm)

    def body(a_ref, w_ref, o_ref):
        o_ref[...] = _raw_dot(a_ref[...], w_ref[...], "nn")

    return pl.pallas_call(
        body, name=name, grid=(nb, t // tm),
        in_specs=[pl.BlockSpec((tm, k), lambda j, i: (i, 0)), pl.BlockSpec((None, k, n), lambda j, i: (j, 0, 0))],
        out_specs=pl.BlockSpec((None, tm, n), lambda j, i: (j, i, 0)),
        out_shape=jax.ShapeDtypeStruct((nb, t, n), F32),
        compiler_params=_cparams("parallel", "parallel"),
    )(a, wg)


def _mm_shards_dx(dy, wg, name, tm=512):
    nb, t, n = dy.shape
    k = wg.shape[1]
    tm = _tile(t, tm)

    def body(d_ref, w_ref, o_ref):
        _accumulate(pl.program_id(1), [o_ref], [_raw_dot(d_ref[...], w_ref[...], "nt")])

    return pl.pallas_call(
        body, name=name, grid=(t // tm, nb),
        in_specs=[pl.BlockSpec((None, tm, n), lambda i, j: (j, i, 0)), pl.BlockSpec((None, k, n), lambda i, j: (j, 0, 0))],
        out_specs=pl.BlockSpec((tm, k), lambda i, j: (i, 0)),
        out_shape=jax.ShapeDtypeStruct((t, k), F32),
        compiler_params=_cparams("parallel", "arbitrary"),
    )(dy, wg)


def _mm_shards_dw(a, dy, name, deps=()):
    t, k = a.shape
    nb, _, n = dy.shape

    def body(a_ref, d_ref, *rest):
        rest[-1][...] = _raw_dot(a_ref[...], d_ref[...], "tn").astype(BF16)

    return pl.pallas_call(
        body, name=name, grid=(nb,),
        in_specs=[pl.BlockSpec((t, k), lambda j: (0, 0)), pl.BlockSpec((None, t, n), lambda j: (j, 0, 0))] + [ANY] * len(deps),
        out_specs=pl.BlockSpec((None, k, n), lambda j: (j, 0, 0)),
        out_shape=jax.ShapeDtypeStruct((nb, k, n), BF16),
        compiler_params=_cparams("parallel"),
    )(a, dy, *deps)


def _shard_columns(arr, start, stop):
    n = arr.shape[2]
    pieces = [arr[j][:, max(start, j * n) - j * n:min(stop, (j + 1) * n) - j * n]
              for j in range(start // n, (stop - 1) // n + 1)]
    return pieces[0] if len(pieces) == 1 else jnp.concatenate(pieces, axis=1)


def _to_shards(segments, n):
    bounds = [0]
    for s in segments:
        bounds.append(bounds[-1] + s.shape[1])
    shards = []
    for j in range(bounds[-1] // n):
        lo, hi = j * n, (j + 1) * n
        pieces = [s[:, max(lo, b) - b:min(hi, b + s.shape[1]) - b].astype(BF16)
                  for s, b in zip(segments, bounds) if b < hi and b + s.shape[1] > lo]
        shards.append(jnp.concatenate(pieces, axis=1))
    return jnp.stack(shards)


def _rowwise(name, fn, rows, params, out_rows, out_accs=(), tile=256, deps=()):
    t = rows[0].shape[0]
    r = min(tile, t)
    assert t % r == 0
    n_in, n_or = len(rows) + len(params), len(out_rows)
    n_all = n_in + len(deps)
    params = list(params) + list(deps)

    def body(*refs):
        vals = [ref[...] for ref in refs[:n_in]]
        outs = refs[n_all:]
        ro, ao = fn(*vals)
        for ref, v in zip(outs[:n_or], ro):
            ref[...] = v.astype(ref.dtype)
        if out_accs:
            _accumulate(pl.program_id(0), outs[n_or:], ao)

    def full(shape):
        return pl.BlockSpec(shape, lambda i, nd=len(shape): (0,) * nd)

    return pl.pallas_call(
        body, name=name, grid=(t // r,),
        in_specs=[pl.BlockSpec((r, a.shape[1]), lambda i: (i, 0)) for a in rows] + [full(p.shape) for p in params],
        out_specs=[pl.BlockSpec((r, w), lambda i: (i, 0)) for w, _ in out_rows] + [full(s) for s, _ in out_accs],
        out_shape=[jax.ShapeDtypeStruct((t, w), dt) for w, dt in out_rows] + [jax.ShapeDtypeStruct(s, dt) for s, dt in out_accs],
        compiler_params=_cparams("arbitrary"),
    )(*rows, *params)


def _accumulate(step, refs, vals):
    for ref, v in zip(refs, vals):
        @pl.when(step == 0)
        def _(ref=ref, v=v):
            ref[...] = v.astype(ref.dtype)

        @pl.when(step > 0)
        def _(ref=ref, v=v):
            ref[...] += v.astype(ref.dtype)


def _gla_chunk(q, k, v, la, st):
    c = q.shape[0]
    row = lax.broadcasted_iota(jnp.int32, (c, c), 0)
    col = lax.broadcasted_iota(jnp.int32, (c, c), 1)
    cum = _cumsum_rows(la)
    cl = jnp.sum(la, axis=0, keepdims=True)
    ep = jnp.exp(cum)
    en = jnp.exp(-cum)
    qs = q * (GLA_DK ** -0.5)
    qp = qs * ep
    a_f = _bdot(qp, k * en, "nt")
    a_b = _bdot(qs * en, k * ep, "nt")
    sc = jnp.where(row >= col, a_f, a_b)
    o = _bdot(sc, v, "nn") + _bdot(qp, st, "nt")
    kd = k * jnp.exp(cl - cum)
    st_new = st * jnp.exp(cl) + _bdot(v, kd, "tn")
    return o, st_new


def _gla_specs(nc, rev):
    def ch(n):
        return (nc - 1 - n) if rev else n
    hm = pl.BlockSpec((HEADS, CHUNK, GLA_DK), lambda n: (0, ch(n), 0))
    tm = pl.BlockSpec((CHUNK, HEADS * HEAD_W), lambda n: (ch(n), 0))
    st = pl.BlockSpec((HEADS, None, HEAD_W, GLA_DK), lambda n: (0, ch(n), 0, 0))
    return hm, tm, st


def _gla_fwd(q, k, v, la):
    t = v.shape[0]
    nc = t // CHUNK
    hm, tm, st = _gla_specs(nc, False)

    def body(q_ref, k_ref, v_ref, la_ref, o_ref, sp_ref, st_ref):
        @pl.when(pl.program_id(0) == 0)
        def _():
            st_ref[...] = jnp.zeros_like(st_ref)

        for h, hs in enumerate(_head_slices(HEAD_W)):
            s = st_ref[h]
            sp_ref[h] = s
            o, s_new = _gla_chunk(q_ref[h], k_ref[h], v_ref[:, hs], la_ref[h], s)
            o_ref[:, hs] = o
            st_ref[h] = s_new

    return pl.pallas_call(
        body, name="gla_fwd", grid=(nc,),
        in_specs=[hm, hm, tm, hm], out_specs=[tm, st],
        out_shape=[jax.ShapeDtypeStruct((t, HEADS * HEAD_W), F32), jax.ShapeDtypeStruct((HEADS, nc, HEAD_W, GLA_DK), F32)],
        scratch_shapes=[pltpu.VMEM((HEADS, HEAD_W, GLA_DK), F32)],
        compiler_params=_cparams("arbitrary"),
    )(q, k, v, la)


def _gla_bwd(q, k, v, la, sp, do):
    t = v.shape[0]
    nc = t // CHUNK
    hm, tm, st = _gla_specs(nc, True)

    def body(q_ref, k_ref, v_ref, la_ref, sp_ref, do_ref, dq_ref, dk_ref, dv_ref, dla_ref, ds_ref):
        @pl.when(pl.program_id(0) == 0)
        def _():
            ds_ref[...] = jnp.zeros_like(ds_ref)

        for h, hs in enumerate(_head_slices(HEAD_W)):
            _, vjp = jax.vjp(_gla_chunk, q_ref[h], k_ref[h], v_ref[:, hs], la_ref[h], sp_ref[h])
            dq, dk, dv, dla, ds = vjp((do_ref[:, hs], ds_ref[h]))
            dq_ref[h] = dq
            dk_ref[h] = dk
            dv_ref[:, hs] = dv
            dla_ref[h] = dla
            ds_ref[h] = ds

    hm_shape = jax.ShapeDtypeStruct((HEADS, t, GLA_DK), F32)
    return pl.pallas_call(
        body, name="gla_bwd", grid=(nc,),
        in_specs=[hm, hm, tm, hm, st, tm], out_specs=[hm, hm, tm, hm],
        out_shape=[hm_shape, hm_shape, jax.ShapeDtypeStruct((t, HEADS * HEAD_W), F32), hm_shape],
        scratch_shapes=[pltpu.VMEM((HEADS, HEAD_W, GLA_DK), F32)],
        compiler_params=_cparams("arbitrary"),
    )(q, k, v, la, sp, do)


def _ml_chunk(q, k, v, li_r, lf_r, cm, nv, m):
    c = q.shape[0]
    row = lax.broadcasted_iota(jnp.int32, (c, c), 0)
    col = lax.broadcasted_iota(jnp.int32, (c, c), 1)
    eye = (row == col).astype(F32)
    li_c = jnp.sum(eye * li_r, axis=1, keepdims=True)
    lf_c = jnp.sum(eye * lf_r, axis=1, keepdims=True)
    fc_c = jnp.sum((col <= row).astype(F32) * lf_r, axis=1, keepdims=True)
    fc_r = jnp.sum((row <= col).astype(F32) * lf_c, axis=0, keepdims=True)
    f_last = jnp.sum(lf_r, axis=1, keepdims=True)
    kc = k * (HEAD_W ** -0.5)
    a_c = f_last - fc_c + li_c
    m_loc = jnp.max(a_c, axis=0, keepdims=True)
    kw = kc * jnp.exp(a_c - m_loc)
    c_chunk = _bdot(kw, v, "tn")
    n_chunk = jnp.sum(kw, axis=0, keepdims=True)
    m_new = jnp.maximum(f_last + m, m_loc)
    sp = jnp.exp(f_last + m - m_new)
    sl = jnp.exp(m_loc - m_new)
    cm_new = sp * cm + sl * c_chunk
    nv_new = sp * nv + sl * n_chunk
    log_d = li_r - _abs(fc_c - fc_r)
    g_inter = fc_c + m
    m_t = jnp.maximum(g_inter, jnp.max(log_d, axis=1, keepdims=True))
    s = _bdot(q, kc, "nt") * jnp.exp(log_d - m_t)
    sc = jnp.exp(g_inter - m_t)
    num = _bdot(s, v, "nn") + sc * _bdot(q, cm, "nn")
    den = jnp.sum(s, axis=1, keepdims=True) + sc * jnp.sum(q * nv, axis=1, keepdims=True)
    den = jnp.maximum(_abs(den), jnp.exp(-m_t))
    return num / den, cm_new, nv_new, m_new


def _ml_specs(nc, rev):
    def ch(n):
        return (nc - 1 - n) if rev else n
    tm = pl.BlockSpec((CHUNK, HEADS * HEAD_W), lambda n: (ch(n), 0))
    gate = pl.BlockSpec((HEADS, None, 1, CHUNK), lambda n: (0, ch(n), 0, 0))
    cm = pl.BlockSpec((HEADS, None, HEAD_W, HEAD_W), lambda n: (0, ch(n), 0, 0))
    vec = pl.BlockSpec((HEADS, None, 1, HEAD_W), lambda n: (0, ch(n), 0, 0))
    return tm, gate, cm, vec


_ML_STATE = [pltpu.VMEM((HEADS, HEAD_W, HEAD_W), F32), pltpu.VMEM((HEADS, 1, HEAD_W), F32), pltpu.VMEM((HEADS, 1, HEAD_W), F32)]


def _ml_fwd(q, k, v, li, lf):
    t = q.shape[0]
    nc = t // CHUNK
    tm, gate, cm, vec = _ml_specs(nc, False)

    def body(q_ref, k_ref, v_ref, li_ref, lf_ref, hc_ref, cp_ref, np_ref, mp_ref, c_ref, n_ref, m_ref):
        @pl.when(pl.program_id(0) == 0)
        def _():
            c_ref[...] = jnp.zeros_like(c_ref)
            n_ref[...] = jnp.zeros_like(n_ref)
            m_ref[...] = jnp.zeros_like(m_ref)

        for h, hs in enumerate(_head_slices(HEAD_W)):
            c0, n0, m0 = c_ref[h], n_ref[h], m_ref[h]
            cp_ref[h] = c0
            np_ref[h] = n0
            mp_ref[h] = m0
            hc, c1, n1, m1 = _ml_chunk(q_ref[:, hs], k_ref[:, hs], v_ref[:, hs], li_ref[h], lf_ref[h], c0, n0, m0[:, 0:1])
            hc_ref[:, hs] = hc
            c_ref[h] = c1
            n_ref[h] = n1
            m_ref[h] = jnp.broadcast_to(m1, (1, HEAD_W))

    return pl.pallas_call(
        body, name="mlstm_fwd", grid=(nc,),
        in_specs=[tm, tm, tm, gate, gate], out_specs=[tm, cm, vec, vec],
        out_shape=[jax.ShapeDtypeStruct((t, HEADS * HEAD_W), F32), jax.ShapeDtypeStruct((HEADS, nc, HEAD_W, HEAD_W), F32),
                   jax.ShapeDtypeStruct((HEADS, nc, 1, HEAD_W), F32), jax.ShapeDtypeStruct((HEADS, nc, 1, HEAD_W), F32)],
        scratch_shapes=_ML_STATE,
        compiler_params=_cparams("arbitrary"),
    )(q, k, v, li, lf)


def _ml_bwd(q, k, v, li, lf, cp, npv, mp, dhc):
    t = q.shape[0]
    nc = t // CHUNK
    tm, gate, cm, vec = _ml_specs(nc, True)

    def body(q_ref, k_ref, v_ref, li_ref, lf_ref, cp_ref, np_ref, mp_ref, dhc_ref,
             dq_ref, dk_ref, dv_ref, dli_ref, dlf_ref, dc_ref, dn_ref, dm_ref):
        @pl.when(pl.program_id(0) == 0)
        def _():
            dc_ref[...] = jnp.zeros_like(dc_ref)
            dn_ref[...] = jnp.zeros_like(dn_ref)
            dm_ref[...] = jnp.zeros_like(dm_ref)

        for h, hs in enumerate(_head_slices(HEAD_W)):
            _, vjp = jax.vjp(_ml_chunk, q_ref[:, hs], k_ref[:, hs], v_ref[:, hs], li_ref[h], lf_ref[h],
                             cp_ref[h], np_ref[h], mp_ref[h][:, 0:1])
            dq, dk, dv, dli, dlf, dc, dn, dm = vjp((dhc_ref[:, hs], dc_ref[h], dn_ref[h], dm_ref[h][:, 0:1]))
            dq_ref[:, hs] = dq
            dk_ref[:, hs] = dk
            dv_ref[:, hs] = dv
            dli_ref[h] = dli
            dlf_ref[h] = dlf
            dc_ref[h] = dc
            dn_ref[h] = dn
            dm_ref[h] = jnp.broadcast_to(dm, (1, HEAD_W))

    tm_shape = jax.ShapeDtypeStruct((t, HEADS * HEAD_W), F32)
    gate_shape = jax.ShapeDtypeStruct((HEADS, nc, 1, CHUNK), F32)
    return pl.pallas_call(
        body, name="mlstm_bwd", grid=(nc,),
        in_specs=[tm, tm, tm, gate, gate, cm, vec, vec, tm], out_specs=[tm, tm, tm, gate, gate],
        out_shape=[tm_shape, tm_shape, tm_shape, gate_shape, gate_shape],
        scratch_shapes=_ML_STATE,
        compiler_params=_cparams("arbitrary"),
    )(q, k, v, li, lf, cp, npv, mp, dhc)


def _ml_pre(s0, s1, s2, s3, cw0, cw1, cw2, cw3, cb, wq, wk, wv, wiq, wik, wiv, bif):
    pre = cb + cw0 * s0 + cw1 * s1 + cw2 * s2 + cw3 * s3
    xc = pre * _sigmoid(pre)
    q = _bdot(xc, wq, "nn")
    k = _bdot(xc, wk, "nn")
    v = _bdot(s3, wv, "nn")
    gates = _bdot(q, wiq, "nn") + _bdot(k, wik, "nn") + _bdot(v, wiv, "nn") + bif
    lane = lax.broadcasted_iota(jnp.int32, gates.shape, 1)
    gl = jnp.where(lane < HEADS, gates, _log_sigmoid(gates))
    return xc, q, k, v, gl


def _delayed(xs_ref, x_ref, halo_ref, r):
    xs_ref[0:HALO, :] = halo_ref[...]
    xs_ref[HALO:HALO + r, :] = x_ref[...]
    return [xs_ref[pl.ds(HALO - (CONV_K - 1) + j, r), :] for j in range(CONV_K)]


def _full_spec(shape):
    return pl.BlockSpec(shape, lambda i, nd=len(shape): (0,) * nd)


def _ml_pre_fwd(x_m, x_pad, params, tile=256):
    t, w = x_m.shape
    r = min(tile, t)

    def body(*refs):
        x_ref, halo_ref = refs[:2]
        p = [ref[...] for ref in refs[2:2 + len(params)]]
        outs = refs[2 + len(params):-1]
        res = _ml_pre(*_delayed(refs[-1], x_ref, halo_ref, r), *p)
        for ref, val in zip(outs, res):
            ref[...] = val

    row = pl.BlockSpec((r, w), lambda i: (i, 0))
    return pl.pallas_call(
        body, name="ml_pre_fwd", grid=(t // r,),
        in_specs=[row, pl.BlockSpec((HALO, w), lambda i: (i * (r // HALO), 0))] + [_full_spec(p.shape) for p in params],
        out_specs=[row] * 4 + [pl.BlockSpec((r, LANES), lambda i: (i, 0))],
        out_shape=[jax.ShapeDtypeStruct((t, w), F32)] * 4 + [jax.ShapeDtypeStruct((t, LANES), F32)],
        scratch_shapes=[pltpu.VMEM((r + HALO, w), F32)],
        compiler_params=_cparams("arbitrary"),
    )(x_m, x_pad, *params)


def _ml_pre_bwd(x_m, x_pad, params, cts, tile=256):
    t, w = x_m.shape
    r = min(tile, t)
    nt = t // r
    n_p = len(params)

    def body(*refs):
        x_ref, halo_ref = refs[:2]
        p = [ref[...] for ref in refs[2:2 + n_p]]
        ct = [ref[...] for ref in refs[2 + n_p:7 + n_p]]
        dx_ref = refs[7 + n_p]
        dp_refs = refs[8 + n_p:8 + 2 * n_p]
        xs_ref, ds_ref, carry_ref = refs[8 + 2 * n_p:]
        step = pl.program_id(0)

        @pl.when(step == 0)
        def _():
            ds_ref[...] = jnp.zeros_like(ds_ref)
            carry_ref[...] = jnp.zeros_like(carry_ref)

        _, vjp = jax.vjp(_ml_pre, *_delayed(xs_ref, x_ref, halo_ref, r), *p)
        grads = vjp(tuple(ct))
        for j in range(CONV_K):
            ds_ref[j, HALO:HALO + r, :] = grads[j]
        lead = HALO + CONV_K - 1
        d_tile = sum(ds_ref[j, pl.ds(lead - j, r), :] for j in range(CONV_K))
        d_halo = sum(ds_ref[j, pl.ds(CONV_K - 1 - j, HALO), :] for j in range(CONV_K))
        dx_ref[...] = d_tile
        dx_ref[r - HALO:r, :] += carry_ref[...]
        carry_ref[...] = d_halo
        _accumulate(step, dp_refs, grads[CONV_K:])

    row = pl.BlockSpec((r, w), lambda i: (nt - 1 - i, 0))
    return pl.pallas_call(
        body, name="ml_pre_bwd", grid=(nt,),
        in_specs=[row, pl.BlockSpec((HALO, w), lambda i: ((nt - 1 - i) * (r // HALO), 0))] + [_full_spec(p.shape) for p in params]
        + [row] * 4 + [pl.BlockSpec((r, LANES), lambda i: (nt - 1 - i, 0))],
        out_specs=[row] + [_full_spec(p.shape) for p in params],
        out_shape=[jax.ShapeDtypeStruct((t, w), F32)] + [jax.ShapeDtypeStruct(p.shape, F32) for p in params],
        scratch_shapes=[pltpu.VMEM((r + HALO, w), F32), pltpu.VMEM((CONV_K, r + 2 * HALO, w), F32), pltpu.VMEM((HALO, w), F32)],
        compiler_params=_cparams("arbitrary"),
    )(x_m, x_pad, *params, *cts)


def _per_head(fn, row_vals, head_params, shared_params=()):
    return [fn(*[a[:, hs] for a in row_vals], *[p[:, hs] for p in head_params], *shared_params) for hs in _head_slices(HEAD_W)]


def _gla_out(o, g, gn):
    return _rms(o, gn) * (g * _sigmoid(g))


def _ml_out(hc, op, xc, g, sk):
    hcell = hc * _sigmoid(op)
    mu = jnp.mean(hcell, axis=-1, keepdims=True)
    d = hcell - mu
    var = jnp.mean(d * d, axis=-1, keepdims=True)
    return d * lax.rsqrt(var + EPS) * g + sk * xc


def _log_decay(al, w, b):
    return _log_sigmoid(_bdot(al, w, "nn") + b) * (1.0 / GLA_GATE_NORM)


def _merge(ga, gb, ya, yb):
    return _sigmoid(ga) * ya + _sigmoid(gb) * yb


def _post_mix(x, z, gpm, gpl):
    x1 = x + _rms(z, gpm)
    return x1, _rms(x1, gpl)


def _loss_rows(x1, dn, tgt, g):
    e = x1 + _rms(dn, g) - tgt
    return 0.5 * jnp.sum(jnp.mean(e * e, axis=-1, keepdims=True), axis=0, keepdims=True)


def _lin(p):
    return 4 * p[0] + 2 * p[1] + p[2]


def _me():
    return lax.axis_index("x"), lax.axis_index("y"), lax.axis_index("c")


def _flip(p, k):
    return tuple((1 - v) if (k >> (2 - i)) & 1 else v for i, v in enumerate(p))


ANY = pl.BlockSpec(memory_space=pl.ANY)


def _allgather_big(shards, name, deps=()):
    n = len(shards)
    n_in = n + len(deps)

    def body(*refs):
        ins, outs = refs[:n], refs[n_in:n_in + n]
        send_sems, recv_sems, local_sems = refs[n_in + n:]
        me = _me()
        x, y, c = me
        sib = (x, y, 1 - c)
        chips = [(1 - x, y), (x, 1 - y), (1 - x, 1 - y)]

        def cp(a, k, block, to, src=None):
            dst = outs[a].at[_lin(block)]
            return pltpu.make_async_remote_copy(src_ref=dst if src is None else src, dst_ref=dst,
                                                send_sem=send_sems.at[a * 7 + k], recv_sem=recv_sems.at[a * 7 + k],
                                                device_id=to, device_id_type=MESH)

        mine = [pltpu.make_async_copy(ins[a], outs[a].at[_lin(me)], local_sems.at[a]) for a in range(n)]
        for m in mine:
            m.start()
        first = []
        for a in range(n):
            first.append(cp(a, 0, me, sib, src=ins[a]))
            first += [cp(a, 1 + j, me, (*chip, c), src=ins[a]) for j, chip in enumerate(chips)]
        for f in first:
            f.start()
        passed = []
        for j, chip in enumerate(chips):
            for a in range(n):
                cp(a, 1 + j, (*chip, c), me).wait_recv()
                fwd = cp(a, 4 + j, (*chip, c), sib)
                fwd.start()
                passed.append(fwd)
        for a in range(n):
            cp(a, 0, sib, me).wait_recv()
            for j, chip in enumerate(chips):
                cp(a, 4 + j, (*chip, 1 - c), me).wait_recv()
        for f in first + passed:
            f.wait_send()
        for m in mine:
            m.wait()

    return pl.pallas_call(
        body, name=name,
        in_specs=[ANY] * n_in, out_specs=[ANY] * n,
        out_shape=[jax.ShapeDtypeStruct((N_DEV, *s.shape), s.dtype) for s in shards],
        scratch_shapes=[pltpu.SemaphoreType.DMA((7 * n,)), pltpu.SemaphoreType.DMA((7 * n,)), pltpu.SemaphoreType.DMA((n,))],
    )(*shards, *deps)


HBM =pl.BlockSpec(memory_space=pltpu.HBM)
SEM = pl.BlockSpec(memory_space=pltpu.SEMAPHORE)
DATAFLOW = pltpu.SideEffectType.DATAFLOW_SIDE_EFFECTING


def _peer_copies(kind, srcs, lands, send_sems, recv_sems):
    me = _me()
    copies = []
    for a, (src, land) in enumerate(zip(srcs, lands)):
        for k in range(1, N_DEV):
            peer = _flip(me, k)
            copies.append(pltpu.make_async_remote_copy(
                src_ref=src if kind == "gather" else src.at[_lin(peer)], dst_ref=land.at[_lin(me)],
                send_sem=send_sems.at[a * 7 + k - 1], recv_sem=recv_sems.at[a * 7 + k - 1],
                device_id=peer, device_id_type=MESH))
    return copies


def _copies_start(kind, srcs, name, after=None):
    n = len(srcs)
    extra = [] if after is None else [after]
    land_shapes = [((N_DEV, *s.shape) if kind == "gather" else s.shape) for s in srcs]

    def body(*refs):
        sems = refs[2 * n + len(extra):]
        for cp in _peer_copies(kind, refs[:n], refs[n:2 * n], sems[0], sems[1]):
            cp.start()
        refs[-1][...] = jnp.zeros_like(refs[-1])

    def hbm(a):
        return pltpu.with_memory_space_constraint(a, pltpu.HBM)

    out = pl.pallas_call(
        body, name=name,
        out_shape=(pltpu.SemaphoreType.DMA((7 * n,)), pltpu.SemaphoreType.DMA((7 * n,)),
                   *[pltpu.HBM(s.shape, s.dtype) for s in srcs],
                   *[pltpu.HBM(ls, s.dtype) for ls, s in zip(land_shapes, srcs)],
                   jax.ShapeDtypeStruct((8, LANES), F32)),
        in_specs=[HBM] * (2 * n) + [ANY] * len(extra),
        out_specs=(SEM, SEM, *[HBM] * (2 * n), pl.BlockSpec(memory_space=pltpu.VMEM)),
        input_output_aliases={i: 2 + i for i in range(2 * n)},
        compiler_params=pltpu.CompilerParams(has_side_effects=DATAFLOW),
    )(*[hbm(s) for s in srcs], *[hbm(lax.empty(ls, s.dtype)) for ls, s in zip(land_shapes, srcs)], *extra)
    return (kind, n, out[:-1]), out[-1]


def _copies_wait(state, after, name):
    kind, n, (send_sems, recv_sems, *thru) = state
    after = list(after) if isinstance(after, (list, tuple)) else [after]

    def body(*refs):
        for cp in _peer_copies(kind, refs[:n], refs[n:2 * n], refs[2 * n], refs[2 * n + 1]):
            cp.wait_send()
            cp.wait_recv()

    out = pl.pallas_call(
        body, name=name,
        out_shape=tuple(pltpu.HBM(t.shape, t.dtype) for t in thru),
        in_specs=[HBM] * (2 * n) + [SEM, SEM] + [ANY] * len(after), out_specs=tuple([HBM] * (2 * n)),
        input_output_aliases={i: i for i in range(2 * n)},
        compiler_params=pltpu.CompilerParams(has_side_effects=DATAFLOW),
    )(*thru, send_sems, recv_sems, *after)
    return out[:n], out[n:]


def _adamw(w, g, m, v):
    m2 = ADAM_B1 * m + (1.0 - ADAM_B1) * g
    v2 = ADAM_B2 * v + (1.0 - ADAM_B2) * (g * g)
    m_hat = m2 / (1.0 - ADAM_B1 ** ADAM_STEP)
    v_hat = v2 / (1.0 - ADAM_B2 ** ADAM_STEP)
    delta = -ADAM_LR * (m_hat / (jnp.sqrt(v_hat) + ADAM_EPS) + ADAM_WD * w)
    return delta, m2, v2


def _sum_adamw(land, part, me_idx, w, m, v, name, tile=256):
    r, c = w.shape
    tr = min(tile, r)

    def body(me_ref, own_ref, *refs):
        slots = refs[:N_DEV]
        w_ref, m_ref, v_ref, g_ref, d_ref, m2_ref, v2_ref = refs[N_DEV:]
        own = own_ref[...].astype(F32)
        g = None
        for s in range(N_DEV):
            term = jnp.where(me_ref[0] == s, own, slots[s][...].astype(F32))
            g = term if g is None else g + term
        d, m2, v2 = _adamw(w_ref[...], g, m_ref[...], v_ref[...])
        g_ref[...] = g
        d_ref[...] = d
        m2_ref[...] = m2
        v2_ref[...] = v2

    def slot_spec(s):
        return pl.BlockSpec((None, tr, c), lambda i, me: (jnp.where(me[0] == s, (s + 1) % N_DEV, s), i, 0))

    row = pl.BlockSpec((tr, c), lambda i, me: (i, 0))
    return pl.pallas_call(
        body, name=name,
        grid_spec=pltpu.PrefetchScalarGridSpec(
            num_scalar_prefetch=1, grid=(r // tr,),
            in_specs=[pl.BlockSpec((None, tr, c), lambda i, me: (me[0], i, 0))] + [slot_spec(s) for s in range(N_DEV)] + [row] * 3,
            out_specs=[row] * 4),
        out_shape=[jax.ShapeDtypeStruct((r, c), F32)] * 4,
        compiler_params=_cparams("parallel"),
    )(me_idx, part, *[land] * N_DEV, w, m, v)


def _sum_slots(gathered, name):
    _, r, w = gathered.shape

    def body(p_ref, o_ref):
        g = p_ref[0]
        for s in range(1, N_DEV):
            g = g + p_ref[s]
        o_ref[...] = g

    return pl.pallas_call(body, name=name, out_shape=jax.ShapeDtypeStruct((r, w), F32))(gathered)


def _pack(arrs):
    flat = jnp.concatenate([a.reshape(-1).astype(F32) for a in arrs])
    rows = -(-flat.shape[0] // (8 * LANES)) * 8
    return jnp.pad(flat, (0, rows * LANES - flat.shape[0])).reshape(rows, LANES)


def _unpack(packed, shapes):
    flat = packed.reshape(-1)
    out, off = [], 0
    for s in shapes:
        size = 1
        for d in s:
            size *= d
        out.append(flat[off:off + size].reshape(s))
        off += size
    return out


def _to_hm(a, d):
    t = a.shape[0]
    return a.reshape(t, HEADS, d).transpose(1, 0, 2)


def _from_hm(a):
    h, t, d = a.shape
    return a.transpose(1, 0, 2).reshape(t, h * d)


def _gate_rows(g):
    t = g.shape[0]
    return g.T.reshape(HEADS, t // CHUNK, 1, CHUNK)


def _gate_cols(g):
    h, nc, _, c = g.shape
    return g.reshape(h, nc * c).T


def _blockdiag_dense(w):
    n = w.shape[0] * QKV_BLOCK
    tiled = jnp.tile(w.reshape(n, QKV_BLOCK), (1, n // QKV_BLOCK))
    r = lax.broadcasted_iota(jnp.int32, (n, n), 0)
    c = lax.broadcasted_iota(jnp.int32, (n, n), 1)
    return jnp.where(r // QKV_BLOCK == c // QKV_BLOCK, tiled, 0.0)


def _blockdiag_blocks(dense):
    n = dense[0].shape[0]
    k = len(dense)

    def body(*refs):
        r = lax.broadcasted_iota(jnp.int32, (n, n), 0)
        c = lax.broadcasted_iota(jnp.int32, (n, n), 1)
        fr = lax.broadcasted_iota(jnp.int32, (n, LANES), 0)
        fc = lax.broadcasted_iota(jnp.int32, (n, LANES), 1)
        fold = ((fr & (QKV_BLOCK - 1)) == fc).astype(BF16)
        for i in range(k):
            kept = jnp.where((r >> 2) == (c >> 2), refs[i][...], 0.0)
            refs[k + i][...] = sum(lax.dot_general(t, fold, _DN["nn"], preferred_element_type=F32) for t in _split3(kept))

    out = pl.pallas_call(body, name="blockdiag_blocks", out_shape=[jax.ShapeDtypeStruct((n, LANES), F32)] * k)(*dense)
    return [o[:, 0:QKV_BLOCK].reshape(n // QKV_BLOCK, QKV_BLOCK, QKV_BLOCK) for o in out]


def _col_blocks(w):
    k, n = w.shape
    return w.reshape(k, N_DEV, n // N_DEV).transpose(1, 0, 2)


def _from_col_blocks(g):
    d, k, n = g.shape
    return g.transpose(1, 0, 2).reshape(k, d * n)


def _local_step(x, tgt, weight, ws, prefetch, on_grads, on_small):
    t, d = x.shape
    g1 = ws["g_pre_mix"]

    def dep(token):
        return () if token is None else (token,)

    (h,) = _rowwise("pre_mix_norm", lambda xv, g: ((_rms(xv, g),), ()), [x], [g1], [(d, BF16)])
    fetch_mix = prefetch(("w_pa", "w_pb", "w_o"), h)
    proj = _mm_shards(h, weight("w_in", h), "proj_in", tm=2048)
    fetch_up = prefetch(("w_up",), proj)
    offs = [0]
    for s in IN_SPLITS:
        offs.append(offs[-1] + s)
    q_a, k_a, v_a, g_a, a_low, x_m, o_pre, gate_a, gate_b = [_shard_columns(proj, offs[i], offs[i + 1]) for i in range(9)]

    a_low_p = jnp.pad(a_low, ((0, 0), (0, LANES - LOWRANK)))
    w_a_up_p = jnp.pad(ws["w_a_up"], ((0, LANES - LOWRANK), (0, 0)))
    b_a_up = ws["b_a_up"]
    (la,) = _rowwise("gla_decay", lambda al, w, b: ((_log_decay(al, w, b),), ()), [a_low_p], [w_a_up_p, b_a_up],
                     [(HEADS * GLA_DK, F32)], deps=dep(fetch_mix) + dep(fetch_up))
    q_hm, k_hm, la_hm = _to_hm(q_a, GLA_DK), _to_hm(k_a, GLA_DK), _to_hm(la, GLA_DK)
    o_gla, s_prev = _gla_fwd(q_hm, k_hm, v_a, la_hm)
    fetch_down = prefetch(("w_down",), o_gla)
    gn = ws["g_gla_norm"]
    (ya_in,) = _rowwise("gla_out", lambda o, g, n_: ((jnp.concatenate(_per_head(_gla_out, [o, g], [], [n_]), axis=1),), ()),
                        [o_gla, g_a], [gn], [(HEADS * HEAD_W, BF16)], deps=dep(fetch_down))
    y_a = _mm(ya_in, weight("w_pa", ya_in), "nn", F32, "proj_a")

    cw = ws["conv_w"]
    w_if_p = jnp.pad(ws["w_if"], ((0, 0), (0, LANES - 2 * HEADS)))
    ml_w = HEADS * HEAD_W
    pre_params = [cw[0:1], cw[1:2], cw[2:3], cw[3:4], ws["conv_b"],
                  _blockdiag_dense(ws["w_q_ml"]), _blockdiag_dense(ws["w_k_ml"]), _blockdiag_dense(ws["w_v_ml"]),
                  w_if_p[0:ml_w], w_if_p[ml_w:2 * ml_w], w_if_p[2 * ml_w:3 * ml_w],
                  jnp.pad(ws["b_if"], ((0, 0), (0, LANES - 2 * HEADS)))]
    x_pad = jnp.pad(x_m, ((HALO, 0), (0, 0)))
    xc, q_m, k_m, v_m, gl = _ml_pre_fwd(x_m, x_pad, pre_params)
    li, lf = _gate_rows(gl[:, 0:HEADS]), _gate_rows(gl[:, HEADS:2 * HEADS])
    hc, c_prev, n_prev, m_prev = _ml_fwd(q_m, k_m, v_m, li, lf)
    g_ml, skip = ws["g_ml_norm"], ws["ml_skip"]
    (h_b,) = _rowwise("mlstm_out", lambda a, b, c_, g, s: ((jnp.concatenate(_per_head(_ml_out, [a, b, c_], [g, s]), axis=1),), ()),
                      [hc, o_pre, xc], [g_ml, skip], [(ml_w, BF16)])
    y_b = _mm(h_b, weight("w_pb", h_b), "nn", F32, "proj_b")

    (merged,) = _rowwise("merge", lambda ga, gb, ya, yb: ((_merge(ga, gb, ya, yb),), ()), [gate_a, gate_b, y_a, y_b], [],
                         [(d, BF16)])
    z = _mm(merged, weight("w_o", merged), "nn", F32, "proj_o")
    gpm, gpl, gpo = ws["g_post_mix"], ws["g_pre_mlp"], ws["g_post_mlp"]
    x1, h2 = _rowwise("post_mix", lambda xv, zv, a, b: (_post_mix(xv, zv, a, b), ()), [x, z], [gpm, gpl], [(d, F32), (d, BF16)])
    up, u = _mm(h2, weight("w_up", h2), "nn", (F32, BF16), "mlp_up", epilogue=lambda p: (p, jnp.square(jnp.maximum(p, 0.0))))
    dn = _mm(u, weight("w_down", u), "nn", F32, "mlp_down", tm=512)

    def loss_and_grads(x1v, dnv, tgtv, g):
        loss, vjp = jax.vjp(lambda a, b, c_: _loss_rows(a, b, tgtv, c_), x1v, dnv, g)
        dx1, ddn, dg = vjp(jnp.ones((1, 1), F32))
        return (dx1, ddn), (jnp.broadcast_to(loss, (1, LANES)), dg)

    dx1_y, d_dn, loss, d_gpo = _rowwise("loss", loss_and_grads, [x1, dn, tgt], [gpo], [(d, F32), (d, BF16)],
                                        [((1, LANES), F32), ((1, d), F32)])

    (d_up,) = _mm(d_dn, weight("w_down", u), "nt", (BF16,), "mlp_down_dx", extra=[up],
                  epilogue=lambda p, a: (p * (2.0 * jnp.maximum(a, 0.0)),))
    dw_down = _mm(u, d_dn, "tn", BF16, "mlp_down_dw", tm=512)
    d_h2 = _mm(d_up, weight("w_up", h2), "nt", F32, "mlp_up_dx", tm=512)
    dw_up = _mm(h2, d_up, "tn", BF16, "mlp_up_dw")
    sent_mlp = on_grads(dict(w_down=dw_down, w_up=dw_up))

    def post_mix_bwd(xv, zv, dx1, dh2, a, b):
        _, vjp = jax.vjp(_post_mix, xv, zv, a, b)
        dx, dz, da, db = vjp((dx1, dh2))
        return (dx, dz), (da, db)

    dx_res, d_z, d_gpm, d_gpl = _rowwise("post_mix_bwd", post_mix_bwd, [x, z, dx1_y, d_h2], [gpm, gpl],
                                         [(d, F32), (d, BF16)], [((1, d), F32), ((1, d), F32)], deps=dep(sent_mlp))
    d_merged = _mm(d_z, weight("w_o", merged), "nt", F32, "proj_o_dx")
    dw_o = _mm(merged, d_z, "tn", BF16, "proj_o_dw")
    d_ga, d_gb, d_ya, d_yb = _rowwise("merge_bwd", lambda *v: (jax.vjp(_merge, *v[:4])[1](v[4]), ()),
                                      [gate_a, gate_b, y_a, y_b, d_merged], [], [(d, F32), (d, F32), (d, BF16), (d, BF16)])
    d_ya_in = _mm(d_ya, weight("w_pa", ya_in), "nt", F32, "proj_a_dx")
    dw_pa = _mm(ya_in, d_ya, "tn", BF16, "proj_a_dw")
    d_hb = _mm(d_yb, weight("w_pb", h_b), "nt", F32, "proj_b_dx")
    dw_pb = _mm(h_b, d_yb, "tn", BF16, "proj_b_dw")
    sent_mix = on_grads(dict(w_o=dw_o, w_pa=dw_pa, w_pb=dw_pb))

    def ml_out_bwd(a, b, c_, ct, g, s):
        parts = []
        for hs in _head_slices(HEAD_W):
            _, vjp = jax.vjp(_ml_out, a[:, hs], b[:, hs], c_[:, hs], g[:, hs], s[:, hs])
            parts.append(vjp(ct[:, hs]))
        cat = lambda i: jnp.concatenate([p[i] for p in parts], axis=1)
        return (cat(0), cat(1), cat(2)), (cat(3), cat(4))

    d_hc, d_opre, d_xc, d_gml, d_skip = _rowwise("mlstm_out_bwd", ml_out_bwd, [hc, o_pre, xc, d_hb], [g_ml, skip],
                                                 [(ml_w, F32)] * 3, [((1, ml_w), F32)] * 2, deps=dep(sent_mix))
    d_qm, d_km, d_vm, d_li, d_lf = _ml_bwd(q_m, k_m, v_m, li, lf, c_prev, n_prev, m_prev, d_hc)
    d_gl = jnp.concatenate([_gate_cols(d_li), _gate_cols(d_lf), jnp.zeros((t, LANES - 2 * HEADS), F32)], axis=1)
    pre_grads = _ml_pre_bwd(x_m, x_pad, pre_params, [d_xc, d_qm, d_km, d_vm, d_gl])
    d_xm = pre_grads[0]
    d_cw = jnp.concatenate(pre_grads[1:5], axis=0)
    d_cb = pre_grads[5]
    d_wq, d_wk, d_wv = _blockdiag_blocks(pre_grads[6:9])
    d_wif = jnp.concatenate(pre_grads[9:12], axis=0)[:, 0:2 * HEADS]
    d_bif = pre_grads[12][:, 0:2 * HEADS]

    def gla_out_bwd(o, g, ct, n_):
        parts = []
        for hs in _head_slices(HEAD_W):
            _, vjp = jax.vjp(_gla_out, o[:, hs], g[:, hs], n_)
            parts.append(vjp(ct[:, hs]))
        cat = lambda i: jnp.concatenate([p[i] for p in parts], axis=1)
        return (cat(0), cat(1)), (sum(p[2] for p in parts),)

    d_o, d_g_a, d_gn = _rowwise("gla_out_bwd", gla_out_bwd, [o_gla, g_a, d_ya_in], [gn], [(ml_w, F32)] * 2, [((1, HEAD_W), F32)])
    dq_hm, dk_hm, d_va, dla_hm = _gla_bwd(q_hm, k_hm, v_a, la_hm, s_prev, d_o)

    def decay_bwd(al, ct, w, b):
        _, vjp = jax.vjp(_log_decay, al, w, b)
        dal, dw, db = vjp(ct)
        return (dal,), (dw, db)

    d_alow_p, d_wa_p, d_ba = _rowwise("gla_decay_bwd", decay_bwd, [a_low_p, _from_hm(dla_hm)], [w_a_up_p, b_a_up],
                                      [(LANES, F32)], [(w_a_up_p.shape, F32), (b_a_up.shape, F32)])
    d_proj = _to_shards([_from_hm(dq_hm), _from_hm(dk_hm), d_va, d_g_a, d_alow_p[:, 0:LOWRANK], d_xm, d_opre, d_ga, d_gb],
                        proj.shape[2])
    d_h = _mm_shards_dx(d_proj, weight("w_in", h), "proj_in_dx", tm=1024)

    def pre_mix_bwd(xv, dh, dres, g):
        _, vjp = jax.vjp(_rms, xv, g)
        dx, dg = vjp(dh)
        return (dx + dres,), (dg,)

    grad_x, d_g1 = _rowwise("pre_mix_norm_bwd", pre_mix_bwd, [x, d_h, dx_res], [g1], [(d, F32)], [((1, d), F32)])
    small = dict(g_pre_mix=d_g1, w_a_up=d_wa_p[0:LOWRANK], b_a_up=d_ba, g_gla_norm=d_gn, conv_w=d_cw, conv_b=d_cb,
                 w_q_ml=d_wq, w_k_ml=d_wk, w_v_ml=d_wv, w_if=d_wif, b_if=d_bif, ml_skip=d_skip, g_ml_norm=d_gml,
                 g_post_mix=d_gpm, g_pre_mlp=d_gpl, g_post_mlp=d_gpo)
    sent_small = on_small(small, loss[:, 0:1])
    dw_in = _mm_shards_dw(h, d_proj, "proj_in_dw", deps=dep(sent_small))
    return grad_x, on_grads(dict(w_in=dw_in))


BIG = ("w_in", "w_pa", "w_pb", "w_o", "w_up", "w_down")
BIG_COL_SHARDED = ("w_in", "w_pa", "w_pb", "w_up")
SMALL_SHARDED = {"w_a_up": 1, "conv_w": 1, "w_if": 0}
SMALL = ("g_pre_mix", "w_a_up", "b_a_up", "g_gla_norm", "conv_w", "conv_b", "w_q_ml", "w_k_ml", "w_v_ml", "w_if", "b_if",
         "ml_skip", "g_ml_norm", "g_post_mix", "g_pre_mlp", "g_post_mlp")
WEIGHTS = ("g_pre_mix", "w_in", "w_a_up", "b_a_up", "g_gla_norm", "conv_w", "conv_b", "w_q_ml", "w_k_ml", "w_v_ml", "w_if", "b_if",
           "ml_skip", "g_ml_norm", "w_pa", "w_pb", "w_o", "g_post_mix", "g_pre_mlp", "w_up", "w_down", "g_post_mlp")


def _my_slice(a, axis):
    n = a.shape[axis] // N_DEV
    return lax.dynamic_slice_in_dim(a, _lin(_me()) * n, n, axis)


def kernel(x, g_pre_mix, w_in, w_a_up, b_a_up, g_gla_norm, conv_w, conv_b, w_q_ml, w_k_ml, w_v_ml, w_if, b_if, ml_skip, g_ml_norm, w_pa, w_pb, w_o, g_post_mix, g_pre_mlp, w_up, w_down, g_post_mlp, loss_target, m_g_pre_mix, m_w_in, m_w_a_up, m_b_a_up, m_g_gla_norm, m_conv_w, m_conv_b, m_w_q_ml, m_w_k_ml, m_w_v_ml, m_w_if, m_b_if, m_ml_skip, m_g_ml_norm, m_w_pa, m_w_pb, m_w_o, m_g_post_mix, m_g_pre_mlp, m_w_up, m_w_down, m_g_post_mlp, v_g_pre_mix, v_w_in, v_w_a_up, v_b_a_up, v_g_gla_norm, v_conv_w, v_conv_b, v_w_q_ml, v_w_k_ml, v_w_v_ml, v_w_if, v_b_if, v_ml_skip, v_g_ml_norm, v_w_pa, v_w_pb, v_w_o, v_g_post_mix, v_g_pre_mlp, v_w_up, v_w_down, v_g_post_mlp):
    args = dict(locals())
    w = {n: args[n][0] for n in WEIGHTS}
    m = {n: args["m_" + n][0] for n in WEIGHTS}
    v = {n: args["v_" + n][0] for n in WEIGHTS}

    me_lin = _lin(_me())
    me_idx = jnp.reshape(me_lin, (1,)).astype(jnp.int32)

    def full_weight(n, g):
        if n == "w_in":
            return g
        return _from_col_blocks(g) if n in BIG_COL_SHARDED else g.reshape(-1, g.shape[-1])

    def grad_parts(n, g):
        if n == "w_in":
            return g
        return (_col_blocks(g) if n in BIG_COL_SHARDED else g.reshape(N_DEV, -1, g.shape[-1])).astype(BF16)

    sharded_names = tuple(SMALL_SHARDED)
    small_w = _pack([w[n] for n in sharded_names])
    small_w_state, small_w_token = _copies_start("gather", [small_w], "allgather_start_small_weights")
    ready = {"w_in": full_weight("w_in", _allgather_big([w["w_in"].astype(BF16)], "allgather_w_in", [small_w_token])[0])}
    pending = {}

    def prefetch(group, after):
        state, token = _copies_start("gather", [w[n].astype(BF16) for n in group], "allgather_start_" + group[0], after)
        for n in group:
            pending[n] = (group, state)
        return token

    def weight(n, after):
        if n not in ready:
            group, state = pending[n]
            shards, lands = _copies_wait(state, after, "allgather_wait_" + group[0])
            for gn, shard, land in zip(group, shards, lands):
                ready[gn] = full_weight(gn, lax.dynamic_update_slice(land, shard[None], (me_lin, 0, 0)))
        return ready[n]

    (small_w,), (small_w_land,) = _copies_wait(small_w_state, ready["w_in"], "allgather_wait_small_weights")
    small_g = lax.dynamic_update_slice(small_w_land, small_w[None], (me_lin, 0, 0))
    ws = {n: (w[n].reshape(1, -1) if w[n].ndim == 1 else w[n]) for n in SMALL if n not in SMALL_SHARDED}
    per_dev = [_unpack(small_g[dev], [w[n].shape for n in sharded_names]) for dev in range(N_DEV)]
    for i, n in enumerate(sharded_names):
        ws[n] = jnp.concatenate([per_dev[dev][i] for dev in range(N_DEV)], axis=SMALL_SHARDED[n])

    sent = []

    def on_grads(grads):
        names = tuple(grads)
        state, token = _copies_start("exchange", [grad_parts(n, grads[n]) for n in names], "exchange_start_" + names[0])
        sent.append((names, state))
        return token

    small_sent = {}

    def on_small(small, loss):
        small_sent["shapes"] = [small[n].shape for n in SMALL] + [(1, 1)]
        small_sent["state"], token = _copies_start("gather", [_pack([small[n] for n in SMALL] + [loss])],
                                                   "allgather_start_small_grads")
        return token

    grad_x, last_token = _local_step(x[0], loss_target[0], weight, ws, prefetch, on_grads, on_small)

    out = {}

    def finish(names, state, after):
        parts, lands = _copies_wait(state, after, "exchange_wait_" + names[0])
        for n, part, land in zip(names, parts, lands):
            out[n] = _sum_adamw(land, part, me_idx, w[n], m[n], v[n], "adamw_" + n)

    for names, state in sent[:-1]:
        finish(names, state, [grad_x, last_token])

    (small_vec,), (small_land,) = _copies_wait(small_sent["state"], [grad_x, last_token], "allgather_wait_small_grads")
    vec = _sum_slots(lax.dynamic_update_slice(small_land, small_vec[None], (me_lin, 0, 0)), "sum_small_grads")
    summed = _unpack(vec, small_sent["shapes"])
    g_small = {}
    for n, g in zip(SMALL, summed[:-1]):
        g_small[n] = _my_slice(g, SMALL_SHARDED[n]) if n in SMALL_SHARDED else g
    shapes = [g_small[n].shape for n in SMALL]
    packed = [_pack([d[n].reshape(g_small[n].shape) for n in SMALL]) for d in (w, g_small, m, v)]
    upd = _rowwise("adamw_small", lambda a, b, c_, d_: (_adamw(a, b, c_, d_), ()), packed, [], [(LANES, F32)] * 3, tile=8 * 1024)
    deltas, new_ms, new_vs = [_unpack(p, shapes) for p in upd]
    for i, n in enumerate(SMALL):
        out[n] = (g_small[n], deltas[i], new_ms[i], new_vs[i])
    finish(*sent[-1], [upd[0]] + [out[n][1] for n in BIG if n in out])

    shaped = lambda a, n: a.reshape(args[n].shape)
    return (summed[-1].reshape(()), grad_x[None],
            *[shaped(out[n][0], n) for n in WEIGHTS], *[shaped(out[n][1], n) for n in WEIGHTS],
            *[shaped(out[n][2], n) for n in WEIGHTS], *[shaped(out[n][3], n) for n in WEIGHTS])
```

```python
import functools

import jax
import jax.numpy as jnp
from jax import lax
from jax.experimental import pallas as pl
from jax.experimental.pallas import tpu as pltpu

F32 = jnp.float32
BF16 = jnp.bfloat16
MESH = pl.DeviceIdType.MESH

N_DEV = 8
EPS = 1e-6
CHUNK = 64
HEADS = 4
GLA_DK = 64
HEAD_W = 128
GLA_GATE_NORM = 16.0
LOWRANK = 16
CONV_K = 4
QKV_BLOCK = 4
LANES = 128
HALO = 8
IN_SPLITS = (256, 256, 512, 512, 16, 512, 512, 1024, 1024)

ADAM_LR = 0.001
ADAM_B1 = 0.9
ADAM_B2 = 0.999
ADAM_EPS = 1e-08
ADAM_WD = 0.01
ADAM_STEP = 10

VMEM_LIMIT = 56 * 1024 * 1024


def _cparams(*sem):
    return pltpu.CompilerParams(dimension_semantics=sem, vmem_limit_bytes=VMEM_LIMIT)


_DN = {"nn": (((1,), (0,)), ((), ())), "nt": (((1,), (1,)), ((), ())), "tn": (((0,), (0,)), ((), ()))}


def _raw_dot(a, b, mode):
    return lax.dot_general(a.astype(BF16), b.astype(BF16), _DN[mode], preferred_element_type=F32)


@functools.partial(jax.custom_vjp, nondiff_argnums=(2,))
def _bdot(a, b, mode):
    return _raw_dot(a, b, mode)


def _bdot_fwd(a, b, mode):
    return _raw_dot(a, b, mode), (a, b)


def _bdot_bwd(mode, res, ct):
    a, b = res
    if mode == "nn":
        da, db = _raw_dot(ct, b, "nt"), _raw_dot(a, ct, "tn")
    elif mode == "nt":
        da, db = _raw_dot(ct, b, "nn"), _raw_dot(ct, a, "tn")
    else:
        da, db = _raw_dot(b, ct, "nt"), _raw_dot(a, ct, "nn")
    return da.astype(a.dtype), db.astype(b.dtype)


_bdot.defvjp(_bdot_fwd, _bdot_bwd)


def _split3(x):
    hi = x.astype(BF16)
    r1 = x - hi.astype(F32)
    mid = r1.astype(BF16)
    return hi, mid, (r1 - mid.astype(F32)).astype(BF16)


def _split_dot(tri, x):
    return sum(lax.dot_general(tri, t, _DN["nn"], preferred_element_type=F32) for t in _split3(x))


def _tri(n, lower):
    r = lax.broadcasted_iota(jnp.int32, (n, n), 0)
    c = lax.broadcasted_iota(jnp.int32, (n, n), 1)
    return ((c <= r) if lower else (c >= r)).astype(BF16)


@jax.custom_vjp
def _cumsum_rows(x):
    return _split_dot(_tri(x.shape[0], True), x)


def _cumsum_rows_fwd(x):
    return _cumsum_rows(x), None


def _cumsum_rows_bwd(_, ct):
    return (_split_dot(_tri(ct.shape[0], False), ct),)


_cumsum_rows.defvjp(_cumsum_rows_fwd, _cumsum_rows_bwd)


def _abs(x):
    return jnp.where(x >= 0, x, -x)


def _sigmoid(x):
    return lax.logistic(x)


def _log_sigmoid(x):
    return jnp.minimum(x, 0.0) - jnp.log(1.0 + jnp.exp(-_abs(x)))


def _rms(x, g):
    return x * lax.rsqrt(jnp.mean(x * x, axis=-1, keepdims=True) + EPS) * g


def _head_slices(w):
    return [slice(h * w, (h + 1) * w) for h in range(HEADS)]


def _tile(dim, want):
    if dim <= want or dim % LANES:
        return dim
    t = want
    while dim % t:
        t -= LANES
    return t


def _mm(a, b, mode, out_dtype, name, tm=1024, tn=1024, tk=4096, epilogue=None, extra=(), deps=()):
    if mode == "nn":
        (m, k), (k2, n) = a.shape, b.shape
    elif mode == "nt":
        (m, k), (n, k2) = a.shape, b.shape
    else:
        (k, m), (k2, n) = a.shape, b.shape
    assert k == k2, (name, a.shape, b.shape)
    tm, tn, tk = _tile(m, tm), _tile(n, tn), _tile(k, tk)
    nk = k // tk
    out_dtypes = out_dtype if epilogue else (out_dtype,)
    assert nk == 1 or (out_dtype == F32 and not epilogue), name
    n_in = 2 + len(extra)

    def body(*refs):
        p = _raw_dot(refs[0][...], refs[1][...], mode)
        if nk > 1:
            _accumulate(pl.program_id(2), [refs[n_in + len(deps)]], [p])
            return
        outs = epilogue(p, *[r[...] for r in refs[2:n_in]]) if epilogue else (p,)
        for ref, val in zip(refs[n_in + len(deps):], outs):
            ref[...] = val.astype(ref.dtype)

    a_spec = pl.BlockSpec((tk, tm), lambda i, j, kk: (kk, i)) if mode == "tn" else pl.BlockSpec((tm, tk), lambda i, j, kk: (i, kk))
    b_spec = pl.BlockSpec((tn, tk), lambda i, j, kk: (j, kk)) if mode == "nt" else pl.BlockSpec((tk, tn), lambda i, j, kk: (kk, j))
    o_spec = pl.BlockSpec((tm, tn), lambda i, j, kk: (i, j))
    res = pl.pallas_call(
        body, name=name, grid=(m // tm, n // tn, nk),
        in_specs=[a_spec, b_spec] + [o_spec] * len(extra) + [ANY] * len(deps), out_specs=[o_spec] * len(out_dtypes),
        out_shape=[jax.ShapeDtypeStruct((m, n), dt) for dt in out_dtypes],
        compiler_params=_cparams("parallel", "parallel", "arbitrary"),
    )(a, b, *extra, *deps)
    return res if epilogue else res[0]


def _mm_shards(a, wg, name, tm=512):
    t, k = a.shape
    nb, _, n = wg.shape
    tm = _tile(t, tm)

    def body(a_ref, w_ref, o_ref):
        o_ref[...] = _raw_dot(a_ref[...], w_ref[...], "nn")

    return pl.pallas_call(
        body, name=name, grid=(nb, t // tm),
        in_specs=[pl.BlockSpec((tm, k), lambda j, i: (i, 0)), pl.BlockSpec((None, k, n), lambda j, i: (j, 0, 0))],
        out_specs=pl.BlockSpec((None, tm, n), lambda j, i: (j, i, 0)),
        out_shape=jax.ShapeDtypeStruct((nb, t, n), F32),
        compiler_params=_cparams("parallel", "parallel"),
    )(a, wg)


def _mm_shards_dx(dy, wg, name, tm=512):
    nb, t, n = dy.shape
    k = wg.shape[1]
    tm = _tile(t, tm)

    def body(d_ref, w_ref, o_ref):
        _accumulate(pl.program_id(1), [o_ref], [_raw_dot(d_ref[...], w_ref[...], "nt")])

    return pl.pallas_call(
        body, name=name, grid=(t // tm, nb),
        in_specs=[pl.BlockSpec((None, tm, n), lambda i, j: (j, i, 0)), pl.BlockSpec((None, k, n), lambda i, j: (j, 0, 0))],
        out_specs=pl.BlockSpec((tm, k), lambda i, j: (i, 0)),
        out_shape=jax.ShapeDtypeStruct((t, k), F32),
        compiler_params=_cparams("parallel", "arbitrary"),
    )(dy, wg)


def _mm_shards_dw(a, dy, name, deps=()):
    t, k = a.shape
    nb, _, n = dy.shape

    def body(a_ref, d_ref, *rest):
        rest[-1][...] = _raw_dot(a_ref[...], d_ref[...], "tn").astype(BF16)

    return pl.pallas_call(
        body, name=name, grid=(nb,),
        in_specs=[pl.BlockSpec((t, k), lambda j: (0, 0)), pl.BlockSpec((None, t, n), lambda j: (j, 0, 0))] + [ANY] * len(deps),
        out_specs=pl.BlockSpec((None, k, n), lambda j: (j, 0, 0)),
        out_shape=jax.ShapeDtypeStruct((nb, k, n), BF16),
        compiler_params=_cparams("parallel"),
    )(a, dy, *deps)


def _shard_columns(arr, start, stop):
    n = arr.shape[2]
    pieces = [arr[j][:, max(start, j * n) - j * n:min(stop, (j + 1) * n) - j * n]
              for j in range(start // n, (stop - 1) // n + 1)]
    return pieces[0] if len(pieces) == 1 else jnp.concatenate(pieces, axis=1)


def _to_shards(segments, n):
    bounds = [0]
    for s in segments:
        bounds.append(bounds[-1] + s.shape[1])
    shards = []
    for j in range(bounds[-1] // n):
        lo, hi = j * n, (j + 1) * n
        pieces = [s[:, max(lo, b) - b:min(hi, b + s.shape[1]) - b].astype(BF16)
                  for s, b in zip(segments, bounds) if b < hi and b + s.shape[1] > lo]
        shards.append(jnp.concatenate(pieces, axis=1))
    return jnp.stack(shards)


def _rowwise(name, fn, rows, params, out_rows, out_accs=(), tile=256, deps=()):
    t = rows[0].shape[0]
    r = min(tile, t)
    assert t % r == 0
    n_in, n_or = len(rows) + len(params), len(out_rows)
    n_all = n_in + len(deps)
    params = list(params) + list(deps)

    def body(*refs):
        vals = [ref[...] for ref in refs[:n_in]]
        outs = refs[n_all:]
        ro, ao = fn(*vals)
        for ref, v in zip(outs[:n_or], ro):
            ref[...] = v.astype(ref.dtype)
        if out_accs:
            _accumulate(pl.program_id(0), outs[n_or:], ao)

    def full(shape):
        return pl.BlockSpec(shape, lambda i, nd=len(shape): (0,) * nd)

    return pl.pallas_call(
        body, name=name, grid=(t // r,),
        in_specs=[pl.BlockSpec((r, a.shape[1]), lambda i: (i, 0)) for a in rows] + [full(p.shape) for p in params],
        out_specs=[pl.BlockSpec((r, w), lambda i: (i, 0)) for w, _ in out_rows] + [full(s) for s, _ in out_accs],
        out_shape=[jax.ShapeDtypeStruct((t, w), dt) for w, dt in out_rows] + [jax.ShapeDtypeStruct(s, dt) for s, dt in out_accs],
        compiler_params=_cparams("arbitrary"),
    )(*rows, *params)


def _accumulate(step, refs, vals):
    for ref, v in zip(refs, vals):
        @pl.when(step == 0)
        def _(ref=ref, v=v):
            ref[...] = v.astype(ref.dtype)

        @pl.when(step > 0)
        def _(ref=ref, v=v):
            ref[...] += v.astype(ref.dtype)


def _gla_chunk(q, k, v, la, st):
    c = q.shape[0]
    row = lax.broadcasted_iota(jnp.int32, (c, c), 0)
    col = lax.broadcasted_iota(jnp.int32, (c, c), 1)
    cum = _cumsum_rows(la)
    cl = jnp.sum(la, axis=0, keepdims=True)
    ep = jnp.exp(cum)
    en = jnp.exp(-cum)
    qs = q * (GLA_DK ** -0.5)
    qp = qs * ep
    a_f = _bdot(qp, k * en, "nt")
    a_b = _bdot(qs * en, k * ep, "nt")
    sc = jnp.where(row >= col, a_f, a_b)
    o = _bdot(sc, v, "nn") + _bdot(qp, st, "nt")
    kd = k * jnp.exp(cl - cum)
    st_new = st * jnp.exp(cl) + _bdot(v, kd, "tn")
    return o, st_new


def _gla_specs(nc, rev):
    def ch(n):
        return (nc - 1 - n) if rev else n
    hm = pl.BlockSpec((HEADS, CHUNK, GLA_DK), lambda n: (0, ch(n), 0))
    tm = pl.BlockSpec((CHUNK, HEADS * HEAD_W), lambda n: (ch(n), 0))
    st = pl.BlockSpec((HEADS, None, HEAD_W, GLA_DK), lambda n: (0, ch(n), 0, 0))
    return hm, tm, st


def _gla_fwd(q, k, v, la):
    t = v.shape[0]
    nc = t // CHUNK
    hm, tm, st = _gla_specs(nc, False)

    def body(q_ref, k_ref, v_ref, la_ref, o_ref, sp_ref, st_ref):
        @pl.when(pl.program_id(0) == 0)
        def _():
            st_ref[...] = jnp.zeros_like(st_ref)

        for h, hs in enumerate(_head_slices(HEAD_W)):
            s = st_ref[h]
            sp_ref[h] = s
            o, s_new = _gla_chunk(q_ref[h], k_ref[h], v_ref[:, hs], la_ref[h], s)
            o_ref[:, hs] = o
            st_ref[h] = s_new

    return pl.pallas_call(
        body, name="gla_fwd", grid=(nc,),
        in_specs=[hm, hm, tm, hm], out_specs=[tm, st],
        out_shape=[jax.ShapeDtypeStruct((t, HEADS * HEAD_W), F32), jax.ShapeDtypeStruct((HEADS, nc, HEAD_W, GLA_DK), F32)],
        scratch_shapes=[pltpu.VMEM((HEADS, HEAD_W, GLA_DK), F32)],
        compiler_params=_cparams("arbitrary"),
    )(q, k, v, la)


def _gla_bwd(q, k, v, la, sp, do):
    t = v.shape[0]
    nc = t // CHUNK
    hm, tm, st = _gla_specs(nc, True)

    def body(q_ref, k_ref, v_ref, la_ref, sp_ref, do_ref, dq_ref, dk_ref, dv_ref, dla_ref, ds_ref):
        @pl.when(pl.program_id(0) == 0)
        def _():
            ds_ref[...] = jnp.zeros_like(ds_ref)

        for h, hs in enumerate(_head_slices(HEAD_W)):
            _, vjp = jax.vjp(_gla_chunk, q_ref[h], k_ref[h], v_ref[:, hs], la_ref[h], sp_ref[h])
            dq, dk, dv, dla, ds = vjp((do_ref[:, hs], ds_ref[h]))
            dq_ref[h] = dq
            dk_ref[h] = dk
            dv_ref[:, hs] = dv
            dla_ref[h] = dla
            ds_ref[h] = ds

    hm_shape = jax.ShapeDtypeStruct((HEADS, t, GLA_DK), F32)
    return pl.pallas_call(
        body, name="gla_bwd", grid=(nc,),
        in_specs=[hm, hm, tm, hm, st, tm], out_specs=[hm, hm, tm, hm],
        out_shape=[hm_shape, hm_shape, jax.ShapeDtypeStruct((t, HEADS * HEAD_W), F32), hm_shape],
        scratch_shapes=[pltpu.VMEM((HEADS, HEAD_W, GLA_DK), F32)],
        compiler_params=_cparams("arbitrary"),
    )(q, k, v, la, sp, do)


def _ml_chunk(q, k, v, li_r, lf_r, cm, nv, m):
    c = q.shape[0]
    row = lax.broadcasted_iota(jnp.int32, (c, c), 0)
    col = lax.broadcasted_iota(jnp.int32, (c, c), 1)
    eye = (row == col).astype(F32)
    li_c = jnp.sum(eye * li_r, axis=1, keepdims=True)
    lf_c = jnp.sum(eye * lf_r, axis=1, keepdims=True)
    fc_c = jnp.sum((col <= row).astype(F32) * lf_r, axis=1, keepdims=True)
    fc_r = jnp.sum((row <= col).astype(F32) * lf_c, axis=0, keepdims=True)
    f_last = jnp.sum(lf_r, axis=1, keepdims=True)
    kc = k * (HEAD_W ** -0.5)
    a_c = f_last - fc_c + li_c
    m_loc = jnp.max(a_c, axis=0, keepdims=True)
    kw = kc * jnp.exp(a_c - m_loc)
    c_chunk = _bdot(kw, v, "tn")
    n_chunk = jnp.sum(kw, axis=0, keepdims=True)
    m_new = jnp.maximum(f_last + m, m_loc)
    sp = jnp.exp(f_last + m - m_new)
    sl = jnp.exp(m_loc - m_new)
    cm_new = sp * cm + sl * c_chunk
    nv_new = sp * nv + sl * n_chunk
    log_d = li_r - _abs(fc_c - fc_r)
    g_inter = fc_c + m
    m_t = jnp.maximum(g_inter, jnp.max(log_d, axis=1, keepdims=True))
    s = _bdot(q, kc, "nt") * jnp.exp(log_d - m_t)
    sc = jnp.exp(g_inter - m_t)
    num = _bdot(s, v, "nn") + sc * _bdot(q, cm, "nn")
    den = jnp.sum(s, axis=1, keepdims=True) + sc * jnp.sum(q * nv, axis=1, keepdims=True)
    den = jnp.maximum(_abs(den), jnp.exp(-m_t))
    return num / den, cm_new, nv_new, m_new


def _ml_specs(nc, rev):
    def ch(n):
        return (nc - 1 - n) if rev else n
    tm = pl.BlockSpec((CHUNK, HEADS * HEAD_W), lambda n: (ch(n), 0))
    gate = pl.BlockSpec((HEADS, None, 1, CHUNK), lambda n: (0, ch(n), 0, 0))
    cm = pl.BlockSpec((HEADS, None, HEAD_W, HEAD_W), lambda n: (0, ch(n), 0, 0))
    vec = pl.BlockSpec((HEADS, None, 1, HEAD_W), lambda n: (0, ch(n), 0, 0))
    return tm, gate, cm, vec


_ML_STATE = [pltpu.VMEM((HEADS, HEAD_W, HEAD_W), F32), pltpu.VMEM((HEADS, 1, HEAD_W), F32), pltpu.VMEM((HEADS, 1, HEAD_W), F32)]


def _ml_fwd(q, k, v, li, lf):
    t = q.shape[0]
    nc = t // CHUNK
    tm, gate, cm, vec = _ml_specs(nc, False)

    def body(q_ref, k_ref, v_ref, li_ref, lf_ref, hc_ref, cp_ref, np_ref, mp_ref, c_ref, n_ref, m_ref):
        @pl.when(pl.program_id(0) == 0)
        def _():
            c_ref[...] = jnp.zeros_like(c_ref)
            n_ref[...] = jnp.zeros_like(n_ref)
            m_ref[...] = jnp.zeros_like(m_ref)

        for h, hs in enumerate(_head_slices(HEAD_W)):
            c0, n0, m0 = c_ref[h], n_ref[h], m_ref[h]
            cp_ref[h] = c0
            np_ref[h] = n0
            mp_ref[h] = m0
            hc, c1, n1, m1 = _ml_chunk(q_ref[:, hs], k_ref[:, hs], v_ref[:, hs], li_ref[h], lf_ref[h], c0, n0, m0[:, 0:1])
            hc_ref[:, hs] = hc
            c_ref[h] = c1
            n_ref[h] = n1
            m_ref[h] = jnp.broadcast_to(m1, (1, HEAD_W))

    return pl.pallas_call(
        body, name="mlstm_fwd", grid=(nc,),
        in_specs=[tm, tm, tm, gate, gate], out_specs=[tm, cm, vec, vec],
        out_shape=[jax.ShapeDtypeStruct((t, HEADS * HEAD_W), F32), jax.ShapeDtypeStruct((HEADS, nc, HEAD_W, HEAD_W), F32),
                   jax.ShapeDtypeStruct((HEADS, nc, 1, HEAD_W), F32), jax.ShapeDtypeStruct((HEADS, nc, 1, HEAD_W), F32)],
        scratch_shapes=_ML_STATE,
        compiler_params=_cparams("arbitrary"),
    )(q, k, v, li, lf)


def _ml_bwd(q, k, v, li, lf, cp, npv, mp, dhc):
    t = q.shape[0]
    nc = t // CHUNK
    tm, gate, cm, vec = _ml_specs(nc, True)

    def body(q_ref, k_ref, v_ref, li_ref, lf_ref, cp_ref, np_ref, mp_ref, dhc_ref,
             dq_ref, dk_ref, dv_ref, dli_ref, dlf_ref, dc_ref, dn_ref, dm_ref):
        @pl.when(pl.program_id(0) == 0)
        def _():
            dc_ref[...] = jnp.zeros_like(dc_ref)
            dn_ref[...] = jnp.zeros_like(dn_ref)
            dm_ref[...] = jnp.zeros_like(dm_ref)

        for h, hs in enumerate(_head_slices(HEAD_W)):
            _, vjp = jax.vjp(_ml_chunk, q_ref[:, hs], k_ref[:, hs], v_ref[:, hs], li_ref[h], lf_ref[h],
                             cp_ref[h], np_ref[h], mp_ref[h][:, 0:1])
            dq, dk, dv, dli, dlf, dc, dn, dm = vjp((dhc_ref[:, hs], dc_ref[h], dn_ref[h], dm_ref[h][:, 0:1]))
            dq_ref[:, hs] = dq
            dk_ref[:, hs] = dk
            dv_ref[:, hs] = dv
            dli_ref[h] = dli
            dlf_ref[h] = dlf
            dc_ref[h] = dc
            dn_ref[h] = dn
            dm_ref[h] = jnp.broadcast_to(dm, (1, HEAD_W))

    tm_shape = jax.ShapeDtypeStruct((t, HEADS * HEAD_W), F32)
    gate_shape = jax.ShapeDtypeStruct((HEADS, nc, 1, CHUNK), F32)
    return pl.pallas_call(
        body, name="mlstm_bwd", grid=(nc,),
        in_specs=[tm, tm, tm, gate, gate, cm, vec, vec, tm], out_specs=[tm, tm, tm, gate, gate],
        out_shape=[tm_shape, tm_shape, tm_shape, gate_shape, gate_shape],
        scratch_shapes=_ML_STATE,
        compiler_params=_cparams("arbitrary"),
    )(q, k, v, li, lf, cp, npv, mp, dhc)


def _ml_pre(s0, s1, s2, s3, cw0, cw1, cw2, cw3, cb, wq, wk, wv, wiq, wik, wiv, bif):
    pre = cb + cw0 * s0 + cw1 * s1 + cw2 * s2 + cw3 * s3
    xc = pre * _sigmoid(pre)
    q = _bdot(xc, wq, "nn")
    k = _bdot(xc, wk, "nn")
    v = _bdot(s3, wv, "nn")
    gates = _bdot(q, wiq, "nn") + _bdot(k, wik, "nn") + _bdot(v, wiv, "nn") + bif
    lane = lax.broadcasted_iota(jnp.int32, gates.shape, 1)
    gl = jnp.where(lane < HEADS, gates, _log_sigmoid(gates))
    return xc, q, k, v, gl


def _delayed(xs_ref, x_ref, halo_ref, r):
    xs_ref[0:HALO, :] = halo_ref[...]
    xs_ref[HALO:HALO + r, :] = x_ref[...]
    return [xs_ref[pl.ds(HALO - (CONV_K - 1) + j, r), :] for j in range(CONV_K)]


def _full_spec(shape):
    return pl.BlockSpec(shape, lambda i, nd=len(shape): (0,) * nd)


def _ml_pre_fwd(x_m, x_pad, params, tile=256):
    t, w = x_m.shape
    r = min(tile, t)

    def body(*refs):
        x_ref, halo_ref = refs[:2]
        p = [ref[...] for ref in refs[2:2 + len(params)]]
        outs = refs[2 + len(params):-1]
        res = _ml_pre(*_delayed(refs[-1], x_ref, halo_ref, r), *p)
        for ref, val in zip(outs, res):
            ref[...] = val

    row = pl.BlockSpec((r, w), lambda i: (i, 0))
    return pl.pallas_call(
        body, name="ml_pre_fwd", grid=(t // r,),
        in_specs=[row, pl.BlockSpec((HALO, w), lambda i: (i * (r // HALO), 0))] + [_full_spec(p.shape) for p in params],
        out_specs=[row] * 4 + [pl.BlockSpec((r, LANES), lambda i: (i, 0))],
        out_shape=[jax.ShapeDtypeStruct((t, w), F32)] * 4 + [jax.ShapeDtypeStruct((t, LANES), F32)],
        scratch_shapes=[pltpu.VMEM((r + HALO, w), F32)],
        compiler_params=_cparams("arbitrary"),
    )(x_m, x_pad, *params)


def _ml_pre_bwd(x_m, x_pad, params, cts, tile=256):
    t, w = x_m.shape
    r = min(tile, t)
    nt = t // r
    n_p = len(params)

    def body(*refs):
        x_ref, halo_ref = refs[:2]
        p = [ref[...] for ref in refs[2:2 + n_p]]
        ct = [ref[...] for ref in refs[2 + n_p:7 + n_p]]
        dx_ref = refs[7 + n_p]
        dp_refs = refs[8 + n_p:8 + 2 * n_p]
        xs_ref, ds_ref, carry_ref = refs[8 + 2 * n_p:]
        step = pl.program_id(0)

        @pl.when(step == 0)
        def _():
            ds_ref[...] = jnp.zeros_like(ds_ref)
            carry_ref[...] = jnp.zeros_like(carry_ref)

        _, vjp = jax.vjp(_ml_pre, *_delayed(xs_ref, x_ref, halo_ref, r), *p)
        grads = vjp(tuple(ct))
        for j in range(CONV_K):
            ds_ref[j, HALO:HALO + r, :] = grads[j]
        lead = HALO + CONV_K - 1
        d_tile = sum(ds_ref[j, pl.ds(lead - j, r), :] for j in range(CONV_K))
        d_halo = sum(ds_ref[j, pl.ds(CONV_K - 1 - j, HALO), :] for j in range(CONV_K))
        dx_ref[...] = d_tile
        dx_ref[r - HALO:r, :] += carry_ref[...]
        carry_ref[...] = d_halo
        _accumulate(step, dp_refs, grads[CONV_K:])

    row = pl.BlockSpec((r, w), lambda i: (nt - 1 - i, 0))
    return pl.pallas_call(
        body, name="ml_pre_bwd", grid=(nt,),
        in_specs=[row, pl.BlockSpec((HALO, w), lambda i: ((nt - 1 - i) * (r // HALO), 0))] + [_full_spec(p.shape) for p in params]
        + [row] * 4 + [pl.BlockSpec((r, LANES), lambda i: (nt - 1 - i, 0))],
        out_specs=[row] + [_full_spec(p.shape) for p in params],
        out_shape=[jax.ShapeDtypeStruct((t, w), F32)] + [jax.ShapeDtypeStruct(p.shape, F32) for p in params],
        scratch_shapes=[pltpu.VMEM((r + HALO, w), F32), pltpu.VMEM((CONV_K, r + 2 * HALO, w), F32), pltpu.VMEM((HALO, w), F32)],
        compiler_params=_cparams("arbitrary"),
    )(x_m, x_pad, *params, *cts)


def _per_head(fn, row_vals, head_params, shared_params=()):
    return [fn(*[a[:, hs] for a in row_vals], *[p[:, hs] for p in head_params], *shared_params) for hs in _head_slices(HEAD_W)]


def _gla_out(o, g, gn):
    return _rms(o, gn) * (g * _sigmoid(g))


def _ml_out(hc, op, xc, g, sk):
    hcell = hc * _sigmoid(op)
    mu = jnp.mean(hcell, axis=-1, keepdims=True)
    d = hcell - mu
    var = jnp.mean(d * d, axis=-1, keepdims=True)
    return d * lax.rsqrt(var + EPS) * g + sk * xc


def _log_decay(al, w, b):
    return _log_sigmoid(_bdot(al, w, "nn") + b) * (1.0 / GLA_GATE_NORM)


def _merge(ga, gb, ya, yb):
    return _sigmoid(ga) * ya + _sigmoid(gb) * yb


def _post_mix(x, z, gpm, gpl):
    x1 = x + _rms(z, gpm)
    return x1, _rms(x1, gpl)


def _loss_rows(x1, dn, tgt, g):
    e = x1 + _rms(dn, g) - tgt
    return 0.5 * jnp.sum(jnp.mean(e * e, axis=-1, keepdims=True), axis=0, keepdims=True)


def _lin(p):
    return 4 * p[0] + 2 * p[1] + p[2]


def _me():
    return lax.axis_index("x"), lax.axis_index("y"), lax.axis_index("c")


def _flip(p, k):
    return tuple((1 - v) if (k >> (2 - i)) & 1 else v for i, v in enumerate(p))


ANY = pl.BlockSpec(memory_space=pl.ANY)


def _allgather_big(shards, name, deps=()):
    n = len(shards)
    n_in = n + len(deps)

    def body(*refs):
        ins, outs = refs[:n], refs[n_in:n_in + n]
        send_sems, recv_sems, local_sems = refs[n_in + n:]
        me = _me()
        x, y, c = me
        sib = (x, y, 1 - c)
        chips = [(1 - x, y), (x, 1 - y), (1 - x, 1 - y)]

        def cp(a, k, block, to, src=None):
            dst = outs[a].at[_lin(block)]
            return pltpu.make_async_remote_copy(src_ref=dst if src is None else src, dst_ref=dst,
                                                send_sem=send_sems.at[a * 7 + k], recv_sem=recv_sems.at[a * 7 + k],
                                                device_id=to, device_id_type=MESH)

        mine = [pltpu.make_async_copy(ins[a], outs[a].at[_lin(me)], local_sems.at[a]) for a in range(n)]
        for m in mine:
            m.start()
        first = []
        for a in range(n):
            first.append(cp(a, 0, me, sib, src=ins[a]))
            first += [cp(a, 1 + j, me, (*chip, c), src=ins[a]) for j, chip in enumerate(chips)]
        for f in first:
            f.start()
        passed = []
        for j, chip in enumerate(chips):
            for a in range(n):
                cp(a, 1 + j, (*chip, c), me).wait_recv()
                fwd = cp(a, 4 + j, (*chip, c), sib)
                fwd.start()
                passed.append(fwd)
        for a in range(n):
            cp(a, 0, sib, me).wait_recv()
            for j, chip in enumerate(chips):
                cp(a, 4 + j, (*chip, 1 - c), me).wait_recv()
        for f in first + passed:
            f.wait_send()
        for m in mine:
            m.wait()

    return pl.pallas_call(
        body, name=name,
        in_specs=[ANY] * n_in, out_specs=[ANY] * n,
        out_shape=[jax.ShapeDtypeStruct((N_DEV, *s.shape), s.dtype) for s in shards],
        scratch_shapes=[pltpu.SemaphoreType.DMA((7 * n,)), pltpu.SemaphoreType.DMA((7 * n,)), pltpu.SemaphoreType.DMA((n,))],
    )(*shards, *deps)


HBM =pl.BlockSpec(memory_space=pltpu.HBM)
SEM = pl.BlockSpec(memory_space=pltpu.SEMAPHORE)
DATAFLOW = pltpu.SideEffectType.DATAFLOW_SIDE_EFFECTING


def _peer_copies(kind, srcs, lands, send_sems, recv_sems):
    me = _me()
    copies = []
    for a, (src, land) in enumerate(zip(srcs, lands)):
        for k in range(1, N_DEV):
            peer = _flip(me, k)
            copies.append(pltpu.make_async_remote_copy(
                src_ref=src if kind == "gather" else src.at[_lin(peer)], dst_ref=land.at[_lin(me)],
                send_sem=send_sems.at[a * 7 + k - 1], recv_sem=recv_sems.at[a * 7 + k - 1],
                device_id=peer, device_id_type=MESH))
    return copies


def _copies_start(kind, srcs, name, after=None):
    n = len(srcs)
    extra = [] if after is None else [after]
    land_shapes = [((N_DEV, *s.shape) if kind == "gather" else s.shape) for s in srcs]

    def body(*refs):
        sems = refs[2 * n + len(extra):]
        for cp in _peer_copies(kind, refs[:n], refs[n:2 * n], sems[0], sems[1]):
            cp.start()
        refs[-1][...] = jnp.zeros_like(refs[-1])

    def hbm(a):
        return pltpu.with_memory_space_constraint(a, pltpu.HBM)

    out = pl.pallas_call(
        body, name=name,
        out_shape=(pltpu.SemaphoreType.DMA((7 * n,)), pltpu.SemaphoreType.DMA((7 * n,)),
                   *[pltpu.HBM(s.shape, s.dtype) for s in srcs],
                   *[pltpu.HBM(ls, s.dtype) for ls, s in zip(land_shapes, srcs)],
                   jax.ShapeDtypeStruct((8, LANES), F32)),
        in_specs=[HBM] * (2 * n) + [ANY] * len(extra),
        out_specs=(SEM, SEM, *[HBM] * (2 * n), pl.BlockSpec(memory_space=pltpu.VMEM)),
        input_output_aliases={i: 2 + i for i in range(2 * n)},
        compiler_params=pltpu.CompilerParams(has_side_effects=DATAFLOW),
    )(*[hbm(s) for s in srcs], *[hbm(lax.empty(ls, s.dtype)) for ls, s in zip(land_shapes, srcs)], *extra)
    return (kind, n, out[:-1]), out[-1]


def _copies_wait(state, after, name):
    kind, n, (send_sems, recv_sems, *thru) = state
    after = list(after) if isinstance(after, (list, tuple)) else [after]

    def body(*refs):
        for cp in _peer_copies(kind, refs[:n], refs[n:2 * n], refs[2 * n], refs[2 * n + 1]):
            cp.wait_send()
            cp.wait_recv()

    out = pl.pallas_call(
        body, name=name,
        out_shape=tuple(pltpu.HBM(t.shape, t.dtype) for t in thru),
        in_specs=[HBM] * (2 * n) + [SEM, SEM] + [ANY] * len(after), out_specs=tuple([HBM] * (2 * n)),
        input_output_aliases={i: i for i in range(2 * n)},
        compiler_params=pltpu.CompilerParams(has_side_effects=DATAFLOW),
    )(*thru, send_sems, recv_sems, *after)
    return out[:n], out[n:]


def _adamw(w, g, m, v):
    m2 = ADAM_B1 * m + (1.0 - ADAM_B1) * g
    v2 = ADAM_B2 * v + (1.0 - ADAM_B2) * (g * g)
    m_hat = m2 / (1.0 - ADAM_B1 ** ADAM_STEP)
    v_hat = v2 / (1.0 - ADAM_B2 ** ADAM_STEP)
    delta = -ADAM_LR * (m_hat / (jnp.sqrt(v_hat) + ADAM_EPS) + ADAM_WD * w)
    return delta, m2, v2


def _sum_adamw(land, part, me_idx, w, m, v, name, tile=256):
    r, c = w.shape
    tr = min(tile, r)

    def body(me_ref, own_ref, *refs):
        slots = refs[:N_DEV]
        w_ref, m_ref, v_ref, g_ref, d_ref, m2_ref, v2_ref = refs[N_DEV:]
        own = own_ref[...].astype(F32)
        g = None
        for s in range(N_DEV):
            term = jnp.where(me_ref[0] == s, own, slots[s][...].astype(F32))
            g = term if g is None else g + term
        d, m2, v2 = _adamw(w_ref[...], g, m_ref[...], v_ref[...])
        g_ref[...] = g
        d_ref[...] = d
        m2_ref[...] = m2
        v2_ref[...] = v2

    def slot_spec(s):
        return pl.BlockSpec((None, tr, c), lambda i, me: (jnp.where(me[0] == s, (s + 1) % N_DEV, s), i, 0))

    row = pl.BlockSpec((tr, c), lambda i, me: (i, 0))
    return pl.pallas_call(
        body, name=name,
        grid_spec=pltpu.PrefetchScalarGridSpec(
            num_scalar_prefetch=1, grid=(r // tr,),
            in_specs=[pl.BlockSpec((None, tr, c), lambda i, me: (me[0], i, 0))] + [slot_spec(s) for s in range(N_DEV)] + [row] * 3,
            out_specs=[row] * 4),
        out_shape=[jax.ShapeDtypeStruct((r, c), F32)] * 4,
        compiler_params=_cparams("parallel"),
    )(me_idx, part, *[land] * N_DEV, w, m, v)


def _sum_slots(gathered, name):
    _, r, w = gathered.shape

    def body(p_ref, o_ref):
        g = p_ref[0]
        for s in range(1, N_DEV):
            g = g + p_ref[s]
        o_ref[...] = g

    return pl.pallas_call(body, name=name, out_shape=jax.ShapeDtypeStruct((r, w), F32))(gathered)


def _pack(arrs):
    flat = jnp.concatenate([a.reshape(-1).astype(F32) for a in arrs])
    rows = -(-flat.shape[0] // (8 * LANES)) * 8
    return jnp.pad(flat, (0, rows * LANES - flat.shape[0])).reshape(rows, LANES)


def _unpack(packed, shapes):
    flat = packed.reshape(-1)
    out, off = [], 0
    for s in shapes:
        size = 1
        for d in s:
            size *= d
        out.append(flat[off:off + size].reshape(s))
        off += size
    return out


def _to_hm(a, d):
    t = a.shape[0]
    return a.reshape(t, HEADS, d).transpose(1, 0, 2)


def _from_hm(a):
    h, t, d = a.shape
    return a.transpose(1, 0, 2).reshape(t, h * d)


def _gate_rows(g):
    t = g.shape[0]
    return g.T.reshape(HEADS, t // CHUNK, 1, CHUNK)


def _gate_cols(g):
    h, nc, _, c = g.shape
    return g.reshape(h, nc * c).T


def _blockdiag_dense(w):
    n = w.shape[0] * QKV_BLOCK
    tiled = jnp.tile(w.reshape(n, QKV_BLOCK), (1, n // QKV_BLOCK))
    r = lax.broadcasted_iota(jnp.int32, (n, n), 0)
    c = lax.broadcasted_iota(jnp.int32, (n, n), 1)
    return jnp.where(r // QKV_BLOCK == c // QKV_BLOCK, tiled, 0.0)


def _blockdiag_blocks(dense):
    n = dense[0].shape[0]
    k = len(dense)

    def body(*refs):
        r = lax.broadcasted_iota(jnp.int32, (n, n), 0)
        c = lax.broadcasted_iota(jnp.int32, (n, n), 1)
        fr = lax.broadcasted_iota(jnp.int32, (n, LANES), 0)
        fc = lax.broadcasted_iota(jnp.int32, (n, LANES), 1)
        fold = ((fr & (QKV_BLOCK - 1)) == fc).astype(BF16)
        for i in range(k):
            kept = jnp.where((r >> 2) == (c >> 2), refs[i][...], 0.0)
            refs[k + i][...] = sum(lax.dot_general(t, fold, _DN["nn"], preferred_element_type=F32) for t in _split3(kept))

    out = pl.pallas_call(body, name="blockdiag_blocks", out_shape=[jax.ShapeDtypeStruct((n, LANES), F32)] * k)(*dense)
    return [o[:, 0:QKV_BLOCK].reshape(n // QKV_BLOCK, QKV_BLOCK, QKV_BLOCK) for o in out]


def _col_blocks(w):
    k, n = w.shape
    return w.reshape(k, N_DEV, n // N_DEV).transpose(1, 0, 2)


def _from_col_blocks(g):
    d, k, n = g.shape
    return g.transpose(1, 0, 2).reshape(k, d * n)


def _local_step(x, tgt, weight, ws, prefetch, on_grads, on_small):
    t, d = x.shape
    g1 = ws["g_pre_mix"]

    def dep(token):
        return () if token is None else (token,)

    (h,) = _rowwise("pre_mix_norm", lambda xv, g: ((_rms(xv, g),), ()), [x], [g1], [(d, BF16)])
    fetch_mix = prefetch(("w_pa", "w_pb", "w_o"), h)
    proj = _mm_shards(h, weight("w_in", h), "proj_in", tm=2048)
    offs = [0]
    for s in IN_SPLITS:
        offs.append(offs[-1] + s)
    q_a, k_a, v_a, g_a, a_low, x_m, o_pre, gate_a, gate_b = [_shard_columns(proj, offs[i], offs[i + 1]) for i in range(9)]

    a_low_p = jnp.pad(a_low, ((0, 0), (0, LANES - LOWRANK)))
    w_a_up_p = jnp.pad(ws["w_a_up"], ((0, LANES - LOWRANK), (0, 0)))
    b_a_up = ws["b_a_up"]
    (la,) = _rowwise("gla_decay", lambda al, w, b: ((_log_decay(al, w, b),), ()), [a_low_p], [w_a_up_p, b_a_up],
                     [(HEADS * GLA_DK, F32)], deps=dep(fetch_mix))
    fetch_up = prefetch(("w_up",), la)
    q_hm, k_hm, la_hm = _to_hm(q_a, GLA_DK), _to_hm(k_a, GLA_DK), _to_hm(la, GLA_DK)
    o_gla, s_prev = _gla_fwd(q_hm, k_hm, v_a, la_hm)
    gn = ws["g_gla_norm"]
    (ya_in,) = _rowwise("gla_out", lambda o, g, n_: ((jnp.concatenate(_per_head(_gla_out, [o, g], [], [n_]), axis=1),), ()),
                        [o_gla, g_a], [gn], [(HEADS * HEAD_W, BF16)], deps=dep(fetch_up))
    y_a = _mm(ya_in, weight("w_pa", ya_in), "nn", F32, "proj_a")

    cw = ws["conv_w"]
    w_if_p = jnp.pad(ws["w_if"], ((0, 0), (0, LANES - 2 * HEADS)))
    ml_w = HEADS * HEAD_W
    pre_params = [cw[0:1], cw[1:2], cw[2:3], cw[3:4], ws["conv_b"],
                  _blockdiag_dense(ws["w_q_ml"]), _blockdiag_dense(ws["w_k_ml"]), _blockdiag_dense(ws["w_v_ml"]),
                  w_if_p[0:ml_w], w_if_p[ml_w:2 * ml_w], w_if_p[2 * ml_w:3 * ml_w],
                  jnp.pad(ws["b_if"], ((0, 0), (0, LANES - 2 * HEADS)))]
    x_pad = jnp.pad(x_m, ((HALO, 0), (0, 0)))
    xc, q_m, k_m, v_m, gl = _ml_pre_fwd(x_m, x_pad, pre_params)
    li, lf = _gate_rows(gl[:, 0:HEADS]), _gate_rows(gl[:, HEADS:2 * HEADS])
    hc, c_prev, n_prev, m_prev = _ml_fwd(q_m, k_m, v_m, li, lf)
    fetch_down = prefetch(("w_down",), hc)
    g_ml, skip = ws["g_ml_norm"], ws["ml_skip"]
    (h_b,) = _rowwise("mlstm_out", lambda a, b, c_, g, s: ((jnp.concatenate(_per_head(_ml_out, [a, b, c_], [g, s]), axis=1),), ()),
                      [hc, o_pre, xc], [g_ml, skip], [(ml_w, BF16)], deps=dep(fetch_down))
    y_b = _mm(h_b, weight("w_pb", h_b), "nn", F32, "proj_b")

    (merged,) = _rowwise("merge", lambda ga, gb, ya, yb: ((_merge(ga, gb, ya, yb),), ()), [gate_a, gate_b, y_a, y_b], [],
                         [(d, BF16)])
    z = _mm(merged, weight("w_o", merged), "nn", F32, "proj_o")
    gpm, gpl, gpo = ws["g_post_mix"], ws["g_pre_mlp"], ws["g_post_mlp"]
    x1, h2 = _rowwise("post_mix", lambda xv, zv, a, b: (_post_mix(xv, zv, a, b), ()), [x, z], [gpm, gpl], [(d, F32), (d, BF16)])
    up, u = _mm(h2, weight("w_up", h2), "nn", (F32, BF16), "mlp_up", epilogue=lambda p: (p, jnp.square(jnp.maximum(p, 0.0))))
    dn = _mm(u, weight("w_down", u), "nn", F32, "mlp_down", tm=512)

    def loss_and_grads(x1v, dnv, tgtv, g):
        loss, vjp = jax.vjp(lambda a, b, c_: _loss_rows(a, b, tgtv, c_), x1v, dnv, g)
        dx1, ddn, dg = vjp(jnp.ones((1, 1), F32))
        return (dx1, ddn), (jnp.broadcast_to(loss, (1, LANES)), dg)

    dx1_y, d_dn, loss, d_gpo = _rowwise("loss", loss_and_grads, [x1, dn, tgt], [gpo], [(d, F32), (d, BF16)],
                                        [((1, LANES), F32), ((1, d), F32)])

    (d_up,) = _mm(d_dn, weight("w_down", u), "nt", (BF16,), "mlp_down_dx", extra=[up],
                  epilogue=lambda p, a: (p * (2.0 * jnp.maximum(a, 0.0)),))
    dw_down = _mm(u, d_dn, "tn", BF16, "mlp_down_dw", tm=512)
    d_h2 = _mm(d_up, weight("w_up", h2), "nt", F32, "mlp_up_dx", tm=512)
    dw_up = _mm(h2, d_up, "tn", BF16, "mlp_up_dw")
    sent_mlp = on_grads(dict(w_down=dw_down, w_up=dw_up))

    def post_mix_bwd(xv, zv, dx1, dh2, a, b):
        _, vjp = jax.vjp(_post_mix, xv, zv, a, b)
        dx, dz, da, db = vjp((dx1, dh2))
        return (dx, dz), (da, db)

    dx_res, d_z, d_gpm, d_gpl = _rowwise("post_mix_bwd", post_mix_bwd, [x, z, dx1_y, d_h2], [gpm, gpl],
                                         [(d, F32), (d, BF16)], [((1, d), F32), ((1, d), F32)], deps=dep(sent_mlp))
    d_merged = _mm(d_z, weight("w_o", merged), "nt", F32, "proj_o_dx")
    dw_o = _mm(merged, d_z, "tn", BF16, "proj_o_dw")
    d_ga, d_gb, d_ya, d_yb = _rowwise("merge_bwd", lambda *v: (jax.vjp(_merge, *v[:4])[1](v[4]), ()),
                                      [gate_a, gate_b, y_a, y_b, d_merged], [], [(d, F32), (d, F32), (d, BF16), (d, BF16)])
    d_ya_in = _mm(d_ya, weight("w_pa", ya_in), "nt", F32, "proj_a_dx")
    dw_pa = _mm(ya_in, d_ya, "tn", BF16, "proj_a_dw")
    d_hb = _mm(d_yb, weight("w_pb", h_b), "nt", F32, "proj_b_dx")
    dw_pb = _mm(h_b, d_yb, "tn", BF16, "proj_b_dw")
    sent_mix = on_grads(dict(w_o=dw_o, w_pa=dw_pa, w_pb=dw_pb))

    def ml_out_bwd(a, b, c_, ct, g, s):
        parts = []
        for hs in _head_slices(HEAD_W):
            _, vjp = jax.vjp(_ml_out, a[:, hs], b[:, hs], c_[:, hs], g[:, hs], s[:, hs])
            parts.append(vjp(ct[:, hs]))
        cat = lambda i: jnp.concatenate([p[i] for p in parts], axis=1)
        return (cat(0), cat(1), cat(2)), (cat(3), cat(4))

    d_hc, d_opre, d_xc, d_gml, d_skip = _rowwise("mlstm_out_bwd", ml_out_bwd, [hc, o_pre, xc, d_hb], [g_ml, skip],
                                                 [(ml_w, F32)] * 3, [((1, ml_w), F32)] * 2, deps=dep(sent_mix))
    d_qm, d_km, d_vm, d_li, d_lf = _ml_bwd(q_m, k_m, v_m, li, lf, c_prev, n_prev, m_prev, d_hc)
    d_gl = jnp.concatenate([_gate_cols(d_li), _gate_cols(d_lf), jnp.zeros((t, LANES - 2 * HEADS), F32)], axis=1)
    pre_grads = _ml_pre_bwd(x_m, x_pad, pre_params, [d_xc, d_qm, d_km, d_vm, d_gl])
    d_xm = pre_grads[0]
    d_cw = jnp.concatenate(pre_grads[1:5], axis=0)
    d_cb = pre_grads[5]
    d_wq, d_wk, d_wv = _blockdiag_blocks(pre_grads[6:9])
    d_wif = jnp.concatenate(pre_grads[9:12], axis=0)[:, 0:2 * HEADS]
    d_bif = pre_grads[12][:, 0:2 * HEADS]

    def gla_out_bwd(o, g, ct, n_):
        parts = []
        for hs in _head_slices(HEAD_W):
            _, vjp = jax.vjp(_gla_out, o[:, hs], g[:, hs], n_)
            parts.append(vjp(ct[:, hs]))
        cat = lambda i: jnp.concatenate([p[i] for p in parts], axis=1)
        return (cat(0), cat(1)), (sum(p[2] for p in parts),)

    d_o, d_g_a, d_gn = _rowwise("gla_out_bwd", gla_out_bwd, [o_gla, g_a, d_ya_in], [gn], [(ml_w, F32)] * 2, [((1, HEAD_W), F32)])
    dq_hm, dk_hm, d_va, dla_hm = _gla_bwd(q_hm, k_hm, v_a, la_hm, s_prev, d_o)

    def decay_bwd(al, ct, w, b):
        _, vjp = jax.vjp(_log_decay, al, w, b)
        dal, dw, db = vjp(ct)
        return (dal,), (dw, db)

    d_alow_p, d_wa_p, d_ba = _rowwise("gla_decay_bwd", decay_bwd, [a_low_p, _from_hm(dla_hm)], [w_a_up_p, b_a_up],
                                      [(LANES, F32)], [(w_a_up_p.shape, F32), (b_a_up.shape, F32)])
    d_proj = _to_shards([_from_hm(dq_hm), _from_hm(dk_hm), d_va, d_g_a, d_alow_p[:, 0:LOWRANK], d_xm, d_opre, d_ga, d_gb],
                        proj.shape[2])
    d_h = _mm_shards_dx(d_proj, weight("w_in", h), "proj_in_dx", tm=2048)

    def pre_mix_bwd(xv, dh, dres, g):
        _, vjp = jax.vjp(_rms, xv, g)
        dx, dg = vjp(dh)
        return (dx + dres,), (dg,)

    grad_x, d_g1 = _rowwise("pre_mix_norm_bwd", pre_mix_bwd, [x, d_h, dx_res], [g1], [(d, F32)], [((1, d), F32)])
    small = dict(g_pre_mix=d_g1, w_a_up=d_wa_p[0:LOWRANK], b_a_up=d_ba, g_gla_norm=d_gn, conv_w=d_cw, conv_b=d_cb,
                 w_q_ml=d_wq, w_k_ml=d_wk, w_v_ml=d_wv, w_if=d_wif, b_if=d_bif, ml_skip=d_skip, g_ml_norm=d_gml,
                 g_post_mix=d_gpm, g_pre_mlp=d_gpl, g_post_mlp=d_gpo)
    sent_small = on_small(small, loss[:, 0:1])
    dw_in = _mm_shards_dw(h, d_proj, "proj_in_dw", deps=dep(sent_small))
    return grad_x, on_grads(dict(w_in=dw_in))


BIG = ("w_in", "w_pa", "w_pb", "w_o", "w_up", "w_down")
BIG_COL_SHARDED = ("w_in", "w_pa", "w_pb", "w_up")
SMALL_SHARDED = {"w_a_up": 1, "conv_w": 1, "w_if": 0}
SMALL = ("g_pre_mix", "w_a_up", "b_a_up", "g_gla_norm", "conv_w", "conv_b", "w_q_ml", "w_k_ml", "w_v_ml", "w_if", "b_if",
         "ml_skip", "g_ml_norm", "g_post_mix", "g_pre_mlp", "g_post_mlp")
WEIGHTS = ("g_pre_mix", "w_in", "w_a_up", "b_a_up", "g_gla_norm", "conv_w", "conv_b", "w_q_ml", "w_k_ml", "w_v_ml", "w_if", "b_if",
           "ml_skip", "g_ml_norm", "w_pa", "w_pb", "w_o", "g_post_mix", "g_pre_mlp", "w_up", "w_down", "g_post_mlp")


def _my_slice(a, axis):
    n = a.shape[axis] // N_DEV
    return lax.dynamic_slice_in_dim(a, _lin(_me()) * n, n, axis)


def kernel(x, g_pre_mix, w_in, w_a_up, b_a_up, g_gla_norm, conv_w, conv_b, w_q_ml, w_k_ml, w_v_ml, w_if, b_if, ml_skip, g_ml_norm, w_pa, w_pb, w_o, g_post_mix, g_pre_mlp, w_up, w_down, g_post_mlp, loss_target, m_g_pre_mix, m_w_in, m_w_a_up, m_b_a_up, m_g_gla_norm, m_conv_w, m_conv_b, m_w_q_ml, m_w_k_ml, m_w_v_ml, m_w_if, m_b_if, m_ml_skip, m_g_ml_norm, m_w_pa, m_w_pb, m_w_o, m_g_post_mix, m_g_pre_mlp, m_w_up, m_w_down, m_g_post_mlp, v_g_pre_mix, v_w_in, v_w_a_up, v_b_a_up, v_g_gla_norm, v_conv_w, v_conv_b, v_w_q_ml, v_w_k_ml, v_w_v_ml, v_w_if, v_b_if, v_ml_skip, v_g_ml_norm, v_w_pa, v_w_pb, v_w_o, v_g_post_mix, v_g_pre_mlp, v_w_up, v_w_down, v_g_post_mlp):
    args = dict(locals())
    w = {n: args[n][0] for n in WEIGHTS}
    m = {n: args["m_" + n][0] for n in WEIGHTS}
    v = {n: args["v_" + n][0] for n in WEIGHTS}

    me_lin = _lin(_me())
    me_idx = jnp.reshape(me_lin, (1,)).astype(jnp.int32)

    def full_weight(n, g):
        if n == "w_in":
            return g
        return _from_col_blocks(g) if n in BIG_COL_SHARDED else g.reshape(-1, g.shape[-1])

    def grad_parts(n, g):
        if n == "w_in":
            return g
        return (_col_blocks(g) if n in BIG_COL_SHARDED else g.reshape(N_DEV, -1, g.shape[-1])).astype(BF16)

    sharded_names = tuple(SMALL_SHARDED)
    small_w = _pack([w[n] for n in sharded_names])
    small_w_state, small_w_token = _copies_start("gather", [small_w], "allgather_start_small_weights")
    ready = {"w_in": full_weight("w_in", _allgather_big([w["w_in"].astype(BF16)], "allgather_w_in", [small_w_token])[0])}
    pending = {}

    def prefetch(group, after):
        state, token = _copies_start("gather", [w[n].astype(BF16) for n in group], "allgather_start_" + group[0], after)
        for n in group:
            pending[n] = (group, state)
        return token

    def weight(n, after):
        if n not in ready:
            group, state = pending[n]
            shards, lands = _copies_wait(state, after, "allgather_wait_" + group[0])
            for gn, shard, land in zip(group, shards, lands):
                ready[gn] = full_weight(gn, lax.dynamic_update_slice(land, shard[None], (me_lin, 0, 0)))
        return ready[n]

    (small_w,), (small_w_land,) = _copies_wait(small_w_state, ready["w_in"], "allgather_wait_small_weights")
    small_g = lax.dynamic_update_slice(small_w_land, small_w[None], (me_lin, 0, 0))
    ws = {n: (w[n].reshape(1, -1) if w[n].ndim == 1 else w[n]) for n in SMALL if n not in SMALL_SHARDED}
    per_dev = [_unpack(small_g[dev], [w[n].shape for n in sharded_names]) for dev in range(N_DEV)]
    for i, n in enumerate(sharded_names):
        ws[n] = jnp.concatenate([per_dev[dev][i] for dev in range(N_DEV)], axis=SMALL_SHARDED[n])

    sent = []

    def on_grads(grads):
        names = tuple(grads)
        state, token = _copies_start("exchange", [grad_parts(n, grads[n]) for n in names], "exchange_start_" + names[0])
        sent.append((names, state))
        return token

    small_sent = {}

    def on_small(small, loss):
        small_sent["shapes"] = [small[n].shape for n in SMALL] + [(1, 1)]
        small_sent["state"], token = _copies_start("gather", [_pack([small[n] for n in SMALL] + [loss])],
                                                   "allgather_start_small_grads")
        return token

    grad_x, last_token = _local_step(x[0], loss_target[0], weight, ws, prefetch, on_grads, on_small)

    out = {}

    def finish(names, state, after):
        parts, lands = _copies_wait(state, after, "exchange_wait_" + names[0])
        for n, part, land in zip(names, parts, lands):
            out[n] = _sum_adamw(land, part, me_idx, w[n], m[n], v[n], "adamw_" + n)

    for names, state in sent[:-1]:
        finish(names, state, [grad_x, last_token])

    (small_vec,), (small_land,) = _copies_wait(small_sent["state"], [grad_x, last_token], "allgather_wait_small_grads")
    vec = _sum_slots(lax.dynamic_update_slice(small_land, small_vec[None], (me_lin, 0, 0)), "sum_small_grads")
    summed = _unpack(vec, small_sent["shapes"])
    g_small = {}
    for n, g in zip(SMALL, summed[:-1]):
        g_small[n] = _my_slice(g, SMALL_SHARDED[n]) if n in SMALL_SHARDED else g
    shapes = [g_small[n].shape for n in SMALL]
    packed = [_pack([d[n].reshape(g_small[n].shape) for n in SMALL]) for d in (w, g_small, m, v)]
    upd = _rowwise("adamw_small", lambda a, b, c_, d_: (_adamw(a, b, c_, d_), ()), packed, [], [(LANES, F32)] * 3, tile=8 * 1024)
    deltas, new_ms, new_vs = [_unpack(p, shapes) for p in upd]
    for i, n in enumerate(SMALL):
        out[n] = (g_small[n], deltas[i], new_ms[i], new_vs[i])
    finish(*sent[-1], [upd[0]] + [out[n][1] for n in BIG if n in out])

    shaped = lambda a, n: a.reshape(args[n].shape)
    return (summed[-1].reshape(()), grad_x[None],
            *[shaped(out[n][0], n) for n in WEIGHTS], *[shaped(out[n][1], n) for n in WEIGHTS],
            *[shaped(out[n][2], n) for n in WEIGHTS], *[shaped(out[n][3], n) for n in WEIGHTS])
```

```python
import functools

import jax
import jax.numpy as jnp
from jax import lax
from jax.experimental import pallas as pl
from jax.experimental.pallas import tpu as pltpu

F32 = jnp.float32
BF16 = jnp.bfloat16
MESH = pl.DeviceIdType.MESH

N_DEV = 8
EPS = 1e-6
CHUNK = 64
HEADS = 4
GLA_DK = 64
HEAD_W = 128
GLA_GATE_NORM = 16.0
LOWRANK = 16
CONV_K = 4
QKV_BLOCK = 4
LANES = 128
HALO = 8
IN_SPLITS = (256, 256, 512, 512, 16, 512, 512, 1024, 1024)

ADAM_LR = 0.001
ADAM_B1 = 0.9
ADAM_B2 = 0.999
ADAM_EPS = 1e-08
ADAM_WD = 0.01
ADAM_STEP = 10

VMEM_LIMIT = 56 * 1024 * 1024


def _cparams(*sem):
    return pltpu.CompilerParams(dimension_semantics=sem, vmem_limit_bytes=VMEM_LIMIT)


def _dims(mode, ndim):
    contract = {"nn": ((ndim - 1,), (ndim - 2,)), "nt": ((ndim - 1,), (ndim - 1,)), "tn": ((ndim - 2,), (ndim - 2,))}[mode]
    return contract, (((0,), (0,)) if ndim == 3 else ((), ()))


def _raw_dot(a, b, mode):
    return lax.dot_general(a.astype(BF16), b.astype(BF16), _dims(mode, a.ndim), preferred_element_type=F32)


@functools.partial(jax.custom_vjp, nondiff_argnums=(2,))
def _bdot(a, b, mode):
    return _raw_dot(a, b, mode)


def _bdot_fwd(a, b, mode):
    return _raw_dot(a, b, mode), (a, b)


def _bdot_bwd(mode, res, ct):
    a, b = res
    if mode == "nn":
        da, db = _raw_dot(ct, b, "nt"), _raw_dot(a, ct, "tn")
    elif mode == "nt":
        da, db = _raw_dot(ct, b, "nn"), _raw_dot(ct, a, "tn")
    else:
        da, db = _raw_dot(b, ct, "nt"), _raw_dot(a, ct, "nn")
    return da.astype(a.dtype), db.astype(b.dtype)


_bdot.defvjp(_bdot_fwd, _bdot_bwd)


def _split3(x):
    hi = x.astype(BF16)
    r1 = x - hi.astype(F32)
    mid = r1.astype(BF16)
    return hi, mid, (r1 - mid.astype(F32)).astype(BF16)


def _split_dot(tri, x):
    if x.ndim == 3:
        tri = jnp.broadcast_to(tri, (x.shape[0], *tri.shape))
    return sum(lax.dot_general(tri, t, _dims("nn", x.ndim), preferred_element_type=F32) for t in _split3(x))


def _tri(n, lower):
    r = lax.broadcasted_iota(jnp.int32, (n, n), 0)
    c = lax.broadcasted_iota(jnp.int32, (n, n), 1)
    return ((c <= r) if lower else (c >= r)).astype(BF16)


@jax.custom_vjp
def _cumsum_rows(x):
    return _split_dot(_tri(x.shape[-2], True), x)


def _cumsum_rows_fwd(x):
    return _cumsum_rows(x), None


def _cumsum_rows_bwd(_, ct):
    return (_split_dot(_tri(ct.shape[-2], False), ct),)


_cumsum_rows.defvjp(_cumsum_rows_fwd, _cumsum_rows_bwd)


def _abs(x):
    return jnp.where(x >= 0, x, -x)


def _sigmoid(x):
    return lax.logistic(x)


def _log_sigmoid(x):
    return jnp.minimum(x, 0.0) - jnp.log(1.0 + jnp.exp(-_abs(x)))


def _rms(x, g):
    return x * lax.rsqrt(jnp.mean(x * x, axis=-1, keepdims=True) + EPS) * g


def _head_slices(w):
    return [slice(h * w, (h + 1) * w) for h in range(HEADS)]


def _heads(ref):
    return jnp.stack([ref[:, hs] for hs in _head_slices(HEAD_W)])


def _put_heads(ref, val):
    for h, hs in enumerate(_head_slices(HEAD_W)):
        ref[:, hs] = val[h]


def _tile(dim, want):
    if dim <= want or dim % LANES:
        return dim
    t = want
    while dim % t:
        t -= LANES
    return t


def _mm(a, b, mode, out_dtype, name, tm=1024, tn=1024, tk=4096, epilogue=None, extra=(), deps=()):
    if mode == "nn":
        (m, k), (k2, n) = a.shape, b.shape
    elif mode == "nt":
        (m, k), (n, k2) = a.shape, b.shape
    else:
        (k, m), (k2, n) = a.shape, b.shape
    assert k == k2, (name, a.shape, b.shape)
    tm, tn, tk = _tile(m, tm), _tile(n, tn), _tile(k, tk)
    nk = k // tk
    out_dtypes = out_dtype if epilogue else (out_dtype,)
    assert nk == 1 or (out_dtype == F32 and not epilogue), name
    n_in = 2 + len(extra)

    def body(*refs):
        p = _raw_dot(refs[0][...], refs[1][...], mode)
        if nk > 1:
            _accumulate(pl.program_id(2), [refs[n_in + len(deps)]], [p])
            return
        outs = epilogue(p, *[r[...] for r in refs[2:n_in]]) if epilogue else (p,)
        for ref, val in zip(refs[n_in + len(deps):], outs):
            ref[...] = val.astype(ref.dtype)

    a_spec = pl.BlockSpec((tk, tm), lambda i, j, kk: (kk, i)) if mode == "tn" else pl.BlockSpec((tm, tk), lambda i, j, kk: (i, kk))
    b_spec = pl.BlockSpec((tn, tk), lambda i, j, kk: (j, kk)) if mode == "nt" else pl.BlockSpec((tk, tn), lambda i, j, kk: (kk, j))
    o_spec = pl.BlockSpec((tm, tn), lambda i, j, kk: (i, j))
    res = pl.pallas_call(
        body, name=name, grid=(m // tm, n // tn, nk),
        in_specs=[a_spec, b_spec] + [o_spec] * len(extra) + [ANY] * len(deps), out_specs=[o_spec] * len(out_dtypes),
        out_shape=[jax.ShapeDtypeStruct((m, n), dt) for dt in out_dtypes],
        compiler_params=_cparams("parallel", "parallel", "arbitrary"),
    )(a, b, *extra, *deps)
    return res if epilogue else res[0]


def _rowwise(name, fn, rows, params, out_rows, out_accs=(), tile=256, deps=()):
    t = rows[0].shape[0]
    r = min(tile, t)
    assert t % r == 0
    n_in, n_or = len(rows) + len(params), len(out_rows)
    n_all = n_in + len(deps)
    params = list(params) + list(deps)

    def body(*refs):
        vals = [ref[...] for ref in refs[:n_in]]
        outs = refs[n_all:]
        ro, ao = fn(*vals)
        for ref, v in zip(outs[:n_or], ro):
            ref[...] = v.astype(ref.dtype)
        if out_accs:
            _accumulate(pl.program_id(0), outs[n_or:], ao)

    def full(shape):
        return pl.BlockSpec(shape, lambda i, nd=len(shape): (0,) * nd)

    return pl.pallas_call(
        body, name=name, grid=(t // r,),
        in_specs=[pl.BlockSpec((r, a.shape[1]), lambda i: (i, 0)) for a in rows] + [full(p.shape) for p in params],
        out_specs=[pl.BlockSpec((r, w), lambda i: (i, 0)) for w, _ in out_rows] + [full(s) for s, _ in out_accs],
        out_shape=[jax.ShapeDtypeStruct((t, w), dt) for w, dt in out_rows] + [jax.ShapeDtypeStruct(s, dt) for s, dt in out_accs],
        compiler_params=_cparams("arbitrary"),
    )(*rows, *params)


def _accumulate(step, refs, vals):
    for ref, v in zip(refs, vals):
        @pl.when(step == 0)
        def _(ref=ref, v=v):
            ref[...] = v.astype(ref.dtype)

        @pl.when(step > 0)
        def _(ref=ref, v=v):
            ref[...] += v.astype(ref.dtype)


def _gla_chunk(q, k, v, la, st):
    c = q.shape[-2]
    row = lax.broadcasted_iota(jnp.int32, (c, c), 0)
    col = lax.broadcasted_iota(jnp.int32, (c, c), 1)
    cum = _cumsum_rows(la)
    cl = jnp.sum(la, axis=-2, keepdims=True)
    ep = jnp.exp(cum)
    en = jnp.exp(-cum)
    qs = q * (GLA_DK ** -0.5)
    qp = qs * ep
    a_f = _bdot(qp, k * en, "nt")
    a_b = _bdot(qs * en, k * ep, "nt")
    sc = jnp.where(row >= col, a_f, a_b)
    o = _bdot(sc, v, "nn") + _bdot(qp, st, "nt")
    kd = k * jnp.exp(cl - cum)
    st_new = st * jnp.exp(cl) + _bdot(v, kd, "tn")
    return o, st_new


def _gla_specs(nc, rev):
    def ch(n):
        return (nc - 1 - n) if rev else n
    hm = pl.BlockSpec((HEADS, CHUNK, GLA_DK), lambda n: (0, ch(n), 0))
    tm = pl.BlockSpec((CHUNK, HEADS * HEAD_W), lambda n: (ch(n), 0))
    st = pl.BlockSpec((HEADS, None, HEAD_W, GLA_DK), lambda n: (0, ch(n), 0, 0))
    return hm, tm, st


def _gla_fwd(q, k, v, la):
    t = v.shape[0]
    nc = t // CHUNK
    hm, tm, st = _gla_specs(nc, False)

    def body(q_ref, k_ref, v_ref, la_ref, o_ref, sp_ref, st_ref):
        @pl.when(pl.program_id(0) == 0)
        def _():
            st_ref[...] = jnp.zeros_like(st_ref)

        s = st_ref[...]
        sp_ref[...] = s
        o, s_new = _gla_chunk(q_ref[...], k_ref[...], _heads(v_ref), la_ref[...], s)
        _put_heads(o_ref, o)
        st_ref[...] = s_new

    return pl.pallas_call(
        body, name="gla_fwd", grid=(nc,),
        in_specs=[hm, hm, tm, hm], out_specs=[tm, st],
        out_shape=[jax.ShapeDtypeStruct((t, HEADS * HEAD_W), F32), jax.ShapeDtypeStruct((HEADS, nc, HEAD_W, GLA_DK), F32)],
        scratch_shapes=[pltpu.VMEM((HEADS, HEAD_W, GLA_DK), F32)],
        compiler_params=_cparams("arbitrary"),
    )(q, k, v, la)


def _gla_bwd(q, k, v, la, sp, do):
    t = v.shape[0]
    nc = t // CHUNK
    hm, tm, st = _gla_specs(nc, True)

    def body(q_ref, k_ref, v_ref, la_ref, sp_ref, do_ref, dq_ref, dk_ref, dv_ref, dla_ref, ds_ref):
        @pl.when(pl.program_id(0) == 0)
        def _():
            ds_ref[...] = jnp.zeros_like(ds_ref)

        _, vjp = jax.vjp(_gla_chunk, q_ref[...], k_ref[...], _heads(v_ref), la_ref[...], sp_ref[...])
        dq, dk, dv, dla, ds = vjp((_heads(do_ref), ds_ref[...]))
        dq_ref[...] = dq
        dk_ref[...] = dk
        _put_heads(dv_ref, dv)
        dla_ref[...] = dla
        ds_ref[...] = ds

    hm_shape = jax.ShapeDtypeStruct((HEADS, t, GLA_DK), F32)
    return pl.pallas_call(
        body, name="gla_bwd", grid=(nc,),
        in_specs=[hm, hm, tm, hm, st, tm], out_specs=[hm, hm, tm, hm],
        out_shape=[hm_shape, hm_shape, jax.ShapeDtypeStruct((t, HEADS * HEAD_W), F32), hm_shape],
        scratch_shapes=[pltpu.VMEM((HEADS, HEAD_W, GLA_DK), F32)],
        compiler_params=_cparams("arbitrary"),
    )(q, k, v, la, sp, do)


def _ml_chunk(q, k, v, li_r, lf_r, cm, nv, m):
    c = q.shape[-2]
    row = lax.broadcasted_iota(jnp.int32, (c, c), 0)
    col = lax.broadcasted_iota(jnp.int32, (c, c), 1)
    eye = (row == col).astype(F32)
    li_c = jnp.sum(eye * li_r, axis=-1, keepdims=True)
    lf_c = jnp.sum(eye * lf_r, axis=-1, keepdims=True)
    fc_c = jnp.sum((col <= row).astype(F32) * lf_r, axis=-1, keepdims=True)
    fc_r = jnp.sum((row <= col).astype(F32) * lf_c, axis=-2, keepdims=True)
    f_last = jnp.sum(lf_r, axis=-1, keepdims=True)
    kc = k * (HEAD_W ** -0.5)
    a_c = f_last - fc_c + li_c
    m_loc = jnp.max(a_c, axis=-2, keepdims=True)
    kw = kc * jnp.exp(a_c - m_loc)
    c_chunk = _bdot(kw, v, "tn")
    n_chunk = jnp.sum(kw, axis=-2, keepdims=True)
    m_new = jnp.maximum(f_last + m, m_loc)
    sp = jnp.exp(f_last + m - m_new)
    sl = jnp.exp(m_loc - m_new)
    cm_new = sp * cm + sl * c_chunk
    nv_new = sp * nv + sl * n_chunk
    log_d = li_r - _abs(fc_c - fc_r)
    g_inter = fc_c + m
    m_t = jnp.maximum(g_inter, jnp.max(log_d, axis=-1, keepdims=True))
    s = _bdot(q, kc, "nt") * jnp.exp(log_d - m_t)
    sc = jnp.exp(g_inter - m_t)
    num = _bdot(s, v, "nn") + sc * _bdot(q, cm, "nn")
    den = jnp.sum(s, axis=-1, keepdims=True) + sc * jnp.sum(q * nv, axis=-1, keepdims=True)
    den = jnp.maximum(_abs(den), jnp.exp(-m_t))
    return num / den, cm_new, nv_new, m_new


def _ml_specs(nc, rev):
    def ch(n):
        return (nc - 1 - n) if rev else n
    tm = pl.BlockSpec((CHUNK, HEADS * HEAD_W), lambda n: (ch(n), 0))
    gate = pl.BlockSpec((HEADS, None, 1, CHUNK), lambda n: (0, ch(n), 0, 0))
    cm = pl.BlockSpec((HEADS, None, HEAD_W, HEAD_W), lambda n: (0, ch(n), 0, 0))
    vec = pl.BlockSpec((HEADS, None, 1, HEAD_W), lambda n: (0, ch(n), 0, 0))
    return tm, gate, cm, vec


_ML_STATE = [pltpu.VMEM((HEADS, HEAD_W, HEAD_W), F32), pltpu.VMEM((HEADS, 1, HEAD_W), F32), pltpu.VMEM((HEADS, 1, HEAD_W), F32)]


def _ml_fwd(q, k, v, li, lf):
    t = q.shape[0]
    nc = t // CHUNK
    tm, gate, cm, vec = _ml_specs(nc, False)

    def body(q_ref, k_ref, v_ref, li_ref, lf_ref, hc_ref, cp_ref, np_ref, mp_ref, c_ref, n_ref, m_ref):
        @pl.when(pl.program_id(0) == 0)
        def _():
            c_ref[...] = jnp.zeros_like(c_ref)
            n_ref[...] = jnp.zeros_like(n_ref)
            m_ref[...] = jnp.zeros_like(m_ref)

        c0, n0, m0 = c_ref[...], n_ref[...], m_ref[...]
        cp_ref[...] = c0
        np_ref[...] = n0
        mp_ref[...] = m0
        hc, c1, n1, m1 = _ml_chunk(_heads(q_ref), _heads(k_ref), _heads(v_ref), li_ref[...], lf_ref[...],
                                   c0, n0, m0[:, :, 0:1])
        _put_heads(hc_ref, hc)
        c_ref[...] = c1
        n_ref[...] = n1
        m_ref[...] = jnp.broadcast_to(m1, m_ref.shape)

    return pl.pallas_call(
        body, name="mlstm_fwd", grid=(nc,),
        in_specs=[tm, tm, tm, gate, gate], out_specs=[tm, cm, vec, vec],
        out_shape=[jax.ShapeDtypeStruct((t, HEADS * HEAD_W), F32), jax.ShapeDtypeStruct((HEADS, nc, HEAD_W, HEAD_W), F32),
                   jax.ShapeDtypeStruct((HEADS, nc, 1, HEAD_W), F32), jax.ShapeDtypeStruct((HEADS, nc, 1, HEAD_W), F32)],
        scratch_shapes=_ML_STATE,
        compiler_params=_cparams("arbitrary"),
    )(q, k, v, li, lf)


def _ml_bwd(q, k, v, li, lf, cp, npv, mp, dhc):
    t = q.shape[0]
    nc = t // CHUNK
    tm, gate, cm, vec = _ml_specs(nc, True)

    def body(q_ref, k_ref, v_ref, li_ref, lf_ref, cp_ref, np_ref, mp_ref, dhc_ref,
             dq_ref, dk_ref, dv_ref, dli_ref, dlf_ref, dc_ref, dn_ref, dm_ref):
        @pl.when(pl.program_id(0) == 0)
        def _():
            dc_ref[...] = jnp.zeros_like(dc_ref)
            dn_ref[...] = jnp.zeros_like(dn_ref)
            dm_ref[...] = jnp.zeros_like(dm_ref)

        _, vjp = jax.vjp(_ml_chunk, _heads(q_ref), _heads(k_ref), _heads(v_ref), li_ref[...], lf_ref[...],
                         cp_ref[...], np_ref[...], mp_ref[...][:, :, 0:1])
        dq, dk, dv, dli, dlf, dc, dn, dm = vjp((_heads(dhc_ref), dc_ref[...], dn_ref[...], dm_ref[...][:, :, 0:1]))
        _put_heads(dq_ref, dq)
        _put_heads(dk_ref, dk)
        _put_heads(dv_ref, dv)
        dli_ref[...] = dli
        dlf_ref[...] = dlf
        dc_ref[...] = dc
        dn_ref[...] = dn
        dm_ref[...] = jnp.broadcast_to(dm, dm_ref.shape)

    tm_shape = jax.ShapeDtypeStruct((t, HEADS * HEAD_W), F32)
    gate_shape = jax.ShapeDtypeStruct((HEADS, nc, 1, CHUNK), F32)
    return pl.pallas_call(
        body, name="mlstm_bwd", grid=(nc,),
        in_specs=[tm, tm, tm, gate, gate, cm, vec, vec, tm], out_specs=[tm, tm, tm, gate, gate],
        out_shape=[tm_shape, tm_shape, tm_shape, gate_shape, gate_shape],
        scratch_shapes=_ML_STATE,
        compiler_params=_cparams("arbitrary"),
    )(q, k, v, li, lf, cp, npv, mp, dhc)


def _ml_pre(s0, s1, s2, s3, cw0, cw1, cw2, cw3, cb, wq, wk, wv, wiq, wik, wiv, bif):
    pre = cb + cw0 * s0 + cw1 * s1 + cw2 * s2 + cw3 * s3
    xc = pre * _sigmoid(pre)
    q = _bdot(xc, wq, "nn")
    k = _bdot(xc, wk, "nn")
    v = _bdot(s3, wv, "nn")
    gates = _bdot(q, wiq, "nn") + _bdot(k, wik, "nn") + _bdot(v, wiv, "nn") + bif
    lane = lax.broadcasted_iota(jnp.int32, gates.shape, 1)
    gl = jnp.where(lane < HEADS, gates, _log_sigmoid(gates))
    return xc, q, k, v, gl


def _delayed(xs_ref, x_ref, halo_ref, r):
    xs_ref[0:HALO, :] = halo_ref[...]
    xs_ref[HALO:HALO + r, :] = x_ref[...]
    return [xs_ref[pl.ds(HALO - (CONV_K - 1) + j, r), :] for j in range(CONV_K)]


def _full_spec(shape):
    return pl.BlockSpec(shape, lambda i, nd=len(shape): (0,) * nd)


def _ml_pre_fwd(x_m, x_pad, params, tile=256):
    t, w = x_m.shape
    r = min(tile, t)

    def body(*refs):
        x_ref, halo_ref = refs[:2]
        p = [ref[...] for ref in refs[2:2 + len(params)]]
        outs = refs[2 + len(params):-1]
        res = _ml_pre(*_delayed(refs[-1], x_ref, halo_ref, r), *p)
        for ref, val in zip(outs, res):
            ref[...] = val

    row = pl.BlockSpec((r, w), lambda i: (i, 0))
    return pl.pallas_call(
        body, name="ml_pre_fwd", grid=(t // r,),
        in_specs=[row, pl.BlockSpec((HALO, w), lambda i: (i * (r // HALO), 0))] + [_full_spec(p.shape) for p in params],
        out_specs=[row] * 4 + [pl.BlockSpec((r, LANES), lambda i: (i, 0))],
        out_shape=[jax.ShapeDtypeStruct((t, w), F32)] * 4 + [jax.ShapeDtypeStruct((t, LANES), F32)],
        scratch_shapes=[pltpu.VMEM((r + HALO, w), F32)],
        compiler_params=_cparams("arbitrary"),
    )(x_m, x_pad, *params)


def _ml_pre_bwd(x_m, x_pad, params, cts, tile=256):
    t, w = x_m.shape
    r = min(tile, t)
    nt = t // r
    n_p = len(params)

    def body(*refs):
        x_ref, halo_ref = refs[:2]
        p = [ref[...] for ref in refs[2:2 + n_p]]
        ct = [ref[...] for ref in refs[2 + n_p:7 + n_p]]
        dx_ref = refs[7 + n_p]
        dp_refs = refs[8 + n_p:8 + 2 * n_p]
        xs_ref, ds_ref, carry_ref = refs[8 + 2 * n_p:]
        step = pl.program_id(0)

        @pl.when(step == 0)
        def _():
            ds_ref[...] = jnp.zeros_like(ds_ref)
            carry_ref[...] = jnp.zeros_like(carry_ref)

        _, vjp = jax.vjp(_ml_pre, *_delayed(xs_ref, x_ref, halo_ref, r), *p)
        grads = vjp(tuple(ct))
        for j in range(CONV_K):
            ds_ref[j, HALO:HALO + r, :] = grads[j]
        lead = HALO + CONV_K - 1
        d_tile = sum(ds_ref[j, pl.ds(lead - j, r), :] for j in range(CONV_K))
        d_halo = sum(ds_ref[j, pl.ds(CONV_K - 1 - j, HALO), :] for j in range(CONV_K))
        dx_ref[...] = d_tile
        dx_ref[r - HALO:r, :] += carry_ref[...]
        carry_ref[...] = d_halo
        _accumulate(step, dp_refs, grads[CONV_K:])

    row = pl.BlockSpec((r, w), lambda i: (nt - 1 - i, 0))
    return pl.pallas_call(
        body, name="ml_pre_bwd", grid=(nt,),
        in_specs=[row, pl.BlockSpec((HALO, w), lambda i: ((nt - 1 - i) * (r // HALO), 0))] + [_full_spec(p.shape) for p in params]
        + [row] * 4 + [pl.BlockSpec((r, LANES), lambda i: (nt - 1 - i, 0))],
        out_specs=[row] + [_full_spec(p.shape) for p in params],
        out_shape=[jax.ShapeDtypeStruct((t, w), F32)] + [jax.ShapeDtypeStruct(p.shape, F32) for p in params],
        scratch_shapes=[pltpu.VMEM((r + HALO, w), F32), pltpu.VMEM((CONV_K, r + 2 * HALO, w), F32), pltpu.VMEM((HALO, w), F32)],
        compiler_params=_cparams("arbitrary"),
    )(x_m, x_pad, *params, *cts)


def _per_head(fn, row_vals, head_params, shared_params=()):
    return [fn(*[a[:, hs] for a in row_vals], *[p[:, hs] for p in head_params], *shared_params) for hs in _head_slices(HEAD_W)]


def _gla_out(o, g, gn):
    return _rms(o, gn) * (g * _sigmoid(g))


def _ml_out(hc, op, xc, g, sk):
    hcell = hc * _sigmoid(op)
    mu = jnp.mean(hcell, axis=-1, keepdims=True)
    d = hcell - mu
    var = jnp.mean(d * d, axis=-1, keepdims=True)
    return d * lax.rsqrt(var + EPS) * g + sk * xc


def _log_decay(al, w, b):
    return _log_sigmoid(_bdot(al, w, "nn") + b) * (1.0 / GLA_GATE_NORM)


def _merge(ga, gb, ya, yb):
    return _sigmoid(ga) * ya + _sigmoid(gb) * yb


def _post_mix(x, z, gpm, gpl):
    x1 = x + _rms(z, gpm)
    return x1, _rms(x1, gpl)


def _loss_rows(x1, dn, tgt, g):
    e = x1 + _rms(dn, g) - tgt
    return 0.5 * jnp.sum(jnp.mean(e * e, axis=-1, keepdims=True), axis=0, keepdims=True)


def _lin(p):
    return 4 * p[0] + 2 * p[1] + p[2]


def _me():
    return lax.axis_index("x"), lax.axis_index("y"), lax.axis_index("c")


def _flip(p, k):
    return tuple((1 - v) if (k >> (2 - i)) & 1 else v for i, v in enumerate(p))


ANY = pl.BlockSpec(memory_space=pl.ANY)


def _allgather_big(shards, name, deps=()):
    n = len(shards)
    n_in = n + len(deps)

    def body(*refs):
        ins, outs = refs[:n], refs[n_in:n_in + n]
        send_sems, recv_sems, local_sems = refs[n_in + n:]
        me = _me()
        x, y, c = me
        sib = (x, y, 1 - c)
        chips = [(1 - x, y), (x, 1 - y), (1 - x, 1 - y)]

        def cp(a, k, block, to, src=None):
            dst = outs[a].at[_lin(block)]
            return pltpu.make_async_remote_copy(src_ref=dst if src is None else src, dst_ref=dst,
                                                send_sem=send_sems.at[a * 7 + k], recv_sem=recv_sems.at[a * 7 + k],
                                                device_id=to, device_id_type=MESH)

        mine = [pltpu.make_async_copy(ins[a], outs[a].at[_lin(me)], local_sems.at[a]) for a in range(n)]
        for m in mine:
            m.start()
        first = []
        for a in range(n):
            first.append(cp(a, 0, me, sib, src=ins[a]))
            first += [cp(a, 1 + j, me, (*chip, c), src=ins[a]) for j, chip in enumerate(chips)]
        for f in first:
            f.start()
        passed = []
        for j, chip in enumerate(chips):
            for a in range(n):
                cp(a, 1 + j, (*chip, c), me).wait_recv()
                fwd = cp(a, 4 + j, (*chip, c), sib)
                fwd.start()
                passed.append(fwd)
        for a in range(n):
            cp(a, 0, sib, me).wait_recv()
            for j, chip in enumerate(chips):
                cp(a, 4 + j, (*chip, 1 - c), me).wait_recv()
        for f in first + passed:
            f.wait_send()
        for m in mine:
            m.wait()

    return pl.pallas_call(
        body, name=name,
        in_specs=[ANY] * n_in, out_specs=[ANY] * n,
        out_shape=[jax.ShapeDtypeStruct((N_DEV, *s.shape), s.dtype) for s in shards],
        scratch_shapes=[pltpu.SemaphoreType.DMA((7 * n,)), pltpu.SemaphoreType.DMA((7 * n,)), pltpu.SemaphoreType.DMA((n,))],
    )(*shards, *deps)


HBM =pl.BlockSpec(memory_space=pltpu.HBM)
SEM = pl.BlockSpec(memory_space=pltpu.SEMAPHORE)
DATAFLOW = pltpu.SideEffectType.DATAFLOW_SIDE_EFFECTING


def _peer_copies(kind, srcs, lands, send_sems, recv_sems):
    me = _me()
    copies = []
    for a, (src, land) in enumerate(zip(srcs, lands)):
        for k in range(1, N_DEV):
            peer = _flip(me, k)
            copies.append(pltpu.make_async_remote_copy(
                src_ref=src if kind == "gather" else src.at[_lin(peer)], dst_ref=land.at[_lin(me)],
                send_sem=send_sems.at[a * 7 + k - 1], recv_sem=recv_sems.at[a * 7 + k - 1],
                device_id=peer, device_id_type=MESH))
    return copies


def _copies_start(kind, srcs, name, after=None):
    n = len(srcs)
    extra = [] if after is None else [after]
    land_shapes = [((N_DEV, *s.shape) if kind == "gather" else s.shape) for s in srcs]

    def body(*refs):
        sems = refs[2 * n + len(extra):]
        for cp in _peer_copies(kind, refs[:n], refs[n:2 * n], sems[0], sems[1]):
            cp.start()
        refs[-1][...] = jnp.zeros_like(refs[-1])

    def hbm(a):
        return pltpu.with_memory_space_constraint(a, pltpu.HBM)

    out = pl.pallas_call(
        body, name=name,
        out_shape=(pltpu.SemaphoreType.DMA((7 * n,)), pltpu.SemaphoreType.DMA((7 * n,)),
                   *[pltpu.HBM(s.shape, s.dtype) for s in srcs],
                   *[pltpu.HBM(ls, s.dtype) for ls, s in zip(land_shapes, srcs)],
                   jax.ShapeDtypeStruct((8, LANES), F32)),
        in_specs=[HBM] * (2 * n) + [ANY] * len(extra),
        out_specs=(SEM, SEM, *[HBM] * (2 * n), pl.BlockSpec(memory_space=pltpu.VMEM)),
        input_output_aliases={i: 2 + i for i in range(2 * n)},
        compiler_params=pltpu.CompilerParams(has_side_effects=DATAFLOW),
    )(*[hbm(s) for s in srcs], *[hbm(lax.empty(ls, s.dtype)) for ls, s in zip(land_shapes, srcs)], *extra)
    return (kind, n, out[:-1]), out[-1]


def _copies_wait(state, after, name):
    kind, n, (send_sems, recv_sems, *thru) = state
    after = list(after) if isinstance(after, (list, tuple)) else [after]

    def body(*refs):
        for cp in _peer_copies(kind, refs[:n], refs[n:2 * n], refs[2 * n], refs[2 * n + 1]):
            cp.wait_send()
            cp.wait_recv()

    out = pl.pallas_call(
        body, name=name,
        out_shape=tuple(pltpu.HBM(t.shape, t.dtype) for t in thru),
        in_specs=[HBM] * (2 * n) + [SEM, SEM] + [ANY] * len(after), out_specs=tuple([HBM] * (2 * n)),
        input_output_aliases={i: i for i in range(2 * n)},
        compiler_params=pltpu.CompilerParams(has_side_effects=DATAFLOW),
    )(*thru, send_sems, recv_sems, *after)
    return out[:n], out[n:]


def _adamw(w, g, m, v):
    m2 = ADAM_B1 * m + (1.0 - ADAM_B1) * g
    v2 = ADAM_B2 * v + (1.0 - ADAM_B2) * (g * g)
    m_hat = m2 / (1.0 - ADAM_B1 ** ADAM_STEP)
    v_hat = v2 / (1.0 - ADAM_B2 ** ADAM_STEP)
    delta = -ADAM_LR * (m_hat / (jnp.sqrt(v_hat) + ADAM_EPS) + ADAM_WD * w)
    return delta, m2, v2


def _sum_adamw(land, part, me_idx, w, m, v, name, tile=256):
    r, c = w.shape
    tr = min(tile, r)

    def body(me_ref, own_ref, *refs):
        slots = refs[:N_DEV]
        w_ref, m_ref, v_ref, g_ref, d_ref, m2_ref, v2_ref = refs[N_DEV:]
        own = own_ref[...].astype(F32)
        g = None
        for s in range(N_DEV):
            term = jnp.where(me_ref[0] == s, own, slots[s][...].astype(F32))
            g = term if g is None else g + term
        d, m2, v2 = _adamw(w_ref[...], g, m_ref[...], v_ref[...])
        g_ref[...] = g
        d_ref[...] = d
        m2_ref[...] = m2
        v2_ref[...] = v2

    def slot_spec(s):
        return pl.BlockSpec((None, tr, c), lambda i, me: (jnp.where(me[0] == s, (s + 1) % N_DEV, s), i, 0))

    row = pl.BlockSpec((tr, c), lambda i, me: (i, 0))
    return pl.pallas_call(
        body, name=name,
        grid_spec=pltpu.PrefetchScalarGridSpec(
            num_scalar_prefetch=1, grid=(r // tr,),
            in_specs=[pl.BlockSpec((None, tr, c), lambda i, me: (me[0], i, 0))] + [slot_spec(s) for s in range(N_DEV)] + [row] * 3,
            out_specs=[row] * 4),
        out_shape=[jax.ShapeDtypeStruct((r, c), F32)] * 4,
        compiler_params=_cparams("parallel"),
    )(me_idx, part, *[land] * N_DEV, w, m, v)


def _sum_slots(gathered, name):
    _, r, w = gathered.shape

    def body(p_ref, o_ref):
        g = p_ref[0]
        for s in range(1, N_DEV):
            g = g + p_ref[s]
        o_ref[...] = g

    return pl.pallas_call(body, name=name, out_shape=jax.ShapeDtypeStruct((r, w), F32))(gathered)


def _pack(arrs):
    flat = jnp.concatenate([a.reshape(-1).astype(F32) for a in arrs])
    rows = -(-flat.shape[0] // (8 * LANES)) * 8
    return jnp.pad(flat, (0, rows * LANES - flat.shape[0])).reshape(rows, LANES)


def _unpack(packed, shapes):
    flat = packed.reshape(-1)
    out, off = [], 0
    for s in shapes:
        size = 1
        for d in s:
            size *= d
        out.append(flat[off:off + size].reshape(s))
        off += size
    return out


def _to_hm(a, d):
    t = a.shape[0]
    return a.reshape(t, HEADS, d).transpose(1, 0, 2)


def _from_hm(a):
    h, t, d = a.shape
    return a.transpose(1, 0, 2).reshape(t, h * d)


def _gate_rows(g):
    t = g.shape[0]
    return g.T.reshape(HEADS, t // CHUNK, 1, CHUNK)


def _gate_cols(g):
    h, nc, _, c = g.shape
    return g.reshape(h, nc * c).T


def _blockdiag_dense(w):
    n = w.shape[0] * QKV_BLOCK
    tiled = jnp.tile(w.reshape(n, QKV_BLOCK), (1, n // QKV_BLOCK))
    r = lax.broadcasted_iota(jnp.int32, (n, n), 0)
    c = lax.broadcasted_iota(jnp.int32, (n, n), 1)
    return jnp.where(r // QKV_BLOCK == c // QKV_BLOCK, tiled, 0.0)


def _blockdiag_blocks(dense):
    n = dense[0].shape[0]
    k = len(dense)

    def body(*refs):
        r = lax.broadcasted_iota(jnp.int32, (n, n), 0)
        c = lax.broadcasted_iota(jnp.int32, (n, n), 1)
        fr = lax.broadcasted_iota(jnp.int32, (n, LANES), 0)
        fc = lax.broadcasted_iota(jnp.int32, (n, LANES), 1)
        fold = ((fr & (QKV_BLOCK - 1)) == fc).astype(BF16)
        for i in range(k):
            kept = jnp.where((r >> 2) == (c >> 2), refs[i][...], 0.0)
            refs[k + i][...] = sum(lax.dot_general(t, fold, _dims("nn", 2), preferred_element_type=F32) for t in _split3(kept))

    out = pl.pallas_call(body, name="blockdiag_blocks", out_shape=[jax.ShapeDtypeStruct((n, LANES), F32)] * k)(*dense)
    return [o[:, 0:QKV_BLOCK].reshape(n // QKV_BLOCK, QKV_BLOCK, QKV_BLOCK) for o in out]


def _col_blocks(w):
    k, n = w.shape
    return w.reshape(k, N_DEV, n // N_DEV).transpose(1, 0, 2)


def _from_col_blocks(g):
    d, k, n = g.shape
    return g.transpose(1, 0, 2).reshape(k, d * n)


def _local_step(x, tgt, weight, ws, prefetch, on_grads, on_small):
    t, d = x.shape
    g1 = ws["g_pre_mix"]

    def dep(token):
        return () if token is None else (token,)

    (h,) = _rowwise("pre_mix_norm", lambda xv, g: ((_rms(xv, g),), ()), [x], [g1], [(d, BF16)])
    fetch_mix = prefetch(("w_pa", "w_pb", "w_o"), h)
    proj = _mm(h, weight("w_in", h), "nn", F32, "proj_in", tm=512)
    offs = [0]
    for s in IN_SPLITS:
        offs.append(offs[-1] + s)
    q_a, k_a, v_a, g_a, a_low, x_m, o_pre, gate_a, gate_b = [proj[:, offs[i]:offs[i + 1]] for i in range(9)]

    a_low_p = jnp.pad(a_low, ((0, 0), (0, LANES - LOWRANK)))
    w_a_up_p = jnp.pad(ws["w_a_up"], ((0, LANES - LOWRANK), (0, 0)))
    b_a_up = ws["b_a_up"]
    (la,) = _rowwise("gla_decay", lambda al, w, b: ((_log_decay(al, w, b),), ()), [a_low_p], [w_a_up_p, b_a_up],
                     [(HEADS * GLA_DK, F32)], deps=dep(fetch_mix))
    fetch_up = prefetch(("w_up",), la)
    q_hm, k_hm, la_hm = _to_hm(q_a, GLA_DK), _to_hm(k_a, GLA_DK), _to_hm(la, GLA_DK)
    o_gla, s_prev = _gla_fwd(q_hm, k_hm, v_a, la_hm)
    gn = ws["g_gla_norm"]
    (ya_in,) = _rowwise("gla_out", lambda o, g, n_: ((jnp.concatenate(_per_head(_gla_out, [o, g], [], [n_]), axis=1),), ()),
                        [o_gla, g_a], [gn], [(HEADS * HEAD_W, BF16)], deps=dep(fetch_up))
    y_a = _mm(ya_in, weight("w_pa", ya_in), "nn", F32, "proj_a")

    cw = ws["conv_w"]
    w_if_p = jnp.pad(ws["w_if"], ((0, 0), (0, LANES - 2 * HEADS)))
    ml_w = HEADS * HEAD_W
    pre_params = [cw[0:1], cw[1:2], cw[2:3], cw[3:4], ws["conv_b"],
                  _blockdiag_dense(ws["w_q_ml"]), _blockdiag_dense(ws["w_k_ml"]), _blockdiag_dense(ws["w_v_ml"]),
                  w_if_p[0:ml_w], w_if_p[ml_w:2 * ml_w], w_if_p[2 * ml_w:3 * ml_w],
                  jnp.pad(ws["b_if"], ((0, 0), (0, LANES - 2 * HEADS)))]
    x_pad = jnp.pad(x_m, ((HALO, 0), (0, 0)))
    xc, q_m, k_m, v_m, gl = _ml_pre_fwd(x_m, x_pad, pre_params)
    li, lf = _gate_rows(gl[:, 0:HEADS]), _gate_rows(gl[:, HEADS:2 * HEADS])
    hc, c_prev, n_prev, m_prev = _ml_fwd(q_m, k_m, v_m, li, lf)
    fetch_down = prefetch(("w_down",), hc)
    g_ml, skip = ws["g_ml_norm"], ws["ml_skip"]
    (h_b,) = _rowwise("mlstm_out", lambda a, b, c_, g, s: ((jnp.concatenate(_per_head(_ml_out, [a, b, c_], [g, s]), axis=1),), ()),
                      [hc, o_pre, xc], [g_ml, skip], [(ml_w, BF16)], deps=dep(fetch_down))
    y_b = _mm(h_b, weight("w_pb", h_b), "nn", F32, "proj_b")

    (merged,) = _rowwise("merge", lambda ga, gb, ya, yb: ((_merge(ga, gb, ya, yb),), ()), [gate_a, gate_b, y_a, y_b], [],
                         [(d, BF16)])
    z = _mm(merged, weight("w_o", merged), "nn", F32, "proj_o")
    gpm, gpl, gpo = ws["g_post_mix"], ws["g_pre_mlp"], ws["g_post_mlp"]
    x1, h2 = _rowwise("post_mix", lambda xv, zv, a, b: (_post_mix(xv, zv, a, b), ()), [x, z], [gpm, gpl], [(d, F32), (d, BF16)])
    up, u = _mm(h2, weight("w_up", h2), "nn", (F32, BF16), "mlp_up", epilogue=lambda p: (p, jnp.square(jnp.maximum(p, 0.0))))
    dn = _mm(u, weight("w_down", u), "nn", F32, "mlp_down", tm=512)

    def loss_and_grads(x1v, dnv, tgtv, g):
        loss, vjp = jax.vjp(lambda a, b, c_: _loss_rows(a, b, tgtv, c_), x1v, dnv, g)
        dx1, ddn, dg = vjp(jnp.ones((1, 1), F32))
        return (dx1, ddn), (jnp.broadcast_to(loss, (1, LANES)), dg)

    dx1_y, d_dn, loss, d_gpo = _rowwise("loss", loss_and_grads, [x1, dn, tgt], [gpo], [(d, F32), (d, BF16)],
                                        [((1, LANES), F32), ((1, d), F32)])

    (d_up,) = _mm(d_dn, weight("w_down", u), "nt", (BF16,), "mlp_down_dx", extra=[up],
                  epilogue=lambda p, a: (p * (2.0 * jnp.maximum(a, 0.0)),))
    dw_down = _mm(u, d_dn, "tn", BF16, "mlp_down_dw", tm=512)
    d_h2 = _mm(d_up, weight("w_up", h2), "nt", F32, "mlp_up_dx", tm=512)
    dw_up = _mm(h2, d_up, "tn", BF16, "mlp_up_dw")
    sent_mlp = on_grads(dict(w_down=dw_down, w_up=dw_up))

    def post_mix_bwd(xv, zv, dx1, dh2, a, b):
        _, vjp = jax.vjp(_post_mix, xv, zv, a, b)
        dx, dz, da, db = vjp((dx1, dh2))
        return (dx, dz), (da, db)

    dx_res, d_z, d_gpm, d_gpl = _rowwise("post_mix_bwd", post_mix_bwd, [x, z, dx1_y, d_h2], [gpm, gpl],
                                         [(d, F32), (d, BF16)], [((1, d), F32), ((1, d), F32)], deps=dep(sent_mlp))
    d_merged = _mm(d_z, weight("w_o", merged), "nt", F32, "proj_o_dx")
    dw_o = _mm(merged, d_z, "tn", BF16, "proj_o_dw")
    d_ga, d_gb, d_ya, d_yb = _rowwise("merge_bwd", lambda *v: (jax.vjp(_merge, *v[:4])[1](v[4]), ()),
                                      [gate_a, gate_b, y_a, y_b, d_merged], [], [(d, F32), (d, F32), (d, BF16), (d, BF16)])
    d_ya_in = _mm(d_ya, weight("w_pa", ya_in), "nt", F32, "proj_a_dx")
    dw_pa = _mm(ya_in, d_ya, "tn", BF16, "proj_a_dw")
    d_hb = _mm(d_yb, weight("w_pb", h_b), "nt", F32, "proj_b_dx")
    dw_pb = _mm(h_b, d_yb, "tn", BF16, "proj_b_dw")
    sent_mix = on_grads(dict(w_o=dw_o, w_pa=dw_pa, w_pb=dw_pb))

    def ml_out_bwd(a, b, c_, ct, g, s):
        parts = []
        for hs in _head_slices(HEAD_W):
            _, vjp = jax.vjp(_ml_out, a[:, hs], b[:, hs], c_[:, hs], g[:, hs], s[:, hs])
            parts.append(vjp(ct[:, hs]))
        cat = lambda i: jnp.concatenate([p[i] for p in parts], axis=1)
        return (cat(0), cat(1), cat(2)), (cat(3), cat(4))

    d_hc, d_opre, d_xc, d_gml, d_skip = _rowwise("mlstm_out_bwd", ml_out_bwd, [hc, o_pre, xc, d_hb], [g_ml, skip],
                                                 [(ml_w, F32)] * 3, [((1, ml_w), F32)] * 2, deps=dep(sent_mix))
    d_qm, d_km, d_vm, d_li, d_lf = _ml_bwd(q_m, k_m, v_m, li, lf, c_prev, n_prev, m_prev, d_hc)
    d_gl = jnp.concatenate([_gate_cols(d_li), _gate_cols(d_lf), jnp.zeros((t, LANES - 2 * HEADS), F32)], axis=1)
    pre_grads = _ml_pre_bwd(x_m, x_pad, pre_params, [d_xc, d_qm, d_km, d_vm, d_gl])
    d_xm = pre_grads[0]
    d_cw = jnp.concatenate(pre_grads[1:5], axis=0)
    d_cb = pre_grads[5]
    d_wq, d_wk, d_wv = _blockdiag_blocks(pre_grads[6:9])
    d_wif = jnp.concatenate(pre_grads[9:12], axis=0)[:, 0:2 * HEADS]
    d_bif = pre_grads[12][:, 0:2 * HEADS]

    def gla_out_bwd(o, g, ct, n_):
        parts = []
        for hs in _head_slices(HEAD_W):
            _, vjp = jax.vjp(_gla_out, o[:, hs], g[:, hs], n_)
            parts.append(vjp(ct[:, hs]))
        cat = lambda i: jnp.concatenate([p[i] for p in parts], axis=1)
        return (cat(0), cat(1)), (sum(p[2] for p in parts),)

    d_o, d_g_a, d_gn = _rowwise("gla_out_bwd", gla_out_bwd, [o_gla, g_a, d_ya_in], [gn], [(ml_w, F32)] * 2, [((1, HEAD_W), F32)])
    dq_hm, dk_hm, d_va, dla_hm = _gla_bwd(q_hm, k_hm, v_a, la_hm, s_prev, d_o)

    def decay_bwd(al, ct, w, b):
        _, vjp = jax.vjp(_log_decay, al, w, b)
        dal, dw, db = vjp(ct)
        return (dal,), (dw, db)

    d_alow_p, d_wa_p, d_ba = _rowwise("gla_decay_bwd", decay_bwd, [a_low_p, _from_hm(dla_hm)], [w_a_up_p, b_a_up],
                                      [(LANES, F32)], [(w_a_up_p.shape, F32), (b_a_up.shape, F32)])
    d_proj = jnp.concatenate([_from_hm(dq_hm), _from_hm(dk_hm), d_va, d_g_a, d_alow_p[:, 0:LOWRANK], d_xm, d_opre, d_ga, d_gb],
                             axis=1).astype(BF16)
    d_h = _mm(d_proj, weight("w_in", h), "nt", F32, "proj_in_dx", tm=512)

    def pre_mix_bwd(xv, dh, dres, g):
        _, vjp = jax.vjp(_rms, xv, g)
        dx, dg = vjp(dh)
        return (dx + dres,), (dg,)

    grad_x, d_g1 = _rowwise("pre_mix_norm_bwd", pre_mix_bwd, [x, d_h, dx_res], [g1], [(d, F32)], [((1, d), F32)])
    small = dict(g_pre_mix=d_g1, w_a_up=d_wa_p[0:LOWRANK], b_a_up=d_ba, g_gla_norm=d_gn, conv_w=d_cw, conv_b=d_cb,
                 w_q_ml=d_wq, w_k_ml=d_wk, w_v_ml=d_wv, w_if=d_wif, b_if=d_bif, ml_skip=d_skip, g_ml_norm=d_gml,
                 g_post_mix=d_gpm, g_pre_mlp=d_gpl, g_post_mlp=d_gpo)
    sent_small = on_small(small, loss[:, 0:1])
    dw_in = _mm(h, d_proj, "tn", F32, "proj_in_dw", tm=512, tk=1024, deps=dep(sent_small))
    return grad_x, on_grads(dict(w_in=dw_in))


BIG = ("w_in", "w_pa", "w_pb", "w_o", "w_up", "w_down")
BIG_COL_SHARDED = ("w_in", "w_pa", "w_pb", "w_up")
SMALL_SHARDED = {"w_a_up": 1, "conv_w": 1, "w_if": 0}
SMALL = ("g_pre_mix", "w_a_up", "b_a_up", "g_gla_norm", "conv_w", "conv_b", "w_q_ml", "w_k_ml", "w_v_ml", "w_if", "b_if",
         "ml_skip", "g_ml_norm", "g_post_mix", "g_pre_mlp", "g_post_mlp")
WEIGHTS = ("g_pre_mix", "w_in", "w_a_up", "b_a_up", "g_gla_norm", "conv_w", "conv_b", "w_q_ml", "w_k_ml", "w_v_ml", "w_if", "b_if",
           "ml_skip", "g_ml_norm", "w_pa", "w_pb", "w_o", "g_post_mix", "g_pre_mlp", "w_up", "w_down", "g_post_mlp")


def _my_slice(a, axis):
    n = a.shape[axis] // N_DEV
    return lax.dynamic_slice_in_dim(a, _lin(_me()) * n, n, axis)


def kernel(x, g_pre_mix, w_in, w_a_up, b_a_up, g_gla_norm, conv_w, conv_b, w_q_ml, w_k_ml, w_v_ml, w_if, b_if, ml_skip, g_ml_norm, w_pa, w_pb, w_o, g_post_mix, g_pre_mlp, w_up, w_down, g_post_mlp, loss_target, m_g_pre_mix, m_w_in, m_w_a_up, m_b_a_up, m_g_gla_norm, m_conv_w, m_conv_b, m_w_q_ml, m_w_k_ml, m_w_v_ml, m_w_if, m_b_if, m_ml_skip, m_g_ml_norm, m_w_pa, m_w_pb, m_w_o, m_g_post_mix, m_g_pre_mlp, m_w_up, m_w_down, m_g_post_mlp, v_g_pre_mix, v_w_in, v_w_a_up, v_b_a_up, v_g_gla_norm, v_conv_w, v_conv_b, v_w_q_ml, v_w_k_ml, v_w_v_ml, v_w_if, v_b_if, v_ml_skip, v_g_ml_norm, v_w_pa, v_w_pb, v_w_o, v_g_post_mix, v_g_pre_mlp, v_w_up, v_w_down, v_g_post_mlp):
    args = dict(locals())
    w = {n: args[n][0] for n in WEIGHTS}
    m = {n: args["m_" + n][0] for n in WEIGHTS}
    v = {n: args["v_" + n][0] for n in WEIGHTS}

    me_lin = _lin(_me())
    me_idx = jnp.reshape(me_lin, (1,)).astype(jnp.int32)

    def full_weight(n, g):
        return _from_col_blocks(g) if n in BIG_COL_SHARDED else g.reshape(-1, g.shape[-1])

    def grad_parts(n, g):
        return (_col_blocks(g) if n in BIG_COL_SHARDED else g.reshape(N_DEV, -1, g.shape[-1])).astype(BF16)

    sharded_names = tuple(SMALL_SHARDED)
    small_w = _pack([w[n] for n in sharded_names])
    small_w_state, small_w_token = _copies_start("gather", [small_w], "allgather_start_small_weights")
    ready = {"w_in": full_weight("w_in", _allgather_big([w["w_in"].astype(BF16)], "allgather_w_in", [small_w_token])[0])}
    pending = {}

    def prefetch(group, after):
        state, token = _copies_start("gather", [w[n].astype(BF16) for n in group], "allgather_start_" + group[0], after)
        for n in group:
            pending[n] = (group, state)
        return token

    def weight(n, after):
        if n not in ready:
            group, state = pending[n]
            shards, lands = _copies_wait(state, after, "allgather_wait_" + group[0])
            for gn, shard, land in zip(group, shards, lands):
                ready[gn] = full_weight(gn, lax.dynamic_update_slice(land, shard[None], (me_lin, 0, 0)))
        return ready[n]

    (small_w,), (small_w_land,) = _copies_wait(small_w_state, ready["w_in"], "allgather_wait_small_weights")
    small_g = lax.dynamic_update_slice(small_w_land, small_w[None], (me_lin, 0, 0))
    ws = {n: (w[n].reshape(1, -1) if w[n].ndim == 1 else w[n]) for n in SMALL if n not in SMALL_SHARDED}
    per_dev = [_unpack(small_g[dev], [w[n].shape for n in sharded_names]) for dev in range(N_DEV)]
    for i, n in enumerate(sharded_names):
        ws[n] = jnp.concatenate([per_dev[dev][i] for dev in range(N_DEV)], axis=SMALL_SHARDED[n])

    sent = []

    def on_grads(grads):
        names = tuple(grads)
        state, token = _copies_start("exchange", [grad_parts(n, grads[n]) for n in names], "exchange_start_" + names[0])
        sent.append((names, state))
        return token

    small_sent = {}

    def on_small(small, loss):
        small_sent["shapes"] = [small[n].shape for n in SMALL] + [(1, 1)]
        small_sent["state"], token = _copies_start("gather", [_pack([small[n] for n in SMALL] + [loss])],
                                                   "allgather_start_small_grads")
        return token

    grad_x, last_token = _local_step(x[0], loss_target[0], weight, ws, prefetch, on_grads, on_small)

    out = {}

    def finish(names, state, after):
        parts, lands = _copies_wait(state, after, "exchange_wait_" + names[0])
        for n, part, land in zip(names, parts, lands):
            out[n] = _sum_adamw(land, part, me_idx, w[n], m[n], v[n], "adamw_" + n)

    for names, state in sent[:-1]:
        finish(names, state, [grad_x, last_token])

    (small_vec,), (small_land,) = _copies_wait(small_sent["state"], [grad_x, last_token], "allgather_wait_small_grads")
    vec = _sum_slots(lax.dynamic_update_slice(small_land, small_vec[None], (me_lin, 0, 0)), "sum_small_grads")
    summed = _unpack(vec, small_sent["shapes"])
    g_small = {}
    for n, g in zip(SMALL, summed[:-1]):
        g_small[n] = _my_slice(g, SMALL_SHARDED[n]) if n in SMALL_SHARDED else g
    shapes = [g_small[n].shape for n in SMALL]
    packed = [_pack([d[n].reshape(g_small[n].shape) for n in SMALL]) for d in (w, g_small, m, v)]
    upd = _rowwise("adamw_small", lambda a, b, c_, d_: (_adamw(a, b, c_, d_), ()), packed, [], [(LANES, F32)] * 3, tile=8 * 1024)
    deltas, new_ms, new_vs = [_unpack(p, shapes) for p in upd]
    for i, n in enumerate(SMALL):
        out[n] = (g_small[n], deltas[i], new_ms[i], new_vs[i])
    finish(*sent[-1], [upd[0]] + [out[n][1] for n in BIG if n in out])

    shaped = lambda a, n: a.reshape(args[n].shape)
    return (summed[-1].reshape(()), grad_x[None],
            *[shaped(out[n][0], n) for n in WEIGHTS], *[shaped(out[n][1], n) for n in WEIGHTS],
            *[shaped(out[n][2], n) for n in WEIGHTS], *[shaped(out[n][3], n) for n in WEIGHTS])
```

```python
import functools

import jax
import jax.numpy as jnp
from jax import lax
from jax.experimental import pallas as pl
from jax.experimental.pallas import tpu as pltpu

F32 = jnp.float32
BF16 = jnp.bfloat16
MESH = pl.DeviceIdType.MESH

N_DEV = 8
EPS = 1e-6
CHUNK = 64
HEADS = 4
GLA_DK = 64
HEAD_W = 128
GLA_GATE_NORM = 16.0
LOWRANK = 16
CONV_K = 4
QKV_BLOCK = 4
LANES = 128
HALO = 8
IN_SPLITS = (256, 256, 512, 512, 16, 512, 512, 1024, 1024)

ADAM_LR = 0.001
ADAM_B1 = 0.9
ADAM_B2 = 0.999
ADAM_EPS = 1e-08
ADAM_WD = 0.01
ADAM_STEP = 10

VMEM_LIMIT = 56 * 1024 * 1024


def _cparams(*sem):
    return pltpu.CompilerParams(dimension_semantics=sem, vmem_limit_bytes=VMEM_LIMIT)


def _dims(mode, ndim):
    contract = {"nn": ((ndim - 1,), (ndim - 2,)), "nt": ((ndim - 1,), (ndim - 1,)), "tn": ((ndim - 2,), (ndim - 2,))}[mode]
    return contract, (((0,), (0,)) if ndim == 3 else ((), ()))


def _raw_dot(a, b, mode):
    return lax.dot_general(a.astype(BF16), b.astype(BF16), _dims(mode, a.ndim), preferred_element_type=F32)


@functools.partial(jax.custom_vjp, nondiff_argnums=(2,))
def _bdot(a, b, mode):
    return _raw_dot(a, b, mode)


def _bdot_fwd(a, b, mode):
    return _raw_dot(a, b, mode), (a, b)


def _bdot_bwd(mode, res, ct):
    a, b = res
    if mode == "nn":
        da, db = _raw_dot(ct, b, "nt"), _raw_dot(a, ct, "tn")
    elif mode == "nt":
        da, db = _raw_dot(ct, b, "nn"), _raw_dot(ct, a, "tn")
    else:
        da, db = _raw_dot(b, ct, "nt"), _raw_dot(a, ct, "nn")
    return da.astype(a.dtype), db.astype(b.dtype)


_bdot.defvjp(_bdot_fwd, _bdot_bwd)


def _split3(x):
    hi = x.astype(BF16)
    r1 = x - hi.astype(F32)
    mid = r1.astype(BF16)
    return hi, mid, (r1 - mid.astype(F32)).astype(BF16)


def _split_dot(tri, x):
    if x.ndim == 3:
        tri = jnp.broadcast_to(tri, (x.shape[0], *tri.shape))
    return sum(lax.dot_general(tri, t, _dims("nn", x.ndim), preferred_element_type=F32) for t in _split3(x))


def _tri(n, lower):
    r = lax.broadcasted_iota(jnp.int32, (n, n), 0)
    c = lax.broadcasted_iota(jnp.int32, (n, n), 1)
    return ((c <= r) if lower else (c >= r)).astype(BF16)


@jax.custom_vjp
def _cumsum_rows(x):
    return _split_dot(_tri(x.shape[-2], True), x)


def _cumsum_rows_fwd(x):
    return _cumsum_rows(x), None


def _cumsum_rows_bwd(_, ct):
    return (_split_dot(_tri(ct.shape[-2], False), ct),)


_cumsum_rows.defvjp(_cumsum_rows_fwd, _cumsum_rows_bwd)


def _abs(x):
    return jnp.where(x >= 0, x, -x)


def _sigmoid(x):
    return lax.logistic(x)


def _log_sigmoid(x):
    return jnp.minimum(x, 0.0) - jnp.log(1.0 + jnp.exp(-_abs(x)))


def _rms(x, g):
    return x * lax.rsqrt(jnp.mean(x * x, axis=-1, keepdims=True) + EPS) * g


def _head_slices(w):
    return [slice(h * w, (h + 1) * w) for h in range(HEADS)]


def _heads(ref):
    return jnp.stack([ref[:, hs] for hs in _head_slices(HEAD_W)])


def _put_heads(ref, val):
    for h, hs in enumerate(_head_slices(HEAD_W)):
        ref[:, hs] = val[h]


def _tile(dim, want):
    if dim <= want or dim % LANES:
        return dim
    t = want
    while dim % t:
        t -= LANES
    return t


def _mm(a, b, mode, out_dtype, name, tm=1024, tn=1024, tk=4096, epilogue=None, extra=(), deps=()):
    if mode == "nn":
        (m, k), (k2, n) = a.shape, b.shape
    elif mode == "nt":
        (m, k), (n, k2) = a.shape, b.shape
    else:
        (k, m), (k2, n) = a.shape, b.shape
    assert k == k2, (name, a.shape, b.shape)
    tm, tn, tk = _tile(m, tm), _tile(n, tn), _tile(k, tk)
    nk = k // tk
    out_dtypes = out_dtype if epilogue else (out_dtype,)
    assert nk == 1 or (out_dtype == F32 and not epilogue), name
    n_in = 2 + len(extra)

    def body(*refs):
        p = _raw_dot(refs[0][...], refs[1][...], mode)
        if nk > 1:
            _accumulate(pl.program_id(2), [refs[n_in + len(deps)]], [p])
            return
        outs = epilogue(p, *[r[...] for r in refs[2:n_in]]) if epilogue else (p,)
        for ref, val in zip(refs[n_in + len(deps):], outs):
            ref[...] = val.astype(ref.dtype)

    a_spec = pl.BlockSpec((tk, tm), lambda i, j, kk: (kk, i)) if mode == "tn" else pl.BlockSpec((tm, tk), lambda i, j, kk: (i, kk))
    b_spec = pl.BlockSpec((tn, tk), lambda i, j, kk: (j, kk)) if mode == "nt" else pl.BlockSpec((tk, tn), lambda i, j, kk: (kk, j))
    o_spec = pl.BlockSpec((tm, tn), lambda i, j, kk: (i, j))
    res = pl.pallas_call(
        body, name=name, grid=(m // tm, n // tn, nk),
        in_specs=[a_spec, b_spec] + [o_spec] * len(extra) + [ANY] * len(deps), out_specs=[o_spec] * len(out_dtypes),
        out_shape=[jax.ShapeDtypeStruct((m, n), dt) for dt in out_dtypes],
        compiler_params=_cparams("parallel", "parallel", "arbitrary"),
    )(a, b, *extra, *deps)
    return res if epilogue else res[0]


def _rowwise(name, fn, rows, params, out_rows, out_accs=(), tile=256, deps=()):
    t = rows[0].shape[0]
    r = min(tile, t)
    assert t % r == 0
    n_in, n_or = len(rows) + len(params), len(out_rows)
    n_all = n_in + len(deps)
    params = list(params) + list(deps)

    def body(*refs):
        vals = [ref[...] for ref in refs[:n_in]]
        outs = refs[n_all:]
        ro, ao = fn(*vals)
        for ref, v in zip(outs[:n_or], ro):
            ref[...] = v.astype(ref.dtype)
        if out_accs:
            _accumulate(pl.program_id(0), outs[n_or:], ao)

    def full(shape):
        return pl.BlockSpec(shape, lambda i, nd=len(shape): (0,) * nd)

    return pl.pallas_call(
        body, name=name, grid=(t // r,),
        in_specs=[pl.BlockSpec((r, a.shape[1]), lambda i: (i, 0)) for a in rows] + [full(p.shape) for p in params],
        out_specs=[pl.BlockSpec((r, w), lambda i: (i, 0)) for w, _ in out_rows] + [full(s) for s, _ in out_accs],
        out_shape=[jax.ShapeDtypeStruct((t, w), dt) for w, dt in out_rows] + [jax.ShapeDtypeStruct(s, dt) for s, dt in out_accs],
        compiler_params=_cparams("arbitrary"),
    )(*rows, *params)


def _accumulate(step, refs, vals):
    for ref, v in zip(refs, vals):
        @pl.when(step == 0)
        def _(ref=ref, v=v):
            ref[...] = v.astype(ref.dtype)

        @pl.when(step > 0)
        def _(ref=ref, v=v):
            ref[...] += v.astype(ref.dtype)


def _gla_chunk(q, k, v, la, st):
    c = q.shape[-2]
    row = lax.broadcasted_iota(jnp.int32, (c, c), 0)
    col = lax.broadcasted_iota(jnp.int32, (c, c), 1)
    cum = _cumsum_rows(la)
    cl = jnp.sum(la, axis=-2, keepdims=True)
    ep = jnp.exp(cum)
    en = jnp.exp(-cum)
    qs = q * (GLA_DK ** -0.5)
    qp = qs * ep
    a_f = _bdot(qp, k * en, "nt")
    a_b = _bdot(qs * en, k * ep, "nt")
    sc = jnp.where(row >= col, a_f, a_b)
    o = _bdot(sc, v, "nn") + _bdot(qp, st, "nt")
    kd = k * jnp.exp(cl - cum)
    st_new = st * jnp.exp(cl) + _bdot(v, kd, "tn")
    return o, st_new


def _gla_specs(nc, rev):
    def ch(n):
        return (nc - 1 - n) if rev else n
    hm = pl.BlockSpec((HEADS, CHUNK, GLA_DK), lambda n: (0, ch(n), 0))
    tm = pl.BlockSpec((CHUNK, HEADS * HEAD_W), lambda n: (ch(n), 0))
    st = pl.BlockSpec((HEADS, None, HEAD_W, GLA_DK), lambda n: (0, ch(n), 0, 0))
    return hm, tm, st


def _gla_fwd(q, k, v, la):
    t = v.shape[0]
    nc = t // CHUNK
    hm, tm, st = _gla_specs(nc, False)

    def body(q_ref, k_ref, v_ref, la_ref, o_ref, sp_ref, st_ref):
        @pl.when(pl.program_id(0) == 0)
        def _():
            st_ref[...] = jnp.zeros_like(st_ref)

        s = st_ref[...]
        sp_ref[...] = s
        o, s_new = _gla_chunk(q_ref[...], k_ref[...], _heads(v_ref), la_ref[...], s)
        _put_heads(o_ref, o)
        st_ref[...] = s_new

    return pl.pallas_call(
        body, name="gla_fwd", grid=(nc,),
        in_specs=[hm, hm, tm, hm], out_specs=[tm, st],
        out_shape=[jax.ShapeDtypeStruct((t, HEADS * HEAD_W), F32), jax.ShapeDtypeStruct((HEADS, nc, HEAD_W, GLA_DK), F32)],
        scratch_shapes=[pltpu.VMEM((HEADS, HEAD_W, GLA_DK), F32)],
        compiler_params=_cparams("arbitrary"),
    )(q, k, v, la)


def _gla_bwd(q, k, v, la, sp, do):
    t = v.shape[0]
    nc = t // CHUNK
    hm, tm, st = _gla_specs(nc, True)

    def body(q_ref, k_ref, v_ref, la_ref, sp_ref, do_ref, dq_ref, dk_ref, dv_ref, dla_ref, ds_ref):
        @pl.when(pl.program_id(0) == 0)
        def _():
            ds_ref[...] = jnp.zeros_like(ds_ref)

        _, vjp = jax.vjp(_gla_chunk, q_ref[...], k_ref[...], _heads(v_ref), la_ref[...], sp_ref[...])
        dq, dk, dv, dla, ds = vjp((_heads(do_ref), ds_ref[...]))
        dq_ref[...] = dq
        dk_ref[...] = dk
        _put_heads(dv_ref, dv)
        dla_ref[...] = dla
        ds_ref[...] = ds

    hm_shape = jax.ShapeDtypeStruct((HEADS, t, GLA_DK), F32)
    return pl.pallas_call(
        body, name="gla_bwd", grid=(nc,),
        in_specs=[hm, hm, tm, hm, st, tm], out_specs=[hm, hm, tm, hm],
        out_shape=[hm_shape, hm_shape, jax.ShapeDtypeStruct((t, HEADS * HEAD_W), F32), hm_shape],
        scratch_shapes=[pltpu.VMEM((HEADS, HEAD_W, GLA_DK), F32)],
        compiler_params=_cparams("arbitrary"),
    )(q, k, v, la, sp, do)


def _ml_chunk(q, k, v, li_r, lf_r, cm, nv, m):
    c = q.shape[-2]
    row = lax.broadcasted_iota(jnp.int32, (c, c), 0)
    col = lax.broadcasted_iota(jnp.int32, (c, c), 1)
    eye = (row == col).astype(F32)
    li_c = jnp.sum(eye * li_r, axis=-1, keepdims=True)
    lf_c = jnp.sum(eye * lf_r, axis=-1, keepdims=True)
    fc_c = jnp.sum((col <= row).astype(F32) * lf_r, axis=-1, keepdims=True)
    fc_r = jnp.sum((row <= col).astype(F32) * lf_c, axis=-2, keepdims=True)
    f_last = jnp.sum(lf_r, axis=-1, keepdims=True)
    kc = k * (HEAD_W ** -0.5)
    a_c = f_last - fc_c + li_c
    m_loc = jnp.max(a_c, axis=-2, keepdims=True)
    kw = kc * jnp.exp(a_c - m_loc)
    c_chunk = _bdot(kw, v, "tn")
    n_chunk = jnp.sum(kw, axis=-2, keepdims=True)
    m_new = jnp.maximum(f_last + m, m_loc)
    sp = jnp.exp(f_last + m - m_new)
    sl = jnp.exp(m_loc - m_new)
    cm_new = sp * cm + sl * c_chunk
    nv_new = sp * nv + sl * n_chunk
    log_d = li_r - _abs(fc_c - fc_r)
    g_inter = fc_c + m
    m_t = jnp.maximum(g_inter, jnp.max(log_d, axis=-1, keepdims=True))
    s = _bdot(q, kc, "nt") * jnp.exp(log_d - m_t)
    sc = jnp.exp(g_inter - m_t)
    num = _bdot(s, v, "nn") + sc * _bdot(q, cm, "nn")
    den = jnp.sum(s, axis=-1, keepdims=True) + sc * jnp.sum(q * nv, axis=-1, keepdims=True)
    den = jnp.maximum(_abs(den), jnp.exp(-m_t))
    return num / den, cm_new, nv_new, m_new


def _ml_specs(nc, rev):
    def ch(n):
        return (nc - 1 - n) if rev else n
    tm = pl.BlockSpec((CHUNK, HEADS * HEAD_W), lambda n: (ch(n), 0))
    gate = pl.BlockSpec((HEADS, None, 1, CHUNK), lambda n: (0, ch(n), 0, 0))
    cm = pl.BlockSpec((HEADS, None, HEAD_W, HEAD_W), lambda n: (0, ch(n), 0, 0))
    vec = pl.BlockSpec((HEADS, None, 1, HEAD_W), lambda n: (0, ch(n), 0, 0))
    return tm, gate, cm, vec


_ML_STATE = [pltpu.VMEM((HEADS, HEAD_W, HEAD_W), F32), pltpu.VMEM((HEADS, 1, HEAD_W), F32), pltpu.VMEM((HEADS, 1, HEAD_W), F32)]


def _ml_fwd(q, k, v, li, lf):
    t = q.shape[0]
    nc = t // CHUNK
    tm, gate, cm, vec = _ml_specs(nc, False)

    def body(q_ref, k_ref, v_ref, li_ref, lf_ref, hc_ref, cp_ref, np_ref, mp_ref, c_ref, n_ref, m_ref):
        @pl.when(pl.program_id(0) == 0)
        def _():
            c_ref[...] = jnp.zeros_like(c_ref)
            n_ref[...] = jnp.zeros_like(n_ref)
            m_ref[...] = jnp.zeros_like(m_ref)

        c0, n0, m0 = c_ref[...], n_ref[...], m_ref[...]
        cp_ref[...] = c0
        np_ref[...] = n0
        mp_ref[...] = m0
        hc, c1, n1, m1 = _ml_chunk(_heads(q_ref), _heads(k_ref), _heads(v_ref), li_ref[...], lf_ref[...],
                                   c0, n0, m0[:, :, 0:1])
        _put_heads(hc_ref, hc)
        c_ref[...] = c1
        n_ref[...] = n1
        m_ref[...] = jnp.broadcast_to(m1, m_ref.shape)

    return pl.pallas_call(
        body, name="mlstm_fwd", grid=(nc,),
        in_specs=[tm, tm, tm, gate, gate], out_specs=[tm, cm, vec, vec],
        out_shape=[jax.ShapeDtypeStruct((t, HEADS * HEAD_W), F32), jax.ShapeDtypeStruct((HEADS, nc, HEAD_W, HEAD_W), F32),
                   jax.ShapeDtypeStruct((HEADS, nc, 1, HEAD_W), F32), jax.ShapeDtypeStruct((HEADS, nc, 1, HEAD_W), F32)],
        scratch_shapes=_ML_STATE,
        compiler_params=_cparams("arbitrary"),
    )(q, k, v, li, lf)


def _ml_bwd(q, k, v, li, lf, cp, npv, mp, dhc):
    t = q.shape[0]
    nc = t // CHUNK
    tm, gate, cm, vec = _ml_specs(nc, True)

    def body(q_ref, k_ref, v_ref, li_ref, lf_ref, cp_ref, np_ref, mp_ref, dhc_ref,
             dq_ref, dk_ref, dv_ref, dli_ref, dlf_ref, dc_ref, dn_ref, dm_ref):
        @pl.when(pl.program_id(0) == 0)
        def _():
            dc_ref[...] = jnp.zeros_like(dc_ref)
            dn_ref[...] = jnp.zeros_like(dn_ref)
            dm_ref[...] = jnp.zeros_like(dm_ref)

        _, vjp = jax.vjp(_ml_chunk, _heads(q_ref), _heads(k_ref), _heads(v_ref), li_ref[...], lf_ref[...],
                         cp_ref[...], np_ref[...], mp_ref[...][:, :, 0:1])
        dq, dk, dv, dli, dlf, dc, dn, dm = vjp((_heads(dhc_ref), dc_ref[...], dn_ref[...], dm_ref[...][:, :, 0:1]))
        _put_heads(dq_ref, dq)
        _put_heads(dk_ref, dk)
        _put_heads(dv_ref, dv)
        dli_ref[...] = dli
        dlf_ref[...] = dlf
        dc_ref[...] = dc
        dn_ref[...] = dn
        dm_ref[...] = jnp.broadcast_to(dm, dm_ref.shape)

    tm_shape = jax.ShapeDtypeStruct((t, HEADS * HEAD_W), F32)
    gate_shape = jax.ShapeDtypeStruct((HEADS, nc, 1, CHUNK), F32)
    return pl.pallas_call(
        body, name="mlstm_bwd", grid=(nc,),
        in_specs=[tm, tm, tm, gate, gate, cm, vec, vec, tm], out_specs=[tm, tm, tm, gate, gate],
        out_shape=[tm_shape, tm_shape, tm_shape, gate_shape, gate_shape],
        scratch_shapes=_ML_STATE,
        compiler_params=_cparams("arbitrary"),
    )(q, k, v, li, lf, cp, npv, mp, dhc)


def _ml_pre(s0, s1, s2, s3, cw0, cw1, cw2, cw3, cb, wq, wk, wv, wiq, wik, wiv, bif):
    pre = cb + cw0 * s0 + cw1 * s1 + cw2 * s2 + cw3 * s3
    xc = pre * _sigmoid(pre)
    q = _bdot(xc, wq, "nn")
    k = _bdot(xc, wk, "nn")
    v = _bdot(s3, wv, "nn")
    gates = _bdot(q, wiq, "nn") + _bdot(k, wik, "nn") + _bdot(v, wiv, "nn") + bif
    lane = lax.broadcasted_iota(jnp.int32, gates.shape, 1)
    gl = jnp.where(lane < HEADS, gates, _log_sigmoid(gates))
    return xc, q, k, v, gl


def _delayed(xs_ref, x_ref, halo_ref, r):
    xs_ref[0:HALO, :] = halo_ref[...]
    xs_ref[HALO:HALO + r, :] = x_ref[...]
    return [xs_ref[pl.ds(HALO - (CONV_K - 1) + j, r), :] for j in range(CONV_K)]


def _full_spec(shape):
    return pl.BlockSpec(shape, lambda i, nd=len(shape): (0,) * nd)


def _ml_pre_fwd(x_m, x_pad, params, tile=256):
    t, w = x_m.shape
    r = min(tile, t)

    def body(*refs):
        x_ref, halo_ref = refs[:2]
        p = [ref[...] for ref in refs[2:2 + len(params)]]
        outs = refs[2 + len(params):-1]
        res = _ml_pre(*_delayed(refs[-1], x_ref, halo_ref, r), *p)
        for ref, val in zip(outs, res):
            ref[...] = val

    row = pl.BlockSpec((r, w), lambda i: (i, 0))
    return pl.pallas_call(
        body, name="ml_pre_fwd", grid=(t // r,),
        in_specs=[row, pl.BlockSpec((HALO, w), lambda i: (i * (r // HALO), 0))] + [_full_spec(p.shape) for p in params],
        out_specs=[row] * 4 + [pl.BlockSpec((r, LANES), lambda i: (i, 0))],
        out_shape=[jax.ShapeDtypeStruct((t, w), F32)] * 4 + [jax.ShapeDtypeStruct((t, LANES), F32)],
        scratch_shapes=[pltpu.VMEM((r + HALO, w), F32)],
        compiler_params=_cparams("arbitrary"),
    )(x_m, x_pad, *params)


def _ml_pre_bwd(x_m, x_pad, params, cts, tile=256):
    t, w = x_m.shape
    r = min(tile, t)
    nt = t // r
    n_p = len(params)

    def body(*refs):
        x_ref, halo_ref = refs[:2]
        p = [ref[...] for ref in refs[2:2 + n_p]]
        ct = [ref[...] for ref in refs[2 + n_p:7 + n_p]]
        dx_ref = refs[7 + n_p]
        dp_refs = refs[8 + n_p:8 + 2 * n_p]
        xs_ref, ds_ref, carry_ref = refs[8 + 2 * n_p:]
        step = pl.program_id(0)

        @pl.when(step == 0)
        def _():
            ds_ref[...] = jnp.zeros_like(ds_ref)
            carry_ref[...] = jnp.zeros_like(carry_ref)

        _, vjp = jax.vjp(_ml_pre, *_delayed(xs_ref, x_ref, halo_ref, r), *p)
        grads = vjp(tuple(ct))
        for j in range(CONV_K):
            ds_ref[j, HALO:HALO + r, :] = grads[j]
        lead = HALO + CONV_K - 1
        d_tile = sum(ds_ref[j, pl.ds(lead - j, r), :] for j in range(CONV_K))
        d_halo = sum(ds_ref[j, pl.ds(CONV_K - 1 - j, HALO), :] for j in range(CONV_K))
        dx_ref[...] = d_tile
        dx_ref[r - HALO:r, :] += carry_ref[...]
        carry_ref[...] = d_halo
        _accumulate(step, dp_refs, grads[CONV_K:])

    row = pl.BlockSpec((r, w), lambda i: (nt - 1 - i, 0))
    return pl.pallas_call(
        body, name="ml_pre_bwd", grid=(nt,),
        in_specs=[row, pl.BlockSpec((HALO, w), lambda i: ((nt - 1 - i) * (r // HALO), 0))] + [_full_spec(p.shape) for p in params]
        + [row] * 4 + [pl.BlockSpec((r, LANES), lambda i: (nt - 1 - i, 0))],
        out_specs=[row] + [_full_spec(p.shape) for p in params],
        out_shape=[jax.ShapeDtypeStruct((t, w), F32)] + [jax.ShapeDtypeStruct(p.shape, F32) for p in params],
        scratch_shapes=[pltpu.VMEM((r + HALO, w), F32), pltpu.VMEM((CONV_K, r + 2 * HALO, w), F32), pltpu.VMEM((HALO, w), F32)],
        compiler_params=_cparams("arbitrary"),
    )(x_m, x_pad, *params, *cts)


def _per_head(fn, row_vals, head_params, shared_params=()):
    return [fn(*[a[:, hs] for a in row_vals], *[p[:, hs] for p in head_params], *shared_params) for hs in _head_slices(HEAD_W)]


def _gla_out(o, g, gn):
    return _rms(o, gn) * (g * _sigmoid(g))


def _ml_out(hc, op, xc, g, sk):
    hcell = hc * _sigmoid(op)
    mu = jnp.mean(hcell, axis=-1, keepdims=True)
    d = hcell - mu
    var = jnp.mean(d * d, axis=-1, keepdims=True)
    return d * lax.rsqrt(var + EPS) * g + sk * xc


def _log_decay(al, w, b):
    return _log_sigmoid(_bdot(al, w, "nn") + b) * (1.0 / GLA_GATE_NORM)


def _merge(ga, gb, ya, yb):
    return _sigmoid(ga) * ya + _sigmoid(gb) * yb


def _post_mix(x, z, gpm, gpl):
    x1 = x + _rms(z, gpm)
    return x1, _rms(x1, gpl)


def _loss_rows(x1, dn, tgt, g):
    e = x1 + _rms(dn, g) - tgt
    return 0.5 * jnp.sum(jnp.mean(e * e, axis=-1, keepdims=True), axis=0, keepdims=True)


def _lin(p):
    return 4 * p[0] + 2 * p[1] + p[2]


def _me():
    return lax.axis_index("x"), lax.axis_index("y"), lax.axis_index("c")


def _flip(p, k):
    return tuple((1 - v) if (k >> (2 - i)) & 1 else v for i, v in enumerate(p))


ANY = pl.BlockSpec(memory_space=pl.ANY)


def _allgather_big(shards, name, deps=()):
    n = len(shards)
    n_in = n + len(deps)

    def body(*refs):
        ins, outs = refs[:n], refs[n_in:n_in + n]
        send_sems, recv_sems, local_sems = refs[n_in + n:]
        me = _me()
        x, y, c = me
        sib = (x, y, 1 - c)
        chips = [(1 - x, y), (x, 1 - y), (1 - x, 1 - y)]

        def cp(a, k, block, to, src=None):
            dst = outs[a].at[_lin(block)]
            return pltpu.make_async_remote_copy(src_ref=dst if src is None else src, dst_ref=dst,
                                                send_sem=send_sems.at[a * 7 + k], recv_sem=recv_sems.at[a * 7 + k],
                                                device_id=to, device_id_type=MESH)

        mine = [pltpu.make_async_copy(ins[a], outs[a].at[_lin(me)], local_sems.at[a]) for a in range(n)]
        for m in mine:
            m.start()
        first = []
        for a in range(n):
            first.append(cp(a, 0, me, sib, src=ins[a]))
            first += [cp(a, 1 + j, me, (*chip, c), src=ins[a]) for j, chip in enumerate(chips)]
        for f in first:
            f.start()
        passed = []
        for j, chip in enumerate(chips):
            for a in range(n):
                cp(a, 1 + j, (*chip, c), me).wait_recv()
                fwd = cp(a, 4 + j, (*chip, c), sib)
                fwd.start()
                passed.append(fwd)
        for a in range(n):
            cp(a, 0, sib, me).wait_recv()
            for j, chip in enumerate(chips):
                cp(a, 4 + j, (*chip, 1 - c), me).wait_recv()
        for f in first + passed:
            f.wait_send()
        for m in mine:
            m.wait()

    return pl.pallas_call(
        body, name=name,
        in_specs=[ANY] * n_in, out_specs=[ANY] * n,
        out_shape=[jax.ShapeDtypeStruct((N_DEV, *s.shape), s.dtype) for s in shards],
        scratch_shapes=[pltpu.SemaphoreType.DMA((7 * n,)), pltpu.SemaphoreType.DMA((7 * n,)), pltpu.SemaphoreType.DMA((n,))],
    )(*shards, *deps)


HBM = pl.BlockSpec(memory_space=pltpu.HBM)
SEM = pl.BlockSpec(memory_space=pltpu.SEMAPHORE)
DATAFLOW = pltpu.SideEffectType.DATAFLOW_SIDE_EFFECTING


def _peer_copies(kinds, srcs, lands, send_sems, recv_sems):
    me = _me()
    copies = []
    for a, (kind, src, land) in enumerate(zip(kinds, srcs, lands)):
        for k in range(1, N_DEV):
            peer = _flip(me, k)
            copies.append(pltpu.make_async_remote_copy(
                src_ref=src if kind == "gather" else src.at[_lin(peer)], dst_ref=land.at[_lin(me)],
                send_sem=send_sems.at[a * 7 + k - 1], recv_sem=recv_sems.at[a * 7 + k - 1],
                device_id=peer, device_id_type=MESH))
    return copies


def _copies_start(kind, srcs, name, after=None):
    n = len(srcs)
    extra = [] if after is None else [after]
    kind = [kind] * n if isinstance(kind, str) else list(kind)
    land_shapes = [((N_DEV, *s.shape) if k == "gather" else s.shape) for k, s in zip(kind, srcs)]

    def body(*refs):
        sems = refs[2 * n + len(extra):]
        for cp in _peer_copies(kind, refs[:n], refs[n:2 * n], sems[0], sems[1]):
            cp.start()
        refs[-1][...] = jnp.zeros_like(refs[-1])

    def hbm(a):
        return pltpu.with_memory_space_constraint(a, pltpu.HBM)

    out = pl.pallas_call(
        body, name=name,
        out_shape=(pltpu.SemaphoreType.DMA((7 * n,)), pltpu.SemaphoreType.DMA((7 * n,)),
                   *[pltpu.HBM(s.shape, s.dtype) for s in srcs],
                   *[pltpu.HBM(ls, s.dtype) for ls, s in zip(land_shapes, srcs)],
                   jax.ShapeDtypeStruct((8, LANES), F32)),
        in_specs=[HBM] * (2 * n) + [ANY] * len(extra),
        out_specs=(SEM, SEM, *[HBM] * (2 * n), pl.BlockSpec(memory_space=pltpu.VMEM)),
        input_output_aliases={i: 2 + i for i in range(2 * n)},
        compiler_params=pltpu.CompilerParams(has_side_effects=DATAFLOW),
    )(*[hbm(s) for s in srcs], *[hbm(lax.empty(ls, s.dtype)) for ls, s in zip(land_shapes, srcs)], *extra)
    return (kind, n, out[:-1]), out[-1]


def _copies_wait(state, after, name):
    kind, n, (send_sems, recv_sems, *thru) = state
    after = list(after) if isinstance(after, (list, tuple)) else [after]

    def body(*refs):
        for cp in _peer_copies(kind, refs[:n], refs[n:2 * n], refs[2 * n], refs[2 * n + 1]):
            cp.wait_send()
            cp.wait_recv()

    out = pl.pallas_call(
        body, name=name,
        out_shape=tuple(pltpu.HBM(t.shape, t.dtype) for t in thru),
        in_specs=[HBM] * (2 * n) + [SEM, SEM] + [ANY] * len(after), out_specs=tuple([HBM] * (2 * n)),
        input_output_aliases={i: i for i in range(2 * n)},
        compiler_params=pltpu.CompilerParams(has_side_effects=DATAFLOW),
    )(*thru, send_sems, recv_sems, *after)
    return out[:n], out[n:]


def _adamw(w, g, m, v):
    m2 = ADAM_B1 * m + (1.0 - ADAM_B1) * g
    v2 = ADAM_B2 * v + (1.0 - ADAM_B2) * (g * g)
    m_hat = m2 / (1.0 - ADAM_B1 ** ADAM_STEP)
    v_hat = v2 / (1.0 - ADAM_B2 ** ADAM_STEP)
    delta = -ADAM_LR * (m_hat / (jnp.sqrt(v_hat) + ADAM_EPS) + ADAM_WD * w)
    return delta, m2, v2


def _sum_adamw(land, part, me_idx, w, m, v, name, tile=256):
    r, c = w.shape
    tr = min(tile, r)

    def body(me_ref, own_ref, *refs):
        slots = refs[:N_DEV]
        w_ref, m_ref, v_ref, g_ref, d_ref, m2_ref, v2_ref = refs[N_DEV:]
        own = own_ref[...].astype(F32)
        g = None
        for s in range(N_DEV):
            term = jnp.where(me_ref[0] == s, own, slots[s][...].astype(F32))
            g = term if g is None else g + term
        d, m2, v2 = _adamw(w_ref[...], g, m_ref[...], v_ref[...])
        g_ref[...] = g
        d_ref[...] = d
        m2_ref[...] = m2
        v2_ref[...] = v2

    def slot_spec(s):
        return pl.BlockSpec((None, tr, c), lambda i, me: (jnp.where(me[0] == s, (s + 1) % N_DEV, s), i, 0))

    row = pl.BlockSpec((tr, c), lambda i, me: (i, 0))
    return pl.pallas_call(
        body, name=name,
        grid_spec=pltpu.PrefetchScalarGridSpec(
            num_scalar_prefetch=1, grid=(r // tr,),
            in_specs=[pl.BlockSpec((None, tr, c), lambda i, me: (me[0], i, 0))] + [slot_spec(s) for s in range(N_DEV)] + [row] * 3,
            out_specs=[row] * 4),
        out_shape=[jax.ShapeDtypeStruct((r, c), F32)] * 4,
        compiler_params=_cparams("parallel"),
    )(me_idx, part, *[land] * N_DEV, w, m, v)


def _small_update(me_idx, kinds, lands, owns, ws, ms, vs, loss_land, loss_own):
    n = len(lands)

    def summed(me, land_ref, own):
        g = None
        for s in range(N_DEV):
            term = jnp.where(me == s, own, land_ref[s])
            g = term if g is None else g + term
        return g

    def body(me_ref, *refs):
        land_refs, own_refs, w_refs, m_refs, v_refs = (refs[i * n:(i + 1) * n] for i in range(5))
        loss_land_ref, loss_own_ref = refs[5 * n:5 * n + 2]
        outs = refs[5 * n + 2:]
        me = me_ref[0]
        for i in range(n):
            g = summed(me, land_refs[i], own_refs[i][...])
            d, m2, v2 = _adamw(w_refs[i][...], g, m_refs[i][...], v_refs[i][...])
            for ref, val in zip(outs[4 * i:4 * i + 4], (g, d, m2, v2)):
                ref[...] = val
        outs[-1][...] = summed(me, loss_land_ref, loss_own_ref[...])

    def whole(shape):
        return pl.BlockSpec(shape, lambda i, me, nd=len(shape): (0,) * nd)

    def own_spec(kind, own):
        if kind == "gather":
            return whole(own.shape)
        return pl.BlockSpec((None, *own.shape[1:]), lambda i, me: (me[0], 0, 0))

    shapes = [w.shape for w in ws]
    return pl.pallas_call(
        body, name="adamw_small",
        grid_spec=pltpu.PrefetchScalarGridSpec(
            num_scalar_prefetch=1, grid=(1,),
            in_specs=[whole(a.shape) for a in lands] + [own_spec(k, o) for k, o in zip(kinds, owns)]
            + [whole(s) for s in shapes] * 3 + [whole(loss_land.shape), whole(loss_own.shape)],
            out_specs=[whole(s) for s in shapes for _ in range(4)] + [whole(loss_own.shape)]),
        out_shape=[jax.ShapeDtypeStruct(s, F32) for s in shapes for _ in range(4)] + [jax.ShapeDtypeStruct(loss_own.shape, F32)],
        compiler_params=_cparams("arbitrary"),
    )(me_idx, *lands, *owns, *ws, *ms, *vs, loss_land, loss_own)


def _small_view(n, a):
    if a.ndim == 1:
        return a.reshape(1, -1)
    if a.ndim == 3:
        return a.transpose(1, 2, 0).reshape(QKV_BLOCK * QKV_BLOCK, -1)
    return a.T if n == "w_if" else a


def _small_unview(n, a, shape):
    if len(shape) == 1:
        return a.reshape(shape)
    if len(shape) == 3:
        return a.reshape(QKV_BLOCK, QKV_BLOCK, -1).transpose(2, 0, 1)
    return a.T if n == "w_if" else a


def _small_shards(n, g):
    if n == "w_if":
        return g.reshape(N_DEV, -1, g.shape[1]).transpose(0, 2, 1)
    return g.reshape(g.shape[0], N_DEV, -1).transpose(1, 0, 2)


def _small_unshard(n, s):
    if n == "w_if":
        return s.transpose(0, 2, 1).reshape(-1, s.shape[1])
    return s.transpose(1, 0, 2).reshape(s.shape[1], -1)


def _to_hm(a, d):
    t = a.shape[0]
    return a.reshape(t, HEADS, d).transpose(1, 0, 2)


def _from_hm(a):
    h, t, d = a.shape
    return a.transpose(1, 0, 2).reshape(t, h * d)


def _gate_rows(g):
    t = g.shape[0]
    return g.T.reshape(HEADS, t // CHUNK, 1, CHUNK)


def _gate_cols(g):
    h, nc, _, c = g.shape
    return g.reshape(h, nc * c).T


def _blockdiag_dense(w):
    n = w.shape[0] * QKV_BLOCK
    tiled = jnp.tile(w.reshape(n, QKV_BLOCK), (1, n // QKV_BLOCK))
    r = lax.broadcasted_iota(jnp.int32, (n, n), 0)
    c = lax.broadcasted_iota(jnp.int32, (n, n), 1)
    return jnp.where(r // QKV_BLOCK == c // QKV_BLOCK, tiled, 0.0)


def _blockdiag_blocks(dense):
    n = dense[0].shape[0]
    k = len(dense)

    def body(*refs):
        r = lax.broadcasted_iota(jnp.int32, (n, n), 0)
        c = lax.broadcasted_iota(jnp.int32, (n, n), 1)
        fr = lax.broadcasted_iota(jnp.int32, (n, LANES), 0)
        fc = lax.broadcasted_iota(jnp.int32, (n, LANES), 1)
        fold = ((fr & (QKV_BLOCK - 1)) == fc).astype(BF16)
        for i in range(k):
            kept = jnp.where((r >> 2) == (c >> 2), refs[i][...], 0.0)
            refs[k + i][...] = sum(lax.dot_general(t, fold, _dims("nn", 2), preferred_element_type=F32) for t in _split3(kept))

    out = pl.pallas_call(body, name="blockdiag_blocks", out_shape=[jax.ShapeDtypeStruct((n, LANES), F32)] * k)(*dense)
    return [o[:, 0:QKV_BLOCK].reshape(n // QKV_BLOCK, QKV_BLOCK, QKV_BLOCK) for o in out]


def _col_blocks(w):
    k, n = w.shape
    return w.reshape(k, N_DEV, n // N_DEV).transpose(1, 0, 2)


def _from_col_blocks(g):
    d, k, n = g.shape
    return g.transpose(1, 0, 2).reshape(k, d * n)


def _local_step(x, tgt, weight, ws, prefetch, on_grads, on_small):
    t, d = x.shape
    g1 = ws["g_pre_mix"]

    def dep(token):
        return () if token is None else (token,)

    (h,) = _rowwise("pre_mix_norm", lambda xv, g: ((_rms(xv, g),), ()), [x], [g1], [(d, BF16)])
    fetch_mix = prefetch(("w_pa", "w_pb", "w_o"), h)
    proj = _mm(h, weight("w_in", h), "nn", F32, "proj_in", tm=512)
    offs = [0]
    for s in IN_SPLITS:
        offs.append(offs[-1] + s)
    q_a, k_a, v_a, g_a, a_low, x_m, o_pre, gate_a, gate_b = [proj[:, offs[i]:offs[i + 1]] for i in range(9)]

    a_low_p = jnp.pad(a_low, ((0, 0), (0, LANES - LOWRANK)))
    w_a_up_p = jnp.pad(ws["w_a_up"], ((0, LANES - LOWRANK), (0, 0)))
    b_a_up = ws["b_a_up"]
    (la,) = _rowwise("gla_decay", lambda al, w, b: ((_log_decay(al, w, b),), ()), [a_low_p], [w_a_up_p, b_a_up],
                     [(HEADS * GLA_DK, F32)], deps=dep(fetch_mix))
    fetch_up = prefetch(("w_up",), la)
    q_hm, k_hm, la_hm = _to_hm(q_a, GLA_DK), _to_hm(k_a, GLA_DK), _to_hm(la, GLA_DK)
    o_gla, s_prev = _gla_fwd(q_hm, k_hm, v_a, la_hm)
    gn = ws["g_gla_norm"]
    (ya_in,) = _rowwise("gla_out", lambda o, g, n_: ((jnp.concatenate(_per_head(_gla_out, [o, g], [], [n_]), axis=1),), ()),
                        [o_gla, g_a], [gn], [(HEADS * HEAD_W, BF16)], deps=dep(fetch_up))
    y_a = _mm(ya_in, weight("w_pa", ya_in), "nn", F32, "proj_a")

    cw = ws["conv_w"]
    w_if_p = jnp.pad(ws["w_if"], ((0, 0), (0, LANES - 2 * HEADS)))
    ml_w = HEADS * HEAD_W
    pre_params = [cw[0:1], cw[1:2], cw[2:3], cw[3:4], ws["conv_b"],
                  _blockdiag_dense(ws["w_q_ml"]), _blockdiag_dense(ws["w_k_ml"]), _blockdiag_dense(ws["w_v_ml"]),
                  w_if_p[0:ml_w], w_if_p[ml_w:2 * ml_w], w_if_p[2 * ml_w:3 * ml_w],
                  jnp.pad(ws["b_if"], ((0, 0), (0, LANES - 2 * HEADS)))]
    x_pad = jnp.pad(x_m, ((HALO, 0), (0, 0)))
    xc, q_m, k_m, v_m, gl = _ml_pre_fwd(x_m, x_pad, pre_params)
    li, lf = _gate_rows(gl[:, 0:HEADS]), _gate_rows(gl[:, HEADS:2 * HEADS])
    hc, c_prev, n_prev, m_prev = _ml_fwd(q_m, k_m, v_m, li, lf)
    fetch_down = prefetch(("w_down",), hc)
    g_ml, skip = ws["g_ml_norm"], ws["ml_skip"]
    (h_b,) = _rowwise("mlstm_out", lambda a, b, c_, g, s: ((jnp.concatenate(_per_head(_ml_out, [a, b, c_], [g, s]), axis=1),), ()),
                      [hc, o_pre, xc], [g_ml, skip], [(ml_w, BF16)], deps=dep(fetch_down))
    y_b = _mm(h_b, weight("w_pb", h_b), "nn", F32, "proj_b")

    (merged,) = _rowwise("merge", lambda ga, gb, ya, yb: ((_merge(ga, gb, ya, yb),), ()), [gate_a, gate_b, y_a, y_b], [],
                         [(d, BF16)])
    z = _mm(merged, weight("w_o", merged), "nn", F32, "proj_o")
    gpm, gpl, gpo = ws["g_post_mix"], ws["g_pre_mlp"], ws["g_post_mlp"]
    x1, h2 = _rowwise("post_mix", lambda xv, zv, a, b: (_post_mix(xv, zv, a, b), ()), [x, z], [gpm, gpl], [(d, F32), (d, BF16)])
    up, u = _mm(h2, weight("w_up", h2), "nn", (F32, BF16), "mlp_up", epilogue=lambda p: (p, jnp.square(jnp.maximum(p, 0.0))))
    dn = _mm(u, weight("w_down", u), "nn", F32, "mlp_down", tm=512)

    def loss_and_grads(x1v, dnv, tgtv, g):
        loss, vjp = jax.vjp(lambda a, b, c_: _loss_rows(a, b, tgtv, c_), x1v, dnv, g)
        dx1, ddn, dg = vjp(jnp.ones((1, 1), F32))
        return (dx1, ddn), (jnp.broadcast_to(loss, (1, LANES)), dg)

    dx1_y, d_dn, loss, d_gpo = _rowwise("loss", loss_and_grads, [x1, dn, tgt], [gpo], [(d, F32), (d, BF16)],
                                        [((1, LANES), F32), ((1, d), F32)])

    (d_up,) = _mm(d_dn, weight("w_down", u), "nt", (BF16,), "mlp_down_dx", extra=[up],
                  epilogue=lambda p, a: (p * (2.0 * jnp.maximum(a, 0.0)),))
    dw_down = _mm(u, d_dn, "tn", BF16, "mlp_down_dw", tm=512)
    d_h2 = _mm(d_up, weight("w_up", h2), "nt", F32, "mlp_up_dx", tm=512)
    dw_up = _mm(h2, d_up, "tn", BF16, "mlp_up_dw")
    sent_mlp = on_grads(dict(w_down=dw_down, w_up=dw_up))

    def post_mix_bwd(xv, zv, dx1, dh2, a, b):
        _, vjp = jax.vjp(_post_mix, xv, zv, a, b)
        dx, dz, da, db = vjp((dx1, dh2))
        return (dx, dz), (da, db)

    dx_res, d_z, d_gpm, d_gpl = _rowwise("post_mix_bwd", post_mix_bwd, [x, z, dx1_y, d_h2], [gpm, gpl],
                                         [(d, F32), (d, BF16)], [((1, d), F32), ((1, d), F32)], deps=dep(sent_mlp))
    d_merged = _mm(d_z, weight("w_o", merged), "nt", F32, "proj_o_dx")
    dw_o = _mm(merged, d_z, "tn", BF16, "proj_o_dw")
    d_ga, d_gb, d_ya, d_yb = _rowwise("merge_bwd", lambda *v: (jax.vjp(_merge, *v[:4])[1](v[4]), ()),
                                      [gate_a, gate_b, y_a, y_b, d_merged], [], [(d, F32), (d, F32), (d, BF16), (d, BF16)])
    d_ya_in = _mm(d_ya, weight("w_pa", ya_in), "nt", F32, "proj_a_dx")
    dw_pa = _mm(ya_in, d_ya, "tn", BF16, "proj_a_dw")
    d_hb = _mm(d_yb, weight("w_pb", h_b), "nt", F32, "proj_b_dx")
    dw_pb = _mm(h_b, d_yb, "tn", BF16, "proj_b_dw")
    sent_mix = on_grads(dict(w_o=dw_o, w_pa=dw_pa, w_pb=dw_pb))

    def ml_out_bwd(a, b, c_, ct, g, s):
        parts = []
        for hs in _head_slices(HEAD_W):
            _, vjp = jax.vjp(_ml_out, a[:, hs], b[:, hs], c_[:, hs], g[:, hs], s[:, hs])
            parts.append(vjp(ct[:, hs]))
        cat = lambda i: jnp.concatenate([p[i] for p in parts], axis=1)
        return (cat(0), cat(1), cat(2)), (cat(3), cat(4))

    d_hc, d_opre, d_xc, d_gml, d_skip = _rowwise("mlstm_out_bwd", ml_out_bwd, [hc, o_pre, xc, d_hb], [g_ml, skip],
                                                 [(ml_w, F32)] * 3, [((1, ml_w), F32)] * 2, deps=dep(sent_mix))
    d_qm, d_km, d_vm, d_li, d_lf = _ml_bwd(q_m, k_m, v_m, li, lf, c_prev, n_prev, m_prev, d_hc)
    d_gl = jnp.concatenate([_gate_cols(d_li), _gate_cols(d_lf), jnp.zeros((t, LANES - 2 * HEADS), F32)], axis=1)
    pre_grads = _ml_pre_bwd(x_m, x_pad, pre_params, [d_xc, d_qm, d_km, d_vm, d_gl])
    d_xm = pre_grads[0]
    d_cw = jnp.concatenate(pre_grads[1:5], axis=0)
    d_cb = pre_grads[5]
    d_wq, d_wk, d_wv = _blockdiag_blocks(pre_grads[6:9])
    d_wif = jnp.concatenate(pre_grads[9:12], axis=0)[:, 0:2 * HEADS]
    d_bif = pre_grads[12][:, 0:2 * HEADS]

    def gla_out_bwd(o, g, ct, n_):
        parts = []
        for hs in _head_slices(HEAD_W):
            _, vjp = jax.vjp(_gla_out, o[:, hs], g[:, hs], n_)
            parts.append(vjp(ct[:, hs]))
        cat = lambda i: jnp.concatenate([p[i] for p in parts], axis=1)
        return (cat(0), cat(1)), (sum(p[2] for p in parts),)

    d_o, d_g_a, d_gn = _rowwise("gla_out_bwd", gla_out_bwd, [o_gla, g_a, d_ya_in], [gn], [(ml_w, F32)] * 2, [((1, HEAD_W), F32)])
    dq_hm, dk_hm, d_va, dla_hm = _gla_bwd(q_hm, k_hm, v_a, la_hm, s_prev, d_o)

    def decay_bwd(al, ct, w, b):
        _, vjp = jax.vjp(_log_decay, al, w, b)
        dal, dw, db = vjp(ct)
        return (dal,), (dw, db)

    d_alow_p, d_wa_p, d_ba = _rowwise("gla_decay_bwd", decay_bwd, [a_low_p, _from_hm(dla_hm)], [w_a_up_p, b_a_up],
                                      [(LANES, F32)], [(w_a_up_p.shape, F32), (b_a_up.shape, F32)])
    d_proj = jnp.concatenate([_from_hm(dq_hm), _from_hm(dk_hm), d_va, d_g_a, d_alow_p[:, 0:LOWRANK], d_xm, d_opre, d_ga, d_gb],
                             axis=1).astype(BF16)
    d_h = _mm(d_proj, weight("w_in", h), "nt", F32, "proj_in_dx", tm=512)

    def pre_mix_bwd(xv, dh, dres, g):
        _, vjp = jax.vjp(_rms, xv, g)
        dx, dg = vjp(dh)
        return (dx + dres,), (dg,)

    grad_x, d_g1 = _rowwise("pre_mix_norm_bwd", pre_mix_bwd, [x, d_h, dx_res], [g1], [(d, F32)], [((1, d), F32)])
    small = dict(g_pre_mix=d_g1, w_a_up=d_wa_p[0:LOWRANK], b_a_up=d_ba, g_gla_norm=d_gn, conv_w=d_cw, conv_b=d_cb,
                 w_q_ml=d_wq, w_k_ml=d_wk, w_v_ml=d_wv, w_if=d_wif, b_if=d_bif, ml_skip=d_skip, g_ml_norm=d_gml,
                 g_post_mix=d_gpm, g_pre_mlp=d_gpl, g_post_mlp=d_gpo)
    sent_small = on_small(small, loss)
    dw_in = _mm(h, d_proj, "tn", F32, "proj_in_dw", tm=512, tk=1024, deps=dep(sent_small))
    return grad_x, on_grads(dict(w_in=dw_in))


BIG = ("w_in", "w_pa", "w_pb", "w_o", "w_up", "w_down")
BIG_COL_SHARDED = ("w_in", "w_pa", "w_pb", "w_up")
SMALL_SHARDED = ("w_a_up", "conv_w", "w_if")
SMALL = ("g_pre_mix", "w_a_up", "b_a_up", "g_gla_norm", "conv_w", "conv_b", "w_q_ml", "w_k_ml", "w_v_ml", "w_if", "b_if",
         "ml_skip", "g_ml_norm", "g_post_mix", "g_pre_mlp", "g_post_mlp")
WEIGHTS = ("g_pre_mix", "w_in", "w_a_up", "b_a_up", "g_gla_norm", "conv_w", "conv_b", "w_q_ml", "w_k_ml", "w_v_ml", "w_if", "b_if",
           "ml_skip", "g_ml_norm", "w_pa", "w_pb", "w_o", "g_post_mix", "g_pre_mlp", "w_up", "w_down", "g_post_mlp")


def kernel(x, g_pre_mix, w_in, w_a_up, b_a_up, g_gla_norm, conv_w, conv_b, w_q_ml, w_k_ml, w_v_ml, w_if, b_if, ml_skip, g_ml_norm, w_pa, w_pb, w_o, g_post_mix, g_pre_mlp, w_up, w_down, g_post_mlp, loss_target, m_g_pre_mix, m_w_in, m_w_a_up, m_b_a_up, m_g_gla_norm, m_conv_w, m_conv_b, m_w_q_ml, m_w_k_ml, m_w_v_ml, m_w_if, m_b_if, m_ml_skip, m_g_ml_norm, m_w_pa, m_w_pb, m_w_o, m_g_post_mix, m_g_pre_mlp, m_w_up, m_w_down, m_g_post_mlp, v_g_pre_mix, v_w_in, v_w_a_up, v_b_a_up, v_g_gla_norm, v_conv_w, v_conv_b, v_w_q_ml, v_w_k_ml, v_w_v_ml, v_w_if, v_b_if, v_ml_skip, v_g_ml_norm, v_w_pa, v_w_pb, v_w_o, v_g_post_mix, v_g_pre_mlp, v_w_up, v_w_down, v_g_post_mlp):
    args = dict(locals())
    w = {n: args[n][0] for n in WEIGHTS}
    m = {n: args["m_" + n][0] for n in WEIGHTS}
    v = {n: args["v_" + n][0] for n in WEIGHTS}

    me_lin = _lin(_me())
    me_idx = jnp.reshape(me_lin, (1,)).astype(jnp.int32)

    def full_weight(n, g):
        return _from_col_blocks(g) if n in BIG_COL_SHARDED else g.reshape(-1, g.shape[-1])

    def grad_parts(n, g):
        return (_col_blocks(g) if n in BIG_COL_SHARDED else g.reshape(N_DEV, -1, g.shape[-1])).astype(BF16)

    sharded_names = tuple(SMALL_SHARDED)
    small_w_state, small_w_token = _copies_start("gather", [_small_view(n, w[n]) for n in sharded_names],
                                                 "allgather_start_small_weights")
    ready = {"w_in": full_weight("w_in", _allgather_big([w["w_in"].astype(BF16)], "allgather_w_in", [small_w_token])[0])}
    pending = {}

    def prefetch(group, after):
        state, token = _copies_start("gather", [w[n].astype(BF16) for n in group], "allgather_start_" + group[0], after)
        for n in group:
            pending[n] = (group, state)
        return token

    def weight(n, after):
        if n not in ready:
            group, state = pending[n]
            shards, lands = _copies_wait(state, after, "allgather_wait_" + group[0])
            for gn, shard, land in zip(group, shards, lands):
                ready[gn] = full_weight(gn, lax.dynamic_update_slice(land, shard[None], (me_lin, 0, 0)))
        return ready[n]

    small_w_own, small_w_lands = _copies_wait(small_w_state, ready["w_in"], "allgather_wait_small_weights")
    ws = {n: (w[n].reshape(1, -1) if w[n].ndim == 1 else w[n]) for n in SMALL if n not in SMALL_SHARDED}
    for n, own, land in zip(sharded_names, small_w_own, small_w_lands):
        ws[n] = _small_unshard(n, lax.dynamic_update_slice(land, own[None], (me_lin, 0, 0)))

    sent = []

    def on_grads(grads):
        names = tuple(grads)
        state, token = _copies_start("exchange", [grad_parts(n, grads[n]) for n in names], "exchange_start_" + names[0])
        sent.append((names, state))
        return token

    small_sent = {}
    small_kinds = ["exchange" if n in SMALL_SHARDED else "gather" for n in SMALL]

    def on_small(small, loss):
        srcs = [_small_shards(n, small[n]) if n in SMALL_SHARDED else _small_view(n, small[n]) for n in SMALL]
        small_sent["state"], token = _copies_start(small_kinds + ["gather"], srcs + [loss], "allgather_start_small_grads")
        return token

    grad_x, last_token = _local_step(x[0], loss_target[0], weight, ws, prefetch, on_grads, on_small)

    out = {}

    def finish(names, state, after):
        parts, lands = _copies_wait(state, after, "exchange_wait_" + names[0])
        for n, part, land in zip(names, parts, lands):
            out[n] = _sum_adamw(land, part, me_idx, w[n], m[n], v[n], "adamw_" + n)

    for names, state in sent[:-1]:
        finish(names, state, [grad_x, last_token])

    small_own, small_lands = _copies_wait(small_sent["state"], [grad_x, last_token], "allgather_wait_small_grads")
    upd = _small_update(me_idx, small_kinds, small_lands[:-1], small_own[:-1],
                        *[[_small_view(n, d[n]) for n in SMALL] for d in (w, m, v)], small_lands[-1], small_own[-1])
    for i, n in enumerate(SMALL):
        out[n] = tuple(_small_unview(n, a, w[n].shape) for a in upd[4 * i:4 * i + 4])
    finish(*sent[-1], [upd[-1]] + [out[n][1] for n in BIG if n in out])

    shaped = lambda a, n: a.reshape(args[n].shape)
    return (upd[-1][0, 0], grad_x[None],
            *[shaped(out[n][0], n) for n in WEIGHTS], *[shaped(out[n][1], n) for n in WEIGHTS],
            *[shaped(out[n][2], n) for n in WEIGHTS], *[shaped(out[n][3], n) for n in WEIGHTS])
```

```python
import functools

import jax
import jax.numpy as jnp
from jax import lax
from jax.experimental import pallas as pl
from jax.experimental.pallas import tpu as pltpu

F32 = jnp.float32
BF16 = jnp.bfloat16
MESH = pl.DeviceIdType.MESH

N_DEV = 8
EPS = 1e-6
CHUNK = 64
HEADS = 4
GLA_DK = 64
HEAD_W = 128
GLA_GATE_NORM = 16.0
LOWRANK = 16
CONV_K = 4
QKV_BLOCK = 4
LANES = 128
HALO = 8
IN_SPLITS = (256, 256, 512, 512, 16, 512, 512, 1024, 1024)

ADAM_LR = 0.001
ADAM_B1 = 0.9
ADAM_B2 = 0.999
ADAM_EPS = 1e-08
ADAM_WD = 0.01
ADAM_STEP = 10

VMEM_LIMIT = 56 * 1024 * 1024


def _cparams(*sem):
    return pltpu.CompilerParams(dimension_semantics=sem, vmem_limit_bytes=VMEM_LIMIT)


def _dims(mode, ndim):
    contract = {"nn": ((ndim - 1,), (ndim - 2,)), "nt": ((ndim - 1,), (ndim - 1,)), "tn": ((ndim - 2,), (ndim - 2,))}[mode]
    return contract, (((0,), (0,)) if ndim == 3 else ((), ()))


def _raw_dot(a, b, mode):
    return lax.dot_general(a.astype(BF16), b.astype(BF16), _dims(mode, a.ndim), preferred_element_type=F32)


@functools.partial(jax.custom_vjp, nondiff_argnums=(2,))
def _bdot(a, b, mode):
    return _raw_dot(a, b, mode)


def _bdot_fwd(a, b, mode):
    return _raw_dot(a, b, mode), (a, b)


def _bdot_bwd(mode, res, ct):
    a, b = res
    if mode == "nn":
        da, db = _raw_dot(ct, b, "nt"), _raw_dot(a, ct, "tn")
    elif mode == "nt":
        da, db = _raw_dot(ct, b, "nn"), _raw_dot(ct, a, "tn")
    else:
        da, db = _raw_dot(b, ct, "nt"), _raw_dot(a, ct, "nn")
    return da.astype(a.dtype), db.astype(b.dtype)


_bdot.defvjp(_bdot_fwd, _bdot_bwd)


def _split3(x):
    hi = x.astype(BF16)
    r1 = x - hi.astype(F32)
    mid = r1.astype(BF16)
    return hi, mid, (r1 - mid.astype(F32)).astype(BF16)


def _split_dot(tri, x):
    if x.ndim == 3:
        tri = jnp.broadcast_to(tri, (x.shape[0], *tri.shape))
    return sum(lax.dot_general(tri, t, _dims("nn", x.ndim), preferred_element_type=F32) for t in _split3(x))


def _tri(n, lower):
    r = lax.broadcasted_iota(jnp.int32, (n, n), 0)
    c = lax.broadcasted_iota(jnp.int32, (n, n), 1)
    return ((c <= r) if lower else (c >= r)).astype(BF16)


@jax.custom_vjp
def _cumsum_rows(x):
    return _split_dot(_tri(x.shape[-2], True), x)


def _cumsum_rows_fwd(x):
    return _cumsum_rows(x), None


def _cumsum_rows_bwd(_, ct):
    return (_split_dot(_tri(ct.shape[-2], False), ct),)


_cumsum_rows.defvjp(_cumsum_rows_fwd, _cumsum_rows_bwd)


def _abs(x):
    return jnp.where(x >= 0, x, -x)


def _sigmoid(x):
    return lax.logistic(x)


def _log_sigmoid(x):
    return jnp.minimum(x, 0.0) - jnp.log(1.0 + jnp.exp(-_abs(x)))


def _rms(x, g):
    return x * lax.rsqrt(jnp.mean(x * x, axis=-1, keepdims=True) + EPS) * g


def _head_slices(w):
    return [slice(h * w, (h + 1) * w) for h in range(HEADS)]


def _heads(ref):
    return jnp.stack([ref[:, hs] for hs in _head_slices(HEAD_W)])


def _put_heads(ref, val):
    for h, hs in enumerate(_head_slices(HEAD_W)):
        ref[:, hs] = val[h]


def _tile(dim, want):
    if dim <= want or dim % LANES:
        return dim
    t = want
    while dim % t:
        t -= LANES
    return t


def _mm(a, b, mode, out_dtype, name, tm=1024, tn=1024, tk=4096, epilogue=None, extra=(), deps=()):
    if mode == "nn":
        (m, k), (k2, n) = a.shape, b.shape
    elif mode == "nt":
        (m, k), (n, k2) = a.shape, b.shape
    else:
        (k, m), (k2, n) = a.shape, b.shape
    assert k == k2, (name, a.shape, b.shape)
    tm, tn, tk = _tile(m, tm), _tile(n, tn), _tile(k, tk)
    nk = k // tk
    out_dtypes = out_dtype if epilogue else (out_dtype,)
    assert nk == 1 or (out_dtype == F32 and not epilogue), name
    n_in = 2 + len(extra)

    def body(*refs):
        p = _raw_dot(refs[0][...], refs[1][...], mode)
        if nk > 1:
            _accumulate(pl.program_id(2), [refs[n_in + len(deps)]], [p])
            return
        outs = epilogue(p, *[r[...] for r in refs[2:n_in]]) if epilogue else (p,)
        for ref, val in zip(refs[n_in + len(deps):], outs):
            ref[...] = val.astype(ref.dtype)

    a_spec = pl.BlockSpec((tk, tm), lambda i, j, kk: (kk, i)) if mode == "tn" else pl.BlockSpec((tm, tk), lambda i, j, kk: (i, kk))
    b_spec = pl.BlockSpec((tn, tk), lambda i, j, kk: (j, kk)) if mode == "nt" else pl.BlockSpec((tk, tn), lambda i, j, kk: (kk, j))
    o_spec = pl.BlockSpec((tm, tn), lambda i, j, kk: (i, j))
    res = pl.pallas_call(
        body, name=name, grid=(m // tm, n // tn, nk),
        in_specs=[a_spec, b_spec] + [o_spec] * len(extra) + [ANY] * len(deps), out_specs=[o_spec] * len(out_dtypes),
        out_shape=[jax.ShapeDtypeStruct((m, n), dt) for dt in out_dtypes],
        compiler_params=_cparams("parallel", "parallel", "arbitrary"),
    )(a, b, *extra, *deps)
    return res if epilogue else res[0]


def _rowwise(name, fn, rows, params, out_rows, out_accs=(), tile=256, deps=()):
    t = rows[0].shape[0]
    r = min(tile, t)
    assert t % r == 0
    n_in, n_or = len(rows) + len(params), len(out_rows)
    n_all = n_in + len(deps)
    params = list(params) + list(deps)

    def body(*refs):
        vals = [ref[...] for ref in refs[:n_in]]
        outs = refs[n_all:]
        ro, ao = fn(*vals)
        for ref, v in zip(outs[:n_or], ro):
            ref[...] = v.astype(ref.dtype)
        if out_accs:
            _accumulate(pl.program_id(0), outs[n_or:], ao)

    def full(shape):
        return pl.BlockSpec(shape, lambda i, nd=len(shape): (0,) * nd)

    return pl.pallas_call(
        body, name=name, grid=(t // r,),
        in_specs=[pl.BlockSpec((r, a.shape[1]), lambda i: (i, 0)) for a in rows] + [full(p.shape) for p in params],
        out_specs=[pl.BlockSpec((r, w), lambda i: (i, 0)) for w, _ in out_rows] + [full(s) for s, _ in out_accs],
        out_shape=[jax.ShapeDtypeStruct((t, w), dt) for w, dt in out_rows] + [jax.ShapeDtypeStruct(s, dt) for s, dt in out_accs],
        compiler_params=_cparams("arbitrary"),
    )(*rows, *params)


def _accumulate(step, refs, vals):
    for ref, v in zip(refs, vals):
        @pl.when(step == 0)
        def _(ref=ref, v=v):
            ref[...] = v.astype(ref.dtype)

        @pl.when(step > 0)
        def _(ref=ref, v=v):
            ref[...] += v.astype(ref.dtype)


def _gla_chunk(q, k, v, la, st):
    c = q.shape[-2]
    row = lax.broadcasted_iota(jnp.int32, (c, c), 0)
    col = lax.broadcasted_iota(jnp.int32, (c, c), 1)
    cum = _cumsum_rows(la)
    cl = jnp.sum(la, axis=-2, keepdims=True)
    ep = jnp.exp(cum)
    en = jnp.exp(-cum)
    qs = q * (GLA_DK ** -0.5)
    qp = qs * ep
    a_f = _bdot(qp, k * en, "nt")
    a_b = _bdot(qs * en, k * ep, "nt")
    sc = jnp.where(row >= col, a_f, a_b)
    o = _bdot(sc, v, "nn") + _bdot(qp, st, "nt")
    kd = k * jnp.exp(cl - cum)
    st_new = st * jnp.exp(cl) + _bdot(v, kd, "tn")
    return o, st_new


def _gla_specs(nc, rev):
    def ch(n):
        return (nc - 1 - n) if rev else n
    hm = pl.BlockSpec((HEADS, CHUNK, GLA_DK), lambda n: (0, ch(n), 0))
    tm = pl.BlockSpec((CHUNK, HEADS * HEAD_W), lambda n: (ch(n), 0))
    st = pl.BlockSpec((HEADS, None, HEAD_W, GLA_DK), lambda n: (0, ch(n), 0, 0))
    return hm, tm, st


def _gla_fwd(q, k, v, la):
    t = v.shape[0]
    nc = t // CHUNK
    hm, tm, st = _gla_specs(nc, False)

    def body(q_ref, k_ref, v_ref, la_ref, o_ref, sp_ref, st_ref):
        @pl.when(pl.program_id(0) == 0)
        def _():
            st_ref[...] = jnp.zeros_like(st_ref)

        s = st_ref[...]
        sp_ref[...] = s
        o, s_new = _gla_chunk(q_ref[...], k_ref[...], _heads(v_ref), la_ref[...], s)
        _put_heads(o_ref, o)
        st_ref[...] = s_new

    return pl.pallas_call(
        body, name="gla_fwd", grid=(nc,),
        in_specs=[hm, hm, tm, hm], out_specs=[tm, st],
        out_shape=[jax.ShapeDtypeStruct((t, HEADS * HEAD_W), F32), jax.ShapeDtypeStruct((HEADS, nc, HEAD_W, GLA_DK), F32)],
        scratch_shapes=[pltpu.VMEM((HEADS, HEAD_W, GLA_DK), F32)],
        compiler_params=_cparams("arbitrary"),
    )(q, k, v, la)


def _gla_bwd(q, k, v, la, sp, do):
    t = v.shape[0]
    nc = t // CHUNK
    hm, tm, st = _gla_specs(nc, True)

    def body(q_ref, k_ref, v_ref, la_ref, sp_ref, do_ref, dq_ref, dk_ref, dv_ref, dla_ref, ds_ref):
        @pl.when(pl.program_id(0) == 0)
        def _():
            ds_ref[...] = jnp.zeros_like(ds_ref)

        _, vjp = jax.vjp(_gla_chunk, q_ref[...], k_ref[...], _heads(v_ref), la_ref[...], sp_ref[...])
        dq, dk, dv, dla, ds = vjp((_heads(do_ref), ds_ref[...]))
        dq_ref[...] = dq
        dk_ref[...] = dk
        _put_heads(dv_ref, dv)
        dla_ref[...] = dla
        ds_ref[...] = ds

    hm_shape = jax.ShapeDtypeStruct((HEADS, t, GLA_DK), F32)
    return pl.pallas_call(
        body, name="gla_bwd", grid=(nc,),
        in_specs=[hm, hm, tm, hm, st, tm], out_specs=[hm, hm, tm, hm],
        out_shape=[hm_shape, hm_shape, jax.ShapeDtypeStruct((t, HEADS * HEAD_W), F32), hm_shape],
        scratch_shapes=[pltpu.VMEM((HEADS, HEAD_W, GLA_DK), F32)],
        compiler_params=_cparams("arbitrary"),
    )(q, k, v, la, sp, do)


def _ml_chunk(q, k, v, li_r, lf_r, cm, nv, m):
    c = q.shape[-2]
    row = lax.broadcasted_iota(jnp.int32, (c, c), 0)
    col = lax.broadcasted_iota(jnp.int32, (c, c), 1)
    eye = (row == col).astype(F32)
    li_c = jnp.sum(eye * li_r, axis=-1, keepdims=True)
    lf_c = jnp.sum(eye * lf_r, axis=-1, keepdims=True)
    fc_c = jnp.sum((col <= row).astype(F32) * lf_r, axis=-1, keepdims=True)
    fc_r = jnp.sum((row <= col).astype(F32) * lf_c, axis=-2, keepdims=True)
    f_last = jnp.sum(lf_r, axis=-1, keepdims=True)
    kc = k * (HEAD_W ** -0.5)
    a_c = f_last - fc_c + li_c
    m_loc = jnp.max(a_c, axis=-2, keepdims=True)
    kw = kc * jnp.exp(a_c - m_loc)
    c_chunk = _bdot(kw, v, "tn")
    n_chunk = jnp.sum(kw, axis=-2, keepdims=True)
    m_new = jnp.maximum(f_last + m, m_loc)
    sp = jnp.exp(f_last + m - m_new)
    sl = jnp.exp(m_loc - m_new)
    cm_new = sp * cm + sl * c_chunk
    nv_new = sp * nv + sl * n_chunk
    log_d = li_r - _abs(fc_c - fc_r)
    g_inter = fc_c + m
    m_t = jnp.maximum(g_inter, jnp.max(log_d, axis=-1, keepdims=True))
    s = _bdot(q, kc, "nt") * jnp.exp(log_d - m_t)
    sc = jnp.exp(g_inter - m_t)
    num = _bdot(s, v, "nn") + sc * _bdot(q, cm, "nn")
    den = jnp.sum(s, axis=-1, keepdims=True) + sc * jnp.sum(q * nv, axis=-1, keepdims=True)
    den = jnp.maximum(_abs(den), jnp.exp(-m_t))
    return num / den, cm_new, nv_new, m_new


def _ml_specs(nc, rev):
    def ch(n):
        return (nc - 1 - n) if rev else n
    tm = pl.BlockSpec((CHUNK, HEADS * HEAD_W), lambda n: (ch(n), 0))
    gate = pl.BlockSpec((HEADS, None, 1, CHUNK), lambda n: (0, ch(n), 0, 0))
    cm = pl.BlockSpec((HEADS, None, HEAD_W, HEAD_W), lambda n: (0, ch(n), 0, 0))
    vec = pl.BlockSpec((HEADS, None, 1, HEAD_W), lambda n: (0, ch(n), 0, 0))
    return tm, gate, cm, vec


_ML_STATE = [pltpu.VMEM((HEADS, HEAD_W, HEAD_W), F32), pltpu.VMEM((HEADS, 1, HEAD_W), F32), pltpu.VMEM((HEADS, 1, HEAD_W), F32)]


def _ml_fwd(q, k, v, li, lf):
    t = q.shape[0]
    nc = t // CHUNK
    tm, gate, cm, vec = _ml_specs(nc, False)

    def body(q_ref, k_ref, v_ref, li_ref, lf_ref, hc_ref, cp_ref, np_ref, mp_ref, c_ref, n_ref, m_ref):
        @pl.when(pl.program_id(0) == 0)
        def _():
            c_ref[...] = jnp.zeros_like(c_ref)
            n_ref[...] = jnp.zeros_like(n_ref)
            m_ref[...] = jnp.zeros_like(m_ref)

        c0, n0, m0 = c_ref[...], n_ref[...], m_ref[...]
        cp_ref[...] = c0
        np_ref[...] = n0
        mp_ref[...] = m0
        hc, c1, n1, m1 = _ml_chunk(_heads(q_ref), _heads(k_ref), _heads(v_ref), li_ref[...], lf_ref[...],
                                   c0, n0, m0[:, :, 0:1])
        _put_heads(hc_ref, hc)
        c_ref[...] = c1
        n_ref[...] = n1
        m_ref[...] = jnp.broadcast_to(m1, m_ref.shape)

    return pl.pallas_call(
        body, name="mlstm_fwd", grid=(nc,),
        in_specs=[tm, tm, tm, gate, gate], out_specs=[tm, cm, vec, vec],
        out_shape=[jax.ShapeDtypeStruct((t, HEADS * HEAD_W), F32), jax.ShapeDtypeStruct((HEADS, nc, HEAD_W, HEAD_W), F32),
                   jax.ShapeDtypeStruct((HEADS, nc, 1, HEAD_W), F32), jax.ShapeDtypeStruct((HEADS, nc, 1, HEAD_W), F32)],
        scratch_shapes=_ML_STATE,
        compiler_params=_cparams("arbitrary"),
    )(q, k, v, li, lf)


def _ml_bwd(q, k, v, li, lf, cp, npv, mp, dhc):
    t = q.shape[0]
    nc = t // CHUNK
    tm, gate, cm, vec = _ml_specs(nc, True)

    def body(q_ref, k_ref, v_ref, li_ref, lf_ref, cp_ref, np_ref, mp_ref, dhc_ref,
             dq_ref, dk_ref, dv_ref, dli_ref, dlf_ref, dc_ref, dn_ref, dm_ref):
        @pl.when(pl.program_id(0) == 0)
        def _():
            dc_ref[...] = jnp.zeros_like(dc_ref)
            dn_ref[...] = jnp.zeros_like(dn_ref)
            dm_ref[...] = jnp.zeros_like(dm_ref)

        _, vjp = jax.vjp(_ml_chunk, _heads(q_ref), _heads(k_ref), _heads(v_ref), li_ref[...], lf_ref[...],
                         cp_ref[...], np_ref[...], mp_ref[...][:, :, 0:1])
        dq, dk, dv, dli, dlf, dc, dn, dm = vjp((_heads(dhc_ref), dc_ref[...], dn_ref[...], dm_ref[...][:, :, 0:1]))
        _put_heads(dq_ref, dq)
        _put_heads(dk_ref, dk)
        _put_heads(dv_ref, dv)
        dli_ref[...] = dli
        dlf_ref[...] = dlf
        dc_ref[...] = dc
        dn_ref[...] = dn
        dm_ref[...] = jnp.broadcast_to(dm, dm_ref.shape)

    tm_shape = jax.ShapeDtypeStruct((t, HEADS * HEAD_W), F32)
    gate_shape = jax.ShapeDtypeStruct((HEADS, nc, 1, CHUNK), F32)
    return pl.pallas_call(
        body, name="mlstm_bwd", grid=(nc,),
        in_specs=[tm, tm, tm, gate, gate, cm, vec, vec, tm], out_specs=[tm, tm, tm, gate, gate],
        out_shape=[tm_shape, tm_shape, tm_shape, gate_shape, gate_shape],
        scratch_shapes=_ML_STATE,
        compiler_params=_cparams("arbitrary"),
    )(q, k, v, li, lf, cp, npv, mp, dhc)


def _ml_pre(s0, s1, s2, s3, cw0, cw1, cw2, cw3, cb, wq, wk, wv, wiq, wik, wiv, bif):
    pre = cb + cw0 * s0 + cw1 * s1 + cw2 * s2 + cw3 * s3
    xc = pre * _sigmoid(pre)
    q = _bdot(xc, wq, "nn")
    k = _bdot(xc, wk, "nn")
    v = _bdot(s3, wv, "nn")
    gates = _bdot(q, wiq, "nn") + _bdot(k, wik, "nn") + _bdot(v, wiv, "nn") + bif
    lane = lax.broadcasted_iota(jnp.int32, gates.shape, 1)
    gl = jnp.where(lane < HEADS, gates, _log_sigmoid(gates))
    return xc, q, k, v, gl


def _delayed(xs_ref, x_ref, halo_ref, r):
    xs_ref[0:HALO, :] = halo_ref[...]
    xs_ref[HALO:HALO + r, :] = x_ref[...]
    return [xs_ref[pl.ds(HALO - (CONV_K - 1) + j, r), :] for j in range(CONV_K)]


def _full_spec(shape):
    return pl.BlockSpec(shape, lambda i, nd=len(shape): (0,) * nd)


def _ml_pre_fwd(x_m, x_pad, params, tile=256):
    t, w = x_m.shape
    r = min(tile, t)

    def body(*refs):
        x_ref, halo_ref = refs[:2]
        p = [ref[...] for ref in refs[2:2 + len(params)]]
        outs = refs[2 + len(params):-1]
        res = _ml_pre(*_delayed(refs[-1], x_ref, halo_ref, r), *p)
        for ref, val in zip(outs, res):
            ref[...] = val

    row = pl.BlockSpec((r, w), lambda i: (i, 0))
    return pl.pallas_call(
        body, name="ml_pre_fwd", grid=(t // r,),
        in_specs=[row, pl.BlockSpec((HALO, w), lambda i: (i * (r // HALO), 0))] + [_full_spec(p.shape) for p in params],
        out_specs=[row] * 4 + [pl.BlockSpec((r, LANES), lambda i: (i, 0))],
        out_shape=[jax.ShapeDtypeStruct((t, w), F32)] * 4 + [jax.ShapeDtypeStruct((t, LANES), F32)],
        scratch_shapes=[pltpu.VMEM((r + HALO, w), F32)],
        compiler_params=_cparams("arbitrary"),
    )(x_m, x_pad, *params)


def _ml_pre_bwd(x_m, x_pad, params, cts, tile=256):
    t, w = x_m.shape
    r = min(tile, t)
    nt = t // r
    n_p = len(params)

    def body(*refs):
        x_ref, halo_ref = refs[:2]
        p = [ref[...] for ref in refs[2:2 + n_p]]
        ct = [ref[...] for ref in refs[2 + n_p:7 + n_p]]
        dx_ref = refs[7 + n_p]
        dp_refs = refs[8 + n_p:8 + 2 * n_p]
        xs_ref, ds_ref, carry_ref = refs[8 + 2 * n_p:]
        step = pl.program_id(0)

        @pl.when(step == 0)
        def _():
            ds_ref[...] = jnp.zeros_like(ds_ref)
            carry_ref[...] = jnp.zeros_like(carry_ref)

        _, vjp = jax.vjp(_ml_pre, *_delayed(xs_ref, x_ref, halo_ref, r), *p)
        grads = vjp(tuple(ct))
        for j in range(CONV_K):
            ds_ref[j, HALO:HALO + r, :] = grads[j]
        lead = HALO + CONV_K - 1
        d_tile = sum(ds_ref[j, pl.ds(lead - j, r), :] for j in range(CONV_K))
        d_halo = sum(ds_ref[j, pl.ds(CONV_K - 1 - j, HALO), :] for j in range(CONV_K))
        dx_ref[...] = d_tile
        dx_ref[r - HALO:r, :] += carry_ref[...]
        carry_ref[...] = d_halo
        _accumulate(step, dp_refs, grads[CONV_K:])

    row = pl.BlockSpec((r, w), lambda i: (nt - 1 - i, 0))
    return pl.pallas_call(
        body, name="ml_pre_bwd", grid=(nt,),
        in_specs=[row, pl.BlockSpec((HALO, w), lambda i: ((nt - 1 - i) * (r // HALO), 0))] + [_full_spec(p.shape) for p in params]
        + [row] * 4 + [pl.BlockSpec((r, LANES), lambda i: (nt - 1 - i, 0))],
        out_specs=[row] + [_full_spec(p.shape) for p in params],
        out_shape=[jax.ShapeDtypeStruct((t, w), F32)] + [jax.ShapeDtypeStruct(p.shape, F32) for p in params],
        scratch_shapes=[pltpu.VMEM((r + HALO, w), F32), pltpu.VMEM((CONV_K, r + 2 * HALO, w), F32), pltpu.VMEM((HALO, w), F32)],
        compiler_params=_cparams("arbitrary"),
    )(x_m, x_pad, *params, *cts)


def _per_head(fn, row_vals, head_params, shared_params=()):
    return [fn(*[a[:, hs] for a in row_vals], *[p[:, hs] for p in head_params], *shared_params) for hs in _head_slices(HEAD_W)]


def _gla_out(o, g, gn):
    return _rms(o, gn) * (g * _sigmoid(g))


def _ml_out(hc, op, xc, g, sk):
    hcell = hc * _sigmoid(op)
    mu = jnp.mean(hcell, axis=-1, keepdims=True)
    d = hcell - mu
    var = jnp.mean(d * d, axis=-1, keepdims=True)
    return d * lax.rsqrt(var + EPS) * g + sk * xc


def _log_decay(al, w, b):
    return _log_sigmoid(_bdot(al, w, "nn") + b) * (1.0 / GLA_GATE_NORM)


def _merge(ga, gb, ya, yb):
    return _sigmoid(ga) * ya + _sigmoid(gb) * yb


def _post_mix(x, z, gpm, gpl):
    x1 = x + _rms(z, gpm)
    return x1, _rms(x1, gpl)


def _loss_rows(x1, dn, tgt, g):
    e = x1 + _rms(dn, g) - tgt
    return 0.5 * jnp.sum(jnp.mean(e * e, axis=-1, keepdims=True), axis=0, keepdims=True)


def _lin(p):
    return 4 * p[0] + 2 * p[1] + p[2]


def _me():
    return lax.axis_index("x"), lax.axis_index("y"), lax.axis_index("c")


def _flip(p, k):
    return tuple((1 - v) if (k >> (2 - i)) & 1 else v for i, v in enumerate(p))


ANY = pl.BlockSpec(memory_space=pl.ANY)


def _allgather_big(shards, name, deps=()):
    n = len(shards)
    n_in = n + len(deps)

    def body(*refs):
        ins, outs = refs[:n], refs[n_in:n_in + n]
        send_sems, recv_sems, local_sems = refs[n_in + n:]
        me = _me()
        x, y, c = me
        sib = (x, y, 1 - c)
        chips = [(1 - x, y), (x, 1 - y), (1 - x, 1 - y)]

        def cp(a, k, block, to, src=None):
            dst = outs[a].at[_lin(block)]
            return pltpu.make_async_remote_copy(src_ref=dst if src is None else src, dst_ref=dst,
                                                send_sem=send_sems.at[a * 7 + k], recv_sem=recv_sems.at[a * 7 + k],
                                                device_id=to, device_id_type=MESH)

        mine = [pltpu.make_async_copy(ins[a], outs[a].at[_lin(me)], local_sems.at[a]) for a in range(n)]
        for m in mine:
            m.start()
        first = []
        for a in range(n):
            first.append(cp(a, 0, me, sib, src=ins[a]))
            first += [cp(a, 1 + j, me, (*chip, c), src=ins[a]) for j, chip in enumerate(chips)]
        for f in first:
            f.start()
        passed = []
        for j, chip in enumerate(chips):
            for a in range(n):
                cp(a, 1 + j, (*chip, c), me).wait_recv()
                fwd = cp(a, 4 + j, (*chip, c), sib)
                fwd.start()
                passed.append(fwd)
        for a in range(n):
            cp(a, 0, sib, me).wait_recv()
            for j, chip in enumerate(chips):
                cp(a, 4 + j, (*chip, 1 - c), me).wait_recv()
        for f in first + passed:
            f.wait_send()
        for m in mine:
            m.wait()

    return pl.pallas_call(
        body, name=name,
        in_specs=[ANY] * n_in, out_specs=[ANY] * n,
        out_shape=[jax.ShapeDtypeStruct((N_DEV, *s.shape), s.dtype) for s in shards],
        scratch_shapes=[pltpu.SemaphoreType.DMA((7 * n,)), pltpu.SemaphoreType.DMA((7 * n,)), pltpu.SemaphoreType.DMA((n,))],
    )(*shards, *deps)


HBM = pl.BlockSpec(memory_space=pltpu.HBM)
SEM = pl.BlockSpec(memory_space=pltpu.SEMAPHORE)
DATAFLOW = pltpu.SideEffectType.DATAFLOW_SIDE_EFFECTING


def _peer_copies(kinds, srcs, lands, send_sems, recv_sems):
    me = _me()
    copies = []
    for a, (kind, src, land) in enumerate(zip(kinds, srcs, lands)):
        for k in range(1, N_DEV):
            peer = _flip(me, k)
            copies.append(pltpu.make_async_remote_copy(
                src_ref=src if kind == "gather" else src.at[_lin(peer)], dst_ref=land.at[_lin(me)],
                send_sem=send_sems.at[a * 7 + k - 1], recv_sem=recv_sems.at[a * 7 + k - 1],
                device_id=peer, device_id_type=MESH))
    return copies


def _copies_start(kind, srcs, name, after=None):
    n = len(srcs)
    extra = [] if after is None else [after]
    kind = [kind] * n if isinstance(kind, str) else list(kind)
    land_shapes = [((N_DEV, *s.shape) if k == "gather" else s.shape) for k, s in zip(kind, srcs)]

    def body(*refs):
        sems = refs[2 * n + len(extra):]
        for cp in _peer_copies(kind, refs[:n], refs[n:2 * n], sems[0], sems[1]):
            cp.start()
        refs[-1][...] = jnp.zeros_like(refs[-1])

    def hbm(a):
        return pltpu.with_memory_space_constraint(a, pltpu.HBM)

    out = pl.pallas_call(
        body, name=name,
        out_shape=(pltpu.SemaphoreType.DMA((7 * n,)), pltpu.SemaphoreType.DMA((7 * n,)),
                   *[pltpu.HBM(s.shape, s.dtype) for s in srcs],
                   *[pltpu.HBM(ls, s.dtype) for ls, s in zip(land_shapes, srcs)],
                   jax.ShapeDtypeStruct((8, LANES), F32)),
        in_specs=[HBM] * (2 * n) + [ANY] * len(extra),
        out_specs=(SEM, SEM, *[HBM] * (2 * n), pl.BlockSpec(memory_space=pltpu.VMEM)),
        input_output_aliases={i: 2 + i for i in range(2 * n)},
        compiler_params=pltpu.CompilerParams(has_side_effects=DATAFLOW),
    )(*[hbm(s) for s in srcs], *[hbm(lax.empty(ls, s.dtype)) for ls, s in zip(land_shapes, srcs)], *extra)
    return (kind, n, out[:-1]), out[-1]


def _copies_wait(state, after, name):
    kind, n, (send_sems, recv_sems, *thru) = state
    after = list(after) if isinstance(after, (list, tuple)) else [after]

    def body(*refs):
        for cp in _peer_copies(kind, refs[:n], refs[n:2 * n], refs[2 * n], refs[2 * n + 1]):
            cp.wait_send()
            cp.wait_recv()

    out = pl.pallas_call(
        body, name=name,
        out_shape=tuple(pltpu.HBM(t.shape, t.dtype) for t in thru),
        in_specs=[HBM] * (2 * n) + [SEM, SEM] + [ANY] * len(after), out_specs=tuple([HBM] * (2 * n)),
        input_output_aliases={i: i for i in range(2 * n)},
        compiler_params=pltpu.CompilerParams(has_side_effects=DATAFLOW),
    )(*thru, send_sems, recv_sems, *after)
    return out[:n], out[n:]


def _adamw(w, g, m, v):
    m2 = ADAM_B1 * m + (1.0 - ADAM_B1) * g
    v2 = ADAM_B2 * v + (1.0 - ADAM_B2) * (g * g)
    m_hat = m2 / (1.0 - ADAM_B1 ** ADAM_STEP)
    v_hat = v2 / (1.0 - ADAM_B2 ** ADAM_STEP)
    delta = -ADAM_LR * (m_hat / (jnp.sqrt(v_hat) + ADAM_EPS) + ADAM_WD * w)
    return delta, m2, v2


def _sum_adamw(land, part, me_idx, w, m, v, name, tile=256):
    r, c = w.shape
    tr = min(tile, r)

    def body(me_ref, own_ref, *refs):
        slots = refs[:N_DEV]
        w_ref, m_ref, v_ref, g_ref, d_ref, m2_ref, v2_ref = refs[N_DEV:]
        own = own_ref[...].astype(F32)
        g = None
        for s in range(N_DEV):
            term = jnp.where(me_ref[0] == s, own, slots[s][...].astype(F32))
            g = term if g is None else g + term
        d, m2, v2 = _adamw(w_ref[...], g, m_ref[...], v_ref[...])
        g_ref[...] = g
        d_ref[...] = d
        m2_ref[...] = m2
        v2_ref[...] = v2

    def slot_spec(s):
        return pl.BlockSpec((None, tr, c), lambda i, me: (jnp.where(me[0] == s, (s + 1) % N_DEV, s), i, 0))

    row = pl.BlockSpec((tr, c), lambda i, me: (i, 0))
    return pl.pallas_call(
        body, name=name,
        grid_spec=pltpu.PrefetchScalarGridSpec(
            num_scalar_prefetch=1, grid=(r // tr,),
            in_specs=[pl.BlockSpec((None, tr, c), lambda i, me: (me[0], i, 0))] + [slot_spec(s) for s in range(N_DEV)] + [row] * 3,
            out_specs=[row] * 4),
        out_shape=[jax.ShapeDtypeStruct((r, c), F32)] * 4,
        compiler_params=_cparams("parallel"),
    )(me_idx, part, *[land] * N_DEV, w, m, v)


def _small_update(name, me_idx, kinds, lands, owns, ws, ms, vs, sums=()):
    n = len(ws)
    lands, owns = list(lands) + [s[0] for s in sums], list(owns) + [s[1] for s in sums]
    kinds = list(kinds) + ["gather"] * len(sums)
    nl = len(lands)

    def summed(me, land_ref, own):
        g = None
        for s in range(N_DEV):
            term = jnp.where(me == s, own, land_ref[s])
            g = term if g is None else g + term
        return g

    def body(me_ref, *refs):
        land_refs, own_refs = refs[:nl], refs[nl:2 * nl]
        w_refs, m_refs, v_refs = (refs[2 * nl + i * n:2 * nl + (i + 1) * n] for i in range(3))
        outs = refs[2 * nl + 3 * n:]
        me = me_ref[0]
        for i in range(n):
            g = summed(me, land_refs[i], own_refs[i][...])
            d, m2, v2 = _adamw(w_refs[i][...], g, m_refs[i][...], v_refs[i][...])
            for ref, val in zip(outs[4 * i:4 * i + 4], (g, d, m2, v2)):
                ref[...] = val
        for i in range(n, nl):
            outs[4 * n + i - n][...] = summed(me, land_refs[i], own_refs[i][...])

    def whole(shape):
        return pl.BlockSpec(shape, lambda i, me, nd=len(shape): (0,) * nd)

    def own_spec(kind, own):
        if kind == "gather":
            return whole(own.shape)
        return pl.BlockSpec((None, *own.shape[1:]), lambda i, me: (me[0], 0, 0))

    shapes = [w.shape for w in ws]
    out_shapes = [s for s in shapes for _ in range(4)] + [s[1].shape for s in sums]
    return pl.pallas_call(
        body, name=name,
        grid_spec=pltpu.PrefetchScalarGridSpec(
            num_scalar_prefetch=1, grid=(1,),
            in_specs=[whole(a.shape) for a in lands] + [own_spec(k, o) for k, o in zip(kinds, owns)]
            + [whole(s) for s in shapes] * 3,
            out_specs=[whole(s) for s in out_shapes]),
        out_shape=[jax.ShapeDtypeStruct(s, F32) for s in out_shapes],
        compiler_params=_cparams("arbitrary"),
    )(me_idx, *lands, *owns, *ws, *ms, *vs)


def _small_view(n, a):
    if a.ndim == 1:
        return a.reshape(1, -1)
    if a.ndim == 3:
        return a.transpose(1, 2, 0).reshape(QKV_BLOCK * QKV_BLOCK, -1)
    return a.T if n == "w_if" else a


def _small_unview(n, a, shape):
    if len(shape) == 1:
        return a.reshape(shape)
    if len(shape) == 3:
        return a.reshape(QKV_BLOCK, QKV_BLOCK, -1).transpose(2, 0, 1)
    return a.T if n == "w_if" else a


def _small_shards(n, g):
    if n == "w_if":
        return g.reshape(N_DEV, -1, g.shape[1]).transpose(0, 2, 1)
    return g.reshape(g.shape[0], N_DEV, -1).transpose(1, 0, 2)


def _small_unshard(n, s):
    if n == "w_if":
        return s.transpose(0, 2, 1).reshape(-1, s.shape[1])
    return s.transpose(1, 0, 2).reshape(s.shape[1], -1)


def _to_hm(a, d):
    t = a.shape[0]
    return a.reshape(t, HEADS, d).transpose(1, 0, 2)


def _from_hm(a):
    h, t, d = a.shape
    return a.transpose(1, 0, 2).reshape(t, h * d)


def _gate_rows(g):
    t = g.shape[0]
    return g.T.reshape(HEADS, t // CHUNK, 1, CHUNK)


def _gate_cols(g):
    h, nc, _, c = g.shape
    return g.reshape(h, nc * c).T


def _blockdiag_dense(w):
    n = w.shape[0] * QKV_BLOCK
    tiled = jnp.tile(w.reshape(n, QKV_BLOCK), (1, n // QKV_BLOCK))
    r = lax.broadcasted_iota(jnp.int32, (n, n), 0)
    c = lax.broadcasted_iota(jnp.int32, (n, n), 1)
    return jnp.where(r // QKV_BLOCK == c // QKV_BLOCK, tiled, 0.0)


def _blockdiag_blocks(dense):
    n = dense[0].shape[0]
    k = len(dense)

    def body(*refs):
        r = lax.broadcasted_iota(jnp.int32, (n, n), 0)
        c = lax.broadcasted_iota(jnp.int32, (n, n), 1)
        fr = lax.broadcasted_iota(jnp.int32, (n, LANES), 0)
        fc = lax.broadcasted_iota(jnp.int32, (n, LANES), 1)
        fold = ((fr & (QKV_BLOCK - 1)) == fc).astype(BF16)
        for i in range(k):
            kept = jnp.where((r >> 2) == (c >> 2), refs[i][...], 0.0)
            refs[k + i][...] = sum(lax.dot_general(t, fold, _dims("nn", 2), preferred_element_type=F32) for t in _split3(kept))

    out = pl.pallas_call(body, name="blockdiag_blocks", out_shape=[jax.ShapeDtypeStruct((n, LANES), F32)] * k)(*dense)
    return [o[:, 0:QKV_BLOCK].reshape(n // QKV_BLOCK, QKV_BLOCK, QKV_BLOCK) for o in out]


def _col_blocks(w):
    k, n = w.shape
    return w.reshape(k, N_DEV, n // N_DEV).transpose(1, 0, 2)


def _from_col_blocks(g):
    d, k, n = g.shape
    return g.transpose(1, 0, 2).reshape(k, d * n)


def _local_step(x, tgt, weight, ws, prefetch, on_grads, on_small):
    t, d = x.shape
    g1 = ws["g_pre_mix"]

    def dep(token):
        return () if token is None else (token,)

    (h,) = _rowwise("pre_mix_norm", lambda xv, g: ((_rms(xv, g),), ()), [x], [g1], [(d, BF16)])
    fetch_mix = prefetch(("w_pa", "w_pb", "w_o"), h)
    proj = _mm(h, weight("w_in", h), "nn", F32, "proj_in", tm=512)
    offs = [0]
    for s in IN_SPLITS:
        offs.append(offs[-1] + s)
    q_a, k_a, v_a, g_a, a_low, x_m, o_pre, gate_a, gate_b = [proj[:, offs[i]:offs[i + 1]] for i in range(9)]

    a_low_p = jnp.pad(a_low, ((0, 0), (0, LANES - LOWRANK)))
    w_a_up_p = jnp.pad(ws["w_a_up"], ((0, LANES - LOWRANK), (0, 0)))
    b_a_up = ws["b_a_up"]
    (la,) = _rowwise("gla_decay", lambda al, w, b: ((_log_decay(al, w, b),), ()), [a_low_p], [w_a_up_p, b_a_up],
                     [(HEADS * GLA_DK, F32)], deps=dep(fetch_mix))
    fetch_up = prefetch(("w_up",), la)
    q_hm, k_hm, la_hm = _to_hm(q_a, GLA_DK), _to_hm(k_a, GLA_DK), _to_hm(la, GLA_DK)
    o_gla, s_prev = _gla_fwd(q_hm, k_hm, v_a, la_hm)
    gn = ws["g_gla_norm"]
    (ya_in,) = _rowwise("gla_out", lambda o, g, n_: ((jnp.concatenate(_per_head(_gla_out, [o, g], [], [n_]), axis=1),), ()),
                        [o_gla, g_a], [gn], [(HEADS * HEAD_W, BF16)], deps=dep(fetch_up))
    y_a = _mm(ya_in, weight("w_pa", ya_in), "nn", F32, "proj_a")

    cw = ws["conv_w"]
    w_if_p = jnp.pad(ws["w_if"], ((0, 0), (0, LANES - 2 * HEADS)))
    ml_w = HEADS * HEAD_W
    pre_params = [cw[0:1], cw[1:2], cw[2:3], cw[3:4], ws["conv_b"],
                  _blockdiag_dense(ws["w_q_ml"]), _blockdiag_dense(ws["w_k_ml"]), _blockdiag_dense(ws["w_v_ml"]),
                  w_if_p[0:ml_w], w_if_p[ml_w:2 * ml_w], w_if_p[2 * ml_w:3 * ml_w],
                  jnp.pad(ws["b_if"], ((0, 0), (0, LANES - 2 * HEADS)))]
    x_pad = jnp.pad(x_m, ((HALO, 0), (0, 0)))
    xc, q_m, k_m, v_m, gl = _ml_pre_fwd(x_m, x_pad, pre_params)
    li, lf = _gate_rows(gl[:, 0:HEADS]), _gate_rows(gl[:, HEADS:2 * HEADS])
    hc, c_prev, n_prev, m_prev = _ml_fwd(q_m, k_m, v_m, li, lf)
    fetch_down = prefetch(("w_down",), hc)
    g_ml, skip = ws["g_ml_norm"], ws["ml_skip"]
    (h_b,) = _rowwise("mlstm_out", lambda a, b, c_, g, s: ((jnp.concatenate(_per_head(_ml_out, [a, b, c_], [g, s]), axis=1),), ()),
                      [hc, o_pre, xc], [g_ml, skip], [(ml_w, BF16)], deps=dep(fetch_down))
    y_b = _mm(h_b, weight("w_pb", h_b), "nn", F32, "proj_b")

    (merged,) = _rowwise("merge", lambda ga, gb, ya, yb: ((_merge(ga, gb, ya, yb),), ()), [gate_a, gate_b, y_a, y_b], [],
                         [(d, BF16)])
    z = _mm(merged, weight("w_o", merged), "nn", F32, "proj_o")
    gpm, gpl, gpo = ws["g_post_mix"], ws["g_pre_mlp"], ws["g_post_mlp"]
    x1, h2 = _rowwise("post_mix", lambda xv, zv, a, b: (_post_mix(xv, zv, a, b), ()), [x, z], [gpm, gpl], [(d, F32), (d, BF16)])
    up, u = _mm(h2, weight("w_up", h2), "nn", (F32, BF16), "mlp_up", epilogue=lambda p: (p, jnp.square(jnp.maximum(p, 0.0))))
    dn = _mm(u, weight("w_down", u), "nn", F32, "mlp_down", tm=512)

    def loss_and_grads(x1v, dnv, tgtv, g):
        loss, vjp = jax.vjp(lambda a, b, c_: _loss_rows(a, b, tgtv, c_), x1v, dnv, g)
        dx1, ddn, dg = vjp(jnp.ones((1, 1), F32))
        return (dx1, ddn), (jnp.broadcast_to(loss, (1, LANES)), dg)

    dx1_y, d_dn, loss, d_gpo = _rowwise("loss", loss_and_grads, [x1, dn, tgt], [gpo], [(d, F32), (d, BF16)],
                                        [((1, LANES), F32), ((1, d), F32)])

    (d_up,) = _mm(d_dn, weight("w_down", u), "nt", (BF16,), "mlp_down_dx", extra=[up],
                  epilogue=lambda p, a: (p * (2.0 * jnp.maximum(a, 0.0)),))
    dw_down = _mm(u, d_dn, "tn", BF16, "mlp_down_dw", tm=512)
    d_h2 = _mm(d_up, weight("w_up", h2), "nt", F32, "mlp_up_dx", tm=512)
    dw_up = _mm(h2, d_up, "tn", BF16, "mlp_up_dw")
    sent_mlp = on_grads(dict(w_down=dw_down, w_up=dw_up))

    def post_mix_bwd(xv, zv, dx1, dh2, a, b):
        _, vjp = jax.vjp(_post_mix, xv, zv, a, b)
        dx, dz, da, db = vjp((dx1, dh2))
        return (dx, dz), (da, db)

    dx_res, d_z, d_gpm, d_gpl = _rowwise("post_mix_bwd", post_mix_bwd, [x, z, dx1_y, d_h2], [gpm, gpl],
                                         [(d, F32), (d, BF16)], [((1, d), F32), ((1, d), F32)], deps=dep(sent_mlp))
    d_merged = _mm(d_z, weight("w_o", merged), "nt", F32, "proj_o_dx")
    dw_o = _mm(merged, d_z, "tn", BF16, "proj_o_dw")
    d_ga, d_gb, d_ya, d_yb = _rowwise("merge_bwd", lambda *v: (jax.vjp(_merge, *v[:4])[1](v[4]), ()),
                                      [gate_a, gate_b, y_a, y_b, d_merged], [], [(d, F32), (d, F32), (d, BF16), (d, BF16)])
    d_ya_in = _mm(d_ya, weight("w_pa", ya_in), "nt", F32, "proj_a_dx")
    dw_pa = _mm(ya_in, d_ya, "tn", BF16, "proj_a_dw")
    d_hb = _mm(d_yb, weight("w_pb", h_b), "nt", F32, "proj_b_dx")
    dw_pb = _mm(h_b, d_yb, "tn", BF16, "proj_b_dw")
    sent_mix = on_grads(dict(w_o=dw_o, w_pa=dw_pa, w_pb=dw_pb))

    def ml_out_bwd(a, b, c_, ct, g, s):
        parts = []
        for hs in _head_slices(HEAD_W):
            _, vjp = jax.vjp(_ml_out, a[:, hs], b[:, hs], c_[:, hs], g[:, hs], s[:, hs])
            parts.append(vjp(ct[:, hs]))
        cat = lambda i: jnp.concatenate([p[i] for p in parts], axis=1)
        return (cat(0), cat(1), cat(2)), (cat(3), cat(4))

    d_hc, d_opre, d_xc, d_gml, d_skip = _rowwise("mlstm_out_bwd", ml_out_bwd, [hc, o_pre, xc, d_hb], [g_ml, skip],
                                                 [(ml_w, F32)] * 3, [((1, ml_w), F32)] * 2, deps=dep(sent_mix))
    d_qm, d_km, d_vm, d_li, d_lf = _ml_bwd(q_m, k_m, v_m, li, lf, c_prev, n_prev, m_prev, d_hc)
    d_gl = jnp.concatenate([_gate_cols(d_li), _gate_cols(d_lf), jnp.zeros((t, LANES - 2 * HEADS), F32)], axis=1)
    pre_grads = _ml_pre_bwd(x_m, x_pad, pre_params, [d_xc, d_qm, d_km, d_vm, d_gl])
    d_xm = pre_grads[0]
    d_cw = jnp.concatenate(pre_grads[1:5], axis=0)
    d_cb = pre_grads[5]
    d_wq, d_wk, d_wv = _blockdiag_blocks(pre_grads[6:9])
    d_wif = jnp.concatenate(pre_grads[9:12], axis=0)[:, 0:2 * HEADS]
    d_bif = pre_grads[12][:, 0:2 * HEADS]

    def gla_out_bwd(o, g, ct, n_):
        parts = []
        for hs in _head_slices(HEAD_W):
            _, vjp = jax.vjp(_gla_out, o[:, hs], g[:, hs], n_)
            parts.append(vjp(ct[:, hs]))
        cat = lambda i: jnp.concatenate([p[i] for p in parts], axis=1)
        return (cat(0), cat(1)), (sum(p[2] for p in parts),)

    d_o, d_g_a, d_gn = _rowwise("gla_out_bwd", gla_out_bwd, [o_gla, g_a, d_ya_in], [gn], [(ml_w, F32)] * 2, [((1, HEAD_W), F32)])
    dq_hm, dk_hm, d_va, dla_hm = _gla_bwd(q_hm, k_hm, v_a, la_hm, s_prev, d_o)

    def decay_bwd(al, ct, w, b):
        _, vjp = jax.vjp(_log_decay, al, w, b)
        dal, dw, db = vjp(ct)
        return (dal,), (dw, db)

    d_alow_p, d_wa_p, d_ba = _rowwise("gla_decay_bwd", decay_bwd, [a_low_p, _from_hm(dla_hm)], [w_a_up_p, b_a_up],
                                      [(LANES, F32)], [(w_a_up_p.shape, F32), (b_a_up.shape, F32)])
    d_proj = jnp.concatenate([_from_hm(dq_hm), _from_hm(dk_hm), d_va, d_g_a, d_alow_p[:, 0:LOWRANK], d_xm, d_opre, d_ga, d_gb],
                             axis=1).astype(BF16)
    small = dict(w_a_up=d_wa_p[0:LOWRANK], b_a_up=d_ba, g_gla_norm=d_gn, conv_w=d_cw, conv_b=d_cb,
                 w_q_ml=d_wq, w_k_ml=d_wk, w_v_ml=d_wv, w_if=d_wif, b_if=d_bif, ml_skip=d_skip, g_ml_norm=d_gml,
                 g_post_mix=d_gpm, g_pre_mlp=d_gpl, g_post_mlp=d_gpo)
    sent_small = on_small(small, loss)
    dw_in = _mm(h, d_proj, "tn", F32, "proj_in_dw", tm=512, tk=1024, deps=dep(sent_small))
    sent_in = on_grads(dict(w_in=dw_in))
    d_h = _mm(d_proj, weight("w_in", h), "nt", F32, "proj_in_dx", tm=512, deps=dep(sent_in))

    def pre_mix_bwd(xv, dh, dres, g):
        _, vjp = jax.vjp(_rms, xv, g)
        dx, dg = vjp(dh)
        return (dx + dres,), (dg,)

    grad_x, d_g1 = _rowwise("pre_mix_norm_bwd", pre_mix_bwd, [x, d_h, dx_res], [g1], [(d, F32)], [((1, d), F32)])
    return grad_x, on_small(dict(g_pre_mix=d_g1), None)


BIG = ("w_in", "w_pa", "w_pb", "w_o", "w_up", "w_down")
BIG_COL_SHARDED = ("w_in", "w_pa", "w_pb", "w_up")
SMALL_SHARDED = ("w_a_up", "conv_w", "w_if")
SMALL = ("g_pre_mix", "w_a_up", "b_a_up", "g_gla_norm", "conv_w", "conv_b", "w_q_ml", "w_k_ml", "w_v_ml", "w_if", "b_if",
         "ml_skip", "g_ml_norm", "g_post_mix", "g_pre_mlp", "g_post_mlp")
WEIGHTS = ("g_pre_mix", "w_in", "w_a_up", "b_a_up", "g_gla_norm", "conv_w", "conv_b", "w_q_ml", "w_k_ml", "w_v_ml", "w_if", "b_if",
           "ml_skip", "g_ml_norm", "w_pa", "w_pb", "w_o", "g_post_mix", "g_pre_mlp", "w_up", "w_down", "g_post_mlp")


def kernel(x, g_pre_mix, w_in, w_a_up, b_a_up, g_gla_norm, conv_w, conv_b, w_q_ml, w_k_ml, w_v_ml, w_if, b_if, ml_skip, g_ml_norm, w_pa, w_pb, w_o, g_post_mix, g_pre_mlp, w_up, w_down, g_post_mlp, loss_target, m_g_pre_mix, m_w_in, m_w_a_up, m_b_a_up, m_g_gla_norm, m_conv_w, m_conv_b, m_w_q_ml, m_w_k_ml, m_w_v_ml, m_w_if, m_b_if, m_ml_skip, m_g_ml_norm, m_w_pa, m_w_pb, m_w_o, m_g_post_mix, m_g_pre_mlp, m_w_up, m_w_down, m_g_post_mlp, v_g_pre_mix, v_w_in, v_w_a_up, v_b_a_up, v_g_gla_norm, v_conv_w, v_conv_b, v_w_q_ml, v_w_k_ml, v_w_v_ml, v_w_if, v_b_if, v_ml_skip, v_g_ml_norm, v_w_pa, v_w_pb, v_w_o, v_g_post_mix, v_g_pre_mlp, v_w_up, v_w_down, v_g_post_mlp):
    args = dict(locals())
    w = {n: args[n][0] for n in WEIGHTS}
    m = {n: args["m_" + n][0] for n in WEIGHTS}
    v = {n: args["v_" + n][0] for n in WEIGHTS}

    me_lin = _lin(_me())
    me_idx = jnp.reshape(me_lin, (1,)).astype(jnp.int32)

    def full_weight(n, g):
        return _from_col_blocks(g) if n in BIG_COL_SHARDED else g.reshape(-1, g.shape[-1])

    def grad_parts(n, g):
        return (_col_blocks(g) if n in BIG_COL_SHARDED else g.reshape(N_DEV, -1, g.shape[-1])).astype(BF16)

    sharded_names = tuple(SMALL_SHARDED)
    small_w_state, small_w_token = _copies_start("gather", [_small_view(n, w[n]) for n in sharded_names],
                                                 "allgather_start_small_weights")
    ready = {"w_in": full_weight("w_in", _allgather_big([w["w_in"].astype(BF16)], "allgather_w_in", [small_w_token])[0])}
    pending = {}

    def prefetch(group, after):
        state, token = _copies_start("gather", [w[n].astype(BF16) for n in group], "allgather_start_" + group[0], after)
        for n in group:
            pending[n] = (group, state)
        return token

    def weight(n, after):
        if n not in ready:
            group, state = pending[n]
            shards, lands = _copies_wait(state, after, "allgather_wait_" + group[0])
            for gn, shard, land in zip(group, shards, lands):
                ready[gn] = full_weight(gn, lax.dynamic_update_slice(land, shard[None], (me_lin, 0, 0)))
        return ready[n]

    small_w_own, small_w_lands = _copies_wait(small_w_state, ready["w_in"], "allgather_wait_small_weights")
    ws = {n: (w[n].reshape(1, -1) if w[n].ndim == 1 else w[n]) for n in SMALL if n not in SMALL_SHARDED}
    for n, own, land in zip(sharded_names, small_w_own, small_w_lands):
        ws[n] = _small_unshard(n, lax.dynamic_update_slice(land, own[None], (me_lin, 0, 0)))

    sent = []

    def on_grads(grads):
        names = tuple(grads)
        state, token = _copies_start("exchange", [grad_parts(n, grads[n]) for n in names], "exchange_start_" + names[0])
        sent.append((names, state))
        return token

    small_sent = []

    def on_small(small, loss):
        names = tuple(small)
        kinds = ["exchange" if n in SMALL_SHARDED else "gather" for n in names]
        srcs = [_small_shards(n, small[n]) if n in SMALL_SHARDED else _small_view(n, small[n]) for n in names]
        extra = [] if loss is None else [loss]
        state, token = _copies_start(kinds + ["gather"] * len(extra), srcs + extra, "allgather_start_small_" + names[0])
        small_sent.append((names, kinds, state))
        return token

    grad_x, last_token = _local_step(x[0], loss_target[0], weight, ws, prefetch, on_grads, on_small)

    out = {}

    def finish(names, state, after):
        parts, lands = _copies_wait(state, after, "exchange_wait_" + names[0])
        for n, part, land in zip(names, parts, lands):
            out[n] = _sum_adamw(land, part, me_idx, w[n], m[n], v[n], "adamw_" + n)

    def finish_small(names, kinds, state, after):
        own, lands = _copies_wait(state, after, "allgather_wait_small_" + names[0])
        k = len(names)
        upd = _small_update("adamw_small_" + names[0], me_idx, kinds, lands[:k], own[:k],
                            *[[_small_view(n, d[n]) for n in names] for d in (w, m, v)], sums=list(zip(lands[k:], own[k:])))
        for i, n in enumerate(names):
            out[n] = tuple(_small_unview(n, a, w[n].shape) for a in upd[4 * i:4 * i + 4])
        return upd[4 * k:]

    (loss_sum,) = finish_small(*small_sent[0], [grad_x, last_token])
    for names, state in sent[:-1]:
        finish(names, state, [grad_x, last_token])
    finish(*sent[-1], [loss_sum] + [out[n][1] for n in BIG if n in out])
    finish_small(*small_sent[1], [out["w_in"][1]])

    shaped = lambda a, n: a.reshape(args[n].shape)
    return (loss_sum[0, 0], grad_x[None],
            *[shaped(out[n][0], n) for n in WEIGHTS], *[shaped(out[n][1], n) for n in WEIGHTS],
            *[shaped(out[n][2], n) for n in WEIGHTS], *[shaped(out[n][3], n) for n in WEIGHTS])
```

```python
import functools

import jax
import jax.numpy as jnp
from jax import lax
from jax.experimental import pallas as pl
from jax.experimental.pallas import tpu as pltpu

F32 = jnp.float32
BF16 = jnp.bfloat16
MESH = pl.DeviceIdType.MESH

N_DEV = 8
EPS = 1e-6
CHUNK = 64
HEADS = 4
GLA_DK = 64
HEAD_W = 128
GLA_GATE_NORM = 16.0
LOWRANK = 16
CONV_K = 4
QKV_BLOCK = 4
LANES = 128
HALO = 8
IN_SPLITS = (256, 256, 512, 512, 16, 512, 512, 1024, 1024)

ADAM_LR = 0.001
ADAM_B1 = 0.9
ADAM_B2 = 0.999
ADAM_EPS = 1e-08
ADAM_WD = 0.01
ADAM_STEP = 10

VMEM_LIMIT = 56 * 1024 * 1024


def _cparams(*sem):
    return pltpu.CompilerParams(dimension_semantics=sem, vmem_limit_bytes=VMEM_LIMIT)


def _dims(mode, ndim):
    contract = {"nn": ((ndim - 1,), (ndim - 2,)), "nt": ((ndim - 1,), (ndim - 1,)), "tn": ((ndim - 2,), (ndim - 2,))}[mode]
    return contract, (((0,), (0,)) if ndim == 3 else ((), ()))


def _raw_dot(a, b, mode):
    return lax.dot_general(a.astype(BF16), b.astype(BF16), _dims(mode, a.ndim), preferred_element_type=F32)


@functools.partial(jax.custom_vjp, nondiff_argnums=(2,))
def _bdot(a, b, mode):
    return _raw_dot(a, b, mode)


def _bdot_fwd(a, b, mode):
    return _raw_dot(a, b, mode), (a, b)


def _bdot_bwd(mode, res, ct):
    a, b = res
    if mode == "nn":
        da, db = _raw_dot(ct, b, "nt"), _raw_dot(a, ct, "tn")
    elif mode == "nt":
        da, db = _raw_dot(ct, b, "nn"), _raw_dot(ct, a, "tn")
    else:
        da, db = _raw_dot(b, ct, "nt"), _raw_dot(a, ct, "nn")
    return da.astype(a.dtype), db.astype(b.dtype)


_bdot.defvjp(_bdot_fwd, _bdot_bwd)


def _split3(x):
    hi = x.astype(BF16)
    r1 = x - hi.astype(F32)
    mid = r1.astype(BF16)
    return hi, mid, (r1 - mid.astype(F32)).astype(BF16)


def _split_dot(tri, x):
    if x.ndim == 3:
        tri = jnp.broadcast_to(tri, (x.shape[0], *tri.shape))
    return sum(lax.dot_general(tri, t, _dims("nn", x.ndim), preferred_element_type=F32) for t in _split3(x))


def _tri(n, lower):
    r = lax.broadcasted_iota(jnp.int32, (n, n), 0)
    c = lax.broadcasted_iota(jnp.int32, (n, n), 1)
    return ((c <= r) if lower else (c >= r)).astype(BF16)


@jax.custom_vjp
def _cumsum_rows(x):
    return _split_dot(_tri(x.shape[-2], True), x)


def _cumsum_rows_fwd(x):
    return _cumsum_rows(x), None


def _cumsum_rows_bwd(_, ct):
    return (_split_dot(_tri(ct.shape[-2], False), ct),)


_cumsum_rows.defvjp(_cumsum_rows_fwd, _cumsum_rows_bwd)


def _abs(x):
    return jnp.where(x >= 0, x, -x)


def _sigmoid(x):
    return lax.logistic(x)


def _log_sigmoid(x):
    return jnp.minimum(x, 0.0) - jnp.log(1.0 + jnp.exp(-_abs(x)))


def _rms(x, g):
    return x * lax.rsqrt(jnp.mean(x * x, axis=-1, keepdims=True) + EPS) * g


def _head_slices(w):
    return [slice(h * w, (h + 1) * w) for h in range(HEADS)]


def _heads(ref):
    return jnp.stack([ref[:, hs] for hs in _head_slices(HEAD_W)])


def _put_heads(ref, val):
    for h, hs in enumerate(_head_slices(HEAD_W)):
        ref[:, hs] = val[h]


def _tile(dim, want):
    if dim <= want or dim % LANES:
        return dim
    t = want
    while dim % t:
        t -= LANES
    return t


def _mm(a, b, mode, out_dtype, name, tm=1024, tn=1024, tk=4096, epilogue=None, extra=(), deps=()):
    if mode == "nn":
        (m, k), (k2, n) = a.shape, b.shape
    elif mode == "nt":
        (m, k), (n, k2) = a.shape, b.shape
    else:
        (k, m), (k2, n) = a.shape, b.shape
    assert k == k2, (name, a.shape, b.shape)
    tm, tn, tk = _tile(m, tm), _tile(n, tn), _tile(k, tk)
    nk = k // tk
    out_dtypes = out_dtype if epilogue else (out_dtype,)
    assert nk == 1 or (out_dtype == F32 and not epilogue), name
    n_in = 2 + len(extra)

    def body(*refs):
        p = _raw_dot(refs[0][...], refs[1][...], mode)
        if nk > 1:
            _accumulate(pl.program_id(2), [refs[n_in + len(deps)]], [p])
            return
        outs = epilogue(p, *[r[...] for r in refs[2:n_in]]) if epilogue else (p,)
        for ref, val in zip(refs[n_in + len(deps):], outs):
            ref[...] = val.astype(ref.dtype)

    a_spec = pl.BlockSpec((tk, tm), lambda i, j, kk: (kk, i)) if mode == "tn" else pl.BlockSpec((tm, tk), lambda i, j, kk: (i, kk))
    b_spec = pl.BlockSpec((tn, tk), lambda i, j, kk: (j, kk)) if mode == "nt" else pl.BlockSpec((tk, tn), lambda i, j, kk: (kk, j))
    o_spec = pl.BlockSpec((tm, tn), lambda i, j, kk: (i, j))
    res = pl.pallas_call(
        body, name=name, grid=(m // tm, n // tn, nk),
        in_specs=[a_spec, b_spec] + [o_spec] * len(extra) + [ANY] * len(deps), out_specs=[o_spec] * len(out_dtypes),
        out_shape=[jax.ShapeDtypeStruct((m, n), dt) for dt in out_dtypes],
        compiler_params=_cparams("parallel", "parallel", "arbitrary"),
    )(a, b, *extra, *deps)
    return res if epilogue else res[0]


def _rowwise(name, fn, rows, params, out_rows, out_accs=(), tile=256, deps=()):
    t = rows[0].shape[0]
    r = min(tile, t)
    assert t % r == 0
    n_in, n_or = len(rows) + len(params), len(out_rows)
    n_all = n_in + len(deps)
    params = list(params) + list(deps)

    def body(*refs):
        vals = [ref[...] for ref in refs[:n_in]]
        outs = refs[n_all:]
        ro, ao = fn(*vals)
        for ref, v in zip(outs[:n_or], ro):
            ref[...] = v.astype(ref.dtype)
        if out_accs:
            _accumulate(pl.program_id(0), outs[n_or:], ao)

    def full(shape):
        return pl.BlockSpec(shape, lambda i, nd=len(shape): (0,) * nd)

    return pl.pallas_call(
        body, name=name, grid=(t // r,),
        in_specs=[pl.BlockSpec((r, a.shape[1]), lambda i: (i, 0)) for a in rows] + [full(p.shape) for p in params],
        out_specs=[pl.BlockSpec((r, w), lambda i: (i, 0)) for w, _ in out_rows] + [full(s) for s, _ in out_accs],
        out_shape=[jax.ShapeDtypeStruct((t, w), dt) for w, dt in out_rows] + [jax.ShapeDtypeStruct(s, dt) for s, dt in out_accs],
        compiler_params=_cparams("arbitrary"),
    )(*rows, *params)


def _accumulate(step, refs, vals):
    for ref, v in zip(refs, vals):
        @pl.when(step == 0)
        def _(ref=ref, v=v):
            ref[...] = v.astype(ref.dtype)

        @pl.when(step > 0)
        def _(ref=ref, v=v):
            ref[...] += v.astype(ref.dtype)


def _gla_chunk(q, k, v, la, st):
    c = q.shape[-2]
    row = lax.broadcasted_iota(jnp.int32, (c, c), 0)
    col = lax.broadcasted_iota(jnp.int32, (c, c), 1)
    cum = _cumsum_rows(la)
    cl = jnp.sum(la, axis=-2, keepdims=True)
    ep = jnp.exp(cum)
    en = jnp.exp(-cum)
    qs = q * (GLA_DK ** -0.5)
    qp = qs * ep
    a_f = _bdot(qp, k * en, "nt")
    a_b = _bdot(qs * en, k * ep, "nt")
    sc = jnp.where(row >= col, a_f, a_b)
    o = _bdot(sc, v, "nn") + _bdot(qp, st, "nt")
    kd = k * jnp.exp(cl - cum)
    st_new = st * jnp.exp(cl) + _bdot(v, kd, "tn")
    return o, st_new


def _gla_specs(nc, rev):
    def ch(n):
        return (nc - 1 - n) if rev else n
    hm = pl.BlockSpec((HEADS, CHUNK, GLA_DK), lambda n: (0, ch(n), 0))
    tm = pl.BlockSpec((CHUNK, HEADS * HEAD_W), lambda n: (ch(n), 0))
    st = pl.BlockSpec((HEADS, None, HEAD_W, GLA_DK), lambda n: (0, ch(n), 0, 0))
    return hm, tm, st


def _gla_fwd(q, k, v, la):
    t = v.shape[0]
    nc = t // CHUNK
    hm, tm, st = _gla_specs(nc, False)

    def body(q_ref, k_ref, v_ref, la_ref, o_ref, sp_ref, st_ref):
        @pl.when(pl.program_id(0) == 0)
        def _():
            st_ref[...] = jnp.zeros_like(st_ref)

        s = st_ref[...]
        sp_ref[...] = s
        o, s_new = _gla_chunk(q_ref[...], k_ref[...], _heads(v_ref), la_ref[...], s)
        _put_heads(o_ref, o)
        st_ref[...] = s_new

    return pl.pallas_call(
        body, name="gla_fwd", grid=(nc,),
        in_specs=[hm, hm, tm, hm], out_specs=[tm, st],
        out_shape=[jax.ShapeDtypeStruct((t, HEADS * HEAD_W), F32), jax.ShapeDtypeStruct((HEADS, nc, HEAD_W, GLA_DK), F32)],
        scratch_shapes=[pltpu.VMEM((HEADS, HEAD_W, GLA_DK), F32)],
        compiler_params=_cparams("arbitrary"),
    )(q, k, v, la)


def _gla_bwd(q, k, v, la, sp, do):
    t = v.shape[0]
    nc = t // CHUNK
    hm, tm, st = _gla_specs(nc, True)

    def body(q_ref, k_ref, v_ref, la_ref, sp_ref, do_ref, dq_ref, dk_ref, dv_ref, dla_ref, ds_ref):
        @pl.when(pl.program_id(0) == 0)
        def _():
            ds_ref[...] = jnp.zeros_like(ds_ref)

        _, vjp = jax.vjp(_gla_chunk, q_ref[...], k_ref[...], _heads(v_ref), la_ref[...], sp_ref[...])
        dq, dk, dv, dla, ds = vjp((_heads(do_ref), ds_ref[...]))
        dq_ref[...] = dq
        dk_ref[...] = dk
        _put_heads(dv_ref, dv)
        dla_ref[...] = dla
        ds_ref[...] = ds

    hm_shape = jax.ShapeDtypeStruct((HEADS, t, GLA_DK), F32)
    return pl.pallas_call(
        body, name="gla_bwd", grid=(nc,),
        in_specs=[hm, hm, tm, hm, st, tm], out_specs=[hm, hm, tm, hm],
        out_shape=[hm_shape, hm_shape, jax.ShapeDtypeStruct((t, HEADS * HEAD_W), F32), hm_shape],
        scratch_shapes=[pltpu.VMEM((HEADS, HEAD_W, GLA_DK), F32)],
        compiler_params=_cparams("arbitrary"),
    )(q, k, v, la, sp, do)


def _ml_chunk(q, k, v, li_r, lf_r, cm, nv, m):
    c = q.shape[-2]
    row = lax.broadcasted_iota(jnp.int32, (c, c), 0)
    col = lax.broadcasted_iota(jnp.int32, (c, c), 1)
    eye = (row == col).astype(F32)
    li_c = jnp.sum(eye * li_r, axis=-1, keepdims=True)
    lf_c = jnp.sum(eye * lf_r, axis=-1, keepdims=True)
    fc_c = jnp.sum((col <= row).astype(F32) * lf_r, axis=-1, keepdims=True)
    fc_r = jnp.sum((row <= col).astype(F32) * lf_c, axis=-2, keepdims=True)
    f_last = jnp.sum(lf_r, axis=-1, keepdims=True)
    kc = k * (HEAD_W ** -0.5)
    a_c = f_last - fc_c + li_c
    m_loc = jnp.max(a_c, axis=-2, keepdims=True)
    kw = kc * jnp.exp(a_c - m_loc)
    c_chunk = _bdot(kw, v, "tn")
    n_chunk = jnp.sum(kw, axis=-2, keepdims=True)
    m_new = jnp.maximum(f_last + m, m_loc)
    sp = jnp.exp(f_last + m - m_new)
    sl = jnp.exp(m_loc - m_new)
    cm_new = sp * cm + sl * c_chunk
    nv_new = sp * nv + sl * n_chunk
    log_d = li_r - _abs(fc_c - fc_r)
    g_inter = fc_c + m
    m_t = jnp.maximum(g_inter, jnp.max(log_d, axis=-1, keepdims=True))
    s = _bdot(q, kc, "nt") * jnp.exp(log_d - m_t)
    sc = jnp.exp(g_inter - m_t)
    num = _bdot(s, v, "nn") + sc * _bdot(q, cm, "nn")
    den = jnp.sum(s, axis=-1, keepdims=True) + sc * jnp.sum(q * nv, axis=-1, keepdims=True)
    den = jnp.maximum(_abs(den), jnp.exp(-m_t))
    return num / den, cm_new, nv_new, m_new


def _ml_specs(nc, rev):
    def ch(n):
        return (nc - 1 - n) if rev else n
    tm = pl.BlockSpec((CHUNK, HEADS * HEAD_W), lambda n: (ch(n), 0))
    gate = pl.BlockSpec((HEADS, None, 1, CHUNK), lambda n: (0, ch(n), 0, 0))
    cm = pl.BlockSpec((HEADS, None, HEAD_W, HEAD_W), lambda n: (0, ch(n), 0, 0))
    vec = pl.BlockSpec((HEADS, None, 1, HEAD_W), lambda n: (0, ch(n), 0, 0))
    return tm, gate, cm, vec


_ML_STATE = [pltpu.VMEM((HEADS, HEAD_W, HEAD_W), F32), pltpu.VMEM((HEADS, 1, HEAD_W), F32), pltpu.VMEM((HEADS, 1, HEAD_W), F32)]


def _ml_fwd(q, k, v, li, lf):
    t = q.shape[0]
    nc = t // CHUNK
    tm, gate, cm, vec = _ml_specs(nc, False)

    def body(q_ref, k_ref, v_ref, li_ref, lf_ref, hc_ref, cp_ref, np_ref, mp_ref, c_ref, n_ref, m_ref):
        @pl.when(pl.program_id(0) == 0)
        def _():
            c_ref[...] = jnp.zeros_like(c_ref)
            n_ref[...] = jnp.zeros_like(n_ref)
            m_ref[...] = jnp.zeros_like(m_ref)

        c0, n0, m0 = c_ref[...], n_ref[...], m_ref[...]
        cp_ref[...] = c0
        np_ref[...] = n0
        mp_ref[...] = m0
        hc, c1, n1, m1 = _ml_chunk(_heads(q_ref), _heads(k_ref), _heads(v_ref), li_ref[...], lf_ref[...],
                                   c0, n0, m0[:, :, 0:1])
        _put_heads(hc_ref, hc)
        c_ref[...] = c1
        n_ref[...] = n1
        m_ref[...] = jnp.broadcast_to(m1, m_ref.shape)

    return pl.pallas_call(
        body, name="mlstm_fwd", grid=(nc,),
        in_specs=[tm, tm, tm, gate, gate], out_specs=[tm, cm, vec, vec],
        out_shape=[jax.ShapeDtypeStruct((t, HEADS * HEAD_W), F32), jax.ShapeDtypeStruct((HEADS, nc, HEAD_W, HEAD_W), F32),
                   jax.ShapeDtypeStruct((HEADS, nc, 1, HEAD_W), F32), jax.ShapeDtypeStruct((HEADS, nc, 1, HEAD_W), F32)],
        scratch_shapes=_ML_STATE,
        compiler_params=_cparams("arbitrary"),
    )(q, k, v, li, lf)


def _ml_bwd(q, k, v, li, lf, cp, npv, mp, dhc):
    t = q.shape[0]
    nc = t // CHUNK
    tm, gate, cm, vec = _ml_specs(nc, True)

    def body(q_ref, k_ref, v_ref, li_ref, lf_ref, cp_ref, np_ref, mp_ref, dhc_ref,
             dq_ref, dk_ref, dv_ref, dli_ref, dlf_ref, dc_ref, dn_ref, dm_ref):
        @pl.when(pl.program_id(0) == 0)
        def _():
            dc_ref[...] = jnp.zeros_like(dc_ref)
            dn_ref[...] = jnp.zeros_like(dn_ref)
            dm_ref[...] = jnp.zeros_like(dm_ref)

        _, vjp = jax.vjp(_ml_chunk, _heads(q_ref), _heads(k_ref), _heads(v_ref), li_ref[...], lf_ref[...],
                         cp_ref[...], np_ref[...], mp_ref[...][:, :, 0:1])
        dq, dk, dv, dli, dlf, dc, dn, dm = vjp((_heads(dhc_ref), dc_ref[...], dn_ref[...], dm_ref[...][:, :, 0:1]))
        _put_heads(dq_ref, dq)
        _put_heads(dk_ref, dk)
        _put_heads(dv_ref, dv)
        dli_ref[...] = dli
        dlf_ref[...] = dlf
        dc_ref[...] = dc
        dn_ref[...] = dn
        dm_ref[...] = jnp.broadcast_to(dm, dm_ref.shape)

    tm_shape = jax.ShapeDtypeStruct((t, HEADS * HEAD_W), F32)
    gate_shape = jax.ShapeDtypeStruct((HEADS, nc, 1, CHUNK), F32)
    return pl.pallas_call(
        body, name="mlstm_bwd", grid=(nc,),
        in_specs=[tm, tm, tm, gate, gate, cm, vec, vec, tm], out_specs=[tm, tm, tm, gate, gate],
        out_shape=[tm_shape, tm_shape, tm_shape, gate_shape, gate_shape],
        scratch_shapes=_ML_STATE,
        compiler_params=_cparams("arbitrary"),
    )(q, k, v, li, lf, cp, npv, mp, dhc)


def _ml_pre(s0, s1, s2, s3, cw0, cw1, cw2, cw3, cb, wq, wk, wv, wiq, wik, wiv, bif):
    pre = cb + cw0 * s0 + cw1 * s1 + cw2 * s2 + cw3 * s3
    xc = pre * _sigmoid(pre)
    q = _bdot(xc, wq, "nn")
    k = _bdot(xc, wk, "nn")
    v = _bdot(s3, wv, "nn")
    gates = _bdot(q, wiq, "nn") + _bdot(k, wik, "nn") + _bdot(v, wiv, "nn") + bif
    lane = lax.broadcasted_iota(jnp.int32, gates.shape, 1)
    gl = jnp.where(lane < HEADS, gates, _log_sigmoid(gates))
    return xc, q, k, v, gl


def _delayed(xs_ref, x_ref, halo_ref, r):
    xs_ref[0:HALO, :] = halo_ref[...]
    xs_ref[HALO:HALO + r, :] = x_ref[...]
    return [xs_ref[pl.ds(HALO - (CONV_K - 1) + j, r), :] for j in range(CONV_K)]


def _full_spec(shape):
    return pl.BlockSpec(shape, lambda i, nd=len(shape): (0,) * nd)


def _ml_pre_fwd(x_m, x_pad, params, tile=256):
    t, w = x_m.shape
    r = min(tile, t)

    def body(*refs):
        x_ref, halo_ref = refs[:2]
        p = [ref[...] for ref in refs[2:2 + len(params)]]
        outs = refs[2 + len(params):-1]
        res = _ml_pre(*_delayed(refs[-1], x_ref, halo_ref, r), *p)
        for ref, val in zip(outs, res):
            ref[...] = val

    row = pl.BlockSpec((r, w), lambda i: (i, 0))
    return pl.pallas_call(
        body, name="ml_pre_fwd", grid=(t // r,),
        in_specs=[row, pl.BlockSpec((HALO, w), lambda i: (i * (r // HALO), 0))] + [_full_spec(p.shape) for p in params],
        out_specs=[row] * 4 + [pl.BlockSpec((r, LANES), lambda i: (i, 0))],
        out_shape=[jax.ShapeDtypeStruct((t, w), F32)] * 4 + [jax.ShapeDtypeStruct((t, LANES), F32)],
        scratch_shapes=[pltpu.VMEM((r + HALO, w), F32)],
        compiler_params=_cparams("arbitrary"),
    )(x_m, x_pad, *params)


def _ml_pre_bwd(x_m, x_pad, params, cts, tile=256):
    t, w = x_m.shape
    r = min(tile, t)
    nt = t // r
    n_p = len(params)

    def body(*refs):
        x_ref, halo_ref = refs[:2]
        p = [ref[...] for ref in refs[2:2 + n_p]]
        ct = [ref[...] for ref in refs[2 + n_p:7 + n_p]]
        dx_ref = refs[7 + n_p]
        dp_refs = refs[8 + n_p:8 + 2 * n_p]
        xs_ref, ds_ref, carry_ref = refs[8 + 2 * n_p:]
        step = pl.program_id(0)

        @pl.when(step == 0)
        def _():
            ds_ref[...] = jnp.zeros_like(ds_ref)
            carry_ref[...] = jnp.zeros_like(carry_ref)

        _, vjp = jax.vjp(_ml_pre, *_delayed(xs_ref, x_ref, halo_ref, r), *p)
        grads = vjp(tuple(ct))
        for j in range(CONV_K):
            ds_ref[j, HALO:HALO + r, :] = grads[j]
        lead = HALO + CONV_K - 1
        d_tile = sum(ds_ref[j, pl.ds(lead - j, r), :] for j in range(CONV_K))
        d_halo = sum(ds_ref[j, pl.ds(CONV_K - 1 - j, HALO), :] for j in range(CONV_K))
        dx_ref[...] = d_tile
        dx_ref[r - HALO:r, :] += carry_ref[...]
        carry_ref[...] = d_halo
        _accumulate(step, dp_refs, grads[CONV_K:])

    row = pl.BlockSpec((r, w), lambda i: (nt - 1 - i, 0))
    return pl.pallas_call(
        body, name="ml_pre_bwd", grid=(nt,),
        in_specs=[row, pl.BlockSpec((HALO, w), lambda i: ((nt - 1 - i) * (r // HALO), 0))] + [_full_spec(p.shape) for p in params]
        + [row] * 4 + [pl.BlockSpec((r, LANES), lambda i: (nt - 1 - i, 0))],
        out_specs=[row] + [_full_spec(p.shape) for p in params],
        out_shape=[jax.ShapeDtypeStruct((t, w), F32)] + [jax.ShapeDtypeStruct(p.shape, F32) for p in params],
        scratch_shapes=[pltpu.VMEM((r + HALO, w), F32), pltpu.VMEM((CONV_K, r + 2 * HALO, w), F32), pltpu.VMEM((HALO, w), F32)],
        compiler_params=_cparams("arbitrary"),
    )(x_m, x_pad, *params, *cts)


def _per_head(fn, row_vals, head_params, shared_params=()):
    return [fn(*[a[:, hs] for a in row_vals], *[p[:, hs] for p in head_params], *shared_params) for hs in _head_slices(HEAD_W)]


def _gla_out(o, g, gn):
    return _rms(o, gn) * (g * _sigmoid(g))


def _ml_out(hc, op, xc, g, sk):
    hcell = hc * _sigmoid(op)
    mu = jnp.mean(hcell, axis=-1, keepdims=True)
    d = hcell - mu
    var = jnp.mean(d * d, axis=-1, keepdims=True)
    return d * lax.rsqrt(var + EPS) * g + sk * xc


def _log_decay(al, w, b):
    return _log_sigmoid(_bdot(al, w, "nn") + b) * (1.0 / GLA_GATE_NORM)


def _merge(ga, gb, ya, yb):
    return _sigmoid(ga) * ya + _sigmoid(gb) * yb


def _post_mix(x, z, gpm, gpl):
    x1 = x + _rms(z, gpm)
    return x1, _rms(x1, gpl)


def _loss_rows(x1, dn, tgt, g):
    e = x1 + _rms(dn, g) - tgt
    return 0.5 * jnp.sum(jnp.mean(e * e, axis=-1, keepdims=True), axis=0, keepdims=True)


def _lin(p):
    return 4 * p[0] + 2 * p[1] + p[2]


def _me():
    return lax.axis_index("x"), lax.axis_index("y"), lax.axis_index("c")


def _flip(p, k):
    return tuple((1 - v) if (k >> (2 - i)) & 1 else v for i, v in enumerate(p))


ANY = pl.BlockSpec(memory_space=pl.ANY)


def _allgather_big(shards, name, deps=()):
    n = len(shards)
    n_in = n + len(deps)

    def body(*refs):
        ins, outs = refs[:n], refs[n_in:n_in + n]
        send_sems, recv_sems, local_sems = refs[n_in + n:]
        me = _me()
        x, y, c = me
        sib = (x, y, 1 - c)
        chips = [(1 - x, y), (x, 1 - y), (1 - x, 1 - y)]

        def cp(a, k, block, to, src=None):
            dst = outs[a].at[_lin(block)]
            return pltpu.make_async_remote_copy(src_ref=dst if src is None else src, dst_ref=dst,
                                                send_sem=send_sems.at[a * 7 + k], recv_sem=recv_sems.at[a * 7 + k],
                                                device_id=to, device_id_type=MESH)

        mine = [pltpu.make_async_copy(ins[a], outs[a].at[_lin(me)], local_sems.at[a]) for a in range(n)]
        for m in mine:
            m.start()
        first = []
        for a in range(n):
            first.append(cp(a, 0, me, sib, src=ins[a]))
            first += [cp(a, 1 + j, me, (*chip, c), src=ins[a]) for j, chip in enumerate(chips)]
        for f in first:
            f.start()
        passed = []
        for j, chip in enumerate(chips):
            for a in range(n):
                cp(a, 1 + j, (*chip, c), me).wait_recv()
                fwd = cp(a, 4 + j, (*chip, c), sib)
                fwd.start()
                passed.append(fwd)
        for a in range(n):
            cp(a, 0, sib, me).wait_recv()
            for j, chip in enumerate(chips):
                cp(a, 4 + j, (*chip, 1 - c), me).wait_recv()
        for f in first + passed:
            f.wait_send()
        for m in mine:
            m.wait()

    return pl.pallas_call(
        body, name=name,
        in_specs=[ANY] * n_in, out_specs=[ANY] * n,
        out_shape=[jax.ShapeDtypeStruct((N_DEV, *s.shape), s.dtype) for s in shards],
        scratch_shapes=[pltpu.SemaphoreType.DMA((7 * n,)), pltpu.SemaphoreType.DMA((7 * n,)), pltpu.SemaphoreType.DMA((n,))],
    )(*shards, *deps)


HBM = pl.BlockSpec(memory_space=pltpu.HBM)
SEM = pl.BlockSpec(memory_space=pltpu.SEMAPHORE)
DATAFLOW = pltpu.SideEffectType.DATAFLOW_SIDE_EFFECTING


def _peer_copies(kinds, srcs, lands, send_sems, recv_sems):
    me = _me()
    copies = []
    for a, (kind, src, land) in enumerate(zip(kinds, srcs, lands)):
        for k in range(1, N_DEV):
            peer = _flip(me, k)
            copies.append(pltpu.make_async_remote_copy(
                src_ref=src if kind == "gather" else src.at[_lin(peer)], dst_ref=land.at[_lin(me)],
                send_sem=send_sems.at[a * 7 + k - 1], recv_sem=recv_sems.at[a * 7 + k - 1],
                device_id=peer, device_id_type=MESH))
    return copies


def _copies_start(kind, srcs, name, after=None):
    n = len(srcs)
    extra = [] if after is None else [after]
    kind = [kind] * n if isinstance(kind, str) else list(kind)
    land_shapes = [((N_DEV, *s.shape) if k == "gather" else s.shape) for k, s in zip(kind, srcs)]

    def body(*refs):
        sems = refs[2 * n + len(extra):]
        for cp in _peer_copies(kind, refs[:n], refs[n:2 * n], sems[0], sems[1]):
            cp.start()
        refs[-1][...] = jnp.zeros_like(refs[-1])

    def hbm(a):
        return pltpu.with_memory_space_constraint(a, pltpu.HBM)

    out = pl.pallas_call(
        body, name=name,
        out_shape=(pltpu.SemaphoreType.DMA((7 * n,)), pltpu.SemaphoreType.DMA((7 * n,)),
                   *[pltpu.HBM(s.shape, s.dtype) for s in srcs],
                   *[pltpu.HBM(ls, s.dtype) for ls, s in zip(land_shapes, srcs)],
                   jax.ShapeDtypeStruct((8, LANES), F32)),
        in_specs=[HBM] * (2 * n) + [ANY] * len(extra),
        out_specs=(SEM, SEM, *[HBM] * (2 * n), pl.BlockSpec(memory_space=pltpu.VMEM)),
        input_output_aliases={i: 2 + i for i in range(2 * n)},
        compiler_params=pltpu.CompilerParams(has_side_effects=DATAFLOW),
    )(*[hbm(s) for s in srcs], *[hbm(lax.empty(ls, s.dtype)) for ls, s in zip(land_shapes, srcs)], *extra)
    return (kind, n, out[:-1]), out[-1]


def _copies_wait(state, after, name):
    kind, n, (send_sems, recv_sems, *thru) = state
    after = list(after) if isinstance(after, (list, tuple)) else [after]

    def body(*refs):
        for cp in _peer_copies(kind, refs[:n], refs[n:2 * n], refs[2 * n], refs[2 * n + 1]):
            cp.wait_send()
            cp.wait_recv()

    out = pl.pallas_call(
        body, name=name,
        out_shape=tuple(pltpu.HBM(t.shape, t.dtype) for t in thru),
        in_specs=[HBM] * (2 * n) + [SEM, SEM] + [ANY] * len(after), out_specs=tuple([HBM] * (2 * n)),
        input_output_aliases={i: i for i in range(2 * n)},
        compiler_params=pltpu.CompilerParams(has_side_effects=DATAFLOW),
    )(*thru, send_sems, recv_sems, *after)
    return out[:n], out[n:]


def _adamw(w, g, m, v):
    m2 = ADAM_B1 * m + (1.0 - ADAM_B1) * g
    v2 = ADAM_B2 * v + (1.0 - ADAM_B2) * (g * g)
    m_hat = m2 / (1.0 - ADAM_B1 ** ADAM_STEP)
    v_hat = v2 / (1.0 - ADAM_B2 ** ADAM_STEP)
    delta = -ADAM_LR * (m_hat / (jnp.sqrt(v_hat) + ADAM_EPS) + ADAM_WD * w)
    return delta, m2, v2


def _sum_adamw(land, part, me_idx, w, m, v, name, tile=256):
    r, c = w.shape
    tr = min(tile, r)

    def body(me_ref, own_ref, *refs):
        slots = refs[:N_DEV]
        w_ref, m_ref, v_ref, g_ref, d_ref, m2_ref, v2_ref = refs[N_DEV:]
        own = own_ref[...].astype(F32)
        g = None
        for s in range(N_DEV):
            term = jnp.where(me_ref[0] == s, own, slots[s][...].astype(F32))
            g = term if g is None else g + term
        d, m2, v2 = _adamw(w_ref[...], g, m_ref[...], v_ref[...])
        g_ref[...] = g
        d_ref[...] = d
        m2_ref[...] = m2
        v2_ref[...] = v2

    def slot_spec(s):
        return pl.BlockSpec((None, tr, c), lambda i, me: (jnp.where(me[0] == s, (s + 1) % N_DEV, s), i, 0))

    row = pl.BlockSpec((tr, c), lambda i, me: (i, 0))
    return pl.pallas_call(
        body, name=name,
        grid_spec=pltpu.PrefetchScalarGridSpec(
            num_scalar_prefetch=1, grid=(r // tr,),
            in_specs=[pl.BlockSpec((None, tr, c), lambda i, me: (me[0], i, 0))] + [slot_spec(s) for s in range(N_DEV)] + [row] * 3,
            out_specs=[row] * 4),
        out_shape=[jax.ShapeDtypeStruct((r, c), F32)] * 4,
        compiler_params=_cparams("parallel"),
    )(me_idx, part, *[land] * N_DEV, w, m, v)


def _small_update(name, me_idx, kinds, lands, owns, ws, ms, vs, sums=()):
    n = len(ws)
    lands, owns = list(lands) + [s[0] for s in sums], list(owns) + [s[1] for s in sums]
    kinds = list(kinds) + ["gather"] * len(sums)
    nl = len(lands)

    def summed(me, land_ref, own):
        g = None
        for s in range(N_DEV):
            term = jnp.where(me == s, own, land_ref[s])
            g = term if g is None else g + term
        return g

    def body(me_ref, *refs):
        land_refs, own_refs = refs[:nl], refs[nl:2 * nl]
        w_refs, m_refs, v_refs = (refs[2 * nl + i * n:2 * nl + (i + 1) * n] for i in range(3))
        outs = refs[2 * nl + 3 * n:]
        me = me_ref[0]
        for i in range(n):
            g = summed(me, land_refs[i], own_refs[i][...])
            d, m2, v2 = _adamw(w_refs[i][...], g, m_refs[i][...], v_refs[i][...])
            for ref, val in zip(outs[4 * i:4 * i + 4], (g, d, m2, v2)):
                ref[...] = val
        for i in range(n, nl):
            outs[4 * n + i - n][...] = summed(me, land_refs[i], own_refs[i][...])

    def whole(shape):
        return pl.BlockSpec(shape, lambda i, me, nd=len(shape): (0,) * nd)

    def own_spec(kind, own):
        if kind == "gather":
            return whole(own.shape)
        return pl.BlockSpec((None, *own.shape[1:]), lambda i, me: (me[0], 0, 0))

    shapes = [w.shape for w in ws]
    out_shapes = [s for s in shapes for _ in range(4)] + [s[1].shape for s in sums]
    return pl.pallas_call(
        body, name=name,
        grid_spec=pltpu.PrefetchScalarGridSpec(
            num_scalar_prefetch=1, grid=(1,),
            in_specs=[whole(a.shape) for a in lands] + [own_spec(k, o) for k, o in zip(kinds, owns)]
            + [whole(s) for s in shapes] * 3,
            out_specs=[whole(s) for s in out_shapes]),
        out_shape=[jax.ShapeDtypeStruct(s, F32) for s in out_shapes],
        compiler_params=_cparams("arbitrary"),
    )(me_idx, *lands, *owns, *ws, *ms, *vs)


def _small_view(n, a):
    if a.ndim == 1:
        return a.reshape(1, -1)
    if a.ndim == 3:
        return a.transpose(1, 2, 0).reshape(QKV_BLOCK * QKV_BLOCK, -1)
    return a.T if n == "w_if" else a


def _small_unview(n, a, shape):
    if len(shape) == 1:
        return a.reshape(shape)
    if len(shape) == 3:
        return a.reshape(QKV_BLOCK, QKV_BLOCK, -1).transpose(2, 0, 1)
    return a.T if n == "w_if" else a


def _small_shards(n, g):
    if n == "w_if":
        return g.reshape(N_DEV, -1, g.shape[1]).transpose(0, 2, 1)
    return g.reshape(g.shape[0], N_DEV, -1).transpose(1, 0, 2)


def _small_unshard(n, s):
    if n == "w_if":
        return s.transpose(0, 2, 1).reshape(-1, s.shape[1])
    return s.transpose(1, 0, 2).reshape(s.shape[1], -1)


def _to_hm(a, d):
    t = a.shape[0]
    return a.reshape(t, HEADS, d).transpose(1, 0, 2)


def _from_hm(a):
    h, t, d = a.shape
    return a.transpose(1, 0, 2).reshape(t, h * d)


def _gate_rows(g):
    t = g.shape[0]
    return g.T.reshape(HEADS, t // CHUNK, 1, CHUNK)


def _gate_cols(g):
    h, nc, _, c = g.shape
    return g.reshape(h, nc * c).T


def _blockdiag_dense(w):
    n = w.shape[0] * QKV_BLOCK
    tiled = jnp.tile(w.reshape(n, QKV_BLOCK), (1, n // QKV_BLOCK))
    r = lax.broadcasted_iota(jnp.int32, (n, n), 0)
    c = lax.broadcasted_iota(jnp.int32, (n, n), 1)
    return jnp.where(r // QKV_BLOCK == c // QKV_BLOCK, tiled, 0.0)


def _blockdiag_blocks(dense):
    n = dense[0].shape[0]
    k = len(dense)

    def body(*refs):
        r = lax.broadcasted_iota(jnp.int32, (n, n), 0)
        c = lax.broadcasted_iota(jnp.int32, (n, n), 1)
        fr = lax.broadcasted_iota(jnp.int32, (n, LANES), 0)
        fc = lax.broadcasted_iota(jnp.int32, (n, LANES), 1)
        fold = ((fr & (QKV_BLOCK - 1)) == fc).astype(BF16)
        for i in range(k):
            kept = jnp.where((r >> 2) == (c >> 2), refs[i][...], 0.0)
            refs[k + i][...] = sum(lax.dot_general(t, fold, _dims("nn", 2), preferred_element_type=F32) for t in _split3(kept))

    out = pl.pallas_call(body, name="blockdiag_blocks", out_shape=[jax.ShapeDtypeStruct((n, LANES), F32)] * k)(*dense)
    return [o[:, 0:QKV_BLOCK].reshape(n // QKV_BLOCK, QKV_BLOCK, QKV_BLOCK) for o in out]


def _col_blocks(w):
    k, n = w.shape
    return w.reshape(k, N_DEV, n // N_DEV).transpose(1, 0, 2)


def _from_col_blocks(g):
    d, k, n = g.shape
    return g.transpose(1, 0, 2).reshape(k, d * n)


def _local_step(x, tgt, weight, ws, prefetch, on_grads, on_small):
    t, d = x.shape
    g1 = ws["g_pre_mix"]

    def dep(token):
        return () if token is None else (token,)

    w_in = weight("w_in", x)
    fetch_mix = prefetch(("w_pa", "w_pb", "w_o"), w_in)

    def proj_in_fwd(xv, g, w):
        hv = _rms(xv, g)
        return (hv, _raw_dot(hv, w, "nn")), ()

    h, proj = _rowwise("proj_in", proj_in_fwd, [x], [g1, w_in], [(d, BF16), (w_in.shape[1], F32)], deps=dep(fetch_mix))
    offs = [0]
    for s in IN_SPLITS:
        offs.append(offs[-1] + s)
    q_a, k_a, v_a, g_a, a_low, x_m, o_pre, gate_a, gate_b = [proj[:, offs[i]:offs[i + 1]] for i in range(9)]

    a_low_p = jnp.pad(a_low, ((0, 0), (0, LANES - LOWRANK)))
    w_a_up_p = jnp.pad(ws["w_a_up"], ((0, LANES - LOWRANK), (0, 0)))
    b_a_up = ws["b_a_up"]
    (la,) = _rowwise("gla_decay", lambda al, w, b: ((_log_decay(al, w, b),), ()), [a_low_p], [w_a_up_p, b_a_up],
                     [(HEADS * GLA_DK, F32)])
    fetch_up = prefetch(("w_up",), la)
    q_hm, k_hm, la_hm = _to_hm(q_a, GLA_DK), _to_hm(k_a, GLA_DK), _to_hm(la, GLA_DK)
    o_gla, s_prev = _gla_fwd(q_hm, k_hm, v_a, la_hm)
    gn = ws["g_gla_norm"]
    ml_w = HEADS * HEAD_W

    def proj_a_fwd(o, g, n_, w):
        ya = jnp.concatenate(_per_head(_gla_out, [o, g], [], [n_]), axis=1)
        return (ya, _raw_dot(ya, w, "nn")), ()

    ya_in, y_a = _rowwise("proj_a", proj_a_fwd, [o_gla, g_a], [gn, weight("w_pa", o_gla)], [(ml_w, BF16), (d, F32)],
                          tile=512, deps=dep(fetch_up))

    cw = ws["conv_w"]
    w_if_p = jnp.pad(ws["w_if"], ((0, 0), (0, LANES - 2 * HEADS)))
    pre_params = [cw[0:1], cw[1:2], cw[2:3], cw[3:4], ws["conv_b"],
                  _blockdiag_dense(ws["w_q_ml"]), _blockdiag_dense(ws["w_k_ml"]), _blockdiag_dense(ws["w_v_ml"]),
                  w_if_p[0:ml_w], w_if_p[ml_w:2 * ml_w], w_if_p[2 * ml_w:3 * ml_w],
                  jnp.pad(ws["b_if"], ((0, 0), (0, LANES - 2 * HEADS)))]
    x_pad = jnp.pad(x_m, ((HALO, 0), (0, 0)))
    xc, q_m, k_m, v_m, gl = _ml_pre_fwd(x_m, x_pad, pre_params)
    li, lf = _gate_rows(gl[:, 0:HEADS]), _gate_rows(gl[:, HEADS:2 * HEADS])
    hc, c_prev, n_prev, m_prev = _ml_fwd(q_m, k_m, v_m, li, lf)
    fetch_down = prefetch(("w_down",), hc)
    g_ml, skip = ws["g_ml_norm"], ws["ml_skip"]

    def proj_b_fwd(a, b, c_, ga, gb, ya, g, s, w):
        hb = jnp.concatenate(_per_head(_ml_out, [a, b, c_], [g, s]), axis=1)
        yb = _raw_dot(hb, w, "nn")
        return (hb, yb, _merge(ga, gb, ya, yb)), ()

    h_b, y_b, merged = _rowwise("proj_b", proj_b_fwd, [hc, o_pre, xc, gate_a, gate_b, y_a], [g_ml, skip, weight("w_pb", hc)],
                                [(ml_w, BF16), (d, F32), (d, BF16)], tile=512, deps=dep(fetch_down))

    gpm, gpl, gpo = ws["g_post_mix"], ws["g_pre_mlp"], ws["g_post_mlp"]

    def proj_o_fwd(mg, xv, w, a, b):
        zv = _raw_dot(mg, w, "nn")
        return (zv, *_post_mix(xv, zv, a, b)), ()

    z, x1, h2 = _rowwise("proj_o", proj_o_fwd, [merged, x], [weight("w_o", merged), gpm, gpl],
                         [(d, F32), (d, F32), (d, BF16)], tile=512)
    up, u = _mm(h2, weight("w_up", h2), "nn", (BF16, BF16), "mlp_up", epilogue=lambda p: (p, jnp.square(jnp.maximum(p, 0.0))))

    def mlp_down_loss(uv, x1v, tgtv, w, g):
        dnv = _raw_dot(uv, w, "nn")
        loss, vjp = jax.vjp(lambda a, b, c_: _loss_rows(a, b, tgtv, c_), x1v, dnv, g)
        dx1, ddn, dg = vjp(jnp.ones((1, 1), F32))
        return (dx1, ddn), (jnp.broadcast_to(loss, (1, LANES)), dg)

    dx1_y, d_dn, loss, d_gpo = _rowwise("mlp_down", mlp_down_loss, [u, x1, tgt], [weight("w_down", u), gpo],
                                        [(d, F32), (d, BF16)], [((1, LANES), F32), ((1, d), F32)], tile=512)

    (d_up,) = _mm(d_dn, weight("w_down", u), "nt", (BF16,), "mlp_down_dx", extra=[up],
                  epilogue=lambda p, a: (p * (2.0 * jnp.maximum(a.astype(F32), 0.0)),))
    dw_down = _mm(u, d_dn, "tn", BF16, "mlp_down_dw", tm=512)
    dw_up = _mm(h2, d_up, "tn", BF16, "mlp_up_dw")
    sent_mlp = on_grads(dict(w_down=dw_down, w_up=dw_up))

    def mlp_up_dx(dup, xv, zv, dx1, w, a, b):
        _, vjp = jax.vjp(_post_mix, xv, zv, a, b)
        dx, dz, da, db = vjp((dx1, _raw_dot(dup, w, "nt")))
        return (dx, dz), (da, db)

    dx_res, d_z, d_gpm, d_gpl = _rowwise("mlp_up_dx", mlp_up_dx, [d_up, x, z, dx1_y], [weight("w_up", h2), gpm, gpl],
                                         [(d, F32), (d, BF16)], [((1, d), F32), ((1, d), F32)], tile=512, deps=dep(sent_mlp))
    dw_o = _mm(merged, d_z, "tn", BF16, "proj_o_dw")

    def proj_o_dx(dz, ga, gb, ya, yb, w):
        return jax.vjp(_merge, ga, gb, ya, yb)[1](_raw_dot(dz, w, "nt")), ()

    d_ga, d_gb, d_ya, d_yb = _rowwise("proj_o_dx", proj_o_dx, [d_z, gate_a, gate_b, y_a, y_b], [weight("w_o", merged)],
                                      [(d, F32), (d, F32), (d, BF16), (d, BF16)], tile=512)
    dw_pa = _mm(ya_in, d_ya, "tn", BF16, "proj_a_dw")
    dw_pb = _mm(h_b, d_yb, "tn", BF16, "proj_b_dw")
    sent_mix = on_grads(dict(w_o=dw_o, w_pa=dw_pa, w_pb=dw_pb))

    def proj_b_dx(dyb, a, b, c_, w, g, s):
        ct = _raw_dot(dyb, w, "nt")
        parts = []
        for hs in _head_slices(HEAD_W):
            _, vjp = jax.vjp(_ml_out, a[:, hs], b[:, hs], c_[:, hs], g[:, hs], s[:, hs])
            parts.append(vjp(ct[:, hs]))
        cat = lambda i: jnp.concatenate([p[i] for p in parts], axis=1)
        return (cat(0), cat(1), cat(2)), (cat(3), cat(4))

    d_hc, d_opre, d_xc, d_gml, d_skip = _rowwise("proj_b_dx", proj_b_dx, [d_yb, hc, o_pre, xc],
                                                 [weight("w_pb", hc), g_ml, skip], [(ml_w, F32)] * 3, [((1, ml_w), F32)] * 2,
                                                 tile=512, deps=dep(sent_mix))
    d_qm, d_km, d_vm, d_li, d_lf = _ml_bwd(q_m, k_m, v_m, li, lf, c_prev, n_prev, m_prev, d_hc)
    d_gl = jnp.concatenate([_gate_cols(d_li), _gate_cols(d_lf), jnp.zeros((t, LANES - 2 * HEADS), F32)], axis=1)
    pre_grads = _ml_pre_bwd(x_m, x_pad, pre_params, [d_xc, d_qm, d_km, d_vm, d_gl])
    d_xm = pre_grads[0]
    d_cw = jnp.concatenate(pre_grads[1:5], axis=0)
    d_cb = pre_grads[5]
    d_wq, d_wk, d_wv = _blockdiag_blocks(pre_grads[6:9])
    d_wif = jnp.concatenate(pre_grads[9:12], axis=0)[:, 0:2 * HEADS]
    d_bif = pre_grads[12][:, 0:2 * HEADS]

    def proj_a_dx(dya, o, g, w, n_):
        ct = _raw_dot(dya, w, "nt")
        parts = []
        for hs in _head_slices(HEAD_W):
            _, vjp = jax.vjp(_gla_out, o[:, hs], g[:, hs], n_)
            parts.append(vjp(ct[:, hs]))
        cat = lambda i: jnp.concatenate([p[i] for p in parts], axis=1)
        return (cat(0), cat(1)), (sum(p[2] for p in parts),)

    d_o, d_g_a, d_gn = _rowwise("proj_a_dx", proj_a_dx, [d_ya, o_gla, g_a], [weight("w_pa", o_gla), gn], [(ml_w, F32)] * 2,
                                [((1, HEAD_W), F32)], tile=512)
    dq_hm, dk_hm, d_va, dla_hm = _gla_bwd(q_hm, k_hm, v_a, la_hm, s_prev, d_o)

    def decay_bwd(al, ct, w, b):
        _, vjp = jax.vjp(_log_decay, al, w, b)
        dal, dw, db = vjp(ct)
        return (dal,), (dw, db)

    d_alow_p, d_wa_p, d_ba = _rowwise("gla_decay_bwd", decay_bwd, [a_low_p, _from_hm(dla_hm)], [w_a_up_p, b_a_up],
                                      [(LANES, F32)], [(w_a_up_p.shape, F32), (b_a_up.shape, F32)])
    d_proj = jnp.concatenate([_from_hm(dq_hm), _from_hm(dk_hm), d_va, d_g_a, d_alow_p[:, 0:LOWRANK], d_xm, d_opre, d_ga, d_gb],
                             axis=1).astype(BF16)
    small = dict(w_a_up=d_wa_p[0:LOWRANK], b_a_up=d_ba, g_gla_norm=d_gn, conv_w=d_cw, conv_b=d_cb,
                 w_q_ml=d_wq, w_k_ml=d_wk, w_v_ml=d_wv, w_if=d_wif, b_if=d_bif, ml_skip=d_skip, g_ml_norm=d_gml,
                 g_post_mix=d_gpm, g_pre_mlp=d_gpl, g_post_mlp=d_gpo)
    sent_small = on_small(small, loss)
    dw_in = _mm(h, d_proj, "tn", F32, "proj_in_dw", tm=512, tk=1024, deps=dep(sent_small))
    sent_in = on_grads(dict(w_in=dw_in))

    def proj_in_dx(dp, xv, dres, w, g):
        _, vjp = jax.vjp(_rms, xv, g)
        dx, dg = vjp(_raw_dot(dp, w, "nt"))
        return (dx + dres,), (dg,)

    grad_x, d_g1 = _rowwise("proj_in_dx", proj_in_dx, [d_proj, x, dx_res], [w_in, g1], [(d, F32)], [((1, d), F32)],
                            deps=dep(sent_in))
    return grad_x, on_small(dict(g_pre_mix=d_g1), None)


BIG = ("w_in", "w_pa", "w_pb", "w_o", "w_up", "w_down")
BIG_COL_SHARDED = ("w_in", "w_pa", "w_pb", "w_up")
SMALL_SHARDED = ("w_a_up", "conv_w", "w_if")
SMALL = ("g_pre_mix", "w_a_up", "b_a_up", "g_gla_norm", "conv_w", "conv_b", "w_q_ml", "w_k_ml", "w_v_ml", "w_if", "b_if",
         "ml_skip", "g_ml_norm", "g_post_mix", "g_pre_mlp", "g_post_mlp")
WEIGHTS = ("g_pre_mix", "w_in", "w_a_up", "b_a_up", "g_gla_norm", "conv_w", "conv_b", "w_q_ml", "w_k_ml", "w_v_ml", "w_if", "b_if",
           "ml_skip", "g_ml_norm", "w_pa", "w_pb", "w_o", "g_post_mix", "g_pre_mlp", "w_up", "w_down", "g_post_mlp")


def kernel(x, g_pre_mix, w_in, w_a_up, b_a_up, g_gla_norm, conv_w, conv_b, w_q_ml, w_k_ml, w_v_ml, w_if, b_if, ml_skip, g_ml_norm, w_pa, w_pb, w_o, g_post_mix, g_pre_mlp, w_up, w_down, g_post_mlp, loss_target, m_g_pre_mix, m_w_in, m_w_a_up, m_b_a_up, m_g_gla_norm, m_conv_w, m_conv_b, m_w_q_ml, m_w_k_ml, m_w_v_ml, m_w_if, m_b_if, m_ml_skip, m_g_ml_norm, m_w_pa, m_w_pb, m_w_o, m_g_post_mix, m_g_pre_mlp, m_w_up, m_w_down, m_g_post_mlp, v_g_pre_mix, v_w_in, v_w_a_up, v_b_a_up, v_g_gla_norm, v_conv_w, v_conv_b, v_w_q_ml, v_w_k_ml, v_w_v_ml, v_w_if, v_b_if, v_ml_skip, v_g_ml_norm, v_w_pa, v_w_pb, v_w_o, v_g_post_mix, v_g_pre_mlp, v_w_up, v_w_down, v_g_post_mlp):
    args = dict(locals())
    w = {n: args[n][0] for n in WEIGHTS}
    m = {n: args["m_" + n][0] for n in WEIGHTS}
    v = {n: args["v_" + n][0] for n in WEIGHTS}

    me_lin = _lin(_me())
    me_idx = jnp.reshape(me_lin, (1,)).astype(jnp.int32)

    def full_weight(n, g):
        return _from_col_blocks(g) if n in BIG_COL_SHARDED else g.reshape(-1, g.shape[-1])

    def grad_parts(n, g):
        return (_col_blocks(g) if n in BIG_COL_SHARDED else g.reshape(N_DEV, -1, g.shape[-1])).astype(BF16)

    sharded_names = tuple(SMALL_SHARDED)
    small_w_state, small_w_token = _copies_start("gather", [_small_view(n, w[n]) for n in sharded_names],
                                                 "allgather_start_small_weights")
    ready = {"w_in": full_weight("w_in", _allgather_big([w["w_in"].astype(BF16)], "allgather_w_in", [small_w_token])[0])}
    pending = {}

    def prefetch(group, after):
        state, token = _copies_start("gather", [w[n].astype(BF16) for n in group], "allgather_start_" + group[0], after)
        for n in group:
            pending[n] = (group, state)
        return token

    def weight(n, after):
        if n not in ready:
            group, state = pending[n]
            shards, lands = _copies_wait(state, after, "allgather_wait_" + group[0])
            for gn, shard, land in zip(group, shards, lands):
                ready[gn] = full_weight(gn, lax.dynamic_update_slice(land, shard[None], (me_lin, 0, 0)))
        return ready[n]

    small_w_own, small_w_lands = _copies_wait(small_w_state, ready["w_in"], "allgather_wait_small_weights")
    ws = {n: (w[n].reshape(1, -1) if w[n].ndim == 1 else w[n]) for n in SMALL if n not in SMALL_SHARDED}
    for n, own, land in zip(sharded_names, small_w_own, small_w_lands):
        ws[n] = _small_unshard(n, lax.dynamic_update_slice(land, own[None], (me_lin, 0, 0)))

    sent = []

    def on_grads(grads):
        names = tuple(grads)
        state, token = _copies_start("exchange", [grad_parts(n, grads[n]) for n in names], "exchange_start_" + names[0])
        sent.append((names, state))
        return token

    small_sent = []

    def on_small(small, loss):
        names = tuple(small)
        kinds = ["exchange" if n in SMALL_SHARDED else "gather" for n in names]
        srcs = [_small_shards(n, small[n]) if n in SMALL_SHARDED else _small_view(n, small[n]) for n in names]
        extra = [] if loss is None else [loss]
        state, token = _copies_start(kinds + ["gather"] * len(extra), srcs + extra, "allgather_start_small_" + names[0])
        small_sent.append((names, kinds, state))
        return token

    grad_x, last_token = _local_step(x[0], loss_target[0], weight, ws, prefetch, on_grads, on_small)

    out = {}

    def finish(names, state, after):
        parts, lands = _copies_wait(state, after, "exchange_wait_" + names[0])
        for n, part, land in zip(names, parts, lands):
            out[n] = _sum_adamw(land, part, me_idx, w[n], m[n], v[n], "adamw_" + n)

    def finish_small(names, kinds, state, after):
        own, lands = _copies_wait(state, after, "allgather_wait_small_" + names[0])
        k = len(names)
        upd = _small_update("adamw_small_" + names[0], me_idx, kinds, lands[:k], own[:k],
                            *[[_small_view(n, d[n]) for n in names] for d in (w, m, v)], sums=list(zip(lands[k:], own[k:])))
        for i, n in enumerate(names):
            out[n] = tuple(_small_unview(n, a, w[n].shape) for a in upd[4 * i:4 * i + 4])
        return upd[4 * k:]

    (loss_sum,) = finish_small(*small_sent[0], [grad_x, last_token])
    for names, state in sent[:-1]:
        finish(names, state, [grad_x, last_token])
    finish(*sent[-1], [loss_sum] + [out[n][1] for n in BIG if n in out])
    finish_small(*small_sent[1], [out["w_in"][1]])

    shaped = lambda a, n: a.reshape(args[n].shape)
    return (loss_sum[0, 0], grad_x[None],
            *[shaped(out[n][0], n) for n in WEIGHTS], *[shaped(out[n][1], n) for n in WEIGHTS],
            *[shaped(out[n][2], n) for n in WEIGHTS], *[shaped(out[n][3], n) for n in WEIGHTS])
```

```python
import functools

import jax
import jax.numpy as jnp
from jax import lax
from jax.experimental import pallas as pl
from jax.experimental.pallas import tpu as pltpu

F32 = jnp.float32
BF16 = jnp.bfloat16
MESH = pl.DeviceIdType.MESH

N_DEV = 8
EPS = 1e-6
CHUNK = 64
HEADS = 4
GLA_DK = 64
HEAD_W = 128
GLA_GATE_NORM = 16.0
LOWRANK = 16
CONV_K = 4
QKV_BLOCK = 4
LANES = 128
HALO = 8
IN_SPLITS = (256, 256, 512, 512, 16, 512, 512, 1024, 1024)

ADAM_LR = 0.001
ADAM_B1 = 0.9
ADAM_B2 = 0.999
ADAM_EPS = 1e-08
ADAM_WD = 0.01
ADAM_STEP = 10

VMEM_LIMIT = 56 * 1024 * 1024


def _cparams(*sem):
    return pltpu.CompilerParams(dimension_semantics=sem, vmem_limit_bytes=VMEM_LIMIT)


def _dims(mode, ndim):
    contract = {"nn": ((ndim - 1,), (ndim - 2,)), "nt": ((ndim - 1,), (ndim - 1,)), "tn": ((ndim - 2,), (ndim - 2,))}[mode]
    return contract, (((0,), (0,)) if ndim == 3 else ((), ()))


def _raw_dot(a, b, mode):
    return lax.dot_general(a.astype(BF16), b.astype(BF16), _dims(mode, a.ndim), preferred_element_type=F32)


@functools.partial(jax.custom_vjp, nondiff_argnums=(2,))
def _bdot(a, b, mode):
    return _raw_dot(a, b, mode)


def _bdot_fwd(a, b, mode):
    return _raw_dot(a, b, mode), (a, b)


def _bdot_bwd(mode, res, ct):
    a, b = res
    if mode == "nn":
        da, db = _raw_dot(ct, b, "nt"), _raw_dot(a, ct, "tn")
    elif mode == "nt":
        da, db = _raw_dot(ct, b, "nn"), _raw_dot(ct, a, "tn")
    else:
        da, db = _raw_dot(b, ct, "nt"), _raw_dot(a, ct, "nn")
    return da.astype(a.dtype), db.astype(b.dtype)


_bdot.defvjp(_bdot_fwd, _bdot_bwd)


def _split3(x):
    hi = x.astype(BF16)
    r1 = x - hi.astype(F32)
    mid = r1.astype(BF16)
    return hi, mid, (r1 - mid.astype(F32)).astype(BF16)


def _split_dot(tri, x):
    if x.ndim == 3:
        tri = jnp.broadcast_to(tri, (x.shape[0], *tri.shape))
    return sum(lax.dot_general(tri, t, _dims("nn", x.ndim), preferred_element_type=F32) for t in _split3(x))


def _tri(n, lower):
    r = lax.broadcasted_iota(jnp.int32, (n, n), 0)
    c = lax.broadcasted_iota(jnp.int32, (n, n), 1)
    return ((c <= r) if lower else (c >= r)).astype(BF16)


@jax.custom_vjp
def _cumsum_rows(x):
    return _split_dot(_tri(x.shape[-2], True), x)


def _cumsum_rows_fwd(x):
    return _cumsum_rows(x), None


def _cumsum_rows_bwd(_, ct):
    return (_split_dot(_tri(ct.shape[-2], False), ct),)


_cumsum_rows.defvjp(_cumsum_rows_fwd, _cumsum_rows_bwd)


def _abs(x):
    return jnp.where(x >= 0, x, -x)


def _sigmoid(x):
    return lax.logistic(x)


def _log_sigmoid(x):
    return jnp.minimum(x, 0.0) - jnp.log(1.0 + jnp.exp(-_abs(x)))


def _rms(x, g):
    return x * lax.rsqrt(jnp.mean(x * x, axis=-1, keepdims=True) + EPS) * g


def _head_slices(w):
    return [slice(h * w, (h + 1) * w) for h in range(HEADS)]


def _heads(ref):
    return jnp.stack([ref[:, hs] for hs in _head_slices(HEAD_W)])


def _put_heads(ref, val):
    for h, hs in enumerate(_head_slices(HEAD_W)):
        ref[:, hs] = val[h]


def _tile(dim, want):
    if dim <= want or dim % LANES:
        return dim
    t = want
    while dim % t:
        t -= LANES
    return t


def _mm(a, b, mode, out_dtype, name, tm=1024, tn=1024, tk=4096, epilogue=None, extra=(), deps=()):
    if mode == "nn":
        (m, k), (k2, n) = a.shape, b.shape
    elif mode == "nt":
        (m, k), (n, k2) = a.shape, b.shape
    else:
        (k, m), (k2, n) = a.shape, b.shape
    assert k == k2, (name, a.shape, b.shape)
    tm, tn, tk = _tile(m, tm), _tile(n, tn), _tile(k, tk)
    nk = k // tk
    out_dtypes = out_dtype if epilogue else (out_dtype,)
    assert nk == 1 or (out_dtype == F32 and not epilogue), name
    n_in = 2 + len(extra)

    def body(*refs):
        p = _raw_dot(refs[0][...], refs[1][...], mode)
        if nk > 1:
            _accumulate(pl.program_id(2), [refs[n_in + len(deps)]], [p])
            return
        outs = epilogue(p, *[r[...] for r in refs[2:n_in]]) if epilogue else (p,)
        for ref, val in zip(refs[n_in + len(deps):], outs):
            ref[...] = val.astype(ref.dtype)

    a_spec = pl.BlockSpec((tk, tm), lambda i, j, kk: (kk, i)) if mode == "tn" else pl.BlockSpec((tm, tk), lambda i, j, kk: (i, kk))
    b_spec = pl.BlockSpec((tn, tk), lambda i, j, kk: (j, kk)) if mode == "nt" else pl.BlockSpec((tk, tn), lambda i, j, kk: (kk, j))
    o_spec = pl.BlockSpec((tm, tn), lambda i, j, kk: (i, j))
    res = pl.pallas_call(
        body, name=name, grid=(m // tm, n // tn, nk),
        in_specs=[a_spec, b_spec] + [o_spec] * len(extra) + [ANY] * len(deps), out_specs=[o_spec] * len(out_dtypes),
        out_shape=[jax.ShapeDtypeStruct((m, n), dt) for dt in out_dtypes],
        compiler_params=_cparams("parallel", "parallel", "arbitrary"),
    )(a, b, *extra, *deps)
    return res if epilogue else res[0]


def _rowwise(name, fn, rows, params, out_rows, out_accs=(), tile=256, deps=()):
    t = rows[0].shape[0]
    r = min(tile, t)
    assert t % r == 0
    n_in, n_or = len(rows) + len(params), len(out_rows)
    n_all = n_in + len(deps)
    params = list(params) + list(deps)

    def body(*refs):
        vals = [ref[...] for ref in refs[:n_in]]
        outs = refs[n_all:]
        ro, ao = fn(*vals)
        for ref, v in zip(outs[:n_or], ro):
            ref[...] = v.astype(ref.dtype)
        if out_accs:
            _accumulate(pl.program_id(0), outs[n_or:], ao)

    def full(shape):
        return pl.BlockSpec(shape, lambda i, nd=len(shape): (0,) * nd)

    return pl.pallas_call(
        body, name=name, grid=(t // r,),
        in_specs=[pl.BlockSpec((r, a.shape[1]), lambda i: (i, 0)) for a in rows] + [full(p.shape) for p in params],
        out_specs=[pl.BlockSpec((r, w), lambda i: (i, 0)) for w, _ in out_rows] + [full(s) for s, _ in out_accs],
        out_shape=[jax.ShapeDtypeStruct((t, w), dt) for w, dt in out_rows] + [jax.ShapeDtypeStruct(s, dt) for s, dt in out_accs],
        compiler_params=_cparams("arbitrary"),
    )(*rows, *params)


def _accumulate(step, refs, vals):
    for ref, v in zip(refs, vals):
        @pl.when(step == 0)
        def _(ref=ref, v=v):
            ref[...] = v.astype(ref.dtype)

        @pl.when(step > 0)
        def _(ref=ref, v=v):
            ref[...] += v.astype(ref.dtype)


def _gla_chunk(q, k, v, la, st):
    c = q.shape[-2]
    row = lax.broadcasted_iota(jnp.int32, (c, c), 0)
    col = lax.broadcasted_iota(jnp.int32, (c, c), 1)
    cum = _cumsum_rows(la)
    cl = jnp.sum(la, axis=-2, keepdims=True)
    ep = jnp.exp(cum)
    en = jnp.exp(-cum)
    qs = q * (GLA_DK ** -0.5)
    qp = qs * ep
    a_f = _bdot(qp, k * en, "nt")
    a_b = _bdot(qs * en, k * ep, "nt")
    sc = jnp.where(row >= col, a_f, a_b)
    o = _bdot(sc, v, "nn") + _bdot(qp, st, "nt")
    kd = k * jnp.exp(cl - cum)
    st_new = st * jnp.exp(cl) + _bdot(v, kd, "tn")
    return o, st_new


def _gla_specs(nc, rev):
    def ch(n):
        return (nc - 1 - n) if rev else n
    hm = pl.BlockSpec((HEADS, CHUNK, GLA_DK), lambda n: (0, ch(n), 0))
    tm = pl.BlockSpec((CHUNK, HEADS * HEAD_W), lambda n: (ch(n), 0))
    st = pl.BlockSpec((HEADS, None, HEAD_W, GLA_DK), lambda n: (0, ch(n), 0, 0))
    return hm, tm, st


def _gla_fwd(q, k, v, la):
    t = v.shape[0]
    nc = t // CHUNK
    hm, tm, st = _gla_specs(nc, False)

    def body(q_ref, k_ref, v_ref, la_ref, o_ref, sp_ref, st_ref):
        @pl.when(pl.program_id(0) == 0)
        def _():
            st_ref[...] = jnp.zeros_like(st_ref)

        s = st_ref[...]
        sp_ref[...] = s
        o, s_new = _gla_chunk(q_ref[...], k_ref[...], _heads(v_ref), la_ref[...], s)
        _put_heads(o_ref, o)
        st_ref[...] = s_new

    return pl.pallas_call(
        body, name="gla_fwd", grid=(nc,),
        in_specs=[hm, hm, tm, hm], out_specs=[tm, st],
        out_shape=[jax.ShapeDtypeStruct((t, HEADS * HEAD_W), F32), jax.ShapeDtypeStruct((HEADS, nc, HEAD_W, GLA_DK), F32)],
        scratch_shapes=[pltpu.VMEM((HEADS, HEAD_W, GLA_DK), F32)],
        compiler_params=_cparams("arbitrary"),
    )(q, k, v, la)


def _gla_bwd(q, k, v, la, sp, do):
    t = v.shape[0]
    nc = t // CHUNK
    hm, tm, st = _gla_specs(nc, True)

    def body(q_ref, k_ref, v_ref, la_ref, sp_ref, do_ref, dq_ref, dk_ref, dv_ref, dla_ref, ds_ref):
        @pl.when(pl.program_id(0) == 0)
        def _():
            ds_ref[...] = jnp.zeros_like(ds_ref)

        _, vjp = jax.vjp(_gla_chunk, q_ref[...], k_ref[...], _heads(v_ref), la_ref[...], sp_ref[...])
        dq, dk, dv, dla, ds = vjp((_heads(do_ref), ds_ref[...]))
        dq_ref[...] = dq
        dk_ref[...] = dk
        _put_heads(dv_ref, dv)
        dla_ref[...] = dla
        ds_ref[...] = ds

    hm_shape = jax.ShapeDtypeStruct((HEADS, t, GLA_DK), F32)
    return pl.pallas_call(
        body, name="gla_bwd", grid=(nc,),
        in_specs=[hm, hm, tm, hm, st, tm], out_specs=[hm, hm, tm, hm],
        out_shape=[hm_shape, hm_shape, jax.ShapeDtypeStruct((t, HEADS * HEAD_W), F32), hm_shape],
        scratch_shapes=[pltpu.VMEM((HEADS, HEAD_W, GLA_DK), F32)],
        compiler_params=_cparams("arbitrary"),
    )(q, k, v, la, sp, do)


def _ml_chunk(q, k, v, li_r, lf_r, cm, nv, m):
    c = q.shape[-2]
    row = lax.broadcasted_iota(jnp.int32, (c, c), 0)
    col = lax.broadcasted_iota(jnp.int32, (c, c), 1)
    eye = (row == col).astype(F32)
    li_c = jnp.sum(eye * li_r, axis=-1, keepdims=True)
    lf_c = jnp.sum(eye * lf_r, axis=-1, keepdims=True)
    fc_c = jnp.sum((col <= row).astype(F32) * lf_r, axis=-1, keepdims=True)
    fc_r = jnp.sum((row <= col).astype(F32) * lf_c, axis=-2, keepdims=True)
    f_last = jnp.sum(lf_r, axis=-1, keepdims=True)
    kc = k * (HEAD_W ** -0.5)
    a_c = f_last - fc_c + li_c
    m_loc = jnp.max(a_c, axis=-2, keepdims=True)
    kw = kc * jnp.exp(a_c - m_loc)
    c_chunk = _bdot(kw, v, "tn")
    n_chunk = jnp.sum(kw, axis=-2, keepdims=True)
    m_new = jnp.maximum(f_last + m, m_loc)
    sp = jnp.exp(f_last + m - m_new)
    sl = jnp.exp(m_loc - m_new)
    cm_new = sp * cm + sl * c_chunk
    nv_new = sp * nv + sl * n_chunk
    log_d = li_r - _abs(fc_c - fc_r)
    g_inter = fc_c + m
    m_t = jnp.maximum(g_inter, jnp.max(log_d, axis=-1, keepdims=True))
    s = _bdot(q, kc, "nt") * jnp.exp(log_d - m_t)
    sc = jnp.exp(g_inter - m_t)
    num = _bdot(s, v, "nn") + sc * _bdot(q, cm, "nn")
    den = jnp.sum(s, axis=-1, keepdims=True) + sc * jnp.sum(q * nv, axis=-1, keepdims=True)
    den = jnp.maximum(_abs(den), jnp.exp(-m_t))
    return num / den, cm_new, nv_new, m_new


def _ml_specs(nc, rev):
    def ch(n):
        return (nc - 1 - n) if rev else n
    tm = pl.BlockSpec((CHUNK, HEADS * HEAD_W), lambda n: (ch(n), 0))
    gate = pl.BlockSpec((HEADS, None, 1, CHUNK), lambda n: (0, ch(n), 0, 0))
    cm = pl.BlockSpec((HEADS, None, HEAD_W, HEAD_W), lambda n: (0, ch(n), 0, 0))
    vec = pl.BlockSpec((HEADS, None, 1, HEAD_W), lambda n: (0, ch(n), 0, 0))
    return tm, gate, cm, vec


_ML_STATE = [pltpu.VMEM((HEADS, HEAD_W, HEAD_W), F32), pltpu.VMEM((HEADS, 1, HEAD_W), F32), pltpu.VMEM((HEADS, 1, HEAD_W), F32)]


def _ml_fwd(q, k, v, li, lf):
    t = q.shape[0]
    nc = t // CHUNK
    tm, gate, cm, vec = _ml_specs(nc, False)

    def body(q_ref, k_ref, v_ref, li_ref, lf_ref, hc_ref, cp_ref, np_ref, mp_ref, c_ref, n_ref, m_ref):
        @pl.when(pl.program_id(0) == 0)
        def _():
            c_ref[...] = jnp.zeros_like(c_ref)
            n_ref[...] = jnp.zeros_like(n_ref)
            m_ref[...] = jnp.zeros_like(m_ref)

        c0, n0, m0 = c_ref[...], n_ref[...], m_ref[...]
        cp_ref[...] = c0
        np_ref[...] = n0
        mp_ref[...] = m0
        hc, c1, n1, m1 = _ml_chunk(_heads(q_ref), _heads(k_ref), _heads(v_ref), li_ref[...], lf_ref[...],
                                   c0, n0, m0[:, :, 0:1])
        _put_heads(hc_ref, hc)
        c_ref[...] = c1
        n_ref[...] = n1
        m_ref[...] = jnp.broadcast_to(m1, m_ref.shape)

    return pl.pallas_call(
        body, name="mlstm_fwd", grid=(nc,),
        in_specs=[tm, tm, tm, gate, gate], out_specs=[tm, cm, vec, vec],
        out_shape=[jax.ShapeDtypeStruct((t, HEADS * HEAD_W), F32), jax.ShapeDtypeStruct((HEADS, nc, HEAD_W, HEAD_W), F32),
                   jax.ShapeDtypeStruct((HEADS, nc, 1, HEAD_W), F32), jax.ShapeDtypeStruct((HEADS, nc, 1, HEAD_W), F32)],
        scratch_shapes=_ML_STATE,
        compiler_params=_cparams("arbitrary"),
    )(q, k, v, li, lf)


def _ml_bwd(q, k, v, li, lf, cp, npv, mp, dhc):
    t = q.shape[0]
    nc = t // CHUNK
    tm, gate, cm, vec = _ml_specs(nc, True)

    def body(q_ref, k_ref, v_ref, li_ref, lf_ref, cp_ref, np_ref, mp_ref, dhc_ref,
             dq_ref, dk_ref, dv_ref, dli_ref, dlf_ref, dc_ref, dn_ref, dm_ref):
        @pl.when(pl.program_id(0) == 0)
        def _():
            dc_ref[...] = jnp.zeros_like(dc_ref)
            dn_ref[...] = jnp.zeros_like(dn_ref)
            dm_ref[...] = jnp.zeros_like(dm_ref)

        _, vjp = jax.vjp(_ml_chunk, _heads(q_ref), _heads(k_ref), _heads(v_ref), li_ref[...], lf_ref[...],
                         cp_ref[...], np_ref[...], mp_ref[...][:, :, 0:1])
        dq, dk, dv, dli, dlf, dc, dn, dm = vjp((_heads(dhc_ref), dc_ref[...], dn_ref[...], dm_ref[...][:, :, 0:1]))
        _put_heads(dq_ref, dq)
        _put_heads(dk_ref, dk)
        _put_heads(dv_ref, dv)
        dli_ref[...] = dli
        dlf_ref[...] = dlf
        dc_ref[...] = dc
        dn_ref[...] = dn
        dm_ref[...] = jnp.broadcast_to(dm, dm_ref.shape)

    tm_shape = jax.ShapeDtypeStruct((t, HEADS * HEAD_W), F32)
    gate_shape = jax.ShapeDtypeStruct((HEADS, nc, 1, CHUNK), F32)
    return pl.pallas_call(
        body, name="mlstm_bwd", grid=(nc,),
        in_specs=[tm, tm, tm, gate, gate, cm, vec, vec, tm], out_specs=[tm, tm, tm, gate, gate],
        out_shape=[tm_shape, tm_shape, tm_shape, gate_shape, gate_shape],
        scratch_shapes=_ML_STATE,
        compiler_params=_cparams("arbitrary"),
    )(q, k, v, li, lf, cp, npv, mp, dhc)


def _ml_pre(s0, s1, s2, s3, cw0, cw1, cw2, cw3, cb, wq, wk, wv, wiq, wik, wiv, bif):
    pre = cb + cw0 * s0 + cw1 * s1 + cw2 * s2 + cw3 * s3
    xc = pre * _sigmoid(pre)
    q = _bdot(xc, wq, "nn")
    k = _bdot(xc, wk, "nn")
    v = _bdot(s3, wv, "nn")
    gates = _bdot(q, wiq, "nn") + _bdot(k, wik, "nn") + _bdot(v, wiv, "nn") + bif
    lane = lax.broadcasted_iota(jnp.int32, gates.shape, 1)
    gl = jnp.where(lane < HEADS, gates, _log_sigmoid(gates))
    return xc, q, k, v, gl


def _delayed(xs_ref, x_ref, halo_ref, r):
    xs_ref[0:HALO, :] = halo_ref[...]
    xs_ref[HALO:HALO + r, :] = x_ref[...]
    return [xs_ref[pl.ds(HALO - (CONV_K - 1) + j, r), :] for j in range(CONV_K)]


def _full_spec(shape):
    return pl.BlockSpec(shape, lambda i, nd=len(shape): (0,) * nd)


def _ml_pre_fwd(x_m, x_pad, params, tile=256):
    t, w = x_m.shape
    r = min(tile, t)

    def body(*refs):
        x_ref, halo_ref = refs[:2]
        p = [ref[...] for ref in refs[2:2 + len(params)]]
        outs = refs[2 + len(params):-1]
        res = _ml_pre(*_delayed(refs[-1], x_ref, halo_ref, r), *p)
        for ref, val in zip(outs, res):
            ref[...] = val

    row = pl.BlockSpec((r, w), lambda i: (i, 0))
    return pl.pallas_call(
        body, name="ml_pre_fwd", grid=(t // r,),
        in_specs=[row, pl.BlockSpec((HALO, w), lambda i: (i * (r // HALO), 0))] + [_full_spec(p.shape) for p in params],
        out_specs=[row] * 4 + [pl.BlockSpec((r, LANES), lambda i: (i, 0))],
        out_shape=[jax.ShapeDtypeStruct((t, w), F32)] * 4 + [jax.ShapeDtypeStruct((t, LANES), F32)],
        scratch_shapes=[pltpu.VMEM((r + HALO, w), F32)],
        compiler_params=_cparams("arbitrary"),
    )(x_m, x_pad, *params)


def _ml_pre_bwd(x_m, x_pad, params, cts, tile=256):
    t, w = x_m.shape
    r = min(tile, t)
    nt = t // r
    n_p = len(params)

    def body(*refs):
        x_ref, halo_ref = refs[:2]
        p = [ref[...] for ref in refs[2:2 + n_p]]
        ct = [ref[...] for ref in refs[2 + n_p:7 + n_p]]
        dx_ref = refs[7 + n_p]
        dp_refs = refs[8 + n_p:8 + 2 * n_p]
        xs_ref, ds_ref, carry_ref = refs[8 + 2 * n_p:]
        step = pl.program_id(0)

        @pl.when(step == 0)
        def _():
            ds_ref[...] = jnp.zeros_like(ds_ref)
            carry_ref[...] = jnp.zeros_like(carry_ref)

        _, vjp = jax.vjp(_ml_pre, *_delayed(xs_ref, x_ref, halo_ref, r), *p)
        grads = vjp(tuple(ct))
        for j in range(CONV_K):
            ds_ref[j, HALO:HALO + r, :] = grads[j]
        lead = HALO + CONV_K - 1
        d_tile = sum(ds_ref[j, pl.ds(lead - j, r), :] for j in range(CONV_K))
        d_halo = sum(ds_ref[j, pl.ds(CONV_K - 1 - j, HALO), :] for j in range(CONV_K))
        dx_ref[...] = d_tile
        dx_ref[r - HALO:r, :] += carry_ref[...]
        carry_ref[...] = d_halo
        _accumulate(step, dp_refs, grads[CONV_K:])

    row = pl.BlockSpec((r, w), lambda i: (nt - 1 - i, 0))
    return pl.pallas_call(
        body, name="ml_pre_bwd", grid=(nt,),
        in_specs=[row, pl.BlockSpec((HALO, w), lambda i: ((nt - 1 - i) * (r // HALO), 0))] + [_full_spec(p.shape) for p in params]
        + [row] * 4 + [pl.BlockSpec((r, LANES), lambda i: (nt - 1 - i, 0))],
        out_specs=[row] + [_full_spec(p.shape) for p in params],
        out_shape=[jax.ShapeDtypeStruct((t, w), F32)] + [jax.ShapeDtypeStruct(p.shape, F32) for p in params],
        scratch_shapes=[pltpu.VMEM((r + HALO, w), F32), pltpu.VMEM((CONV_K, r + 2 * HALO, w), F32), pltpu.VMEM((HALO, w), F32)],
        compiler_params=_cparams("arbitrary"),
    )(x_m, x_pad, *params, *cts)


def _per_head(fn, row_vals, head_params, shared_params=()):
    return [fn(*[a[:, hs] for a in row_vals], *[p[:, hs] for p in head_params], *shared_params) for hs in _head_slices(HEAD_W)]


def _gla_out(o, g, gn):
    return _rms(o, gn) * (g * _sigmoid(g))


def _ml_out(hc, op, xc, g, sk):
    hcell = hc * _sigmoid(op)
    mu = jnp.mean(hcell, axis=-1, keepdims=True)
    d = hcell - mu
    var = jnp.mean(d * d, axis=-1, keepdims=True)
    return d * lax.rsqrt(var + EPS) * g + sk * xc


def _log_decay(al, w, b):
    return _log_sigmoid(_bdot(al, w, "nn") + b) * (1.0 / GLA_GATE_NORM)


def _merge(ga, gb, ya, yb):
    return _sigmoid(ga) * ya + _sigmoid(gb) * yb


def _post_mix(x, z, gpm, gpl):
    x1 = x + _rms(z, gpm)
    return x1, _rms(x1, gpl)


def _loss_rows(x1, dn, tgt, g):
    e = x1 + _rms(dn, g) - tgt
    return 0.5 * jnp.sum(jnp.mean(e * e, axis=-1, keepdims=True), axis=0, keepdims=True)


def _lin(p):
    return 4 * p[0] + 2 * p[1] + p[2]


def _me():
    return lax.axis_index("x"), lax.axis_index("y"), lax.axis_index("c")


def _flip(p, k):
    return tuple((1 - v) if (k >> (2 - i)) & 1 else v for i, v in enumerate(p))


ANY = pl.BlockSpec(memory_space=pl.ANY)


def _allgather_big(shards, name, deps=()):
    n = len(shards)
    n_in = n + len(deps)

    def body(*refs):
        ins, outs = refs[:n], refs[n_in:n_in + n]
        send_sems, recv_sems, local_sems = refs[n_in + n:]
        me = _me()
        x, y, c = me
        sib = (x, y, 1 - c)
        chips = [(1 - x, y), (x, 1 - y), (1 - x, 1 - y)]

        def cp(a, k, block, to, src=None):
            dst = outs[a].at[_lin(block)]
            return pltpu.make_async_remote_copy(src_ref=dst if src is None else src, dst_ref=dst,
                                                send_sem=send_sems.at[a * 7 + k], recv_sem=recv_sems.at[a * 7 + k],
                                                device_id=to, device_id_type=MESH)

        mine = [pltpu.make_async_copy(ins[a], outs[a].at[_lin(me)], local_sems.at[a]) for a in range(n)]
        for m in mine:
            m.start()
        first = []
        for a in range(n):
            first.append(cp(a, 0, me, sib, src=ins[a]))
            first += [cp(a, 1 + j, me, (*chip, c), src=ins[a]) for j, chip in enumerate(chips)]
        for f in first:
            f.start()
        passed = []
        for j, chip in enumerate(chips):
            for a in range(n):
                cp(a, 1 + j, (*chip, c), me).wait_recv()
                fwd = cp(a, 4 + j, (*chip, c), sib)
                fwd.start()
                passed.append(fwd)
        for a in range(n):
            cp(a, 0, sib, me).wait_recv()
            for j, chip in enumerate(chips):
                cp(a, 4 + j, (*chip, 1 - c), me).wait_recv()
        for f in first + passed:
            f.wait_send()
        for m in mine:
            m.wait()

    return pl.pallas_call(
        body, name=name,
        in_specs=[ANY] * n_in, out_specs=[ANY] * n,
        out_shape=[jax.ShapeDtypeStruct((N_DEV, *s.shape), s.dtype) for s in shards],
        scratch_shapes=[pltpu.SemaphoreType.DMA((7 * n,)), pltpu.SemaphoreType.DMA((7 * n,)), pltpu.SemaphoreType.DMA((n,))],
    )(*shards, *deps)


HBM = pl.BlockSpec(memory_space=pltpu.HBM)
SEM = pl.BlockSpec(memory_space=pltpu.SEMAPHORE)
DATAFLOW = pltpu.SideEffectType.DATAFLOW_SIDE_EFFECTING


SIBLING = 1
OTHER_CHIPS = (2, 4, 6)


def _peer_copies(kinds, srcs, lands, send_sems, recv_sems):
    me = _me()
    copies = []
    for a, (kind, src, land) in enumerate(zip(kinds, srcs, lands)):
        masks = {"gather": range(1, N_DEV), "exchange": range(1, N_DEV), "gather_chips": (SIBLING, *OTHER_CHIPS),
                 "gather_pass": OTHER_CHIPS}[kind]
        for k in masks:
            peer = _flip(me, k)
            if kind == "gather_pass":
                block = land.at[_lin(peer)]
                src_ref, dst_ref, target = block, block, _flip(me, SIBLING)
            else:
                src_ref, dst_ref, target = (src.at[_lin(peer)] if kind == "exchange" else src), land.at[_lin(me)], peer
            copies.append(pltpu.make_async_remote_copy(
                src_ref=src_ref, dst_ref=dst_ref, send_sem=send_sems.at[a * 7 + k - 1], recv_sem=recv_sems.at[a * 7 + k - 1],
                device_id=target, device_id_type=MESH))
    return copies


def _copies_start(kind, srcs, name, after=None, lands=None):
    n = len(srcs)
    extra = [] if after is None else [after]
    kind = [kind] * n if isinstance(kind, str) else list(kind)
    land_shapes = [(s.shape if k == "exchange" else (N_DEV, *s.shape)) for k, s in zip(kind, srcs)]
    lands = [lax.empty(ls, s.dtype) for ls, s in zip(land_shapes, srcs)] if lands is None else lands

    def body(*refs):
        sems = refs[2 * n + len(extra):]
        for cp in _peer_copies(kind, refs[:n], refs[n:2 * n], sems[0], sems[1]):
            cp.start()
        refs[-1][...] = jnp.zeros_like(refs[-1])

    def hbm(a):
        return pltpu.with_memory_space_constraint(a, pltpu.HBM)

    out = pl.pallas_call(
        body, name=name,
        out_shape=(pltpu.SemaphoreType.DMA((7 * n,)), pltpu.SemaphoreType.DMA((7 * n,)),
                   *[pltpu.HBM(s.shape, s.dtype) for s in srcs],
                   *[pltpu.HBM(ls, s.dtype) for ls, s in zip(land_shapes, srcs)],
                   jax.ShapeDtypeStruct((8, LANES), F32)),
        in_specs=[HBM] * (2 * n) + [ANY] * len(extra),
        out_specs=(SEM, SEM, *[HBM] * (2 * n), pl.BlockSpec(memory_space=pltpu.VMEM)),
        input_output_aliases={i: 2 + i for i in range(2 * n)},
        compiler_params=pltpu.CompilerParams(has_side_effects=DATAFLOW),
    )(*[hbm(s) for s in srcs], *[hbm(a) for a in lands], *extra)
    return (kind, n, out[:-1]), out[-1]


def _copies_wait(state, after, name):
    kind, n, (send_sems, recv_sems, *thru) = state
    after = list(after) if isinstance(after, (list, tuple)) else [after]

    def body(*refs):
        for cp in _peer_copies(kind, refs[:n], refs[n:2 * n], refs[2 * n], refs[2 * n + 1]):
            cp.wait_send()
            cp.wait_recv()

    out = pl.pallas_call(
        body, name=name,
        out_shape=tuple(pltpu.HBM(t.shape, t.dtype) for t in thru),
        in_specs=[HBM] * (2 * n) + [SEM, SEM] + [ANY] * len(after), out_specs=tuple([HBM] * (2 * n)),
        input_output_aliases={i: i for i in range(2 * n)},
        compiler_params=pltpu.CompilerParams(has_side_effects=DATAFLOW),
    )(*thru, send_sems, recv_sems, *after)
    return out[:n], out[n:]


def _adamw(w, g, m, v):
    m2 = ADAM_B1 * m + (1.0 - ADAM_B1) * g
    v2 = ADAM_B2 * v + (1.0 - ADAM_B2) * (g * g)
    m_hat = m2 / (1.0 - ADAM_B1 ** ADAM_STEP)
    v_hat = v2 / (1.0 - ADAM_B2 ** ADAM_STEP)
    delta = -ADAM_LR * (m_hat / (jnp.sqrt(v_hat) + ADAM_EPS) + ADAM_WD * w)
    return delta, m2, v2


def _sum_adamw(land, part, me_idx, w, m, v, name, tile=256):
    r, c = w.shape
    tr = min(tile, r)

    def body(me_ref, own_ref, *refs):
        slots = refs[:N_DEV]
        w_ref, m_ref, v_ref, g_ref, d_ref, m2_ref, v2_ref = refs[N_DEV:]
        own = own_ref[...].astype(F32)
        g = None
        for s in range(N_DEV):
            term = jnp.where(me_ref[0] == s, own, slots[s][...].astype(F32))
            g = term if g is None else g + term
        d, m2, v2 = _adamw(w_ref[...], g, m_ref[...], v_ref[...])
        g_ref[...] = g
        d_ref[...] = d
        m2_ref[...] = m2
        v2_ref[...] = v2

    def slot_spec(s):
        return pl.BlockSpec((None, tr, c), lambda i, me: (jnp.where(me[0] == s, (s + 1) % N_DEV, s), i, 0))

    row = pl.BlockSpec((tr, c), lambda i, me: (i, 0))
    return pl.pallas_call(
        body, name=name,
        grid_spec=pltpu.PrefetchScalarGridSpec(
            num_scalar_prefetch=1, grid=(r // tr,),
            in_specs=[pl.BlockSpec((None, tr, c), lambda i, me: (me[0], i, 0))] + [slot_spec(s) for s in range(N_DEV)] + [row] * 3,
            out_specs=[row] * 4),
        out_shape=[jax.ShapeDtypeStruct((r, c), F32)] * 4,
        compiler_params=_cparams("parallel"),
    )(me_idx, part, *[land] * N_DEV, w, m, v)


def _small_update(name, me_idx, kinds, lands, owns, ws, ms, vs, sums=()):
    n = len(ws)
    lands, owns = list(lands) + [s[0] for s in sums], list(owns) + [s[1] for s in sums]
    kinds = list(kinds) + ["gather"] * len(sums)
    nl = len(lands)

    def summed(me, land_ref, own):
        g = None
        for s in range(N_DEV):
            term = jnp.where(me == s, own, land_ref[s])
            g = term if g is None else g + term
        return g

    def body(me_ref, *refs):
        land_refs, own_refs = refs[:nl], refs[nl:2 * nl]
        w_refs, m_refs, v_refs = (refs[2 * nl + i * n:2 * nl + (i + 1) * n] for i in range(3))
        outs = refs[2 * nl + 3 * n:]
        me = me_ref[0]
        for i in range(n):
            g = summed(me, land_refs[i], own_refs[i][...])
            d, m2, v2 = _adamw(w_refs[i][...], g, m_refs[i][...], v_refs[i][...])
            for ref, val in zip(outs[4 * i:4 * i + 4], (g, d, m2, v2)):
                ref[...] = val
        for i in range(n, nl):
            outs[4 * n + i - n][...] = summed(me, land_refs[i], own_refs[i][...])

    def whole(shape):
        return pl.BlockSpec(shape, lambda i, me, nd=len(shape): (0,) * nd)

    def own_spec(kind, own):
        if kind == "gather":
            return whole(own.shape)
        return pl.BlockSpec((None, *own.shape[1:]), lambda i, me: (me[0], 0, 0))

    shapes = [w.shape for w in ws]
    out_shapes = [s for s in shapes for _ in range(4)] + [s[1].shape for s in sums]
    return pl.pallas_call(
        body, name=name,
        grid_spec=pltpu.PrefetchScalarGridSpec(
            num_scalar_prefetch=1, grid=(1,),
            in_specs=[whole(a.shape) for a in lands] + [own_spec(k, o) for k, o in zip(kinds, owns)]
            + [whole(s) for s in shapes] * 3,
            out_specs=[whole(s) for s in out_shapes]),
        out_shape=[jax.ShapeDtypeStruct(s, F32) for s in out_shapes],
        compiler_params=_cparams("arbitrary"),
    )(me_idx, *lands, *owns, *ws, *ms, *vs)


def _small_view(n, a):
    if a.ndim == 1:
        return a.reshape(1, -1)
    if a.ndim == 3:
        return a.transpose(1, 2, 0).reshape(QKV_BLOCK * QKV_BLOCK, -1)
    return a.T if n == "w_if" else a


def _small_unview(n, a, shape):
    if len(shape) == 1:
        return a.reshape(shape)
    if len(shape) == 3:
        return a.reshape(QKV_BLOCK, QKV_BLOCK, -1).transpose(2, 0, 1)
    return a.T if n == "w_if" else a


def _small_shards(n, g):
    if n == "w_if":
        return g.reshape(N_DEV, -1, g.shape[1]).transpose(0, 2, 1)
    return g.reshape(g.shape[0], N_DEV, -1).transpose(1, 0, 2)


def _small_unshard(n, s):
    if n == "w_if":
        return s.transpose(0, 2, 1).reshape(-1, s.shape[1])
    return s.transpose(1, 0, 2).reshape(s.shape[1], -1)


def _to_hm(a, d):
    t = a.shape[0]
    return a.reshape(t, HEADS, d).transpose(1, 0, 2)


def _from_hm(a):
    h, t, d = a.shape
    return a.transpose(1, 0, 2).reshape(t, h * d)


def _gate_rows(g):
    t = g.shape[0]
    return g.T.reshape(HEADS, t // CHUNK, 1, CHUNK)


def _gate_cols(g):
    h, nc, _, c = g.shape
    return g.reshape(h, nc * c).T


def _blockdiag_dense(w):
    n = w.shape[0] * QKV_BLOCK
    tiled = jnp.tile(w.reshape(n, QKV_BLOCK), (1, n // QKV_BLOCK))
    r = lax.broadcasted_iota(jnp.int32, (n, n), 0)
    c = lax.broadcasted_iota(jnp.int32, (n, n), 1)
    return jnp.where(r // QKV_BLOCK == c // QKV_BLOCK, tiled, 0.0)


def _blockdiag_blocks(dense):
    n = dense[0].shape[0]
    k = len(dense)

    def body(*refs):
        r = lax.broadcasted_iota(jnp.int32, (n, n), 0)
        c = lax.broadcasted_iota(jnp.int32, (n, n), 1)
        fr = lax.broadcasted_iota(jnp.int32, (n, LANES), 0)
        fc = lax.broadcasted_iota(jnp.int32, (n, LANES), 1)
        fold = ((fr & (QKV_BLOCK - 1)) == fc).astype(BF16)
        for i in range(k):
            kept = jnp.where((r >> 2) == (c >> 2), refs[i][...], 0.0)
            refs[k + i][...] = sum(lax.dot_general(t, fold, _dims("nn", 2), preferred_element_type=F32) for t in _split3(kept))

    out = pl.pallas_call(body, name="blockdiag_blocks", out_shape=[jax.ShapeDtypeStruct((n, LANES), F32)] * k)(*dense)
    return [o[:, 0:QKV_BLOCK].reshape(n // QKV_BLOCK, QKV_BLOCK, QKV_BLOCK) for o in out]


def _col_blocks(w):
    k, n = w.shape
    return w.reshape(k, N_DEV, n // N_DEV).transpose(1, 0, 2)


def _from_col_blocks(g):
    d, k, n = g.shape
    return g.transpose(1, 0, 2).reshape(k, d * n)


def _local_step(x, tgt, weight, ws, prefetch, on_grads, on_small):
    t, d = x.shape
    g1 = ws["g_pre_mix"]

    def dep(token):
        return () if token is None else (token,)

    w_in = weight("w_in", x)
    fetch_mix = prefetch(("w_pa", "w_pb", "w_o"), w_in)

    def proj_in_fwd(xv, g, w):
        hv = _rms(xv, g)
        return (hv, _raw_dot(hv, w, "nn")), ()

    h, proj = _rowwise("proj_in", proj_in_fwd, [x], [g1, w_in], [(d, BF16), (w_in.shape[1], F32)], deps=dep(fetch_mix))
    offs = [0]
    for s in IN_SPLITS:
        offs.append(offs[-1] + s)
    q_a, k_a, v_a, g_a, a_low, x_m, o_pre, gate_a, gate_b = [proj[:, offs[i]:offs[i + 1]] for i in range(9)]

    a_low_p = jnp.pad(a_low, ((0, 0), (0, LANES - LOWRANK)))
    w_a_up_p = jnp.pad(ws["w_a_up"], ((0, LANES - LOWRANK), (0, 0)))
    b_a_up = ws["b_a_up"]
    (la,) = _rowwise("gla_decay", lambda al, w, b: ((_log_decay(al, w, b),), ()), [a_low_p], [w_a_up_p, b_a_up],
                     [(HEADS * GLA_DK, F32)])
    fetch_up = prefetch(("w_up",), la)
    q_hm, k_hm, la_hm = _to_hm(q_a, GLA_DK), _to_hm(k_a, GLA_DK), _to_hm(la, GLA_DK)
    o_gla, s_prev = _gla_fwd(q_hm, k_hm, v_a, la_hm)
    gn = ws["g_gla_norm"]
    ml_w = HEADS * HEAD_W

    def proj_a_fwd(o, g, n_, w):
        ya = jnp.concatenate(_per_head(_gla_out, [o, g], [], [n_]), axis=1)
        return (ya, _raw_dot(ya, w, "nn")), ()

    ya_in, y_a = _rowwise("proj_a", proj_a_fwd, [o_gla, g_a], [gn, weight("w_pa", o_gla)], [(ml_w, BF16), (d, F32)],
                          tile=512, deps=dep(fetch_up))

    cw = ws["conv_w"]
    w_if_p = jnp.pad(ws["w_if"], ((0, 0), (0, LANES - 2 * HEADS)))
    pre_params = [cw[0:1], cw[1:2], cw[2:3], cw[3:4], ws["conv_b"],
                  _blockdiag_dense(ws["w_q_ml"]), _blockdiag_dense(ws["w_k_ml"]), _blockdiag_dense(ws["w_v_ml"]),
                  w_if_p[0:ml_w], w_if_p[ml_w:2 * ml_w], w_if_p[2 * ml_w:3 * ml_w],
                  jnp.pad(ws["b_if"], ((0, 0), (0, LANES - 2 * HEADS)))]
    x_pad = jnp.pad(x_m, ((HALO, 0), (0, 0)))
    xc, q_m, k_m, v_m, gl = _ml_pre_fwd(x_m, x_pad, pre_params)
    li, lf = _gate_rows(gl[:, 0:HEADS]), _gate_rows(gl[:, HEADS:2 * HEADS])
    hc, c_prev, n_prev, m_prev = _ml_fwd(q_m, k_m, v_m, li, lf)
    fetch_down = prefetch(("w_down",), hc)
    g_ml, skip = ws["g_ml_norm"], ws["ml_skip"]

    def proj_b_fwd(a, b, c_, ga, gb, ya, g, s, w):
        hb = jnp.concatenate(_per_head(_ml_out, [a, b, c_], [g, s]), axis=1)
        yb = _raw_dot(hb, w, "nn")
        return (hb, yb, _merge(ga, gb, ya, yb)), ()

    h_b, y_b, merged = _rowwise("proj_b", proj_b_fwd, [hc, o_pre, xc, gate_a, gate_b, y_a], [g_ml, skip, weight("w_pb", hc)],
                                [(ml_w, BF16), (d, F32), (d, BF16)], tile=512, deps=dep(fetch_down))

    gpm, gpl, gpo = ws["g_post_mix"], ws["g_pre_mlp"], ws["g_post_mlp"]

    def proj_o_fwd(mg, xv, w, a, b):
        zv = _raw_dot(mg, w, "nn")
        return (zv, *_post_mix(xv, zv, a, b)), ()

    z, x1, h2 = _rowwise("proj_o", proj_o_fwd, [merged, x], [weight("w_o", merged), gpm, gpl],
                         [(d, F32), (d, F32), (d, BF16)], tile=512)
    up, u = _mm(h2, weight("w_up", h2), "nn", (BF16, BF16), "mlp_up", epilogue=lambda p: (p, jnp.square(jnp.maximum(p, 0.0))))

    def mlp_down_loss(uv, x1v, tgtv, w, g):
        dnv = _raw_dot(uv, w, "nn")
        loss, vjp = jax.vjp(lambda a, b, c_: _loss_rows(a, b, tgtv, c_), x1v, dnv, g)
        dx1, ddn, dg = vjp(jnp.ones((1, 1), F32))
        return (dx1, ddn), (jnp.broadcast_to(loss, (1, LANES)), dg)

    dx1_y, d_dn, loss, d_gpo = _rowwise("mlp_down", mlp_down_loss, [u, x1, tgt], [weight("w_down", u), gpo],
                                        [(d, F32), (d, BF16)], [((1, LANES), F32), ((1, d), F32)], tile=512)

    (d_up,) = _mm(d_dn, weight("w_down", u), "nt", (BF16,), "mlp_down_dx", extra=[up],
                  epilogue=lambda p, a: (p * (2.0 * jnp.maximum(a.astype(F32), 0.0)),))
    dw_down = _mm(u, d_dn, "tn", BF16, "mlp_down_dw", tm=512)
    dw_up = _mm(h2, d_up, "tn", BF16, "mlp_up_dw")
    sent_mlp = on_grads(dict(w_down=dw_down, w_up=dw_up))

    def mlp_up_dx(dup, xv, zv, dx1, w, a, b):
        _, vjp = jax.vjp(_post_mix, xv, zv, a, b)
        dx, dz, da, db = vjp((dx1, _raw_dot(dup, w, "nt")))
        return (dx, dz), (da, db)

    dx_res, d_z, d_gpm, d_gpl = _rowwise("mlp_up_dx", mlp_up_dx, [d_up, x, z, dx1_y], [weight("w_up", h2), gpm, gpl],
                                         [(d, F32), (d, BF16)], [((1, d), F32), ((1, d), F32)], tile=512, deps=dep(sent_mlp))
    dw_o = _mm(merged, d_z, "tn", BF16, "proj_o_dw")

    def proj_o_dx(dz, ga, gb, ya, yb, w):
        return jax.vjp(_merge, ga, gb, ya, yb)[1](_raw_dot(dz, w, "nt")), ()

    d_ga, d_gb, d_ya, d_yb = _rowwise("proj_o_dx", proj_o_dx, [d_z, gate_a, gate_b, y_a, y_b], [weight("w_o", merged)],
                                      [(d, F32), (d, F32), (d, BF16), (d, BF16)], tile=512)
    dw_pa = _mm(ya_in, d_ya, "tn", BF16, "proj_a_dw")
    dw_pb = _mm(h_b, d_yb, "tn", BF16, "proj_b_dw")
    sent_mix = on_grads(dict(w_o=dw_o, w_pa=dw_pa, w_pb=dw_pb))

    def proj_b_dx(dyb, a, b, c_, w, g, s):
        ct = _raw_dot(dyb, w, "nt")
        parts = []
        for hs in _head_slices(HEAD_W):
            _, vjp = jax.vjp(_ml_out, a[:, hs], b[:, hs], c_[:, hs], g[:, hs], s[:, hs])
            parts.append(vjp(ct[:, hs]))
        cat = lambda i: jnp.concatenate([p[i] for p in parts], axis=1)
        return (cat(0), cat(1), cat(2)), (cat(3), cat(4))

    d_hc, d_opre, d_xc, d_gml, d_skip = _rowwise("proj_b_dx", proj_b_dx, [d_yb, hc, o_pre, xc],
                                                 [weight("w_pb", hc), g_ml, skip], [(ml_w, F32)] * 3, [((1, ml_w), F32)] * 2,
                                                 tile=512, deps=dep(sent_mix))
    d_qm, d_km, d_vm, d_li, d_lf = _ml_bwd(q_m, k_m, v_m, li, lf, c_prev, n_prev, m_prev, d_hc)
    d_gl = jnp.concatenate([_gate_cols(d_li), _gate_cols(d_lf), jnp.zeros((t, LANES - 2 * HEADS), F32)], axis=1)
    pre_grads = _ml_pre_bwd(x_m, x_pad, pre_params, [d_xc, d_qm, d_km, d_vm, d_gl])
    d_xm = pre_grads[0]
    d_cw = jnp.concatenate(pre_grads[1:5], axis=0)
    d_cb = pre_grads[5]
    d_wq, d_wk, d_wv = _blockdiag_blocks(pre_grads[6:9])
    d_wif = jnp.concatenate(pre_grads[9:12], axis=0)[:, 0:2 * HEADS]
    d_bif = pre_grads[12][:, 0:2 * HEADS]

    def proj_a_dx(dya, o, g, w, n_):
        ct = _raw_dot(dya, w, "nt")
        parts = []
        for hs in _head_slices(HEAD_W):
            _, vjp = jax.vjp(_gla_out, o[:, hs], g[:, hs], n_)
            parts.append(vjp(ct[:, hs]))
        cat = lambda i: jnp.concatenate([p[i] for p in parts], axis=1)
        return (cat(0), cat(1)), (sum(p[2] for p in parts),)

    d_o, d_g_a, d_gn = _rowwise("proj_a_dx", proj_a_dx, [d_ya, o_gla, g_a], [weight("w_pa", o_gla), gn], [(ml_w, F32)] * 2,
                                [((1, HEAD_W), F32)], tile=512)
    dq_hm, dk_hm, d_va, dla_hm = _gla_bwd(q_hm, k_hm, v_a, la_hm, s_prev, d_o)

    def decay_bwd(al, ct, w, b):
        _, vjp = jax.vjp(_log_decay, al, w, b)
        dal, dw, db = vjp(ct)
        return (dal,), (dw, db)

    d_alow_p, d_wa_p, d_ba = _rowwise("gla_decay_bwd", decay_bwd, [a_low_p, _from_hm(dla_hm)], [w_a_up_p, b_a_up],
                                      [(LANES, F32)], [(w_a_up_p.shape, F32), (b_a_up.shape, F32)])
    d_proj = jnp.concatenate([_from_hm(dq_hm), _from_hm(dk_hm), d_va, d_g_a, d_alow_p[:, 0:LOWRANK], d_xm, d_opre, d_ga, d_gb],
                             axis=1).astype(BF16)
    small = dict(w_a_up=d_wa_p[0:LOWRANK], b_a_up=d_ba, g_gla_norm=d_gn, conv_w=d_cw, conv_b=d_cb,
                 w_q_ml=d_wq, w_k_ml=d_wk, w_v_ml=d_wv, w_if=d_wif, b_if=d_bif, ml_skip=d_skip, g_ml_norm=d_gml,
                 g_post_mix=d_gpm, g_pre_mlp=d_gpl, g_post_mlp=d_gpo)
    sent_small = on_small(small, loss)
    dw_in = _mm(h, d_proj, "tn", F32, "proj_in_dw", tm=512, tk=1024, deps=dep(sent_small))
    sent_in = on_grads(dict(w_in=dw_in))

    def proj_in_dx(dp, xv, dres, w, g):
        _, vjp = jax.vjp(_rms, xv, g)
        dx, dg = vjp(_raw_dot(dp, w, "nt"))
        return (dx + dres,), (dg,)

    grad_x, d_g1 = _rowwise("proj_in_dx", proj_in_dx, [d_proj, x, dx_res], [w_in, g1], [(d, F32)], [((1, d), F32)],
                            deps=dep(sent_in))
    return grad_x, on_small(dict(g_pre_mix=d_g1), None)


BIG = ("w_in", "w_pa", "w_pb", "w_o", "w_up", "w_down")
BIG_COL_SHARDED = ("w_in", "w_pa", "w_pb", "w_up")
SMALL_SHARDED = ("w_a_up", "conv_w", "w_if")
SMALL = ("g_pre_mix", "w_a_up", "b_a_up", "g_gla_norm", "conv_w", "conv_b", "w_q_ml", "w_k_ml", "w_v_ml", "w_if", "b_if",
         "ml_skip", "g_ml_norm", "g_post_mix", "g_pre_mlp", "g_post_mlp")
WEIGHTS = ("g_pre_mix", "w_in", "w_a_up", "b_a_up", "g_gla_norm", "conv_w", "conv_b", "w_q_ml", "w_k_ml", "w_v_ml", "w_if", "b_if",
           "ml_skip", "g_ml_norm", "w_pa", "w_pb", "w_o", "g_post_mix", "g_pre_mlp", "w_up", "w_down", "g_post_mlp")


def kernel(x, g_pre_mix, w_in, w_a_up, b_a_up, g_gla_norm, conv_w, conv_b, w_q_ml, w_k_ml, w_v_ml, w_if, b_if, ml_skip, g_ml_norm, w_pa, w_pb, w_o, g_post_mix, g_pre_mlp, w_up, w_down, g_post_mlp, loss_target, m_g_pre_mix, m_w_in, m_w_a_up, m_b_a_up, m_g_gla_norm, m_conv_w, m_conv_b, m_w_q_ml, m_w_k_ml, m_w_v_ml, m_w_if, m_b_if, m_ml_skip, m_g_ml_norm, m_w_pa, m_w_pb, m_w_o, m_g_post_mix, m_g_pre_mlp, m_w_up, m_w_down, m_g_post_mlp, v_g_pre_mix, v_w_in, v_w_a_up, v_b_a_up, v_g_gla_norm, v_conv_w, v_conv_b, v_w_q_ml, v_w_k_ml, v_w_v_ml, v_w_if, v_b_if, v_ml_skip, v_g_ml_norm, v_w_pa, v_w_pb, v_w_o, v_g_post_mix, v_g_pre_mlp, v_w_up, v_w_down, v_g_post_mlp):
    args = dict(locals())
    w = {n: args[n][0] for n in WEIGHTS}
    m = {n: args["m_" + n][0] for n in WEIGHTS}
    v = {n: args["v_" + n][0] for n in WEIGHTS}

    me_lin = _lin(_me())
    me_idx = jnp.reshape(me_lin, (1,)).astype(jnp.int32)

    def full_weight(n, g):
        return _from_col_blocks(g) if n in BIG_COL_SHARDED else g.reshape(-1, g.shape[-1])

    def grad_parts(n, g):
        return (_col_blocks(g) if n in BIG_COL_SHARDED else g.reshape(N_DEV, -1, g.shape[-1])).astype(BF16)

    sharded_names = tuple(SMALL_SHARDED)
    small_w_state, small_w_token = _copies_start("gather", [_small_view(n, w[n]) for n in sharded_names],
                                                 "allgather_start_small_weights")
    ready = {"w_in": full_weight("w_in", _allgather_big([w["w_in"].astype(BF16)], "allgather_w_in", [small_w_token])[0])}
    pending = {}

    def prefetch(group, after):
        state, token = _copies_start("gather_chips", [w[n].astype(BF16) for n in group], "allgather_start_" + group[0], after)
        for n in group:
            pending[n] = (group, state)
        return token

    def weight(n, after):
        if n not in ready:
            group, state = pending[n]
            shards, lands = _copies_wait(state, after, "allgather_wait_" + group[0])
            state, token = _copies_start("gather_pass", shards, "allgather_pass_" + group[0], lands=lands)
            shards, lands = _copies_wait(state, token, "allgather_passed_" + group[0])
            for gn, shard, land in zip(group, shards, lands):
                ready[gn] = full_weight(gn, lax.dynamic_update_slice(land, shard[None], (me_lin, 0, 0)))
        return ready[n]

    small_w_own, small_w_lands = _copies_wait(small_w_state, ready["w_in"], "allgather_wait_small_weights")
    ws = {n: (w[n].reshape(1, -1) if w[n].ndim == 1 else w[n]) for n in SMALL if n not in SMALL_SHARDED}
    for n, own, land in zip(sharded_names, small_w_own, small_w_lands):
        ws[n] = _small_unshard(n, lax.dynamic_update_slice(land, own[None], (me_lin, 0, 0)))

    sent = []

    def on_grads(grads):
        names = tuple(grads)
        state, token = _copies_start("exchange", [grad_parts(n, grads[n]) for n in names], "exchange_start_" + names[0])
        sent.append((names, state))
        return token

    small_sent = []

    def on_small(small, loss):
        names = tuple(small)
        kinds = ["exchange" if n in SMALL_SHARDED else "gather" for n in names]
        srcs = [_small_shards(n, small[n]) if n in SMALL_SHARDED else _small_view(n, small[n]) for n in names]
        extra = [] if loss is None else [loss]
        state, token = _copies_start(kinds + ["gather"] * len(extra), srcs + extra, "allgather_start_small_" + names[0])
        small_sent.append((names, kinds, state))
        return token

    grad_x, last_token = _local_step(x[0], loss_target[0], weight, ws, prefetch, on_grads, on_small)

    out = {}

    def finish(names, state, after):
        parts, lands = _copies_wait(state, after, "exchange_wait_" + names[0])
        for n, part, land in zip(names, parts, lands):
            out[n] = _sum_adamw(land, part, me_idx, w[n], m[n], v[n], "adamw_" + n)

    def finish_small(names, kinds, state, after):
        own, lands = _copies_wait(state, after, "allgather_wait_small_" + names[0])
        k = len(names)
        upd = _small_update("adamw_small_" + names[0], me_idx, kinds, lands[:k], own[:k],
                            *[[_small_view(n, d[n]) for n in names] for d in (w, m, v)], sums=list(zip(lands[k:], own[k:])))
        for i, n in enumerate(names):
            out[n] = tuple(_small_unview(n, a, w[n].shape) for a in upd[4 * i:4 * i + 4])
        return upd[4 * k:]

    (loss_sum,) = finish_small(*small_sent[0], [grad_x, last_token])
    for names, state in sent[:-1]:
        finish(names, state, [grad_x, last_token])
    finish(*sent[-1], [loss_sum] + [out[n][1] for n in BIG if n in out])
    finish_small(*small_sent[1], [out["w_in"][1]])

    shaped = lambda a, n: a.reshape(args[n].shape)
    return (loss_sum[0, 0], grad_x[None],
            *[shaped(out[n][0], n) for n in WEIGHTS], *[shaped(out[n][1], n) for n in WEIGHTS],
            *[shaped(out[n][2], n) for n in WEIGHTS], *[shaped(out[n][3], n) for n in WEIGHTS])
```

```python
import functools

import jax
import jax.numpy as jnp
from jax import lax
from jax.experimental import pallas as pl
from jax.experimental.pallas import tpu as pltpu

F32 = jnp.float32
BF16 = jnp.bfloat16
MESH = pl.DeviceIdType.MESH

N_DEV = 8
EPS = 1e-6
CHUNK = 64
CHUNKS_PER_STEP = 4
HEADS = 4
GLA_DK = 64
HEAD_W = 128
GLA_GATE_NORM = 16.0
LOWRANK = 16
CONV_K = 4
QKV_BLOCK = 4
LANES = 128
HALO = 8
IN_SPLITS = (256, 256, 512, 512, 16, 512, 512, 1024, 1024)

ADAM_LR = 0.001
ADAM_B1 = 0.9
ADAM_B2 = 0.999
ADAM_EPS = 1e-08
ADAM_WD = 0.01
ADAM_STEP = 10

VMEM_LIMIT = 56 * 1024 * 1024


def _cparams(*sem):
    return pltpu.CompilerParams(dimension_semantics=sem, vmem_limit_bytes=VMEM_LIMIT)


def _dims(mode, ndim):
    contract = {"nn": ((ndim - 1,), (ndim - 2,)), "nt": ((ndim - 1,), (ndim - 1,)), "tn": ((ndim - 2,), (ndim - 2,))}[mode]
    return contract, (((0,), (0,)) if ndim == 3 else ((), ()))


def _raw_dot(a, b, mode):
    return lax.dot_general(a.astype(BF16), b.astype(BF16), _dims(mode, a.ndim), preferred_element_type=F32)


@functools.partial(jax.custom_vjp, nondiff_argnums=(2,))
def _bdot(a, b, mode):
    return _raw_dot(a, b, mode)


def _bdot_fwd(a, b, mode):
    return _raw_dot(a, b, mode), (a, b)


def _bdot_bwd(mode, res, ct):
    a, b = res
    if mode == "nn":
        da, db = _raw_dot(ct, b, "nt"), _raw_dot(a, ct, "tn")
    elif mode == "nt":
        da, db = _raw_dot(ct, b, "nn"), _raw_dot(ct, a, "tn")
    else:
        da, db = _raw_dot(b, ct, "nt"), _raw_dot(a, ct, "nn")
    return da.astype(a.dtype), db.astype(b.dtype)


_bdot.defvjp(_bdot_fwd, _bdot_bwd)


def _split3(x):
    hi = x.astype(BF16)
    r1 = x - hi.astype(F32)
    mid = r1.astype(BF16)
    return hi, mid, (r1 - mid.astype(F32)).astype(BF16)


def _split_dot(tri, x):
    if x.ndim == 3:
        tri = jnp.broadcast_to(tri, (x.shape[0], *tri.shape))
    return sum(lax.dot_general(tri, t, _dims("nn", x.ndim), preferred_element_type=F32) for t in _split3(x))


def _tri(n, lower):
    r = lax.broadcasted_iota(jnp.int32, (n, n), 0)
    c = lax.broadcasted_iota(jnp.int32, (n, n), 1)
    return ((c <= r) if lower else (c >= r)).astype(BF16)


@jax.custom_vjp
def _cumsum_rows(x):
    return _split_dot(_tri(x.shape[-2], True), x)


def _cumsum_rows_fwd(x):
    return _cumsum_rows(x), None


def _cumsum_rows_bwd(_, ct):
    return (_split_dot(_tri(ct.shape[-2], False), ct),)


_cumsum_rows.defvjp(_cumsum_rows_fwd, _cumsum_rows_bwd)


def _abs(x):
    return jnp.where(x >= 0, x, -x)


def _sigmoid(x):
    return lax.logistic(x)


def _log_sigmoid(x):
    return jnp.minimum(x, 0.0) - jnp.log(1.0 + jnp.exp(-_abs(x)))


def _rms(x, g):
    return x * lax.rsqrt(jnp.mean(x * x, axis=-1, keepdims=True) + EPS) * g


def _head_slices(w):
    return [slice(h * w, (h + 1) * w) for h in range(HEADS)]


def _heads(ref, rows=slice(None)):
    return jnp.stack([ref[rows, hs] for hs in _head_slices(HEAD_W)])


def _put_heads(ref, val, rows=slice(None)):
    for h, hs in enumerate(_head_slices(HEAD_W)):
        ref[rows, hs] = val[h]


def _tile(dim, want):
    if dim <= want or dim % LANES:
        return dim
    t = want
    while dim % t:
        t -= LANES
    return t


def _mm(a, b, mode, out_dtype, name, tm=1024, tn=1024, tk=4096, epilogue=None, extra=(), deps=()):
    if mode == "nn":
        (m, k), (k2, n) = a.shape, b.shape
    elif mode == "nt":
        (m, k), (n, k2) = a.shape, b.shape
    else:
        (k, m), (k2, n) = a.shape, b.shape
    assert k == k2, (name, a.shape, b.shape)
    tm, tn, tk = _tile(m, tm), _tile(n, tn), _tile(k, tk)
    nk = k // tk
    out_dtypes = out_dtype if epilogue else (out_dtype,)
    assert nk == 1 or (out_dtype == F32 and not epilogue), name
    n_in = 2 + len(extra)

    def body(*refs):
        p = _raw_dot(refs[0][...], refs[1][...], mode)
        if nk > 1:
            _accumulate(pl.program_id(2), [refs[n_in + len(deps)]], [p])
            return
        outs = epilogue(p, *[r[...] for r in refs[2:n_in]]) if epilogue else (p,)
        for ref, val in zip(refs[n_in + len(deps):], outs):
            ref[...] = val.astype(ref.dtype)

    a_spec = pl.BlockSpec((tk, tm), lambda i, j, kk: (kk, i)) if mode == "tn" else pl.BlockSpec((tm, tk), lambda i, j, kk: (i, kk))
    b_spec = pl.BlockSpec((tn, tk), lambda i, j, kk: (j, kk)) if mode == "nt" else pl.BlockSpec((tk, tn), lambda i, j, kk: (kk, j))
    o_spec = pl.BlockSpec((tm, tn), lambda i, j, kk: (i, j))
    res = pl.pallas_call(
        body, name=name, grid=(m // tm, n // tn, nk),
        in_specs=[a_spec, b_spec] + [o_spec] * len(extra) + [ANY] * len(deps), out_specs=[o_spec] * len(out_dtypes),
        out_shape=[jax.ShapeDtypeStruct((m, n), dt) for dt in out_dtypes],
        compiler_params=_cparams("parallel", "parallel", "arbitrary"),
    )(a, b, *extra, *deps)
    return res if epilogue else res[0]


def _rowwise(name, fn, rows, params, out_rows, out_accs=(), tile=256, deps=()):
    t = rows[0].shape[0]
    r = min(tile, t)
    assert t % r == 0
    n_in, n_or = len(rows) + len(params), len(out_rows)
    n_all = n_in + len(deps)
    params = list(params) + list(deps)

    def body(*refs):
        vals = [ref[...] for ref in refs[:n_in]]
        outs = refs[n_all:]
        ro, ao = fn(*vals)
        for ref, v in zip(outs[:n_or], ro):
            ref[...] = v.astype(ref.dtype)
        if out_accs:
            _accumulate(pl.program_id(0), outs[n_or:], ao)

    def full(shape):
        return pl.BlockSpec(shape, lambda i, nd=len(shape): (0,) * nd)

    return pl.pallas_call(
        body, name=name, grid=(t // r,),
        in_specs=[pl.BlockSpec((r, a.shape[1]), lambda i: (i, 0)) for a in rows] + [full(p.shape) for p in params],
        out_specs=[pl.BlockSpec((r, w), lambda i: (i, 0)) for w, _ in out_rows] + [full(s) for s, _ in out_accs],
        out_shape=[jax.ShapeDtypeStruct((t, w), dt) for w, dt in out_rows] + [jax.ShapeDtypeStruct(s, dt) for s, dt in out_accs],
        compiler_params=_cparams("arbitrary"),
    )(*rows, *params)


def _accumulate(step, refs, vals):
    for ref, v in zip(refs, vals):
        @pl.when(step == 0)
        def _(ref=ref, v=v):
            ref[...] = v.astype(ref.dtype)

        @pl.when(step > 0)
        def _(ref=ref, v=v):
            ref[...] += v.astype(ref.dtype)


def _gla_chunk(q, k, v, la, st):
    c = q.shape[-2]
    row = lax.broadcasted_iota(jnp.int32, (c, c), 0)
    col = lax.broadcasted_iota(jnp.int32, (c, c), 1)
    cum = _cumsum_rows(la)
    cl = jnp.sum(la, axis=-2, keepdims=True)
    ep = jnp.exp(cum)
    en = jnp.exp(-cum)
    qs = q * (GLA_DK ** -0.5)
    qp = qs * ep
    a_f = _bdot(qp, k * en, "nt")
    a_b = _bdot(qs * en, k * ep, "nt")
    sc = jnp.where(row >= col, a_f, a_b)
    o = _bdot(sc, v, "nn") + _bdot(qp, st, "nt")
    kd = k * jnp.exp(cl - cum)
    st_new = st * jnp.exp(cl) + _bdot(v, kd, "tn")
    return o, st_new


def _gla_specs(nc, rev):
    nb = nc // CHUNKS_PER_STEP
    rows = CHUNKS_PER_STEP * CHUNK

    def blk(n):
        return (nb - 1 - n) if rev else n
    hm = pl.BlockSpec((HEADS, rows, GLA_DK), lambda n: (0, blk(n), 0))
    tm = pl.BlockSpec((rows, HEADS * HEAD_W), lambda n: (blk(n), 0))
    st = pl.BlockSpec((HEADS, CHUNKS_PER_STEP, HEAD_W, GLA_DK), lambda n: (0, blk(n), 0, 0))
    return nb, hm, tm, st


def _chunk_rows(c):
    return slice(c * CHUNK, (c + 1) * CHUNK)


def _gla_fwd(q, k, v, la):
    t = v.shape[0]
    nc = t // CHUNK
    nb, hm, tm, st = _gla_specs(nc, False)

    def body(q_ref, k_ref, v_ref, la_ref, o_ref, sp_ref, st_ref):
        @pl.when(pl.program_id(0) == 0)
        def _():
            st_ref[...] = jnp.zeros_like(st_ref)

        s = st_ref[...]
        for c in range(CHUNKS_PER_STEP):
            r = _chunk_rows(c)
            sp_ref[:, c] = s
            o, s = _gla_chunk(q_ref[:, r], k_ref[:, r], _heads(v_ref, r), la_ref[:, r], s)
            _put_heads(o_ref, o, r)
        st_ref[...] = s

    return pl.pallas_call(
        body, name="gla_fwd", grid=(nb,),
        in_specs=[hm, hm, tm, hm], out_specs=[tm, st],
        out_shape=[jax.ShapeDtypeStruct((t, HEADS * HEAD_W), F32), jax.ShapeDtypeStruct((HEADS, nc, HEAD_W, GLA_DK), F32)],
        scratch_shapes=[pltpu.VMEM((HEADS, HEAD_W, GLA_DK), F32)],
        compiler_params=_cparams("arbitrary"),
    )(q, k, v, la)


def _gla_bwd(q, k, v, la, sp, do):
    t = v.shape[0]
    nc = t // CHUNK
    nb, hm, tm, st = _gla_specs(nc, True)

    def body(q_ref, k_ref, v_ref, la_ref, sp_ref, do_ref, dq_ref, dk_ref, dv_ref, dla_ref, ds_ref):
        @pl.when(pl.program_id(0) == 0)
        def _():
            ds_ref[...] = jnp.zeros_like(ds_ref)

        ds = ds_ref[...]
        for c in reversed(range(CHUNKS_PER_STEP)):
            r = _chunk_rows(c)
            _, vjp = jax.vjp(_gla_chunk, q_ref[:, r], k_ref[:, r], _heads(v_ref, r), la_ref[:, r], sp_ref[:, c])
            dq, dk, dv, dla, ds = vjp((_heads(do_ref, r), ds))
            dq_ref[:, r] = dq
            dk_ref[:, r] = dk
            _put_heads(dv_ref, dv, r)
            dla_ref[:, r] = dla
        ds_ref[...] = ds

    hm_shape = jax.ShapeDtypeStruct((HEADS, t, GLA_DK), F32)
    return pl.pallas_call(
        body, name="gla_bwd", grid=(nb,),
        in_specs=[hm, hm, tm, hm, st, tm], out_specs=[hm, hm, tm, hm],
        out_shape=[hm_shape, hm_shape, jax.ShapeDtypeStruct((t, HEADS * HEAD_W), F32), hm_shape],
        scratch_shapes=[pltpu.VMEM((HEADS, HEAD_W, GLA_DK), F32)],
        compiler_params=_cparams("arbitrary"),
    )(q, k, v, la, sp, do)


def _ml_chunk(q, k, v, li_r, lf_r, cm, nv, m):
    c = q.shape[-2]
    row = lax.broadcasted_iota(jnp.int32, (c, c), 0)
    col = lax.broadcasted_iota(jnp.int32, (c, c), 1)
    eye = (row == col).astype(F32)
    li_c = jnp.sum(eye * li_r, axis=-1, keepdims=True)
    lf_c = jnp.sum(eye * lf_r, axis=-1, keepdims=True)
    fc_c = jnp.sum((col <= row).astype(F32) * lf_r, axis=-1, keepdims=True)
    fc_r = jnp.sum((row <= col).astype(F32) * lf_c, axis=-2, keepdims=True)
    f_last = jnp.sum(lf_r, axis=-1, keepdims=True)
    kc = k * (HEAD_W ** -0.5)
    a_c = f_last - fc_c + li_c
    m_loc = jnp.max(a_c, axis=-2, keepdims=True)
    kw = kc * jnp.exp(a_c - m_loc)
    c_chunk = _bdot(kw, v, "tn")
    n_chunk = jnp.sum(kw, axis=-2, keepdims=True)
    m_new = jnp.maximum(f_last + m, m_loc)
    sp = jnp.exp(f_last + m - m_new)
    sl = jnp.exp(m_loc - m_new)
    cm_new = sp * cm + sl * c_chunk
    nv_new = sp * nv + sl * n_chunk
    log_d = li_r - _abs(fc_c - fc_r)
    g_inter = fc_c + m
    m_t = jnp.maximum(g_inter, jnp.max(log_d, axis=-1, keepdims=True))
    s = _bdot(q, kc, "nt") * jnp.exp(log_d - m_t)
    sc = jnp.exp(g_inter - m_t)
    num = _bdot(s, v, "nn") + sc * _bdot(q, cm, "nn")
    den = jnp.sum(s, axis=-1, keepdims=True) + sc * jnp.sum(q * nv, axis=-1, keepdims=True)
    den = jnp.maximum(_abs(den), jnp.exp(-m_t))
    return num / den, cm_new, nv_new, m_new


def _ml_specs(nc, rev):
    nb = nc // CHUNKS_PER_STEP

    def blk(n):
        return (nb - 1 - n) if rev else n
    tm = pl.BlockSpec((CHUNKS_PER_STEP * CHUNK, HEADS * HEAD_W), lambda n: (blk(n), 0))
    gate = pl.BlockSpec((HEADS, CHUNKS_PER_STEP, 1, CHUNK), lambda n: (0, blk(n), 0, 0))
    cm = pl.BlockSpec((HEADS, CHUNKS_PER_STEP, HEAD_W, HEAD_W), lambda n: (0, blk(n), 0, 0))
    vec = pl.BlockSpec((HEADS, CHUNKS_PER_STEP, 1, HEAD_W), lambda n: (0, blk(n), 0, 0))
    return nb, tm, gate, cm, vec


_ML_STATE = [pltpu.VMEM((HEADS, HEAD_W, HEAD_W), F32), pltpu.VMEM((HEADS, 1, HEAD_W), F32), pltpu.VMEM((HEADS, 1, HEAD_W), F32)]


def _ml_fwd(q, k, v, li, lf):
    t = q.shape[0]
    nc = t // CHUNK
    nb, tm, gate, cm, vec = _ml_specs(nc, False)

    def body(q_ref, k_ref, v_ref, li_ref, lf_ref, hc_ref, cp_ref, np_ref, mp_ref, c_ref, n_ref, m_ref):
        @pl.when(pl.program_id(0) == 0)
        def _():
            c_ref[...] = jnp.zeros_like(c_ref)
            n_ref[...] = jnp.zeros_like(n_ref)
            m_ref[...] = jnp.zeros_like(m_ref)

        cs, ns, ms = c_ref[...], n_ref[...], m_ref[...][:, :, 0:1]
        for c in range(CHUNKS_PER_STEP):
            r = _chunk_rows(c)
            cp_ref[:, c] = cs
            np_ref[:, c] = ns
            mp_ref[:, c] = jnp.broadcast_to(ms, m_ref.shape)
            hc, cs, ns, ms = _ml_chunk(_heads(q_ref, r), _heads(k_ref, r), _heads(v_ref, r), li_ref[:, c], lf_ref[:, c],
                                       cs, ns, ms)
            _put_heads(hc_ref, hc, r)
        c_ref[...] = cs
        n_ref[...] = ns
        m_ref[...] = jnp.broadcast_to(ms, m_ref.shape)

    return pl.pallas_call(
        body, name="mlstm_fwd", grid=(nb,),
        in_specs=[tm, tm, tm, gate, gate], out_specs=[tm, cm, vec, vec],
        out_shape=[jax.ShapeDtypeStruct((t, HEADS * HEAD_W), F32), jax.ShapeDtypeStruct((HEADS, nc, HEAD_W, HEAD_W), F32),
                   jax.ShapeDtypeStruct((HEADS, nc, 1, HEAD_W), F32), jax.ShapeDtypeStruct((HEADS, nc, 1, HEAD_W), F32)],
        scratch_shapes=_ML_STATE,
        compiler_params=_cparams("arbitrary"),
    )(q, k, v, li, lf)


def _ml_bwd(q, k, v, li, lf, cp, npv, mp, dhc):
    t = q.shape[0]
    nc = t // CHUNK
    nb, tm, gate, cm, vec = _ml_specs(nc, True)

    def body(q_ref, k_ref, v_ref, li_ref, lf_ref, cp_ref, np_ref, mp_ref, dhc_ref,
             dq_ref, dk_ref, dv_ref, dli_ref, dlf_ref, dc_ref, dn_ref, dm_ref):
        @pl.when(pl.program_id(0) == 0)
        def _():
            dc_ref[...] = jnp.zeros_like(dc_ref)
            dn_ref[...] = jnp.zeros_like(dn_ref)
            dm_ref[...] = jnp.zeros_like(dm_ref)

        dc, dn, dm = dc_ref[...], dn_ref[...], dm_ref[...][:, :, 0:1]
        for c in reversed(range(CHUNKS_PER_STEP)):
            r = _chunk_rows(c)
            _, vjp = jax.vjp(_ml_chunk, _heads(q_ref, r), _heads(k_ref, r), _heads(v_ref, r), li_ref[:, c], lf_ref[:, c],
                             cp_ref[:, c], np_ref[:, c], mp_ref[:, c][:, :, 0:1])
            dq, dk, dv, dli, dlf, dc, dn, dm = vjp((_heads(dhc_ref, r), dc, dn, dm))
            _put_heads(dq_ref, dq, r)
            _put_heads(dk_ref, dk, r)
            _put_heads(dv_ref, dv, r)
            dli_ref[:, c] = dli
            dlf_ref[:, c] = dlf
        dc_ref[...] = dc
        dn_ref[...] = dn
        dm_ref[...] = jnp.broadcast_to(dm, dm_ref.shape)

    tm_shape = jax.ShapeDtypeStruct((t, HEADS * HEAD_W), F32)
    gate_shape = jax.ShapeDtypeStruct((HEADS, nc, 1, CHUNK), F32)
    return pl.pallas_call(
        body, name="mlstm_bwd", grid=(nb,),
        in_specs=[tm, tm, tm, gate, gate, cm, vec, vec, tm], out_specs=[tm, tm, tm, gate, gate],
        out_shape=[tm_shape, tm_shape, tm_shape, gate_shape, gate_shape],
        scratch_shapes=_ML_STATE,
        compiler_params=_cparams("arbitrary"),
    )(q, k, v, li, lf, cp, npv, mp, dhc)


def _ml_pre(s0, s1, s2, s3, cw0, cw1, cw2, cw3, cb, wq, wk, wv, wiq, wik, wiv, bif):
    pre = cb + cw0 * s0 + cw1 * s1 + cw2 * s2 + cw3 * s3
    xc = pre * _sigmoid(pre)
    q = _bdot(xc, wq, "nn")
    k = _bdot(xc, wk, "nn")
    v = _bdot(s3, wv, "nn")
    gates = _bdot(q, wiq, "nn") + _bdot(k, wik, "nn") + _bdot(v, wiv, "nn") + bif
    lane = lax.broadcasted_iota(jnp.int32, gates.shape, 1)
    gl = jnp.where(lane < HEADS, gates, _log_sigmoid(gates))
    return xc, q, k, v, gl


def _delayed(xs_ref, x_ref, halo_ref, r):
    xs_ref[0:HALO, :] = halo_ref[...]
    xs_ref[HALO:HALO + r, :] = x_ref[...]
    return [xs_ref[pl.ds(HALO - (CONV_K - 1) + j, r), :] for j in range(CONV_K)]


def _full_spec(shape):
    return pl.BlockSpec(shape, lambda i, nd=len(shape): (0,) * nd)


def _ml_pre_fwd(x_m, x_pad, params, tile=256):
    t, w = x_m.shape
    r = min(tile, t)

    def body(*refs):
        x_ref, halo_ref = refs[:2]
        p = [ref[...] for ref in refs[2:2 + len(params)]]
        outs = refs[2 + len(params):-1]
        res = _ml_pre(*_delayed(refs[-1], x_ref, halo_ref, r), *p)
        for ref, val in zip(outs, res):
            ref[...] = val

    row = pl.BlockSpec((r, w), lambda i: (i, 0))
    return pl.pallas_call(
        body, name="ml_pre_fwd", grid=(t // r,),
        in_specs=[row, pl.BlockSpec((HALO, w), lambda i: (i * (r // HALO), 0))] + [_full_spec(p.shape) for p in params],
        out_specs=[row] * 4 + [pl.BlockSpec((r, LANES), lambda i: (i, 0))],
        out_shape=[jax.ShapeDtypeStruct((t, w), F32)] * 4 + [jax.ShapeDtypeStruct((t, LANES), F32)],
        scratch_shapes=[pltpu.VMEM((r + HALO, w), F32)],
        compiler_params=_cparams("arbitrary"),
    )(x_m, x_pad, *params)


def _ml_pre_bwd(x_m, x_pad, params, cts, tile=256):
    t, w = x_m.shape
    r = min(tile, t)
    nt = t // r
    n_p = len(params)

    def body(*refs):
        x_ref, halo_ref = refs[:2]
        p = [ref[...] for ref in refs[2:2 + n_p]]
        ct = [ref[...] for ref in refs[2 + n_p:7 + n_p]]
        dx_ref = refs[7 + n_p]
        dp_refs = refs[8 + n_p:8 + 2 * n_p]
        xs_ref, ds_ref, carry_ref = refs[8 + 2 * n_p:]
        step = pl.program_id(0)

        @pl.when(step == 0)
        def _():
            ds_ref[...] = jnp.zeros_like(ds_ref)
            carry_ref[...] = jnp.zeros_like(carry_ref)

        _, vjp = jax.vjp(_ml_pre, *_delayed(xs_ref, x_ref, halo_ref, r), *p)
        grads = vjp(tuple(ct))
        for j in range(CONV_K):
            ds_ref[j, HALO:HALO + r, :] = grads[j]
        lead = HALO + CONV_K - 1
        d_tile = sum(ds_ref[j, pl.ds(lead - j, r), :] for j in range(CONV_K))
        d_halo = sum(ds_ref[j, pl.ds(CONV_K - 1 - j, HALO), :] for j in range(CONV_K))
        dx_ref[...] = d_tile
        dx_ref[r - HALO:r, :] += carry_ref[...]
        carry_ref[...] = d_halo
        _accumulate(step, dp_refs, grads[CONV_K:])

    row = pl.BlockSpec((r, w), lambda i: (nt - 1 - i, 0))
    return pl.pallas_call(
        body, name="ml_pre_bwd", grid=(nt,),
        in_specs=[row, pl.BlockSpec((HALO, w), lambda i: ((nt - 1 - i) * (r // HALO), 0))] + [_full_spec(p.shape) for p in params]
        + [row] * 4 + [pl.BlockSpec((r, LANES), lambda i: (nt - 1 - i, 0))],
        out_specs=[row] + [_full_spec(p.shape) for p in params],
        out_shape=[jax.ShapeDtypeStruct((t, w), F32)] + [jax.ShapeDtypeStruct(p.shape, F32) for p in params],
        scratch_shapes=[pltpu.VMEM((r + HALO, w), F32), pltpu.VMEM((CONV_K, r + 2 * HALO, w), F32), pltpu.VMEM((HALO, w), F32)],
        compiler_params=_cparams("arbitrary"),
    )(x_m, x_pad, *params, *cts)


def _per_head(fn, row_vals, head_params, shared_params=()):
    return [fn(*[a[:, hs] for a in row_vals], *[p[:, hs] for p in head_params], *shared_params) for hs in _head_slices(HEAD_W)]


def _gla_out(o, g, gn):
    return _rms(o, gn) * (g * _sigmoid(g))


def _ml_out(hc, op, xc, g, sk):
    hcell = hc * _sigmoid(op)
    mu = jnp.mean(hcell, axis=-1, keepdims=True)
    d = hcell - mu
    var = jnp.mean(d * d, axis=-1, keepdims=True)
    return d * lax.rsqrt(var + EPS) * g + sk * xc


def _log_decay(al, w, b):
    return _log_sigmoid(_bdot(al, w, "nn") + b) * (1.0 / GLA_GATE_NORM)


def _merge(ga, gb, ya, yb):
    return _sigmoid(ga) * ya + _sigmoid(gb) * yb


def _post_mix(x, z, gpm, gpl):
    x1 = x + _rms(z, gpm)
    return x1, _rms(x1, gpl)


def _loss_rows(x1, dn, tgt, g):
    e = x1 + _rms(dn, g) - tgt
    return 0.5 * jnp.sum(jnp.mean(e * e, axis=-1, keepdims=True), axis=0, keepdims=True)


def _lin(p):
    return 4 * p[0] + 2 * p[1] + p[2]


def _me():
    return lax.axis_index("x"), lax.axis_index("y"), lax.axis_index("c")


def _flip(p, k):
    return tuple((1 - v) if (k >> (2 - i)) & 1 else v for i, v in enumerate(p))


ANY = pl.BlockSpec(memory_space=pl.ANY)


def _allgather_big(shards, name, deps=()):
    n = len(shards)
    n_in = n + len(deps)

    def body(*refs):
        ins, outs = refs[:n], refs[n_in:n_in + n]
        send_sems, recv_sems, local_sems = refs[n_in + n:]
        me = _me()
        x, y, c = me
        sib = (x, y, 1 - c)
        chips = [(1 - x, y), (x, 1 - y), (1 - x, 1 - y)]

        def cp(a, k, block, to, src=None):
            dst = outs[a].at[_lin(block)]
            return pltpu.make_async_remote_copy(src_ref=dst if src is None else src, dst_ref=dst,
                                                send_sem=send_sems.at[a * 7 + k], recv_sem=recv_sems.at[a * 7 + k],
                                                device_id=to, device_id_type=MESH)

        mine = [pltpu.make_async_copy(ins[a], outs[a].at[_lin(me)], local_sems.at[a]) for a in range(n)]
        for m in mine:
            m.start()
        first = []
        for a in range(n):
            first.append(cp(a, 0, me, sib, src=ins[a]))
            first += [cp(a, 1 + j, me, (*chip, c), src=ins[a]) for j, chip in enumerate(chips)]
        for f in first:
            f.start()
        passed = []
        for j, chip in enumerate(chips):
            for a in range(n):
                cp(a, 1 + j, (*chip, c), me).wait_recv()
                fwd = cp(a, 4 + j, (*chip, c), sib)
                fwd.start()
                passed.append(fwd)
        for a in range(n):
            cp(a, 0, sib, me).wait_recv()
            for j, chip in enumerate(chips):
                cp(a, 4 + j, (*chip, 1 - c), me).wait_recv()
        for f in first + passed:
            f.wait_send()
        for m in mine:
            m.wait()

    return pl.pallas_call(
        body, name=name,
        in_specs=[ANY] * n_in, out_specs=[ANY] * n,
        out_shape=[jax.ShapeDtypeStruct((N_DEV, *s.shape), s.dtype) for s in shards],
        scratch_shapes=[pltpu.SemaphoreType.DMA((7 * n,)), pltpu.SemaphoreType.DMA((7 * n,)), pltpu.SemaphoreType.DMA((n,))],
    )(*shards, *deps)


HBM = pl.BlockSpec(memory_space=pltpu.HBM)
SEM = pl.BlockSpec(memory_space=pltpu.SEMAPHORE)
DATAFLOW = pltpu.SideEffectType.DATAFLOW_SIDE_EFFECTING


SIBLING = 1
OTHER_CHIPS = (2, 4, 6)


def _peer_copies(kinds, srcs, lands, send_sems, recv_sems):
    me = _me()
    copies = []
    for a, (kind, src, land) in enumerate(zip(kinds, srcs, lands)):
        masks = {"gather": range(1, N_DEV), "exchange": range(1, N_DEV), "gather_chips": (SIBLING, *OTHER_CHIPS),
                 "gather_pass": OTHER_CHIPS}[kind]
        for k in masks:
            peer = _flip(me, k)
            if kind == "gather_pass":
                block = land.at[_lin(peer)]
                src_ref, dst_ref, target = block, block, _flip(me, SIBLING)
            else:
                src_ref, dst_ref, target = (src.at[_lin(peer)] if kind == "exchange" else src), land.at[_lin(me)], peer
            copies.append(pltpu.make_async_remote_copy(
                src_ref=src_ref, dst_ref=dst_ref, send_sem=send_sems.at[a * 7 + k - 1], recv_sem=recv_sems.at[a * 7 + k - 1],
                device_id=target, device_id_type=MESH))
    return copies


def _copies_start(kind, srcs, name, after=None, lands=None):
    n = len(srcs)
    extra = [] if after is None else [after]
    kind = [kind] * n if isinstance(kind, str) else list(kind)
    land_shapes = [(s.shape if k == "exchange" else (N_DEV, *s.shape)) for k, s in zip(kind, srcs)]
    lands = [lax.empty(ls, s.dtype) for ls, s in zip(land_shapes, srcs)] if lands is None else lands

    def body(*refs):
        sems = refs[2 * n + len(extra):]
        for cp in _peer_copies(kind, refs[:n], refs[n:2 * n], sems[0], sems[1]):
            cp.start()
        refs[-1][...] = jnp.zeros_like(refs[-1])

    def hbm(a):
        return pltpu.with_memory_space_constraint(a, pltpu.HBM)

    out = pl.pallas_call(
        body, name=name,
        out_shape=(pltpu.SemaphoreType.DMA((7 * n,)), pltpu.SemaphoreType.DMA((7 * n,)),
                   *[pltpu.HBM(s.shape, s.dtype) for s in srcs],
                   *[pltpu.HBM(ls, s.dtype) for ls, s in zip(land_shapes, srcs)],
                   jax.ShapeDtypeStruct((8, LANES), F32)),
        in_specs=[HBM] * (2 * n) + [ANY] * len(extra),
        out_specs=(SEM, SEM, *[HBM] * (2 * n), pl.BlockSpec(memory_space=pltpu.VMEM)),
        input_output_aliases={i: 2 + i for i in range(2 * n)},
        compiler_params=pltpu.CompilerParams(has_side_effects=DATAFLOW),
    )(*[hbm(s) for s in srcs], *[hbm(a) for a in lands], *extra)
    return (kind, n, out[:-1]), out[-1]


def _copies_wait(state, after, name):
    kind, n, (send_sems, recv_sems, *thru) = state
    after = list(after) if isinstance(after, (list, tuple)) else [after]

    def body(*refs):
        for cp in _peer_copies(kind, refs[:n], refs[n:2 * n], refs[2 * n], refs[2 * n + 1]):
            cp.wait_send()
            cp.wait_recv()

    out = pl.pallas_call(
        body, name=name,
        out_shape=tuple(pltpu.HBM(t.shape, t.dtype) for t in thru),
        in_specs=[HBM] * (2 * n) + [SEM, SEM] + [ANY] * len(after), out_specs=tuple([HBM] * (2 * n)),
        input_output_aliases={i: i for i in range(2 * n)},
        compiler_params=pltpu.CompilerParams(has_side_effects=DATAFLOW),
    )(*thru, send_sems, recv_sems, *after)
    return out[:n], out[n:]


def _adamw(w, g, m, v):
    m2 = ADAM_B1 * m + (1.0 - ADAM_B1) * g
    v2 = ADAM_B2 * v + (1.0 - ADAM_B2) * (g * g)
    m_hat = m2 / (1.0 - ADAM_B1 ** ADAM_STEP)
    v_hat = v2 / (1.0 - ADAM_B2 ** ADAM_STEP)
    delta = -ADAM_LR * (m_hat / (jnp.sqrt(v_hat) + ADAM_EPS) + ADAM_WD * w)
    return delta, m2, v2


def _sum_adamw(land, part, me_idx, w, m, v, name, tile=256):
    r, c = w.shape
    tr = min(tile, r)

    def body(me_ref, own_ref, *refs):
        slots = refs[:N_DEV]
        w_ref, m_ref, v_ref, g_ref, d_ref, m2_ref, v2_ref = refs[N_DEV:]
        own = own_ref[...].astype(F32)
        g = None
        for s in range(N_DEV):
            term = jnp.where(me_ref[0] == s, own, slots[s][...].astype(F32))
            g = term if g is None else g + term
        d, m2, v2 = _adamw(w_ref[...], g, m_ref[...], v_ref[...])
        g_ref[...] = g
        d_ref[...] = d
        m2_ref[...] = m2
        v2_ref[...] = v2

    def slot_spec(s):
        return pl.BlockSpec((None, tr, c), lambda i, me: (jnp.where(me[0] == s, (s + 1) % N_DEV, s), i, 0))

    row = pl.BlockSpec((tr, c), lambda i, me: (i, 0))
    return pl.pallas_call(
        body, name=name,
        grid_spec=pltpu.PrefetchScalarGridSpec(
            num_scalar_prefetch=1, grid=(r // tr,),
            in_specs=[pl.BlockSpec((None, tr, c), lambda i, me: (me[0], i, 0))] + [slot_spec(s) for s in range(N_DEV)] + [row] * 3,
            out_specs=[row] * 4),
        out_shape=[jax.ShapeDtypeStruct((r, c), F32)] * 4,
        compiler_params=_cparams("parallel"),
    )(me_idx, part, *[land] * N_DEV, w, m, v)


def _small_update(name, me_idx, kinds, lands, owns, ws, ms, vs, sums=()):
    n = len(ws)
    lands, owns = list(lands) + [s[0] for s in sums], list(owns) + [s[1] for s in sums]
    kinds = list(kinds) + ["gather"] * len(sums)
    nl = len(lands)

    def summed(me, land_ref, own):
        g = None
        for s in range(N_DEV):
            term = jnp.where(me == s, own, land_ref[s])
            g = term if g is None else g + term
        return g

    def body(me_ref, *refs):
        land_refs, own_refs = refs[:nl], refs[nl:2 * nl]
        w_refs, m_refs, v_refs = (refs[2 * nl + i * n:2 * nl + (i + 1) * n] for i in range(3))
        outs = refs[2 * nl + 3 * n:]
        me = me_ref[0]
        for i in range(n):
            g = summed(me, land_refs[i], own_refs[i][...])
            d, m2, v2 = _adamw(w_refs[i][...], g, m_refs[i][...], v_refs[i][...])
            for ref, val in zip(outs[4 * i:4 * i + 4], (g, d, m2, v2)):
                ref[...] = val
        for i in range(n, nl):
            outs[4 * n + i - n][...] = summed(me, land_refs[i], own_refs[i][...])

    def whole(shape):
        return pl.BlockSpec(shape, lambda i, me, nd=len(shape): (0,) * nd)

    def own_spec(kind, own):
        if kind == "gather":
            return whole(own.shape)
        return pl.BlockSpec((None, *own.shape[1:]), lambda i, me: (me[0], 0, 0))

    shapes = [w.shape for w in ws]
    out_shapes = [s for s in shapes for _ in range(4)] + [s[1].shape for s in sums]
    return pl.pallas_call(
        body, name=name,
        grid_spec=pltpu.PrefetchScalarGridSpec(
            num_scalar_prefetch=1, grid=(1,),
            in_specs=[whole(a.shape) for a in lands] + [own_spec(k, o) for k, o in zip(kinds, owns)]
            + [whole(s) for s in shapes] * 3,
            out_specs=[whole(s) for s in out_shapes]),
        out_shape=[jax.ShapeDtypeStruct(s, F32) for s in out_shapes],
        compiler_params=_cparams("arbitrary"),
    )(me_idx, *lands, *owns, *ws, *ms, *vs)


def _small_view(n, a):
    if a.ndim == 1:
        return a.reshape(1, -1)
    if a.ndim == 3:
        return a.transpose(1, 2, 0).reshape(QKV_BLOCK * QKV_BLOCK, -1)
    return a.T if n == "w_if" else a


def _small_unview(n, a, shape):
    if len(shape) == 1:
        return a.reshape(shape)
    if len(shape) == 3:
        return a.reshape(QKV_BLOCK, QKV_BLOCK, -1).transpose(2, 0, 1)
    return a.T if n == "w_if" else a


def _small_shards(n, g):
    if n == "w_if":
        return g.reshape(N_DEV, -1, g.shape[1]).transpose(0, 2, 1)
    return g.reshape(g.shape[0], N_DEV, -1).transpose(1, 0, 2)


def _small_unshard(n, s):
    if n == "w_if":
        return s.transpose(0, 2, 1).reshape(-1, s.shape[1])
    return s.transpose(1, 0, 2).reshape(s.shape[1], -1)


def _to_hm(a, d):
    t = a.shape[0]
    return a.reshape(t, HEADS, d).transpose(1, 0, 2)


def _from_hm(a):
    h, t, d = a.shape
    return a.transpose(1, 0, 2).reshape(t, h * d)


def _gate_rows(g):
    t = g.shape[0]
    return g.T.reshape(HEADS, t // CHUNK, 1, CHUNK)


def _gate_cols(g):
    h, nc, _, c = g.shape
    return g.reshape(h, nc * c).T


def _blockdiag_dense(w):
    n = w.shape[0] * QKV_BLOCK
    tiled = jnp.tile(w.reshape(n, QKV_BLOCK), (1, n // QKV_BLOCK))
    r = lax.broadcasted_iota(jnp.int32, (n, n), 0)
    c = lax.broadcasted_iota(jnp.int32, (n, n), 1)
    return jnp.where(r // QKV_BLOCK == c // QKV_BLOCK, tiled, 0.0)


def _blockdiag_blocks(dense):
    n = dense[0].shape[0]
    k = len(dense)

    def body(*refs):
        r = lax.broadcasted_iota(jnp.int32, (n, n), 0)
        c = lax.broadcasted_iota(jnp.int32, (n, n), 1)
        fr = lax.broadcasted_iota(jnp.int32, (n, LANES), 0)
        fc = lax.broadcasted_iota(jnp.int32, (n, LANES), 1)
        fold = ((fr & (QKV_BLOCK - 1)) == fc).astype(BF16)
        for i in range(k):
            kept = jnp.where((r >> 2) == (c >> 2), refs[i][...], 0.0)
            refs[k + i][...] = sum(lax.dot_general(t, fold, _dims("nn", 2), preferred_element_type=F32) for t in _split3(kept))

    out = pl.pallas_call(body, name="blockdiag_blocks", out_shape=[jax.ShapeDtypeStruct((n, LANES), F32)] * k)(*dense)
    return [o[:, 0:QKV_BLOCK].reshape(n // QKV_BLOCK, QKV_BLOCK, QKV_BLOCK) for o in out]


def _col_blocks(w):
    k, n = w.shape
    return w.reshape(k, N_DEV, n // N_DEV).transpose(1, 0, 2)


def _from_col_blocks(g):
    d, k, n = g.shape
    return g.transpose(1, 0, 2).reshape(k, d * n)


def _local_step(x, tgt, weight, ws, prefetch, on_grads, on_small):
    t, d = x.shape
    g1 = ws["g_pre_mix"]

    def dep(token):
        return () if token is None else (token,)

    w_in = weight("w_in", x)
    fetch_mix = prefetch(("w_pa", "w_pb", "w_o"), w_in)

    def proj_in_fwd(xv, g, w):
        hv = _rms(xv, g)
        return (hv, _raw_dot(hv, w, "nn")), ()

    h, proj = _rowwise("proj_in", proj_in_fwd, [x], [g1, w_in], [(d, BF16), (w_in.shape[1], F32)], deps=dep(fetch_mix))
    offs = [0]
    for s in IN_SPLITS:
        offs.append(offs[-1] + s)
    q_a, k_a, v_a, g_a, a_low, x_m, o_pre, gate_a, gate_b = [proj[:, offs[i]:offs[i + 1]] for i in range(9)]

    a_low_p = jnp.pad(a_low, ((0, 0), (0, LANES - LOWRANK)))
    w_a_up_p = jnp.pad(ws["w_a_up"], ((0, LANES - LOWRANK), (0, 0)))
    b_a_up = ws["b_a_up"]
    (la,) = _rowwise("gla_decay", lambda al, w, b: ((_log_decay(al, w, b),), ()), [a_low_p], [w_a_up_p, b_a_up],
                     [(HEADS * GLA_DK, F32)])
    fetch_up = prefetch(("w_up",), la)
    q_hm, k_hm, la_hm = _to_hm(q_a, GLA_DK), _to_hm(k_a, GLA_DK), _to_hm(la, GLA_DK)
    o_gla, s_prev = _gla_fwd(q_hm, k_hm, v_a, la_hm)
    gn = ws["g_gla_norm"]
    ml_w = HEADS * HEAD_W

    def proj_a_fwd(o, g, n_, w):
        ya = jnp.concatenate(_per_head(_gla_out, [o, g], [], [n_]), axis=1)
        return (ya, _raw_dot(ya, w, "nn")), ()

    ya_in, y_a = _rowwise("proj_a", proj_a_fwd, [o_gla, g_a], [gn, weight("w_pa", o_gla)], [(ml_w, BF16), (d, F32)],
                          tile=512, deps=dep(fetch_up))

    cw = ws["conv_w"]
    w_if_p = jnp.pad(ws["w_if"], ((0, 0), (0, LANES - 2 * HEADS)))
    pre_params = [cw[0:1], cw[1:2], cw[2:3], cw[3:4], ws["conv_b"],
                  _blockdiag_dense(ws["w_q_ml"]), _blockdiag_dense(ws["w_k_ml"]), _blockdiag_dense(ws["w_v_ml"]),
                  w_if_p[0:ml_w], w_if_p[ml_w:2 * ml_w], w_if_p[2 * ml_w:3 * ml_w],
                  jnp.pad(ws["b_if"], ((0, 0), (0, LANES - 2 * HEADS)))]
    x_pad = jnp.pad(x_m, ((HALO, 0), (0, 0)))
    xc, q_m, k_m, v_m, gl = _ml_pre_fwd(x_m, x_pad, pre_params)
    li, lf = _gate_rows(gl[:, 0:HEADS]), _gate_rows(gl[:, HEADS:2 * HEADS])
    hc, c_prev, n_prev, m_prev = _ml_fwd(q_m, k_m, v_m, li, lf)
    fetch_down = prefetch(("w_down",), hc)
    g_ml, skip = ws["g_ml_norm"], ws["ml_skip"]

    def proj_b_fwd(a, b, c_, ga, gb, ya, g, s, w):
        hb = jnp.concatenate(_per_head(_ml_out, [a, b, c_], [g, s]), axis=1)
        yb = _raw_dot(hb, w, "nn")
        return (hb, yb, _merge(ga, gb, ya, yb)), ()

    h_b, y_b, merged = _rowwise("proj_b", proj_b_fwd, [hc, o_pre, xc, gate_a, gate_b, y_a], [g_ml, skip, weight("w_pb", hc)],
                                [(ml_w, BF16), (d, F32), (d, BF16)], tile=512, deps=dep(fetch_down))

    gpm, gpl, gpo = ws["g_post_mix"], ws["g_pre_mlp"], ws["g_post_mlp"]

    def proj_o_fwd(mg, xv, w, a, b):
        zv = _raw_dot(mg, w, "nn")
        return (zv, *_post_mix(xv, zv, a, b)), ()

    z, x1, h2 = _rowwise("proj_o", proj_o_fwd, [merged, x], [weight("w_o", merged), gpm, gpl],
                         [(d, F32), (d, F32), (d, BF16)], tile=512)
    up, u = _mm(h2, weight("w_up", h2), "nn", (BF16, BF16), "mlp_up", epilogue=lambda p: (p, jnp.square(jnp.maximum(p, 0.0))))

    def mlp_down_loss(uv, x1v, tgtv, w, g):
        dnv = _raw_dot(uv, w, "nn")
        loss, vjp = jax.vjp(lambda a, b, c_: _loss_rows(a, b, tgtv, c_), x1v, dnv, g)
        dx1, ddn, dg = vjp(jnp.ones((1, 1), F32))
        return (dx1, ddn), (jnp.broadcast_to(loss, (1, LANES)), dg)

    dx1_y, d_dn, loss, d_gpo = _rowwise("mlp_down", mlp_down_loss, [u, x1, tgt], [weight("w_down", u), gpo],
                                        [(d, F32), (d, BF16)], [((1, LANES), F32), ((1, d), F32)], tile=512)

    (d_up,) = _mm(d_dn, weight("w_down", u), "nt", (BF16,), "mlp_down_dx", extra=[up],
                  epilogue=lambda p, a: (p * (2.0 * jnp.maximum(a.astype(F32), 0.0)),))
    dw_down = _mm(u, d_dn, "tn", BF16, "mlp_down_dw", tm=512)
    dw_up = _mm(h2, d_up, "tn", BF16, "mlp_up_dw")
    sent_mlp = on_grads(dict(w_down=dw_down, w_up=dw_up))

    def mlp_up_dx(dup, xv, zv, dx1, w, a, b):
        _, vjp = jax.vjp(_post_mix, xv, zv, a, b)
        dx, dz, da, db = vjp((dx1, _raw_dot(dup, w, "nt")))
        return (dx, dz), (da, db)

    dx_res, d_z, d_gpm, d_gpl = _rowwise("mlp_up_dx", mlp_up_dx, [d_up, x, z, dx1_y], [weight("w_up", h2), gpm, gpl],
                                         [(d, F32), (d, BF16)], [((1, d), F32), ((1, d), F32)], tile=512, deps=dep(sent_mlp))
    dw_o = _mm(merged, d_z, "tn", BF16, "proj_o_dw")

    def proj_o_dx(dz, ga, gb, ya, yb, w):
        return jax.vjp(_merge, ga, gb, ya, yb)[1](_raw_dot(dz, w, "nt")), ()

    d_ga, d_gb, d_ya, d_yb = _rowwise("proj_o_dx", proj_o_dx, [d_z, gate_a, gate_b, y_a, y_b], [weight("w_o", merged)],
                                      [(d, F32), (d, F32), (d, BF16), (d, BF16)], tile=512)
    dw_pa = _mm(ya_in, d_ya, "tn", BF16, "proj_a_dw")
    dw_pb = _mm(h_b, d_yb, "tn", BF16, "proj_b_dw")
    sent_mix = on_grads(dict(w_o=dw_o, w_pa=dw_pa, w_pb=dw_pb))

    def proj_b_dx(dyb, a, b, c_, w, g, s):
        ct = _raw_dot(dyb, w, "nt")
        parts = []
        for hs in _head_slices(HEAD_W):
            _, vjp = jax.vjp(_ml_out, a[:, hs], b[:, hs], c_[:, hs], g[:, hs], s[:, hs])
            parts.append(vjp(ct[:, hs]))
        cat = lambda i: jnp.concatenate([p[i] for p in parts], axis=1)
        return (cat(0), cat(1), cat(2)), (cat(3), cat(4))

    d_hc, d_opre, d_xc, d_gml, d_skip = _rowwise("proj_b_dx", proj_b_dx, [d_yb, hc, o_pre, xc],
                                                 [weight("w_pb", hc), g_ml, skip], [(ml_w, F32)] * 3, [((1, ml_w), F32)] * 2,
                                                 tile=512, deps=dep(sent_mix))
    d_qm, d_km, d_vm, d_li, d_lf = _ml_bwd(q_m, k_m, v_m, li, lf, c_prev, n_prev, m_prev, d_hc)
    d_gl = jnp.concatenate([_gate_cols(d_li), _gate_cols(d_lf), jnp.zeros((t, LANES - 2 * HEADS), F32)], axis=1)
    pre_grads = _ml_pre_bwd(x_m, x_pad, pre_params, [d_xc, d_qm, d_km, d_vm, d_gl])
    d_xm = pre_grads[0]
    d_cw = jnp.concatenate(pre_grads[1:5], axis=0)
    d_cb = pre_grads[5]
    d_wq, d_wk, d_wv = _blockdiag_blocks(pre_grads[6:9])
    d_wif = jnp.concatenate(pre_grads[9:12], axis=0)[:, 0:2 * HEADS]
    d_bif = pre_grads[12][:, 0:2 * HEADS]

    def proj_a_dx(dya, o, g, w, n_):
        ct = _raw_dot(dya, w, "nt")
        parts = []
        for hs in _head_slices(HEAD_W):
            _, vjp = jax.vjp(_gla_out, o[:, hs], g[:, hs], n_)
            parts.append(vjp(ct[:, hs]))
        cat = lambda i: jnp.concatenate([p[i] for p in parts], axis=1)
        return (cat(0), cat(1)), (sum(p[2] for p in parts),)

    d_o, d_g_a, d_gn = _rowwise("proj_a_dx", proj_a_dx, [d_ya, o_gla, g_a], [weight("w_pa", o_gla), gn], [(ml_w, F32)] * 2,
                                [((1, HEAD_W), F32)], tile=512)
    dq_hm, dk_hm, d_va, dla_hm = _gla_bwd(q_hm, k_hm, v_a, la_hm, s_prev, d_o)

    def decay_bwd(al, ct, w, b):
        _, vjp = jax.vjp(_log_decay, al, w, b)
        dal, dw, db = vjp(ct)
        return (dal,), (dw, db)

    d_alow_p, d_wa_p, d_ba = _rowwise("gla_decay_bwd", decay_bwd, [a_low_p, _from_hm(dla_hm)], [w_a_up_p, b_a_up],
                                      [(LANES, F32)], [(w_a_up_p.shape, F32), (b_a_up.shape, F32)])
    d_proj = jnp.concatenate([_from_hm(dq_hm), _from_hm(dk_hm), d_va, d_g_a, d_alow_p[:, 0:LOWRANK], d_xm, d_opre, d_ga, d_gb],
                             axis=1).astype(BF16)
    small = dict(w_a_up=d_wa_p[0:LOWRANK], b_a_up=d_ba, g_gla_norm=d_gn, conv_w=d_cw, conv_b=d_cb,
                 w_q_ml=d_wq, w_k_ml=d_wk, w_v_ml=d_wv, w_if=d_wif, b_if=d_bif, ml_skip=d_skip, g_ml_norm=d_gml,
                 g_post_mix=d_gpm, g_pre_mlp=d_gpl, g_post_mlp=d_gpo)
    sent_small = on_small(small, loss)
    dw_in = _mm(h, d_proj, "tn", F32, "proj_in_dw", tm=512, tk=1024, deps=dep(sent_small))
    sent_in = on_grads(dict(w_in=dw_in))

    def proj_in_dx(dp, xv, dres, w, g):
        _, vjp = jax.vjp(_rms, xv, g)
        dx, dg = vjp(_raw_dot(dp, w, "nt"))
        return (dx + dres,), (dg,)

    grad_x, d_g1 = _rowwise("proj_in_dx", proj_in_dx, [d_proj, x, dx_res], [w_in, g1], [(d, F32)], [((1, d), F32)],
                            deps=dep(sent_in))
    return grad_x, on_small(dict(g_pre_mix=d_g1), None)


BIG = ("w_in", "w_pa", "w_pb", "w_o", "w_up", "w_down")
BIG_COL_SHARDED = ("w_in", "w_pa", "w_pb", "w_up")
SMALL_SHARDED = ("w_a_up", "conv_w", "w_if")
SMALL = ("g_pre_mix", "w_a_up", "b_a_up", "g_gla_norm", "conv_w", "conv_b", "w_q_ml", "w_k_ml", "w_v_ml", "w_if", "b_if",
         "ml_skip", "g_ml_norm", "g_post_mix", "g_pre_mlp", "g_post_mlp")
WEIGHTS = ("g_pre_mix", "w_in", "w_a_up", "b_a_up", "g_gla_norm", "conv_w", "conv_b", "w_q_ml", "w_k_ml", "w_v_ml", "w_if", "b_if",
           "ml_skip", "g_ml_norm", "w_pa", "w_pb", "w_o", "g_post_mix", "g_pre_mlp", "w_up", "w_down", "g_post_mlp")


def kernel(x, g_pre_mix, w_in, w_a_up, b_a_up, g_gla_norm, conv_w, conv_b, w_q_ml, w_k_ml, w_v_ml, w_if, b_if, ml_skip, g_ml_norm, w_pa, w_pb, w_o, g_post_mix, g_pre_mlp, w_up, w_down, g_post_mlp, loss_target, m_g_pre_mix, m_w_in, m_w_a_up, m_b_a_up, m_g_gla_norm, m_conv_w, m_conv_b, m_w_q_ml, m_w_k_ml, m_w_v_ml, m_w_if, m_b_if, m_ml_skip, m_g_ml_norm, m_w_pa, m_w_pb, m_w_o, m_g_post_mix, m_g_pre_mlp, m_w_up, m_w_down, m_g_post_mlp, v_g_pre_mix, v_w_in, v_w_a_up, v_b_a_up, v_g_gla_norm, v_conv_w, v_conv_b, v_w_q_ml, v_w_k_ml, v_w_v_ml, v_w_if, v_b_if, v_ml_skip, v_g_ml_norm, v_w_pa, v_w_pb, v_w_o, v_g_post_mix, v_g_pre_mlp, v_w_up, v_w_down, v_g_post_mlp):
    args = dict(locals())
    w = {n: args[n][0] for n in WEIGHTS}
    m = {n: args["m_" + n][0] for n in WEIGHTS}
    v = {n: args["v_" + n][0] for n in WEIGHTS}

    me_lin = _lin(_me())
    me_idx = jnp.reshape(me_lin, (1,)).astype(jnp.int32)

    def full_weight(n, g):
        return _from_col_blocks(g) if n in BIG_COL_SHARDED else g.reshape(-1, g.shape[-1])

    def grad_parts(n, g):
        return (_col_blocks(g) if n in BIG_COL_SHARDED else g.reshape(N_DEV, -1, g.shape[-1])).astype(BF16)

    sharded_names = tuple(SMALL_SHARDED)
    small_w_state, small_w_token = _copies_start("gather", [_small_view(n, w[n]) for n in sharded_names],
                                                 "allgather_start_small_weights")
    ready = {"w_in": full_weight("w_in", _allgather_big([w["w_in"].astype(BF16)], "allgather_w_in", [small_w_token])[0])}
    pending = {}

    def prefetch(group, after):
        state, token = _copies_start("gather_chips", [w[n].astype(BF16) for n in group], "allgather_start_" + group[0], after)
        for n in group:
            pending[n] = (group, state)
        return token

    def weight(n, after):
        if n not in ready:
            group, state = pending[n]
            shards, lands = _copies_wait(state, after, "allgather_wait_" + group[0])
            state, token = _copies_start("gather_pass", shards, "allgather_pass_" + group[0], lands=lands)
            shards, lands = _copies_wait(state, token, "allgather_passed_" + group[0])
            for gn, shard, land in zip(group, shards, lands):
                ready[gn] = full_weight(gn, lax.dynamic_update_slice(land, shard[None], (me_lin, 0, 0)))
        return ready[n]

    small_w_own, small_w_lands = _copies_wait(small_w_state, ready["w_in"], "allgather_wait_small_weights")
    ws = {n: (w[n].reshape(1, -1) if w[n].ndim == 1 else w[n]) for n in SMALL if n not in SMALL_SHARDED}
    for n, own, land in zip(sharded_names, small_w_own, small_w_lands):
        ws[n] = _small_unshard(n, lax.dynamic_update_slice(land, own[None], (me_lin, 0, 0)))

    sent = []

    def on_grads(grads):
        names = tuple(grads)
        state, token = _copies_start("exchange", [grad_parts(n, grads[n]) for n in names], "exchange_start_" + names[0])
        sent.append((names, state))
        return token

    small_sent = []

    def on_small(small, loss):
        names = tuple(small)
        kinds = ["exchange" if n in SMALL_SHARDED else "gather" for n in names]
        srcs = [_small_shards(n, small[n]) if n in SMALL_SHARDED else _small_view(n, small[n]) for n in names]
        extra = [] if loss is None else [loss]
        state, token = _copies_start(kinds + ["gather"] * len(extra), srcs + extra, "allgather_start_small_" + names[0])
        small_sent.append((names, kinds, state))
        return token

    grad_x, last_token = _local_step(x[0], loss_target[0], weight, ws, prefetch, on_grads, on_small)

    out = {}

    def finish(names, state, after):
        parts, lands = _copies_wait(state, after, "exchange_wait_" + names[0])
        for n, part, land in zip(names, parts, lands):
            out[n] = _sum_adamw(land, part, me_idx, w[n], m[n], v[n], "adamw_" + n)

    def finish_small(names, kinds, state, after):
        own, lands = _copies_wait(state, after, "allgather_wait_small_" + names[0])
        k = len(names)
        upd = _small_update("adamw_small_" + names[0], me_idx, kinds, lands[:k], own[:k],
                            *[[_small_view(n, d[n]) for n in names] for d in (w, m, v)], sums=list(zip(lands[k:], own[k:])))
        for i, n in enumerate(names):
            out[n] = tuple(_small_unview(n, a, w[n].shape) for a in upd[4 * i:4 * i + 4])
        return upd[4 * k:]

    (loss_sum,) = finish_small(*small_sent[0], [grad_x, last_token])
    for names, state in sent[:-1]:
        finish(names, state, [grad_x, last_token])
    finish(*sent[-1], [loss_sum] + [out[n][1] for n in BIG if n in out])
    finish_small(*small_sent[1], [out["w_in"][1]])

    shaped = lambda a, n: a.reshape(args[n].shape)
    return (loss_sum[0, 0], grad_x[None],
            *[shaped(out[n][0], n) for n in WEIGHTS], *[shaped(out[n][1], n) for n in WEIGHTS],
            *[shaped(out[n][2], n) for n in WEIGHTS], *[shaped(out[n][3], n) for n in WEIGHTS])
```

```python
import functools

import jax
import jax.numpy as jnp
from jax import lax
from jax.experimental import pallas as pl
from jax.experimental.pallas import tpu as pltpu

F32 = jnp.float32
BF16 = jnp.bfloat16
MESH = pl.DeviceIdType.MESH

N_DEV = 8
EPS = 1e-6
CHUNK = 64
CHUNKS_PER_STEP = 4
HEADS = 4
GLA_DK = 64
HEAD_W = 128
GLA_GATE_NORM = 16.0
LOWRANK = 16
CONV_K = 4
QKV_BLOCK = 4
LANES = 128
HALO = 8
IN_SPLITS = (256, 256, 512, 512, 16, 512, 512, 1024, 1024)

ADAM_LR = 0.001
ADAM_B1 = 0.9
ADAM_B2 = 0.999
ADAM_EPS = 1e-08
ADAM_WD = 0.01
ADAM_STEP = 10

VMEM_LIMIT = 56 * 1024 * 1024


def _cparams(*sem):
    return pltpu.CompilerParams(dimension_semantics=sem, vmem_limit_bytes=VMEM_LIMIT)


def _dims(mode, ndim):
    contract = {"nn": ((ndim - 1,), (ndim - 2,)), "nt": ((ndim - 1,), (ndim - 1,)), "tn": ((ndim - 2,), (ndim - 2,))}[mode]
    return contract, (((0,), (0,)) if ndim == 3 else ((), ()))


def _raw_dot(a, b, mode):
    return lax.dot_general(a.astype(BF16), b.astype(BF16), _dims(mode, a.ndim), preferred_element_type=F32)


@functools.partial(jax.custom_vjp, nondiff_argnums=(2,))
def _bdot(a, b, mode):
    return _raw_dot(a, b, mode)


def _bdot_fwd(a, b, mode):
    return _raw_dot(a, b, mode), (a, b)


def _bdot_bwd(mode, res, ct):
    a, b = res
    if mode == "nn":
        da, db = _raw_dot(ct, b, "nt"), _raw_dot(a, ct, "tn")
    elif mode == "nt":
        da, db = _raw_dot(ct, b, "nn"), _raw_dot(ct, a, "tn")
    else:
        da, db = _raw_dot(b, ct, "nt"), _raw_dot(a, ct, "nn")
    return da.astype(a.dtype), db.astype(b.dtype)


_bdot.defvjp(_bdot_fwd, _bdot_bwd)


def _split3(x):
    hi = x.astype(BF16)
    r1 = x - hi.astype(F32)
    mid = r1.astype(BF16)
    return hi, mid, (r1 - mid.astype(F32)).astype(BF16)


def _split_dot(tri, x):
    if x.ndim == 3:
        tri = jnp.broadcast_to(tri, (x.shape[0], *tri.shape))
    return sum(lax.dot_general(tri, t, _dims("nn", x.ndim), preferred_element_type=F32) for t in _split3(x))


def _tri(n, lower):
    r = lax.broadcasted_iota(jnp.int32, (n, n), 0)
    c = lax.broadcasted_iota(jnp.int32, (n, n), 1)
    return ((c <= r) if lower else (c >= r)).astype(BF16)


@jax.custom_vjp
def _cumsum_rows(x):
    return _split_dot(_tri(x.shape[-2], True), x)


def _cumsum_rows_fwd(x):
    return _cumsum_rows(x), None


def _cumsum_rows_bwd(_, ct):
    return (_split_dot(_tri(ct.shape[-2], False), ct),)


_cumsum_rows.defvjp(_cumsum_rows_fwd, _cumsum_rows_bwd)


def _abs(x):
    return jnp.where(x >= 0, x, -x)


def _sigmoid(x):
    return lax.logistic(x)


def _log_sigmoid(x):
    return jnp.minimum(x, 0.0) - jnp.log(1.0 + jnp.exp(-_abs(x)))


def _rms(x, g):
    return x * lax.rsqrt(jnp.mean(x * x, axis=-1, keepdims=True) + EPS) * g


def _head_slices(w):
    return [slice(h * w, (h + 1) * w) for h in range(HEADS)]


def _heads(ref, rows=slice(None)):
    return jnp.stack([ref[rows, hs] for hs in _head_slices(HEAD_W)])


def _put_heads(ref, val, rows=slice(None)):
    for h, hs in enumerate(_head_slices(HEAD_W)):
        ref[rows, hs] = val[h]


def _tile(dim, want):
    if dim <= want or dim % LANES:
        return dim
    t = want
    while dim % t:
        t -= LANES
    return t


def _mm(a, b, mode, out_dtype, name, tm=1024, tn=1024, tk=4096, epilogue=None, extra=(), deps=()):
    if mode == "nn":
        (m, k), (k2, n) = a.shape, b.shape
    elif mode == "nt":
        (m, k), (n, k2) = a.shape, b.shape
    else:
        (k, m), (k2, n) = a.shape, b.shape
    assert k == k2, (name, a.shape, b.shape)
    tm, tn, tk = _tile(m, tm), _tile(n, tn), _tile(k, tk)
    nk = k // tk
    out_dtypes = out_dtype if epilogue else (out_dtype,)
    assert nk == 1 or (out_dtype == F32 and not epilogue), name
    n_in = 2 + len(extra)

    def body(*refs):
        p = _raw_dot(refs[0][...], refs[1][...], mode)
        if nk > 1:
            _accumulate(pl.program_id(2), [refs[n_in + len(deps)]], [p])
            return
        outs = epilogue(p, *[r[...] for r in refs[2:n_in]]) if epilogue else (p,)
        for ref, val in zip(refs[n_in + len(deps):], outs):
            ref[...] = val.astype(ref.dtype)

    a_spec = pl.BlockSpec((tk, tm), lambda i, j, kk: (kk, i)) if mode == "tn" else pl.BlockSpec((tm, tk), lambda i, j, kk: (i, kk))
    b_spec = pl.BlockSpec((tn, tk), lambda i, j, kk: (j, kk)) if mode == "nt" else pl.BlockSpec((tk, tn), lambda i, j, kk: (kk, j))
    o_spec = pl.BlockSpec((tm, tn), lambda i, j, kk: (i, j))
    res = pl.pallas_call(
        body, name=name, grid=(m // tm, n // tn, nk),
        in_specs=[a_spec, b_spec] + [o_spec] * len(extra) + [ANY] * len(deps), out_specs=[o_spec] * len(out_dtypes),
        out_shape=[jax.ShapeDtypeStruct((m, n), dt) for dt in out_dtypes],
        compiler_params=_cparams("parallel", "parallel", "arbitrary"),
    )(a, b, *extra, *deps)
    return res if epilogue else res[0]


def _mm_shard_cols(a, b, n, name, tm=512, tk=1024, deps=()):
    t, k = a.shape
    nb = b.shape[1] // n
    tk = min(tk, t)
    nk = t // tk

    def body(a_ref, b_ref, *rest):
        o_ref, acc_ref = rest[len(deps):]
        step = pl.program_id(1)
        a_t = a_ref[...].astype(BF16).T
        for j in range(nb):
            _accumulate(step, [acc_ref.at[j]], [_raw_dot(a_t, b_ref[:, j * n:(j + 1) * n], "nn")])

        @pl.when(step == nk - 1)
        def _():
            o_ref[...] = acc_ref[...].astype(BF16)

    return pl.pallas_call(
        body, name=name, grid=(k // tm, nk),
        in_specs=[pl.BlockSpec((tk, tm), lambda i, s: (s, i)), pl.BlockSpec((tk, nb * n), lambda i, s: (s, 0))] + [ANY] * len(deps),
        out_specs=pl.BlockSpec((nb, tm, n), lambda i, s: (0, i, 0)),
        out_shape=jax.ShapeDtypeStruct((nb, k, n), BF16),
        scratch_shapes=[pltpu.VMEM((nb, tm, n), F32)],
        compiler_params=_cparams("parallel", "arbitrary"),
    )(a, b, *deps)


def _rowwise(name, fn, rows, params, out_rows, out_accs=(), tile=256, deps=()):
    t = rows[0].shape[0]
    r = min(tile, t)
    assert t % r == 0
    n_in, n_or = len(rows) + len(params), len(out_rows)
    n_all = n_in + len(deps)
    params = list(params) + list(deps)

    def body(*refs):
        vals = [ref[...] for ref in refs[:n_in]]
        outs = refs[n_all:]
        ro, ao = fn(*vals)
        for ref, v in zip(outs[:n_or], ro):
            ref[...] = v.astype(ref.dtype)
        if out_accs:
            _accumulate(pl.program_id(0), outs[n_or:], ao)

    def full(shape):
        return pl.BlockSpec(shape, lambda i, nd=len(shape): (0,) * nd)

    return pl.pallas_call(
        body, name=name, grid=(t // r,),
        in_specs=[pl.BlockSpec((r, a.shape[1]), lambda i: (i, 0)) for a in rows] + [full(p.shape) for p in params],
        out_specs=[pl.BlockSpec((r, w), lambda i: (i, 0)) for w, _ in out_rows] + [full(s) for s, _ in out_accs],
        out_shape=[jax.ShapeDtypeStruct((t, w), dt) for w, dt in out_rows] + [jax.ShapeDtypeStruct(s, dt) for s, dt in out_accs],
        compiler_params=_cparams("arbitrary"),
    )(*rows, *params)


def _accumulate(step, refs, vals):
    for ref, v in zip(refs, vals):
        @pl.when(step == 0)
        def _(ref=ref, v=v):
            ref[...] = v.astype(ref.dtype)

        @pl.when(step > 0)
        def _(ref=ref, v=v):
            ref[...] += v.astype(ref.dtype)


def _gla_chunk(q, k, v, la, st):
    c = q.shape[-2]
    row = lax.broadcasted_iota(jnp.int32, (c, c), 0)
    col = lax.broadcasted_iota(jnp.int32, (c, c), 1)
    cum = _cumsum_rows(la)
    cl = jnp.sum(la, axis=-2, keepdims=True)
    ep = jnp.exp(cum)
    en = jnp.exp(-cum)
    qs = q * (GLA_DK ** -0.5)
    qp = qs * ep
    a_f = _bdot(qp, k * en, "nt")
    a_b = _bdot(qs * en, k * ep, "nt")
    sc = jnp.where(row >= col, a_f, a_b)
    o = _bdot(sc, v, "nn") + _bdot(qp, st, "nt")
    kd = k * jnp.exp(cl - cum)
    st_new = st * jnp.exp(cl) + _bdot(v, kd, "tn")
    return o, st_new


def _gla_specs(nc, rev):
    nb = nc // CHUNKS_PER_STEP
    rows = CHUNKS_PER_STEP * CHUNK

    def blk(n):
        return (nb - 1 - n) if rev else n
    hm = pl.BlockSpec((HEADS, rows, GLA_DK), lambda n: (0, blk(n), 0))
    tm = pl.BlockSpec((rows, HEADS * HEAD_W), lambda n: (blk(n), 0))
    st = pl.BlockSpec((HEADS, CHUNKS_PER_STEP, HEAD_W, GLA_DK), lambda n: (0, blk(n), 0, 0))
    return nb, hm, tm, st


def _chunk_rows(c):
    return slice(c * CHUNK, (c + 1) * CHUNK)


def _gla_fwd(q, k, v, la):
    t = v.shape[0]
    nc = t // CHUNK
    nb, hm, tm, st = _gla_specs(nc, False)

    def body(q_ref, k_ref, v_ref, la_ref, o_ref, sp_ref, st_ref):
        @pl.when(pl.program_id(0) == 0)
        def _():
            st_ref[...] = jnp.zeros_like(st_ref)

        s = st_ref[...]
        for c in range(CHUNKS_PER_STEP):
            r = _chunk_rows(c)
            sp_ref[:, c] = s
            o, s = _gla_chunk(q_ref[:, r], k_ref[:, r], _heads(v_ref, r), la_ref[:, r], s)
            _put_heads(o_ref, o, r)
        st_ref[...] = s

    return pl.pallas_call(
        body, name="gla_fwd", grid=(nb,),
        in_specs=[hm, hm, tm, hm], out_specs=[tm, st],
        out_shape=[jax.ShapeDtypeStruct((t, HEADS * HEAD_W), F32), jax.ShapeDtypeStruct((HEADS, nc, HEAD_W, GLA_DK), F32)],
        scratch_shapes=[pltpu.VMEM((HEADS, HEAD_W, GLA_DK), F32)],
        compiler_params=_cparams("arbitrary"),
    )(q, k, v, la)


def _gla_bwd(q, k, v, la, sp, do):
    t = v.shape[0]
    nc = t // CHUNK
    nb, hm, tm, st = _gla_specs(nc, True)

    def body(q_ref, k_ref, v_ref, la_ref, sp_ref, do_ref, dq_ref, dk_ref, dv_ref, dla_ref, ds_ref):
        @pl.when(pl.program_id(0) == 0)
        def _():
            ds_ref[...] = jnp.zeros_like(ds_ref)

        ds = ds_ref[...]
        for c in reversed(range(CHUNKS_PER_STEP)):
            r = _chunk_rows(c)
            _, vjp = jax.vjp(_gla_chunk, q_ref[:, r], k_ref[:, r], _heads(v_ref, r), la_ref[:, r], sp_ref[:, c])
            dq, dk, dv, dla, ds = vjp((_heads(do_ref, r), ds))
            dq_ref[:, r] = dq
            dk_ref[:, r] = dk
            _put_heads(dv_ref, dv, r)
            dla_ref[:, r] = dla
        ds_ref[...] = ds

    hm_shape = jax.ShapeDtypeStruct((HEADS, t, GLA_DK), F32)
    return pl.pallas_call(
        body, name="gla_bwd", grid=(nb,),
        in_specs=[hm, hm, tm, hm, st, tm], out_specs=[hm, hm, tm, hm],
        out_shape=[hm_shape, hm_shape, jax.ShapeDtypeStruct((t, HEADS * HEAD_W), F32), hm_shape],
        scratch_shapes=[pltpu.VMEM((HEADS, HEAD_W, GLA_DK), F32)],
        compiler_params=_cparams("arbitrary"),
    )(q, k, v, la, sp, do)


def _ml_chunk(q, k, v, li_r, lf_r, cm, nv, m):
    c = q.shape[-2]
    row = lax.broadcasted_iota(jnp.int32, (c, c), 0)
    col = lax.broadcasted_iota(jnp.int32, (c, c), 1)
    eye = (row == col).astype(F32)
    li_c = jnp.sum(eye * li_r, axis=-1, keepdims=True)
    lf_c = jnp.sum(eye * lf_r, axis=-1, keepdims=True)
    fc_c = jnp.sum((col <= row).astype(F32) * lf_r, axis=-1, keepdims=True)
    fc_r = jnp.sum((row <= col).astype(F32) * lf_c, axis=-2, keepdims=True)
    f_last = jnp.sum(lf_r, axis=-1, keepdims=True)
    kc = k * (HEAD_W ** -0.5)
    a_c = f_last - fc_c + li_c
    m_loc = jnp.max(a_c, axis=-2, keepdims=True)
    kw = kc * jnp.exp(a_c - m_loc)
    c_chunk = _bdot(kw, v, "tn")
    n_chunk = jnp.sum(kw, axis=-2, keepdims=True)
    m_new = jnp.maximum(f_last + m, m_loc)
    sp = jnp.exp(f_last + m - m_new)
    sl = jnp.exp(m_loc - m_new)
    cm_new = sp * cm + sl * c_chunk
    nv_new = sp * nv + sl * n_chunk
    log_d = li_r - _abs(fc_c - fc_r)
    g_inter = fc_c + m
    m_t = jnp.maximum(g_inter, jnp.max(log_d, axis=-1, keepdims=True))
    s = _bdot(q, kc, "nt") * jnp.exp(log_d - m_t)
    sc = jnp.exp(g_inter - m_t)
    num = _bdot(s, v, "nn") + sc * _bdot(q, cm, "nn")
    den = jnp.sum(s, axis=-1, keepdims=True) + sc * jnp.sum(q * nv, axis=-1, keepdims=True)
    den = jnp.maximum(_abs(den), jnp.exp(-m_t))
    return num / den, cm_new, nv_new, m_new


def _ml_specs(nc, rev):
    nb = nc // CHUNKS_PER_STEP

    def blk(n):
        return (nb - 1 - n) if rev else n
    tm = pl.BlockSpec((CHUNKS_PER_STEP * CHUNK, HEADS * HEAD_W), lambda n: (blk(n), 0))
    gate = pl.BlockSpec((HEADS, CHUNKS_PER_STEP, 1, CHUNK), lambda n: (0, blk(n), 0, 0))
    cm = pl.BlockSpec((HEADS, CHUNKS_PER_STEP, HEAD_W, HEAD_W), lambda n: (0, blk(n), 0, 0))
    vec = pl.BlockSpec((HEADS, CHUNKS_PER_STEP, 1, HEAD_W), lambda n: (0, blk(n), 0, 0))
    return nb, tm, gate, cm, vec


_ML_STATE = [pltpu.VMEM((HEADS, HEAD_W, HEAD_W), F32), pltpu.VMEM((HEADS, 1, HEAD_W), F32), pltpu.VMEM((HEADS, 1, HEAD_W), F32)]


def _ml_fwd(q, k, v, li, lf):
    t = q.shape[0]
    nc = t // CHUNK
    nb, tm, gate, cm, vec = _ml_specs(nc, False)

    def body(q_ref, k_ref, v_ref, li_ref, lf_ref, hc_ref, cp_ref, np_ref, mp_ref, c_ref, n_ref, m_ref):
        @pl.when(pl.program_id(0) == 0)
        def _():
            c_ref[...] = jnp.zeros_like(c_ref)
            n_ref[...] = jnp.zeros_like(n_ref)
            m_ref[...] = jnp.zeros_like(m_ref)

        cs, ns, ms = c_ref[...], n_ref[...], m_ref[...][:, :, 0:1]
        for c in range(CHUNKS_PER_STEP):
            r = _chunk_rows(c)
            cp_ref[:, c] = cs
            np_ref[:, c] = ns
            mp_ref[:, c] = jnp.broadcast_to(ms, m_ref.shape)
            hc, cs, ns, ms = _ml_chunk(_heads(q_ref, r), _heads(k_ref, r), _heads(v_ref, r), li_ref[:, c], lf_ref[:, c],
                                       cs, ns, ms)
            _put_heads(hc_ref, hc, r)
        c_ref[...] = cs
        n_ref[...] = ns
        m_ref[...] = jnp.broadcast_to(ms, m_ref.shape)

    return pl.pallas_call(
        body, name="mlstm_fwd", grid=(nb,),
        in_specs=[tm, tm, tm, gate, gate], out_specs=[tm, cm, vec, vec],
        out_shape=[jax.ShapeDtypeStruct((t, HEADS * HEAD_W), F32), jax.ShapeDtypeStruct((HEADS, nc, HEAD_W, HEAD_W), F32),
                   jax.ShapeDtypeStruct((HEADS, nc, 1, HEAD_W), F32), jax.ShapeDtypeStruct((HEADS, nc, 1, HEAD_W), F32)],
        scratch_shapes=_ML_STATE,
        compiler_params=_cparams("arbitrary"),
    )(q, k, v, li, lf)


def _ml_bwd(q, k, v, li, lf, cp, npv, mp, dhc):
    t = q.shape[0]
    nc = t // CHUNK
    nb, tm, gate, cm, vec = _ml_specs(nc, True)

    def body(q_ref, k_ref, v_ref, li_ref, lf_ref, cp_ref, np_ref, mp_ref, dhc_ref,
             dq_ref, dk_ref, dv_ref, dli_ref, dlf_ref, dc_ref, dn_ref, dm_ref):
        @pl.when(pl.program_id(0) == 0)
        def _():
            dc_ref[...] = jnp.zeros_like(dc_ref)
            dn_ref[...] = jnp.zeros_like(dn_ref)
            dm_ref[...] = jnp.zeros_like(dm_ref)

        dc, dn, dm = dc_ref[...], dn_ref[...], dm_ref[...][:, :, 0:1]
        for c in reversed(range(CHUNKS_PER_STEP)):
            r = _chunk_rows(c)
            _, vjp = jax.vjp(_ml_chunk, _heads(q_ref, r), _heads(k_ref, r), _heads(v_ref, r), li_ref[:, c], lf_ref[:, c],
                             cp_ref[:, c], np_ref[:, c], mp_ref[:, c][:, :, 0:1])
            dq, dk, dv, dli, dlf, dc, dn, dm = vjp((_heads(dhc_ref, r), dc, dn, dm))
            _put_heads(dq_ref, dq, r)
            _put_heads(dk_ref, dk, r)
            _put_heads(dv_ref, dv, r)
            dli_ref[:, c] = dli
            dlf_ref[:, c] = dlf
        dc_ref[...] = dc
        dn_ref[...] = dn
        dm_ref[...] = jnp.broadcast_to(dm, dm_ref.shape)

    tm_shape = jax.ShapeDtypeStruct((t, HEADS * HEAD_W), F32)
    gate_shape = jax.ShapeDtypeStruct((HEADS, nc, 1, CHUNK), F32)
    return pl.pallas_call(
        body, name="mlstm_bwd", grid=(nb,),
        in_specs=[tm, tm, tm, gate, gate, cm, vec, vec, tm], out_specs=[tm, tm, tm, gate, gate],
        out_shape=[tm_shape, tm_shape, tm_shape, gate_shape, gate_shape],
        scratch_shapes=_ML_STATE,
        compiler_params=_cparams("arbitrary"),
    )(q, k, v, li, lf, cp, npv, mp, dhc)


def _ml_pre(s0, s1, s2, s3, cw0, cw1, cw2, cw3, cb, wq, wk, wv, wiq, wik, wiv, bif):
    pre = cb + cw0 * s0 + cw1 * s1 + cw2 * s2 + cw3 * s3
    xc = pre * _sigmoid(pre)
    q = _bdot(xc, wq, "nn")
    k = _bdot(xc, wk, "nn")
    v = _bdot(s3, wv, "nn")
    gates = _bdot(q, wiq, "nn") + _bdot(k, wik, "nn") + _bdot(v, wiv, "nn") + bif
    lane = lax.broadcasted_iota(jnp.int32, gates.shape, 1)
    gl = jnp.where(lane < HEADS, gates, _log_sigmoid(gates))
    return xc, q, k, v, gl


def _delayed(xs_ref, x_ref, halo_ref, r):
    xs_ref[0:HALO, :] = halo_ref[...]
    xs_ref[HALO:HALO + r, :] = x_ref[...]
    return [xs_ref[pl.ds(HALO - (CONV_K - 1) + j, r), :] for j in range(CONV_K)]


def _full_spec(shape):
    return pl.BlockSpec(shape, lambda i, nd=len(shape): (0,) * nd)


def _ml_pre_fwd(x_m, x_pad, params, tile=256):
    t, w = x_m.shape
    r = min(tile, t)

    def body(*refs):
        x_ref, halo_ref = refs[:2]
        p = [ref[...] for ref in refs[2:2 + len(params)]]
        outs = refs[2 + len(params):-1]
        res = _ml_pre(*_delayed(refs[-1], x_ref, halo_ref, r), *p)
        for ref, val in zip(outs, res):
            ref[...] = val

    row = pl.BlockSpec((r, w), lambda i: (i, 0))
    return pl.pallas_call(
        body, name="ml_pre_fwd", grid=(t // r,),
        in_specs=[row, pl.BlockSpec((HALO, w), lambda i: (i * (r // HALO), 0))] + [_full_spec(p.shape) for p in params],
        out_specs=[row] * 4 + [pl.BlockSpec((r, LANES), lambda i: (i, 0))],
        out_shape=[jax.ShapeDtypeStruct((t, w), F32)] * 4 + [jax.ShapeDtypeStruct((t, LANES), F32)],
        scratch_shapes=[pltpu.VMEM((r + HALO, w), F32)],
        compiler_params=_cparams("arbitrary"),
    )(x_m, x_pad, *params)


def _ml_pre_bwd(x_m, x_pad, params, cts, tile=256):
    t, w = x_m.shape
    r = min(tile, t)
    nt = t // r
    n_p = len(params)

    def body(*refs):
        x_ref, halo_ref = refs[:2]
        p = [ref[...] for ref in refs[2:2 + n_p]]
        ct = [ref[...] for ref in refs[2 + n_p:7 + n_p]]
        dx_ref = refs[7 + n_p]
        dp_refs = refs[8 + n_p:8 + 2 * n_p]
        xs_ref, ds_ref, carry_ref = refs[8 + 2 * n_p:]
        step = pl.program_id(0)

        @pl.when(step == 0)
        def _():
            ds_ref[...] = jnp.zeros_like(ds_ref)
            carry_ref[...] = jnp.zeros_like(carry_ref)

        _, vjp = jax.vjp(_ml_pre, *_delayed(xs_ref, x_ref, halo_ref, r), *p)
        grads = vjp(tuple(ct))
        for j in range(CONV_K):
            ds_ref[j, HALO:HALO + r, :] = grads[j]
        lead = HALO + CONV_K - 1
        d_tile = sum(ds_ref[j, pl.ds(lead - j, r), :] for j in range(CONV_K))
        d_halo = sum(ds_ref[j, pl.ds(CONV_K - 1 - j, HALO), :] for j in range(CONV_K))
        dx_ref[...] = d_tile
        dx_ref[r - HALO:r, :] += carry_ref[...]
        carry_ref[...] = d_halo
        _accumulate(step, dp_refs, grads[CONV_K:])

    row = pl.BlockSpec((r, w), lambda i: (nt - 1 - i, 0))
    return pl.pallas_call(
        body, name="ml_pre_bwd", grid=(nt,),
        in_specs=[row, pl.BlockSpec((HALO, w), lambda i: ((nt - 1 - i) * (r // HALO), 0))] + [_full_spec(p.shape) for p in params]
        + [row] * 4 + [pl.BlockSpec((r, LANES), lambda i: (nt - 1 - i, 0))],
        out_specs=[row] + [_full_spec(p.shape) for p in params],
        out_shape=[jax.ShapeDtypeStruct((t, w), F32)] + [jax.ShapeDtypeStruct(p.shape, F32) for p in params],
        scratch_shapes=[pltpu.VMEM((r + HALO, w), F32), pltpu.VMEM((CONV_K, r + 2 * HALO, w), F32), pltpu.VMEM((HALO, w), F32)],
        compiler_params=_cparams("arbitrary"),
    )(x_m, x_pad, *params, *cts)


def _per_head(fn, row_vals, head_params, shared_params=()):
    return [fn(*[a[:, hs] for a in row_vals], *[p[:, hs] for p in head_params], *shared_params) for hs in _head_slices(HEAD_W)]


def _gla_out(o, g, gn):
    return _rms(o, gn) * (g * _sigmoid(g))


def _ml_out(hc, op, xc, g, sk):
    hcell = hc * _sigmoid(op)
    mu = jnp.mean(hcell, axis=-1, keepdims=True)
    d = hcell - mu
    var = jnp.mean(d * d, axis=-1, keepdims=True)
    return d * lax.rsqrt(var + EPS) * g + sk * xc


def _log_decay(al, w, b):
    return _log_sigmoid(_bdot(al, w, "nn") + b) * (1.0 / GLA_GATE_NORM)


def _merge(ga, gb, ya, yb):
    return _sigmoid(ga) * ya + _sigmoid(gb) * yb


def _post_mix(x, z, gpm, gpl):
    x1 = x + _rms(z, gpm)
    return x1, _rms(x1, gpl)


def _loss_rows(x1, dn, tgt, g):
    e = x1 + _rms(dn, g) - tgt
    return 0.5 * jnp.sum(jnp.mean(e * e, axis=-1, keepdims=True), axis=0, keepdims=True)


def _lin(p):
    return 4 * p[0] + 2 * p[1] + p[2]


def _me():
    return lax.axis_index("x"), lax.axis_index("y"), lax.axis_index("c")


def _flip(p, k):
    return tuple((1 - v) if (k >> (2 - i)) & 1 else v for i, v in enumerate(p))


ANY = pl.BlockSpec(memory_space=pl.ANY)


def _allgather_big(shards, name, deps=()):
    n = len(shards)
    n_in = n + len(deps)

    def body(*refs):
        ins, outs = refs[:n], refs[n_in:n_in + n]
        send_sems, recv_sems, local_sems = refs[n_in + n:]
        me = _me()
        x, y, c = me
        sib = (x, y, 1 - c)
        chips = [(1 - x, y), (x, 1 - y), (1 - x, 1 - y)]

        def cp(a, k, block, to, src=None):
            dst = outs[a].at[_lin(block)]
            return pltpu.make_async_remote_copy(src_ref=dst if src is None else src, dst_ref=dst,
                                                send_sem=send_sems.at[a * 7 + k], recv_sem=recv_sems.at[a * 7 + k],
                                                device_id=to, device_id_type=MESH)

        mine = [pltpu.make_async_copy(ins[a], outs[a].at[_lin(me)], local_sems.at[a]) for a in range(n)]
        for m in mine:
            m.start()
        first = []
        for a in range(n):
            first.append(cp(a, 0, me, sib, src=ins[a]))
            first += [cp(a, 1 + j, me, (*chip, c), src=ins[a]) for j, chip in enumerate(chips)]
        for f in first:
            f.start()
        passed = []
        for j, chip in enumerate(chips):
            for a in range(n):
                cp(a, 1 + j, (*chip, c), me).wait_recv()
                fwd = cp(a, 4 + j, (*chip, c), sib)
                fwd.start()
                passed.append(fwd)
        for a in range(n):
            cp(a, 0, sib, me).wait_recv()
            for j, chip in enumerate(chips):
                cp(a, 4 + j, (*chip, 1 - c), me).wait_recv()
        for f in first + passed:
            f.wait_send()
        for m in mine:
            m.wait()

    return pl.pallas_call(
        body, name=name,
        in_specs=[ANY] * n_in, out_specs=[ANY] * n,
        out_shape=[jax.ShapeDtypeStruct((N_DEV, *s.shape), s.dtype) for s in shards],
        scratch_shapes=[pltpu.SemaphoreType.DMA((7 * n,)), pltpu.SemaphoreType.DMA((7 * n,)), pltpu.SemaphoreType.DMA((n,))],
    )(*shards, *deps)


HBM = pl.BlockSpec(memory_space=pltpu.HBM)
SEM = pl.BlockSpec(memory_space=pltpu.SEMAPHORE)
DATAFLOW = pltpu.SideEffectType.DATAFLOW_SIDE_EFFECTING


SIBLING = 1
OTHER_CHIPS = (2, 4, 6)


def _peer_copies(kinds, srcs, lands, send_sems, recv_sems):
    me = _me()
    copies = []
    for a, (kind, src, land) in enumerate(zip(kinds, srcs, lands)):
        masks = {"gather": range(1, N_DEV), "exchange": range(1, N_DEV), "gather_chips": (SIBLING, *OTHER_CHIPS),
                 "gather_pass": OTHER_CHIPS}[kind]
        for k in masks:
            peer = _flip(me, k)
            if kind == "gather_pass":
                block = land.at[_lin(peer)]
                src_ref, dst_ref, target = block, block, _flip(me, SIBLING)
            else:
                src_ref, dst_ref, target = (src.at[_lin(peer)] if kind == "exchange" else src), land.at[_lin(me)], peer
            copies.append(pltpu.make_async_remote_copy(
                src_ref=src_ref, dst_ref=dst_ref, send_sem=send_sems.at[a * 7 + k - 1], recv_sem=recv_sems.at[a * 7 + k - 1],
                device_id=target, device_id_type=MESH))
    return copies


def _copies_start(kind, srcs, name, after=None, lands=None):
    n = len(srcs)
    extra = [] if after is None else [after]
    kind = [kind] * n if isinstance(kind, str) else list(kind)
    land_shapes = [(s.shape if k == "exchange" else (N_DEV, *s.shape)) for k, s in zip(kind, srcs)]
    lands = [lax.empty(ls, s.dtype) for ls, s in zip(land_shapes, srcs)] if lands is None else lands

    def body(*refs):
        sems = refs[2 * n + len(extra):]
        for cp in _peer_copies(kind, refs[:n], refs[n:2 * n], sems[0], sems[1]):
            cp.start()
        refs[-1][...] = jnp.zeros_like(refs[-1])

    def hbm(a):
        return pltpu.with_memory_space_constraint(a, pltpu.HBM)

    out = pl.pallas_call(
        body, name=name,
        out_shape=(pltpu.SemaphoreType.DMA((7 * n,)), pltpu.SemaphoreType.DMA((7 * n,)),
                   *[pltpu.HBM(s.shape, s.dtype) for s in srcs],
                   *[pltpu.HBM(ls, s.dtype) for ls, s in zip(land_shapes, srcs)],
                   jax.ShapeDtypeStruct((8, LANES), F32)),
        in_specs=[HBM] * (2 * n) + [ANY] * len(extra),
        out_specs=(SEM, SEM, *[HBM] * (2 * n), pl.BlockSpec(memory_space=pltpu.VMEM)),
        input_output_aliases={i: 2 + i for i in range(2 * n)},
        compiler_params=pltpu.CompilerParams(has_side_effects=DATAFLOW),
    )(*[hbm(s) for s in srcs], *[hbm(a) for a in lands], *extra)
    return (kind, n, out[:-1]), out[-1]


def _copies_wait(state, after, name):
    kind, n, (send_sems, recv_sems, *thru) = state
    after = list(after) if isinstance(after, (list, tuple)) else [after]

    def body(*refs):
        for cp in _peer_copies(kind, refs[:n], refs[n:2 * n], refs[2 * n], refs[2 * n + 1]):
            cp.wait_send()
            cp.wait_recv()

    out = pl.pallas_call(
        body, name=name,
        out_shape=tuple(pltpu.HBM(t.shape, t.dtype) for t in thru),
        in_specs=[HBM] * (2 * n) + [SEM, SEM] + [ANY] * len(after), out_specs=tuple([HBM] * (2 * n)),
        input_output_aliases={i: i for i in range(2 * n)},
        compiler_params=pltpu.CompilerParams(has_side_effects=DATAFLOW),
    )(*thru, send_sems, recv_sems, *after)
    return out[:n], out[n:]


def _adamw(w, g, m, v):
    m2 = ADAM_B1 * m + (1.0 - ADAM_B1) * g
    v2 = ADAM_B2 * v + (1.0 - ADAM_B2) * (g * g)
    m_hat = m2 / (1.0 - ADAM_B1 ** ADAM_STEP)
    v_hat = v2 / (1.0 - ADAM_B2 ** ADAM_STEP)
    delta = -ADAM_LR * (m_hat / (jnp.sqrt(v_hat) + ADAM_EPS) + ADAM_WD * w)
    return delta, m2, v2


def _sum_adamw(land, part, me_idx, w, m, v, name, tile=256):
    r, c = w.shape
    tr = min(tile, r)

    def body(me_ref, own_ref, *refs):
        slots = refs[:N_DEV]
        w_ref, m_ref, v_ref, g_ref, d_ref, m2_ref, v2_ref = refs[N_DEV:]
        own = own_ref[...].astype(F32)
        g = None
        for s in range(N_DEV):
            term = jnp.where(me_ref[0] == s, own, slots[s][...].astype(F32))
            g = term if g is None else g + term
        d, m2, v2 = _adamw(w_ref[...], g, m_ref[...], v_ref[...])
        g_ref[...] = g
        d_ref[...] = d
        m2_ref[...] = m2
        v2_ref[...] = v2

    def slot_spec(s):
        return pl.BlockSpec((None, tr, c), lambda i, me: (jnp.where(me[0] == s, (s + 1) % N_DEV, s), i, 0))

    row = pl.BlockSpec((tr, c), lambda i, me: (i, 0))
    return pl.pallas_call(
        body, name=name,
        grid_spec=pltpu.PrefetchScalarGridSpec(
            num_scalar_prefetch=1, grid=(r // tr,),
            in_specs=[pl.BlockSpec((None, tr, c), lambda i, me: (me[0], i, 0))] + [slot_spec(s) for s in range(N_DEV)] + [row] * 3,
            out_specs=[row] * 4),
        out_shape=[jax.ShapeDtypeStruct((r, c), F32)] * 4,
        compiler_params=_cparams("parallel"),
    )(me_idx, part, *[land] * N_DEV, w, m, v)


def _small_update(name, me_idx, kinds, lands, owns, ws, ms, vs, sums=()):
    n = len(ws)
    lands, owns = list(lands) + [s[0] for s in sums], list(owns) + [s[1] for s in sums]
    kinds = list(kinds) + ["gather"] * len(sums)
    nl = len(lands)

    def summed(me, land_ref, own):
        g = None
        for s in range(N_DEV):
            term = jnp.where(me == s, own, land_ref[s])
            g = term if g is None else g + term
        return g

    def body(me_ref, *refs):
        land_refs, own_refs = refs[:nl], refs[nl:2 * nl]
        w_refs, m_refs, v_refs = (refs[2 * nl + i * n:2 * nl + (i + 1) * n] for i in range(3))
        outs = refs[2 * nl + 3 * n:]
        me = me_ref[0]
        for i in range(n):
            g = summed(me, land_refs[i], own_refs[i][...])
            d, m2, v2 = _adamw(w_refs[i][...], g, m_refs[i][...], v_refs[i][...])
            for ref, val in zip(outs[4 * i:4 * i + 4], (g, d, m2, v2)):
                ref[...] = val
        for i in range(n, nl):
            outs[4 * n + i - n][...] = summed(me, land_refs[i], own_refs[i][...])

    def whole(shape):
        return pl.BlockSpec(shape, lambda i, me, nd=len(shape): (0,) * nd)

    def own_spec(kind, own):
        if kind == "gather":
            return whole(own.shape)
        return pl.BlockSpec((None, *own.shape[1:]), lambda i, me: (me[0], 0, 0))

    shapes = [w.shape for w in ws]
    out_shapes = [s for s in shapes for _ in range(4)] + [s[1].shape for s in sums]
    return pl.pallas_call(
        body, name=name,
        grid_spec=pltpu.PrefetchScalarGridSpec(
            num_scalar_prefetch=1, grid=(1,),
            in_specs=[whole(a.shape) for a in lands] + [own_spec(k, o) for k, o in zip(kinds, owns)]
            + [whole(s) for s in shapes] * 3,
            out_specs=[whole(s) for s in out_shapes]),
        out_shape=[jax.ShapeDtypeStruct(s, F32) for s in out_shapes],
        compiler_params=_cparams("arbitrary"),
    )(me_idx, *lands, *owns, *ws, *ms, *vs)


def _small_view(n, a):
    if a.ndim == 1:
        return a.reshape(1, -1)
    if a.ndim == 3:
        return a.transpose(1, 2, 0).reshape(QKV_BLOCK * QKV_BLOCK, -1)
    return a.T if n == "w_if" else a


def _small_unview(n, a, shape):
    if len(shape) == 1:
        return a.reshape(shape)
    if len(shape) == 3:
        return a.reshape(QKV_BLOCK, QKV_BLOCK, -1).transpose(2, 0, 1)
    return a.T if n == "w_if" else a


def _small_shards(n, g):
    if n == "w_if":
        return g.reshape(N_DEV, -1, g.shape[1]).transpose(0, 2, 1)
    return g.reshape(g.shape[0], N_DEV, -1).transpose(1, 0, 2)


def _small_unshard(n, s):
    if n == "w_if":
        return s.transpose(0, 2, 1).reshape(-1, s.shape[1])
    return s.transpose(1, 0, 2).reshape(s.shape[1], -1)


def _to_hm(a, d):
    t = a.shape[0]
    return a.reshape(t, HEADS, d).transpose(1, 0, 2)


def _from_hm(a):
    h, t, d = a.shape
    return a.transpose(1, 0, 2).reshape(t, h * d)


def _gate_rows(g):
    t = g.shape[0]
    return g.T.reshape(HEADS, t // CHUNK, 1, CHUNK)


def _gate_cols(g):
    h, nc, _, c = g.shape
    return g.reshape(h, nc * c).T


def _blockdiag_dense(w):
    n = w.shape[0] * QKV_BLOCK
    tiled = jnp.tile(w.reshape(n, QKV_BLOCK), (1, n // QKV_BLOCK))
    r = lax.broadcasted_iota(jnp.int32, (n, n), 0)
    c = lax.broadcasted_iota(jnp.int32, (n, n), 1)
    return jnp.where(r // QKV_BLOCK == c // QKV_BLOCK, tiled, 0.0)


def _blockdiag_blocks(dense):
    n = dense[0].shape[0]
    k = len(dense)

    def body(*refs):
        r = lax.broadcasted_iota(jnp.int32, (n, n), 0)
        c = lax.broadcasted_iota(jnp.int32, (n, n), 1)
        fr = lax.broadcasted_iota(jnp.int32, (n, LANES), 0)
        fc = lax.broadcasted_iota(jnp.int32, (n, LANES), 1)
        fold = ((fr & (QKV_BLOCK - 1)) == fc).astype(BF16)
        for i in range(k):
            kept = jnp.where((r >> 2) == (c >> 2), refs[i][...], 0.0)
            refs[k + i][...] = sum(lax.dot_general(t, fold, _dims("nn", 2), preferred_element_type=F32) for t in _split3(kept))

    out = pl.pallas_call(body, name="blockdiag_blocks", out_shape=[jax.ShapeDtypeStruct((n, LANES), F32)] * k)(*dense)
    return [o[:, 0:QKV_BLOCK].reshape(n // QKV_BLOCK, QKV_BLOCK, QKV_BLOCK) for o in out]


def _col_blocks(w):
    k, n = w.shape
    return w.reshape(k, N_DEV, n // N_DEV).transpose(1, 0, 2)


def _from_col_blocks(g):
    d, k, n = g.shape
    return g.transpose(1, 0, 2).reshape(k, d * n)


def _local_step(x, tgt, weight, ws, prefetch, on_grads, on_small):
    t, d = x.shape
    g1 = ws["g_pre_mix"]

    def dep(token):
        return () if token is None else (token,)

    w_in = weight("w_in", x)
    fetch_mix = prefetch(("w_pa", "w_pb", "w_o"), w_in)

    n_in = w_in.shape[2]

    def proj_in_fwd(xv, g, w):
        hv = _rms(xv, g)
        return (hv, jnp.concatenate([_raw_dot(hv, w[j], "nn") for j in range(N_DEV)], axis=1)), ()

    h, proj = _rowwise("proj_in", proj_in_fwd, [x], [g1, w_in], [(d, BF16), (N_DEV * n_in, F32)], deps=dep(fetch_mix))
    offs = [0]
    for s in IN_SPLITS:
        offs.append(offs[-1] + s)
    q_a, k_a, v_a, g_a, a_low, x_m, o_pre, gate_a, gate_b = [proj[:, offs[i]:offs[i + 1]] for i in range(9)]

    a_low_p = jnp.pad(a_low, ((0, 0), (0, LANES - LOWRANK)))
    w_a_up_p = jnp.pad(ws["w_a_up"], ((0, LANES - LOWRANK), (0, 0)))
    b_a_up = ws["b_a_up"]
    (la,) = _rowwise("gla_decay", lambda al, w, b: ((_log_decay(al, w, b),), ()), [a_low_p], [w_a_up_p, b_a_up],
                     [(HEADS * GLA_DK, F32)])
    fetch_up = prefetch(("w_up",), la)
    q_hm, k_hm, la_hm = _to_hm(q_a, GLA_DK), _to_hm(k_a, GLA_DK), _to_hm(la, GLA_DK)
    o_gla, s_prev = _gla_fwd(q_hm, k_hm, v_a, la_hm)
    gn = ws["g_gla_norm"]
    ml_w = HEADS * HEAD_W

    def proj_a_fwd(o, g, n_, w):
        ya = jnp.concatenate(_per_head(_gla_out, [o, g], [], [n_]), axis=1)
        return (ya, _raw_dot(ya, w, "nn")), ()

    ya_in, y_a = _rowwise("proj_a", proj_a_fwd, [o_gla, g_a], [gn, weight("w_pa", o_gla)], [(ml_w, BF16), (d, F32)],
                          tile=512, deps=dep(fetch_up))

    cw = ws["conv_w"]
    w_if_p = jnp.pad(ws["w_if"], ((0, 0), (0, LANES - 2 * HEADS)))
    pre_params = [cw[0:1], cw[1:2], cw[2:3], cw[3:4], ws["conv_b"],
                  _blockdiag_dense(ws["w_q_ml"]), _blockdiag_dense(ws["w_k_ml"]), _blockdiag_dense(ws["w_v_ml"]),
                  w_if_p[0:ml_w], w_if_p[ml_w:2 * ml_w], w_if_p[2 * ml_w:3 * ml_w],
                  jnp.pad(ws["b_if"], ((0, 0), (0, LANES - 2 * HEADS)))]
    x_pad = jnp.pad(x_m, ((HALO, 0), (0, 0)))
    xc, q_m, k_m, v_m, gl = _ml_pre_fwd(x_m, x_pad, pre_params)
    li, lf = _gate_rows(gl[:, 0:HEADS]), _gate_rows(gl[:, HEADS:2 * HEADS])
    hc, c_prev, n_prev, m_prev = _ml_fwd(q_m, k_m, v_m, li, lf)
    fetch_down = prefetch(("w_down",), hc)
    g_ml, skip = ws["g_ml_norm"], ws["ml_skip"]

    def proj_b_fwd(a, b, c_, ga, gb, ya, g, s, w):
        hb = jnp.concatenate(_per_head(_ml_out, [a, b, c_], [g, s]), axis=1)
        yb = _raw_dot(hb, w, "nn")
        return (hb, yb, _merge(ga, gb, ya, yb)), ()

    h_b, y_b, merged = _rowwise("proj_b", proj_b_fwd, [hc, o_pre, xc, gate_a, gate_b, y_a], [g_ml, skip, weight("w_pb", hc)],
                                [(ml_w, BF16), (d, F32), (d, BF16)], tile=512, deps=dep(fetch_down))

    gpm, gpl, gpo = ws["g_post_mix"], ws["g_pre_mlp"], ws["g_post_mlp"]

    def proj_o_fwd(mg, xv, w, a, b):
        zv = _raw_dot(mg, w, "nn")
        return (zv, *_post_mix(xv, zv, a, b)), ()

    z, x1, h2 = _rowwise("proj_o", proj_o_fwd, [merged, x], [weight("w_o", merged), gpm, gpl],
                         [(d, F32), (d, F32), (d, BF16)], tile=512)
    up, u = _mm(h2, weight("w_up", h2), "nn", (BF16, BF16), "mlp_up", epilogue=lambda p: (p, jnp.square(jnp.maximum(p, 0.0))))

    def mlp_down_loss(uv, x1v, tgtv, w, g):
        dnv = _raw_dot(uv, w, "nn")
        loss, vjp = jax.vjp(lambda a, b, c_: _loss_rows(a, b, tgtv, c_), x1v, dnv, g)
        dx1, ddn, dg = vjp(jnp.ones((1, 1), F32))
        return (dx1, ddn), (jnp.broadcast_to(loss, (1, LANES)), dg)

    dx1_y, d_dn, loss, d_gpo = _rowwise("mlp_down", mlp_down_loss, [u, x1, tgt], [weight("w_down", u), gpo],
                                        [(d, F32), (d, BF16)], [((1, LANES), F32), ((1, d), F32)], tile=512)

    (d_up,) = _mm(d_dn, weight("w_down", u), "nt", (BF16,), "mlp_down_dx", extra=[up],
                  epilogue=lambda p, a: (p * (2.0 * jnp.maximum(a.astype(F32), 0.0)),))
    dw_down = _mm(u, d_dn, "tn", BF16, "mlp_down_dw", tm=512)
    dw_up = _mm(h2, d_up, "tn", BF16, "mlp_up_dw")
    sent_mlp = on_grads(dict(w_down=dw_down, w_up=dw_up))

    def mlp_up_dx(dup, xv, zv, dx1, w, a, b):
        _, vjp = jax.vjp(_post_mix, xv, zv, a, b)
        dx, dz, da, db = vjp((dx1, _raw_dot(dup, w, "nt")))
        return (dx, dz), (da, db)

    dx_res, d_z, d_gpm, d_gpl = _rowwise("mlp_up_dx", mlp_up_dx, [d_up, x, z, dx1_y], [weight("w_up", h2), gpm, gpl],
                                         [(d, F32), (d, BF16)], [((1, d), F32), ((1, d), F32)], tile=512, deps=dep(sent_mlp))
    dw_o = _mm(merged, d_z, "tn", BF16, "proj_o_dw")

    def proj_o_dx(dz, ga, gb, ya, yb, w):
        return jax.vjp(_merge, ga, gb, ya, yb)[1](_raw_dot(dz, w, "nt")), ()

    d_ga, d_gb, d_ya, d_yb = _rowwise("proj_o_dx", proj_o_dx, [d_z, gate_a, gate_b, y_a, y_b], [weight("w_o", merged)],
                                      [(d, F32), (d, F32), (d, BF16), (d, BF16)], tile=512)
    dw_pa = _mm(ya_in, d_ya, "tn", BF16, "proj_a_dw")
    dw_pb = _mm(h_b, d_yb, "tn", BF16, "proj_b_dw")
    sent_mix = on_grads(dict(w_o=dw_o, w_pa=dw_pa, w_pb=dw_pb))

    def proj_b_dx(dyb, a, b, c_, w, g, s):
        ct = _raw_dot(dyb, w, "nt")
        parts = []
        for hs in _head_slices(HEAD_W):
            _, vjp = jax.vjp(_ml_out, a[:, hs], b[:, hs], c_[:, hs], g[:, hs], s[:, hs])
            parts.append(vjp(ct[:, hs]))
        cat = lambda i: jnp.concatenate([p[i] for p in parts], axis=1)
        return (cat(0), cat(1), cat(2)), (cat(3), cat(4))

    d_hc, d_opre, d_xc, d_gml, d_skip = _rowwise("proj_b_dx", proj_b_dx, [d_yb, hc, o_pre, xc],
                                                 [weight("w_pb", hc), g_ml, skip], [(ml_w, F32)] * 3, [((1, ml_w), F32)] * 2,
                                                 tile=512, deps=dep(sent_mix))
    d_qm, d_km, d_vm, d_li, d_lf = _ml_bwd(q_m, k_m, v_m, li, lf, c_prev, n_prev, m_prev, d_hc)
    d_gl = jnp.concatenate([_gate_cols(d_li), _gate_cols(d_lf), jnp.zeros((t, LANES - 2 * HEADS), F32)], axis=1)
    pre_grads = _ml_pre_bwd(x_m, x_pad, pre_params, [d_xc, d_qm, d_km, d_vm, d_gl])
    d_xm = pre_grads[0]
    d_cw = jnp.concatenate(pre_grads[1:5], axis=0)
    d_cb = pre_grads[5]
    d_wq, d_wk, d_wv = _blockdiag_blocks(pre_grads[6:9])
    d_wif = jnp.concatenate(pre_grads[9:12], axis=0)[:, 0:2 * HEADS]
    d_bif = pre_grads[12][:, 0:2 * HEADS]

    def proj_a_dx(dya, o, g, w, n_):
        ct = _raw_dot(dya, w, "nt")
        parts = []
        for hs in _head_slices(HEAD_W):
            _, vjp = jax.vjp(_gla_out, o[:, hs], g[:, hs], n_)
            parts.append(vjp(ct[:, hs]))
        cat = lambda i: jnp.concatenate([p[i] for p in parts], axis=1)
        return (cat(0), cat(1)), (sum(p[2] for p in parts),)

    d_o, d_g_a, d_gn = _rowwise("proj_a_dx", proj_a_dx, [d_ya, o_gla, g_a], [weight("w_pa", o_gla), gn], [(ml_w, F32)] * 2,
                                [((1, HEAD_W), F32)], tile=512)
    dq_hm, dk_hm, d_va, dla_hm = _gla_bwd(q_hm, k_hm, v_a, la_hm, s_prev, d_o)

    def decay_bwd(al, ct, w, b):
        _, vjp = jax.vjp(_log_decay, al, w, b)
        dal, dw, db = vjp(ct)
        return (dal,), (dw, db)

    d_alow_p, d_wa_p, d_ba = _rowwise("gla_decay_bwd", decay_bwd, [a_low_p, _from_hm(dla_hm)], [w_a_up_p, b_a_up],
                                      [(LANES, F32)], [(w_a_up_p.shape, F32), (b_a_up.shape, F32)])
    d_proj = jnp.concatenate([_from_hm(dq_hm), _from_hm(dk_hm), d_va, d_g_a, d_alow_p[:, 0:LOWRANK], d_xm, d_opre, d_ga, d_gb],
                             axis=1).astype(BF16)
    small = dict(w_a_up=d_wa_p[0:LOWRANK], b_a_up=d_ba, g_gla_norm=d_gn, conv_w=d_cw, conv_b=d_cb,
                 w_q_ml=d_wq, w_k_ml=d_wk, w_v_ml=d_wv, w_if=d_wif, b_if=d_bif, ml_skip=d_skip, g_ml_norm=d_gml,
                 g_post_mix=d_gpm, g_pre_mlp=d_gpl, g_post_mlp=d_gpo)
    sent_small = on_small(small, loss)
    dw_in = _mm_shard_cols(h, d_proj, n_in, "proj_in_dw", deps=dep(sent_small))
    sent_in = on_grads(dict(w_in=dw_in))

    def proj_in_dx(dp, xv, dres, w, g):
        _, vjp = jax.vjp(_rms, xv, g)
        dx, dg = vjp(sum(_raw_dot(dp[:, j * n_in:(j + 1) * n_in], w[j], "nt") for j in range(N_DEV)))
        return (dx + dres,), (dg,)

    grad_x, d_g1 = _rowwise("proj_in_dx", proj_in_dx, [d_proj, x, dx_res], [w_in, g1], [(d, F32)], [((1, d), F32)],
                            deps=dep(sent_in))
    return grad_x, on_small(dict(g_pre_mix=d_g1), None)


BIG = ("w_in", "w_pa", "w_pb", "w_o", "w_up", "w_down")
BIG_COL_SHARDED = ("w_in", "w_pa", "w_pb", "w_up")
SMALL_SHARDED = ("w_a_up", "conv_w", "w_if")
SMALL = ("g_pre_mix", "w_a_up", "b_a_up", "g_gla_norm", "conv_w", "conv_b", "w_q_ml", "w_k_ml", "w_v_ml", "w_if", "b_if",
         "ml_skip", "g_ml_norm", "g_post_mix", "g_pre_mlp", "g_post_mlp")
WEIGHTS = ("g_pre_mix", "w_in", "w_a_up", "b_a_up", "g_gla_norm", "conv_w", "conv_b", "w_q_ml", "w_k_ml", "w_v_ml", "w_if", "b_if",
           "ml_skip", "g_ml_norm", "w_pa", "w_pb", "w_o", "g_post_mix", "g_pre_mlp", "w_up", "w_down", "g_post_mlp")


def kernel(x, g_pre_mix, w_in, w_a_up, b_a_up, g_gla_norm, conv_w, conv_b, w_q_ml, w_k_ml, w_v_ml, w_if, b_if, ml_skip, g_ml_norm, w_pa, w_pb, w_o, g_post_mix, g_pre_mlp, w_up, w_down, g_post_mlp, loss_target, m_g_pre_mix, m_w_in, m_w_a_up, m_b_a_up, m_g_gla_norm, m_conv_w, m_conv_b, m_w_q_ml, m_w_k_ml, m_w_v_ml, m_w_if, m_b_if, m_ml_skip, m_g_ml_norm, m_w_pa, m_w_pb, m_w_o, m_g_post_mix, m_g_pre_mlp, m_w_up, m_w_down, m_g_post_mlp, v_g_pre_mix, v_w_in, v_w_a_up, v_b_a_up, v_g_gla_norm, v_conv_w, v_conv_b, v_w_q_ml, v_w_k_ml, v_w_v_ml, v_w_if, v_b_if, v_ml_skip, v_g_ml_norm, v_w_pa, v_w_pb, v_w_o, v_g_post_mix, v_g_pre_mlp, v_w_up, v_w_down, v_g_post_mlp):
    args = dict(locals())
    w = {n: args[n][0] for n in WEIGHTS}
    m = {n: args["m_" + n][0] for n in WEIGHTS}
    v = {n: args["v_" + n][0] for n in WEIGHTS}

    me_lin = _lin(_me())
    me_idx = jnp.reshape(me_lin, (1,)).astype(jnp.int32)

    def full_weight(n, g):
        if n == "w_in":
            return g
        return _from_col_blocks(g) if n in BIG_COL_SHARDED else g.reshape(-1, g.shape[-1])

    def grad_parts(n, g):
        if n == "w_in":
            return g
        return (_col_blocks(g) if n in BIG_COL_SHARDED else g.reshape(N_DEV, -1, g.shape[-1])).astype(BF16)

    sharded_names = tuple(SMALL_SHARDED)
    small_w_state, small_w_token = _copies_start("gather", [_small_view(n, w[n]) for n in sharded_names],
                                                 "allgather_start_small_weights")
    ready = {"w_in": full_weight("w_in", _allgather_big([w["w_in"].astype(BF16)], "allgather_w_in", [small_w_token])[0])}
    pending = {}

    def prefetch(group, after):
        state, token = _copies_start("gather_chips", [w[n].astype(BF16) for n in group], "allgather_start_" + group[0], after)
        for n in group:
            pending[n] = (group, state)
        return token

    def weight(n, after):
        if n not in ready:
            group, state = pending[n]
            shards, lands = _copies_wait(state, after, "allgather_wait_" + group[0])
            state, token = _copies_start("gather_pass", shards, "allgather_pass_" + group[0], lands=lands)
            shards, lands = _copies_wait(state, token, "allgather_passed_" + group[0])
            for gn, shard, land in zip(group, shards, lands):
                ready[gn] = full_weight(gn, lax.dynamic_update_slice(land, shard[None], (me_lin, 0, 0)))
        return ready[n]

    small_w_own, small_w_lands = _copies_wait(small_w_state, ready["w_in"], "allgather_wait_small_weights")
    ws = {n: (w[n].reshape(1, -1) if w[n].ndim == 1 else w[n]) for n in SMALL if n not in SMALL_SHARDED}
    for n, own, land in zip(sharded_names, small_w_own, small_w_lands):
        ws[n] = _small_unshard(n, lax.dynamic_update_slice(land, own[None], (me_lin, 0, 0)))

    sent = []

    def on_grads(grads):
        names = tuple(grads)
        state, token = _copies_start("exchange", [grad_parts(n, grads[n]) for n in names], "exchange_start_" + names[0])
        sent.append((names, state))
        return token

    small_sent = []

    def on_small(small, loss):
        names = tuple(small)
        kinds = ["exchange" if n in SMALL_SHARDED else "gather" for n in names]
        srcs = [_small_shards(n, small[n]) if n in SMALL_SHARDED else _small_view(n, small[n]) for n in names]
        extra = [] if loss is None else [loss]
        state, token = _copies_start(kinds + ["gather"] * len(extra), srcs + extra, "allgather_start_small_" + names[0])
        small_sent.append((names, kinds, state))
        return token

    grad_x, last_token = _local_step(x[0], loss_target[0], weight, ws, prefetch, on_grads, on_small)

    out = {}

    def finish(names, state, after):
        parts, lands = _copies_wait(state, after, "exchange_wait_" + names[0])
        for n, part, land in zip(names, parts, lands):
            out[n] = _sum_adamw(land, part, me_idx, w[n], m[n], v[n], "adamw_" + n)

    def finish_small(names, kinds, state, after):
        own, lands = _copies_wait(state, after, "allgather_wait_small_" + names[0])
        k = len(names)
        upd = _small_update("adamw_small_" + names[0], me_idx, kinds, lands[:k], own[:k],
                            *[[_small_view(n, d[n]) for n in names] for d in (w, m, v)], sums=list(zip(lands[k:], own[k:])))
        for i, n in enumerate(names):
            out[n] = tuple(_small_unview(n, a, w[n].shape) for a in upd[4 * i:4 * i + 4])
        return upd[4 * k:]

    (loss_sum,) = finish_small(*small_sent[0], [grad_x, last_token])
    for names, state in sent[:-1]:
        finish(names, state, [grad_x, last_token])
    finish(*sent[-1], [loss_sum] + [out[n][1] for n in BIG if n in out])
    finish_small(*small_sent[1], [out["w_in"][1]])

    shaped = lambda a, n: a.reshape(args[n].shape)
    return (loss_sum[0, 0], grad_x[None],
            *[shaped(out[n][0], n) for n in WEIGHTS], *[shaped(out[n][1], n) for n in WEIGHTS],
            *[shaped(out[n][2], n) for n in WEIGHTS], *[shaped(out[n][3], n) for n in WEIGHTS])
```

```python
import functools

import jax
import jax.numpy as jnp
from jax import lax
from jax.experimental import pallas as pl
from jax.experimental.pallas import tpu as pltpu

F32 = jnp.float32
BF16 = jnp.bfloat16
MESH = pl.DeviceIdType.MESH

N_DEV = 8
EPS = 1e-6
CHUNK = 64
CHUNKS_PER_STEP = 4
HEADS = 4
GLA_DK = 64
HEAD_W = 128
GLA_GATE_NORM = 16.0
LOWRANK = 16
CONV_K = 4
QKV_BLOCK = 4
LANES = 128
HALO = 8
IN_SPLITS = (256, 256, 512, 512, 16, 512, 512, 1024, 1024)

ADAM_LR = 0.001
ADAM_B1 = 0.9
ADAM_B2 = 0.999
ADAM_EPS = 1e-08
ADAM_WD = 0.01
ADAM_STEP = 10

VMEM_LIMIT = 56 * 1024 * 1024


def _cparams(*sem):
    return pltpu.CompilerParams(dimension_semantics=sem, vmem_limit_bytes=VMEM_LIMIT)


def _dims(mode, ndim):
    contract = {"nn": ((ndim - 1,), (ndim - 2,)), "nt": ((ndim - 1,), (ndim - 1,)), "tn": ((ndim - 2,), (ndim - 2,))}[mode]
    return contract, (((0,), (0,)) if ndim == 3 else ((), ()))


def _raw_dot(a, b, mode):
    return lax.dot_general(a.astype(BF16), b.astype(BF16), _dims(mode, a.ndim), preferred_element_type=F32)


@functools.partial(jax.custom_vjp, nondiff_argnums=(2,))
def _bdot(a, b, mode):
    return _raw_dot(a, b, mode)


def _bdot_fwd(a, b, mode):
    return _raw_dot(a, b, mode), (a, b)


def _bdot_bwd(mode, res, ct):
    a, b = res
    if mode == "nn":
        da, db = _raw_dot(ct, b, "nt"), _raw_dot(a, ct, "tn")
    elif mode == "nt":
        da, db = _raw_dot(ct, b, "nn"), _raw_dot(ct, a, "tn")
    else:
        da, db = _raw_dot(b, ct, "nt"), _raw_dot(a, ct, "nn")
    return da.astype(a.dtype), db.astype(b.dtype)


_bdot.defvjp(_bdot_fwd, _bdot_bwd)


def _split3(x):
    hi = x.astype(BF16)
    r1 = x - hi.astype(F32)
    mid = r1.astype(BF16)
    return hi, mid, (r1 - mid.astype(F32)).astype(BF16)


def _split_dot(tri, x):
    if x.ndim == 3:
        tri = jnp.broadcast_to(tri, (x.shape[0], *tri.shape))
    return sum(lax.dot_general(tri, t, _dims("nn", x.ndim), preferred_element_type=F32) for t in _split3(x))


def _tri(n, lower):
    r = lax.broadcasted_iota(jnp.int32, (n, n), 0)
    c = lax.broadcasted_iota(jnp.int32, (n, n), 1)
    return ((c <= r) if lower else (c >= r)).astype(BF16)


@jax.custom_vjp
def _cumsum_rows(x):
    return _split_dot(_tri(x.shape[-2], True), x)


def _cumsum_rows_fwd(x):
    return _cumsum_rows(x), None


def _cumsum_rows_bwd(_, ct):
    return (_split_dot(_tri(ct.shape[-2], False), ct),)


_cumsum_rows.defvjp(_cumsum_rows_fwd, _cumsum_rows_bwd)


def _abs(x):
    return jnp.where(x >= 0, x, -x)


def _sigmoid(x):
    return lax.logistic(x)


def _log_sigmoid(x):
    return jnp.minimum(x, 0.0) - jnp.log(1.0 + jnp.exp(-_abs(x)))


def _rms(x, g):
    return x * lax.rsqrt(jnp.mean(x * x, axis=-1, keepdims=True) + EPS) * g


def _head_slices(w):
    return [slice(h * w, (h + 1) * w) for h in range(HEADS)]


def _heads(ref, rows=slice(None)):
    return jnp.stack([ref[rows, hs] for hs in _head_slices(HEAD_W)])


def _put_heads(ref, val, rows=slice(None)):
    for h, hs in enumerate(_head_slices(HEAD_W)):
        ref[rows, hs] = val[h]


def _tile(dim, want):
    if dim <= want or dim % LANES:
        return dim
    t = want
    while dim % t:
        t -= LANES
    return t


def _mm(a, b, mode, out_dtype, name, tm=1024, tn=1024, tk=4096, epilogue=None, extra=(), deps=(), shards=None):
    if shards == "b":
        assert mode == "nn"
        ns = b.shape[2]
        (m, k), (k2, n) = a.shape, (b.shape[1], b.shape[0] * ns)
        tn = ns
    elif mode == "nn":
        (m, k), (k2, n) = a.shape, b.shape
    elif mode == "nt":
        (m, k), (n, k2) = a.shape, b.shape
    else:
        (k, m), (k2, n) = a.shape, b.shape
    assert k == k2, (name, a.shape, b.shape)
    tm, tn, tk = _tile(m, tm), _tile(n, tn), _tile(k, tk)
    nk = k // tk
    out_dtypes = out_dtype if epilogue else (out_dtype,)
    assert nk == 1 or (out_dtype == F32 and not epilogue), name
    n_in = 2 + len(extra)

    def body(*refs):
        p = _raw_dot(refs[0][...], refs[1][...], mode)
        if nk > 1:
            _accumulate(pl.program_id(2), [refs[n_in + len(deps)]], [p])
            return
        outs = epilogue(p, *[r[...] for r in refs[2:n_in]]) if epilogue else (p,)
        for ref, val in zip(refs[n_in + len(deps):], outs):
            ref[...] = val.astype(ref.dtype)

    a_spec = pl.BlockSpec((tk, tm), lambda i, j, kk: (kk, i)) if mode == "tn" else pl.BlockSpec((tm, tk), lambda i, j, kk: (i, kk))
    if shards == "b":
        b_spec = pl.BlockSpec((None, tk, tn), lambda i, j, kk: (j, kk, 0))
    elif mode == "nt":
        b_spec = pl.BlockSpec((tn, tk), lambda i, j, kk: (j, kk))
    else:
        b_spec = pl.BlockSpec((tk, tn), lambda i, j, kk: (kk, j))
    o_spec = pl.BlockSpec((tm, tn), lambda i, j, kk: (i, j))
    if shards == "out":
        out_specs, out_shape = [pl.BlockSpec((None, tm, tn), lambda i, j, kk: (j, i, 0))], (n // tn, m, tn)
    else:
        out_specs, out_shape = [o_spec] * len(out_dtypes), (m, n)
    res = pl.pallas_call(
        body, name=name, grid=(m // tm, n // tn, nk),
        in_specs=[a_spec, b_spec] + [o_spec] * len(extra) + [ANY] * len(deps), out_specs=out_specs,
        out_shape=[jax.ShapeDtypeStruct(out_shape, dt) for dt in out_dtypes],
        compiler_params=_cparams("parallel", "parallel", "arbitrary"),
    )(a, b, *extra, *deps)
    return res if epilogue else res[0]


def _mm_shard_cols(a, b, n, name, tm=512, tk=1024, deps=()):
    t, k = a.shape
    nb = b.shape[1] // n
    tk = min(tk, t)
    nk = t // tk

    def body(a_ref, b_ref, *rest):
        o_ref, acc_ref = rest[len(deps):]
        step = pl.program_id(1)
        a_t = a_ref[...].astype(BF16).T
        for j in range(nb):
            _accumulate(step, [acc_ref.at[j]], [_raw_dot(a_t, b_ref[:, j * n:(j + 1) * n], "nn")])

        @pl.when(step == nk - 1)
        def _():
            o_ref[...] = acc_ref[...].astype(BF16)

    return pl.pallas_call(
        body, name=name, grid=(k // tm, nk),
        in_specs=[pl.BlockSpec((tk, tm), lambda i, s: (s, i)), pl.BlockSpec((tk, nb * n), lambda i, s: (s, 0))] + [ANY] * len(deps),
        out_specs=pl.BlockSpec((nb, tm, n), lambda i, s: (0, i, 0)),
        out_shape=jax.ShapeDtypeStruct((nb, k, n), BF16),
        scratch_shapes=[pltpu.VMEM((nb, tm, n), F32)],
        compiler_params=_cparams("parallel", "arbitrary"),
    )(a, b, *deps)


def _rowwise(name, fn, rows, params, out_rows, out_accs=(), tile=256, deps=()):
    t = rows[0].shape[0]
    r = min(tile, t)
    assert t % r == 0
    n_in, n_or = len(rows) + len(params), len(out_rows)
    n_all = n_in + len(deps)
    params = list(params) + list(deps)

    def body(*refs):
        vals = [ref[...] for ref in refs[:n_in]]
        outs = refs[n_all:]
        ro, ao = fn(*vals)
        for ref, v in zip(outs[:n_or], ro):
            ref[...] = v.astype(ref.dtype)
        if out_accs:
            _accumulate(pl.program_id(0), outs[n_or:], ao)

    def full(shape):
        return pl.BlockSpec(shape, lambda i, nd=len(shape): (0,) * nd)

    return pl.pallas_call(
        body, name=name, grid=(t // r,),
        in_specs=[pl.BlockSpec((r, a.shape[1]), lambda i: (i, 0)) for a in rows] + [full(p.shape) for p in params],
        out_specs=[pl.BlockSpec((r, w), lambda i: (i, 0)) for w, _ in out_rows] + [full(s) for s, _ in out_accs],
        out_shape=[jax.ShapeDtypeStruct((t, w), dt) for w, dt in out_rows] + [jax.ShapeDtypeStruct(s, dt) for s, dt in out_accs],
        compiler_params=_cparams("arbitrary"),
    )(*rows, *params)


def _accumulate(step, refs, vals):
    for ref, v in zip(refs, vals):
        @pl.when(step == 0)
        def _(ref=ref, v=v):
            ref[...] = v.astype(ref.dtype)

        @pl.when(step > 0)
        def _(ref=ref, v=v):
            ref[...] += v.astype(ref.dtype)


def _gla_chunk(q, k, v, la, st):
    c = q.shape[-2]
    row = lax.broadcasted_iota(jnp.int32, (c, c), 0)
    col = lax.broadcasted_iota(jnp.int32, (c, c), 1)
    cum = _cumsum_rows(la)
    cl = jnp.sum(la, axis=-2, keepdims=True)
    ep = jnp.exp(cum)
    en = jnp.exp(-cum)
    qs = q * (GLA_DK ** -0.5)
    qp = qs * ep
    a_f = _bdot(qp, k * en, "nt")
    a_b = _bdot(qs * en, k * ep, "nt")
    sc = jnp.where(row >= col, a_f, a_b)
    o = _bdot(sc, v, "nn") + _bdot(qp, st, "nt")
    kd = k * jnp.exp(cl - cum)
    st_new = st * jnp.exp(cl) + _bdot(v, kd, "tn")
    return o, st_new


def _gla_specs(nc, rev):
    nb = nc // CHUNKS_PER_STEP
    rows = CHUNKS_PER_STEP * CHUNK

    def blk(n):
        return (nb - 1 - n) if rev else n
    hm = pl.BlockSpec((HEADS, rows, GLA_DK), lambda n: (0, blk(n), 0))
    tm = pl.BlockSpec((rows, HEADS * HEAD_W), lambda n: (blk(n), 0))
    st = pl.BlockSpec((HEADS, CHUNKS_PER_STEP, HEAD_W, GLA_DK), lambda n: (0, blk(n), 0, 0))
    return nb, hm, tm, st


def _chunk_rows(c):
    return slice(c * CHUNK, (c + 1) * CHUNK)


def _gla_fwd(q, k, v, la, deps=()):
    t = v.shape[0]
    nc = t // CHUNK
    nb, hm, tm, st = _gla_specs(nc, False)

    def body(q_ref, k_ref, v_ref, la_ref, *rest):
        o_ref, sp_ref, st_ref = rest[len(deps):]

        @pl.when(pl.program_id(0) == 0)
        def _():
            st_ref[...] = jnp.zeros_like(st_ref)

        s = st_ref[...]
        for c in range(CHUNKS_PER_STEP):
            r = _chunk_rows(c)
            sp_ref[:, c] = s
            o, s = _gla_chunk(q_ref[:, r], k_ref[:, r], _heads(v_ref, r), la_ref[:, r], s)
            _put_heads(o_ref, o, r)
        st_ref[...] = s

    return pl.pallas_call(
        body, name="gla_fwd", grid=(nb,),
        in_specs=[hm, hm, tm, hm] + [ANY] * len(deps), out_specs=[tm, st],
        out_shape=[jax.ShapeDtypeStruct((t, HEADS * HEAD_W), F32), jax.ShapeDtypeStruct((HEADS, nc, HEAD_W, GLA_DK), F32)],
        scratch_shapes=[pltpu.VMEM((HEADS, HEAD_W, GLA_DK), F32)],
        compiler_params=_cparams("arbitrary"),
    )(q, k, v, la, *deps)


def _gla_bwd(q, k, v, la, sp, do):
    t = v.shape[0]
    nc = t // CHUNK
    nb, hm, tm, st = _gla_specs(nc, True)

    def body(q_ref, k_ref, v_ref, la_ref, sp_ref, do_ref, dq_ref, dk_ref, dv_ref, dla_ref, ds_ref):
        @pl.when(pl.program_id(0) == 0)
        def _():
            ds_ref[...] = jnp.zeros_like(ds_ref)

        ds = ds_ref[...]
        for c in reversed(range(CHUNKS_PER_STEP)):
            r = _chunk_rows(c)
            _, vjp = jax.vjp(_gla_chunk, q_ref[:, r], k_ref[:, r], _heads(v_ref, r), la_ref[:, r], sp_ref[:, c])
            dq, dk, dv, dla, ds = vjp((_heads(do_ref, r), ds))
            dq_ref[:, r] = dq
            dk_ref[:, r] = dk
            _put_heads(dv_ref, dv, r)
            dla_ref[:, r] = dla
        ds_ref[...] = ds

    hm_shape = jax.ShapeDtypeStruct((HEADS, t, GLA_DK), F32)
    return pl.pallas_call(
        body, name="gla_bwd", grid=(nb,),
        in_specs=[hm, hm, tm, hm, st, tm], out_specs=[hm, hm, tm, hm],
        out_shape=[hm_shape, hm_shape, jax.ShapeDtypeStruct((t, HEADS * HEAD_W), F32), hm_shape],
        scratch_shapes=[pltpu.VMEM((HEADS, HEAD_W, GLA_DK), F32)],
        compiler_params=_cparams("arbitrary"),
    )(q, k, v, la, sp, do)


def _ml_chunk(q, k, v, li_r, lf_r, cm, nv, m):
    c = q.shape[-2]
    row = lax.broadcasted_iota(jnp.int32, (c, c), 0)
    col = lax.broadcasted_iota(jnp.int32, (c, c), 1)
    eye = (row == col).astype(F32)
    li_c = jnp.sum(eye * li_r, axis=-1, keepdims=True)
    lf_c = jnp.sum(eye * lf_r, axis=-1, keepdims=True)
    fc_c = jnp.sum((col <= row).astype(F32) * lf_r, axis=-1, keepdims=True)
    fc_r = jnp.sum((row <= col).astype(F32) * lf_c, axis=-2, keepdims=True)
    f_last = jnp.sum(lf_r, axis=-1, keepdims=True)
    kc = k * (HEAD_W ** -0.5)
    a_c = f_last - fc_c + li_c
    m_loc = jnp.max(a_c, axis=-2, keepdims=True)
    kw = kc * jnp.exp(a_c - m_loc)
    c_chunk = _bdot(kw, v, "tn")
    n_chunk = jnp.sum(kw, axis=-2, keepdims=True)
    m_new = jnp.maximum(f_last + m, m_loc)
    sp = jnp.exp(f_last + m - m_new)
    sl = jnp.exp(m_loc - m_new)
    cm_new = sp * cm + sl * c_chunk
    nv_new = sp * nv + sl * n_chunk
    log_d = li_r - _abs(fc_c - fc_r)
    g_inter = fc_c + m
    m_t = jnp.maximum(g_inter, jnp.max(log_d, axis=-1, keepdims=True))
    s = _bdot(q, kc, "nt") * jnp.exp(log_d - m_t)
    sc = jnp.exp(g_inter - m_t)
    num = _bdot(s, v, "nn") + sc * _bdot(q, cm, "nn")
    den = jnp.sum(s, axis=-1, keepdims=True) + sc * jnp.sum(q * nv, axis=-1, keepdims=True)
    den = jnp.maximum(_abs(den), jnp.exp(-m_t))
    return num / den, cm_new, nv_new, m_new


def _ml_specs(nc, rev):
    nb = nc // CHUNKS_PER_STEP

    def blk(n):
        return (nb - 1 - n) if rev else n
    tm = pl.BlockSpec((CHUNKS_PER_STEP * CHUNK, HEADS * HEAD_W), lambda n: (blk(n), 0))
    gate = pl.BlockSpec((HEADS, CHUNKS_PER_STEP, 1, CHUNK), lambda n: (0, blk(n), 0, 0))
    cm = pl.BlockSpec((HEADS, CHUNKS_PER_STEP, HEAD_W, HEAD_W), lambda n: (0, blk(n), 0, 0))
    vec = pl.BlockSpec((HEADS, CHUNKS_PER_STEP, 1, HEAD_W), lambda n: (0, blk(n), 0, 0))
    return nb, tm, gate, cm, vec


_ML_STATE = [pltpu.VMEM((HEADS, HEAD_W, HEAD_W), F32), pltpu.VMEM((HEADS, 1, HEAD_W), F32), pltpu.VMEM((HEADS, 1, HEAD_W), F32)]


def _ml_fwd(q, k, v, li, lf):
    t = q.shape[0]
    nc = t // CHUNK
    nb, tm, gate, cm, vec = _ml_specs(nc, False)

    def body(q_ref, k_ref, v_ref, li_ref, lf_ref, hc_ref, cp_ref, np_ref, mp_ref, c_ref, n_ref, m_ref):
        @pl.when(pl.program_id(0) == 0)
        def _():
            c_ref[...] = jnp.zeros_like(c_ref)
            n_ref[...] = jnp.zeros_like(n_ref)
            m_ref[...] = jnp.zeros_like(m_ref)

        cs, ns, ms = c_ref[...], n_ref[...], m_ref[...][:, :, 0:1]
        for c in range(CHUNKS_PER_STEP):
            r = _chunk_rows(c)
            cp_ref[:, c] = cs
            np_ref[:, c] = ns
            mp_ref[:, c] = jnp.broadcast_to(ms, m_ref.shape)
            hc, cs, ns, ms = _ml_chunk(_heads(q_ref, r), _heads(k_ref, r), _heads(v_ref, r), li_ref[:, c], lf_ref[:, c],
                                       cs, ns, ms)
            _put_heads(hc_ref, hc, r)
        c_ref[...] = cs
        n_ref[...] = ns
        m_ref[...] = jnp.broadcast_to(ms, m_ref.shape)

    return pl.pallas_call(
        body, name="mlstm_fwd", grid=(nb,),
        in_specs=[tm, tm, tm, gate, gate], out_specs=[tm, cm, vec, vec],
        out_shape=[jax.ShapeDtypeStruct((t, HEADS * HEAD_W), F32), jax.ShapeDtypeStruct((HEADS, nc, HEAD_W, HEAD_W), F32),
                   jax.ShapeDtypeStruct((HEADS, nc, 1, HEAD_W), F32), jax.ShapeDtypeStruct((HEADS, nc, 1, HEAD_W), F32)],
        scratch_shapes=_ML_STATE,
        compiler_params=_cparams("arbitrary"),
    )(q, k, v, li, lf)


def _ml_bwd(q, k, v, li, lf, cp, npv, mp, dhc):
    t = q.shape[0]
    nc = t // CHUNK
    nb, tm, gate, cm, vec = _ml_specs(nc, True)

    def body(q_ref, k_ref, v_ref, li_ref, lf_ref, cp_ref, np_ref, mp_ref, dhc_ref,
             dq_ref, dk_ref, dv_ref, dli_ref, dlf_ref, dc_ref, dn_ref, dm_ref):
        @pl.when(pl.program_id(0) == 0)
        def _():
            dc_ref[...] = jnp.zeros_like(dc_ref)
            dn_ref[...] = jnp.zeros_like(dn_ref)
            dm_ref[...] = jnp.zeros_like(dm_ref)

        dc, dn, dm = dc_ref[...], dn_ref[...], dm_ref[...][:, :, 0:1]
        for c in reversed(range(CHUNKS_PER_STEP)):
            r = _chunk_rows(c)
            _, vjp = jax.vjp(_ml_chunk, _heads(q_ref, r), _heads(k_ref, r), _heads(v_ref, r), li_ref[:, c], lf_ref[:, c],
                             cp_ref[:, c], np_ref[:, c], mp_ref[:, c][:, :, 0:1])
            dq, dk, dv, dli, dlf, dc, dn, dm = vjp((_heads(dhc_ref, r), dc, dn, dm))
            _put_heads(dq_ref, dq, r)
            _put_heads(dk_ref, dk, r)
            _put_heads(dv_ref, dv, r)
            dli_ref[:, c] = dli
            dlf_ref[:, c] = dlf
        dc_ref[...] = dc
        dn_ref[...] = dn
        dm_ref[...] = jnp.broadcast_to(dm, dm_ref.shape)

    tm_shape = jax.ShapeDtypeStruct((t, HEADS * HEAD_W), F32)
    gate_shape = jax.ShapeDtypeStruct((HEADS, nc, 1, CHUNK), F32)
    return pl.pallas_call(
        body, name="mlstm_bwd", grid=(nb,),
        in_specs=[tm, tm, tm, gate, gate, cm, vec, vec, tm], out_specs=[tm, tm, tm, gate, gate],
        out_shape=[tm_shape, tm_shape, tm_shape, gate_shape, gate_shape],
        scratch_shapes=_ML_STATE,
        compiler_params=_cparams("arbitrary"),
    )(q, k, v, li, lf, cp, npv, mp, dhc)


def _ml_pre(s0, s1, s2, s3, cw0, cw1, cw2, cw3, cb, wq, wk, wv, wiq, wik, wiv, bif):
    pre = cb + cw0 * s0 + cw1 * s1 + cw2 * s2 + cw3 * s3
    xc = pre * _sigmoid(pre)
    q = _bdot(xc, wq, "nn")
    k = _bdot(xc, wk, "nn")
    v = _bdot(s3, wv, "nn")
    gates = _bdot(q, wiq, "nn") + _bdot(k, wik, "nn") + _bdot(v, wiv, "nn") + bif
    lane = lax.broadcasted_iota(jnp.int32, gates.shape, 1)
    gl = jnp.where(lane < HEADS, gates, _log_sigmoid(gates))
    return xc, q, k, v, gl


def _delayed(xs_ref, x_ref, halo_ref, r):
    xs_ref[0:HALO, :] = halo_ref[...]
    xs_ref[HALO:HALO + r, :] = x_ref[...]
    return [xs_ref[pl.ds(HALO - (CONV_K - 1) + j, r), :] for j in range(CONV_K)]


def _full_spec(shape):
    return pl.BlockSpec(shape, lambda i, nd=len(shape): (0,) * nd)


def _ml_pre_fwd(x_m, x_pad, params, tile=256):
    t, w = x_m.shape
    r = min(tile, t)

    def body(*refs):
        x_ref, halo_ref = refs[:2]
        p = [ref[...] for ref in refs[2:2 + len(params)]]
        outs = refs[2 + len(params):-1]
        res = _ml_pre(*_delayed(refs[-1], x_ref, halo_ref, r), *p)
        for ref, val in zip(outs, res):
            ref[...] = val

    row = pl.BlockSpec((r, w), lambda i: (i, 0))
    return pl.pallas_call(
        body, name="ml_pre_fwd", grid=(t // r,),
        in_specs=[row, pl.BlockSpec((HALO, w), lambda i: (i * (r // HALO), 0))] + [_full_spec(p.shape) for p in params],
        out_specs=[row] * 4 + [pl.BlockSpec((r, LANES), lambda i: (i, 0))],
        out_shape=[jax.ShapeDtypeStruct((t, w), F32)] * 4 + [jax.ShapeDtypeStruct((t, LANES), F32)],
        scratch_shapes=[pltpu.VMEM((r + HALO, w), F32)],
        compiler_params=_cparams("arbitrary"),
    )(x_m, x_pad, *params)


def _ml_pre_bwd(x_m, x_pad, params, cts, tile=256):
    t, w = x_m.shape
    r = min(tile, t)
    nt = t // r
    n_p = len(params)

    def body(*refs):
        x_ref, halo_ref = refs[:2]
        p = [ref[...] for ref in refs[2:2 + n_p]]
        ct = [ref[...] for ref in refs[2 + n_p:7 + n_p]]
        dx_ref = refs[7 + n_p]
        dp_refs = refs[8 + n_p:8 + 2 * n_p]
        xs_ref, ds_ref, carry_ref = refs[8 + 2 * n_p:]
        step = pl.program_id(0)

        @pl.when(step == 0)
        def _():
            ds_ref[...] = jnp.zeros_like(ds_ref)
            carry_ref[...] = jnp.zeros_like(carry_ref)

        _, vjp = jax.vjp(_ml_pre, *_delayed(xs_ref, x_ref, halo_ref, r), *p)
        grads = vjp(tuple(ct))
        for j in range(CONV_K):
            ds_ref[j, HALO:HALO + r, :] = grads[j]
        lead = HALO + CONV_K - 1
        d_tile = sum(ds_ref[j, pl.ds(lead - j, r), :] for j in range(CONV_K))
        d_halo = sum(ds_ref[j, pl.ds(CONV_K - 1 - j, HALO), :] for j in range(CONV_K))
        dx_ref[...] = d_tile
        dx_ref[r - HALO:r, :] += carry_ref[...]
        carry_ref[...] = d_halo
        _accumulate(step, dp_refs, grads[CONV_K:])

    row = pl.BlockSpec((r, w), lambda i: (nt - 1 - i, 0))
    return pl.pallas_call(
        body, name="ml_pre_bwd", grid=(nt,),
        in_specs=[row, pl.BlockSpec((HALO, w), lambda i: ((nt - 1 - i) * (r // HALO), 0))] + [_full_spec(p.shape) for p in params]
        + [row] * 4 + [pl.BlockSpec((r, LANES), lambda i: (nt - 1 - i, 0))],
        out_specs=[row] + [_full_spec(p.shape) for p in params],
        out_shape=[jax.ShapeDtypeStruct((t, w), F32)] + [jax.ShapeDtypeStruct(p.shape, F32) for p in params],
        scratch_shapes=[pltpu.VMEM((r + HALO, w), F32), pltpu.VMEM((CONV_K, r + 2 * HALO, w), F32), pltpu.VMEM((HALO, w), F32)],
        compiler_params=_cparams("arbitrary"),
    )(x_m, x_pad, *params, *cts)


def _per_head(fn, row_vals, head_params, shared_params=()):
    return [fn(*[a[:, hs] for a in row_vals], *[p[:, hs] for p in head_params], *shared_params) for hs in _head_slices(HEAD_W)]


def _gla_out(o, g, gn):
    return _rms(o, gn) * (g * _sigmoid(g))


def _ml_out(hc, op, xc, g, sk):
    hcell = hc * _sigmoid(op)
    mu = jnp.mean(hcell, axis=-1, keepdims=True)
    d = hcell - mu
    var = jnp.mean(d * d, axis=-1, keepdims=True)
    return d * lax.rsqrt(var + EPS) * g + sk * xc


def _log_decay(al, w, b):
    return _log_sigmoid(_bdot(al, w, "nn") + b) * (1.0 / GLA_GATE_NORM)


def _merge(ga, gb, ya, yb):
    return _sigmoid(ga) * ya + _sigmoid(gb) * yb


def _post_mix(x, z, gpm, gpl):
    x1 = x + _rms(z, gpm)
    return x1, _rms(x1, gpl)


def _loss_rows(x1, dn, tgt, g):
    e = x1 + _rms(dn, g) - tgt
    return 0.5 * jnp.sum(jnp.mean(e * e, axis=-1, keepdims=True), axis=0, keepdims=True)


def _lin(p):
    return 4 * p[0] + 2 * p[1] + p[2]


def _me():
    return lax.axis_index("x"), lax.axis_index("y"), lax.axis_index("c")


def _flip(p, k):
    return tuple((1 - v) if (k >> (2 - i)) & 1 else v for i, v in enumerate(p))


ANY = pl.BlockSpec(memory_space=pl.ANY)


def _allgather_big(shards, name, deps=()):
    n = len(shards)
    n_in = n + len(deps)

    def body(*refs):
        ins, outs = refs[:n], refs[n_in:n_in + n]
        send_sems, recv_sems, local_sems = refs[n_in + n:]
        me = _me()
        x, y, c = me
        sib = (x, y, 1 - c)
        chips = [(1 - x, y), (x, 1 - y), (1 - x, 1 - y)]

        def cp(a, k, block, to, src=None):
            dst = outs[a].at[_lin(block)]
            return pltpu.make_async_remote_copy(src_ref=dst if src is None else src, dst_ref=dst,
                                                send_sem=send_sems.at[a * 7 + k], recv_sem=recv_sems.at[a * 7 + k],
                                                device_id=to, device_id_type=MESH)

        mine = [pltpu.make_async_copy(ins[a], outs[a].at[_lin(me)], local_sems.at[a]) for a in range(n)]
        for m in mine:
            m.start()
        first = []
        for a in range(n):
            first.append(cp(a, 0, me, sib, src=ins[a]))
            first += [cp(a, 1 + j, me, (*chip, c), src=ins[a]) for j, chip in enumerate(chips)]
        for f in first:
            f.start()
        passed = []
        for j, chip in enumerate(chips):
            for a in range(n):
                cp(a, 1 + j, (*chip, c), me).wait_recv()
                fwd = cp(a, 4 + j, (*chip, c), sib)
                fwd.start()
                passed.append(fwd)
        for a in range(n):
            cp(a, 0, sib, me).wait_recv()
            for j, chip in enumerate(chips):
                cp(a, 4 + j, (*chip, 1 - c), me).wait_recv()
        for f in first + passed:
            f.wait_send()
        for m in mine:
            m.wait()

    return pl.pallas_call(
        body, name=name,
        in_specs=[ANY] * n_in, out_specs=[ANY] * n,
        out_shape=[jax.ShapeDtypeStruct((N_DEV, *s.shape), s.dtype) for s in shards],
        scratch_shapes=[pltpu.SemaphoreType.DMA((7 * n,)), pltpu.SemaphoreType.DMA((7 * n,)), pltpu.SemaphoreType.DMA((n,))],
    )(*shards, *deps)


HBM = pl.BlockSpec(memory_space=pltpu.HBM)
SEM = pl.BlockSpec(memory_space=pltpu.SEMAPHORE)
DATAFLOW = pltpu.SideEffectType.DATAFLOW_SIDE_EFFECTING


SIBLING = 1
OTHER_CHIPS = (2, 4, 6)


def _peer_copies(kinds, srcs, lands, send_sems, recv_sems):
    me = _me()
    copies = []
    for a, (kind, src, land) in enumerate(zip(kinds, srcs, lands)):
        masks = {"gather": range(1, N_DEV), "exchange": range(1, N_DEV), "gather_chips": (SIBLING, *OTHER_CHIPS),
                 "gather_pass": OTHER_CHIPS}[kind]
        for k in masks:
            peer = _flip(me, k)
            if kind == "gather_pass":
                block = land.at[_lin(peer)]
                src_ref, dst_ref, target = block, block, _flip(me, SIBLING)
            else:
                src_ref, dst_ref, target = (src.at[_lin(peer)] if kind == "exchange" else src), land.at[_lin(me)], peer
            copies.append(pltpu.make_async_remote_copy(
                src_ref=src_ref, dst_ref=dst_ref, send_sem=send_sems.at[a * 7 + k - 1], recv_sem=recv_sems.at[a * 7 + k - 1],
                device_id=target, device_id_type=MESH))
    return copies


def _copies_start(kind, srcs, name, after=None, lands=None):
    n = len(srcs)
    extra = [] if after is None else [after]
    kind = [kind] * n if isinstance(kind, str) else list(kind)
    land_shapes = [(s.shape if k == "exchange" else (N_DEV, *s.shape)) for k, s in zip(kind, srcs)]
    lands = [lax.empty(ls, s.dtype) for ls, s in zip(land_shapes, srcs)] if lands is None else lands

    def body(*refs):
        sems = refs[2 * n + len(extra):]
        for cp in _peer_copies(kind, refs[:n], refs[n:2 * n], sems[0], sems[1]):
            cp.start()
        refs[-1][...] = jnp.zeros_like(refs[-1])

    def hbm(a):
        return pltpu.with_memory_space_constraint(a, pltpu.HBM)

    out = pl.pallas_call(
        body, name=name,
        out_shape=(pltpu.SemaphoreType.DMA((7 * n,)), pltpu.SemaphoreType.DMA((7 * n,)),
                   *[pltpu.HBM(s.shape, s.dtype) for s in srcs],
                   *[pltpu.HBM(ls, s.dtype) for ls, s in zip(land_shapes, srcs)],
                   jax.ShapeDtypeStruct((8, LANES), F32)),
        in_specs=[HBM] * (2 * n) + [ANY] * len(extra),
        out_specs=(SEM, SEM, *[HBM] * (2 * n), pl.BlockSpec(memory_space=pltpu.VMEM)),
        input_output_aliases={i: 2 + i for i in range(2 * n)},
        compiler_params=pltpu.CompilerParams(has_side_effects=DATAFLOW),
    )(*[hbm(s) for s in srcs], *[hbm(a) for a in lands], *extra)
    return (kind, n, out[:-1]), out[-1]


def _copies_wait(state, after, name):
    kind, n, (send_sems, recv_sems, *thru) = state
    after = list(after) if isinstance(after, (list, tuple)) else [after]

    def body(*refs):
        for cp in _peer_copies(kind, refs[:n], refs[n:2 * n], refs[2 * n], refs[2 * n + 1]):
            cp.wait_send()
            cp.wait_recv()

    out = pl.pallas_call(
        body, name=name,
        out_shape=tuple(pltpu.HBM(t.shape, t.dtype) for t in thru),
        in_specs=[HBM] * (2 * n) + [SEM, SEM] + [ANY] * len(after), out_specs=tuple([HBM] * (2 * n)),
        input_output_aliases={i: i for i in range(2 * n)},
        compiler_params=pltpu.CompilerParams(has_side_effects=DATAFLOW),
    )(*thru, send_sems, recv_sems, *after)
    return out[:n], out[n:]


def _adamw(w, g, m, v):
    m2 = ADAM_B1 * m + (1.0 - ADAM_B1) * g
    v2 = ADAM_B2 * v + (1.0 - ADAM_B2) * (g * g)
    m_hat = m2 / (1.0 - ADAM_B1 ** ADAM_STEP)
    v_hat = v2 / (1.0 - ADAM_B2 ** ADAM_STEP)
    delta = -ADAM_LR * (m_hat / (jnp.sqrt(v_hat) + ADAM_EPS) + ADAM_WD * w)
    return delta, m2, v2


def _sum_adamw(land, part, me_idx, w, m, v, name, tile=256):
    r, c = w.shape
    tr = min(tile, r)

    def body(me_ref, own_ref, *refs):
        slots = refs[:N_DEV]
        w_ref, m_ref, v_ref, g_ref, d_ref, m2_ref, v2_ref = refs[N_DEV:]
        own = own_ref[...].astype(F32)
        g = None
        for s in range(N_DEV):
            term = jnp.where(me_ref[0] == s, own, slots[s][...].astype(F32))
            g = term if g is None else g + term
        d, m2, v2 = _adamw(w_ref[...], g, m_ref[...], v_ref[...])
        g_ref[...] = g
        d_ref[...] = d
        m2_ref[...] = m2
        v2_ref[...] = v2

    def slot_spec(s):
        return pl.BlockSpec((None, tr, c), lambda i, me: (jnp.where(me[0] == s, (s + 1) % N_DEV, s), i, 0))

    row = pl.BlockSpec((tr, c), lambda i, me: (i, 0))
    return pl.pallas_call(
        body, name=name,
        grid_spec=pltpu.PrefetchScalarGridSpec(
            num_scalar_prefetch=1, grid=(r // tr,),
            in_specs=[pl.BlockSpec((None, tr, c), lambda i, me: (me[0], i, 0))] + [slot_spec(s) for s in range(N_DEV)] + [row] * 3,
            out_specs=[row] * 4),
        out_shape=[jax.ShapeDtypeStruct((r, c), F32)] * 4,
        compiler_params=_cparams("parallel"),
    )(me_idx, part, *[land] * N_DEV, w, m, v)


def _small_update(name, me_idx, kinds, lands, owns, ws, ms, vs, sums=()):
    n = len(ws)
    lands, owns = list(lands) + [s[0] for s in sums], list(owns) + [s[1] for s in sums]
    kinds = list(kinds) + ["gather"] * len(sums)
    nl = len(lands)

    def summed(me, land_ref, own):
        g = None
        for s in range(N_DEV):
            term = jnp.where(me == s, own, land_ref[s])
            g = term if g is None else g + term
        return g

    def body(me_ref, *refs):
        land_refs, own_refs = refs[:nl], refs[nl:2 * nl]
        w_refs, m_refs, v_refs = (refs[2 * nl + i * n:2 * nl + (i + 1) * n] for i in range(3))
        outs = refs[2 * nl + 3 * n:]
        me = me_ref[0]
        for i in range(n):
            g = summed(me, land_refs[i], own_refs[i][...])
            d, m2, v2 = _adamw(w_refs[i][...], g, m_refs[i][...], v_refs[i][...])
            for ref, val in zip(outs[4 * i:4 * i + 4], (g, d, m2, v2)):
                ref[...] = val
        for i in range(n, nl):
            outs[4 * n + i - n][...] = summed(me, land_refs[i], own_refs[i][...])

    def whole(shape):
        return pl.BlockSpec(shape, lambda i, me, nd=len(shape): (0,) * nd)

    def own_spec(kind, own):
        if kind == "gather":
            return whole(own.shape)
        return pl.BlockSpec((None, *own.shape[1:]), lambda i, me: (me[0], 0, 0))

    shapes = [w.shape for w in ws]
    out_shapes = [s for s in shapes for _ in range(4)] + [s[1].shape for s in sums]
    return pl.pallas_call(
        body, name=name,
        grid_spec=pltpu.PrefetchScalarGridSpec(
            num_scalar_prefetch=1, grid=(1,),
            in_specs=[whole(a.shape) for a in lands] + [own_spec(k, o) for k, o in zip(kinds, owns)]
            + [whole(s) for s in shapes] * 3,
            out_specs=[whole(s) for s in out_shapes]),
        out_shape=[jax.ShapeDtypeStruct(s, F32) for s in out_shapes],
        compiler_params=_cparams("arbitrary"),
    )(me_idx, *lands, *owns, *ws, *ms, *vs)


def _small_view(n, a):
    if a.ndim == 1:
        return a.reshape(1, -1)
    if a.ndim == 3:
        return a.transpose(1, 2, 0).reshape(QKV_BLOCK * QKV_BLOCK, -1)
    return a.T if n == "w_if" else a


def _small_unview(n, a, shape):
    if len(shape) == 1:
        return a.reshape(shape)
    if len(shape) == 3:
        return a.reshape(QKV_BLOCK, QKV_BLOCK, -1).transpose(2, 0, 1)
    return a.T if n == "w_if" else a


def _small_shards(n, g):
    if n == "w_if":
        return g.reshape(N_DEV, -1, g.shape[1]).transpose(0, 2, 1)
    return g.reshape(g.shape[0], N_DEV, -1).transpose(1, 0, 2)


def _small_unshard(n, s):
    if n == "w_if":
        return s.transpose(0, 2, 1).reshape(-1, s.shape[1])
    return s.transpose(1, 0, 2).reshape(s.shape[1], -1)


def _to_hm(a, d):
    t = a.shape[0]
    return a.reshape(t, HEADS, d).transpose(1, 0, 2)


def _from_hm(a):
    h, t, d = a.shape
    return a.transpose(1, 0, 2).reshape(t, h * d)


def _gate_rows(g):
    t = g.shape[0]
    return g.T.reshape(HEADS, t // CHUNK, 1, CHUNK)


def _gate_cols(g):
    h, nc, _, c = g.shape
    return g.reshape(h, nc * c).T


def _blockdiag_dense(w):
    n = w.shape[0] * QKV_BLOCK
    tiled = jnp.tile(w.reshape(n, QKV_BLOCK), (1, n // QKV_BLOCK))
    r = lax.broadcasted_iota(jnp.int32, (n, n), 0)
    c = lax.broadcasted_iota(jnp.int32, (n, n), 1)
    return jnp.where(r // QKV_BLOCK == c // QKV_BLOCK, tiled, 0.0)


def _blockdiag_blocks(dense):
    n = dense[0].shape[0]
    k = len(dense)

    def body(*refs):
        r = lax.broadcasted_iota(jnp.int32, (n, n), 0)
        c = lax.broadcasted_iota(jnp.int32, (n, n), 1)
        fr = lax.broadcasted_iota(jnp.int32, (n, LANES), 0)
        fc = lax.broadcasted_iota(jnp.int32, (n, LANES), 1)
        fold = ((fr & (QKV_BLOCK - 1)) == fc).astype(BF16)
        for i in range(k):
            kept = jnp.where((r >> 2) == (c >> 2), refs[i][...], 0.0)
            refs[k + i][...] = sum(lax.dot_general(t, fold, _dims("nn", 2), preferred_element_type=F32) for t in _split3(kept))

    out = pl.pallas_call(body, name="blockdiag_blocks", out_shape=[jax.ShapeDtypeStruct((n, LANES), F32)] * k)(*dense)
    return [o[:, 0:QKV_BLOCK].reshape(n // QKV_BLOCK, QKV_BLOCK, QKV_BLOCK) for o in out]


def _col_blocks(w):
    k, n = w.shape
    return w.reshape(k, N_DEV, n // N_DEV).transpose(1, 0, 2)


def _from_col_blocks(g):
    d, k, n = g.shape
    return g.transpose(1, 0, 2).reshape(k, d * n)


def _local_step(x, tgt, weight, ws, prefetch, pass_on, on_grads, on_small):
    t, d = x.shape
    g1 = ws["g_pre_mix"]

    def dep(token):
        return () if token is None else (token,)

    w_in = weight("w_in", x)
    fetch_mix = prefetch(("w_pa", "w_pb", "w_o"), w_in)

    n_in = w_in.shape[2]

    def proj_in_fwd(xv, g, w):
        hv = _rms(xv, g)
        return (hv, jnp.concatenate([_raw_dot(hv, w[j], "nn") for j in range(N_DEV)], axis=1)), ()

    h, proj = _rowwise("proj_in", proj_in_fwd, [x], [g1, w_in], [(d, BF16), (N_DEV * n_in, F32)], deps=dep(fetch_mix))
    offs = [0]
    for s in IN_SPLITS:
        offs.append(offs[-1] + s)
    q_a, k_a, v_a, g_a, a_low, x_m, o_pre, gate_a, gate_b = [proj[:, offs[i]:offs[i + 1]] for i in range(9)]

    a_low_p = jnp.pad(a_low, ((0, 0), (0, LANES - LOWRANK)))
    w_a_up_p = jnp.pad(ws["w_a_up"], ((0, LANES - LOWRANK), (0, 0)))
    b_a_up = ws["b_a_up"]
    (la,) = _rowwise("gla_decay", lambda al, w, b: ((_log_decay(al, w, b),), ()), [a_low_p], [w_a_up_p, b_a_up],
                     [(HEADS * GLA_DK, F32)])
    fetch_up = prefetch(("w_up",), la)
    pass_mix = pass_on("w_pa", la)
    q_hm, k_hm, la_hm = _to_hm(q_a, GLA_DK), _to_hm(k_a, GLA_DK), _to_hm(la, GLA_DK)
    o_gla, s_prev = _gla_fwd(q_hm, k_hm, v_a, la_hm, deps=dep(fetch_up) + dep(pass_mix))
    gn = ws["g_gla_norm"]
    ml_w = HEADS * HEAD_W

    def proj_a_fwd(o, g, n_, w):
        ya = jnp.concatenate(_per_head(_gla_out, [o, g], [], [n_]), axis=1)
        return (ya, _raw_dot(ya, w, "nn")), ()

    ya_in, y_a = _rowwise("proj_a", proj_a_fwd, [o_gla, g_a], [gn, weight("w_pa", o_gla)], [(ml_w, BF16), (d, F32)],
                          tile=512)

    cw = ws["conv_w"]
    w_if_p = jnp.pad(ws["w_if"], ((0, 0), (0, LANES - 2 * HEADS)))
    pre_params = [cw[0:1], cw[1:2], cw[2:3], cw[3:4], ws["conv_b"],
                  _blockdiag_dense(ws["w_q_ml"]), _blockdiag_dense(ws["w_k_ml"]), _blockdiag_dense(ws["w_v_ml"]),
                  w_if_p[0:ml_w], w_if_p[ml_w:2 * ml_w], w_if_p[2 * ml_w:3 * ml_w],
                  jnp.pad(ws["b_if"], ((0, 0), (0, LANES - 2 * HEADS)))]
    x_pad = jnp.pad(x_m, ((HALO, 0), (0, 0)))
    xc, q_m, k_m, v_m, gl = _ml_pre_fwd(x_m, x_pad, pre_params)
    li, lf = _gate_rows(gl[:, 0:HEADS]), _gate_rows(gl[:, HEADS:2 * HEADS])
    hc, c_prev, n_prev, m_prev = _ml_fwd(q_m, k_m, v_m, li, lf)
    fetch_down = prefetch(("w_down",), hc)
    g_ml, skip = ws["g_ml_norm"], ws["ml_skip"]

    def proj_b_fwd(a, b, c_, ga, gb, ya, g, s, w):
        hb = jnp.concatenate(_per_head(_ml_out, [a, b, c_], [g, s]), axis=1)
        yb = _raw_dot(hb, w, "nn")
        return (hb, yb, _merge(ga, gb, ya, yb)), ()

    h_b, y_b, merged = _rowwise("proj_b", proj_b_fwd, [hc, o_pre, xc, gate_a, gate_b, y_a], [g_ml, skip, weight("w_pb", hc)],
                                [(ml_w, BF16), (d, F32), (d, BF16)], tile=512, deps=dep(fetch_down))

    gpm, gpl, gpo = ws["g_post_mix"], ws["g_pre_mlp"], ws["g_post_mlp"]

    def proj_o_fwd(mg, xv, w, a, b):
        zv = _raw_dot(mg, w, "nn")
        return (zv, *_post_mix(xv, zv, a, b)), ()

    pass_up = pass_on("w_up", merged)
    z, x1, h2 = _rowwise("proj_o", proj_o_fwd, [merged, x], [weight("w_o", merged), gpm, gpl],
                         [(d, F32), (d, F32), (d, BF16)], tile=512, deps=dep(pass_up))
    pass_down = pass_on("w_down", h2)
    w_up = weight("w_up", h2)
    up, u = _mm(h2, w_up, "nn", (BF16, BF16), "mlp_up", tm=2048, shards="b", deps=dep(pass_down),
                epilogue=lambda p: (p, jnp.square(jnp.maximum(p, 0.0))))

    def mlp_down_loss(uv, x1v, tgtv, w, g):
        dnv = _raw_dot(uv, w, "nn")
        loss, vjp = jax.vjp(lambda a, b, c_: _loss_rows(a, b, tgtv, c_), x1v, dnv, g)
        dx1, ddn, dg = vjp(jnp.ones((1, 1), F32))
        return (dx1, ddn), (jnp.broadcast_to(loss, (1, LANES)), dg)

    dx1_y, d_dn, loss, d_gpo = _rowwise("mlp_down", mlp_down_loss, [u, x1, tgt], [weight("w_down", u), gpo],
                                        [(d, F32), (d, BF16)], [((1, LANES), F32), ((1, d), F32)], tile=512)

    (d_up,) = _mm(d_dn, weight("w_down", u), "nt", (BF16,), "mlp_down_dx", extra=[up],
                  epilogue=lambda p, a: (p * (2.0 * jnp.maximum(a.astype(F32), 0.0)),))
    dw_down = _mm(u, d_dn, "tn", BF16, "mlp_down_dw", tm=512)
    dw_up = _mm(h2, d_up, "tn", BF16, "mlp_up_dw", tn=w_up.shape[2], shards="out")
    sent_mlp = on_grads(dict(w_down=dw_down, w_up=dw_up))

    def mlp_up_dx(dup, xv, zv, dx1, w, a, b):
        _, vjp = jax.vjp(_post_mix, xv, zv, a, b)
        ns = w.shape[2]
        dh2 = sum(_raw_dot(dup[:, j * ns:(j + 1) * ns], w[j], "nt") for j in range(w.shape[0]))
        dx, dz, da, db = vjp((dx1, dh2))
        return (dx, dz), (da, db)

    dx_res, d_z, d_gpm, d_gpl = _rowwise("mlp_up_dx", mlp_up_dx, [d_up, x, z, dx1_y], [w_up, gpm, gpl],
                                         [(d, F32), (d, BF16)], [((1, d), F32), ((1, d), F32)], tile=512, deps=dep(sent_mlp))
    dw_o = _mm(merged, d_z, "tn", BF16, "proj_o_dw")

    def proj_o_dx(dz, ga, gb, ya, yb, w):
        return jax.vjp(_merge, ga, gb, ya, yb)[1](_raw_dot(dz, w, "nt")), ()

    d_ga, d_gb, d_ya, d_yb = _rowwise("proj_o_dx", proj_o_dx, [d_z, gate_a, gate_b, y_a, y_b], [weight("w_o", merged)],
                                      [(d, F32), (d, F32), (d, BF16), (d, BF16)], tile=512)
    dw_pa = _mm(ya_in, d_ya, "tn", BF16, "proj_a_dw")
    dw_pb = _mm(h_b, d_yb, "tn", BF16, "proj_b_dw")
    sent_mix = on_grads(dict(w_o=dw_o, w_pa=dw_pa, w_pb=dw_pb))

    def proj_b_dx(dyb, a, b, c_, w, g, s):
        ct = _raw_dot(dyb, w, "nt")
        parts = []
        for hs in _head_slices(HEAD_W):
            _, vjp = jax.vjp(_ml_out, a[:, hs], b[:, hs], c_[:, hs], g[:, hs], s[:, hs])
            parts.append(vjp(ct[:, hs]))
        cat = lambda i: jnp.concatenate([p[i] for p in parts], axis=1)
        return (cat(0), cat(1), cat(2)), (cat(3), cat(4))

    d_hc, d_opre, d_xc, d_gml, d_skip = _rowwise("proj_b_dx", proj_b_dx, [d_yb, hc, o_pre, xc],
                                                 [weight("w_pb", hc), g_ml, skip], [(ml_w, F32)] * 3, [((1, ml_w), F32)] * 2,
                                                 tile=512, deps=dep(sent_mix))
    d_qm, d_km, d_vm, d_li, d_lf = _ml_bwd(q_m, k_m, v_m, li, lf, c_prev, n_prev, m_prev, d_hc)
    d_gl = jnp.concatenate([_gate_cols(d_li), _gate_cols(d_lf), jnp.zeros((t, LANES - 2 * HEADS), F32)], axis=1)
    pre_grads = _ml_pre_bwd(x_m, x_pad, pre_params, [d_xc, d_qm, d_km, d_vm, d_gl])
    d_xm = pre_grads[0]
    d_cw = jnp.concatenate(pre_grads[1:5], axis=0)
    d_cb = pre_grads[5]
    d_wq, d_wk, d_wv = _blockdiag_blocks(pre_grads[6:9])
    d_wif = jnp.concatenate(pre_grads[9:12], axis=0)[:, 0:2 * HEADS]
    d_bif = pre_grads[12][:, 0:2 * HEADS]

    def proj_a_dx(dya, o, g, w, n_):
        ct = _raw_dot(dya, w, "nt")
        parts = []
        for hs in _head_slices(HEAD_W):
            _, vjp = jax.vjp(_gla_out, o[:, hs], g[:, hs], n_)
            parts.append(vjp(ct[:, hs]))
        cat = lambda i: jnp.concatenate([p[i] for p in parts], axis=1)
        return (cat(0), cat(1)), (sum(p[2] for p in parts),)

    d_o, d_g_a, d_gn = _rowwise("proj_a_dx", proj_a_dx, [d_ya, o_gla, g_a], [weight("w_pa", o_gla), gn], [(ml_w, F32)] * 2,
                                [((1, HEAD_W), F32)], tile=512)
    dq_hm, dk_hm, d_va, dla_hm = _gla_bwd(q_hm, k_hm, v_a, la_hm, s_prev, d_o)

    def decay_bwd(al, ct, w, b):
        _, vjp = jax.vjp(_log_decay, al, w, b)
        dal, dw, db = vjp(ct)
        return (dal,), (dw, db)

    d_alow_p, d_wa_p, d_ba = _rowwise("gla_decay_bwd", decay_bwd, [a_low_p, _from_hm(dla_hm)], [w_a_up_p, b_a_up],
                                      [(LANES, F32)], [(w_a_up_p.shape, F32), (b_a_up.shape, F32)])
    d_proj = jnp.concatenate([_from_hm(dq_hm), _from_hm(dk_hm), d_va, d_g_a, d_alow_p[:, 0:LOWRANK], d_xm, d_opre, d_ga, d_gb],
                             axis=1).astype(BF16)
    small = dict(w_a_up=d_wa_p[0:LOWRANK], b_a_up=d_ba, g_gla_norm=d_gn, conv_w=d_cw, conv_b=d_cb,
                 w_q_ml=d_wq, w_k_ml=d_wk, w_v_ml=d_wv, w_if=d_wif, b_if=d_bif, ml_skip=d_skip, g_ml_norm=d_gml,
                 g_post_mix=d_gpm, g_pre_mlp=d_gpl, g_post_mlp=d_gpo)
    sent_small = on_small(small, loss)
    dw_in = _mm_shard_cols(h, d_proj, n_in, "proj_in_dw", deps=dep(sent_small))
    sent_in = on_grads(dict(w_in=dw_in))

    def proj_in_dx(dp, xv, dres, w, g):
        _, vjp = jax.vjp(_rms, xv, g)
        dx, dg = vjp(sum(_raw_dot(dp[:, j * n_in:(j + 1) * n_in], w[j], "nt") for j in range(N_DEV)))
        return (dx + dres,), (dg,)

    grad_x, d_g1 = _rowwise("proj_in_dx", proj_in_dx, [d_proj, x, dx_res], [w_in, g1], [(d, F32)], [((1, d), F32)],
                            deps=dep(sent_in))
    return grad_x, on_small(dict(g_pre_mix=d_g1), None)


BIG = ("w_in", "w_pa", "w_pb", "w_o", "w_up", "w_down")
BIG_COL_SHARDED = ("w_in", "w_pa", "w_pb", "w_up")
SMALL_SHARDED = ("w_a_up", "conv_w", "w_if")
SMALL = ("g_pre_mix", "w_a_up", "b_a_up", "g_gla_norm", "conv_w", "conv_b", "w_q_ml", "w_k_ml", "w_v_ml", "w_if", "b_if",
         "ml_skip", "g_ml_norm", "g_post_mix", "g_pre_mlp", "g_post_mlp")
WEIGHTS = ("g_pre_mix", "w_in", "w_a_up", "b_a_up", "g_gla_norm", "conv_w", "conv_b", "w_q_ml", "w_k_ml", "w_v_ml", "w_if", "b_if",
           "ml_skip", "g_ml_norm", "w_pa", "w_pb", "w_o", "g_post_mix", "g_pre_mlp", "w_up", "w_down", "g_post_mlp")


def kernel(x, g_pre_mix, w_in, w_a_up, b_a_up, g_gla_norm, conv_w, conv_b, w_q_ml, w_k_ml, w_v_ml, w_if, b_if, ml_skip, g_ml_norm, w_pa, w_pb, w_o, g_post_mix, g_pre_mlp, w_up, w_down, g_post_mlp, loss_target, m_g_pre_mix, m_w_in, m_w_a_up, m_b_a_up, m_g_gla_norm, m_conv_w, m_conv_b, m_w_q_ml, m_w_k_ml, m_w_v_ml, m_w_if, m_b_if, m_ml_skip, m_g_ml_norm, m_w_pa, m_w_pb, m_w_o, m_g_post_mix, m_g_pre_mlp, m_w_up, m_w_down, m_g_post_mlp, v_g_pre_mix, v_w_in, v_w_a_up, v_b_a_up, v_g_gla_norm, v_conv_w, v_conv_b, v_w_q_ml, v_w_k_ml, v_w_v_ml, v_w_if, v_b_if, v_ml_skip, v_g_ml_norm, v_w_pa, v_w_pb, v_w_o, v_g_post_mix, v_g_pre_mlp, v_w_up, v_w_down, v_g_post_mlp):
    args = dict(locals())
    w = {n: args[n][0] for n in WEIGHTS}
    m = {n: args["m_" + n][0] for n in WEIGHTS}
    v = {n: args["v_" + n][0] for n in WEIGHTS}

    me_lin = _lin(_me())
    me_idx = jnp.reshape(me_lin, (1,)).astype(jnp.int32)

    def full_weight(n, g):
        if n in ("w_in", "w_up"):
            return g
        return _from_col_blocks(g) if n in BIG_COL_SHARDED else g.reshape(-1, g.shape[-1])

    def grad_parts(n, g):
        if n in ("w_in", "w_up"):
            return g
        return (_col_blocks(g) if n in BIG_COL_SHARDED else g.reshape(N_DEV, -1, g.shape[-1])).astype(BF16)

    sharded_names = tuple(SMALL_SHARDED)
    small_w_state, small_w_token = _copies_start("gather", [_small_view(n, w[n]) for n in sharded_names],
                                                 "allgather_start_small_weights")
    ready = {"w_in": full_weight("w_in", _allgather_big([w["w_in"].astype(BF16)], "allgather_w_in", [small_w_token])[0])}
    pending = {}

    def prefetch(group, after):
        state, token = _copies_start("gather_chips", [w[n].astype(BF16) for n in group], "allgather_start_" + group[0], after)
        for n in group:
            pending[n] = (group, state)
        return token

    passing = {}

    def pass_on(n, after):
        group, state = pending[n]
        shards, lands = _copies_wait(state, after, "allgather_wait_" + group[0])
        state, token = _copies_start("gather_pass", shards, "allgather_pass_" + group[0], lands=lands)
        for gn in group:
            passing[gn] = (group, state)
        return token

    def weight(n, after):
        if n not in ready:
            group, state = passing[n]
            shards, lands = _copies_wait(state, after, "allgather_passed_" + group[0])
            for gn, shard, land in zip(group, shards, lands):
                ready[gn] = full_weight(gn, lax.dynamic_update_slice(land, shard[None], (me_lin, 0, 0)))
        return ready[n]

    small_w_own, small_w_lands = _copies_wait(small_w_state, ready["w_in"], "allgather_wait_small_weights")
    ws = {n: (w[n].reshape(1, -1) if w[n].ndim == 1 else w[n]) for n in SMALL if n not in SMALL_SHARDED}
    for n, own, land in zip(sharded_names, small_w_own, small_w_lands):
        ws[n] = _small_unshard(n, lax.dynamic_update_slice(land, own[None], (me_lin, 0, 0)))

    sent = []

    def on_grads(grads):
        names = tuple(grads)
        state, token = _copies_start("exchange", [grad_parts(n, grads[n]) for n in names], "exchange_start_" + names[0])
        sent.append((names, state))
        return token

    small_sent = []

    def on_small(small, loss):
        names = tuple(small)
        kinds = ["exchange" if n in SMALL_SHARDED else "gather" for n in names]
        srcs = [_small_shards(n, small[n]) if n in SMALL_SHARDED else _small_view(n, small[n]) for n in names]
        extra = [] if loss is None else [loss]
        state, token = _copies_start(kinds + ["gather"] * len(extra), srcs + extra, "allgather_start_small_" + names[0])
        small_sent.append((names, kinds, state))
        return token

    grad_x, last_token = _local_step(x[0], loss_target[0], weight, ws, prefetch, pass_on, on_grads, on_small)

    out = {}

    def finish(names, state, after):
        parts, lands = _copies_wait(state, after, "exchange_wait_" + names[0])
        for n, part, land in zip(names, parts, lands):
            out[n] = _sum_adamw(land, part, me_idx, w[n], m[n], v[n], "adamw_" + n)

    def finish_small(names, kinds, state, after):
        own, lands = _copies_wait(state, after, "allgather_wait_small_" + names[0])
        k = len(names)
        upd = _small_update("adamw_small_" + names[0], me_idx, kinds, lands[:k], own[:k],
                            *[[_small_view(n, d[n]) for n in names] for d in (w, m, v)], sums=list(zip(lands[k:], own[k:])))
        for i, n in enumerate(names):
            out[n] = tuple(_small_unview(n, a, w[n].shape) for a in upd[4 * i:4 * i + 4])
        return upd[4 * k:]

    (loss_sum,) = finish_small(*small_sent[0], [grad_x, last_token])
    for names, state in sent[:-1]:
        finish(names, state, [grad_x, last_token])
    finish(*sent[-1], [loss_sum] + [out[n][1] for n in BIG if n in out])
    finish_small(*small_sent[1], [out["w_in"][1]])

    shaped = lambda a, n: a.reshape(args[n].shape)
    return (loss_sum[0, 0], grad_x[None],
            *[shaped(out[n][0], n) for n in WEIGHTS], *[shaped(out[n][1], n) for n in WEIGHTS],
            *[shaped(out[n][2], n) for n in WEIGHTS], *[shaped(out[n][3], n) for n in WEIGHTS])
```

```python
import functools

import jax
import jax.numpy as jnp
from jax import lax
from jax.experimental import pallas as pl
from jax.experimental.pallas import tpu as pltpu

F32 = jnp.float32
BF16 = jnp.bfloat16
MESH = pl.DeviceIdType.MESH

N_DEV = 8
EPS = 1e-6
CHUNK = 64
CHUNKS_PER_STEP = 4
HEADS = 4
GLA_DK = 64
HEAD_W = 128
GLA_GATE_NORM = 16.0
LOWRANK = 16
CONV_K = 4
QKV_BLOCK = 4
LANES = 128
HALO = 8
IN_SPLITS = (256, 256, 512, 512, 16, 512, 512, 1024, 1024)

ADAM_LR = 0.001
ADAM_B1 = 0.9
ADAM_B2 = 0.999
ADAM_EPS = 1e-08
ADAM_WD = 0.01
ADAM_STEP = 10

VMEM_LIMIT = 56 * 1024 * 1024


def _cparams(*sem):
    return pltpu.CompilerParams(dimension_semantics=sem, vmem_limit_bytes=VMEM_LIMIT)


def _dims(mode, ndim):
    contract = {"nn": ((ndim - 1,), (ndim - 2,)), "nt": ((ndim - 1,), (ndim - 1,)), "tn": ((ndim - 2,), (ndim - 2,))}[mode]
    return contract, (((0,), (0,)) if ndim == 3 else ((), ()))


def _raw_dot(a, b, mode):
    return lax.dot_general(a.astype(BF16), b.astype(BF16), _dims(mode, a.ndim), preferred_element_type=F32)


@functools.partial(jax.custom_vjp, nondiff_argnums=(2,))
def _bdot(a, b, mode):
    return _raw_dot(a, b, mode)


def _bdot_fwd(a, b, mode):
    return _raw_dot(a, b, mode), (a, b)


def _bdot_bwd(mode, res, ct):
    a, b = res
    if mode == "nn":
        da, db = _raw_dot(ct, b, "nt"), _raw_dot(a, ct, "tn")
    elif mode == "nt":
        da, db = _raw_dot(ct, b, "nn"), _raw_dot(ct, a, "tn")
    else:
        da, db = _raw_dot(b, ct, "nt"), _raw_dot(a, ct, "nn")
    return da.astype(a.dtype), db.astype(b.dtype)


_bdot.defvjp(_bdot_fwd, _bdot_bwd)


def _split3(x):
    hi = x.astype(BF16)
    r1 = x - hi.astype(F32)
    mid = r1.astype(BF16)
    return hi, mid, (r1 - mid.astype(F32)).astype(BF16)


def _split_dot(tri, x):
    if x.ndim == 3:
        tri = jnp.broadcast_to(tri, (x.shape[0], *tri.shape))
    return sum(lax.dot_general(tri, t, _dims("nn", x.ndim), preferred_element_type=F32) for t in _split3(x))


def _tri(n, lower):
    r = lax.broadcasted_iota(jnp.int32, (n, n), 0)
    c = lax.broadcasted_iota(jnp.int32, (n, n), 1)
    return ((c <= r) if lower else (c >= r)).astype(BF16)


@jax.custom_vjp
def _cumsum_rows(x):
    return _split_dot(_tri(x.shape[-2], True), x)


def _cumsum_rows_fwd(x):
    return _cumsum_rows(x), None


def _cumsum_rows_bwd(_, ct):
    return (_split_dot(_tri(ct.shape[-2], False), ct),)


_cumsum_rows.defvjp(_cumsum_rows_fwd, _cumsum_rows_bwd)


def _abs(x):
    return jnp.where(x >= 0, x, -x)


def _sigmoid(x):
    return lax.logistic(x)


def _log_sigmoid(x):
    return jnp.minimum(x, 0.0) - jnp.log(1.0 + jnp.exp(-_abs(x)))


def _rms(x, g):
    return x * lax.rsqrt(jnp.mean(x * x, axis=-1, keepdims=True) + EPS) * g


def _head_slices(w):
    return [slice(h * w, (h + 1) * w) for h in range(HEADS)]


def _heads(ref, rows=slice(None)):
    return jnp.stack([ref[rows, hs] for hs in _head_slices(HEAD_W)])


def _put_heads(ref, val, rows=slice(None)):
    for h, hs in enumerate(_head_slices(HEAD_W)):
        ref[rows, hs] = val[h]


def _tile(dim, want):
    if dim <= want or dim % LANES:
        return dim
    t = want
    while dim % t:
        t -= LANES
    return t


def _mm(a, b, mode, out_dtype, name, tm=1024, tn=1024, tk=4096, epilogue=None, extra=(), deps=(), shards=None):
    if shards == "b":
        assert mode == "nn"
        ns = b.shape[2]
        (m, k), (k2, n) = a.shape, (b.shape[1], b.shape[0] * ns)
        tn = ns
    elif mode == "nn":
        (m, k), (k2, n) = a.shape, b.shape
    elif mode == "nt":
        (m, k), (n, k2) = a.shape, b.shape
    else:
        (k, m), (k2, n) = a.shape, b.shape
    assert k == k2, (name, a.shape, b.shape)
    tm, tn, tk = _tile(m, tm), _tile(n, tn), _tile(k, tk)
    nk = k // tk
    out_dtypes = out_dtype if epilogue else (out_dtype,)
    assert nk == 1 or (out_dtype == F32 and not epilogue), name
    n_in = 2 + len(extra)

    def body(*refs):
        p = _raw_dot(refs[0][...], refs[1][...], mode)
        if nk > 1:
            _accumulate(pl.program_id(2), [refs[n_in + len(deps)]], [p])
            return
        outs = epilogue(p, *[r[...] for r in refs[2:n_in]]) if epilogue else (p,)
        for ref, val in zip(refs[n_in + len(deps):], outs):
            ref[...] = val.astype(ref.dtype)

    a_spec = pl.BlockSpec((tk, tm), lambda i, j, kk: (kk, i)) if mode == "tn" else pl.BlockSpec((tm, tk), lambda i, j, kk: (i, kk))
    if shards == "b":
        b_spec = pl.BlockSpec((None, tk, tn), lambda i, j, kk: (j, kk, 0))
    elif mode == "nt":
        b_spec = pl.BlockSpec((tn, tk), lambda i, j, kk: (j, kk))
    else:
        b_spec = pl.BlockSpec((tk, tn), lambda i, j, kk: (kk, j))
    o_spec = pl.BlockSpec((tm, tn), lambda i, j, kk: (i, j))
    if shards == "out":
        out_specs, out_shape = [pl.BlockSpec((None, tm, tn), lambda i, j, kk: (j, i, 0))], (n // tn, m, tn)
    else:
        out_specs, out_shape = [o_spec] * len(out_dtypes), (m, n)
    res = pl.pallas_call(
        body, name=name, grid=(m // tm, n // tn, nk),
        in_specs=[a_spec, b_spec] + [o_spec] * len(extra) + [ANY] * len(deps), out_specs=out_specs,
        out_shape=[jax.ShapeDtypeStruct(out_shape, dt) for dt in out_dtypes],
        compiler_params=_cparams("parallel", "parallel", "arbitrary"),
    )(a, b, *extra, *deps)
    return res if epilogue else res[0]


def _mm_shard_cols(a, b, n, name, tm=512, tk=1024, deps=()):
    t, k = a.shape
    nb = b.shape[1] // n
    tk = min(tk, t)
    nk = t // tk

    def body(a_ref, b_ref, *rest):
        o_ref, acc_ref = rest[len(deps):]
        step = pl.program_id(1)
        a_t = a_ref[...].astype(BF16).T
        for j in range(nb):
            _accumulate(step, [acc_ref.at[j]], [_raw_dot(a_t, b_ref[:, j * n:(j + 1) * n], "nn")])

        @pl.when(step == nk - 1)
        def _():
            o_ref[...] = acc_ref[...].astype(BF16)

    return pl.pallas_call(
        body, name=name, grid=(k // tm, nk),
        in_specs=[pl.BlockSpec((tk, tm), lambda i, s: (s, i)), pl.BlockSpec((tk, nb * n), lambda i, s: (s, 0))] + [ANY] * len(deps),
        out_specs=pl.BlockSpec((nb, tm, n), lambda i, s: (0, i, 0)),
        out_shape=jax.ShapeDtypeStruct((nb, k, n), BF16),
        scratch_shapes=[pltpu.VMEM((nb, tm, n), F32)],
        compiler_params=_cparams("parallel", "arbitrary"),
    )(a, b, *deps)


def _rowwise(name, fn, rows, params, out_rows, out_accs=(), tile=256, deps=()):
    t = rows[0].shape[0]
    r = min(tile, t)
    assert t % r == 0
    n_in, n_or = len(rows) + len(params), len(out_rows)
    n_all = n_in + len(deps)
    params = list(params) + list(deps)

    def body(*refs):
        vals = [ref[...] for ref in refs[:n_in]]
        outs = refs[n_all:]
        ro, ao = fn(*vals)
        for ref, v in zip(outs[:n_or], ro):
            ref[...] = v.astype(ref.dtype)
        if out_accs:
            _accumulate(pl.program_id(0), outs[n_or:], ao)

    def full(shape):
        return pl.BlockSpec(shape, lambda i, nd=len(shape): (0,) * nd)

    return pl.pallas_call(
        body, name=name, grid=(t // r,),
        in_specs=[pl.BlockSpec((r, a.shape[1]), lambda i: (i, 0)) for a in rows] + [full(p.shape) for p in params],
        out_specs=[pl.BlockSpec((r, w), lambda i: (i, 0)) for w, _ in out_rows] + [full(s) for s, _ in out_accs],
        out_shape=[jax.ShapeDtypeStruct((t, w), dt) for w, dt in out_rows] + [jax.ShapeDtypeStruct(s, dt) for s, dt in out_accs],
        compiler_params=_cparams("arbitrary"),
    )(*rows, *params)


def _accumulate(step, refs, vals):
    for ref, v in zip(refs, vals):
        @pl.when(step == 0)
        def _(ref=ref, v=v):
            ref[...] = v.astype(ref.dtype)

        @pl.when(step > 0)
        def _(ref=ref, v=v):
            ref[...] += v.astype(ref.dtype)


def _gla_chunk(q, k, v, la, st):
    c = q.shape[-2]
    row = lax.broadcasted_iota(jnp.int32, (c, c), 0)
    col = lax.broadcasted_iota(jnp.int32, (c, c), 1)
    cum = _cumsum_rows(la)
    cl = jnp.sum(la, axis=-2, keepdims=True)
    ep = jnp.exp(cum)
    en = jnp.exp(-cum)
    qs = q * (GLA_DK ** -0.5)
    qp = qs * ep
    a_f = _bdot(qp, k * en, "nt")
    a_b = _bdot(qs * en, k * ep, "nt")
    sc = jnp.where(row >= col, a_f, a_b)
    o = _bdot(sc, v, "nn") + _bdot(qp, st, "nt")
    kd = k * jnp.exp(cl - cum)
    st_new = st * jnp.exp(cl) + _bdot(v, kd, "tn")
    return o, st_new


def _gla_specs(nc, rev):
    nb = nc // CHUNKS_PER_STEP
    rows = CHUNKS_PER_STEP * CHUNK

    def blk(n):
        return (nb - 1 - n) if rev else n
    hm = pl.BlockSpec((HEADS, rows, GLA_DK), lambda n: (0, blk(n), 0))
    tm = pl.BlockSpec((rows, HEADS * HEAD_W), lambda n: (blk(n), 0))
    st = pl.BlockSpec((HEADS, CHUNKS_PER_STEP, HEAD_W, GLA_DK), lambda n: (0, blk(n), 0, 0))
    return nb, hm, tm, st


def _chunk_rows(c):
    return slice(c * CHUNK, (c + 1) * CHUNK)


def _gla_fwd(q, k, v, la, deps=()):
    t = v.shape[0]
    nc = t // CHUNK
    nb, hm, tm, st = _gla_specs(nc, False)

    def body(q_ref, k_ref, v_ref, la_ref, *rest):
        o_ref, sp_ref, st_ref = rest[len(deps):]

        @pl.when(pl.program_id(0) == 0)
        def _():
            st_ref[...] = jnp.zeros_like(st_ref)

        s = st_ref[...]
        for c in range(CHUNKS_PER_STEP):
            r = _chunk_rows(c)
            sp_ref[:, c] = s
            o, s = _gla_chunk(q_ref[:, r], k_ref[:, r], _heads(v_ref, r), la_ref[:, r], s)
            _put_heads(o_ref, o, r)
        st_ref[...] = s

    return pl.pallas_call(
        body, name="gla_fwd", grid=(nb,),
        in_specs=[hm, hm, tm, hm] + [ANY] * len(deps), out_specs=[tm, st],
        out_shape=[jax.ShapeDtypeStruct((t, HEADS * HEAD_W), F32), jax.ShapeDtypeStruct((HEADS, nc, HEAD_W, GLA_DK), F32)],
        scratch_shapes=[pltpu.VMEM((HEADS, HEAD_W, GLA_DK), F32)],
        compiler_params=_cparams("arbitrary"),
    )(q, k, v, la, *deps)


def _gla_bwd(q, k, v, la, sp, do):
    t = v.shape[0]
    nc = t // CHUNK
    nb, hm, tm, st = _gla_specs(nc, True)

    def body(q_ref, k_ref, v_ref, la_ref, sp_ref, do_ref, dq_ref, dk_ref, dv_ref, dla_ref, ds_ref):
        @pl.when(pl.program_id(0) == 0)
        def _():
            ds_ref[...] = jnp.zeros_like(ds_ref)

        ds = ds_ref[...]
        for c in reversed(range(CHUNKS_PER_STEP)):
            r = _chunk_rows(c)
            _, vjp = jax.vjp(_gla_chunk, q_ref[:, r], k_ref[:, r], _heads(v_ref, r), la_ref[:, r], sp_ref[:, c])
            dq, dk, dv, dla, ds = vjp((_heads(do_ref, r), ds))
            dq_ref[:, r] = dq
            dk_ref[:, r] = dk
            _put_heads(dv_ref, dv, r)
            dla_ref[:, r] = dla
        ds_ref[...] = ds

    hm_shape = jax.ShapeDtypeStruct((HEADS, t, GLA_DK), F32)
    return pl.pallas_call(
        body, name="gla_bwd", grid=(nb,),
        in_specs=[hm, hm, tm, hm, st, tm], out_specs=[hm, hm, tm, hm],
        out_shape=[hm_shape, hm_shape, jax.ShapeDtypeStruct((t, HEADS * HEAD_W), F32), hm_shape],
        scratch_shapes=[pltpu.VMEM((HEADS, HEAD_W, GLA_DK), F32)],
        compiler_params=_cparams("arbitrary"),
    )(q, k, v, la, sp, do)


def _ml_chunk(q, k, v, li_r, lf_r, cm, nv, m):
    c = q.shape[-2]
    row = lax.broadcasted_iota(jnp.int32, (c, c), 0)
    col = lax.broadcasted_iota(jnp.int32, (c, c), 1)
    eye = (row == col).astype(F32)
    li_c = jnp.sum(eye * li_r, axis=-1, keepdims=True)
    lf_c = jnp.sum(eye * lf_r, axis=-1, keepdims=True)
    fc_c = jnp.sum((col <= row).astype(F32) * lf_r, axis=-1, keepdims=True)
    fc_r = jnp.sum((row <= col).astype(F32) * lf_c, axis=-2, keepdims=True)
    f_last = jnp.sum(lf_r, axis=-1, keepdims=True)
    kc = k * (HEAD_W ** -0.5)
    a_c = f_last - fc_c + li_c
    m_loc = jnp.max(a_c, axis=-2, keepdims=True)
    kw = kc * jnp.exp(a_c - m_loc)
    c_chunk = _bdot(kw, v, "tn")
    n_chunk = jnp.sum(kw, axis=-2, keepdims=True)
    m_new = jnp.maximum(f_last + m, m_loc)
    sp = jnp.exp(f_last + m - m_new)
    sl = jnp.exp(m_loc - m_new)
    cm_new = sp * cm + sl * c_chunk
    nv_new = sp * nv + sl * n_chunk
    log_d = li_r - _abs(fc_c - fc_r)
    g_inter = fc_c + m
    m_t = jnp.maximum(g_inter, jnp.max(log_d, axis=-1, keepdims=True))
    s = _bdot(q, kc, "nt") * jnp.exp(log_d - m_t)
    sc = jnp.exp(g_inter - m_t)
    num = _bdot(s, v, "nn") + sc * _bdot(q, cm, "nn")
    den = jnp.sum(s, axis=-1, keepdims=True) + sc * jnp.sum(q * nv, axis=-1, keepdims=True)
    den = jnp.maximum(_abs(den), jnp.exp(-m_t))
    return num / den, cm_new, nv_new, m_new


def _ml_specs(nc, rev):
    nb = nc // CHUNKS_PER_STEP

    def blk(n):
        return (nb - 1 - n) if rev else n
    tm = pl.BlockSpec((CHUNKS_PER_STEP * CHUNK, HEADS * HEAD_W), lambda n: (blk(n), 0))
    gate = pl.BlockSpec((HEADS, CHUNKS_PER_STEP, 1, CHUNK), lambda n: (0, blk(n), 0, 0))
    cm = pl.BlockSpec((HEADS, CHUNKS_PER_STEP, HEAD_W, HEAD_W), lambda n: (0, blk(n), 0, 0))
    vec = pl.BlockSpec((HEADS, CHUNKS_PER_STEP, 1, HEAD_W), lambda n: (0, blk(n), 0, 0))
    return nb, tm, gate, cm, vec


_ML_STATE = [pltpu.VMEM((HEADS, HEAD_W, HEAD_W), F32), pltpu.VMEM((HEADS, 1, HEAD_W), F32), pltpu.VMEM((HEADS, 1, HEAD_W), F32)]


def _ml_fwd(q, k, v, li, lf):
    t = q.shape[0]
    nc = t // CHUNK
    nb, tm, gate, cm, vec = _ml_specs(nc, False)

    def body(q_ref, k_ref, v_ref, li_ref, lf_ref, hc_ref, cp_ref, np_ref, mp_ref, c_ref, n_ref, m_ref):
        @pl.when(pl.program_id(0) == 0)
        def _():
            c_ref[...] = jnp.zeros_like(c_ref)
            n_ref[...] = jnp.zeros_like(n_ref)
            m_ref[...] = jnp.zeros_like(m_ref)

        cs, ns, ms = c_ref[...], n_ref[...], m_ref[...][:, :, 0:1]
        for c in range(CHUNKS_PER_STEP):
            r = _chunk_rows(c)
            cp_ref[:, c] = cs
            np_ref[:, c] = ns
            mp_ref[:, c] = jnp.broadcast_to(ms, m_ref.shape)
            hc, cs, ns, ms = _ml_chunk(_heads(q_ref, r), _heads(k_ref, r), _heads(v_ref, r), li_ref[:, c], lf_ref[:, c],
                                       cs, ns, ms)
            _put_heads(hc_ref, hc, r)
        c_ref[...] = cs
        n_ref[...] = ns
        m_ref[...] = jnp.broadcast_to(ms, m_ref.shape)

    return pl.pallas_call(
        body, name="mlstm_fwd", grid=(nb,),
        in_specs=[tm, tm, tm, gate, gate], out_specs=[tm, cm, vec, vec],
        out_shape=[jax.ShapeDtypeStruct((t, HEADS * HEAD_W), F32), jax.ShapeDtypeStruct((HEADS, nc, HEAD_W, HEAD_W), F32),
                   jax.ShapeDtypeStruct((HEADS, nc, 1, HEAD_W), F32), jax.ShapeDtypeStruct((HEADS, nc, 1, HEAD_W), F32)],
        scratch_shapes=_ML_STATE,
        compiler_params=_cparams("arbitrary"),
    )(q, k, v, li, lf)


def _ml_bwd(q, k, v, li, lf, cp, npv, mp, dhc):
    t = q.shape[0]
    nc = t // CHUNK
    nb, tm, gate, cm, vec = _ml_specs(nc, True)

    def body(q_ref, k_ref, v_ref, li_ref, lf_ref, cp_ref, np_ref, mp_ref, dhc_ref,
             dq_ref, dk_ref, dv_ref, dli_ref, dlf_ref, dc_ref, dn_ref, dm_ref):
        @pl.when(pl.program_id(0) == 0)
        def _():
            dc_ref[...] = jnp.zeros_like(dc_ref)
            dn_ref[...] = jnp.zeros_like(dn_ref)
            dm_ref[...] = jnp.zeros_like(dm_ref)

        dc, dn, dm = dc_ref[...], dn_ref[...], dm_ref[...][:, :, 0:1]
        for c in reversed(range(CHUNKS_PER_STEP)):
            r = _chunk_rows(c)
            _, vjp = jax.vjp(_ml_chunk, _heads(q_ref, r), _heads(k_ref, r), _heads(v_ref, r), li_ref[:, c], lf_ref[:, c],
                             cp_ref[:, c], np_ref[:, c], mp_ref[:, c][:, :, 0:1])
            dq, dk, dv, dli, dlf, dc, dn, dm = vjp((_heads(dhc_ref, r), dc, dn, dm))
            _put_heads(dq_ref, dq, r)
            _put_heads(dk_ref, dk, r)
            _put_heads(dv_ref, dv, r)
            dli_ref[:, c] = dli
            dlf_ref[:, c] = dlf
        dc_ref[...] = dc
        dn_ref[...] = dn
        dm_ref[...] = jnp.broadcast_to(dm, dm_ref.shape)

    tm_shape = jax.ShapeDtypeStruct((t, HEADS * HEAD_W), F32)
    gate_shape = jax.ShapeDtypeStruct((HEADS, nc, 1, CHUNK), F32)
    return pl.pallas_call(
        body, name="mlstm_bwd", grid=(nb,),
        in_specs=[tm, tm, tm, gate, gate, cm, vec, vec, tm], out_specs=[tm, tm, tm, gate, gate],
        out_shape=[tm_shape, tm_shape, tm_shape, gate_shape, gate_shape],
        scratch_shapes=_ML_STATE,
        compiler_params=_cparams("arbitrary"),
    )(q, k, v, li, lf, cp, npv, mp, dhc)


def _ml_pre(s0, s1, s2, s3, cw0, cw1, cw2, cw3, cb, wq, wk, wv, wiq, wik, wiv, bif):
    pre = cb + cw0 * s0 + cw1 * s1 + cw2 * s2 + cw3 * s3
    xc = pre * _sigmoid(pre)
    q = _bdot(xc, wq, "nn")
    k = _bdot(xc, wk, "nn")
    v = _bdot(s3, wv, "nn")
    gates = _bdot(q, wiq, "nn") + _bdot(k, wik, "nn") + _bdot(v, wiv, "nn") + bif
    lane = lax.broadcasted_iota(jnp.int32, gates.shape, 1)
    gl = jnp.where(lane < HEADS, gates, _log_sigmoid(gates))
    return xc, q, k, v, gl


def _delayed(xs_ref, x_ref, halo_ref, r):
    xs_ref[0:HALO, :] = halo_ref[...]
    xs_ref[HALO:HALO + r, :] = x_ref[...]
    return [xs_ref[pl.ds(HALO - (CONV_K - 1) + j, r), :] for j in range(CONV_K)]


def _full_spec(shape):
    return pl.BlockSpec(shape, lambda i, nd=len(shape): (0,) * nd)


def _ml_pre_fwd(x_m, x_pad, params, tile=256):
    t, w = x_m.shape
    r = min(tile, t)

    def body(*refs):
        x_ref, halo_ref = refs[:2]
        p = [ref[...] for ref in refs[2:2 + len(params)]]
        outs = refs[2 + len(params):-1]
        res = _ml_pre(*_delayed(refs[-1], x_ref, halo_ref, r), *p)
        for ref, val in zip(outs, res):
            ref[...] = val

    row = pl.BlockSpec((r, w), lambda i: (i, 0))
    return pl.pallas_call(
        body, name="ml_pre_fwd", grid=(t // r,),
        in_specs=[row, pl.BlockSpec((HALO, w), lambda i: (i * (r // HALO), 0))] + [_full_spec(p.shape) for p in params],
        out_specs=[row] * 4 + [pl.BlockSpec((r, LANES), lambda i: (i, 0))],
        out_shape=[jax.ShapeDtypeStruct((t, w), F32)] * 4 + [jax.ShapeDtypeStruct((t, LANES), F32)],
        scratch_shapes=[pltpu.VMEM((r + HALO, w), F32)],
        compiler_params=_cparams("arbitrary"),
    )(x_m, x_pad, *params)


def _ml_pre_bwd(x_m, x_pad, params, cts, tile=256):
    t, w = x_m.shape
    r = min(tile, t)
    nt = t // r
    n_p = len(params)

    def body(*refs):
        x_ref, halo_ref = refs[:2]
        p = [ref[...] for ref in refs[2:2 + n_p]]
        ct = [ref[...] for ref in refs[2 + n_p:7 + n_p]]
        dx_ref = refs[7 + n_p]
        dp_refs = refs[8 + n_p:8 + 2 * n_p]
        xs_ref, ds_ref, carry_ref = refs[8 + 2 * n_p:]
        step = pl.program_id(0)

        @pl.when(step == 0)
        def _():
            ds_ref[...] = jnp.zeros_like(ds_ref)
            carry_ref[...] = jnp.zeros_like(carry_ref)

        _, vjp = jax.vjp(_ml_pre, *_delayed(xs_ref, x_ref, halo_ref, r), *p)
        grads = vjp(tuple(ct))
        for j in range(CONV_K):
            ds_ref[j, HALO:HALO + r, :] = grads[j]
        lead = HALO + CONV_K - 1
        d_tile = sum(ds_ref[j, pl.ds(lead - j, r), :] for j in range(CONV_K))
        d_halo = sum(ds_ref[j, pl.ds(CONV_K - 1 - j, HALO), :] for j in range(CONV_K))
        dx_ref[...] = d_tile
        dx_ref[r - HALO:r, :] += carry_ref[...]
        carry_ref[...] = d_halo
        _accumulate(step, dp_refs, grads[CONV_K:])

    row = pl.BlockSpec((r, w), lambda i: (nt - 1 - i, 0))
    return pl.pallas_call(
        body, name="ml_pre_bwd", grid=(nt,),
        in_specs=[row, pl.BlockSpec((HALO, w), lambda i: ((nt - 1 - i) * (r // HALO), 0))] + [_full_spec(p.shape) for p in params]
        + [row] * 4 + [pl.BlockSpec((r, LANES), lambda i: (nt - 1 - i, 0))],
        out_specs=[row] + [_full_spec(p.shape) for p in params],
        out_shape=[jax.ShapeDtypeStruct((t, w), F32)] + [jax.ShapeDtypeStruct(p.shape, F32) for p in params],
        scratch_shapes=[pltpu.VMEM((r + HALO, w), F32), pltpu.VMEM((CONV_K, r + 2 * HALO, w), F32), pltpu.VMEM((HALO, w), F32)],
        compiler_params=_cparams("arbitrary"),
    )(x_m, x_pad, *params, *cts)


def _per_head(fn, row_vals, head_params, shared_params=()):
    return [fn(*[a[:, hs] for a in row_vals], *[p[:, hs] for p in head_params], *shared_params) for hs in _head_slices(HEAD_W)]


def _gla_out(o, g, gn):
    return _rms(o, gn) * (g * _sigmoid(g))


def _ml_out(hc, op, xc, g, sk):
    hcell = hc * _sigmoid(op)
    mu = jnp.mean(hcell, axis=-1, keepdims=True)
    d = hcell - mu
    var = jnp.mean(d * d, axis=-1, keepdims=True)
    return d * lax.rsqrt(var + EPS) * g + sk * xc


def _log_decay(al, w, b):
    return _log_sigmoid(_bdot(al, w, "nn") + b) * (1.0 / GLA_GATE_NORM)


def _merge(ga, gb, ya, yb):
    return _sigmoid(ga) * ya + _sigmoid(gb) * yb


def _post_mix(x, z, gpm, gpl):
    x1 = x + _rms(z, gpm)
    return x1, _rms(x1, gpl)


def _loss_rows(x1, dn, tgt, g):
    e = x1 + _rms(dn, g) - tgt
    return 0.5 * jnp.sum(jnp.mean(e * e, axis=-1, keepdims=True), axis=0, keepdims=True)


def _lin(p):
    return 4 * p[0] + 2 * p[1] + p[2]


def _me():
    return lax.axis_index("x"), lax.axis_index("y"), lax.axis_index("c")


def _flip(p, k):
    return tuple((1 - v) if (k >> (2 - i)) & 1 else v for i, v in enumerate(p))


ANY = pl.BlockSpec(memory_space=pl.ANY)


HBM = pl.BlockSpec(memory_space=pltpu.HBM)
SEM = pl.BlockSpec(memory_space=pltpu.SEMAPHORE)
DATAFLOW = pltpu.SideEffectType.DATAFLOW_SIDE_EFFECTING


SIBLING = 1
OTHER_CHIPS = (2, 4, 6)


def _peer_copies(kinds, srcs, lands, send_sems, recv_sems):
    me = _me()
    copies = []
    for a, (kind, src, land) in enumerate(zip(kinds, srcs, lands)):
        masks = {"gather": range(1, N_DEV), "exchange": range(1, N_DEV), "gather_chips": (SIBLING, *OTHER_CHIPS),
                 "gather_pass": OTHER_CHIPS}[kind]
        for k in masks:
            peer = _flip(me, k)
            if kind == "gather_pass":
                block = land.at[_lin(peer)]
                src_ref, dst_ref, target = block, block, _flip(me, SIBLING)
            else:
                src_ref, dst_ref, target = (src.at[_lin(peer)] if kind == "exchange" else src), land.at[_lin(me)], peer
            copies.append(pltpu.make_async_remote_copy(
                src_ref=src_ref, dst_ref=dst_ref, send_sem=send_sems.at[a * 7 + k - 1], recv_sem=recv_sems.at[a * 7 + k - 1],
                device_id=target, device_id_type=MESH))
    return copies


def _copies_start(kind, srcs, name, after=None, lands=None):
    n = len(srcs)
    extra = [] if after is None else [after]
    kind = [kind] * n if isinstance(kind, str) else list(kind)
    land_shapes = [(s.shape if k == "exchange" else (N_DEV, *s.shape)) for k, s in zip(kind, srcs)]
    lands = [lax.empty(ls, s.dtype) for ls, s in zip(land_shapes, srcs)] if lands is None else lands

    def body(*refs):
        sems = refs[2 * n + len(extra):]
        for cp in _peer_copies(kind, refs[:n], refs[n:2 * n], sems[0], sems[1]):
            cp.start()
        refs[-1][...] = jnp.zeros_like(refs[-1])

    def hbm(a):
        return pltpu.with_memory_space_constraint(a, pltpu.HBM)

    out = pl.pallas_call(
        body, name=name,
        out_shape=(pltpu.SemaphoreType.DMA((7 * n,)), pltpu.SemaphoreType.DMA((7 * n,)),
                   *[pltpu.HBM(s.shape, s.dtype) for s in srcs],
                   *[pltpu.HBM(ls, s.dtype) for ls, s in zip(land_shapes, srcs)],
                   jax.ShapeDtypeStruct((8, LANES), F32)),
        in_specs=[HBM] * (2 * n) + [ANY] * len(extra),
        out_specs=(SEM, SEM, *[HBM] * (2 * n), pl.BlockSpec(memory_space=pltpu.VMEM)),
        input_output_aliases={i: 2 + i for i in range(2 * n)},
        compiler_params=pltpu.CompilerParams(has_side_effects=DATAFLOW),
    )(*[hbm(s) for s in srcs], *[hbm(a) for a in lands], *extra)
    return (kind, n, out[:-1]), out[-1]


def _copies_wait(state, after, name):
    kind, n, (send_sems, recv_sems, *thru) = state
    after = list(after) if isinstance(after, (list, tuple)) else [after]

    def body(*refs):
        for cp in _peer_copies(kind, refs[:n], refs[n:2 * n], refs[2 * n], refs[2 * n + 1]):
            cp.wait_send()
            cp.wait_recv()

    out = pl.pallas_call(
        body, name=name,
        out_shape=tuple(pltpu.HBM(t.shape, t.dtype) for t in thru),
        in_specs=[HBM] * (2 * n) + [SEM, SEM] + [ANY] * len(after), out_specs=tuple([HBM] * (2 * n)),
        input_output_aliases={i: i for i in range(2 * n)},
        compiler_params=pltpu.CompilerParams(has_side_effects=DATAFLOW),
    )(*thru, send_sems, recv_sems, *after)
    return out[:n], out[n:]


def _adamw(w, g, m, v):
    m2 = ADAM_B1 * m + (1.0 - ADAM_B1) * g
    v2 = ADAM_B2 * v + (1.0 - ADAM_B2) * (g * g)
    m_hat = m2 / (1.0 - ADAM_B1 ** ADAM_STEP)
    v_hat = v2 / (1.0 - ADAM_B2 ** ADAM_STEP)
    delta = -ADAM_LR * (m_hat / (jnp.sqrt(v_hat) + ADAM_EPS) + ADAM_WD * w)
    return delta, m2, v2


def _sum_adamw(land, part, me_idx, w, m, v, name, tile=256):
    r, c = w.shape
    tr = min(tile, r)

    def body(me_ref, own_ref, *refs):
        slots = refs[:N_DEV]
        w_ref, m_ref, v_ref, g_ref, d_ref, m2_ref, v2_ref = refs[N_DEV:]
        own = own_ref[...].astype(F32)
        g = None
        for s in range(N_DEV):
            term = jnp.where(me_ref[0] == s, own, slots[s][...].astype(F32))
            g = term if g is None else g + term
        d, m2, v2 = _adamw(w_ref[...], g, m_ref[...], v_ref[...])
        g_ref[...] = g
        d_ref[...] = d
        m2_ref[...] = m2
        v2_ref[...] = v2

    def slot_spec(s):
        return pl.BlockSpec((None, tr, c), lambda i, me: (jnp.where(me[0] == s, (s + 1) % N_DEV, s), i, 0))

    row = pl.BlockSpec((tr, c), lambda i, me: (i, 0))
    return pl.pallas_call(
        body, name=name,
        grid_spec=pltpu.PrefetchScalarGridSpec(
            num_scalar_prefetch=1, grid=(r // tr,),
            in_specs=[pl.BlockSpec((None, tr, c), lambda i, me: (me[0], i, 0))] + [slot_spec(s) for s in range(N_DEV)] + [row] * 3,
            out_specs=[row] * 4),
        out_shape=[jax.ShapeDtypeStruct((r, c), F32)] * 4,
        compiler_params=_cparams("parallel"),
    )(me_idx, part, *[land] * N_DEV, w, m, v)


def _small_update(name, me_idx, kinds, lands, owns, ws, ms, vs, sums=()):
    n = len(ws)
    lands, owns = list(lands) + [s[0] for s in sums], list(owns) + [s[1] for s in sums]
    kinds = list(kinds) + ["gather"] * len(sums)
    nl = len(lands)

    def summed(me, land_ref, own):
        g = None
        for s in range(N_DEV):
            term = jnp.where(me == s, own, land_ref[s])
            g = term if g is None else g + term
        return g

    def body(me_ref, *refs):
        land_refs, own_refs = refs[:nl], refs[nl:2 * nl]
        w_refs, m_refs, v_refs = (refs[2 * nl + i * n:2 * nl + (i + 1) * n] for i in range(3))
        outs = refs[2 * nl + 3 * n:]
        me = me_ref[0]
        for i in range(n):
            g = summed(me, land_refs[i], own_refs[i][...])
            d, m2, v2 = _adamw(w_refs[i][...], g, m_refs[i][...], v_refs[i][...])
            for ref, val in zip(outs[4 * i:4 * i + 4], (g, d, m2, v2)):
                ref[...] = val
        for i in range(n, nl):
            outs[4 * n + i - n][...] = summed(me, land_refs[i], own_refs[i][...])

    def whole(shape):
        return pl.BlockSpec(shape, lambda i, me, nd=len(shape): (0,) * nd)

    def own_spec(kind, own):
        if kind == "gather":
            return whole(own.shape)
        return pl.BlockSpec((None, *own.shape[1:]), lambda i, me: (me[0], 0, 0))

    shapes = [w.shape for w in ws]
    out_shapes = [s for s in shapes for _ in range(4)] + [s[1].shape for s in sums]
    return pl.pallas_call(
        body, name=name,
        grid_spec=pltpu.PrefetchScalarGridSpec(
            num_scalar_prefetch=1, grid=(1,),
            in_specs=[whole(a.shape) for a in lands] + [own_spec(k, o) for k, o in zip(kinds, owns)]
            + [whole(s) for s in shapes] * 3,
            out_specs=[whole(s) for s in out_shapes]),
        out_shape=[jax.ShapeDtypeStruct(s, F32) for s in out_shapes],
        compiler_params=_cparams("arbitrary"),
    )(me_idx, *lands, *owns, *ws, *ms, *vs)


def _small_view(n, a):
    if a.ndim == 1:
        return a.reshape(1, -1)
    if a.ndim == 3:
        return a.transpose(1, 2, 0).reshape(QKV_BLOCK * QKV_BLOCK, -1)
    return a.T if n == "w_if" else a


def _small_unview(n, a, shape):
    if len(shape) == 1:
        return a.reshape(shape)
    if len(shape) == 3:
        return a.reshape(QKV_BLOCK, QKV_BLOCK, -1).transpose(2, 0, 1)
    return a.T if n == "w_if" else a


def _small_shards(n, g):
    if n == "w_if":
        return g.reshape(N_DEV, -1, g.shape[1]).transpose(0, 2, 1)
    return g.reshape(g.shape[0], N_DEV, -1).transpose(1, 0, 2)


def _small_unshard(n, s):
    if n == "w_if":
        return s.transpose(0, 2, 1).reshape(-1, s.shape[1])
    return s.transpose(1, 0, 2).reshape(s.shape[1], -1)


def _to_hm(a, d):
    t = a.shape[0]
    return a.reshape(t, HEADS, d).transpose(1, 0, 2)


def _from_hm(a):
    h, t, d = a.shape
    return a.transpose(1, 0, 2).reshape(t, h * d)


def _gate_rows(g):
    t = g.shape[0]
    return g.T.reshape(HEADS, t // CHUNK, 1, CHUNK)


def _gate_cols(g):
    h, nc, _, c = g.shape
    return g.reshape(h, nc * c).T


def _blockdiag_dense(w):
    n = w.shape[0] * QKV_BLOCK
    tiled = jnp.tile(w.reshape(n, QKV_BLOCK), (1, n // QKV_BLOCK))
    r = lax.broadcasted_iota(jnp.int32, (n, n), 0)
    c = lax.broadcasted_iota(jnp.int32, (n, n), 1)
    return jnp.where(r // QKV_BLOCK == c // QKV_BLOCK, tiled, 0.0)


def _blockdiag_blocks(dense):
    n = dense[0].shape[0]
    k = len(dense)

    def body(*refs):
        r = lax.broadcasted_iota(jnp.int32, (n, n), 0)
        c = lax.broadcasted_iota(jnp.int32, (n, n), 1)
        fr = lax.broadcasted_iota(jnp.int32, (n, LANES), 0)
        fc = lax.broadcasted_iota(jnp.int32, (n, LANES), 1)
        fold = ((fr & (QKV_BLOCK - 1)) == fc).astype(BF16)
        for i in range(k):
            kept = jnp.where((r >> 2) == (c >> 2), refs[i][...], 0.0)
            refs[k + i][...] = sum(lax.dot_general(t, fold, _dims("nn", 2), preferred_element_type=F32) for t in _split3(kept))

    out = pl.pallas_call(body, name="blockdiag_blocks", out_shape=[jax.ShapeDtypeStruct((n, LANES), F32)] * k)(*dense)
    return [o[:, 0:QKV_BLOCK].reshape(n // QKV_BLOCK, QKV_BLOCK, QKV_BLOCK) for o in out]


def _col_blocks(w):
    k, n = w.shape
    return w.reshape(k, N_DEV, n // N_DEV).transpose(1, 0, 2)


def _from_col_blocks(g):
    d, k, n = g.shape
    return g.transpose(1, 0, 2).reshape(k, d * n)


def _local_step(x, tgt, weight, ws, prefetch, pass_on, on_grads, on_small):
    t, d = x.shape
    g1 = ws["g_pre_mix"]

    def dep(token):
        return () if token is None else (token,)

    w_in = weight("w_in", x)
    fetch_mix = prefetch(("w_pa", "w_pb", "w_o"), w_in)

    n_in = w_in.shape[2]

    def proj_in_fwd(xv, g, w):
        hv = _rms(xv, g)
        return (hv, jnp.concatenate([_raw_dot(hv, w[j], "nn") for j in range(N_DEV)], axis=1)), ()

    h, proj = _rowwise("proj_in", proj_in_fwd, [x], [g1, w_in], [(d, BF16), (N_DEV * n_in, F32)], deps=dep(fetch_mix))
    offs = [0]
    for s in IN_SPLITS:
        offs.append(offs[-1] + s)
    q_a, k_a, v_a, g_a, a_low, x_m, o_pre, gate_a, gate_b = [proj[:, offs[i]:offs[i + 1]] for i in range(9)]

    a_low_p = jnp.pad(a_low, ((0, 0), (0, LANES - LOWRANK)))
    w_a_up_p = jnp.pad(ws["w_a_up"], ((0, LANES - LOWRANK), (0, 0)))
    b_a_up = ws["b_a_up"]
    (la,) = _rowwise("gla_decay", lambda al, w, b: ((_log_decay(al, w, b),), ()), [a_low_p], [w_a_up_p, b_a_up],
                     [(HEADS * GLA_DK, F32)])
    fetch_up = prefetch(("w_up",), la)
    pass_mix = pass_on("w_pa", la)
    q_hm, k_hm, la_hm = _to_hm(q_a, GLA_DK), _to_hm(k_a, GLA_DK), _to_hm(la, GLA_DK)
    o_gla, s_prev = _gla_fwd(q_hm, k_hm, v_a, la_hm, deps=dep(fetch_up) + dep(pass_mix))
    gn = ws["g_gla_norm"]
    ml_w = HEADS * HEAD_W

    def proj_a_fwd(o, g, n_, w):
        ya = jnp.concatenate(_per_head(_gla_out, [o, g], [], [n_]), axis=1)
        return (ya, _raw_dot(ya, w, "nn")), ()

    ya_in, y_a = _rowwise("proj_a", proj_a_fwd, [o_gla, g_a], [gn, weight("w_pa", o_gla)], [(ml_w, BF16), (d, F32)],
                          tile=512)

    cw = ws["conv_w"]
    w_if_p = jnp.pad(ws["w_if"], ((0, 0), (0, LANES - 2 * HEADS)))
    pre_params = [cw[0:1], cw[1:2], cw[2:3], cw[3:4], ws["conv_b"],
                  _blockdiag_dense(ws["w_q_ml"]), _blockdiag_dense(ws["w_k_ml"]), _blockdiag_dense(ws["w_v_ml"]),
                  w_if_p[0:ml_w], w_if_p[ml_w:2 * ml_w], w_if_p[2 * ml_w:3 * ml_w],
                  jnp.pad(ws["b_if"], ((0, 0), (0, LANES - 2 * HEADS)))]
    x_pad = jnp.pad(x_m, ((HALO, 0), (0, 0)))
    xc, q_m, k_m, v_m, gl = _ml_pre_fwd(x_m, x_pad, pre_params)
    li, lf = _gate_rows(gl[:, 0:HEADS]), _gate_rows(gl[:, HEADS:2 * HEADS])
    hc, c_prev, n_prev, m_prev = _ml_fwd(q_m, k_m, v_m, li, lf)
    fetch_down = prefetch(("w_down",), hc)
    g_ml, skip = ws["g_ml_norm"], ws["ml_skip"]

    def proj_b_fwd(a, b, c_, ga, gb, ya, g, s, w):
        hb = jnp.concatenate(_per_head(_ml_out, [a, b, c_], [g, s]), axis=1)
        yb = _raw_dot(hb, w, "nn")
        return (hb, yb, _merge(ga, gb, ya, yb)), ()

    h_b, y_b, merged = _rowwise("proj_b", proj_b_fwd, [hc, o_pre, xc, gate_a, gate_b, y_a], [g_ml, skip, weight("w_pb", hc)],
                                [(ml_w, BF16), (d, F32), (d, BF16)], tile=512, deps=dep(fetch_down))

    gpm, gpl, gpo = ws["g_post_mix"], ws["g_pre_mlp"], ws["g_post_mlp"]

    def proj_o_fwd(mg, xv, w, a, b):
        zv = _raw_dot(mg, w, "nn")
        return (zv, *_post_mix(xv, zv, a, b)), ()

    pass_up = pass_on("w_up", merged)
    z, x1, h2 = _rowwise("proj_o", proj_o_fwd, [merged, x], [weight("w_o", merged), gpm, gpl],
                         [(d, F32), (d, F32), (d, BF16)], tile=512, deps=dep(pass_up))
    pass_down = pass_on("w_down", h2)
    w_up = weight("w_up", h2)
    up, u = _mm(h2, w_up, "nn", (BF16, BF16), "mlp_up", tm=2048, shards="b", deps=dep(pass_down),
                epilogue=lambda p: (p, jnp.square(jnp.maximum(p, 0.0))))

    def mlp_down_loss(uv, x1v, tgtv, w, g):
        dnv = _raw_dot(uv, w, "nn")
        loss, vjp = jax.vjp(lambda a, b, c_: _loss_rows(a, b, tgtv, c_), x1v, dnv, g)
        dx1, ddn, dg = vjp(jnp.ones((1, 1), F32))
        return (dx1, ddn), (jnp.broadcast_to(loss, (1, LANES)), dg)

    dx1_y, d_dn, loss, d_gpo = _rowwise("mlp_down", mlp_down_loss, [u, x1, tgt], [weight("w_down", u), gpo],
                                        [(d, F32), (d, BF16)], [((1, LANES), F32), ((1, d), F32)], tile=512)

    (d_up,) = _mm(d_dn, weight("w_down", u), "nt", (BF16,), "mlp_down_dx", extra=[up],
                  epilogue=lambda p, a: (p * (2.0 * jnp.maximum(a.astype(F32), 0.0)),))
    dw_down = _mm(u, d_dn, "tn", BF16, "mlp_down_dw", tm=512)
    dw_up = _mm(h2, d_up, "tn", BF16, "mlp_up_dw", tn=w_up.shape[2], shards="out")
    sent_mlp = on_grads(dict(w_down=dw_down, w_up=dw_up))

    def mlp_up_dx(dup, xv, zv, dx1, w, a, b):
        _, vjp = jax.vjp(_post_mix, xv, zv, a, b)
        ns = w.shape[2]
        dh2 = sum(_raw_dot(dup[:, j * ns:(j + 1) * ns], w[j], "nt") for j in range(w.shape[0]))
        dx, dz, da, db = vjp((dx1, dh2))
        return (dx, dz), (da, db)

    dx_res, d_z, d_gpm, d_gpl = _rowwise("mlp_up_dx", mlp_up_dx, [d_up, x, z, dx1_y], [w_up, gpm, gpl],
                                         [(d, F32), (d, BF16)], [((1, d), F32), ((1, d), F32)], tile=512, deps=dep(sent_mlp))
    dw_o = _mm(merged, d_z, "tn", BF16, "proj_o_dw")

    def proj_o_dx(dz, ga, gb, ya, yb, w):
        return jax.vjp(_merge, ga, gb, ya, yb)[1](_raw_dot(dz, w, "nt")), ()

    d_ga, d_gb, d_ya, d_yb = _rowwise("proj_o_dx", proj_o_dx, [d_z, gate_a, gate_b, y_a, y_b], [weight("w_o", merged)],
                                      [(d, F32), (d, F32), (d, BF16), (d, BF16)], tile=512)
    dw_pa = _mm(ya_in, d_ya, "tn", BF16, "proj_a_dw")
    dw_pb = _mm(h_b, d_yb, "tn", BF16, "proj_b_dw")
    sent_mix = on_grads(dict(w_o=dw_o, w_pa=dw_pa, w_pb=dw_pb))

    def proj_b_dx(dyb, a, b, c_, w, g, s):
        ct = _raw_dot(dyb, w, "nt")
        parts = []
        for hs in _head_slices(HEAD_W):
            _, vjp = jax.vjp(_ml_out, a[:, hs], b[:, hs], c_[:, hs], g[:, hs], s[:, hs])
            parts.append(vjp(ct[:, hs]))
        cat = lambda i: jnp.concatenate([p[i] for p in parts], axis=1)
        return (cat(0), cat(1), cat(2)), (cat(3), cat(4))

    d_hc, d_opre, d_xc, d_gml, d_skip = _rowwise("proj_b_dx", proj_b_dx, [d_yb, hc, o_pre, xc],
                                                 [weight("w_pb", hc), g_ml, skip], [(ml_w, F32)] * 3, [((1, ml_w), F32)] * 2,
                                                 tile=512, deps=dep(sent_mix))
    d_qm, d_km, d_vm, d_li, d_lf = _ml_bwd(q_m, k_m, v_m, li, lf, c_prev, n_prev, m_prev, d_hc)
    d_gl = jnp.concatenate([_gate_cols(d_li), _gate_cols(d_lf), jnp.zeros((t, LANES - 2 * HEADS), F32)], axis=1)
    pre_grads = _ml_pre_bwd(x_m, x_pad, pre_params, [d_xc, d_qm, d_km, d_vm, d_gl])
    d_xm = pre_grads[0]
    d_cw = jnp.concatenate(pre_grads[1:5], axis=0)
    d_cb = pre_grads[5]
    d_wq, d_wk, d_wv = _blockdiag_blocks(pre_grads[6:9])
    d_wif = jnp.concatenate(pre_grads[9:12], axis=0)[:, 0:2 * HEADS]
    d_bif = pre_grads[12][:, 0:2 * HEADS]

    def proj_a_dx(dya, o, g, w, n_):
        ct = _raw_dot(dya, w, "nt")
        parts = []
        for hs in _head_slices(HEAD_W):
            _, vjp = jax.vjp(_gla_out, o[:, hs], g[:, hs], n_)
            parts.append(vjp(ct[:, hs]))
        cat = lambda i: jnp.concatenate([p[i] for p in parts], axis=1)
        return (cat(0), cat(1)), (sum(p[2] for p in parts),)

    d_o, d_g_a, d_gn = _rowwise("proj_a_dx", proj_a_dx, [d_ya, o_gla, g_a], [weight("w_pa", o_gla), gn], [(ml_w, F32)] * 2,
                                [((1, HEAD_W), F32)], tile=512)
    dq_hm, dk_hm, d_va, dla_hm = _gla_bwd(q_hm, k_hm, v_a, la_hm, s_prev, d_o)

    def decay_bwd(al, ct, w, b):
        _, vjp = jax.vjp(_log_decay, al, w, b)
        dal, dw, db = vjp(ct)
        return (dal,), (dw, db)

    d_alow_p, d_wa_p, d_ba = _rowwise("gla_decay_bwd", decay_bwd, [a_low_p, _from_hm(dla_hm)], [w_a_up_p, b_a_up],
                                      [(LANES, F32)], [(w_a_up_p.shape, F32), (b_a_up.shape, F32)])
    d_proj = jnp.concatenate([_from_hm(dq_hm), _from_hm(dk_hm), d_va, d_g_a, d_alow_p[:, 0:LOWRANK], d_xm, d_opre, d_ga, d_gb],
                             axis=1).astype(BF16)
    small = dict(w_a_up=d_wa_p[0:LOWRANK], b_a_up=d_ba, g_gla_norm=d_gn, conv_w=d_cw, conv_b=d_cb,
                 w_q_ml=d_wq, w_k_ml=d_wk, w_v_ml=d_wv, w_if=d_wif, b_if=d_bif, ml_skip=d_skip, g_ml_norm=d_gml,
                 g_post_mix=d_gpm, g_pre_mlp=d_gpl, g_post_mlp=d_gpo)
    sent_small = on_small(small, loss)
    dw_in = _mm_shard_cols(h, d_proj, n_in, "proj_in_dw", deps=dep(sent_small))
    sent_in = on_grads(dict(w_in=dw_in))

    def proj_in_dx(dp, xv, dres, w, g):
        _, vjp = jax.vjp(_rms, xv, g)
        dx, dg = vjp(sum(_raw_dot(dp[:, j * n_in:(j + 1) * n_in], w[j], "nt") for j in range(N_DEV)))
        return (dx + dres,), (dg,)

    grad_x, d_g1 = _rowwise("proj_in_dx", proj_in_dx, [d_proj, x, dx_res], [w_in, g1], [(d, F32)], [((1, d), F32)],
                            deps=dep(sent_in))
    return grad_x, on_small(dict(g_pre_mix=d_g1), None)


BIG = ("w_in", "w_pa", "w_pb", "w_o", "w_up", "w_down")
BIG_COL_SHARDED = ("w_in", "w_pa", "w_pb", "w_up")
SMALL_SHARDED = ("w_a_up", "conv_w", "w_if")
SMALL = ("g_pre_mix", "w_a_up", "b_a_up", "g_gla_norm", "conv_w", "conv_b", "w_q_ml", "w_k_ml", "w_v_ml", "w_if", "b_if",
         "ml_skip", "g_ml_norm", "g_post_mix", "g_pre_mlp", "g_post_mlp")
WEIGHTS = ("g_pre_mix", "w_in", "w_a_up", "b_a_up", "g_gla_norm", "conv_w", "conv_b", "w_q_ml", "w_k_ml", "w_v_ml", "w_if", "b_if",
           "ml_skip", "g_ml_norm", "w_pa", "w_pb", "w_o", "g_post_mix", "g_pre_mlp", "w_up", "w_down", "g_post_mlp")


def kernel(x, g_pre_mix, w_in, w_a_up, b_a_up, g_gla_norm, conv_w, conv_b, w_q_ml, w_k_ml, w_v_ml, w_if, b_if, ml_skip, g_ml_norm, w_pa, w_pb, w_o, g_post_mix, g_pre_mlp, w_up, w_down, g_post_mlp, loss_target, m_g_pre_mix, m_w_in, m_w_a_up, m_b_a_up, m_g_gla_norm, m_conv_w, m_conv_b, m_w_q_ml, m_w_k_ml, m_w_v_ml, m_w_if, m_b_if, m_ml_skip, m_g_ml_norm, m_w_pa, m_w_pb, m_w_o, m_g_post_mix, m_g_pre_mlp, m_w_up, m_w_down, m_g_post_mlp, v_g_pre_mix, v_w_in, v_w_a_up, v_b_a_up, v_g_gla_norm, v_conv_w, v_conv_b, v_w_q_ml, v_w_k_ml, v_w_v_ml, v_w_if, v_b_if, v_ml_skip, v_g_ml_norm, v_w_pa, v_w_pb, v_w_o, v_g_post_mix, v_g_pre_mlp, v_w_up, v_w_down, v_g_post_mlp):
    args = dict(locals())
    w = {n: args[n][0] for n in WEIGHTS}
    m = {n: args["m_" + n][0] for n in WEIGHTS}
    v = {n: args["v_" + n][0] for n in WEIGHTS}

    me_lin = _lin(_me())
    me_idx = jnp.reshape(me_lin, (1,)).astype(jnp.int32)

    def full_weight(n, g):
        if n in ("w_in", "w_up"):
            return g
        return _from_col_blocks(g) if n in BIG_COL_SHARDED else g.reshape(-1, g.shape[-1])

    def grad_parts(n, g):
        if n in ("w_in", "w_up"):
            return g
        return (_col_blocks(g) if n in BIG_COL_SHARDED else g.reshape(N_DEV, -1, g.shape[-1])).astype(BF16)

    sharded_names = tuple(SMALL_SHARDED)
    small_w_state, small_w_token = _copies_start("gather", [_small_view(n, w[n]) for n in sharded_names],
                                                 "allgather_start_small_weights")
    narrow = {n: w[n].astype(BF16) for n in BIG}
    ready, pending = {}, {}

    def prefetch(group, after):
        state, token = _copies_start("gather_chips", [narrow[n] for n in group], "allgather_start_" + group[0], after)
        for n in group:
            pending[n] = (group, state)
        return token

    prefetch(("w_in",), small_w_token)

    passing = {}

    def pass_on(n, after):
        group, state = pending[n]
        shards, lands = _copies_wait(state, after, "allgather_wait_" + group[0])
        state, token = _copies_start("gather_pass", shards, "allgather_pass_" + group[0], lands=lands)
        for gn in group:
            passing[gn] = (group, state)
        return token

    def weight(n, after):
        if n not in ready:
            group, state = passing[n]
            shards, lands = _copies_wait(state, after, "allgather_passed_" + group[0])
            for gn, shard, land in zip(group, shards, lands):
                ready[gn] = full_weight(gn, lax.dynamic_update_slice(land, shard[None], (me_lin, 0, 0)))
        return ready[n]

    small_w_own, small_w_lands = _copies_wait(small_w_state, [narrow[n] for n in BIG if n != "w_in"], "allgather_wait_small_weights")
    ws = {n: (w[n].reshape(1, -1) if w[n].ndim == 1 else w[n]) for n in SMALL if n not in SMALL_SHARDED}
    for n, own, land in zip(sharded_names, small_w_own, small_w_lands):
        ws[n] = _small_unshard(n, lax.dynamic_update_slice(land, own[None], (me_lin, 0, 0)))
    pass_on("w_in", [ws[n] for n in sharded_names])

    sent = []

    def on_grads(grads):
        names = tuple(grads)
        state, token = _copies_start("exchange", [grad_parts(n, grads[n]) for n in names], "exchange_start_" + names[0])
        sent.append((names, state))
        return token

    small_sent = []

    def on_small(small, loss):
        names = tuple(small)
        kinds = ["exchange" if n in SMALL_SHARDED else "gather" for n in names]
        srcs = [_small_shards(n, small[n]) if n in SMALL_SHARDED else _small_view(n, small[n]) for n in names]
        extra = [] if loss is None else [loss]
        state, token = _copies_start(kinds + ["gather"] * len(extra), srcs + extra, "allgather_start_small_" + names[0])
        small_sent.append((names, kinds, state))
        return token

    grad_x, last_token = _local_step(x[0], loss_target[0], weight, ws, prefetch, pass_on, on_grads, on_small)

    out = {}

    def finish(names, state, after):
        parts, lands = _copies_wait(state, after, "exchange_wait_" + names[0])
        for n, part, land in zip(names, parts, lands):
            out[n] = _sum_adamw(land, part, me_idx, w[n], m[n], v[n], "adamw_" + n)

    def finish_small(names, kinds, state, after):
        own, lands = _copies_wait(state, after, "allgather_wait_small_" + names[0])
        k = len(names)
        upd = _small_update("adamw_small_" + names[0], me_idx, kinds, lands[:k], own[:k],
                            *[[_small_view(n, d[n]) for n in names] for d in (w, m, v)], sums=list(zip(lands[k:], own[k:])))
        for i, n in enumerate(names):
            out[n] = tuple(_small_unview(n, a, w[n].shape) for a in upd[4 * i:4 * i + 4])
        return upd[4 * k:]

    (loss_sum,) = finish_small(*small_sent[0], [grad_x, last_token])
    for names, state in sent[:-1]:
        finish(names, state, [grad_x, last_token])
    finish(*sent[-1], [loss_sum] + [out[n][1] for n in BIG if n in out])
    finish_small(*small_sent[1], [out["w_in"][1]])

    shaped = lambda a, n: a.reshape(args[n].shape)
    return (loss_sum[0, 0], grad_x[None],
            *[shaped(out[n][0], n) for n in WEIGHTS], *[shaped(out[n][1], n) for n in WEIGHTS],
            *[shaped(out[n][2], n) for n in WEIGHTS], *[shaped(out[n][3], n) for n in WEIGHTS])
```

```python
import functools

import jax
import jax.numpy as jnp
from jax import lax
from jax.experimental import pallas as pl
from jax.experimental.pallas import tpu as pltpu

F32 = jnp.float32
BF16 = jnp.bfloat16
MESH = pl.DeviceIdType.MESH

N_DEV = 8
EPS = 1e-6
CHUNK = 64
CHUNKS_PER_STEP = 4
HEADS = 4
GLA_DK = 64
HEAD_W = 128
GLA_GATE_NORM = 16.0
LOWRANK = 16
CONV_K = 4
QKV_BLOCK = 4
LANES = 128
HALO = 8
IN_SPLITS = (256, 256, 512, 512, 16, 512, 512, 1024, 1024)

ADAM_LR = 0.001
ADAM_B1 = 0.9
ADAM_B2 = 0.999
ADAM_EPS = 1e-08
ADAM_WD = 0.01
ADAM_STEP = 10

VMEM_LIMIT = 56 * 1024 * 1024


def _cparams(*sem):
    return pltpu.CompilerParams(dimension_semantics=sem, vmem_limit_bytes=VMEM_LIMIT)


def _dims(mode, ndim):
    contract = {"nn": ((ndim - 1,), (ndim - 2,)), "nt": ((ndim - 1,), (ndim - 1,)), "tn": ((ndim - 2,), (ndim - 2,))}[mode]
    return contract, (((0,), (0,)) if ndim == 3 else ((), ()))


def _raw_dot(a, b, mode):
    return lax.dot_general(a.astype(BF16), b.astype(BF16), _dims(mode, a.ndim), preferred_element_type=F32)


@functools.partial(jax.custom_vjp, nondiff_argnums=(2,))
def _bdot(a, b, mode):
    return _raw_dot(a, b, mode)


def _bdot_fwd(a, b, mode):
    return _raw_dot(a, b, mode), (a, b)


def _bdot_bwd(mode, res, ct):
    a, b = res
    if mode == "nn":
        da, db = _raw_dot(ct, b, "nt"), _raw_dot(a, ct, "tn")
    elif mode == "nt":
        da, db = _raw_dot(ct, b, "nn"), _raw_dot(ct, a, "tn")
    else:
        da, db = _raw_dot(b, ct, "nt"), _raw_dot(a, ct, "nn")
    return da.astype(a.dtype), db.astype(b.dtype)


_bdot.defvjp(_bdot_fwd, _bdot_bwd)


def _split3(x):
    hi = x.astype(BF16)
    r1 = x - hi.astype(F32)
    mid = r1.astype(BF16)
    return hi, mid, (r1 - mid.astype(F32)).astype(BF16)


def _split_dot(tri, x):
    if x.ndim == 3:
        tri = jnp.broadcast_to(tri, (x.shape[0], *tri.shape))
    return sum(lax.dot_general(tri, t, _dims("nn", x.ndim), preferred_element_type=F32) for t in _split3(x))


def _tri(n, lower):
    r = lax.broadcasted_iota(jnp.int32, (n, n), 0)
    c = lax.broadcasted_iota(jnp.int32, (n, n), 1)
    return ((c <= r) if lower else (c >= r)).astype(BF16)


@jax.custom_vjp
def _cumsum_rows(x):
    return _split_dot(_tri(x.shape[-2], True), x)


def _cumsum_rows_fwd(x):
    return _cumsum_rows(x), None


def _cumsum_rows_bwd(_, ct):
    return (_split_dot(_tri(ct.shape[-2], False), ct),)


_cumsum_rows.defvjp(_cumsum_rows_fwd, _cumsum_rows_bwd)


def _abs(x):
    return jnp.where(x >= 0, x, -x)


def _sigmoid(x):
    return lax.logistic(x)


def _log_sigmoid(x):
    return jnp.minimum(x, 0.0) - jnp.log(1.0 + jnp.exp(-_abs(x)))


def _rms(x, g):
    return x * lax.rsqrt(jnp.mean(x * x, axis=-1, keepdims=True) + EPS) * g


def _head_slices(w):
    return [slice(h * w, (h + 1) * w) for h in range(HEADS)]


def _heads(ref, rows=slice(None)):
    return jnp.stack([ref[rows, hs] for hs in _head_slices(HEAD_W)])


def _put_heads(ref, val, rows=slice(None)):
    for h, hs in enumerate(_head_slices(HEAD_W)):
        ref[rows, hs] = val[h]


def _tile(dim, want):
    if dim <= want or dim % LANES:
        return dim
    t = want
    while dim % t:
        t -= LANES
    return t


def _mm(a, b, mode, out_dtype, name, tm=1024, tn=1024, tk=4096, epilogue=None, extra=(), deps=(), shards=None):
    if shards == "b":
        assert mode == "nn"
        ns = b.shape[2]
        (m, k), (k2, n) = a.shape, (b.shape[1], b.shape[0] * ns)
        tn = ns
    elif mode == "nn":
        (m, k), (k2, n) = a.shape, b.shape
    elif mode == "nt":
        (m, k), (n, k2) = a.shape, b.shape
    else:
        (k, m), (k2, n) = a.shape, b.shape
    assert k == k2, (name, a.shape, b.shape)
    tm, tn, tk = _tile(m, tm), _tile(n, tn), _tile(k, tk)
    nk = k // tk
    out_dtypes = out_dtype if epilogue else (out_dtype,)
    assert nk == 1 or (out_dtype == F32 and not epilogue), name
    n_in = 2 + len(extra)

    def body(*refs):
        p = _raw_dot(refs[0][...], refs[1][...], mode)
        if nk > 1:
            _accumulate(pl.program_id(2), [refs[n_in + len(deps)]], [p])
            return
        outs = epilogue(p, *[r[...] for r in refs[2:n_in]]) if epilogue else (p,)
        for ref, val in zip(refs[n_in + len(deps):], outs):
            ref[...] = val.astype(ref.dtype)

    a_spec = pl.BlockSpec((tk, tm), lambda i, j, kk: (kk, i)) if mode == "tn" else pl.BlockSpec((tm, tk), lambda i, j, kk: (i, kk))
    if shards == "b":
        b_spec = pl.BlockSpec((None, tk, tn), lambda i, j, kk: (j, kk, 0))
    elif mode == "nt":
        b_spec = pl.BlockSpec((tn, tk), lambda i, j, kk: (j, kk))
    else:
        b_spec = pl.BlockSpec((tk, tn), lambda i, j, kk: (kk, j))
    o_spec = pl.BlockSpec((tm, tn), lambda i, j, kk: (i, j))
    if shards == "out":
        out_specs, out_shape = [pl.BlockSpec((None, tm, tn), lambda i, j, kk: (j, i, 0))], (n // tn, m, tn)
    else:
        out_specs, out_shape = [o_spec] * len(out_dtypes), (m, n)
    res = pl.pallas_call(
        body, name=name, grid=(m // tm, n // tn, nk),
        in_specs=[a_spec, b_spec] + [o_spec] * len(extra) + [ANY] * len(deps), out_specs=out_specs,
        out_shape=[jax.ShapeDtypeStruct(out_shape, dt) for dt in out_dtypes],
        compiler_params=_cparams("parallel", "parallel", "arbitrary"),
    )(a, b, *extra, *deps)
    return res if epilogue else res[0]


def _mm_shard_cols(a, b, n, name, row_tile, tm, deps=()):
    t = a.shape[0]
    nb = b.shape[1] // n

    def body(a_ref, b_ref, *rest):
        o_ref, at_ref = rest[len(deps):]
        j = pl.program_id(0)

        @pl.when(j == 0)
        def _():
            at_ref[...] = a_ref[...].astype(BF16).T

        for s in range(nb):
            @pl.when(j == s)
            def _(s=s):
                o_ref[...] = _raw_dot(at_ref[...], b_ref[:, s * n:(s + 1) * n], "nn").astype(BF16)

    return pl.pallas_call(
        body, name=name, grid=(nb,),
        in_specs=[pl.BlockSpec((t, tm), lambda j: (0, row_tile)),
                  pl.BlockSpec((t, nb * n), lambda j: (0, 0), pipeline_mode=pl.Buffered(1))] + [ANY] * len(deps),
        out_specs=pl.BlockSpec((None, tm, n), lambda j: (j, 0, 0)),
        out_shape=jax.ShapeDtypeStruct((nb, tm, n), BF16),
        scratch_shapes=[pltpu.VMEM((tm, t), BF16)],
        compiler_params=_cparams("arbitrary"),
    )(a, b, *deps)


def _rowwise(name, fn, rows, params, out_rows, out_accs=(), tile=256, deps=()):
    t = rows[0].shape[0]
    r = min(tile, t)
    assert t % r == 0
    n_in, n_or = len(rows) + len(params), len(out_rows)
    n_all = n_in + len(deps)
    params = list(params) + list(deps)

    def body(*refs):
        vals = [ref[...] for ref in refs[:n_in]]
        outs = refs[n_all:]
        ro, ao = fn(*vals)
        for ref, v in zip(outs[:n_or], ro):
            ref[...] = v.astype(ref.dtype)
        if out_accs:
            _accumulate(pl.program_id(0), outs[n_or:], ao)

    def full(shape):
        return pl.BlockSpec(shape, lambda i, nd=len(shape): (0,) * nd)

    return pl.pallas_call(
        body, name=name, grid=(t // r,),
        in_specs=[pl.BlockSpec((r, a.shape[1]), lambda i: (i, 0)) for a in rows] + [full(p.shape) for p in params],
        out_specs=[pl.BlockSpec((r, w), lambda i: (i, 0)) for w, _ in out_rows] + [full(s) for s, _ in out_accs],
        out_shape=[jax.ShapeDtypeStruct((t, w), dt) for w, dt in out_rows] + [jax.ShapeDtypeStruct(s, dt) for s, dt in out_accs],
        compiler_params=_cparams("arbitrary"),
    )(*rows, *params)


def _accumulate(step, refs, vals):
    for ref, v in zip(refs, vals):
        @pl.when(step == 0)
        def _(ref=ref, v=v):
            ref[...] = v.astype(ref.dtype)

        @pl.when(step > 0)
        def _(ref=ref, v=v):
            ref[...] += v.astype(ref.dtype)


def _gla_chunk(q, k, v, la, st):
    c = q.shape[-2]
    row = lax.broadcasted_iota(jnp.int32, (c, c), 0)
    col = lax.broadcasted_iota(jnp.int32, (c, c), 1)
    cum = _cumsum_rows(la)
    cl = jnp.sum(la, axis=-2, keepdims=True)
    ep = jnp.exp(cum)
    en = jnp.exp(-cum)
    qs = q * (GLA_DK ** -0.5)
    qp = qs * ep
    a_f = _bdot(qp, k * en, "nt")
    a_b = _bdot(qs * en, k * ep, "nt")
    sc = jnp.where(row >= col, a_f, a_b)
    o = _bdot(sc, v, "nn") + _bdot(qp, st, "nt")
    kd = k * jnp.exp(cl - cum)
    st_new = st * jnp.exp(cl) + _bdot(v, kd, "tn")
    return o, st_new


def _gla_specs(nc, rev):
    nb = nc // CHUNKS_PER_STEP
    rows = CHUNKS_PER_STEP * CHUNK

    def blk(n):
        return (nb - 1 - n) if rev else n
    hm = pl.BlockSpec((HEADS, rows, GLA_DK), lambda n: (0, blk(n), 0))
    tm = pl.BlockSpec((rows, HEADS * HEAD_W), lambda n: (blk(n), 0))
    st = pl.BlockSpec((HEADS, CHUNKS_PER_STEP, HEAD_W, GLA_DK), lambda n: (0, blk(n), 0, 0))
    return nb, hm, tm, st


def _chunk_rows(c):
    return slice(c * CHUNK, (c + 1) * CHUNK)


def _gla_fwd(q, k, v, la, deps=()):
    t = v.shape[0]
    nc = t // CHUNK
    nb, hm, tm, st = _gla_specs(nc, False)

    def body(q_ref, k_ref, v_ref, la_ref, *rest):
        o_ref, sp_ref, st_ref = rest[len(deps):]

        @pl.when(pl.program_id(0) == 0)
        def _():
            st_ref[...] = jnp.zeros_like(st_ref)

        s = st_ref[...]
        for c in range(CHUNKS_PER_STEP):
            r = _chunk_rows(c)
            sp_ref[:, c] = s
            o, s = _gla_chunk(q_ref[:, r], k_ref[:, r], _heads(v_ref, r), la_ref[:, r], s)
            _put_heads(o_ref, o, r)
        st_ref[...] = s

    return pl.pallas_call(
        body, name="gla_fwd", grid=(nb,),
        in_specs=[hm, hm, tm, hm] + [ANY] * len(deps), out_specs=[tm, st],
        out_shape=[jax.ShapeDtypeStruct((t, HEADS * HEAD_W), F32), jax.ShapeDtypeStruct((HEADS, nc, HEAD_W, GLA_DK), F32)],
        scratch_shapes=[pltpu.VMEM((HEADS, HEAD_W, GLA_DK), F32)],
        compiler_params=_cparams("arbitrary"),
    )(q, k, v, la, *deps)


def _gla_bwd(q, k, v, la, sp, do):
    t = v.shape[0]
    nc = t // CHUNK
    nb, hm, tm, st = _gla_specs(nc, True)

    def body(q_ref, k_ref, v_ref, la_ref, sp_ref, do_ref, dq_ref, dk_ref, dv_ref, dla_ref, ds_ref):
        @pl.when(pl.program_id(0) == 0)
        def _():
            ds_ref[...] = jnp.zeros_like(ds_ref)

        ds = ds_ref[...]
        for c in reversed(range(CHUNKS_PER_STEP)):
            r = _chunk_rows(c)
            _, vjp = jax.vjp(_gla_chunk, q_ref[:, r], k_ref[:, r], _heads(v_ref, r), la_ref[:, r], sp_ref[:, c])
            dq, dk, dv, dla, ds = vjp((_heads(do_ref, r), ds))
            dq_ref[:, r] = dq
            dk_ref[:, r] = dk
            _put_heads(dv_ref, dv, r)
            dla_ref[:, r] = dla
        ds_ref[...] = ds

    hm_shape = jax.ShapeDtypeStruct((HEADS, t, GLA_DK), F32)
    return pl.pallas_call(
        body, name="gla_bwd", grid=(nb,),
        in_specs=[hm, hm, tm, hm, st, tm], out_specs=[hm, hm, tm, hm],
        out_shape=[hm_shape, hm_shape, jax.ShapeDtypeStruct((t, HEADS * HEAD_W), F32), hm_shape],
        scratch_shapes=[pltpu.VMEM((HEADS, HEAD_W, GLA_DK), F32)],
        compiler_params=_cparams("arbitrary"),
    )(q, k, v, la, sp, do)


def _ml_chunk(q, k, v, li_r, lf_r, cm, nv, m):
    c = q.shape[-2]
    row = lax.broadcasted_iota(jnp.int32, (c, c), 0)
    col = lax.broadcasted_iota(jnp.int32, (c, c), 1)
    eye = (row == col).astype(F32)
    li_c = jnp.sum(eye * li_r, axis=-1, keepdims=True)
    lf_c = jnp.sum(eye * lf_r, axis=-1, keepdims=True)
    fc_c = jnp.sum((col <= row).astype(F32) * lf_r, axis=-1, keepdims=True)
    fc_r = jnp.sum((row <= col).astype(F32) * lf_c, axis=-2, keepdims=True)
    f_last = jnp.sum(lf_r, axis=-1, keepdims=True)
    kc = k * (HEAD_W ** -0.5)
    a_c = f_last - fc_c + li_c
    m_loc = jnp.max(a_c, axis=-2, keepdims=True)
    kw = kc * jnp.exp(a_c - m_loc)
    c_chunk = _bdot(kw, v, "tn")
    n_chunk = jnp.sum(kw, axis=-2, keepdims=True)
    m_new = jnp.maximum(f_last + m, m_loc)
    sp = jnp.exp(f_last + m - m_new)
    sl = jnp.exp(m_loc - m_new)
    cm_new = sp * cm + sl * c_chunk
    nv_new = sp * nv + sl * n_chunk
    log_d = li_r - _abs(fc_c - fc_r)
    g_inter = fc_c + m
    m_t = jnp.maximum(g_inter, jnp.max(log_d, axis=-1, keepdims=True))
    s = _bdot(q, kc, "nt") * jnp.exp(log_d - m_t)
    sc = jnp.exp(g_inter - m_t)
    num = _bdot(s, v, "nn") + sc * _bdot(q, cm, "nn")
    den = jnp.sum(s, axis=-1, keepdims=True) + sc * jnp.sum(q * nv, axis=-1, keepdims=True)
    den = jnp.maximum(_abs(den), jnp.exp(-m_t))
    return num / den, cm_new, nv_new, m_new


def _ml_specs(nc, rev):
    nb = nc // CHUNKS_PER_STEP

    def blk(n):
        return (nb - 1 - n) if rev else n
    tm = pl.BlockSpec((CHUNKS_PER_STEP * CHUNK, HEADS * HEAD_W), lambda n: (blk(n), 0))
    gate = pl.BlockSpec((HEADS, CHUNKS_PER_STEP, 1, CHUNK), lambda n: (0, blk(n), 0, 0))
    cm = pl.BlockSpec((HEADS, CHUNKS_PER_STEP, HEAD_W, HEAD_W), lambda n: (0, blk(n), 0, 0))
    vec = pl.BlockSpec((HEADS, CHUNKS_PER_STEP, 1, HEAD_W), lambda n: (0, blk(n), 0, 0))
    return nb, tm, gate, cm, vec


_ML_STATE = [pltpu.VMEM((HEADS, HEAD_W, HEAD_W), F32), pltpu.VMEM((HEADS, 1, HEAD_W), F32), pltpu.VMEM((HEADS, 1, HEAD_W), F32)]


def _ml_fwd(q, k, v, li, lf):
    t = q.shape[0]
    nc = t // CHUNK
    nb, tm, gate, cm, vec = _ml_specs(nc, False)

    def body(q_ref, k_ref, v_ref, li_ref, lf_ref, hc_ref, cp_ref, np_ref, mp_ref, c_ref, n_ref, m_ref):
        @pl.when(pl.program_id(0) == 0)
        def _():
            c_ref[...] = jnp.zeros_like(c_ref)
            n_ref[...] = jnp.zeros_like(n_ref)
            m_ref[...] = jnp.zeros_like(m_ref)

        cs, ns, ms = c_ref[...], n_ref[...], m_ref[...][:, :, 0:1]
        for c in range(CHUNKS_PER_STEP):
            r = _chunk_rows(c)
            cp_ref[:, c] = cs
            np_ref[:, c] = ns
            mp_ref[:, c] = jnp.broadcast_to(ms, m_ref.shape)
            hc, cs, ns, ms = _ml_chunk(_heads(q_ref, r), _heads(k_ref, r), _heads(v_ref, r), li_ref[:, c], lf_ref[:, c],
                                       cs, ns, ms)
            _put_heads(hc_ref, hc, r)
        c_ref[...] = cs
        n_ref[...] = ns
        m_ref[...] = jnp.broadcast_to(ms, m_ref.shape)

    return pl.pallas_call(
        body, name="mlstm_fwd", grid=(nb,),
        in_specs=[tm, tm, tm, gate, gate], out_specs=[tm, cm, vec, vec],
        out_shape=[jax.ShapeDtypeStruct((t, HEADS * HEAD_W), F32), jax.ShapeDtypeStruct((HEADS, nc, HEAD_W, HEAD_W), F32),
                   jax.ShapeDtypeStruct((HEADS, nc, 1, HEAD_W), F32), jax.ShapeDtypeStruct((HEADS, nc, 1, HEAD_W), F32)],
        scratch_shapes=_ML_STATE,
        compiler_params=_cparams("arbitrary"),
    )(q, k, v, li, lf)


def _ml_bwd(q, k, v, li, lf, cp, npv, mp, dhc):
    t = q.shape[0]
    nc = t // CHUNK
    nb, tm, gate, cm, vec = _ml_specs(nc, True)

    def body(q_ref, k_ref, v_ref, li_ref, lf_ref, cp_ref, np_ref, mp_ref, dhc_ref,
             dq_ref, dk_ref, dv_ref, dli_ref, dlf_ref, dc_ref, dn_ref, dm_ref):
        @pl.when(pl.program_id(0) == 0)
        def _():
            dc_ref[...] = jnp.zeros_like(dc_ref)
            dn_ref[...] = jnp.zeros_like(dn_ref)
            dm_ref[...] = jnp.zeros_like(dm_ref)

        dc, dn, dm = dc_ref[...], dn_ref[...], dm_ref[...][:, :, 0:1]
        for c in reversed(range(CHUNKS_PER_STEP)):
            r = _chunk_rows(c)
            _, vjp = jax.vjp(_ml_chunk, _heads(q_ref, r), _heads(k_ref, r), _heads(v_ref, r), li_ref[:, c], lf_ref[:, c],
                             cp_ref[:, c], np_ref[:, c], mp_ref[:, c][:, :, 0:1])
            dq, dk, dv, dli, dlf, dc, dn, dm = vjp((_heads(dhc_ref, r), dc, dn, dm))
            _put_heads(dq_ref, dq, r)
            _put_heads(dk_ref, dk, r)
            _put_heads(dv_ref, dv, r)
            dli_ref[:, c] = dli
            dlf_ref[:, c] = dlf
        dc_ref[...] = dc
        dn_ref[...] = dn
        dm_ref[...] = jnp.broadcast_to(dm, dm_ref.shape)

    tm_shape = jax.ShapeDtypeStruct((t, HEADS * HEAD_W), F32)
    gate_shape = jax.ShapeDtypeStruct((HEADS, nc, 1, CHUNK), F32)
    return pl.pallas_call(
        body, name="mlstm_bwd", grid=(nb,),
        in_specs=[tm, tm, tm, gate, gate, cm, vec, vec, tm], out_specs=[tm, tm, tm, gate, gate],
        out_shape=[tm_shape, tm_shape, tm_shape, gate_shape, gate_shape],
        scratch_shapes=_ML_STATE,
        compiler_params=_cparams("arbitrary"),
    )(q, k, v, li, lf, cp, npv, mp, dhc)


def _ml_pre(s0, s1, s2, s3, cw0, cw1, cw2, cw3, cb, wq, wk, wv, wiq, wik, wiv, bif):
    pre = cb + cw0 * s0 + cw1 * s1 + cw2 * s2 + cw3 * s3
    xc = pre * _sigmoid(pre)
    q = _bdot(xc, wq, "nn")
    k = _bdot(xc, wk, "nn")
    v = _bdot(s3, wv, "nn")
    gates = _bdot(q, wiq, "nn") + _bdot(k, wik, "nn") + _bdot(v, wiv, "nn") + bif
    lane = lax.broadcasted_iota(jnp.int32, gates.shape, 1)
    gl = jnp.where(lane < HEADS, gates, _log_sigmoid(gates))
    return xc, q, k, v, gl


def _delayed(xs_ref, x_ref, halo_ref, r):
    xs_ref[0:HALO, :] = halo_ref[...]
    xs_ref[HALO:HALO + r, :] = x_ref[...]
    return [xs_ref[pl.ds(HALO - (CONV_K - 1) + j, r), :] for j in range(CONV_K)]


def _full_spec(shape):
    return pl.BlockSpec(shape, lambda i, nd=len(shape): (0,) * nd)


def _ml_pre_fwd(x_m, x_pad, params, tile=256):
    t, w = x_m.shape
    r = min(tile, t)

    def body(*refs):
        x_ref, halo_ref = refs[:2]
        p = [ref[...] for ref in refs[2:2 + len(params)]]
        outs = refs[2 + len(params):-1]
        res = _ml_pre(*_delayed(refs[-1], x_ref, halo_ref, r), *p)
        for ref, val in zip(outs, res):
            ref[...] = val

    row = pl.BlockSpec((r, w), lambda i: (i, 0))
    return pl.pallas_call(
        body, name="ml_pre_fwd", grid=(t // r,),
        in_specs=[row, pl.BlockSpec((HALO, w), lambda i: (i * (r // HALO), 0))] + [_full_spec(p.shape) for p in params],
        out_specs=[row] * 4 + [pl.BlockSpec((r, LANES), lambda i: (i, 0))],
        out_shape=[jax.ShapeDtypeStruct((t, w), F32)] * 4 + [jax.ShapeDtypeStruct((t, LANES), F32)],
        scratch_shapes=[pltpu.VMEM((r + HALO, w), F32)],
        compiler_params=_cparams("arbitrary"),
    )(x_m, x_pad, *params)


def _ml_pre_bwd(x_m, x_pad, params, cts, tile=256):
    t, w = x_m.shape
    r = min(tile, t)
    nt = t // r
    n_p = len(params)

    def body(*refs):
        x_ref, halo_ref = refs[:2]
        p = [ref[...] for ref in refs[2:2 + n_p]]
        ct = [ref[...] for ref in refs[2 + n_p:7 + n_p]]
        dx_ref = refs[7 + n_p]
        dp_refs = refs[8 + n_p:8 + 2 * n_p]
        xs_ref, ds_ref, carry_ref = refs[8 + 2 * n_p:]
        step = pl.program_id(0)

        @pl.when(step == 0)
        def _():
            ds_ref[...] = jnp.zeros_like(ds_ref)
            carry_ref[...] = jnp.zeros_like(carry_ref)

        _, vjp = jax.vjp(_ml_pre, *_delayed(xs_ref, x_ref, halo_ref, r), *p)
        grads = vjp(tuple(ct))
        for j in range(CONV_K):
            ds_ref[j, HALO:HALO + r, :] = grads[j]
        lead = HALO + CONV_K - 1
        d_tile = sum(ds_ref[j, pl.ds(lead - j, r), :] for j in range(CONV_K))
        d_halo = sum(ds_ref[j, pl.ds(CONV_K - 1 - j, HALO), :] for j in range(CONV_K))
        dx_ref[...] = d_tile
        dx_ref[r - HALO:r, :] += carry_ref[...]
        carry_ref[...] = d_halo
        _accumulate(step, dp_refs, grads[CONV_K:])

    row = pl.BlockSpec((r, w), lambda i: (nt - 1 - i, 0))
    return pl.pallas_call(
        body, name="ml_pre_bwd", grid=(nt,),
        in_specs=[row, pl.BlockSpec((HALO, w), lambda i: ((nt - 1 - i) * (r // HALO), 0))] + [_full_spec(p.shape) for p in params]
        + [row] * 4 + [pl.BlockSpec((r, LANES), lambda i: (nt - 1 - i, 0))],
        out_specs=[row] + [_full_spec(p.shape) for p in params],
        out_shape=[jax.ShapeDtypeStruct((t, w), F32)] + [jax.ShapeDtypeStruct(p.shape, F32) for p in params],
        scratch_shapes=[pltpu.VMEM((r + HALO, w), F32), pltpu.VMEM((CONV_K, r + 2 * HALO, w), F32), pltpu.VMEM((HALO, w), F32)],
        compiler_params=_cparams("arbitrary"),
    )(x_m, x_pad, *params, *cts)


def _per_head(fn, row_vals, head_params, shared_params=()):
    return [fn(*[a[:, hs] for a in row_vals], *[p[:, hs] for p in head_params], *shared_params) for hs in _head_slices(HEAD_W)]


def _gla_out(o, g, gn):
    return _rms(o, gn) * (g * _sigmoid(g))


def _ml_out(hc, op, xc, g, sk):
    hcell = hc * _sigmoid(op)
    mu = jnp.mean(hcell, axis=-1, keepdims=True)
    d = hcell - mu
    var = jnp.mean(d * d, axis=-1, keepdims=True)
    return d * lax.rsqrt(var + EPS) * g + sk * xc


def _log_decay(al, w, b):
    return _log_sigmoid(_bdot(al, w, "nn") + b) * (1.0 / GLA_GATE_NORM)


def _merge(ga, gb, ya, yb):
    return _sigmoid(ga) * ya + _sigmoid(gb) * yb


def _post_mix(x, z, gpm, gpl):
    x1 = x + _rms(z, gpm)
    return x1, _rms(x1, gpl)


def _loss_rows(x1, dn, tgt, g):
    e = x1 + _rms(dn, g) - tgt
    return 0.5 * jnp.sum(jnp.mean(e * e, axis=-1, keepdims=True), axis=0, keepdims=True)


def _lin(p):
    return 4 * p[0] + 2 * p[1] + p[2]


def _me():
    return lax.axis_index("x"), lax.axis_index("y"), lax.axis_index("c")


def _flip(p, k):
    return tuple((1 - v) if (k >> (2 - i)) & 1 else v for i, v in enumerate(p))


ANY = pl.BlockSpec(memory_space=pl.ANY)


HBM = pl.BlockSpec(memory_space=pltpu.HBM)
SEM = pl.BlockSpec(memory_space=pltpu.SEMAPHORE)
DATAFLOW = pltpu.SideEffectType.DATAFLOW_SIDE_EFFECTING


SIBLING = 1
OTHER_CHIPS = (2, 4, 6)


def _peer_copies(kinds, srcs, lands, send_sems, recv_sems):
    me = _me()
    copies = []
    for a, (kind, src, land) in enumerate(zip(kinds, srcs, lands)):
        masks = {"gather": range(1, N_DEV), "exchange": range(1, N_DEV), "gather_chips": (SIBLING, *OTHER_CHIPS),
                 "gather_pass": OTHER_CHIPS}[kind]
        for k in masks:
            peer = _flip(me, k)
            if kind == "gather_pass":
                block = land.at[_lin(peer)]
                src_ref, dst_ref, target = block, block, _flip(me, SIBLING)
            else:
                src_ref, dst_ref, target = (src.at[_lin(peer)] if kind == "exchange" else src), land.at[_lin(me)], peer
            copies.append(pltpu.make_async_remote_copy(
                src_ref=src_ref, dst_ref=dst_ref, send_sem=send_sems.at[a * 7 + k - 1], recv_sem=recv_sems.at[a * 7 + k - 1],
                device_id=target, device_id_type=MESH))
    return copies


def _copies_start(kind, srcs, name, after=None, lands=None):
    n = len(srcs)
    extra = [] if after is None else [after]
    kind = [kind] * n if isinstance(kind, str) else list(kind)
    land_shapes = [(s.shape if k == "exchange" else (N_DEV, *s.shape)) for k, s in zip(kind, srcs)]
    lands = [lax.empty(ls, s.dtype) for ls, s in zip(land_shapes, srcs)] if lands is None else lands

    def body(*refs):
        sems = refs[2 * n + len(extra):]
        for cp in _peer_copies(kind, refs[:n], refs[n:2 * n], sems[0], sems[1]):
            cp.start()
        refs[-1][...] = jnp.zeros_like(refs[-1])

    def hbm(a):
        return pltpu.with_memory_space_constraint(a, pltpu.HBM)

    out = pl.pallas_call(
        body, name=name,
        out_shape=(pltpu.SemaphoreType.DMA((7 * n,)), pltpu.SemaphoreType.DMA((7 * n,)),
                   *[pltpu.HBM(s.shape, s.dtype) for s in srcs],
                   *[pltpu.HBM(ls, s.dtype) for ls, s in zip(land_shapes, srcs)],
                   jax.ShapeDtypeStruct((8, LANES), F32)),
        in_specs=[HBM] * (2 * n) + [ANY] * len(extra),
        out_specs=(SEM, SEM, *[HBM] * (2 * n), pl.BlockSpec(memory_space=pltpu.VMEM)),
        input_output_aliases={i: 2 + i for i in range(2 * n)},
        compiler_params=pltpu.CompilerParams(has_side_effects=DATAFLOW),
    )(*[hbm(s) for s in srcs], *[hbm(a) for a in lands], *extra)
    return (kind, n, out[:-1]), out[-1]


def _copies_wait(state, after, name):
    kind, n, (send_sems, recv_sems, *thru) = state
    after = list(after) if isinstance(after, (list, tuple)) else [after]

    def body(*refs):
        for cp in _peer_copies(kind, refs[:n], refs[n:2 * n], refs[2 * n], refs[2 * n + 1]):
            cp.wait_send()
            cp.wait_recv()

    out = pl.pallas_call(
        body, name=name,
        out_shape=tuple(pltpu.HBM(t.shape, t.dtype) for t in thru),
        in_specs=[HBM] * (2 * n) + [SEM, SEM] + [ANY] * len(after), out_specs=tuple([HBM] * (2 * n)),
        input_output_aliases={i: i for i in range(2 * n)},
        compiler_params=pltpu.CompilerParams(has_side_effects=DATAFLOW),
    )(*thru, send_sems, recv_sems, *after)
    return out[:n], out[n:]


def _adamw(w, g, m, v):
    m2 = ADAM_B1 * m + (1.0 - ADAM_B1) * g
    v2 = ADAM_B2 * v + (1.0 - ADAM_B2) * (g * g)
    m_hat = m2 / (1.0 - ADAM_B1 ** ADAM_STEP)
    v_hat = v2 / (1.0 - ADAM_B2 ** ADAM_STEP)
    delta = -ADAM_LR * (m_hat / (jnp.sqrt(v_hat) + ADAM_EPS) + ADAM_WD * w)
    return delta, m2, v2


def _sum_adamw(lands, parts, me_idx, w, m, v, name, tile=256):
    r, c = w.shape
    nchunks = len(lands)
    tr = min(tile, r // nchunks)
    per_chunk = r // nchunks // tr
    per = 1 + N_DEV

    def body(me_ref, *refs):
        w_ref, m_ref, v_ref, g_ref, d_ref, m2_ref, v2_ref = refs[nchunks * per:]
        for k in range(nchunks):
            own_ref, slots = refs[k * per], refs[k * per + 1:(k + 1) * per]

            @pl.when(pl.program_id(0) // per_chunk == k)
            def _(own_ref=own_ref, slots=slots):
                own = own_ref[...].astype(F32)
                g = None
                for s in range(N_DEV):
                    term = jnp.where(me_ref[0] == s, own, slots[s][...].astype(F32))
                    g = term if g is None else g + term
                d, m2, v2 = _adamw(w_ref[...], g, m_ref[...], v_ref[...])
                g_ref[...] = g
                d_ref[...] = d
                m2_ref[...] = m2
                v2_ref[...] = v2

    def chunk_specs(k):
        def tile_of(i):
            return jnp.clip(i - k * per_chunk, 0, per_chunk - 1)

        def slot_spec(s):
            return pl.BlockSpec((None, tr, c), lambda i, me: (jnp.where(me[0] == s, (s + 1) % N_DEV, s), tile_of(i), 0))
        return [pl.BlockSpec((None, tr, c), lambda i, me: (me[0], tile_of(i), 0))] + [slot_spec(s) for s in range(N_DEV)]

    row = pl.BlockSpec((tr, c), lambda i, me: (i, 0))
    operands = [a for land, part in zip(lands, parts) for a in (part, *[land] * N_DEV)]
    return pl.pallas_call(
        body, name=name,
        grid_spec=pltpu.PrefetchScalarGridSpec(
            num_scalar_prefetch=1, grid=(r // tr,),
            in_specs=[s for k in range(nchunks) for s in chunk_specs(k)] + [row] * 3,
            out_specs=[row] * 4),
        out_shape=[jax.ShapeDtypeStruct((r, c), F32)] * 4,
        compiler_params=_cparams("parallel"),
    )(me_idx, *operands, w, m, v)


def _small_update(name, me_idx, kinds, lands, owns, ws, ms, vs, sums=()):
    n = len(ws)
    lands, owns = list(lands) + [s[0] for s in sums], list(owns) + [s[1] for s in sums]
    kinds = list(kinds) + ["gather"] * len(sums)
    nl = len(lands)

    def summed(me, land_ref, own):
        g = None
        for s in range(N_DEV):
            term = jnp.where(me == s, own, land_ref[s])
            g = term if g is None else g + term
        return g

    def body(me_ref, *refs):
        land_refs, own_refs = refs[:nl], refs[nl:2 * nl]
        w_refs, m_refs, v_refs = (refs[2 * nl + i * n:2 * nl + (i + 1) * n] for i in range(3))
        outs = refs[2 * nl + 3 * n:]
        me = me_ref[0]
        for i in range(n):
            g = summed(me, land_refs[i], own_refs[i][...])
            d, m2, v2 = _adamw(w_refs[i][...], g, m_refs[i][...], v_refs[i][...])
            for ref, val in zip(outs[4 * i:4 * i + 4], (g, d, m2, v2)):
                ref[...] = val
        for i in range(n, nl):
            outs[4 * n + i - n][...] = summed(me, land_refs[i], own_refs[i][...])

    def whole(shape):
        return pl.BlockSpec(shape, lambda i, me, nd=len(shape): (0,) * nd)

    def own_spec(kind, own):
        if kind == "gather":
            return whole(own.shape)
        return pl.BlockSpec((None, *own.shape[1:]), lambda i, me: (me[0], 0, 0))

    shapes = [w.shape for w in ws]
    out_shapes = [s for s in shapes for _ in range(4)] + [s[1].shape for s in sums]
    return pl.pallas_call(
        body, name=name,
        grid_spec=pltpu.PrefetchScalarGridSpec(
            num_scalar_prefetch=1, grid=(1,),
            in_specs=[whole(a.shape) for a in lands] + [own_spec(k, o) for k, o in zip(kinds, owns)]
            + [whole(s) for s in shapes] * 3,
            out_specs=[whole(s) for s in out_shapes]),
        out_shape=[jax.ShapeDtypeStruct(s, F32) for s in out_shapes],
        compiler_params=_cparams("arbitrary"),
    )(me_idx, *lands, *owns, *ws, *ms, *vs)


def _small_view(n, a):
    if a.ndim == 1:
        return a.reshape(1, -1)
    if a.ndim == 3:
        return a.transpose(1, 2, 0).reshape(QKV_BLOCK * QKV_BLOCK, -1)
    return a.T if n == "w_if" else a


def _small_unview(n, a, shape):
    if len(shape) == 1:
        return a.reshape(shape)
    if len(shape) == 3:
        return a.reshape(QKV_BLOCK, QKV_BLOCK, -1).transpose(2, 0, 1)
    return a.T if n == "w_if" else a


def _small_shards(n, g):
    if n == "w_if":
        return g.reshape(N_DEV, -1, g.shape[1]).transpose(0, 2, 1)
    return g.reshape(g.shape[0], N_DEV, -1).transpose(1, 0, 2)


def _small_unshard(n, s):
    if n == "w_if":
        return s.transpose(0, 2, 1).reshape(-1, s.shape[1])
    return s.transpose(1, 0, 2).reshape(s.shape[1], -1)


def _to_hm(a, d):
    t = a.shape[0]
    return a.reshape(t, HEADS, d).transpose(1, 0, 2)


def _from_hm(a):
    h, t, d = a.shape
    return a.transpose(1, 0, 2).reshape(t, h * d)


def _gate_rows(g):
    t = g.shape[0]
    return g.T.reshape(HEADS, t // CHUNK, 1, CHUNK)


def _gate_cols(g):
    h, nc, _, c = g.shape
    return g.reshape(h, nc * c).T


def _blockdiag_dense(w):
    n = w.shape[0] * QKV_BLOCK
    tiled = jnp.tile(w.reshape(n, QKV_BLOCK), (1, n // QKV_BLOCK))
    r = lax.broadcasted_iota(jnp.int32, (n, n), 0)
    c = lax.broadcasted_iota(jnp.int32, (n, n), 1)
    return jnp.where(r // QKV_BLOCK == c // QKV_BLOCK, tiled, 0.0)


def _blockdiag_blocks(dense):
    n = dense[0].shape[0]
    k = len(dense)

    def body(*refs):
        r = lax.broadcasted_iota(jnp.int32, (n, n), 0)
        c = lax.broadcasted_iota(jnp.int32, (n, n), 1)
        fr = lax.broadcasted_iota(jnp.int32, (n, LANES), 0)
        fc = lax.broadcasted_iota(jnp.int32, (n, LANES), 1)
        fold = ((fr & (QKV_BLOCK - 1)) == fc).astype(BF16)
        for i in range(k):
            kept = jnp.where((r >> 2) == (c >> 2), refs[i][...], 0.0)
            refs[k + i][...] = sum(lax.dot_general(t, fold, _dims("nn", 2), preferred_element_type=F32) for t in _split3(kept))

    out = pl.pallas_call(body, name="blockdiag_blocks", out_shape=[jax.ShapeDtypeStruct((n, LANES), F32)] * k)(*dense)
    return [o[:, 0:QKV_BLOCK].reshape(n // QKV_BLOCK, QKV_BLOCK, QKV_BLOCK) for o in out]


def _col_blocks(w):
    k, n = w.shape
    return w.reshape(k, N_DEV, n // N_DEV).transpose(1, 0, 2)


def _from_col_blocks(g):
    d, k, n = g.shape
    return g.transpose(1, 0, 2).reshape(k, d * n)


def _local_step(x, tgt, weight, ws, prefetch, pass_on, on_grads, on_small):
    t, d = x.shape
    g1 = ws["g_pre_mix"]

    def dep(token):
        return () if token is None else (token,)

    w_in = weight("w_in", x)
    fetch_mix = prefetch(("w_pa", "w_pb", "w_o"), w_in)

    n_in = w_in.shape[2]

    def proj_in_fwd(xv, g, w):
        hv = _rms(xv, g)
        return (hv, jnp.concatenate([_raw_dot(hv, w[j], "nn") for j in range(N_DEV)], axis=1)), ()

    h, proj = _rowwise("proj_in", proj_in_fwd, [x], [g1, w_in], [(d, BF16), (N_DEV * n_in, F32)], deps=dep(fetch_mix))
    offs = [0]
    for s in IN_SPLITS:
        offs.append(offs[-1] + s)
    q_a, k_a, v_a, g_a, a_low, x_m, o_pre, gate_a, gate_b = [proj[:, offs[i]:offs[i + 1]] for i in range(9)]

    a_low_p = jnp.pad(a_low, ((0, 0), (0, LANES - LOWRANK)))
    w_a_up_p = jnp.pad(ws["w_a_up"], ((0, LANES - LOWRANK), (0, 0)))
    b_a_up = ws["b_a_up"]
    (la,) = _rowwise("gla_decay", lambda al, w, b: ((_log_decay(al, w, b),), ()), [a_low_p], [w_a_up_p, b_a_up],
                     [(HEADS * GLA_DK, F32)])
    fetch_up = prefetch(("w_up",), la)
    pass_mix = pass_on("w_pa", la)
    q_hm, k_hm, la_hm = _to_hm(q_a, GLA_DK), _to_hm(k_a, GLA_DK), _to_hm(la, GLA_DK)
    o_gla, s_prev = _gla_fwd(q_hm, k_hm, v_a, la_hm, deps=dep(fetch_up) + dep(pass_mix))
    gn = ws["g_gla_norm"]
    ml_w = HEADS * HEAD_W

    def proj_a_fwd(o, g, n_, w):
        ya = jnp.concatenate(_per_head(_gla_out, [o, g], [], [n_]), axis=1)
        return (ya, _raw_dot(ya, w, "nn")), ()

    ya_in, y_a = _rowwise("proj_a", proj_a_fwd, [o_gla, g_a], [gn, weight("w_pa", o_gla)], [(ml_w, BF16), (d, F32)],
                          tile=512)

    cw = ws["conv_w"]
    w_if_p = jnp.pad(ws["w_if"], ((0, 0), (0, LANES - 2 * HEADS)))
    pre_params = [cw[0:1], cw[1:2], cw[2:3], cw[3:4], ws["conv_b"],
                  _blockdiag_dense(ws["w_q_ml"]), _blockdiag_dense(ws["w_k_ml"]), _blockdiag_dense(ws["w_v_ml"]),
                  w_if_p[0:ml_w], w_if_p[ml_w:2 * ml_w], w_if_p[2 * ml_w:3 * ml_w],
                  jnp.pad(ws["b_if"], ((0, 0), (0, LANES - 2 * HEADS)))]
    x_pad = jnp.pad(x_m, ((HALO, 0), (0, 0)))
    xc, q_m, k_m, v_m, gl = _ml_pre_fwd(x_m, x_pad, pre_params)
    li, lf = _gate_rows(gl[:, 0:HEADS]), _gate_rows(gl[:, HEADS:2 * HEADS])
    hc, c_prev, n_prev, m_prev = _ml_fwd(q_m, k_m, v_m, li, lf)
    fetch_down = prefetch(("w_down",), hc)
    g_ml, skip = ws["g_ml_norm"], ws["ml_skip"]

    def proj_b_fwd(a, b, c_, ga, gb, ya, g, s, w):
        hb = jnp.concatenate(_per_head(_ml_out, [a, b, c_], [g, s]), axis=1)
        yb = _raw_dot(hb, w, "nn")
        return (hb, yb, _merge(ga, gb, ya, yb)), ()

    h_b, y_b, merged = _rowwise("proj_b", proj_b_fwd, [hc, o_pre, xc, gate_a, gate_b, y_a], [g_ml, skip, weight("w_pb", hc)],
                                [(ml_w, BF16), (d, F32), (d, BF16)], tile=512, deps=dep(fetch_down))

    gpm, gpl, gpo = ws["g_post_mix"], ws["g_pre_mlp"], ws["g_post_mlp"]

    def proj_o_fwd(mg, xv, w, a, b):
        zv = _raw_dot(mg, w, "nn")
        return (zv, *_post_mix(xv, zv, a, b)), ()

    pass_up = pass_on("w_up", merged)
    z, x1, h2 = _rowwise("proj_o", proj_o_fwd, [merged, x], [weight("w_o", merged), gpm, gpl],
                         [(d, F32), (d, F32), (d, BF16)], tile=512, deps=dep(pass_up))
    pass_down = pass_on("w_down", h2)
    w_up = weight("w_up", h2)
    up, u = _mm(h2, w_up, "nn", (BF16, BF16), "mlp_up", tm=2048, shards="b", deps=dep(pass_down),
                epilogue=lambda p: (p, jnp.square(jnp.maximum(p, 0.0))))

    def mlp_down_loss(uv, x1v, tgtv, w, g):
        dnv = _raw_dot(uv, w, "nn")
        loss, vjp = jax.vjp(lambda a, b, c_: _loss_rows(a, b, tgtv, c_), x1v, dnv, g)
        dx1, ddn, dg = vjp(jnp.ones((1, 1), F32))
        return (dx1, ddn), (jnp.broadcast_to(loss, (1, LANES)), dg)

    dx1_y, d_dn, loss, d_gpo = _rowwise("mlp_down", mlp_down_loss, [u, x1, tgt], [weight("w_down", u), gpo],
                                        [(d, F32), (d, BF16)], [((1, LANES), F32), ((1, d), F32)], tile=512)

    (d_up,) = _mm(d_dn, weight("w_down", u), "nt", (BF16,), "mlp_down_dx", extra=[up],
                  epilogue=lambda p, a: (p * (2.0 * jnp.maximum(a.astype(F32), 0.0)),))
    dw_down = _mm(u, d_dn, "tn", BF16, "mlp_down_dw", tm=512)
    dw_up = _mm(h2, d_up, "tn", BF16, "mlp_up_dw", tn=w_up.shape[2], shards="out")
    sent_mlp = on_grads(dict(w_down=dw_down, w_up=dw_up))

    def mlp_up_dx(dup, xv, zv, dx1, w, a, b):
        _, vjp = jax.vjp(_post_mix, xv, zv, a, b)
        ns = w.shape[2]
        dh2 = sum(_raw_dot(dup[:, j * ns:(j + 1) * ns], w[j], "nt") for j in range(w.shape[0]))
        dx, dz, da, db = vjp((dx1, dh2))
        return (dx, dz), (da, db)

    dx_res, d_z, d_gpm, d_gpl = _rowwise("mlp_up_dx", mlp_up_dx, [d_up, x, z, dx1_y], [w_up, gpm, gpl],
                                         [(d, F32), (d, BF16)], [((1, d), F32), ((1, d), F32)], tile=512, deps=dep(sent_mlp))
    dw_o = _mm(merged, d_z, "tn", BF16, "proj_o_dw")

    def proj_o_dx(dz, ga, gb, ya, yb, w):
        return jax.vjp(_merge, ga, gb, ya, yb)[1](_raw_dot(dz, w, "nt")), ()

    d_ga, d_gb, d_ya, d_yb = _rowwise("proj_o_dx", proj_o_dx, [d_z, gate_a, gate_b, y_a, y_b], [weight("w_o", merged)],
                                      [(d, F32), (d, F32), (d, BF16), (d, BF16)], tile=512)
    dw_pa = _mm(ya_in, d_ya, "tn", BF16, "proj_a_dw")
    dw_pb = _mm(h_b, d_yb, "tn", BF16, "proj_b_dw")
    sent_mix = on_grads(dict(w_o=dw_o, w_pa=dw_pa, w_pb=dw_pb))

    def proj_b_dx(dyb, a, b, c_, w, g, s):
        ct = _raw_dot(dyb, w, "nt")
        parts = []
        for hs in _head_slices(HEAD_W):
            _, vjp = jax.vjp(_ml_out, a[:, hs], b[:, hs], c_[:, hs], g[:, hs], s[:, hs])
            parts.append(vjp(ct[:, hs]))
        cat = lambda i: jnp.concatenate([p[i] for p in parts], axis=1)
        return (cat(0), cat(1), cat(2)), (cat(3), cat(4))

    d_hc, d_opre, d_xc, d_gml, d_skip = _rowwise("proj_b_dx", proj_b_dx, [d_yb, hc, o_pre, xc],
                                                 [weight("w_pb", hc), g_ml, skip], [(ml_w, F32)] * 3, [((1, ml_w), F32)] * 2,
                                                 tile=512, deps=dep(sent_mix))
    d_qm, d_km, d_vm, d_li, d_lf = _ml_bwd(q_m, k_m, v_m, li, lf, c_prev, n_prev, m_prev, d_hc)
    d_gl = jnp.concatenate([_gate_cols(d_li), _gate_cols(d_lf), jnp.zeros((t, LANES - 2 * HEADS), F32)], axis=1)
    pre_grads = _ml_pre_bwd(x_m, x_pad, pre_params, [d_xc, d_qm, d_km, d_vm, d_gl])
    d_xm = pre_grads[0]
    d_cw = jnp.concatenate(pre_grads[1:5], axis=0)
    d_cb = pre_grads[5]
    d_wq, d_wk, d_wv = _blockdiag_blocks(pre_grads[6:9])
    d_wif = jnp.concatenate(pre_grads[9:12], axis=0)[:, 0:2 * HEADS]
    d_bif = pre_grads[12][:, 0:2 * HEADS]

    def proj_a_dx(dya, o, g, w, n_):
        ct = _raw_dot(dya, w, "nt")
        parts = []
        for hs in _head_slices(HEAD_W):
            _, vjp = jax.vjp(_gla_out, o[:, hs], g[:, hs], n_)
            parts.append(vjp(ct[:, hs]))
        cat = lambda i: jnp.concatenate([p[i] for p in parts], axis=1)
        return (cat(0), cat(1)), (sum(p[2] for p in parts),)

    d_o, d_g_a, d_gn = _rowwise("proj_a_dx", proj_a_dx, [d_ya, o_gla, g_a], [weight("w_pa", o_gla), gn], [(ml_w, F32)] * 2,
                                [((1, HEAD_W), F32)], tile=512)
    dq_hm, dk_hm, d_va, dla_hm = _gla_bwd(q_hm, k_hm, v_a, la_hm, s_prev, d_o)

    def decay_bwd(al, ct, w, b):
        _, vjp = jax.vjp(_log_decay, al, w, b)
        dal, dw, db = vjp(ct)
        return (dal,), (dw, db)

    d_alow_p, d_wa_p, d_ba = _rowwise("gla_decay_bwd", decay_bwd, [a_low_p, _from_hm(dla_hm)], [w_a_up_p, b_a_up],
                                      [(LANES, F32)], [(w_a_up_p.shape, F32), (b_a_up.shape, F32)])
    d_proj = jnp.concatenate([_from_hm(dq_hm), _from_hm(dk_hm), d_va, d_g_a, d_alow_p[:, 0:LOWRANK], d_xm, d_opre, d_ga, d_gb],
                             axis=1).astype(BF16)
    small = dict(w_a_up=d_wa_p[0:LOWRANK], b_a_up=d_ba, g_gla_norm=d_gn, conv_w=d_cw, conv_b=d_cb,
                 w_q_ml=d_wq, w_k_ml=d_wk, w_v_ml=d_wv, w_if=d_wif, b_if=d_bif, ml_skip=d_skip, g_ml_norm=d_gml,
                 g_post_mix=d_gpm, g_pre_mlp=d_gpl, g_post_mlp=d_gpo)
    sent_small = on_small(small, loss)
    sent_in = sent_small
    for half in range(2):
        dw_half = _mm_shard_cols(h, d_proj, n_in, "proj_in_dw_%d" % half, half, d // 2, deps=dep(sent_in))
        sent_in = on_grads({"w_in#%d" % half: dw_half})

    def proj_in_dx(dp, xv, dres, w, g):
        _, vjp = jax.vjp(_rms, xv, g)
        dx, dg = vjp(sum(_raw_dot(dp[:, j * n_in:(j + 1) * n_in], w[j], "nt") for j in range(N_DEV)))
        return (dx + dres,), (dg,)

    grad_x, d_g1 = _rowwise("proj_in_dx", proj_in_dx, [d_proj, x, dx_res], [w_in, g1], [(d, F32)], [((1, d), F32)],
                            deps=dep(sent_in))
    return grad_x, on_small(dict(g_pre_mix=d_g1), None)


BIG = ("w_in", "w_pa", "w_pb", "w_o", "w_up", "w_down")
BIG_COL_SHARDED = ("w_in", "w_pa", "w_pb", "w_up")
SMALL_SHARDED = ("w_a_up", "conv_w", "w_if")
SMALL = ("g_pre_mix", "w_a_up", "b_a_up", "g_gla_norm", "conv_w", "conv_b", "w_q_ml", "w_k_ml", "w_v_ml", "w_if", "b_if",
         "ml_skip", "g_ml_norm", "g_post_mix", "g_pre_mlp", "g_post_mlp")
WEIGHTS = ("g_pre_mix", "w_in", "w_a_up", "b_a_up", "g_gla_norm", "conv_w", "conv_b", "w_q_ml", "w_k_ml", "w_v_ml", "w_if", "b_if",
           "ml_skip", "g_ml_norm", "w_pa", "w_pb", "w_o", "g_post_mix", "g_pre_mlp", "w_up", "w_down", "g_post_mlp")


def kernel(x, g_pre_mix, w_in, w_a_up, b_a_up, g_gla_norm, conv_w, conv_b, w_q_ml, w_k_ml, w_v_ml, w_if, b_if, ml_skip, g_ml_norm, w_pa, w_pb, w_o, g_post_mix, g_pre_mlp, w_up, w_down, g_post_mlp, loss_target, m_g_pre_mix, m_w_in, m_w_a_up, m_b_a_up, m_g_gla_norm, m_conv_w, m_conv_b, m_w_q_ml, m_w_k_ml, m_w_v_ml, m_w_if, m_b_if, m_ml_skip, m_g_ml_norm, m_w_pa, m_w_pb, m_w_o, m_g_post_mix, m_g_pre_mlp, m_w_up, m_w_down, m_g_post_mlp, v_g_pre_mix, v_w_in, v_w_a_up, v_b_a_up, v_g_gla_norm, v_conv_w, v_conv_b, v_w_q_ml, v_w_k_ml, v_w_v_ml, v_w_if, v_b_if, v_ml_skip, v_g_ml_norm, v_w_pa, v_w_pb, v_w_o, v_g_post_mix, v_g_pre_mlp, v_w_up, v_w_down, v_g_post_mlp):
    args = dict(locals())
    w = {n: args[n][0] for n in WEIGHTS}
    m = {n: args["m_" + n][0] for n in WEIGHTS}
    v = {n: args["v_" + n][0] for n in WEIGHTS}

    me_lin = _lin(_me())
    me_idx = jnp.reshape(me_lin, (1,)).astype(jnp.int32)

    def full_weight(n, g):
        if n in ("w_in", "w_up"):
            return g
        return _from_col_blocks(g) if n in BIG_COL_SHARDED else g.reshape(-1, g.shape[-1])

    def grad_parts(n, g):
        if n.partition("#")[0] in ("w_in", "w_up"):
            return g
        return (_col_blocks(g) if n in BIG_COL_SHARDED else g.reshape(N_DEV, -1, g.shape[-1])).astype(BF16)

    sharded_names = tuple(SMALL_SHARDED)
    small_w_state, small_w_token = _copies_start("gather", [_small_view(n, w[n]) for n in sharded_names],
                                                 "allgather_start_small_weights")
    narrow = {n: w[n].astype(BF16) for n in BIG}
    ready, pending = {}, {}

    def prefetch(group, after):
        state, token = _copies_start("gather_chips", [narrow[n] for n in group], "allgather_start_" + group[0], after)
        for n in group:
            pending[n] = (group, state)
        return token

    prefetch(("w_in",), small_w_token)

    passing = {}

    def pass_on(n, after):
        group, state = pending[n]
        shards, lands = _copies_wait(state, after, "allgather_wait_" + group[0])
        state, token = _copies_start("gather_pass", shards, "allgather_pass_" + group[0], lands=lands)
        for gn in group:
            passing[gn] = (group, state)
        return token

    def weight(n, after):
        if n not in ready:
            group, state = passing[n]
            shards, lands = _copies_wait(state, after, "allgather_passed_" + group[0])
            for gn, shard, land in zip(group, shards, lands):
                ready[gn] = full_weight(gn, lax.dynamic_update_slice(land, shard[None], (me_lin, 0, 0)))
        return ready[n]

    small_w_own, small_w_lands = _copies_wait(small_w_state, [narrow[n] for n in BIG if n != "w_in"], "allgather_wait_small_weights")
    ws = {n: (w[n].reshape(1, -1) if w[n].ndim == 1 else w[n]) for n in SMALL if n not in SMALL_SHARDED}
    for n, own, land in zip(sharded_names, small_w_own, small_w_lands):
        ws[n] = _small_unshard(n, lax.dynamic_update_slice(land, own[None], (me_lin, 0, 0)))
    pass_on("w_in", [ws[n] for n in sharded_names])

    sent = []

    def on_grads(grads):
        names = tuple(grads)
        state, token = _copies_start("exchange", [grad_parts(n, grads[n]) for n in names],
                                     "exchange_start_" + names[0].replace("#", "_"))
        sent.append((names, state))
        return token

    small_sent = []

    def on_small(small, loss):
        names = tuple(small)
        kinds = ["exchange" if n in SMALL_SHARDED else "gather" for n in names]
        srcs = [_small_shards(n, small[n]) if n in SMALL_SHARDED else _small_view(n, small[n]) for n in names]
        extra = [] if loss is None else [loss]
        state, token = _copies_start(kinds + ["gather"] * len(extra), srcs + extra, "allgather_start_small_" + names[0])
        small_sent.append((names, kinds, state))
        return token

    grad_x, last_token = _local_step(x[0], loss_target[0], weight, ws, prefetch, pass_on, on_grads, on_small)

    out = {}

    chunks = {}

    def finish(names, state, after):
        parts, lands = _copies_wait(state, after, "exchange_wait_" + names[0].replace("#", "_"))
        for name, part, land in zip(names, parts, lands):
            n, _, chunk = name.partition("#")
            chunks.setdefault(n, []).append((land, part))
            if chunk in ("", "1"):
                got_lands, got_parts = zip(*chunks[n])
                out[n] = _sum_adamw(got_lands, got_parts, me_idx, w[n], m[n], v[n], "adamw_" + n)

    def finish_small(names, kinds, state, after):
        own, lands = _copies_wait(state, after, "allgather_wait_small_" + names[0])
        k = len(names)
        upd = _small_update("adamw_small_" + names[0], me_idx, kinds, lands[:k], own[:k],
                            *[[_small_view(n, d[n]) for n in names] for d in (w, m, v)], sums=list(zip(lands[k:], own[k:])))
        for i, n in enumerate(names):
            out[n] = tuple(_small_unview(n, a, w[n].shape) for a in upd[4 * i:4 * i + 4])
        return upd[4 * k:]

    (loss_sum,) = finish_small(*small_sent[0], [grad_x, last_token])
    for names, state in sent[:-2]:
        finish(names, state, [grad_x, last_token])
    finish(*sent[-2], [loss_sum] + [out[n][1] for n in BIG if n in out])
    finish(*sent[-1], [loss_sum])
    finish_small(*small_sent[1], [out["w_in"][1]])

    shaped = lambda a, n: a.reshape(args[n].shape)
    return (loss_sum[0, 0], grad_x[None],
            *[shaped(out[n][0], n) for n in WEIGHTS], *[shaped(out[n][1], n) for n in WEIGHTS],
            *[shaped(out[n][2], n) for n in WEIGHTS], *[shaped(out[n][3], n) for n in WEIGHTS])
```

```python
import functools

import jax
import jax.numpy as jnp
from jax import lax
from jax.experimental import pallas as pl
from jax.experimental.pallas import tpu as pltpu

F32 = jnp.float32
BF16 = jnp.bfloat16
MESH = pl.DeviceIdType.MESH

N_DEV = 8
EPS = 1e-6
CHUNK = 64
CHUNKS_PER_STEP = 4
HEADS = 4
GLA_DK = 64
HEAD_W = 128
GLA_GATE_NORM = 16.0
LOWRANK = 16
CONV_K = 4
QKV_BLOCK = 4
LANES = 128
HALO = 8
IN_SPLITS = (256, 256, 512, 512, 16, 512, 512, 1024, 1024)

ADAM_LR = 0.001
ADAM_B1 = 0.9
ADAM_B2 = 0.999
ADAM_EPS = 1e-08
ADAM_WD = 0.01
ADAM_STEP = 10

VMEM_LIMIT = 56 * 1024 * 1024


def _cparams(*sem):
    return pltpu.CompilerParams(dimension_semantics=sem, vmem_limit_bytes=VMEM_LIMIT)


def _dims(mode, ndim):
    contract = {"nn": ((ndim - 1,), (ndim - 2,)), "nt": ((ndim - 1,), (ndim - 1,)), "tn": ((ndim - 2,), (ndim - 2,))}[mode]
    return contract, (((0,), (0,)) if ndim == 3 else ((), ()))


def _raw_dot(a, b, mode):
    return lax.dot_general(a.astype(BF16), b.astype(BF16), _dims(mode, a.ndim), preferred_element_type=F32)


@functools.partial(jax.custom_vjp, nondiff_argnums=(2,))
def _bdot(a, b, mode):
    return _raw_dot(a, b, mode)


def _bdot_fwd(a, b, mode):
    return _raw_dot(a, b, mode), (a, b)


def _bdot_bwd(mode, res, ct):
    a, b = res
    if mode == "nn":
        da, db = _raw_dot(ct, b, "nt"), _raw_dot(a, ct, "tn")
    elif mode == "nt":
        da, db = _raw_dot(ct, b, "nn"), _raw_dot(ct, a, "tn")
    else:
        da, db = _raw_dot(b, ct, "nt"), _raw_dot(a, ct, "nn")
    return da.astype(a.dtype), db.astype(b.dtype)


_bdot.defvjp(_bdot_fwd, _bdot_bwd)


def _split3(x):
    hi = x.astype(BF16)
    r1 = x - hi.astype(F32)
    mid = r1.astype(BF16)
    return hi, mid, (r1 - mid.astype(F32)).astype(BF16)


def _split_dot(tri, x):
    if x.ndim == 3:
        tri = jnp.broadcast_to(tri, (x.shape[0], *tri.shape))
    return sum(lax.dot_general(tri, t, _dims("nn", x.ndim), preferred_element_type=F32) for t in _split3(x))


def _tri(n, lower):
    r = lax.broadcasted_iota(jnp.int32, (n, n), 0)
    c = lax.broadcasted_iota(jnp.int32, (n, n), 1)
    return ((c <= r) if lower else (c >= r)).astype(BF16)


@jax.custom_vjp
def _cumsum_rows(x):
    return _split_dot(_tri(x.shape[-2], True), x)


def _cumsum_rows_fwd(x):
    return _cumsum_rows(x), None


def _cumsum_rows_bwd(_, ct):
    return (_split_dot(_tri(ct.shape[-2], False), ct),)


_cumsum_rows.defvjp(_cumsum_rows_fwd, _cumsum_rows_bwd)


def _abs(x):
    return jnp.where(x >= 0, x, -x)


def _sigmoid(x):
    return lax.logistic(x)


def _log_sigmoid(x):
    return jnp.minimum(x, 0.0) - jnp.log(1.0 + jnp.exp(-_abs(x)))


def _rms(x, g):
    return x * lax.rsqrt(jnp.mean(x * x, axis=-1, keepdims=True) + EPS) * g


def _head_slices(w):
    return [slice(h * w, (h + 1) * w) for h in range(HEADS)]


def _heads(ref, rows=slice(None)):
    return jnp.stack([ref[rows, hs] for hs in _head_slices(HEAD_W)])


def _put_heads(ref, val, rows=slice(None)):
    for h, hs in enumerate(_head_slices(HEAD_W)):
        ref[rows, hs] = val[h].astype(ref.dtype)


def _tile(dim, want):
    if dim <= want or dim % LANES:
        return dim
    t = want
    while dim % t:
        t -= LANES
    return t


def _mm(a, b, mode, out_dtype, name, tm=1024, tn=1024, tk=4096, epilogue=None, extra=(), deps=(), shards=None):
    if shards == "b":
        assert mode == "nn"
        ns = b.shape[2]
        (m, k), (k2, n) = a.shape, (b.shape[1], b.shape[0] * ns)
        tn = ns
    elif mode == "nn":
        (m, k), (k2, n) = a.shape, b.shape
    elif mode == "nt":
        (m, k), (n, k2) = a.shape, b.shape
    else:
        (k, m), (k2, n) = a.shape, b.shape
    assert k == k2, (name, a.shape, b.shape)
    tm, tn, tk = _tile(m, tm), _tile(n, tn), _tile(k, tk)
    nk = k // tk
    out_dtypes = out_dtype if epilogue else (out_dtype,)
    assert nk == 1 or (out_dtype == F32 and not epilogue), name
    n_in = 2 + len(extra)

    def body(*refs):
        p = _raw_dot(refs[0][...], refs[1][...], mode)
        if nk > 1:
            _accumulate(pl.program_id(2), [refs[n_in + len(deps)]], [p])
            return
        outs = epilogue(p, *[r[...] for r in refs[2:n_in]]) if epilogue else (p,)
        for ref, val in zip(refs[n_in + len(deps):], outs):
            ref[...] = val.astype(ref.dtype)

    a_spec = pl.BlockSpec((tk, tm), lambda i, j, kk: (kk, i)) if mode == "tn" else pl.BlockSpec((tm, tk), lambda i, j, kk: (i, kk))
    if shards == "b":
        b_spec = pl.BlockSpec((None, tk, tn), lambda i, j, kk: (j, kk, 0))
    elif mode == "nt":
        b_spec = pl.BlockSpec((tn, tk), lambda i, j, kk: (j, kk))
    else:
        b_spec = pl.BlockSpec((tk, tn), lambda i, j, kk: (kk, j))
    o_spec = pl.BlockSpec((tm, tn), lambda i, j, kk: (i, j))
    if shards == "out":
        out_specs, out_shape = [pl.BlockSpec((None, tm, tn), lambda i, j, kk: (j, i, 0))], (n // tn, m, tn)
    else:
        out_specs, out_shape = [o_spec] * len(out_dtypes), (m, n)
    res = pl.pallas_call(
        body, name=name, grid=(m // tm, n // tn, nk),
        in_specs=[a_spec, b_spec] + [o_spec] * len(extra) + [ANY] * len(deps), out_specs=out_specs,
        out_shape=[jax.ShapeDtypeStruct(out_shape, dt) for dt in out_dtypes],
        compiler_params=_cparams("parallel", "parallel", "arbitrary"),
    )(a, b, *extra, *deps)
    return res if epilogue else res[0]


def _mm_shard_cols(a, b, n, name, row_tile, tm, deps=()):
    t = a.shape[0]
    nb = b.shape[1] // n

    def body(a_ref, b_ref, *rest):
        o_ref, at_ref = rest[len(deps):]
        j = pl.program_id(0)

        @pl.when(j == 0)
        def _():
            at_ref[...] = a_ref[...].astype(BF16).T

        for s in range(nb):
            @pl.when(j == s)
            def _(s=s):
                o_ref[...] = _raw_dot(at_ref[...], b_ref[:, s * n:(s + 1) * n], "nn").astype(BF16)

    return pl.pallas_call(
        body, name=name, grid=(nb,),
        in_specs=[pl.BlockSpec((t, tm), lambda j: (0, row_tile)),
                  pl.BlockSpec((t, nb * n), lambda j: (0, 0), pipeline_mode=pl.Buffered(1))] + [ANY] * len(deps),
        out_specs=pl.BlockSpec((None, tm, n), lambda j: (j, 0, 0)),
        out_shape=jax.ShapeDtypeStruct((nb, tm, n), BF16),
        scratch_shapes=[pltpu.VMEM((tm, t), BF16)],
        compiler_params=_cparams("arbitrary"),
    )(a, b, *deps)


def _rowwise(name, fn, rows, params, out_rows, out_accs=(), tile=256, deps=()):
    t = rows[0].shape[0]
    r = min(tile, t)
    assert t % r == 0
    n_in, n_or = len(rows) + len(params), len(out_rows)
    n_all = n_in + len(deps)
    params = list(params) + list(deps)

    def body(*refs):
        vals = [ref[...] for ref in refs[:n_in]]
        outs = refs[n_all:]
        ro, ao = fn(*vals)
        for ref, v in zip(outs[:n_or], ro):
            ref[...] = v.astype(ref.dtype)
        if out_accs:
            _accumulate(pl.program_id(0), outs[n_or:], ao)

    def full(shape):
        return pl.BlockSpec(shape, lambda i, nd=len(shape): (0,) * nd)

    return pl.pallas_call(
        body, name=name, grid=(t // r,),
        in_specs=[pl.BlockSpec((r, a.shape[1]), lambda i: (i, 0)) for a in rows] + [full(p.shape) for p in params],
        out_specs=[pl.BlockSpec((r, w), lambda i: (i, 0)) for w, _ in out_rows] + [full(s) for s, _ in out_accs],
        out_shape=[jax.ShapeDtypeStruct((t, w), dt) for w, dt in out_rows] + [jax.ShapeDtypeStruct(s, dt) for s, dt in out_accs],
        compiler_params=_cparams("arbitrary"),
    )(*rows, *params)


def _accumulate(step, refs, vals):
    for ref, v in zip(refs, vals):
        @pl.when(step == 0)
        def _(ref=ref, v=v):
            ref[...] = v.astype(ref.dtype)

        @pl.when(step > 0)
        def _(ref=ref, v=v):
            ref[...] += v.astype(ref.dtype)


def _gla_chunk(q, k, v, la, st):
    c = q.shape[-2]
    row = lax.broadcasted_iota(jnp.int32, (c, c), 0)
    col = lax.broadcasted_iota(jnp.int32, (c, c), 1)
    cum = _cumsum_rows(la)
    cl = jnp.sum(la, axis=-2, keepdims=True)
    ep = jnp.exp(cum)
    en = jnp.exp(-cum)
    qs = q * (GLA_DK ** -0.5)
    qp = qs * ep
    a_f = _bdot(qp, k * en, "nt")
    a_b = _bdot(qs * en, k * ep, "nt")
    sc = jnp.where(row >= col, a_f, a_b)
    o = _bdot(sc, v, "nn") + _bdot(qp, st, "nt")
    kd = k * jnp.exp(cl - cum)
    st_new = st * jnp.exp(cl) + _bdot(v, kd, "tn")
    return o, st_new


def _gla_specs(nc, rev):
    nb = nc // CHUNKS_PER_STEP
    rows = CHUNKS_PER_STEP * CHUNK

    def blk(n):
        return (nb - 1 - n) if rev else n
    hm = pl.BlockSpec((HEADS, rows, GLA_DK), lambda n: (0, blk(n), 0))
    tm = pl.BlockSpec((rows, HEADS * HEAD_W), lambda n: (blk(n), 0))
    st = pl.BlockSpec((HEADS, CHUNKS_PER_STEP, HEAD_W, GLA_DK), lambda n: (0, blk(n), 0, 0))
    return nb, hm, tm, st


def _chunk_rows(c):
    return slice(c * CHUNK, (c + 1) * CHUNK)


def _gla_fwd(q, k, v, la, deps=()):
    t = v.shape[0]
    nc = t // CHUNK
    nb, hm, tm, st = _gla_specs(nc, False)

    def body(q_ref, k_ref, v_ref, la_ref, *rest):
        o_ref, sp_ref, st_ref = rest[len(deps):]

        @pl.when(pl.program_id(0) == 0)
        def _():
            st_ref[...] = jnp.zeros_like(st_ref)

        s = st_ref[...]
        for c in range(CHUNKS_PER_STEP):
            r = _chunk_rows(c)
            sp_ref[:, c] = s
            o, s = _gla_chunk(q_ref[:, r], k_ref[:, r], _heads(v_ref, r), la_ref[:, r], s)
            _put_heads(o_ref, o, r)
        st_ref[...] = s

    return pl.pallas_call(
        body, name="gla_fwd", grid=(nb,),
        in_specs=[hm, hm, tm, hm] + [ANY] * len(deps), out_specs=[tm, st],
        out_shape=[jax.ShapeDtypeStruct((t, HEADS * HEAD_W), F32), jax.ShapeDtypeStruct((HEADS, nc, HEAD_W, GLA_DK), F32)],
        scratch_shapes=[pltpu.VMEM((HEADS, HEAD_W, GLA_DK), F32)],
        compiler_params=_cparams("arbitrary"),
    )(q, k, v, la, *deps)


def _gla_bwd(q, k, v, la, sp, do):
    t = v.shape[0]
    nc = t // CHUNK
    nb, hm, tm, st = _gla_specs(nc, True)

    def body(q_ref, k_ref, v_ref, la_ref, sp_ref, do_ref, dq_ref, dk_ref, dv_ref, dla_ref, ds_ref):
        @pl.when(pl.program_id(0) == 0)
        def _():
            ds_ref[...] = jnp.zeros_like(ds_ref)

        ds = ds_ref[...]
        for c in reversed(range(CHUNKS_PER_STEP)):
            r = _chunk_rows(c)
            _, vjp = jax.vjp(_gla_chunk, q_ref[:, r], k_ref[:, r], _heads(v_ref, r), la_ref[:, r], sp_ref[:, c])
            dq, dk, dv, dla, ds = vjp((_heads(do_ref, r), ds))
            dq_ref[:, r] = dq.astype(dq_ref.dtype)
            dk_ref[:, r] = dk.astype(dk_ref.dtype)
            _put_heads(dv_ref, dv, r)
            dla_ref[:, r] = dla
        ds_ref[...] = ds

    hm_shape = jax.ShapeDtypeStruct((HEADS, t, GLA_DK), BF16)
    return pl.pallas_call(
        body, name="gla_bwd", grid=(nb,),
        in_specs=[hm, hm, tm, hm, st, tm], out_specs=[hm, hm, tm, hm],
        out_shape=[hm_shape, hm_shape, jax.ShapeDtypeStruct((t, HEADS * HEAD_W), BF16),
                   jax.ShapeDtypeStruct((HEADS, t, GLA_DK), F32)],
        scratch_shapes=[pltpu.VMEM((HEADS, HEAD_W, GLA_DK), F32)],
        compiler_params=_cparams("arbitrary"),
    )(q, k, v, la, sp, do)


def _ml_chunk(q, k, v, li_r, lf_r, cm, nv, m):
    c = q.shape[-2]
    row = lax.broadcasted_iota(jnp.int32, (c, c), 0)
    col = lax.broadcasted_iota(jnp.int32, (c, c), 1)
    eye = (row == col).astype(F32)
    li_c = jnp.sum(eye * li_r, axis=-1, keepdims=True)
    lf_c = jnp.sum(eye * lf_r, axis=-1, keepdims=True)
    fc_c = jnp.sum((col <= row).astype(F32) * lf_r, axis=-1, keepdims=True)
    fc_r = jnp.sum((row <= col).astype(F32) * lf_c, axis=-2, keepdims=True)
    f_last = jnp.sum(lf_r, axis=-1, keepdims=True)
    kc = k * (HEAD_W ** -0.5)
    a_c = f_last - fc_c + li_c
    m_loc = jnp.max(a_c, axis=-2, keepdims=True)
    kw = kc * jnp.exp(a_c - m_loc)
    c_chunk = _bdot(kw, v, "tn")
    n_chunk = jnp.sum(kw, axis=-2, keepdims=True)
    m_new = jnp.maximum(f_last + m, m_loc)
    sp = jnp.exp(f_last + m - m_new)
    sl = jnp.exp(m_loc - m_new)
    cm_new = sp * cm + sl * c_chunk
    nv_new = sp * nv + sl * n_chunk
    log_d = li_r - _abs(fc_c - fc_r)
    g_inter = fc_c + m
    m_t = jnp.maximum(g_inter, jnp.max(log_d, axis=-1, keepdims=True))
    s = _bdot(q, kc, "nt") * jnp.exp(log_d - m_t)
    sc = jnp.exp(g_inter - m_t)
    num = _bdot(s, v, "nn") + sc * _bdot(q, cm, "nn")
    den = jnp.sum(s, axis=-1, keepdims=True) + sc * jnp.sum(q * nv, axis=-1, keepdims=True)
    den = jnp.maximum(_abs(den), jnp.exp(-m_t))
    return num / den, cm_new, nv_new, m_new


def _ml_specs(nc, rev):
    nb = nc // CHUNKS_PER_STEP

    def blk(n):
        return (nb - 1 - n) if rev else n
    tm = pl.BlockSpec((CHUNKS_PER_STEP * CHUNK, HEADS * HEAD_W), lambda n: (blk(n), 0))
    gate = pl.BlockSpec((HEADS, CHUNKS_PER_STEP, 1, CHUNK), lambda n: (0, blk(n), 0, 0))
    cm = pl.BlockSpec((HEADS, CHUNKS_PER_STEP, HEAD_W, HEAD_W), lambda n: (0, blk(n), 0, 0))
    vec = pl.BlockSpec((HEADS, CHUNKS_PER_STEP, 1, HEAD_W), lambda n: (0, blk(n), 0, 0))
    return nb, tm, gate, cm, vec


_ML_STATE = [pltpu.VMEM((HEADS, HEAD_W, HEAD_W), F32), pltpu.VMEM((HEADS, 1, HEAD_W), F32), pltpu.VMEM((HEADS, 1, HEAD_W), F32)]


def _ml_fwd(q, k, v, li, lf):
    t = q.shape[0]
    nc = t // CHUNK
    nb, tm, gate, cm, vec = _ml_specs(nc, False)

    def body(q_ref, k_ref, v_ref, li_ref, lf_ref, hc_ref, cp_ref, np_ref, mp_ref, c_ref, n_ref, m_ref):
        @pl.when(pl.program_id(0) == 0)
        def _():
            c_ref[...] = jnp.zeros_like(c_ref)
            n_ref[...] = jnp.zeros_like(n_ref)
            m_ref[...] = jnp.zeros_like(m_ref)

        cs, ns, ms = c_ref[...], n_ref[...], m_ref[...][:, :, 0:1]
        for c in range(CHUNKS_PER_STEP):
            r = _chunk_rows(c)
            cp_ref[:, c] = cs
            np_ref[:, c] = ns
            mp_ref[:, c] = jnp.broadcast_to(ms, m_ref.shape)
            hc, cs, ns, ms = _ml_chunk(_heads(q_ref, r), _heads(k_ref, r), _heads(v_ref, r), li_ref[:, c], lf_ref[:, c],
                                       cs, ns, ms)
            _put_heads(hc_ref, hc, r)
        c_ref[...] = cs
        n_ref[...] = ns
        m_ref[...] = jnp.broadcast_to(ms, m_ref.shape)

    return pl.pallas_call(
        body, name="mlstm_fwd", grid=(nb,),
        in_specs=[tm, tm, tm, gate, gate], out_specs=[tm, cm, vec, vec],
        out_shape=[jax.ShapeDtypeStruct((t, HEADS * HEAD_W), F32), jax.ShapeDtypeStruct((HEADS, nc, HEAD_W, HEAD_W), F32),
                   jax.ShapeDtypeStruct((HEADS, nc, 1, HEAD_W), F32), jax.ShapeDtypeStruct((HEADS, nc, 1, HEAD_W), F32)],
        scratch_shapes=_ML_STATE,
        compiler_params=_cparams("arbitrary"),
    )(q, k, v, li, lf)


def _ml_bwd(q, k, v, li, lf, cp, npv, mp, dhc):
    t = q.shape[0]
    nc = t // CHUNK
    nb, tm, gate, cm, vec = _ml_specs(nc, True)

    def body(q_ref, k_ref, v_ref, li_ref, lf_ref, cp_ref, np_ref, mp_ref, dhc_ref,
             dq_ref, dk_ref, dv_ref, dli_ref, dlf_ref, dc_ref, dn_ref, dm_ref):
        @pl.when(pl.program_id(0) == 0)
        def _():
            dc_ref[...] = jnp.zeros_like(dc_ref)
            dn_ref[...] = jnp.zeros_like(dn_ref)
            dm_ref[...] = jnp.zeros_like(dm_ref)

        dc, dn, dm = dc_ref[...], dn_ref[...], dm_ref[...][:, :, 0:1]
        for c in reversed(range(CHUNKS_PER_STEP)):
            r = _chunk_rows(c)
            _, vjp = jax.vjp(_ml_chunk, _heads(q_ref, r), _heads(k_ref, r), _heads(v_ref, r), li_ref[:, c], lf_ref[:, c],
                             cp_ref[:, c], np_ref[:, c], mp_ref[:, c][:, :, 0:1])
            dq, dk, dv, dli, dlf, dc, dn, dm = vjp((_heads(dhc_ref, r), dc, dn, dm))
            _put_heads(dq_ref, dq, r)
            _put_heads(dk_ref, dk, r)
            _put_heads(dv_ref, dv, r)
            dli_ref[:, c] = dli
            dlf_ref[:, c] = dlf
        dc_ref[...] = dc
        dn_ref[...] = dn
        dm_ref[...] = jnp.broadcast_to(dm, dm_ref.shape)

    tm_shape = jax.ShapeDtypeStruct((t, HEADS * HEAD_W), F32)
    gate_shape = jax.ShapeDtypeStruct((HEADS, nc, 1, CHUNK), F32)
    return pl.pallas_call(
        body, name="mlstm_bwd", grid=(nb,),
        in_specs=[tm, tm, tm, gate, gate, cm, vec, vec, tm], out_specs=[tm, tm, tm, gate, gate],
        out_shape=[tm_shape, tm_shape, tm_shape, gate_shape, gate_shape],
        scratch_shapes=_ML_STATE,
        compiler_params=_cparams("arbitrary"),
    )(q, k, v, li, lf, cp, npv, mp, dhc)


def _ml_pre(s0, s1, s2, s3, cw0, cw1, cw2, cw3, cb, wq, wk, wv, wiq, wik, wiv, bif):
    pre = cb + cw0 * s0 + cw1 * s1 + cw2 * s2 + cw3 * s3
    xc = pre * _sigmoid(pre)
    q = _bdot(xc, wq, "nn")
    k = _bdot(xc, wk, "nn")
    v = _bdot(s3, wv, "nn")
    gates = _bdot(q, wiq, "nn") + _bdot(k, wik, "nn") + _bdot(v, wiv, "nn") + bif
    lane = lax.broadcasted_iota(jnp.int32, gates.shape, 1)
    gl = jnp.where(lane < HEADS, gates, _log_sigmoid(gates))
    return xc, q, k, v, gl


def _delayed(xs_ref, x_ref, halo_ref, r):
    xs_ref[0:HALO, :] = halo_ref[...]
    xs_ref[HALO:HALO + r, :] = x_ref[...]
    return [xs_ref[pl.ds(HALO - (CONV_K - 1) + j, r), :] for j in range(CONV_K)]


def _full_spec(shape):
    return pl.BlockSpec(shape, lambda i, nd=len(shape): (0,) * nd)


def _ml_pre_fwd(x_m, x_pad, params, tile=256):
    t, w = x_m.shape
    r = min(tile, t)

    def body(*refs):
        x_ref, halo_ref = refs[:2]
        p = [ref[...] for ref in refs[2:2 + len(params)]]
        outs = refs[2 + len(params):-1]
        res = _ml_pre(*_delayed(refs[-1], x_ref, halo_ref, r), *p)
        for ref, val in zip(outs, res):
            ref[...] = val

    row = pl.BlockSpec((r, w), lambda i: (i, 0))
    return pl.pallas_call(
        body, name="ml_pre_fwd", grid=(t // r,),
        in_specs=[row, pl.BlockSpec((HALO, w), lambda i: (i * (r // HALO), 0))] + [_full_spec(p.shape) for p in params],
        out_specs=[row] * 4 + [pl.BlockSpec((r, LANES), lambda i: (i, 0))],
        out_shape=[jax.ShapeDtypeStruct((t, w), F32)] * 4 + [jax.ShapeDtypeStruct((t, LANES), F32)],
        scratch_shapes=[pltpu.VMEM((r + HALO, w), F32)],
        compiler_params=_cparams("arbitrary"),
    )(x_m, x_pad, *params)


def _ml_pre_bwd(x_m, x_pad, params, cts, tile=256):
    t, w = x_m.shape
    r = min(tile, t)
    nt = t // r
    n_p = len(params)

    def body(*refs):
        x_ref, halo_ref = refs[:2]
        p = [ref[...] for ref in refs[2:2 + n_p]]
        ct = [ref[...] for ref in refs[2 + n_p:7 + n_p]]
        dx_ref = refs[7 + n_p]
        dp_refs = refs[8 + n_p:8 + 2 * n_p]
        xs_ref, ds_ref, carry_ref = refs[8 + 2 * n_p:]
        step = pl.program_id(0)

        @pl.when(step == 0)
        def _():
            ds_ref[...] = jnp.zeros_like(ds_ref)
            carry_ref[...] = jnp.zeros_like(carry_ref)

        _, vjp = jax.vjp(_ml_pre, *_delayed(xs_ref, x_ref, halo_ref, r), *p)
        grads = vjp(tuple(ct))
        for j in range(CONV_K):
            ds_ref[j, HALO:HALO + r, :] = grads[j]
        lead = HALO + CONV_K - 1
        d_tile = sum(ds_ref[j, pl.ds(lead - j, r), :] for j in range(CONV_K))
        d_halo = sum(ds_ref[j, pl.ds(CONV_K - 1 - j, HALO), :] for j in range(CONV_K))
        dx_ref[...] = jnp.concatenate([d_tile[:r - HALO], d_tile[r - HALO:] + carry_ref[...]], axis=0).astype(dx_ref.dtype)
        carry_ref[...] = d_halo
        _accumulate(step, dp_refs, grads[CONV_K:])

    row = pl.BlockSpec((r, w), lambda i: (nt - 1 - i, 0))
    return pl.pallas_call(
        body, name="ml_pre_bwd", grid=(nt,),
        in_specs=[row, pl.BlockSpec((HALO, w), lambda i: ((nt - 1 - i) * (r // HALO), 0))] + [_full_spec(p.shape) for p in params]
        + [row] * 4 + [pl.BlockSpec((r, LANES), lambda i: (nt - 1 - i, 0))],
        out_specs=[row] + [_full_spec(p.shape) for p in params],
        out_shape=[jax.ShapeDtypeStruct((t, w), BF16)] + [jax.ShapeDtypeStruct(p.shape, F32) for p in params],
        scratch_shapes=[pltpu.VMEM((r + HALO, w), F32), pltpu.VMEM((CONV_K, r + 2 * HALO, w), F32), pltpu.VMEM((HALO, w), F32)],
        compiler_params=_cparams("arbitrary"),
    )(x_m, x_pad, *params, *cts)


def _per_head(fn, row_vals, head_params, shared_params=()):
    return [fn(*[a[:, hs] for a in row_vals], *[p[:, hs] for p in head_params], *shared_params) for hs in _head_slices(HEAD_W)]


def _gla_out(o, g, gn):
    return _rms(o, gn) * (g * _sigmoid(g))


def _ml_out(hc, op, xc, g, sk):
    hcell = hc * _sigmoid(op)
    mu = jnp.mean(hcell, axis=-1, keepdims=True)
    d = hcell - mu
    var = jnp.mean(d * d, axis=-1, keepdims=True)
    return d * lax.rsqrt(var + EPS) * g + sk * xc


def _log_decay(al, w, b):
    return _log_sigmoid(_bdot(al, w, "nn") + b) * (1.0 / GLA_GATE_NORM)


def _merge(ga, gb, ya, yb):
    return _sigmoid(ga) * ya + _sigmoid(gb) * yb


def _post_mix(x, z, gpm, gpl):
    x1 = x + _rms(z, gpm)
    return x1, _rms(x1, gpl)


def _loss_rows(x1, dn, tgt, g):
    e = x1 + _rms(dn, g) - tgt
    return 0.5 * jnp.sum(jnp.mean(e * e, axis=-1, keepdims=True), axis=0, keepdims=True)


def _lin(p):
    return 4 * p[0] + 2 * p[1] + p[2]


def _me():
    return lax.axis_index("x"), lax.axis_index("y"), lax.axis_index("c")


def _flip(p, k):
    return tuple((1 - v) if (k >> (2 - i)) & 1 else v for i, v in enumerate(p))


ANY = pl.BlockSpec(memory_space=pl.ANY)


HBM = pl.BlockSpec(memory_space=pltpu.HBM)
SEM = pl.BlockSpec(memory_space=pltpu.SEMAPHORE)
DATAFLOW = pltpu.SideEffectType.DATAFLOW_SIDE_EFFECTING


SIBLING = 1
OTHER_CHIPS = (2, 4, 6)


def _peer_copies(kinds, srcs, lands, send_sems, recv_sems):
    me = _me()
    copies = []
    for a, (kind, src, land) in enumerate(zip(kinds, srcs, lands)):
        masks = {"gather": range(1, N_DEV), "exchange": range(1, N_DEV), "gather_chips": (SIBLING, *OTHER_CHIPS),
                 "gather_pass": OTHER_CHIPS}[kind]
        for k in masks:
            peer = _flip(me, k)
            if kind == "gather_pass":
                block = land.at[_lin(peer)]
                src_ref, dst_ref, target = block, block, _flip(me, SIBLING)
            else:
                src_ref, dst_ref, target = (src.at[_lin(peer)] if kind == "exchange" else src), land.at[_lin(me)], peer
            copies.append(pltpu.make_async_remote_copy(
                src_ref=src_ref, dst_ref=dst_ref, send_sem=send_sems.at[a * 7 + k - 1], recv_sem=recv_sems.at[a * 7 + k - 1],
                device_id=target, device_id_type=MESH))
    return copies


def _copies_start(kind, srcs, name, after=None, lands=None):
    n = len(srcs)
    extra = [] if after is None else [after]
    kind = [kind] * n if isinstance(kind, str) else list(kind)
    land_shapes = [(s.shape if k == "exchange" else (N_DEV, *s.shape)) for k, s in zip(kind, srcs)]
    lands = [lax.empty(ls, s.dtype) for ls, s in zip(land_shapes, srcs)] if lands is None else lands

    def body(*refs):
        sems = refs[2 * n + len(extra):]
        for cp in _peer_copies(kind, refs[:n], refs[n:2 * n], sems[0], sems[1]):
            cp.start()
        refs[-1][...] = jnp.zeros_like(refs[-1])

    def hbm(a):
        return pltpu.with_memory_space_constraint(a, pltpu.HBM)

    out = pl.pallas_call(
        body, name=name,
        out_shape=(pltpu.SemaphoreType.DMA((7 * n,)), pltpu.SemaphoreType.DMA((7 * n,)),
                   *[pltpu.HBM(s.shape, s.dtype) for s in srcs],
                   *[pltpu.HBM(ls, s.dtype) for ls, s in zip(land_shapes, srcs)],
                   jax.ShapeDtypeStruct((8, LANES), F32)),
        in_specs=[HBM] * (2 * n) + [ANY] * len(extra),
        out_specs=(SEM, SEM, *[HBM] * (2 * n), pl.BlockSpec(memory_space=pltpu.VMEM)),
        input_output_aliases={i: 2 + i for i in range(2 * n)},
        compiler_params=pltpu.CompilerParams(has_side_effects=DATAFLOW),
    )(*[hbm(s) for s in srcs], *[hbm(a) for a in lands], *extra)
    return (kind, n, out[:-1]), out[-1]


def _copies_wait(state, after, name):
    kind, n, (send_sems, recv_sems, *thru) = state
    after = list(after) if isinstance(after, (list, tuple)) else [after]

    def body(*refs):
        for cp in _peer_copies(kind, refs[:n], refs[n:2 * n], refs[2 * n], refs[2 * n + 1]):
            cp.wait_send()
            cp.wait_recv()

    out = pl.pallas_call(
        body, name=name,
        out_shape=tuple(pltpu.HBM(t.shape, t.dtype) for t in thru),
        in_specs=[HBM] * (2 * n) + [SEM, SEM] + [ANY] * len(after), out_specs=tuple([HBM] * (2 * n)),
        input_output_aliases={i: i for i in range(2 * n)},
        compiler_params=pltpu.CompilerParams(has_side_effects=DATAFLOW),
    )(*thru, send_sems, recv_sems, *after)
    return out[:n], out[n:]


def _adamw(w, g, m, v):
    m2 = ADAM_B1 * m + (1.0 - ADAM_B1) * g
    v2 = ADAM_B2 * v + (1.0 - ADAM_B2) * (g * g)
    m_hat = m2 / (1.0 - ADAM_B1 ** ADAM_STEP)
    v_hat = v2 / (1.0 - ADAM_B2 ** ADAM_STEP)
    delta = -ADAM_LR * (m_hat / (jnp.sqrt(v_hat) + ADAM_EPS) + ADAM_WD * w)
    return delta, m2, v2


def _sum_adamw(lands, parts, me_idx, w, m, v, name, tile=256):
    r, c = w.shape
    nchunks = len(lands)
    tr = min(tile, r // nchunks)
    per_chunk = r // nchunks // tr
    per = 1 + N_DEV

    def body(me_ref, *refs):
        w_ref, m_ref, v_ref, g_ref, d_ref, m2_ref, v2_ref = refs[nchunks * per:]
        for k in range(nchunks):
            own_ref, slots = refs[k * per], refs[k * per + 1:(k + 1) * per]

            @pl.when(pl.program_id(0) // per_chunk == k)
            def _(own_ref=own_ref, slots=slots):
                own = own_ref[...].astype(F32)
                g = None
                for s in range(N_DEV):
                    term = jnp.where(me_ref[0] == s, own, slots[s][...].astype(F32))
                    g = term if g is None else g + term
                d, m2, v2 = _adamw(w_ref[...], g, m_ref[...], v_ref[...])
                g_ref[...] = g
                d_ref[...] = d
                m2_ref[...] = m2
                v2_ref[...] = v2

    def chunk_specs(k):
        def tile_of(i):
            return jnp.clip(i - k * per_chunk, 0, per_chunk - 1)

        def slot_spec(s):
            return pl.BlockSpec((None, tr, c), lambda i, me: (jnp.where(me[0] == s, (s + 1) % N_DEV, s), tile_of(i), 0))
        return [pl.BlockSpec((None, tr, c), lambda i, me: (me[0], tile_of(i), 0))] + [slot_spec(s) for s in range(N_DEV)]

    row = pl.BlockSpec((tr, c), lambda i, me: (i, 0))
    operands = [a for land, part in zip(lands, parts) for a in (part, *[land] * N_DEV)]
    return pl.pallas_call(
        body, name=name,
        grid_spec=pltpu.PrefetchScalarGridSpec(
            num_scalar_prefetch=1, grid=(r // tr,),
            in_specs=[s for k in range(nchunks) for s in chunk_specs(k)] + [row] * 3,
            out_specs=[row] * 4),
        out_shape=[jax.ShapeDtypeStruct((r, c), F32)] * 4,
        compiler_params=_cparams("parallel"),
    )(me_idx, *operands, w, m, v)


def _small_update(name, me_idx, kinds, lands, owns, ws, ms, vs, sums=()):
    n = len(ws)
    lands, owns = list(lands) + [s[0] for s in sums], list(owns) + [s[1] for s in sums]
    kinds = list(kinds) + ["gather"] * len(sums)
    nl = len(lands)

    def summed(me, land_ref, own):
        g = None
        for s in range(N_DEV):
            term = jnp.where(me == s, own, land_ref[s])
            g = term if g is None else g + term
        return g

    def body(me_ref, *refs):
        land_refs, own_refs = refs[:nl], refs[nl:2 * nl]
        w_refs, m_refs, v_refs = (refs[2 * nl + i * n:2 * nl + (i + 1) * n] for i in range(3))
        outs = refs[2 * nl + 3 * n:]
        me = me_ref[0]
        for i in range(n):
            g = summed(me, land_refs[i], own_refs[i][...])
            d, m2, v2 = _adamw(w_refs[i][...], g, m_refs[i][...], v_refs[i][...])
            for ref, val in zip(outs[4 * i:4 * i + 4], (g, d, m2, v2)):
                ref[...] = val
        for i in range(n, nl):
            outs[4 * n + i - n][...] = summed(me, land_refs[i], own_refs[i][...])

    def whole(shape):
        return pl.BlockSpec(shape, lambda i, me, nd=len(shape): (0,) * nd)

    def own_spec(kind, own):
        if kind == "gather":
            return whole(own.shape)
        return pl.BlockSpec((None, *own.shape[1:]), lambda i, me: (me[0], 0, 0))

    shapes = [w.shape for w in ws]
    out_shapes = [s for s in shapes for _ in range(4)] + [s[1].shape for s in sums]
    return pl.pallas_call(
        body, name=name,
        grid_spec=pltpu.PrefetchScalarGridSpec(
            num_scalar_prefetch=1, grid=(1,),
            in_specs=[whole(a.shape) for a in lands] + [own_spec(k, o) for k, o in zip(kinds, owns)]
            + [whole(s) for s in shapes] * 3,
            out_specs=[whole(s) for s in out_shapes]),
        out_shape=[jax.ShapeDtypeStruct(s, F32) for s in out_shapes],
        compiler_params=_cparams("arbitrary"),
    )(me_idx, *lands, *owns, *ws, *ms, *vs)


def _small_view(n, a):
    if a.ndim == 1:
        return a.reshape(1, -1)
    if a.ndim == 3:
        return a.transpose(1, 2, 0).reshape(QKV_BLOCK * QKV_BLOCK, -1)
    return a.T if n == "w_if" else a


def _small_unview(n, a, shape):
    if len(shape) == 1:
        return a.reshape(shape)
    if len(shape) == 3:
        return a.reshape(QKV_BLOCK, QKV_BLOCK, -1).transpose(2, 0, 1)
    return a.T if n == "w_if" else a


def _small_shards(n, g):
    if n == "w_if":
        return g.reshape(N_DEV, -1, g.shape[1]).transpose(0, 2, 1)
    return g.reshape(g.shape[0], N_DEV, -1).transpose(1, 0, 2)


def _small_unshard(n, s):
    if n == "w_if":
        return s.transpose(0, 2, 1).reshape(-1, s.shape[1])
    return s.transpose(1, 0, 2).reshape(s.shape[1], -1)


def _to_hm(a, d):
    t = a.shape[0]
    return a.reshape(t, HEADS, d).transpose(1, 0, 2)


def _from_hm(a):
    h, t, d = a.shape
    return a.transpose(1, 0, 2).reshape(t, h * d)


def _gate_rows(g):
    t = g.shape[0]
    return g.T.reshape(HEADS, t // CHUNK, 1, CHUNK)


def _gate_cols(g):
    h, nc, _, c = g.shape
    return g.reshape(h, nc * c).T


def _blockdiag_dense(w):
    n = w.shape[0] * QKV_BLOCK
    tiled = jnp.tile(w.reshape(n, QKV_BLOCK), (1, n // QKV_BLOCK))
    r = lax.broadcasted_iota(jnp.int32, (n, n), 0)
    c = lax.broadcasted_iota(jnp.int32, (n, n), 1)
    return jnp.where(r // QKV_BLOCK == c // QKV_BLOCK, tiled, 0.0)


def _blockdiag_blocks(dense):
    n = dense[0].shape[0]
    k = len(dense)

    def body(*refs):
        r = lax.broadcasted_iota(jnp.int32, (n, n), 0)
        c = lax.broadcasted_iota(jnp.int32, (n, n), 1)
        fr = lax.broadcasted_iota(jnp.int32, (n, LANES), 0)
        fc = lax.broadcasted_iota(jnp.int32, (n, LANES), 1)
        fold = ((fr & (QKV_BLOCK - 1)) == fc).astype(BF16)
        for i in range(k):
            kept = jnp.where((r >> 2) == (c >> 2), refs[i][...], 0.0)
            refs[k + i][...] = sum(lax.dot_general(t, fold, _dims("nn", 2), preferred_element_type=F32) for t in _split3(kept))

    out = pl.pallas_call(body, name="blockdiag_blocks", out_shape=[jax.ShapeDtypeStruct((n, LANES), F32)] * k)(*dense)
    return [o[:, 0:QKV_BLOCK].reshape(n // QKV_BLOCK, QKV_BLOCK, QKV_BLOCK) for o in out]


def _col_blocks(w):
    k, n = w.shape
    return w.reshape(k, N_DEV, n // N_DEV).transpose(1, 0, 2)


def _from_col_blocks(g):
    d, k, n = g.shape
    return g.transpose(1, 0, 2).reshape(k, d * n)


def _local_step(x, tgt, weight, ws, prefetch, pass_on, on_grads, on_small):
    t, d = x.shape
    g1 = ws["g_pre_mix"]

    def dep(token):
        return () if token is None else (token,)

    w_in = weight("w_in", x)
    fetch_mix = prefetch(("w_pa", "w_pb", "w_o"), w_in)

    n_in = w_in.shape[2]

    offs = [0]
    for s in IN_SPLITS:
        offs.append(offs[-1] + s)

    def proj_in_fwd(xv, g, w):
        hv = _rms(xv, g)
        proj = jnp.concatenate([_raw_dot(hv, w[j], "nn") for j in range(N_DEV)], axis=1)
        parts = [proj[:, offs[i]:offs[i + 1]] for i in range(len(IN_SPLITS))]
        parts[4] = jnp.concatenate([parts[4], jnp.zeros((parts[4].shape[0], LANES - LOWRANK), F32)], axis=1)
        return (hv, *parts), ()

    widths = [LANES if s == LOWRANK else s for s in IN_SPLITS]
    h, q_a, k_a, v_a, g_a, a_low_p, x_m, o_pre, gate_a, gate_b = _rowwise(
        "proj_in", proj_in_fwd, [x], [g1, w_in], [(d, BF16)] + [(wd, F32) for wd in widths], deps=dep(fetch_mix))

    w_a_up_p = jnp.pad(ws["w_a_up"], ((0, LANES - LOWRANK), (0, 0)))
    b_a_up = ws["b_a_up"]
    (la,) = _rowwise("gla_decay", lambda al, w, b: ((_log_decay(al, w, b),), ()), [a_low_p], [w_a_up_p, b_a_up],
                     [(HEADS * GLA_DK, F32)])
    fetch_up = prefetch(("w_up",), la)
    pass_mix = pass_on("w_pa", la)
    q_hm, k_hm, la_hm = _to_hm(q_a, GLA_DK), _to_hm(k_a, GLA_DK), _to_hm(la, GLA_DK)
    o_gla, s_prev = _gla_fwd(q_hm, k_hm, v_a, la_hm, deps=dep(fetch_up) + dep(pass_mix))
    gn = ws["g_gla_norm"]
    ml_w = HEADS * HEAD_W

    def proj_a_fwd(o, g, n_, w):
        ya = jnp.concatenate(_per_head(_gla_out, [o, g], [], [n_]), axis=1)
        return (ya, _raw_dot(ya, w, "nn")), ()

    ya_in, y_a = _rowwise("proj_a", proj_a_fwd, [o_gla, g_a], [gn, weight("w_pa", o_gla)], [(ml_w, BF16), (d, F32)],
                          tile=512)

    cw = ws["conv_w"]
    w_if_p = jnp.pad(ws["w_if"], ((0, 0), (0, LANES - 2 * HEADS)))
    pre_params = [cw[0:1], cw[1:2], cw[2:3], cw[3:4], ws["conv_b"],
                  _blockdiag_dense(ws["w_q_ml"]), _blockdiag_dense(ws["w_k_ml"]), _blockdiag_dense(ws["w_v_ml"]),
                  w_if_p[0:ml_w], w_if_p[ml_w:2 * ml_w], w_if_p[2 * ml_w:3 * ml_w],
                  jnp.pad(ws["b_if"], ((0, 0), (0, LANES - 2 * HEADS)))]
    x_pad = jnp.pad(x_m, ((HALO, 0), (0, 0)))
    xc, q_m, k_m, v_m, gl = _ml_pre_fwd(x_m, x_pad, pre_params)
    li, lf = _gate_rows(gl[:, 0:HEADS]), _gate_rows(gl[:, HEADS:2 * HEADS])
    hc, c_prev, n_prev, m_prev = _ml_fwd(q_m, k_m, v_m, li, lf)
    fetch_down = prefetch(("w_down",), hc)
    g_ml, skip = ws["g_ml_norm"], ws["ml_skip"]

    def proj_b_fwd(a, b, c_, ga, gb, ya, g, s, w):
        hb = jnp.concatenate(_per_head(_ml_out, [a, b, c_], [g, s]), axis=1)
        yb = _raw_dot(hb, w, "nn")
        return (hb, yb, _merge(ga, gb, ya, yb)), ()

    h_b, y_b, merged = _rowwise("proj_b", proj_b_fwd, [hc, o_pre, xc, gate_a, gate_b, y_a], [g_ml, skip, weight("w_pb", hc)],
                                [(ml_w, BF16), (d, F32), (d, BF16)], tile=512, deps=dep(fetch_down))

    gpm, gpl, gpo = ws["g_post_mix"], ws["g_pre_mlp"], ws["g_post_mlp"]

    def proj_o_fwd(mg, xv, w, a, b):
        zv = _raw_dot(mg, w, "nn")
        return (zv, *_post_mix(xv, zv, a, b)), ()

    pass_up = pass_on("w_up", merged)
    z, x1, h2 = _rowwise("proj_o", proj_o_fwd, [merged, x], [weight("w_o", merged), gpm, gpl],
                         [(d, F32), (d, F32), (d, BF16)], tile=512, deps=dep(pass_up))
    pass_down = pass_on("w_down", h2)
    w_up = weight("w_up", h2)
    up, u = _mm(h2, w_up, "nn", (BF16, BF16), "mlp_up", tm=2048, shards="b", deps=dep(pass_down),
                epilogue=lambda p: (p, jnp.square(jnp.maximum(p, 0.0))))

    def mlp_down_loss(uv, x1v, tgtv, w, g):
        dnv = _raw_dot(uv, w, "nn")
        loss, vjp = jax.vjp(lambda a, b, c_: _loss_rows(a, b, tgtv, c_), x1v, dnv, g)
        dx1, ddn, dg = vjp(jnp.ones((1, 1), F32))
        return (dx1, ddn), (jnp.broadcast_to(loss, (1, LANES)), dg)

    dx1_y, d_dn, loss, d_gpo = _rowwise("mlp_down", mlp_down_loss, [u, x1, tgt], [weight("w_down", u), gpo],
                                        [(d, F32), (d, BF16)], [((1, LANES), F32), ((1, d), F32)], tile=512)

    (d_up,) = _mm(d_dn, weight("w_down", u), "nt", (BF16,), "mlp_down_dx", extra=[up],
                  epilogue=lambda p, a: (p * (2.0 * jnp.maximum(a.astype(F32), 0.0)),))
    dw_down = _mm(u, d_dn, "tn", BF16, "mlp_down_dw", tm=512)
    dw_up = _mm(h2, d_up, "tn", BF16, "mlp_up_dw", tn=w_up.shape[2], shards="out")
    sent_mlp = on_grads(dict(w_down=dw_down, w_up=dw_up))

    def mlp_up_dx(dup, xv, zv, dx1, w, a, b):
        _, vjp = jax.vjp(_post_mix, xv, zv, a, b)
        ns = w.shape[2]
        dh2 = sum(_raw_dot(dup[:, j * ns:(j + 1) * ns], w[j], "nt") for j in range(w.shape[0]))
        dx, dz, da, db = vjp((dx1, dh2))
        return (dx, dz), (da, db)

    dx_res, d_z, d_gpm, d_gpl = _rowwise("mlp_up_dx", mlp_up_dx, [d_up, x, z, dx1_y], [w_up, gpm, gpl],
                                         [(d, F32), (d, BF16)], [((1, d), F32), ((1, d), F32)], tile=512, deps=dep(sent_mlp))
    dw_o = _mm(merged, d_z, "tn", BF16, "proj_o_dw")

    def proj_o_dx(dz, ga, gb, ya, yb, w):
        return jax.vjp(_merge, ga, gb, ya, yb)[1](_raw_dot(dz, w, "nt")), ()

    d_ga, d_gb, d_ya, d_yb = _rowwise("proj_o_dx", proj_o_dx, [d_z, gate_a, gate_b, y_a, y_b], [weight("w_o", merged)],
                                      [(d, BF16)] * 4, tile=512)
    dw_pa = _mm(ya_in, d_ya, "tn", BF16, "proj_a_dw")
    dw_pb = _mm(h_b, d_yb, "tn", BF16, "proj_b_dw")
    sent_mix = on_grads(dict(w_o=dw_o, w_pa=dw_pa, w_pb=dw_pb))

    def proj_b_dx(dyb, a, b, c_, w, g, s):
        ct = _raw_dot(dyb, w, "nt")
        parts = []
        for hs in _head_slices(HEAD_W):
            _, vjp = jax.vjp(_ml_out, a[:, hs], b[:, hs], c_[:, hs], g[:, hs], s[:, hs])
            parts.append(vjp(ct[:, hs]))
        cat = lambda i: jnp.concatenate([p[i] for p in parts], axis=1)
        return (cat(0), cat(1), cat(2)), (cat(3), cat(4))

    d_hc, d_opre, d_xc, d_gml, d_skip = _rowwise("proj_b_dx", proj_b_dx, [d_yb, hc, o_pre, xc],
                                                 [weight("w_pb", hc), g_ml, skip], [(ml_w, F32), (ml_w, BF16), (ml_w, F32)],
                                                 [((1, ml_w), F32)] * 2,
                                                 tile=512, deps=dep(sent_mix))
    d_qm, d_km, d_vm, d_li, d_lf = _ml_bwd(q_m, k_m, v_m, li, lf, c_prev, n_prev, m_prev, d_hc)
    d_gl = jnp.concatenate([_gate_cols(d_li), _gate_cols(d_lf), jnp.zeros((t, LANES - 2 * HEADS), F32)], axis=1)
    pre_grads = _ml_pre_bwd(x_m, x_pad, pre_params, [d_xc, d_qm, d_km, d_vm, d_gl])
    d_xm = pre_grads[0]
    d_cw = jnp.concatenate(pre_grads[1:5], axis=0)
    d_cb = pre_grads[5]
    d_wq, d_wk, d_wv = _blockdiag_blocks(pre_grads[6:9])
    d_wif = jnp.concatenate(pre_grads[9:12], axis=0)[:, 0:2 * HEADS]
    d_bif = pre_grads[12][:, 0:2 * HEADS]

    def proj_a_dx(dya, o, g, w, n_):
        ct = _raw_dot(dya, w, "nt")
        parts = []
        for hs in _head_slices(HEAD_W):
            _, vjp = jax.vjp(_gla_out, o[:, hs], g[:, hs], n_)
            parts.append(vjp(ct[:, hs]))
        cat = lambda i: jnp.concatenate([p[i] for p in parts], axis=1)
        return (cat(0), cat(1)), (sum(p[2] for p in parts),)

    d_o, d_g_a, d_gn = _rowwise("proj_a_dx", proj_a_dx, [d_ya, o_gla, g_a], [weight("w_pa", o_gla), gn],
                                [(ml_w, F32), (ml_w, BF16)],
                                [((1, HEAD_W), F32)], tile=512)
    dq_hm, dk_hm, d_va, dla_hm = _gla_bwd(q_hm, k_hm, v_a, la_hm, s_prev, d_o)

    def decay_bwd(al, ct, w, b):
        _, vjp = jax.vjp(_log_decay, al, w, b)
        dal, dw, db = vjp(ct)
        return (dal,), (dw, db)

    d_alow_p, d_wa_p, d_ba = _rowwise("gla_decay_bwd", decay_bwd, [a_low_p, _from_hm(dla_hm)], [w_a_up_p, b_a_up],
                                      [(LANES, BF16)], [(w_a_up_p.shape, F32), (b_a_up.shape, F32)])
    d_proj = jnp.concatenate([_from_hm(dq_hm), _from_hm(dk_hm), d_va, d_g_a, d_alow_p[:, 0:LOWRANK], d_xm, d_opre, d_ga, d_gb],
                             axis=1).astype(BF16)
    small = dict(w_a_up=d_wa_p[0:LOWRANK], b_a_up=d_ba, g_gla_norm=d_gn, conv_w=d_cw, conv_b=d_cb,
                 w_q_ml=d_wq, w_k_ml=d_wk, w_v_ml=d_wv, w_if=d_wif, b_if=d_bif, ml_skip=d_skip, g_ml_norm=d_gml,
                 g_post_mix=d_gpm, g_pre_mlp=d_gpl, g_post_mlp=d_gpo)
    sent_small = on_small(small, loss)
    sent_in = sent_small
    for half in range(2):
        dw_half = _mm_shard_cols(h, d_proj, n_in, "proj_in_dw_%d" % half, half, d // 2, deps=dep(sent_in))
        sent_in = on_grads({"w_in#%d" % half: dw_half})

    def proj_in_dx(dp, xv, dres, w, g):
        _, vjp = jax.vjp(_rms, xv, g)
        dx, dg = vjp(sum(_raw_dot(dp[:, j * n_in:(j + 1) * n_in], w[j], "nt") for j in range(N_DEV)))
        return (dx + dres,), (dg,)

    grad_x, d_g1 = _rowwise("proj_in_dx", proj_in_dx, [d_proj, x, dx_res], [w_in, g1], [(d, F32)], [((1, d), F32)],
                            deps=dep(sent_in))
    return grad_x, on_small(dict(g_pre_mix=d_g1), None)


BIG = ("w_in", "w_pa", "w_pb", "w_o", "w_up", "w_down")
BIG_COL_SHARDED = ("w_in", "w_pa", "w_pb", "w_up")
SMALL_SHARDED = ("w_a_up", "conv_w", "w_if")
SMALL = ("g_pre_mix", "w_a_up", "b_a_up", "g_gla_norm", "conv_w", "conv_b", "w_q_ml", "w_k_ml", "w_v_ml", "w_if", "b_if",
         "ml_skip", "g_ml_norm", "g_post_mix", "g_pre_mlp", "g_post_mlp")
WEIGHTS = ("g_pre_mix", "w_in", "w_a_up", "b_a_up", "g_gla_norm", "conv_w", "conv_b", "w_q_ml", "w_k_ml", "w_v_ml", "w_if", "b_if",
           "ml_skip", "g_ml_norm", "w_pa", "w_pb", "w_o", "g_post_mix", "g_pre_mlp", "w_up", "w_down", "g_post_mlp")


def kernel(x, g_pre_mix, w_in, w_a_up, b_a_up, g_gla_norm, conv_w, conv_b, w_q_ml, w_k_ml, w_v_ml, w_if, b_if, ml_skip, g_ml_norm, w_pa, w_pb, w_o, g_post_mix, g_pre_mlp, w_up, w_down, g_post_mlp, loss_target, m_g_pre_mix, m_w_in, m_w_a_up, m_b_a_up, m_g_gla_norm, m_conv_w, m_conv_b, m_w_q_ml, m_w_k_ml, m_w_v_ml, m_w_if, m_b_if, m_ml_skip, m_g_ml_norm, m_w_pa, m_w_pb, m_w_o, m_g_post_mix, m_g_pre_mlp, m_w_up, m_w_down, m_g_post_mlp, v_g_pre_mix, v_w_in, v_w_a_up, v_b_a_up, v_g_gla_norm, v_conv_w, v_conv_b, v_w_q_ml, v_w_k_ml, v_w_v_ml, v_w_if, v_b_if, v_ml_skip, v_g_ml_norm, v_w_pa, v_w_pb, v_w_o, v_g_post_mix, v_g_pre_mlp, v_w_up, v_w_down, v_g_post_mlp):
    args = dict(locals())
    w = {n: args[n][0] for n in WEIGHTS}
    m = {n: args["m_" + n][0] for n in WEIGHTS}
    v = {n: args["v_" + n][0] for n in WEIGHTS}

    me_lin = _lin(_me())
    me_idx = jnp.reshape(me_lin, (1,)).astype(jnp.int32)

    def full_weight(n, g):
        if n in ("w_in", "w_up"):
            return g
        return _from_col_blocks(g) if n in BIG_COL_SHARDED else g.reshape(-1, g.shape[-1])

    def grad_parts(n, g):
        if n.partition("#")[0] in ("w_in", "w_up"):
            return g
        return (_col_blocks(g) if n in BIG_COL_SHARDED else g.reshape(N_DEV, -1, g.shape[-1])).astype(BF16)

    sharded_names = tuple(SMALL_SHARDED)
    small_w_state, small_w_token = _copies_start("gather", [_small_view(n, w[n]) for n in sharded_names],
                                                 "allgather_start_small_weights")
    narrow = {n: w[n].astype(BF16) for n in BIG}
    ready, pending = {}, {}

    def prefetch(group, after):
        state, token = _copies_start("gather_chips", [narrow[n] for n in group], "allgather_start_" + group[0], after)
        for n in group:
            pending[n] = (group, state)
        return token

    prefetch(("w_in",), small_w_token)

    passing = {}

    def pass_on(n, after):
        group, state = pending[n]
        shards, lands = _copies_wait(state, after, "allgather_wait_" + group[0])
        state, token = _copies_start("gather_pass", shards, "allgather_pass_" + group[0], lands=lands)
        for gn in group:
            passing[gn] = (group, state)
        return token

    def weight(n, after):
        if n not in ready:
            group, state = passing[n]
            shards, lands = _copies_wait(state, after, "allgather_passed_" + group[0])
            for gn, shard, land in zip(group, shards, lands):
                ready[gn] = full_weight(gn, lax.dynamic_update_slice(land, shard[None], (me_lin, 0, 0)))
        return ready[n]

    small_w_own, small_w_lands = _copies_wait(small_w_state, [narrow[n] for n in BIG if n != "w_in"], "allgather_wait_small_weights")
    ws = {n: (w[n].reshape(1, -1) if w[n].ndim == 1 else w[n]) for n in SMALL if n not in SMALL_SHARDED}
    for n, own, land in zip(sharded_names, small_w_own, small_w_lands):
        ws[n] = _small_unshard(n, lax.dynamic_update_slice(land, own[None], (me_lin, 0, 0)))
    pass_on("w_in", [ws[n] for n in sharded_names])

    sent = []

    def on_grads(grads):
        names = tuple(grads)
        state, token = _copies_start("exchange", [grad_parts(n, grads[n]) for n in names],
                                     "exchange_start_" + names[0].replace("#", "_"))
        sent.append((names, state))
        return token

    small_sent = []

    def on_small(small, loss):
        names = tuple(small)
        kinds = ["exchange" if n in SMALL_SHARDED else "gather" for n in names]
        srcs = [_small_shards(n, small[n]) if n in SMALL_SHARDED else _small_view(n, small[n]) for n in names]
        extra = [] if loss is None else [loss]
        state, token = _copies_start(kinds + ["gather"] * len(extra), srcs + extra, "allgather_start_small_" + names[0])
        small_sent.append((names, kinds, state))
        return token

    grad_x, last_token = _local_step(x[0], loss_target[0], weight, ws, prefetch, pass_on, on_grads, on_small)

    out = {}

    chunks = {}

    def finish(names, state, after):
        parts, lands = _copies_wait(state, after, "exchange_wait_" + names[0].replace("#", "_"))
        for name, part, land in zip(names, parts, lands):
            n, _, chunk = name.partition("#")
            chunks.setdefault(n, []).append((land, part))
            if chunk in ("", "1"):
                got_lands, got_parts = zip(*chunks[n])
                out[n] = _sum_adamw(got_lands, got_parts, me_idx, w[n], m[n], v[n], "adamw_" + n)

    def finish_small(names, kinds, state, after):
        own, lands = _copies_wait(state, after, "allgather_wait_small_" + names[0])
        k = len(names)
        upd = _small_update("adamw_small_" + names[0], me_idx, kinds, lands[:k], own[:k],
                            *[[_small_view(n, d[n]) for n in names] for d in (w, m, v)], sums=list(zip(lands[k:], own[k:])))
        for i, n in enumerate(names):
            out[n] = tuple(_small_unview(n, a, w[n].shape) for a in upd[4 * i:4 * i + 4])
        return upd[4 * k:]

    (loss_sum,) = finish_small(*small_sent[0], [grad_x, last_token])
    for names, state in sent[:-2]:
        finish(names, state, [grad_x, last_token])
    finish(*sent[-2], [loss_sum] + [out[n][1] for n in BIG if n in out])
    finish(*sent[-1], [loss_sum])
    finish_small(*small_sent[1], [out["w_in"][1]])

    shaped = lambda a, n: a.reshape(args[n].shape)
    return (loss_sum[0, 0], grad_x[None],
            *[shaped(out[n][0], n) for n in WEIGHTS], *[shaped(out[n][1], n) for n in WEIGHTS],
            *[shaped(out[n][2], n) for n in WEIGHTS], *[shaped(out[n][3], n) for n in WEIGHTS])
```

```python
import functools

import jax
import jax.numpy as jnp
from jax import lax
from jax.experimental import pallas as pl
from jax.experimental.pallas import tpu as pltpu

F32 = jnp.float32
BF16 = jnp.bfloat16
MESH = pl.DeviceIdType.MESH

N_DEV = 8
EPS = 1e-6
CHUNK = 64
CHUNKS_PER_STEP = 4
HEADS = 4
GLA_DK = 64
HEAD_W = 128
GLA_GATE_NORM = 16.0
LOWRANK = 16
CONV_K = 4
QKV_BLOCK = 4
LANES = 128
HALO = 8
IN_SPLITS = (256, 256, 512, 512, 16, 512, 512, 1024, 1024)

ADAM_LR = 0.001
ADAM_B1 = 0.9
ADAM_B2 = 0.999
ADAM_EPS = 1e-08
ADAM_WD = 0.01
ADAM_STEP = 10

VMEM_LIMIT = 56 * 1024 * 1024


def _cparams(*sem):
    return pltpu.CompilerParams(dimension_semantics=sem, vmem_limit_bytes=VMEM_LIMIT)


def _dims(mode, ndim):
    contract = {"nn": ((ndim - 1,), (ndim - 2,)), "nt": ((ndim - 1,), (ndim - 1,)), "tn": ((ndim - 2,), (ndim - 2,))}[mode]
    return contract, (((0,), (0,)) if ndim == 3 else ((), ()))


def _raw_dot(a, b, mode):
    return lax.dot_general(a.astype(BF16), b.astype(BF16), _dims(mode, a.ndim), preferred_element_type=F32)


@functools.partial(jax.custom_vjp, nondiff_argnums=(2,))
def _bdot(a, b, mode):
    return _raw_dot(a, b, mode)


def _bdot_fwd(a, b, mode):
    return _raw_dot(a, b, mode), (a, b)


def _bdot_bwd(mode, res, ct):
    a, b = res
    if mode == "nn":
        da, db = _raw_dot(ct, b, "nt"), _raw_dot(a, ct, "tn")
    elif mode == "nt":
        da, db = _raw_dot(ct, b, "nn"), _raw_dot(ct, a, "tn")
    else:
        da, db = _raw_dot(b, ct, "nt"), _raw_dot(a, ct, "nn")
    return da.astype(a.dtype), db.astype(b.dtype)


_bdot.defvjp(_bdot_fwd, _bdot_bwd)


def _split3(x):
    hi = x.astype(BF16)
    r1 = x - hi.astype(F32)
    mid = r1.astype(BF16)
    return hi, mid, (r1 - mid.astype(F32)).astype(BF16)


def _split_dot(tri, x):
    if x.ndim == 3:
        tri = jnp.broadcast_to(tri, (x.shape[0], *tri.shape))
    return sum(lax.dot_general(tri, t, _dims("nn", x.ndim), preferred_element_type=F32) for t in _split3(x))


def _tri(n, lower):
    r = lax.broadcasted_iota(jnp.int32, (n, n), 0)
    c = lax.broadcasted_iota(jnp.int32, (n, n), 1)
    return ((c <= r) if lower else (c >= r)).astype(BF16)


@jax.custom_vjp
def _cumsum_rows(x):
    return _split_dot(_tri(x.shape[-2], True), x)


def _cumsum_rows_fwd(x):
    return _cumsum_rows(x), None


def _cumsum_rows_bwd(_, ct):
    return (_split_dot(_tri(ct.shape[-2], False), ct),)


_cumsum_rows.defvjp(_cumsum_rows_fwd, _cumsum_rows_bwd)


def _abs(x):
    return jnp.where(x >= 0, x, -x)


def _sigmoid(x):
    return lax.logistic(x)


def _log_sigmoid(x):
    return jnp.minimum(x, 0.0) - jnp.log(1.0 + jnp.exp(-_abs(x)))


def _rms(x, g):
    return x * lax.rsqrt(jnp.mean(x * x, axis=-1, keepdims=True) + EPS) * g


def _head_slices(w):
    return [slice(h * w, (h + 1) * w) for h in range(HEADS)]


def _heads(ref, rows=slice(None)):
    return jnp.stack([ref[rows, hs] for hs in _head_slices(HEAD_W)])


def _put_heads(ref, val, rows=slice(None)):
    for h, hs in enumerate(_head_slices(HEAD_W)):
        ref[rows, hs] = val[h].astype(ref.dtype)


def _tile(dim, want):
    if dim <= want or dim % LANES:
        return dim
    t = want
    while dim % t:
        t -= LANES
    return t


def _mm(a, b, mode, out_dtype, name, tm=1024, tn=1024, tk=4096, epilogue=None, extra=(), deps=(), shards=None):
    if shards == "b":
        assert mode == "nn"
        ns = b.shape[2]
        (m, k), (k2, n) = a.shape, (b.shape[1], b.shape[0] * ns)
        tn = ns
    elif mode == "nn":
        (m, k), (k2, n) = a.shape, b.shape
    elif mode == "nt":
        (m, k), (n, k2) = a.shape, b.shape
    else:
        (k, m), (k2, n) = a.shape, b.shape
    assert k == k2, (name, a.shape, b.shape)
    tm, tn, tk = _tile(m, tm), _tile(n, tn), _tile(k, tk)
    nk = k // tk
    out_dtypes = out_dtype if epilogue else (out_dtype,)
    assert nk == 1 or (out_dtype == F32 and not epilogue), name
    n_in = 2 + len(extra)

    def body(*refs):
        p = _raw_dot(refs[0][...], refs[1][...], mode)
        if nk > 1:
            _accumulate(pl.program_id(2), [refs[n_in + len(deps)]], [p])
            return
        outs = epilogue(p, *[r[...] for r in refs[2:n_in]]) if epilogue else (p,)
        for ref, val in zip(refs[n_in + len(deps):], outs):
            ref[...] = val.astype(ref.dtype)

    a_spec = pl.BlockSpec((tk, tm), lambda i, j, kk: (kk, i)) if mode == "tn" else pl.BlockSpec((tm, tk), lambda i, j, kk: (i, kk))
    if shards == "b":
        b_spec = pl.BlockSpec((None, tk, tn), lambda i, j, kk: (j, kk, 0))
    elif mode == "nt":
        b_spec = pl.BlockSpec((tn, tk), lambda i, j, kk: (j, kk))
    else:
        b_spec = pl.BlockSpec((tk, tn), lambda i, j, kk: (kk, j))
    o_spec = pl.BlockSpec((tm, tn), lambda i, j, kk: (i, j))
    if shards == "out":
        out_specs, out_shape = [pl.BlockSpec((None, tm, tn), lambda i, j, kk: (j, i, 0))], (n // tn, m, tn)
    else:
        out_specs, out_shape = [o_spec] * len(out_dtypes), (m, n)
    res = pl.pallas_call(
        body, name=name, grid=(m // tm, n // tn, nk),
        in_specs=[a_spec, b_spec] + [o_spec] * len(extra) + [ANY] * len(deps), out_specs=out_specs,
        out_shape=[jax.ShapeDtypeStruct(out_shape, dt) for dt in out_dtypes],
        compiler_params=_cparams("parallel", "parallel", "arbitrary"),
    )(a, b, *extra, *deps)
    return res if epilogue else res[0]


def _mm_shard_cols(a, b, n, name, row_tile, tm, deps=()):
    t = a.shape[0]
    nb = b.shape[1] // n

    def body(a_ref, b_ref, *rest):
        o_ref, at_ref = rest[len(deps):]
        j = pl.program_id(0)

        @pl.when(j == 0)
        def _():
            at_ref[...] = a_ref[...].astype(BF16).T

        for s in range(nb):
            @pl.when(j == s)
            def _(s=s):
                o_ref[...] = _raw_dot(at_ref[...], b_ref[:, s * n:(s + 1) * n], "nn").astype(BF16)

    return pl.pallas_call(
        body, name=name, grid=(nb,),
        in_specs=[pl.BlockSpec((t, tm), lambda j: (0, row_tile)),
                  pl.BlockSpec((t, nb * n), lambda j: (0, 0), pipeline_mode=pl.Buffered(1))] + [ANY] * len(deps),
        out_specs=pl.BlockSpec((None, tm, n), lambda j: (j, 0, 0)),
        out_shape=jax.ShapeDtypeStruct((nb, tm, n), BF16),
        scratch_shapes=[pltpu.VMEM((tm, t), BF16)],
        compiler_params=_cparams("arbitrary"),
    )(a, b, *deps)


def _rowwise(name, fn, rows, params, out_rows, out_accs=(), tile=256, deps=()):
    t = rows[0].shape[0]
    r = min(tile, t)
    assert t % r == 0
    n_in, n_or = len(rows) + len(params), len(out_rows)
    n_all = n_in + len(deps)
    params = list(params) + list(deps)

    def body(*refs):
        vals = [ref[...] for ref in refs[:n_in]]
        outs = refs[n_all:]
        ro, ao = fn(*vals)
        for ref, v in zip(outs[:n_or], ro):
            ref[...] = v.astype(ref.dtype)
        if out_accs:
            _accumulate(pl.program_id(0), outs[n_or:], ao)

    def full(shape):
        return pl.BlockSpec(shape, lambda i, nd=len(shape): (0,) * nd)

    return pl.pallas_call(
        body, name=name, grid=(t // r,),
        in_specs=[pl.BlockSpec((r, a.shape[1]), lambda i: (i, 0)) for a in rows] + [full(p.shape) for p in params],
        out_specs=[pl.BlockSpec((r, w), lambda i: (i, 0)) for w, _ in out_rows] + [full(s) for s, _ in out_accs],
        out_shape=[jax.ShapeDtypeStruct((t, w), dt) for w, dt in out_rows] + [jax.ShapeDtypeStruct(s, dt) for s, dt in out_accs],
        compiler_params=_cparams("arbitrary"),
    )(*rows, *params)


def _accumulate(step, refs, vals):
    for ref, v in zip(refs, vals):
        @pl.when(step == 0)
        def _(ref=ref, v=v):
            ref[...] = v.astype(ref.dtype)

        @pl.when(step > 0)
        def _(ref=ref, v=v):
            ref[...] += v.astype(ref.dtype)


def _gla_chunk(q, k, v, la, st):
    c = q.shape[-2]
    row = lax.broadcasted_iota(jnp.int32, (c, c), 0)
    col = lax.broadcasted_iota(jnp.int32, (c, c), 1)
    cum = _cumsum_rows(la)
    cl = jnp.sum(la, axis=-2, keepdims=True)
    ep = jnp.exp(cum)
    en = jnp.exp(-cum)
    qs = q * (GLA_DK ** -0.5)
    qp = qs * ep
    a_f = _bdot(qp, k * en, "nt")
    a_b = _bdot(qs * en, k * ep, "nt")
    sc = jnp.where(row >= col, a_f, a_b)
    o = _bdot(sc, v, "nn") + _bdot(qp, st, "nt")
    kd = k * jnp.exp(cl - cum)
    st_new = st * jnp.exp(cl) + _bdot(v, kd, "tn")
    return o, st_new


def _gla_specs(nc, rev):
    nb = nc // CHUNKS_PER_STEP
    rows = CHUNKS_PER_STEP * CHUNK

    def blk(n):
        return (nb - 1 - n) if rev else n
    hm = pl.BlockSpec((HEADS, rows, GLA_DK), lambda n: (0, blk(n), 0))
    tm = pl.BlockSpec((rows, HEADS * HEAD_W), lambda n: (blk(n), 0))
    st = pl.BlockSpec((HEADS, CHUNKS_PER_STEP, HEAD_W, GLA_DK), lambda n: (0, blk(n), 0, 0))
    return nb, hm, tm, st


def _chunk_rows(c):
    return slice(c * CHUNK, (c + 1) * CHUNK)


def _gla_fwd(q, k, v, la, deps=()):
    t = v.shape[0]
    nc = t // CHUNK
    nb, hm, tm, st = _gla_specs(nc, False)

    def body(q_ref, k_ref, v_ref, la_ref, *rest):
        o_ref, sp_ref, st_ref = rest[len(deps):]

        @pl.when(pl.program_id(0) == 0)
        def _():
            st_ref[...] = jnp.zeros_like(st_ref)

        s = st_ref[...]
        for c in range(CHUNKS_PER_STEP):
            r = _chunk_rows(c)
            sp_ref[:, c] = s
            o, s = _gla_chunk(q_ref[:, r], k_ref[:, r], _heads(v_ref, r), la_ref[:, r], s)
            _put_heads(o_ref, o, r)
        st_ref[...] = s

    return pl.pallas_call(
        body, name="gla_fwd", grid=(nb,),
        in_specs=[hm, hm, tm, hm] + [ANY] * len(deps), out_specs=[tm, st],
        out_shape=[jax.ShapeDtypeStruct((t, HEADS * HEAD_W), F32), jax.ShapeDtypeStruct((HEADS, nc, HEAD_W, GLA_DK), F32)],
        scratch_shapes=[pltpu.VMEM((HEADS, HEAD_W, GLA_DK), F32)],
        compiler_params=_cparams("arbitrary"),
    )(q, k, v, la, *deps)


def _gla_bwd(q, k, v, la, sp, do):
    t = v.shape[0]
    nc = t // CHUNK
    nb, hm, tm, st = _gla_specs(nc, True)

    def body(q_ref, k_ref, v_ref, la_ref, sp_ref, do_ref, dq_ref, dk_ref, dv_ref, dla_ref, ds_ref):
        @pl.when(pl.program_id(0) == 0)
        def _():
            ds_ref[...] = jnp.zeros_like(ds_ref)

        ds = ds_ref[...]
        for c in reversed(range(CHUNKS_PER_STEP)):
            r = _chunk_rows(c)
            _, vjp = jax.vjp(_gla_chunk, q_ref[:, r], k_ref[:, r], _heads(v_ref, r), la_ref[:, r], sp_ref[:, c])
            dq, dk, dv, dla, ds = vjp((_heads(do_ref, r), ds))
            dq_ref[:, r] = dq.astype(dq_ref.dtype)
            dk_ref[:, r] = dk.astype(dk_ref.dtype)
            _put_heads(dv_ref, dv, r)
            dla_ref[:, r] = dla
        ds_ref[...] = ds

    hm_shape = jax.ShapeDtypeStruct((HEADS, t, GLA_DK), BF16)
    return pl.pallas_call(
        body, name="gla_bwd", grid=(nb,),
        in_specs=[hm, hm, tm, hm, st, tm], out_specs=[hm, hm, tm, hm],
        out_shape=[hm_shape, hm_shape, jax.ShapeDtypeStruct((t, HEADS * HEAD_W), BF16),
                   jax.ShapeDtypeStruct((HEADS, t, GLA_DK), F32)],
        scratch_shapes=[pltpu.VMEM((HEADS, HEAD_W, GLA_DK), F32)],
        compiler_params=_cparams("arbitrary"),
    )(q, k, v, la, sp, do)


def _ml_chunk(q, k, v, li_r, lf_r, cm, nv, m):
    c = q.shape[-2]
    row = lax.broadcasted_iota(jnp.int32, (c, c), 0)
    col = lax.broadcasted_iota(jnp.int32, (c, c), 1)
    eye = (row == col).astype(F32)
    li_c = jnp.sum(eye * li_r, axis=-1, keepdims=True)
    lf_c = jnp.sum(eye * lf_r, axis=-1, keepdims=True)
    fc_c = jnp.sum((col <= row).astype(F32) * lf_r, axis=-1, keepdims=True)
    fc_r = jnp.sum((row <= col).astype(F32) * lf_c, axis=-2, keepdims=True)
    f_last = jnp.sum(lf_r, axis=-1, keepdims=True)
    kc = k * (HEAD_W ** -0.5)
    a_c = f_last - fc_c + li_c
    m_loc = jnp.max(a_c, axis=-2, keepdims=True)
    kw = kc * jnp.exp(a_c - m_loc)
    c_chunk = _bdot(kw, v, "tn")
    n_chunk = jnp.sum(kw, axis=-2, keepdims=True)
    m_new = jnp.maximum(f_last + m, m_loc)
    sp = jnp.exp(f_last + m - m_new)
    sl = jnp.exp(m_loc - m_new)
    cm_new = sp * cm + sl * c_chunk
    nv_new = sp * nv + sl * n_chunk
    log_d = li_r - _abs(fc_c - fc_r)
    g_inter = fc_c + m
    m_t = jnp.maximum(g_inter, jnp.max(log_d, axis=-1, keepdims=True))
    s = _bdot(q, kc, "nt") * jnp.exp(log_d - m_t)
    sc = jnp.exp(g_inter - m_t)
    num = _bdot(s, v, "nn") + sc * _bdot(q, cm, "nn")
    den = jnp.sum(s, axis=-1, keepdims=True) + sc * jnp.sum(q * nv, axis=-1, keepdims=True)
    den = jnp.maximum(_abs(den), jnp.exp(-m_t))
    return num / den, cm_new, nv_new, m_new


def _ml_specs(nc, rev):
    nb = nc // CHUNKS_PER_STEP

    def blk(n):
        return (nb - 1 - n) if rev else n
    tm = pl.BlockSpec((CHUNKS_PER_STEP * CHUNK, HEADS * HEAD_W), lambda n: (blk(n), 0))
    gate = pl.BlockSpec((HEADS, CHUNKS_PER_STEP, 1, CHUNK), lambda n: (0, blk(n), 0, 0))
    cm = pl.BlockSpec((HEADS, CHUNKS_PER_STEP, HEAD_W, HEAD_W), lambda n: (0, blk(n), 0, 0))
    vec = pl.BlockSpec((HEADS, CHUNKS_PER_STEP, 1, HEAD_W), lambda n: (0, blk(n), 0, 0))
    return nb, tm, gate, cm, vec


_ML_STATE = [pltpu.VMEM((HEADS, HEAD_W, HEAD_W), F32), pltpu.VMEM((HEADS, 1, HEAD_W), F32), pltpu.VMEM((HEADS, 1, HEAD_W), F32)]


def _ml_fwd(q, k, v, li, lf):
    t = q.shape[0]
    nc = t // CHUNK
    nb, tm, gate, cm, vec = _ml_specs(nc, False)

    def body(q_ref, k_ref, v_ref, li_ref, lf_ref, hc_ref, cp_ref, np_ref, mp_ref, c_ref, n_ref, m_ref):
        @pl.when(pl.program_id(0) == 0)
        def _():
            c_ref[...] = jnp.zeros_like(c_ref)
            n_ref[...] = jnp.zeros_like(n_ref)
            m_ref[...] = jnp.zeros_like(m_ref)

        cs, ns, ms = c_ref[...], n_ref[...], m_ref[...][:, :, 0:1]
        for c in range(CHUNKS_PER_STEP):
            r = _chunk_rows(c)
            cp_ref[:, c] = cs
            np_ref[:, c] = ns
            mp_ref[:, c] = jnp.broadcast_to(ms, m_ref.shape)
            hc, cs, ns, ms = _ml_chunk(_heads(q_ref, r), _heads(k_ref, r), _heads(v_ref, r), li_ref[:, c], lf_ref[:, c],
                                       cs, ns, ms)
            _put_heads(hc_ref, hc, r)
        c_ref[...] = cs
        n_ref[...] = ns
        m_ref[...] = jnp.broadcast_to(ms, m_ref.shape)

    return pl.pallas_call(
        body, name="mlstm_fwd", grid=(nb,),
        in_specs=[tm, tm, tm, gate, gate], out_specs=[tm, cm, vec, vec],
        out_shape=[jax.ShapeDtypeStruct((t, HEADS * HEAD_W), F32), jax.ShapeDtypeStruct((HEADS, nc, HEAD_W, HEAD_W), F32),
                   jax.ShapeDtypeStruct((HEADS, nc, 1, HEAD_W), F32), jax.ShapeDtypeStruct((HEADS, nc, 1, HEAD_W), F32)],
        scratch_shapes=_ML_STATE,
        compiler_params=_cparams("arbitrary"),
    )(q, k, v, li, lf)


def _ml_bwd(q, k, v, li, lf, cp, npv, mp, dhc):
    t = q.shape[0]
    nc = t // CHUNK
    nb, tm, gate, cm, vec = _ml_specs(nc, True)

    def body(q_ref, k_ref, v_ref, li_ref, lf_ref, cp_ref, np_ref, mp_ref, dhc_ref,
             dq_ref, dk_ref, dv_ref, dli_ref, dlf_ref, dc_ref, dn_ref, dm_ref):
        @pl.when(pl.program_id(0) == 0)
        def _():
            dc_ref[...] = jnp.zeros_like(dc_ref)
            dn_ref[...] = jnp.zeros_like(dn_ref)
            dm_ref[...] = jnp.zeros_like(dm_ref)

        dc, dn, dm = dc_ref[...], dn_ref[...], dm_ref[...][:, :, 0:1]
        for c in reversed(range(CHUNKS_PER_STEP)):
            r = _chunk_rows(c)
            _, vjp = jax.vjp(_ml_chunk, _heads(q_ref, r), _heads(k_ref, r), _heads(v_ref, r), li_ref[:, c], lf_ref[:, c],
                             cp_ref[:, c], np_ref[:, c], mp_ref[:, c][:, :, 0:1])
            dq, dk, dv, dli, dlf, dc, dn, dm = vjp((_heads(dhc_ref, r), dc, dn, dm))
            _put_heads(dq_ref, dq, r)
            _put_heads(dk_ref, dk, r)
            _put_heads(dv_ref, dv, r)
            dli_ref[:, c] = dli
            dlf_ref[:, c] = dlf
        dc_ref[...] = dc
        dn_ref[...] = dn
        dm_ref[...] = jnp.broadcast_to(dm, dm_ref.shape)

    tm_shape = jax.ShapeDtypeStruct((t, HEADS * HEAD_W), F32)
    gate_shape = jax.ShapeDtypeStruct((HEADS, nc, 1, CHUNK), F32)
    return pl.pallas_call(
        body, name="mlstm_bwd", grid=(nb,),
        in_specs=[tm, tm, tm, gate, gate, cm, vec, vec, tm], out_specs=[tm, tm, tm, gate, gate],
        out_shape=[tm_shape, tm_shape, tm_shape, gate_shape, gate_shape],
        scratch_shapes=_ML_STATE,
        compiler_params=_cparams("arbitrary"),
    )(q, k, v, li, lf, cp, npv, mp, dhc)


def _ml_pre(s0, s1, s2, s3, cw0, cw1, cw2, cw3, cb, wq, wk, wv, wiq, wik, wiv, bif):
    pre = cb + cw0 * s0 + cw1 * s1 + cw2 * s2 + cw3 * s3
    xc = pre * _sigmoid(pre)
    q = _bdot(xc, wq, "nn")
    k = _bdot(xc, wk, "nn")
    v = _bdot(s3, wv, "nn")
    gates = _bdot(q, wiq, "nn") + _bdot(k, wik, "nn") + _bdot(v, wiv, "nn") + bif
    lane = lax.broadcasted_iota(jnp.int32, gates.shape, 1)
    gl = jnp.where(lane < HEADS, gates, _log_sigmoid(gates))
    return xc, q, k, v, gl


def _delayed(xs_ref, x_ref, halo_ref, r):
    xs_ref[0:HALO, :] = halo_ref[...]
    xs_ref[HALO:HALO + r, :] = x_ref[...]
    return [xs_ref[pl.ds(HALO - (CONV_K - 1) + j, r), :] for j in range(CONV_K)]


def _full_spec(shape):
    return pl.BlockSpec(shape, lambda i, nd=len(shape): (0,) * nd)


def _ml_pre_fwd(x_m, x_pad, params, tile=256, deps=()):
    t, w = x_m.shape
    r = min(tile, t)

    def body(*refs):
        x_ref, halo_ref = refs[:2]
        p = [ref[...] for ref in refs[2:2 + len(params)]]
        outs = refs[2 + len(params) + len(deps):-1]
        res = _ml_pre(*_delayed(refs[-1], x_ref, halo_ref, r), *p)
        for ref, val in zip(outs, res):
            ref[...] = val

    row = pl.BlockSpec((r, w), lambda i: (i, 0))
    return pl.pallas_call(
        body, name="ml_pre_fwd", grid=(t // r,),
        in_specs=[row, pl.BlockSpec((HALO, w), lambda i: (i * (r // HALO), 0))] + [_full_spec(p.shape) for p in params]
        + [ANY] * len(deps),
        out_specs=[row] * 4 + [pl.BlockSpec((r, LANES), lambda i: (i, 0))],
        out_shape=[jax.ShapeDtypeStruct((t, w), F32)] * 4 + [jax.ShapeDtypeStruct((t, LANES), F32)],
        scratch_shapes=[pltpu.VMEM((r + HALO, w), F32)],
        compiler_params=_cparams("arbitrary"),
    )(x_m, x_pad, *params, *deps)


def _ml_pre_bwd(x_m, x_pad, params, cts, tile=256):
    t, w = x_m.shape
    r = min(tile, t)
    nt = t // r
    n_p = len(params)

    def body(*refs):
        x_ref, halo_ref = refs[:2]
        p = [ref[...] for ref in refs[2:2 + n_p]]
        ct = [ref[...] for ref in refs[2 + n_p:7 + n_p]]
        dx_ref = refs[7 + n_p]
        dp_refs = refs[8 + n_p:8 + 2 * n_p]
        xs_ref, ds_ref, carry_ref = refs[8 + 2 * n_p:]
        step = pl.program_id(0)

        @pl.when(step == 0)
        def _():
            ds_ref[...] = jnp.zeros_like(ds_ref)
            carry_ref[...] = jnp.zeros_like(carry_ref)

        _, vjp = jax.vjp(_ml_pre, *_delayed(xs_ref, x_ref, halo_ref, r), *p)
        grads = vjp(tuple(ct))
        for j in range(CONV_K):
            ds_ref[j, HALO:HALO + r, :] = grads[j]
        lead = HALO + CONV_K - 1
        d_tile = sum(ds_ref[j, pl.ds(lead - j, r), :] for j in range(CONV_K))
        d_halo = sum(ds_ref[j, pl.ds(CONV_K - 1 - j, HALO), :] for j in range(CONV_K))
        dx_ref[...] = jnp.concatenate([d_tile[:r - HALO], d_tile[r - HALO:] + carry_ref[...]], axis=0).astype(dx_ref.dtype)
        carry_ref[...] = d_halo
        _accumulate(step, dp_refs, grads[CONV_K:])

    row = pl.BlockSpec((r, w), lambda i: (nt - 1 - i, 0))
    return pl.pallas_call(
        body, name="ml_pre_bwd", grid=(nt,),
        in_specs=[row, pl.BlockSpec((HALO, w), lambda i: ((nt - 1 - i) * (r // HALO), 0))] + [_full_spec(p.shape) for p in params]
        + [row] * 4 + [pl.BlockSpec((r, LANES), lambda i: (nt - 1 - i, 0))],
        out_specs=[row] + [_full_spec(p.shape) for p in params],
        out_shape=[jax.ShapeDtypeStruct((t, w), BF16)] + [jax.ShapeDtypeStruct(p.shape, F32) for p in params],
        scratch_shapes=[pltpu.VMEM((r + HALO, w), F32), pltpu.VMEM((CONV_K, r + 2 * HALO, w), F32), pltpu.VMEM((HALO, w), F32)],
        compiler_params=_cparams("arbitrary"),
    )(x_m, x_pad, *params, *cts)


def _per_head(fn, row_vals, head_params, shared_params=()):
    return [fn(*[a[:, hs] for a in row_vals], *[p[:, hs] for p in head_params], *shared_params) for hs in _head_slices(HEAD_W)]


def _gla_out(o, g, gn):
    return _rms(o, gn) * (g * _sigmoid(g))


def _ml_out(hc, op, xc, g, sk):
    hcell = hc * _sigmoid(op)
    mu = jnp.mean(hcell, axis=-1, keepdims=True)
    d = hcell - mu
    var = jnp.mean(d * d, axis=-1, keepdims=True)
    return d * lax.rsqrt(var + EPS) * g + sk * xc


def _log_decay(al, w, b):
    return _log_sigmoid(_bdot(al, w, "nn") + b) * (1.0 / GLA_GATE_NORM)


def _merge(ga, gb, ya, yb):
    return _sigmoid(ga) * ya + _sigmoid(gb) * yb


def _post_mix(x, z, gpm, gpl):
    x1 = x + _rms(z, gpm)
    return x1, _rms(x1, gpl)


def _loss_rows(x1, dn, tgt, g):
    e = x1 + _rms(dn, g) - tgt
    return 0.5 * jnp.sum(jnp.mean(e * e, axis=-1, keepdims=True), axis=0, keepdims=True)


def _lin(p):
    return 4 * p[0] + 2 * p[1] + p[2]


def _me():
    return lax.axis_index("x"), lax.axis_index("y"), lax.axis_index("c")


def _flip(p, k):
    return tuple((1 - v) if (k >> (2 - i)) & 1 else v for i, v in enumerate(p))


ANY = pl.BlockSpec(memory_space=pl.ANY)


HBM = pl.BlockSpec(memory_space=pltpu.HBM)
SEM = pl.BlockSpec(memory_space=pltpu.SEMAPHORE)
DATAFLOW = pltpu.SideEffectType.DATAFLOW_SIDE_EFFECTING


SIBLING = 1
OTHER_CHIPS = (2, 4, 6)


def _peer_copies(kinds, srcs, lands, send_sems, recv_sems):
    me = _me()
    copies = []
    for a, (kind, src, land) in enumerate(zip(kinds, srcs, lands)):
        masks = {"gather": range(1, N_DEV), "exchange": range(1, N_DEV), "gather_chips": (SIBLING, *OTHER_CHIPS),
                 "gather_pass": OTHER_CHIPS}[kind]
        for k in masks:
            peer = _flip(me, k)
            if kind == "gather_pass":
                block = land.at[_lin(peer)]
                src_ref, dst_ref, target = block, block, _flip(me, SIBLING)
            else:
                src_ref, dst_ref, target = (src.at[_lin(peer)] if kind == "exchange" else src), land.at[_lin(me)], peer
            copies.append(pltpu.make_async_remote_copy(
                src_ref=src_ref, dst_ref=dst_ref, send_sem=send_sems.at[a * 7 + k - 1], recv_sem=recv_sems.at[a * 7 + k - 1],
                device_id=target, device_id_type=MESH))
    return copies


def _copies_start(kind, srcs, name, after=None, lands=None):
    n = len(srcs)
    extra = [] if after is None else [after]
    kind = [kind] * n if isinstance(kind, str) else list(kind)
    land_shapes = [(s.shape if k == "exchange" else (N_DEV, *s.shape)) for k, s in zip(kind, srcs)]
    lands = [lax.empty(ls, s.dtype) for ls, s in zip(land_shapes, srcs)] if lands is None else lands

    def body(*refs):
        sems = refs[2 * n + len(extra):]
        for cp in _peer_copies(kind, refs[:n], refs[n:2 * n], sems[0], sems[1]):
            cp.start()
        refs[-1][...] = jnp.zeros_like(refs[-1])

    def hbm(a):
        return pltpu.with_memory_space_constraint(a, pltpu.HBM)

    out = pl.pallas_call(
        body, name=name,
        out_shape=(pltpu.SemaphoreType.DMA((7 * n,)), pltpu.SemaphoreType.DMA((7 * n,)),
                   *[pltpu.HBM(s.shape, s.dtype) for s in srcs],
                   *[pltpu.HBM(ls, s.dtype) for ls, s in zip(land_shapes, srcs)],
                   jax.ShapeDtypeStruct((8, LANES), F32)),
        in_specs=[HBM] * (2 * n) + [ANY] * len(extra),
        out_specs=(SEM, SEM, *[HBM] * (2 * n), pl.BlockSpec(memory_space=pltpu.VMEM)),
        input_output_aliases={i: 2 + i for i in range(2 * n)},
        compiler_params=pltpu.CompilerParams(has_side_effects=DATAFLOW),
    )(*[hbm(s) for s in srcs], *[hbm(a) for a in lands], *extra)
    return (kind, n, out[:-1]), out[-1]


def _copies_wait(state, after, name):
    kind, n, (send_sems, recv_sems, *thru) = state
    after = list(after) if isinstance(after, (list, tuple)) else [after]

    def body(*refs):
        for cp in _peer_copies(kind, refs[:n], refs[n:2 * n], refs[2 * n], refs[2 * n + 1]):
            cp.wait_send()
            cp.wait_recv()

    out = pl.pallas_call(
        body, name=name,
        out_shape=tuple(pltpu.HBM(t.shape, t.dtype) for t in thru),
        in_specs=[HBM] * (2 * n) + [SEM, SEM] + [ANY] * len(after), out_specs=tuple([HBM] * (2 * n)),
        input_output_aliases={i: i for i in range(2 * n)},
        compiler_params=pltpu.CompilerParams(has_side_effects=DATAFLOW),
    )(*thru, send_sems, recv_sems, *after)
    return out[:n], out[n:]


def _adamw(w, g, m, v):
    m2 = ADAM_B1 * m + (1.0 - ADAM_B1) * g
    v2 = ADAM_B2 * v + (1.0 - ADAM_B2) * (g * g)
    m_hat = m2 / (1.0 - ADAM_B1 ** ADAM_STEP)
    v_hat = v2 / (1.0 - ADAM_B2 ** ADAM_STEP)
    delta = -ADAM_LR * (m_hat / (jnp.sqrt(v_hat) + ADAM_EPS) + ADAM_WD * w)
    return delta, m2, v2


def _sum_adamw(lands, parts, me_idx, w, m, v, name, tile=256):
    r, c = w.shape
    nchunks = len(lands)
    tr = min(tile, r // nchunks)
    per_chunk = r // nchunks // tr
    per = 1 + N_DEV

    def body(me_ref, *refs):
        w_ref, m_ref, v_ref, g_ref, d_ref, m2_ref, v2_ref = refs[nchunks * per:]
        for k in range(nchunks):
            own_ref, slots = refs[k * per], refs[k * per + 1:(k + 1) * per]

            @pl.when(pl.program_id(0) // per_chunk == k)
            def _(own_ref=own_ref, slots=slots):
                own = own_ref[...].astype(F32)
                g = None
                for s in range(N_DEV):
                    term = jnp.where(me_ref[0] == s, own, slots[s][...].astype(F32))
                    g = term if g is None else g + term
                d, m2, v2 = _adamw(w_ref[...], g, m_ref[...], v_ref[...])
                g_ref[...] = g
                d_ref[...] = d
                m2_ref[...] = m2
                v2_ref[...] = v2

    def chunk_specs(k):
        def tile_of(i):
            return jnp.clip(i - k * per_chunk, 0, per_chunk - 1)

        def slot_spec(s):
            return pl.BlockSpec((None, tr, c), lambda i, me: (jnp.where(me[0] == s, (s + 1) % N_DEV, s), tile_of(i), 0))
        return [pl.BlockSpec((None, tr, c), lambda i, me: (me[0], tile_of(i), 0))] + [slot_spec(s) for s in range(N_DEV)]

    row = pl.BlockSpec((tr, c), lambda i, me: (i, 0))
    operands = [a for land, part in zip(lands, parts) for a in (part, *[land] * N_DEV)]
    return pl.pallas_call(
        body, name=name,
        grid_spec=pltpu.PrefetchScalarGridSpec(
            num_scalar_prefetch=1, grid=(r // tr,),
            in_specs=[s for k in range(nchunks) for s in chunk_specs(k)] + [row] * 3,
            out_specs=[row] * 4),
        out_shape=[jax.ShapeDtypeStruct((r, c), F32)] * 4,
        compiler_params=_cparams("parallel"),
    )(me_idx, *operands, w, m, v)


def _small_update(name, me_idx, kinds, lands, owns, ws, ms, vs, sums=()):
    n = len(ws)
    lands, owns = list(lands) + [s[0] for s in sums], list(owns) + [s[1] for s in sums]
    kinds = list(kinds) + ["gather"] * len(sums)
    nl = len(lands)

    def summed(me, land_ref, own):
        g = None
        for s in range(N_DEV):
            term = jnp.where(me == s, own, land_ref[s])
            g = term if g is None else g + term
        return g

    def body(me_ref, *refs):
        land_refs, own_refs = refs[:nl], refs[nl:2 * nl]
        w_refs, m_refs, v_refs = (refs[2 * nl + i * n:2 * nl + (i + 1) * n] for i in range(3))
        outs = refs[2 * nl + 3 * n:]
        me = me_ref[0]
        for i in range(n):
            g = summed(me, land_refs[i], own_refs[i][...])
            d, m2, v2 = _adamw(w_refs[i][...], g, m_refs[i][...], v_refs[i][...])
            for ref, val in zip(outs[4 * i:4 * i + 4], (g, d, m2, v2)):
                ref[...] = val
        for i in range(n, nl):
            outs[4 * n + i - n][...] = summed(me, land_refs[i], own_refs[i][...])

    def whole(shape):
        return pl.BlockSpec(shape, lambda i, me, nd=len(shape): (0,) * nd)

    def own_spec(kind, own):
        if kind == "gather":
            return whole(own.shape)
        return pl.BlockSpec((None, *own.shape[1:]), lambda i, me: (me[0], 0, 0))

    shapes = [w.shape for w in ws]
    out_shapes = [s for s in shapes for _ in range(4)] + [s[1].shape for s in sums]
    return pl.pallas_call(
        body, name=name,
        grid_spec=pltpu.PrefetchScalarGridSpec(
            num_scalar_prefetch=1, grid=(1,),
            in_specs=[whole(a.shape) for a in lands] + [own_spec(k, o) for k, o in zip(kinds, owns)]
            + [whole(s) for s in shapes] * 3,
            out_specs=[whole(s) for s in out_shapes]),
        out_shape=[jax.ShapeDtypeStruct(s, F32) for s in out_shapes],
        compiler_params=_cparams("arbitrary"),
    )(me_idx, *lands, *owns, *ws, *ms, *vs)


def _small_view(n, a):
    if a.ndim == 1:
        return a.reshape(1, -1)
    if a.ndim == 3:
        return a.transpose(1, 2, 0).reshape(QKV_BLOCK * QKV_BLOCK, -1)
    return a.T if n == "w_if" else a


def _small_unview(n, a, shape):
    if len(shape) == 1:
        return a.reshape(shape)
    if len(shape) == 3:
        return a.reshape(QKV_BLOCK, QKV_BLOCK, -1).transpose(2, 0, 1)
    return a.T if n == "w_if" else a


def _small_shards(n, g):
    if n == "w_if":
        return g.reshape(N_DEV, -1, g.shape[1]).transpose(0, 2, 1)
    return g.reshape(g.shape[0], N_DEV, -1).transpose(1, 0, 2)


def _small_unshard(n, s):
    if n == "w_if":
        return s.transpose(0, 2, 1).reshape(-1, s.shape[1])
    return s.transpose(1, 0, 2).reshape(s.shape[1], -1)


def _to_hm(a, d):
    t = a.shape[0]
    return a.reshape(t, HEADS, d).transpose(1, 0, 2)


def _from_hm(a):
    h, t, d = a.shape
    return a.transpose(1, 0, 2).reshape(t, h * d)


def _gate_rows(g):
    t = g.shape[0]
    return g.T.reshape(HEADS, t // CHUNK, 1, CHUNK)


def _gate_cols(g):
    h, nc, _, c = g.shape
    return g.reshape(h, nc * c).T


def _blockdiag_dense(w):
    n = w.shape[0] * QKV_BLOCK
    tiled = jnp.tile(w.reshape(n, QKV_BLOCK), (1, n // QKV_BLOCK))
    r = lax.broadcasted_iota(jnp.int32, (n, n), 0)
    c = lax.broadcasted_iota(jnp.int32, (n, n), 1)
    return jnp.where(r // QKV_BLOCK == c // QKV_BLOCK, tiled, 0.0)


def _blockdiag_blocks(dense):
    n = dense[0].shape[0]
    k = len(dense)

    def body(*refs):
        r = lax.broadcasted_iota(jnp.int32, (n, n), 0)
        c = lax.broadcasted_iota(jnp.int32, (n, n), 1)
        fr = lax.broadcasted_iota(jnp.int32, (n, LANES), 0)
        fc = lax.broadcasted_iota(jnp.int32, (n, LANES), 1)
        fold = ((fr & (QKV_BLOCK - 1)) == fc).astype(BF16)
        for i in range(k):
            kept = jnp.where((r >> 2) == (c >> 2), refs[i][...], 0.0)
            refs[k + i][...] = sum(lax.dot_general(t, fold, _dims("nn", 2), preferred_element_type=F32) for t in _split3(kept))

    out = pl.pallas_call(body, name="blockdiag_blocks", out_shape=[jax.ShapeDtypeStruct((n, LANES), F32)] * k)(*dense)
    return [o[:, 0:QKV_BLOCK].reshape(n // QKV_BLOCK, QKV_BLOCK, QKV_BLOCK) for o in out]


def _col_blocks(w):
    k, n = w.shape
    return w.reshape(k, N_DEV, n // N_DEV).transpose(1, 0, 2)


def _from_col_blocks(g):
    d, k, n = g.shape
    return g.transpose(1, 0, 2).reshape(k, d * n)


def _local_step(x, tgt, weight, ws, prefetch, pass_on, on_grads, on_small):
    t, d = x.shape
    g1 = ws["g_pre_mix"]

    def dep(token):
        return () if token is None else (token,)

    w_in = weight("w_in", x)
    fetch_mix = prefetch(("w_pa", "w_pb", "w_o"), w_in)

    n_in = w_in.shape[2]

    offs = [0]
    for s in IN_SPLITS:
        offs.append(offs[-1] + s)

    def proj_in_fwd(xv, g, w):
        hv = _rms(xv, g)
        proj = jnp.concatenate([_raw_dot(hv, w[j], "nn") for j in range(N_DEV)], axis=1)
        parts = [proj[:, offs[i]:offs[i + 1]] for i in range(len(IN_SPLITS))]
        parts[4] = jnp.concatenate([parts[4], jnp.zeros((parts[4].shape[0], LANES - LOWRANK), F32)], axis=1)
        return (hv, *parts), ()

    widths = [LANES if s == LOWRANK else s for s in IN_SPLITS]
    h, q_a, k_a, v_a, g_a, a_low_p, x_m, o_pre, gate_a, gate_b = _rowwise(
        "proj_in", proj_in_fwd, [x], [g1, w_in], [(d, BF16)] + [(wd, F32) for wd in widths], deps=dep(fetch_mix))

    w_a_up_p = jnp.pad(ws["w_a_up"], ((0, LANES - LOWRANK), (0, 0)))
    b_a_up = ws["b_a_up"]
    (la,) = _rowwise("gla_decay", lambda al, w, b: ((_log_decay(al, w, b),), ()), [a_low_p], [w_a_up_p, b_a_up],
                     [(HEADS * GLA_DK, F32)])
    fetch_up = prefetch(("w_up",), la)
    q_hm, k_hm, la_hm = _to_hm(q_a, GLA_DK), _to_hm(k_a, GLA_DK), _to_hm(la, GLA_DK)
    o_gla, s_prev = _gla_fwd(q_hm, k_hm, v_a, la_hm, deps=dep(fetch_up))
    pass_mix = pass_on("w_pa", o_gla)
    gn = ws["g_gla_norm"]
    ml_w = HEADS * HEAD_W

    cw = ws["conv_w"]
    w_if_p = jnp.pad(ws["w_if"], ((0, 0), (0, LANES - 2 * HEADS)))
    pre_params = [cw[0:1], cw[1:2], cw[2:3], cw[3:4], ws["conv_b"],
                  _blockdiag_dense(ws["w_q_ml"]), _blockdiag_dense(ws["w_k_ml"]), _blockdiag_dense(ws["w_v_ml"]),
                  w_if_p[0:ml_w], w_if_p[ml_w:2 * ml_w], w_if_p[2 * ml_w:3 * ml_w],
                  jnp.pad(ws["b_if"], ((0, 0), (0, LANES - 2 * HEADS)))]
    x_pad = jnp.pad(x_m, ((HALO, 0), (0, 0)))
    xc, q_m, k_m, v_m, gl = _ml_pre_fwd(x_m, x_pad, pre_params, deps=dep(pass_mix))
    li, lf = _gate_rows(gl[:, 0:HEADS]), _gate_rows(gl[:, HEADS:2 * HEADS])
    hc, c_prev, n_prev, m_prev = _ml_fwd(q_m, k_m, v_m, li, lf)
    fetch_down = prefetch(("w_down",), hc)
    g_ml, skip = ws["g_ml_norm"], ws["ml_skip"]

    def proj_a_fwd(o, g, n_, w):
        ya = jnp.concatenate(_per_head(_gla_out, [o, g], [], [n_]), axis=1)
        return (ya, _raw_dot(ya, w, "nn")), ()

    ya_in, y_a = _rowwise("proj_a", proj_a_fwd, [o_gla, g_a], [gn, weight("w_pa", hc)], [(ml_w, BF16), (d, F32)],
                          tile=512)

    def proj_b_fwd(a, b, c_, ga, gb, ya, g, s, w):
        hb = jnp.concatenate(_per_head(_ml_out, [a, b, c_], [g, s]), axis=1)
        yb = _raw_dot(hb, w, "nn")
        return (hb, yb, _merge(ga, gb, ya, yb)), ()

    h_b, y_b, merged = _rowwise("proj_b", proj_b_fwd, [hc, o_pre, xc, gate_a, gate_b, y_a], [g_ml, skip, weight("w_pb", hc)],
                                [(ml_w, BF16), (d, F32), (d, BF16)], tile=512, deps=dep(fetch_down))

    gpm, gpl, gpo = ws["g_post_mix"], ws["g_pre_mlp"], ws["g_post_mlp"]

    def proj_o_fwd(mg, xv, w, a, b):
        zv = _raw_dot(mg, w, "nn")
        return (zv, *_post_mix(xv, zv, a, b)), ()

    pass_up = pass_on("w_up", merged)
    z, x1, h2 = _rowwise("proj_o", proj_o_fwd, [merged, x], [weight("w_o", merged), gpm, gpl],
                         [(d, F32), (d, F32), (d, BF16)], tile=512, deps=dep(pass_up))
    pass_down = pass_on("w_down", h2)
    w_up = weight("w_up", h2)
    up, u = _mm(h2, w_up, "nn", (BF16, BF16), "mlp_up", tm=2048, shards="b", deps=dep(pass_down),
                epilogue=lambda p: (p, jnp.square(jnp.maximum(p, 0.0))))

    def mlp_down_loss(uv, x1v, tgtv, w, g):
        dnv = _raw_dot(uv, w, "nn")
        loss, vjp = jax.vjp(lambda a, b, c_: _loss_rows(a, b, tgtv, c_), x1v, dnv, g)
        dx1, ddn, dg = vjp(jnp.ones((1, 1), F32))
        return (dx1, ddn), (jnp.broadcast_to(loss, (1, LANES)), dg)

    dx1_y, d_dn, loss, d_gpo = _rowwise("mlp_down", mlp_down_loss, [u, x1, tgt], [weight("w_down", u), gpo],
                                        [(d, F32), (d, BF16)], [((1, LANES), F32), ((1, d), F32)], tile=512)

    (d_up,) = _mm(d_dn, weight("w_down", u), "nt", (BF16,), "mlp_down_dx", extra=[up],
                  epilogue=lambda p, a: (p * (2.0 * jnp.maximum(a.astype(F32), 0.0)),))
    dw_down = _mm(u, d_dn, "tn", BF16, "mlp_down_dw", tm=512)
    dw_up = _mm(h2, d_up, "tn", BF16, "mlp_up_dw", tn=w_up.shape[2], shards="out")
    sent_mlp = on_grads(dict(w_down=dw_down, w_up=dw_up))

    def mlp_up_dx(dup, xv, zv, dx1, w, a, b):
        _, vjp = jax.vjp(_post_mix, xv, zv, a, b)
        ns = w.shape[2]
        dh2 = sum(_raw_dot(dup[:, j * ns:(j + 1) * ns], w[j], "nt") for j in range(w.shape[0]))
        dx, dz, da, db = vjp((dx1, dh2))
        return (dx, dz), (da, db)

    dx_res, d_z, d_gpm, d_gpl = _rowwise("mlp_up_dx", mlp_up_dx, [d_up, x, z, dx1_y], [w_up, gpm, gpl],
                                         [(d, F32), (d, BF16)], [((1, d), F32), ((1, d), F32)], tile=512, deps=dep(sent_mlp))
    dw_o = _mm(merged, d_z, "tn", BF16, "proj_o_dw")

    def proj_o_dx(dz, ga, gb, ya, yb, w):
        return jax.vjp(_merge, ga, gb, ya, yb)[1](_raw_dot(dz, w, "nt")), ()

    d_ga, d_gb, d_ya, d_yb = _rowwise("proj_o_dx", proj_o_dx, [d_z, gate_a, gate_b, y_a, y_b], [weight("w_o", merged)],
                                      [(d, BF16)] * 4, tile=512)
    dw_pa = _mm(ya_in, d_ya, "tn", BF16, "proj_a_dw")
    dw_pb = _mm(h_b, d_yb, "tn", BF16, "proj_b_dw")
    sent_mix = on_grads(dict(w_o=dw_o, w_pa=dw_pa, w_pb=dw_pb))

    def proj_b_dx(dyb, a, b, c_, w, g, s):
        ct = _raw_dot(dyb, w, "nt")
        parts = []
        for hs in _head_slices(HEAD_W):
            _, vjp = jax.vjp(_ml_out, a[:, hs], b[:, hs], c_[:, hs], g[:, hs], s[:, hs])
            parts.append(vjp(ct[:, hs]))
        cat = lambda i: jnp.concatenate([p[i] for p in parts], axis=1)
        return (cat(0), cat(1), cat(2)), (cat(3), cat(4))

    d_hc, d_opre, d_xc, d_gml, d_skip = _rowwise("proj_b_dx", proj_b_dx, [d_yb, hc, o_pre, xc],
                                                 [weight("w_pb", hc), g_ml, skip], [(ml_w, F32), (ml_w, BF16), (ml_w, F32)],
                                                 [((1, ml_w), F32)] * 2,
                                                 tile=512, deps=dep(sent_mix))
    d_qm, d_km, d_vm, d_li, d_lf = _ml_bwd(q_m, k_m, v_m, li, lf, c_prev, n_prev, m_prev, d_hc)
    d_gl = jnp.concatenate([_gate_cols(d_li), _gate_cols(d_lf), jnp.zeros((t, LANES - 2 * HEADS), F32)], axis=1)
    pre_grads = _ml_pre_bwd(x_m, x_pad, pre_params, [d_xc, d_qm, d_km, d_vm, d_gl])
    d_xm = pre_grads[0]
    d_cw = jnp.concatenate(pre_grads[1:5], axis=0)
    d_cb = pre_grads[5]
    d_wq, d_wk, d_wv = _blockdiag_blocks(pre_grads[6:9])
    d_wif = jnp.concatenate(pre_grads[9:12], axis=0)[:, 0:2 * HEADS]
    d_bif = pre_grads[12][:, 0:2 * HEADS]

    def proj_a_dx(dya, o, g, w, n_):
        ct = _raw_dot(dya, w, "nt")
        parts = []
        for hs in _head_slices(HEAD_W):
            _, vjp = jax.vjp(_gla_out, o[:, hs], g[:, hs], n_)
            parts.append(vjp(ct[:, hs]))
        cat = lambda i: jnp.concatenate([p[i] for p in parts], axis=1)
        return (cat(0), cat(1)), (sum(p[2] for p in parts),)

    d_o, d_g_a, d_gn = _rowwise("proj_a_dx", proj_a_dx, [d_ya, o_gla, g_a], [weight("w_pa", o_gla), gn],
                                [(ml_w, F32), (ml_w, BF16)],
                                [((1, HEAD_W), F32)], tile=512)
    dq_hm, dk_hm, d_va, dla_hm = _gla_bwd(q_hm, k_hm, v_a, la_hm, s_prev, d_o)

    def decay_bwd(al, ct, w, b):
        _, vjp = jax.vjp(_log_decay, al, w, b)
        dal, dw, db = vjp(ct)
        return (dal,), (dw, db)

    d_alow_p, d_wa_p, d_ba = _rowwise("gla_decay_bwd", decay_bwd, [a_low_p, _from_hm(dla_hm)], [w_a_up_p, b_a_up],
                                      [(LANES, BF16)], [(w_a_up_p.shape, F32), (b_a_up.shape, F32)])
    d_proj = jnp.concatenate([_from_hm(dq_hm), _from_hm(dk_hm), d_va, d_g_a, d_alow_p[:, 0:LOWRANK], d_xm, d_opre, d_ga, d_gb],
                             axis=1).astype(BF16)
    small = dict(w_a_up=d_wa_p[0:LOWRANK], b_a_up=d_ba, g_gla_norm=d_gn, conv_w=d_cw, conv_b=d_cb,
                 w_q_ml=d_wq, w_k_ml=d_wk, w_v_ml=d_wv, w_if=d_wif, b_if=d_bif, ml_skip=d_skip, g_ml_norm=d_gml,
                 g_post_mix=d_gpm, g_pre_mlp=d_gpl, g_post_mlp=d_gpo)
    sent_small = on_small(small, loss)
    sent_in = sent_small
    for half in range(2):
        dw_half = _mm_shard_cols(h, d_proj, n_in, "proj_in_dw_%d" % half, half, d // 2, deps=dep(sent_in))
        sent_in = on_grads({"w_in#%d" % half: dw_half})

    def proj_in_dx(dp, xv, dres, w, g):
        _, vjp = jax.vjp(_rms, xv, g)
        dx, dg = vjp(sum(_raw_dot(dp[:, j * n_in:(j + 1) * n_in], w[j], "nt") for j in range(N_DEV)))
        return (dx + dres,), (dg,)

    grad_x, d_g1 = _rowwise("proj_in_dx", proj_in_dx, [d_proj, x, dx_res], [w_in, g1], [(d, F32)], [((1, d), F32)],
                            deps=dep(sent_in))
    return grad_x, on_small(dict(g_pre_mix=d_g1), None)


BIG = ("w_in", "w_pa", "w_pb", "w_o", "w_up", "w_down")
BIG_COL_SHARDED = ("w_in", "w_pa", "w_pb", "w_up")
SMALL_SHARDED = ("w_a_up", "conv_w", "w_if")
SMALL = ("g_pre_mix", "w_a_up", "b_a_up", "g_gla_norm", "conv_w", "conv_b", "w_q_ml", "w_k_ml", "w_v_ml", "w_if", "b_if",
         "ml_skip", "g_ml_norm", "g_post_mix", "g_pre_mlp", "g_post_mlp")
WEIGHTS = ("g_pre_mix", "w_in", "w_a_up", "b_a_up", "g_gla_norm", "conv_w", "conv_b", "w_q_ml", "w_k_ml", "w_v_ml", "w_if", "b_if",
           "ml_skip", "g_ml_norm", "w_pa", "w_pb", "w_o", "g_post_mix", "g_pre_mlp", "w_up", "w_down", "g_post_mlp")


def kernel(x, g_pre_mix, w_in, w_a_up, b_a_up, g_gla_norm, conv_w, conv_b, w_q_ml, w_k_ml, w_v_ml, w_if, b_if, ml_skip, g_ml_norm, w_pa, w_pb, w_o, g_post_mix, g_pre_mlp, w_up, w_down, g_post_mlp, loss_target, m_g_pre_mix, m_w_in, m_w_a_up, m_b_a_up, m_g_gla_norm, m_conv_w, m_conv_b, m_w_q_ml, m_w_k_ml, m_w_v_ml, m_w_if, m_b_if, m_ml_skip, m_g_ml_norm, m_w_pa, m_w_pb, m_w_o, m_g_post_mix, m_g_pre_mlp, m_w_up, m_w_down, m_g_post_mlp, v_g_pre_mix, v_w_in, v_w_a_up, v_b_a_up, v_g_gla_norm, v_conv_w, v_conv_b, v_w_q_ml, v_w_k_ml, v_w_v_ml, v_w_if, v_b_if, v_ml_skip, v_g_ml_norm, v_w_pa, v_w_pb, v_w_o, v_g_post_mix, v_g_pre_mlp, v_w_up, v_w_down, v_g_post_mlp):
    args = dict(locals())
    w = {n: args[n][0] for n in WEIGHTS}
    m = {n: args["m_" + n][0] for n in WEIGHTS}
    v = {n: args["v_" + n][0] for n in WEIGHTS}

    me_lin = _lin(_me())
    me_idx = jnp.reshape(me_lin, (1,)).astype(jnp.int32)

    def full_weight(n, g):
        if n in ("w_in", "w_up"):
            return g
        return _from_col_blocks(g) if n in BIG_COL_SHARDED else g.reshape(-1, g.shape[-1])

    def grad_parts(n, g):
        if n.partition("#")[0] in ("w_in", "w_up"):
            return g
        return (_col_blocks(g) if n in BIG_COL_SHARDED else g.reshape(N_DEV, -1, g.shape[-1])).astype(BF16)

    sharded_names = tuple(SMALL_SHARDED)
    small_w_state, small_w_token = _copies_start("gather", [_small_view(n, w[n]) for n in sharded_names],
                                                 "allgather_start_small_weights")
    narrow = {n: w[n].astype(BF16) for n in BIG}
    ready, pending = {}, {}

    def prefetch(group, after):
        state, token = _copies_start("gather_chips", [narrow[n] for n in group], "allgather_start_" + group[0], after)
        for n in group:
            pending[n] = (group, state)
        return token

    prefetch(("w_in",), small_w_token)

    passing = {}

    def pass_on(n, after):
        group, state = pending[n]
        shards, lands = _copies_wait(state, after, "allgather_wait_" + group[0])
        state, token = _copies_start("gather_pass", shards, "allgather_pass_" + group[0], lands=lands)
        for gn in group:
            passing[gn] = (group, state)
        return token

    def weight(n, after):
        if n not in ready:
            group, state = passing[n]
            shards, lands = _copies_wait(state, after, "allgather_passed_" + group[0])
            for gn, shard, land in zip(group, shards, lands):
                ready[gn] = full_weight(gn, lax.dynamic_update_slice(land, shard[None], (me_lin, 0, 0)))
        return ready[n]

    small_w_own, small_w_lands = _copies_wait(small_w_state, [narrow[n] for n in BIG if n != "w_in"], "allgather_wait_small_weights")
    ws = {n: (w[n].reshape(1, -1) if w[n].ndim == 1 else w[n]) for n in SMALL if n not in SMALL_SHARDED}
    for n, own, land in zip(sharded_names, small_w_own, small_w_lands):
        ws[n] = _small_unshard(n, lax.dynamic_update_slice(land, own[None], (me_lin, 0, 0)))
    pass_on("w_in", [ws[n] for n in sharded_names])

    sent = []

    def on_grads(grads):
        names = tuple(grads)
        state, token = _copies_start("exchange", [grad_parts(n, grads[n]) for n in names],
                                     "exchange_start_" + names[0].replace("#", "_"))
        sent.append((names, state))
        return token

    small_sent = []

    def on_small(small, loss):
        names = tuple(small)
        kinds = ["exchange" if n in SMALL_SHARDED else "gather" for n in names]
        srcs = [_small_shards(n, small[n]) if n in SMALL_SHARDED else _small_view(n, small[n]) for n in names]
        extra = [] if loss is None else [loss]
        state, token = _copies_start(kinds + ["gather"] * len(extra), srcs + extra, "allgather_start_small_" + names[0])
        small_sent.append((names, kinds, state))
        return token

    grad_x, last_token = _local_step(x[0], loss_target[0], weight, ws, prefetch, pass_on, on_grads, on_small)

    out = {}

    chunks = {}

    def finish(names, state, after):
        parts, lands = _copies_wait(state, after, "exchange_wait_" + names[0].replace("#", "_"))
        for name, part, land in zip(names, parts, lands):
            n, _, chunk = name.partition("#")
            chunks.setdefault(n, []).append((land, part))
            if chunk in ("", "1"):
                got_lands, got_parts = zip(*chunks[n])
                out[n] = _sum_adamw(got_lands, got_parts, me_idx, w[n], m[n], v[n], "adamw_" + n)

    def finish_small(names, kinds, state, after):
        own, lands = _copies_wait(state, after, "allgather_wait_small_" + names[0])
        k = len(names)
        upd = _small_update("adamw_small_" + names[0], me_idx, kinds, lands[:k], own[:k],
                            *[[_small_view(n, d[n]) for n in names] for d in (w, m, v)], sums=list(zip(lands[k:], own[k:])))
        for i, n in enumerate(names):
            out[n] = tuple(_small_unview(n, a, w[n].shape) for a in upd[4 * i:4 * i + 4])
        return upd[4 * k:]

    (loss_sum,) = finish_small(*small_sent[0], [grad_x, last_token])
    for names, state in sent[:-2]:
        finish(names, state, [grad_x, last_token])
    finish(*sent[-2], [loss_sum] + [out[n][1] for n in BIG if n in out])
    finish(*sent[-1], [loss_sum])
    finish_small(*small_sent[1], [out["w_in"][1]])

    shaped = lambda a, n: a.reshape(args[n].shape)
    return (loss_sum[0, 0], grad_x[None],
            *[shaped(out[n][0], n) for n in WEIGHTS], *[shaped(out[n][1], n) for n in WEIGHTS],
            *[shaped(out[n][2], n) for n in WEIGHTS], *[shaped(out[n][3], n) for n in WEIGHTS])
```

```python
import functools

import jax
import jax.numpy as jnp
from jax import lax
from jax.experimental import pallas as pl
from jax.experimental.pallas import tpu as pltpu

F32 = jnp.float32
BF16 = jnp.bfloat16
MESH = pl.DeviceIdType.MESH

N_DEV = 8
EPS = 1e-6
CHUNK = 64
CHUNKS_PER_STEP = 4
HEADS = 4
GLA_DK = 64
HEAD_W = 128
GLA_GATE_NORM = 16.0
LOWRANK = 16
CONV_K = 4
QKV_BLOCK = 4
LANES = 128
HALO = 8
IN_SPLITS = (256, 256, 512, 512, 16, 512, 512, 1024, 1024)

ADAM_LR = 0.001
ADAM_B1 = 0.9
ADAM_B2 = 0.999
ADAM_EPS = 1e-08
ADAM_WD = 0.01
ADAM_STEP = 10

VMEM_LIMIT = 56 * 1024 * 1024


def _cparams(*sem):
    return pltpu.CompilerParams(dimension_semantics=sem, vmem_limit_bytes=VMEM_LIMIT)


def _dims(mode, ndim):
    contract = {"nn": ((ndim - 1,), (ndim - 2,)), "nt": ((ndim - 1,), (ndim - 1,)), "tn": ((ndim - 2,), (ndim - 2,))}[mode]
    return contract, (((0,), (0,)) if ndim == 3 else ((), ()))


def _raw_dot(a, b, mode):
    return lax.dot_general(a.astype(BF16), b.astype(BF16), _dims(mode, a.ndim), preferred_element_type=F32)


@functools.partial(jax.custom_vjp, nondiff_argnums=(2,))
def _bdot(a, b, mode):
    return _raw_dot(a, b, mode)


def _bdot_fwd(a, b, mode):
    return _raw_dot(a, b, mode), (a, b)


def _bdot_bwd(mode, res, ct):
    a, b = res
    if mode == "nn":
        da, db = _raw_dot(ct, b, "nt"), _raw_dot(a, ct, "tn")
    elif mode == "nt":
        da, db = _raw_dot(ct, b, "nn"), _raw_dot(ct, a, "tn")
    else:
        da, db = _raw_dot(b, ct, "nt"), _raw_dot(a, ct, "nn")
    return da.astype(a.dtype), db.astype(b.dtype)


_bdot.defvjp(_bdot_fwd, _bdot_bwd)


def _split3(x):
    hi = x.astype(BF16)
    r1 = x - hi.astype(F32)
    mid = r1.astype(BF16)
    return hi, mid, (r1 - mid.astype(F32)).astype(BF16)


def _split_dot(tri, x):
    if x.ndim == 3:
        tri = jnp.broadcast_to(tri, (x.shape[0], *tri.shape))
    return sum(lax.dot_general(tri, t, _dims("nn", x.ndim), preferred_element_type=F32) for t in _split3(x))


def _tri(n, lower):
    r = lax.broadcasted_iota(jnp.int32, (n, n), 0)
    c = lax.broadcasted_iota(jnp.int32, (n, n), 1)
    return ((c <= r) if lower else (c >= r)).astype(BF16)


@jax.custom_vjp
def _cumsum_rows(x):
    return _split_dot(_tri(x.shape[-2], True), x)


def _cumsum_rows_fwd(x):
    return _cumsum_rows(x), None


def _cumsum_rows_bwd(_, ct):
    return (_split_dot(_tri(ct.shape[-2], False), ct),)


_cumsum_rows.defvjp(_cumsum_rows_fwd, _cumsum_rows_bwd)


def _abs(x):
    return jnp.where(x >= 0, x, -x)


def _sigmoid(x):
    return lax.logistic(x)


def _log_sigmoid(x):
    return jnp.minimum(x, 0.0) - jnp.log(1.0 + jnp.exp(-_abs(x)))


def _rms(x, g):
    return x * lax.rsqrt(jnp.mean(x * x, axis=-1, keepdims=True) + EPS) * g


def _head_slices(w):
    return [slice(h * w, (h + 1) * w) for h in range(HEADS)]


def _heads(ref, rows=slice(None)):
    return jnp.stack([ref[rows, hs] for hs in _head_slices(HEAD_W)])


def _put_heads(ref, val, rows=slice(None)):
    for h, hs in enumerate(_head_slices(HEAD_W)):
        ref[rows, hs] = val[h].astype(ref.dtype)


def _tile(dim, want):
    if dim <= want or dim % LANES:
        return dim
    t = want
    while dim % t:
        t -= LANES
    return t


def _mm(a, b, mode, out_dtype, name, tm=1024, tn=1024, tk=4096, epilogue=None, extra=(), deps=(), shards=None):
    if shards == "b":
        assert mode == "nn"
        ns = b.shape[2]
        (m, k), (k2, n) = a.shape, (b.shape[1], b.shape[0] * ns)
        tn = ns
    elif mode == "nn":
        (m, k), (k2, n) = a.shape, b.shape
    elif mode == "nt":
        (m, k), (n, k2) = a.shape, b.shape
    else:
        (k, m), (k2, n) = a.shape, b.shape
    assert k == k2, (name, a.shape, b.shape)
    tm, tn, tk = _tile(m, tm), _tile(n, tn), _tile(k, tk)
    nk = k // tk
    out_dtypes = out_dtype if epilogue else (out_dtype,)
    assert nk == 1 or (out_dtype == F32 and not epilogue), name
    n_in = 2 + len(extra)

    def body(*refs):
        p = _raw_dot(refs[0][...], refs[1][...], mode)
        if nk > 1:
            _accumulate(pl.program_id(2), [refs[n_in + len(deps)]], [p])
            return
        outs = epilogue(p, *[r[...] for r in refs[2:n_in]]) if epilogue else (p,)
        for ref, val in zip(refs[n_in + len(deps):], outs):
            ref[...] = val.astype(ref.dtype)

    a_spec = pl.BlockSpec((tk, tm), lambda i, j, kk: (kk, i)) if mode == "tn" else pl.BlockSpec((tm, tk), lambda i, j, kk: (i, kk))
    if shards == "b":
        b_spec = pl.BlockSpec((None, tk, tn), lambda i, j, kk: (j, kk, 0))
    elif mode == "nt":
        b_spec = pl.BlockSpec((tn, tk), lambda i, j, kk: (j, kk))
    else:
        b_spec = pl.BlockSpec((tk, tn), lambda i, j, kk: (kk, j))
    o_spec = pl.BlockSpec((tm, tn), lambda i, j, kk: (i, j))
    if shards == "out":
        out_specs, out_shape = [pl.BlockSpec((None, tm, tn), lambda i, j, kk: (j, i, 0))], (n // tn, m, tn)
    else:
        out_specs, out_shape = [o_spec] * len(out_dtypes), (m, n)
    res = pl.pallas_call(
        body, name=name, grid=(m // tm, n // tn, nk),
        in_specs=[a_spec, b_spec] + [o_spec] * len(extra) + [ANY] * len(deps), out_specs=out_specs,
        out_shape=[jax.ShapeDtypeStruct(out_shape, dt) for dt in out_dtypes],
        compiler_params=_cparams("parallel", "parallel", "arbitrary"),
    )(a, b, *extra, *deps)
    return res if epilogue else res[0]


def _mm_shard_cols(a, b, n, name, row_tile, tm, deps=()):
    t = a.shape[0]
    nb = b.shape[1] // n

    def body(a_ref, b_ref, *rest):
        o_ref, at_ref = rest[len(deps):]
        j = pl.program_id(0)

        @pl.when(j == 0)
        def _():
            at_ref[...] = a_ref[...].astype(BF16).T

        for s in range(nb):
            @pl.when(j == s)
            def _(s=s):
                o_ref[...] = _raw_dot(at_ref[...], b_ref[:, s * n:(s + 1) * n], "nn").astype(BF16)

    return pl.pallas_call(
        body, name=name, grid=(nb,),
        in_specs=[pl.BlockSpec((t, tm), lambda j: (0, row_tile)),
                  pl.BlockSpec((t, nb * n), lambda j: (0, 0), pipeline_mode=pl.Buffered(1))] + [ANY] * len(deps),
        out_specs=pl.BlockSpec((None, tm, n), lambda j: (j, 0, 0)),
        out_shape=jax.ShapeDtypeStruct((nb, tm, n), BF16),
        scratch_shapes=[pltpu.VMEM((tm, t), BF16)],
        compiler_params=_cparams("arbitrary"),
    )(a, b, *deps)


def _rowwise(name, fn, rows, params, out_rows, out_accs=(), tile=256, deps=()):
    t = rows[0].shape[0]
    r = min(tile, t)
    assert t % r == 0
    n_in, n_or = len(rows) + len(params), len(out_rows)
    n_all = n_in + len(deps)
    params = list(params) + list(deps)

    def body(*refs):
        vals = [ref[...] for ref in refs[:n_in]]
        outs = refs[n_all:]
        ro, ao = fn(*vals)
        for ref, v in zip(outs[:n_or], ro):
            ref[...] = v.astype(ref.dtype)
        if out_accs:
            _accumulate(pl.program_id(0), outs[n_or:], ao)

    def full(shape):
        return pl.BlockSpec(shape, lambda i, nd=len(shape): (0,) * nd)

    return pl.pallas_call(
        body, name=name, grid=(t // r,),
        in_specs=[pl.BlockSpec((r, a.shape[1]), lambda i: (i, 0)) for a in rows] + [full(p.shape) for p in params],
        out_specs=[pl.BlockSpec((r, w), lambda i: (i, 0)) for w, _ in out_rows] + [full(s) for s, _ in out_accs],
        out_shape=[jax.ShapeDtypeStruct((t, w), dt) for w, dt in out_rows] + [jax.ShapeDtypeStruct(s, dt) for s, dt in out_accs],
        compiler_params=_cparams("arbitrary"),
    )(*rows, *params)


def _accumulate(step, refs, vals):
    for ref, v in zip(refs, vals):
        @pl.when(step == 0)
        def _(ref=ref, v=v):
            ref[...] = v.astype(ref.dtype)

        @pl.when(step > 0)
        def _(ref=ref, v=v):
            ref[...] += v.astype(ref.dtype)


def _gla_chunk(q, k, v, la, st):
    c = q.shape[-2]
    row = lax.broadcasted_iota(jnp.int32, (c, c), 0)
    col = lax.broadcasted_iota(jnp.int32, (c, c), 1)
    cum = _cumsum_rows(la)
    cl = jnp.sum(la, axis=-2, keepdims=True)
    ep = jnp.exp(cum)
    en = jnp.exp(-cum)
    qs = q * (GLA_DK ** -0.5)
    qp = qs * ep
    a_f = _bdot(qp, k * en, "nt")
    a_b = _bdot(qs * en, k * ep, "nt")
    sc = jnp.where(row >= col, a_f, a_b)
    o = _bdot(sc, v, "nn") + _bdot(qp, st, "nt")
    kd = k * jnp.exp(cl - cum)
    st_new = st * jnp.exp(cl) + _bdot(v, kd, "tn")
    return o, st_new


def _gla_specs(nc, rev):
    nb = nc // CHUNKS_PER_STEP
    rows = CHUNKS_PER_STEP * CHUNK

    def blk(n):
        return (nb - 1 - n) if rev else n
    hm = pl.BlockSpec((HEADS, rows, GLA_DK), lambda n: (0, blk(n), 0))
    tm = pl.BlockSpec((rows, HEADS * HEAD_W), lambda n: (blk(n), 0))
    st = pl.BlockSpec((HEADS, CHUNKS_PER_STEP, HEAD_W, GLA_DK), lambda n: (0, blk(n), 0, 0))
    return nb, hm, tm, st


def _chunk_rows(c):
    return slice(c * CHUNK, (c + 1) * CHUNK)


def _gla_fwd(q, k, v, la, deps=()):
    t = v.shape[0]
    nc = t // CHUNK
    nb, hm, tm, st = _gla_specs(nc, False)

    def body(q_ref, k_ref, v_ref, la_ref, *rest):
        o_ref, sp_ref, st_ref = rest[len(deps):]

        @pl.when(pl.program_id(0) == 0)
        def _():
            st_ref[...] = jnp.zeros_like(st_ref)

        s = st_ref[...]
        for c in range(CHUNKS_PER_STEP):
            r = _chunk_rows(c)
            sp_ref[:, c] = s
            o, s = _gla_chunk(q_ref[:, r], k_ref[:, r], _heads(v_ref, r), la_ref[:, r], s)
            _put_heads(o_ref, o, r)
        st_ref[...] = s

    return pl.pallas_call(
        body, name="gla_fwd", grid=(nb,),
        in_specs=[hm, hm, tm, hm] + [ANY] * len(deps), out_specs=[tm, st],
        out_shape=[jax.ShapeDtypeStruct((t, HEADS * HEAD_W), F32), jax.ShapeDtypeStruct((HEADS, nc, HEAD_W, GLA_DK), F32)],
        scratch_shapes=[pltpu.VMEM((HEADS, HEAD_W, GLA_DK), F32)],
        compiler_params=_cparams("arbitrary"),
    )(q, k, v, la, *deps)


def _gla_bwd(q, k, v, la, sp, do):
    t = v.shape[0]
    nc = t // CHUNK
    nb, hm, tm, st = _gla_specs(nc, True)

    def body(q_ref, k_ref, v_ref, la_ref, sp_ref, do_ref, dq_ref, dk_ref, dv_ref, dla_ref, ds_ref):
        @pl.when(pl.program_id(0) == 0)
        def _():
            ds_ref[...] = jnp.zeros_like(ds_ref)

        ds = ds_ref[...]
        for c in reversed(range(CHUNKS_PER_STEP)):
            r = _chunk_rows(c)
            _, vjp = jax.vjp(_gla_chunk, q_ref[:, r], k_ref[:, r], _heads(v_ref, r), la_ref[:, r], sp_ref[:, c])
            dq, dk, dv, dla, ds = vjp((_heads(do_ref, r), ds))
            dq_ref[:, r] = dq.astype(dq_ref.dtype)
            dk_ref[:, r] = dk.astype(dk_ref.dtype)
            _put_heads(dv_ref, dv, r)
            dla_ref[:, r] = dla
        ds_ref[...] = ds

    hm_shape = jax.ShapeDtypeStruct((HEADS, t, GLA_DK), BF16)
    return pl.pallas_call(
        body, name="gla_bwd", grid=(nb,),
        in_specs=[hm, hm, tm, hm, st, tm], out_specs=[hm, hm, tm, hm],
        out_shape=[hm_shape, hm_shape, jax.ShapeDtypeStruct((t, HEADS * HEAD_W), BF16),
                   jax.ShapeDtypeStruct((HEADS, t, GLA_DK), F32)],
        scratch_shapes=[pltpu.VMEM((HEADS, HEAD_W, GLA_DK), F32)],
        compiler_params=_cparams("arbitrary"),
    )(q, k, v, la, sp, do)


def _ml_chunk(q, k, v, li_r, lf_r, cm, nv, m):
    c = q.shape[-2]
    row = lax.broadcasted_iota(jnp.int32, (c, c), 0)
    col = lax.broadcasted_iota(jnp.int32, (c, c), 1)
    eye = (row == col).astype(F32)
    li_c = jnp.sum(eye * li_r, axis=-1, keepdims=True)
    lf_c = jnp.sum(eye * lf_r, axis=-1, keepdims=True)
    fc_c = jnp.sum((col <= row).astype(F32) * lf_r, axis=-1, keepdims=True)
    fc_r = jnp.sum((row <= col).astype(F32) * lf_c, axis=-2, keepdims=True)
    f_last = jnp.sum(lf_r, axis=-1, keepdims=True)
    kc = k * (HEAD_W ** -0.5)
    a_c = f_last - fc_c + li_c
    m_loc = jnp.max(a_c, axis=-2, keepdims=True)
    kw = kc * jnp.exp(a_c - m_loc)
    c_chunk = _bdot(kw, v, "tn")
    n_chunk = jnp.sum(kw, axis=-2, keepdims=True)
    m_new = jnp.maximum(f_last + m, m_loc)
    sp = jnp.exp(f_last + m - m_new)
    sl = jnp.exp(m_loc - m_new)
    cm_new = sp * cm + sl * c_chunk
    nv_new = sp * nv + sl * n_chunk
    log_d = li_r - _abs(fc_c - fc_r)
    g_inter = fc_c + m
    m_t = jnp.maximum(g_inter, jnp.max(log_d, axis=-1, keepdims=True))
    s = _bdot(q, kc, "nt") * jnp.exp(log_d - m_t)
    sc = jnp.exp(g_inter - m_t)
    num = _bdot(s, v, "nn") + sc * _bdot(q, cm, "nn")
    den = jnp.sum(s, axis=-1, keepdims=True) + sc * jnp.sum(q * nv, axis=-1, keepdims=True)
    den = jnp.maximum(_abs(den), jnp.exp(-m_t))
    return num / den, cm_new, nv_new, m_new


def _ml_specs(nc, rev):
    nb = nc // CHUNKS_PER_STEP

    def blk(n):
        return (nb - 1 - n) if rev else n
    tm = pl.BlockSpec((CHUNKS_PER_STEP * CHUNK, HEADS * HEAD_W), lambda n: (blk(n), 0))
    gate = pl.BlockSpec((HEADS, CHUNKS_PER_STEP, 1, CHUNK), lambda n: (0, blk(n), 0, 0))
    cm = pl.BlockSpec((HEADS, CHUNKS_PER_STEP, HEAD_W, HEAD_W), lambda n: (0, blk(n), 0, 0))
    vec = pl.BlockSpec((HEADS, CHUNKS_PER_STEP, 1, HEAD_W), lambda n: (0, blk(n), 0, 0))
    return nb, tm, gate, cm, vec


_ML_STATE = [pltpu.VMEM((HEADS, HEAD_W, HEAD_W), F32), pltpu.VMEM((HEADS, 1, HEAD_W), F32), pltpu.VMEM((HEADS, 1, HEAD_W), F32)]


def _ml_fwd(q, k, v, li, lf):
    t = q.shape[0]
    nc = t // CHUNK
    nb, tm, gate, cm, vec = _ml_specs(nc, False)

    def body(q_ref, k_ref, v_ref, li_ref, lf_ref, hc_ref, cp_ref, np_ref, mp_ref, c_ref, n_ref, m_ref):
        @pl.when(pl.program_id(0) == 0)
        def _():
            c_ref[...] = jnp.zeros_like(c_ref)
            n_ref[...] = jnp.zeros_like(n_ref)
            m_ref[...] = jnp.zeros_like(m_ref)

        cs, ns, ms = c_ref[...], n_ref[...], m_ref[...][:, :, 0:1]
        for c in range(CHUNKS_PER_STEP):
            r = _chunk_rows(c)
            cp_ref[:, c] = cs
            np_ref[:, c] = ns
            mp_ref[:, c] = jnp.broadcast_to(ms, m_ref.shape)
            hc, cs, ns, ms = _ml_chunk(_heads(q_ref, r), _heads(k_ref, r), _heads(v_ref, r), li_ref[:, c], lf_ref[:, c],
                                       cs, ns, ms)
            _put_heads(hc_ref, hc, r)
        c_ref[...] = cs
        n_ref[...] = ns
        m_ref[...] = jnp.broadcast_to(ms, m_ref.shape)

    return pl.pallas_call(
        body, name="mlstm_fwd", grid=(nb,),
        in_specs=[tm, tm, tm, gate, gate], out_specs=[tm, cm, vec, vec],
        out_shape=[jax.ShapeDtypeStruct((t, HEADS * HEAD_W), F32), jax.ShapeDtypeStruct((HEADS, nc, HEAD_W, HEAD_W), F32),
                   jax.ShapeDtypeStruct((HEADS, nc, 1, HEAD_W), F32), jax.ShapeDtypeStruct((HEADS, nc, 1, HEAD_W), F32)],
        scratch_shapes=_ML_STATE,
        compiler_params=_cparams("arbitrary"),
    )(q, k, v, li, lf)


def _ml_bwd(q, k, v, li, lf, cp, npv, mp, dhc):
    t = q.shape[0]
    nc = t // CHUNK
    nb, tm, gate, cm, vec = _ml_specs(nc, True)

    def body(q_ref, k_ref, v_ref, li_ref, lf_ref, cp_ref, np_ref, mp_ref, dhc_ref,
             dq_ref, dk_ref, dv_ref, dli_ref, dlf_ref, dc_ref, dn_ref, dm_ref):
        @pl.when(pl.program_id(0) == 0)
        def _():
            dc_ref[...] = jnp.zeros_like(dc_ref)
            dn_ref[...] = jnp.zeros_like(dn_ref)
            dm_ref[...] = jnp.zeros_like(dm_ref)

        dc, dn, dm = dc_ref[...], dn_ref[...], dm_ref[...][:, :, 0:1]
        for c in reversed(range(CHUNKS_PER_STEP)):
            r = _chunk_rows(c)
            _, vjp = jax.vjp(_ml_chunk, _heads(q_ref, r), _heads(k_ref, r), _heads(v_ref, r), li_ref[:, c], lf_ref[:, c],
                             cp_ref[:, c], np_ref[:, c], mp_ref[:, c][:, :, 0:1])
            dq, dk, dv, dli, dlf, dc, dn, dm = vjp((_heads(dhc_ref, r), dc, dn, dm))
            _put_heads(dq_ref, dq, r)
            _put_heads(dk_ref, dk, r)
            _put_heads(dv_ref, dv, r)
            dli_ref[:, c] = dli
            dlf_ref[:, c] = dlf
        dc_ref[...] = dc
        dn_ref[...] = dn
        dm_ref[...] = jnp.broadcast_to(dm, dm_ref.shape)

    tm_shape = jax.ShapeDtypeStruct((t, HEADS * HEAD_W), F32)
    gate_shape = jax.ShapeDtypeStruct((HEADS, nc, 1, CHUNK), F32)
    return pl.pallas_call(
        body, name="mlstm_bwd", grid=(nb,),
        in_specs=[tm, tm, tm, gate, gate, cm, vec, vec, tm], out_specs=[tm, tm, tm, gate, gate],
        out_shape=[tm_shape, tm_shape, tm_shape, gate_shape, gate_shape],
        scratch_shapes=_ML_STATE,
        compiler_params=_cparams("arbitrary"),
    )(q, k, v, li, lf, cp, npv, mp, dhc)


def _ml_pre(s0, s1, s2, s3, cw0, cw1, cw2, cw3, cb, wq, wk, wv, wiq, wik, wiv, bif):
    pre = cb + cw0 * s0 + cw1 * s1 + cw2 * s2 + cw3 * s3
    xc = pre * _sigmoid(pre)
    q = _bdot(xc, wq, "nn")
    k = _bdot(xc, wk, "nn")
    v = _bdot(s3, wv, "nn")
    gates = _bdot(q, wiq, "nn") + _bdot(k, wik, "nn") + _bdot(v, wiv, "nn") + bif
    lane = lax.broadcasted_iota(jnp.int32, gates.shape, 1)
    gl = jnp.where(lane < HEADS, gates, _log_sigmoid(gates))
    return xc, q, k, v, gl


def _delayed(xs_ref, x_ref, halo_ref, r):
    xs_ref[0:HALO, :] = halo_ref[...]
    xs_ref[HALO:HALO + r, :] = x_ref[...]
    return [xs_ref[pl.ds(HALO - (CONV_K - 1) + j, r), :] for j in range(CONV_K)]


def _full_spec(shape):
    return pl.BlockSpec(shape, lambda i, nd=len(shape): (0,) * nd)


def _ml_pre_fwd(x_m, x_pad, params, tile=256, deps=()):
    t, w = x_m.shape
    r = min(tile, t)

    def body(*refs):
        x_ref, halo_ref = refs[:2]
        p = [ref[...] for ref in refs[2:2 + len(params)]]
        outs = refs[2 + len(params) + len(deps):-1]
        res = _ml_pre(*_delayed(refs[-1], x_ref, halo_ref, r), *p)
        for ref, val in zip(outs, res):
            ref[...] = val

    row = pl.BlockSpec((r, w), lambda i: (i, 0))
    return pl.pallas_call(
        body, name="ml_pre_fwd", grid=(t // r,),
        in_specs=[row, pl.BlockSpec((HALO, w), lambda i: (i * (r // HALO), 0))] + [_full_spec(p.shape) for p in params]
        + [ANY] * len(deps),
        out_specs=[row] * 4 + [pl.BlockSpec((r, LANES), lambda i: (i, 0))],
        out_shape=[jax.ShapeDtypeStruct((t, w), F32)] * 4 + [jax.ShapeDtypeStruct((t, LANES), F32)],
        scratch_shapes=[pltpu.VMEM((r + HALO, w), F32)],
        compiler_params=_cparams("arbitrary"),
    )(x_m, x_pad, *params, *deps)


def _ml_pre_bwd(x_m, x_pad, params, cts, tile=256):
    t, w = x_m.shape
    r = min(tile, t)
    nt = t // r
    n_p = len(params)

    def body(*refs):
        x_ref, halo_ref = refs[:2]
        p = [ref[...] for ref in refs[2:2 + n_p]]
        ct = [ref[...] for ref in refs[2 + n_p:7 + n_p]]
        dx_ref = refs[7 + n_p]
        dp_refs = refs[8 + n_p:8 + 2 * n_p]
        xs_ref, ds_ref, carry_ref = refs[8 + 2 * n_p:]
        step = pl.program_id(0)

        @pl.when(step == 0)
        def _():
            ds_ref[...] = jnp.zeros_like(ds_ref)
            carry_ref[...] = jnp.zeros_like(carry_ref)

        _, vjp = jax.vjp(_ml_pre, *_delayed(xs_ref, x_ref, halo_ref, r), *p)
        grads = vjp(tuple(ct))
        for j in range(CONV_K):
            ds_ref[j, HALO:HALO + r, :] = grads[j]
        lead = HALO + CONV_K - 1
        d_tile = sum(ds_ref[j, pl.ds(lead - j, r), :] for j in range(CONV_K))
        d_halo = sum(ds_ref[j, pl.ds(CONV_K - 1 - j, HALO), :] for j in range(CONV_K))
        dx_ref[...] = jnp.concatenate([d_tile[:r - HALO], d_tile[r - HALO:] + carry_ref[...]], axis=0).astype(dx_ref.dtype)
        carry_ref[...] = d_halo
        _accumulate(step, dp_refs, grads[CONV_K:])

    row = pl.BlockSpec((r, w), lambda i: (nt - 1 - i, 0))
    return pl.pallas_call(
        body, name="ml_pre_bwd", grid=(nt,),
        in_specs=[row, pl.BlockSpec((HALO, w), lambda i: ((nt - 1 - i) * (r // HALO), 0))] + [_full_spec(p.shape) for p in params]
        + [row] * 4 + [pl.BlockSpec((r, LANES), lambda i: (nt - 1 - i, 0))],
        out_specs=[row] + [_full_spec(p.shape) for p in params],
        out_shape=[jax.ShapeDtypeStruct((t, w), BF16)] + [jax.ShapeDtypeStruct(p.shape, F32) for p in params],
        scratch_shapes=[pltpu.VMEM((r + HALO, w), F32), pltpu.VMEM((CONV_K, r + 2 * HALO, w), F32), pltpu.VMEM((HALO, w), F32)],
        compiler_params=_cparams("arbitrary"),
    )(x_m, x_pad, *params, *cts)


def _per_head(fn, row_vals, head_params, shared_params=()):
    return [fn(*[a[:, hs] for a in row_vals], *[p[:, hs] for p in head_params], *shared_params) for hs in _head_slices(HEAD_W)]


def _gla_out(o, g, gn):
    return _rms(o, gn) * (g * _sigmoid(g))


def _ml_out(hc, op, xc, g, sk):
    hcell = hc * _sigmoid(op)
    mu = jnp.mean(hcell, axis=-1, keepdims=True)
    d = hcell - mu
    var = jnp.mean(d * d, axis=-1, keepdims=True)
    return d * lax.rsqrt(var + EPS) * g + sk * xc


def _log_decay(al, w, b):
    return _log_sigmoid(_bdot(al, w, "nn") + b) * (1.0 / GLA_GATE_NORM)


def _merge(ga, gb, ya, yb):
    return _sigmoid(ga) * ya + _sigmoid(gb) * yb


def _post_mix(x, z, gpm, gpl):
    x1 = x + _rms(z, gpm)
    return x1, _rms(x1, gpl)


def _loss_rows(x1, dn, tgt, g):
    e = x1 + _rms(dn, g) - tgt
    return 0.5 * jnp.sum(jnp.mean(e * e, axis=-1, keepdims=True), axis=0, keepdims=True)


def _lin(p):
    return 4 * p[0] + 2 * p[1] + p[2]


def _me():
    return lax.axis_index("x"), lax.axis_index("y"), lax.axis_index("c")


def _flip(p, k):
    return tuple((1 - v) if (k >> (2 - i)) & 1 else v for i, v in enumerate(p))


ANY = pl.BlockSpec(memory_space=pl.ANY)


HBM = pl.BlockSpec(memory_space=pltpu.HBM)
SEM = pl.BlockSpec(memory_space=pltpu.SEMAPHORE)
DATAFLOW = pltpu.SideEffectType.DATAFLOW_SIDE_EFFECTING


SIBLING = 1
OTHER_CHIPS = (2, 4, 6)


def _peer_copies(kinds, srcs, lands, send_sems, recv_sems):
    me = _me()
    copies = []
    for a, (kind, src, land) in enumerate(zip(kinds, srcs, lands)):
        masks = {"gather": range(1, N_DEV), "exchange": range(1, N_DEV), "gather_chips": (SIBLING, *OTHER_CHIPS),
                 "gather_pass": OTHER_CHIPS}[kind]
        for k in masks:
            peer = _flip(me, k)
            if kind == "gather_pass":
                block = land.at[_lin(peer)]
                src_ref, dst_ref, target = block, block, _flip(me, SIBLING)
            else:
                src_ref, dst_ref, target = (src.at[_lin(peer)] if kind == "exchange" else src), land.at[_lin(me)], peer
            copies.append(pltpu.make_async_remote_copy(
                src_ref=src_ref, dst_ref=dst_ref, send_sem=send_sems.at[a * 7 + k - 1], recv_sem=recv_sems.at[a * 7 + k - 1],
                device_id=target, device_id_type=MESH))
    return copies


def _copies_start(kind, srcs, name, after=None, lands=None):
    n = len(srcs)
    extra = [] if after is None else [after]
    kind = [kind] * n if isinstance(kind, str) else list(kind)
    land_shapes = [(s.shape if k == "exchange" else (N_DEV, *s.shape)) for k, s in zip(kind, srcs)]
    lands = [lax.empty(ls, s.dtype) for ls, s in zip(land_shapes, srcs)] if lands is None else lands

    def body(*refs):
        sems = refs[2 * n + len(extra):]
        for cp in _peer_copies(kind, refs[:n], refs[n:2 * n], sems[0], sems[1]):
            cp.start()
        refs[-1][...] = jnp.zeros_like(refs[-1])

    def hbm(a):
        return pltpu.with_memory_space_constraint(a, pltpu.HBM)

    out = pl.pallas_call(
        body, name=name,
        out_shape=(pltpu.SemaphoreType.DMA((7 * n,)), pltpu.SemaphoreType.DMA((7 * n,)),
                   *[pltpu.HBM(s.shape, s.dtype) for s in srcs],
                   *[pltpu.HBM(ls, s.dtype) for ls, s in zip(land_shapes, srcs)],
                   jax.ShapeDtypeStruct((8, LANES), F32)),
        in_specs=[HBM] * (2 * n) + [ANY] * len(extra),
        out_specs=(SEM, SEM, *[HBM] * (2 * n), pl.BlockSpec(memory_space=pltpu.VMEM)),
        input_output_aliases={i: 2 + i for i in range(2 * n)},
        compiler_params=pltpu.CompilerParams(has_side_effects=DATAFLOW),
    )(*[hbm(s) for s in srcs], *[hbm(a) for a in lands], *extra)
    return (kind, n, out[:-1]), out[-1]


def _copies_wait(state, after, name):
    kind, n, (send_sems, recv_sems, *thru) = state
    after = list(after) if isinstance(after, (list, tuple)) else [after]

    def body(*refs):
        for cp in _peer_copies(kind, refs[:n], refs[n:2 * n], refs[2 * n], refs[2 * n + 1]):
            cp.wait_send()
            cp.wait_recv()

    out = pl.pallas_call(
        body, name=name,
        out_shape=tuple(pltpu.HBM(t.shape, t.dtype) for t in thru),
        in_specs=[HBM] * (2 * n) + [SEM, SEM] + [ANY] * len(after), out_specs=tuple([HBM] * (2 * n)),
        input_output_aliases={i: i for i in range(2 * n)},
        compiler_params=pltpu.CompilerParams(has_side_effects=DATAFLOW),
    )(*thru, send_sems, recv_sems, *after)
    return out[:n], out[n:]


def _adamw(w, g, m, v):
    m2 = ADAM_B1 * m + (1.0 - ADAM_B1) * g
    v2 = ADAM_B2 * v + (1.0 - ADAM_B2) * (g * g)
    m_hat = m2 / (1.0 - ADAM_B1 ** ADAM_STEP)
    v_hat = v2 / (1.0 - ADAM_B2 ** ADAM_STEP)
    delta = -ADAM_LR * (m_hat / (jnp.sqrt(v_hat) + ADAM_EPS) + ADAM_WD * w)
    return delta, m2, v2


def _sum_adamw(lands, parts, me_idx, w, m, v, name, tile=256):
    r, c = w.shape
    nchunks = len(lands)
    tr = min(tile, r // nchunks)
    per_chunk = r // nchunks // tr
    per = 1 + N_DEV

    def body(me_ref, *refs):
        w_ref, m_ref, v_ref, g_ref, d_ref, m2_ref, v2_ref = refs[nchunks * per:]
        for k in range(nchunks):
            own_ref, slots = refs[k * per], refs[k * per + 1:(k + 1) * per]

            @pl.when(pl.program_id(0) // per_chunk == k)
            def _(own_ref=own_ref, slots=slots):
                own = own_ref[...].astype(F32)
                g = None
                for s in range(N_DEV):
                    term = jnp.where(me_ref[0] == s, own, slots[s][...].astype(F32))
                    g = term if g is None else g + term
                d, m2, v2 = _adamw(w_ref[...], g, m_ref[...], v_ref[...])
                g_ref[...] = g
                d_ref[...] = d
                m2_ref[...] = m2
                v2_ref[...] = v2

    def chunk_specs(k):
        def tile_of(i):
            return jnp.clip(i - k * per_chunk, 0, per_chunk - 1)

        def slot_spec(s):
            return pl.BlockSpec((None, tr, c), lambda i, me: (jnp.where(me[0] == s, (s + 1) % N_DEV, s), tile_of(i), 0))
        return [pl.BlockSpec((None, tr, c), lambda i, me: (me[0], tile_of(i), 0))] + [slot_spec(s) for s in range(N_DEV)]

    row = pl.BlockSpec((tr, c), lambda i, me: (i, 0))
    operands = [a for land, part in zip(lands, parts) for a in (part, *[land] * N_DEV)]
    return pl.pallas_call(
        body, name=name,
        grid_spec=pltpu.PrefetchScalarGridSpec(
            num_scalar_prefetch=1, grid=(r // tr,),
            in_specs=[s for k in range(nchunks) for s in chunk_specs(k)] + [row] * 3,
            out_specs=[row] * 4),
        out_shape=[jax.ShapeDtypeStruct((r, c), F32)] * 4,
        compiler_params=_cparams("parallel"),
    )(me_idx, *operands, w, m, v)


def _small_update(name, me_idx, kinds, lands, owns, ws, ms, vs, sums=()):
    n = len(ws)
    lands, owns = list(lands) + [s[0] for s in sums], list(owns) + [s[1] for s in sums]
    kinds = list(kinds) + ["gather"] * len(sums)
    nl = len(lands)

    def summed(me, land_ref, own):
        g = None
        for s in range(N_DEV):
            term = jnp.where(me == s, own, land_ref[s])
            g = term if g is None else g + term
        return g

    def body(me_ref, *refs):
        land_refs, own_refs = refs[:nl], refs[nl:2 * nl]
        w_refs, m_refs, v_refs = (refs[2 * nl + i * n:2 * nl + (i + 1) * n] for i in range(3))
        outs = refs[2 * nl + 3 * n:]
        me = me_ref[0]
        for i in range(n):
            g = summed(me, land_refs[i], own_refs[i][...])
            d, m2, v2 = _adamw(w_refs[i][...], g, m_refs[i][...], v_refs[i][...])
            for ref, val in zip(outs[4 * i:4 * i + 4], (g, d, m2, v2)):
                ref[...] = val
        for i in range(n, nl):
            outs[4 * n + i - n][...] = summed(me, land_refs[i], own_refs[i][...])

    def whole(shape):
        return pl.BlockSpec(shape, lambda i, me, nd=len(shape): (0,) * nd)

    def own_spec(kind, own):
        if kind == "gather":
            return whole(own.shape)
        return pl.BlockSpec((None, *own.shape[1:]), lambda i, me: (me[0], 0, 0))

    shapes = [w.shape for w in ws]
    out_shapes = [s for s in shapes for _ in range(4)] + [s[1].shape for s in sums]
    return pl.pallas_call(
        body, name=name,
        grid_spec=pltpu.PrefetchScalarGridSpec(
            num_scalar_prefetch=1, grid=(1,),
            in_specs=[whole(a.shape) for a in lands] + [own_spec(k, o) for k, o in zip(kinds, owns)]
            + [whole(s) for s in shapes] * 3,
            out_specs=[whole(s) for s in out_shapes]),
        out_shape=[jax.ShapeDtypeStruct(s, F32) for s in out_shapes],
        compiler_params=_cparams("arbitrary"),
    )(me_idx, *lands, *owns, *ws, *ms, *vs)


def _small_view(n, a):
    if a.ndim == 1:
        return a.reshape(1, -1)
    if a.ndim == 3:
        return a.transpose(1, 2, 0).reshape(QKV_BLOCK * QKV_BLOCK, -1)
    return a.T if n == "w_if" else a


def _small_unview(n, a, shape):
    if len(shape) == 1:
        return a.reshape(shape)
    if len(shape) == 3:
        return a.reshape(QKV_BLOCK, QKV_BLOCK, -1).transpose(2, 0, 1)
    return a.T if n == "w_if" else a


def _small_shards(n, g):
    if n == "w_if":
        return g.reshape(N_DEV, -1, g.shape[1]).transpose(0, 2, 1)
    return g.reshape(g.shape[0], N_DEV, -1).transpose(1, 0, 2)


def _small_unshard(n, s):
    if n == "w_if":
        return s.transpose(0, 2, 1).reshape(-1, s.shape[1])
    return s.transpose(1, 0, 2).reshape(s.shape[1], -1)


def _to_hm(a, d):
    t = a.shape[0]
    return a.reshape(t, HEADS, d).transpose(1, 0, 2)


def _from_hm(a):
    h, t, d = a.shape
    return a.transpose(1, 0, 2).reshape(t, h * d)


def _gate_rows(g):
    t = g.shape[0]
    return g.T.reshape(HEADS, t // CHUNK, 1, CHUNK)


def _gate_cols(g):
    h, nc, _, c = g.shape
    return g.reshape(h, nc * c).T


def _blockdiag_dense(w):
    n = w.shape[0] * QKV_BLOCK
    tiled = jnp.tile(w.reshape(n, QKV_BLOCK), (1, n // QKV_BLOCK))
    r = lax.broadcasted_iota(jnp.int32, (n, n), 0)
    c = lax.broadcasted_iota(jnp.int32, (n, n), 1)
    return jnp.where(r // QKV_BLOCK == c // QKV_BLOCK, tiled, 0.0)


def _blockdiag_blocks(dense):
    n = dense[0].shape[0]
    k = len(dense)

    def body(*refs):
        r = lax.broadcasted_iota(jnp.int32, (n, n), 0)
        c = lax.broadcasted_iota(jnp.int32, (n, n), 1)
        fr = lax.broadcasted_iota(jnp.int32, (n, LANES), 0)
        fc = lax.broadcasted_iota(jnp.int32, (n, LANES), 1)
        fold = ((fr & (QKV_BLOCK - 1)) == fc).astype(BF16)
        for i in range(k):
            kept = jnp.where((r >> 2) == (c >> 2), refs[i][...], 0.0)
            refs[k + i][...] = sum(lax.dot_general(t, fold, _dims("nn", 2), preferred_element_type=F32) for t in _split3(kept))

    out = pl.pallas_call(body, name="blockdiag_blocks", out_shape=[jax.ShapeDtypeStruct((n, LANES), F32)] * k)(*dense)
    return [o[:, 0:QKV_BLOCK].reshape(n // QKV_BLOCK, QKV_BLOCK, QKV_BLOCK) for o in out]


def _col_blocks(w):
    k, n = w.shape
    return w.reshape(k, N_DEV, n // N_DEV).transpose(1, 0, 2)


def _from_col_blocks(g):
    d, k, n = g.shape
    return g.transpose(1, 0, 2).reshape(k, d * n)


def _local_step(x, tgt, weight, ws, prefetch, pass_on, on_grads, on_small):
    t, d = x.shape
    g1 = ws["g_pre_mix"]

    def dep(token):
        return () if token is None else (token,)

    w_in = weight("w_in", x)
    fetch_mix = prefetch(("w_pa", "w_pb", "w_o"), w_in)

    n_in = w_in.shape[2]

    offs = [0]
    for s in IN_SPLITS:
        offs.append(offs[-1] + s)

    def proj_in_fwd(xv, g, w):
        hv = _rms(xv, g)
        proj = jnp.concatenate([_raw_dot(hv, w[j], "nn") for j in range(N_DEV)], axis=1)
        parts = [proj[:, offs[i]:offs[i + 1]] for i in range(len(IN_SPLITS))]
        parts[4] = jnp.concatenate([parts[4], jnp.zeros((parts[4].shape[0], LANES - LOWRANK), F32)], axis=1)
        return (hv, *parts), ()

    widths = [LANES if s == LOWRANK else s for s in IN_SPLITS]
    h, q_a, k_a, v_a, g_a, a_low_p, x_m, o_pre, gate_a, gate_b = _rowwise(
        "proj_in", proj_in_fwd, [x], [g1, w_in], [(d, BF16)] + [(wd, F32) for wd in widths], deps=dep(fetch_mix))

    w_a_up_p = jnp.pad(ws["w_a_up"], ((0, LANES - LOWRANK), (0, 0)))
    b_a_up = ws["b_a_up"]
    (la,) = _rowwise("gla_decay", lambda al, w, b: ((_log_decay(al, w, b),), ()), [a_low_p], [w_a_up_p, b_a_up],
                     [(HEADS * GLA_DK, F32)])
    fetch_up = prefetch(("w_up",), la)
    q_hm, k_hm, la_hm = _to_hm(q_a, GLA_DK), _to_hm(k_a, GLA_DK), _to_hm(la, GLA_DK)
    o_gla, s_prev = _gla_fwd(q_hm, k_hm, v_a, la_hm, deps=dep(fetch_up))
    pass_mix = pass_on("w_pa", o_gla)
    gn = ws["g_gla_norm"]
    ml_w = HEADS * HEAD_W

    cw = ws["conv_w"]
    w_if_p = jnp.pad(ws["w_if"], ((0, 0), (0, LANES - 2 * HEADS)))
    pre_params = [cw[0:1], cw[1:2], cw[2:3], cw[3:4], ws["conv_b"],
                  _blockdiag_dense(ws["w_q_ml"]), _blockdiag_dense(ws["w_k_ml"]), _blockdiag_dense(ws["w_v_ml"]),
                  w_if_p[0:ml_w], w_if_p[ml_w:2 * ml_w], w_if_p[2 * ml_w:3 * ml_w],
                  jnp.pad(ws["b_if"], ((0, 0), (0, LANES - 2 * HEADS)))]
    x_pad = jnp.pad(x_m, ((HALO, 0), (0, 0)))
    xc, q_m, k_m, v_m, gl = _ml_pre_fwd(x_m, x_pad, pre_params, deps=dep(pass_mix))
    li, lf = _gate_rows(gl[:, 0:HEADS]), _gate_rows(gl[:, HEADS:2 * HEADS])
    hc, c_prev, n_prev, m_prev = _ml_fwd(q_m, k_m, v_m, li, lf)
    fetch_down = prefetch(("w_down",), hc)
    g_ml, skip = ws["g_ml_norm"], ws["ml_skip"]

    def proj_a_fwd(o, g, n_, w):
        ya = jnp.concatenate(_per_head(_gla_out, [o, g], [], [n_]), axis=1)
        return (ya, _raw_dot(ya, w, "nn")), ()

    ya_in, y_a = _rowwise("proj_a", proj_a_fwd, [o_gla, g_a], [gn, weight("w_pa", hc)], [(ml_w, BF16), (d, F32)],
                          tile=512)

    def proj_b_fwd(a, b, c_, ga, gb, ya, g, s, w):
        hb = jnp.concatenate(_per_head(_ml_out, [a, b, c_], [g, s]), axis=1)
        yb = _raw_dot(hb, w, "nn")
        return (hb, yb, _merge(ga, gb, ya, yb)), ()

    h_b, y_b, merged = _rowwise("proj_b", proj_b_fwd, [hc, o_pre, xc, gate_a, gate_b, y_a], [g_ml, skip, weight("w_pb", hc)],
                                [(ml_w, BF16), (d, F32), (d, BF16)], tile=512, deps=dep(fetch_down))

    gpm, gpl, gpo = ws["g_post_mix"], ws["g_pre_mlp"], ws["g_post_mlp"]

    def proj_o_fwd(mg, xv, w, a, b):
        zv = _raw_dot(mg, w, "nn")
        return (zv, *_post_mix(xv, zv, a, b)), ()

    pass_up = pass_on("w_up", merged)
    z, x1, h2 = _rowwise("proj_o", proj_o_fwd, [merged, x], [weight("w_o", merged), gpm, gpl],
                         [(d, F32), (d, F32), (d, BF16)], tile=512, deps=dep(pass_up))
    pass_down = pass_on("w_down", h2)
    w_up = weight("w_up", h2)
    up, u = _mm(h2, w_up, "nn", (BF16, BF16), "mlp_up", tm=2048, shards="b", deps=dep(pass_down),
                epilogue=lambda p: (p, jnp.square(jnp.maximum(p, 0.0))))

    def mlp_down_loss(uv, x1v, tgtv, w, g):
        dnv = _raw_dot(uv, w, "nn")
        loss, vjp = jax.vjp(lambda a, b, c_: _loss_rows(a, b, tgtv, c_), x1v, dnv, g)
        dx1, ddn, dg = vjp(jnp.ones((1, 1), F32))
        return (dx1, ddn), (jnp.broadcast_to(loss, (1, LANES)), dg)

    dx1_y, d_dn, loss, d_gpo = _rowwise("mlp_down", mlp_down_loss, [u, x1, tgt], [weight("w_down", u), gpo],
                                        [(d, F32), (d, BF16)], [((1, LANES), F32), ((1, d), F32)], tile=512)

    (d_up,) = _mm(d_dn, weight("w_down", u), "nt", (BF16,), "mlp_down_dx", extra=[up],
                  epilogue=lambda p, a: (p * (2.0 * jnp.maximum(a.astype(F32), 0.0)),))
    dw_down = _mm(u, d_dn, "tn", BF16, "mlp_down_dw", tm=512)
    dw_up = _mm(h2, d_up, "tn", BF16, "mlp_up_dw", tn=w_up.shape[2], shards="out")
    sent_mlp = on_grads(dict(w_down=dw_down, w_up=dw_up))

    def mlp_up_dx(dup, xv, zv, dx1, w, a, b):
        _, vjp = jax.vjp(_post_mix, xv, zv, a, b)
        ns = w.shape[2]
        dh2 = sum(_raw_dot(dup[:, j * ns:(j + 1) * ns], w[j], "nt") for j in range(w.shape[0]))
        dx, dz, da, db = vjp((dx1, dh2))
        return (dx, dz), (da, db)

    dx_res, d_z, d_gpm, d_gpl = _rowwise("mlp_up_dx", mlp_up_dx, [d_up, x, z, dx1_y], [w_up, gpm, gpl],
                                         [(d, F32), (d, BF16)], [((1, d), F32), ((1, d), F32)], tile=512, deps=dep(sent_mlp))
    dw_o = _mm(merged, d_z, "tn", BF16, "proj_o_dw")

    def proj_o_dx(dz, ga, gb, ya, yb, w):
        return jax.vjp(_merge, ga, gb, ya, yb)[1](_raw_dot(dz, w, "nt")), ()

    d_ga, d_gb, d_ya, d_yb = _rowwise("proj_o_dx", proj_o_dx, [d_z, gate_a, gate_b, y_a, y_b], [weight("w_o", merged)],
                                      [(d, BF16)] * 4, tile=512)
    dw_pa = _mm(ya_in, d_ya, "tn", BF16, "proj_a_dw")
    dw_pb = _mm(h_b, d_yb, "tn", BF16, "proj_b_dw")
    sent_mix = on_grads(dict(w_o=dw_o, w_pa=dw_pa, w_pb=dw_pb))

    def proj_b_dx(dyb, a, b, c_, w, g, s):
        ct = _raw_dot(dyb, w, "nt")
        parts = []
        for hs in _head_slices(HEAD_W):
            _, vjp = jax.vjp(_ml_out, a[:, hs], b[:, hs], c_[:, hs], g[:, hs], s[:, hs])
            parts.append(vjp(ct[:, hs]))
        cat = lambda i: jnp.concatenate([p[i] for p in parts], axis=1)
        return (cat(0), cat(1), cat(2)), (cat(3), cat(4))

    d_hc, d_opre, d_xc, d_gml, d_skip = _rowwise("proj_b_dx", proj_b_dx, [d_yb, hc, o_pre, xc],
                                                 [weight("w_pb", hc), g_ml, skip], [(ml_w, F32), (ml_w, BF16), (ml_w, F32)],
                                                 [((1, ml_w), F32)] * 2,
                                                 tile=512, deps=dep(sent_mix))
    d_qm, d_km, d_vm, d_li, d_lf = _ml_bwd(q_m, k_m, v_m, li, lf, c_prev, n_prev, m_prev, d_hc)
    d_gl = jnp.concatenate([_gate_cols(d_li), _gate_cols(d_lf), jnp.zeros((t, LANES - 2 * HEADS), F32)], axis=1)
    pre_grads = _ml_pre_bwd(x_m, x_pad, pre_params, [d_xc, d_qm, d_km, d_vm, d_gl], tile=512)
    d_xm = pre_grads[0]
    d_cw = jnp.concatenate(pre_grads[1:5], axis=0)
    d_cb = pre_grads[5]
    d_wq, d_wk, d_wv = _blockdiag_blocks(pre_grads[6:9])
    d_wif = jnp.concatenate(pre_grads[9:12], axis=0)[:, 0:2 * HEADS]
    d_bif = pre_grads[12][:, 0:2 * HEADS]

    def proj_a_dx(dya, o, g, w, n_):
        ct = _raw_dot(dya, w, "nt")
        parts = []
        for hs in _head_slices(HEAD_W):
            _, vjp = jax.vjp(_gla_out, o[:, hs], g[:, hs], n_)
            parts.append(vjp(ct[:, hs]))
        cat = lambda i: jnp.concatenate([p[i] for p in parts], axis=1)
        return (cat(0), cat(1)), (sum(p[2] for p in parts),)

    d_o, d_g_a, d_gn = _rowwise("proj_a_dx", proj_a_dx, [d_ya, o_gla, g_a], [weight("w_pa", o_gla), gn],
                                [(ml_w, F32), (ml_w, BF16)],
                                [((1, HEAD_W), F32)], tile=512)
    dq_hm, dk_hm, d_va, dla_hm = _gla_bwd(q_hm, k_hm, v_a, la_hm, s_prev, d_o)

    def decay_bwd(al, ct, w, b):
        _, vjp = jax.vjp(_log_decay, al, w, b)
        dal, dw, db = vjp(ct)
        return (dal,), (dw, db)

    d_alow_p, d_wa_p, d_ba = _rowwise("gla_decay_bwd", decay_bwd, [a_low_p, _from_hm(dla_hm)], [w_a_up_p, b_a_up],
                                      [(LANES, BF16)], [(w_a_up_p.shape, F32), (b_a_up.shape, F32)])
    d_proj = jnp.concatenate([_from_hm(dq_hm), _from_hm(dk_hm), d_va, d_g_a, d_alow_p[:, 0:LOWRANK], d_xm, d_opre, d_ga, d_gb],
                             axis=1).astype(BF16)
    small = dict(w_a_up=d_wa_p[0:LOWRANK], b_a_up=d_ba, g_gla_norm=d_gn, conv_w=d_cw, conv_b=d_cb,
                 w_q_ml=d_wq, w_k_ml=d_wk, w_v_ml=d_wv, w_if=d_wif, b_if=d_bif, ml_skip=d_skip, g_ml_norm=d_gml,
                 g_post_mix=d_gpm, g_pre_mlp=d_gpl, g_post_mlp=d_gpo)
    sent_small = on_small(small, loss)
    sent_in = sent_small
    for half in range(2):
        dw_half = _mm_shard_cols(h, d_proj, n_in, "proj_in_dw_%d" % half, half, d // 2, deps=dep(sent_in))
        sent_in = on_grads({"w_in#%d" % half: dw_half})

    def proj_in_dx(dp, xv, dres, w, g):
        _, vjp = jax.vjp(_rms, xv, g)
        dx, dg = vjp(sum(_raw_dot(dp[:, j * n_in:(j + 1) * n_in], w[j], "nt") for j in range(N_DEV)))
        return (dx + dres,), (dg,)

    grad_x, d_g1 = _rowwise("proj_in_dx", proj_in_dx, [d_proj, x, dx_res], [w_in, g1], [(d, F32)], [((1, d), F32)],
                            deps=dep(sent_in))
    return grad_x, on_small(dict(g_pre_mix=d_g1), None)


BIG = ("w_in", "w_pa", "w_pb", "w_o", "w_up", "w_down")
BIG_COL_SHARDED = ("w_in", "w_pa", "w_pb", "w_up")
SMALL_SHARDED = ("w_a_up", "conv_w", "w_if")
SMALL = ("g_pre_mix", "w_a_up", "b_a_up", "g_gla_norm", "conv_w", "conv_b", "w_q_ml", "w_k_ml", "w_v_ml", "w_if", "b_if",
         "ml_skip", "g_ml_norm", "g_post_mix", "g_pre_mlp", "g_post_mlp")
WEIGHTS = ("g_pre_mix", "w_in", "w_a_up", "b_a_up", "g_gla_norm", "conv_w", "conv_b", "w_q_ml", "w_k_ml", "w_v_ml", "w_if", "b_if",
           "ml_skip", "g_ml_norm", "w_pa", "w_pb", "w_o", "g_post_mix", "g_pre_mlp", "w_up", "w_down", "g_post_mlp")


def kernel(x, g_pre_mix, w_in, w_a_up, b_a_up, g_gla_norm, conv_w, conv_b, w_q_ml, w_k_ml, w_v_ml, w_if, b_if, ml_skip, g_ml_norm, w_pa, w_pb, w_o, g_post_mix, g_pre_mlp, w_up, w_down, g_post_mlp, loss_target, m_g_pre_mix, m_w_in, m_w_a_up, m_b_a_up, m_g_gla_norm, m_conv_w, m_conv_b, m_w_q_ml, m_w_k_ml, m_w_v_ml, m_w_if, m_b_if, m_ml_skip, m_g_ml_norm, m_w_pa, m_w_pb, m_w_o, m_g_post_mix, m_g_pre_mlp, m_w_up, m_w_down, m_g_post_mlp, v_g_pre_mix, v_w_in, v_w_a_up, v_b_a_up, v_g_gla_norm, v_conv_w, v_conv_b, v_w_q_ml, v_w_k_ml, v_w_v_ml, v_w_if, v_b_if, v_ml_skip, v_g_ml_norm, v_w_pa, v_w_pb, v_w_o, v_g_post_mix, v_g_pre_mlp, v_w_up, v_w_down, v_g_post_mlp):
    args = dict(locals())
    w = {n: args[n][0] for n in WEIGHTS}
    m = {n: args["m_" + n][0] for n in WEIGHTS}
    v = {n: args["v_" + n][0] for n in WEIGHTS}

    me_lin = _lin(_me())
    me_idx = jnp.reshape(me_lin, (1,)).astype(jnp.int32)

    def full_weight(n, g):
        if n in ("w_in", "w_up"):
            return g
        return _from_col_blocks(g) if n in BIG_COL_SHARDED else g.reshape(-1, g.shape[-1])

    def grad_parts(n, g):
        if n.partition("#")[0] in ("w_in", "w_up"):
            return g
        return (_col_blocks(g) if n in BIG_COL_SHARDED else g.reshape(N_DEV, -1, g.shape[-1])).astype(BF16)

    sharded_names = tuple(SMALL_SHARDED)
    narrow = {n: w[n].astype(BF16) for n in BIG}
    ready, pending, passing = {}, {}, {}
    first_state, _ = _copies_start(["gather"] * len(sharded_names) + ["gather_chips"],
                                   [_small_view(n, w[n]) for n in sharded_names] + [narrow["w_in"]], "allgather_start_first")

    def prefetch(group, after):
        state, token = _copies_start("gather_chips", [narrow[n] for n in group], "allgather_start_" + group[0], after)
        for n in group:
            pending[n] = (group, state)
        return token

    def pass_on(n, after):
        group, state = pending[n]
        shards, lands = _copies_wait(state, after, "allgather_wait_" + group[0])
        state, token = _copies_start("gather_pass", shards, "allgather_pass_" + group[0], lands=lands)
        for gn in group:
            passing[gn] = (group, state)
        return token

    def weight(n, after):
        if n not in ready:
            group, state = passing[n]
            shards, lands = _copies_wait(state, after, "allgather_passed_" + group[0])
            for gn, shard, land in zip(group, shards, lands):
                ready[gn] = full_weight(gn, lax.dynamic_update_slice(land, shard[None], (me_lin, 0, 0)))
        return ready[n]

    first_own, first_lands = _copies_wait(first_state, [narrow[n] for n in BIG if n != "w_in"], "allgather_wait_first")
    state, _ = _copies_start("gather_pass", first_own[-1:], "allgather_pass_w_in", lands=first_lands[-1:])
    passing["w_in"] = (("w_in",), state)
    ws = {n: (w[n].reshape(1, -1) if w[n].ndim == 1 else w[n]) for n in SMALL if n not in SMALL_SHARDED}
    for n, own, land in zip(sharded_names, first_own, first_lands):
        ws[n] = _small_unshard(n, lax.dynamic_update_slice(land, own[None], (me_lin, 0, 0)))

    sent = []

    def on_grads(grads):
        names = tuple(grads)
        state, token = _copies_start("exchange", [grad_parts(n, grads[n]) for n in names],
                                     "exchange_start_" + names[0].replace("#", "_"))
        sent.append((names, state))
        return token

    small_sent = []

    def on_small(small, loss):
        names = tuple(small)
        kinds = ["exchange" if n in SMALL_SHARDED else "gather" for n in names]
        srcs = [_small_shards(n, small[n]) if n in SMALL_SHARDED else _small_view(n, small[n]) for n in names]
        extra = [] if loss is None else [loss]
        state, token = _copies_start(kinds + ["gather"] * len(extra), srcs + extra, "allgather_start_small_" + names[0])
        small_sent.append((names, kinds, state))
        return token

    grad_x, last_token = _local_step(x[0], loss_target[0], weight, ws, prefetch, pass_on, on_grads, on_small)

    out = {}

    chunks = {}

    def finish(names, state, after):
        parts, lands = _copies_wait(state, after, "exchange_wait_" + names[0].replace("#", "_"))
        for name, part, land in zip(names, parts, lands):
            n, _, chunk = name.partition("#")
            chunks.setdefault(n, []).append((land, part))
            if chunk in ("", "1"):
                got_lands, got_parts = zip(*chunks[n])
                out[n] = _sum_adamw(got_lands, got_parts, me_idx, w[n], m[n], v[n], "adamw_" + n)

    def finish_small(names, kinds, state, after):
        own, lands = _copies_wait(state, after, "allgather_wait_small_" + names[0])
        k = len(names)
        upd = _small_update("adamw_small_" + names[0], me_idx, kinds, lands[:k], own[:k],
                            *[[_small_view(n, d[n]) for n in names] for d in (w, m, v)], sums=list(zip(lands[k:], own[k:])))
        for i, n in enumerate(names):
            out[n] = tuple(_small_unview(n, a, w[n].shape) for a in upd[4 * i:4 * i + 4])
        return upd[4 * k:]

    (loss_sum,) = finish_small(*small_sent[0], [grad_x, last_token])
    for names, state in sent[:-2]:
        finish(names, state, [grad_x, last_token])
    finish(*sent[-2], [loss_sum] + [out[n][1] for n in BIG if n in out])
    finish(*sent[-1], [loss_sum])
    finish_small(*small_sent[1], [out["w_in"][1]])

    shaped = lambda a, n: a.reshape(args[n].shape)
    return (loss_sum[0, 0], grad_x[None],
            *[shaped(out[n][0], n) for n in WEIGHTS], *[shaped(out[n][1], n) for n in WEIGHTS],
            *[shaped(out[n][2], n) for n in WEIGHTS], *[shaped(out[n][3], n) for n in WEIGHTS])
```

```python
import functools

import jax
import jax.numpy as jnp
from jax import lax
from jax.experimental import pallas as pl
from jax.experimental.pallas import tpu as pltpu

F32 = jnp.float32
BF16 = jnp.bfloat16
MESH = pl.DeviceIdType.MESH

N_DEV = 8
EPS = 1e-6
CHUNK = 64
CHUNKS_PER_STEP = 4
HEADS = 4
GLA_DK = 64
HEAD_W = 128
GLA_GATE_NORM = 16.0
LOWRANK = 16
CONV_K = 4
QKV_BLOCK = 4
LANES = 128
HALO = 8
IN_SPLITS = (256, 256, 512, 512, 16, 512, 512, 1024, 1024)

ADAM_LR = 0.001
ADAM_B1 = 0.9
ADAM_B2 = 0.999
ADAM_EPS = 1e-08
ADAM_WD = 0.01
ADAM_STEP = 10

VMEM_LIMIT = 56 * 1024 * 1024


def _cparams(*sem):
    return pltpu.CompilerParams(dimension_semantics=sem, vmem_limit_bytes=VMEM_LIMIT)


def _dims(mode, ndim):
    contract = {"nn": ((ndim - 1,), (ndim - 2,)), "nt": ((ndim - 1,), (ndim - 1,)), "tn": ((ndim - 2,), (ndim - 2,))}[mode]
    return contract, (((0,), (0,)) if ndim == 3 else ((), ()))


def _raw_dot(a, b, mode):
    return lax.dot_general(a.astype(BF16), b.astype(BF16), _dims(mode, a.ndim), preferred_element_type=F32)


@functools.partial(jax.custom_vjp, nondiff_argnums=(2,))
def _bdot(a, b, mode):
    return _raw_dot(a, b, mode)


def _bdot_fwd(a, b, mode):
    return _raw_dot(a, b, mode), (a, b)


def _bdot_bwd(mode, res, ct):
    a, b = res
    if mode == "nn":
        da, db = _raw_dot(ct, b, "nt"), _raw_dot(a, ct, "tn")
    elif mode == "nt":
        da, db = _raw_dot(ct, b, "nn"), _raw_dot(ct, a, "tn")
    else:
        da, db = _raw_dot(b, ct, "nt"), _raw_dot(a, ct, "nn")
    return da.astype(a.dtype), db.astype(b.dtype)


_bdot.defvjp(_bdot_fwd, _bdot_bwd)


def _split3(x):
    hi = x.astype(BF16)
    r1 = x - hi.astype(F32)
    mid = r1.astype(BF16)
    return hi, mid, (r1 - mid.astype(F32)).astype(BF16)


def _split_dot(tri, x):
    if x.ndim == 3:
        tri = jnp.broadcast_to(tri, (x.shape[0], *tri.shape))
    return sum(lax.dot_general(tri, t, _dims("nn", x.ndim), preferred_element_type=F32) for t in _split3(x))


def _tri(n, lower):
    r = lax.broadcasted_iota(jnp.int32, (n, n), 0)
    c = lax.broadcasted_iota(jnp.int32, (n, n), 1)
    return ((c <= r) if lower else (c >= r)).astype(BF16)


@jax.custom_vjp
def _cumsum_rows(x):
    return _split_dot(_tri(x.shape[-2], True), x)


def _cumsum_rows_fwd(x):
    return _cumsum_rows(x), None


def _cumsum_rows_bwd(_, ct):
    return (_split_dot(_tri(ct.shape[-2], False), ct),)


_cumsum_rows.defvjp(_cumsum_rows_fwd, _cumsum_rows_bwd)


def _abs(x):
    return jnp.where(x >= 0, x, -x)


def _sigmoid(x):
    return lax.logistic(x)


def _log_sigmoid(x):
    return jnp.minimum(x, 0.0) - jnp.log(1.0 + jnp.exp(-_abs(x)))


def _rms(x, g):
    return x * lax.rsqrt(jnp.mean(x * x, axis=-1, keepdims=True) + EPS) * g


def _head_slices(w):
    return [slice(h * w, (h + 1) * w) for h in range(HEADS)]


def _heads(ref, rows=slice(None)):
    return jnp.stack([ref[rows, hs] for hs in _head_slices(HEAD_W)])


def _put_heads(ref, val, rows=slice(None)):
    for h, hs in enumerate(_head_slices(HEAD_W)):
        ref[rows, hs] = val[h].astype(ref.dtype)


def _tile(dim, want):
    if dim <= want or dim % LANES:
        return dim
    t = want
    while dim % t:
        t -= LANES
    return t


def _mm(a, b, mode, out_dtype, name, tm=1024, tn=1024, tk=4096, epilogue=None, extra=(), deps=(), shards=None):
    if shards == "b":
        assert mode == "nn"
        ns = b.shape[2]
        (m, k), (k2, n) = a.shape, (b.shape[1], b.shape[0] * ns)
        tn = ns
    elif mode == "nn":
        (m, k), (k2, n) = a.shape, b.shape
    elif mode == "nt":
        (m, k), (n, k2) = a.shape, b.shape
    else:
        (k, m), (k2, n) = a.shape, b.shape
    assert k == k2, (name, a.shape, b.shape)
    tm, tn, tk = _tile(m, tm), _tile(n, tn), _tile(k, tk)
    nk = k // tk
    out_dtypes = out_dtype if epilogue else (out_dtype,)
    assert nk == 1 or (out_dtype == F32 and not epilogue), name
    n_in = 2 + len(extra)

    def body(*refs):
        p = _raw_dot(refs[0][...], refs[1][...], mode)
        if nk > 1:
            _accumulate(pl.program_id(2), [refs[n_in + len(deps)]], [p])
            return
        outs = epilogue(p, *[r[...] for r in refs[2:n_in]]) if epilogue else (p,)
        for ref, val in zip(refs[n_in + len(deps):], outs):
            ref[...] = val.astype(ref.dtype)

    a_spec = pl.BlockSpec((tk, tm), lambda i, j, kk: (kk, i)) if mode == "tn" else pl.BlockSpec((tm, tk), lambda i, j, kk: (i, kk))
    if shards == "b":
        b_spec = pl.BlockSpec((None, tk, tn), lambda i, j, kk: (j, kk, 0))
    elif mode == "nt":
        b_spec = pl.BlockSpec((tn, tk), lambda i, j, kk: (j, kk))
    else:
        b_spec = pl.BlockSpec((tk, tn), lambda i, j, kk: (kk, j))
    o_spec = pl.BlockSpec((tm, tn), lambda i, j, kk: (i, j))
    if shards == "out":
        out_specs, out_shape = [pl.BlockSpec((None, tm, tn), lambda i, j, kk: (j, i, 0))], (n // tn, m, tn)
    else:
        out_specs, out_shape = [o_spec] * len(out_dtypes), (m, n)
    res = pl.pallas_call(
        body, name=name, grid=(m // tm, n // tn, nk),
        in_specs=[a_spec, b_spec] + [o_spec] * len(extra) + [ANY] * len(deps), out_specs=out_specs,
        out_shape=[jax.ShapeDtypeStruct(out_shape, dt) for dt in out_dtypes],
        compiler_params=_cparams("parallel", "parallel", "arbitrary"),
    )(a, b, *extra, *deps)
    return res if epilogue else res[0]


def _mm_shard_cols(a, b, n, name, row_tile, tm, deps=()):
    t = a.shape[0]
    nb = b.shape[1] // n

    def body(a_ref, b_ref, *rest):
        o_ref, at_ref = rest[len(deps):]
        j = pl.program_id(0)

        @pl.when(j == 0)
        def _():
            at_ref[...] = a_ref[...].astype(BF16).T

        for s in range(nb):
            @pl.when(j == s)
            def _(s=s):
                o_ref[...] = _raw_dot(at_ref[...], b_ref[:, s * n:(s + 1) * n], "nn").astype(BF16)

    return pl.pallas_call(
        body, name=name, grid=(nb,),
        in_specs=[pl.BlockSpec((t, tm), lambda j: (0, row_tile)),
                  pl.BlockSpec((t, nb * n), lambda j: (0, 0), pipeline_mode=pl.Buffered(1))] + [ANY] * len(deps),
        out_specs=pl.BlockSpec((None, tm, n), lambda j: (j, 0, 0)),
        out_shape=jax.ShapeDtypeStruct((nb, tm, n), BF16),
        scratch_shapes=[pltpu.VMEM((tm, t), BF16)],
        compiler_params=_cparams("arbitrary"),
    )(a, b, *deps)


def _rowwise(name, fn, rows, params, out_rows, out_accs=(), tile=256, deps=()):
    t = rows[0].shape[0]
    r = min(tile, t)
    assert t % r == 0
    n_in, n_or = len(rows) + len(params), len(out_rows)
    n_all = n_in + len(deps)
    params = list(params) + list(deps)

    def body(*refs):
        vals = [ref[...] for ref in refs[:n_in]]
        outs = refs[n_all:]
        ro, ao = fn(*vals)
        for ref, v in zip(outs[:n_or], ro):
            ref[...] = v.astype(ref.dtype)
        if out_accs:
            _accumulate(pl.program_id(0), outs[n_or:], ao)

    def full(shape):
        return pl.BlockSpec(shape, lambda i, nd=len(shape): (0,) * nd)

    return pl.pallas_call(
        body, name=name, grid=(t // r,),
        in_specs=[pl.BlockSpec((r, a.shape[1]), lambda i: (i, 0)) for a in rows] + [full(p.shape) for p in params],
        out_specs=[pl.BlockSpec((r, w), lambda i: (i, 0)) for w, _ in out_rows] + [full(s) for s, _ in out_accs],
        out_shape=[jax.ShapeDtypeStruct((t, w), dt) for w, dt in out_rows] + [jax.ShapeDtypeStruct(s, dt) for s, dt in out_accs],
        compiler_params=_cparams("arbitrary"),
    )(*rows, *params)


def _accumulate(step, refs, vals):
    for ref, v in zip(refs, vals):
        @pl.when(step == 0)
        def _(ref=ref, v=v):
            ref[...] = v.astype(ref.dtype)

        @pl.when(step > 0)
        def _(ref=ref, v=v):
            ref[...] += v.astype(ref.dtype)


def _gla_chunk(q, k, v, la, st):
    c = q.shape[-2]
    row = lax.broadcasted_iota(jnp.int32, (c, c), 0)
    col = lax.broadcasted_iota(jnp.int32, (c, c), 1)
    cum = _cumsum_rows(la)
    cl = jnp.sum(la, axis=-2, keepdims=True)
    ep = jnp.exp(cum)
    en = jnp.exp(-cum)
    qs = q * (GLA_DK ** -0.5)
    qp = qs * ep
    a_f = _bdot(qp, k * en, "nt")
    a_b = _bdot(qs * en, k * ep, "nt")
    sc = jnp.where(row >= col, a_f, a_b)
    o = _bdot(sc, v, "nn") + _bdot(qp, st, "nt")
    kd = k * jnp.exp(cl - cum)
    st_new = st * jnp.exp(cl) + _bdot(v, kd, "tn")
    return o, st_new


def _gla_specs(nc, rev):
    nb = nc // CHUNKS_PER_STEP
    rows = CHUNKS_PER_STEP * CHUNK

    def blk(n):
        return (nb - 1 - n) if rev else n
    hm = pl.BlockSpec((HEADS, rows, GLA_DK), lambda n: (0, blk(n), 0))
    tm = pl.BlockSpec((rows, HEADS * HEAD_W), lambda n: (blk(n), 0))
    st = pl.BlockSpec((HEADS, CHUNKS_PER_STEP, HEAD_W, GLA_DK), lambda n: (0, blk(n), 0, 0))
    return nb, hm, tm, st


def _chunk_rows(c):
    return slice(c * CHUNK, (c + 1) * CHUNK)


def _gla_fwd(q, k, v, la, deps=()):
    t = v.shape[0]
    nc = t // CHUNK
    nb, hm, tm, st = _gla_specs(nc, False)

    def body(q_ref, k_ref, v_ref, la_ref, *rest):
        o_ref, sp_ref, st_ref = rest[len(deps):]

        @pl.when(pl.program_id(0) == 0)
        def _():
            st_ref[...] = jnp.zeros_like(st_ref)

        s = st_ref[...]
        for c in range(CHUNKS_PER_STEP):
            r = _chunk_rows(c)
            sp_ref[:, c] = s
            o, s = _gla_chunk(q_ref[:, r], k_ref[:, r], _heads(v_ref, r), la_ref[:, r], s)
            _put_heads(o_ref, o, r)
        st_ref[...] = s

    return pl.pallas_call(
        body, name="gla_fwd", grid=(nb,),
        in_specs=[hm, hm, tm, hm] + [ANY] * len(deps), out_specs=[tm, st],
        out_shape=[jax.ShapeDtypeStruct((t, HEADS * HEAD_W), F32), jax.ShapeDtypeStruct((HEADS, nc, HEAD_W, GLA_DK), F32)],
        scratch_shapes=[pltpu.VMEM((HEADS, HEAD_W, GLA_DK), F32)],
        compiler_params=_cparams("arbitrary"),
    )(q, k, v, la, *deps)


def _gla_bwd(q, k, v, la, sp, do):
    t = v.shape[0]
    nc = t // CHUNK
    nb, hm, tm, st = _gla_specs(nc, True)

    def body(q_ref, k_ref, v_ref, la_ref, sp_ref, do_ref, dq_ref, dk_ref, dv_ref, dla_ref, ds_ref):
        @pl.when(pl.program_id(0) == 0)
        def _():
            ds_ref[...] = jnp.zeros_like(ds_ref)

        ds = ds_ref[...]
        for c in reversed(range(CHUNKS_PER_STEP)):
            r = _chunk_rows(c)
            _, vjp = jax.vjp(_gla_chunk, q_ref[:, r], k_ref[:, r], _heads(v_ref, r), la_ref[:, r], sp_ref[:, c])
            dq, dk, dv, dla, ds = vjp((_heads(do_ref, r), ds))
            dq_ref[:, r] = dq.astype(dq_ref.dtype)
            dk_ref[:, r] = dk.astype(dk_ref.dtype)
            _put_heads(dv_ref, dv, r)
            dla_ref[:, r] = dla
        ds_ref[...] = ds

    hm_shape = jax.ShapeDtypeStruct((HEADS, t, GLA_DK), BF16)
    return pl.pallas_call(
        body, name="gla_bwd", grid=(nb,),
        in_specs=[hm, hm, tm, hm, st, tm], out_specs=[hm, hm, tm, hm],
        out_shape=[hm_shape, hm_shape, jax.ShapeDtypeStruct((t, HEADS * HEAD_W), BF16),
                   jax.ShapeDtypeStruct((HEADS, t, GLA_DK), F32)],
        scratch_shapes=[pltpu.VMEM((HEADS, HEAD_W, GLA_DK), F32)],
        compiler_params=_cparams("arbitrary"),
    )(q, k, v, la, sp, do)


def _ml_chunk(q, k, v, li_r, lf_r, cm, nv, m):
    c = q.shape[-2]
    row = lax.broadcasted_iota(jnp.int32, (c, c), 0)
    col = lax.broadcasted_iota(jnp.int32, (c, c), 1)
    eye = (row == col).astype(F32)
    li_c = jnp.sum(eye * li_r, axis=-1, keepdims=True)
    lf_c = jnp.sum(eye * lf_r, axis=-1, keepdims=True)
    fc_c = jnp.sum((col <= row).astype(F32) * lf_r, axis=-1, keepdims=True)
    fc_r = jnp.sum((row <= col).astype(F32) * lf_c, axis=-2, keepdims=True)
    f_last = jnp.sum(lf_r, axis=-1, keepdims=True)
    kc = k * (HEAD_W ** -0.5)
    a_c = f_last - fc_c + li_c
    m_loc = jnp.max(a_c, axis=-2, keepdims=True)
    kw = kc * jnp.exp(a_c - m_loc)
    c_chunk = _bdot(kw, v, "tn")
    n_chunk = jnp.sum(kw, axis=-2, keepdims=True)
    m_new = jnp.maximum(f_last + m, m_loc)
    sp = jnp.exp(f_last + m - m_new)
    sl = jnp.exp(m_loc - m_new)
    cm_new = sp * cm + sl * c_chunk
    nv_new = sp * nv + sl * n_chunk
    log_d = li_r - _abs(fc_c - fc_r)
    g_inter = fc_c + m
    m_t = jnp.maximum(g_inter, jnp.max(log_d, axis=-1, keepdims=True))
    s = _bdot(q, kc, "nt") * jnp.exp(log_d - m_t)
    sc = jnp.exp(g_inter - m_t)
    num = _bdot(s, v, "nn") + sc * _bdot(q, cm, "nn")
    den = jnp.sum(s, axis=-1, keepdims=True) + sc * jnp.sum(q * nv, axis=-1, keepdims=True)
    den = jnp.maximum(_abs(den), jnp.exp(-m_t))
    return num / den, cm_new, nv_new, m_new


def _ml_specs(nc, rev):
    nb = nc // CHUNKS_PER_STEP

    def blk(n):
        return (nb - 1 - n) if rev else n
    tm = pl.BlockSpec((CHUNKS_PER_STEP * CHUNK, HEADS * HEAD_W), lambda n: (blk(n), 0))
    gate = pl.BlockSpec((HEADS, CHUNKS_PER_STEP, 1, CHUNK), lambda n: (0, blk(n), 0, 0))
    cm = pl.BlockSpec((HEADS, CHUNKS_PER_STEP, HEAD_W, HEAD_W), lambda n: (0, blk(n), 0, 0))
    vec = pl.BlockSpec((HEADS, CHUNKS_PER_STEP, 1, HEAD_W), lambda n: (0, blk(n), 0, 0))
    return nb, tm, gate, cm, vec


_ML_STATE = [pltpu.VMEM((HEADS, HEAD_W, HEAD_W), F32), pltpu.VMEM((HEADS, 1, HEAD_W), F32), pltpu.VMEM((HEADS, 1, HEAD_W), F32)]


def _ml_fwd(q, k, v, li, lf):
    t = q.shape[0]
    nc = t // CHUNK
    nb, tm, gate, cm, vec = _ml_specs(nc, False)

    def body(q_ref, k_ref, v_ref, li_ref, lf_ref, hc_ref, cp_ref, np_ref, mp_ref, c_ref, n_ref, m_ref):
        @pl.when(pl.program_id(0) == 0)
        def _():
            c_ref[...] = jnp.zeros_like(c_ref)
            n_ref[...] = jnp.zeros_like(n_ref)
            m_ref[...] = jnp.zeros_like(m_ref)

        cs, ns, ms = c_ref[...], n_ref[...], m_ref[...][:, :, 0:1]
        for c in range(CHUNKS_PER_STEP):
            r = _chunk_rows(c)
            cp_ref[:, c] = cs
            np_ref[:, c] = ns
            mp_ref[:, c] = jnp.broadcast_to(ms, m_ref.shape)
            hc, cs, ns, ms = _ml_chunk(_heads(q_ref, r), _heads(k_ref, r), _heads(v_ref, r), li_ref[:, c], lf_ref[:, c],
                                       cs, ns, ms)
            _put_heads(hc_ref, hc, r)
        c_ref[...] = cs
        n_ref[...] = ns
        m_ref[...] = jnp.broadcast_to(ms, m_ref.shape)

    return pl.pallas_call(
        body, name="mlstm_fwd", grid=(nb,),
        in_specs=[tm, tm, tm, gate, gate], out_specs=[tm, cm, vec, vec],
        out_shape=[jax.ShapeDtypeStruct((t, HEADS * HEAD_W), F32), jax.ShapeDtypeStruct((HEADS, nc, HEAD_W, HEAD_W), F32),
                   jax.ShapeDtypeStruct((HEADS, nc, 1, HEAD_W), F32), jax.ShapeDtypeStruct((HEADS, nc, 1, HEAD_W), F32)],
        scratch_shapes=_ML_STATE,
        compiler_params=_cparams("arbitrary"),
    )(q, k, v, li, lf)


def _ml_bwd(q, k, v, li, lf, cp, npv, mp, dhc):
    t = q.shape[0]
    nc = t // CHUNK
    nb, tm, gate, cm, vec = _ml_specs(nc, True)

    def body(q_ref, k_ref, v_ref, li_ref, lf_ref, cp_ref, np_ref, mp_ref, dhc_ref,
             dq_ref, dk_ref, dv_ref, dli_ref, dlf_ref, dc_ref, dn_ref, dm_ref):
        @pl.when(pl.program_id(0) == 0)
        def _():
            dc_ref[...] = jnp.zeros_like(dc_ref)
            dn_ref[...] = jnp.zeros_like(dn_ref)
            dm_ref[...] = jnp.zeros_like(dm_ref)

        dc, dn, dm = dc_ref[...], dn_ref[...], dm_ref[...][:, :, 0:1]
        for c in reversed(range(CHUNKS_PER_STEP)):
            r = _chunk_rows(c)
            _, vjp = jax.vjp(_ml_chunk, _heads(q_ref, r), _heads(k_ref, r), _heads(v_ref, r), li_ref[:, c], lf_ref[:, c],
                             cp_ref[:, c], np_ref[:, c], mp_ref[:, c][:, :, 0:1])
            dq, dk, dv, dli, dlf, dc, dn, dm = vjp((_heads(dhc_ref, r), dc, dn, dm))
            _put_heads(dq_ref, dq, r)
            _put_heads(dk_ref, dk, r)
            _put_heads(dv_ref, dv, r)
            dli_ref[:, c] = dli
            dlf_ref[:, c] = dlf
        dc_ref[...] = dc
        dn_ref[...] = dn
        dm_ref[...] = jnp.broadcast_to(dm, dm_ref.shape)

    tm_shape = jax.ShapeDtypeStruct((t, HEADS * HEAD_W), F32)
    gate_shape = jax.ShapeDtypeStruct((HEADS, nc, 1, CHUNK), F32)
    return pl.pallas_call(
        body, name="mlstm_bwd", grid=(nb,),
        in_specs=[tm, tm, tm, gate, gate, cm, vec, vec, tm], out_specs=[tm, tm, tm, gate, gate],
        out_shape=[tm_shape, tm_shape, tm_shape, gate_shape, gate_shape],
        scratch_shapes=_ML_STATE,
        compiler_params=_cparams("arbitrary"),
    )(q, k, v, li, lf, cp, npv, mp, dhc)


def _ml_pre(s0, s1, s2, s3, cw0, cw1, cw2, cw3, cb, wq, wk, wv, wiq, wik, wiv, bif):
    pre = cb + cw0 * s0 + cw1 * s1 + cw2 * s2 + cw3 * s3
    xc = pre * _sigmoid(pre)
    q = _bdot(xc, wq, "nn")
    k = _bdot(xc, wk, "nn")
    v = _bdot(s3, wv, "nn")
    gates = _bdot(q, wiq, "nn") + _bdot(k, wik, "nn") + _bdot(v, wiv, "nn") + bif
    lane = lax.broadcasted_iota(jnp.int32, gates.shape, 1)
    gl = jnp.where(lane < HEADS, gates, _log_sigmoid(gates))
    return xc, q, k, v, gl


def _delayed(xs_ref, x_ref, halo_ref, r):
    xs_ref[0:HALO, :] = halo_ref[...]
    xs_ref[HALO:HALO + r, :] = x_ref[...]
    return [xs_ref[pl.ds(HALO - (CONV_K - 1) + j, r), :] for j in range(CONV_K)]


def _full_spec(shape):
    return pl.BlockSpec(shape, lambda i, nd=len(shape): (0,) * nd)


def _ml_pre_fwd(x_m, x_pad, params, tile=256, deps=()):
    t, w = x_m.shape
    r = min(tile, t)

    def body(*refs):
        x_ref, halo_ref = refs[:2]
        p = [ref[...] for ref in refs[2:2 + len(params)]]
        outs = refs[2 + len(params) + len(deps):-1]
        res = _ml_pre(*_delayed(refs[-1], x_ref, halo_ref, r), *p)
        for ref, val in zip(outs, res):
            ref[...] = val

    row = pl.BlockSpec((r, w), lambda i: (i, 0))
    return pl.pallas_call(
        body, name="ml_pre_fwd", grid=(t // r,),
        in_specs=[row, pl.BlockSpec((HALO, w), lambda i: (i * (r // HALO), 0))] + [_full_spec(p.shape) for p in params]
        + [ANY] * len(deps),
        out_specs=[row] * 4 + [pl.BlockSpec((r, LANES), lambda i: (i, 0))],
        out_shape=[jax.ShapeDtypeStruct((t, w), F32)] * 4 + [jax.ShapeDtypeStruct((t, LANES), F32)],
        scratch_shapes=[pltpu.VMEM((r + HALO, w), F32)],
        compiler_params=_cparams("arbitrary"),
    )(x_m, x_pad, *params, *deps)


def _ml_pre_bwd(x_m, x_pad, params, cts, tile=256):
    t, w = x_m.shape
    r = min(tile, t)
    nt = t // r
    n_p = len(params)

    def body(*refs):
        x_ref, halo_ref = refs[:2]
        p = [ref[...] for ref in refs[2:2 + n_p]]
        ct = [ref[...] for ref in refs[2 + n_p:7 + n_p]]
        dx_ref = refs[7 + n_p]
        dp_refs = refs[8 + n_p:8 + 2 * n_p]
        xs_ref, ds_ref, carry_ref = refs[8 + 2 * n_p:]
        step = pl.program_id(0)

        @pl.when(step == 0)
        def _():
            ds_ref[...] = jnp.zeros_like(ds_ref)
            carry_ref[...] = jnp.zeros_like(carry_ref)

        _, vjp = jax.vjp(_ml_pre, *_delayed(xs_ref, x_ref, halo_ref, r), *p)
        grads = vjp(tuple(ct))
        for j in range(CONV_K):
            ds_ref[j, HALO:HALO + r, :] = grads[j]
        lead = HALO + CONV_K - 1
        d_tile = sum(ds_ref[j, pl.ds(lead - j, r), :] for j in range(CONV_K))
        d_halo = sum(ds_ref[j, pl.ds(CONV_K - 1 - j, HALO), :] for j in range(CONV_K))
        dx_ref[...] = jnp.concatenate([d_tile[:r - HALO], d_tile[r - HALO:] + carry_ref[...]], axis=0).astype(dx_ref.dtype)
        carry_ref[...] = d_halo
        _accumulate(step, dp_refs, grads[CONV_K:])

    row = pl.BlockSpec((r, w), lambda i: (nt - 1 - i, 0))
    return pl.pallas_call(
        body, name="ml_pre_bwd", grid=(nt,),
        in_specs=[row, pl.BlockSpec((HALO, w), lambda i: ((nt - 1 - i) * (r // HALO), 0))] + [_full_spec(p.shape) for p in params]
        + [row] * 4 + [pl.BlockSpec((r, LANES), lambda i: (nt - 1 - i, 0))],
        out_specs=[row] + [_full_spec(p.shape) for p in params],
        out_shape=[jax.ShapeDtypeStruct((t, w), BF16)] + [jax.ShapeDtypeStruct(p.shape, F32) for p in params],
        scratch_shapes=[pltpu.VMEM((r + HALO, w), F32), pltpu.VMEM((CONV_K, r + 2 * HALO, w), F32), pltpu.VMEM((HALO, w), F32)],
        compiler_params=_cparams("arbitrary"),
    )(x_m, x_pad, *params, *cts)


def _per_head(fn, row_vals, head_params, shared_params=()):
    return [fn(*[a[:, hs] for a in row_vals], *[p[:, hs] for p in head_params], *shared_params) for hs in _head_slices(HEAD_W)]


def _gla_out(o, g, gn):
    return _rms(o, gn) * (g * _sigmoid(g))


def _ml_out(hc, op, xc, g, sk):
    hcell = hc * _sigmoid(op)
    mu = jnp.mean(hcell, axis=-1, keepdims=True)
    d = hcell - mu
    var = jnp.mean(d * d, axis=-1, keepdims=True)
    return d * lax.rsqrt(var + EPS) * g + sk * xc


def _log_decay(al, w, b):
    return _log_sigmoid(_bdot(al, w, "nn") + b) * (1.0 / GLA_GATE_NORM)


def _merge(ga, gb, ya, yb):
    return _sigmoid(ga) * ya + _sigmoid(gb) * yb


def _post_mix(x, z, gpm, gpl):
    x1 = x + _rms(z, gpm)
    return x1, _rms(x1, gpl)


def _loss_rows(x1, dn, tgt, g):
    e = x1 + _rms(dn, g) - tgt
    return 0.5 * jnp.sum(jnp.mean(e * e, axis=-1, keepdims=True), axis=0, keepdims=True)


def _lin(p):
    return 4 * p[0] + 2 * p[1] + p[2]


def _me():
    return lax.axis_index("x"), lax.axis_index("y"), lax.axis_index("c")


def _flip(p, k):
    return tuple((1 - v) if (k >> (2 - i)) & 1 else v for i, v in enumerate(p))


ANY = pl.BlockSpec(memory_space=pl.ANY)


HBM = pl.BlockSpec(memory_space=pltpu.HBM)
SEM = pl.BlockSpec(memory_space=pltpu.SEMAPHORE)
DATAFLOW = pltpu.SideEffectType.DATAFLOW_SIDE_EFFECTING


SIBLING = 1
OTHER_CHIPS = (2, 4, 6)


def _peer_copies(kinds, srcs, lands, send_sems, recv_sems):
    me = _me()
    copies = []
    for a, (kind, src, land) in enumerate(zip(kinds, srcs, lands)):
        masks = {"gather": range(1, N_DEV), "exchange": range(1, N_DEV), "gather_chips": (SIBLING, *OTHER_CHIPS),
                 "gather_pass": OTHER_CHIPS}[kind]
        for k in masks:
            peer = _flip(me, k)
            if kind == "gather_pass":
                block = land.at[_lin(peer)]
                src_ref, dst_ref, target = block, block, _flip(me, SIBLING)
            else:
                src_ref, dst_ref, target = (src.at[_lin(peer)] if kind == "exchange" else src), land.at[_lin(me)], peer
            copies.append(pltpu.make_async_remote_copy(
                src_ref=src_ref, dst_ref=dst_ref, send_sem=send_sems.at[a * 7 + k - 1], recv_sem=recv_sems.at[a * 7 + k - 1],
                device_id=target, device_id_type=MESH))
    return copies


def _copies_start(kind, srcs, name, after=None, lands=None):
    n = len(srcs)
    extra = [] if after is None else [after]
    kind = [kind] * n if isinstance(kind, str) else list(kind)
    land_shapes = [(s.shape if k == "exchange" else (N_DEV, *s.shape)) for k, s in zip(kind, srcs)]
    lands = [lax.empty(ls, s.dtype) for ls, s in zip(land_shapes, srcs)] if lands is None else lands

    def body(*refs):
        sems = refs[2 * n + len(extra):]
        for cp in _peer_copies(kind, refs[:n], refs[n:2 * n], sems[0], sems[1]):
            cp.start()
        refs[-1][...] = jnp.zeros_like(refs[-1])

    def hbm(a):
        return pltpu.with_memory_space_constraint(a, pltpu.HBM)

    out = pl.pallas_call(
        body, name=name,
        out_shape=(pltpu.SemaphoreType.DMA((7 * n,)), pltpu.SemaphoreType.DMA((7 * n,)),
                   *[pltpu.HBM(s.shape, s.dtype) for s in srcs],
                   *[pltpu.HBM(ls, s.dtype) for ls, s in zip(land_shapes, srcs)],
                   jax.ShapeDtypeStruct((8, LANES), F32)),
        in_specs=[HBM] * (2 * n) + [ANY] * len(extra),
        out_specs=(SEM, SEM, *[HBM] * (2 * n), pl.BlockSpec(memory_space=pltpu.VMEM)),
        input_output_aliases={i: 2 + i for i in range(2 * n)},
        compiler_params=pltpu.CompilerParams(has_side_effects=DATAFLOW),
    )(*[hbm(s) for s in srcs], *[hbm(a) for a in lands], *extra)
    return (kind, n, out[:-1]), out[-1]


def _copies_wait(state, after, name):
    kind, n, (send_sems, recv_sems, *thru) = state
    after = list(after) if isinstance(after, (list, tuple)) else [after]

    def body(*refs):
        for cp in _peer_copies(kind, refs[:n], refs[n:2 * n], refs[2 * n], refs[2 * n + 1]):
            cp.wait_send()
            cp.wait_recv()

    out = pl.pallas_call(
        body, name=name,
        out_shape=tuple(pltpu.HBM(t.shape, t.dtype) for t in thru),
        in_specs=[HBM] * (2 * n) + [SEM, SEM] + [ANY] * len(after), out_specs=tuple([HBM] * (2 * n)),
        input_output_aliases={i: i for i in range(2 * n)},
        compiler_params=pltpu.CompilerParams(has_side_effects=DATAFLOW),
    )(*thru, send_sems, recv_sems, *after)
    return out[:n], out[n:]


def _adamw(w, g, m, v):
    m2 = ADAM_B1 * m + (1.0 - ADAM_B1) * g
    v2 = ADAM_B2 * v + (1.0 - ADAM_B2) * (g * g)
    m_hat = m2 / (1.0 - ADAM_B1 ** ADAM_STEP)
    v_hat = v2 / (1.0 - ADAM_B2 ** ADAM_STEP)
    delta = -ADAM_LR * (m_hat / (jnp.sqrt(v_hat) + ADAM_EPS) + ADAM_WD * w)
    return delta, m2, v2


def _sum_adamw(lands, parts, me_idx, w, m, v, name, tile=256):
    r, c = w.shape
    nchunks = len(lands)
    tr = min(tile, r // nchunks)
    per_chunk = r // nchunks // tr
    per = 1 + N_DEV

    def body(me_ref, *refs):
        w_ref, m_ref, v_ref, g_ref, d_ref, m2_ref, v2_ref = refs[nchunks * per:]
        for k in range(nchunks):
            own_ref, slots = refs[k * per], refs[k * per + 1:(k + 1) * per]

            @pl.when(pl.program_id(0) // per_chunk == k)
            def _(own_ref=own_ref, slots=slots):
                own = own_ref[...].astype(F32)
                g = None
                for s in range(N_DEV):
                    term = jnp.where(me_ref[0] == s, own, slots[s][...].astype(F32))
                    g = term if g is None else g + term
                d, m2, v2 = _adamw(w_ref[...], g, m_ref[...], v_ref[...])
                g_ref[...] = g
                d_ref[...] = d
                m2_ref[...] = m2
                v2_ref[...] = v2

    def chunk_specs(k):
        def tile_of(i):
            return jnp.clip(i - k * per_chunk, 0, per_chunk - 1)

        def slot_spec(s):
            return pl.BlockSpec((None, tr, c), lambda i, me: (jnp.where(me[0] == s, (s + 1) % N_DEV, s), tile_of(i), 0))
        return [pl.BlockSpec((None, tr, c), lambda i, me: (me[0], tile_of(i), 0))] + [slot_spec(s) for s in range(N_DEV)]

    row = pl.BlockSpec((tr, c), lambda i, me: (i, 0))
    operands = [a for land, part in zip(lands, parts) for a in (part, *[land] * N_DEV)]
    return pl.pallas_call(
        body, name=name,
        grid_spec=pltpu.PrefetchScalarGridSpec(
            num_scalar_prefetch=1, grid=(r // tr,),
            in_specs=[s for k in range(nchunks) for s in chunk_specs(k)] + [row] * 3,
            out_specs=[row] * 4),
        out_shape=[jax.ShapeDtypeStruct((r, c), F32)] * 4,
        compiler_params=_cparams("parallel"),
    )(me_idx, *operands, w, m, v)


def _small_update(name, me_idx, kinds, lands, owns, ws, ms, vs, sums=()):
    n = len(ws)
    lands, owns = list(lands) + [s[0] for s in sums], list(owns) + [s[1] for s in sums]
    kinds = list(kinds) + ["gather"] * len(sums)
    nl = len(lands)

    def summed(me, land_ref, own):
        g = None
        for s in range(N_DEV):
            term = jnp.where(me == s, own, land_ref[s])
            g = term if g is None else g + term
        return g

    def body(me_ref, *refs):
        land_refs, own_refs = refs[:nl], refs[nl:2 * nl]
        w_refs, m_refs, v_refs = (refs[2 * nl + i * n:2 * nl + (i + 1) * n] for i in range(3))
        outs = refs[2 * nl + 3 * n:]
        me = me_ref[0]
        for i in range(n):
            g = summed(me, land_refs[i], own_refs[i][...])
            d, m2, v2 = _adamw(w_refs[i][...], g, m_refs[i][...], v_refs[i][...])
            for ref, val in zip(outs[4 * i:4 * i + 4], (g, d, m2, v2)):
                ref[...] = val
        for i in range(n, nl):
            outs[4 * n + i - n][...] = summed(me, land_refs[i], own_refs[i][...])

    def whole(shape):
        return pl.BlockSpec(shape, lambda i, me, nd=len(shape): (0,) * nd)

    def own_spec(kind, own):
        if kind == "gather":
            return whole(own.shape)
        return pl.BlockSpec((None, *own.shape[1:]), lambda i, me: (me[0], 0, 0))

    shapes = [w.shape for w in ws]
    out_shapes = [s for s in shapes for _ in range(4)] + [s[1].shape for s in sums]
    return pl.pallas_call(
        body, name=name,
        grid_spec=pltpu.PrefetchScalarGridSpec(
            num_scalar_prefetch=1, grid=(1,),
            in_specs=[whole(a.shape) for a in lands] + [own_spec(k, o) for k, o in zip(kinds, owns)]
            + [whole(s) for s in shapes] * 3,
            out_specs=[whole(s) for s in out_shapes]),
        out_shape=[jax.ShapeDtypeStruct(s, F32) for s in out_shapes],
        compiler_params=_cparams("arbitrary"),
    )(me_idx, *lands, *owns, *ws, *ms, *vs)


def _small_view(n, a):
    if a.ndim == 1:
        return a.reshape(1, -1)
    if a.ndim == 3:
        return a.transpose(1, 2, 0).reshape(QKV_BLOCK * QKV_BLOCK, -1)
    return a.T if n == "w_if" else a


def _small_unview(n, a, shape):
    if len(shape) == 1:
        return a.reshape(shape)
    if len(shape) == 3:
        return a.reshape(QKV_BLOCK, QKV_BLOCK, -1).transpose(2, 0, 1)
    return a.T if n == "w_if" else a


def _small_shards(n, g):
    if n == "w_if":
        return g.reshape(N_DEV, -1, g.shape[1]).transpose(0, 2, 1)
    return g.reshape(g.shape[0], N_DEV, -1).transpose(1, 0, 2)


def _small_unshard(n, s):
    if n == "w_if":
        return s.transpose(0, 2, 1).reshape(-1, s.shape[1])
    return s.transpose(1, 0, 2).reshape(s.shape[1], -1)


def _to_hm(a, d):
    t = a.shape[0]
    return a.reshape(t, HEADS, d).transpose(1, 0, 2)


def _from_hm(a):
    h, t, d = a.shape
    return a.transpose(1, 0, 2).reshape(t, h * d)


def _gate_rows(g):
    t = g.shape[0]
    return g.T.reshape(HEADS, t // CHUNK, 1, CHUNK)


def _gate_cols(g):
    h, nc, _, c = g.shape
    return g.reshape(h, nc * c).T


def _blockdiag_dense(w):
    n = w.shape[0] * QKV_BLOCK
    tiled = jnp.tile(w.reshape(n, QKV_BLOCK), (1, n // QKV_BLOCK))
    r = lax.broadcasted_iota(jnp.int32, (n, n), 0)
    c = lax.broadcasted_iota(jnp.int32, (n, n), 1)
    return jnp.where(r // QKV_BLOCK == c // QKV_BLOCK, tiled, 0.0)


def _blockdiag_blocks(dense):
    n = dense[0].shape[0]
    k = len(dense)

    def body(*refs):
        r = lax.broadcasted_iota(jnp.int32, (n, n), 0)
        c = lax.broadcasted_iota(jnp.int32, (n, n), 1)
        fr = lax.broadcasted_iota(jnp.int32, (n, LANES), 0)
        fc = lax.broadcasted_iota(jnp.int32, (n, LANES), 1)
        fold = ((fr & (QKV_BLOCK - 1)) == fc).astype(BF16)
        for i in range(k):
            kept = jnp.where((r >> 2) == (c >> 2), refs[i][...], 0.0)
            refs[k + i][...] = sum(lax.dot_general(t, fold, _dims("nn", 2), preferred_element_type=F32) for t in _split3(kept))

    out = pl.pallas_call(body, name="blockdiag_blocks", out_shape=[jax.ShapeDtypeStruct((n, LANES), F32)] * k)(*dense)
    return [o[:, 0:QKV_BLOCK].reshape(n // QKV_BLOCK, QKV_BLOCK, QKV_BLOCK) for o in out]


def _col_blocks(w):
    k, n = w.shape
    return w.reshape(k, N_DEV, n // N_DEV).transpose(1, 0, 2)


def _from_col_blocks(g):
    d, k, n = g.shape
    return g.transpose(1, 0, 2).reshape(k, d * n)


def _local_step(x, tgt, weight, ws, prefetch, pass_on, on_grads, on_small):
    t, d = x.shape
    g1 = ws["g_pre_mix"]

    def dep(token):
        return () if token is None else (token,)

    w_in = weight("w_in", x)
    fetch_mix = prefetch(("w_pa", "w_pb", "w_o"), w_in)

    n_in = w_in.shape[2]

    offs = [0]
    for s in IN_SPLITS:
        offs.append(offs[-1] + s)

    def proj_in_fwd(xv, g, w):
        hv = _rms(xv, g)
        proj = jnp.concatenate([_raw_dot(hv, w[j], "nn") for j in range(N_DEV)], axis=1)
        parts = [proj[:, offs[i]:offs[i + 1]] for i in range(len(IN_SPLITS))]
        parts[4] = jnp.concatenate([parts[4], jnp.zeros((parts[4].shape[0], LANES - LOWRANK), F32)], axis=1)
        return (hv, *parts), ()

    widths = [LANES if s == LOWRANK else s for s in IN_SPLITS]
    h, q_a, k_a, v_a, g_a, a_low_p, x_m, o_pre, gate_a, gate_b = _rowwise(
        "proj_in", proj_in_fwd, [x], [g1, w_in], [(d, BF16)] + [(wd, F32) for wd in widths], deps=dep(fetch_mix))

    w_a_up_p = jnp.pad(ws["w_a_up"], ((0, LANES - LOWRANK), (0, 0)))
    b_a_up = ws["b_a_up"]
    (la,) = _rowwise("gla_decay", lambda al, w, b: ((_log_decay(al, w, b),), ()), [a_low_p], [w_a_up_p, b_a_up],
                     [(HEADS * GLA_DK, F32)])
    fetch_up = prefetch(("w_up", "w_down"), la)
    q_hm, k_hm, la_hm = _to_hm(q_a, GLA_DK), _to_hm(k_a, GLA_DK), _to_hm(la, GLA_DK)
    o_gla, s_prev = _gla_fwd(q_hm, k_hm, v_a, la_hm, deps=dep(fetch_up))
    pass_mix = pass_on("w_pa", o_gla)
    gn = ws["g_gla_norm"]
    ml_w = HEADS * HEAD_W

    cw = ws["conv_w"]
    w_if_p = jnp.pad(ws["w_if"], ((0, 0), (0, LANES - 2 * HEADS)))
    pre_params = [cw[0:1], cw[1:2], cw[2:3], cw[3:4], ws["conv_b"],
                  _blockdiag_dense(ws["w_q_ml"]), _blockdiag_dense(ws["w_k_ml"]), _blockdiag_dense(ws["w_v_ml"]),
                  w_if_p[0:ml_w], w_if_p[ml_w:2 * ml_w], w_if_p[2 * ml_w:3 * ml_w],
                  jnp.pad(ws["b_if"], ((0, 0), (0, LANES - 2 * HEADS)))]
    x_pad = jnp.pad(x_m, ((HALO, 0), (0, 0)))
    xc, q_m, k_m, v_m, gl = _ml_pre_fwd(x_m, x_pad, pre_params, deps=dep(pass_mix))
    li, lf = _gate_rows(gl[:, 0:HEADS]), _gate_rows(gl[:, HEADS:2 * HEADS])
    hc, c_prev, n_prev, m_prev = _ml_fwd(q_m, k_m, v_m, li, lf)
    g_ml, skip = ws["g_ml_norm"], ws["ml_skip"]

    def proj_a_fwd(o, g, n_, w):
        ya = jnp.concatenate(_per_head(_gla_out, [o, g], [], [n_]), axis=1)
        return (ya, _raw_dot(ya, w, "nn")), ()

    ya_in, y_a = _rowwise("proj_a", proj_a_fwd, [o_gla, g_a], [gn, weight("w_pa", hc)], [(ml_w, BF16), (d, F32)],
                          tile=512)

    def proj_b_fwd(a, b, c_, ga, gb, ya, g, s, w):
        hb = jnp.concatenate(_per_head(_ml_out, [a, b, c_], [g, s]), axis=1)
        yb = _raw_dot(hb, w, "nn")
        return (hb, yb, _merge(ga, gb, ya, yb)), ()

    h_b, y_b, merged = _rowwise("proj_b", proj_b_fwd, [hc, o_pre, xc, gate_a, gate_b, y_a], [g_ml, skip, weight("w_pb", hc)],
                                [(ml_w, BF16), (d, F32), (d, BF16)], tile=512)

    gpm, gpl, gpo = ws["g_post_mix"], ws["g_pre_mlp"], ws["g_post_mlp"]

    def proj_o_fwd(mg, xv, w, a, b):
        zv = _raw_dot(mg, w, "nn")
        return (zv, *_post_mix(xv, zv, a, b)), ()

    pass_up = pass_on("w_up", merged)
    z, x1, h2 = _rowwise("proj_o", proj_o_fwd, [merged, x], [weight("w_o", merged), gpm, gpl],
                         [(d, F32), (d, F32), (d, BF16)], tile=512, deps=dep(pass_up))
    w_up = weight("w_up", h2)
    up, u = _mm(h2, w_up, "nn", (BF16, BF16), "mlp_up", tm=2048, shards="b",
                epilogue=lambda p: (p, jnp.square(jnp.maximum(p, 0.0))))

    def mlp_down_loss(uv, x1v, tgtv, w, g):
        dnv = _raw_dot(uv, w, "nn")
        loss, vjp = jax.vjp(lambda a, b, c_: _loss_rows(a, b, tgtv, c_), x1v, dnv, g)
        dx1, ddn, dg = vjp(jnp.ones((1, 1), F32))
        return (dx1, ddn), (jnp.broadcast_to(loss, (1, LANES)), dg)

    dx1_y, d_dn, loss, d_gpo = _rowwise("mlp_down", mlp_down_loss, [u, x1, tgt], [weight("w_down", u), gpo],
                                        [(d, F32), (d, BF16)], [((1, LANES), F32), ((1, d), F32)], tile=512)

    (d_up,) = _mm(d_dn, weight("w_down", u), "nt", (BF16,), "mlp_down_dx", extra=[up],
                  epilogue=lambda p, a: (p * (2.0 * jnp.maximum(a.astype(F32), 0.0)),))
    dw_down = _mm(u, d_dn, "tn", BF16, "mlp_down_dw", tm=512)
    dw_up = _mm(h2, d_up, "tn", BF16, "mlp_up_dw", tn=w_up.shape[2], shards="out")

    def mlp_up_dx(dup, xv, zv, dx1, w, a, b):
        _, vjp = jax.vjp(_post_mix, xv, zv, a, b)
        ns = w.shape[2]
        dh2 = sum(_raw_dot(dup[:, j * ns:(j + 1) * ns], w[j], "nt") for j in range(w.shape[0]))
        dx, dz, da, db = vjp((dx1, dh2))
        return (dx, dz), (da, db)

    dx_res, d_z, d_gpm, d_gpl = _rowwise("mlp_up_dx", mlp_up_dx, [d_up, x, z, dx1_y], [w_up, gpm, gpl],
                                         [(d, F32), (d, BF16)], [((1, d), F32), ((1, d), F32)], tile=512)
    dw_o = _mm(merged, d_z, "tn", BF16, "proj_o_dw")

    def proj_o_dx(dz, ga, gb, ya, yb, w):
        return jax.vjp(_merge, ga, gb, ya, yb)[1](_raw_dot(dz, w, "nt")), ()

    d_ga, d_gb, d_ya, d_yb = _rowwise("proj_o_dx", proj_o_dx, [d_z, gate_a, gate_b, y_a, y_b], [weight("w_o", merged)],
                                      [(d, BF16)] * 4, tile=512)
    dw_pa = _mm(ya_in, d_ya, "tn", BF16, "proj_a_dw")
    dw_pb = _mm(h_b, d_yb, "tn", BF16, "proj_b_dw")
    sent_mix = on_grads(dict(w_down=dw_down, w_up=dw_up, w_o=dw_o, w_pa=dw_pa, w_pb=dw_pb))

    def proj_b_dx(dyb, a, b, c_, w, g, s):
        ct = _raw_dot(dyb, w, "nt")
        parts = []
        for hs in _head_slices(HEAD_W):
            _, vjp = jax.vjp(_ml_out, a[:, hs], b[:, hs], c_[:, hs], g[:, hs], s[:, hs])
            parts.append(vjp(ct[:, hs]))
        cat = lambda i: jnp.concatenate([p[i] for p in parts], axis=1)
        return (cat(0), cat(1), cat(2)), (cat(3), cat(4))

    d_hc, d_opre, d_xc, d_gml, d_skip = _rowwise("proj_b_dx", proj_b_dx, [d_yb, hc, o_pre, xc],
                                                 [weight("w_pb", hc), g_ml, skip], [(ml_w, F32), (ml_w, BF16), (ml_w, F32)],
                                                 [((1, ml_w), F32)] * 2,
                                                 tile=512, deps=dep(sent_mix))
    d_qm, d_km, d_vm, d_li, d_lf = _ml_bwd(q_m, k_m, v_m, li, lf, c_prev, n_prev, m_prev, d_hc)
    d_gl = jnp.concatenate([_gate_cols(d_li), _gate_cols(d_lf), jnp.zeros((t, LANES - 2 * HEADS), F32)], axis=1)
    pre_grads = _ml_pre_bwd(x_m, x_pad, pre_params, [d_xc, d_qm, d_km, d_vm, d_gl], tile=512)
    d_xm = pre_grads[0]
    d_cw = jnp.concatenate(pre_grads[1:5], axis=0)
    d_cb = pre_grads[5]
    d_wq, d_wk, d_wv = _blockdiag_blocks(pre_grads[6:9])
    d_wif = jnp.concatenate(pre_grads[9:12], axis=0)[:, 0:2 * HEADS]
    d_bif = pre_grads[12][:, 0:2 * HEADS]

    def proj_a_dx(dya, o, g, w, n_):
        ct = _raw_dot(dya, w, "nt")
        parts = []
        for hs in _head_slices(HEAD_W):
            _, vjp = jax.vjp(_gla_out, o[:, hs], g[:, hs], n_)
            parts.append(vjp(ct[:, hs]))
        cat = lambda i: jnp.concatenate([p[i] for p in parts], axis=1)
        return (cat(0), cat(1)), (sum(p[2] for p in parts),)

    d_o, d_g_a, d_gn = _rowwise("proj_a_dx", proj_a_dx, [d_ya, o_gla, g_a], [weight("w_pa", o_gla), gn],
                                [(ml_w, F32), (ml_w, BF16)],
                                [((1, HEAD_W), F32)], tile=512)
    dq_hm, dk_hm, d_va, dla_hm = _gla_bwd(q_hm, k_hm, v_a, la_hm, s_prev, d_o)

    def decay_bwd(al, ct, w, b):
        _, vjp = jax.vjp(_log_decay, al, w, b)
        dal, dw, db = vjp(ct)
        return (dal,), (dw, db)

    d_alow_p, d_wa_p, d_ba = _rowwise("gla_decay_bwd", decay_bwd, [a_low_p, _from_hm(dla_hm)], [w_a_up_p, b_a_up],
                                      [(LANES, BF16)], [(w_a_up_p.shape, F32), (b_a_up.shape, F32)])
    d_proj = jnp.concatenate([_from_hm(dq_hm), _from_hm(dk_hm), d_va, d_g_a, d_alow_p[:, 0:LOWRANK], d_xm, d_opre, d_ga, d_gb],
                             axis=1).astype(BF16)
    small = dict(w_a_up=d_wa_p[0:LOWRANK], b_a_up=d_ba, g_gla_norm=d_gn, conv_w=d_cw, conv_b=d_cb,
                 w_q_ml=d_wq, w_k_ml=d_wk, w_v_ml=d_wv, w_if=d_wif, b_if=d_bif, ml_skip=d_skip, g_ml_norm=d_gml,
                 g_post_mix=d_gpm, g_pre_mlp=d_gpl, g_post_mlp=d_gpo)
    sent_small = on_small(small, loss)
    sent_in = sent_small
    for half in range(2):
        dw_half = _mm_shard_cols(h, d_proj, n_in, "proj_in_dw_%d" % half, half, d // 2, deps=dep(sent_in))
        sent_in = on_grads({"w_in#%d" % half: dw_half})

    def proj_in_dx(dp, xv, dres, w, g):
        _, vjp = jax.vjp(_rms, xv, g)
        dx, dg = vjp(sum(_raw_dot(dp[:, j * n_in:(j + 1) * n_in], w[j], "nt") for j in range(N_DEV)))
        return (dx + dres,), (dg,)

    grad_x, d_g1 = _rowwise("proj_in_dx", proj_in_dx, [d_proj, x, dx_res], [w_in, g1], [(d, F32)], [((1, d), F32)],
                            deps=dep(sent_in))
    return grad_x, on_small(dict(g_pre_mix=d_g1), None)


BIG = ("w_in", "w_pa", "w_pb", "w_o", "w_up", "w_down")
BIG_COL_SHARDED = ("w_in", "w_pa", "w_pb", "w_up")
SMALL_SHARDED = ("w_a_up", "conv_w", "w_if")
SMALL = ("g_pre_mix", "w_a_up", "b_a_up", "g_gla_norm", "conv_w", "conv_b", "w_q_ml", "w_k_ml", "w_v_ml", "w_if", "b_if",
         "ml_skip", "g_ml_norm", "g_post_mix", "g_pre_mlp", "g_post_mlp")
WEIGHTS = ("g_pre_mix", "w_in", "w_a_up", "b_a_up", "g_gla_norm", "conv_w", "conv_b", "w_q_ml", "w_k_ml", "w_v_ml", "w_if", "b_if",
           "ml_skip", "g_ml_norm", "w_pa", "w_pb", "w_o", "g_post_mix", "g_pre_mlp", "w_up", "w_down", "g_post_mlp")


def kernel(x, g_pre_mix, w_in, w_a_up, b_a_up, g_gla_norm, conv_w, conv_b, w_q_ml, w_k_ml, w_v_ml, w_if, b_if, ml_skip, g_ml_norm, w_pa, w_pb, w_o, g_post_mix, g_pre_mlp, w_up, w_down, g_post_mlp, loss_target, m_g_pre_mix, m_w_in, m_w_a_up, m_b_a_up, m_g_gla_norm, m_conv_w, m_conv_b, m_w_q_ml, m_w_k_ml, m_w_v_ml, m_w_if, m_b_if, m_ml_skip, m_g_ml_norm, m_w_pa, m_w_pb, m_w_o, m_g_post_mix, m_g_pre_mlp, m_w_up, m_w_down, m_g_post_mlp, v_g_pre_mix, v_w_in, v_w_a_up, v_b_a_up, v_g_gla_norm, v_conv_w, v_conv_b, v_w_q_ml, v_w_k_ml, v_w_v_ml, v_w_if, v_b_if, v_ml_skip, v_g_ml_norm, v_w_pa, v_w_pb, v_w_o, v_g_post_mix, v_g_pre_mlp, v_w_up, v_w_down, v_g_post_mlp):
    args = dict(locals())
    w = {n: args[n][0] for n in WEIGHTS}
    m = {n: args["m_" + n][0] for n in WEIGHTS}
    v = {n: args["v_" + n][0] for n in WEIGHTS}

    me_lin = _lin(_me())
    me_idx = jnp.reshape(me_lin, (1,)).astype(jnp.int32)

    def full_weight(n, g):
        if n in ("w_in", "w_up"):
            return g
        return _from_col_blocks(g) if n in BIG_COL_SHARDED else g.reshape(-1, g.shape[-1])

    def grad_parts(n, g):
        if n.partition("#")[0] in ("w_in", "w_up"):
            return g
        return (_col_blocks(g) if n in BIG_COL_SHARDED else g.reshape(N_DEV, -1, g.shape[-1])).astype(BF16)

    sharded_names = tuple(SMALL_SHARDED)
    narrow = {n: w[n].astype(BF16) for n in BIG}
    ready, pending, passing = {}, {}, {}
    first_state, _ = _copies_start(["gather"] * len(sharded_names) + ["gather_chips"],
                                   [_small_view(n, w[n]) for n in sharded_names] + [narrow["w_in"]], "allgather_start_first")

    def prefetch(group, after):
        state, token = _copies_start("gather_chips", [narrow[n] for n in group], "allgather_start_" + group[0], after)
        for n in group:
            pending[n] = (group, state)
        return token

    def pass_on(n, after):
        group, state = pending[n]
        shards, lands = _copies_wait(state, after, "allgather_wait_" + group[0])
        state, token = _copies_start("gather_pass", shards, "allgather_pass_" + group[0], lands=lands)
        for gn in group:
            passing[gn] = (group, state)
        return token

    def weight(n, after):
        if n not in ready:
            group, state = passing[n]
            shards, lands = _copies_wait(state, after, "allgather_passed_" + group[0])
            for gn, shard, land in zip(group, shards, lands):
                ready[gn] = full_weight(gn, lax.dynamic_update_slice(land, shard[None], (me_lin, 0, 0)))
        return ready[n]

    first_own, first_lands = _copies_wait(first_state, [narrow[n] for n in BIG if n != "w_in"], "allgather_wait_first")
    state, _ = _copies_start("gather_pass", first_own[-1:], "allgather_pass_w_in", lands=first_lands[-1:])
    passing["w_in"] = (("w_in",), state)
    ws = {n: (w[n].reshape(1, -1) if w[n].ndim == 1 else w[n]) for n in SMALL if n not in SMALL_SHARDED}
    for n, own, land in zip(sharded_names, first_own, first_lands):
        ws[n] = _small_unshard(n, lax.dynamic_update_slice(land, own[None], (me_lin, 0, 0)))

    sent = []

    def on_grads(grads):
        names = tuple(grads)
        state, token = _copies_start("exchange", [grad_parts(n, grads[n]) for n in names],
                                     "exchange_start_" + names[0].replace("#", "_"))
        sent.append((names, state))
        return token

    small_sent = []

    def on_small(small, loss):
        names = tuple(small)
        kinds = ["exchange" if n in SMALL_SHARDED else "gather" for n in names]
        srcs = [_small_shards(n, small[n]) if n in SMALL_SHARDED else _small_view(n, small[n]) for n in names]
        extra = [] if loss is None else [loss]
        state, token = _copies_start(kinds + ["gather"] * len(extra), srcs + extra, "allgather_start_small_" + names[0])
        small_sent.append((names, kinds, state))
        return token

    grad_x, last_token = _local_step(x[0], loss_target[0], weight, ws, prefetch, pass_on, on_grads, on_small)

    out = {}

    chunks = {}

    def finish(names, state, after):
        parts, lands = _copies_wait(state, after, "exchange_wait_" + names[0].replace("#", "_"))
        for name, part, land in zip(names, parts, lands):
            n, _, chunk = name.partition("#")
            chunks.setdefault(n, []).append((land, part))
            if chunk in ("", "1"):
                got_lands, got_parts = zip(*chunks[n])
                out[n] = _sum_adamw(got_lands, got_parts, me_idx, w[n], m[n], v[n], "adamw_" + n)

    def finish_small(names, kinds, state, after):
        own, lands = _copies_wait(state, after, "allgather_wait_small_" + names[0])
        k = len(names)
        upd = _small_update("adamw_small_" + names[0], me_idx, kinds, lands[:k], own[:k],
                            *[[_small_view(n, d[n]) for n in names] for d in (w, m, v)], sums=list(zip(lands[k:], own[k:])))
        for i, n in enumerate(names):
            out[n] = tuple(_small_unview(n, a, w[n].shape) for a in upd[4 * i:4 * i + 4])
        return upd[4 * k:]

    (loss_sum,) = finish_small(*small_sent[0], [grad_x, last_token])
    for names, state in sent[:-2]:
        finish(names, state, [grad_x, last_token])
    finish(*sent[-2], [loss_sum] + [out[n][1] for n in BIG if n in out])
    finish(*sent[-1], [loss_sum])
    finish_small(*small_sent[1], [out["w_in"][1]])

    shaped = lambda a, n: a.reshape(args[n].shape)
    return (loss_sum[0, 0], grad_x[None],
            *[shaped(out[n][0], n) for n in WEIGHTS], *[shaped(out[n][1], n) for n in WEIGHTS],
            *[shaped(out[n][2], n) for n in WEIGHTS], *[shaped(out[n][3], n) for n in WEIGHTS])
```

```python
import functools

import jax
import jax.numpy as jnp
from jax import lax
from jax.experimental import pallas as pl
from jax.experimental.pallas import tpu as pltpu

F32 = jnp.float32
BF16 = jnp.bfloat16
MESH = pl.DeviceIdType.MESH

N_DEV = 8
EPS = 1e-6
CHUNK = 64
CHUNKS_PER_STEP = 4
HEADS = 4
GLA_DK = 64
HEAD_W = 128
GLA_GATE_NORM = 16.0
LOWRANK = 16
CONV_K = 4
QKV_BLOCK = 4
LANES = 128
HALO = 8
IN_SPLITS = (256, 256, 512, 512, 16, 512, 512, 1024, 1024)

ADAM_LR = 0.001
ADAM_B1 = 0.9
ADAM_B2 = 0.999
ADAM_EPS = 1e-08
ADAM_WD = 0.01
ADAM_STEP = 10

VMEM_LIMIT = 56 * 1024 * 1024


def _cparams(*sem):
    return pltpu.CompilerParams(dimension_semantics=sem, vmem_limit_bytes=VMEM_LIMIT)


def _dims(mode, ndim):
    contract = {"nn": ((ndim - 1,), (ndim - 2,)), "nt": ((ndim - 1,), (ndim - 1,)), "tn": ((ndim - 2,), (ndim - 2,))}[mode]
    return contract, (((0,), (0,)) if ndim == 3 else ((), ()))


def _raw_dot(a, b, mode):
    return lax.dot_general(a.astype(BF16), b.astype(BF16), _dims(mode, a.ndim), preferred_element_type=F32)


@functools.partial(jax.custom_vjp, nondiff_argnums=(2,))
def _bdot(a, b, mode):
    return _raw_dot(a, b, mode)


def _bdot_fwd(a, b, mode):
    return _raw_dot(a, b, mode), (a, b)


def _bdot_bwd(mode, res, ct):
    a, b = res
    if mode == "nn":
        da, db = _raw_dot(ct, b, "nt"), _raw_dot(a, ct, "tn")
    elif mode == "nt":
        da, db = _raw_dot(ct, b, "nn"), _raw_dot(ct, a, "tn")
    else:
        da, db = _raw_dot(b, ct, "nt"), _raw_dot(a, ct, "nn")
    return da.astype(a.dtype), db.astype(b.dtype)


_bdot.defvjp(_bdot_fwd, _bdot_bwd)


def _split3(x):
    hi = x.astype(BF16)
    r1 = x - hi.astype(F32)
    mid = r1.astype(BF16)
    return hi, mid, (r1 - mid.astype(F32)).astype(BF16)


def _split_dot(tri, x):
    if x.ndim == 3:
        tri = jnp.broadcast_to(tri, (x.shape[0], *tri.shape))
    return sum(lax.dot_general(tri, t, _dims("nn", x.ndim), preferred_element_type=F32) for t in _split3(x))


def _tri(n, lower):
    r = lax.broadcasted_iota(jnp.int32, (n, n), 0)
    c = lax.broadcasted_iota(jnp.int32, (n, n), 1)
    return ((c <= r) if lower else (c >= r)).astype(BF16)


@jax.custom_vjp
def _cumsum_rows(x):
    return _split_dot(_tri(x.shape[-2], True), x)


def _cumsum_rows_fwd(x):
    return _cumsum_rows(x), None


def _cumsum_rows_bwd(_, ct):
    return (_split_dot(_tri(ct.shape[-2], False), ct),)


_cumsum_rows.defvjp(_cumsum_rows_fwd, _cumsum_rows_bwd)


def _abs(x):
    return jnp.where(x >= 0, x, -x)


def _sigmoid(x):
    return lax.logistic(x)


def _log_sigmoid(x):
    return jnp.minimum(x, 0.0) - jnp.log(1.0 + jnp.exp(-_abs(x)))


def _rms(x, g):
    return x * lax.rsqrt(jnp.mean(x * x, axis=-1, keepdims=True) + EPS) * g


def _head_slices(w):
    return [slice(h * w, (h + 1) * w) for h in range(HEADS)]


def _heads(ref, rows=slice(None)):
    return jnp.stack([ref[rows, hs] for hs in _head_slices(HEAD_W)])


def _put_heads(ref, val, rows=slice(None)):
    for h, hs in enumerate(_head_slices(HEAD_W)):
        ref[rows, hs] = val[h].astype(ref.dtype)


def _tile(dim, want):
    if dim <= want or dim % LANES:
        return dim
    t = want
    while dim % t:
        t -= LANES
    return t


def _mm(a, b, mode, out_dtype, name, tm=1024, tn=1024, tk=4096, epilogue=None, extra=(), deps=(), shards=None):
    if shards == "b":
        assert mode == "nn"
        ns = b.shape[2]
        (m, k), (k2, n) = a.shape, (b.shape[1], b.shape[0] * ns)
        tn = ns
    elif mode == "nn":
        (m, k), (k2, n) = a.shape, b.shape
    elif mode == "nt":
        (m, k), (n, k2) = a.shape, b.shape
    else:
        (k, m), (k2, n) = a.shape, b.shape
    assert k == k2, (name, a.shape, b.shape)
    tm, tn, tk = _tile(m, tm), _tile(n, tn), _tile(k, tk)
    nk = k // tk
    out_dtypes = out_dtype if epilogue else (out_dtype,)
    assert nk == 1 or (out_dtype == F32 and not epilogue), name
    n_in = 2 + len(extra)

    def body(*refs):
        p = _raw_dot(refs[0][...], refs[1][...], mode)
        if nk > 1:
            _accumulate(pl.program_id(2), [refs[n_in + len(deps)]], [p])
            return
        outs = epilogue(p, *[r[...] for r in refs[2:n_in]]) if epilogue else (p,)
        for ref, val in zip(refs[n_in + len(deps):], outs):
            ref[...] = val.astype(ref.dtype)

    a_spec = pl.BlockSpec((tk, tm), lambda i, j, kk: (kk, i)) if mode == "tn" else pl.BlockSpec((tm, tk), lambda i, j, kk: (i, kk))
    if shards == "b":
        b_spec = pl.BlockSpec((None, tk, tn), lambda i, j, kk: (j, kk, 0))
    elif mode == "nt":
        b_spec = pl.BlockSpec((tn, tk), lambda i, j, kk: (j, kk))
    else:
        b_spec = pl.BlockSpec((tk, tn), lambda i, j, kk: (kk, j))
    o_spec = pl.BlockSpec((tm, tn), lambda i, j, kk: (i, j))
    if shards == "out":
        out_specs, out_shape = [pl.BlockSpec((None, tm, tn), lambda i, j, kk: (j, i, 0))], (n // tn, m, tn)
    else:
        out_specs, out_shape = [o_spec] * len(out_dtypes), (m, n)
    res = pl.pallas_call(
        body, name=name, grid=(m // tm, n // tn, nk),
        in_specs=[a_spec, b_spec] + [o_spec] * len(extra) + [ANY] * len(deps), out_specs=out_specs,
        out_shape=[jax.ShapeDtypeStruct(out_shape, dt) for dt in out_dtypes],
        compiler_params=_cparams("parallel", "parallel", "arbitrary"),
    )(a, b, *extra, *deps)
    return res if epilogue else res[0]


def _shard_pieces(widths, n):
    bounds = [0]
    for wd in widths:
        bounds.append(bounds[-1] + wd)
    assert bounds[-1] == N_DEV * n
    return [[(i, max(s * n, b) - b, max(s * n, b) - s * n, min((s + 1) * n, b + wd) - max(s * n, b))
             for i, (b, wd) in enumerate(zip(bounds, widths)) if b < (s + 1) * n and b + wd > s * n]
            for s in range(N_DEV)]


def _mm_shard_cols(a, bs, widths, n, name, row_tile, tm, deps=()):
    t = a.shape[0]
    nb = len(bs)
    pieces = _shard_pieces(widths, n)

    def body(a_ref, *rest):
        b_refs = rest[:nb]
        o_ref, at_ref = rest[nb + len(deps):]
        j = pl.program_id(0)

        @pl.when(j == 0)
        def _():
            at_ref[...] = a_ref[...].astype(BF16).T

        for s in range(N_DEV):
            @pl.when(j == s)
            def _(s=s):
                for i, c_in, c_out, wd in pieces[s]:
                    cols = min(_round_up(wd, LANES), bs[i].shape[1] - c_in) if wd < LANES else wd
                    p = _raw_dot(at_ref[...], b_refs[i][:, c_in:c_in + cols], "nn")
                    o_ref[:, c_out:c_out + wd] = p[:, 0:wd].astype(BF16)

    return pl.pallas_call(
        body, name=name, grid=(N_DEV,),
        in_specs=[pl.BlockSpec((t, tm), lambda j: (0, row_tile))]
        + [pl.BlockSpec(b.shape, lambda j: (0, 0), pipeline_mode=pl.Buffered(1)) for b in bs] + [ANY] * len(deps),
        out_specs=pl.BlockSpec((None, tm, n), lambda j: (j, 0, 0)),
        out_shape=jax.ShapeDtypeStruct((N_DEV, tm, n), BF16),
        scratch_shapes=[pltpu.VMEM((tm, t), BF16)],
        compiler_params=_cparams("arbitrary"),
    )(a, *bs, *deps)


def _round_up(v, m):
    return -(-v // m) * m


def _rowwise(name, fn, rows, params, out_rows, out_accs=(), tile=256, deps=()):
    t = rows[0].shape[0]
    r = min(tile, t)
    assert t % r == 0
    n_in, n_or = len(rows) + len(params), len(out_rows)
    n_all = n_in + len(deps)
    params = list(params) + list(deps)

    def body(*refs):
        vals = [ref[...] for ref in refs[:n_in]]
        outs = refs[n_all:]
        ro, ao = fn(*vals)
        for ref, v in zip(outs[:n_or], ro):
            ref[...] = v.astype(ref.dtype)
        if out_accs:
            _accumulate(pl.program_id(0), outs[n_or:], ao)

    def full(shape):
        return pl.BlockSpec(shape, lambda i, nd=len(shape): (0,) * nd)

    return pl.pallas_call(
        body, name=name, grid=(t // r,),
        in_specs=[pl.BlockSpec((r, a.shape[1]), lambda i: (i, 0)) for a in rows] + [full(p.shape) for p in params],
        out_specs=[pl.BlockSpec((r, w), lambda i: (i, 0)) for w, _ in out_rows] + [full(s) for s, _ in out_accs],
        out_shape=[jax.ShapeDtypeStruct((t, w), dt) for w, dt in out_rows] + [jax.ShapeDtypeStruct(s, dt) for s, dt in out_accs],
        compiler_params=_cparams("arbitrary"),
    )(*rows, *params)


def _accumulate(step, refs, vals):
    for ref, v in zip(refs, vals):
        @pl.when(step == 0)
        def _(ref=ref, v=v):
            ref[...] = v.astype(ref.dtype)

        @pl.when(step > 0)
        def _(ref=ref, v=v):
            ref[...] += v.astype(ref.dtype)


def _gla_chunk(q, k, v, la, st):
    c = q.shape[-2]
    row = lax.broadcasted_iota(jnp.int32, (c, c), 0)
    col = lax.broadcasted_iota(jnp.int32, (c, c), 1)
    cum = _cumsum_rows(la)
    cl = jnp.sum(la, axis=-2, keepdims=True)
    ep = jnp.exp(cum)
    en = jnp.exp(-cum)
    qs = q * (GLA_DK ** -0.5)
    qp = qs * ep
    a_f = _bdot(qp, k * en, "nt")
    a_b = _bdot(qs * en, k * ep, "nt")
    sc = jnp.where(row >= col, a_f, a_b)
    o = _bdot(sc, v, "nn") + _bdot(qp, st, "nt")
    kd = k * jnp.exp(cl - cum)
    st_new = st * jnp.exp(cl) + _bdot(v, kd, "tn")
    return o, st_new


def _gla_specs(nc, rev):
    nb = nc // CHUNKS_PER_STEP
    rows = CHUNKS_PER_STEP * CHUNK

    def blk(n):
        return (nb - 1 - n) if rev else n
    hm = pl.BlockSpec((HEADS, rows, GLA_DK), lambda n: (0, blk(n), 0))
    tm = pl.BlockSpec((rows, HEADS * HEAD_W), lambda n: (blk(n), 0))
    st = pl.BlockSpec((HEADS, CHUNKS_PER_STEP, HEAD_W, GLA_DK), lambda n: (0, blk(n), 0, 0))
    return nb, hm, tm, st


def _chunk_rows(c):
    return slice(c * CHUNK, (c + 1) * CHUNK)


def _gla_fwd(q, k, v, la, deps=()):
    t = v.shape[0]
    nc = t // CHUNK
    nb, hm, tm, st = _gla_specs(nc, False)

    def body(q_ref, k_ref, v_ref, la_ref, *rest):
        o_ref, sp_ref, st_ref = rest[len(deps):]

        @pl.when(pl.program_id(0) == 0)
        def _():
            st_ref[...] = jnp.zeros_like(st_ref)

        s = st_ref[...]
        for c in range(CHUNKS_PER_STEP):
            r = _chunk_rows(c)
            sp_ref[:, c] = s
            o, s = _gla_chunk(q_ref[:, r], k_ref[:, r], _heads(v_ref, r), la_ref[:, r], s)
            _put_heads(o_ref, o, r)
        st_ref[...] = s

    return pl.pallas_call(
        body, name="gla_fwd", grid=(nb,),
        in_specs=[hm, hm, tm, hm] + [ANY] * len(deps), out_specs=[tm, st],
        out_shape=[jax.ShapeDtypeStruct((t, HEADS * HEAD_W), F32), jax.ShapeDtypeStruct((HEADS, nc, HEAD_W, GLA_DK), F32)],
        scratch_shapes=[pltpu.VMEM((HEADS, HEAD_W, GLA_DK), F32)],
        compiler_params=_cparams("arbitrary"),
    )(q, k, v, la, *deps)


def _gla_bwd(q, k, v, la, sp, do):
    t = v.shape[0]
    nc = t // CHUNK
    nb, hm, tm, st = _gla_specs(nc, True)

    def body(q_ref, k_ref, v_ref, la_ref, sp_ref, do_ref, dq_ref, dk_ref, dv_ref, dla_ref, ds_ref):
        @pl.when(pl.program_id(0) == 0)
        def _():
            ds_ref[...] = jnp.zeros_like(ds_ref)

        ds = ds_ref[...]
        for c in reversed(range(CHUNKS_PER_STEP)):
            r = _chunk_rows(c)
            _, vjp = jax.vjp(_gla_chunk, q_ref[:, r], k_ref[:, r], _heads(v_ref, r), la_ref[:, r], sp_ref[:, c])
            dq, dk, dv, dla, ds = vjp((_heads(do_ref, r), ds))
            dq_ref[:, r] = dq.astype(dq_ref.dtype)
            dk_ref[:, r] = dk.astype(dk_ref.dtype)
            _put_heads(dv_ref, dv, r)
            dla_ref[:, r] = dla
        ds_ref[...] = ds

    hm_shape = jax.ShapeDtypeStruct((HEADS, t, GLA_DK), BF16)
    return pl.pallas_call(
        body, name="gla_bwd", grid=(nb,),
        in_specs=[hm, hm, tm, hm, st, tm], out_specs=[hm, hm, tm, hm],
        out_shape=[hm_shape, hm_shape, jax.ShapeDtypeStruct((t, HEADS * HEAD_W), BF16),
                   jax.ShapeDtypeStruct((HEADS, t, GLA_DK), F32)],
        scratch_shapes=[pltpu.VMEM((HEADS, HEAD_W, GLA_DK), F32)],
        compiler_params=_cparams("arbitrary"),
    )(q, k, v, la, sp, do)


def _ml_chunk(q, k, v, li_r, lf_r, cm, nv, m):
    c = q.shape[-2]
    row = lax.broadcasted_iota(jnp.int32, (c, c), 0)
    col = lax.broadcasted_iota(jnp.int32, (c, c), 1)
    eye = (row == col).astype(F32)
    li_c = jnp.sum(eye * li_r, axis=-1, keepdims=True)
    lf_c = jnp.sum(eye * lf_r, axis=-1, keepdims=True)
    fc_c = jnp.sum((col <= row).astype(F32) * lf_r, axis=-1, keepdims=True)
    fc_r = jnp.sum((row <= col).astype(F32) * lf_c, axis=-2, keepdims=True)
    f_last = jnp.sum(lf_r, axis=-1, keepdims=True)
    kc = k * (HEAD_W ** -0.5)
    a_c = f_last - fc_c + li_c
    m_loc = jnp.max(a_c, axis=-2, keepdims=True)
    kw = kc * jnp.exp(a_c - m_loc)
    c_chunk = _bdot(kw, v, "tn")
    n_chunk = jnp.sum(kw, axis=-2, keepdims=True)
    m_new = jnp.maximum(f_last + m, m_loc)
    sp = jnp.exp(f_last + m - m_new)
    sl = jnp.exp(m_loc - m_new)
    cm_new = sp * cm + sl * c_chunk
    nv_new = sp * nv + sl * n_chunk
    log_d = li_r - _abs(fc_c - fc_r)
    g_inter = fc_c + m
    m_t = jnp.maximum(g_inter, jnp.max(log_d, axis=-1, keepdims=True))
    s = _bdot(q, kc, "nt") * jnp.exp(log_d - m_t)
    sc = jnp.exp(g_inter - m_t)
    num = _bdot(s, v, "nn") + sc * _bdot(q, cm, "nn")
    den = jnp.sum(s, axis=-1, keepdims=True) + sc * jnp.sum(q * nv, axis=-1, keepdims=True)
    den = jnp.maximum(_abs(den), jnp.exp(-m_t))
    return num / den, cm_new, nv_new, m_new


def _ml_specs(nc, rev):
    nb = nc // CHUNKS_PER_STEP

    def blk(n):
        return (nb - 1 - n) if rev else n
    tm = pl.BlockSpec((CHUNKS_PER_STEP * CHUNK, HEADS * HEAD_W), lambda n: (blk(n), 0))
    gate = pl.BlockSpec((HEADS, CHUNKS_PER_STEP, 1, CHUNK), lambda n: (0, blk(n), 0, 0))
    cm = pl.BlockSpec((HEADS, CHUNKS_PER_STEP, HEAD_W, HEAD_W), lambda n: (0, blk(n), 0, 0))
    vec = pl.BlockSpec((HEADS, CHUNKS_PER_STEP, 1, HEAD_W), lambda n: (0, blk(n), 0, 0))
    return nb, tm, gate, cm, vec


_ML_STATE = [pltpu.VMEM((HEADS, HEAD_W, HEAD_W), F32), pltpu.VMEM((HEADS, 1, HEAD_W), F32), pltpu.VMEM((HEADS, 1, HEAD_W), F32)]


def _ml_fwd(q, k, v, li, lf):
    t = q.shape[0]
    nc = t // CHUNK
    nb, tm, gate, cm, vec = _ml_specs(nc, False)

    def body(q_ref, k_ref, v_ref, li_ref, lf_ref, hc_ref, cp_ref, np_ref, mp_ref, c_ref, n_ref, m_ref):
        @pl.when(pl.program_id(0) == 0)
        def _():
            c_ref[...] = jnp.zeros_like(c_ref)
            n_ref[...] = jnp.zeros_like(n_ref)
            m_ref[...] = jnp.zeros_like(m_ref)

        cs, ns, ms = c_ref[...], n_ref[...], m_ref[...][:, :, 0:1]
        for c in range(CHUNKS_PER_STEP):
            r = _chunk_rows(c)
            cp_ref[:, c] = cs
            np_ref[:, c] = ns
            mp_ref[:, c] = jnp.broadcast_to(ms, m_ref.shape)
            hc, cs, ns, ms = _ml_chunk(_heads(q_ref, r), _heads(k_ref, r), _heads(v_ref, r), li_ref[:, c], lf_ref[:, c],
                                       cs, ns, ms)
            _put_heads(hc_ref, hc, r)
        c_ref[...] = cs
        n_ref[...] = ns
        m_ref[...] = jnp.broadcast_to(ms, m_ref.shape)

    return pl.pallas_call(
        body, name="mlstm_fwd", grid=(nb,),
        in_specs=[tm, tm, tm, gate, gate], out_specs=[tm, cm, vec, vec],
        out_shape=[jax.ShapeDtypeStruct((t, HEADS * HEAD_W), F32), jax.ShapeDtypeStruct((HEADS, nc, HEAD_W, HEAD_W), F32),
                   jax.ShapeDtypeStruct((HEADS, nc, 1, HEAD_W), F32), jax.ShapeDtypeStruct((HEADS, nc, 1, HEAD_W), F32)],
        scratch_shapes=_ML_STATE,
        compiler_params=_cparams("arbitrary"),
    )(q, k, v, li, lf)


def _ml_bwd(q, k, v, li, lf, cp, npv, mp, dhc):
    t = q.shape[0]
    nc = t // CHUNK
    nb, tm, gate, cm, vec = _ml_specs(nc, True)

    def body(q_ref, k_ref, v_ref, li_ref, lf_ref, cp_ref, np_ref, mp_ref, dhc_ref,
             dq_ref, dk_ref, dv_ref, dli_ref, dlf_ref, dc_ref, dn_ref, dm_ref):
        @pl.when(pl.program_id(0) == 0)
        def _():
            dc_ref[...] = jnp.zeros_like(dc_ref)
            dn_ref[...] = jnp.zeros_like(dn_ref)
            dm_ref[...] = jnp.zeros_like(dm_ref)

        dc, dn, dm = dc_ref[...], dn_ref[...], dm_ref[...][:, :, 0:1]
        for c in reversed(range(CHUNKS_PER_STEP)):
            r = _chunk_rows(c)
            _, vjp = jax.vjp(_ml_chunk, _heads(q_ref, r), _heads(k_ref, r), _heads(v_ref, r), li_ref[:, c], lf_ref[:, c],
                             cp_ref[:, c], np_ref[:, c], mp_ref[:, c][:, :, 0:1])
            dq, dk, dv, dli, dlf, dc, dn, dm = vjp((_heads(dhc_ref, r), dc, dn, dm))
            _put_heads(dq_ref, dq, r)
            _put_heads(dk_ref, dk, r)
            _put_heads(dv_ref, dv, r)
            dli_ref[:, c] = dli
            dlf_ref[:, c] = dlf
        dc_ref[...] = dc
        dn_ref[...] = dn
        dm_ref[...] = jnp.broadcast_to(dm, dm_ref.shape)

    tm_shape = jax.ShapeDtypeStruct((t, HEADS * HEAD_W), F32)
    gate_shape = jax.ShapeDtypeStruct((HEADS, nc, 1, CHUNK), F32)
    return pl.pallas_call(
        body, name="mlstm_bwd", grid=(nb,),
        in_specs=[tm, tm, tm, gate, gate, cm, vec, vec, tm], out_specs=[tm, tm, tm, gate, gate],
        out_shape=[tm_shape, tm_shape, tm_shape, gate_shape, gate_shape],
        scratch_shapes=_ML_STATE,
        compiler_params=_cparams("arbitrary"),
    )(q, k, v, li, lf, cp, npv, mp, dhc)


def _ml_pre(s0, s1, s2, s3, cw0, cw1, cw2, cw3, cb, wq, wk, wv, wiq, wik, wiv, bif):
    pre = cb + cw0 * s0 + cw1 * s1 + cw2 * s2 + cw3 * s3
    xc = pre * _sigmoid(pre)
    q = _bdot(xc, wq, "nn")
    k = _bdot(xc, wk, "nn")
    v = _bdot(s3, wv, "nn")
    gates = _bdot(q, wiq, "nn") + _bdot(k, wik, "nn") + _bdot(v, wiv, "nn") + bif
    lane = lax.broadcasted_iota(jnp.int32, gates.shape, 1)
    gl = jnp.where(lane < HEADS, gates, _log_sigmoid(gates))
    return xc, q, k, v, gl


def _delayed(xs_ref, x_ref, halo_ref, r):
    xs_ref[0:HALO, :] = halo_ref[...]
    xs_ref[HALO:HALO + r, :] = x_ref[...]
    return [xs_ref[pl.ds(HALO - (CONV_K - 1) + j, r), :] for j in range(CONV_K)]


def _full_spec(shape):
    return pl.BlockSpec(shape, lambda i, nd=len(shape): (0,) * nd)


def _ml_pre_fwd(x_m, x_pad, params, tile=256, deps=()):
    t, w = x_m.shape
    r = min(tile, t)

    def body(*refs):
        x_ref, halo_ref = refs[:2]
        p = [ref[...] for ref in refs[2:2 + len(params)]]
        outs = refs[2 + len(params) + len(deps):-1]
        res = _ml_pre(*_delayed(refs[-1], x_ref, halo_ref, r), *p)
        for ref, val in zip(outs, res):
            ref[...] = val

    row = pl.BlockSpec((r, w), lambda i: (i, 0))
    return pl.pallas_call(
        body, name="ml_pre_fwd", grid=(t // r,),
        in_specs=[row, pl.BlockSpec((HALO, w), lambda i: (i * (r // HALO), 0))] + [_full_spec(p.shape) for p in params]
        + [ANY] * len(deps),
        out_specs=[row] * 4 + [pl.BlockSpec((r, LANES), lambda i: (i, 0))],
        out_shape=[jax.ShapeDtypeStruct((t, w), F32)] * 4 + [jax.ShapeDtypeStruct((t, LANES), F32)],
        scratch_shapes=[pltpu.VMEM((r + HALO, w), F32)],
        compiler_params=_cparams("arbitrary"),
    )(x_m, x_pad, *params, *deps)


def _ml_pre_bwd(x_m, x_pad, params, cts, tile=256):
    t, w = x_m.shape
    r = min(tile, t)
    nt = t // r
    n_p = len(params)

    def body(*refs):
        x_ref, halo_ref = refs[:2]
        p = [ref[...] for ref in refs[2:2 + n_p]]
        ct = [ref[...] for ref in refs[2 + n_p:7 + n_p]]
        dx_ref = refs[7 + n_p]
        dp_refs = refs[8 + n_p:8 + 2 * n_p]
        xs_ref, ds_ref, carry_ref = refs[8 + 2 * n_p:]
        step = pl.program_id(0)

        @pl.when(step == 0)
        def _():
            ds_ref[...] = jnp.zeros_like(ds_ref)
            carry_ref[...] = jnp.zeros_like(carry_ref)

        _, vjp = jax.vjp(_ml_pre, *_delayed(xs_ref, x_ref, halo_ref, r), *p)
        grads = vjp(tuple(ct))
        for j in range(CONV_K):
            ds_ref[j, HALO:HALO + r, :] = grads[j]
        lead = HALO + CONV_K - 1
        d_tile = sum(ds_ref[j, pl.ds(lead - j, r), :] for j in range(CONV_K))
        d_halo = sum(ds_ref[j, pl.ds(CONV_K - 1 - j, HALO), :] for j in range(CONV_K))
        dx_ref[...] = jnp.concatenate([d_tile[:r - HALO], d_tile[r - HALO:] + carry_ref[...]], axis=0).astype(dx_ref.dtype)
        carry_ref[...] = d_halo
        _accumulate(step, dp_refs, grads[CONV_K:])

    row = pl.BlockSpec((r, w), lambda i: (nt - 1 - i, 0))
    return pl.pallas_call(
        body, name="ml_pre_bwd", grid=(nt,),
        in_specs=[row, pl.BlockSpec((HALO, w), lambda i: ((nt - 1 - i) * (r // HALO), 0))] + [_full_spec(p.shape) for p in params]
        + [row] * 4 + [pl.BlockSpec((r, LANES), lambda i: (nt - 1 - i, 0))],
        out_specs=[row] + [_full_spec(p.shape) for p in params],
        out_shape=[jax.ShapeDtypeStruct((t, w), BF16)] + [jax.ShapeDtypeStruct(p.shape, F32) for p in params],
        scratch_shapes=[pltpu.VMEM((r + HALO, w), F32), pltpu.VMEM((CONV_K, r + 2 * HALO, w), F32), pltpu.VMEM((HALO, w), F32)],
        compiler_params=_cparams("arbitrary"),
    )(x_m, x_pad, *params, *cts)


def _per_head(fn, row_vals, head_params, shared_params=()):
    return [fn(*[a[:, hs] for a in row_vals], *[p[:, hs] for p in head_params], *shared_params) for hs in _head_slices(HEAD_W)]


def _gla_out(o, g, gn):
    return _rms(o, gn) * (g * _sigmoid(g))


def _ml_out(hc, op, xc, g, sk):
    hcell = hc * _sigmoid(op)
    mu = jnp.mean(hcell, axis=-1, keepdims=True)
    d = hcell - mu
    var = jnp.mean(d * d, axis=-1, keepdims=True)
    return d * lax.rsqrt(var + EPS) * g + sk * xc


def _log_decay(al, w, b):
    return _log_sigmoid(_bdot(al, w, "nn") + b) * (1.0 / GLA_GATE_NORM)


def _merge(ga, gb, ya, yb):
    return _sigmoid(ga) * ya + _sigmoid(gb) * yb


def _post_mix(x, z, gpm, gpl):
    x1 = x + _rms(z, gpm)
    return x1, _rms(x1, gpl)


def _loss_rows(x1, dn, tgt, g):
    e = x1 + _rms(dn, g) - tgt
    return 0.5 * jnp.sum(jnp.mean(e * e, axis=-1, keepdims=True), axis=0, keepdims=True)


def _lin(p):
    return 4 * p[0] + 2 * p[1] + p[2]


def _me():
    return lax.axis_index("x"), lax.axis_index("y"), lax.axis_index("c")


def _flip(p, k):
    return tuple((1 - v) if (k >> (2 - i)) & 1 else v for i, v in enumerate(p))


ANY = pl.BlockSpec(memory_space=pl.ANY)


HBM = pl.BlockSpec(memory_space=pltpu.HBM)
SEM = pl.BlockSpec(memory_space=pltpu.SEMAPHORE)
DATAFLOW = pltpu.SideEffectType.DATAFLOW_SIDE_EFFECTING


SIBLING = 1
OTHER_CHIPS = (2, 4, 6)


def _peer_copies(kinds, srcs, lands, send_sems, recv_sems):
    me = _me()
    copies = []
    for a, (kind, src, land) in enumerate(zip(kinds, srcs, lands)):
        masks = {"gather": range(1, N_DEV), "exchange": range(1, N_DEV), "gather_chips": (SIBLING, *OTHER_CHIPS),
                 "gather_pass": OTHER_CHIPS}[kind]
        for k in masks:
            peer = _flip(me, k)
            if kind == "gather_pass":
                block = land.at[_lin(peer)]
                src_ref, dst_ref, target = block, block, _flip(me, SIBLING)
            else:
                src_ref, dst_ref, target = (src.at[_lin(peer)] if kind == "exchange" else src), land.at[_lin(me)], peer
            copies.append(pltpu.make_async_remote_copy(
                src_ref=src_ref, dst_ref=dst_ref, send_sem=send_sems.at[a * 7 + k - 1], recv_sem=recv_sems.at[a * 7 + k - 1],
                device_id=target, device_id_type=MESH))
    return copies


def _copies_start(kind, srcs, name, after=None, lands=None):
    n = len(srcs)
    extra = [] if after is None else [after]
    kind = [kind] * n if isinstance(kind, str) else list(kind)
    land_shapes = [(s.shape if k == "exchange" else (N_DEV, *s.shape)) for k, s in zip(kind, srcs)]
    lands = [lax.empty(ls, s.dtype) for ls, s in zip(land_shapes, srcs)] if lands is None else lands

    def body(*refs):
        sems = refs[2 * n + len(extra):]
        for cp in _peer_copies(kind, refs[:n], refs[n:2 * n], sems[0], sems[1]):
            cp.start()
        refs[-1][...] = jnp.zeros_like(refs[-1])

    def hbm(a):
        return pltpu.with_memory_space_constraint(a, pltpu.HBM)

    out = pl.pallas_call(
        body, name=name,
        out_shape=(pltpu.SemaphoreType.DMA((7 * n,)), pltpu.SemaphoreType.DMA((7 * n,)),
                   *[pltpu.HBM(s.shape, s.dtype) for s in srcs],
                   *[pltpu.HBM(ls, s.dtype) for ls, s in zip(land_shapes, srcs)],
                   jax.ShapeDtypeStruct((8, LANES), F32)),
        in_specs=[HBM] * (2 * n) + [ANY] * len(extra),
        out_specs=(SEM, SEM, *[HBM] * (2 * n), pl.BlockSpec(memory_space=pltpu.VMEM)),
        input_output_aliases={i: 2 + i for i in range(2 * n)},
        compiler_params=pltpu.CompilerParams(has_side_effects=DATAFLOW),
    )(*[hbm(s) for s in srcs], *[hbm(a) for a in lands], *extra)
    return (kind, n, out[:-1]), out[-1]


def _copies_wait(state, after, name):
    kind, n, (send_sems, recv_sems, *thru) = state
    after = list(after) if isinstance(after, (list, tuple)) else [after]

    def body(*refs):
        for cp in _peer_copies(kind, refs[:n], refs[n:2 * n], refs[2 * n], refs[2 * n + 1]):
            cp.wait_send()
            cp.wait_recv()

    out = pl.pallas_call(
        body, name=name,
        out_shape=tuple(pltpu.HBM(t.shape, t.dtype) for t in thru),
        in_specs=[HBM] * (2 * n) + [SEM, SEM] + [ANY] * len(after), out_specs=tuple([HBM] * (2 * n)),
        input_output_aliases={i: i for i in range(2 * n)},
        compiler_params=pltpu.CompilerParams(has_side_effects=DATAFLOW),
    )(*thru, send_sems, recv_sems, *after)
    return out[:n], out[n:]


def _adamw(w, g, m, v):
    m2 = ADAM_B1 * m + (1.0 - ADAM_B1) * g
    v2 = ADAM_B2 * v + (1.0 - ADAM_B2) * (g * g)
    m_hat = m2 / (1.0 - ADAM_B1 ** ADAM_STEP)
    v_hat = v2 / (1.0 - ADAM_B2 ** ADAM_STEP)
    delta = -ADAM_LR * (m_hat / (jnp.sqrt(v_hat) + ADAM_EPS) + ADAM_WD * w)
    return delta, m2, v2


def _sum_adamw(lands, parts, me_idx, w, m, v, name, tile=256):
    r, c = w.shape
    nchunks = len(lands)
    tr = min(tile, r // nchunks)
    per_chunk = r // nchunks // tr
    per = 1 + N_DEV

    def body(me_ref, *refs):
        w_ref, m_ref, v_ref, g_ref, d_ref, m2_ref, v2_ref = refs[nchunks * per:]
        for k in range(nchunks):
            own_ref, slots = refs[k * per], refs[k * per + 1:(k + 1) * per]

            @pl.when(pl.program_id(0) // per_chunk == k)
            def _(own_ref=own_ref, slots=slots):
                own = own_ref[...].astype(F32)
                g = None
                for s in range(N_DEV):
                    term = jnp.where(me_ref[0] == s, own, slots[s][...].astype(F32))
                    g = term if g is None else g + term
                d, m2, v2 = _adamw(w_ref[...], g, m_ref[...], v_ref[...])
                g_ref[...] = g
                d_ref[...] = d
                m2_ref[...] = m2
                v2_ref[...] = v2

    def chunk_specs(k):
        def tile_of(i):
            return jnp.clip(i - k * per_chunk, 0, per_chunk - 1)

        def slot_spec(s):
            return pl.BlockSpec((None, tr, c), lambda i, me: (jnp.where(me[0] == s, (s + 1) % N_DEV, s), tile_of(i), 0))
        return [pl.BlockSpec((None, tr, c), lambda i, me: (me[0], tile_of(i), 0))] + [slot_spec(s) for s in range(N_DEV)]

    row = pl.BlockSpec((tr, c), lambda i, me: (i, 0))
    operands = [a for land, part in zip(lands, parts) for a in (part, *[land] * N_DEV)]
    return pl.pallas_call(
        body, name=name,
        grid_spec=pltpu.PrefetchScalarGridSpec(
            num_scalar_prefetch=1, grid=(r // tr,),
            in_specs=[s for k in range(nchunks) for s in chunk_specs(k)] + [row] * 3,
            out_specs=[row] * 4),
        out_shape=[jax.ShapeDtypeStruct((r, c), F32)] * 4,
        compiler_params=_cparams("parallel"),
    )(me_idx, *operands, w, m, v)


def _small_update(name, me_idx, kinds, lands, owns, ws, ms, vs, sums=()):
    n = len(ws)
    lands, owns = list(lands) + [s[0] for s in sums], list(owns) + [s[1] for s in sums]
    kinds = list(kinds) + ["gather"] * len(sums)
    nl = len(lands)

    def summed(me, land_ref, own):
        g = None
        for s in range(N_DEV):
            term = jnp.where(me == s, own, land_ref[s])
            g = term if g is None else g + term
        return g

    def body(me_ref, *refs):
        land_refs, own_refs = refs[:nl], refs[nl:2 * nl]
        w_refs, m_refs, v_refs = (refs[2 * nl + i * n:2 * nl + (i + 1) * n] for i in range(3))
        outs = refs[2 * nl + 3 * n:]
        me = me_ref[0]
        for i in range(n):
            g = summed(me, land_refs[i], own_refs[i][...])
            d, m2, v2 = _adamw(w_refs[i][...], g, m_refs[i][...], v_refs[i][...])
            for ref, val in zip(outs[4 * i:4 * i + 4], (g, d, m2, v2)):
                ref[...] = val
        for i in range(n, nl):
            outs[4 * n + i - n][...] = summed(me, land_refs[i], own_refs[i][...])

    def whole(shape):
        return pl.BlockSpec(shape, lambda i, me, nd=len(shape): (0,) * nd)

    def own_spec(kind, own):
        if kind == "gather":
            return whole(own.shape)
        return pl.BlockSpec((None, *own.shape[1:]), lambda i, me: (me[0], 0, 0))

    shapes = [w.shape for w in ws]
    out_shapes = [s for s in shapes for _ in range(4)] + [s[1].shape for s in sums]
    return pl.pallas_call(
        body, name=name,
        grid_spec=pltpu.PrefetchScalarGridSpec(
            num_scalar_prefetch=1, grid=(1,),
            in_specs=[whole(a.shape) for a in lands] + [own_spec(k, o) for k, o in zip(kinds, owns)]
            + [whole(s) for s in shapes] * 3,
            out_specs=[whole(s) for s in out_shapes]),
        out_shape=[jax.ShapeDtypeStruct(s, F32) for s in out_shapes],
        compiler_params=_cparams("arbitrary"),
    )(me_idx, *lands, *owns, *ws, *ms, *vs)


def _small_view(n, a):
    if a.ndim == 1:
        return a.reshape(1, -1)
    if a.ndim == 3:
        return a.transpose(1, 2, 0).reshape(QKV_BLOCK * QKV_BLOCK, -1)
    return a.T if n == "w_if" else a


def _small_unview(n, a, shape):
    if len(shape) == 1:
        return a.reshape(shape)
    if len(shape) == 3:
        return a.reshape(QKV_BLOCK, QKV_BLOCK, -1).transpose(2, 0, 1)
    return a.T if n == "w_if" else a


def _small_shards(n, g):
    if n == "w_if":
        return g.reshape(N_DEV, -1, g.shape[1]).transpose(0, 2, 1)
    return g.reshape(g.shape[0], N_DEV, -1).transpose(1, 0, 2)


def _small_unshard(n, s):
    if n == "w_if":
        return s.transpose(0, 2, 1).reshape(-1, s.shape[1])
    return s.transpose(1, 0, 2).reshape(s.shape[1], -1)


def _to_hm(a, d):
    t = a.shape[0]
    return a.reshape(t, HEADS, d).transpose(1, 0, 2)


def _from_hm(a):
    h, t, d = a.shape
    return a.transpose(1, 0, 2).reshape(t, h * d)


def _gate_rows(g):
    t = g.shape[0]
    return g.T.reshape(HEADS, t // CHUNK, 1, CHUNK)


def _gate_cols(g):
    h, nc, _, c = g.shape
    return g.reshape(h, nc * c).T


def _blockdiag_dense(w):
    n = w.shape[0] * QKV_BLOCK
    tiled = jnp.tile(w.reshape(n, QKV_BLOCK), (1, n // QKV_BLOCK))
    r = lax.broadcasted_iota(jnp.int32, (n, n), 0)
    c = lax.broadcasted_iota(jnp.int32, (n, n), 1)
    return jnp.where(r // QKV_BLOCK == c // QKV_BLOCK, tiled, 0.0)


def _blockdiag_blocks(dense):
    n = dense[0].shape[0]
    k = len(dense)

    def body(*refs):
        r = lax.broadcasted_iota(jnp.int32, (n, n), 0)
        c = lax.broadcasted_iota(jnp.int32, (n, n), 1)
        fr = lax.broadcasted_iota(jnp.int32, (n, LANES), 0)
        fc = lax.broadcasted_iota(jnp.int32, (n, LANES), 1)
        fold = ((fr & (QKV_BLOCK - 1)) == fc).astype(BF16)
        for i in range(k):
            kept = jnp.where((r >> 2) == (c >> 2), refs[i][...], 0.0)
            refs[k + i][...] = sum(lax.dot_general(t, fold, _dims("nn", 2), preferred_element_type=F32) for t in _split3(kept))

    out = pl.pallas_call(body, name="blockdiag_blocks", out_shape=[jax.ShapeDtypeStruct((n, LANES), F32)] * k)(*dense)
    return [o[:, 0:QKV_BLOCK].reshape(n // QKV_BLOCK, QKV_BLOCK, QKV_BLOCK) for o in out]


def _col_blocks(w):
    k, n = w.shape
    return w.reshape(k, N_DEV, n // N_DEV).transpose(1, 0, 2)


def _from_col_blocks(g):
    d, k, n = g.shape
    return g.transpose(1, 0, 2).reshape(k, d * n)


def _local_step(x, tgt, weight, ws, prefetch, pass_on, on_grads, on_small):
    t, d = x.shape
    g1 = ws["g_pre_mix"]

    def dep(token):
        return () if token is None else (token,)

    w_in = weight("w_in", x)
    fetch_mix = prefetch(("w_pa", "w_pb", "w_o"), w_in)

    n_in = w_in.shape[2]

    offs = [0]
    for s in IN_SPLITS:
        offs.append(offs[-1] + s)

    def proj_in_fwd(xv, g, w):
        hv = _rms(xv, g)
        proj = jnp.concatenate([_raw_dot(hv, w[j], "nn") for j in range(N_DEV)], axis=1)
        parts = [proj[:, offs[i]:offs[i + 1]] for i in range(len(IN_SPLITS))]
        parts[4] = jnp.concatenate([parts[4], jnp.zeros((parts[4].shape[0], LANES - LOWRANK), F32)], axis=1)
        return (hv, *parts), ()

    widths = [LANES if s == LOWRANK else s for s in IN_SPLITS]
    h, q_a, k_a, v_a, g_a, a_low_p, x_m, o_pre, gate_a, gate_b = _rowwise(
        "proj_in", proj_in_fwd, [x], [g1, w_in], [(d, BF16)] + [(wd, F32) for wd in widths], deps=dep(fetch_mix))

    w_a_up_p = jnp.pad(ws["w_a_up"], ((0, LANES - LOWRANK), (0, 0)))
    b_a_up = ws["b_a_up"]
    (la,) = _rowwise("gla_decay", lambda al, w, b: ((_log_decay(al, w, b),), ()), [a_low_p], [w_a_up_p, b_a_up],
                     [(HEADS * GLA_DK, F32)])
    fetch_up = prefetch(("w_up", "w_down"), la)
    q_hm, k_hm, la_hm = _to_hm(q_a, GLA_DK), _to_hm(k_a, GLA_DK), _to_hm(la, GLA_DK)
    o_gla, s_prev = _gla_fwd(q_hm, k_hm, v_a, la_hm, deps=dep(fetch_up))
    pass_mix = pass_on("w_pa", o_gla)
    gn = ws["g_gla_norm"]
    ml_w = HEADS * HEAD_W

    cw = ws["conv_w"]
    w_if_p = jnp.pad(ws["w_if"], ((0, 0), (0, LANES - 2 * HEADS)))
    pre_params = [cw[0:1], cw[1:2], cw[2:3], cw[3:4], ws["conv_b"],
                  _blockdiag_dense(ws["w_q_ml"]), _blockdiag_dense(ws["w_k_ml"]), _blockdiag_dense(ws["w_v_ml"]),
                  w_if_p[0:ml_w], w_if_p[ml_w:2 * ml_w], w_if_p[2 * ml_w:3 * ml_w],
                  jnp.pad(ws["b_if"], ((0, 0), (0, LANES - 2 * HEADS)))]
    x_pad = jnp.pad(x_m, ((HALO, 0), (0, 0)))
    xc, q_m, k_m, v_m, gl = _ml_pre_fwd(x_m, x_pad, pre_params, deps=dep(pass_mix))
    li, lf = _gate_rows(gl[:, 0:HEADS]), _gate_rows(gl[:, HEADS:2 * HEADS])
    hc, c_prev, n_prev, m_prev = _ml_fwd(q_m, k_m, v_m, li, lf)
    g_ml, skip = ws["g_ml_norm"], ws["ml_skip"]

    def proj_a_fwd(o, g, n_, w):
        ya = jnp.concatenate(_per_head(_gla_out, [o, g], [], [n_]), axis=1)
        return (ya, _raw_dot(ya, w, "nn")), ()

    ya_in, y_a = _rowwise("proj_a", proj_a_fwd, [o_gla, g_a], [gn, weight("w_pa", hc)], [(ml_w, BF16), (d, F32)],
                          tile=512)

    def proj_b_fwd(a, b, c_, ga, gb, ya, g, s, w):
        hb = jnp.concatenate(_per_head(_ml_out, [a, b, c_], [g, s]), axis=1)
        yb = _raw_dot(hb, w, "nn")
        return (hb, yb, _merge(ga, gb, ya, yb)), ()

    h_b, y_b, merged = _rowwise("proj_b", proj_b_fwd, [hc, o_pre, xc, gate_a, gate_b, y_a], [g_ml, skip, weight("w_pb", hc)],
                                [(ml_w, BF16), (d, F32), (d, BF16)], tile=512)

    gpm, gpl, gpo = ws["g_post_mix"], ws["g_pre_mlp"], ws["g_post_mlp"]

    def proj_o_fwd(mg, xv, w, a, b):
        zv = _raw_dot(mg, w, "nn")
        return (zv, *_post_mix(xv, zv, a, b)), ()

    pass_up = pass_on("w_up", merged)
    z, x1, h2 = _rowwise("proj_o", proj_o_fwd, [merged, x], [weight("w_o", merged), gpm, gpl],
                         [(d, F32), (d, F32), (d, BF16)], tile=512, deps=dep(pass_up))
    w_up = weight("w_up", h2)
    up, u = _mm(h2, w_up, "nn", (BF16, BF16), "mlp_up", tm=2048, shards="b",
                epilogue=lambda p: (p, jnp.square(jnp.maximum(p, 0.0))))

    def mlp_down_loss(uv, x1v, tgtv, w, g):
        dnv = _raw_dot(uv, w, "nn")
        loss, vjp = jax.vjp(lambda a, b, c_: _loss_rows(a, b, tgtv, c_), x1v, dnv, g)
        dx1, ddn, dg = vjp(jnp.ones((1, 1), F32))
        return (dx1, ddn), (jnp.broadcast_to(loss, (1, LANES)), dg)

    dx1_y, d_dn, loss, d_gpo = _rowwise("mlp_down", mlp_down_loss, [u, x1, tgt], [weight("w_down", u), gpo],
                                        [(d, F32), (d, BF16)], [((1, LANES), F32), ((1, d), F32)], tile=512)

    (d_up,) = _mm(d_dn, weight("w_down", u), "nt", (BF16,), "mlp_down_dx", extra=[up],
                  epilogue=lambda p, a: (p * (2.0 * jnp.maximum(a.astype(F32), 0.0)),))
    dw_down = _mm(u, d_dn, "tn", BF16, "mlp_down_dw", tm=512)
    dw_up = _mm(h2, d_up, "tn", BF16, "mlp_up_dw", tn=w_up.shape[2], shards="out")
    sent_mlp = on_grads(dict(w_down=dw_down, w_up=dw_up))

    def mlp_up_dx(dup, xv, zv, dx1, w, a, b):
        _, vjp = jax.vjp(_post_mix, xv, zv, a, b)
        ns = w.shape[2]
        dh2 = sum(_raw_dot(dup[:, j * ns:(j + 1) * ns], w[j], "nt") for j in range(w.shape[0]))
        dx, dz, da, db = vjp((dx1, dh2))
        return (dx, dz), (da, db)

    dx_res, d_z, d_gpm, d_gpl = _rowwise("mlp_up_dx", mlp_up_dx, [d_up, x, z, dx1_y], [w_up, gpm, gpl],
                                         [(d, F32), (d, BF16)], [((1, d), F32), ((1, d), F32)], tile=512, deps=dep(sent_mlp))
    dw_o = _mm(merged, d_z, "tn", BF16, "proj_o_dw")

    def proj_o_dx(dz, ga, gb, ya, yb, w):
        return jax.vjp(_merge, ga, gb, ya, yb)[1](_raw_dot(dz, w, "nt")), ()

    d_ga, d_gb, d_ya, d_yb = _rowwise("proj_o_dx", proj_o_dx, [d_z, gate_a, gate_b, y_a, y_b], [weight("w_o", merged)],
                                      [(d, BF16)] * 4, tile=512)
    dw_pa = _mm(ya_in, d_ya, "tn", BF16, "proj_a_dw")
    dw_pb = _mm(h_b, d_yb, "tn", BF16, "proj_b_dw")
    sent_mix = on_grads(dict(w_o=dw_o, w_pa=dw_pa, w_pb=dw_pb))

    def proj_b_dx(dyb, a, b, c_, w, g, s):
        ct = _raw_dot(dyb, w, "nt")
        parts = []
        for hs in _head_slices(HEAD_W):
            _, vjp = jax.vjp(_ml_out, a[:, hs], b[:, hs], c_[:, hs], g[:, hs], s[:, hs])
            parts.append(vjp(ct[:, hs]))
        cat = lambda i: jnp.concatenate([p[i] for p in parts], axis=1)
        return (cat(0), cat(1), cat(2)), (cat(3), cat(4))

    d_hc, d_opre, d_xc, d_gml, d_skip = _rowwise("proj_b_dx", proj_b_dx, [d_yb, hc, o_pre, xc],
                                                 [weight("w_pb", hc), g_ml, skip], [(ml_w, F32), (ml_w, BF16), (ml_w, F32)],
                                                 [((1, ml_w), F32)] * 2,
                                                 tile=512, deps=dep(sent_mix))
    d_qm, d_km, d_vm, d_li, d_lf = _ml_bwd(q_m, k_m, v_m, li, lf, c_prev, n_prev, m_prev, d_hc)
    d_gl = jnp.concatenate([_gate_cols(d_li), _gate_cols(d_lf), jnp.zeros((t, LANES - 2 * HEADS), F32)], axis=1)
    pre_grads = _ml_pre_bwd(x_m, x_pad, pre_params, [d_xc, d_qm, d_km, d_vm, d_gl], tile=512)
    d_xm = pre_grads[0]
    d_cw = jnp.concatenate(pre_grads[1:5], axis=0)
    d_cb = pre_grads[5]
    d_wq, d_wk, d_wv = _blockdiag_blocks(pre_grads[6:9])
    d_wif = jnp.concatenate(pre_grads[9:12], axis=0)[:, 0:2 * HEADS]
    d_bif = pre_grads[12][:, 0:2 * HEADS]

    def proj_a_dx(dya, o, g, w, n_):
        ct = _raw_dot(dya, w, "nt")
        parts = []
        for hs in _head_slices(HEAD_W):
            _, vjp = jax.vjp(_gla_out, o[:, hs], g[:, hs], n_)
            parts.append(vjp(ct[:, hs]))
        cat = lambda i: jnp.concatenate([p[i] for p in parts], axis=1)
        return (cat(0), cat(1)), (sum(p[2] for p in parts),)

    d_o, d_g_a, d_gn = _rowwise("proj_a_dx", proj_a_dx, [d_ya, o_gla, g_a], [weight("w_pa", o_gla), gn],
                                [(ml_w, F32), (ml_w, BF16)],
                                [((1, HEAD_W), F32)], tile=512)
    dq_hm, dk_hm, d_va, dla_hm = _gla_bwd(q_hm, k_hm, v_a, la_hm, s_prev, d_o)

    def decay_bwd(al, ct, w, b):
        _, vjp = jax.vjp(_log_decay, al, w, b)
        dal, dw, db = vjp(ct)
        return (dal,), (dw, db)

    d_alow_p, d_wa_p, d_ba = _rowwise("gla_decay_bwd", decay_bwd, [a_low_p, _from_hm(dla_hm)], [w_a_up_p, b_a_up],
                                      [(LANES, BF16)], [(w_a_up_p.shape, F32), (b_a_up.shape, F32)])
    d_proj = [jnp.concatenate([_from_hm(dq_hm), _from_hm(dk_hm), d_va, d_g_a], axis=1), d_alow_p,
              jnp.concatenate([d_xm, d_opre, d_ga, d_gb], axis=1)]
    d_widths = [offs[4], LOWRANK, offs[9] - offs[5]]
    d_pieces = _shard_pieces(d_widths, n_in)
    small = dict(w_a_up=d_wa_p[0:LOWRANK], b_a_up=d_ba, g_gla_norm=d_gn, conv_w=d_cw, conv_b=d_cb,
                 w_q_ml=d_wq, w_k_ml=d_wk, w_v_ml=d_wv, w_if=d_wif, b_if=d_bif, ml_skip=d_skip, g_ml_norm=d_gml,
                 g_post_mix=d_gpm, g_pre_mlp=d_gpl, g_post_mlp=d_gpo)
    sent_small = on_small(small, loss)
    sent_in = sent_small
    for half in range(2):
        dw_half = _mm_shard_cols(h, d_proj, d_widths, n_in, "proj_in_dw_%d" % half, half, d // 2, deps=dep(sent_in))
        sent_in = on_grads({"w_in#%d" % half: dw_half})

    def proj_in_dx(dp_a, dp_low, dp_b, xv, dres, w, g):
        dh = 0.0
        for s in range(N_DEV):
            for i, c_in, c_w, wd in d_pieces[s]:
                src = (dp_a, dp_low, dp_b)[i]
                cols = src.shape[1] - c_in if wd < LANES else wd
                dh = dh + _raw_dot(src[:, c_in:c_in + cols], w[s][:, c_w:c_w + cols], "nt")
        _, vjp = jax.vjp(_rms, xv, g)
        dx, dg = vjp(dh)
        return (dx + dres,), (dg,)

    grad_x, d_g1 = _rowwise("proj_in_dx", proj_in_dx, [*d_proj, x, dx_res], [w_in, g1], [(d, F32)], [((1, d), F32)],
                            deps=dep(sent_in))
    return grad_x, on_small(dict(g_pre_mix=d_g1), None)


BIG = ("w_in", "w_pa", "w_pb", "w_o", "w_up", "w_down")
BIG_COL_SHARDED = ("w_in", "w_pa", "w_pb", "w_up")
SMALL_SHARDED = ("w_a_up", "conv_w", "w_if")
SMALL = ("g_pre_mix", "w_a_up", "b_a_up", "g_gla_norm", "conv_w", "conv_b", "w_q_ml", "w_k_ml", "w_v_ml", "w_if", "b_if",
         "ml_skip", "g_ml_norm", "g_post_mix", "g_pre_mlp", "g_post_mlp")
WEIGHTS = ("g_pre_mix", "w_in", "w_a_up", "b_a_up", "g_gla_norm", "conv_w", "conv_b", "w_q_ml", "w_k_ml", "w_v_ml", "w_if", "b_if",
           "ml_skip", "g_ml_norm", "w_pa", "w_pb", "w_o", "g_post_mix", "g_pre_mlp", "w_up", "w_down", "g_post_mlp")


def kernel(x, g_pre_mix, w_in, w_a_up, b_a_up, g_gla_norm, conv_w, conv_b, w_q_ml, w_k_ml, w_v_ml, w_if, b_if, ml_skip, g_ml_norm, w_pa, w_pb, w_o, g_post_mix, g_pre_mlp, w_up, w_down, g_post_mlp, loss_target, m_g_pre_mix, m_w_in, m_w_a_up, m_b_a_up, m_g_gla_norm, m_conv_w, m_conv_b, m_w_q_ml, m_w_k_ml, m_w_v_ml, m_w_if, m_b_if, m_ml_skip, m_g_ml_norm, m_w_pa, m_w_pb, m_w_o, m_g_post_mix, m_g_pre_mlp, m_w_up, m_w_down, m_g_post_mlp, v_g_pre_mix, v_w_in, v_w_a_up, v_b_a_up, v_g_gla_norm, v_conv_w, v_conv_b, v_w_q_ml, v_w_k_ml, v_w_v_ml, v_w_if, v_b_if, v_ml_skip, v_g_ml_norm, v_w_pa, v_w_pb, v_w_o, v_g_post_mix, v_g_pre_mlp, v_w_up, v_w_down, v_g_post_mlp):
    args = dict(locals())
    w = {n: args[n][0] for n in WEIGHTS}
    m = {n: args["m_" + n][0] for n in WEIGHTS}
    v = {n: args["v_" + n][0] for n in WEIGHTS}

    me_lin = _lin(_me())
    me_idx = jnp.reshape(me_lin, (1,)).astype(jnp.int32)

    def full_weight(n, g):
        if n in ("w_in", "w_up"):
            return g
        return _from_col_blocks(g) if n in BIG_COL_SHARDED else g.reshape(-1, g.shape[-1])

    def grad_parts(n, g):
        if n.partition("#")[0] in ("w_in", "w_up"):
            return g
        return (_col_blocks(g) if n in BIG_COL_SHARDED else g.reshape(N_DEV, -1, g.shape[-1])).astype(BF16)

    sharded_names = tuple(SMALL_SHARDED)
    narrow = {n: w[n].astype(BF16) for n in BIG}
    ready, pending, passing = {}, {}, {}
    first_state, _ = _copies_start(["gather"] * len(sharded_names) + ["gather_chips"],
                                   [_small_view(n, w[n]) for n in sharded_names] + [narrow["w_in"]], "allgather_start_first")

    def prefetch(group, after):
        state, token = _copies_start("gather_chips", [narrow[n] for n in group], "allgather_start_" + group[0], after)
        for n in group:
            pending[n] = (group, state)
        return token

    def pass_on(n, after):
        group, state = pending[n]
        shards, lands = _copies_wait(state, after, "allgather_wait_" + group[0])
        state, token = _copies_start("gather_pass", shards, "allgather_pass_" + group[0], lands=lands)
        for gn in group:
            passing[gn] = (group, state)
        return token

    def weight(n, after):
        if n not in ready:
            group, state = passing[n]
            shards, lands = _copies_wait(state, after, "allgather_passed_" + group[0])
            for gn, shard, land in zip(group, shards, lands):
                ready[gn] = full_weight(gn, lax.dynamic_update_slice(land, shard[None], (me_lin, 0, 0)))
        return ready[n]

    first_own, first_lands = _copies_wait(first_state, [narrow[n] for n in BIG if n != "w_in"], "allgather_wait_first")
    state, _ = _copies_start("gather_pass", first_own[-1:], "allgather_pass_w_in", lands=first_lands[-1:])
    passing["w_in"] = (("w_in",), state)
    ws = {n: (w[n].reshape(1, -1) if w[n].ndim == 1 else w[n]) for n in SMALL if n not in SMALL_SHARDED}
    for n, own, land in zip(sharded_names, first_own, first_lands):
        ws[n] = _small_unshard(n, lax.dynamic_update_slice(land, own[None], (me_lin, 0, 0)))

    sent = []

    def on_grads(grads):
        names = tuple(grads)
        state, token = _copies_start("exchange", [grad_parts(n, grads[n]) for n in names],
                                     "exchange_start_" + names[0].replace("#", "_"))
        sent.append((names, state))
        return token

    small_sent = []

    def on_small(small, loss):
        names = tuple(small)
        kinds = ["exchange" if n in SMALL_SHARDED else "gather" for n in names]
        srcs = [_small_shards(n, small[n]) if n in SMALL_SHARDED else _small_view(n, small[n]) for n in names]
        extra = [] if loss is None else [loss]
        state, token = _copies_start(kinds + ["gather"] * len(extra), srcs + extra, "allgather_start_small_" + names[0])
        small_sent.append((names, kinds, state))
        return token

    grad_x, last_token = _local_step(x[0], loss_target[0], weight, ws, prefetch, pass_on, on_grads, on_small)

    out = {}

    chunks = {}

    def finish(names, state, after):
        parts, lands = _copies_wait(state, after, "exchange_wait_" + names[0].replace("#", "_"))
        for name, part, land in zip(names, parts, lands):
            n, _, chunk = name.partition("#")
            chunks.setdefault(n, []).append((land, part))
            if chunk in ("", "1"):
                got_lands, got_parts = zip(*chunks[n])
                out[n] = _sum_adamw(got_lands, got_parts, me_idx, w[n], m[n], v[n], "adamw_" + n)

    def finish_small(names, kinds, state, after):
        own, lands = _copies_wait(state, after, "allgather_wait_small_" + names[0])
        k = len(names)
        upd = _small_update("adamw_small_" + names[0], me_idx, kinds, lands[:k], own[:k],
                            *[[_small_view(n, d[n]) for n in names] for d in (w, m, v)], sums=list(zip(lands[k:], own[k:])))
        for i, n in enumerate(names):
            out[n] = tuple(_small_unview(n, a, w[n].shape) for a in upd[4 * i:4 * i + 4])
        return upd[4 * k:]

    (loss_sum,) = finish_small(*small_sent[0], [grad_x, last_token])
    for names, state in sent[:-2]:
        finish(names, state, [grad_x, last_token])
    finish(*sent[-2], [loss_sum] + [out[n][1] for n in BIG if n in out])
    finish(*sent[-1], [loss_sum])
    finish_small(*small_sent[1], [out["w_in"][1]])

    shaped = lambda a, n: a.reshape(args[n].shape)
    return (loss_sum[0, 0], grad_x[None],
            *[shaped(out[n][0], n) for n in WEIGHTS], *[shaped(out[n][1], n) for n in WEIGHTS],
            *[shaped(out[n][2], n) for n in WEIGHTS], *[shaped(out[n][3], n) for n in WEIGHTS])
```

```python
import functools

import jax
import jax.numpy as jnp
from jax import lax
from jax.experimental import pallas as pl
from jax.experimental.pallas import tpu as pltpu

F32 = jnp.float32
BF16 = jnp.bfloat16
MESH = pl.DeviceIdType.MESH

N_DEV = 8
EPS = 1e-6
CHUNK = 64
CHUNKS_PER_STEP = 4
HEADS = 4
GLA_DK = 64
HEAD_W = 128
GLA_GATE_NORM = 16.0
LOWRANK = 16
CONV_K = 4
QKV_BLOCK = 4
LANES = 128
HALO = 8
IN_SPLITS = (256, 256, 512, 512, 16, 512, 512, 1024, 1024)

ADAM_LR = 0.001
ADAM_B1 = 0.9
ADAM_B2 = 0.999
ADAM_EPS = 1e-08
ADAM_WD = 0.01
ADAM_STEP = 10

VMEM_LIMIT = 56 * 1024 * 1024


def _cparams(*sem):
    return pltpu.CompilerParams(dimension_semantics=sem, vmem_limit_bytes=VMEM_LIMIT)


def _dims(mode, ndim):
    contract = {"nn": ((ndim - 1,), (ndim - 2,)), "nt": ((ndim - 1,), (ndim - 1,)), "tn": ((ndim - 2,), (ndim - 2,))}[mode]
    return contract, (((0,), (0,)) if ndim == 3 else ((), ()))


def _raw_dot(a, b, mode):
    return lax.dot_general(a.astype(BF16), b.astype(BF16), _dims(mode, a.ndim), preferred_element_type=F32)


@functools.partial(jax.custom_vjp, nondiff_argnums=(2,))
def _bdot(a, b, mode):
    return _raw_dot(a, b, mode)


def _bdot_fwd(a, b, mode):
    return _raw_dot(a, b, mode), (a, b)


def _bdot_bwd(mode, res, ct):
    a, b = res
    if mode == "nn":
        da, db = _raw_dot(ct, b, "nt"), _raw_dot(a, ct, "tn")
    elif mode == "nt":
        da, db = _raw_dot(ct, b, "nn"), _raw_dot(ct, a, "tn")
    else:
        da, db = _raw_dot(b, ct, "nt"), _raw_dot(a, ct, "nn")
    return da.astype(a.dtype), db.astype(b.dtype)


_bdot.defvjp(_bdot_fwd, _bdot_bwd)


def _split3(x):
    hi = x.astype(BF16)
    r1 = x - hi.astype(F32)
    mid = r1.astype(BF16)
    return hi, mid, (r1 - mid.astype(F32)).astype(BF16)


def _split_dot(tri, x):
    if x.ndim == 3:
        tri = jnp.broadcast_to(tri, (x.shape[0], *tri.shape))
    return sum(lax.dot_general(tri, t, _dims("nn", x.ndim), preferred_element_type=F32) for t in _split3(x))


def _tri(n, lower):
    r = lax.broadcasted_iota(jnp.int32, (n, n), 0)
    c = lax.broadcasted_iota(jnp.int32, (n, n), 1)
    return ((c <= r) if lower else (c >= r)).astype(BF16)


@jax.custom_vjp
def _cumsum_rows(x):
    return _split_dot(_tri(x.shape[-2], True), x)


def _cumsum_rows_fwd(x):
    return _cumsum_rows(x), None


def _cumsum_rows_bwd(_, ct):
    return (_split_dot(_tri(ct.shape[-2], False), ct),)


_cumsum_rows.defvjp(_cumsum_rows_fwd, _cumsum_rows_bwd)


def _abs(x):
    return jnp.where(x >= 0, x, -x)


def _sigmoid(x):
    return lax.logistic(x)


def _log_sigmoid(x):
    return jnp.minimum(x, 0.0) - jnp.log(1.0 + jnp.exp(-_abs(x)))


def _rms(x, g):
    return x * lax.rsqrt(jnp.mean(x * x, axis=-1, keepdims=True) + EPS) * g


def _head_slices(w):
    return [slice(h * w, (h + 1) * w) for h in range(HEADS)]


def _heads(ref, rows=slice(None)):
    return jnp.stack([ref[rows, hs] for hs in _head_slices(HEAD_W)])


def _put_heads(ref, val, rows=slice(None)):
    for h, hs in enumerate(_head_slices(HEAD_W)):
        ref[rows, hs] = val[h].astype(ref.dtype)


def _tile(dim, want):
    if dim <= want or dim % LANES:
        return dim
    t = want
    while dim % t:
        t -= LANES
    return t


def _mm(a, b, mode, out_dtype, name, tm=1024, tn=1024, tk=4096, epilogue=None, extra=(), deps=(), shards=None):
    if shards == "b":
        assert mode == "nn"
        ns = b.shape[2]
        (m, k), (k2, n) = a.shape, (b.shape[1], b.shape[0] * ns)
        tn = ns
    elif mode == "nn":
        (m, k), (k2, n) = a.shape, b.shape
    elif mode == "nt":
        (m, k), (n, k2) = a.shape, b.shape
    else:
        (k, m), (k2, n) = a.shape, b.shape
    assert k == k2, (name, a.shape, b.shape)
    tm, tn, tk = _tile(m, tm), _tile(n, tn), _tile(k, tk)
    nk = k // tk
    out_dtypes = out_dtype if epilogue else (out_dtype,)
    assert nk == 1 or (out_dtype == F32 and not epilogue), name
    n_in = 2 + len(extra)

    def body(*refs):
        p = _raw_dot(refs[0][...], refs[1][...], mode)
        if nk > 1:
            _accumulate(pl.program_id(2), [refs[n_in + len(deps)]], [p])
            return
        outs = epilogue(p, *[r[...] for r in refs[2:n_in]]) if epilogue else (p,)
        for ref, val in zip(refs[n_in + len(deps):], outs):
            ref[...] = val.astype(ref.dtype)

    a_spec = pl.BlockSpec((tk, tm), lambda i, j, kk: (kk, i)) if mode == "tn" else pl.BlockSpec((tm, tk), lambda i, j, kk: (i, kk))
    if shards == "b":
        b_spec = pl.BlockSpec((None, tk, tn), lambda i, j, kk: (j, kk, 0))
    elif mode == "nt":
        b_spec = pl.BlockSpec((tn, tk), lambda i, j, kk: (j, kk))
    else:
        b_spec = pl.BlockSpec((tk, tn), lambda i, j, kk: (kk, j))
    o_spec = pl.BlockSpec((tm, tn), lambda i, j, kk: (i, j))
    res = pl.pallas_call(
        body, name=name, grid=(m // tm, n // tn, nk),
        in_specs=[a_spec, b_spec] + [o_spec] * len(extra) + [ANY] * len(deps), out_specs=[o_spec] * len(out_dtypes),
        out_shape=[jax.ShapeDtypeStruct((m, n), dt) for dt in out_dtypes],
        compiler_params=_cparams("parallel", "parallel", "arbitrary"),
    )(a, b, *extra, *deps)
    return res if epilogue else res[0]


def _shard_pieces(widths, n):
    bounds = [0]
    for wd in widths:
        bounds.append(bounds[-1] + wd)
    assert bounds[-1] == N_DEV * n
    return [[(i, max(s * n, b) - b, max(s * n, b) - s * n, min((s + 1) * n, b + wd) - max(s * n, b))
             for i, (b, wd) in enumerate(zip(bounds, widths)) if b < (s + 1) * n and b + wd > s * n]
            for s in range(N_DEV)]


def _mm_shard_cols(a, bs, widths, n, name, row_tile, tm, deps=()):
    t = a.shape[0]
    nb = len(bs)
    pieces = _shard_pieces(widths, n)

    def body(a_ref, *rest):
        b_refs = rest[:nb]
        o_ref, at_ref = rest[nb + len(deps):]
        j = pl.program_id(0)

        @pl.when(j == 0)
        def _():
            at_ref[...] = a_ref[...].astype(BF16).T

        for s in range(N_DEV):
            @pl.when(j == s)
            def _(s=s):
                for i, c_in, c_out, wd in pieces[s]:
                    cols = min(_round_up(wd, LANES), bs[i].shape[1] - c_in) if wd < LANES else wd
                    p = _raw_dot(at_ref[...], b_refs[i][:, c_in:c_in + cols], "nn")
                    o_ref[:, c_out:c_out + wd] = p[:, 0:wd].astype(BF16)

    return pl.pallas_call(
        body, name=name, grid=(N_DEV,),
        in_specs=[pl.BlockSpec((t, tm), lambda j: (0, row_tile))]
        + [pl.BlockSpec(b.shape, lambda j: (0, 0), pipeline_mode=pl.Buffered(1)) for b in bs] + [ANY] * len(deps),
        out_specs=pl.BlockSpec((None, tm, n), lambda j: (j, 0, 0)),
        out_shape=jax.ShapeDtypeStruct((N_DEV, tm, n), BF16),
        scratch_shapes=[pltpu.VMEM((tm, t), BF16)],
        compiler_params=_cparams("arbitrary"),
    )(a, *bs, *deps)


def _round_up(v, m):
    return -(-v // m) * m


def _rowwise(name, fn, rows, params, out_rows, out_accs=(), tile=256, deps=()):
    t = rows[0].shape[0]
    r = min(tile, t)
    assert t % r == 0
    n_in, n_or = len(rows) + len(params), len(out_rows)
    n_all = n_in + len(deps)
    params = list(params) + list(deps)

    def body(*refs):
        vals = [ref[...] for ref in refs[:n_in]]
        outs = refs[n_all:]
        ro, ao = fn(*vals)
        for ref, v in zip(outs[:n_or], ro):
            ref[...] = v.astype(ref.dtype)
        if out_accs:
            _accumulate(pl.program_id(0), outs[n_or:], ao)

    def full(shape):
        return pl.BlockSpec(shape, lambda i, nd=len(shape): (0,) * nd)

    return pl.pallas_call(
        body, name=name, grid=(t // r,),
        in_specs=[pl.BlockSpec((r, a.shape[1]), lambda i: (i, 0)) for a in rows] + [full(p.shape) for p in params],
        out_specs=[pl.BlockSpec((r, w), lambda i: (i, 0)) for w, _ in out_rows] + [full(s) for s, _ in out_accs],
        out_shape=[jax.ShapeDtypeStruct((t, w), dt) for w, dt in out_rows] + [jax.ShapeDtypeStruct(s, dt) for s, dt in out_accs],
        compiler_params=_cparams("arbitrary"),
    )(*rows, *params)


def _accumulate(step, refs, vals):
    for ref, v in zip(refs, vals):
        @pl.when(step == 0)
        def _(ref=ref, v=v):
            ref[...] = v.astype(ref.dtype)

        @pl.when(step > 0)
        def _(ref=ref, v=v):
            ref[...] += v.astype(ref.dtype)


def _gla_chunk(q, k, v, la, st):
    c = q.shape[-2]
    row = lax.broadcasted_iota(jnp.int32, (c, c), 0)
    col = lax.broadcasted_iota(jnp.int32, (c, c), 1)
    cum = _cumsum_rows(la)
    cl = jnp.sum(la, axis=-2, keepdims=True)
    ep = jnp.exp(cum)
    en = jnp.exp(-cum)
    qs = q * (GLA_DK ** -0.5)
    qp = qs * ep
    a_f = _bdot(qp, k * en, "nt")
    a_b = _bdot(qs * en, k * ep, "nt")
    sc = jnp.where(row >= col, a_f, a_b)
    o = _bdot(sc, v, "nn") + _bdot(qp, st, "nt")
    kd = k * jnp.exp(cl - cum)
    st_new = st * jnp.exp(cl) + _bdot(v, kd, "tn")
    return o, st_new


def _gla_specs(nc, rev):
    nb = nc // CHUNKS_PER_STEP
    rows = CHUNKS_PER_STEP * CHUNK

    def blk(n):
        return (nb - 1 - n) if rev else n
    hm = pl.BlockSpec((HEADS, rows, GLA_DK), lambda n: (0, blk(n), 0))
    tm = pl.BlockSpec((rows, HEADS * HEAD_W), lambda n: (blk(n), 0))
    st = pl.BlockSpec((HEADS, CHUNKS_PER_STEP, HEAD_W, GLA_DK), lambda n: (0, blk(n), 0, 0))
    return nb, hm, tm, st


def _chunk_rows(c):
    return slice(c * CHUNK, (c + 1) * CHUNK)


def _gla_fwd(q, k, v, la, deps=()):
    t = v.shape[0]
    nc = t // CHUNK
    nb, hm, tm, st = _gla_specs(nc, False)

    def body(q_ref, k_ref, v_ref, la_ref, *rest):
        o_ref, sp_ref, st_ref = rest[len(deps):]

        @pl.when(pl.program_id(0) == 0)
        def _():
            st_ref[...] = jnp.zeros_like(st_ref)

        s = st_ref[...]
        for c in range(CHUNKS_PER_STEP):
            r = _chunk_rows(c)
            sp_ref[:, c] = s
            o, s = _gla_chunk(q_ref[:, r], k_ref[:, r], _heads(v_ref, r), la_ref[:, r], s)
            _put_heads(o_ref, o, r)
        st_ref[...] = s

    return pl.pallas_call(
        body, name="gla_fwd", grid=(nb,),
        in_specs=[hm, hm, tm, hm] + [ANY] * len(deps), out_specs=[tm, st],
        out_shape=[jax.ShapeDtypeStruct((t, HEADS * HEAD_W), F32), jax.ShapeDtypeStruct((HEADS, nc, HEAD_W, GLA_DK), F32)],
        scratch_shapes=[pltpu.VMEM((HEADS, HEAD_W, GLA_DK), F32)],
        compiler_params=_cparams("arbitrary"),
    )(q, k, v, la, *deps)


def _gla_bwd(q, k, v, la, sp, do):
    t = v.shape[0]
    nc = t // CHUNK
    nb, hm, tm, st = _gla_specs(nc, True)

    def body(q_ref, k_ref, v_ref, la_ref, sp_ref, do_ref, dq_ref, dk_ref, dv_ref, dla_ref, ds_ref):
        @pl.when(pl.program_id(0) == 0)
        def _():
            ds_ref[...] = jnp.zeros_like(ds_ref)

        ds = ds_ref[...]
        for c in reversed(range(CHUNKS_PER_STEP)):
            r = _chunk_rows(c)
            _, vjp = jax.vjp(_gla_chunk, q_ref[:, r], k_ref[:, r], _heads(v_ref, r), la_ref[:, r], sp_ref[:, c])
            dq, dk, dv, dla, ds = vjp((_heads(do_ref, r), ds))
            dq_ref[:, r] = dq.astype(dq_ref.dtype)
            dk_ref[:, r] = dk.astype(dk_ref.dtype)
            _put_heads(dv_ref, dv, r)
            dla_ref[:, r] = dla
        ds_ref[...] = ds

    hm_shape = jax.ShapeDtypeStruct((HEADS, t, GLA_DK), BF16)
    return pl.pallas_call(
        body, name="gla_bwd", grid=(nb,),
        in_specs=[hm, hm, tm, hm, st, tm], out_specs=[hm, hm, tm, hm],
        out_shape=[hm_shape, hm_shape, jax.ShapeDtypeStruct((t, HEADS * HEAD_W), BF16),
                   jax.ShapeDtypeStruct((HEADS, t, GLA_DK), F32)],
        scratch_shapes=[pltpu.VMEM((HEADS, HEAD_W, GLA_DK), F32)],
        compiler_params=_cparams("arbitrary"),
    )(q, k, v, la, sp, do)


def _ml_chunk(q, k, v, li_r, lf_r, cm, nv, m):
    c = q.shape[-2]
    row = lax.broadcasted_iota(jnp.int32, (c, c), 0)
    col = lax.broadcasted_iota(jnp.int32, (c, c), 1)
    eye = (row == col).astype(F32)
    li_c = jnp.sum(eye * li_r, axis=-1, keepdims=True)
    lf_c = jnp.sum(eye * lf_r, axis=-1, keepdims=True)
    fc_c = jnp.sum((col <= row).astype(F32) * lf_r, axis=-1, keepdims=True)
    fc_r = jnp.sum((row <= col).astype(F32) * lf_c, axis=-2, keepdims=True)
    f_last = jnp.sum(lf_r, axis=-1, keepdims=True)
    kc = k * (HEAD_W ** -0.5)
    a_c = f_last - fc_c + li_c
    m_loc = jnp.max(a_c, axis=-2, keepdims=True)
    kw = kc * jnp.exp(a_c - m_loc)
    c_chunk = _bdot(kw, v, "tn")
    n_chunk = jnp.sum(kw, axis=-2, keepdims=True)
    m_new = jnp.maximum(f_last + m, m_loc)
    sp = jnp.exp(f_last + m - m_new)
    sl = jnp.exp(m_loc - m_new)
    cm_new = sp * cm + sl * c_chunk
    nv_new = sp * nv + sl * n_chunk
    log_d = li_r - _abs(fc_c - fc_r)
    g_inter = fc_c + m
    m_t = jnp.maximum(g_inter, jnp.max(log_d, axis=-1, keepdims=True))
    s = _bdot(q, kc, "nt") * jnp.exp(log_d - m_t)
    sc = jnp.exp(g_inter - m_t)
    num = _bdot(s, v, "nn") + sc * _bdot(q, cm, "nn")
    den = jnp.sum(s, axis=-1, keepdims=True) + sc * jnp.sum(q * nv, axis=-1, keepdims=True)
    den = jnp.maximum(_abs(den), jnp.exp(-m_t))
    return num / den, cm_new, nv_new, m_new


def _ml_specs(nc, rev):
    nb = nc // CHUNKS_PER_STEP

    def blk(n):
        return (nb - 1 - n) if rev else n
    tm = pl.BlockSpec((CHUNKS_PER_STEP * CHUNK, HEADS * HEAD_W), lambda n: (blk(n), 0))
    gate = pl.BlockSpec((HEADS, CHUNKS_PER_STEP, 1, CHUNK), lambda n: (0, blk(n), 0, 0))
    cm = pl.BlockSpec((HEADS, CHUNKS_PER_STEP, HEAD_W, HEAD_W), lambda n: (0, blk(n), 0, 0))
    vec = pl.BlockSpec((HEADS, CHUNKS_PER_STEP, 1, HEAD_W), lambda n: (0, blk(n), 0, 0))
    return nb, tm, gate, cm, vec


_ML_STATE = [pltpu.VMEM((HEADS, HEAD_W, HEAD_W), F32), pltpu.VMEM((HEADS, 1, HEAD_W), F32), pltpu.VMEM((HEADS, 1, HEAD_W), F32)]


def _ml_fwd(q, k, v, li, lf):
    t = q.shape[0]
    nc = t // CHUNK
    nb, tm, gate, cm, vec = _ml_specs(nc, False)

    def body(q_ref, k_ref, v_ref, li_ref, lf_ref, hc_ref, cp_ref, np_ref, mp_ref, c_ref, n_ref, m_ref):
        @pl.when(pl.program_id(0) == 0)
        def _():
            c_ref[...] = jnp.zeros_like(c_ref)
            n_ref[...] = jnp.zeros_like(n_ref)
            m_ref[...] = jnp.zeros_like(m_ref)

        cs, ns, ms = c_ref[...], n_ref[...], m_ref[...][:, :, 0:1]
        for c in range(CHUNKS_PER_STEP):
            r = _chunk_rows(c)
            cp_ref[:, c] = cs
            np_ref[:, c] = ns
            mp_ref[:, c] = jnp.broadcast_to(ms, m_ref.shape)
            hc, cs, ns, ms = _ml_chunk(_heads(q_ref, r), _heads(k_ref, r), _heads(v_ref, r), li_ref[:, c], lf_ref[:, c],
                                       cs, ns, ms)
            _put_heads(hc_ref, hc, r)
        c_ref[...] = cs
        n_ref[...] = ns
        m_ref[...] = jnp.broadcast_to(ms, m_ref.shape)

    return pl.pallas_call(
        body, name="mlstm_fwd", grid=(nb,),
        in_specs=[tm, tm, tm, gate, gate], out_specs=[tm, cm, vec, vec],
        out_shape=[jax.ShapeDtypeStruct((t, HEADS * HEAD_W), F32), jax.ShapeDtypeStruct((HEADS, nc, HEAD_W, HEAD_W), F32),
                   jax.ShapeDtypeStruct((HEADS, nc, 1, HEAD_W), F32), jax.ShapeDtypeStruct((HEADS, nc, 1, HEAD_W), F32)],
        scratch_shapes=_ML_STATE,
        compiler_params=_cparams("arbitrary"),
    )(q, k, v, li, lf)


def _ml_bwd(q, k, v, li, lf, cp, npv, mp, dhc):
    t = q.shape[0]
    nc = t // CHUNK
    nb, tm, gate, cm, vec = _ml_specs(nc, True)

    def body(q_ref, k_ref, v_ref, li_ref, lf_ref, cp_ref, np_ref, mp_ref, dhc_ref,
             dq_ref, dk_ref, dv_ref, dli_ref, dlf_ref, dc_ref, dn_ref, dm_ref):
        @pl.when(pl.program_id(0) == 0)
        def _():
            dc_ref[...] = jnp.zeros_like(dc_ref)
            dn_ref[...] = jnp.zeros_like(dn_ref)
            dm_ref[...] = jnp.zeros_like(dm_ref)

        dc, dn, dm = dc_ref[...], dn_ref[...], dm_ref[...][:, :, 0:1]
        for c in reversed(range(CHUNKS_PER_STEP)):
            r = _chunk_rows(c)
            _, vjp = jax.vjp(_ml_chunk, _heads(q_ref, r), _heads(k_ref, r), _heads(v_ref, r), li_ref[:, c], lf_ref[:, c],
                             cp_ref[:, c], np_ref[:, c], mp_ref[:, c][:, :, 0:1])
            dq, dk, dv, dli, dlf, dc, dn, dm = vjp((_heads(dhc_ref, r), dc, dn, dm))
            _put_heads(dq_ref, dq, r)
            _put_heads(dk_ref, dk, r)
            _put_heads(dv_ref, dv, r)
            dli_ref[:, c] = dli
            dlf_ref[:, c] = dlf
        dc_ref[...] = dc
        dn_ref[...] = dn
        dm_ref[...] = jnp.broadcast_to(dm, dm_ref.shape)

    tm_shape = jax.ShapeDtypeStruct((t, HEADS * HEAD_W), F32)
    gate_shape = jax.ShapeDtypeStruct((HEADS, nc, 1, CHUNK), F32)
    return pl.pallas_call(
        body, name="mlstm_bwd", grid=(nb,),
        in_specs=[tm, tm, tm, gate, gate, cm, vec, vec, tm], out_specs=[tm, tm, tm, gate, gate],
        out_shape=[tm_shape, tm_shape, tm_shape, gate_shape, gate_shape],
        scratch_shapes=_ML_STATE,
        compiler_params=_cparams("arbitrary"),
    )(q, k, v, li, lf, cp, npv, mp, dhc)


def _ml_pre(s0, s1, s2, s3, cw0, cw1, cw2, cw3, cb, wq, wk, wv, wiq, wik, wiv, bif):
    pre = cb + cw0 * s0 + cw1 * s1 + cw2 * s2 + cw3 * s3
    xc = pre * _sigmoid(pre)
    q = _bdot(xc, wq, "nn")
    k = _bdot(xc, wk, "nn")
    v = _bdot(s3, wv, "nn")
    gates = _bdot(q, wiq, "nn") + _bdot(k, wik, "nn") + _bdot(v, wiv, "nn") + bif
    lane = lax.broadcasted_iota(jnp.int32, gates.shape, 1)
    gl = jnp.where(lane < HEADS, gates, _log_sigmoid(gates))
    return xc, q, k, v, gl


def _delayed(xs_ref, x_ref, halo_ref, r):
    xs_ref[0:HALO, :] = halo_ref[...]
    xs_ref[HALO:HALO + r, :] = x_ref[...]
    return [xs_ref[pl.ds(HALO - (CONV_K - 1) + j, r), :] for j in range(CONV_K)]


def _full_spec(shape):
    return pl.BlockSpec(shape, lambda i, nd=len(shape): (0,) * nd)


def _ml_pre_fwd(x_m, x_pad, params, tile=256, deps=()):
    t, w = x_m.shape
    r = min(tile, t)

    def body(*refs):
        x_ref, halo_ref = refs[:2]
        p = [ref[...] for ref in refs[2:2 + len(params)]]
        outs = refs[2 + len(params) + len(deps):-1]
        res = _ml_pre(*_delayed(refs[-1], x_ref, halo_ref, r), *p)
        for ref, val in zip(outs, res):
            ref[...] = val

    row = pl.BlockSpec((r, w), lambda i: (i, 0))
    return pl.pallas_call(
        body, name="ml_pre_fwd", grid=(t // r,),
        in_specs=[row, pl.BlockSpec((HALO, w), lambda i: (i * (r // HALO), 0))] + [_full_spec(p.shape) for p in params]
        + [ANY] * len(deps),
        out_specs=[row] * 4 + [pl.BlockSpec((r, LANES), lambda i: (i, 0))],
        out_shape=[jax.ShapeDtypeStruct((t, w), F32)] * 4 + [jax.ShapeDtypeStruct((t, LANES), F32)],
        scratch_shapes=[pltpu.VMEM((r + HALO, w), F32)],
        compiler_params=_cparams("arbitrary"),
    )(x_m, x_pad, *params, *deps)


def _ml_pre_bwd(x_m, x_pad, params, cts, tile=256):
    t, w = x_m.shape
    r = min(tile, t)
    nt = t // r
    n_p = len(params)

    def body(*refs):
        x_ref, halo_ref = refs[:2]
        p = [ref[...] for ref in refs[2:2 + n_p]]
        ct = [ref[...] for ref in refs[2 + n_p:7 + n_p]]
        dx_ref = refs[7 + n_p]
        dp_refs = refs[8 + n_p:8 + 2 * n_p]
        xs_ref, ds_ref, carry_ref = refs[8 + 2 * n_p:]
        step = pl.program_id(0)

        @pl.when(step == 0)
        def _():
            ds_ref[...] = jnp.zeros_like(ds_ref)
            carry_ref[...] = jnp.zeros_like(carry_ref)

        _, vjp = jax.vjp(_ml_pre, *_delayed(xs_ref, x_ref, halo_ref, r), *p)
        grads = vjp(tuple(ct))
        for j in range(CONV_K):
            ds_ref[j, HALO:HALO + r, :] = grads[j]
        lead = HALO + CONV_K - 1
        d_tile = sum(ds_ref[j, pl.ds(lead - j, r), :] for j in range(CONV_K))
        d_halo = sum(ds_ref[j, pl.ds(CONV_K - 1 - j, HALO), :] for j in range(CONV_K))
        dx_ref[...] = jnp.concatenate([d_tile[:r - HALO], d_tile[r - HALO:] + carry_ref[...]], axis=0).astype(dx_ref.dtype)
        carry_ref[...] = d_halo
        _accumulate(step, dp_refs, grads[CONV_K:])

    row = pl.BlockSpec((r, w), lambda i: (nt - 1 - i, 0))
    return pl.pallas_call(
        body, name="ml_pre_bwd", grid=(nt,),
        in_specs=[row, pl.BlockSpec((HALO, w), lambda i: ((nt - 1 - i) * (r // HALO), 0))] + [_full_spec(p.shape) for p in params]
        + [row] * 4 + [pl.BlockSpec((r, LANES), lambda i: (nt - 1 - i, 0))],
        out_specs=[row] + [_full_spec(p.shape) for p in params],
        out_shape=[jax.ShapeDtypeStruct((t, w), BF16)] + [jax.ShapeDtypeStruct(p.shape, F32) for p in params],
        scratch_shapes=[pltpu.VMEM((r + HALO, w), F32), pltpu.VMEM((CONV_K, r + 2 * HALO, w), F32), pltpu.VMEM((HALO, w), F32)],
        compiler_params=_cparams("arbitrary"),
    )(x_m, x_pad, *params, *cts)


def _per_head(fn, row_vals, head_params, shared_params=()):
    return [fn(*[a[:, hs] for a in row_vals], *[p[:, hs] for p in head_params], *shared_params) for hs in _head_slices(HEAD_W)]


def _gla_out(o, g, gn):
    return _rms(o, gn) * (g * _sigmoid(g))


def _ml_out(hc, op, xc, g, sk):
    hcell = hc * _sigmoid(op)
    mu = jnp.mean(hcell, axis=-1, keepdims=True)
    d = hcell - mu
    var = jnp.mean(d * d, axis=-1, keepdims=True)
    return d * lax.rsqrt(var + EPS) * g + sk * xc


def _log_decay(al, w, b):
    return _log_sigmoid(_bdot(al, w, "nn") + b) * (1.0 / GLA_GATE_NORM)


def _merge(ga, gb, ya, yb):
    return _sigmoid(ga) * ya + _sigmoid(gb) * yb


def _post_mix(x, z, gpm, gpl):
    x1 = x + _rms(z, gpm)
    return x1, _rms(x1, gpl)


def _loss_rows(x1, dn, tgt, g):
    e = x1 + _rms(dn, g) - tgt
    return 0.5 * jnp.sum(jnp.mean(e * e, axis=-1, keepdims=True), axis=0, keepdims=True)


def _lin(p):
    return 4 * p[0] + 2 * p[1] + p[2]


def _me():
    return lax.axis_index("x"), lax.axis_index("y"), lax.axis_index("c")


def _flip(p, k):
    return tuple((1 - v) if (k >> (2 - i)) & 1 else v for i, v in enumerate(p))


ANY = pl.BlockSpec(memory_space=pl.ANY)


HBM = pl.BlockSpec(memory_space=pltpu.HBM)
SEM = pl.BlockSpec(memory_space=pltpu.SEMAPHORE)
DATAFLOW = pltpu.SideEffectType.DATAFLOW_SIDE_EFFECTING


SIBLING = 1
OTHER_CHIPS = (2, 4, 6)


def _peer_copies(kinds, srcs, lands, send_sems, recv_sems):
    me = _me()
    copies = []
    for a, (kind, src, land) in enumerate(zip(kinds, srcs, lands)):
        masks = {"gather": range(1, N_DEV), "exchange": range(1, N_DEV), "gather_chips": (SIBLING, *OTHER_CHIPS),
                 "gather_pass": OTHER_CHIPS}[kind]
        for k in masks:
            peer = _flip(me, k)
            if kind == "gather_pass":
                block = land.at[_lin(peer)]
                src_ref, dst_ref, target = block, block, _flip(me, SIBLING)
            else:
                src_ref, dst_ref, target = (src.at[_lin(peer)] if kind == "exchange" else src), land.at[_lin(me)], peer
            copies.append(pltpu.make_async_remote_copy(
                src_ref=src_ref, dst_ref=dst_ref, send_sem=send_sems.at[a * 7 + k - 1], recv_sem=recv_sems.at[a * 7 + k - 1],
                device_id=target, device_id_type=MESH))
    return copies


def _copies_start(kind, srcs, name, after=None, lands=None):
    n = len(srcs)
    extra = [] if after is None else [after]
    kind = [kind] * n if isinstance(kind, str) else list(kind)
    land_shapes = [(s.shape if k == "exchange" else (N_DEV, *s.shape)) for k, s in zip(kind, srcs)]
    lands = [lax.empty(ls, s.dtype) for ls, s in zip(land_shapes, srcs)] if lands is None else lands

    def body(*refs):
        sems = refs[2 * n + len(extra):]
        for cp in _peer_copies(kind, refs[:n], refs[n:2 * n], sems[0], sems[1]):
            cp.start()
        refs[-1][...] = jnp.zeros_like(refs[-1])

    def hbm(a):
        return pltpu.with_memory_space_constraint(a, pltpu.HBM)

    out = pl.pallas_call(
        body, name=name,
        out_shape=(pltpu.SemaphoreType.DMA((7 * n,)), pltpu.SemaphoreType.DMA((7 * n,)),
                   *[pltpu.HBM(s.shape, s.dtype) for s in srcs],
                   *[pltpu.HBM(ls, s.dtype) for ls, s in zip(land_shapes, srcs)],
                   jax.ShapeDtypeStruct((8, LANES), F32)),
        in_specs=[HBM] * (2 * n) + [ANY] * len(extra),
        out_specs=(SEM, SEM, *[HBM] * (2 * n), pl.BlockSpec(memory_space=pltpu.VMEM)),
        input_output_aliases={i: 2 + i for i in range(2 * n)},
        compiler_params=pltpu.CompilerParams(has_side_effects=DATAFLOW),
    )(*[hbm(s) for s in srcs], *[hbm(a) for a in lands], *extra)
    return (kind, n, out[:-1]), out[-1]


def _copies_wait(state, after, name):
    kind, n, (send_sems, recv_sems, *thru) = state
    after = list(after) if isinstance(after, (list, tuple)) else [after]

    def body(*refs):
        for cp in _peer_copies(kind, refs[:n], refs[n:2 * n], refs[2 * n], refs[2 * n + 1]):
            cp.wait_send()
            cp.wait_recv()

    out = pl.pallas_call(
        body, name=name,
        out_shape=tuple(pltpu.HBM(t.shape, t.dtype) for t in thru),
        in_specs=[HBM] * (2 * n) + [SEM, SEM] + [ANY] * len(after), out_specs=tuple([HBM] * (2 * n)),
        input_output_aliases={i: i for i in range(2 * n)},
        compiler_params=pltpu.CompilerParams(has_side_effects=DATAFLOW),
    )(*thru, send_sems, recv_sems, *after)
    return out[:n], out[n:]


def _adamw(w, g, m, v):
    m2 = ADAM_B1 * m + (1.0 - ADAM_B1) * g
    v2 = ADAM_B2 * v + (1.0 - ADAM_B2) * (g * g)
    m_hat = m2 / (1.0 - ADAM_B1 ** ADAM_STEP)
    v_hat = v2 / (1.0 - ADAM_B2 ** ADAM_STEP)
    delta = -ADAM_LR * (m_hat / (jnp.sqrt(v_hat) + ADAM_EPS) + ADAM_WD * w)
    return delta, m2, v2


def _sum_adamw(lands, parts, me_idx, w, m, v, name, tile=256):
    r, c = w.shape
    nchunks = len(lands)
    tr = min(tile, r // nchunks)
    per_chunk = r // nchunks // tr
    per = 1 + N_DEV

    def body(me_ref, *refs):
        w_ref, m_ref, v_ref, g_ref, d_ref, m2_ref, v2_ref = refs[nchunks * per:]
        for k in range(nchunks):
            own_ref, slots = refs[k * per], refs[k * per + 1:(k + 1) * per]

            @pl.when(pl.program_id(0) // per_chunk == k)
            def _(own_ref=own_ref, slots=slots):
                own = own_ref[...].astype(F32)
                g = None
                for s in range(N_DEV):
                    term = jnp.where(me_ref[0] == s, own, slots[s][...].astype(F32))
                    g = term if g is None else g + term
                d, m2, v2 = _adamw(w_ref[...], g, m_ref[...], v_ref[...])
                g_ref[...] = g
                d_ref[...] = d
                m2_ref[...] = m2
                v2_ref[...] = v2

    def chunk_specs(k):
        def tile_of(i):
            return jnp.clip(i - k * per_chunk, 0, per_chunk - 1)

        def slot_spec(s):
            return pl.BlockSpec((None, tr, c), lambda i, me: (jnp.where(me[0] == s, (s + 1) % N_DEV, s), tile_of(i), 0))
        return [pl.BlockSpec((None, tr, c), lambda i, me: (me[0], tile_of(i), 0))] + [slot_spec(s) for s in range(N_DEV)]

    row = pl.BlockSpec((tr, c), lambda i, me: (i, 0))
    operands = [a for land, part in zip(lands, parts) for a in (part, *[land] * N_DEV)]
    return pl.pallas_call(
        body, name=name,
        grid_spec=pltpu.PrefetchScalarGridSpec(
            num_scalar_prefetch=1, grid=(r // tr,),
            in_specs=[s for k in range(nchunks) for s in chunk_specs(k)] + [row] * 3,
            out_specs=[row] * 4),
        out_shape=[jax.ShapeDtypeStruct((r, c), F32)] * 4,
        compiler_params=_cparams("parallel"),
    )(me_idx, *operands, w, m, v)


def _small_update(name, me_idx, kinds, lands, owns, ws, ms, vs, sums=()):
    n = len(ws)
    lands, owns = list(lands) + [s[0] for s in sums], list(owns) + [s[1] for s in sums]
    kinds = list(kinds) + ["gather"] * len(sums)
    nl = len(lands)

    def summed(me, land_ref, own):
        g = None
        for s in range(N_DEV):
            term = jnp.where(me == s, own, land_ref[s])
            g = term if g is None else g + term
        return g

    def body(me_ref, *refs):
        land_refs, own_refs = refs[:nl], refs[nl:2 * nl]
        w_refs, m_refs, v_refs = (refs[2 * nl + i * n:2 * nl + (i + 1) * n] for i in range(3))
        outs = refs[2 * nl + 3 * n:]
        me = me_ref[0]
        for i in range(n):
            g = summed(me, land_refs[i], own_refs[i][...])
            d, m2, v2 = _adamw(w_refs[i][...], g, m_refs[i][...], v_refs[i][...])
            for ref, val in zip(outs[4 * i:4 * i + 4], (g, d, m2, v2)):
                ref[...] = val
        for i in range(n, nl):
            outs[4 * n + i - n][...] = summed(me, land_refs[i], own_refs[i][...])

    def whole(shape):
        return pl.BlockSpec(shape, lambda i, me, nd=len(shape): (0,) * nd)

    def own_spec(kind, own):
        if kind == "gather":
            return whole(own.shape)
        return pl.BlockSpec((None, *own.shape[1:]), lambda i, me: (me[0], 0, 0))

    shapes = [w.shape for w in ws]
    out_shapes = [s for s in shapes for _ in range(4)] + [s[1].shape for s in sums]
    return pl.pallas_call(
        body, name=name,
        grid_spec=pltpu.PrefetchScalarGridSpec(
            num_scalar_prefetch=1, grid=(1,),
            in_specs=[whole(a.shape) for a in lands] + [own_spec(k, o) for k, o in zip(kinds, owns)]
            + [whole(s) for s in shapes] * 3,
            out_specs=[whole(s) for s in out_shapes]),
        out_shape=[jax.ShapeDtypeStruct(s, F32) for s in out_shapes],
        compiler_params=_cparams("arbitrary"),
    )(me_idx, *lands, *owns, *ws, *ms, *vs)


def _small_view(n, a):
    if a.ndim == 1:
        return a.reshape(1, -1)
    if a.ndim == 3:
        return a.transpose(1, 2, 0).reshape(QKV_BLOCK * QKV_BLOCK, -1)
    return a.T if n == "w_if" else a


def _small_unview(n, a, shape):
    if len(shape) == 1:
        return a.reshape(shape)
    if len(shape) == 3:
        return a.reshape(QKV_BLOCK, QKV_BLOCK, -1).transpose(2, 0, 1)
    return a.T if n == "w_if" else a


def _small_shards(n, g):
    if n == "w_if":
        return g.reshape(N_DEV, -1, g.shape[1]).transpose(0, 2, 1)
    return g.reshape(g.shape[0], N_DEV, -1).transpose(1, 0, 2)


def _small_unshard(n, s):
    if n == "w_if":
        return s.transpose(0, 2, 1).reshape(-1, s.shape[1])
    return s.transpose(1, 0, 2).reshape(s.shape[1], -1)


def _to_hm(a, d):
    t = a.shape[0]
    return a.reshape(t, HEADS, d).transpose(1, 0, 2)


def _from_hm(a):
    h, t, d = a.shape
    return a.transpose(1, 0, 2).reshape(t, h * d)


def _gate_rows(g):
    t = g.shape[0]
    return g.T.reshape(HEADS, t // CHUNK, 1, CHUNK)


def _gate_cols(g):
    h, nc, _, c = g.shape
    return g.reshape(h, nc * c).T


def _blockdiag_dense(w):
    n = w.shape[0] * QKV_BLOCK
    tiled = jnp.tile(w.reshape(n, QKV_BLOCK), (1, n // QKV_BLOCK))
    r = lax.broadcasted_iota(jnp.int32, (n, n), 0)
    c = lax.broadcasted_iota(jnp.int32, (n, n), 1)
    return jnp.where(r // QKV_BLOCK == c // QKV_BLOCK, tiled, 0.0)


def _blockdiag_blocks(dense):
    n = dense[0].shape[0]
    k = len(dense)

    def body(*refs):
        r = lax.broadcasted_iota(jnp.int32, (n, n), 0)
        c = lax.broadcasted_iota(jnp.int32, (n, n), 1)
        fr = lax.broadcasted_iota(jnp.int32, (n, LANES), 0)
        fc = lax.broadcasted_iota(jnp.int32, (n, LANES), 1)
        fold = ((fr & (QKV_BLOCK - 1)) == fc).astype(BF16)
        for i in range(k):
            kept = jnp.where((r >> 2) == (c >> 2), refs[i][...], 0.0)
            refs[k + i][...] = sum(lax.dot_general(t, fold, _dims("nn", 2), preferred_element_type=F32) for t in _split3(kept))

    out = pl.pallas_call(body, name="blockdiag_blocks", out_shape=[jax.ShapeDtypeStruct((n, LANES), F32)] * k)(*dense)
    return [o[:, 0:QKV_BLOCK].reshape(n // QKV_BLOCK, QKV_BLOCK, QKV_BLOCK) for o in out]


def _col_blocks(w):
    k, n = w.shape
    return w.reshape(k, N_DEV, n // N_DEV).transpose(1, 0, 2)


def _from_col_blocks(g):
    d, k, n = g.shape
    return g.transpose(1, 0, 2).reshape(k, d * n)


def _first_norm(x, g):
    return _rowwise("pre_mix_norm", lambda xv, gv: ((_rms(xv, gv),), ()), [x], [g], [(x.shape[1], BF16)])[0]


def _local_step(x, h, tgt, weight, ws, prefetch, pass_on, on_grads, on_small):
    t, d = x.shape
    g1 = ws["g_pre_mix"]

    def dep(token):
        return () if token is None else (token,)

    w_in = weight("w_in", x)
    fetch_mix = prefetch(("w_pa", "w_pb", "w_o"), w_in)

    n_in = w_in.shape[2]

    offs = [0]
    for s in IN_SPLITS:
        offs.append(offs[-1] + s)

    def proj_in_fwd(hv, w):
        proj = jnp.concatenate([_raw_dot(hv, w[j], "nn") for j in range(N_DEV)], axis=1)
        parts = [proj[:, offs[i]:offs[i + 1]] for i in range(len(IN_SPLITS))]
        parts[4] = jnp.concatenate([parts[4], jnp.zeros((parts[4].shape[0], LANES - LOWRANK), F32)], axis=1)
        return parts, ()

    widths = [LANES if s == LOWRANK else s for s in IN_SPLITS]
    q_a, k_a, v_a, g_a, a_low_p, x_m, o_pre, gate_a, gate_b = _rowwise(
        "proj_in", proj_in_fwd, [h], [w_in], [(wd, F32) for wd in widths], deps=dep(fetch_mix))

    w_a_up_p = jnp.pad(ws["w_a_up"], ((0, LANES - LOWRANK), (0, 0)))
    b_a_up = ws["b_a_up"]
    (la,) = _rowwise("gla_decay", lambda al, w, b: ((_log_decay(al, w, b),), ()), [a_low_p], [w_a_up_p, b_a_up],
                     [(HEADS * GLA_DK, F32)])
    fetch_up = prefetch(("w_up", "w_down"), la)
    q_hm, k_hm, la_hm = _to_hm(q_a, GLA_DK), _to_hm(k_a, GLA_DK), _to_hm(la, GLA_DK)
    o_gla, s_prev = _gla_fwd(q_hm, k_hm, v_a, la_hm, deps=dep(fetch_up))
    pass_mix = pass_on("w_pa", o_gla)
    gn = ws["g_gla_norm"]
    ml_w = HEADS * HEAD_W

    cw = ws["conv_w"]
    w_if_p = jnp.pad(ws["w_if"], ((0, 0), (0, LANES - 2 * HEADS)))
    pre_params = [cw[0:1], cw[1:2], cw[2:3], cw[3:4], ws["conv_b"],
                  _blockdiag_dense(ws["w_q_ml"]), _blockdiag_dense(ws["w_k_ml"]), _blockdiag_dense(ws["w_v_ml"]),
                  w_if_p[0:ml_w], w_if_p[ml_w:2 * ml_w], w_if_p[2 * ml_w:3 * ml_w],
                  jnp.pad(ws["b_if"], ((0, 0), (0, LANES - 2 * HEADS)))]
    x_pad = jnp.pad(x_m, ((HALO, 0), (0, 0)))
    xc, q_m, k_m, v_m, gl = _ml_pre_fwd(x_m, x_pad, pre_params, deps=dep(pass_mix))
    li, lf = _gate_rows(gl[:, 0:HEADS]), _gate_rows(gl[:, HEADS:2 * HEADS])
    hc, c_prev, n_prev, m_prev = _ml_fwd(q_m, k_m, v_m, li, lf)
    g_ml, skip = ws["g_ml_norm"], ws["ml_skip"]

    def proj_a_fwd(o, g, n_, w):
        ya = jnp.concatenate(_per_head(_gla_out, [o, g], [], [n_]), axis=1)
        return (ya, _raw_dot(ya, w, "nn")), ()

    ya_in, y_a = _rowwise("proj_a", proj_a_fwd, [o_gla, g_a], [gn, weight("w_pa", hc)], [(ml_w, BF16), (d, F32)],
                          tile=512)

    def proj_b_fwd(a, b, c_, ga, gb, ya, g, s, w):
        hb = jnp.concatenate(_per_head(_ml_out, [a, b, c_], [g, s]), axis=1)
        yb = _raw_dot(hb, w, "nn")
        return (hb, yb, _merge(ga, gb, ya, yb)), ()

    h_b, y_b, merged = _rowwise("proj_b", proj_b_fwd, [hc, o_pre, xc, gate_a, gate_b, y_a], [g_ml, skip, weight("w_pb", hc)],
                                [(ml_w, BF16), (d, F32), (d, BF16)], tile=512)

    gpm, gpl, gpo = ws["g_post_mix"], ws["g_pre_mlp"], ws["g_post_mlp"]

    def proj_o_fwd(mg, xv, w, a, b):
        zv = _raw_dot(mg, w, "nn")
        return (zv, *_post_mix(xv, zv, a, b)), ()

    pass_up = pass_on("w_up", merged)
    z, x1, h2 = _rowwise("proj_o", proj_o_fwd, [merged, x], [weight("w_o", merged), gpm, gpl],
                         [(d, F32), (d, F32), (d, BF16)], tile=512, deps=dep(pass_up))
    w_up = weight("w_up", h2)
    up, u = _mm(h2, w_up, "nn", (BF16, BF16), "mlp_up", tm=2048, shards="b",
                epilogue=lambda p: (p, jnp.square(jnp.maximum(p, 0.0))))

    def mlp_down_loss(uv, x1v, tgtv, w, g):
        dnv = _raw_dot(uv, w, "nn")
        loss, vjp = jax.vjp(lambda a, b, c_: _loss_rows(a, b, tgtv, c_), x1v, dnv, g)
        dx1, ddn, dg = vjp(jnp.ones((1, 1), F32))
        return (dx1, ddn), (jnp.broadcast_to(loss, (1, LANES)), dg)

    dx1_y, d_dn, loss, d_gpo = _rowwise("mlp_down", mlp_down_loss, [u, x1, tgt], [weight("w_down", u), gpo],
                                        [(d, F32), (d, BF16)], [((1, LANES), F32), ((1, d), F32)], tile=512)

    (d_up,) = _mm(d_dn, weight("w_down", u), "nt", (BF16,), "mlp_down_dx", extra=[up],
                  epilogue=lambda p, a: (p * (2.0 * jnp.maximum(a.astype(F32), 0.0)),))
    dw_down = _mm(u, d_dn, "tn", BF16, "mlp_down_dw", tm=512)
    dw_up = _mm_shard_cols(h2, [d_up], [d_up.shape[1]], w_up.shape[2], "mlp_up_dw", 0, d)
    sent_mlp = on_grads(dict(w_down=dw_down, w_up=dw_up))

    def mlp_up_dx(dup, xv, zv, dx1, w, a, b):
        _, vjp = jax.vjp(_post_mix, xv, zv, a, b)
        ns = w.shape[2]
        dh2 = sum(_raw_dot(dup[:, j * ns:(j + 1) * ns], w[j], "nt") for j in range(w.shape[0]))
        dx, dz, da, db = vjp((dx1, dh2))
        return (dx, dz), (da, db)

    dx_res, d_z, d_gpm, d_gpl = _rowwise("mlp_up_dx", mlp_up_dx, [d_up, x, z, dx1_y], [w_up, gpm, gpl],
                                         [(d, F32), (d, BF16)], [((1, d), F32), ((1, d), F32)], tile=512, deps=dep(sent_mlp))
    dw_o = _mm(merged, d_z, "tn", BF16, "proj_o_dw")

    def proj_o_dx(dz, ga, gb, ya, yb, w):
        return jax.vjp(_merge, ga, gb, ya, yb)[1](_raw_dot(dz, w, "nt")), ()

    d_ga, d_gb, d_ya, d_yb = _rowwise("proj_o_dx", proj_o_dx, [d_z, gate_a, gate_b, y_a, y_b], [weight("w_o", merged)],
                                      [(d, BF16)] * 4, tile=512)
    dw_pa = _mm(ya_in, d_ya, "tn", BF16, "proj_a_dw")
    dw_pb = _mm(h_b, d_yb, "tn", BF16, "proj_b_dw")
    sent_mix = on_grads(dict(w_o=dw_o, w_pa=dw_pa, w_pb=dw_pb))

    def proj_b_dx(dyb, a, b, c_, w, g, s):
        ct = _raw_dot(dyb, w, "nt")
        parts = []
        for hs in _head_slices(HEAD_W):
            _, vjp = jax.vjp(_ml_out, a[:, hs], b[:, hs], c_[:, hs], g[:, hs], s[:, hs])
            parts.append(vjp(ct[:, hs]))
        cat = lambda i: jnp.concatenate([p[i] for p in parts], axis=1)
        return (cat(0), cat(1), cat(2)), (cat(3), cat(4))

    d_hc, d_opre, d_xc, d_gml, d_skip = _rowwise("proj_b_dx", proj_b_dx, [d_yb, hc, o_pre, xc],
                                                 [weight("w_pb", hc), g_ml, skip], [(ml_w, F32), (ml_w, BF16), (ml_w, F32)],
                                                 [((1, ml_w), F32)] * 2,
                                                 tile=512, deps=dep(sent_mix))
    d_qm, d_km, d_vm, d_li, d_lf = _ml_bwd(q_m, k_m, v_m, li, lf, c_prev, n_prev, m_prev, d_hc)
    d_gl = jnp.concatenate([_gate_cols(d_li), _gate_cols(d_lf), jnp.zeros((t, LANES - 2 * HEADS), F32)], axis=1)
    pre_grads = _ml_pre_bwd(x_m, x_pad, pre_params, [d_xc, d_qm, d_km, d_vm, d_gl], tile=512)
    d_xm = pre_grads[0]
    d_cw = jnp.concatenate(pre_grads[1:5], axis=0)
    d_cb = pre_grads[5]
    d_wq, d_wk, d_wv = _blockdiag_blocks(pre_grads[6:9])
    d_wif = jnp.concatenate(pre_grads[9:12], axis=0)[:, 0:2 * HEADS]
    d_bif = pre_grads[12][:, 0:2 * HEADS]

    def proj_a_dx(dya, o, g, w, n_):
        ct = _raw_dot(dya, w, "nt")
        parts = []
        for hs in _head_slices(HEAD_W):
            _, vjp = jax.vjp(_gla_out, o[:, hs], g[:, hs], n_)
            parts.append(vjp(ct[:, hs]))
        cat = lambda i: jnp.concatenate([p[i] for p in parts], axis=1)
        return (cat(0), cat(1)), (sum(p[2] for p in parts),)

    d_o, d_g_a, d_gn = _rowwise("proj_a_dx", proj_a_dx, [d_ya, o_gla, g_a], [weight("w_pa", o_gla), gn],
                                [(ml_w, F32), (ml_w, BF16)],
                                [((1, HEAD_W), F32)], tile=512)
    dq_hm, dk_hm, d_va, dla_hm = _gla_bwd(q_hm, k_hm, v_a, la_hm, s_prev, d_o)

    def decay_bwd(al, ct, w, b):
        _, vjp = jax.vjp(_log_decay, al, w, b)
        dal, dw, db = vjp(ct)
        return (dal,), (dw, db)

    d_alow_p, d_wa_p, d_ba = _rowwise("gla_decay_bwd", decay_bwd, [a_low_p, _from_hm(dla_hm)], [w_a_up_p, b_a_up],
                                      [(LANES, BF16)], [(w_a_up_p.shape, F32), (b_a_up.shape, F32)])
    d_proj = [jnp.concatenate([_from_hm(dq_hm), _from_hm(dk_hm), d_va, d_g_a], axis=1), d_alow_p,
              jnp.concatenate([d_xm, d_opre, d_ga, d_gb], axis=1)]
    d_widths = [offs[4], LOWRANK, offs[9] - offs[5]]
    d_pieces = _shard_pieces(d_widths, n_in)
    small = dict(w_a_up=d_wa_p[0:LOWRANK], b_a_up=d_ba, g_gla_norm=d_gn, conv_w=d_cw, conv_b=d_cb,
                 w_q_ml=d_wq, w_k_ml=d_wk, w_v_ml=d_wv, w_if=d_wif, b_if=d_bif, ml_skip=d_skip, g_ml_norm=d_gml,
                 g_post_mix=d_gpm, g_pre_mlp=d_gpl, g_post_mlp=d_gpo)
    sent_small = on_small(small, loss)
    sent_in = sent_small
    for half in range(2):
        dw_half = _mm_shard_cols(h, d_proj, d_widths, n_in, "proj_in_dw_%d" % half, half, d // 2, deps=dep(sent_in))
        sent_in = on_grads({"w_in#%d" % half: dw_half})

    def proj_in_dx(dp_a, dp_low, dp_b, xv, dres, w, g):
        dh = 0.0
        for s in range(N_DEV):
            for i, c_in, c_w, wd in d_pieces[s]:
                src = (dp_a, dp_low, dp_b)[i]
                cols = src.shape[1] - c_in if wd < LANES else wd
                dh = dh + _raw_dot(src[:, c_in:c_in + cols], w[s][:, c_w:c_w + cols], "nt")
        _, vjp = jax.vjp(_rms, xv, g)
        dx, dg = vjp(dh)
        return (dx + dres,), (dg,)

    grad_x, d_g1 = _rowwise("proj_in_dx", proj_in_dx, [*d_proj, x, dx_res], [w_in, g1], [(d, F32)], [((1, d), F32)],
                            deps=dep(sent_in))
    return grad_x, on_small(dict(g_pre_mix=d_g1), None)


BIG = ("w_in", "w_pa", "w_pb", "w_o", "w_up", "w_down")
BIG_COL_SHARDED = ("w_in", "w_pa", "w_pb", "w_up")
SMALL_SHARDED = ("w_a_up", "conv_w", "w_if")
SMALL = ("g_pre_mix", "w_a_up", "b_a_up", "g_gla_norm", "conv_w", "conv_b", "w_q_ml", "w_k_ml", "w_v_ml", "w_if", "b_if",
         "ml_skip", "g_ml_norm", "g_post_mix", "g_pre_mlp", "g_post_mlp")
WEIGHTS = ("g_pre_mix", "w_in", "w_a_up", "b_a_up", "g_gla_norm", "conv_w", "conv_b", "w_q_ml", "w_k_ml", "w_v_ml", "w_if", "b_if",
           "ml_skip", "g_ml_norm", "w_pa", "w_pb", "w_o", "g_post_mix", "g_pre_mlp", "w_up", "w_down", "g_post_mlp")


def kernel(x, g_pre_mix, w_in, w_a_up, b_a_up, g_gla_norm, conv_w, conv_b, w_q_ml, w_k_ml, w_v_ml, w_if, b_if, ml_skip, g_ml_norm, w_pa, w_pb, w_o, g_post_mix, g_pre_mlp, w_up, w_down, g_post_mlp, loss_target, m_g_pre_mix, m_w_in, m_w_a_up, m_b_a_up, m_g_gla_norm, m_conv_w, m_conv_b, m_w_q_ml, m_w_k_ml, m_w_v_ml, m_w_if, m_b_if, m_ml_skip, m_g_ml_norm, m_w_pa, m_w_pb, m_w_o, m_g_post_mix, m_g_pre_mlp, m_w_up, m_w_down, m_g_post_mlp, v_g_pre_mix, v_w_in, v_w_a_up, v_b_a_up, v_g_gla_norm, v_conv_w, v_conv_b, v_w_q_ml, v_w_k_ml, v_w_v_ml, v_w_if, v_b_if, v_ml_skip, v_g_ml_norm, v_w_pa, v_w_pb, v_w_o, v_g_post_mix, v_g_pre_mlp, v_w_up, v_w_down, v_g_post_mlp):
    args = dict(locals())
    w = {n: args[n][0] for n in WEIGHTS}
    m = {n: args["m_" + n][0] for n in WEIGHTS}
    v = {n: args["v_" + n][0] for n in WEIGHTS}

    me_lin = _lin(_me())
    me_idx = jnp.reshape(me_lin, (1,)).astype(jnp.int32)

    def full_weight(n, g):
        if n in ("w_in", "w_up"):
            return g
        return _from_col_blocks(g) if n in BIG_COL_SHARDED else g.reshape(-1, g.shape[-1])

    def grad_parts(n, g):
        if n.partition("#")[0] in ("w_in", "w_up"):
            return g
        return (_col_blocks(g) if n in BIG_COL_SHARDED else g.reshape(N_DEV, -1, g.shape[-1])).astype(BF16)

    sharded_names = tuple(SMALL_SHARDED)
    narrow = {n: w[n].astype(BF16) for n in BIG}
    ready, pending, passing = {}, {}, {}
    first_state, _ = _copies_start(["gather"] * len(sharded_names) + ["gather_chips"],
                                   [_small_view(n, w[n]) for n in sharded_names] + [narrow["w_in"]], "allgather_start_first")

    def prefetch(group, after):
        state, token = _copies_start("gather_chips", [narrow[n] for n in group], "allgather_start_" + group[0], after)
        for n in group:
            pending[n] = (group, state)
        return token

    def pass_on(n, after):
        group, state = pending[n]
        shards, lands = _copies_wait(state, after, "allgather_wait_" + group[0])
        state, token = _copies_start("gather_pass", shards, "allgather_pass_" + group[0], lands=lands)
        for gn in group:
            passing[gn] = (group, state)
        return token

    def weight(n, after):
        if n not in ready:
            group, state = passing[n]
            shards, lands = _copies_wait(state, after, "allgather_passed_" + group[0])
            for gn, shard, land in zip(group, shards, lands):
                ready[gn] = full_weight(gn, lax.dynamic_update_slice(land, shard[None], (me_lin, 0, 0)))
        return ready[n]

    h = _first_norm(x[0], w["g_pre_mix"].reshape(1, -1))
    first_own, first_lands = _copies_wait(first_state, [h] + [narrow[n] for n in BIG if n != "w_in"], "allgather_wait_first")
    state, _ = _copies_start("gather_pass", first_own[-1:], "allgather_pass_w_in", lands=first_lands[-1:])
    passing["w_in"] = (("w_in",), state)
    ws = {n: (w[n].reshape(1, -1) if w[n].ndim == 1 else w[n]) for n in SMALL if n not in SMALL_SHARDED}
    for n, own, land in zip(sharded_names, first_own, first_lands):
        ws[n] = _small_unshard(n, lax.dynamic_update_slice(land, own[None], (me_lin, 0, 0)))

    sent = []

    def on_grads(grads):
        names = tuple(grads)
        state, token = _copies_start("exchange", [grad_parts(n, grads[n]) for n in names],
                                     "exchange_start_" + names[0].replace("#", "_"))
        sent.append((names, state))
        return token

    small_sent = []

    def on_small(small, loss):
        names = tuple(small)
        kinds = ["exchange" if n in SMALL_SHARDED else "gather" for n in names]
        srcs = [_small_shards(n, small[n]) if n in SMALL_SHARDED else _small_view(n, small[n]) for n in names]
        extra = [] if loss is None else [loss]
        state, token = _copies_start(kinds + ["gather"] * len(extra), srcs + extra, "allgather_start_small_" + names[0])
        small_sent.append((names, kinds, state))
        return token

    grad_x, last_token = _local_step(x[0], h, loss_target[0], weight, ws, prefetch, pass_on, on_grads, on_small)

    out = {}

    chunks = {}

    def finish(names, state, after):
        parts, lands = _copies_wait(state, after, "exchange_wait_" + names[0].replace("#", "_"))
        for name, part, land in zip(names, parts, lands):
            n, _, chunk = name.partition("#")
            chunks.setdefault(n, []).append((land, part))
            if chunk in ("", "1"):
                got_lands, got_parts = zip(*chunks[n])
                out[n] = _sum_adamw(got_lands, got_parts, me_idx, w[n], m[n], v[n], "adamw_" + n)

    def finish_small(names, kinds, state, after):
        own, lands = _copies_wait(state, after, "allgather_wait_small_" + names[0])
        k = len(names)
        upd = _small_update("adamw_small_" + names[0], me_idx, kinds, lands[:k], own[:k],
                            *[[_small_view(n, d[n]) for n in names] for d in (w, m, v)], sums=list(zip(lands[k:], own[k:])))
        for i, n in enumerate(names):
            out[n] = tuple(_small_unview(n, a, w[n].shape) for a in upd[4 * i:4 * i + 4])
        return upd[4 * k:]

    (loss_sum,) = finish_small(*small_sent[0], [grad_x, last_token])
    for names, state in sent[:-2]:
        finish(names, state, [grad_x, last_token])
    finish(*sent[-2], [loss_sum] + [out[n][1] for n in BIG if n in out])
    finish(*sent[-1], [loss_sum])
    finish_small(*small_sent[1], [out["w_in"][1]])

    shaped = lambda a, n: a.reshape(args[n].shape)
    return (loss_sum[0, 0], grad_x[None],
            *[shaped(out[n][0], n) for n in WEIGHTS], *[shaped(out[n][1], n) for n in WEIGHTS],
            *[shaped(out[n][2], n) for n in WEIGHTS], *[shaped(out[n][3], n) for n in WEIGHTS])
```

```python
import functools

import jax
import jax.numpy as jnp
from jax import lax
from jax.experimental import pallas as pl
from jax.experimental.pallas import tpu as pltpu

F32 = jnp.float32
BF16 = jnp.bfloat16
MESH = pl.DeviceIdType.MESH

N_DEV = 8
EPS = 1e-6
CHUNK = 64
CHUNKS_PER_STEP = 4
HEADS = 4
GLA_DK = 64
HEAD_W = 128
GLA_GATE_NORM = 16.0
LOWRANK = 16
CONV_K = 4
QKV_BLOCK = 4
LANES = 128
HALO = 8
IN_SPLITS = (256, 256, 512, 512, 16, 512, 512, 1024, 1024)

ADAM_LR = 0.001
ADAM_B1 = 0.9
ADAM_B2 = 0.999
ADAM_EPS = 1e-08
ADAM_WD = 0.01
ADAM_STEP = 10

VMEM_LIMIT = 56 * 1024 * 1024


def _cparams(*sem):
    return pltpu.CompilerParams(dimension_semantics=sem, vmem_limit_bytes=VMEM_LIMIT)


def _dims(mode, ndim):
    contract = {"nn": ((ndim - 1,), (ndim - 2,)), "nt": ((ndim - 1,), (ndim - 1,)), "tn": ((ndim - 2,), (ndim - 2,))}[mode]
    return contract, (((0,), (0,)) if ndim == 3 else ((), ()))


def _raw_dot(a, b, mode):
    return lax.dot_general(a.astype(BF16), b.astype(BF16), _dims(mode, a.ndim), preferred_element_type=F32)


@functools.partial(jax.custom_vjp, nondiff_argnums=(2,))
def _bdot(a, b, mode):
    return _raw_dot(a, b, mode)


def _bdot_fwd(a, b, mode):
    return _raw_dot(a, b, mode), (a, b)


def _bdot_bwd(mode, res, ct):
    a, b = res
    if mode == "nn":
        da, db = _raw_dot(ct, b, "nt"), _raw_dot(a, ct, "tn")
    elif mode == "nt":
        da, db = _raw_dot(ct, b, "nn"), _raw_dot(ct, a, "tn")
    else:
        da, db = _raw_dot(b, ct, "nt"), _raw_dot(a, ct, "nn")
    return da.astype(a.dtype), db.astype(b.dtype)


_bdot.defvjp(_bdot_fwd, _bdot_bwd)


def _split3(x):
    hi = x.astype(BF16)
    r1 = x - hi.astype(F32)
    mid = r1.astype(BF16)
    return hi, mid, (r1 - mid.astype(F32)).astype(BF16)


def _split_dot(tri, x):
    if x.ndim == 3:
        tri = jnp.broadcast_to(tri, (x.shape[0], *tri.shape))
    return sum(lax.dot_general(tri, t, _dims("nn", x.ndim), preferred_element_type=F32) for t in _split3(x))


def _tri(n, lower):
    r = lax.broadcasted_iota(jnp.int32, (n, n), 0)
    c = lax.broadcasted_iota(jnp.int32, (n, n), 1)
    return ((c <= r) if lower else (c >= r)).astype(BF16)


@jax.custom_vjp
def _cumsum_rows(x):
    return _split_dot(_tri(x.shape[-2], True), x)


def _cumsum_rows_fwd(x):
    return _cumsum_rows(x), None


def _cumsum_rows_bwd(_, ct):
    return (_split_dot(_tri(ct.shape[-2], False), ct),)


_cumsum_rows.defvjp(_cumsum_rows_fwd, _cumsum_rows_bwd)


def _abs(x):
    return jnp.where(x >= 0, x, -x)


def _sigmoid(x):
    return lax.logistic(x)


def _log_sigmoid(x):
    return jnp.minimum(x, 0.0) - jnp.log(1.0 + jnp.exp(-_abs(x)))


def _rms(x, g):
    return x * lax.rsqrt(jnp.mean(x * x, axis=-1, keepdims=True) + EPS) * g


def _head_slices(w):
    return [slice(h * w, (h + 1) * w) for h in range(HEADS)]


def _heads(ref, rows=slice(None)):
    return jnp.stack([ref[rows, hs] for hs in _head_slices(HEAD_W)])


def _put_heads(ref, val, rows=slice(None)):
    for h, hs in enumerate(_head_slices(HEAD_W)):
        ref[rows, hs] = val[h].astype(ref.dtype)


def _tile(dim, want):
    if dim <= want or dim % LANES:
        return dim
    t = want
    while dim % t:
        t -= LANES
    return t


def _mm(a, b, mode, out_dtype, name, tm=1024, tn=1024, tk=4096, epilogue=None, extra=(), deps=(), shards=None):
    if shards == "b":
        assert mode == "nn"
        ns = b.shape[2]
        (m, k), (k2, n) = a.shape, (b.shape[1], b.shape[0] * ns)
        tn = ns
    elif mode == "nn":
        (m, k), (k2, n) = a.shape, b.shape
    elif mode == "nt":
        (m, k), (n, k2) = a.shape, b.shape
    else:
        (k, m), (k2, n) = a.shape, b.shape
    assert k == k2, (name, a.shape, b.shape)
    tm, tn, tk = _tile(m, tm), _tile(n, tn), _tile(k, tk)
    nk = k // tk
    out_dtypes = out_dtype if epilogue else (out_dtype,)
    assert nk == 1 or (out_dtype == F32 and not epilogue), name
    n_in = 2 + len(extra)

    def body(*refs):
        p = _raw_dot(refs[0][...], refs[1][...], mode)
        if nk > 1:
            _accumulate(pl.program_id(2), [refs[n_in + len(deps)]], [p])
            return
        outs = epilogue(p, *[r[...] for r in refs[2:n_in]]) if epilogue else (p,)
        for ref, val in zip(refs[n_in + len(deps):], outs):
            ref[...] = val.astype(ref.dtype)

    a_spec = pl.BlockSpec((tk, tm), lambda i, j, kk: (kk, i)) if mode == "tn" else pl.BlockSpec((tm, tk), lambda i, j, kk: (i, kk))
    if shards == "b":
        b_spec = pl.BlockSpec((None, tk, tn), lambda i, j, kk: (j, kk, 0))
    elif mode == "nt":
        b_spec = pl.BlockSpec((tn, tk), lambda i, j, kk: (j, kk))
    else:
        b_spec = pl.BlockSpec((tk, tn), lambda i, j, kk: (kk, j))
    o_spec = pl.BlockSpec((tm, tn), lambda i, j, kk: (i, j))
    res = pl.pallas_call(
        body, name=name, grid=(m // tm, n // tn, nk),
        in_specs=[a_spec, b_spec] + [o_spec] * len(extra) + [ANY] * len(deps), out_specs=[o_spec] * len(out_dtypes),
        out_shape=[jax.ShapeDtypeStruct((m, n), dt) for dt in out_dtypes],
        compiler_params=_cparams("parallel", "parallel", "arbitrary"),
    )(a, b, *extra, *deps)
    return res if epilogue else res[0]


def _shard_pieces(widths, n):
    bounds = [0]
    for wd in widths:
        bounds.append(bounds[-1] + wd)
    assert bounds[-1] == N_DEV * n
    return [[(i, max(s * n, b) - b, max(s * n, b) - s * n, min((s + 1) * n, b + wd) - max(s * n, b))
             for i, (b, wd) in enumerate(zip(bounds, widths)) if b < (s + 1) * n and b + wd > s * n]
            for s in range(N_DEV)]


def _mm_shard_cols(a, bs, widths, n, name, row_tile, tm, deps=()):
    t = a.shape[0]
    nb = len(bs)
    pieces = _shard_pieces(widths, n)

    def body(a_ref, *rest):
        b_refs = rest[:nb]
        o_ref, at_ref = rest[nb + len(deps):]
        j = pl.program_id(0)

        @pl.when(j == 0)
        def _():
            at_ref[...] = a_ref[...].astype(BF16).T

        for s in range(N_DEV):
            @pl.when(j == s)
            def _(s=s):
                for i, c_in, c_out, wd in pieces[s]:
                    cols = min(_round_up(wd, LANES), bs[i].shape[1] - c_in) if wd < LANES else wd
                    p = _raw_dot(at_ref[...], b_refs[i][:, c_in:c_in + cols], "nn")
                    o_ref[:, c_out:c_out + wd] = p[:, 0:wd].astype(BF16)

    return pl.pallas_call(
        body, name=name, grid=(N_DEV,),
        in_specs=[pl.BlockSpec((t, tm), lambda j: (0, row_tile))]
        + [pl.BlockSpec(b.shape, lambda j: (0, 0), pipeline_mode=pl.Buffered(1)) for b in bs] + [ANY] * len(deps),
        out_specs=pl.BlockSpec((None, tm, n), lambda j: (j, 0, 0)),
        out_shape=jax.ShapeDtypeStruct((N_DEV, tm, n), BF16),
        scratch_shapes=[pltpu.VMEM((tm, t), BF16)],
        compiler_params=_cparams("arbitrary"),
    )(a, *bs, *deps)


def _round_up(v, m):
    return -(-v // m) * m


def _rowwise(name, fn, rows, params, out_rows, out_accs=(), tile=256, deps=()):
    t = rows[0].shape[0]
    r = min(tile, t)
    assert t % r == 0
    n_in, n_or = len(rows) + len(params), len(out_rows)
    n_all = n_in + len(deps)
    params = list(params) + list(deps)

    def body(*refs):
        vals = [ref[...] for ref in refs[:n_in]]
        outs = refs[n_all:]
        ro, ao = fn(*vals)
        for ref, v in zip(outs[:n_or], ro):
            ref[...] = v.astype(ref.dtype)
        if out_accs:
            _accumulate(pl.program_id(0), outs[n_or:], ao)

    def full(shape):
        return pl.BlockSpec(shape, lambda i, nd=len(shape): (0,) * nd)

    return pl.pallas_call(
        body, name=name, grid=(t // r,),
        in_specs=[pl.BlockSpec((r, a.shape[1]), lambda i: (i, 0)) for a in rows] + [full(p.shape) for p in params],
        out_specs=[pl.BlockSpec((r, w), lambda i: (i, 0)) for w, _ in out_rows] + [full(s) for s, _ in out_accs],
        out_shape=[jax.ShapeDtypeStruct((t, w), dt) for w, dt in out_rows] + [jax.ShapeDtypeStruct(s, dt) for s, dt in out_accs],
        compiler_params=_cparams("arbitrary"),
    )(*rows, *params)


def _accumulate(step, refs, vals):
    for ref, v in zip(refs, vals):
        @pl.when(step == 0)
        def _(ref=ref, v=v):
            ref[...] = v.astype(ref.dtype)

        @pl.when(step > 0)
        def _(ref=ref, v=v):
            ref[...] += v.astype(ref.dtype)


def _gla_chunk(q, k, v, la, st):
    c = q.shape[-2]
    row = lax.broadcasted_iota(jnp.int32, (c, c), 0)
    col = lax.broadcasted_iota(jnp.int32, (c, c), 1)
    cum = _cumsum_rows(la)
    cl = jnp.sum(la, axis=-2, keepdims=True)
    ep = jnp.exp(cum)
    en = jnp.exp(-cum)
    qs = q * (GLA_DK ** -0.5)
    qp = qs * ep
    a_f = _bdot(qp, k * en, "nt")
    a_b = _bdot(qs * en, k * ep, "nt")
    sc = jnp.where(row >= col, a_f, a_b)
    o = _bdot(sc, v, "nn") + _bdot(qp, st, "nt")
    kd = k * jnp.exp(cl - cum)
    st_new = st * jnp.exp(cl) + _bdot(v, kd, "tn")
    return o, st_new


def _gla_specs(nc, rev):
    nb = nc // CHUNKS_PER_STEP
    rows = CHUNKS_PER_STEP * CHUNK

    def blk(n):
        return (nb - 1 - n) if rev else n
    hm = pl.BlockSpec((HEADS, rows, GLA_DK), lambda n: (0, blk(n), 0))
    tm = pl.BlockSpec((rows, HEADS * HEAD_W), lambda n: (blk(n), 0))
    st = pl.BlockSpec((HEADS, CHUNKS_PER_STEP, HEAD_W, GLA_DK), lambda n: (0, blk(n), 0, 0))
    return nb, hm, tm, st


def _chunk_rows(c):
    return slice(c * CHUNK, (c + 1) * CHUNK)


def _gla_fwd(q, k, v, la, deps=()):
    t = v.shape[0]
    nc = t // CHUNK
    nb, hm, tm, st = _gla_specs(nc, False)

    def body(q_ref, k_ref, v_ref, la_ref, *rest):
        o_ref, sp_ref, st_ref = rest[len(deps):]

        @pl.when(pl.program_id(0) == 0)
        def _():
            st_ref[...] = jnp.zeros_like(st_ref)

        s = st_ref[...]
        for c in range(CHUNKS_PER_STEP):
            r = _chunk_rows(c)
            sp_ref[:, c] = s
            o, s = _gla_chunk(q_ref[:, r], k_ref[:, r], _heads(v_ref, r), la_ref[:, r], s)
            _put_heads(o_ref, o, r)
        st_ref[...] = s

    return pl.pallas_call(
        body, name="gla_fwd", grid=(nb,),
        in_specs=[hm, hm, tm, hm] + [ANY] * len(deps), out_specs=[tm, st],
        out_shape=[jax.ShapeDtypeStruct((t, HEADS * HEAD_W), F32), jax.ShapeDtypeStruct((HEADS, nc, HEAD_W, GLA_DK), F32)],
        scratch_shapes=[pltpu.VMEM((HEADS, HEAD_W, GLA_DK), F32)],
        compiler_params=_cparams("arbitrary"),
    )(q, k, v, la, *deps)


def _gla_bwd(q, k, v, la, sp, do):
    t = v.shape[0]
    nc = t // CHUNK
    nb, hm, tm, st = _gla_specs(nc, True)

    def body(q_ref, k_ref, v_ref, la_ref, sp_ref, do_ref, dq_ref, dk_ref, dv_ref, dla_ref, ds_ref):
        @pl.when(pl.program_id(0) == 0)
        def _():
            ds_ref[...] = jnp.zeros_like(ds_ref)

        ds = ds_ref[...]
        for c in reversed(range(CHUNKS_PER_STEP)):
            r = _chunk_rows(c)
            _, vjp = jax.vjp(_gla_chunk, q_ref[:, r], k_ref[:, r], _heads(v_ref, r), la_ref[:, r], sp_ref[:, c])
            dq, dk, dv, dla, ds = vjp((_heads(do_ref, r), ds))
            dq_ref[:, r] = dq.astype(dq_ref.dtype)
            dk_ref[:, r] = dk.astype(dk_ref.dtype)
            _put_heads(dv_ref, dv, r)
            dla_ref[:, r] = dla
        ds_ref[...] = ds

    hm_shape = jax.ShapeDtypeStruct((HEADS, t, GLA_DK), BF16)
    return pl.pallas_call(
        body, name="gla_bwd", grid=(nb,),
        in_specs=[hm, hm, tm, hm, st, tm], out_specs=[hm, hm, tm, hm],
        out_shape=[hm_shape, hm_shape, jax.ShapeDtypeStruct((t, HEADS * HEAD_W), BF16),
                   jax.ShapeDtypeStruct((HEADS, t, GLA_DK), F32)],
        scratch_shapes=[pltpu.VMEM((HEADS, HEAD_W, GLA_DK), F32)],
        compiler_params=_cparams("arbitrary"),
    )(q, k, v, la, sp, do)


def _ml_chunk(q, k, v, li_r, lf_r, cm, nv, m):
    c = q.shape[-2]
    row = lax.broadcasted_iota(jnp.int32, (c, c), 0)
    col = lax.broadcasted_iota(jnp.int32, (c, c), 1)
    eye = (row == col).astype(F32)
    li_c = jnp.sum(eye * li_r, axis=-1, keepdims=True)
    lf_c = jnp.sum(eye * lf_r, axis=-1, keepdims=True)
    fc_c = jnp.sum((col <= row).astype(F32) * lf_r, axis=-1, keepdims=True)
    fc_r = jnp.sum((row <= col).astype(F32) * lf_c, axis=-2, keepdims=True)
    f_last = jnp.sum(lf_r, axis=-1, keepdims=True)
    kc = k * (HEAD_W ** -0.5)
    a_c = f_last - fc_c + li_c
    m_loc = jnp.max(a_c, axis=-2, keepdims=True)
    kw = kc * jnp.exp(a_c - m_loc)
    c_chunk = _bdot(kw, v, "tn")
    n_chunk = jnp.sum(kw, axis=-2, keepdims=True)
    m_new = jnp.maximum(f_last + m, m_loc)
    sp = jnp.exp(f_last + m - m_new)
    sl = jnp.exp(m_loc - m_new)
    cm_new = sp * cm + sl * c_chunk
    nv_new = sp * nv + sl * n_chunk
    log_d = li_r - _abs(fc_c - fc_r)
    g_inter = fc_c + m
    m_t = jnp.maximum(g_inter, jnp.max(log_d, axis=-1, keepdims=True))
    s = _bdot(q, kc, "nt") * jnp.exp(log_d - m_t)
    sc = jnp.exp(g_inter - m_t)
    num = _bdot(s, v, "nn") + sc * _bdot(q, cm, "nn")
    den = jnp.sum(s, axis=-1, keepdims=True) + sc * jnp.sum(q * nv, axis=-1, keepdims=True)
    den = jnp.maximum(_abs(den), jnp.exp(-m_t))
    return num / den, cm_new, nv_new, m_new


def _ml_specs(nc, rev):
    nb = nc // CHUNKS_PER_STEP

    def blk(n):
        return (nb - 1 - n) if rev else n
    tm = pl.BlockSpec((CHUNKS_PER_STEP * CHUNK, HEADS * HEAD_W), lambda n: (blk(n), 0))
    gate = pl.BlockSpec((HEADS, CHUNKS_PER_STEP, 1, CHUNK), lambda n: (0, blk(n), 0, 0))
    cm = pl.BlockSpec((HEADS, CHUNKS_PER_STEP, HEAD_W, HEAD_W), lambda n: (0, blk(n), 0, 0))
    vec = pl.BlockSpec((HEADS, CHUNKS_PER_STEP, 1, HEAD_W), lambda n: (0, blk(n), 0, 0))
    return nb, tm, gate, cm, vec


_ML_STATE = [pltpu.VMEM((HEADS, HEAD_W, HEAD_W), F32), pltpu.VMEM((HEADS, 1, HEAD_W), F32), pltpu.VMEM((HEADS, 1, HEAD_W), F32)]


def _ml_fwd(q, k, v, li, lf):
    t = q.shape[0]
    nc = t // CHUNK
    nb, tm, gate, cm, vec = _ml_specs(nc, False)

    def body(q_ref, k_ref, v_ref, li_ref, lf_ref, hc_ref, cp_ref, np_ref, mp_ref, c_ref, n_ref, m_ref):
        @pl.when(pl.program_id(0) == 0)
        def _():
            c_ref[...] = jnp.zeros_like(c_ref)
            n_ref[...] = jnp.zeros_like(n_ref)
            m_ref[...] = jnp.zeros_like(m_ref)

        cs, ns, ms = c_ref[...], n_ref[...], m_ref[...][:, :, 0:1]
        for c in range(CHUNKS_PER_STEP):
            r = _chunk_rows(c)
            cp_ref[:, c] = cs
            np_ref[:, c] = ns
            mp_ref[:, c] = jnp.broadcast_to(ms, m_ref.shape)
            hc, cs, ns, ms = _ml_chunk(_heads(q_ref, r), _heads(k_ref, r), _heads(v_ref, r), li_ref[:, c], lf_ref[:, c],
                                       cs, ns, ms)
            _put_heads(hc_ref, hc, r)
        c_ref[...] = cs
        n_ref[...] = ns
        m_ref[...] = jnp.broadcast_to(ms, m_ref.shape)

    return pl.pallas_call(
        body, name="mlstm_fwd", grid=(nb,),
        in_specs=[tm, tm, tm, gate, gate], out_specs=[tm, cm, vec, vec],
        out_shape=[jax.ShapeDtypeStruct((t, HEADS * HEAD_W), F32), jax.ShapeDtypeStruct((HEADS, nc, HEAD_W, HEAD_W), F32),
                   jax.ShapeDtypeStruct((HEADS, nc, 1, HEAD_W), F32), jax.ShapeDtypeStruct((HEADS, nc, 1, HEAD_W), F32)],
        scratch_shapes=_ML_STATE,
        compiler_params=_cparams("arbitrary"),
    )(q, k, v, li, lf)


def _ml_bwd(q, k, v, li, lf, cp, npv, mp, dhc):
    t = q.shape[0]
    nc = t // CHUNK
    nb, tm, gate, cm, vec = _ml_specs(nc, True)

    def body(q_ref, k_ref, v_ref, li_ref, lf_ref, cp_ref, np_ref, mp_ref, dhc_ref,
             dq_ref, dk_ref, dv_ref, dli_ref, dlf_ref, dc_ref, dn_ref, dm_ref):
        @pl.when(pl.program_id(0) == 0)
        def _():
            dc_ref[...] = jnp.zeros_like(dc_ref)
            dn_ref[...] = jnp.zeros_like(dn_ref)
            dm_ref[...] = jnp.zeros_like(dm_ref)

        dc, dn, dm = dc_ref[...], dn_ref[...], dm_ref[...][:, :, 0:1]
        for c in reversed(range(CHUNKS_PER_STEP)):
            r = _chunk_rows(c)
            _, vjp = jax.vjp(_ml_chunk, _heads(q_ref, r), _heads(k_ref, r), _heads(v_ref, r), li_ref[:, c], lf_ref[:, c],
                             cp_ref[:, c], np_ref[:, c], mp_ref[:, c][:, :, 0:1])
            dq, dk, dv, dli, dlf, dc, dn, dm = vjp((_heads(dhc_ref, r), dc, dn, dm))
            _put_heads(dq_ref, dq, r)
            _put_heads(dk_ref, dk, r)
            _put_heads(dv_ref, dv, r)
            dli_ref[:, c] = dli
            dlf_ref[:, c] = dlf
        dc_ref[...] = dc
        dn_ref[...] = dn
        dm_ref[...] = jnp.broadcast_to(dm, dm_ref.shape)

    tm_shape = jax.ShapeDtypeStruct((t, HEADS * HEAD_W), F32)
    gate_shape = jax.ShapeDtypeStruct((HEADS, nc, 1, CHUNK), F32)
    return pl.pallas_call(
        body, name="mlstm_bwd", grid=(nb,),
        in_specs=[tm, tm, tm, gate, gate, cm, vec, vec, tm], out_specs=[tm, tm, tm, gate, gate],
        out_shape=[tm_shape, tm_shape, tm_shape, gate_shape, gate_shape],
        scratch_shapes=_ML_STATE,
        compiler_params=_cparams("arbitrary"),
    )(q, k, v, li, lf, cp, npv, mp, dhc)


def _ml_pre(s0, s1, s2, s3, cw0, cw1, cw2, cw3, cb, wq, wk, wv, wiq, wik, wiv, bif):
    pre = cb + cw0 * s0 + cw1 * s1 + cw2 * s2 + cw3 * s3
    xc = pre * _sigmoid(pre)
    q = _bdot(xc, wq, "nn")
    k = _bdot(xc, wk, "nn")
    v = _bdot(s3, wv, "nn")
    gates = _bdot(q, wiq, "nn") + _bdot(k, wik, "nn") + _bdot(v, wiv, "nn") + bif
    lane = lax.broadcasted_iota(jnp.int32, gates.shape, 1)
    gl = jnp.where(lane < HEADS, gates, _log_sigmoid(gates))
    return xc, q, k, v, gl


def _delayed(xs_ref, x_ref, halo_ref, r):
    xs_ref[0:HALO, :] = halo_ref[...]
    xs_ref[HALO:HALO + r, :] = x_ref[...]
    return [xs_ref[pl.ds(HALO - (CONV_K - 1) + j, r), :] for j in range(CONV_K)]


def _full_spec(shape):
    return pl.BlockSpec(shape, lambda i, nd=len(shape): (0,) * nd)


def _ml_pre_fwd(x_m, x_pad, params, tile=256, deps=()):
    t, w = x_m.shape
    r = min(tile, t)

    def body(*refs):
        x_ref, halo_ref = refs[:2]
        p = [ref[...] for ref in refs[2:2 + len(params)]]
        outs = refs[2 + len(params) + len(deps):-1]
        res = _ml_pre(*_delayed(refs[-1], x_ref, halo_ref, r), *p)
        for ref, val in zip(outs, res):
            ref[...] = val

    row = pl.BlockSpec((r, w), lambda i: (i, 0))
    return pl.pallas_call(
        body, name="ml_pre_fwd", grid=(t // r,),
        in_specs=[row, pl.BlockSpec((HALO, w), lambda i: (i * (r // HALO), 0))] + [_full_spec(p.shape) for p in params]
        + [ANY] * len(deps),
        out_specs=[row] * 4 + [pl.BlockSpec((r, LANES), lambda i: (i, 0))],
        out_shape=[jax.ShapeDtypeStruct((t, w), F32)] * 4 + [jax.ShapeDtypeStruct((t, LANES), F32)],
        scratch_shapes=[pltpu.VMEM((r + HALO, w), F32)],
        compiler_params=_cparams("arbitrary"),
    )(x_m, x_pad, *params, *deps)


def _ml_pre_bwd(x_m, x_pad, params, cts, tile=256):
    t, w = x_m.shape
    r = min(tile, t)
    nt = t // r
    n_p = len(params)

    def body(*refs):
        x_ref, halo_ref = refs[:2]
        p = [ref[...] for ref in refs[2:2 + n_p]]
        ct = [ref[...] for ref in refs[2 + n_p:7 + n_p]]
        dx_ref = refs[7 + n_p]
        dp_refs = refs[8 + n_p:8 + 2 * n_p]
        xs_ref, ds_ref, carry_ref = refs[8 + 2 * n_p:]
        step = pl.program_id(0)

        @pl.when(step == 0)
        def _():
            ds_ref[...] = jnp.zeros_like(ds_ref)
            carry_ref[...] = jnp.zeros_like(carry_ref)

        _, vjp = jax.vjp(_ml_pre, *_delayed(xs_ref, x_ref, halo_ref, r), *p)
        grads = vjp(tuple(ct))
        for j in range(CONV_K):
            ds_ref[j, HALO:HALO + r, :] = grads[j]
        lead = HALO + CONV_K - 1
        d_tile = sum(ds_ref[j, pl.ds(lead - j, r), :] for j in range(CONV_K))
        d_halo = sum(ds_ref[j, pl.ds(CONV_K - 1 - j, HALO), :] for j in range(CONV_K))
        dx_ref[...] = jnp.concatenate([d_tile[:r - HALO], d_tile[r - HALO:] + carry_ref[...]], axis=0).astype(dx_ref.dtype)
        carry_ref[...] = d_halo
        _accumulate(step, dp_refs, grads[CONV_K:])

    row = pl.BlockSpec((r, w), lambda i: (nt - 1 - i, 0))
    return pl.pallas_call(
        body, name="ml_pre_bwd", grid=(nt,),
        in_specs=[row, pl.BlockSpec((HALO, w), lambda i: ((nt - 1 - i) * (r // HALO), 0))] + [_full_spec(p.shape) for p in params]
        + [row] * 4 + [pl.BlockSpec((r, LANES), lambda i: (nt - 1 - i, 0))],
        out_specs=[row] + [_full_spec(p.shape) for p in params],
        out_shape=[jax.ShapeDtypeStruct((t, w), BF16)] + [jax.ShapeDtypeStruct(p.shape, F32) for p in params],
        scratch_shapes=[pltpu.VMEM((r + HALO, w), F32), pltpu.VMEM((CONV_K, r + 2 * HALO, w), F32), pltpu.VMEM((HALO, w), F32)],
        compiler_params=_cparams("arbitrary"),
    )(x_m, x_pad, *params, *cts)


def _per_head(fn, row_vals, head_params, shared_params=()):
    return [fn(*[a[:, hs] for a in row_vals], *[p[:, hs] for p in head_params], *shared_params) for hs in _head_slices(HEAD_W)]


def _gla_out(o, g, gn):
    return _rms(o, gn) * (g * _sigmoid(g))


def _ml_out(hc, op, xc, g, sk):
    hcell = hc * _sigmoid(op)
    mu = jnp.mean(hcell, axis=-1, keepdims=True)
    d = hcell - mu
    var = jnp.mean(d * d, axis=-1, keepdims=True)
    return d * lax.rsqrt(var + EPS) * g + sk * xc


def _log_decay(al, w, b):
    return _log_sigmoid(_bdot(al, w, "nn") + b) * (1.0 / GLA_GATE_NORM)


def _merge(ga, gb, ya, yb):
    ga, gb, ya, yb = (a.astype(F32) for a in (ga, gb, ya, yb))
    return _sigmoid(ga) * ya + _sigmoid(gb) * yb


def _post_mix(x, z, gpm, gpl):
    x1 = x + _rms(z, gpm)
    return x1, _rms(x1, gpl)


def _loss_rows(x1, dn, tgt, g):
    e = x1 + _rms(dn, g) - tgt
    return 0.5 * jnp.sum(jnp.mean(e * e, axis=-1, keepdims=True), axis=0, keepdims=True)


def _lin(p):
    return 4 * p[0] + 2 * p[1] + p[2]


def _me():
    return lax.axis_index("x"), lax.axis_index("y"), lax.axis_index("c")


def _flip(p, k):
    return tuple((1 - v) if (k >> (2 - i)) & 1 else v for i, v in enumerate(p))


ANY = pl.BlockSpec(memory_space=pl.ANY)


HBM = pl.BlockSpec(memory_space=pltpu.HBM)
SEM = pl.BlockSpec(memory_space=pltpu.SEMAPHORE)
DATAFLOW = pltpu.SideEffectType.DATAFLOW_SIDE_EFFECTING


SIBLING = 1
OTHER_CHIPS = (2, 4, 6)


def _peer_copies(kinds, srcs, lands, send_sems, recv_sems):
    me = _me()
    copies = []
    for a, (kind, src, land) in enumerate(zip(kinds, srcs, lands)):
        masks = {"gather": range(1, N_DEV), "exchange": range(1, N_DEV), "gather_chips": (SIBLING, *OTHER_CHIPS),
                 "gather_pass": OTHER_CHIPS}[kind]
        for k in masks:
            peer = _flip(me, k)
            if kind == "gather_pass":
                block = land.at[_lin(peer)]
                src_ref, dst_ref, target = block, block, _flip(me, SIBLING)
            else:
                src_ref, dst_ref, target = (src.at[_lin(peer)] if kind == "exchange" else src), land.at[_lin(me)], peer
            copies.append(pltpu.make_async_remote_copy(
                src_ref=src_ref, dst_ref=dst_ref, send_sem=send_sems.at[a * 7 + k - 1], recv_sem=recv_sems.at[a * 7 + k - 1],
                device_id=target, device_id_type=MESH))
    return copies


def _copies_start(kind, srcs, name, after=None, lands=None):
    n = len(srcs)
    extra = [] if after is None else [after]
    kind = [kind] * n if isinstance(kind, str) else list(kind)
    land_shapes = [(s.shape if k == "exchange" else (N_DEV, *s.shape)) for k, s in zip(kind, srcs)]
    lands = [lax.empty(ls, s.dtype) for ls, s in zip(land_shapes, srcs)] if lands is None else lands

    def body(*refs):
        sems = refs[2 * n + len(extra):]
        for cp in _peer_copies(kind, refs[:n], refs[n:2 * n], sems[0], sems[1]):
            cp.start()
        refs[-1][...] = jnp.zeros_like(refs[-1])

    def hbm(a):
        return pltpu.with_memory_space_constraint(a, pltpu.HBM)

    out = pl.pallas_call(
        body, name=name,
        out_shape=(pltpu.SemaphoreType.DMA((7 * n,)), pltpu.SemaphoreType.DMA((7 * n,)),
                   *[pltpu.HBM(s.shape, s.dtype) for s in srcs],
                   *[pltpu.HBM(ls, s.dtype) for ls, s in zip(land_shapes, srcs)],
                   jax.ShapeDtypeStruct((8, LANES), F32)),
        in_specs=[HBM] * (2 * n) + [ANY] * len(extra),
        out_specs=(SEM, SEM, *[HBM] * (2 * n), pl.BlockSpec(memory_space=pltpu.VMEM)),
        input_output_aliases={i: 2 + i for i in range(2 * n)},
        compiler_params=pltpu.CompilerParams(has_side_effects=DATAFLOW),
    )(*[hbm(s) for s in srcs], *[hbm(a) for a in lands], *extra)
    return (kind, n, out[:-1]), out[-1]


def _copies_wait(state, after, name):
    kind, n, (send_sems, recv_sems, *thru) = state
    after = list(after) if isinstance(after, (list, tuple)) else [after]

    def body(*refs):
        for cp in _peer_copies(kind, refs[:n], refs[n:2 * n], refs[2 * n], refs[2 * n + 1]):
            cp.wait_send()
            cp.wait_recv()

    out = pl.pallas_call(
        body, name=name,
        out_shape=tuple(pltpu.HBM(t.shape, t.dtype) for t in thru),
        in_specs=[HBM] * (2 * n) + [SEM, SEM] + [ANY] * len(after), out_specs=tuple([HBM] * (2 * n)),
        input_output_aliases={i: i for i in range(2 * n)},
        compiler_params=pltpu.CompilerParams(has_side_effects=DATAFLOW),
    )(*thru, send_sems, recv_sems, *after)
    return out[:n], out[n:]


def _adamw(w, g, m, v):
    m2 = ADAM_B1 * m + (1.0 - ADAM_B1) * g
    v2 = ADAM_B2 * v + (1.0 - ADAM_B2) * (g * g)
    m_hat = m2 / (1.0 - ADAM_B1 ** ADAM_STEP)
    v_hat = v2 / (1.0 - ADAM_B2 ** ADAM_STEP)
    delta = -ADAM_LR * (m_hat / (jnp.sqrt(v_hat) + ADAM_EPS) + ADAM_WD * w)
    return delta, m2, v2


def _sum_adamw(lands, parts, me_idx, w, m, v, name, tile=256):
    r, c = w.shape
    nchunks = len(lands)
    tr = min(tile, r // nchunks)
    per_chunk = r // nchunks // tr
    per = 1 + N_DEV

    def body(me_ref, *refs):
        w_ref, m_ref, v_ref, g_ref, d_ref, m2_ref, v2_ref = refs[nchunks * per:]
        for k in range(nchunks):
            own_ref, slots = refs[k * per], refs[k * per + 1:(k + 1) * per]

            @pl.when(pl.program_id(0) // per_chunk == k)
            def _(own_ref=own_ref, slots=slots):
                own = own_ref[...].astype(F32)
                g = None
                for s in range(N_DEV):
                    term = jnp.where(me_ref[0] == s, own, slots[s][...].astype(F32))
                    g = term if g is None else g + term
                d, m2, v2 = _adamw(w_ref[...], g, m_ref[...], v_ref[...])
                g_ref[...] = g
                d_ref[...] = d
                m2_ref[...] = m2
                v2_ref[...] = v2

    def chunk_specs(k):
        def tile_of(i):
            return jnp.clip(i - k * per_chunk, 0, per_chunk - 1)

        def slot_spec(s):
            return pl.BlockSpec((None, tr, c), lambda i, me: (jnp.where(me[0] == s, (s + 1) % N_DEV, s), tile_of(i), 0))
        return [pl.BlockSpec((None, tr, c), lambda i, me: (me[0], tile_of(i), 0))] + [slot_spec(s) for s in range(N_DEV)]

    row = pl.BlockSpec((tr, c), lambda i, me: (i, 0))
    operands = [a for land, part in zip(lands, parts) for a in (part, *[land] * N_DEV)]
    return pl.pallas_call(
        body, name=name,
        grid_spec=pltpu.PrefetchScalarGridSpec(
            num_scalar_prefetch=1, grid=(r // tr,),
            in_specs=[s for k in range(nchunks) for s in chunk_specs(k)] + [row] * 3,
            out_specs=[row] * 4),
        out_shape=[jax.ShapeDtypeStruct((r, c), F32)] * 4,
        compiler_params=_cparams("parallel"),
    )(me_idx, *operands, w, m, v)


def _small_update(name, me_idx, kinds, lands, owns, ws, ms, vs, sums=()):
    n = len(ws)
    lands, owns = list(lands) + [s[0] for s in sums], list(owns) + [s[1] for s in sums]
    kinds = list(kinds) + ["gather"] * len(sums)
    nl = len(lands)

    def summed(me, land_ref, own):
        g = None
        for s in range(N_DEV):
            term = jnp.where(me == s, own, land_ref[s])
            g = term if g is None else g + term
        return g

    def body(me_ref, *refs):
        land_refs, own_refs = refs[:nl], refs[nl:2 * nl]
        w_refs, m_refs, v_refs = (refs[2 * nl + i * n:2 * nl + (i + 1) * n] for i in range(3))
        outs = refs[2 * nl + 3 * n:]
        me = me_ref[0]
        for i in range(n):
            g = summed(me, land_refs[i], own_refs[i][...])
            d, m2, v2 = _adamw(w_refs[i][...], g, m_refs[i][...], v_refs[i][...])
            for ref, val in zip(outs[4 * i:4 * i + 4], (g, d, m2, v2)):
                ref[...] = val
        for i in range(n, nl):
            outs[4 * n + i - n][...] = summed(me, land_refs[i], own_refs[i][...])

    def whole(shape):
        return pl.BlockSpec(shape, lambda i, me, nd=len(shape): (0,) * nd)

    def own_spec(kind, own):
        if kind == "gather":
            return whole(own.shape)
        return pl.BlockSpec((None, *own.shape[1:]), lambda i, me: (me[0], 0, 0))

    shapes = [w.shape for w in ws]
    out_shapes = [s for s in shapes for _ in range(4)] + [s[1].shape for s in sums]
    return pl.pallas_call(
        body, name=name,
        grid_spec=pltpu.PrefetchScalarGridSpec(
            num_scalar_prefetch=1, grid=(1,),
            in_specs=[whole(a.shape) for a in lands] + [own_spec(k, o) for k, o in zip(kinds, owns)]
            + [whole(s) for s in shapes] * 3,
            out_specs=[whole(s) for s in out_shapes]),
        out_shape=[jax.ShapeDtypeStruct(s, F32) for s in out_shapes],
        compiler_params=_cparams("arbitrary"),
    )(me_idx, *lands, *owns, *ws, *ms, *vs)


def _small_view(n, a):
    if a.ndim == 1:
        return a.reshape(1, -1)
    if a.ndim == 3:
        return a.transpose(1, 2, 0).reshape(QKV_BLOCK * QKV_BLOCK, -1)
    return a.T if n == "w_if" else a


def _small_unview(n, a, shape):
    if len(shape) == 1:
        return a.reshape(shape)
    if len(shape) == 3:
        return a.reshape(QKV_BLOCK, QKV_BLOCK, -1).transpose(2, 0, 1)
    return a.T if n == "w_if" else a


def _small_shards(n, g):
    if n == "w_if":
        return g.reshape(N_DEV, -1, g.shape[1]).transpose(0, 2, 1)
    return g.reshape(g.shape[0], N_DEV, -1).transpose(1, 0, 2)


def _small_unshard(n, s):
    if n == "w_if":
        return s.transpose(0, 2, 1).reshape(-1, s.shape[1])
    return s.transpose(1, 0, 2).reshape(s.shape[1], -1)


def _to_hm(a, d):
    t = a.shape[0]
    return a.reshape(t, HEADS, d).transpose(1, 0, 2)


def _from_hm(a):
    h, t, d = a.shape
    return a.transpose(1, 0, 2).reshape(t, h * d)


def _gate_rows(g):
    t = g.shape[0]
    return g.T.reshape(HEADS, t // CHUNK, 1, CHUNK)


def _gate_cols(g):
    h, nc, _, c = g.shape
    return g.reshape(h, nc * c).T


def _blockdiag_dense(w):
    n = w.shape[0] * QKV_BLOCK
    tiled = jnp.tile(w.reshape(n, QKV_BLOCK), (1, n // QKV_BLOCK))
    r = lax.broadcasted_iota(jnp.int32, (n, n), 0)
    c = lax.broadcasted_iota(jnp.int32, (n, n), 1)
    return jnp.where(r // QKV_BLOCK == c // QKV_BLOCK, tiled, 0.0)


def _blockdiag_blocks(dense):
    n = dense[0].shape[0]
    k = len(dense)

    def body(*refs):
        r = lax.broadcasted_iota(jnp.int32, (n, n), 0)
        c = lax.broadcasted_iota(jnp.int32, (n, n), 1)
        fr = lax.broadcasted_iota(jnp.int32, (n, LANES), 0)
        fc = lax.broadcasted_iota(jnp.int32, (n, LANES), 1)
        fold = ((fr & (QKV_BLOCK - 1)) == fc).astype(BF16)
        for i in range(k):
            kept = jnp.where((r >> 2) == (c >> 2), refs[i][...], 0.0)
            refs[k + i][...] = sum(lax.dot_general(t, fold, _dims("nn", 2), preferred_element_type=F32) for t in _split3(kept))

    out = pl.pallas_call(body, name="blockdiag_blocks", out_shape=[jax.ShapeDtypeStruct((n, LANES), F32)] * k)(*dense)
    return [o[:, 0:QKV_BLOCK].reshape(n // QKV_BLOCK, QKV_BLOCK, QKV_BLOCK) for o in out]


def _col_blocks(w):
    k, n = w.shape
    return w.reshape(k, N_DEV, n // N_DEV).transpose(1, 0, 2)


def _from_col_blocks(g):
    d, k, n = g.shape
    return g.transpose(1, 0, 2).reshape(k, d * n)


def _first_norm(x, g):
    return _rowwise("pre_mix_norm", lambda xv, gv: ((_rms(xv, gv),), ()), [x], [g], [(x.shape[1], BF16)])[0]


def _local_step(x, h, tgt, weight, ws, prefetch, pass_on, on_grads, on_small):
    t, d = x.shape
    g1 = ws["g_pre_mix"]

    def dep(token):
        return () if token is None else (token,)

    w_in = weight("w_in", x)
    fetch_mix = prefetch(("w_pa", "w_pb", "w_o"), w_in)

    n_in = w_in.shape[2]

    offs = [0]
    for s in IN_SPLITS:
        offs.append(offs[-1] + s)

    def proj_in_fwd(hv, w):
        proj = jnp.concatenate([_raw_dot(hv, w[j], "nn") for j in range(N_DEV)], axis=1)
        parts = [proj[:, offs[i]:offs[i + 1]] for i in range(len(IN_SPLITS))]
        parts[4] = jnp.concatenate([parts[4], jnp.zeros((parts[4].shape[0], LANES - LOWRANK), F32)], axis=1)
        return parts, ()

    widths = [LANES if s == LOWRANK else s for s in IN_SPLITS]
    q_a, k_a, v_a, g_a, a_low_p, x_m, o_pre, gate_a, gate_b = _rowwise(
        "proj_in", proj_in_fwd, [h], [w_in], [(wd, BF16 if i in (2, 7, 8) else F32) for i, wd in enumerate(widths)],
        deps=dep(fetch_mix))

    w_a_up_p = jnp.pad(ws["w_a_up"], ((0, LANES - LOWRANK), (0, 0)))
    b_a_up = ws["b_a_up"]
    (la,) = _rowwise("gla_decay", lambda al, w, b: ((_log_decay(al, w, b),), ()), [a_low_p], [w_a_up_p, b_a_up],
                     [(HEADS * GLA_DK, F32)])
    fetch_up = prefetch(("w_up", "w_down"), la)
    q_hm, k_hm, la_hm = _to_hm(q_a, GLA_DK), _to_hm(k_a, GLA_DK), _to_hm(la, GLA_DK)
    o_gla, s_prev = _gla_fwd(q_hm, k_hm, v_a, la_hm, deps=dep(fetch_up))
    pass_mix = pass_on("w_pa", o_gla)
    gn = ws["g_gla_norm"]
    ml_w = HEADS * HEAD_W

    cw = ws["conv_w"]
    w_if_p = jnp.pad(ws["w_if"], ((0, 0), (0, LANES - 2 * HEADS)))
    pre_params = [cw[0:1], cw[1:2], cw[2:3], cw[3:4], ws["conv_b"],
                  _blockdiag_dense(ws["w_q_ml"]), _blockdiag_dense(ws["w_k_ml"]), _blockdiag_dense(ws["w_v_ml"]),
                  w_if_p[0:ml_w], w_if_p[ml_w:2 * ml_w], w_if_p[2 * ml_w:3 * ml_w],
                  jnp.pad(ws["b_if"], ((0, 0), (0, LANES - 2 * HEADS)))]
    x_pad = jnp.pad(x_m, ((HALO, 0), (0, 0)))
    xc, q_m, k_m, v_m, gl = _ml_pre_fwd(x_m, x_pad, pre_params, deps=dep(pass_mix))
    li, lf = _gate_rows(gl[:, 0:HEADS]), _gate_rows(gl[:, HEADS:2 * HEADS])
    hc, c_prev, n_prev, m_prev = _ml_fwd(q_m, k_m, v_m, li, lf)
    g_ml, skip = ws["g_ml_norm"], ws["ml_skip"]

    def proj_a_fwd(o, g, n_, w):
        ya = jnp.concatenate(_per_head(_gla_out, [o, g], [], [n_]), axis=1)
        return (ya, _raw_dot(ya, w, "nn")), ()

    ya_in, y_a = _rowwise("proj_a", proj_a_fwd, [o_gla, g_a], [gn, weight("w_pa", hc)], [(ml_w, BF16), (d, BF16)],
                          tile=512)

    def proj_b_fwd(a, b, c_, ga, gb, ya, g, s, w):
        hb = jnp.concatenate(_per_head(_ml_out, [a, b, c_], [g, s]), axis=1)
        yb = _raw_dot(hb, w, "nn")
        return (hb, yb, _merge(ga, gb, ya, yb)), ()

    h_b, y_b, merged = _rowwise("proj_b", proj_b_fwd, [hc, o_pre, xc, gate_a, gate_b, y_a], [g_ml, skip, weight("w_pb", hc)],
                                [(ml_w, BF16), (d, BF16), (d, BF16)], tile=512)

    gpm, gpl, gpo = ws["g_post_mix"], ws["g_pre_mlp"], ws["g_post_mlp"]

    def proj_o_fwd(mg, xv, w, a, b):
        zv = _raw_dot(mg, w, "nn")
        return (zv, *_post_mix(xv, zv, a, b)), ()

    pass_up = pass_on("w_up", merged)
    z, x1, h2 = _rowwise("proj_o", proj_o_fwd, [merged, x], [weight("w_o", merged), gpm, gpl],
                         [(d, F32), (d, F32), (d, BF16)], tile=512, deps=dep(pass_up))
    w_up = weight("w_up", h2)
    up, u = _mm(h2, w_up, "nn", (BF16, BF16), "mlp_up", tm=2048, shards="b",
                epilogue=lambda p: (p, jnp.square(jnp.maximum(p, 0.0))))

    def mlp_down_loss(uv, x1v, tgtv, w, g):
        dnv = _raw_dot(uv, w, "nn")
        loss, vjp = jax.vjp(lambda a, b, c_: _loss_rows(a, b, tgtv, c_), x1v, dnv, g)
        dx1, ddn, dg = vjp(jnp.ones((1, 1), F32))
        return (dx1, ddn), (jnp.broadcast_to(loss, (1, LANES)), dg)

    dx1_y, d_dn, loss, d_gpo = _rowwise("mlp_down", mlp_down_loss, [u, x1, tgt], [weight("w_down", u), gpo],
                                        [(d, F32), (d, BF16)], [((1, LANES), F32), ((1, d), F32)], tile=512)

    (d_up,) = _mm(d_dn, weight("w_down", u), "nt", (BF16,), "mlp_down_dx", extra=[up],
                  epilogue=lambda p, a: (p * (2.0 * jnp.maximum(a.astype(F32), 0.0)),))
    dw_down = _mm(u, d_dn, "tn", BF16, "mlp_down_dw", tm=512)
    dw_up = _mm_shard_cols(h2, [d_up], [d_up.shape[1]], w_up.shape[2], "mlp_up_dw", 0, d)
    sent_mlp = on_grads(dict(w_down=dw_down, w_up=dw_up))

    def mlp_up_dx(dup, xv, zv, dx1, w, a, b):
        _, vjp = jax.vjp(_post_mix, xv, zv, a, b)
        ns = w.shape[2]
        dh2 = sum(_raw_dot(dup[:, j * ns:(j + 1) * ns], w[j], "nt") for j in range(w.shape[0]))
        dx, dz, da, db = vjp((dx1, dh2))
        return (dx, dz), (da, db)

    dx_res, d_z, d_gpm, d_gpl = _rowwise("mlp_up_dx", mlp_up_dx, [d_up, x, z, dx1_y], [w_up, gpm, gpl],
                                         [(d, F32), (d, BF16)], [((1, d), F32), ((1, d), F32)], tile=512, deps=dep(sent_mlp))
    dw_o = _mm(merged, d_z, "tn", BF16, "proj_o_dw")

    def proj_o_dx(dz, ga, gb, ya, yb, w):
        return jax.vjp(_merge, ga, gb, ya, yb)[1](_raw_dot(dz, w, "nt")), ()

    d_ga, d_gb, d_ya, d_yb = _rowwise("proj_o_dx", proj_o_dx, [d_z, gate_a, gate_b, y_a, y_b], [weight("w_o", merged)],
                                      [(d, BF16)] * 4, tile=512)
    dw_pa = _mm(ya_in, d_ya, "tn", BF16, "proj_a_dw")
    dw_pb = _mm(h_b, d_yb, "tn", BF16, "proj_b_dw")
    sent_mix = on_grads(dict(w_o=dw_o, w_pa=dw_pa, w_pb=dw_pb))

    def proj_b_dx(dyb, a, b, c_, w, g, s):
        ct = _raw_dot(dyb, w, "nt")
        parts = []
        for hs in _head_slices(HEAD_W):
            _, vjp = jax.vjp(_ml_out, a[:, hs], b[:, hs], c_[:, hs], g[:, hs], s[:, hs])
            parts.append(vjp(ct[:, hs]))
        cat = lambda i: jnp.concatenate([p[i] for p in parts], axis=1)
        return (cat(0), cat(1), cat(2)), (cat(3), cat(4))

    d_hc, d_opre, d_xc, d_gml, d_skip = _rowwise("proj_b_dx", proj_b_dx, [d_yb, hc, o_pre, xc],
                                                 [weight("w_pb", hc), g_ml, skip], [(ml_w, F32), (ml_w, BF16), (ml_w, F32)],
                                                 [((1, ml_w), F32)] * 2,
                                                 tile=512, deps=dep(sent_mix))
    d_qm, d_km, d_vm, d_li, d_lf = _ml_bwd(q_m, k_m, v_m, li, lf, c_prev, n_prev, m_prev, d_hc)
    d_gl = jnp.concatenate([_gate_cols(d_li), _gate_cols(d_lf), jnp.zeros((t, LANES - 2 * HEADS), F32)], axis=1)
    pre_grads = _ml_pre_bwd(x_m, x_pad, pre_params, [d_xc, d_qm, d_km, d_vm, d_gl], tile=512)
    d_xm = pre_grads[0]
    d_cw = jnp.concatenate(pre_grads[1:5], axis=0)
    d_cb = pre_grads[5]
    d_wq, d_wk, d_wv = _blockdiag_blocks(pre_grads[6:9])
    d_wif = jnp.concatenate(pre_grads[9:12], axis=0)[:, 0:2 * HEADS]
    d_bif = pre_grads[12][:, 0:2 * HEADS]

    def proj_a_dx(dya, o, g, w, n_):
        ct = _raw_dot(dya, w, "nt")
        parts = []
        for hs in _head_slices(HEAD_W):
            _, vjp = jax.vjp(_gla_out, o[:, hs], g[:, hs], n_)
            parts.append(vjp(ct[:, hs]))
        cat = lambda i: jnp.concatenate([p[i] for p in parts], axis=1)
        return (cat(0), cat(1)), (sum(p[2] for p in parts),)

    d_o, d_g_a, d_gn = _rowwise("proj_a_dx", proj_a_dx, [d_ya, o_gla, g_a], [weight("w_pa", o_gla), gn],
                                [(ml_w, F32), (ml_w, BF16)],
                                [((1, HEAD_W), F32)], tile=512)
    dq_hm, dk_hm, d_va, dla_hm = _gla_bwd(q_hm, k_hm, v_a, la_hm, s_prev, d_o)

    def decay_bwd(al, ct, w, b):
        _, vjp = jax.vjp(_log_decay, al, w, b)
        dal, dw, db = vjp(ct)
        return (dal,), (dw, db)

    d_alow_p, d_wa_p, d_ba = _rowwise("gla_decay_bwd", decay_bwd, [a_low_p, _from_hm(dla_hm)], [w_a_up_p, b_a_up],
                                      [(LANES, BF16)], [(w_a_up_p.shape, F32), (b_a_up.shape, F32)])
    d_proj = [jnp.concatenate([_from_hm(dq_hm), _from_hm(dk_hm), d_va, d_g_a], axis=1), d_alow_p,
              jnp.concatenate([d_xm, d_opre, d_ga, d_gb], axis=1)]
    d_widths = [offs[4], LOWRANK, offs[9] - offs[5]]
    d_pieces = _shard_pieces(d_widths, n_in)
    small = dict(w_a_up=d_wa_p[0:LOWRANK], b_a_up=d_ba, g_gla_norm=d_gn, conv_w=d_cw, conv_b=d_cb,
                 w_q_ml=d_wq, w_k_ml=d_wk, w_v_ml=d_wv, w_if=d_wif, b_if=d_bif, ml_skip=d_skip, g_ml_norm=d_gml,
                 g_post_mix=d_gpm, g_pre_mlp=d_gpl, g_post_mlp=d_gpo)
    sent_small = on_small(small, loss)
    sent_in = sent_small
    for half in range(2):
        dw_half = _mm_shard_cols(h, d_proj, d_widths, n_in, "proj_in_dw_%d" % half, half, d // 2, deps=dep(sent_in))
        sent_in = on_grads({"w_in#%d" % half: dw_half})

    def proj_in_dx(dp_a, dp_low, dp_b, xv, dres, w, g):
        dh = 0.0
        for s in range(N_DEV):
            for i, c_in, c_w, wd in d_pieces[s]:
                src = (dp_a, dp_low, dp_b)[i]
                cols = src.shape[1] - c_in if wd < LANES else wd
                dh = dh + _raw_dot(src[:, c_in:c_in + cols], w[s][:, c_w:c_w + cols], "nt")
        _, vjp = jax.vjp(_rms, xv, g)
        dx, dg = vjp(dh)
        return (dx + dres,), (dg,)

    grad_x, d_g1 = _rowwise("proj_in_dx", proj_in_dx, [*d_proj, x, dx_res], [w_in, g1], [(d, F32)], [((1, d), F32)],
                            deps=dep(sent_in))
    return grad_x, on_small(dict(g_pre_mix=d_g1), None)


BIG = ("w_in", "w_pa", "w_pb", "w_o", "w_up", "w_down")
BIG_COL_SHARDED = ("w_in", "w_pa", "w_pb", "w_up")
SMALL_SHARDED = ("w_a_up", "conv_w", "w_if")
SMALL = ("g_pre_mix", "w_a_up", "b_a_up", "g_gla_norm", "conv_w", "conv_b", "w_q_ml", "w_k_ml", "w_v_ml", "w_if", "b_if",
         "ml_skip", "g_ml_norm", "g_post_mix", "g_pre_mlp", "g_post_mlp")
WEIGHTS = ("g_pre_mix", "w_in", "w_a_up", "b_a_up", "g_gla_norm", "conv_w", "conv_b", "w_q_ml", "w_k_ml", "w_v_ml", "w_if", "b_if",
           "ml_skip", "g_ml_norm", "w_pa", "w_pb", "w_o", "g_post_mix", "g_pre_mlp", "w_up", "w_down", "g_post_mlp")


def kernel(x, g_pre_mix, w_in, w_a_up, b_a_up, g_gla_norm, conv_w, conv_b, w_q_ml, w_k_ml, w_v_ml, w_if, b_if, ml_skip, g_ml_norm, w_pa, w_pb, w_o, g_post_mix, g_pre_mlp, w_up, w_down, g_post_mlp, loss_target, m_g_pre_mix, m_w_in, m_w_a_up, m_b_a_up, m_g_gla_norm, m_conv_w, m_conv_b, m_w_q_ml, m_w_k_ml, m_w_v_ml, m_w_if, m_b_if, m_ml_skip, m_g_ml_norm, m_w_pa, m_w_pb, m_w_o, m_g_post_mix, m_g_pre_mlp, m_w_up, m_w_down, m_g_post_mlp, v_g_pre_mix, v_w_in, v_w_a_up, v_b_a_up, v_g_gla_norm, v_conv_w, v_conv_b, v_w_q_ml, v_w_k_ml, v_w_v_ml, v_w_if, v_b_if, v_ml_skip, v_g_ml_norm, v_w_pa, v_w_pb, v_w_o, v_g_post_mix, v_g_pre_mlp, v_w_up, v_w_down, v_g_post_mlp):
    args = dict(locals())
    w = {n: args[n][0] for n in WEIGHTS}
    m = {n: args["m_" + n][0] for n in WEIGHTS}
    v = {n: args["v_" + n][0] for n in WEIGHTS}

    me_lin = _lin(_me())
    me_idx = jnp.reshape(me_lin, (1,)).astype(jnp.int32)

    def full_weight(n, g):
        if n in ("w_in", "w_up"):
            return g
        return _from_col_blocks(g) if n in BIG_COL_SHARDED else g.reshape(-1, g.shape[-1])

    def grad_parts(n, g):
        if n.partition("#")[0] in ("w_in", "w_up"):
            return g
        return (_col_blocks(g) if n in BIG_COL_SHARDED else g.reshape(N_DEV, -1, g.shape[-1])).astype(BF16)

    sharded_names = tuple(SMALL_SHARDED)
    narrow = {n: w[n].astype(BF16) for n in BIG}
    ready, pending, passing = {}, {}, {}
    first_state, _ = _copies_start(["gather"] * len(sharded_names) + ["gather_chips"],
                                   [_small_view(n, w[n]) for n in sharded_names] + [narrow["w_in"]], "allgather_start_first")

    def prefetch(group, after):
        state, token = _copies_start("gather_chips", [narrow[n] for n in group], "allgather_start_" + group[0], after)
        for n in group:
            pending[n] = (group, state)
        return token

    def pass_on(n, after):
        group, state = pending[n]
        shards, lands = _copies_wait(state, after, "allgather_wait_" + group[0])
        state, token = _copies_start("gather_pass", shards, "allgather_pass_" + group[0], lands=lands)
        for gn in group:
            passing[gn] = (group, state)
        return token

    def weight(n, after):
        if n not in ready:
            group, state = passing[n]
            shards, lands = _copies_wait(state, after, "allgather_passed_" + group[0])
            for gn, shard, land in zip(group, shards, lands):
                ready[gn] = full_weight(gn, lax.dynamic_update_slice(land, shard[None], (me_lin, 0, 0)))
        return ready[n]

    h = _first_norm(x[0], w["g_pre_mix"].reshape(1, -1))
    first_own, first_lands = _copies_wait(first_state, [h] + [narrow[n] for n in BIG if n != "w_in"], "allgather_wait_first")
    state, _ = _copies_start("gather_pass", first_own[-1:], "allgather_pass_w_in", lands=first_lands[-1:])
    passing["w_in"] = (("w_in",), state)
    ws = {n: (w[n].reshape(1, -1) if w[n].ndim == 1 else w[n]) for n in SMALL if n not in SMALL_SHARDED}
    for n, own, land in zip(sharded_names, first_own, first_lands):
        ws[n] = _small_unshard(n, lax.dynamic_update_slice(land, own[None], (me_lin, 0, 0)))

    sent = []

    def on_grads(grads):
        names = tuple(grads)
        state, token = _copies_start("exchange", [grad_parts(n, grads[n]) for n in names],
                                     "exchange_start_" + names[0].replace("#", "_"))
        sent.append((names, state))
        return token

    small_sent = []

    def on_small(small, loss):
        names = tuple(small)
        kinds = ["exchange" if n in SMALL_SHARDED else "gather" for n in names]
        srcs = [_small_shards(n, small[n]) if n in SMALL_SHARDED else _small_view(n, small[n]) for n in names]
        extra = [] if loss is None else [loss]
        state, token = _copies_start(kinds + ["gather"] * len(extra), srcs + extra, "allgather_start_small_" + names[0])
        small_sent.append((names, kinds, state))
        return token

    grad_x, last_token = _local_step(x[0], h, loss_target[0], weight, ws, prefetch, pass_on, on_grads, on_small)

    out = {}

    chunks = {}

    def finish(names, state, after):
        parts, lands = _copies_wait(state, after, "exchange_wait_" + names[0].replace("#", "_"))
        for name, part, land in zip(names, parts, lands):
            n, _, chunk = name.partition("#")
            chunks.setdefault(n, []).append((land, part))
            if chunk in ("", "1"):
                got_lands, got_parts = zip(*chunks[n])
                out[n] = _sum_adamw(got_lands, got_parts, me_idx, w[n], m[n], v[n], "adamw_" + n)

    def finish_small(names, kinds, state, after):
        own, lands = _copies_wait(state, after, "allgather_wait_small_" + names[0])
        k = len(names)
        upd = _small_update("adamw_small_" + names[0], me_idx, kinds, lands[:k], own[:k],
                            *[[_small_view(n, d[n]) for n in names] for d in (w, m, v)], sums=list(zip(lands[k:], own[k:])))
        for i, n in enumerate(names):
            out[n] = tuple(_small_unview(n, a, w[n].shape) for a in upd[4 * i:4 * i + 4])
        return upd[4 * k:]

    (loss_sum,) = finish_small(*small_sent[0], [grad_x, last_token])
    for names, state in sent[:-2]:
        finish(names, state, [grad_x, last_token])
    finish(*sent[-2], [loss_sum] + [out[n][1] for n in BIG if n in out])
    finish(*sent[-1], [loss_sum])
    finish_small(*small_sent[1], [out["w_in"][1]])

    shaped = lambda a, n: a.reshape(args[n].shape)
    return (loss_sum[0, 0], grad_x[None],
            *[shaped(out[n][0], n) for n in WEIGHTS], *[shaped(out[n][1], n) for n in WEIGHTS],
            *[shaped(out[n][2], n) for n in WEIGHTS], *[shaped(out[n][3], n) for n in WEIGHTS])
```

```python
import functools

import jax
import jax.numpy as jnp
from jax import lax
from jax.experimental import pallas as pl
from jax.experimental.pallas import tpu as pltpu

F32 = jnp.float32
BF16 = jnp.bfloat16
MESH = pl.DeviceIdType.MESH

N_DEV = 8
EPS = 1e-6
CHUNK = 64
CHUNKS_PER_STEP = 4
HEADS = 4
GLA_DK = 64
HEAD_W = 128
GLA_GATE_NORM = 16.0
LOWRANK = 16
CONV_K = 4
QKV_BLOCK = 4
LANES = 128
HALO = 8
IN_SPLITS = (256, 256, 512, 512, 16, 512, 512, 1024, 1024)

ADAM_LR = 0.001
ADAM_B1 = 0.9
ADAM_B2 = 0.999
ADAM_EPS = 1e-08
ADAM_WD = 0.01
ADAM_STEP = 10

VMEM_LIMIT = 56 * 1024 * 1024


def _cparams(*sem):
    return pltpu.CompilerParams(dimension_semantics=sem, vmem_limit_bytes=VMEM_LIMIT)


def _dims(mode, ndim):
    contract = {"nn": ((ndim - 1,), (ndim - 2,)), "nt": ((ndim - 1,), (ndim - 1,)), "tn": ((ndim - 2,), (ndim - 2,))}[mode]
    return contract, (((0,), (0,)) if ndim == 3 else ((), ()))


def _raw_dot(a, b, mode):
    return lax.dot_general(a.astype(BF16), b.astype(BF16), _dims(mode, a.ndim), preferred_element_type=F32)


@functools.partial(jax.custom_vjp, nondiff_argnums=(2,))
def _bdot(a, b, mode):
    return _raw_dot(a, b, mode)


def _bdot_fwd(a, b, mode):
    return _raw_dot(a, b, mode), (a, b)


def _bdot_bwd(mode, res, ct):
    a, b = res
    if mode == "nn":
        da, db = _raw_dot(ct, b, "nt"), _raw_dot(a, ct, "tn")
    elif mode == "nt":
        da, db = _raw_dot(ct, b, "nn"), _raw_dot(ct, a, "tn")
    else:
        da, db = _raw_dot(b, ct, "nt"), _raw_dot(a, ct, "nn")
    return da.astype(a.dtype), db.astype(b.dtype)


_bdot.defvjp(_bdot_fwd, _bdot_bwd)


def _split3(x):
    hi = x.astype(BF16)
    r1 = x - hi.astype(F32)
    mid = r1.astype(BF16)
    return hi, mid, (r1 - mid.astype(F32)).astype(BF16)


def _split_dot(tri, x):
    if x.ndim == 3:
        tri = jnp.broadcast_to(tri, (x.shape[0], *tri.shape))
    return sum(lax.dot_general(tri, t, _dims("nn", x.ndim), preferred_element_type=F32) for t in _split3(x))


def _tri(n, lower):
    r = lax.broadcasted_iota(jnp.int32, (n, n), 0)
    c = lax.broadcasted_iota(jnp.int32, (n, n), 1)
    return ((c <= r) if lower else (c >= r)).astype(BF16)


@jax.custom_vjp
def _cumsum_rows(x):
    return _split_dot(_tri(x.shape[-2], True), x)


def _cumsum_rows_fwd(x):
    return _cumsum_rows(x), None


def _cumsum_rows_bwd(_, ct):
    return (_split_dot(_tri(ct.shape[-2], False), ct),)


_cumsum_rows.defvjp(_cumsum_rows_fwd, _cumsum_rows_bwd)


def _abs(x):
    return jnp.where(x >= 0, x, -x)


def _sigmoid(x):
    return lax.logistic(x)


def _log_sigmoid(x):
    return jnp.minimum(x, 0.0) - jnp.log(1.0 + jnp.exp(-_abs(x)))


def _rms(x, g):
    return x * lax.rsqrt(jnp.mean(x * x, axis=-1, keepdims=True) + EPS) * g


def _head_slices(w):
    return [slice(h * w, (h + 1) * w) for h in range(HEADS)]


def _heads(ref, rows=slice(None)):
    return jnp.stack([ref[rows, hs] for hs in _head_slices(HEAD_W)])


def _put_heads(ref, val, rows=slice(None)):
    for h, hs in enumerate(_head_slices(HEAD_W)):
        ref[rows, hs] = val[h].astype(ref.dtype)


def _tile(dim, want):
    if dim <= want or dim % LANES:
        return dim
    t = want
    while dim % t:
        t -= LANES
    return t


def _mm(a, b, mode, out_dtype, name, tm=1024, tn=1024, tk=4096, epilogue=None, extra=(), deps=(), shards=None):
    if shards == "b":
        assert mode == "nn"
        ns = b.shape[2]
        (m, k), (k2, n) = a.shape, (b.shape[1], b.shape[0] * ns)
        tn = ns
    elif mode == "nn":
        (m, k), (k2, n) = a.shape, b.shape
    elif mode == "nt":
        (m, k), (n, k2) = a.shape, b.shape
    else:
        (k, m), (k2, n) = a.shape, b.shape
    assert k == k2, (name, a.shape, b.shape)
    tm, tn, tk = _tile(m, tm), _tile(n, tn), _tile(k, tk)
    nk = k // tk
    out_dtypes = out_dtype if epilogue else (out_dtype,)
    assert nk == 1 or (out_dtype == F32 and not epilogue), name
    n_in = 2 + len(extra)

    def body(*refs):
        p = _raw_dot(refs[0][...], refs[1][...], mode)
        if nk > 1:
            _accumulate(pl.program_id(2), [refs[n_in + len(deps)]], [p])
            return
        outs = epilogue(p, *[r[...] for r in refs[2:n_in]]) if epilogue else (p,)
        for ref, val in zip(refs[n_in + len(deps):], outs):
            ref[...] = val.astype(ref.dtype)

    a_spec = pl.BlockSpec((tk, tm), lambda i, j, kk: (kk, i)) if mode == "tn" else pl.BlockSpec((tm, tk), lambda i, j, kk: (i, kk))
    if shards == "b":
        b_spec = pl.BlockSpec((None, tk, tn), lambda i, j, kk: (j, kk, 0))
    elif mode == "nt":
        b_spec = pl.BlockSpec((tn, tk), lambda i, j, kk: (j, kk))
    else:
        b_spec = pl.BlockSpec((tk, tn), lambda i, j, kk: (kk, j))
    o_spec = pl.BlockSpec((tm, tn), lambda i, j, kk: (i, j))
    res = pl.pallas_call(
        body, name=name, grid=(m // tm, n // tn, nk),
        in_specs=[a_spec, b_spec] + [o_spec] * len(extra) + [ANY] * len(deps), out_specs=[o_spec] * len(out_dtypes),
        out_shape=[jax.ShapeDtypeStruct((m, n), dt) for dt in out_dtypes],
        compiler_params=_cparams("parallel", "parallel", "arbitrary"),
    )(a, b, *extra, *deps)
    return res if epilogue else res[0]


def _shard_pieces(widths, n):
    bounds = [0]
    for wd in widths:
        bounds.append(bounds[-1] + wd)
    assert bounds[-1] == N_DEV * n
    return [[(i, max(s * n, b) - b, max(s * n, b) - s * n, min((s + 1) * n, b + wd) - max(s * n, b))
             for i, (b, wd) in enumerate(zip(bounds, widths)) if b < (s + 1) * n and b + wd > s * n]
            for s in range(N_DEV)]


def _mm_shard_cols(a, bs, widths, n, name, row_tile, tm, deps=()):
    t = a.shape[0]
    nb = len(bs)
    pieces = _shard_pieces(widths, n)

    def body(a_ref, *rest):
        b_refs = rest[:nb]
        o_ref, at_ref = rest[nb + len(deps):]
        j = pl.program_id(0)

        @pl.when(j == 0)
        def _():
            at_ref[...] = a_ref[...].astype(BF16).T

        for s in range(N_DEV):
            @pl.when(j == s)
            def _(s=s):
                for i, c_in, c_out, wd in pieces[s]:
                    cols = min(_round_up(wd, LANES), bs[i].shape[1] - c_in) if wd < LANES else wd
                    p = _raw_dot(at_ref[...], b_refs[i][:, c_in:c_in + cols], "nn")
                    o_ref[:, c_out:c_out + wd] = p[:, 0:wd].astype(BF16)

    return pl.pallas_call(
        body, name=name, grid=(N_DEV,),
        in_specs=[pl.BlockSpec((t, tm), lambda j: (0, row_tile))]
        + [pl.BlockSpec(b.shape, lambda j: (0, 0), pipeline_mode=pl.Buffered(1)) for b in bs] + [ANY] * len(deps),
        out_specs=pl.BlockSpec((None, tm, n), lambda j: (j, 0, 0)),
        out_shape=jax.ShapeDtypeStruct((N_DEV, tm, n), BF16),
        scratch_shapes=[pltpu.VMEM((tm, t), BF16)],
        compiler_params=_cparams("arbitrary"),
    )(a, *bs, *deps)


def _round_up(v, m):
    return -(-v // m) * m


def _rowwise(name, fn, rows, params, out_rows, out_accs=(), tile=256, deps=()):
    t = rows[0].shape[0]
    r = min(tile, t)
    assert t % r == 0
    n_in, n_or = len(rows) + len(params), len(out_rows)
    n_all = n_in + len(deps)
    params = list(params) + list(deps)

    def body(*refs):
        vals = [ref[...] for ref in refs[:n_in]]
        outs = refs[n_all:]
        ro, ao = fn(*vals)
        for ref, v in zip(outs[:n_or], ro):
            ref[...] = v.astype(ref.dtype)
        if out_accs:
            _accumulate(pl.program_id(0), outs[n_or:], ao)

    def full(shape):
        return pl.BlockSpec(shape, lambda i, nd=len(shape): (0,) * nd)

    return pl.pallas_call(
        body, name=name, grid=(t // r,),
        in_specs=[pl.BlockSpec((r, a.shape[1]), lambda i: (i, 0)) for a in rows] + [full(p.shape) for p in params],
        out_specs=[pl.BlockSpec((r, w), lambda i: (i, 0)) for w, _ in out_rows] + [full(s) for s, _ in out_accs],
        out_shape=[jax.ShapeDtypeStruct((t, w), dt) for w, dt in out_rows] + [jax.ShapeDtypeStruct(s, dt) for s, dt in out_accs],
        compiler_params=_cparams("arbitrary"),
    )(*rows, *params)


def _accumulate(step, refs, vals):
    for ref, v in zip(refs, vals):
        @pl.when(step == 0)
        def _(ref=ref, v=v):
            ref[...] = v.astype(ref.dtype)

        @pl.when(step > 0)
        def _(ref=ref, v=v):
            ref[...] += v.astype(ref.dtype)


def _gla_chunk(q, k, v, la, st):
    c = q.shape[-2]
    row = lax.broadcasted_iota(jnp.int32, (c, c), 0)
    col = lax.broadcasted_iota(jnp.int32, (c, c), 1)
    cum = _cumsum_rows(la)
    cl = jnp.sum(la, axis=-2, keepdims=True)
    ep = jnp.exp(cum)
    en = jnp.exp(-cum)
    qs = q * (GLA_DK ** -0.5)
    qp = qs * ep
    a_f = _bdot(qp, k * en, "nt")
    a_b = _bdot(qs * en, k * ep, "nt")
    sc = jnp.where(row >= col, a_f, a_b)
    o = _bdot(sc, v, "nn") + _bdot(qp, st, "nt")
    kd = k * jnp.exp(cl - cum)
    st_new = st * jnp.exp(cl) + _bdot(v, kd, "tn")
    return o, st_new


def _gla_specs(nc, rev):
    nb = nc // CHUNKS_PER_STEP
    rows = CHUNKS_PER_STEP * CHUNK

    def blk(n):
        return (nb - 1 - n) if rev else n
    hm = pl.BlockSpec((HEADS, rows, GLA_DK), lambda n: (0, blk(n), 0))
    tm = pl.BlockSpec((rows, HEADS * HEAD_W), lambda n: (blk(n), 0))
    st = pl.BlockSpec((HEADS, CHUNKS_PER_STEP, HEAD_W, GLA_DK), lambda n: (0, blk(n), 0, 0))
    return nb, hm, tm, st


def _chunk_rows(c):
    return slice(c * CHUNK, (c + 1) * CHUNK)


def _gla_fwd(q, k, v, la, deps=()):
    t = v.shape[0]
    nc = t // CHUNK
    nb, hm, tm, st = _gla_specs(nc, False)

    def body(q_ref, k_ref, v_ref, la_ref, *rest):
        o_ref, sp_ref, st_ref = rest[len(deps):]

        @pl.when(pl.program_id(0) == 0)
        def _():
            st_ref[...] = jnp.zeros_like(st_ref)

        s = st_ref[...]
        for c in range(CHUNKS_PER_STEP):
            r = _chunk_rows(c)
            sp_ref[:, c] = s
            o, s = _gla_chunk(q_ref[:, r], k_ref[:, r], _heads(v_ref, r), la_ref[:, r], s)
            _put_heads(o_ref, o, r)
        st_ref[...] = s

    return pl.pallas_call(
        body, name="gla_fwd", grid=(nb,),
        in_specs=[hm, hm, tm, hm] + [ANY] * len(deps), out_specs=[tm, st],
        out_shape=[jax.ShapeDtypeStruct((t, HEADS * HEAD_W), F32), jax.ShapeDtypeStruct((HEADS, nc, HEAD_W, GLA_DK), F32)],
        scratch_shapes=[pltpu.VMEM((HEADS, HEAD_W, GLA_DK), F32)],
        compiler_params=_cparams("arbitrary"),
    )(q, k, v, la, *deps)


def _gla_bwd(q, k, v, la, sp, do):
    t = v.shape[0]
    nc = t // CHUNK
    nb, hm, tm, st = _gla_specs(nc, True)

    def body(q_ref, k_ref, v_ref, la_ref, sp_ref, do_ref, dq_ref, dk_ref, dv_ref, dla_ref, ds_ref):
        @pl.when(pl.program_id(0) == 0)
        def _():
            ds_ref[...] = jnp.zeros_like(ds_ref)

        ds = ds_ref[...]
        for c in reversed(range(CHUNKS_PER_STEP)):
            r = _chunk_rows(c)
            _, vjp = jax.vjp(_gla_chunk, q_ref[:, r], k_ref[:, r], _heads(v_ref, r), la_ref[:, r], sp_ref[:, c])
            dq, dk, dv, dla, ds = vjp((_heads(do_ref, r), ds))
            dq_ref[:, r] = dq.astype(dq_ref.dtype)
            dk_ref[:, r] = dk.astype(dk_ref.dtype)
            _put_heads(dv_ref, dv, r)
            dla_ref[:, r] = dla
        ds_ref[...] = ds

    hm_shape = jax.ShapeDtypeStruct((HEADS, t, GLA_DK), BF16)
    return pl.pallas_call(
        body, name="gla_bwd", grid=(nb,),
        in_specs=[hm, hm, tm, hm, st, tm], out_specs=[hm, hm, tm, hm],
        out_shape=[hm_shape, hm_shape, jax.ShapeDtypeStruct((t, HEADS * HEAD_W), BF16),
                   jax.ShapeDtypeStruct((HEADS, t, GLA_DK), F32)],
        scratch_shapes=[pltpu.VMEM((HEADS, HEAD_W, GLA_DK), F32)],
        compiler_params=_cparams("arbitrary"),
    )(q, k, v, la, sp, do)


def _ml_chunk(q, k, v, li_r, lf_r, cm, nv, m):
    c = q.shape[-2]
    row = lax.broadcasted_iota(jnp.int32, (c, c), 0)
    col = lax.broadcasted_iota(jnp.int32, (c, c), 1)
    eye = (row == col).astype(F32)
    li_c = jnp.sum(eye * li_r, axis=-1, keepdims=True)
    lf_c = jnp.sum(eye * lf_r, axis=-1, keepdims=True)
    fc_c = jnp.sum((col <= row).astype(F32) * lf_r, axis=-1, keepdims=True)
    fc_r = jnp.sum((row <= col).astype(F32) * lf_c, axis=-2, keepdims=True)
    f_last = jnp.sum(lf_r, axis=-1, keepdims=True)
    kc = k * (HEAD_W ** -0.5)
    a_c = f_last - fc_c + li_c
    m_loc = jnp.max(a_c, axis=-2, keepdims=True)
    kw = kc * jnp.exp(a_c - m_loc)
    c_chunk = _bdot(kw, v, "tn")
    n_chunk = jnp.sum(kw, axis=-2, keepdims=True)
    m_new = jnp.maximum(f_last + m, m_loc)
    sp = jnp.exp(f_last + m - m_new)
    sl = jnp.exp(m_loc - m_new)
    cm_new = sp * cm + sl * c_chunk
    nv_new = sp * nv + sl * n_chunk
    log_d = li_r - _abs(fc_c - fc_r)
    g_inter = fc_c + m
    m_t = jnp.maximum(g_inter, jnp.max(log_d, axis=-1, keepdims=True))
    s = _bdot(q, kc, "nt") * jnp.exp(log_d - m_t)
    sc = jnp.exp(g_inter - m_t)
    num = _bdot(s, v, "nn") + sc * _bdot(q, cm, "nn")
    den = jnp.sum(s, axis=-1, keepdims=True) + sc * jnp.sum(q * nv, axis=-1, keepdims=True)
    den = jnp.maximum(_abs(den), jnp.exp(-m_t))
    return num / den, cm_new, nv_new, m_new


def _ml_specs(nc, rev):
    nb = nc // CHUNKS_PER_STEP

    def blk(n):
        return (nb - 1 - n) if rev else n
    tm = pl.BlockSpec((CHUNKS_PER_STEP * CHUNK, HEADS * HEAD_W), lambda n: (blk(n), 0))
    gate = pl.BlockSpec((HEADS, CHUNKS_PER_STEP, 1, CHUNK), lambda n: (0, blk(n), 0, 0))
    cm = pl.BlockSpec((HEADS, CHUNKS_PER_STEP, HEAD_W, HEAD_W), lambda n: (0, blk(n), 0, 0))
    vec = pl.BlockSpec((HEADS, CHUNKS_PER_STEP, 1, HEAD_W), lambda n: (0, blk(n), 0, 0))
    return nb, tm, gate, cm, vec


_ML_STATE = [pltpu.VMEM((HEADS, HEAD_W, HEAD_W), F32), pltpu.VMEM((HEADS, 1, HEAD_W), F32), pltpu.VMEM((HEADS, 1, HEAD_W), F32)]


def _ml_fwd(q, k, v, li, lf):
    t = q.shape[0]
    nc = t // CHUNK
    nb, tm, gate, cm, vec = _ml_specs(nc, False)

    def body(q_ref, k_ref, v_ref, li_ref, lf_ref, hc_ref, cp_ref, np_ref, mp_ref, c_ref, n_ref, m_ref):
        @pl.when(pl.program_id(0) == 0)
        def _():
            c_ref[...] = jnp.zeros_like(c_ref)
            n_ref[...] = jnp.zeros_like(n_ref)
            m_ref[...] = jnp.zeros_like(m_ref)

        cs, ns, ms = c_ref[...], n_ref[...], m_ref[...][:, :, 0:1]
        for c in range(CHUNKS_PER_STEP):
            r = _chunk_rows(c)
            cp_ref[:, c] = cs
            np_ref[:, c] = ns
            mp_ref[:, c] = jnp.broadcast_to(ms, m_ref.shape)
            hc, cs, ns, ms = _ml_chunk(_heads(q_ref, r), _heads(k_ref, r), _heads(v_ref, r), li_ref[:, c], lf_ref[:, c],
                                       cs, ns, ms)
            _put_heads(hc_ref, hc, r)
        c_ref[...] = cs
        n_ref[...] = ns
        m_ref[...] = jnp.broadcast_to(ms, m_ref.shape)

    return pl.pallas_call(
        body, name="mlstm_fwd", grid=(nb,),
        in_specs=[tm, tm, tm, gate, gate], out_specs=[tm, cm, vec, vec],
        out_shape=[jax.ShapeDtypeStruct((t, HEADS * HEAD_W), F32), jax.ShapeDtypeStruct((HEADS, nc, HEAD_W, HEAD_W), F32),
                   jax.ShapeDtypeStruct((HEADS, nc, 1, HEAD_W), F32), jax.ShapeDtypeStruct((HEADS, nc, 1, HEAD_W), F32)],
        scratch_shapes=_ML_STATE,
        compiler_params=_cparams("arbitrary"),
    )(q, k, v, li, lf)


def _ml_bwd(q, k, v, li, lf, cp, npv, mp, dhc):
    t = q.shape[0]
    nc = t // CHUNK
    nb, tm, gate, cm, vec = _ml_specs(nc, True)

    def body(q_ref, k_ref, v_ref, li_ref, lf_ref, cp_ref, np_ref, mp_ref, dhc_ref,
             dq_ref, dk_ref, dv_ref, dli_ref, dlf_ref, dc_ref, dn_ref, dm_ref):
        @pl.when(pl.program_id(0) == 0)
        def _():
            dc_ref[...] = jnp.zeros_like(dc_ref)
            dn_ref[...] = jnp.zeros_like(dn_ref)
            dm_ref[...] = jnp.zeros_like(dm_ref)

        dc, dn, dm = dc_ref[...], dn_ref[...], dm_ref[...][:, :, 0:1]
        for c in reversed(range(CHUNKS_PER_STEP)):
            r = _chunk_rows(c)
            _, vjp = jax.vjp(_ml_chunk, _heads(q_ref, r), _heads(k_ref, r), _heads(v_ref, r), li_ref[:, c], lf_ref[:, c],
                             cp_ref[:, c], np_ref[:, c], mp_ref[:, c][:, :, 0:1])
            dq, dk, dv, dli, dlf, dc, dn, dm = vjp((_heads(dhc_ref, r), dc, dn, dm))
            _put_heads(dq_ref, dq, r)
            _put_heads(dk_ref, dk, r)
            _put_heads(dv_ref, dv, r)
            dli_ref[:, c] = dli
            dlf_ref[:, c] = dlf
        dc_ref[...] = dc
        dn_ref[...] = dn
        dm_ref[...] = jnp.broadcast_to(dm, dm_ref.shape)

    tm_shape = jax.ShapeDtypeStruct((t, HEADS * HEAD_W), F32)
    gate_shape = jax.ShapeDtypeStruct((HEADS, nc, 1, CHUNK), F32)
    return pl.pallas_call(
        body, name="mlstm_bwd", grid=(nb,),
        in_specs=[tm, tm, tm, gate, gate, cm, vec, vec, tm], out_specs=[tm, tm, tm, gate, gate],
        out_shape=[tm_shape, tm_shape, tm_shape, gate_shape, gate_shape],
        scratch_shapes=_ML_STATE,
        compiler_params=_cparams("arbitrary"),
    )(q, k, v, li, lf, cp, npv, mp, dhc)


@jax.custom_vjp
def _bdot_diag(x, w):
    b = w.shape[1]
    return jnp.concatenate([_raw_dot(x[:, :b], w[0], "nn"), _raw_dot(x[:, b:], w[1], "nn")], axis=1)


def _bdot_diag_fwd(x, w):
    return _bdot_diag(x, w), (x, w)


def _bdot_diag_bwd(res, ct):
    x, w = res
    b = w.shape[1]
    dx = jnp.concatenate([_raw_dot(ct[:, :b], w[0], "nt"), _raw_dot(ct[:, b:], w[1], "nt")], axis=1)
    dw = jnp.stack([_raw_dot(x[:, :b], ct[:, :b], "tn"), _raw_dot(x[:, b:], ct[:, b:], "tn")])
    return dx.astype(x.dtype), dw.astype(w.dtype)


_bdot_diag.defvjp(_bdot_diag_fwd, _bdot_diag_bwd)


def _ml_pre(s0, s1, s2, s3, cw0, cw1, cw2, cw3, cb, wq, wk, wv, wiq, wik, wiv, bif):
    pre = cb + cw0 * s0 + cw1 * s1 + cw2 * s2 + cw3 * s3
    xc = pre * _sigmoid(pre)
    q = _bdot_diag(xc, wq)
    k = _bdot_diag(xc, wk)
    v = _bdot_diag(s3, wv)
    gates = _bdot(q, wiq, "nn") + _bdot(k, wik, "nn") + _bdot(v, wiv, "nn") + bif
    lane = lax.broadcasted_iota(jnp.int32, gates.shape, 1)
    gl = jnp.where(lane < HEADS, gates, _log_sigmoid(gates))
    return xc, q, k, v, gl


def _delayed(xs_ref, x_ref, halo_ref, r):
    xs_ref[0:HALO, :] = halo_ref[...]
    xs_ref[HALO:HALO + r, :] = x_ref[...]
    return [xs_ref[pl.ds(HALO - (CONV_K - 1) + j, r), :] for j in range(CONV_K)]


def _full_spec(shape):
    return pl.BlockSpec(shape, lambda i, nd=len(shape): (0,) * nd)


def _ml_pre_fwd(x_m, x_pad, params, tile=256, deps=()):
    t, w = x_m.shape
    r = min(tile, t)

    def body(*refs):
        x_ref, halo_ref = refs[:2]
        p = [ref[...] for ref in refs[2:2 + len(params)]]
        outs = refs[2 + len(params) + len(deps):-1]
        res = _ml_pre(*_delayed(refs[-1], x_ref, halo_ref, r), *p)
        for ref, val in zip(outs, res):
            ref[...] = val

    row = pl.BlockSpec((r, w), lambda i: (i, 0))
    return pl.pallas_call(
        body, name="ml_pre_fwd", grid=(t // r,),
        in_specs=[row, pl.BlockSpec((HALO, w), lambda i: (i * (r // HALO), 0))] + [_full_spec(p.shape) for p in params]
        + [ANY] * len(deps),
        out_specs=[row] * 4 + [pl.BlockSpec((r, LANES), lambda i: (i, 0))],
        out_shape=[jax.ShapeDtypeStruct((t, w), F32)] * 4 + [jax.ShapeDtypeStruct((t, LANES), F32)],
        scratch_shapes=[pltpu.VMEM((r + HALO, w), F32)],
        compiler_params=_cparams("arbitrary"),
    )(x_m, x_pad, *params, *deps)


def _ml_pre_bwd(x_m, x_pad, params, cts, tile=256):
    t, w = x_m.shape
    r = min(tile, t)
    nt = t // r
    n_p = len(params)

    def body(*refs):
        x_ref, halo_ref = refs[:2]
        p = [ref[...] for ref in refs[2:2 + n_p]]
        ct = [ref[...] for ref in refs[2 + n_p:7 + n_p]]
        dx_ref = refs[7 + n_p]
        dp_refs = refs[8 + n_p:8 + 2 * n_p]
        xs_ref, ds_ref, carry_ref = refs[8 + 2 * n_p:]
        step = pl.program_id(0)

        @pl.when(step == 0)
        def _():
            ds_ref[...] = jnp.zeros_like(ds_ref)
            carry_ref[...] = jnp.zeros_like(carry_ref)

        _, vjp = jax.vjp(_ml_pre, *_delayed(xs_ref, x_ref, halo_ref, r), *p)
        grads = vjp(tuple(ct))
        for j in range(CONV_K):
            ds_ref[j, HALO:HALO + r, :] = grads[j]
        lead = HALO + CONV_K - 1
        d_tile = sum(ds_ref[j, pl.ds(lead - j, r), :] for j in range(CONV_K))
        d_halo = sum(ds_ref[j, pl.ds(CONV_K - 1 - j, HALO), :] for j in range(CONV_K))
        dx_ref[...] = jnp.concatenate([d_tile[:r - HALO], d_tile[r - HALO:] + carry_ref[...]], axis=0).astype(dx_ref.dtype)
        carry_ref[...] = d_halo
        _accumulate(step, dp_refs, grads[CONV_K:])

    row = pl.BlockSpec((r, w), lambda i: (nt - 1 - i, 0))
    return pl.pallas_call(
        body, name="ml_pre_bwd", grid=(nt,),
        in_specs=[row, pl.BlockSpec((HALO, w), lambda i: ((nt - 1 - i) * (r // HALO), 0))] + [_full_spec(p.shape) for p in params]
        + [row] * 4 + [pl.BlockSpec((r, LANES), lambda i: (nt - 1 - i, 0))],
        out_specs=[row] + [_full_spec(p.shape) for p in params],
        out_shape=[jax.ShapeDtypeStruct((t, w), BF16)] + [jax.ShapeDtypeStruct(p.shape, F32) for p in params],
        scratch_shapes=[pltpu.VMEM((r + HALO, w), F32), pltpu.VMEM((CONV_K, r + 2 * HALO, w), F32), pltpu.VMEM((HALO, w), F32)],
        compiler_params=_cparams("arbitrary"),
    )(x_m, x_pad, *params, *cts)


def _per_head(fn, row_vals, head_params, shared_params=()):
    return [fn(*[a[:, hs] for a in row_vals], *[p[:, hs] for p in head_params], *shared_params) for hs in _head_slices(HEAD_W)]


def _gla_out(o, g, gn):
    return _rms(o, gn) * (g * _sigmoid(g))


def _ml_out(hc, op, xc, g, sk):
    hcell = hc * _sigmoid(op)
    mu = jnp.mean(hcell, axis=-1, keepdims=True)
    d = hcell - mu
    var = jnp.mean(d * d, axis=-1, keepdims=True)
    return d * lax.rsqrt(var + EPS) * g + sk * xc


def _log_decay(al, w, b):
    return _log_sigmoid(_bdot(al, w, "nn") + b) * (1.0 / GLA_GATE_NORM)


def _merge(ga, gb, ya, yb):
    ga, gb, ya, yb = (a.astype(F32) for a in (ga, gb, ya, yb))
    return _sigmoid(ga) * ya + _sigmoid(gb) * yb


def _post_mix(x, z, gpm, gpl):
    x1 = x + _rms(z, gpm)
    return x1, _rms(x1, gpl)


def _loss_rows(x1, dn, tgt, g):
    e = x1 + _rms(dn, g) - tgt
    return 0.5 * jnp.sum(jnp.mean(e * e, axis=-1, keepdims=True), axis=0, keepdims=True)


def _lin(p):
    return 4 * p[0] + 2 * p[1] + p[2]


def _me():
    return lax.axis_index("x"), lax.axis_index("y"), lax.axis_index("c")


def _flip(p, k):
    return tuple((1 - v) if (k >> (2 - i)) & 1 else v for i, v in enumerate(p))


ANY = pl.BlockSpec(memory_space=pl.ANY)


HBM = pl.BlockSpec(memory_space=pltpu.HBM)
SEM = pl.BlockSpec(memory_space=pltpu.SEMAPHORE)
DATAFLOW = pltpu.SideEffectType.DATAFLOW_SIDE_EFFECTING


SIBLING = 1
OTHER_CHIPS = (2, 4, 6)


def _peer_copies(kinds, srcs, lands, send_sems, recv_sems):
    me = _me()
    copies = []
    for a, (kind, src, land) in enumerate(zip(kinds, srcs, lands)):
        masks = {"gather": range(1, N_DEV), "exchange": range(1, N_DEV), "gather_chips": (SIBLING, *OTHER_CHIPS),
                 "gather_pass": OTHER_CHIPS}[kind]
        for k in masks:
            peer = _flip(me, k)
            if kind == "gather_pass":
                block = land.at[_lin(peer)]
                src_ref, dst_ref, target = block, block, _flip(me, SIBLING)
            else:
                src_ref, dst_ref, target = (src.at[_lin(peer)] if kind == "exchange" else src), land.at[_lin(me)], peer
            copies.append(pltpu.make_async_remote_copy(
                src_ref=src_ref, dst_ref=dst_ref, send_sem=send_sems.at[a * 7 + k - 1], recv_sem=recv_sems.at[a * 7 + k - 1],
                device_id=target, device_id_type=MESH))
    return copies


def _copies_start(kind, srcs, name, after=None, lands=None):
    n = len(srcs)
    extra = [] if after is None else [after]
    kind = [kind] * n if isinstance(kind, str) else list(kind)
    land_shapes = [(s.shape if k == "exchange" else (N_DEV, *s.shape)) for k, s in zip(kind, srcs)]
    lands = [lax.empty(ls, s.dtype) for ls, s in zip(land_shapes, srcs)] if lands is None else lands

    def body(*refs):
        sems = refs[2 * n + len(extra):]
        for cp in _peer_copies(kind, refs[:n], refs[n:2 * n], sems[0], sems[1]):
            cp.start()
        refs[-1][...] = jnp.zeros_like(refs[-1])

    def hbm(a):
        return pltpu.with_memory_space_constraint(a, pltpu.HBM)

    out = pl.pallas_call(
        body, name=name,
        out_shape=(pltpu.SemaphoreType.DMA((7 * n,)), pltpu.SemaphoreType.DMA((7 * n,)),
                   *[pltpu.HBM(s.shape, s.dtype) for s in srcs],
                   *[pltpu.HBM(ls, s.dtype) for ls, s in zip(land_shapes, srcs)],
                   jax.ShapeDtypeStruct((8, LANES), F32)),
        in_specs=[HBM] * (2 * n) + [ANY] * len(extra),
        out_specs=(SEM, SEM, *[HBM] * (2 * n), pl.BlockSpec(memory_space=pltpu.VMEM)),
        input_output_aliases={i: 2 + i for i in range(2 * n)},
        compiler_params=pltpu.CompilerParams(has_side_effects=DATAFLOW),
    )(*[hbm(s) for s in srcs], *[hbm(a) for a in lands], *extra)
    return (kind, n, out[:-1]), out[-1]


def _copies_wait(state, after, name):
    kind, n, (send_sems, recv_sems, *thru) = state
    after = list(after) if isinstance(after, (list, tuple)) else [after]

    def body(*refs):
        for cp in _peer_copies(kind, refs[:n], refs[n:2 * n], refs[2 * n], refs[2 * n + 1]):
            cp.wait_send()
            cp.wait_recv()

    out = pl.pallas_call(
        body, name=name,
        out_shape=tuple(pltpu.HBM(t.shape, t.dtype) for t in thru),
        in_specs=[HBM] * (2 * n) + [SEM, SEM] + [ANY] * len(after), out_specs=tuple([HBM] * (2 * n)),
        input_output_aliases={i: i for i in range(2 * n)},
        compiler_params=pltpu.CompilerParams(has_side_effects=DATAFLOW),
    )(*thru, send_sems, recv_sems, *after)
    return out[:n], out[n:]


def _adamw(w, g, m, v):
    m2 = ADAM_B1 * m + (1.0 - ADAM_B1) * g
    v2 = ADAM_B2 * v + (1.0 - ADAM_B2) * (g * g)
    m_hat = m2 / (1.0 - ADAM_B1 ** ADAM_STEP)
    v_hat = v2 / (1.0 - ADAM_B2 ** ADAM_STEP)
    delta = -ADAM_LR * (m_hat / (jnp.sqrt(v_hat) + ADAM_EPS) + ADAM_WD * w)
    return delta, m2, v2


def _sum_adamw(lands, parts, me_idx, w, m, v, name, tile=256):
    r, c = w.shape
    nchunks = len(lands)
    tr = min(tile, r // nchunks)
    per_chunk = r // nchunks // tr
    per = 1 + N_DEV

    def body(me_ref, *refs):
        w_ref, m_ref, v_ref, g_ref, d_ref, m2_ref, v2_ref = refs[nchunks * per:]
        for k in range(nchunks):
            own_ref, slots = refs[k * per], refs[k * per + 1:(k + 1) * per]

            @pl.when(pl.program_id(0) // per_chunk == k)
            def _(own_ref=own_ref, slots=slots):
                own = own_ref[...].astype(F32)
                g = None
                for s in range(N_DEV):
                    term = jnp.where(me_ref[0] == s, own, slots[s][...].astype(F32))
                    g = term if g is None else g + term
                d, m2, v2 = _adamw(w_ref[...], g, m_ref[...], v_ref[...])
                g_ref[...] = g
                d_ref[...] = d
                m2_ref[...] = m2
                v2_ref[...] = v2

    def chunk_specs(k):
        def tile_of(i):
            return jnp.clip(i - k * per_chunk, 0, per_chunk - 1)

        def slot_spec(s):
            return pl.BlockSpec((None, tr, c), lambda i, me: (jnp.where(me[0] == s, (s + 1) % N_DEV, s), tile_of(i), 0))
        return [pl.BlockSpec((None, tr, c), lambda i, me: (me[0], tile_of(i), 0))] + [slot_spec(s) for s in range(N_DEV)]

    row = pl.BlockSpec((tr, c), lambda i, me: (i, 0))
    operands = [a for land, part in zip(lands, parts) for a in (part, *[land] * N_DEV)]
    return pl.pallas_call(
        body, name=name,
        grid_spec=pltpu.PrefetchScalarGridSpec(
            num_scalar_prefetch=1, grid=(r // tr,),
            in_specs=[s for k in range(nchunks) for s in chunk_specs(k)] + [row] * 3,
            out_specs=[row] * 4),
        out_shape=[jax.ShapeDtypeStruct((r, c), F32)] * 4,
        compiler_params=_cparams("parallel"),
    )(me_idx, *operands, w, m, v)


def _small_update(name, me_idx, kinds, lands, owns, ws, ms, vs, sums=()):
    n = len(ws)
    lands, owns = list(lands) + [s[0] for s in sums], list(owns) + [s[1] for s in sums]
    kinds = list(kinds) + ["gather"] * len(sums)
    nl = len(lands)

    def summed(me, land_ref, own):
        g = None
        for s in range(N_DEV):
            term = jnp.where(me == s, own, land_ref[s])
            g = term if g is None else g + term
        return g

    def body(me_ref, *refs):
        land_refs, own_refs = refs[:nl], refs[nl:2 * nl]
        w_refs, m_refs, v_refs = (refs[2 * nl + i * n:2 * nl + (i + 1) * n] for i in range(3))
        outs = refs[2 * nl + 3 * n:]
        me = me_ref[0]
        for i in range(n):
            g = summed(me, land_refs[i], own_refs[i][...])
            d, m2, v2 = _adamw(w_refs[i][...], g, m_refs[i][...], v_refs[i][...])
            for ref, val in zip(outs[4 * i:4 * i + 4], (g, d, m2, v2)):
                ref[...] = val
        for i in range(n, nl):
            outs[4 * n + i - n][...] = summed(me, land_refs[i], own_refs[i][...])

    def whole(shape):
        return pl.BlockSpec(shape, lambda i, me, nd=len(shape): (0,) * nd)

    def own_spec(kind, own):
        if kind == "gather":
            return whole(own.shape)
        return pl.BlockSpec((None, *own.shape[1:]), lambda i, me: (me[0], 0, 0))

    shapes = [w.shape for w in ws]
    out_shapes = [s for s in shapes for _ in range(4)] + [s[1].shape for s in sums]
    return pl.pallas_call(
        body, name=name,
        grid_spec=pltpu.PrefetchScalarGridSpec(
            num_scalar_prefetch=1, grid=(1,),
            in_specs=[whole(a.shape) for a in lands] + [own_spec(k, o) for k, o in zip(kinds, owns)]
            + [whole(s) for s in shapes] * 3,
            out_specs=[whole(s) for s in out_shapes]),
        out_shape=[jax.ShapeDtypeStruct(s, F32) for s in out_shapes],
        compiler_params=_cparams("arbitrary"),
    )(me_idx, *lands, *owns, *ws, *ms, *vs)


def _small_view(n, a):
    if a.ndim == 1:
        return a.reshape(1, -1)
    if a.ndim == 3:
        return a.transpose(1, 2, 0).reshape(QKV_BLOCK * QKV_BLOCK, -1)
    return a.T if n == "w_if" else a


def _small_unview(n, a, shape):
    if len(shape) == 1:
        return a.reshape(shape)
    if len(shape) == 3:
        return a.reshape(QKV_BLOCK, QKV_BLOCK, -1).transpose(2, 0, 1)
    return a.T if n == "w_if" else a


def _small_shards(n, g):
    if n == "w_if":
        return g.reshape(N_DEV, -1, g.shape[1]).transpose(0, 2, 1)
    return g.reshape(g.shape[0], N_DEV, -1).transpose(1, 0, 2)


def _small_unshard(n, s):
    if n == "w_if":
        return s.transpose(0, 2, 1).reshape(-1, s.shape[1])
    return s.transpose(1, 0, 2).reshape(s.shape[1], -1)


def _to_hm(a, d):
    t = a.shape[0]
    return a.reshape(t, HEADS, d).transpose(1, 0, 2)


def _from_hm(a):
    h, t, d = a.shape
    return a.transpose(1, 0, 2).reshape(t, h * d)


def _gate_rows(g):
    t = g.shape[0]
    return g.T.reshape(HEADS, t // CHUNK, 1, CHUNK)


def _gate_cols(g):
    h, nc, _, c = g.shape
    return g.reshape(h, nc * c).T


def _blockdiag_dense(w):
    n = w.shape[0] * QKV_BLOCK // 2
    tiled = jnp.tile(w.reshape(2, n, QKV_BLOCK), (1, 1, n // QKV_BLOCK))
    r = lax.broadcasted_iota(jnp.int32, (2, n, n), 1)
    c = lax.broadcasted_iota(jnp.int32, (2, n, n), 2)
    return jnp.where(r // QKV_BLOCK == c // QKV_BLOCK, tiled, 0.0)


def _blockdiag_blocks(dense):
    _, n, _ = dense[0].shape
    k = len(dense)

    def body(*refs):
        r = lax.broadcasted_iota(jnp.int32, (n, n), 0)
        c = lax.broadcasted_iota(jnp.int32, (n, n), 1)
        fr = lax.broadcasted_iota(jnp.int32, (n, LANES), 0)
        fc = lax.broadcasted_iota(jnp.int32, (n, LANES), 1)
        fold = ((fr & (QKV_BLOCK - 1)) == fc).astype(BF16)
        for i in range(k):
            for half in range(2):
                kept = jnp.where((r >> 2) == (c >> 2), refs[i][half], 0.0)
                refs[k + i][half] = sum(lax.dot_general(t, fold, _dims("nn", 2), preferred_element_type=F32)
                                        for t in _split3(kept))

    out = pl.pallas_call(body, name="blockdiag_blocks", out_shape=[jax.ShapeDtypeStruct((2, n, LANES), F32)] * k)(*dense)
    return [o[:, :, 0:QKV_BLOCK].reshape(2 * n // QKV_BLOCK, QKV_BLOCK, QKV_BLOCK) for o in out]


def _col_blocks(w):
    k, n = w.shape
    return w.reshape(k, N_DEV, n // N_DEV).transpose(1, 0, 2)


def _from_col_blocks(g):
    d, k, n = g.shape
    return g.transpose(1, 0, 2).reshape(k, d * n)


def _first_norm(x, g):
    return _rowwise("pre_mix_norm", lambda xv, gv: ((_rms(xv, gv),), ()), [x], [g], [(x.shape[1], BF16)])[0]


def _local_step(x, h, tgt, weight, ws, prefetch, pass_on, on_grads, on_small):
    t, d = x.shape
    g1 = ws["g_pre_mix"]

    def dep(token):
        return () if token is None else (token,)

    w_in = weight("w_in", x)
    fetch_mix = prefetch(("w_pa", "w_pb", "w_o"), w_in)

    n_in = w_in.shape[2]

    offs = [0]
    for s in IN_SPLITS:
        offs.append(offs[-1] + s)

    def proj_in_fwd(hv, w):
        proj = jnp.concatenate([_raw_dot(hv, w[j], "nn") for j in range(N_DEV)], axis=1)
        parts = [proj[:, offs[i]:offs[i + 1]] for i in range(len(IN_SPLITS))]
        parts[4] = jnp.concatenate([parts[4], jnp.zeros((parts[4].shape[0], LANES - LOWRANK), F32)], axis=1)
        return parts, ()

    widths = [LANES if s == LOWRANK else s for s in IN_SPLITS]
    q_a, k_a, v_a, g_a, a_low_p, x_m, o_pre, gate_a, gate_b = _rowwise(
        "proj_in", proj_in_fwd, [h], [w_in], [(wd, BF16 if i == 2 else F32) for i, wd in enumerate(widths)],
        deps=dep(fetch_mix))

    w_a_up_p = jnp.pad(ws["w_a_up"], ((0, LANES - LOWRANK), (0, 0)))
    b_a_up = ws["b_a_up"]
    (la,) = _rowwise("gla_decay", lambda al, w, b: ((_log_decay(al, w, b),), ()), [a_low_p], [w_a_up_p, b_a_up],
                     [(HEADS * GLA_DK, F32)])
    fetch_up = prefetch(("w_up", "w_down"), la)
    q_hm, k_hm, la_hm = _to_hm(q_a, GLA_DK), _to_hm(k_a, GLA_DK), _to_hm(la, GLA_DK)
    o_gla, s_prev = _gla_fwd(q_hm, k_hm, v_a, la_hm, deps=dep(fetch_up))
    pass_mix = pass_on("w_pa", o_gla)
    gn = ws["g_gla_norm"]
    ml_w = HEADS * HEAD_W

    cw = ws["conv_w"]
    w_if_p = jnp.pad(ws["w_if"], ((0, 0), (0, LANES - 2 * HEADS)))
    pre_params = [cw[0:1], cw[1:2], cw[2:3], cw[3:4], ws["conv_b"],
                  _blockdiag_dense(ws["w_q_ml"]), _blockdiag_dense(ws["w_k_ml"]), _blockdiag_dense(ws["w_v_ml"]),
                  w_if_p[0:ml_w], w_if_p[ml_w:2 * ml_w], w_if_p[2 * ml_w:3 * ml_w],
                  jnp.pad(ws["b_if"], ((0, 0), (0, LANES - 2 * HEADS)))]
    x_pad = jnp.pad(x_m, ((HALO, 0), (0, 0)))
    xc, q_m, k_m, v_m, gl = _ml_pre_fwd(x_m, x_pad, pre_params, deps=dep(pass_mix))
    li, lf = _gate_rows(gl[:, 0:HEADS]), _gate_rows(gl[:, HEADS:2 * HEADS])
    hc, c_prev, n_prev, m_prev = _ml_fwd(q_m, k_m, v_m, li, lf)
    g_ml, skip = ws["g_ml_norm"], ws["ml_skip"]

    def proj_a_fwd(o, g, n_, w):
        ya = jnp.concatenate(_per_head(_gla_out, [o, g], [], [n_]), axis=1)
        return (ya, _raw_dot(ya, w, "nn")), ()

    ya_in, y_a = _rowwise("proj_a", proj_a_fwd, [o_gla, g_a], [gn, weight("w_pa", hc)], [(ml_w, BF16), (d, BF16)],
                          tile=512)

    def proj_b_fwd(a, b, c_, ga, gb, ya, g, s, w):
        hb = jnp.concatenate(_per_head(_ml_out, [a, b, c_], [g, s]), axis=1)
        yb = _raw_dot(hb, w, "nn")
        return (hb, yb, _merge(ga, gb, ya, yb)), ()

    h_b, y_b, merged = _rowwise("proj_b", proj_b_fwd, [hc, o_pre, xc, gate_a, gate_b, y_a], [g_ml, skip, weight("w_pb", hc)],
                                [(ml_w, BF16), (d, BF16), (d, BF16)], tile=512)

    gpm, gpl, gpo = ws["g_post_mix"], ws["g_pre_mlp"], ws["g_post_mlp"]

    def proj_o_fwd(mg, xv, w, a, b):
        zv = _raw_dot(mg, w, "nn")
        return (zv, *_post_mix(xv, zv, a, b)), ()

    pass_up = pass_on("w_up", merged)
    z, x1, h2 = _rowwise("proj_o", proj_o_fwd, [merged, x], [weight("w_o", merged), gpm, gpl],
                         [(d, F32), (d, F32), (d, BF16)], tile=512, deps=dep(pass_up))
    w_up = weight("w_up", h2)
    up, u = _mm(h2, w_up, "nn", (BF16, BF16), "mlp_up", tm=2048, shards="b",
                epilogue=lambda p: (p, jnp.square(jnp.maximum(p, 0.0))))

    def mlp_down_loss(uv, x1v, tgtv, w, g):
        dnv = _raw_dot(uv, w, "nn")
        loss, vjp = jax.vjp(lambda a, b, c_: _loss_rows(a, b, tgtv, c_), x1v, dnv, g)
        dx1, ddn, dg = vjp(jnp.ones((1, 1), F32))
        return (dx1, ddn), (jnp.broadcast_to(loss, (1, LANES)), dg)

    dx1_y, d_dn, loss, d_gpo = _rowwise("mlp_down", mlp_down_loss, [u, x1, tgt], [weight("w_down", u), gpo],
                                        [(d, F32), (d, BF16)], [((1, LANES), F32), ((1, d), F32)], tile=512)

    (d_up,) = _mm(d_dn, weight("w_down", u), "nt", (BF16,), "mlp_down_dx", extra=[up],
                  epilogue=lambda p, a: (p * (2.0 * jnp.maximum(a.astype(F32), 0.0)),))
    dw_down = _mm(u, d_dn, "tn", BF16, "mlp_down_dw", tm=512)
    dw_up = _mm_shard_cols(h2, [d_up], [d_up.shape[1]], w_up.shape[2], "mlp_up_dw", 0, d)
    sent_mlp = on_grads(dict(w_down=dw_down, w_up=dw_up))

    def mlp_up_dx(dup, xv, zv, dx1, w, a, b):
        _, vjp = jax.vjp(_post_mix, xv, zv, a, b)
        ns = w.shape[2]
        dh2 = sum(_raw_dot(dup[:, j * ns:(j + 1) * ns], w[j], "nt") for j in range(w.shape[0]))
        dx, dz, da, db = vjp((dx1, dh2))
        return (dx, dz), (da, db)

    dx_res, d_z, d_gpm, d_gpl = _rowwise("mlp_up_dx", mlp_up_dx, [d_up, x, z, dx1_y], [w_up, gpm, gpl],
                                         [(d, F32), (d, BF16)], [((1, d), F32), ((1, d), F32)], tile=512, deps=dep(sent_mlp))
    dw_o = _mm(merged, d_z, "tn", BF16, "proj_o_dw")

    def proj_o_dx(dz, ga, gb, ya, yb, w):
        return jax.vjp(_merge, ga, gb, ya, yb)[1](_raw_dot(dz, w, "nt")), ()

    d_ga, d_gb, d_ya, d_yb = _rowwise("proj_o_dx", proj_o_dx, [d_z, gate_a, gate_b, y_a, y_b], [weight("w_o", merged)],
                                      [(d, BF16)] * 4, tile=512)
    dw_pa = _mm(ya_in, d_ya, "tn", BF16, "proj_a_dw")
    dw_pb = _mm(h_b, d_yb, "tn", BF16, "proj_b_dw")
    sent_mix = on_grads(dict(w_o=dw_o, w_pa=dw_pa, w_pb=dw_pb))

    def proj_b_dx(dyb, a, b, c_, w, g, s):
        ct = _raw_dot(dyb, w, "nt")
        parts = []
        for hs in _head_slices(HEAD_W):
            _, vjp = jax.vjp(_ml_out, a[:, hs], b[:, hs], c_[:, hs], g[:, hs], s[:, hs])
            parts.append(vjp(ct[:, hs]))
        cat = lambda i: jnp.concatenate([p[i] for p in parts], axis=1)
        return (cat(0), cat(1), cat(2)), (cat(3), cat(4))

    d_hc, d_opre, d_xc, d_gml, d_skip = _rowwise("proj_b_dx", proj_b_dx, [d_yb, hc, o_pre, xc],
                                                 [weight("w_pb", hc), g_ml, skip], [(ml_w, F32), (ml_w, BF16), (ml_w, F32)],
                                                 [((1, ml_w), F32)] * 2,
                                                 tile=512, deps=dep(sent_mix))
    d_qm, d_km, d_vm, d_li, d_lf = _ml_bwd(q_m, k_m, v_m, li, lf, c_prev, n_prev, m_prev, d_hc)
    d_gl = jnp.concatenate([_gate_cols(d_li), _gate_cols(d_lf), jnp.zeros((t, LANES - 2 * HEADS), F32)], axis=1)
    pre_grads = _ml_pre_bwd(x_m, x_pad, pre_params, [d_xc, d_qm, d_km, d_vm, d_gl], tile=512)
    d_xm = pre_grads[0]
    d_cw = jnp.concatenate(pre_grads[1:5], axis=0)
    d_cb = pre_grads[5]
    d_wq, d_wk, d_wv = _blockdiag_blocks(pre_grads[6:9])
    d_wif = jnp.concatenate(pre_grads[9:12], axis=0)[:, 0:2 * HEADS]
    d_bif = pre_grads[12][:, 0:2 * HEADS]

    def proj_a_dx(dya, o, g, w, n_):
        ct = _raw_dot(dya, w, "nt")
        parts = []
        for hs in _head_slices(HEAD_W):
            _, vjp = jax.vjp(_gla_out, o[:, hs], g[:, hs], n_)
            parts.append(vjp(ct[:, hs]))
        cat = lambda i: jnp.concatenate([p[i] for p in parts], axis=1)
        return (cat(0), cat(1)), (sum(p[2] for p in parts),)

    d_o, d_g_a, d_gn = _rowwise("proj_a_dx", proj_a_dx, [d_ya, o_gla, g_a], [weight("w_pa", o_gla), gn],
                                [(ml_w, F32), (ml_w, BF16)],
                                [((1, HEAD_W), F32)], tile=512)
    dq_hm, dk_hm, d_va, dla_hm = _gla_bwd(q_hm, k_hm, v_a, la_hm, s_prev, d_o)

    def decay_bwd(al, ct, w, b):
        _, vjp = jax.vjp(_log_decay, al, w, b)
        dal, dw, db = vjp(ct)
        return (dal,), (dw, db)

    d_alow_p, d_wa_p, d_ba = _rowwise("gla_decay_bwd", decay_bwd, [a_low_p, _from_hm(dla_hm)], [w_a_up_p, b_a_up],
                                      [(LANES, BF16)], [(w_a_up_p.shape, F32), (b_a_up.shape, F32)])
    d_proj = [jnp.concatenate([_from_hm(dq_hm), _from_hm(dk_hm), d_va, d_g_a], axis=1), d_alow_p,
              jnp.concatenate([d_xm, d_opre, d_ga, d_gb], axis=1)]
    d_widths = [offs[4], LOWRANK, offs[9] - offs[5]]
    d_pieces = _shard_pieces(d_widths, n_in)
    small = dict(w_a_up=d_wa_p[0:LOWRANK], b_a_up=d_ba, g_gla_norm=d_gn, conv_w=d_cw, conv_b=d_cb,
                 w_q_ml=d_wq, w_k_ml=d_wk, w_v_ml=d_wv, w_if=d_wif, b_if=d_bif, ml_skip=d_skip, g_ml_norm=d_gml,
                 g_post_mix=d_gpm, g_pre_mlp=d_gpl, g_post_mlp=d_gpo)
    sent_small = on_small(small, loss)
    sent_in = sent_small
    for half in range(2):
        dw_half = _mm_shard_cols(h, d_proj, d_widths, n_in, "proj_in_dw_%d" % half, half, d // 2, deps=dep(sent_in))
        sent_in = on_grads({"w_in#%d" % half: dw_half})

    def proj_in_dx(dp_a, dp_low, dp_b, xv, dres, w, g):
        dh = 0.0
        for s in range(N_DEV):
            for i, c_in, c_w, wd in d_pieces[s]:
                src = (dp_a, dp_low, dp_b)[i]
                cols = src.shape[1] - c_in if wd < LANES else wd
                dh = dh + _raw_dot(src[:, c_in:c_in + cols], w[s][:, c_w:c_w + cols], "nt")
        _, vjp = jax.vjp(_rms, xv, g)
        dx, dg = vjp(dh)
        return (dx + dres,), (dg,)

    grad_x, d_g1 = _rowwise("proj_in_dx", proj_in_dx, [*d_proj, x, dx_res], [w_in, g1], [(d, F32)], [((1, d), F32)],
                            deps=dep(sent_in))
    return grad_x, on_small(dict(g_pre_mix=d_g1), None)


BIG = ("w_in", "w_pa", "w_pb", "w_o", "w_up", "w_down")
BIG_COL_SHARDED = ("w_in", "w_pa", "w_pb", "w_up")
SMALL_SHARDED = ("w_a_up", "conv_w", "w_if")
SMALL = ("g_pre_mix", "w_a_up", "b_a_up", "g_gla_norm", "conv_w", "conv_b", "w_q_ml", "w_k_ml", "w_v_ml", "w_if", "b_if",
         "ml_skip", "g_ml_norm", "g_post_mix", "g_pre_mlp", "g_post_mlp")
WEIGHTS = ("g_pre_mix", "w_in", "w_a_up", "b_a_up", "g_gla_norm", "conv_w", "conv_b", "w_q_ml", "w_k_ml", "w_v_ml", "w_if", "b_if",
           "ml_skip", "g_ml_norm", "w_pa", "w_pb", "w_o", "g_post_mix", "g_pre_mlp", "w_up", "w_down", "g_post_mlp")


def kernel(x, g_pre_mix, w_in, w_a_up, b_a_up, g_gla_norm, conv_w, conv_b, w_q_ml, w_k_ml, w_v_ml, w_if, b_if, ml_skip, g_ml_norm, w_pa, w_pb, w_o, g_post_mix, g_pre_mlp, w_up, w_down, g_post_mlp, loss_target, m_g_pre_mix, m_w_in, m_w_a_up, m_b_a_up, m_g_gla_norm, m_conv_w, m_conv_b, m_w_q_ml, m_w_k_ml, m_w_v_ml, m_w_if, m_b_if, m_ml_skip, m_g_ml_norm, m_w_pa, m_w_pb, m_w_o, m_g_post_mix, m_g_pre_mlp, m_w_up, m_w_down, m_g_post_mlp, v_g_pre_mix, v_w_in, v_w_a_up, v_b_a_up, v_g_gla_norm, v_conv_w, v_conv_b, v_w_q_ml, v_w_k_ml, v_w_v_ml, v_w_if, v_b_if, v_ml_skip, v_g_ml_norm, v_w_pa, v_w_pb, v_w_o, v_g_post_mix, v_g_pre_mlp, v_w_up, v_w_down, v_g_post_mlp):
    args = dict(locals())
    w = {n: args[n][0] for n in WEIGHTS}
    m = {n: args["m_" + n][0] for n in WEIGHTS}
    v = {n: args["v_" + n][0] for n in WEIGHTS}

    me_lin = _lin(_me())
    me_idx = jnp.reshape(me_lin, (1,)).astype(jnp.int32)

    def full_weight(n, g):
        if n in ("w_in", "w_up"):
            return g
        return _from_col_blocks(g) if n in BIG_COL_SHARDED else g.reshape(-1, g.shape[-1])

    def grad_parts(n, g):
        if n.partition("#")[0] in ("w_in", "w_up"):
            return g
        return (_col_blocks(g) if n in BIG_COL_SHARDED else g.reshape(N_DEV, -1, g.shape[-1])).astype(BF16)

    sharded_names = tuple(SMALL_SHARDED)
    narrow = {n: w[n].astype(BF16) for n in BIG}
    ready, pending, passing = {}, {}, {}
    first_state, _ = _copies_start(["gather"] * len(sharded_names) + ["gather_chips"],
                                   [_small_view(n, w[n]) for n in sharded_names] + [narrow["w_in"]], "allgather_start_first")

    def prefetch(group, after):
        state, token = _copies_start("gather_chips", [narrow[n] for n in group], "allgather_start_" + group[0], after)
        for n in group:
            pending[n] = (group, state)
        return token

    def pass_on(n, after):
        group, state = pending[n]
        shards, lands = _copies_wait(state, after, "allgather_wait_" + group[0])
        state, token = _copies_start("gather_pass", shards, "allgather_pass_" + group[0], lands=lands)
        for gn in group:
            passing[gn] = (group, state)
        return token

    def weight(n, after):
        if n not in ready:
            group, state = passing[n]
            shards, lands = _copies_wait(state, after, "allgather_passed_" + group[0])
            for gn, shard, land in zip(group, shards, lands):
                ready[gn] = full_weight(gn, lax.dynamic_update_slice(land, shard[None], (me_lin, 0, 0)))
        return ready[n]

    h = _first_norm(x[0], w["g_pre_mix"].reshape(1, -1))
    first_own, first_lands = _copies_wait(first_state, [h] + [narrow[n] for n in BIG if n != "w_in"], "allgather_wait_first")
    state, _ = _copies_start("gather_pass", first_own[-1:], "allgather_pass_w_in", lands=first_lands[-1:])
    passing["w_in"] = (("w_in",), state)
    ws = {n: (w[n].reshape(1, -1) if w[n].ndim == 1 else w[n]) for n in SMALL if n not in SMALL_SHARDED}
    for n, own, land in zip(sharded_names, first_own, first_lands):
        ws[n] = _small_unshard(n, lax.dynamic_update_slice(land, own[None], (me_lin, 0, 0)))

    sent = []

    def on_grads(grads):
        names = tuple(grads)
        state, token = _copies_start("exchange", [grad_parts(n, grads[n]) for n in names],
                                     "exchange_start_" + names[0].replace("#", "_"))
        sent.append((names, state))
        return token

    small_sent = []

    def on_small(small, loss):
        names = tuple(small)
        kinds = ["exchange" if n in SMALL_SHARDED else "gather" for n in names]
        srcs = [_small_shards(n, small[n]) if n in SMALL_SHARDED else _small_view(n, small[n]) for n in names]
        extra = [] if loss is None else [loss]
        state, token = _copies_start(kinds + ["gather"] * len(extra), srcs + extra, "allgather_start_small_" + names[0])
        small_sent.append((names, kinds, state))
        return token

    grad_x, last_token = _local_step(x[0], h, loss_target[0], weight, ws, prefetch, pass_on, on_grads, on_small)

    out = {}

    chunks = {}

    def finish(names, state, after):
        parts, lands = _copies_wait(state, after, "exchange_wait_" + names[0].replace("#", "_"))
        for name, part, land in zip(names, parts, lands):
            n, _, chunk = name.partition("#")
            chunks.setdefault(n, []).append((land, part))
            if chunk in ("", "1"):
                got_lands, got_parts = zip(*chunks[n])
                out[n] = _sum_adamw(got_lands, got_parts, me_idx, w[n], m[n], v[n], "adamw_" + n)

    def finish_small(names, kinds, state, after):
        own, lands = _copies_wait(state, after, "allgather_wait_small_" + names[0])
        k = len(names)
        upd = _small_update("adamw_small_" + names[0], me_idx, kinds, lands[:k], own[:k],
                            *[[_small_view(n, d[n]) for n in names] for d in (w, m, v)], sums=list(zip(lands[k:], own[k:])))
        for i, n in enumerate(names):
            out[n] = tuple(_small_unview(n, a, w[n].shape) for a in upd[4 * i:4 * i + 4])
        return upd[4 * k:]

    (loss_sum,) = finish_small(*small_sent[0], [grad_x, last_token])
    for names, state in sent[:-2]:
        finish(names, state, [grad_x, last_token])
    finish(*sent[-2], [loss_sum] + [out[n][1] for n in BIG if n in out])
    finish(*sent[-1], [loss_sum])
    finish_small(*small_sent[1], [out["w_in"][1]])

    shaped = lambda a, n: a.reshape(args[n].shape)
    return (loss_sum[0, 0], grad_x[None],
            *[shaped(out[n][0], n) for n in WEIGHTS], *[shaped(out[n][1], n) for n in WEIGHTS],
            *[shaped(out[n][2], n) for n in WEIGHTS], *[shaped(out[n][3], n) for n in WEIGHTS])
```

```python
import functools

import jax
import jax.numpy as jnp
from jax import lax
from jax.experimental import pallas as pl
from jax.experimental.pallas import tpu as pltpu

F32 = jnp.float32
BF16 = jnp.bfloat16
MESH = pl.DeviceIdType.MESH

N_DEV = 8
EPS = 1e-6
CHUNK = 64
CHUNKS_PER_STEP = 4
HEADS = 4
GLA_DK = 64
HEAD_W = 128
GLA_GATE_NORM = 16.0
LOWRANK = 16
CONV_K = 4
QKV_BLOCK = 4
LANES = 128
HALO = 8
IN_SPLITS = (256, 256, 512, 512, 16, 512, 512, 1024, 1024)

ADAM_LR = 0.001
ADAM_B1 = 0.9
ADAM_B2 = 0.999
ADAM_EPS = 1e-08
ADAM_WD = 0.01
ADAM_STEP = 10

VMEM_LIMIT = 56 * 1024 * 1024


def _cparams(*sem):
    return pltpu.CompilerParams(dimension_semantics=sem, vmem_limit_bytes=VMEM_LIMIT)


def _dims(mode, ndim):
    contract = {"nn": ((ndim - 1,), (ndim - 2,)), "nt": ((ndim - 1,), (ndim - 1,)), "tn": ((ndim - 2,), (ndim - 2,))}[mode]
    return contract, (((0,), (0,)) if ndim == 3 else ((), ()))


def _raw_dot(a, b, mode):
    return lax.dot_general(a.astype(BF16), b.astype(BF16), _dims(mode, a.ndim), preferred_element_type=F32)


@functools.partial(jax.custom_vjp, nondiff_argnums=(2,))
def _bdot(a, b, mode):
    return _raw_dot(a, b, mode)


def _bdot_fwd(a, b, mode):
    return _raw_dot(a, b, mode), (a, b)


def _bdot_bwd(mode, res, ct):
    a, b = res
    if mode == "nn":
        da, db = _raw_dot(ct, b, "nt"), _raw_dot(a, ct, "tn")
    elif mode == "nt":
        da, db = _raw_dot(ct, b, "nn"), _raw_dot(ct, a, "tn")
    else:
        da, db = _raw_dot(b, ct, "nt"), _raw_dot(a, ct, "nn")
    return da.astype(a.dtype), db.astype(b.dtype)


_bdot.defvjp(_bdot_fwd, _bdot_bwd)


def _split3(x):
    hi = x.astype(BF16)
    r1 = x - hi.astype(F32)
    mid = r1.astype(BF16)
    return hi, mid, (r1 - mid.astype(F32)).astype(BF16)


def _split_dot(tri, x):
    if x.ndim == 3:
        tri = jnp.broadcast_to(tri, (x.shape[0], *tri.shape))
    return sum(lax.dot_general(tri, t, _dims("nn", x.ndim), preferred_element_type=F32) for t in _split3(x))


def _tri(n, lower):
    r = lax.broadcasted_iota(jnp.int32, (n, n), 0)
    c = lax.broadcasted_iota(jnp.int32, (n, n), 1)
    return ((c <= r) if lower else (c >= r)).astype(BF16)


@jax.custom_vjp
def _cumsum_rows(x):
    return _split_dot(_tri(x.shape[-2], True), x)


def _cumsum_rows_fwd(x):
    return _cumsum_rows(x), None


def _cumsum_rows_bwd(_, ct):
    return (_split_dot(_tri(ct.shape[-2], False), ct),)


_cumsum_rows.defvjp(_cumsum_rows_fwd, _cumsum_rows_bwd)


def _abs(x):
    return jnp.where(x >= 0, x, -x)


def _sigmoid(x):
    return lax.logistic(x)


def _log_sigmoid(x):
    return jnp.minimum(x, 0.0) - jnp.log(1.0 + jnp.exp(-_abs(x)))


def _rms(x, g):
    return x * lax.rsqrt(jnp.mean(x * x, axis=-1, keepdims=True) + EPS) * g


def _head_slices(w):
    return [slice(h * w, (h + 1) * w) for h in range(HEADS)]


def _heads(ref, rows=slice(None)):
    return jnp.stack([ref[rows, hs] for hs in _head_slices(HEAD_W)])


def _put_heads(ref, val, rows=slice(None)):
    for h, hs in enumerate(_head_slices(HEAD_W)):
        ref[rows, hs] = val[h].astype(ref.dtype)


def _tile(dim, want):
    if dim <= want or dim % LANES:
        return dim
    t = want
    while dim % t:
        t -= LANES
    return t


def _mm(a, b, mode, out_dtype, name, tm=1024, tn=1024, tk=4096, epilogue=None, extra=(), deps=(), shards=None):
    if shards == "b":
        assert mode == "nn"
        ns = b.shape[2]
        (m, k), (k2, n) = a.shape, (b.shape[1], b.shape[0] * ns)
        tn = ns
    elif mode == "nn":
        (m, k), (k2, n) = a.shape, b.shape
    elif mode == "nt":
        (m, k), (n, k2) = a.shape, b.shape
    else:
        (k, m), (k2, n) = a.shape, b.shape
    assert k == k2, (name, a.shape, b.shape)
    tm, tn, tk = _tile(m, tm), _tile(n, tn), _tile(k, tk)
    nk = k // tk
    out_dtypes = out_dtype if epilogue else (out_dtype,)
    assert nk == 1 or (out_dtype == F32 and not epilogue), name
    n_in = 2 + len(extra)

    def body(*refs):
        p = _raw_dot(refs[0][...], refs[1][...], mode)
        if nk > 1:
            _accumulate(pl.program_id(2), [refs[n_in + len(deps)]], [p])
            return
        outs = epilogue(p, *[r[...] for r in refs[2:n_in]]) if epilogue else (p,)
        for ref, val in zip(refs[n_in + len(deps):], outs):
            ref[...] = val.astype(ref.dtype)

    a_spec = pl.BlockSpec((tk, tm), lambda i, j, kk: (kk, i)) if mode == "tn" else pl.BlockSpec((tm, tk), lambda i, j, kk: (i, kk))
    if shards == "b":
        b_spec = pl.BlockSpec((None, tk, tn), lambda i, j, kk: (j, kk, 0))
    elif mode == "nt":
        b_spec = pl.BlockSpec((tn, tk), lambda i, j, kk: (j, kk))
    else:
        b_spec = pl.BlockSpec((tk, tn), lambda i, j, kk: (kk, j))
    o_spec = pl.BlockSpec((tm, tn), lambda i, j, kk: (i, j))
    res = pl.pallas_call(
        body, name=name, grid=(m // tm, n // tn, nk),
        in_specs=[a_spec, b_spec] + [o_spec] * len(extra) + [ANY] * len(deps), out_specs=[o_spec] * len(out_dtypes),
        out_shape=[jax.ShapeDtypeStruct((m, n), dt) for dt in out_dtypes],
        compiler_params=_cparams("parallel", "parallel", "arbitrary"),
    )(a, b, *extra, *deps)
    return res if epilogue else res[0]


def _shard_pieces(widths, n):
    bounds = [0]
    for wd in widths:
        bounds.append(bounds[-1] + wd)
    assert bounds[-1] == N_DEV * n
    return [[(i, max(s * n, b) - b, max(s * n, b) - s * n, min((s + 1) * n, b + wd) - max(s * n, b))
             for i, (b, wd) in enumerate(zip(bounds, widths)) if b < (s + 1) * n and b + wd > s * n]
            for s in range(N_DEV)]


def _mm_shard_cols(a, bs, widths, n, name, row_tile, tm, deps=()):
    t = a.shape[0]
    nb = len(bs)
    pieces = _shard_pieces(widths, n)

    def body(a_ref, *rest):
        b_refs = rest[:nb]
        o_ref, at_ref = rest[nb + len(deps):]
        j = pl.program_id(0)

        @pl.when(j == 0)
        def _():
            at_ref[...] = a_ref[...].astype(BF16).T

        for s in range(N_DEV):
            @pl.when(j == s)
            def _(s=s):
                for i, c_in, c_out, wd in pieces[s]:
                    cols = min(_round_up(wd, LANES), bs[i].shape[1] - c_in) if wd < LANES else wd
                    p = _raw_dot(at_ref[...], b_refs[i][:, c_in:c_in + cols], "nn")
                    o_ref[:, c_out:c_out + wd] = p[:, 0:wd].astype(BF16)

    return pl.pallas_call(
        body, name=name, grid=(N_DEV,),
        in_specs=[pl.BlockSpec((t, tm), lambda j: (0, row_tile))]
        + [pl.BlockSpec(b.shape, lambda j: (0, 0), pipeline_mode=pl.Buffered(1)) for b in bs] + [ANY] * len(deps),
        out_specs=pl.BlockSpec((None, tm, n), lambda j: (j, 0, 0)),
        out_shape=jax.ShapeDtypeStruct((N_DEV, tm, n), BF16),
        scratch_shapes=[pltpu.VMEM((tm, t), BF16)],
        compiler_params=_cparams("arbitrary"),
    )(a, *bs, *deps)


def _round_up(v, m):
    return -(-v // m) * m


def _rowwise(name, fn, rows, params, out_rows, out_accs=(), tile=256, deps=()):
    t = rows[0].shape[0]
    r = min(tile, t)
    assert t % r == 0
    n_in, n_or = len(rows) + len(params), len(out_rows)
    n_all = n_in + len(deps)
    params = list(params) + list(deps)

    def body(*refs):
        vals = [ref[...] for ref in refs[:n_in]]
        outs = refs[n_all:]
        ro, ao = fn(*vals)
        for ref, v in zip(outs[:n_or], ro):
            ref[...] = v.astype(ref.dtype)
        if out_accs:
            _accumulate(pl.program_id(0), outs[n_or:], ao)

    def full(shape):
        return pl.BlockSpec(shape, lambda i, nd=len(shape): (0,) * nd)

    return pl.pallas_call(
        body, name=name, grid=(t // r,),
        in_specs=[pl.BlockSpec((r, a.shape[1]), lambda i: (i, 0)) for a in rows] + [full(p.shape) for p in params],
        out_specs=[pl.BlockSpec((r, w), lambda i: (i, 0)) for w, _ in out_rows] + [full(s) for s, _ in out_accs],
        out_shape=[jax.ShapeDtypeStruct((t, w), dt) for w, dt in out_rows] + [jax.ShapeDtypeStruct(s, dt) for s, dt in out_accs],
        compiler_params=_cparams("arbitrary"),
    )(*rows, *params)


def _accumulate(step, refs, vals):
    for ref, v in zip(refs, vals):
        @pl.when(step == 0)
        def _(ref=ref, v=v):
            ref[...] = v.astype(ref.dtype)

        @pl.when(step > 0)
        def _(ref=ref, v=v):
            ref[...] += v.astype(ref.dtype)


def _gla_chunk(q, k, v, la, st):
    c = q.shape[-2]
    row = lax.broadcasted_iota(jnp.int32, (c, c), 0)
    col = lax.broadcasted_iota(jnp.int32, (c, c), 1)
    cum = _cumsum_rows(la)
    cl = jnp.sum(la, axis=-2, keepdims=True)
    ep = jnp.exp(cum)
    en = jnp.exp(-cum)
    qs = q * (GLA_DK ** -0.5)
    qp = qs * ep
    a_f = _bdot(qp, k * en, "nt")
    a_b = _bdot(qs * en, k * ep, "nt")
    sc = jnp.where(row >= col, a_f, a_b)
    o = _bdot(sc, v, "nn") + _bdot(qp, st, "nt")
    kd = k * jnp.exp(cl - cum)
    st_new = st * jnp.exp(cl) + _bdot(v, kd, "tn")
    return o, st_new


def _gla_specs(nc, rev):
    nb = nc // CHUNKS_PER_STEP
    rows = CHUNKS_PER_STEP * CHUNK

    def blk(n):
        return (nb - 1 - n) if rev else n
    hm = pl.BlockSpec((HEADS, rows, GLA_DK), lambda n: (0, blk(n), 0))
    tm = pl.BlockSpec((rows, HEADS * HEAD_W), lambda n: (blk(n), 0))
    st = pl.BlockSpec((HEADS, CHUNKS_PER_STEP, HEAD_W, GLA_DK), lambda n: (0, blk(n), 0, 0))
    return nb, hm, tm, st


def _chunk_rows(c):
    return slice(c * CHUNK, (c + 1) * CHUNK)


def _gla_fwd(q, k, v, la, deps=()):
    t = v.shape[0]
    nc = t // CHUNK
    nb, hm, tm, st = _gla_specs(nc, False)

    def body(q_ref, k_ref, v_ref, la_ref, *rest):
        o_ref, sp_ref, st_ref = rest[len(deps):]

        @pl.when(pl.program_id(0) == 0)
        def _():
            st_ref[...] = jnp.zeros_like(st_ref)

        s = st_ref[...]
        for c in range(CHUNKS_PER_STEP):
            r = _chunk_rows(c)
            sp_ref[:, c] = s
            o, s = _gla_chunk(q_ref[:, r], k_ref[:, r], _heads(v_ref, r), la_ref[:, r], s)
            _put_heads(o_ref, o, r)
        st_ref[...] = s

    return pl.pallas_call(
        body, name="gla_fwd", grid=(nb,),
        in_specs=[hm, hm, tm, hm] + [ANY] * len(deps), out_specs=[tm, st],
        out_shape=[jax.ShapeDtypeStruct((t, HEADS * HEAD_W), F32), jax.ShapeDtypeStruct((HEADS, nc, HEAD_W, GLA_DK), F32)],
        scratch_shapes=[pltpu.VMEM((HEADS, HEAD_W, GLA_DK), F32)],
        compiler_params=_cparams("arbitrary"),
    )(q, k, v, la, *deps)


def _gla_bwd(q, k, v, la, sp, do):
    t = v.shape[0]
    nc = t // CHUNK
    nb, hm, tm, st = _gla_specs(nc, True)

    def body(q_ref, k_ref, v_ref, la_ref, sp_ref, do_ref, dq_ref, dk_ref, dv_ref, dla_ref, ds_ref):
        @pl.when(pl.program_id(0) == 0)
        def _():
            ds_ref[...] = jnp.zeros_like(ds_ref)

        ds = ds_ref[...]
        for c in reversed(range(CHUNKS_PER_STEP)):
            r = _chunk_rows(c)
            _, vjp = jax.vjp(_gla_chunk, q_ref[:, r], k_ref[:, r], _heads(v_ref, r), la_ref[:, r], sp_ref[:, c])
            dq, dk, dv, dla, ds = vjp((_heads(do_ref, r), ds))
            dq_ref[:, r] = dq.astype(dq_ref.dtype)
            dk_ref[:, r] = dk.astype(dk_ref.dtype)
            _put_heads(dv_ref, dv, r)
            dla_ref[:, r] = dla
        ds_ref[...] = ds

    hm_shape = jax.ShapeDtypeStruct((HEADS, t, GLA_DK), BF16)
    return pl.pallas_call(
        body, name="gla_bwd", grid=(nb,),
        in_specs=[hm, hm, tm, hm, st, tm], out_specs=[hm, hm, tm, hm],
        out_shape=[hm_shape, hm_shape, jax.ShapeDtypeStruct((t, HEADS * HEAD_W), BF16),
                   jax.ShapeDtypeStruct((HEADS, t, GLA_DK), F32)],
        scratch_shapes=[pltpu.VMEM((HEADS, HEAD_W, GLA_DK), F32)],
        compiler_params=_cparams("arbitrary"),
    )(q, k, v, la, sp, do)


def _ml_chunk(q, k, v, li_r, lf_r, cm, nv, m):
    c = q.shape[-2]
    row = lax.broadcasted_iota(jnp.int32, (c, c), 0)
    col = lax.broadcasted_iota(jnp.int32, (c, c), 1)
    eye = (row == col).astype(F32)
    li_c = jnp.sum(eye * li_r, axis=-1, keepdims=True)
    lf_c = jnp.sum(eye * lf_r, axis=-1, keepdims=True)
    fc_c = jnp.sum((col <= row).astype(F32) * lf_r, axis=-1, keepdims=True)
    fc_r = jnp.sum((row <= col).astype(F32) * lf_c, axis=-2, keepdims=True)
    f_last = jnp.sum(lf_r, axis=-1, keepdims=True)
    kc = k * (HEAD_W ** -0.5)
    a_c = f_last - fc_c + li_c
    m_loc = jnp.max(a_c, axis=-2, keepdims=True)
    kw = kc * jnp.exp(a_c - m_loc)
    c_chunk = _bdot(kw, v, "tn")
    n_chunk = jnp.sum(kw, axis=-2, keepdims=True)
    m_new = jnp.maximum(f_last + m, m_loc)
    sp = jnp.exp(f_last + m - m_new)
    sl = jnp.exp(m_loc - m_new)
    cm_new = sp * cm + sl * c_chunk
    nv_new = sp * nv + sl * n_chunk
    log_d = li_r - _abs(fc_c - fc_r)
    g_inter = fc_c + m
    m_t = jnp.maximum(g_inter, jnp.max(log_d, axis=-1, keepdims=True))
    s = _bdot(q, kc, "nt") * jnp.exp(log_d - m_t)
    sc = jnp.exp(g_inter - m_t)
    num = _bdot(s, v, "nn") + sc * _bdot(q, cm, "nn")
    den = jnp.sum(s, axis=-1, keepdims=True) + sc * jnp.sum(q * nv, axis=-1, keepdims=True)
    den = jnp.maximum(_abs(den), jnp.exp(-m_t))
    return num / den, cm_new, nv_new, m_new


def _ml_specs(nc, rev):
    nb = nc // CHUNKS_PER_STEP

    def blk(n):
        return (nb - 1 - n) if rev else n
    tm = pl.BlockSpec((CHUNKS_PER_STEP * CHUNK, HEADS * HEAD_W), lambda n: (blk(n), 0))
    gate = pl.BlockSpec((HEADS, CHUNKS_PER_STEP, 1, CHUNK), lambda n: (0, blk(n), 0, 0))
    cm = pl.BlockSpec((HEADS, CHUNKS_PER_STEP, HEAD_W, HEAD_W), lambda n: (0, blk(n), 0, 0))
    vec = pl.BlockSpec((HEADS, CHUNKS_PER_STEP, 1, HEAD_W), lambda n: (0, blk(n), 0, 0))
    return nb, tm, gate, cm, vec


_ML_STATE = [pltpu.VMEM((HEADS, HEAD_W, HEAD_W), F32), pltpu.VMEM((HEADS, 1, HEAD_W), F32), pltpu.VMEM((HEADS, 1, HEAD_W), F32)]


def _ml_fwd(q, k, v, li, lf):
    t = q.shape[0]
    nc = t // CHUNK
    nb, tm, gate, cm, vec = _ml_specs(nc, False)

    def body(q_ref, k_ref, v_ref, li_ref, lf_ref, hc_ref, cp_ref, np_ref, mp_ref, c_ref, n_ref, m_ref):
        @pl.when(pl.program_id(0) == 0)
        def _():
            c_ref[...] = jnp.zeros_like(c_ref)
            n_ref[...] = jnp.zeros_like(n_ref)
            m_ref[...] = jnp.zeros_like(m_ref)

        cs, ns, ms = c_ref[...], n_ref[...], m_ref[...][:, :, 0:1]
        for c in range(CHUNKS_PER_STEP):
            r = _chunk_rows(c)
            cp_ref[:, c] = cs
            np_ref[:, c] = ns
            mp_ref[:, c] = jnp.broadcast_to(ms, m_ref.shape)
            hc, cs, ns, ms = _ml_chunk(_heads(q_ref, r), _heads(k_ref, r), _heads(v_ref, r), li_ref[:, c], lf_ref[:, c],
                                       cs, ns, ms)
            _put_heads(hc_ref, hc, r)
        c_ref[...] = cs
        n_ref[...] = ns
        m_ref[...] = jnp.broadcast_to(ms, m_ref.shape)

    return pl.pallas_call(
        body, name="mlstm_fwd", grid=(nb,),
        in_specs=[tm, tm, tm, gate, gate], out_specs=[tm, cm, vec, vec],
        out_shape=[jax.ShapeDtypeStruct((t, HEADS * HEAD_W), F32), jax.ShapeDtypeStruct((HEADS, nc, HEAD_W, HEAD_W), F32),
                   jax.ShapeDtypeStruct((HEADS, nc, 1, HEAD_W), F32), jax.ShapeDtypeStruct((HEADS, nc, 1, HEAD_W), F32)],
        scratch_shapes=_ML_STATE,
        compiler_params=_cparams("arbitrary"),
    )(q, k, v, li, lf)


def _ml_bwd(q, k, v, li, lf, cp, npv, mp, dhc):
    t = q.shape[0]
    nc = t // CHUNK
    nb, tm, gate, cm, vec = _ml_specs(nc, True)

    def body(q_ref, k_ref, v_ref, li_ref, lf_ref, cp_ref, np_ref, mp_ref, dhc_ref,
             dq_ref, dk_ref, dv_ref, dli_ref, dlf_ref, dc_ref, dn_ref, dm_ref):
        @pl.when(pl.program_id(0) == 0)
        def _():
            dc_ref[...] = jnp.zeros_like(dc_ref)
            dn_ref[...] = jnp.zeros_like(dn_ref)
            dm_ref[...] = jnp.zeros_like(dm_ref)

        dc, dn, dm = dc_ref[...], dn_ref[...], dm_ref[...][:, :, 0:1]
        for c in reversed(range(CHUNKS_PER_STEP)):
            r = _chunk_rows(c)
            _, vjp = jax.vjp(_ml_chunk, _heads(q_ref, r), _heads(k_ref, r), _heads(v_ref, r), li_ref[:, c], lf_ref[:, c],
                             cp_ref[:, c], np_ref[:, c], mp_ref[:, c][:, :, 0:1])
            dq, dk, dv, dli, dlf, dc, dn, dm = vjp((_heads(dhc_ref, r), dc, dn, dm))
            _put_heads(dq_ref, dq, r)
            _put_heads(dk_ref, dk, r)
            _put_heads(dv_ref, dv, r)
            dli_ref[:, c] = dli
            dlf_ref[:, c] = dlf
        dc_ref[...] = dc
        dn_ref[...] = dn
        dm_ref[...] = jnp.broadcast_to(dm, dm_ref.shape)

    tm_shape = jax.ShapeDtypeStruct((t, HEADS * HEAD_W), F32)
    gate_shape = jax.ShapeDtypeStruct((HEADS, nc, 1, CHUNK), F32)
    return pl.pallas_call(
        body, name="mlstm_bwd", grid=(nb,),
        in_specs=[tm, tm, tm, gate, gate, cm, vec, vec, tm], out_specs=[tm, tm, tm, gate, gate],
        out_shape=[tm_shape, tm_shape, tm_shape, gate_shape, gate_shape],
        scratch_shapes=_ML_STATE,
        compiler_params=_cparams("arbitrary"),
    )(q, k, v, li, lf, cp, npv, mp, dhc)


@jax.custom_vjp
def _bdot_diag(x, w):
    b = w.shape[1]
    return jnp.concatenate([_raw_dot(x[:, :b], w[0], "nn"), _raw_dot(x[:, b:], w[1], "nn")], axis=1)


def _bdot_diag_fwd(x, w):
    return _bdot_diag(x, w), (x, w)


def _bdot_diag_bwd(res, ct):
    x, w = res
    b = w.shape[1]
    dx = jnp.concatenate([_raw_dot(ct[:, :b], w[0], "nt"), _raw_dot(ct[:, b:], w[1], "nt")], axis=1)
    dw = jnp.stack([_raw_dot(x[:, :b], ct[:, :b], "tn"), _raw_dot(x[:, b:], ct[:, b:], "tn")])
    return dx.astype(x.dtype), dw.astype(w.dtype)


_bdot_diag.defvjp(_bdot_diag_fwd, _bdot_diag_bwd)


def _ml_pre(s0, s1, s2, s3, cw0, cw1, cw2, cw3, cb, wq, wk, wv, wiq, wik, wiv, bif):
    pre = cb + cw0 * s0 + cw1 * s1 + cw2 * s2 + cw3 * s3
    xc = pre * _sigmoid(pre)
    q = _bdot_diag(xc, wq)
    k = _bdot_diag(xc, wk)
    v = _bdot_diag(s3, wv)
    gates = _bdot(q, wiq, "nn") + _bdot(k, wik, "nn") + _bdot(v, wiv, "nn") + bif
    lane = lax.broadcasted_iota(jnp.int32, gates.shape, 1)
    gl = jnp.where(lane < HEADS, gates, _log_sigmoid(gates))
    return xc, q, k, v, gl


def _delayed(xs_ref, x_ref, halo_ref, r):
    xs_ref[0:HALO, :] = halo_ref[...]
    xs_ref[HALO:HALO + r, :] = x_ref[...]
    return [xs_ref[pl.ds(HALO - (CONV_K - 1) + j, r), :] for j in range(CONV_K)]


def _full_spec(shape):
    return pl.BlockSpec(shape, lambda i, nd=len(shape): (0,) * nd)


def _ml_pre_fwd(x_m, x_pad, params, tile=256, deps=()):
    t, w = x_m.shape
    r = min(tile, t)

    def body(*refs):
        x_ref, halo_ref = refs[:2]
        p = [ref[...] for ref in refs[2:2 + len(params)]]
        outs = refs[2 + len(params) + len(deps):-1]
        res = _ml_pre(*_delayed(refs[-1], x_ref, halo_ref, r), *p)
        for ref, val in zip(outs, res):
            ref[...] = val

    row = pl.BlockSpec((r, w), lambda i: (i, 0))
    return pl.pallas_call(
        body, name="ml_pre_fwd", grid=(t // r,),
        in_specs=[row, pl.BlockSpec((HALO, w), lambda i: (i * (r // HALO), 0))] + [_full_spec(p.shape) for p in params]
        + [ANY] * len(deps),
        out_specs=[row] * 4 + [pl.BlockSpec((r, LANES), lambda i: (i, 0))],
        out_shape=[jax.ShapeDtypeStruct((t, w), F32)] * 4 + [jax.ShapeDtypeStruct((t, LANES), F32)],
        scratch_shapes=[pltpu.VMEM((r + HALO, w), F32)],
        compiler_params=_cparams("arbitrary"),
    )(x_m, x_pad, *params, *deps)


def _ml_pre_bwd(x_m, x_pad, params, cts, tile=256):
    t, w = x_m.shape
    r = min(tile, t)
    nt = t // r
    n_p = len(params)

    def body(*refs):
        x_ref, halo_ref = refs[:2]
        p = [ref[...] for ref in refs[2:2 + n_p]]
        ct = [ref[...] for ref in refs[2 + n_p:7 + n_p]]
        dx_ref = refs[7 + n_p]
        dp_refs = refs[8 + n_p:8 + 2 * n_p]
        xs_ref, ds_ref, carry_ref = refs[8 + 2 * n_p:]
        step = pl.program_id(0)

        @pl.when(step == 0)
        def _():
            ds_ref[...] = jnp.zeros_like(ds_ref)
            carry_ref[...] = jnp.zeros_like(carry_ref)

        _, vjp = jax.vjp(_ml_pre, *_delayed(xs_ref, x_ref, halo_ref, r), *p)
        grads = vjp(tuple(ct))
        for j in range(CONV_K):
            ds_ref[j, HALO:HALO + r, :] = grads[j]
        lead = HALO + CONV_K - 1
        d_tile = sum(ds_ref[j, pl.ds(lead - j, r), :] for j in range(CONV_K))
        d_halo = sum(ds_ref[j, pl.ds(CONV_K - 1 - j, HALO), :] for j in range(CONV_K))
        dx_ref[...] = jnp.concatenate([d_tile[:r - HALO], d_tile[r - HALO:] + carry_ref[...]], axis=0).astype(dx_ref.dtype)
        carry_ref[...] = d_halo
        _accumulate(step, dp_refs, grads[CONV_K:])

    row = pl.BlockSpec((r, w), lambda i: (nt - 1 - i, 0))
    return pl.pallas_call(
        body, name="ml_pre_bwd", grid=(nt,),
        in_specs=[row, pl.BlockSpec((HALO, w), lambda i: ((nt - 1 - i) * (r // HALO), 0))] + [_full_spec(p.shape) for p in params]
        + [row] * 4 + [pl.BlockSpec((r, LANES), lambda i: (nt - 1 - i, 0))],
        out_specs=[row] + [_full_spec(p.shape) for p in params],
        out_shape=[jax.ShapeDtypeStruct((t, w), BF16)] + [jax.ShapeDtypeStruct(p.shape, F32) for p in params],
        scratch_shapes=[pltpu.VMEM((r + HALO, w), F32), pltpu.VMEM((CONV_K, r + 2 * HALO, w), F32), pltpu.VMEM((HALO, w), F32)],
        compiler_params=_cparams("arbitrary"),
    )(x_m, x_pad, *params, *cts)


def _per_head(fn, row_vals, head_params, shared_params=()):
    return [fn(*[a[:, hs] for a in row_vals], *[p[:, hs] for p in head_params], *shared_params) for hs in _head_slices(HEAD_W)]


def _gla_out(o, g, gn):
    return _rms(o, gn) * (g * _sigmoid(g))


def _ml_out(hc, op, xc, g, sk):
    hcell = hc * _sigmoid(op)
    mu = jnp.mean(hcell, axis=-1, keepdims=True)
    d = hcell - mu
    var = jnp.mean(d * d, axis=-1, keepdims=True)
    return d * lax.rsqrt(var + EPS) * g + sk * xc


def _log_decay(al, w, b):
    return _log_sigmoid(_bdot(al, w, "nn") + b) * (1.0 / GLA_GATE_NORM)


def _merge(ga, gb, ya, yb):
    ga, gb, ya, yb = (a.astype(F32) for a in (ga, gb, ya, yb))
    return _sigmoid(ga) * ya + _sigmoid(gb) * yb


def _post_mix(x, z, gpm, gpl):
    x1 = x + _rms(z, gpm)
    return x1, _rms(x1, gpl)


def _loss_rows(x1, dn, tgt, g):
    e = x1 + _rms(dn, g) - tgt
    return 0.5 * jnp.sum(jnp.mean(e * e, axis=-1, keepdims=True), axis=0, keepdims=True)


def _lin(p):
    return 4 * p[0] + 2 * p[1] + p[2]


def _me():
    return lax.axis_index("x"), lax.axis_index("y"), lax.axis_index("c")


def _flip(p, k):
    return tuple((1 - v) if (k >> (2 - i)) & 1 else v for i, v in enumerate(p))


ANY = pl.BlockSpec(memory_space=pl.ANY)


HBM = pl.BlockSpec(memory_space=pltpu.HBM)
SEM = pl.BlockSpec(memory_space=pltpu.SEMAPHORE)
DATAFLOW = pltpu.SideEffectType.DATAFLOW_SIDE_EFFECTING


SIBLING = 1
OTHER_CHIPS = (2, 4, 6)


def _peer_copies(kinds, srcs, lands, send_sems, recv_sems):
    me = _me()
    copies = []
    for a, (kind, src, land) in enumerate(zip(kinds, srcs, lands)):
        masks = {"gather": range(1, N_DEV), "exchange": range(1, N_DEV), "gather_chips": (SIBLING, *OTHER_CHIPS),
                 "gather_pass": OTHER_CHIPS}[kind]
        for k in masks:
            peer = _flip(me, k)
            if kind == "gather_pass":
                block = land.at[_lin(peer)]
                src_ref, dst_ref, target = block, block, _flip(me, SIBLING)
            else:
                src_ref, dst_ref, target = (src.at[_lin(peer)] if kind == "exchange" else src), land.at[_lin(me)], peer
            copies.append(pltpu.make_async_remote_copy(
                src_ref=src_ref, dst_ref=dst_ref, send_sem=send_sems.at[a * 7 + k - 1], recv_sem=recv_sems.at[a * 7 + k - 1],
                device_id=target, device_id_type=MESH))
    return copies


def _copies_start(kind, srcs, name, after=None, lands=None):
    n = len(srcs)
    extra = [] if after is None else [after]
    kind = [kind] * n if isinstance(kind, str) else list(kind)
    land_shapes = [(s.shape if k == "exchange" else (N_DEV, *s.shape)) for k, s in zip(kind, srcs)]
    lands = [lax.empty(ls, s.dtype) for ls, s in zip(land_shapes, srcs)] if lands is None else lands

    def body(*refs):
        sems = refs[2 * n + len(extra):]
        for cp in _peer_copies(kind, refs[:n], refs[n:2 * n], sems[0], sems[1]):
            cp.start()
        refs[-1][...] = jnp.zeros_like(refs[-1])

    def hbm(a):
        return pltpu.with_memory_space_constraint(a, pltpu.HBM)

    out = pl.pallas_call(
        body, name=name,
        out_shape=(pltpu.SemaphoreType.DMA((7 * n,)), pltpu.SemaphoreType.DMA((7 * n,)),
                   *[pltpu.HBM(s.shape, s.dtype) for s in srcs],
                   *[pltpu.HBM(ls, s.dtype) for ls, s in zip(land_shapes, srcs)],
                   jax.ShapeDtypeStruct((8, LANES), F32)),
        in_specs=[HBM] * (2 * n) + [ANY] * len(extra),
        out_specs=(SEM, SEM, *[HBM] * (2 * n), pl.BlockSpec(memory_space=pltpu.VMEM)),
        input_output_aliases={i: 2 + i for i in range(2 * n)},
        compiler_params=pltpu.CompilerParams(has_side_effects=DATAFLOW),
    )(*[hbm(s) for s in srcs], *[hbm(a) for a in lands], *extra)
    return (kind, n, out[:-1]), out[-1]


def _copies_wait(state, after, name):
    kind, n, (send_sems, recv_sems, *thru) = state
    after = list(after) if isinstance(after, (list, tuple)) else [after]

    def body(*refs):
        for cp in _peer_copies(kind, refs[:n], refs[n:2 * n], refs[2 * n], refs[2 * n + 1]):
            cp.wait_send()
            cp.wait_recv()

    out = pl.pallas_call(
        body, name=name,
        out_shape=tuple(pltpu.HBM(t.shape, t.dtype) for t in thru),
        in_specs=[HBM] * (2 * n) + [SEM, SEM] + [ANY] * len(after), out_specs=tuple([HBM] * (2 * n)),
        input_output_aliases={i: i for i in range(2 * n)},
        compiler_params=pltpu.CompilerParams(has_side_effects=DATAFLOW),
    )(*thru, send_sems, recv_sems, *after)
    return out[:n], out[n:]


def _adamw(w, g, m, v):
    m2 = ADAM_B1 * m + (1.0 - ADAM_B1) * g
    v2 = ADAM_B2 * v + (1.0 - ADAM_B2) * (g * g)
    m_hat = m2 / (1.0 - ADAM_B1 ** ADAM_STEP)
    v_hat = v2 / (1.0 - ADAM_B2 ** ADAM_STEP)
    delta = -ADAM_LR * (m_hat / (jnp.sqrt(v_hat) + ADAM_EPS) + ADAM_WD * w)
    return delta, m2, v2


def _sum_adamw(lands, parts, me_idx, w, m, v, name, tile=256):
    r, c = w.shape
    nchunks = len(lands)
    tr = min(tile, r // nchunks)
    per_chunk = r // nchunks // tr
    per = 1 + N_DEV

    def body(me_ref, *refs):
        w_ref, m_ref, v_ref, g_ref, d_ref, m2_ref, v2_ref = refs[nchunks * per:]
        for k in range(nchunks):
            own_ref, slots = refs[k * per], refs[k * per + 1:(k + 1) * per]

            @pl.when(pl.program_id(0) // per_chunk == k)
            def _(own_ref=own_ref, slots=slots):
                own = own_ref[...].astype(F32)
                g = None
                for s in range(N_DEV):
                    term = jnp.where(me_ref[0] == s, own, slots[s][...].astype(F32))
                    g = term if g is None else g + term
                d, m2, v2 = _adamw(w_ref[...], g, m_ref[...], v_ref[...])
                g_ref[...] = g
                d_ref[...] = d
                m2_ref[...] = m2
                v2_ref[...] = v2

    def chunk_specs(k):
        def tile_of(i):
            return jnp.clip(i - k * per_chunk, 0, per_chunk - 1)

        def slot_spec(s):
            return pl.BlockSpec((None, tr, c), lambda i, me: (jnp.where(me[0] == s, (s + 1) % N_DEV, s), tile_of(i), 0))
        return [pl.BlockSpec((None, tr, c), lambda i, me: (me[0], tile_of(i), 0))] + [slot_spec(s) for s in range(N_DEV)]

    row = pl.BlockSpec((tr, c), lambda i, me: (i, 0))
    operands = [a for land, part in zip(lands, parts) for a in (part, *[land] * N_DEV)]
    return pl.pallas_call(
        body, name=name,
        grid_spec=pltpu.PrefetchScalarGridSpec(
            num_scalar_prefetch=1, grid=(r // tr,),
            in_specs=[s for k in range(nchunks) for s in chunk_specs(k)] + [row] * 3,
            out_specs=[row] * 4),
        out_shape=[jax.ShapeDtypeStruct((r, c), F32)] * 4,
        compiler_params=_cparams("parallel"),
    )(me_idx, *operands, w, m, v)


def _small_update(name, me_idx, kinds, lands, owns, ws, ms, vs, sums=()):
    n = len(ws)
    lands, owns = list(lands) + [s[0] for s in sums], list(owns) + [s[1] for s in sums]
    kinds = list(kinds) + ["gather"] * len(sums)
    nl = len(lands)

    def summed(me, land_ref, own):
        g = None
        for s in range(N_DEV):
            term = jnp.where(me == s, own, land_ref[s])
            g = term if g is None else g + term
        return g

    def body(me_ref, *refs):
        land_refs, own_refs = refs[:nl], refs[nl:2 * nl]
        w_refs, m_refs, v_refs = (refs[2 * nl + i * n:2 * nl + (i + 1) * n] for i in range(3))
        outs = refs[2 * nl + 3 * n:]
        me = me_ref[0]
        for i in range(n):
            g = summed(me, land_refs[i], own_refs[i][...])
            d, m2, v2 = _adamw(w_refs[i][...], g, m_refs[i][...], v_refs[i][...])
            for ref, val in zip(outs[4 * i:4 * i + 4], (g, d, m2, v2)):
                ref[...] = val
        for i in range(n, nl):
            outs[4 * n + i - n][...] = summed(me, land_refs[i], own_refs[i][...])

    def whole(shape):
        return pl.BlockSpec(shape, lambda i, me, nd=len(shape): (0,) * nd)

    def own_spec(kind, own):
        if kind == "gather":
            return whole(own.shape)
        return pl.BlockSpec((None, *own.shape[1:]), lambda i, me: (me[0], 0, 0))

    shapes = [w.shape for w in ws]
    out_shapes = [s for s in shapes for _ in range(4)] + [s[1].shape for s in sums]
    return pl.pallas_call(
        body, name=name,
        grid_spec=pltpu.PrefetchScalarGridSpec(
            num_scalar_prefetch=1, grid=(1,),
            in_specs=[whole(a.shape) for a in lands] + [own_spec(k, o) for k, o in zip(kinds, owns)]
            + [whole(s) for s in shapes] * 3,
            out_specs=[whole(s) for s in out_shapes]),
        out_shape=[jax.ShapeDtypeStruct(s, F32) for s in out_shapes],
        compiler_params=_cparams("arbitrary"),
    )(me_idx, *lands, *owns, *ws, *ms, *vs)


def _small_view(n, a):
    if a.ndim == 1:
        return a.reshape(1, -1)
    if a.ndim == 3:
        return a.transpose(1, 2, 0).reshape(QKV_BLOCK * QKV_BLOCK, -1)
    return a.T if n == "w_if" else a


def _small_unview(n, a, shape):
    if len(shape) == 1:
        return a.reshape(shape)
    if len(shape) == 3:
        return a.reshape(QKV_BLOCK, QKV_BLOCK, -1).transpose(2, 0, 1)
    return a.T if n == "w_if" else a


def _small_shards(n, g):
    if n == "w_if":
        return g.reshape(N_DEV, -1, g.shape[1]).transpose(0, 2, 1)
    return g.reshape(g.shape[0], N_DEV, -1).transpose(1, 0, 2)


def _small_unshard(n, s):
    if n == "w_if":
        return s.transpose(0, 2, 1).reshape(-1, s.shape[1])
    return s.transpose(1, 0, 2).reshape(s.shape[1], -1)


def _to_hm(a, d):
    t = a.shape[0]
    return a.reshape(t, HEADS, d).transpose(1, 0, 2)


def _from_hm(a):
    h, t, d = a.shape
    return a.transpose(1, 0, 2).reshape(t, h * d)


def _gate_rows(g):
    t = g.shape[0]
    return g.T.reshape(HEADS, t // CHUNK, 1, CHUNK)


def _gate_cols(g):
    h, nc, _, c = g.shape
    return g.reshape(h, nc * c).T


def _blockdiag_dense(w):
    n = w.shape[0] * QKV_BLOCK // 2
    tiled = jnp.tile(w.reshape(2, n, QKV_BLOCK), (1, 1, n // QKV_BLOCK))
    r = lax.broadcasted_iota(jnp.int32, (2, n, n), 1)
    c = lax.broadcasted_iota(jnp.int32, (2, n, n), 2)
    return jnp.where(r // QKV_BLOCK == c // QKV_BLOCK, tiled, 0.0)


def _blockdiag_blocks(dense):
    _, n, _ = dense[0].shape
    k = len(dense)

    def body(*refs):
        r = lax.broadcasted_iota(jnp.int32, (n, n), 0)
        c = lax.broadcasted_iota(jnp.int32, (n, n), 1)
        fr = lax.broadcasted_iota(jnp.int32, (n, LANES), 0)
        fc = lax.broadcasted_iota(jnp.int32, (n, LANES), 1)
        fold = ((fr & (QKV_BLOCK - 1)) == fc).astype(BF16)
        for i in range(k):
            for half in range(2):
                kept = jnp.where((r >> 2) == (c >> 2), refs[i][half], 0.0)
                refs[k + i][half] = sum(lax.dot_general(t, fold, _dims("nn", 2), preferred_element_type=F32)
                                        for t in _split3(kept))

    out = pl.pallas_call(body, name="blockdiag_blocks", out_shape=[jax.ShapeDtypeStruct((2, n, LANES), F32)] * k)(*dense)
    return [o[:, :, 0:QKV_BLOCK].reshape(2 * n // QKV_BLOCK, QKV_BLOCK, QKV_BLOCK) for o in out]


def _col_blocks(w):
    k, n = w.shape
    return w.reshape(k, N_DEV, n // N_DEV).transpose(1, 0, 2)


def _from_col_blocks(g):
    d, k, n = g.shape
    return g.transpose(1, 0, 2).reshape(k, d * n)


def _first_norm(x, g):
    return _rowwise("pre_mix_norm", lambda xv, gv: ((_rms(xv, gv),), ()), [x], [g], [(x.shape[1], BF16)])[0]


def _local_step(x, h, tgt, weight, ws, prefetch, pass_on, on_grads, on_small):
    t, d = x.shape
    g1 = ws["g_pre_mix"]

    def dep(token):
        return () if token is None else (token,)

    w_in = weight("w_in", x)
    fetch_mix = prefetch(("w_pa", "w_pb", "w_o"), w_in)

    n_in = w_in.shape[2]

    offs = [0]
    for s in IN_SPLITS:
        offs.append(offs[-1] + s)

    w_a_up_p = jnp.pad(ws["w_a_up"], ((0, LANES - LOWRANK), (0, 0)))
    b_a_up = ws["b_a_up"]

    def proj_in_fwd(hv, w, wa, ba):
        proj = jnp.concatenate([_raw_dot(hv, w[j], "nn") for j in range(N_DEV)], axis=1)
        parts = [proj[:, offs[i]:offs[i + 1]] for i in range(len(IN_SPLITS))]
        parts[4] = jnp.concatenate([parts[4], jnp.zeros((parts[4].shape[0], LANES - LOWRANK), F32)], axis=1)
        return (*parts, _log_decay(parts[4], wa, ba)), ()

    widths = [LANES if s == LOWRANK else s for s in IN_SPLITS]
    q_a, k_a, v_a, g_a, a_low_p, x_m, o_pre, gate_a, gate_b, la = _rowwise(
        "proj_in", proj_in_fwd, [h], [w_in, w_a_up_p, b_a_up],
        [(wd, BF16 if i == 2 else F32) for i, wd in enumerate(widths)] + [(HEADS * GLA_DK, F32)],
        deps=dep(fetch_mix))

    fetch_up = prefetch(("w_up", "w_down"), la)
    q_hm, k_hm, la_hm = _to_hm(q_a, GLA_DK), _to_hm(k_a, GLA_DK), _to_hm(la, GLA_DK)
    o_gla, s_prev = _gla_fwd(q_hm, k_hm, v_a, la_hm, deps=dep(fetch_up))
    pass_mix = pass_on("w_pa", o_gla)
    gn = ws["g_gla_norm"]
    ml_w = HEADS * HEAD_W

    cw = ws["conv_w"]
    w_if_p = jnp.pad(ws["w_if"], ((0, 0), (0, LANES - 2 * HEADS)))
    pre_params = [cw[0:1], cw[1:2], cw[2:3], cw[3:4], ws["conv_b"],
                  _blockdiag_dense(ws["w_q_ml"]), _blockdiag_dense(ws["w_k_ml"]), _blockdiag_dense(ws["w_v_ml"]),
                  w_if_p[0:ml_w], w_if_p[ml_w:2 * ml_w], w_if_p[2 * ml_w:3 * ml_w],
                  jnp.pad(ws["b_if"], ((0, 0), (0, LANES - 2 * HEADS)))]
    x_pad = jnp.pad(x_m, ((HALO, 0), (0, 0)))
    xc, q_m, k_m, v_m, gl = _ml_pre_fwd(x_m, x_pad, pre_params, deps=dep(pass_mix))
    li, lf = _gate_rows(gl[:, 0:HEADS]), _gate_rows(gl[:, HEADS:2 * HEADS])
    hc, c_prev, n_prev, m_prev = _ml_fwd(q_m, k_m, v_m, li, lf)
    g_ml, skip = ws["g_ml_norm"], ws["ml_skip"]

    def proj_a_fwd(o, g, n_, w):
        ya = jnp.concatenate(_per_head(_gla_out, [o, g], [], [n_]), axis=1)
        return (ya, _raw_dot(ya, w, "nn")), ()

    ya_in, y_a = _rowwise("proj_a", proj_a_fwd, [o_gla, g_a], [gn, weight("w_pa", hc)], [(ml_w, BF16), (d, BF16)],
                          tile=512)

    def proj_b_fwd(a, b, c_, ga, gb, ya, g, s, w):
        hb = jnp.concatenate(_per_head(_ml_out, [a, b, c_], [g, s]), axis=1)
        yb = _raw_dot(hb, w, "nn")
        return (hb, yb, _merge(ga, gb, ya, yb)), ()

    h_b, y_b, merged = _rowwise("proj_b", proj_b_fwd, [hc, o_pre, xc, gate_a, gate_b, y_a], [g_ml, skip, weight("w_pb", hc)],
                                [(ml_w, BF16), (d, BF16), (d, BF16)], tile=512)

    gpm, gpl, gpo = ws["g_post_mix"], ws["g_pre_mlp"], ws["g_post_mlp"]

    def proj_o_fwd(mg, xv, w, a, b):
        zv = _raw_dot(mg, w, "nn")
        return (zv, *_post_mix(xv, zv, a, b)), ()

    pass_up = pass_on("w_up", merged)
    z, x1, h2 = _rowwise("proj_o", proj_o_fwd, [merged, x], [weight("w_o", merged), gpm, gpl],
                         [(d, F32), (d, F32), (d, BF16)], tile=512, deps=dep(pass_up))
    w_up = weight("w_up", h2)
    up, u = _mm(h2, w_up, "nn", (BF16, BF16), "mlp_up", tm=2048, shards="b",
                epilogue=lambda p: (p, jnp.square(jnp.maximum(p, 0.0))))

    def mlp_down_loss(uv, x1v, tgtv, w, g):
        dnv = _raw_dot(uv, w, "nn")
        loss, vjp = jax.vjp(lambda a, b, c_: _loss_rows(a, b, tgtv, c_), x1v, dnv, g)
        dx1, ddn, dg = vjp(jnp.ones((1, 1), F32))
        return (dx1, ddn), (jnp.broadcast_to(loss, (1, LANES)), dg)

    dx1_y, d_dn, loss, d_gpo = _rowwise("mlp_down", mlp_down_loss, [u, x1, tgt], [weight("w_down", u), gpo],
                                        [(d, F32), (d, BF16)], [((1, LANES), F32), ((1, d), F32)], tile=512)

    (d_up,) = _mm(d_dn, weight("w_down", u), "nt", (BF16,), "mlp_down_dx", extra=[up],
                  epilogue=lambda p, a: (p * (2.0 * jnp.maximum(a.astype(F32), 0.0)),))
    dw_down = _mm(u, d_dn, "tn", BF16, "mlp_down_dw", tm=512)
    dw_up = _mm_shard_cols(h2, [d_up], [d_up.shape[1]], w_up.shape[2], "mlp_up_dw", 0, d)
    sent_mlp = on_grads(dict(w_down=dw_down, w_up=dw_up))

    def mlp_up_dx(dup, xv, zv, dx1, w, a, b):
        _, vjp = jax.vjp(_post_mix, xv, zv, a, b)
        ns = w.shape[2]
        dh2 = sum(_raw_dot(dup[:, j * ns:(j + 1) * ns], w[j], "nt") for j in range(w.shape[0]))
        dx, dz, da, db = vjp((dx1, dh2))
        return (dx, dz), (da, db)

    dx_res, d_z, d_gpm, d_gpl = _rowwise("mlp_up_dx", mlp_up_dx, [d_up, x, z, dx1_y], [w_up, gpm, gpl],
                                         [(d, F32), (d, BF16)], [((1, d), F32), ((1, d), F32)], tile=512, deps=dep(sent_mlp))
    dw_o = _mm(merged, d_z, "tn", BF16, "proj_o_dw")

    def proj_o_dx(dz, ga, gb, ya, yb, w):
        return jax.vjp(_merge, ga, gb, ya, yb)[1](_raw_dot(dz, w, "nt")), ()

    d_ga, d_gb, d_ya, d_yb = _rowwise("proj_o_dx", proj_o_dx, [d_z, gate_a, gate_b, y_a, y_b], [weight("w_o", merged)],
                                      [(d, BF16)] * 4, tile=512)
    dw_pa = _mm(ya_in, d_ya, "tn", BF16, "proj_a_dw")
    dw_pb = _mm(h_b, d_yb, "tn", BF16, "proj_b_dw")
    sent_mix = on_grads(dict(w_o=dw_o, w_pa=dw_pa, w_pb=dw_pb))

    def proj_b_dx(dyb, a, b, c_, w, g, s):
        ct = _raw_dot(dyb, w, "nt")
        parts = []
        for hs in _head_slices(HEAD_W):
            _, vjp = jax.vjp(_ml_out, a[:, hs], b[:, hs], c_[:, hs], g[:, hs], s[:, hs])
            parts.append(vjp(ct[:, hs]))
        cat = lambda i: jnp.concatenate([p[i] for p in parts], axis=1)
        return (cat(0), cat(1), cat(2)), (cat(3), cat(4))

    d_hc, d_opre, d_xc, d_gml, d_skip = _rowwise("proj_b_dx", proj_b_dx, [d_yb, hc, o_pre, xc],
                                                 [weight("w_pb", hc), g_ml, skip], [(ml_w, F32), (ml_w, BF16), (ml_w, F32)],
                                                 [((1, ml_w), F32)] * 2,
                                                 tile=512, deps=dep(sent_mix))
    d_qm, d_km, d_vm, d_li, d_lf = _ml_bwd(q_m, k_m, v_m, li, lf, c_prev, n_prev, m_prev, d_hc)
    d_gl = jnp.concatenate([_gate_cols(d_li), _gate_cols(d_lf), jnp.zeros((t, LANES - 2 * HEADS), F32)], axis=1)
    pre_grads = _ml_pre_bwd(x_m, x_pad, pre_params, [d_xc, d_qm, d_km, d_vm, d_gl], tile=512)
    d_xm = pre_grads[0]
    d_cw = jnp.concatenate(pre_grads[1:5], axis=0)
    d_cb = pre_grads[5]
    d_wq, d_wk, d_wv = _blockdiag_blocks(pre_grads[6:9])
    d_wif = jnp.concatenate(pre_grads[9:12], axis=0)[:, 0:2 * HEADS]
    d_bif = pre_grads[12][:, 0:2 * HEADS]

    def proj_a_dx(dya, o, g, w, n_):
        ct = _raw_dot(dya, w, "nt")
        parts = []
        for hs in _head_slices(HEAD_W):
            _, vjp = jax.vjp(_gla_out, o[:, hs], g[:, hs], n_)
            parts.append(vjp(ct[:, hs]))
        cat = lambda i: jnp.concatenate([p[i] for p in parts], axis=1)
        return (cat(0), cat(1)), (sum(p[2] for p in parts),)

    d_o, d_g_a, d_gn = _rowwise("proj_a_dx", proj_a_dx, [d_ya, o_gla, g_a], [weight("w_pa", o_gla), gn],
                                [(ml_w, F32), (ml_w, BF16)],
                                [((1, HEAD_W), F32)], tile=512)
    dq_hm, dk_hm, d_va, dla_hm = _gla_bwd(q_hm, k_hm, v_a, la_hm, s_prev, d_o)

    def decay_bwd(al, ct, w, b):
        _, vjp = jax.vjp(_log_decay, al, w, b)
        dal, dw, db = vjp(ct)
        return (dal,), (dw, db)

    d_alow_p, d_wa_p, d_ba = _rowwise("gla_decay_bwd", decay_bwd, [a_low_p, _from_hm(dla_hm)], [w_a_up_p, b_a_up],
                                      [(LANES, BF16)], [(w_a_up_p.shape, F32), (b_a_up.shape, F32)])
    d_proj = [jnp.concatenate([_from_hm(dq_hm), _from_hm(dk_hm), d_va, d_g_a], axis=1), d_alow_p,
              jnp.concatenate([d_xm, d_opre, d_ga, d_gb], axis=1)]
    d_widths = [offs[4], LOWRANK, offs[9] - offs[5]]
    d_pieces = _shard_pieces(d_widths, n_in)
    small = dict(w_a_up=d_wa_p[0:LOWRANK], b_a_up=d_ba, g_gla_norm=d_gn, conv_w=d_cw, conv_b=d_cb,
                 w_q_ml=d_wq, w_k_ml=d_wk, w_v_ml=d_wv, w_if=d_wif, b_if=d_bif, ml_skip=d_skip, g_ml_norm=d_gml,
                 g_post_mix=d_gpm, g_pre_mlp=d_gpl, g_post_mlp=d_gpo)
    sent_small = on_small(small, loss)
    sent_in = sent_small
    for half in range(2):
        dw_half = _mm_shard_cols(h, d_proj, d_widths, n_in, "proj_in_dw_%d" % half, half, d // 2, deps=dep(sent_in))
        sent_in = on_grads({"w_in#%d" % half: dw_half})

    def proj_in_dx(dp_a, dp_low, dp_b, xv, dres, w, g):
        dh = 0.0
        for s in range(N_DEV):
            for i, c_in, c_w, wd in d_pieces[s]:
                src = (dp_a, dp_low, dp_b)[i]
                cols = src.shape[1] - c_in if wd < LANES else wd
                dh = dh + _raw_dot(src[:, c_in:c_in + cols], w[s][:, c_w:c_w + cols], "nt")
        _, vjp = jax.vjp(_rms, xv, g)
        dx, dg = vjp(dh)
        return (dx + dres,), (dg,)

    grad_x, d_g1 = _rowwise("proj_in_dx", proj_in_dx, [*d_proj, x, dx_res], [w_in, g1], [(d, F32)], [((1, d), F32)],
                            deps=dep(sent_in))
    return grad_x, on_small(dict(g_pre_mix=d_g1), None)


BIG = ("w_in", "w_pa", "w_pb", "w_o", "w_up", "w_down")
BIG_COL_SHARDED = ("w_in", "w_pa", "w_pb", "w_up")
SMALL_SHARDED = ("w_a_up", "conv_w", "w_if")
SMALL = ("g_pre_mix", "w_a_up", "b_a_up", "g_gla_norm", "conv_w", "conv_b", "w_q_ml", "w_k_ml", "w_v_ml", "w_if", "b_if",
         "ml_skip", "g_ml_norm", "g_post_mix", "g_pre_mlp", "g_post_mlp")
WEIGHTS = ("g_pre_mix", "w_in", "w_a_up", "b_a_up", "g_gla_norm", "conv_w", "conv_b", "w_q_ml", "w_k_ml", "w_v_ml", "w_if", "b_if",
           "ml_skip", "g_ml_norm", "w_pa", "w_pb", "w_o", "g_post_mix", "g_pre_mlp", "w_up", "w_down", "g_post_mlp")


def kernel(x, g_pre_mix, w_in, w_a_up, b_a_up, g_gla_norm, conv_w, conv_b, w_q_ml, w_k_ml, w_v_ml, w_if, b_if, ml_skip, g_ml_norm, w_pa, w_pb, w_o, g_post_mix, g_pre_mlp, w_up, w_down, g_post_mlp, loss_target, m_g_pre_mix, m_w_in, m_w_a_up, m_b_a_up, m_g_gla_norm, m_conv_w, m_conv_b, m_w_q_ml, m_w_k_ml, m_w_v_ml, m_w_if, m_b_if, m_ml_skip, m_g_ml_norm, m_w_pa, m_w_pb, m_w_o, m_g_post_mix, m_g_pre_mlp, m_w_up, m_w_down, m_g_post_mlp, v_g_pre_mix, v_w_in, v_w_a_up, v_b_a_up, v_g_gla_norm, v_conv_w, v_conv_b, v_w_q_ml, v_w_k_ml, v_w_v_ml, v_w_if, v_b_if, v_ml_skip, v_g_ml_norm, v_w_pa, v_w_pb, v_w_o, v_g_post_mix, v_g_pre_mlp, v_w_up, v_w_down, v_g_post_mlp):
    args = dict(locals())
    w = {n: args[n][0] for n in WEIGHTS}
    m = {n: args["m_" + n][0] for n in WEIGHTS}
    v = {n: args["v_" + n][0] for n in WEIGHTS}

    me_lin = _lin(_me())
    me_idx = jnp.reshape(me_lin, (1,)).astype(jnp.int32)

    def full_weight(n, g):
        if n in ("w_in", "w_up"):
            return g
        return _from_col_blocks(g) if n in BIG_COL_SHARDED else g.reshape(-1, g.shape[-1])

    def grad_parts(n, g):
        if n.partition("#")[0] in ("w_in", "w_up"):
            return g
        return (_col_blocks(g) if n in BIG_COL_SHARDED else g.reshape(N_DEV, -1, g.shape[-1])).astype(BF16)

    sharded_names = tuple(SMALL_SHARDED)
    narrow = {n: w[n].astype(BF16) for n in BIG}
    ready, pending, passing = {}, {}, {}
    first_state, _ = _copies_start(["gather"] * len(sharded_names) + ["gather_chips"],
                                   [_small_view(n, w[n]) for n in sharded_names] + [narrow["w_in"]], "allgather_start_first")

    def prefetch(group, after):
        state, token = _copies_start("gather_chips", [narrow[n] for n in group], "allgather_start_" + group[0], after)
        for n in group:
            pending[n] = (group, state)
        return token

    def pass_on(n, after):
        group, state = pending[n]
        shards, lands = _copies_wait(state, after, "allgather_wait_" + group[0])
        state, token = _copies_start("gather_pass", shards, "allgather_pass_" + group[0], lands=lands)
        for gn in group:
            passing[gn] = (group, state)
        return token

    def weight(n, after):
        if n not in ready:
            group, state = passing[n]
            shards, lands = _copies_wait(state, after, "allgather_passed_" + group[0])
            for gn, shard, land in zip(group, shards, lands):
                ready[gn] = full_weight(gn, lax.dynamic_update_slice(land, shard[None], (me_lin, 0, 0)))
        return ready[n]

    h = _first_norm(x[0], w["g_pre_mix"].reshape(1, -1))
    first_own, first_lands = _copies_wait(first_state, [h] + [narrow[n] for n in BIG if n != "w_in"], "allgather_wait_first")
    state, _ = _copies_start("gather_pass", first_own[-1:], "allgather_pass_w_in", lands=first_lands[-1:])
    passing["w_in"] = (("w_in",), state)
    ws = {n: (w[n].reshape(1, -1) if w[n].ndim == 1 else w[n]) for n in SMALL if n not in SMALL_SHARDED}
    for n, own, land in zip(sharded_names, first_own, first_lands):
        ws[n] = _small_unshard(n, lax.dynamic_update_slice(land, own[None], (me_lin, 0, 0)))

    sets, waiting_small = [], []

    def start_set(large, small):
        names = tuple(large)
        s_names, s_kinds, s_srcs = small if small else ((), [], [])
        state, token = _copies_start(s_kinds + ["exchange"] * len(names), s_srcs + [grad_parts(n, large[n]) for n in names],
                                     "exchange_start_" + (names + s_names)[0].replace("#", "_"))
        sets.append((names, s_names, s_kinds, state))
        return token

    def on_grads(grads):
        return start_set(grads, waiting_small.pop() if waiting_small else None)

    def on_small(small, loss):
        names = tuple(small)
        kinds = ["exchange" if n in SMALL_SHARDED else "gather" for n in names]
        srcs = [_small_shards(n, small[n]) if n in SMALL_SHARDED else _small_view(n, small[n]) for n in names]
        if loss is None:
            return start_set({}, (names, kinds, srcs))
        waiting_small.append((names, kinds + ["gather"], srcs + [loss]))
        return None

    grad_x, last_token = _local_step(x[0], h, loss_target[0], weight, ws, prefetch, pass_on, on_grads, on_small)

    out, chunks, sums = {}, {}, []

    def finish_set(names, s_names, s_kinds, state, after):
        own, lands = _copies_wait(state, after, "exchange_wait_" + (names + s_names)[0].replace("#", "_"))
        ns = len(s_kinds)
        if s_names:
            k = len(s_names)
            upd = _small_update("adamw_small_" + s_names[0], me_idx, s_kinds[:k], lands[:k], own[:k],
                                *[[_small_view(n, d[n]) for n in s_names] for d in (w, m, v)],
                                sums=list(zip(lands[k:ns], own[k:ns])))
            for i, n in enumerate(s_names):
                out[n] = tuple(_small_unview(n, a, w[n].shape) for a in upd[4 * i:4 * i + 4])
            sums.extend(upd[4 * k:])
        for name, part, land in zip(names, own[ns:], lands[ns:]):
            n, _, chunk = name.partition("#")
            chunks.setdefault(n, []).append((land, part))
            if chunk in ("", "1"):
                got_lands, got_parts = zip(*chunks[n])
                out[n] = _sum_adamw(got_lands, got_parts, me_idx, w[n], m[n], v[n], "adamw_" + n)

    for entry in sets:
        finish_set(*entry, [grad_x, last_token] + [out[n][1] for n in BIG if n in out])
    loss_sum = sums[0]

    shaped = lambda a, n: a.reshape(args[n].shape)
    return (loss_sum[0, 0], grad_x[None],
            *[shaped(out[n][0], n) for n in WEIGHTS], *[shaped(out[n][1], n) for n in WEIGHTS],
            *[shaped(out[n][2], n) for n in WEIGHTS], *[shaped(out[n][3], n) for n in WEIGHTS])
```

```python
import functools

import jax
import jax.numpy as jnp
from jax import lax
from jax.experimental import pallas as pl
from jax.experimental.pallas import tpu as pltpu

F32 = jnp.float32
BF16 = jnp.bfloat16
MESH = pl.DeviceIdType.MESH

N_DEV = 8
EPS = 1e-6
CHUNK = 64
CHUNKS_PER_STEP = 8
HEADS = 4
GLA_DK = 64
HEAD_W = 128
GLA_GATE_NORM = 16.0
LOWRANK = 16
CONV_K = 4
QKV_BLOCK = 4
LANES = 128
HALO = 8
IN_SPLITS = (256, 256, 512, 512, 16, 512, 512, 1024, 1024)

ADAM_LR = 0.001
ADAM_B1 = 0.9
ADAM_B2 = 0.999
ADAM_EPS = 1e-08
ADAM_WD = 0.01
ADAM_STEP = 10

VMEM_LIMIT = 56 * 1024 * 1024


def _cparams(*sem):
    return pltpu.CompilerParams(dimension_semantics=sem, vmem_limit_bytes=VMEM_LIMIT)


def _dims(mode, ndim):
    contract = {"nn": ((ndim - 1,), (ndim - 2,)), "nt": ((ndim - 1,), (ndim - 1,)), "tn": ((ndim - 2,), (ndim - 2,))}[mode]
    return contract, (((0,), (0,)) if ndim == 3 else ((), ()))


def _raw_dot(a, b, mode):
    return lax.dot_general(a.astype(BF16), b.astype(BF16), _dims(mode, a.ndim), preferred_element_type=F32)


@functools.partial(jax.custom_vjp, nondiff_argnums=(2,))
def _bdot(a, b, mode):
    return _raw_dot(a, b, mode)


def _bdot_fwd(a, b, mode):
    return _raw_dot(a, b, mode), (a, b)


def _bdot_bwd(mode, res, ct):
    a, b = res
    if mode == "nn":
        da, db = _raw_dot(ct, b, "nt"), _raw_dot(a, ct, "tn")
    elif mode == "nt":
        da, db = _raw_dot(ct, b, "nn"), _raw_dot(ct, a, "tn")
    else:
        da, db = _raw_dot(b, ct, "nt"), _raw_dot(a, ct, "nn")
    return da.astype(a.dtype), db.astype(b.dtype)


_bdot.defvjp(_bdot_fwd, _bdot_bwd)


def _split3(x):
    hi = x.astype(BF16)
    r1 = x - hi.astype(F32)
    mid = r1.astype(BF16)
    return hi, mid, (r1 - mid.astype(F32)).astype(BF16)


def _split_dot(tri, x):
    if x.ndim == 3:
        tri = jnp.broadcast_to(tri, (x.shape[0], *tri.shape))
    return sum(lax.dot_general(tri, t, _dims("nn", x.ndim), preferred_element_type=F32) for t in _split3(x))


def _tri(n, lower):
    r = lax.broadcasted_iota(jnp.int32, (n, n), 0)
    c = lax.broadcasted_iota(jnp.int32, (n, n), 1)
    return ((c <= r) if lower else (c >= r)).astype(BF16)


@jax.custom_vjp
def _cumsum_rows(x):
    return _split_dot(_tri(x.shape[-2], True), x)


def _cumsum_rows_fwd(x):
    return _cumsum_rows(x), None


def _cumsum_rows_bwd(_, ct):
    return (_split_dot(_tri(ct.shape[-2], False), ct),)


_cumsum_rows.defvjp(_cumsum_rows_fwd, _cumsum_rows_bwd)


def _abs(x):
    return jnp.where(x >= 0, x, -x)


def _sigmoid(x):
    return lax.logistic(x)


def _log_sigmoid(x):
    return jnp.minimum(x, 0.0) - jnp.log(1.0 + jnp.exp(-_abs(x)))


def _rms(x, g):
    return x * lax.rsqrt(jnp.mean(x * x, axis=-1, keepdims=True) + EPS) * g


def _head_slices(w):
    return [slice(h * w, (h + 1) * w) for h in range(HEADS)]


def _heads(ref, rows=slice(None)):
    return jnp.stack([ref[rows, hs] for hs in _head_slices(HEAD_W)])


def _put_heads(ref, val, rows=slice(None)):
    for h, hs in enumerate(_head_slices(HEAD_W)):
        ref[rows, hs] = val[h].astype(ref.dtype)


def _tile(dim, want):
    if dim <= want or dim % LANES:
        return dim
    t = want
    while dim % t:
        t -= LANES
    return t


def _mm(a, b, mode, out_dtype, name, tm=1024, tn=1024, tk=4096, epilogue=None, extra=(), deps=(), shards=None):
    if shards == "b":
        assert mode == "nn"
        ns = b.shape[2]
        (m, k), (k2, n) = a.shape, (b.shape[1], b.shape[0] * ns)
        tn = ns
    elif mode == "nn":
        (m, k), (k2, n) = a.shape, b.shape
    elif mode == "nt":
        (m, k), (n, k2) = a.shape, b.shape
    else:
        (k, m), (k2, n) = a.shape, b.shape
    assert k == k2, (name, a.shape, b.shape)
    tm, tn, tk = _tile(m, tm), _tile(n, tn), _tile(k, tk)
    nk = k // tk
    out_dtypes = out_dtype if epilogue else (out_dtype,)
    assert nk == 1 or (out_dtype == F32 and not epilogue), name
    n_in = 2 + len(extra)

    def body(*refs):
        p = _raw_dot(refs[0][...], refs[1][...], mode)
        if nk > 1:
            _accumulate(pl.program_id(2), [refs[n_in + len(deps)]], [p])
            return
        outs = epilogue(p, *[r[...] for r in refs[2:n_in]]) if epilogue else (p,)
        for ref, val in zip(refs[n_in + len(deps):], outs):
            ref[...] = val.astype(ref.dtype)

    a_spec = pl.BlockSpec((tk, tm), lambda i, j, kk: (kk, i)) if mode == "tn" else pl.BlockSpec((tm, tk), lambda i, j, kk: (i, kk))
    if shards == "b":
        b_spec = pl.BlockSpec((None, tk, tn), lambda i, j, kk: (j, kk, 0))
    elif mode == "nt":
        b_spec = pl.BlockSpec((tn, tk), lambda i, j, kk: (j, kk))
    else:
        b_spec = pl.BlockSpec((tk, tn), lambda i, j, kk: (kk, j))
    o_spec = pl.BlockSpec((tm, tn), lambda i, j, kk: (i, j))
    res = pl.pallas_call(
        body, name=name, grid=(m // tm, n // tn, nk),
        in_specs=[a_spec, b_spec] + [o_spec] * len(extra) + [ANY] * len(deps), out_specs=[o_spec] * len(out_dtypes),
        out_shape=[jax.ShapeDtypeStruct((m, n), dt) for dt in out_dtypes],
        compiler_params=_cparams("parallel", "parallel", "arbitrary"),
    )(a, b, *extra, *deps)
    return res if epilogue else res[0]


def _mm_tn_whole(pairs, name):
    def body(*refs):
        for i in range(len(pairs)):
            refs[2 * len(pairs) + i][...] = _raw_dot(refs[2 * i][...], refs[2 * i + 1][...], "tn").astype(BF16)

    return pl.pallas_call(
        body, name=name,
        out_shape=[jax.ShapeDtypeStruct((a.shape[1], b.shape[1]), BF16) for a, b in pairs],
        compiler_params=pltpu.CompilerParams(vmem_limit_bytes=VMEM_LIMIT),
    )(*[x for pair in pairs for x in pair])


def _shard_pieces(widths, n):
    bounds = [0]
    for wd in widths:
        bounds.append(bounds[-1] + wd)
    assert bounds[-1] == N_DEV * n
    return [[(i, max(s * n, b) - b, max(s * n, b) - s * n, min((s + 1) * n, b + wd) - max(s * n, b))
             for i, (b, wd) in enumerate(zip(bounds, widths)) if b < (s + 1) * n and b + wd > s * n]
            for s in range(N_DEV)]


def _mm_shard_cols(a, bs, widths, n, name, row_tile, tm, deps=()):
    t = a.shape[0]
    nb = len(bs)
    pieces = _shard_pieces(widths, n)

    def body(a_ref, *rest):
        b_refs = rest[:nb]
        o_ref, at_ref = rest[nb + len(deps):]
        j = pl.program_id(0)

        @pl.when(j == 0)
        def _():
            at_ref[...] = a_ref[...].astype(BF16).T

        for s in range(N_DEV):
            @pl.when(j == s)
            def _(s=s):
                for i, c_in, c_out, wd in pieces[s]:
                    cols = min(_round_up(wd, LANES), bs[i].shape[1] - c_in) if wd < LANES else wd
                    p = _raw_dot(at_ref[...], b_refs[i][:, c_in:c_in + cols], "nn")
                    o_ref[:, c_out:c_out + wd] = p[:, 0:wd].astype(BF16)

    return pl.pallas_call(
        body, name=name, grid=(N_DEV,),
        in_specs=[pl.BlockSpec((t, tm), lambda j: (0, row_tile))]
        + [pl.BlockSpec(b.shape, lambda j: (0, 0), pipeline_mode=pl.Buffered(1)) for b in bs] + [ANY] * len(deps),
        out_specs=pl.BlockSpec((None, tm, n), lambda j: (j, 0, 0)),
        out_shape=jax.ShapeDtypeStruct((N_DEV, tm, n), BF16),
        scratch_shapes=[pltpu.VMEM((tm, t), BF16)],
        compiler_params=_cparams("arbitrary"),
    )(a, *bs, *deps)


def _round_up(v, m):
    return -(-v // m) * m


def _rowwise(name, fn, rows, params, out_rows, out_accs=(), tile=256, deps=()):
    t = rows[0].shape[0]
    r = min(tile, t)
    assert t % r == 0
    n_in, n_or = len(rows) + len(params), len(out_rows)
    n_all = n_in + len(deps)
    params = list(params) + list(deps)

    def body(*refs):
        vals = [ref[...] for ref in refs[:n_in]]
        outs = refs[n_all:]
        ro, ao = fn(*vals)
        for ref, v in zip(outs[:n_or], ro):
            ref[...] = v.astype(ref.dtype)
        if out_accs:
            _accumulate(pl.program_id(0), outs[n_or:], ao)

    def full(shape):
        return pl.BlockSpec(shape, lambda i, nd=len(shape): (0,) * nd)

    return pl.pallas_call(
        body, name=name, grid=(t // r,),
        in_specs=[pl.BlockSpec((r, a.shape[1]), lambda i: (i, 0)) for a in rows] + [full(p.shape) for p in params],
        out_specs=[pl.BlockSpec((r, w), lambda i: (i, 0)) for w, _ in out_rows] + [full(s) for s, _ in out_accs],
        out_shape=[jax.ShapeDtypeStruct((t, w), dt) for w, dt in out_rows] + [jax.ShapeDtypeStruct(s, dt) for s, dt in out_accs],
        compiler_params=_cparams("arbitrary"),
    )(*rows, *params)


def _accumulate(step, refs, vals):
    for ref, v in zip(refs, vals):
        @pl.when(step == 0)
        def _(ref=ref, v=v):
            ref[...] = v.astype(ref.dtype)

        @pl.when(step > 0)
        def _(ref=ref, v=v):
            ref[...] += v.astype(ref.dtype)


def _gla_chunk(q, k, v, la, st):
    c = q.shape[-2]
    row = lax.broadcasted_iota(jnp.int32, (c, c), 0)
    col = lax.broadcasted_iota(jnp.int32, (c, c), 1)
    cum = _cumsum_rows(la)
    cl = jnp.sum(la, axis=-2, keepdims=True)
    ep = jnp.exp(cum)
    en = jnp.exp(-cum)
    qs = q * (GLA_DK ** -0.5)
    qp = qs * ep
    a_f = _bdot(qp, k * en, "nt")
    a_b = _bdot(qs * en, k * ep, "nt")
    sc = jnp.where(row >= col, a_f, a_b)
    o = _bdot(sc, v, "nn") + _bdot(qp, st, "nt")
    kd = k * jnp.exp(cl - cum)
    st_new = st * jnp.exp(cl) + _bdot(v, kd, "tn")
    return o, st_new


def _gla_specs(nc, rev):
    nb = nc // CHUNKS_PER_STEP
    rows = CHUNKS_PER_STEP * CHUNK

    def blk(n):
        return (nb - 1 - n) if rev else n
    hm = pl.BlockSpec((HEADS, rows, GLA_DK), lambda n: (0, blk(n), 0))
    tm = pl.BlockSpec((rows, HEADS * HEAD_W), lambda n: (blk(n), 0))
    st = pl.BlockSpec((HEADS, CHUNKS_PER_STEP, HEAD_W, GLA_DK), lambda n: (0, blk(n), 0, 0))
    return nb, hm, tm, st


def _chunk_rows(c):
    return slice(c * CHUNK, (c + 1) * CHUNK)


def _gla_fwd(q, k, v, la, deps=()):
    t = v.shape[0]
    nc = t // CHUNK
    nb, hm, tm, st = _gla_specs(nc, False)

    def body(q_ref, k_ref, v_ref, la_ref, *rest):
        o_ref, sp_ref, st_ref = rest[len(deps):]

        @pl.when(pl.program_id(0) == 0)
        def _():
            st_ref[...] = jnp.zeros_like(st_ref)

        s = st_ref[...]
        for c in range(CHUNKS_PER_STEP):
            r = _chunk_rows(c)
            sp_ref[:, c] = s
            o, s = _gla_chunk(q_ref[:, r], k_ref[:, r], _heads(v_ref, r), la_ref[:, r], s)
            _put_heads(o_ref, o, r)
        st_ref[...] = s

    return pl.pallas_call(
        body, name="gla_fwd", grid=(nb,),
        in_specs=[hm, hm, tm, hm] + [ANY] * len(deps), out_specs=[tm, st],
        out_shape=[jax.ShapeDtypeStruct((t, HEADS * HEAD_W), F32), jax.ShapeDtypeStruct((HEADS, nc, HEAD_W, GLA_DK), F32)],
        scratch_shapes=[pltpu.VMEM((HEADS, HEAD_W, GLA_DK), F32)],
        compiler_params=_cparams("arbitrary"),
    )(q, k, v, la, *deps)


def _gla_bwd(q, k, v, la, sp, do):
    t = v.shape[0]
    nc = t // CHUNK
    nb, hm, tm, st = _gla_specs(nc, True)

    def body(q_ref, k_ref, v_ref, la_ref, sp_ref, do_ref, dq_ref, dk_ref, dv_ref, dla_ref, ds_ref):
        @pl.when(pl.program_id(0) == 0)
        def _():
            ds_ref[...] = jnp.zeros_like(ds_ref)

        ds = ds_ref[...]
        for c in reversed(range(CHUNKS_PER_STEP)):
            r = _chunk_rows(c)
            _, vjp = jax.vjp(_gla_chunk, q_ref[:, r], k_ref[:, r], _heads(v_ref, r), la_ref[:, r], sp_ref[:, c])
            dq, dk, dv, dla, ds = vjp((_heads(do_ref, r), ds))
            dq_ref[:, r] = dq.astype(dq_ref.dtype)
            dk_ref[:, r] = dk.astype(dk_ref.dtype)
            _put_heads(dv_ref, dv, r)
            dla_ref[:, r] = dla
        ds_ref[...] = ds

    hm_shape = jax.ShapeDtypeStruct((HEADS, t, GLA_DK), BF16)
    return pl.pallas_call(
        body, name="gla_bwd", grid=(nb,),
        in_specs=[hm, hm, tm, hm, st, tm], out_specs=[hm, hm, tm, hm],
        out_shape=[hm_shape, hm_shape, jax.ShapeDtypeStruct((t, HEADS * HEAD_W), BF16),
                   jax.ShapeDtypeStruct((HEADS, t, GLA_DK), F32)],
        scratch_shapes=[pltpu.VMEM((HEADS, HEAD_W, GLA_DK), F32)],
        compiler_params=_cparams("arbitrary"),
    )(q, k, v, la, sp, do)


def _ml_chunk(q, k, v, li_r, lf_r, cm, nv, m):
    c = q.shape[-2]
    row = lax.broadcasted_iota(jnp.int32, (c, c), 0)
    col = lax.broadcasted_iota(jnp.int32, (c, c), 1)
    eye = (row == col).astype(F32)
    li_c = jnp.sum(eye * li_r, axis=-1, keepdims=True)
    lf_c = jnp.sum(eye * lf_r, axis=-1, keepdims=True)
    fc_c = jnp.sum((col <= row).astype(F32) * lf_r, axis=-1, keepdims=True)
    fc_r = jnp.sum((row <= col).astype(F32) * lf_c, axis=-2, keepdims=True)
    f_last = jnp.sum(lf_r, axis=-1, keepdims=True)
    kc = k * (HEAD_W ** -0.5)
    a_c = f_last - fc_c + li_c
    m_loc = jnp.max(a_c, axis=-2, keepdims=True)
    kw = kc * jnp.exp(a_c - m_loc)
    c_chunk = _bdot(kw, v, "tn")
    n_chunk = jnp.sum(kw, axis=-2, keepdims=True)
    m_new = jnp.maximum(f_last + m, m_loc)
    sp = jnp.exp(f_last + m - m_new)
    sl = jnp.exp(m_loc - m_new)
    cm_new = sp * cm + sl * c_chunk
    nv_new = sp * nv + sl * n_chunk
    log_d = li_r - _abs(fc_c - fc_r)
    g_inter = fc_c + m
    m_t = jnp.maximum(g_inter, jnp.max(log_d, axis=-1, keepdims=True))
    s = _bdot(q, kc, "nt") * jnp.exp(log_d - m_t)
    sc = jnp.exp(g_inter - m_t)
    num = _bdot(s, v, "nn") + sc * _bdot(q, cm, "nn")
    den = jnp.sum(s, axis=-1, keepdims=True) + sc * jnp.sum(q * nv, axis=-1, keepdims=True)
    den = jnp.maximum(_abs(den), jnp.exp(-m_t))
    return num / den, cm_new, nv_new, m_new


def _ml_specs(nc, rev):
    nb = nc // CHUNKS_PER_STEP

    def blk(n):
        return (nb - 1 - n) if rev else n
    tm = pl.BlockSpec((CHUNKS_PER_STEP * CHUNK, HEADS * HEAD_W), lambda n: (blk(n), 0))
    gate = pl.BlockSpec((HEADS, CHUNKS_PER_STEP, 1, CHUNK), lambda n: (0, blk(n), 0, 0))
    cm = pl.BlockSpec((HEADS, CHUNKS_PER_STEP, HEAD_W, HEAD_W), lambda n: (0, blk(n), 0, 0))
    vec = pl.BlockSpec((HEADS, CHUNKS_PER_STEP, 1, HEAD_W), lambda n: (0, blk(n), 0, 0))
    return nb, tm, gate, cm, vec


_ML_STATE = [pltpu.VMEM((HEADS, HEAD_W, HEAD_W), F32), pltpu.VMEM((HEADS, 1, HEAD_W), F32), pltpu.VMEM((HEADS, 1, HEAD_W), F32)]


def _ml_fwd(q, k, v, li, lf):
    t = q.shape[0]
    nc = t // CHUNK
    nb, tm, gate, cm, vec = _ml_specs(nc, False)

    def body(q_ref, k_ref, v_ref, li_ref, lf_ref, hc_ref, cp_ref, np_ref, mp_ref, c_ref, n_ref, m_ref):
        @pl.when(pl.program_id(0) == 0)
        def _():
            c_ref[...] = jnp.zeros_like(c_ref)
            n_ref[...] = jnp.zeros_like(n_ref)
            m_ref[...] = jnp.zeros_like(m_ref)

        cs, ns, ms = c_ref[...], n_ref[...], m_ref[...][:, :, 0:1]
        for c in range(CHUNKS_PER_STEP):
            r = _chunk_rows(c)
            cp_ref[:, c] = cs
            np_ref[:, c] = ns
            mp_ref[:, c] = jnp.broadcast_to(ms, m_ref.shape)
            hc, cs, ns, ms = _ml_chunk(_heads(q_ref, r), _heads(k_ref, r), _heads(v_ref, r), li_ref[:, c], lf_ref[:, c],
                                       cs, ns, ms)
            _put_heads(hc_ref, hc, r)
        c_ref[...] = cs
        n_ref[...] = ns
        m_ref[...] = jnp.broadcast_to(ms, m_ref.shape)

    return pl.pallas_call(
        body, name="mlstm_fwd", grid=(nb,),
        in_specs=[tm, tm, tm, gate, gate], out_specs=[tm, cm, vec, vec],
        out_shape=[jax.ShapeDtypeStruct((t, HEADS * HEAD_W), F32), jax.ShapeDtypeStruct((HEADS, nc, HEAD_W, HEAD_W), F32),
                   jax.ShapeDtypeStruct((HEADS, nc, 1, HEAD_W), F32), jax.ShapeDtypeStruct((HEADS, nc, 1, HEAD_W), F32)],
        scratch_shapes=_ML_STATE,
        compiler_params=_cparams("arbitrary"),
    )(q, k, v, li, lf)


def _ml_bwd(q, k, v, li, lf, cp, npv, mp, dhc):
    t = q.shape[0]
    nc = t // CHUNK
    nb, tm, gate, cm, vec = _ml_specs(nc, True)

    def body(q_ref, k_ref, v_ref, li_ref, lf_ref, cp_ref, np_ref, mp_ref, dhc_ref,
             dq_ref, dk_ref, dv_ref, dli_ref, dlf_ref, dc_ref, dn_ref, dm_ref):
        @pl.when(pl.program_id(0) == 0)
        def _():
            dc_ref[...] = jnp.zeros_like(dc_ref)
            dn_ref[...] = jnp.zeros_like(dn_ref)
            dm_ref[...] = jnp.zeros_like(dm_ref)

        dc, dn, dm = dc_ref[...], dn_ref[...], dm_ref[...][:, :, 0:1]
        for c in reversed(range(CHUNKS_PER_STEP)):
            r = _chunk_rows(c)
            _, vjp = jax.vjp(_ml_chunk, _heads(q_ref, r), _heads(k_ref, r), _heads(v_ref, r), li_ref[:, c], lf_ref[:, c],
                             cp_ref[:, c], np_ref[:, c], mp_ref[:, c][:, :, 0:1])
            dq, dk, dv, dli, dlf, dc, dn, dm = vjp((_heads(dhc_ref, r), dc, dn, dm))
            _put_heads(dq_ref, dq, r)
            _put_heads(dk_ref, dk, r)
            _put_heads(dv_ref, dv, r)
            dli_ref[:, c] = dli
            dlf_ref[:, c] = dlf
        dc_ref[...] = dc
        dn_ref[...] = dn
        dm_ref[...] = jnp.broadcast_to(dm, dm_ref.shape)

    tm_shape = jax.ShapeDtypeStruct((t, HEADS * HEAD_W), F32)
    gate_shape = jax.ShapeDtypeStruct((HEADS, nc, 1, CHUNK), F32)
    return pl.pallas_call(
        body, name="mlstm_bwd", grid=(nb,),
        in_specs=[tm, tm, tm, gate, gate, cm, vec, vec, tm], out_specs=[tm, tm, tm, gate, gate],
        out_shape=[tm_shape, tm_shape, tm_shape, gate_shape, gate_shape],
        scratch_shapes=_ML_STATE,
        compiler_params=_cparams("arbitrary"),
    )(q, k, v, li, lf, cp, npv, mp, dhc)


@jax.custom_vjp
def _bdot_diag(x, w):
    b = w.shape[1]
    return jnp.concatenate([_raw_dot(x[:, :b], w[0], "nn"), _raw_dot(x[:, b:], w[1], "nn")], axis=1)


def _bdot_diag_fwd(x, w):
    return _bdot_diag(x, w), (x, w)


def _bdot_diag_bwd(res, ct):
    x, w = res
    b = w.shape[1]
    dx = jnp.concatenate([_raw_dot(ct[:, :b], w[0], "nt"), _raw_dot(ct[:, b:], w[1], "nt")], axis=1)
    dw = jnp.stack([_raw_dot(x[:, :b], ct[:, :b], "tn"), _raw_dot(x[:, b:], ct[:, b:], "tn")])
    return dx.astype(x.dtype), dw.astype(w.dtype)


_bdot_diag.defvjp(_bdot_diag_fwd, _bdot_diag_bwd)


def _ml_pre(s0, s1, s2, s3, cw0, cw1, cw2, cw3, cb, wq, wk, wv, wiq, wik, wiv, bif):
    pre = cb + cw0 * s0 + cw1 * s1 + cw2 * s2 + cw3 * s3
    xc = pre * _sigmoid(pre)
    q = _bdot_diag(xc, wq)
    k = _bdot_diag(xc, wk)
    v = _bdot_diag(s3, wv)
    gates = _bdot(q, wiq, "nn") + _bdot(k, wik, "nn") + _bdot(v, wiv, "nn") + bif
    lane = lax.broadcasted_iota(jnp.int32, gates.shape, 1)
    gl = jnp.where(lane < HEADS, gates, _log_sigmoid(gates))
    return xc, q, k, v, gl


def _delayed(xs_ref, x_ref, halo_ref, r):
    xs_ref[0:HALO, :] = halo_ref[...]
    xs_ref[HALO:HALO + r, :] = x_ref[...]
    return [xs_ref[pl.ds(HALO - (CONV_K - 1) + j, r), :] for j in range(CONV_K)]


def _full_spec(shape):
    return pl.BlockSpec(shape, lambda i, nd=len(shape): (0,) * nd)


def _ml_pre_fwd(x_m, x_pad, params, tile=256, deps=()):
    t, w = x_m.shape
    r = min(tile, t)

    def body(*refs):
        x_ref, halo_ref = refs[:2]
        p = [ref[...] for ref in refs[2:2 + len(params)]]
        outs = refs[2 + len(params) + len(deps):-1]
        res = _ml_pre(*_delayed(refs[-1], x_ref, halo_ref, r), *p)
        for ref, val in zip(outs, res):
            ref[...] = val

    row = pl.BlockSpec((r, w), lambda i: (i, 0))
    return pl.pallas_call(
        body, name="ml_pre_fwd", grid=(t // r,),
        in_specs=[row, pl.BlockSpec((HALO, w), lambda i: (i * (r // HALO), 0))] + [_full_spec(p.shape) for p in params]
        + [ANY] * len(deps),
        out_specs=[row] * 4 + [pl.BlockSpec((r, LANES), lambda i: (i, 0))],
        out_shape=[jax.ShapeDtypeStruct((t, w), F32)] * 4 + [jax.ShapeDtypeStruct((t, LANES), F32)],
        scratch_shapes=[pltpu.VMEM((r + HALO, w), F32)],
        compiler_params=_cparams("arbitrary"),
    )(x_m, x_pad, *params, *deps)


def _ml_pre_bwd(x_m, x_pad, params, cts, tile=256):
    t, w = x_m.shape
    r = min(tile, t)
    nt = t // r
    n_p = len(params)

    def body(*refs):
        x_ref, halo_ref = refs[:2]
        p = [ref[...] for ref in refs[2:2 + n_p]]
        ct = [ref[...] for ref in refs[2 + n_p:7 + n_p]]
        dx_ref = refs[7 + n_p]
        dp_refs = refs[8 + n_p:8 + 2 * n_p]
        xs_ref, ds_ref, carry_ref = refs[8 + 2 * n_p:]
        step = pl.program_id(0)

        @pl.when(step == 0)
        def _():
            ds_ref[...] = jnp.zeros_like(ds_ref)
            carry_ref[...] = jnp.zeros_like(carry_ref)

        _, vjp = jax.vjp(_ml_pre, *_delayed(xs_ref, x_ref, halo_ref, r), *p)
        grads = vjp(tuple(ct))
        for j in range(CONV_K):
            ds_ref[j, HALO:HALO + r, :] = grads[j]
        lead = HALO + CONV_K - 1
        d_tile = sum(ds_ref[j, pl.ds(lead - j, r), :] for j in range(CONV_K))
        d_halo = sum(ds_ref[j, pl.ds(CONV_K - 1 - j, HALO), :] for j in range(CONV_K))
        dx_ref[...] = jnp.concatenate([d_tile[:r - HALO], d_tile[r - HALO:] + carry_ref[...]], axis=0).astype(dx_ref.dtype)
        carry_ref[...] = d_halo
        _accumulate(step, dp_refs, grads[CONV_K:])

    row = pl.BlockSpec((r, w), lambda i: (nt - 1 - i, 0))
    return pl.pallas_call(
        body, name="ml_pre_bwd", grid=(nt,),
        in_specs=[row, pl.BlockSpec((HALO, w), lambda i: ((nt - 1 - i) * (r // HALO), 0))] + [_full_spec(p.shape) for p in params]
        + [row] * 4 + [pl.BlockSpec((r, LANES), lambda i: (nt - 1 - i, 0))],
        out_specs=[row] + [_full_spec(p.shape) for p in params],
        out_shape=[jax.ShapeDtypeStruct((t, w), BF16)] + [jax.ShapeDtypeStruct(p.shape, F32) for p in params],
        scratch_shapes=[pltpu.VMEM((r + HALO, w), F32), pltpu.VMEM((CONV_K, r + 2 * HALO, w), F32), pltpu.VMEM((HALO, w), F32)],
        compiler_params=_cparams("arbitrary"),
    )(x_m, x_pad, *params, *cts)


def _per_head(fn, row_vals, head_params, shared_params=()):
    return [fn(*[a[:, hs] for a in row_vals], *[p[:, hs] for p in head_params], *shared_params) for hs in _head_slices(HEAD_W)]


def _gla_out(o, g, gn):
    return _rms(o, gn) * (g * _sigmoid(g))


def _ml_out(hc, op, xc, g, sk):
    hcell = hc * _sigmoid(op)
    mu = jnp.mean(hcell, axis=-1, keepdims=True)
    d = hcell - mu
    var = jnp.mean(d * d, axis=-1, keepdims=True)
    return d * lax.rsqrt(var + EPS) * g + sk * xc


def _log_decay(al, w, b):
    return _log_sigmoid(_bdot(al, w, "nn") + b) * (1.0 / GLA_GATE_NORM)


def _merge(ga, gb, ya, yb):
    ga, gb, ya, yb = (a.astype(F32) for a in (ga, gb, ya, yb))
    return _sigmoid(ga) * ya + _sigmoid(gb) * yb


def _post_mix(x, z, gpm, gpl):
    x1 = x + _rms(z, gpm)
    return x1, _rms(x1, gpl)


def _loss_rows(x1, dn, tgt, g):
    e = x1 + _rms(dn, g) - tgt
    return 0.5 * jnp.sum(jnp.mean(e * e, axis=-1, keepdims=True), axis=0, keepdims=True)


def _lin(p):
    return 4 * p[0] + 2 * p[1] + p[2]


def _me():
    return lax.axis_index("x"), lax.axis_index("y"), lax.axis_index("c")


def _flip(p, k):
    return tuple((1 - v) if (k >> (2 - i)) & 1 else v for i, v in enumerate(p))


ANY = pl.BlockSpec(memory_space=pl.ANY)


HBM = pl.BlockSpec(memory_space=pltpu.HBM)
SEM = pl.BlockSpec(memory_space=pltpu.SEMAPHORE)
DATAFLOW = pltpu.SideEffectType.DATAFLOW_SIDE_EFFECTING


SIBLING = 1
OTHER_CHIPS = (2, 4, 6)


def _peer_copies(kinds, srcs, lands, send_sems, recv_sems):
    me = _me()
    copies = []
    for a, (kind, src, land) in enumerate(zip(kinds, srcs, lands)):
        masks = {"gather": range(1, N_DEV), "exchange": range(1, N_DEV), "gather_chips": (SIBLING, *OTHER_CHIPS),
                 "gather_pass": OTHER_CHIPS}[kind]
        for k in masks:
            peer = _flip(me, k)
            if kind == "gather_pass":
                block = land.at[_lin(peer)]
                src_ref, dst_ref, target = block, block, _flip(me, SIBLING)
            else:
                src_ref, dst_ref, target = (src.at[_lin(peer)] if kind == "exchange" else src), land.at[_lin(me)], peer
            copies.append(pltpu.make_async_remote_copy(
                src_ref=src_ref, dst_ref=dst_ref, send_sem=send_sems.at[a * 7 + k - 1], recv_sem=recv_sems.at[a * 7 + k - 1],
                device_id=target, device_id_type=MESH))
    return copies


def _copies_start(kind, srcs, name, after=None, lands=None):
    n = len(srcs)
    extra = [] if after is None else [after]
    kind = [kind] * n if isinstance(kind, str) else list(kind)
    land_shapes = [(s.shape if k == "exchange" else (N_DEV, *s.shape)) for k, s in zip(kind, srcs)]
    lands = [lax.empty(ls, s.dtype) for ls, s in zip(land_shapes, srcs)] if lands is None else lands

    def body(*refs):
        sems = refs[2 * n + len(extra):]
        for cp in _peer_copies(kind, refs[:n], refs[n:2 * n], sems[0], sems[1]):
            cp.start()
        refs[-1][...] = jnp.zeros_like(refs[-1])

    def hbm(a):
        return pltpu.with_memory_space_constraint(a, pltpu.HBM)

    out = pl.pallas_call(
        body, name=name,
        out_shape=(pltpu.SemaphoreType.DMA((7 * n,)), pltpu.SemaphoreType.DMA((7 * n,)),
                   *[pltpu.HBM(s.shape, s.dtype) for s in srcs],
                   *[pltpu.HBM(ls, s.dtype) for ls, s in zip(land_shapes, srcs)],
                   jax.ShapeDtypeStruct((8, LANES), F32)),
        in_specs=[HBM] * (2 * n) + [ANY] * len(extra),
        out_specs=(SEM, SEM, *[HBM] * (2 * n), pl.BlockSpec(memory_space=pltpu.VMEM)),
        input_output_aliases={i: 2 + i for i in range(2 * n)},
        compiler_params=pltpu.CompilerParams(has_side_effects=DATAFLOW),
    )(*[hbm(s) for s in srcs], *[hbm(a) for a in lands], *extra)
    return (kind, n, out[:-1]), out[-1]


def _copies_wait(state, after, name):
    kind, n, (send_sems, recv_sems, *thru) = state
    after = list(after) if isinstance(after, (list, tuple)) else [after]

    def body(*refs):
        for cp in _peer_copies(kind, refs[:n], refs[n:2 * n], refs[2 * n], refs[2 * n + 1]):
            cp.wait_send()
            cp.wait_recv()

    out = pl.pallas_call(
        body, name=name,
        out_shape=tuple(pltpu.HBM(t.shape, t.dtype) for t in thru),
        in_specs=[HBM] * (2 * n) + [SEM, SEM] + [ANY] * len(after), out_specs=tuple([HBM] * (2 * n)),
        input_output_aliases={i: i for i in range(2 * n)},
        compiler_params=pltpu.CompilerParams(has_side_effects=DATAFLOW),
    )(*thru, send_sems, recv_sems, *after)
    return out[:n], out[n:]


def _adamw(w, g, m, v):
    m2 = ADAM_B1 * m + (1.0 - ADAM_B1) * g
    v2 = ADAM_B2 * v + (1.0 - ADAM_B2) * (g * g)
    m_hat = m2 / (1.0 - ADAM_B1 ** ADAM_STEP)
    v_hat = v2 / (1.0 - ADAM_B2 ** ADAM_STEP)
    delta = -ADAM_LR * (m_hat / (jnp.sqrt(v_hat) + ADAM_EPS) + ADAM_WD * w)
    return delta, m2, v2


def _sum_adamw(lands, parts, me_idx, w, m, v, name, tile=256):
    r, c = w.shape
    nchunks = len(lands)
    tr = min(tile, r // nchunks)
    per_chunk = r // nchunks // tr
    per = 1 + N_DEV

    def body(me_ref, *refs):
        w_ref, m_ref, v_ref, g_ref, d_ref, m2_ref, v2_ref = refs[nchunks * per:]
        for k in range(nchunks):
            own_ref, slots = refs[k * per], refs[k * per + 1:(k + 1) * per]

            @pl.when(pl.program_id(0) // per_chunk == k)
            def _(own_ref=own_ref, slots=slots):
                own = own_ref[...].astype(F32)
                g = None
                for s in range(N_DEV):
                    term = jnp.where(me_ref[0] == s, own, slots[s][...].astype(F32))
                    g = term if g is None else g + term
                d, m2, v2 = _adamw(w_ref[...], g, m_ref[...], v_ref[...])
                g_ref[...] = g
                d_ref[...] = d
                m2_ref[...] = m2
                v2_ref[...] = v2

    def chunk_specs(k):
        def tile_of(i):
            return jnp.clip(i - k * per_chunk, 0, per_chunk - 1)

        def slot_spec(s):
            return pl.BlockSpec((None, tr, c), lambda i, me: (jnp.where(me[0] == s, (s + 1) % N_DEV, s), tile_of(i), 0))
        return [pl.BlockSpec((None, tr, c), lambda i, me: (me[0], tile_of(i), 0))] + [slot_spec(s) for s in range(N_DEV)]

    row = pl.BlockSpec((tr, c), lambda i, me: (i, 0))
    operands = [a for land, part in zip(lands, parts) for a in (part, *[land] * N_DEV)]
    return pl.pallas_call(
        body, name=name,
        grid_spec=pltpu.PrefetchScalarGridSpec(
            num_scalar_prefetch=1, grid=(r // tr,),
            in_specs=[s for k in range(nchunks) for s in chunk_specs(k)] + [row] * 3,
            out_specs=[row] * 4),
        out_shape=[jax.ShapeDtypeStruct((r, c), F32)] * 4,
        compiler_params=_cparams("parallel"),
    )(me_idx, *operands, w, m, v)


def _small_update(name, me_idx, kinds, lands, owns, ws, ms, vs, sums=()):
    n = len(ws)
    lands, owns = list(lands) + [s[0] for s in sums], list(owns) + [s[1] for s in sums]
    kinds = list(kinds) + ["gather"] * len(sums)
    nl = len(lands)

    def summed(me, land_ref, own):
        g = None
        for s in range(N_DEV):
            term = jnp.where(me == s, own, land_ref[s]).astype(F32)
            g = term if g is None else g + term
        return g

    def body(me_ref, *refs):
        land_refs, own_refs = refs[:nl], refs[nl:2 * nl]
        w_refs, m_refs, v_refs = (refs[2 * nl + i * n:2 * nl + (i + 1) * n] for i in range(3))
        outs = refs[2 * nl + 3 * n:]
        me = me_ref[0]
        for i in range(n):
            g = summed(me, land_refs[i], own_refs[i][...])
            d, m2, v2 = _adamw(w_refs[i][...], g, m_refs[i][...], v_refs[i][...])
            for ref, val in zip(outs[4 * i:4 * i + 4], (g, d, m2, v2)):
                ref[...] = val
        for i in range(n, nl):
            outs[4 * n + i - n][...] = summed(me, land_refs[i], own_refs[i][...])

    def whole(shape):
        return pl.BlockSpec(shape, lambda i, me, nd=len(shape): (0,) * nd)

    def own_spec(kind, own):
        if kind == "gather":
            return whole(own.shape)
        return pl.BlockSpec((None, *own.shape[1:]), lambda i, me: (me[0], 0, 0))

    shapes = [w.shape for w in ws]
    out_shapes = [s for s in shapes for _ in range(4)] + [s[1].shape for s in sums]
    return pl.pallas_call(
        body, name=name,
        grid_spec=pltpu.PrefetchScalarGridSpec(
            num_scalar_prefetch=1, grid=(1,),
            in_specs=[whole(a.shape) for a in lands] + [own_spec(k, o) for k, o in zip(kinds, owns)]
            + [whole(s) for s in shapes] * 3,
            out_specs=[whole(s) for s in out_shapes]),
        out_shape=[jax.ShapeDtypeStruct(s, F32) for s in out_shapes],
        compiler_params=_cparams("arbitrary"),
    )(me_idx, *lands, *owns, *ws, *ms, *vs)


def _small_view(n, a):
    if a.ndim == 1:
        return a.reshape(1, -1)
    if a.ndim == 3:
        return a.transpose(1, 2, 0).reshape(QKV_BLOCK * QKV_BLOCK, -1)
    return a.T if n == "w_if" else a


def _small_unview(n, a, shape):
    if len(shape) == 1:
        return a.reshape(shape)
    if len(shape) == 3:
        return a.reshape(QKV_BLOCK, QKV_BLOCK, -1).transpose(2, 0, 1)
    return a.T if n == "w_if" else a


def _small_shards(n, g):
    if n == "w_if":
        return g.reshape(N_DEV, -1, g.shape[1]).transpose(0, 2, 1)
    return g.reshape(g.shape[0], N_DEV, -1).transpose(1, 0, 2)


def _small_unshard(n, s):
    if n == "w_if":
        return s.transpose(0, 2, 1).reshape(-1, s.shape[1])
    return s.transpose(1, 0, 2).reshape(s.shape[1], -1)


def _to_hm(a, d):
    t = a.shape[0]
    return a.reshape(t, HEADS, d).transpose(1, 0, 2)


def _from_hm(a):
    h, t, d = a.shape
    return a.transpose(1, 0, 2).reshape(t, h * d)


def _gate_rows(g):
    t = g.shape[0]
    return g.T.reshape(HEADS, t // CHUNK, 1, CHUNK)


def _gate_cols(g):
    h, nc, _, c = g.shape
    return g.reshape(h, nc * c).T


def _blockdiag_dense(w):
    n = w.shape[0] * QKV_BLOCK // 2
    tiled = jnp.tile(w.reshape(2, n, QKV_BLOCK), (1, 1, n // QKV_BLOCK))
    r = lax.broadcasted_iota(jnp.int32, (2, n, n), 1)
    c = lax.broadcasted_iota(jnp.int32, (2, n, n), 2)
    return jnp.where(r // QKV_BLOCK == c // QKV_BLOCK, tiled, 0.0)


def _blockdiag_blocks(dense):
    _, n, _ = dense[0].shape
    k = len(dense)

    def body(*refs):
        r = lax.broadcasted_iota(jnp.int32, (n, n), 0)
        c = lax.broadcasted_iota(jnp.int32, (n, n), 1)
        fr = lax.broadcasted_iota(jnp.int32, (n, LANES), 0)
        fc = lax.broadcasted_iota(jnp.int32, (n, LANES), 1)
        fold = ((fr & (QKV_BLOCK - 1)) == fc).astype(BF16)
        for i in range(k):
            for half in range(2):
                kept = jnp.where((r >> 2) == (c >> 2), refs[i][half], 0.0)
                refs[k + i][half] = sum(lax.dot_general(t, fold, _dims("nn", 2), preferred_element_type=F32)
                                        for t in _split3(kept))

    out = pl.pallas_call(body, name="blockdiag_blocks", out_shape=[jax.ShapeDtypeStruct((2, n, LANES), F32)] * k)(*dense)
    return [o[:, :, 0:QKV_BLOCK].reshape(2 * n // QKV_BLOCK, QKV_BLOCK, QKV_BLOCK) for o in out]


def _col_blocks(w):
    k, n = w.shape
    return w.reshape(k, N_DEV, n // N_DEV).transpose(1, 0, 2)


def _from_col_blocks(g):
    d, k, n = g.shape
    return g.transpose(1, 0, 2).reshape(k, d * n)


def _first_norm(x, g):
    return _rowwise("pre_mix_norm", lambda xv, gv: ((_rms(xv, gv),), ()), [x], [g], [(x.shape[1], BF16)])[0]


def _local_step(x, h, tgt, weight, ws, prefetch, pass_on, on_grads, on_small):
    t, d = x.shape
    g1 = ws["g_pre_mix"]

    def dep(token):
        return () if token is None else (token,)

    w_in = weight("w_in", x)
    fetch_mix = prefetch(("w_pa", "w_pb", "w_o"), w_in)

    n_in = w_in.shape[2]

    offs = [0]
    for s in IN_SPLITS:
        offs.append(offs[-1] + s)

    w_a_up_p = jnp.pad(ws["w_a_up"], ((0, LANES - LOWRANK), (0, 0)))
    b_a_up = ws["b_a_up"]

    def proj_in_fwd(hv, w, wa, ba):
        proj = jnp.concatenate([_raw_dot(hv, w[j], "nn") for j in range(N_DEV)], axis=1)
        parts = [proj[:, offs[i]:offs[i + 1]] for i in range(len(IN_SPLITS))]
        parts[4] = jnp.concatenate([parts[4], jnp.zeros((parts[4].shape[0], LANES - LOWRANK), F32)], axis=1)
        return (*parts, _log_decay(parts[4], wa, ba)), ()

    widths = [LANES if s == LOWRANK else s for s in IN_SPLITS]
    q_a, k_a, v_a, g_a, a_low_p, x_m, o_pre, gate_a, gate_b, la = _rowwise(
        "proj_in", proj_in_fwd, [h], [w_in, w_a_up_p, b_a_up],
        [(wd, BF16 if i == 2 else F32) for i, wd in enumerate(widths)] + [(HEADS * GLA_DK, F32)],
        deps=dep(fetch_mix))

    fetch_up = prefetch(("w_up", "w_down"), la)
    q_hm, k_hm, la_hm = _to_hm(q_a, GLA_DK), _to_hm(k_a, GLA_DK), _to_hm(la, GLA_DK)
    o_gla, s_prev = _gla_fwd(q_hm, k_hm, v_a, la_hm, deps=dep(fetch_up))
    pass_mix = pass_on("w_pa", o_gla)
    gn = ws["g_gla_norm"]
    ml_w = HEADS * HEAD_W

    cw = ws["conv_w"]
    w_if_p = jnp.pad(ws["w_if"], ((0, 0), (0, LANES - 2 * HEADS)))
    pre_params = [cw[0:1], cw[1:2], cw[2:3], cw[3:4], ws["conv_b"],
                  _blockdiag_dense(ws["w_q_ml"]), _blockdiag_dense(ws["w_k_ml"]), _blockdiag_dense(ws["w_v_ml"]),
                  w_if_p[0:ml_w], w_if_p[ml_w:2 * ml_w], w_if_p[2 * ml_w:3 * ml_w],
                  jnp.pad(ws["b_if"], ((0, 0), (0, LANES - 2 * HEADS)))]
    x_pad = jnp.pad(x_m, ((HALO, 0), (0, 0)))
    xc, q_m, k_m, v_m, gl = _ml_pre_fwd(x_m, x_pad, pre_params, tile=512, deps=dep(pass_mix))
    li, lf = _gate_rows(gl[:, 0:HEADS]), _gate_rows(gl[:, HEADS:2 * HEADS])
    hc, c_prev, n_prev, m_prev = _ml_fwd(q_m, k_m, v_m, li, lf)
    g_ml, skip = ws["g_ml_norm"], ws["ml_skip"]

    def proj_a_fwd(o, g, n_, w):
        ya = jnp.concatenate(_per_head(_gla_out, [o, g], [], [n_]), axis=1)
        return (ya, _raw_dot(ya, w, "nn")), ()

    ya_in, y_a = _rowwise("proj_a", proj_a_fwd, [o_gla, g_a], [gn, weight("w_pa", hc)], [(ml_w, BF16), (d, BF16)],
                          tile=512)

    def proj_b_fwd(a, b, c_, ga, gb, ya, g, s, w):
        hb = jnp.concatenate(_per_head(_ml_out, [a, b, c_], [g, s]), axis=1)
        yb = _raw_dot(hb, w, "nn")
        return (hb, yb, _merge(ga, gb, ya, yb)), ()

    h_b, y_b, merged = _rowwise("proj_b", proj_b_fwd, [hc, o_pre, xc, gate_a, gate_b, y_a], [g_ml, skip, weight("w_pb", hc)],
                                [(ml_w, BF16), (d, BF16), (d, BF16)], tile=512)

    gpm, gpl, gpo = ws["g_post_mix"], ws["g_pre_mlp"], ws["g_post_mlp"]

    def proj_o_fwd(mg, xv, w, a, b):
        zv = _raw_dot(mg, w, "nn")
        return (zv, *_post_mix(xv, zv, a, b)), ()

    pass_up = pass_on("w_up", merged)
    z, x1, h2 = _rowwise("proj_o", proj_o_fwd, [merged, x], [weight("w_o", merged), gpm, gpl],
                         [(d, F32), (d, F32), (d, BF16)], tile=512, deps=dep(pass_up))
    w_up = weight("w_up", h2)
    up, u = _mm(h2, w_up, "nn", (BF16, BF16), "mlp_up", tm=2048, shards="b",
                epilogue=lambda p: (p, jnp.square(jnp.maximum(p, 0.0))))

    def mlp_down_loss(uv, x1v, tgtv, w, g):
        dnv = _raw_dot(uv, w, "nn")
        loss, vjp = jax.vjp(lambda a, b, c_: _loss_rows(a, b, tgtv, c_), x1v, dnv, g)
        dx1, ddn, dg = vjp(jnp.ones((1, 1), F32))
        return (dx1, ddn), (jnp.broadcast_to(loss, (1, LANES)), dg)

    dx1_y, d_dn, loss, d_gpo = _rowwise("mlp_down", mlp_down_loss, [u, x1, tgt], [weight("w_down", u), gpo],
                                        [(d, F32), (d, BF16)], [((1, LANES), F32), ((1, d), F32)], tile=512)

    (d_up,) = _mm(d_dn, weight("w_down", u), "nt", (BF16,), "mlp_down_dx", extra=[up],
                  epilogue=lambda p, a: (p * (2.0 * jnp.maximum(a.astype(F32), 0.0)),))
    dw_down = _mm(u, d_dn, "tn", BF16, "mlp_down_dw", tm=512)
    dw_up = _mm_shard_cols(h2, [d_up], [d_up.shape[1]], w_up.shape[2], "mlp_up_dw", 0, d)
    sent_mlp = on_grads(dict(w_down=dw_down, w_up=dw_up))

    def mlp_up_dx(dup, xv, zv, dx1, w, a, b):
        _, vjp = jax.vjp(_post_mix, xv, zv, a, b)
        ns = w.shape[2]
        dh2 = sum(_raw_dot(dup[:, j * ns:(j + 1) * ns], w[j], "nt") for j in range(w.shape[0]))
        dx, dz, da, db = vjp((dx1, dh2))
        return (dx, dz), (da, db)

    dx_res, d_z, d_gpm, d_gpl = _rowwise("mlp_up_dx", mlp_up_dx, [d_up, x, z, dx1_y], [w_up, gpm, gpl],
                                         [(d, F32), (d, BF16)], [((1, d), F32), ((1, d), F32)], tile=512, deps=dep(sent_mlp))

    def proj_o_dx(dz, ga, gb, ya, yb, w):
        return jax.vjp(_merge, ga, gb, ya, yb)[1](_raw_dot(dz, w, "nt")), ()

    d_ga, d_gb, d_ya, d_yb = _rowwise("proj_o_dx", proj_o_dx, [d_z, gate_a, gate_b, y_a, y_b], [weight("w_o", merged)],
                                      [(d, BF16)] * 4, tile=512)
    dw_o, dw_pa, dw_pb = _mm_tn_whole([(merged, d_z), (ya_in, d_ya), (h_b, d_yb)], "mix_dw")
    sent_mix = on_grads(dict(w_o=dw_o, w_pa=dw_pa, w_pb=dw_pb))

    def proj_b_dx(dyb, a, b, c_, w, g, s):
        ct = _raw_dot(dyb, w, "nt")
        parts = []
        for hs in _head_slices(HEAD_W):
            _, vjp = jax.vjp(_ml_out, a[:, hs], b[:, hs], c_[:, hs], g[:, hs], s[:, hs])
            parts.append(vjp(ct[:, hs]))
        cat = lambda i: jnp.concatenate([p[i] for p in parts], axis=1)
        return (cat(0), cat(1), cat(2)), (cat(3), cat(4))

    d_hc, d_opre, d_xc, d_gml, d_skip = _rowwise("proj_b_dx", proj_b_dx, [d_yb, hc, o_pre, xc],
                                                 [weight("w_pb", hc), g_ml, skip], [(ml_w, F32), (ml_w, BF16), (ml_w, F32)],
                                                 [((1, ml_w), F32)] * 2,
                                                 tile=512, deps=dep(sent_mix))
    d_qm, d_km, d_vm, d_li, d_lf = _ml_bwd(q_m, k_m, v_m, li, lf, c_prev, n_prev, m_prev, d_hc)
    d_gl = jnp.concatenate([_gate_cols(d_li), _gate_cols(d_lf), jnp.zeros((t, LANES - 2 * HEADS), F32)], axis=1)
    pre_grads = _ml_pre_bwd(x_m, x_pad, pre_params, [d_xc, d_qm, d_km, d_vm, d_gl], tile=512)
    d_xm = pre_grads[0]
    d_cw = jnp.concatenate(pre_grads[1:5], axis=0)
    d_cb = pre_grads[5]
    d_wq, d_wk, d_wv = _blockdiag_blocks(pre_grads[6:9])
    d_wif = jnp.concatenate(pre_grads[9:12], axis=0)[:, 0:2 * HEADS]
    d_bif = pre_grads[12][:, 0:2 * HEADS]

    def proj_a_dx(dya, o, g, w, n_):
        ct = _raw_dot(dya, w, "nt")
        parts = []
        for hs in _head_slices(HEAD_W):
            _, vjp = jax.vjp(_gla_out, o[:, hs], g[:, hs], n_)
            parts.append(vjp(ct[:, hs]))
        cat = lambda i: jnp.concatenate([p[i] for p in parts], axis=1)
        return (cat(0), cat(1)), (sum(p[2] for p in parts),)

    d_o, d_g_a, d_gn = _rowwise("proj_a_dx", proj_a_dx, [d_ya, o_gla, g_a], [weight("w_pa", o_gla), gn],
                                [(ml_w, F32), (ml_w, BF16)],
                                [((1, HEAD_W), F32)], tile=512)
    dq_hm, dk_hm, d_va, dla_hm = _gla_bwd(q_hm, k_hm, v_a, la_hm, s_prev, d_o)

    def decay_bwd(al, ct, w, b):
        _, vjp = jax.vjp(_log_decay, al, w, b)
        dal, dw, db = vjp(ct)
        return (dal,), (dw, db)

    d_alow_p, d_wa_p, d_ba = _rowwise("gla_decay_bwd", decay_bwd, [a_low_p, _from_hm(dla_hm)], [w_a_up_p, b_a_up],
                                      [(LANES, BF16)], [(w_a_up_p.shape, F32), (b_a_up.shape, F32)])
    d_proj = [jnp.concatenate([_from_hm(dq_hm), _from_hm(dk_hm), d_va, d_g_a], axis=1), d_alow_p,
              jnp.concatenate([d_xm, d_opre, d_ga, d_gb], axis=1)]
    d_widths = [offs[4], LOWRANK, offs[9] - offs[5]]
    d_pieces = _shard_pieces(d_widths, n_in)
    small = dict(w_a_up=d_wa_p[0:LOWRANK], b_a_up=d_ba, g_gla_norm=d_gn, conv_w=d_cw, conv_b=d_cb,
                 w_q_ml=d_wq, w_k_ml=d_wk, w_v_ml=d_wv, w_if=d_wif, b_if=d_bif, ml_skip=d_skip, g_ml_norm=d_gml,
                 g_post_mix=d_gpm, g_pre_mlp=d_gpl, g_post_mlp=d_gpo)
    sent_small = on_small(small, loss)
    sent_in = sent_small
    for half in range(2):
        dw_half = _mm_shard_cols(h, d_proj, d_widths, n_in, "proj_in_dw_%d" % half, half, d // 2, deps=dep(sent_in))
        sent_in = on_grads({"w_in#%d" % half: dw_half})

    def proj_in_dx(dp_a, dp_low, dp_b, xv, dres, w, g):
        dh = 0.0
        for s in range(N_DEV):
            for i, c_in, c_w, wd in d_pieces[s]:
                src = (dp_a, dp_low, dp_b)[i]
                cols = src.shape[1] - c_in if wd < LANES else wd
                dh = dh + _raw_dot(src[:, c_in:c_in + cols], w[s][:, c_w:c_w + cols], "nt")
        _, vjp = jax.vjp(_rms, xv, g)
        dx, dg = vjp(dh)
        return (dx + dres,), (dg,)

    grad_x, d_g1 = _rowwise("proj_in_dx", proj_in_dx, [*d_proj, x, dx_res], [w_in, g1], [(d, F32)], [((1, d), F32)],
                            deps=dep(sent_in))
    return grad_x, on_small(dict(g_pre_mix=d_g1), None)


BIG = ("w_in", "w_pa", "w_pb", "w_o", "w_up", "w_down")
MIX = ("w_o", "w_pa", "w_pb")
BIG_COL_SHARDED = ("w_in", "w_pa", "w_pb", "w_up")
SMALL_SHARDED = ("w_a_up", "conv_w", "w_if")
SMALL = ("g_pre_mix", "w_a_up", "b_a_up", "g_gla_norm", "conv_w", "conv_b", "w_q_ml", "w_k_ml", "w_v_ml", "w_if", "b_if",
         "ml_skip", "g_ml_norm", "g_post_mix", "g_pre_mlp", "g_post_mlp")
WEIGHTS = ("g_pre_mix", "w_in", "w_a_up", "b_a_up", "g_gla_norm", "conv_w", "conv_b", "w_q_ml", "w_k_ml", "w_v_ml", "w_if", "b_if",
           "ml_skip", "g_ml_norm", "w_pa", "w_pb", "w_o", "g_post_mix", "g_pre_mlp", "w_up", "w_down", "g_post_mlp")


def kernel(x, g_pre_mix, w_in, w_a_up, b_a_up, g_gla_norm, conv_w, conv_b, w_q_ml, w_k_ml, w_v_ml, w_if, b_if, ml_skip, g_ml_norm, w_pa, w_pb, w_o, g_post_mix, g_pre_mlp, w_up, w_down, g_post_mlp, loss_target, m_g_pre_mix, m_w_in, m_w_a_up, m_b_a_up, m_g_gla_norm, m_conv_w, m_conv_b, m_w_q_ml, m_w_k_ml, m_w_v_ml, m_w_if, m_b_if, m_ml_skip, m_g_ml_norm, m_w_pa, m_w_pb, m_w_o, m_g_post_mix, m_g_pre_mlp, m_w_up, m_w_down, m_g_post_mlp, v_g_pre_mix, v_w_in, v_w_a_up, v_b_a_up, v_g_gla_norm, v_conv_w, v_conv_b, v_w_q_ml, v_w_k_ml, v_w_v_ml, v_w_if, v_b_if, v_ml_skip, v_g_ml_norm, v_w_pa, v_w_pb, v_w_o, v_g_post_mix, v_g_pre_mlp, v_w_up, v_w_down, v_g_post_mlp):
    args = dict(locals())
    w = {n: args[n][0] for n in WEIGHTS}
    m = {n: args["m_" + n][0] for n in WEIGHTS}
    v = {n: args["v_" + n][0] for n in WEIGHTS}

    me_lin = _lin(_me())
    me_idx = jnp.reshape(me_lin, (1,)).astype(jnp.int32)

    def full_weight(n, g):
        if n in ("w_in", "w_up"):
            return g
        return _from_col_blocks(g) if n in BIG_COL_SHARDED else g.reshape(-1, g.shape[-1])

    def grad_parts(n, g):
        if n.partition("#")[0] in ("w_in", "w_up"):
            return g
        return (_col_blocks(g) if n in BIG_COL_SHARDED else g.reshape(N_DEV, -1, g.shape[-1])).astype(BF16)

    sharded_names = tuple(SMALL_SHARDED)
    narrow = {n: w[n].astype(BF16) for n in BIG}
    ready, pending, passing = {}, {}, {}
    first_state, _ = _copies_start(["gather"] * len(sharded_names) + ["gather_chips"],
                                   [_small_view(n, w[n]) for n in sharded_names] + [narrow["w_in"]], "allgather_start_first")

    def prefetch(group, after):
        state, token = _copies_start("gather_chips", [narrow[n] for n in group], "allgather_start_" + group[0], after)
        for n in group:
            pending[n] = (group, state)
        return token

    def pass_on(n, after):
        group, state = pending[n]
        shards, lands = _copies_wait(state, after, "allgather_wait_" + group[0])
        state, token = _copies_start("gather_pass", shards, "allgather_pass_" + group[0], lands=lands)
        for gn in group:
            passing[gn] = (group, state)
        return token

    def weight(n, after):
        if n not in ready:
            group, state = passing[n]
            shards, lands = _copies_wait(state, after, "allgather_passed_" + group[0])
            for gn, shard, land in zip(group, shards, lands):
                ready[gn] = full_weight(gn, lax.dynamic_update_slice(land, shard[None], (me_lin, 0, 0)))
        return ready[n]

    h = _first_norm(x[0], w["g_pre_mix"].reshape(1, -1))
    first_own, first_lands = _copies_wait(first_state, [h] + [narrow[n] for n in BIG if n != "w_in"], "allgather_wait_first")
    state, _ = _copies_start("gather_pass", first_own[-1:], "allgather_pass_w_in", lands=first_lands[-1:])
    passing["w_in"] = (("w_in",), state)
    ws = {n: (w[n].reshape(1, -1) if w[n].ndim == 1 else w[n]) for n in SMALL if n not in SMALL_SHARDED}
    for n, own, land in zip(sharded_names, first_own, first_lands):
        ws[n] = _small_unshard(n, lax.dynamic_update_slice(land, own[None], (me_lin, 0, 0)))

    sets, waiting_small = [], []

    def start_set(large, small):
        names = tuple(large)
        s_names, s_kinds, s_srcs = small if small else ((), [], [])
        state, token = _copies_start(s_kinds + ["exchange"] * len(names), s_srcs + [grad_parts(n, large[n]) for n in names],
                                     "exchange_start_" + (names + s_names)[0].replace("#", "_"))
        sets.append((names, s_names, s_kinds, state))
        return token

    def on_grads(grads):
        return start_set(grads, waiting_small.pop() if waiting_small else None)

    def on_small(small, loss):
        names = tuple(small)
        kinds = ["exchange" if n in SMALL_SHARDED else "gather" for n in names]
        srcs = [_small_shards(n, small[n]) if n in SMALL_SHARDED else _small_view(n, small[n]) for n in names]
        if loss is None:
            return start_set({}, (names, kinds, srcs))
        waiting_small.append((names, kinds + ["gather"], srcs + [loss]))
        return None

    grad_x, last_token = _local_step(x[0], h, loss_target[0], weight, ws, prefetch, pass_on, on_grads, on_small)

    out, chunks, sums = {}, {}, []

    def finish_set(names, s_names, s_kinds, state, after):
        own, lands = _copies_wait(state, after, "exchange_wait_" + (names + s_names)[0].replace("#", "_"))
        ns = len(s_kinds)
        if s_names:
            k = len(s_names)
            upd = _small_update("adamw_small_" + s_names[0], me_idx, s_kinds[:k], lands[:k], own[:k],
                                *[[_small_view(n, d[n]) for n in s_names] for d in (w, m, v)],
                                sums=list(zip(lands[k:ns], own[k:ns])))
            for i, n in enumerate(s_names):
                out[n] = tuple(_small_unview(n, a, w[n].shape) for a in upd[4 * i:4 * i + 4])
            sums.extend(upd[4 * k:])
        if names == MIX:
            upd = _small_update("adamw_mix", me_idx, ["exchange"] * len(names), lands[ns:], own[ns:],
                                *[[d[n] for n in names] for d in (w, m, v)])
            for i, n in enumerate(names):
                out[n] = tuple(upd[4 * i:4 * i + 4])
            return
        for name, part, land in zip(names, own[ns:], lands[ns:]):
            n, _, chunk = name.partition("#")
            chunks.setdefault(n, []).append((land, part))
            if chunk in ("", "1"):
                got_lands, got_parts = zip(*chunks[n])
                out[n] = _sum_adamw(got_lands, got_parts, me_idx, w[n], m[n], v[n], "adamw_" + n)

    for entry in sets:
        finish_set(*entry, [grad_x, last_token] + [out[n][1] for n in BIG if n in out])
    loss_sum = sums[0]

    shaped = lambda a, n: a.reshape(args[n].shape)
    return (loss_sum[0, 0], grad_x[None],
            *[shaped(out[n][0], n) for n in WEIGHTS], *[shaped(out[n][1], n) for n in WEIGHTS],
            *[shaped(out[n][2], n) for n in WEIGHTS], *[shaped(out[n][3], n) for n in WEIGHTS])
```

```python
import functools

import jax
import jax.numpy as jnp
from jax import lax
from jax.experimental import pallas as pl
from jax.experimental.pallas import tpu as pltpu

F32 = jnp.float32
BF16 = jnp.bfloat16
MESH = pl.DeviceIdType.MESH

N_DEV = 8
EPS = 1e-6
CHUNK = 64
CHUNKS_PER_STEP = 8
HEADS = 4
GLA_DK = 64
HEAD_W = 128
GLA_GATE_NORM = 16.0
LOWRANK = 16
CONV_K = 4
QKV_BLOCK = 4
LANES = 128
HALO = 8
IN_SPLITS = (256, 256, 512, 512, 16, 512, 512, 1024, 1024)

ADAM_LR = 0.001
ADAM_B1 = 0.9
ADAM_B2 = 0.999
ADAM_EPS = 1e-08
ADAM_WD = 0.01
ADAM_STEP = 10

VMEM_LIMIT = 56 * 1024 * 1024


def _cparams(*sem):
    return pltpu.CompilerParams(dimension_semantics=sem, vmem_limit_bytes=VMEM_LIMIT)


def _dims(mode, ndim):
    contract = {"nn": ((ndim - 1,), (ndim - 2,)), "nt": ((ndim - 1,), (ndim - 1,)), "tn": ((ndim - 2,), (ndim - 2,))}[mode]
    return contract, (((0,), (0,)) if ndim == 3 else ((), ()))


def _raw_dot(a, b, mode):
    return lax.dot_general(a.astype(BF16), b.astype(BF16), _dims(mode, a.ndim), preferred_element_type=F32)


@functools.partial(jax.custom_vjp, nondiff_argnums=(2,))
def _bdot(a, b, mode):
    return _raw_dot(a, b, mode)


def _bdot_fwd(a, b, mode):
    return _raw_dot(a, b, mode), (a, b)


def _bdot_bwd(mode, res, ct):
    a, b = res
    if mode == "nn":
        da, db = _raw_dot(ct, b, "nt"), _raw_dot(a, ct, "tn")
    elif mode == "nt":
        da, db = _raw_dot(ct, b, "nn"), _raw_dot(ct, a, "tn")
    else:
        da, db = _raw_dot(b, ct, "nt"), _raw_dot(a, ct, "nn")
    return da.astype(a.dtype), db.astype(b.dtype)


_bdot.defvjp(_bdot_fwd, _bdot_bwd)


def _split3(x):
    hi = x.astype(BF16)
    r1 = x - hi.astype(F32)
    mid = r1.astype(BF16)
    return hi, mid, (r1 - mid.astype(F32)).astype(BF16)


def _split_dot(tri, x):
    if x.ndim == 3:
        tri = jnp.broadcast_to(tri, (x.shape[0], *tri.shape))
    return sum(lax.dot_general(tri, t, _dims("nn", x.ndim), preferred_element_type=F32) for t in _split3(x))


def _tri(n, lower):
    r = lax.broadcasted_iota(jnp.int32, (n, n), 0)
    c = lax.broadcasted_iota(jnp.int32, (n, n), 1)
    return ((c <= r) if lower else (c >= r)).astype(BF16)


@jax.custom_vjp
def _cumsum_rows(x):
    return _split_dot(_tri(x.shape[-2], True), x)


def _cumsum_rows_fwd(x):
    return _cumsum_rows(x), None


def _cumsum_rows_bwd(_, ct):
    return (_split_dot(_tri(ct.shape[-2], False), ct),)


_cumsum_rows.defvjp(_cumsum_rows_fwd, _cumsum_rows_bwd)


def _abs(x):
    return jnp.where(x >= 0, x, -x)


def _sigmoid(x):
    return lax.logistic(x)


def _log_sigmoid(x):
    return jnp.minimum(x, 0.0) - jnp.log(1.0 + jnp.exp(-_abs(x)))


def _rms(x, g):
    return x * lax.rsqrt(jnp.mean(x * x, axis=-1, keepdims=True) + EPS) * g


def _head_slices(w):
    return [slice(h * w, (h + 1) * w) for h in range(HEADS)]


def _heads(ref, rows=slice(None)):
    return jnp.stack([ref[rows, hs] for hs in _head_slices(HEAD_W)])


def _put_heads(ref, val, rows=slice(None)):
    for h, hs in enumerate(_head_slices(HEAD_W)):
        ref[rows, hs] = val[h].astype(ref.dtype)


def _tile(dim, want):
    if dim <= want or dim % LANES:
        return dim
    t = want
    while dim % t:
        t -= LANES
    return t


def _mm(a, b, mode, out_dtype, name, tm=1024, tn=1024, tk=4096, epilogue=None, extra=(), deps=(), shards=None):
    if shards == "b":
        assert mode == "nn"
        ns = b.shape[2]
        (m, k), (k2, n) = a.shape, (b.shape[1], b.shape[0] * ns)
        tn = ns
    elif mode == "nn":
        (m, k), (k2, n) = a.shape, b.shape
    elif mode == "nt":
        (m, k), (n, k2) = a.shape, b.shape
    else:
        (k, m), (k2, n) = a.shape, b.shape
    assert k == k2, (name, a.shape, b.shape)
    tm, tn, tk = _tile(m, tm), _tile(n, tn), _tile(k, tk)
    nk = k // tk
    out_dtypes = out_dtype if epilogue else (out_dtype,)
    assert nk == 1 or (out_dtype == F32 and not epilogue), name
    n_in = 2 + len(extra)

    def body(*refs):
        p = _raw_dot(refs[0][...], refs[1][...], mode)
        if nk > 1:
            _accumulate(pl.program_id(2), [refs[n_in + len(deps)]], [p])
            return
        outs = epilogue(p, *[r[...] for r in refs[2:n_in]]) if epilogue else (p,)
        for ref, val in zip(refs[n_in + len(deps):], outs):
            ref[...] = val.astype(ref.dtype)

    a_spec = pl.BlockSpec((tk, tm), lambda i, j, kk: (kk, i)) if mode == "tn" else pl.BlockSpec((tm, tk), lambda i, j, kk: (i, kk))
    if shards == "b":
        b_spec = pl.BlockSpec((None, tk, tn), lambda i, j, kk: (j, kk, 0))
    elif mode == "nt":
        b_spec = pl.BlockSpec((tn, tk), lambda i, j, kk: (j, kk))
    else:
        b_spec = pl.BlockSpec((tk, tn), lambda i, j, kk: (kk, j))
    o_spec = pl.BlockSpec((tm, tn), lambda i, j, kk: (i, j))
    res = pl.pallas_call(
        body, name=name, grid=(m // tm, n // tn, nk),
        in_specs=[a_spec, b_spec] + [o_spec] * len(extra) + [ANY] * len(deps), out_specs=[o_spec] * len(out_dtypes),
        out_shape=[jax.ShapeDtypeStruct((m, n), dt) for dt in out_dtypes],
        compiler_params=_cparams("parallel", "parallel", "arbitrary"),
    )(a, b, *extra, *deps)
    return res if epilogue else res[0]


def _mm_tn_whole(pairs, name):
    def body(*refs):
        for i in range(len(pairs)):
            refs[2 * len(pairs) + i][...] = _raw_dot(refs[2 * i][...], refs[2 * i + 1][...], "tn").astype(BF16)

    return pl.pallas_call(
        body, name=name,
        out_shape=[jax.ShapeDtypeStruct((a.shape[1], b.shape[1]), BF16) for a, b in pairs],
        compiler_params=pltpu.CompilerParams(vmem_limit_bytes=VMEM_LIMIT),
    )(*[x for pair in pairs for x in pair])


def _shard_pieces(widths, n):
    bounds = [0]
    for wd in widths:
        bounds.append(bounds[-1] + wd)
    assert bounds[-1] == N_DEV * n
    return [[(i, max(s * n, b) - b, max(s * n, b) - s * n, min((s + 1) * n, b + wd) - max(s * n, b))
             for i, (b, wd) in enumerate(zip(bounds, widths)) if b < (s + 1) * n and b + wd > s * n]
            for s in range(N_DEV)]


def _mm_shard_cols(a, bs, widths, n, name, row_tile, tm, deps=()):
    t = a.shape[0]
    nb = len(bs)
    pieces = _shard_pieces(widths, n)

    def body(a_ref, *rest):
        b_refs = rest[:nb]
        o_ref, at_ref = rest[nb + len(deps):]
        j = pl.program_id(0)

        @pl.when(j == 0)
        def _():
            at_ref[...] = a_ref[...].astype(BF16).T

        for s in range(N_DEV):
            @pl.when(j == s)
            def _(s=s):
                for i, c_in, c_out, wd in pieces[s]:
                    cols = min(_round_up(wd, LANES), bs[i].shape[1] - c_in) if wd < LANES else wd
                    p = _raw_dot(at_ref[...], b_refs[i][:, c_in:c_in + cols], "nn")
                    o_ref[:, c_out:c_out + wd] = p[:, 0:wd].astype(BF16)

    return pl.pallas_call(
        body, name=name, grid=(N_DEV,),
        in_specs=[pl.BlockSpec((t, tm), lambda j: (0, row_tile))]
        + [pl.BlockSpec(b.shape, lambda j: (0, 0), pipeline_mode=pl.Buffered(1)) for b in bs] + [ANY] * len(deps),
        out_specs=pl.BlockSpec((None, tm, n), lambda j: (j, 0, 0)),
        out_shape=jax.ShapeDtypeStruct((N_DEV, tm, n), BF16),
        scratch_shapes=[pltpu.VMEM((tm, t), BF16)],
        compiler_params=_cparams("arbitrary"),
    )(a, *bs, *deps)


def _round_up(v, m):
    return -(-v // m) * m


def _rowwise(name, fn, rows, params, out_rows, out_accs=(), tile=256, deps=()):
    t = rows[0].shape[0]
    r = min(tile, t)
    assert t % r == 0
    n_in, n_or = len(rows) + len(params), len(out_rows)
    n_all = n_in + len(deps)
    params = list(params) + list(deps)

    def body(*refs):
        vals = [ref[...] for ref in refs[:n_in]]
        outs = refs[n_all:]
        ro, ao = fn(*vals)
        for ref, v in zip(outs[:n_or], ro):
            ref[...] = v.astype(ref.dtype)
        if out_accs:
            _accumulate(pl.program_id(0), outs[n_or:], ao)

    def full(shape, **kw):
        return pl.BlockSpec(shape, lambda i, nd=len(shape): (0,) * nd, **kw)

    return pl.pallas_call(
        body, name=name, grid=(t // r,),
        in_specs=[pl.BlockSpec((r, a.shape[1]), lambda i: (i, 0)) for a in rows]
        + [full(p.shape, pipeline_mode=pl.Buffered(1)) for p in params],
        out_specs=[pl.BlockSpec((r, w), lambda i: (i, 0)) for w, _ in out_rows] + [full(s) for s, _ in out_accs],
        out_shape=[jax.ShapeDtypeStruct((t, w), dt) for w, dt in out_rows] + [jax.ShapeDtypeStruct(s, dt) for s, dt in out_accs],
        compiler_params=_cparams("arbitrary"),
    )(*rows, *params)


def _accumulate(step, refs, vals):
    for ref, v in zip(refs, vals):
        @pl.when(step == 0)
        def _(ref=ref, v=v):
            ref[...] = v.astype(ref.dtype)

        @pl.when(step > 0)
        def _(ref=ref, v=v):
            ref[...] += v.astype(ref.dtype)


def _gla_chunk(q, k, v, la, st):
    c = q.shape[-2]
    row = lax.broadcasted_iota(jnp.int32, (c, c), 0)
    col = lax.broadcasted_iota(jnp.int32, (c, c), 1)
    cum = _cumsum_rows(la)
    cl = jnp.sum(la, axis=-2, keepdims=True)
    ep = jnp.exp(cum)
    en = jnp.exp(-cum)
    qs = q * (GLA_DK ** -0.5)
    qp = qs * ep
    a_f = _bdot(qp, k * en, "nt")
    a_b = _bdot(qs * en, k * ep, "nt")
    sc = jnp.where(row >= col, a_f, a_b)
    o = _bdot(sc, v, "nn") + _bdot(qp, st, "nt")
    kd = k * jnp.exp(cl - cum)
    st_new = st * jnp.exp(cl) + _bdot(v, kd, "tn")
    return o, st_new


def _gla_specs(nc, rev):
    nb = nc // CHUNKS_PER_STEP
    rows = CHUNKS_PER_STEP * CHUNK

    def blk(n):
        return (nb - 1 - n) if rev else n
    hm = pl.BlockSpec((HEADS, rows, GLA_DK), lambda n: (0, blk(n), 0))
    tm = pl.BlockSpec((rows, HEADS * HEAD_W), lambda n: (blk(n), 0))
    st = pl.BlockSpec((HEADS, CHUNKS_PER_STEP, HEAD_W, GLA_DK), lambda n: (0, blk(n), 0, 0))
    return nb, hm, tm, st


def _chunk_rows(c):
    return slice(c * CHUNK, (c + 1) * CHUNK)


def _gla_fwd(q, k, v, la, deps=()):
    t = v.shape[0]
    nc = t // CHUNK
    nb, hm, tm, st = _gla_specs(nc, False)

    def body(q_ref, k_ref, v_ref, la_ref, *rest):
        o_ref, sp_ref, st_ref = rest[len(deps):]

        @pl.when(pl.program_id(0) == 0)
        def _():
            st_ref[...] = jnp.zeros_like(st_ref)

        s = st_ref[...]
        for c in range(CHUNKS_PER_STEP):
            r = _chunk_rows(c)
            sp_ref[:, c] = s
            o, s = _gla_chunk(q_ref[:, r], k_ref[:, r], _heads(v_ref, r), la_ref[:, r], s)
            _put_heads(o_ref, o, r)
        st_ref[...] = s

    return pl.pallas_call(
        body, name="gla_fwd", grid=(nb,),
        in_specs=[hm, hm, tm, hm] + [ANY] * len(deps), out_specs=[tm, st],
        out_shape=[jax.ShapeDtypeStruct((t, HEADS * HEAD_W), F32), jax.ShapeDtypeStruct((HEADS, nc, HEAD_W, GLA_DK), F32)],
        scratch_shapes=[pltpu.VMEM((HEADS, HEAD_W, GLA_DK), F32)],
        compiler_params=_cparams("arbitrary"),
    )(q, k, v, la, *deps)


def _gla_bwd(q, k, v, la, sp, do):
    t = v.shape[0]
    nc = t // CHUNK
    nb, hm, tm, st = _gla_specs(nc, True)

    def body(q_ref, k_ref, v_ref, la_ref, sp_ref, do_ref, dq_ref, dk_ref, dv_ref, dla_ref, ds_ref):
        @pl.when(pl.program_id(0) == 0)
        def _():
            ds_ref[...] = jnp.zeros_like(ds_ref)

        ds = ds_ref[...]
        for c in reversed(range(CHUNKS_PER_STEP)):
            r = _chunk_rows(c)
            _, vjp = jax.vjp(_gla_chunk, q_ref[:, r], k_ref[:, r], _heads(v_ref, r), la_ref[:, r], sp_ref[:, c])
            dq, dk, dv, dla, ds = vjp((_heads(do_ref, r), ds))
            dq_ref[:, r] = dq.astype(dq_ref.dtype)
            dk_ref[:, r] = dk.astype(dk_ref.dtype)
            _put_heads(dv_ref, dv, r)
            dla_ref[:, r] = dla
        ds_ref[...] = ds

    hm_shape = jax.ShapeDtypeStruct((HEADS, t, GLA_DK), BF16)
    return pl.pallas_call(
        body, name="gla_bwd", grid=(nb,),
        in_specs=[hm, hm, tm, hm, st, tm], out_specs=[hm, hm, tm, hm],
        out_shape=[hm_shape, hm_shape, jax.ShapeDtypeStruct((t, HEADS * HEAD_W), BF16),
                   jax.ShapeDtypeStruct((HEADS, t, GLA_DK), F32)],
        scratch_shapes=[pltpu.VMEM((HEADS, HEAD_W, GLA_DK), F32)],
        compiler_params=_cparams("arbitrary"),
    )(q, k, v, la, sp, do)


def _ml_chunk(q, k, v, li_r, lf_r, cm, nv, m):
    c = q.shape[-2]
    row = lax.broadcasted_iota(jnp.int32, (c, c), 0)
    col = lax.broadcasted_iota(jnp.int32, (c, c), 1)
    eye = (row == col).astype(F32)
    li_c = jnp.sum(eye * li_r, axis=-1, keepdims=True)
    lf_c = jnp.sum(eye * lf_r, axis=-1, keepdims=True)
    fc_c = jnp.sum((col <= row).astype(F32) * lf_r, axis=-1, keepdims=True)
    fc_r = jnp.sum((row <= col).astype(F32) * lf_c, axis=-2, keepdims=True)
    f_last = jnp.sum(lf_r, axis=-1, keepdims=True)
    kc = k * (HEAD_W ** -0.5)
    a_c = f_last - fc_c + li_c
    m_loc = jnp.max(a_c, axis=-2, keepdims=True)
    kw = kc * jnp.exp(a_c - m_loc)
    c_chunk = _bdot(kw, v, "tn")
    n_chunk = jnp.sum(kw, axis=-2, keepdims=True)
    m_new = jnp.maximum(f_last + m, m_loc)
    sp = jnp.exp(f_last + m - m_new)
    sl = jnp.exp(m_loc - m_new)
    cm_new = sp * cm + sl * c_chunk
    nv_new = sp * nv + sl * n_chunk
    log_d = li_r - _abs(fc_c - fc_r)
    g_inter = fc_c + m
    m_t = jnp.maximum(g_inter, jnp.max(log_d, axis=-1, keepdims=True))
    s = _bdot(q, kc, "nt") * jnp.exp(log_d - m_t)
    sc = jnp.exp(g_inter - m_t)
    num = _bdot(s, v, "nn") + sc * _bdot(q, cm, "nn")
    den = jnp.sum(s, axis=-1, keepdims=True) + sc * jnp.sum(q * nv, axis=-1, keepdims=True)
    den = jnp.maximum(_abs(den), jnp.exp(-m_t))
    return num / den, cm_new, nv_new, m_new


def _ml_specs(nc, rev):
    nb = nc // CHUNKS_PER_STEP

    def blk(n):
        return (nb - 1 - n) if rev else n
    tm = pl.BlockSpec((CHUNKS_PER_STEP * CHUNK, HEADS * HEAD_W), lambda n: (blk(n), 0))
    gate = pl.BlockSpec((HEADS, CHUNKS_PER_STEP, 1, CHUNK), lambda n: (0, blk(n), 0, 0))
    cm = pl.BlockSpec((HEADS, CHUNKS_PER_STEP, HEAD_W, HEAD_W), lambda n: (0, blk(n), 0, 0))
    vec = pl.BlockSpec((HEADS, CHUNKS_PER_STEP, 1, HEAD_W), lambda n: (0, blk(n), 0, 0))
    return nb, tm, gate, cm, vec


_ML_STATE = [pltpu.VMEM((HEADS, HEAD_W, HEAD_W), F32), pltpu.VMEM((HEADS, 1, HEAD_W), F32), pltpu.VMEM((HEADS, 1, HEAD_W), F32)]


def _ml_fwd(q, k, v, li, lf):
    t = q.shape[0]
    nc = t // CHUNK
    nb, tm, gate, cm, vec = _ml_specs(nc, False)

    def body(q_ref, k_ref, v_ref, li_ref, lf_ref, hc_ref, cp_ref, np_ref, mp_ref, c_ref, n_ref, m_ref):
        @pl.when(pl.program_id(0) == 0)
        def _():
            c_ref[...] = jnp.zeros_like(c_ref)
            n_ref[...] = jnp.zeros_like(n_ref)
            m_ref[...] = jnp.zeros_like(m_ref)

        cs, ns, ms = c_ref[...], n_ref[...], m_ref[...][:, :, 0:1]
        for c in range(CHUNKS_PER_STEP):
            r = _chunk_rows(c)
            cp_ref[:, c] = cs
            np_ref[:, c] = ns
            mp_ref[:, c] = jnp.broadcast_to(ms, m_ref.shape)
            hc, cs, ns, ms = _ml_chunk(_heads(q_ref, r), _heads(k_ref, r), _heads(v_ref, r), li_ref[:, c], lf_ref[:, c],
                                       cs, ns, ms)
            _put_heads(hc_ref, hc, r)
        c_ref[...] = cs
        n_ref[...] = ns
        m_ref[...] = jnp.broadcast_to(ms, m_ref.shape)

    return pl.pallas_call(
        body, name="mlstm_fwd", grid=(nb,),
        in_specs=[tm, tm, tm, gate, gate], out_specs=[tm, cm, vec, vec],
        out_shape=[jax.ShapeDtypeStruct((t, HEADS * HEAD_W), F32), jax.ShapeDtypeStruct((HEADS, nc, HEAD_W, HEAD_W), F32),
                   jax.ShapeDtypeStruct((HEADS, nc, 1, HEAD_W), F32), jax.ShapeDtypeStruct((HEADS, nc, 1, HEAD_W), F32)],
        scratch_shapes=_ML_STATE,
        compiler_params=_cparams("arbitrary"),
    )(q, k, v, li, lf)


def _ml_bwd(q, k, v, li, lf, cp, npv, mp, dhc):
    t = q.shape[0]
    nc = t // CHUNK
    nb, tm, gate, cm, vec = _ml_specs(nc, True)

    def body(q_ref, k_ref, v_ref, li_ref, lf_ref, cp_ref, np_ref, mp_ref, dhc_ref,
             dq_ref, dk_ref, dv_ref, dli_ref, dlf_ref, dc_ref, dn_ref, dm_ref):
        @pl.when(pl.program_id(0) == 0)
        def _():
            dc_ref[...] = jnp.zeros_like(dc_ref)
            dn_ref[...] = jnp.zeros_like(dn_ref)
            dm_ref[...] = jnp.zeros_like(dm_ref)

        dc, dn, dm = dc_ref[...], dn_ref[...], dm_ref[...][:, :, 0:1]
        for c in reversed(range(CHUNKS_PER_STEP)):
            r = _chunk_rows(c)
            _, vjp = jax.vjp(_ml_chunk, _heads(q_ref, r), _heads(k_ref, r), _heads(v_ref, r), li_ref[:, c], lf_ref[:, c],
                             cp_ref[:, c], np_ref[:, c], mp_ref[:, c][:, :, 0:1])
            dq, dk, dv, dli, dlf, dc, dn, dm = vjp((_heads(dhc_ref, r), dc, dn, dm))
            _put_heads(dq_ref, dq, r)
            _put_heads(dk_ref, dk, r)
            _put_heads(dv_ref, dv, r)
            dli_ref[:, c] = dli
            dlf_ref[:, c] = dlf
        dc_ref[...] = dc
        dn_ref[...] = dn
        dm_ref[...] = jnp.broadcast_to(dm, dm_ref.shape)

    tm_shape = jax.ShapeDtypeStruct((t, HEADS * HEAD_W), F32)
    gate_shape = jax.ShapeDtypeStruct((HEADS, nc, 1, CHUNK), F32)
    return pl.pallas_call(
        body, name="mlstm_bwd", grid=(nb,),
        in_specs=[tm, tm, tm, gate, gate, cm, vec, vec, tm], out_specs=[tm, tm, tm, gate, gate],
        out_shape=[tm_shape, tm_shape, tm_shape, gate_shape, gate_shape],
        scratch_shapes=_ML_STATE,
        compiler_params=_cparams("arbitrary"),
    )(q, k, v, li, lf, cp, npv, mp, dhc)


@jax.custom_vjp
def _bdot_diag(x, w):
    b = w.shape[1]
    return jnp.concatenate([_raw_dot(x[:, :b], w[0], "nn"), _raw_dot(x[:, b:], w[1], "nn")], axis=1)


def _bdot_diag_fwd(x, w):
    return _bdot_diag(x, w), (x, w)


def _bdot_diag_bwd(res, ct):
    x, w = res
    b = w.shape[1]
    dx = jnp.concatenate([_raw_dot(ct[:, :b], w[0], "nt"), _raw_dot(ct[:, b:], w[1], "nt")], axis=1)
    dw = jnp.stack([_raw_dot(x[:, :b], ct[:, :b], "tn"), _raw_dot(x[:, b:], ct[:, b:], "tn")])
    return dx.astype(x.dtype), dw.astype(w.dtype)


_bdot_diag.defvjp(_bdot_diag_fwd, _bdot_diag_bwd)


def _ml_pre(s0, s1, s2, s3, cw0, cw1, cw2, cw3, cb, wq, wk, wv, wiq, wik, wiv, bif):
    pre = cb + cw0 * s0 + cw1 * s1 + cw2 * s2 + cw3 * s3
    xc = pre * _sigmoid(pre)
    q = _bdot_diag(xc, wq)
    k = _bdot_diag(xc, wk)
    v = _bdot_diag(s3, wv)
    gates = _bdot(q, wiq, "nn") + _bdot(k, wik, "nn") + _bdot(v, wiv, "nn") + bif
    lane = lax.broadcasted_iota(jnp.int32, gates.shape, 1)
    gl = jnp.where(lane < HEADS, gates, _log_sigmoid(gates))
    return xc, q, k, v, gl


def _delayed(xs_ref, x_ref, halo_ref, r):
    xs_ref[0:HALO, :] = halo_ref[...]
    xs_ref[HALO:HALO + r, :] = x_ref[...]
    return [xs_ref[pl.ds(HALO - (CONV_K - 1) + j, r), :] for j in range(CONV_K)]


def _full_spec(shape):
    return pl.BlockSpec(shape, lambda i, nd=len(shape): (0,) * nd)


def _ml_pre_fwd(x_m, x_pad, params, tile=256, deps=()):
    t, w = x_m.shape
    r = min(tile, t)

    def body(*refs):
        x_ref, halo_ref = refs[:2]
        p = [ref[...] for ref in refs[2:2 + len(params)]]
        outs = refs[2 + len(params) + len(deps):-1]
        res = _ml_pre(*_delayed(refs[-1], x_ref, halo_ref, r), *p)
        for ref, val in zip(outs, res):
            ref[...] = val

    row = pl.BlockSpec((r, w), lambda i: (i, 0))
    return pl.pallas_call(
        body, name="ml_pre_fwd", grid=(t // r,),
        in_specs=[row, pl.BlockSpec((HALO, w), lambda i: (i * (r // HALO), 0))] + [_full_spec(p.shape) for p in params]
        + [ANY] * len(deps),
        out_specs=[row] * 4 + [pl.BlockSpec((r, LANES), lambda i: (i, 0))],
        out_shape=[jax.ShapeDtypeStruct((t, w), F32)] * 4 + [jax.ShapeDtypeStruct((t, LANES), F32)],
        scratch_shapes=[pltpu.VMEM((r + HALO, w), F32)],
        compiler_params=_cparams("arbitrary"),
    )(x_m, x_pad, *params, *deps)


def _ml_pre_bwd(x_m, x_pad, params, cts, tile=256):
    t, w = x_m.shape
    r = min(tile, t)
    nt = t // r
    n_p = len(params)

    def body(*refs):
        x_ref, halo_ref = refs[:2]
        p = [ref[...] for ref in refs[2:2 + n_p]]
        ct = [ref[...] for ref in refs[2 + n_p:7 + n_p]]
        dx_ref = refs[7 + n_p]
        dp_refs = refs[8 + n_p:8 + 2 * n_p]
        xs_ref, ds_ref, carry_ref = refs[8 + 2 * n_p:]
        step = pl.program_id(0)

        @pl.when(step == 0)
        def _():
            ds_ref[...] = jnp.zeros_like(ds_ref)
            carry_ref[...] = jnp.zeros_like(carry_ref)

        _, vjp = jax.vjp(_ml_pre, *_delayed(xs_ref, x_ref, halo_ref, r), *p)
        grads = vjp(tuple(ct))
        for j in range(CONV_K):
            ds_ref[j, HALO:HALO + r, :] = grads[j]
        lead = HALO + CONV_K - 1
        d_tile = sum(ds_ref[j, pl.ds(lead - j, r), :] for j in range(CONV_K))
        d_halo = sum(ds_ref[j, pl.ds(CONV_K - 1 - j, HALO), :] for j in range(CONV_K))
        dx_ref[...] = jnp.concatenate([d_tile[:r - HALO], d_tile[r - HALO:] + carry_ref[...]], axis=0).astype(dx_ref.dtype)
        carry_ref[...] = d_halo
        _accumulate(step, dp_refs, grads[CONV_K:])

    row = pl.BlockSpec((r, w), lambda i: (nt - 1 - i, 0))
    return pl.pallas_call(
        body, name="ml_pre_bwd", grid=(nt,),
        in_specs=[row, pl.BlockSpec((HALO, w), lambda i: ((nt - 1 - i) * (r // HALO), 0))] + [_full_spec(p.shape) for p in params]
        + [row] * 4 + [pl.BlockSpec((r, LANES), lambda i: (nt - 1 - i, 0))],
        out_specs=[row] + [_full_spec(p.shape) for p in params],
        out_shape=[jax.ShapeDtypeStruct((t, w), BF16)] + [jax.ShapeDtypeStruct(p.shape, F32) for p in params],
        scratch_shapes=[pltpu.VMEM((r + HALO, w), F32), pltpu.VMEM((CONV_K, r + 2 * HALO, w), F32), pltpu.VMEM((HALO, w), F32)],
        compiler_params=_cparams("arbitrary"),
    )(x_m, x_pad, *params, *cts)


def _per_head(fn, row_vals, head_params, shared_params=()):
    return [fn(*[a[:, hs] for a in row_vals], *[p[:, hs] for p in head_params], *shared_params) for hs in _head_slices(HEAD_W)]


def _gla_out(o, g, gn):
    return _rms(o, gn) * (g * _sigmoid(g))


def _ml_out(hc, op, xc, g, sk):
    hcell = hc * _sigmoid(op)
    mu = jnp.mean(hcell, axis=-1, keepdims=True)
    d = hcell - mu
    var = jnp.mean(d * d, axis=-1, keepdims=True)
    return d * lax.rsqrt(var + EPS) * g + sk * xc


def _log_decay(al, w, b):
    return _log_sigmoid(_bdot(al, w, "nn") + b) * (1.0 / GLA_GATE_NORM)


def _merge(ga, gb, ya, yb):
    ga, gb, ya, yb = (a.astype(F32) for a in (ga, gb, ya, yb))
    return _sigmoid(ga) * ya + _sigmoid(gb) * yb


def _post_mix(x, z, gpm, gpl):
    x1 = x + _rms(z, gpm)
    return x1, _rms(x1, gpl)


def _loss_rows(x1, dn, tgt, g):
    e = x1 + _rms(dn, g) - tgt
    return 0.5 * jnp.sum(jnp.mean(e * e, axis=-1, keepdims=True), axis=0, keepdims=True)


def _lin(p):
    return 4 * p[0] + 2 * p[1] + p[2]


def _me():
    return lax.axis_index("x"), lax.axis_index("y"), lax.axis_index("c")


def _flip(p, k):
    return tuple((1 - v) if (k >> (2 - i)) & 1 else v for i, v in enumerate(p))


ANY = pl.BlockSpec(memory_space=pl.ANY)


HBM = pl.BlockSpec(memory_space=pltpu.HBM)
SEM = pl.BlockSpec(memory_space=pltpu.SEMAPHORE)
DATAFLOW = pltpu.SideEffectType.DATAFLOW_SIDE_EFFECTING


SIBLING = 1
OTHER_CHIPS = (2, 4, 6)


def _peer_copies(kinds, srcs, lands, send_sems, recv_sems):
    me = _me()
    copies = []
    for a, (kind, src, land) in enumerate(zip(kinds, srcs, lands)):
        masks = {"gather": range(1, N_DEV), "exchange": range(1, N_DEV), "gather_chips": (SIBLING, *OTHER_CHIPS),
                 "gather_pass": OTHER_CHIPS}[kind]
        for k in masks:
            peer = _flip(me, k)
            if kind == "gather_pass":
                block = land.at[_lin(peer)]
                src_ref, dst_ref, target = block, block, _flip(me, SIBLING)
            else:
                src_ref, dst_ref, target = (src.at[_lin(peer)] if kind == "exchange" else src), land.at[_lin(me)], peer
            copies.append(pltpu.make_async_remote_copy(
                src_ref=src_ref, dst_ref=dst_ref, send_sem=send_sems.at[a * 7 + k - 1], recv_sem=recv_sems.at[a * 7 + k - 1],
                device_id=target, device_id_type=MESH))
    return copies


def _copies_start(kind, srcs, name, after=None, lands=None):
    n = len(srcs)
    extra = [] if after is None else [after]
    kind = [kind] * n if isinstance(kind, str) else list(kind)
    land_shapes = [(s.shape if k == "exchange" else (N_DEV, *s.shape)) for k, s in zip(kind, srcs)]
    lands = [lax.empty(ls, s.dtype) for ls, s in zip(land_shapes, srcs)] if lands is None else lands

    def body(*refs):
        sems = refs[2 * n + len(extra):]
        for cp in _peer_copies(kind, refs[:n], refs[n:2 * n], sems[0], sems[1]):
            cp.start()
        refs[-1][...] = jnp.zeros_like(refs[-1])

    def hbm(a):
        return pltpu.with_memory_space_constraint(a, pltpu.HBM)

    out = pl.pallas_call(
        body, name=name,
        out_shape=(pltpu.SemaphoreType.DMA((7 * n,)), pltpu.SemaphoreType.DMA((7 * n,)),
                   *[pltpu.HBM(s.shape, s.dtype) for s in srcs],
                   *[pltpu.HBM(ls, s.dtype) for ls, s in zip(land_shapes, srcs)],
                   jax.ShapeDtypeStruct((8, LANES), F32)),
        in_specs=[HBM] * (2 * n) + [ANY] * len(extra),
        out_specs=(SEM, SEM, *[HBM] * (2 * n), pl.BlockSpec(memory_space=pltpu.VMEM)),
        input_output_aliases={i: 2 + i for i in range(2 * n)},
        compiler_params=pltpu.CompilerParams(has_side_effects=DATAFLOW),
    )(*[hbm(s) for s in srcs], *[hbm(a) for a in lands], *extra)
    return (kind, n, out[:-1]), out[-1]


def _copies_wait(state, after, name):
    kind, n, (send_sems, recv_sems, *thru) = state
    after = list(after) if isinstance(after, (list, tuple)) else [after]

    def body(*refs):
        for cp in _peer_copies(kind, refs[:n], refs[n:2 * n], refs[2 * n], refs[2 * n + 1]):
            cp.wait_send()
            cp.wait_recv()

    out = pl.pallas_call(
        body, name=name,
        out_shape=tuple(pltpu.HBM(t.shape, t.dtype) for t in thru),
        in_specs=[HBM] * (2 * n) + [SEM, SEM] + [ANY] * len(after), out_specs=tuple([HBM] * (2 * n)),
        input_output_aliases={i: i for i in range(2 * n)},
        compiler_params=pltpu.CompilerParams(has_side_effects=DATAFLOW),
    )(*thru, send_sems, recv_sems, *after)
    return out[:n], out[n:]


def _adamw(w, g, m, v):
    m2 = ADAM_B1 * m + (1.0 - ADAM_B1) * g
    v2 = ADAM_B2 * v + (1.0 - ADAM_B2) * (g * g)
    m_hat = m2 / (1.0 - ADAM_B1 ** ADAM_STEP)
    v_hat = v2 / (1.0 - ADAM_B2 ** ADAM_STEP)
    delta = -ADAM_LR * (m_hat / (jnp.sqrt(v_hat) + ADAM_EPS) + ADAM_WD * w)
    return delta, m2, v2


def _sum_adamw(lands, parts, me_idx, w, m, v, name, tile=256):
    r, c = w.shape
    nchunks = len(lands)
    tr = min(tile, r // nchunks)
    per_chunk = r // nchunks // tr
    per = 1 + N_DEV

    def body(me_ref, *refs):
        w_ref, m_ref, v_ref, g_ref, d_ref, m2_ref, v2_ref = refs[nchunks * per:]
        for k in range(nchunks):
            own_ref, slots = refs[k * per], refs[k * per + 1:(k + 1) * per]

            @pl.when(pl.program_id(0) // per_chunk == k)
            def _(own_ref=own_ref, slots=slots):
                own = own_ref[...].astype(F32)
                g = None
                for s in range(N_DEV):
                    term = jnp.where(me_ref[0] == s, own, slots[s][...].astype(F32))
                    g = term if g is None else g + term
                d, m2, v2 = _adamw(w_ref[...], g, m_ref[...], v_ref[...])
                g_ref[...] = g
                d_ref[...] = d
                m2_ref[...] = m2
                v2_ref[...] = v2

    def chunk_specs(k):
        def tile_of(i):
            return jnp.clip(i - k * per_chunk, 0, per_chunk - 1)

        def slot_spec(s):
            return pl.BlockSpec((None, tr, c), lambda i, me: (jnp.where(me[0] == s, (s + 1) % N_DEV, s), tile_of(i), 0))
        return [pl.BlockSpec((None, tr, c), lambda i, me: (me[0], tile_of(i), 0))] + [slot_spec(s) for s in range(N_DEV)]

    row = pl.BlockSpec((tr, c), lambda i, me: (i, 0))
    operands = [a for land, part in zip(lands, parts) for a in (part, *[land] * N_DEV)]
    return pl.pallas_call(
        body, name=name,
        grid_spec=pltpu.PrefetchScalarGridSpec(
            num_scalar_prefetch=1, grid=(r // tr,),
            in_specs=[s for k in range(nchunks) for s in chunk_specs(k)] + [row] * 3,
            out_specs=[row] * 4),
        out_shape=[jax.ShapeDtypeStruct((r, c), F32)] * 4,
        compiler_params=_cparams("parallel"),
    )(me_idx, *operands, w, m, v)


def _small_update(name, me_idx, kinds, lands, owns, ws, ms, vs, sums=()):
    n = len(ws)
    lands, owns = list(lands) + [s[0] for s in sums], list(owns) + [s[1] for s in sums]
    kinds = list(kinds) + ["gather"] * len(sums)
    nl = len(lands)

    def summed(me, land_ref, own):
        g = None
        for s in range(N_DEV):
            term = jnp.where(me == s, own, land_ref[s]).astype(F32)
            g = term if g is None else g + term
        return g

    def body(me_ref, *refs):
        land_refs, own_refs = refs[:nl], refs[nl:2 * nl]
        w_refs, m_refs, v_refs = (refs[2 * nl + i * n:2 * nl + (i + 1) * n] for i in range(3))
        outs = refs[2 * nl + 3 * n:]
        me = me_ref[0]
        for i in range(n):
            g = summed(me, land_refs[i], own_refs[i][...])
            d, m2, v2 = _adamw(w_refs[i][...], g, m_refs[i][...], v_refs[i][...])
            for ref, val in zip(outs[4 * i:4 * i + 4], (g, d, m2, v2)):
                ref[...] = val
        for i in range(n, nl):
            outs[4 * n + i - n][...] = summed(me, land_refs[i], own_refs[i][...])

    def whole(shape):
        return pl.BlockSpec(shape, lambda i, me, nd=len(shape): (0,) * nd)

    def own_spec(kind, own):
        if kind == "gather":
            return whole(own.shape)
        return pl.BlockSpec((None, *own.shape[1:]), lambda i, me: (me[0], 0, 0))

    shapes = [w.shape for w in ws]
    out_shapes = [s for s in shapes for _ in range(4)] + [s[1].shape for s in sums]
    return pl.pallas_call(
        body, name=name,
        grid_spec=pltpu.PrefetchScalarGridSpec(
            num_scalar_prefetch=1, grid=(1,),
            in_specs=[whole(a.shape) for a in lands] + [own_spec(k, o) for k, o in zip(kinds, owns)]
            + [whole(s) for s in shapes] * 3,
            out_specs=[whole(s) for s in out_shapes]),
        out_shape=[jax.ShapeDtypeStruct(s, F32) for s in out_shapes],
        compiler_params=_cparams("arbitrary"),
    )(me_idx, *lands, *owns, *ws, *ms, *vs)


def _small_view(n, a):
    if a.ndim == 1:
        return a.reshape(1, -1)
    if a.ndim == 3:
        return a.transpose(1, 2, 0).reshape(QKV_BLOCK * QKV_BLOCK, -1)
    return a.T if n == "w_if" else a


def _small_unview(n, a, shape):
    if len(shape) == 1:
        return a.reshape(shape)
    if len(shape) == 3:
        return a.reshape(QKV_BLOCK, QKV_BLOCK, -1).transpose(2, 0, 1)
    return a.T if n == "w_if" else a


def _small_shards(n, g):
    if n == "w_if":
        return g.reshape(N_DEV, -1, g.shape[1]).transpose(0, 2, 1)
    return g.reshape(g.shape[0], N_DEV, -1).transpose(1, 0, 2)


def _small_unshard(n, s):
    if n == "w_if":
        return s.transpose(0, 2, 1).reshape(-1, s.shape[1])
    return s.transpose(1, 0, 2).reshape(s.shape[1], -1)


def _to_hm(a, d):
    t = a.shape[0]
    return a.reshape(t, HEADS, d).transpose(1, 0, 2)


def _from_hm(a):
    h, t, d = a.shape
    return a.transpose(1, 0, 2).reshape(t, h * d)


def _gate_rows(g):
    t = g.shape[0]
    return g.T.reshape(HEADS, t // CHUNK, 1, CHUNK)


def _gate_cols(g):
    h, nc, _, c = g.shape
    return g.reshape(h, nc * c).T


def _blockdiag_dense(w):
    n = w.shape[0] * QKV_BLOCK // 2
    tiled = jnp.tile(w.reshape(2, n, QKV_BLOCK), (1, 1, n // QKV_BLOCK))
    r = lax.broadcasted_iota(jnp.int32, (2, n, n), 1)
    c = lax.broadcasted_iota(jnp.int32, (2, n, n), 2)
    return jnp.where(r // QKV_BLOCK == c // QKV_BLOCK, tiled, 0.0)


def _blockdiag_blocks(dense):
    _, n, _ = dense[0].shape
    k = len(dense)

    def body(*refs):
        r = lax.broadcasted_iota(jnp.int32, (n, n), 0)
        c = lax.broadcasted_iota(jnp.int32, (n, n), 1)
        fr = lax.broadcasted_iota(jnp.int32, (n, LANES), 0)
        fc = lax.broadcasted_iota(jnp.int32, (n, LANES), 1)
        fold = ((fr & (QKV_BLOCK - 1)) == fc).astype(BF16)
        for i in range(k):
            for half in range(2):
                kept = jnp.where((r >> 2) == (c >> 2), refs[i][half], 0.0)
                refs[k + i][half] = sum(lax.dot_general(t, fold, _dims("nn", 2), preferred_element_type=F32)
                                        for t in _split3(kept))

    out = pl.pallas_call(body, name="blockdiag_blocks", out_shape=[jax.ShapeDtypeStruct((2, n, LANES), F32)] * k)(*dense)
    return [o[:, :, 0:QKV_BLOCK].reshape(2 * n // QKV_BLOCK, QKV_BLOCK, QKV_BLOCK) for o in out]


def _col_blocks(w):
    k, n = w.shape
    return w.reshape(k, N_DEV, n // N_DEV).transpose(1, 0, 2)


def _from_col_blocks(g):
    d, k, n = g.shape
    return g.transpose(1, 0, 2).reshape(k, d * n)


def _first_norm(x, g):
    return _rowwise("pre_mix_norm", lambda xv, gv: ((_rms(xv, gv),), ()), [x], [g], [(x.shape[1], BF16)])[0]


def _local_step(x, h, tgt, weight, ws, prefetch, pass_on, on_grads, on_small):
    t, d = x.shape
    g1 = ws["g_pre_mix"]

    def dep(token):
        return () if token is None else (token,)

    w_in = weight("w_in", x)
    fetch_mix = prefetch(("w_pa", "w_pb", "w_o"), w_in)

    n_in = w_in.shape[2]

    offs = [0]
    for s in IN_SPLITS:
        offs.append(offs[-1] + s)

    w_a_up_p = jnp.pad(ws["w_a_up"], ((0, LANES - LOWRANK), (0, 0)))
    b_a_up = ws["b_a_up"]

    def proj_in_fwd(hv, w, wa, ba):
        proj = jnp.concatenate([_raw_dot(hv, w[j], "nn") for j in range(N_DEV)], axis=1)
        parts = [proj[:, offs[i]:offs[i + 1]] for i in range(len(IN_SPLITS))]
        parts[4] = jnp.concatenate([parts[4], jnp.zeros((parts[4].shape[0], LANES - LOWRANK), F32)], axis=1)
        return (*parts, _log_decay(parts[4], wa, ba)), ()

    widths = [LANES if s == LOWRANK else s for s in IN_SPLITS]
    q_a, k_a, v_a, g_a, a_low_p, x_m, o_pre, gate_a, gate_b, la = _rowwise(
        "proj_in", proj_in_fwd, [h], [w_in, w_a_up_p, b_a_up],
        [(wd, BF16 if i == 2 else F32) for i, wd in enumerate(widths)] + [(HEADS * GLA_DK, F32)],
        deps=dep(fetch_mix))

    fetch_up = prefetch(("w_up", "w_down"), la)
    q_hm, k_hm, la_hm = _to_hm(q_a, GLA_DK), _to_hm(k_a, GLA_DK), _to_hm(la, GLA_DK)
    o_gla, s_prev = _gla_fwd(q_hm, k_hm, v_a, la_hm, deps=dep(fetch_up))
    pass_mix = pass_on("w_pa", o_gla)
    gn = ws["g_gla_norm"]
    ml_w = HEADS * HEAD_W

    cw = ws["conv_w"]
    w_if_p = jnp.pad(ws["w_if"], ((0, 0), (0, LANES - 2 * HEADS)))
    pre_params = [cw[0:1], cw[1:2], cw[2:3], cw[3:4], ws["conv_b"],
                  _blockdiag_dense(ws["w_q_ml"]), _blockdiag_dense(ws["w_k_ml"]), _blockdiag_dense(ws["w_v_ml"]),
                  w_if_p[0:ml_w], w_if_p[ml_w:2 * ml_w], w_if_p[2 * ml_w:3 * ml_w],
                  jnp.pad(ws["b_if"], ((0, 0), (0, LANES - 2 * HEADS)))]
    x_pad = jnp.pad(x_m, ((HALO, 0), (0, 0)))
    xc, q_m, k_m, v_m, gl = _ml_pre_fwd(x_m, x_pad, pre_params, tile=512, deps=dep(pass_mix))
    li, lf = _gate_rows(gl[:, 0:HEADS]), _gate_rows(gl[:, HEADS:2 * HEADS])
    hc, c_prev, n_prev, m_prev = _ml_fwd(q_m, k_m, v_m, li, lf)
    g_ml, skip = ws["g_ml_norm"], ws["ml_skip"]

    def proj_a_fwd(o, g, n_, w):
        ya = jnp.concatenate(_per_head(_gla_out, [o, g], [], [n_]), axis=1)
        return (ya, _raw_dot(ya, w, "nn")), ()

    ya_in, y_a = _rowwise("proj_a", proj_a_fwd, [o_gla, g_a], [gn, weight("w_pa", hc)], [(ml_w, BF16), (d, BF16)],
                          tile=512)

    def proj_b_fwd(a, b, c_, ga, gb, ya, g, s, w):
        hb = jnp.concatenate(_per_head(_ml_out, [a, b, c_], [g, s]), axis=1)
        yb = _raw_dot(hb, w, "nn")
        return (hb, yb, _merge(ga, gb, ya, yb)), ()

    h_b, y_b, merged = _rowwise("proj_b", proj_b_fwd, [hc, o_pre, xc, gate_a, gate_b, y_a], [g_ml, skip, weight("w_pb", hc)],
                                [(ml_w, BF16), (d, BF16), (d, BF16)], tile=512)

    gpm, gpl, gpo = ws["g_post_mix"], ws["g_pre_mlp"], ws["g_post_mlp"]

    def proj_o_fwd(mg, xv, w, a, b):
        zv = _raw_dot(mg, w, "nn")
        return (zv, *_post_mix(xv, zv, a, b)), ()

    pass_up = pass_on("w_up", merged)
    z, x1, h2 = _rowwise("proj_o", proj_o_fwd, [merged, x], [weight("w_o", merged), gpm, gpl],
                         [(d, F32), (d, F32), (d, BF16)], tile=512, deps=dep(pass_up))
    w_up = weight("w_up", h2)
    w_down = weight("w_down", h2)
    d_ff = w_down.shape[0]

    def mlp_loss(h2v, x1v, tgtv, wu, wd, g):
        upv = jnp.concatenate([_raw_dot(h2v, wu[j], "nn") for j in range(wu.shape[0])], axis=1)
        uv = jnp.square(jnp.maximum(upv, 0.0))
        dnv = _raw_dot(uv, wd, "nn")
        loss, vjp = jax.vjp(lambda a, b, c_: _loss_rows(a, b, tgtv, c_), x1v, dnv, g)
        dx1, ddn, dg = vjp(jnp.ones((1, 1), F32))
        return (upv, uv, dx1, ddn), (jnp.broadcast_to(loss, (1, LANES)), dg)

    up, u, dx1_y, d_dn, loss, d_gpo = _rowwise("mlp_loss", mlp_loss, [h2, x1, tgt], [w_up, w_down, gpo],
                                               [(d_ff, BF16), (d_ff, BF16), (d, F32), (d, BF16)],
                                               [((1, LANES), F32), ((1, d), F32)])

    dw_down = _mm(u, d_dn, "tn", BF16, "mlp_down_dw", tm=512)

    def mlp_dx(ddn, upv, xv, zv, dx1, wd, wu, a, b):
        dup = (_raw_dot(ddn, wd, "nt") * (2.0 * jnp.maximum(upv.astype(F32), 0.0))).astype(BF16)
        ns = wu.shape[2]
        dh2 = sum(_raw_dot(dup[:, j * ns:(j + 1) * ns], wu[j], "nt") for j in range(wu.shape[0]))
        _, vjp = jax.vjp(_post_mix, xv, zv, a, b)
        dx, dz, da, db = vjp((dx1, dh2))
        return (dup, dx, dz), (da, db)

    d_up, dx_res, d_z, d_gpm, d_gpl = _rowwise("mlp_dx", mlp_dx, [d_dn, up, x, z, dx1_y], [w_down, w_up, gpm, gpl],
                                               [(d_ff, BF16), (d, F32), (d, BF16)], [((1, d), F32), ((1, d), F32)])
    dw_up = _mm_shard_cols(h2, [d_up], [d_up.shape[1]], w_up.shape[2], "mlp_up_dw", 0, d)
    sent_mlp = on_grads(dict(w_down=dw_down, w_up=dw_up))

    def proj_o_dx(dz, ga, gb, ya, yb, w):
        return jax.vjp(_merge, ga, gb, ya, yb)[1](_raw_dot(dz, w, "nt")), ()

    d_ga, d_gb, d_ya, d_yb = _rowwise("proj_o_dx", proj_o_dx, [d_z, gate_a, gate_b, y_a, y_b], [weight("w_o", merged)],
                                      [(d, BF16)] * 4, tile=512, deps=dep(sent_mlp))
    dw_o, dw_pa, dw_pb = _mm_tn_whole([(merged, d_z), (ya_in, d_ya), (h_b, d_yb)], "mix_dw")
    sent_mix = on_grads(dict(w_o=dw_o, w_pa=dw_pa, w_pb=dw_pb))

    def proj_b_dx(dyb, a, b, c_, w, g, s):
        ct = _raw_dot(dyb, w, "nt")
        parts = []
        for hs in _head_slices(HEAD_W):
            _, vjp = jax.vjp(_ml_out, a[:, hs], b[:, hs], c_[:, hs], g[:, hs], s[:, hs])
            parts.append(vjp(ct[:, hs]))
        cat = lambda i: jnp.concatenate([p[i] for p in parts], axis=1)
        return (cat(0), cat(1), cat(2)), (cat(3), cat(4))

    d_hc, d_opre, d_xc, d_gml, d_skip = _rowwise("proj_b_dx", proj_b_dx, [d_yb, hc, o_pre, xc],
                                                 [weight("w_pb", hc), g_ml, skip], [(ml_w, F32), (ml_w, BF16), (ml_w, F32)],
                                                 [((1, ml_w), F32)] * 2,
                                                 tile=512, deps=dep(sent_mix))
    d_qm, d_km, d_vm, d_li, d_lf = _ml_bwd(q_m, k_m, v_m, li, lf, c_prev, n_prev, m_prev, d_hc)
    d_gl = jnp.concatenate([_gate_cols(d_li), _gate_cols(d_lf), jnp.zeros((t, LANES - 2 * HEADS), F32)], axis=1)
    pre_grads = _ml_pre_bwd(x_m, x_pad, pre_params, [d_xc, d_qm, d_km, d_vm, d_gl], tile=512)
    d_xm = pre_grads[0]
    d_cw = jnp.concatenate(pre_grads[1:5], axis=0)
    d_cb = pre_grads[5]
    d_wq, d_wk, d_wv = _blockdiag_blocks(pre_grads[6:9])
    d_wif = jnp.concatenate(pre_grads[9:12], axis=0)[:, 0:2 * HEADS]
    d_bif = pre_grads[12][:, 0:2 * HEADS]

    def proj_a_dx(dya, o, g, w, n_):
        ct = _raw_dot(dya, w, "nt")
        parts = []
        for hs in _head_slices(HEAD_W):
            _, vjp = jax.vjp(_gla_out, o[:, hs], g[:, hs], n_)
            parts.append(vjp(ct[:, hs]))
        cat = lambda i: jnp.concatenate([p[i] for p in parts], axis=1)
        return (cat(0), cat(1)), (sum(p[2] for p in parts),)

    d_o, d_g_a, d_gn = _rowwise("proj_a_dx", proj_a_dx, [d_ya, o_gla, g_a], [weight("w_pa", o_gla), gn],
                                [(ml_w, F32), (ml_w, BF16)],
                                [((1, HEAD_W), F32)], tile=512)
    dq_hm, dk_hm, d_va, dla_hm = _gla_bwd(q_hm, k_hm, v_a, la_hm, s_prev, d_o)

    def decay_bwd(al, ct, w, b):
        _, vjp = jax.vjp(_log_decay, al, w, b)
        dal, dw, db = vjp(ct)
        return (dal,), (dw, db)

    d_alow_p, d_wa_p, d_ba = _rowwise("gla_decay_bwd", decay_bwd, [a_low_p, _from_hm(dla_hm)], [w_a_up_p, b_a_up],
                                      [(LANES, BF16)], [(w_a_up_p.shape, F32), (b_a_up.shape, F32)])
    d_proj = [jnp.concatenate([_from_hm(dq_hm), _from_hm(dk_hm), d_va, d_g_a], axis=1), d_alow_p,
              jnp.concatenate([d_xm, d_opre, d_ga, d_gb], axis=1)]
    d_widths = [offs[4], LOWRANK, offs[9] - offs[5]]
    d_pieces = _shard_pieces(d_widths, n_in)
    small = dict(w_a_up=d_wa_p[0:LOWRANK], b_a_up=d_ba, g_gla_norm=d_gn, conv_w=d_cw, conv_b=d_cb,
                 w_q_ml=d_wq, w_k_ml=d_wk, w_v_ml=d_wv, w_if=d_wif, b_if=d_bif, ml_skip=d_skip, g_ml_norm=d_gml,
                 g_post_mix=d_gpm, g_pre_mlp=d_gpl, g_post_mlp=d_gpo)
    sent_small = on_small(small, loss)
    sent_in = sent_small
    for half in range(2):
        dw_half = _mm_shard_cols(h, d_proj, d_widths, n_in, "proj_in_dw_%d" % half, half, d // 2, deps=dep(sent_in))
        sent_in = on_grads({"w_in#%d" % half: dw_half})

    def proj_in_dx(dp_a, dp_low, dp_b, xv, dres, w, g):
        dh = 0.0
        for s in range(N_DEV):
            for i, c_in, c_w, wd in d_pieces[s]:
                src = (dp_a, dp_low, dp_b)[i]
                cols = src.shape[1] - c_in if wd < LANES else wd
                dh = dh + _raw_dot(src[:, c_in:c_in + cols], w[s][:, c_w:c_w + cols], "nt")
        _, vjp = jax.vjp(_rms, xv, g)
        dx, dg = vjp(dh)
        return (dx + dres,), (dg,)

    grad_x, d_g1 = _rowwise("proj_in_dx", proj_in_dx, [*d_proj, x, dx_res], [w_in, g1], [(d, F32)], [((1, d), F32)],
                            deps=dep(sent_in))
    return grad_x, on_small(dict(g_pre_mix=d_g1), None)


BIG = ("w_in", "w_pa", "w_pb", "w_o", "w_up", "w_down")
MIX = ("w_o", "w_pa", "w_pb")
BIG_COL_SHARDED = ("w_in", "w_pa", "w_pb", "w_up")
SMALL_SHARDED = ("w_a_up", "conv_w", "w_if")
SMALL = ("g_pre_mix", "w_a_up", "b_a_up", "g_gla_norm", "conv_w", "conv_b", "w_q_ml", "w_k_ml", "w_v_ml", "w_if", "b_if",
         "ml_skip", "g_ml_norm", "g_post_mix", "g_pre_mlp", "g_post_mlp")
WEIGHTS = ("g_pre_mix", "w_in", "w_a_up", "b_a_up", "g_gla_norm", "conv_w", "conv_b", "w_q_ml", "w_k_ml", "w_v_ml", "w_if", "b_if",
           "ml_skip", "g_ml_norm", "w_pa", "w_pb", "w_o", "g_post_mix", "g_pre_mlp", "w_up", "w_down", "g_post_mlp")


def kernel(x, g_pre_mix, w_in, w_a_up, b_a_up, g_gla_norm, conv_w, conv_b, w_q_ml, w_k_ml, w_v_ml, w_if, b_if, ml_skip, g_ml_norm, w_pa, w_pb, w_o, g_post_mix, g_pre_mlp, w_up, w_down, g_post_mlp, loss_target, m_g_pre_mix, m_w_in, m_w_a_up, m_b_a_up, m_g_gla_norm, m_conv_w, m_conv_b, m_w_q_ml, m_w_k_ml, m_w_v_ml, m_w_if, m_b_if, m_ml_skip, m_g_ml_norm, m_w_pa, m_w_pb, m_w_o, m_g_post_mix, m_g_pre_mlp, m_w_up, m_w_down, m_g_post_mlp, v_g_pre_mix, v_w_in, v_w_a_up, v_b_a_up, v_g_gla_norm, v_conv_w, v_conv_b, v_w_q_ml, v_w_k_ml, v_w_v_ml, v_w_if, v_b_if, v_ml_skip, v_g_ml_norm, v_w_pa, v_w_pb, v_w_o, v_g_post_mix, v_g_pre_mlp, v_w_up, v_w_down, v_g_post_mlp):
    args = dict(locals())
    w = {n: args[n][0] for n in WEIGHTS}
    m = {n: args["m_" + n][0] for n in WEIGHTS}
    v = {n: args["v_" + n][0] for n in WEIGHTS}

    me_lin = _lin(_me())
    me_idx = jnp.reshape(me_lin, (1,)).astype(jnp.int32)

    def full_weight(n, g):
        if n in ("w_in", "w_up"):
            return g
        return _from_col_blocks(g) if n in BIG_COL_SHARDED else g.reshape(-1, g.shape[-1])

    def grad_parts(n, g):
        if n.partition("#")[0] in ("w_in", "w_up"):
            return g
        return (_col_blocks(g) if n in BIG_COL_SHARDED else g.reshape(N_DEV, -1, g.shape[-1])).astype(BF16)

    sharded_names = tuple(SMALL_SHARDED)
    narrow = {n: w[n].astype(BF16) for n in BIG}
    ready, pending, passing = {}, {}, {}
    first_state, _ = _copies_start(["gather"] * len(sharded_names) + ["gather_chips"],
                                   [_small_view(n, w[n]) for n in sharded_names] + [narrow["w_in"]], "allgather_start_first")

    def prefetch(group, after):
        state, token = _copies_start("gather_chips", [narrow[n] for n in group], "allgather_start_" + group[0], after)
        for n in group:
            pending[n] = (group, state)
        return token

    def pass_on(n, after):
        group, state = pending[n]
        shards, lands = _copies_wait(state, after, "allgather_wait_" + group[0])
        state, token = _copies_start("gather_pass", shards, "allgather_pass_" + group[0], lands=lands)
        for gn in group:
            passing[gn] = (group, state)
        return token

    def weight(n, after):
        if n not in ready:
            group, state = passing[n]
            shards, lands = _copies_wait(state, after, "allgather_passed_" + group[0])
            for gn, shard, land in zip(group, shards, lands):
                ready[gn] = full_weight(gn, lax.dynamic_update_slice(land, shard[None], (me_lin, 0, 0)))
        return ready[n]

    h = _first_norm(x[0], w["g_pre_mix"].reshape(1, -1))
    first_own, first_lands = _copies_wait(first_state, [h] + [narrow[n] for n in BIG if n != "w_in"], "allgather_wait_first")
    state, _ = _copies_start("gather_pass", first_own[-1:], "allgather_pass_w_in", lands=first_lands[-1:])
    passing["w_in"] = (("w_in",), state)
    ws = {n: (w[n].reshape(1, -1) if w[n].ndim == 1 else w[n]) for n in SMALL if n not in SMALL_SHARDED}
    for n, own, land in zip(sharded_names, first_own, first_lands):
        ws[n] = _small_unshard(n, lax.dynamic_update_slice(land, own[None], (me_lin, 0, 0)))

    sets, waiting_small = [], []

    def start_set(large, small):
        names = tuple(large)
        s_names, s_kinds, s_srcs = small if small else ((), [], [])
        state, token = _copies_start(s_kinds + ["exchange"] * len(names), s_srcs + [grad_parts(n, large[n]) for n in names],
                                     "exchange_start_" + (names + s_names)[0].replace("#", "_"))
        sets.append((names, s_names, s_kinds, state))
        return token

    def on_grads(grads):
        return start_set(grads, waiting_small.pop() if waiting_small else None)

    def on_small(small, loss):
        names = tuple(small)
        kinds = ["exchange" if n in SMALL_SHARDED else "gather" for n in names]
        srcs = [_small_shards(n, small[n]) if n in SMALL_SHARDED else _small_view(n, small[n]) for n in names]
        if loss is None:
            return start_set({}, (names, kinds, srcs))
        waiting_small.append((names, kinds + ["gather"], srcs + [loss]))
        return None

    grad_x, last_token = _local_step(x[0], h, loss_target[0], weight, ws, prefetch, pass_on, on_grads, on_small)

    out, chunks, sums = {}, {}, []

    def finish_set(names, s_names, s_kinds, state, after):
        own, lands = _copies_wait(state, after, "exchange_wait_" + (names + s_names)[0].replace("#", "_"))
        ns = len(s_kinds)
        if s_names:
            k = len(s_names)
            upd = _small_update("adamw_small_" + s_names[0], me_idx, s_kinds[:k], lands[:k], own[:k],
                                *[[_small_view(n, d[n]) for n in s_names] for d in (w, m, v)],
                                sums=list(zip(lands[k:ns], own[k:ns])))
            for i, n in enumerate(s_names):
                out[n] = tuple(_small_unview(n, a, w[n].shape) for a in upd[4 * i:4 * i + 4])
            sums.extend(upd[4 * k:])
        if names == MIX:
            upd = _small_update("adamw_mix", me_idx, ["exchange"] * len(names), lands[ns:], own[ns:],
                                *[[d[n] for n in names] for d in (w, m, v)])
            for i, n in enumerate(names):
                out[n] = tuple(upd[4 * i:4 * i + 4])
            return
        for name, part, land in zip(names, own[ns:], lands[ns:]):
            n, _, chunk = name.partition("#")
            chunks.setdefault(n, []).append((land, part))
            if chunk in ("", "1"):
                got_lands, got_parts = zip(*chunks[n])
                out[n] = _sum_adamw(got_lands, got_parts, me_idx, w[n], m[n], v[n], "adamw_" + n)

    for entry in sets:
        finish_set(*entry, [grad_x, last_token] + [out[n][1] for n in BIG if n in out])
    loss_sum = sums[0]

    shaped = lambda a, n: a.reshape(args[n].shape)
    return (loss_sum[0, 0], grad_x[None],
            *[shaped(out[n][0], n) for n in WEIGHTS], *[shaped(out[n][1], n) for n in WEIGHTS],
            *[shaped(out[n][2], n) for n in WEIGHTS], *[shaped(out[n][3], n) for n in WEIGHTS])
```

```python
import functools

import jax
import jax.numpy as jnp
from jax import lax
from jax.experimental import pallas as pl
from jax.experimental.pallas import tpu as pltpu

F32 = jnp.float32
BF16 = jnp.bfloat16
MESH = pl.DeviceIdType.MESH

N_DEV = 8
EPS = 1e-6
CHUNK = 64
CHUNKS_PER_STEP = 8
HEADS = 4
GLA_DK = 64
HEAD_W = 128
GLA_GATE_NORM = 16.0
LOWRANK = 16
CONV_K = 4
QKV_BLOCK = 4
LANES = 128
HALO = 8
IN_SPLITS = (256, 256, 512, 512, 16, 512, 512, 1024, 1024)

ADAM_LR = 0.001
ADAM_B1 = 0.9
ADAM_B2 = 0.999
ADAM_EPS = 1e-08
ADAM_WD = 0.01
ADAM_STEP = 10

VMEM_LIMIT = 56 * 1024 * 1024


def _cparams(*sem):
    return pltpu.CompilerParams(dimension_semantics=sem, vmem_limit_bytes=VMEM_LIMIT)


def _dims(mode, ndim):
    contract = {"nn": ((ndim - 1,), (ndim - 2,)), "nt": ((ndim - 1,), (ndim - 1,)), "tn": ((ndim - 2,), (ndim - 2,))}[mode]
    return contract, (((0,), (0,)) if ndim == 3 else ((), ()))


def _raw_dot(a, b, mode):
    return lax.dot_general(a.astype(BF16), b.astype(BF16), _dims(mode, a.ndim), preferred_element_type=F32)


@functools.partial(jax.custom_vjp, nondiff_argnums=(2,))
def _bdot(a, b, mode):
    return _raw_dot(a, b, mode)


def _bdot_fwd(a, b, mode):
    return _raw_dot(a, b, mode), (a, b)


def _bdot_bwd(mode, res, ct):
    a, b = res
    if mode == "nn":
        da, db = _raw_dot(ct, b, "nt"), _raw_dot(a, ct, "tn")
    elif mode == "nt":
        da, db = _raw_dot(ct, b, "nn"), _raw_dot(ct, a, "tn")
    else:
        da, db = _raw_dot(b, ct, "nt"), _raw_dot(a, ct, "nn")
    return da.astype(a.dtype), db.astype(b.dtype)


_bdot.defvjp(_bdot_fwd, _bdot_bwd)


def _split3(x):
    hi = x.astype(BF16)
    r1 = x - hi.astype(F32)
    mid = r1.astype(BF16)
    return hi, mid, (r1 - mid.astype(F32)).astype(BF16)


def _split_dot(tri, x):
    if x.ndim == 3:
        tri = jnp.broadcast_to(tri, (x.shape[0], *tri.shape))
    return sum(lax.dot_general(tri, t, _dims("nn", x.ndim), preferred_element_type=F32) for t in _split3(x))


def _tri(n, lower):
    r = lax.broadcasted_iota(jnp.int32, (n, n), 0)
    c = lax.broadcasted_iota(jnp.int32, (n, n), 1)
    return ((c <= r) if lower else (c >= r)).astype(BF16)


@jax.custom_vjp
def _cumsum_rows(x):
    return _split_dot(_tri(x.shape[-2], True), x)


def _cumsum_rows_fwd(x):
    return _cumsum_rows(x), None


def _cumsum_rows_bwd(_, ct):
    return (_split_dot(_tri(ct.shape[-2], False), ct),)


_cumsum_rows.defvjp(_cumsum_rows_fwd, _cumsum_rows_bwd)


def _abs(x):
    return jnp.where(x >= 0, x, -x)


def _sigmoid(x):
    return lax.logistic(x)


def _log_sigmoid(x):
    return jnp.minimum(x, 0.0) - jnp.log(1.0 + jnp.exp(-_abs(x)))


def _rms(x, g):
    return x * lax.rsqrt(jnp.mean(x * x, axis=-1, keepdims=True) + EPS) * g


def _head_slices(w):
    return [slice(h * w, (h + 1) * w) for h in range(HEADS)]


def _heads(ref, rows=slice(None)):
    return jnp.stack([ref[rows, hs] for hs in _head_slices(HEAD_W)])


def _put_heads(ref, val, rows=slice(None)):
    for h, hs in enumerate(_head_slices(HEAD_W)):
        ref[rows, hs] = val[h].astype(ref.dtype)


def _tile(dim, want):
    if dim <= want or dim % LANES:
        return dim
    t = want
    while dim % t:
        t -= LANES
    return t


def _mm(a, b, mode, out_dtype, name, tm=1024, tn=1024, tk=4096, epilogue=None, extra=(), deps=(), shards=None):
    if shards == "b":
        assert mode == "nn"
        ns = b.shape[2]
        (m, k), (k2, n) = a.shape, (b.shape[1], b.shape[0] * ns)
        tn = ns
    elif mode == "nn":
        (m, k), (k2, n) = a.shape, b.shape
    elif mode == "nt":
        (m, k), (n, k2) = a.shape, b.shape
    else:
        (k, m), (k2, n) = a.shape, b.shape
    assert k == k2, (name, a.shape, b.shape)
    tm, tn, tk = _tile(m, tm), _tile(n, tn), _tile(k, tk)
    nk = k // tk
    out_dtypes = out_dtype if epilogue else (out_dtype,)
    assert nk == 1 or (out_dtype == F32 and not epilogue), name
    n_in = 2 + len(extra)

    def body(*refs):
        p = _raw_dot(refs[0][...], refs[1][...], mode)
        if nk > 1:
            _accumulate(pl.program_id(2), [refs[n_in + len(deps)]], [p])
            return
        outs = epilogue(p, *[r[...] for r in refs[2:n_in]]) if epilogue else (p,)
        for ref, val in zip(refs[n_in + len(deps):], outs):
            ref[...] = val.astype(ref.dtype)

    a_spec = pl.BlockSpec((tk, tm), lambda i, j, kk: (kk, i)) if mode == "tn" else pl.BlockSpec((tm, tk), lambda i, j, kk: (i, kk))
    if shards == "b":
        b_spec = pl.BlockSpec((None, tk, tn), lambda i, j, kk: (j, kk, 0))
    elif mode == "nt":
        b_spec = pl.BlockSpec((tn, tk), lambda i, j, kk: (j, kk))
    else:
        b_spec = pl.BlockSpec((tk, tn), lambda i, j, kk: (kk, j))
    o_spec = pl.BlockSpec((tm, tn), lambda i, j, kk: (i, j))
    res = pl.pallas_call(
        body, name=name, grid=(m // tm, n // tn, nk),
        in_specs=[a_spec, b_spec] + [o_spec] * len(extra) + [ANY] * len(deps), out_specs=[o_spec] * len(out_dtypes),
        out_shape=[jax.ShapeDtypeStruct((m, n), dt) for dt in out_dtypes],
        compiler_params=_cparams("parallel", "parallel", "arbitrary"),
    )(a, b, *extra, *deps)
    return res if epilogue else res[0]


def _mm_tn_whole(pairs, name):
    def body(*refs):
        for i in range(len(pairs)):
            refs[2 * len(pairs) + i][...] = _raw_dot(refs[2 * i][...], refs[2 * i + 1][...], "tn").astype(BF16)

    return pl.pallas_call(
        body, name=name,
        out_shape=[jax.ShapeDtypeStruct((a.shape[1], b.shape[1]), BF16) for a, b in pairs],
        compiler_params=pltpu.CompilerParams(vmem_limit_bytes=VMEM_LIMIT),
    )(*[x for pair in pairs for x in pair])


def _shard_pieces(widths, n):
    bounds = [0]
    for wd in widths:
        bounds.append(bounds[-1] + wd)
    assert bounds[-1] == N_DEV * n
    return [[(i, max(s * n, b) - b, max(s * n, b) - s * n, min((s + 1) * n, b + wd) - max(s * n, b))
             for i, (b, wd) in enumerate(zip(bounds, widths)) if b < (s + 1) * n and b + wd > s * n]
            for s in range(N_DEV)]


def _mm_shard_cols(a, bs, widths, n, name, row_tile, tm, deps=()):
    t = a.shape[0]
    nb = len(bs)
    pieces = _shard_pieces(widths, n)

    def body(a_ref, *rest):
        b_refs = rest[:nb]
        o_ref, at_ref = rest[nb + len(deps):]
        j = pl.program_id(0)

        @pl.when(j == 0)
        def _():
            at_ref[...] = a_ref[...].astype(BF16).T

        for s in range(N_DEV):
            @pl.when(j == s)
            def _(s=s):
                for i, c_in, c_out, wd in pieces[s]:
                    cols = min(_round_up(wd, LANES), bs[i].shape[1] - c_in) if wd < LANES else wd
                    p = _raw_dot(at_ref[...], b_refs[i][:, c_in:c_in + cols], "nn")
                    o_ref[:, c_out:c_out + wd] = p[:, 0:wd].astype(BF16)

    return pl.pallas_call(
        body, name=name, grid=(N_DEV,),
        in_specs=[pl.BlockSpec((t, tm), lambda j: (0, row_tile))]
        + [pl.BlockSpec(b.shape, lambda j: (0, 0), pipeline_mode=pl.Buffered(1)) for b in bs] + [ANY] * len(deps),
        out_specs=pl.BlockSpec((None, tm, n), lambda j: (j, 0, 0)),
        out_shape=jax.ShapeDtypeStruct((N_DEV, tm, n), BF16),
        scratch_shapes=[pltpu.VMEM((tm, t), BF16)],
        compiler_params=_cparams("arbitrary"),
    )(a, *bs, *deps)


def _round_up(v, m):
    return -(-v // m) * m


def _rowwise(name, fn, rows, params, out_rows, out_accs=(), tile=256, deps=()):
    t = rows[0].shape[0]
    r = min(tile, t)
    assert t % r == 0
    n_in, n_or = len(rows) + len(params), len(out_rows)
    n_all = n_in + len(deps)
    params = list(params) + list(deps)

    def body(*refs):
        vals = [ref[...] for ref in refs[:n_in]]
        outs = refs[n_all:]
        ro, ao = fn(*vals)
        for ref, v in zip(outs[:n_or], ro):
            ref[...] = v.astype(ref.dtype)
        if out_accs:
            _accumulate(pl.program_id(0), outs[n_or:], ao)

    def full(shape, **kw):
        return pl.BlockSpec(shape, lambda i, nd=len(shape): (0,) * nd, **kw)

    return pl.pallas_call(
        body, name=name, grid=(t // r,),
        in_specs=[pl.BlockSpec((r, a.shape[1]), lambda i: (i, 0)) for a in rows]
        + [full(p.shape, pipeline_mode=pl.Buffered(1)) for p in params],
        out_specs=[pl.BlockSpec((r, w), lambda i: (i, 0)) for w, _ in out_rows] + [full(s) for s, _ in out_accs],
        out_shape=[jax.ShapeDtypeStruct((t, w), dt) for w, dt in out_rows] + [jax.ShapeDtypeStruct(s, dt) for s, dt in out_accs],
        compiler_params=_cparams("arbitrary"),
    )(*rows, *params)


def _accumulate(step, refs, vals):
    for ref, v in zip(refs, vals):
        @pl.when(step == 0)
        def _(ref=ref, v=v):
            ref[...] = v.astype(ref.dtype)

        @pl.when(step > 0)
        def _(ref=ref, v=v):
            ref[...] += v.astype(ref.dtype)


def _gla_chunk(q, k, v, la, st):
    c = q.shape[-2]
    row = lax.broadcasted_iota(jnp.int32, (c, c), 0)
    col = lax.broadcasted_iota(jnp.int32, (c, c), 1)
    cum = _cumsum_rows(la)
    cl = jnp.sum(la, axis=-2, keepdims=True)
    ep = jnp.exp(cum)
    en = jnp.exp(-cum)
    qs = q * (GLA_DK ** -0.5)
    qp = qs * ep
    a_f = _bdot(qp, k * en, "nt")
    a_b = _bdot(qs * en, k * ep, "nt")
    sc = jnp.where(row >= col, a_f, a_b)
    o = _bdot(sc, v, "nn") + _bdot(qp, st, "nt")
    kd = k * jnp.exp(cl - cum)
    st_new = st * jnp.exp(cl) + _bdot(v, kd, "tn")
    return o, st_new


def _gla_specs(nc, rev):
    nb = nc // CHUNKS_PER_STEP
    rows = CHUNKS_PER_STEP * CHUNK

    def blk(n):
        return (nb - 1 - n) if rev else n
    hm = pl.BlockSpec((HEADS, rows, GLA_DK), lambda n: (0, blk(n), 0))
    tm = pl.BlockSpec((rows, HEADS * HEAD_W), lambda n: (blk(n), 0))
    st = pl.BlockSpec((HEADS, CHUNKS_PER_STEP, HEAD_W, GLA_DK), lambda n: (0, blk(n), 0, 0))
    return nb, hm, tm, st


def _chunk_rows(c):
    return slice(c * CHUNK, (c + 1) * CHUNK)


def _gla_fwd(q, k, v, la, deps=()):
    t = v.shape[0]
    nc = t // CHUNK
    nb, hm, tm, st = _gla_specs(nc, False)

    def body(q_ref, k_ref, v_ref, la_ref, *rest):
        o_ref, sp_ref, st_ref = rest[len(deps):]

        @pl.when(pl.program_id(0) == 0)
        def _():
            st_ref[...] = jnp.zeros_like(st_ref)

        s = st_ref[...]
        for c in range(CHUNKS_PER_STEP):
            r = _chunk_rows(c)
            sp_ref[:, c] = s
            o, s = _gla_chunk(q_ref[:, r], k_ref[:, r], _heads(v_ref, r), la_ref[:, r], s)
            _put_heads(o_ref, o, r)
        st_ref[...] = s

    return pl.pallas_call(
        body, name="gla_fwd", grid=(nb,),
        in_specs=[hm, hm, tm, hm] + [ANY] * len(deps), out_specs=[tm, st],
        out_shape=[jax.ShapeDtypeStruct((t, HEADS * HEAD_W), F32), jax.ShapeDtypeStruct((HEADS, nc, HEAD_W, GLA_DK), F32)],
        scratch_shapes=[pltpu.VMEM((HEADS, HEAD_W, GLA_DK), F32)],
        compiler_params=_cparams("arbitrary"),
    )(q, k, v, la, *deps)


def _gla_bwd(q, k, v, la, sp, do):
    t = v.shape[0]
    nc = t // CHUNK
    nb, hm, tm, st = _gla_specs(nc, True)

    def body(q_ref, k_ref, v_ref, la_ref, sp_ref, do_ref, dq_ref, dk_ref, dv_ref, dla_ref, ds_ref):
        @pl.when(pl.program_id(0) == 0)
        def _():
            ds_ref[...] = jnp.zeros_like(ds_ref)

        ds = ds_ref[...]
        for c in reversed(range(CHUNKS_PER_STEP)):
            r = _chunk_rows(c)
            _, vjp = jax.vjp(_gla_chunk, q_ref[:, r], k_ref[:, r], _heads(v_ref, r), la_ref[:, r], sp_ref[:, c])
            dq, dk, dv, dla, ds = vjp((_heads(do_ref, r), ds))
            dq_ref[:, r] = dq.astype(dq_ref.dtype)
            dk_ref[:, r] = dk.astype(dk_ref.dtype)
            _put_heads(dv_ref, dv, r)
            dla_ref[:, r] = dla
        ds_ref[...] = ds

    hm_shape = jax.ShapeDtypeStruct((HEADS, t, GLA_DK), BF16)
    return pl.pallas_call(
        body, name="gla_bwd", grid=(nb,),
        in_specs=[hm, hm, tm, hm, st, tm], out_specs=[hm, hm, tm, hm],
        out_shape=[hm_shape, hm_shape, jax.ShapeDtypeStruct((t, HEADS * HEAD_W), BF16),
                   jax.ShapeDtypeStruct((HEADS, t, GLA_DK), F32)],
        scratch_shapes=[pltpu.VMEM((HEADS, HEAD_W, GLA_DK), F32)],
        compiler_params=_cparams("arbitrary"),
    )(q, k, v, la, sp, do)


def _ml_chunk(q, k, v, li_r, lf_r, cm, nv, m):
    c = q.shape[-2]
    row = lax.broadcasted_iota(jnp.int32, (c, c), 0)
    col = lax.broadcasted_iota(jnp.int32, (c, c), 1)
    eye = (row == col).astype(F32)
    li_c = jnp.sum(eye * li_r, axis=-1, keepdims=True)
    lf_c = jnp.sum(eye * lf_r, axis=-1, keepdims=True)
    fc_c = jnp.sum((col <= row).astype(F32) * lf_r, axis=-1, keepdims=True)
    fc_r = jnp.sum((row <= col).astype(F32) * lf_c, axis=-2, keepdims=True)
    f_last = jnp.sum(lf_r, axis=-1, keepdims=True)
    kc = k * (HEAD_W ** -0.5)
    a_c = f_last - fc_c + li_c
    m_loc = jnp.max(a_c, axis=-2, keepdims=True)
    kw = kc * jnp.exp(a_c - m_loc)
    c_chunk = _bdot(kw, v, "tn")
    n_chunk = jnp.sum(kw, axis=-2, keepdims=True)
    m_new = jnp.maximum(f_last + m, m_loc)
    sp = jnp.exp(f_last + m - m_new)
    sl = jnp.exp(m_loc - m_new)
    cm_new = sp * cm + sl * c_chunk
    nv_new = sp * nv + sl * n_chunk
    log_d = li_r - _abs(fc_c - fc_r)
    g_inter = fc_c + m
    m_t = jnp.maximum(g_inter, jnp.max(log_d, axis=-1, keepdims=True))
    s = _bdot(q, kc, "nt") * jnp.exp(log_d - m_t)
    sc = jnp.exp(g_inter - m_t)
    num = _bdot(s, v, "nn") + sc * _bdot(q, cm, "nn")
    den = jnp.sum(s, axis=-1, keepdims=True) + sc * jnp.sum(q * nv, axis=-1, keepdims=True)
    den = jnp.maximum(_abs(den), jnp.exp(-m_t))
    return num / den, cm_new, nv_new, m_new


def _ml_specs(nc, rev):
    nb = nc // CHUNKS_PER_STEP

    def blk(n):
        return (nb - 1 - n) if rev else n
    tm = pl.BlockSpec((CHUNKS_PER_STEP * CHUNK, HEADS * HEAD_W), lambda n: (blk(n), 0))
    gate = pl.BlockSpec((HEADS, CHUNKS_PER_STEP, 1, CHUNK), lambda n: (0, blk(n), 0, 0))
    cm = pl.BlockSpec((HEADS, CHUNKS_PER_STEP, HEAD_W, HEAD_W), lambda n: (0, blk(n), 0, 0))
    vec = pl.BlockSpec((HEADS, CHUNKS_PER_STEP, 1, HEAD_W), lambda n: (0, blk(n), 0, 0))
    return nb, tm, gate, cm, vec


_ML_STATE = [pltpu.VMEM((HEADS, HEAD_W, HEAD_W), F32), pltpu.VMEM((HEADS, 1, HEAD_W), F32), pltpu.VMEM((HEADS, 1, HEAD_W), F32)]


def _ml_fwd(q, k, v, li, lf):
    t = q.shape[0]
    nc = t // CHUNK
    nb, tm, gate, cm, vec = _ml_specs(nc, False)

    def body(q_ref, k_ref, v_ref, li_ref, lf_ref, hc_ref, cp_ref, np_ref, mp_ref, c_ref, n_ref, m_ref):
        @pl.when(pl.program_id(0) == 0)
        def _():
            c_ref[...] = jnp.zeros_like(c_ref)
            n_ref[...] = jnp.zeros_like(n_ref)
            m_ref[...] = jnp.zeros_like(m_ref)

        cs, ns, ms = c_ref[...], n_ref[...], m_ref[...][:, :, 0:1]
        for c in range(CHUNKS_PER_STEP):
            r = _chunk_rows(c)
            cp_ref[:, c] = cs
            np_ref[:, c] = ns
            mp_ref[:, c] = jnp.broadcast_to(ms, m_ref.shape)
            hc, cs, ns, ms = _ml_chunk(_heads(q_ref, r), _heads(k_ref, r), _heads(v_ref, r), li_ref[:, c], lf_ref[:, c],
                                       cs, ns, ms)
            _put_heads(hc_ref, hc, r)
        c_ref[...] = cs
        n_ref[...] = ns
        m_ref[...] = jnp.broadcast_to(ms, m_ref.shape)

    return pl.pallas_call(
        body, name="mlstm_fwd", grid=(nb,),
        in_specs=[tm, tm, tm, gate, gate], out_specs=[tm, cm, vec, vec],
        out_shape=[jax.ShapeDtypeStruct((t, HEADS * HEAD_W), F32), jax.ShapeDtypeStruct((HEADS, nc, HEAD_W, HEAD_W), F32),
                   jax.ShapeDtypeStruct((HEADS, nc, 1, HEAD_W), F32), jax.ShapeDtypeStruct((HEADS, nc, 1, HEAD_W), F32)],
        scratch_shapes=_ML_STATE,
        compiler_params=_cparams("arbitrary"),
    )(q, k, v, li, lf)


def _ml_bwd(q, k, v, li, lf, cp, npv, mp, dhc, deps=()):
    t = q.shape[0]
    nc = t // CHUNK
    nb, tm, gate, cm, vec = _ml_specs(nc, True)

    def body(q_ref, k_ref, v_ref, li_ref, lf_ref, cp_ref, np_ref, mp_ref, dhc_ref, *rest):
        dq_ref, dk_ref, dv_ref, dli_ref, dlf_ref, dc_ref, dn_ref, dm_ref = rest[len(deps):]

        @pl.when(pl.program_id(0) == 0)
        def _():
            dc_ref[...] = jnp.zeros_like(dc_ref)
            dn_ref[...] = jnp.zeros_like(dn_ref)
            dm_ref[...] = jnp.zeros_like(dm_ref)

        dc, dn, dm = dc_ref[...], dn_ref[...], dm_ref[...][:, :, 0:1]
        for c in reversed(range(CHUNKS_PER_STEP)):
            r = _chunk_rows(c)
            _, vjp = jax.vjp(_ml_chunk, _heads(q_ref, r), _heads(k_ref, r), _heads(v_ref, r), li_ref[:, c], lf_ref[:, c],
                             cp_ref[:, c], np_ref[:, c], mp_ref[:, c][:, :, 0:1])
            dq, dk, dv, dli, dlf, dc, dn, dm = vjp((_heads(dhc_ref, r), dc, dn, dm))
            _put_heads(dq_ref, dq, r)
            _put_heads(dk_ref, dk, r)
            _put_heads(dv_ref, dv, r)
            dli_ref[:, c] = dli
            dlf_ref[:, c] = dlf
        dc_ref[...] = dc
        dn_ref[...] = dn
        dm_ref[...] = jnp.broadcast_to(dm, dm_ref.shape)

    tm_shape = jax.ShapeDtypeStruct((t, HEADS * HEAD_W), F32)
    gate_shape = jax.ShapeDtypeStruct((HEADS, nc, 1, CHUNK), F32)
    return pl.pallas_call(
        body, name="mlstm_bwd", grid=(nb,),
        in_specs=[tm, tm, tm, gate, gate, cm, vec, vec, tm] + [ANY] * len(deps), out_specs=[tm, tm, tm, gate, gate],
        out_shape=[tm_shape, tm_shape, tm_shape, gate_shape, gate_shape],
        scratch_shapes=_ML_STATE,
        compiler_params=_cparams("arbitrary"),
    )(q, k, v, li, lf, cp, npv, mp, dhc, *deps)


@jax.custom_vjp
def _bdot_diag(x, w):
    b = w.shape[1]
    return jnp.concatenate([_raw_dot(x[:, :b], w[0], "nn"), _raw_dot(x[:, b:], w[1], "nn")], axis=1)


def _bdot_diag_fwd(x, w):
    return _bdot_diag(x, w), (x, w)


def _bdot_diag_bwd(res, ct):
    x, w = res
    b = w.shape[1]
    dx = jnp.concatenate([_raw_dot(ct[:, :b], w[0], "nt"), _raw_dot(ct[:, b:], w[1], "nt")], axis=1)
    dw = jnp.stack([_raw_dot(x[:, :b], ct[:, :b], "tn"), _raw_dot(x[:, b:], ct[:, b:], "tn")])
    return dx.astype(x.dtype), dw.astype(w.dtype)


_bdot_diag.defvjp(_bdot_diag_fwd, _bdot_diag_bwd)


def _ml_pre(s0, s1, s2, s3, cw0, cw1, cw2, cw3, cb, wq, wk, wv, wiq, wik, wiv, bif):
    pre = cb + cw0 * s0 + cw1 * s1 + cw2 * s2 + cw3 * s3
    xc = pre * _sigmoid(pre)
    q = _bdot_diag(xc, wq)
    k = _bdot_diag(xc, wk)
    v = _bdot_diag(s3, wv)
    gates = _bdot(q, wiq, "nn") + _bdot(k, wik, "nn") + _bdot(v, wiv, "nn") + bif
    lane = lax.broadcasted_iota(jnp.int32, gates.shape, 1)
    gl = jnp.where(lane < HEADS, gates, _log_sigmoid(gates))
    return xc, q, k, v, gl


def _delayed(xs_ref, x_ref, halo_ref, r):
    xs_ref[0:HALO, :] = halo_ref[...]
    xs_ref[HALO:HALO + r, :] = x_ref[...]
    return [xs_ref[pl.ds(HALO - (CONV_K - 1) + j, r), :] for j in range(CONV_K)]


def _full_spec(shape):
    return pl.BlockSpec(shape, lambda i, nd=len(shape): (0,) * nd)


def _ml_pre_fwd(x_m, x_pad, params, tile=256, deps=()):
    t, w = x_m.shape
    r = min(tile, t)

    def body(*refs):
        x_ref, halo_ref = refs[:2]
        p = [ref[...] for ref in refs[2:2 + len(params)]]
        outs = refs[2 + len(params) + len(deps):-1]
        res = _ml_pre(*_delayed(refs[-1], x_ref, halo_ref, r), *p)
        for ref, val in zip(outs, res):
            ref[...] = val

    row = pl.BlockSpec((r, w), lambda i: (i, 0))
    return pl.pallas_call(
        body, name="ml_pre_fwd", grid=(t // r,),
        in_specs=[row, pl.BlockSpec((HALO, w), lambda i: (i * (r // HALO), 0))] + [_full_spec(p.shape) for p in params]
        + [ANY] * len(deps),
        out_specs=[row] * 4 + [pl.BlockSpec((r, LANES), lambda i: (i, 0))],
        out_shape=[jax.ShapeDtypeStruct((t, w), F32)] * 4 + [jax.ShapeDtypeStruct((t, LANES), F32)],
        scratch_shapes=[pltpu.VMEM((r + HALO, w), F32)],
        compiler_params=_cparams("arbitrary"),
    )(x_m, x_pad, *params, *deps)


def _ml_pre_bwd(x_m, x_pad, params, cts, tile=256):
    t, w = x_m.shape
    r = min(tile, t)
    nt = t // r
    n_p = len(params)

    def body(*refs):
        x_ref, halo_ref = refs[:2]
        p = [ref[...] for ref in refs[2:2 + n_p]]
        ct = [ref[...] for ref in refs[2 + n_p:7 + n_p]]
        dx_ref = refs[7 + n_p]
        dp_refs = refs[8 + n_p:8 + 2 * n_p]
        xs_ref, ds_ref, carry_ref = refs[8 + 2 * n_p:]
        step = pl.program_id(0)

        @pl.when(step == 0)
        def _():
            ds_ref[...] = jnp.zeros_like(ds_ref)
            carry_ref[...] = jnp.zeros_like(carry_ref)

        _, vjp = jax.vjp(_ml_pre, *_delayed(xs_ref, x_ref, halo_ref, r), *p)
        grads = vjp(tuple(ct))
        for j in range(CONV_K):
            ds_ref[j, HALO:HALO + r, :] = grads[j]
        lead = HALO + CONV_K - 1
        d_tile = sum(ds_ref[j, pl.ds(lead - j, r), :] for j in range(CONV_K))
        d_halo = sum(ds_ref[j, pl.ds(CONV_K - 1 - j, HALO), :] for j in range(CONV_K))
        dx_ref[...] = jnp.concatenate([d_tile[:r - HALO], d_tile[r - HALO:] + carry_ref[...]], axis=0).astype(dx_ref.dtype)
        carry_ref[...] = d_halo
        _accumulate(step, dp_refs, grads[CONV_K:])

    row = pl.BlockSpec((r, w), lambda i: (nt - 1 - i, 0))
    return pl.pallas_call(
        body, name="ml_pre_bwd", grid=(nt,),
        in_specs=[row, pl.BlockSpec((HALO, w), lambda i: ((nt - 1 - i) * (r // HALO), 0))] + [_full_spec(p.shape) for p in params]
        + [row] * 4 + [pl.BlockSpec((r, LANES), lambda i: (nt - 1 - i, 0))],
        out_specs=[row] + [_full_spec(p.shape) for p in params],
        out_shape=[jax.ShapeDtypeStruct((t, w), BF16)] + [jax.ShapeDtypeStruct(p.shape, F32) for p in params],
        scratch_shapes=[pltpu.VMEM((r + HALO, w), F32), pltpu.VMEM((CONV_K, r + 2 * HALO, w), F32), pltpu.VMEM((HALO, w), F32)],
        compiler_params=_cparams("arbitrary"),
    )(x_m, x_pad, *params, *cts)


def _per_head(fn, row_vals, head_params, shared_params=()):
    return [fn(*[a[:, hs] for a in row_vals], *[p[:, hs] for p in head_params], *shared_params) for hs in _head_slices(HEAD_W)]


def _gla_out(o, g, gn):
    return _rms(o, gn) * (g * _sigmoid(g))


def _ml_out(hc, op, xc, g, sk):
    hcell = hc * _sigmoid(op)
    mu = jnp.mean(hcell, axis=-1, keepdims=True)
    d = hcell - mu
    var = jnp.mean(d * d, axis=-1, keepdims=True)
    return d * lax.rsqrt(var + EPS) * g + sk * xc


def _log_decay(al, w, b):
    return _log_sigmoid(_bdot(al, w, "nn") + b) * (1.0 / GLA_GATE_NORM)


def _merge(ga, gb, ya, yb):
    ga, gb, ya, yb = (a.astype(F32) for a in (ga, gb, ya, yb))
    return _sigmoid(ga) * ya + _sigmoid(gb) * yb


def _post_mix(x, z, gpm, gpl):
    x1 = x + _rms(z, gpm)
    return x1, _rms(x1, gpl)


def _loss_rows(x1, dn, tgt, g):
    e = x1 + _rms(dn, g) - tgt
    return 0.5 * jnp.sum(jnp.mean(e * e, axis=-1, keepdims=True), axis=0, keepdims=True)


def _lin(p):
    return 4 * p[0] + 2 * p[1] + p[2]


def _me():
    return lax.axis_index("x"), lax.axis_index("y"), lax.axis_index("c")


def _flip(p, k):
    return tuple((1 - v) if (k >> (2 - i)) & 1 else v for i, v in enumerate(p))


ANY = pl.BlockSpec(memory_space=pl.ANY)


HBM = pl.BlockSpec(memory_space=pltpu.HBM)
SEM = pl.BlockSpec(memory_space=pltpu.SEMAPHORE)
DATAFLOW = pltpu.SideEffectType.DATAFLOW_SIDE_EFFECTING


SIBLING = 1
OTHER_CHIPS = (2, 4, 6)


def _peer_copies(kinds, srcs, lands, send_sems, recv_sems):
    me = _me()
    copies = []
    for a, (kind, src, land) in enumerate(zip(kinds, srcs, lands)):
        masks = {"gather": range(1, N_DEV), "exchange": range(1, N_DEV), "gather_chips": (SIBLING, *OTHER_CHIPS),
                 "gather_pass": OTHER_CHIPS}[kind]
        for k in masks:
            peer = _flip(me, k)
            if kind == "gather_pass":
                block = land.at[_lin(peer)]
                src_ref, dst_ref, target = block, block, _flip(me, SIBLING)
            else:
                src_ref, dst_ref, target = (src.at[_lin(peer)] if kind == "exchange" else src), land.at[_lin(me)], peer
            copies.append(pltpu.make_async_remote_copy(
                src_ref=src_ref, dst_ref=dst_ref, send_sem=send_sems.at[a * 7 + k - 1], recv_sem=recv_sems.at[a * 7 + k - 1],
                device_id=target, device_id_type=MESH))
    return copies


def _copies_start(kind, srcs, name, after=None, lands=None):
    n = len(srcs)
    extra = [] if after is None else [after]
    kind = [kind] * n if isinstance(kind, str) else list(kind)
    land_shapes = [(s.shape if k == "exchange" else (N_DEV, *s.shape)) for k, s in zip(kind, srcs)]
    lands = [lax.empty(ls, s.dtype) for ls, s in zip(land_shapes, srcs)] if lands is None else lands

    def body(*refs):
        sems = refs[2 * n + len(extra):]
        for cp in _peer_copies(kind, refs[:n], refs[n:2 * n], sems[0], sems[1]):
            cp.start()
        refs[-1][...] = jnp.zeros_like(refs[-1])

    def hbm(a):
        return pltpu.with_memory_space_constraint(a, pltpu.HBM)

    out = pl.pallas_call(
        body, name=name,
        out_shape=(pltpu.SemaphoreType.DMA((7 * n,)), pltpu.SemaphoreType.DMA((7 * n,)),
                   *[pltpu.HBM(s.shape, s.dtype) for s in srcs],
                   *[pltpu.HBM(ls, s.dtype) for ls, s in zip(land_shapes, srcs)],
                   jax.ShapeDtypeStruct((8, LANES), F32)),
        in_specs=[HBM] * (2 * n) + [ANY] * len(extra),
        out_specs=(SEM, SEM, *[HBM] * (2 * n), pl.BlockSpec(memory_space=pltpu.VMEM)),
        input_output_aliases={i: 2 + i for i in range(2 * n)},
        compiler_params=pltpu.CompilerParams(has_side_effects=DATAFLOW),
    )(*[hbm(s) for s in srcs], *[hbm(a) for a in lands], *extra)
    return (kind, n, out[:-1]), out[-1]


def _copies_wait(state, after, name):
    kind, n, (send_sems, recv_sems, *thru) = state
    after = list(after) if isinstance(after, (list, tuple)) else [after]

    def body(*refs):
        for cp in _peer_copies(kind, refs[:n], refs[n:2 * n], refs[2 * n], refs[2 * n + 1]):
            cp.wait_send()
            cp.wait_recv()

    out = pl.pallas_call(
        body, name=name,
        out_shape=tuple(pltpu.HBM(t.shape, t.dtype) for t in thru),
        in_specs=[HBM] * (2 * n) + [SEM, SEM] + [ANY] * len(after), out_specs=tuple([HBM] * (2 * n)),
        input_output_aliases={i: i for i in range(2 * n)},
        compiler_params=pltpu.CompilerParams(has_side_effects=DATAFLOW),
    )(*thru, send_sems, recv_sems, *after)
    return out[:n], out[n:]


def _adamw(w, g, m, v):
    m2 = ADAM_B1 * m + (1.0 - ADAM_B1) * g
    v2 = ADAM_B2 * v + (1.0 - ADAM_B2) * (g * g)
    m_hat = m2 / (1.0 - ADAM_B1 ** ADAM_STEP)
    v_hat = v2 / (1.0 - ADAM_B2 ** ADAM_STEP)
    delta = -ADAM_LR * (m_hat / (jnp.sqrt(v_hat) + ADAM_EPS) + ADAM_WD * w)
    return delta, m2, v2


def _sum_adamw(lands, parts, me_idx, w, m, v, name, tile=256):
    r, c = w.shape
    nchunks = len(lands)
    tr = min(tile, r // nchunks)
    per_chunk = r // nchunks // tr
    per = 1 + N_DEV

    def body(me_ref, *refs):
        w_ref, m_ref, v_ref, g_ref, d_ref, m2_ref, v2_ref = refs[nchunks * per:]
        for k in range(nchunks):
            own_ref, slots = refs[k * per], refs[k * per + 1:(k + 1) * per]

            @pl.when(pl.program_id(0) // per_chunk == k)
            def _(own_ref=own_ref, slots=slots):
                own = own_ref[...].astype(F32)
                g = None
                for s in range(N_DEV):
                    term = jnp.where(me_ref[0] == s, own, slots[s][...].astype(F32))
                    g = term if g is None else g + term
                d, m2, v2 = _adamw(w_ref[...], g, m_ref[...], v_ref[...])
                g_ref[...] = g
                d_ref[...] = d
                m2_ref[...] = m2
                v2_ref[...] = v2

    def chunk_specs(k):
        def tile_of(i):
            return jnp.clip(i - k * per_chunk, 0, per_chunk - 1)

        def slot_spec(s):
            return pl.BlockSpec((None, tr, c), lambda i, me: (jnp.where(me[0] == s, (s + 1) % N_DEV, s), tile_of(i), 0))
        return [pl.BlockSpec((None, tr, c), lambda i, me: (me[0], tile_of(i), 0))] + [slot_spec(s) for s in range(N_DEV)]

    row = pl.BlockSpec((tr, c), lambda i, me: (i, 0))
    operands = [a for land, part in zip(lands, parts) for a in (part, *[land] * N_DEV)]
    return pl.pallas_call(
        body, name=name,
        grid_spec=pltpu.PrefetchScalarGridSpec(
            num_scalar_prefetch=1, grid=(r // tr,),
            in_specs=[s for k in range(nchunks) for s in chunk_specs(k)] + [row] * 3,
            out_specs=[row] * 4),
        out_shape=[jax.ShapeDtypeStruct((r, c), F32)] * 4,
        compiler_params=_cparams("parallel"),
    )(me_idx, *operands, w, m, v)


def _small_update(name, me_idx, kinds, lands, owns, ws, ms, vs, sums=()):
    n = len(ws)
    lands, owns = list(lands) + [s[0] for s in sums], list(owns) + [s[1] for s in sums]
    kinds = list(kinds) + ["gather"] * len(sums)
    nl = len(lands)

    def summed(me, land_ref, own):
        g = None
        for s in range(N_DEV):
            term = jnp.where(me == s, own, land_ref[s]).astype(F32)
            g = term if g is None else g + term
        return g

    def body(me_ref, *refs):
        land_refs, own_refs = refs[:nl], refs[nl:2 * nl]
        w_refs, m_refs, v_refs = (refs[2 * nl + i * n:2 * nl + (i + 1) * n] for i in range(3))
        outs = refs[2 * nl + 3 * n:]
        me = me_ref[0]
        for i in range(n):
            g = summed(me, land_refs[i], own_refs[i][...])
            d, m2, v2 = _adamw(w_refs[i][...], g, m_refs[i][...], v_refs[i][...])
            for ref, val in zip(outs[4 * i:4 * i + 4], (g, d, m2, v2)):
                ref[...] = val
        for i in range(n, nl):
            outs[4 * n + i - n][...] = summed(me, land_refs[i], own_refs[i][...])

    def whole(shape):
        return pl.BlockSpec(shape, lambda i, me, nd=len(shape): (0,) * nd)

    def own_spec(kind, own):
        if kind == "gather":
            return whole(own.shape)
        return pl.BlockSpec((None, *own.shape[1:]), lambda i, me: (me[0], 0, 0))

    shapes = [w.shape for w in ws]
    out_shapes = [s for s in shapes for _ in range(4)] + [s[1].shape for s in sums]
    return pl.pallas_call(
        body, name=name,
        grid_spec=pltpu.PrefetchScalarGridSpec(
            num_scalar_prefetch=1, grid=(1,),
            in_specs=[whole(a.shape) for a in lands] + [own_spec(k, o) for k, o in zip(kinds, owns)]
            + [whole(s) for s in shapes] * 3,
            out_specs=[whole(s) for s in out_shapes]),
        out_shape=[jax.ShapeDtypeStruct(s, F32) for s in out_shapes],
        compiler_params=_cparams("arbitrary"),
    )(me_idx, *lands, *owns, *ws, *ms, *vs)


def _small_view(n, a):
    if a.ndim == 1:
        return a.reshape(1, -1)
    if a.ndim == 3:
        return a.transpose(1, 2, 0).reshape(QKV_BLOCK * QKV_BLOCK, -1)
    return a.T if n == "w_if" else a


def _small_unview(n, a, shape):
    if len(shape) == 1:
        return a.reshape(shape)
    if len(shape) == 3:
        return a.reshape(QKV_BLOCK, QKV_BLOCK, -1).transpose(2, 0, 1)
    return a.T if n == "w_if" else a


def _small_shards(n, g):
    if n == "w_if":
        return g.reshape(N_DEV, -1, g.shape[1]).transpose(0, 2, 1)
    return g.reshape(g.shape[0], N_DEV, -1).transpose(1, 0, 2)


def _small_unshard(n, s):
    if n == "w_if":
        return s.transpose(0, 2, 1).reshape(-1, s.shape[1])
    return s.transpose(1, 0, 2).reshape(s.shape[1], -1)


def _to_hm(a, d):
    t = a.shape[0]
    return a.reshape(t, HEADS, d).transpose(1, 0, 2)


def _from_hm(a):
    h, t, d = a.shape
    return a.transpose(1, 0, 2).reshape(t, h * d)


def _gate_rows(g):
    t = g.shape[0]
    return g.T.reshape(HEADS, t // CHUNK, 1, CHUNK)


def _gate_cols(g):
    h, nc, _, c = g.shape
    return g.reshape(h, nc * c).T


def _blockdiag_dense(w):
    n = w.shape[0] * QKV_BLOCK // 2
    tiled = jnp.tile(w.reshape(2, n, QKV_BLOCK), (1, 1, n // QKV_BLOCK))
    r = lax.broadcasted_iota(jnp.int32, (2, n, n), 1)
    c = lax.broadcasted_iota(jnp.int32, (2, n, n), 2)
    return jnp.where(r // QKV_BLOCK == c // QKV_BLOCK, tiled, 0.0)


def _blockdiag_blocks(dense):
    _, n, _ = dense[0].shape
    k = len(dense)

    def body(*refs):
        r = lax.broadcasted_iota(jnp.int32, (n, n), 0)
        c = lax.broadcasted_iota(jnp.int32, (n, n), 1)
        fr = lax.broadcasted_iota(jnp.int32, (n, LANES), 0)
        fc = lax.broadcasted_iota(jnp.int32, (n, LANES), 1)
        fold = ((fr & (QKV_BLOCK - 1)) == fc).astype(BF16)
        for i in range(k):
            for half in range(2):
                kept = jnp.where((r >> 2) == (c >> 2), refs[i][half], 0.0)
                refs[k + i][half] = sum(lax.dot_general(t, fold, _dims("nn", 2), preferred_element_type=F32)
                                        for t in _split3(kept))

    out = pl.pallas_call(body, name="blockdiag_blocks", out_shape=[jax.ShapeDtypeStruct((2, n, LANES), F32)] * k)(*dense)
    return [o[:, :, 0:QKV_BLOCK].reshape(2 * n // QKV_BLOCK, QKV_BLOCK, QKV_BLOCK) for o in out]


def _col_blocks(w):
    k, n = w.shape
    return w.reshape(k, N_DEV, n // N_DEV).transpose(1, 0, 2)


def _from_col_blocks(g):
    d, k, n = g.shape
    return g.transpose(1, 0, 2).reshape(k, d * n)


def _first_norm(x, g):
    return _rowwise("pre_mix_norm", lambda xv, gv: ((_rms(xv, gv),), ()), [x], [g], [(x.shape[1], BF16)])[0]


def _local_step(x, h, tgt, weight, ws, prefetch, pass_on, on_grads, on_small):
    t, d = x.shape
    g1 = ws["g_pre_mix"]

    def dep(token):
        return () if token is None else (token,)

    w_in = weight("w_in", x)
    fetch_mix = prefetch(("w_pa", "w_pb", "w_o"), w_in)

    n_in = w_in.shape[2]

    offs = [0]
    for s in IN_SPLITS:
        offs.append(offs[-1] + s)

    w_a_up_p = jnp.pad(ws["w_a_up"], ((0, LANES - LOWRANK), (0, 0)))
    b_a_up = ws["b_a_up"]

    def proj_in_fwd(hv, w, wa, ba):
        proj = jnp.concatenate([_raw_dot(hv, w[j], "nn") for j in range(N_DEV)], axis=1)
        parts = [proj[:, offs[i]:offs[i + 1]] for i in range(len(IN_SPLITS))]
        parts[4] = jnp.concatenate([parts[4], jnp.zeros((parts[4].shape[0], LANES - LOWRANK), F32)], axis=1)
        return (*parts, _log_decay(parts[4], wa, ba)), ()

    widths = [LANES if s == LOWRANK else s for s in IN_SPLITS]
    q_a, k_a, v_a, g_a, a_low_p, x_m, o_pre, gate_a, gate_b, la = _rowwise(
        "proj_in", proj_in_fwd, [h], [w_in, w_a_up_p, b_a_up],
        [(wd, BF16 if i == 2 else F32) for i, wd in enumerate(widths)] + [(HEADS * GLA_DK, F32)],
        deps=dep(fetch_mix))

    fetch_up = prefetch(("w_up", "w_down"), la)
    q_hm, k_hm, la_hm = _to_hm(q_a, GLA_DK), _to_hm(k_a, GLA_DK), _to_hm(la, GLA_DK)
    o_gla, s_prev = _gla_fwd(q_hm, k_hm, v_a, la_hm, deps=dep(fetch_up))
    pass_mix = pass_on("w_pa", o_gla)
    gn = ws["g_gla_norm"]
    ml_w = HEADS * HEAD_W

    cw = ws["conv_w"]
    w_if_p = jnp.pad(ws["w_if"], ((0, 0), (0, LANES - 2 * HEADS)))
    pre_params = [cw[0:1], cw[1:2], cw[2:3], cw[3:4], ws["conv_b"],
                  _blockdiag_dense(ws["w_q_ml"]), _blockdiag_dense(ws["w_k_ml"]), _blockdiag_dense(ws["w_v_ml"]),
                  w_if_p[0:ml_w], w_if_p[ml_w:2 * ml_w], w_if_p[2 * ml_w:3 * ml_w],
                  jnp.pad(ws["b_if"], ((0, 0), (0, LANES - 2 * HEADS)))]
    x_pad = jnp.pad(x_m, ((HALO, 0), (0, 0)))
    xc, q_m, k_m, v_m, gl = _ml_pre_fwd(x_m, x_pad, pre_params, tile=512, deps=dep(pass_mix))
    li, lf = _gate_rows(gl[:, 0:HEADS]), _gate_rows(gl[:, HEADS:2 * HEADS])
    hc, c_prev, n_prev, m_prev = _ml_fwd(q_m, k_m, v_m, li, lf)
    g_ml, skip = ws["g_ml_norm"], ws["ml_skip"]

    def branches_out(o, g, a, b, c_, ga, gb, n_, wa, gm, s, wb):
        ya_in = jnp.concatenate(_per_head(_gla_out, [o, g], [], [n_]), axis=1)
        ya = _raw_dot(ya_in, wa, "nn")
        hb = jnp.concatenate(_per_head(_ml_out, [a, b, c_], [gm, s]), axis=1)
        yb = _raw_dot(hb, wb, "nn")
        return (ya_in, ya, hb, yb, _merge(ga, gb, ya, yb)), ()

    ya_in, y_a, h_b, y_b, merged = _rowwise(
        "branches_out", branches_out, [o_gla, g_a, hc, o_pre, xc, gate_a, gate_b],
        [gn, weight("w_pa", hc), g_ml, skip, weight("w_pb", hc)],
        [(ml_w, BF16), (d, BF16), (ml_w, BF16), (d, BF16), (d, BF16)], tile=512)

    gpm, gpl, gpo = ws["g_post_mix"], ws["g_pre_mlp"], ws["g_post_mlp"]

    def proj_o_fwd(mg, xv, w, a, b):
        zv = _raw_dot(mg, w, "nn")
        return (zv, *_post_mix(xv, zv, a, b)), ()

    pass_up = pass_on("w_up", merged)
    z, x1, h2 = _rowwise("proj_o", proj_o_fwd, [merged, x], [weight("w_o", merged), gpm, gpl],
                         [(d, F32), (d, F32), (d, BF16)], tile=512, deps=dep(pass_up))
    w_up = weight("w_up", h2)
    w_down = weight("w_down", h2)
    d_ff = w_down.shape[0]

    def mlp_loss(h2v, x1v, tgtv, wu, wd, g):
        upv = jnp.concatenate([_raw_dot(h2v, wu[j], "nn") for j in range(wu.shape[0])], axis=1)
        uv = jnp.square(jnp.maximum(upv, 0.0))
        dnv = _raw_dot(uv, wd, "nn")
        loss, vjp = jax.vjp(lambda a, b, c_: _loss_rows(a, b, tgtv, c_), x1v, dnv, g)
        dx1, ddn, dg = vjp(jnp.ones((1, 1), F32))
        return (upv, uv, dx1, ddn), (jnp.broadcast_to(loss, (1, LANES)), dg)

    up, u, dx1_y, d_dn, loss, d_gpo = _rowwise("mlp_loss", mlp_loss, [h2, x1, tgt], [w_up, w_down, gpo],
                                               [(d_ff, BF16), (d_ff, BF16), (d, F32), (d, BF16)],
                                               [((1, LANES), F32), ((1, d), F32)])

    dw_down = _mm(u, d_dn, "tn", BF16, "mlp_down_dw", tm=512)

    def mlp_dx(ddn, upv, xv, zv, dx1, wd, wu, a, b):
        dup = (_raw_dot(ddn, wd, "nt") * (2.0 * jnp.maximum(upv.astype(F32), 0.0))).astype(BF16)
        ns = wu.shape[2]
        dh2 = sum(_raw_dot(dup[:, j * ns:(j + 1) * ns], wu[j], "nt") for j in range(wu.shape[0]))
        _, vjp = jax.vjp(_post_mix, xv, zv, a, b)
        dx, dz, da, db = vjp((dx1, dh2))
        return (dup, dx, dz), (da, db)

    d_up, dx_res, d_z, d_gpm, d_gpl = _rowwise("mlp_dx", mlp_dx, [d_dn, up, x, z, dx1_y], [w_down, w_up, gpm, gpl],
                                               [(d_ff, BF16), (d, F32), (d, BF16)], [((1, d), F32), ((1, d), F32)])
    dw_up = _mm_shard_cols(h2, [d_up], [d_up.shape[1]], w_up.shape[2], "mlp_up_dw", 0, d)
    sent_mlp = on_grads(dict(w_down=dw_down, w_up=dw_up))

    def branches_out_bwd(dz, ga, gb, ya, yb, o, g, a, b, c_, wo, wa, n_, wb, gm, s):
        d_ga_, d_gb_, d_ya_, d_yb_ = jax.vjp(_merge, ga, gb, ya, yb)[1](_raw_dot(dz, wo, "nt"))
        ct_a, ct_b = _raw_dot(d_ya_, wa, "nt"), _raw_dot(d_yb_, wb, "nt")
        parts_a, parts_b = [], []
        for hs in _head_slices(HEAD_W):
            parts_a.append(jax.vjp(_gla_out, o[:, hs], g[:, hs], n_)[1](ct_a[:, hs]))
            parts_b.append(jax.vjp(_ml_out, a[:, hs], b[:, hs], c_[:, hs], gm[:, hs], s[:, hs])[1](ct_b[:, hs]))
        cat = lambda parts, i: jnp.concatenate([p[i] for p in parts], axis=1)
        return ((d_ga_, d_gb_, d_ya_, d_yb_, cat(parts_a, 0), cat(parts_a, 1), cat(parts_b, 0), cat(parts_b, 1), cat(parts_b, 2)),
                (sum(p[2] for p in parts_a), cat(parts_b, 3), cat(parts_b, 4)))

    d_ga, d_gb, d_ya, d_yb, d_o, d_g_a, d_hc, d_opre, d_xc, d_gn, d_gml, d_skip = _rowwise(
        "branches_out_bwd", branches_out_bwd, [d_z, gate_a, gate_b, y_a, y_b, o_gla, g_a, hc, o_pre, xc],
        [weight("w_o", merged), weight("w_pa", hc), gn, weight("w_pb", hc), g_ml, skip],
        [(d, BF16)] * 4 + [(ml_w, F32), (ml_w, BF16), (ml_w, F32), (ml_w, BF16), (ml_w, F32)],
        [((1, HEAD_W), F32), ((1, ml_w), F32), ((1, ml_w), F32)], deps=dep(sent_mlp))
    dw_o, dw_pa, dw_pb = _mm_tn_whole([(merged, d_z), (ya_in, d_ya), (h_b, d_yb)], "mix_dw")
    sent_mix = on_grads(dict(w_o=dw_o, w_pa=dw_pa, w_pb=dw_pb))

    d_qm, d_km, d_vm, d_li, d_lf = _ml_bwd(q_m, k_m, v_m, li, lf, c_prev, n_prev, m_prev, d_hc, deps=dep(sent_mix))
    d_gl = jnp.concatenate([_gate_cols(d_li), _gate_cols(d_lf), jnp.zeros((t, LANES - 2 * HEADS), F32)], axis=1)
    pre_grads = _ml_pre_bwd(x_m, x_pad, pre_params, [d_xc, d_qm, d_km, d_vm, d_gl], tile=512)
    d_xm = pre_grads[0]
    d_cw = jnp.concatenate(pre_grads[1:5], axis=0)
    d_cb = pre_grads[5]
    d_wq, d_wk, d_wv = _blockdiag_blocks(pre_grads[6:9])
    d_wif = jnp.concatenate(pre_grads[9:12], axis=0)[:, 0:2 * HEADS]
    d_bif = pre_grads[12][:, 0:2 * HEADS]

    dq_hm, dk_hm, d_va, dla_hm = _gla_bwd(q_hm, k_hm, v_a, la_hm, s_prev, d_o)

    def decay_bwd(al, ct, w, b):
        _, vjp = jax.vjp(_log_decay, al, w, b)
        dal, dw, db = vjp(ct)
        return (dal,), (dw, db)

    d_alow_p, d_wa_p, d_ba = _rowwise("gla_decay_bwd", decay_bwd, [a_low_p, _from_hm(dla_hm)], [w_a_up_p, b_a_up],
                                      [(LANES, BF16)], [(w_a_up_p.shape, F32), (b_a_up.shape, F32)])
    d_proj = [jnp.concatenate([_from_hm(dq_hm), _from_hm(dk_hm), d_va, d_g_a], axis=1), d_alow_p,
              jnp.concatenate([d_xm, d_opre, d_ga, d_gb], axis=1)]
    d_widths = [offs[4], LOWRANK, offs[9] - offs[5]]
    d_pieces = _shard_pieces(d_widths, n_in)
    small = dict(w_a_up=d_wa_p[0:LOWRANK], b_a_up=d_ba, g_gla_norm=d_gn, conv_w=d_cw, conv_b=d_cb,
                 w_q_ml=d_wq, w_k_ml=d_wk, w_v_ml=d_wv, w_if=d_wif, b_if=d_bif, ml_skip=d_skip, g_ml_norm=d_gml,
                 g_post_mix=d_gpm, g_pre_mlp=d_gpl, g_post_mlp=d_gpo)
    sent_small = on_small(small, loss)
    sent_in = sent_small
    for half in range(2):
        dw_half = _mm_shard_cols(h, d_proj, d_widths, n_in, "proj_in_dw_%d" % half, half, d // 2, deps=dep(sent_in))
        sent_in = on_grads({"w_in#%d" % half: dw_half})

    def proj_in_dx(dp_a, dp_low, dp_b, xv, dres, w, g):
        dh = 0.0
        for s in range(N_DEV):
            for i, c_in, c_w, wd in d_pieces[s]:
                src = (dp_a, dp_low, dp_b)[i]
                cols = src.shape[1] - c_in if wd < LANES else wd
                dh = dh + _raw_dot(src[:, c_in:c_in + cols], w[s][:, c_w:c_w + cols], "nt")
        _, vjp = jax.vjp(_rms, xv, g)
        dx, dg = vjp(dh)
        return (dx + dres,), (dg,)

    grad_x, d_g1 = _rowwise("proj_in_dx", proj_in_dx, [*d_proj, x, dx_res], [w_in, g1], [(d, F32)], [((1, d), F32)],
                            deps=dep(sent_in))
    return grad_x, on_small(dict(g_pre_mix=d_g1), None)


BIG = ("w_in", "w_pa", "w_pb", "w_o", "w_up", "w_down")
MIX = ("w_o", "w_pa", "w_pb")
BIG_COL_SHARDED = ("w_in", "w_pa", "w_pb", "w_up")
SMALL_SHARDED = ("w_a_up", "conv_w", "w_if")
SMALL = ("g_pre_mix", "w_a_up", "b_a_up", "g_gla_norm", "conv_w", "conv_b", "w_q_ml", "w_k_ml", "w_v_ml", "w_if", "b_if",
         "ml_skip", "g_ml_norm", "g_post_mix", "g_pre_mlp", "g_post_mlp")
WEIGHTS = ("g_pre_mix", "w_in", "w_a_up", "b_a_up", "g_gla_norm", "conv_w", "conv_b", "w_q_ml", "w_k_ml", "w_v_ml", "w_if", "b_if",
           "ml_skip", "g_ml_norm", "w_pa", "w_pb", "w_o", "g_post_mix", "g_pre_mlp", "w_up", "w_down", "g_post_mlp")


def kernel(x, g_pre_mix, w_in, w_a_up, b_a_up, g_gla_norm, conv_w, conv_b, w_q_ml, w_k_ml, w_v_ml, w_if, b_if, ml_skip, g_ml_norm, w_pa, w_pb, w_o, g_post_mix, g_pre_mlp, w_up, w_down, g_post_mlp, loss_target, m_g_pre_mix, m_w_in, m_w_a_up, m_b_a_up, m_g_gla_norm, m_conv_w, m_conv_b, m_w_q_ml, m_w_k_ml, m_w_v_ml, m_w_if, m_b_if, m_ml_skip, m_g_ml_norm, m_w_pa, m_w_pb, m_w_o, m_g_post_mix, m_g_pre_mlp, m_w_up, m_w_down, m_g_post_mlp, v_g_pre_mix, v_w_in, v_w_a_up, v_b_a_up, v_g_gla_norm, v_conv_w, v_conv_b, v_w_q_ml, v_w_k_ml, v_w_v_ml, v_w_if, v_b_if, v_ml_skip, v_g_ml_norm, v_w_pa, v_w_pb, v_w_o, v_g_post_mix, v_g_pre_mlp, v_w_up, v_w_down, v_g_post_mlp):
    args = dict(locals())
    w = {n: args[n][0] for n in WEIGHTS}
    m = {n: args["m_" + n][0] for n in WEIGHTS}
    v = {n: args["v_" + n][0] for n in WEIGHTS}

    me_lin = _lin(_me())
    me_idx = jnp.reshape(me_lin, (1,)).astype(jnp.int32)

    def full_weight(n, g):
        if n in ("w_in", "w_up"):
            return g
        return _from_col_blocks(g) if n in BIG_COL_SHARDED else g.reshape(-1, g.shape[-1])

    def grad_parts(n, g):
        if n.partition("#")[0] in ("w_in", "w_up"):
            return g
        return (_col_blocks(g) if n in BIG_COL_SHARDED else g.reshape(N_DEV, -1, g.shape[-1])).astype(BF16)

    sharded_names = tuple(SMALL_SHARDED)
    narrow = {n: w[n].astype(BF16) for n in BIG}
    ready, pending, passing = {}, {}, {}
    first_state, _ = _copies_start(["gather"] * len(sharded_names) + ["gather_chips"],
                                   [_small_view(n, w[n]) for n in sharded_names] + [narrow["w_in"]], "allgather_start_first")

    def prefetch(group, after):
        state, token = _copies_start("gather_chips", [narrow[n] for n in group], "allgather_start_" + group[0], after)
        for n in group:
            pending[n] = (group, state)
        return token

    def pass_on(n, after):
        group, state = pending[n]
        shards, lands = _copies_wait(state, after, "allgather_wait_" + group[0])
        state, token = _copies_start("gather_pass", shards, "allgather_pass_" + group[0], lands=lands)
        for gn in group:
            passing[gn] = (group, state)
        return token

    def weight(n, after):
        if n not in ready:
            group, state = passing[n]
            shards, lands = _copies_wait(state, after, "allgather_passed_" + group[0])
            for gn, shard, land in zip(group, shards, lands):
                ready[gn] = full_weight(gn, lax.dynamic_update_slice(land, shard[None], (me_lin, 0, 0)))
        return ready[n]

    h = _first_norm(x[0], w["g_pre_mix"].reshape(1, -1))
    first_own, first_lands = _copies_wait(first_state, [h] + [narrow[n] for n in BIG if n != "w_in"], "allgather_wait_first")
    state, _ = _copies_start("gather_pass", first_own[-1:], "allgather_pass_w_in", lands=first_lands[-1:])
    passing["w_in"] = (("w_in",), state)
    ws = {n: (w[n].reshape(1, -1) if w[n].ndim == 1 else w[n]) for n in SMALL if n not in SMALL_SHARDED}
    for n, own, land in zip(sharded_names, first_own, first_lands):
        ws[n] = _small_unshard(n, lax.dynamic_update_slice(land, own[None], (me_lin, 0, 0)))

    sets, waiting_small = [], []

    def start_set(large, small):
        names = tuple(large)
        s_names, s_kinds, s_srcs = small if small else ((), [], [])
        state, token = _copies_start(s_kinds + ["exchange"] * len(names), s_srcs + [grad_parts(n, large[n]) for n in names],
                                     "exchange_start_" + (names + s_names)[0].replace("#", "_"))
        sets.append((names, s_names, s_kinds, state))
        return token

    def on_grads(grads):
        return start_set(grads, waiting_small.pop() if waiting_small else None)

    def on_small(small, loss):
        names = tuple(small)
        kinds = ["exchange" if n in SMALL_SHARDED else "gather" for n in names]
        srcs = [_small_shards(n, small[n]) if n in SMALL_SHARDED else _small_view(n, small[n]) for n in names]
        if loss is None:
            return start_set({}, (names, kinds, srcs))
        waiting_small.append((names, kinds + ["gather"], srcs + [loss]))
        return None

    grad_x, last_token = _local_step(x[0], h, loss_target[0], weight, ws, prefetch, pass_on, on_grads, on_small)

    out, chunks, sums = {}, {}, []

    def finish_set(names, s_names, s_kinds, state, after):
        own, lands = _copies_wait(state, after, "exchange_wait_" + (names + s_names)[0].replace("#", "_"))
        ns = len(s_kinds)
        if s_names:
            k = len(s_names)
            upd = _small_update("adamw_small_" + s_names[0], me_idx, s_kinds[:k], lands[:k], own[:k],
                                *[[_small_view(n, d[n]) for n in s_names] for d in (w, m, v)],
                                sums=list(zip(lands[k:ns], own[k:ns])))
            for i, n in enumerate(s_names):
                out[n] = tuple(_small_unview(n, a, w[n].shape) for a in upd[4 * i:4 * i + 4])
            sums.extend(upd[4 * k:])
        if names == MIX:
            upd = _small_update("adamw_mix", me_idx, ["exchange"] * len(names), lands[ns:], own[ns:],
                                *[[d[n] for n in names] for d in (w, m, v)])
            for i, n in enumerate(names):
                out[n] = tuple(upd[4 * i:4 * i + 4])
            return
        for name, part, land in zip(names, own[ns:], lands[ns:]):
            n, _, chunk = name.partition("#")
            chunks.setdefault(n, []).append((land, part))
            if chunk in ("", "1"):
                got_lands, got_parts = zip(*chunks[n])
                out[n] = _sum_adamw(got_lands, got_parts, me_idx, w[n], m[n], v[n], "adamw_" + n)

    for entry in sets:
        finish_set(*entry, [grad_x, last_token] + [out[n][1] for n in BIG if n in out])
    loss_sum = sums[0]

    shaped = lambda a, n: a.reshape(args[n].shape)
    return (loss_sum[0, 0], grad_x[None],
            *[shaped(out[n][0], n) for n in WEIGHTS], *[shaped(out[n][1], n) for n in WEIGHTS],
            *[shaped(out[n][2], n) for n in WEIGHTS], *[shaped(out[n][3], n) for n in WEIGHTS])
```

```python
import functools

import jax
import jax.numpy as jnp
from jax import lax
from jax.experimental import pallas as pl
from jax.experimental.pallas import tpu as pltpu

F32 = jnp.float32
BF16 = jnp.bfloat16
MESH = pl.DeviceIdType.MESH

N_DEV = 8
EPS = 1e-6
CHUNK = 64
CHUNKS_PER_STEP = 8
HEADS = 4
GLA_DK = 64
HEAD_W = 128
GLA_GATE_NORM = 16.0
LOWRANK = 16
CONV_K = 4
QKV_BLOCK = 4
LANES = 128
HALO = 8
IN_SPLITS = (256, 256, 512, 512, 16, 512, 512, 1024, 1024)

ADAM_LR = 0.001
ADAM_B1 = 0.9
ADAM_B2 = 0.999
ADAM_EPS = 1e-08
ADAM_WD = 0.01
ADAM_STEP = 10

VMEM_LIMIT = 56 * 1024 * 1024


def _cparams(*sem):
    return pltpu.CompilerParams(dimension_semantics=sem, vmem_limit_bytes=VMEM_LIMIT)


def _dims(mode, ndim):
    contract = {"nn": ((ndim - 1,), (ndim - 2,)), "nt": ((ndim - 1,), (ndim - 1,)), "tn": ((ndim - 2,), (ndim - 2,))}[mode]
    return contract, (((0,), (0,)) if ndim == 3 else ((), ()))


def _raw_dot(a, b, mode):
    return lax.dot_general(a.astype(BF16), b.astype(BF16), _dims(mode, a.ndim), preferred_element_type=F32)


@functools.partial(jax.custom_vjp, nondiff_argnums=(2,))
def _bdot(a, b, mode):
    return _raw_dot(a, b, mode)


def _bdot_fwd(a, b, mode):
    return _raw_dot(a, b, mode), (a, b)


def _bdot_bwd(mode, res, ct):
    a, b = res
    if mode == "nn":
        da, db = _raw_dot(ct, b, "nt"), _raw_dot(a, ct, "tn")
    elif mode == "nt":
        da, db = _raw_dot(ct, b, "nn"), _raw_dot(ct, a, "tn")
    else:
        da, db = _raw_dot(b, ct, "nt"), _raw_dot(a, ct, "nn")
    return da.astype(a.dtype), db.astype(b.dtype)


_bdot.defvjp(_bdot_fwd, _bdot_bwd)


def _split3(x):
    hi = x.astype(BF16)
    r1 = x - hi.astype(F32)
    mid = r1.astype(BF16)
    return hi, mid, (r1 - mid.astype(F32)).astype(BF16)


def _split_dot(tri, x):
    if x.ndim == 3:
        tri = jnp.broadcast_to(tri, (x.shape[0], *tri.shape))
    return sum(lax.dot_general(tri, t, _dims("nn", x.ndim), preferred_element_type=F32) for t in _split3(x))


def _tri(n, lower):
    r = lax.broadcasted_iota(jnp.int32, (n, n), 0)
    c = lax.broadcasted_iota(jnp.int32, (n, n), 1)
    return ((c <= r) if lower else (c >= r)).astype(BF16)


@jax.custom_vjp
def _cumsum_rows(x):
    return _split_dot(_tri(x.shape[-2], True), x)


def _cumsum_rows_fwd(x):
    return _cumsum_rows(x), None


def _cumsum_rows_bwd(_, ct):
    return (_split_dot(_tri(ct.shape[-2], False), ct),)


_cumsum_rows.defvjp(_cumsum_rows_fwd, _cumsum_rows_bwd)


def _abs(x):
    return jnp.where(x >= 0, x, -x)


def _sigmoid(x):
    return lax.logistic(x)


def _log_sigmoid(x):
    return jnp.minimum(x, 0.0) - jnp.log(1.0 + jnp.exp(-_abs(x)))


def _rms(x, g):
    return x * lax.rsqrt(jnp.mean(x * x, axis=-1, keepdims=True) + EPS) * g


def _head_slices(w):
    return [slice(h * w, (h + 1) * w) for h in range(HEADS)]


def _heads(ref, rows=slice(None)):
    return jnp.stack([ref[rows, hs] for hs in _head_slices(HEAD_W)])


def _put_heads(ref, val, rows=slice(None)):
    for h, hs in enumerate(_head_slices(HEAD_W)):
        ref[rows, hs] = val[h].astype(ref.dtype)


def _tile(dim, want):
    if dim <= want or dim % LANES:
        return dim
    t = want
    while dim % t:
        t -= LANES
    return t


def _mm(a, b, mode, out_dtype, name, tm=1024, tn=1024, tk=4096, epilogue=None, extra=(), deps=(), shards=None):
    if shards == "b":
        assert mode == "nn"
        ns = b.shape[2]
        (m, k), (k2, n) = a.shape, (b.shape[1], b.shape[0] * ns)
        tn = ns
    elif mode == "nn":
        (m, k), (k2, n) = a.shape, b.shape
    elif mode == "nt":
        (m, k), (n, k2) = a.shape, b.shape
    else:
        (k, m), (k2, n) = a.shape, b.shape
    assert k == k2, (name, a.shape, b.shape)
    tm, tn, tk = _tile(m, tm), _tile(n, tn), _tile(k, tk)
    nk = k // tk
    out_dtypes = out_dtype if epilogue else (out_dtype,)
    assert nk == 1 or (out_dtype == F32 and not epilogue), name
    n_in = 2 + len(extra)

    def body(*refs):
        p = _raw_dot(refs[0][...], refs[1][...], mode)
        if nk > 1:
            _accumulate(pl.program_id(2), [refs[n_in + len(deps)]], [p])
            return
        outs = epilogue(p, *[r[...] for r in refs[2:n_in]]) if epilogue else (p,)
        for ref, val in zip(refs[n_in + len(deps):], outs):
            ref[...] = val.astype(ref.dtype)

    a_spec = pl.BlockSpec((tk, tm), lambda i, j, kk: (kk, i)) if mode == "tn" else pl.BlockSpec((tm, tk), lambda i, j, kk: (i, kk))
    if shards == "b":
        b_spec = pl.BlockSpec((None, tk, tn), lambda i, j, kk: (j, kk, 0))
    elif mode == "nt":
        b_spec = pl.BlockSpec((tn, tk), lambda i, j, kk: (j, kk))
    else:
        b_spec = pl.BlockSpec((tk, tn), lambda i, j, kk: (kk, j))
    o_spec = pl.BlockSpec((tm, tn), lambda i, j, kk: (i, j))
    res = pl.pallas_call(
        body, name=name, grid=(m // tm, n // tn, nk),
        in_specs=[a_spec, b_spec] + [o_spec] * len(extra) + [ANY] * len(deps), out_specs=[o_spec] * len(out_dtypes),
        out_shape=[jax.ShapeDtypeStruct((m, n), dt) for dt in out_dtypes],
        compiler_params=_cparams("parallel", "parallel", "arbitrary"),
    )(a, b, *extra, *deps)
    return res if epilogue else res[0]


def _mm_tn_whole(pairs, name):
    def body(*refs):
        for i in range(len(pairs)):
            refs[2 * len(pairs) + i][...] = _raw_dot(refs[2 * i][...], refs[2 * i + 1][...], "tn").astype(BF16)

    return pl.pallas_call(
        body, name=name,
        out_shape=[jax.ShapeDtypeStruct((a.shape[1], b.shape[1]), BF16) for a, b in pairs],
        compiler_params=pltpu.CompilerParams(vmem_limit_bytes=VMEM_LIMIT),
    )(*[x for pair in pairs for x in pair])


def _shard_pieces(widths, n):
    bounds = [0]
    for wd in widths:
        bounds.append(bounds[-1] + wd)
    assert bounds[-1] == N_DEV * n
    return [[(i, max(s * n, b) - b, max(s * n, b) - s * n, min((s + 1) * n, b + wd) - max(s * n, b))
             for i, (b, wd) in enumerate(zip(bounds, widths)) if b < (s + 1) * n and b + wd > s * n]
            for s in range(N_DEV)]


def _mm_shard_cols(a, bs, widths, n, name, row_tile, tm, deps=()):
    t = a.shape[0]
    nb = len(bs)
    pieces = _shard_pieces(widths, n)

    def body(a_ref, *rest):
        b_refs = rest[:nb]
        o_ref, at_ref = rest[nb + len(deps):]
        j = pl.program_id(0)

        @pl.when(j == 0)
        def _():
            at_ref[...] = a_ref[...].astype(BF16).T

        for s in range(N_DEV):
            @pl.when(j == s)
            def _(s=s):
                for i, c_in, c_out, wd in pieces[s]:
                    cols = min(_round_up(wd, LANES), bs[i].shape[1] - c_in) if wd < LANES else wd
                    p = _raw_dot(at_ref[...], b_refs[i][:, c_in:c_in + cols], "nn")
                    o_ref[:, c_out:c_out + wd] = p[:, 0:wd].astype(BF16)

    return pl.pallas_call(
        body, name=name, grid=(N_DEV,),
        in_specs=[pl.BlockSpec((t, tm), lambda j: (0, row_tile))]
        + [pl.BlockSpec(b.shape, lambda j: (0, 0), pipeline_mode=pl.Buffered(1)) for b in bs] + [ANY] * len(deps),
        out_specs=pl.BlockSpec((None, tm, n), lambda j: (j, 0, 0)),
        out_shape=jax.ShapeDtypeStruct((N_DEV, tm, n), BF16),
        scratch_shapes=[pltpu.VMEM((tm, t), BF16)],
        compiler_params=_cparams("arbitrary"),
    )(a, *bs, *deps)


def _round_up(v, m):
    return -(-v // m) * m


def _rowwise(name, fn, rows, params, out_rows, out_accs=(), tile=256, deps=()):
    t = rows[0].shape[0]
    r = min(tile, t)
    assert t % r == 0
    n_in, n_or = len(rows) + len(params), len(out_rows)
    n_all = n_in + len(deps)
    params = list(params) + list(deps)

    def body(*refs):
        vals = [ref[...] for ref in refs[:n_in]]
        outs = refs[n_all:]
        ro, ao = fn(*vals)
        for ref, v in zip(outs[:n_or], ro):
            ref[...] = v.astype(ref.dtype)
        if out_accs:
            _accumulate(pl.program_id(0), outs[n_or:], ao)

    def full(shape, **kw):
        return pl.BlockSpec(shape, lambda i, nd=len(shape): (0,) * nd, **kw)

    return pl.pallas_call(
        body, name=name, grid=(t // r,),
        in_specs=[pl.BlockSpec((r, a.shape[1]), lambda i: (i, 0)) for a in rows]
        + [full(p.shape, pipeline_mode=pl.Buffered(1)) for p in params],
        out_specs=[pl.BlockSpec((r, w), lambda i: (i, 0)) for w, _ in out_rows] + [full(s) for s, _ in out_accs],
        out_shape=[jax.ShapeDtypeStruct((t, w), dt) for w, dt in out_rows] + [jax.ShapeDtypeStruct(s, dt) for s, dt in out_accs],
        compiler_params=_cparams("arbitrary"),
    )(*rows, *params)


def _accumulate(step, refs, vals):
    for ref, v in zip(refs, vals):
        @pl.when(step == 0)
        def _(ref=ref, v=v):
            ref[...] = v.astype(ref.dtype)

        @pl.when(step > 0)
        def _(ref=ref, v=v):
            ref[...] += v.astype(ref.dtype)


def _gla_chunk(q, k, v, la, st):
    c = q.shape[-2]
    row = lax.broadcasted_iota(jnp.int32, (c, c), 0)
    col = lax.broadcasted_iota(jnp.int32, (c, c), 1)
    cum = _cumsum_rows(la)
    cl = jnp.sum(la, axis=-2, keepdims=True)
    ep = jnp.exp(cum)
    en = jnp.exp(-cum)
    qs = q * (GLA_DK ** -0.5)
    qp = qs * ep
    a_f = _bdot(qp, k * en, "nt")
    a_b = _bdot(qs * en, k * ep, "nt")
    sc = jnp.where(row >= col, a_f, a_b)
    o = _bdot(sc, v, "nn") + _bdot(qp, st, "nt")
    kd = k * jnp.exp(cl - cum)
    st_new = st * jnp.exp(cl) + _bdot(v, kd, "tn")
    return o, st_new


def _gla_specs(nc, rev):
    nb = nc // CHUNKS_PER_STEP
    rows = CHUNKS_PER_STEP * CHUNK

    def blk(n):
        return (nb - 1 - n) if rev else n
    hm = pl.BlockSpec((HEADS, rows, GLA_DK), lambda n: (0, blk(n), 0))
    tm = pl.BlockSpec((rows, HEADS * HEAD_W), lambda n: (blk(n), 0))
    st = pl.BlockSpec((HEADS, CHUNKS_PER_STEP, HEAD_W, GLA_DK), lambda n: (0, blk(n), 0, 0))
    return nb, hm, tm, st


def _chunk_rows(c):
    return slice(c * CHUNK, (c + 1) * CHUNK)


def _gla_fwd(q, k, v, la, deps=()):
    t = v.shape[0]
    nc = t // CHUNK
    nb, hm, tm, st = _gla_specs(nc, False)

    def body(q_ref, k_ref, v_ref, la_ref, *rest):
        o_ref, sp_ref, st_ref = rest[len(deps):]

        @pl.when(pl.program_id(0) == 0)
        def _():
            st_ref[...] = jnp.zeros_like(st_ref)

        s = st_ref[...]
        for c in range(CHUNKS_PER_STEP):
            r = _chunk_rows(c)
            sp_ref[:, c] = s
            o, s = _gla_chunk(q_ref[:, r], k_ref[:, r], _heads(v_ref, r), la_ref[:, r], s)
            _put_heads(o_ref, o, r)
        st_ref[...] = s

    return pl.pallas_call(
        body, name="gla_fwd", grid=(nb,),
        in_specs=[hm, hm, tm, hm] + [ANY] * len(deps), out_specs=[tm, st],
        out_shape=[jax.ShapeDtypeStruct((t, HEADS * HEAD_W), F32), jax.ShapeDtypeStruct((HEADS, nc, HEAD_W, GLA_DK), F32)],
        scratch_shapes=[pltpu.VMEM((HEADS, HEAD_W, GLA_DK), F32)],
        compiler_params=_cparams("arbitrary"),
    )(q, k, v, la, *deps)


def _ml_chunk(q, k, v, li_r, lf_r, cm, nv, m):
    c = q.shape[-2]
    row = lax.broadcasted_iota(jnp.int32, (c, c), 0)
    col = lax.broadcasted_iota(jnp.int32, (c, c), 1)
    eye = (row == col).astype(F32)
    li_c = jnp.sum(eye * li_r, axis=-1, keepdims=True)
    lf_c = jnp.sum(eye * lf_r, axis=-1, keepdims=True)
    fc_c = jnp.sum((col <= row).astype(F32) * lf_r, axis=-1, keepdims=True)
    fc_r = jnp.sum((row <= col).astype(F32) * lf_c, axis=-2, keepdims=True)
    f_last = jnp.sum(lf_r, axis=-1, keepdims=True)
    kc = k * (HEAD_W ** -0.5)
    a_c = f_last - fc_c + li_c
    m_loc = jnp.max(a_c, axis=-2, keepdims=True)
    kw = kc * jnp.exp(a_c - m_loc)
    c_chunk = _bdot(kw, v, "tn")
    n_chunk = jnp.sum(kw, axis=-2, keepdims=True)
    m_new = jnp.maximum(f_last + m, m_loc)
    sp = jnp.exp(f_last + m - m_new)
    sl = jnp.exp(m_loc - m_new)
    cm_new = sp * cm + sl * c_chunk
    nv_new = sp * nv + sl * n_chunk
    log_d = li_r - _abs(fc_c - fc_r)
    g_inter = fc_c + m
    m_t = jnp.maximum(g_inter, jnp.max(log_d, axis=-1, keepdims=True))
    s = _bdot(q, kc, "nt") * jnp.exp(log_d - m_t)
    sc = jnp.exp(g_inter - m_t)
    num = _bdot(s, v, "nn") + sc * _bdot(q, cm, "nn")
    den = jnp.sum(s, axis=-1, keepdims=True) + sc * jnp.sum(q * nv, axis=-1, keepdims=True)
    den = jnp.maximum(_abs(den), jnp.exp(-m_t))
    return num / den, cm_new, nv_new, m_new


def _ml_specs(nc, rev):
    nb = nc // CHUNKS_PER_STEP

    def blk(n):
        return (nb - 1 - n) if rev else n
    tm = pl.BlockSpec((CHUNKS_PER_STEP * CHUNK, HEADS * HEAD_W), lambda n: (blk(n), 0))
    gate = pl.BlockSpec((HEADS, CHUNKS_PER_STEP, 1, CHUNK), lambda n: (0, blk(n), 0, 0))
    cm = pl.BlockSpec((HEADS, CHUNKS_PER_STEP, HEAD_W, HEAD_W), lambda n: (0, blk(n), 0, 0))
    vec = pl.BlockSpec((HEADS, CHUNKS_PER_STEP, 1, HEAD_W), lambda n: (0, blk(n), 0, 0))
    return nb, tm, gate, cm, vec


_ML_STATE = [pltpu.VMEM((HEADS, HEAD_W, HEAD_W), F32), pltpu.VMEM((HEADS, 1, HEAD_W), F32), pltpu.VMEM((HEADS, 1, HEAD_W), F32)]


def _ml_fwd(q, k, v, li, lf):
    t = q.shape[0]
    nc = t // CHUNK
    nb, tm, gate, cm, vec = _ml_specs(nc, False)

    def body(q_ref, k_ref, v_ref, li_ref, lf_ref, hc_ref, cp_ref, np_ref, mp_ref, c_ref, n_ref, m_ref):
        @pl.when(pl.program_id(0) == 0)
        def _():
            c_ref[...] = jnp.zeros_like(c_ref)
            n_ref[...] = jnp.zeros_like(n_ref)
            m_ref[...] = jnp.zeros_like(m_ref)

        cs, ns, ms = c_ref[...], n_ref[...], m_ref[...][:, :, 0:1]
        for c in range(CHUNKS_PER_STEP):
            r = _chunk_rows(c)
            cp_ref[:, c] = cs
            np_ref[:, c] = ns
            mp_ref[:, c] = jnp.broadcast_to(ms, m_ref.shape)
            hc, cs, ns, ms = _ml_chunk(_heads(q_ref, r), _heads(k_ref, r), _heads(v_ref, r), li_ref[:, c], lf_ref[:, c],
                                       cs, ns, ms)
            _put_heads(hc_ref, hc, r)
        c_ref[...] = cs
        n_ref[...] = ns
        m_ref[...] = jnp.broadcast_to(ms, m_ref.shape)

    return pl.pallas_call(
        body, name="mlstm_fwd", grid=(nb,),
        in_specs=[tm, tm, tm, gate, gate], out_specs=[tm, cm, vec, vec],
        out_shape=[jax.ShapeDtypeStruct((t, HEADS * HEAD_W), F32), jax.ShapeDtypeStruct((HEADS, nc, HEAD_W, HEAD_W), F32),
                   jax.ShapeDtypeStruct((HEADS, nc, 1, HEAD_W), F32), jax.ShapeDtypeStruct((HEADS, nc, 1, HEAD_W), F32)],
        scratch_shapes=_ML_STATE,
        compiler_params=_cparams("arbitrary"),
    )(q, k, v, li, lf)


def _recurrences_bwd(q, k, v, la, sp, do, qm, km, vm, li, lf, cp, npv, mp, dhc, deps=()):
    t = v.shape[0]
    nc = t // CHUNK
    nb, hm, tm, st = _gla_specs(nc, True)
    _, _, gate, cm, vec = _ml_specs(nc, True)
    n_in = 15 + len(deps)

    def body(*refs):
        (q_ref, k_ref, v_ref, la_ref, sp_ref, do_ref,
         qm_ref, km_ref, vm_ref, li_ref, lf_ref, cp_ref, np_ref, mp_ref, dhc_ref) = refs[:15]
        (dq_ref, dk_ref, dv_ref, dla_ref, dqm_ref, dkm_ref, dvm_ref, dli_ref, dlf_ref,
         ds_ref, dc_ref, dn_ref, dm_ref) = refs[n_in:]

        @pl.when(pl.program_id(0) == 0)
        def _():
            for ref in (ds_ref, dc_ref, dn_ref, dm_ref):
                ref[...] = jnp.zeros_like(ref)

        ds, dc, dn, dm = ds_ref[...], dc_ref[...], dn_ref[...], dm_ref[...][:, :, 0:1]
        for c in reversed(range(CHUNKS_PER_STEP)):
            r = _chunk_rows(c)
            _, vjp = jax.vjp(_gla_chunk, q_ref[:, r], k_ref[:, r], _heads(v_ref, r), la_ref[:, r], sp_ref[:, c])
            dq, dk, dv, dla, ds = vjp((_heads(do_ref, r), ds))
            dq_ref[:, r] = dq.astype(dq_ref.dtype)
            dk_ref[:, r] = dk.astype(dk_ref.dtype)
            _put_heads(dv_ref, dv, r)
            dla_ref[:, r] = dla
            _, vjp = jax.vjp(_ml_chunk, _heads(qm_ref, r), _heads(km_ref, r), _heads(vm_ref, r), li_ref[:, c], lf_ref[:, c],
                             cp_ref[:, c], np_ref[:, c], mp_ref[:, c][:, :, 0:1])
            dqm, dkm, dvm, dli, dlf, dc, dn, dm = vjp((_heads(dhc_ref, r), dc, dn, dm))
            _put_heads(dqm_ref, dqm, r)
            _put_heads(dkm_ref, dkm, r)
            _put_heads(dvm_ref, dvm, r)
            dli_ref[:, c] = dli
            dlf_ref[:, c] = dlf
        ds_ref[...] = ds
        dc_ref[...] = dc
        dn_ref[...] = dn
        dm_ref[...] = jnp.broadcast_to(dm, dm_ref.shape)

    hm_shape = jax.ShapeDtypeStruct((HEADS, t, GLA_DK), BF16)
    tm_shape = jax.ShapeDtypeStruct((t, HEADS * HEAD_W), F32)
    gate_shape = jax.ShapeDtypeStruct((HEADS, nc, 1, CHUNK), F32)
    return pl.pallas_call(
        body, name="recurrences_bwd", grid=(nb,),
        in_specs=[hm, hm, tm, hm, st, tm, tm, tm, tm, gate, gate, cm, vec, vec, tm] + [ANY] * len(deps),
        out_specs=[hm, hm, tm, hm, tm, tm, tm, gate, gate],
        out_shape=[hm_shape, hm_shape, jax.ShapeDtypeStruct((t, HEADS * HEAD_W), BF16),
                   jax.ShapeDtypeStruct((HEADS, t, GLA_DK), F32), tm_shape, tm_shape, tm_shape, gate_shape, gate_shape],
        scratch_shapes=[pltpu.VMEM((HEADS, HEAD_W, GLA_DK), F32)] + _ML_STATE,
        compiler_params=_cparams("arbitrary"),
    )(q, k, v, la, sp, do, qm, km, vm, li, lf, cp, npv, mp, dhc, *deps)


@jax.custom_vjp
def _bdot_diag(x, w):
    b = w.shape[1]
    return jnp.concatenate([_raw_dot(x[:, :b], w[0], "nn"), _raw_dot(x[:, b:], w[1], "nn")], axis=1)


def _bdot_diag_fwd(x, w):
    return _bdot_diag(x, w), (x, w)


def _bdot_diag_bwd(res, ct):
    x, w = res
    b = w.shape[1]
    dx = jnp.concatenate([_raw_dot(ct[:, :b], w[0], "nt"), _raw_dot(ct[:, b:], w[1], "nt")], axis=1)
    dw = jnp.stack([_raw_dot(x[:, :b], ct[:, :b], "tn"), _raw_dot(x[:, b:], ct[:, b:], "tn")])
    return dx.astype(x.dtype), dw.astype(w.dtype)


_bdot_diag.defvjp(_bdot_diag_fwd, _bdot_diag_bwd)


def _ml_pre(s0, s1, s2, s3, cw0, cw1, cw2, cw3, cb, wq, wk, wv, wiq, wik, wiv, bif):
    pre = cb + cw0 * s0 + cw1 * s1 + cw2 * s2 + cw3 * s3
    xc = pre * _sigmoid(pre)
    q = _bdot_diag(xc, wq)
    k = _bdot_diag(xc, wk)
    v = _bdot_diag(s3, wv)
    gates = _bdot(q, wiq, "nn") + _bdot(k, wik, "nn") + _bdot(v, wiv, "nn") + bif
    lane = lax.broadcasted_iota(jnp.int32, gates.shape, 1)
    gl = jnp.where(lane < HEADS, gates, _log_sigmoid(gates))
    return xc, q, k, v, gl


def _delayed(xs_ref, x_ref, halo_ref, r):
    xs_ref[0:HALO, :] = halo_ref[...]
    xs_ref[HALO:HALO + r, :] = x_ref[...]
    return [xs_ref[pl.ds(HALO - (CONV_K - 1) + j, r), :] for j in range(CONV_K)]


def _full_spec(shape):
    return pl.BlockSpec(shape, lambda i, nd=len(shape): (0,) * nd)


def _ml_pre_fwd(x_m, x_pad, params, tile=256, deps=()):
    t, w = x_m.shape
    r = min(tile, t)

    def body(*refs):
        x_ref, halo_ref = refs[:2]
        p = [ref[...] for ref in refs[2:2 + len(params)]]
        outs = refs[2 + len(params) + len(deps):-1]
        res = _ml_pre(*_delayed(refs[-1], x_ref, halo_ref, r), *p)
        for ref, val in zip(outs, res):
            ref[...] = val

    row = pl.BlockSpec((r, w), lambda i: (i, 0))
    return pl.pallas_call(
        body, name="ml_pre_fwd", grid=(t // r,),
        in_specs=[row, pl.BlockSpec((HALO, w), lambda i: (i * (r // HALO), 0))] + [_full_spec(p.shape) for p in params]
        + [ANY] * len(deps),
        out_specs=[row] * 4 + [pl.BlockSpec((r, LANES), lambda i: (i, 0))],
        out_shape=[jax.ShapeDtypeStruct((t, w), F32)] * 4 + [jax.ShapeDtypeStruct((t, LANES), F32)],
        scratch_shapes=[pltpu.VMEM((r + HALO, w), F32)],
        compiler_params=_cparams("arbitrary"),
    )(x_m, x_pad, *params, *deps)


def _ml_pre_bwd(x_m, x_pad, params, cts, tile=256):
    t, w = x_m.shape
    r = min(tile, t)
    nt = t // r
    n_p = len(params)

    def body(*refs):
        x_ref, halo_ref = refs[:2]
        p = [ref[...] for ref in refs[2:2 + n_p]]
        ct = [ref[...] for ref in refs[2 + n_p:7 + n_p]]
        dx_ref = refs[7 + n_p]
        dp_refs = refs[8 + n_p:8 + 2 * n_p]
        xs_ref, ds_ref, carry_ref = refs[8 + 2 * n_p:]
        step = pl.program_id(0)

        @pl.when(step == 0)
        def _():
            ds_ref[...] = jnp.zeros_like(ds_ref)
            carry_ref[...] = jnp.zeros_like(carry_ref)

        _, vjp = jax.vjp(_ml_pre, *_delayed(xs_ref, x_ref, halo_ref, r), *p)
        grads = vjp(tuple(ct))
        for j in range(CONV_K):
            ds_ref[j, HALO:HALO + r, :] = grads[j]
        lead = HALO + CONV_K - 1
        d_tile = sum(ds_ref[j, pl.ds(lead - j, r), :] for j in range(CONV_K))
        d_halo = sum(ds_ref[j, pl.ds(CONV_K - 1 - j, HALO), :] for j in range(CONV_K))
        dx_ref[...] = jnp.concatenate([d_tile[:r - HALO], d_tile[r - HALO:] + carry_ref[...]], axis=0).astype(dx_ref.dtype)
        carry_ref[...] = d_halo
        _accumulate(step, dp_refs, grads[CONV_K:])

    row = pl.BlockSpec((r, w), lambda i: (nt - 1 - i, 0))
    return pl.pallas_call(
        body, name="ml_pre_bwd", grid=(nt,),
        in_specs=[row, pl.BlockSpec((HALO, w), lambda i: ((nt - 1 - i) * (r // HALO), 0))] + [_full_spec(p.shape) for p in params]
        + [row] * 4 + [pl.BlockSpec((r, LANES), lambda i: (nt - 1 - i, 0))],
        out_specs=[row] + [_full_spec(p.shape) for p in params],
        out_shape=[jax.ShapeDtypeStruct((t, w), BF16)] + [jax.ShapeDtypeStruct(p.shape, F32) for p in params],
        scratch_shapes=[pltpu.VMEM((r + HALO, w), F32), pltpu.VMEM((CONV_K, r + 2 * HALO, w), F32), pltpu.VMEM((HALO, w), F32)],
        compiler_params=_cparams("arbitrary"),
    )(x_m, x_pad, *params, *cts)


def _per_head(fn, row_vals, head_params, shared_params=()):
    return [fn(*[a[:, hs] for a in row_vals], *[p[:, hs] for p in head_params], *shared_params) for hs in _head_slices(HEAD_W)]


def _gla_out(o, g, gn):
    return _rms(o, gn) * (g * _sigmoid(g))


def _ml_out(hc, op, xc, g, sk):
    hcell = hc * _sigmoid(op)
    mu = jnp.mean(hcell, axis=-1, keepdims=True)
    d = hcell - mu
    var = jnp.mean(d * d, axis=-1, keepdims=True)
    return d * lax.rsqrt(var + EPS) * g + sk * xc


def _log_decay(al, w, b):
    return _log_sigmoid(_bdot(al, w, "nn") + b) * (1.0 / GLA_GATE_NORM)


def _merge(ga, gb, ya, yb):
    ga, gb, ya, yb = (a.astype(F32) for a in (ga, gb, ya, yb))
    return _sigmoid(ga) * ya + _sigmoid(gb) * yb


def _post_mix(x, z, gpm, gpl):
    x1 = x + _rms(z, gpm)
    return x1, _rms(x1, gpl)


def _loss_rows(x1, dn, tgt, g):
    e = x1 + _rms(dn, g) - tgt
    return 0.5 * jnp.sum(jnp.mean(e * e, axis=-1, keepdims=True), axis=0, keepdims=True)


def _lin(p):
    return 4 * p[0] + 2 * p[1] + p[2]


def _me():
    return lax.axis_index("x"), lax.axis_index("y"), lax.axis_index("c")


def _flip(p, k):
    return tuple((1 - v) if (k >> (2 - i)) & 1 else v for i, v in enumerate(p))


ANY = pl.BlockSpec(memory_space=pl.ANY)


HBM = pl.BlockSpec(memory_space=pltpu.HBM)
SEM = pl.BlockSpec(memory_space=pltpu.SEMAPHORE)
DATAFLOW = pltpu.SideEffectType.DATAFLOW_SIDE_EFFECTING


SIBLING = 1
OTHER_CHIPS = (2, 4, 6)


def _peer_copies(kinds, srcs, lands, send_sems, recv_sems):
    me = _me()
    copies = []
    for a, (kind, src, land) in enumerate(zip(kinds, srcs, lands)):
        masks = {"gather": range(1, N_DEV), "exchange": range(1, N_DEV), "gather_chips": (SIBLING, *OTHER_CHIPS),
                 "gather_pass": OTHER_CHIPS}[kind]
        for k in masks:
            peer = _flip(me, k)
            if kind == "gather_pass":
                block = land.at[_lin(peer)]
                src_ref, dst_ref, target = block, block, _flip(me, SIBLING)
            else:
                src_ref, dst_ref, target = (src.at[_lin(peer)] if kind == "exchange" else src), land.at[_lin(me)], peer
            copies.append(pltpu.make_async_remote_copy(
                src_ref=src_ref, dst_ref=dst_ref, send_sem=send_sems.at[a * 7 + k - 1], recv_sem=recv_sems.at[a * 7 + k - 1],
                device_id=target, device_id_type=MESH))
    return copies


def _copies_start(kind, srcs, name, after=None, lands=None):
    n = len(srcs)
    extra = [] if after is None else [after]
    kind = [kind] * n if isinstance(kind, str) else list(kind)
    land_shapes = [(s.shape if k == "exchange" else (N_DEV, *s.shape)) for k, s in zip(kind, srcs)]
    lands = [lax.empty(ls, s.dtype) for ls, s in zip(land_shapes, srcs)] if lands is None else lands

    def body(*refs):
        sems = refs[2 * n + len(extra):]
        for cp in _peer_copies(kind, refs[:n], refs[n:2 * n], sems[0], sems[1]):
            cp.start()
        refs[-1][...] = jnp.zeros_like(refs[-1])

    def hbm(a):
        return pltpu.with_memory_space_constraint(a, pltpu.HBM)

    out = pl.pallas_call(
        body, name=name,
        out_shape=(pltpu.SemaphoreType.DMA((7 * n,)), pltpu.SemaphoreType.DMA((7 * n,)),
                   *[pltpu.HBM(s.shape, s.dtype) for s in srcs],
                   *[pltpu.HBM(ls, s.dtype) for ls, s in zip(land_shapes, srcs)],
                   jax.ShapeDtypeStruct((8, LANES), F32)),
        in_specs=[HBM] * (2 * n) + [ANY] * len(extra),
        out_specs=(SEM, SEM, *[HBM] * (2 * n), pl.BlockSpec(memory_space=pltpu.VMEM)),
        input_output_aliases={i: 2 + i for i in range(2 * n)},
        compiler_params=pltpu.CompilerParams(has_side_effects=DATAFLOW),
    )(*[hbm(s) for s in srcs], *[hbm(a) for a in lands], *extra)
    return (kind, n, out[:-1]), out[-1]


def _copies_wait(state, after, name):
    kind, n, (send_sems, recv_sems, *thru) = state
    after = list(after) if isinstance(after, (list, tuple)) else [after]

    def body(*refs):
        for cp in _peer_copies(kind, refs[:n], refs[n:2 * n], refs[2 * n], refs[2 * n + 1]):
            cp.wait_send()
            cp.wait_recv()

    out = pl.pallas_call(
        body, name=name,
        out_shape=tuple(pltpu.HBM(t.shape, t.dtype) for t in thru),
        in_specs=[HBM] * (2 * n) + [SEM, SEM] + [ANY] * len(after), out_specs=tuple([HBM] * (2 * n)),
        input_output_aliases={i: i for i in range(2 * n)},
        compiler_params=pltpu.CompilerParams(has_side_effects=DATAFLOW),
    )(*thru, send_sems, recv_sems, *after)
    return out[:n], out[n:]


def _adamw(w, g, m, v):
    m2 = ADAM_B1 * m + (1.0 - ADAM_B1) * g
    v2 = ADAM_B2 * v + (1.0 - ADAM_B2) * (g * g)
    m_hat = m2 / (1.0 - ADAM_B1 ** ADAM_STEP)
    v_hat = v2 / (1.0 - ADAM_B2 ** ADAM_STEP)
    delta = -ADAM_LR * (m_hat / (jnp.sqrt(v_hat) + ADAM_EPS) + ADAM_WD * w)
    return delta, m2, v2


def _sum_adamw(lands, parts, me_idx, w, m, v, name, tile=256):
    r, c = w.shape
    nchunks = len(lands)
    tr = min(tile, r // nchunks)
    per_chunk = r // nchunks // tr
    per = 1 + N_DEV

    def body(me_ref, *refs):
        w_ref, m_ref, v_ref, g_ref, d_ref, m2_ref, v2_ref = refs[nchunks * per:]
        for k in range(nchunks):
            own_ref, slots = refs[k * per], refs[k * per + 1:(k + 1) * per]

            @pl.when(pl.program_id(0) // per_chunk == k)
            def _(own_ref=own_ref, slots=slots):
                own = own_ref[...].astype(F32)
                g = None
                for s in range(N_DEV):
                    term = jnp.where(me_ref[0] == s, own, slots[s][...].astype(F32))
                    g = term if g is None else g + term
                d, m2, v2 = _adamw(w_ref[...], g, m_ref[...], v_ref[...])
                g_ref[...] = g
                d_ref[...] = d
                m2_ref[...] = m2
                v2_ref[...] = v2

    def chunk_specs(k):
        def tile_of(i):
            return jnp.clip(i - k * per_chunk, 0, per_chunk - 1)

        def slot_spec(s):
            return pl.BlockSpec((None, tr, c), lambda i, me: (jnp.where(me[0] == s, (s + 1) % N_DEV, s), tile_of(i), 0))
        return [pl.BlockSpec((None, tr, c), lambda i, me: (me[0], tile_of(i), 0))] + [slot_spec(s) for s in range(N_DEV)]

    row = pl.BlockSpec((tr, c), lambda i, me: (i, 0))
    operands = [a for land, part in zip(lands, parts) for a in (part, *[land] * N_DEV)]
    return pl.pallas_call(
        body, name=name,
        grid_spec=pltpu.PrefetchScalarGridSpec(
            num_scalar_prefetch=1, grid=(r // tr,),
            in_specs=[s for k in range(nchunks) for s in chunk_specs(k)] + [row] * 3,
            out_specs=[row] * 4),
        out_shape=[jax.ShapeDtypeStruct((r, c), F32)] * 4,
        compiler_params=_cparams("parallel"),
    )(me_idx, *operands, w, m, v)


def _small_update(name, me_idx, kinds, lands, owns, ws, ms, vs, sums=()):
    n = len(ws)
    lands, owns = list(lands) + [s[0] for s in sums], list(owns) + [s[1] for s in sums]
    kinds = list(kinds) + ["gather"] * len(sums)
    nl = len(lands)

    def summed(me, land_ref, own):
        g = None
        for s in range(N_DEV):
            term = jnp.where(me == s, own, land_ref[s]).astype(F32)
            g = term if g is None else g + term
        return g

    def body(me_ref, *refs):
        land_refs, own_refs = refs[:nl], refs[nl:2 * nl]
        w_refs, m_refs, v_refs = (refs[2 * nl + i * n:2 * nl + (i + 1) * n] for i in range(3))
        outs = refs[2 * nl + 3 * n:]
        me = me_ref[0]
        for i in range(n):
            g = summed(me, land_refs[i], own_refs[i][...])
            d, m2, v2 = _adamw(w_refs[i][...], g, m_refs[i][...], v_refs[i][...])
            for ref, val in zip(outs[4 * i:4 * i + 4], (g, d, m2, v2)):
                ref[...] = val
        for i in range(n, nl):
            outs[4 * n + i - n][...] = summed(me, land_refs[i], own_refs[i][...])

    def whole(shape):
        return pl.BlockSpec(shape, lambda i, me, nd=len(shape): (0,) * nd)

    def own_spec(kind, own):
        if kind == "gather":
            return whole(own.shape)
        return pl.BlockSpec((None, *own.shape[1:]), lambda i, me: (me[0], 0, 0))

    shapes = [w.shape for w in ws]
    out_shapes = [s for s in shapes for _ in range(4)] + [s[1].shape for s in sums]
    return pl.pallas_call(
        body, name=name,
        grid_spec=pltpu.PrefetchScalarGridSpec(
            num_scalar_prefetch=1, grid=(1,),
            in_specs=[whole(a.shape) for a in lands] + [own_spec(k, o) for k, o in zip(kinds, owns)]
            + [whole(s) for s in shapes] * 3,
            out_specs=[whole(s) for s in out_shapes]),
        out_shape=[jax.ShapeDtypeStruct(s, F32) for s in out_shapes],
        compiler_params=_cparams("arbitrary"),
    )(me_idx, *lands, *owns, *ws, *ms, *vs)


def _small_view(n, a):
    if a.ndim == 1:
        return a.reshape(1, -1)
    if a.ndim == 3:
        return a.transpose(1, 2, 0).reshape(QKV_BLOCK * QKV_BLOCK, -1)
    return a.T if n == "w_if" else a


def _small_unview(n, a, shape):
    if len(shape) == 1:
        return a.reshape(shape)
    if len(shape) == 3:
        return a.reshape(QKV_BLOCK, QKV_BLOCK, -1).transpose(2, 0, 1)
    return a.T if n == "w_if" else a


def _small_shards(n, g):
    if n == "w_if":
        return g.reshape(N_DEV, -1, g.shape[1]).transpose(0, 2, 1)
    return g.reshape(g.shape[0], N_DEV, -1).transpose(1, 0, 2)


def _small_unshard(n, s):
    if n == "w_if":
        return s.transpose(0, 2, 1).reshape(-1, s.shape[1])
    return s.transpose(1, 0, 2).reshape(s.shape[1], -1)


def _to_hm(a, d):
    t = a.shape[0]
    return a.reshape(t, HEADS, d).transpose(1, 0, 2)


def _from_hm(a):
    h, t, d = a.shape
    return a.transpose(1, 0, 2).reshape(t, h * d)


def _gate_rows(g):
    t = g.shape[0]
    return g.T.reshape(HEADS, t // CHUNK, 1, CHUNK)


def _gate_cols(g):
    h, nc, _, c = g.shape
    return g.reshape(h, nc * c).T


def _blockdiag_dense(w):
    n = w.shape[0] * QKV_BLOCK // 2
    tiled = jnp.tile(w.reshape(2, n, QKV_BLOCK), (1, 1, n // QKV_BLOCK))
    r = lax.broadcasted_iota(jnp.int32, (2, n, n), 1)
    c = lax.broadcasted_iota(jnp.int32, (2, n, n), 2)
    return jnp.where(r // QKV_BLOCK == c // QKV_BLOCK, tiled, 0.0)


def _blockdiag_blocks(dense):
    _, n, _ = dense[0].shape
    k = len(dense)

    def body(*refs):
        r = lax.broadcasted_iota(jnp.int32, (n, n), 0)
        c = lax.broadcasted_iota(jnp.int32, (n, n), 1)
        fr = lax.broadcasted_iota(jnp.int32, (n, LANES), 0)
        fc = lax.broadcasted_iota(jnp.int32, (n, LANES), 1)
        fold = ((fr & (QKV_BLOCK - 1)) == fc).astype(BF16)
        for i in range(k):
            for half in range(2):
                kept = jnp.where((r >> 2) == (c >> 2), refs[i][half], 0.0)
                refs[k + i][half] = sum(lax.dot_general(t, fold, _dims("nn", 2), preferred_element_type=F32)
                                        for t in _split3(kept))

    out = pl.pallas_call(body, name="blockdiag_blocks", out_shape=[jax.ShapeDtypeStruct((2, n, LANES), F32)] * k)(*dense)
    return [o[:, :, 0:QKV_BLOCK].reshape(2 * n // QKV_BLOCK, QKV_BLOCK, QKV_BLOCK) for o in out]


def _col_blocks(w):
    k, n = w.shape
    return w.reshape(k, N_DEV, n // N_DEV).transpose(1, 0, 2)


def _from_col_blocks(g):
    d, k, n = g.shape
    return g.transpose(1, 0, 2).reshape(k, d * n)


def _first_norm(x, g):
    return _rowwise("pre_mix_norm", lambda xv, gv: ((_rms(xv, gv),), ()), [x], [g], [(x.shape[1], BF16)])[0]


def _local_step(x, h, tgt, weight, ws, prefetch, pass_on, on_grads, on_small):
    t, d = x.shape
    g1 = ws["g_pre_mix"]

    def dep(token):
        return () if token is None else (token,)

    w_in = weight("w_in", x)
    fetch_mix = prefetch(("w_pa", "w_pb", "w_o"), w_in)

    n_in = w_in.shape[2]

    offs = [0]
    for s in IN_SPLITS:
        offs.append(offs[-1] + s)

    w_a_up_p = jnp.pad(ws["w_a_up"], ((0, LANES - LOWRANK), (0, 0)))
    b_a_up = ws["b_a_up"]

    def proj_in_fwd(hv, w, wa, ba):
        proj = jnp.concatenate([_raw_dot(hv, w[j], "nn") for j in range(N_DEV)], axis=1)
        parts = [proj[:, offs[i]:offs[i + 1]] for i in range(len(IN_SPLITS))]
        parts[4] = jnp.concatenate([parts[4], jnp.zeros((parts[4].shape[0], LANES - LOWRANK), F32)], axis=1)
        return (*parts, _log_decay(parts[4], wa, ba)), ()

    widths = [LANES if s == LOWRANK else s for s in IN_SPLITS]
    q_a, k_a, v_a, g_a, a_low_p, x_m, o_pre, gate_a, gate_b, la = _rowwise(
        "proj_in", proj_in_fwd, [h], [w_in, w_a_up_p, b_a_up],
        [(wd, BF16 if i == 2 else F32) for i, wd in enumerate(widths)] + [(HEADS * GLA_DK, F32)],
        deps=dep(fetch_mix))

    fetch_up = prefetch(("w_up", "w_down"), la)
    q_hm, k_hm, la_hm = _to_hm(q_a, GLA_DK), _to_hm(k_a, GLA_DK), _to_hm(la, GLA_DK)
    o_gla, s_prev = _gla_fwd(q_hm, k_hm, v_a, la_hm, deps=dep(fetch_up))
    pass_mix = pass_on("w_pa", o_gla)
    gn = ws["g_gla_norm"]
    ml_w = HEADS * HEAD_W

    cw = ws["conv_w"]
    w_if_p = jnp.pad(ws["w_if"], ((0, 0), (0, LANES - 2 * HEADS)))
    pre_params = [cw[0:1], cw[1:2], cw[2:3], cw[3:4], ws["conv_b"],
                  _blockdiag_dense(ws["w_q_ml"]), _blockdiag_dense(ws["w_k_ml"]), _blockdiag_dense(ws["w_v_ml"]),
                  w_if_p[0:ml_w], w_if_p[ml_w:2 * ml_w], w_if_p[2 * ml_w:3 * ml_w],
                  jnp.pad(ws["b_if"], ((0, 0), (0, LANES - 2 * HEADS)))]
    x_pad = jnp.pad(x_m, ((HALO, 0), (0, 0)))
    xc, q_m, k_m, v_m, gl = _ml_pre_fwd(x_m, x_pad, pre_params, tile=512, deps=dep(pass_mix))
    li, lf = _gate_rows(gl[:, 0:HEADS]), _gate_rows(gl[:, HEADS:2 * HEADS])
    hc, c_prev, n_prev, m_prev = _ml_fwd(q_m, k_m, v_m, li, lf)
    g_ml, skip = ws["g_ml_norm"], ws["ml_skip"]

    def branches_out(o, g, a, b, c_, ga, gb, n_, wa, gm, s, wb):
        ya_in = jnp.concatenate(_per_head(_gla_out, [o, g], [], [n_]), axis=1)
        ya = _raw_dot(ya_in, wa, "nn")
        hb = jnp.concatenate(_per_head(_ml_out, [a, b, c_], [gm, s]), axis=1)
        yb = _raw_dot(hb, wb, "nn")
        return (ya_in, ya, hb, yb, _merge(ga, gb, ya, yb)), ()

    ya_in, y_a, h_b, y_b, merged = _rowwise(
        "branches_out", branches_out, [o_gla, g_a, hc, o_pre, xc, gate_a, gate_b],
        [gn, weight("w_pa", hc), g_ml, skip, weight("w_pb", hc)],
        [(ml_w, BF16), (d, BF16), (ml_w, BF16), (d, BF16), (d, BF16)], tile=512)

    gpm, gpl, gpo = ws["g_post_mix"], ws["g_pre_mlp"], ws["g_post_mlp"]

    def proj_o_fwd(mg, xv, w, a, b):
        zv = _raw_dot(mg, w, "nn")
        return (zv, *_post_mix(xv, zv, a, b)), ()

    pass_up = pass_on("w_up", merged)
    z, x1, h2 = _rowwise("proj_o", proj_o_fwd, [merged, x], [weight("w_o", merged), gpm, gpl],
                         [(d, F32), (d, F32), (d, BF16)], tile=512, deps=dep(pass_up))
    w_up = weight("w_up", h2)
    w_down = weight("w_down", h2)
    d_ff = w_down.shape[0]

    def mlp_loss(h2v, x1v, tgtv, wu, wd, g):
        upv = jnp.concatenate([_raw_dot(h2v, wu[j], "nn") for j in range(wu.shape[0])], axis=1)
        uv = jnp.square(jnp.maximum(upv, 0.0))
        dnv = _raw_dot(uv, wd, "nn")
        loss, vjp = jax.vjp(lambda a, b, c_: _loss_rows(a, b, tgtv, c_), x1v, dnv, g)
        dx1, ddn, dg = vjp(jnp.ones((1, 1), F32))
        return (upv, uv, dx1, ddn), (jnp.broadcast_to(loss, (1, LANES)), dg)

    up, u, dx1_y, d_dn, loss, d_gpo = _rowwise("mlp_loss", mlp_loss, [h2, x1, tgt], [w_up, w_down, gpo],
                                               [(d_ff, BF16), (d_ff, BF16), (d, F32), (d, BF16)],
                                               [((1, LANES), F32), ((1, d), F32)])

    dw_down = _mm(u, d_dn, "tn", BF16, "mlp_down_dw", tm=512)

    def mlp_dx(ddn, upv, xv, zv, dx1, wd, wu, a, b):
        dup = (_raw_dot(ddn, wd, "nt") * (2.0 * jnp.maximum(upv.astype(F32), 0.0))).astype(BF16)
        ns = wu.shape[2]
        dh2 = sum(_raw_dot(dup[:, j * ns:(j + 1) * ns], wu[j], "nt") for j in range(wu.shape[0]))
        _, vjp = jax.vjp(_post_mix, xv, zv, a, b)
        dx, dz, da, db = vjp((dx1, dh2))
        return (dup, dx, dz), (da, db)

    d_up, dx_res, d_z, d_gpm, d_gpl = _rowwise("mlp_dx", mlp_dx, [d_dn, up, x, z, dx1_y], [w_down, w_up, gpm, gpl],
                                               [(d_ff, BF16), (d, F32), (d, BF16)], [((1, d), F32), ((1, d), F32)])
    dw_up = _mm_shard_cols(h2, [d_up], [d_up.shape[1]], w_up.shape[2], "mlp_up_dw", 0, d)
    sent_mlp = on_grads(dict(w_down=dw_down, w_up=dw_up))

    def branches_out_bwd(dz, ga, gb, ya, yb, o, g, a, b, c_, wo, wa, n_, wb, gm, s):
        d_ga_, d_gb_, d_ya_, d_yb_ = jax.vjp(_merge, ga, gb, ya, yb)[1](_raw_dot(dz, wo, "nt"))
        ct_a, ct_b = _raw_dot(d_ya_, wa, "nt"), _raw_dot(d_yb_, wb, "nt")
        parts_a, parts_b = [], []
        for hs in _head_slices(HEAD_W):
            parts_a.append(jax.vjp(_gla_out, o[:, hs], g[:, hs], n_)[1](ct_a[:, hs]))
            parts_b.append(jax.vjp(_ml_out, a[:, hs], b[:, hs], c_[:, hs], gm[:, hs], s[:, hs])[1](ct_b[:, hs]))
        cat = lambda parts, i: jnp.concatenate([p[i] for p in parts], axis=1)
        return ((d_ga_, d_gb_, d_ya_, d_yb_, cat(parts_a, 0), cat(parts_a, 1), cat(parts_b, 0), cat(parts_b, 1), cat(parts_b, 2)),
                (sum(p[2] for p in parts_a), cat(parts_b, 3), cat(parts_b, 4)))

    d_ga, d_gb, d_ya, d_yb, d_o, d_g_a, d_hc, d_opre, d_xc, d_gn, d_gml, d_skip = _rowwise(
        "branches_out_bwd", branches_out_bwd, [d_z, gate_a, gate_b, y_a, y_b, o_gla, g_a, hc, o_pre, xc],
        [weight("w_o", merged), weight("w_pa", hc), gn, weight("w_pb", hc), g_ml, skip],
        [(d, BF16)] * 4 + [(ml_w, F32), (ml_w, BF16), (ml_w, F32), (ml_w, BF16), (ml_w, F32)],
        [((1, HEAD_W), F32), ((1, ml_w), F32), ((1, ml_w), F32)], deps=dep(sent_mlp))
    dw_o, dw_pa, dw_pb = _mm_tn_whole([(merged, d_z), (ya_in, d_ya), (h_b, d_yb)], "mix_dw")
    sent_mix = on_grads(dict(w_o=dw_o, w_pa=dw_pa, w_pb=dw_pb))

    dq_hm, dk_hm, d_va, dla_hm, d_qm, d_km, d_vm, d_li, d_lf = _recurrences_bwd(
        q_hm, k_hm, v_a, la_hm, s_prev, d_o, q_m, k_m, v_m, li, lf, c_prev, n_prev, m_prev, d_hc, deps=dep(sent_mix))
    d_gl = jnp.concatenate([_gate_cols(d_li), _gate_cols(d_lf), jnp.zeros((t, LANES - 2 * HEADS), F32)], axis=1)
    pre_grads = _ml_pre_bwd(x_m, x_pad, pre_params, [d_xc, d_qm, d_km, d_vm, d_gl], tile=512)
    d_xm = pre_grads[0]
    d_cw = jnp.concatenate(pre_grads[1:5], axis=0)
    d_cb = pre_grads[5]
    d_wq, d_wk, d_wv = _blockdiag_blocks(pre_grads[6:9])
    d_wif = jnp.concatenate(pre_grads[9:12], axis=0)[:, 0:2 * HEADS]
    d_bif = pre_grads[12][:, 0:2 * HEADS]

    def decay_bwd(al, ct, w, b):
        _, vjp = jax.vjp(_log_decay, al, w, b)
        dal, dw, db = vjp(ct)
        return (dal,), (dw, db)

    d_alow_p, d_wa_p, d_ba = _rowwise("gla_decay_bwd", decay_bwd, [a_low_p, _from_hm(dla_hm)], [w_a_up_p, b_a_up],
                                      [(LANES, BF16)], [(w_a_up_p.shape, F32), (b_a_up.shape, F32)])
    d_proj = [jnp.concatenate([_from_hm(dq_hm), _from_hm(dk_hm), d_va, d_g_a], axis=1), d_alow_p,
              jnp.concatenate([d_xm, d_opre, d_ga, d_gb], axis=1)]
    d_widths = [offs[4], LOWRANK, offs[9] - offs[5]]
    d_pieces = _shard_pieces(d_widths, n_in)
    small = dict(w_a_up=d_wa_p[0:LOWRANK], b_a_up=d_ba, g_gla_norm=d_gn, conv_w=d_cw, conv_b=d_cb,
                 w_q_ml=d_wq, w_k_ml=d_wk, w_v_ml=d_wv, w_if=d_wif, b_if=d_bif, ml_skip=d_skip, g_ml_norm=d_gml,
                 g_post_mix=d_gpm, g_pre_mlp=d_gpl, g_post_mlp=d_gpo)
    sent_small = on_small(small, loss)
    sent_in = sent_small
    for half in range(2):
        dw_half = _mm_shard_cols(h, d_proj, d_widths, n_in, "proj_in_dw_%d" % half, half, d // 2, deps=dep(sent_in))
        sent_in = on_grads({"w_in#%d" % half: dw_half})

    def proj_in_dx(dp_a, dp_low, dp_b, xv, dres, w, g):
        dh = 0.0
        for s in range(N_DEV):
            for i, c_in, c_w, wd in d_pieces[s]:
                src = (dp_a, dp_low, dp_b)[i]
                cols = src.shape[1] - c_in if wd < LANES else wd
                dh = dh + _raw_dot(src[:, c_in:c_in + cols], w[s][:, c_w:c_w + cols], "nt")
        _, vjp = jax.vjp(_rms, xv, g)
        dx, dg = vjp(dh)
        return (dx + dres,), (dg,)

    grad_x, d_g1 = _rowwise("proj_in_dx", proj_in_dx, [*d_proj, x, dx_res], [w_in, g1], [(d, F32)], [((1, d), F32)],
                            deps=dep(sent_in))
    return grad_x, on_small(dict(g_pre_mix=d_g1), None)


BIG = ("w_in", "w_pa", "w_pb", "w_o", "w_up", "w_down")
MIX = ("w_o", "w_pa", "w_pb")
BIG_COL_SHARDED = ("w_in", "w_pa", "w_pb", "w_up")
SMALL_SHARDED = ("w_a_up", "conv_w", "w_if")
SMALL = ("g_pre_mix", "w_a_up", "b_a_up", "g_gla_norm", "conv_w", "conv_b", "w_q_ml", "w_k_ml", "w_v_ml", "w_if", "b_if",
         "ml_skip", "g_ml_norm", "g_post_mix", "g_pre_mlp", "g_post_mlp")
WEIGHTS = ("g_pre_mix", "w_in", "w_a_up", "b_a_up", "g_gla_norm", "conv_w", "conv_b", "w_q_ml", "w_k_ml", "w_v_ml", "w_if", "b_if",
           "ml_skip", "g_ml_norm", "w_pa", "w_pb", "w_o", "g_post_mix", "g_pre_mlp", "w_up", "w_down", "g_post_mlp")


def kernel(x, g_pre_mix, w_in, w_a_up, b_a_up, g_gla_norm, conv_w, conv_b, w_q_ml, w_k_ml, w_v_ml, w_if, b_if, ml_skip, g_ml_norm, w_pa, w_pb, w_o, g_post_mix, g_pre_mlp, w_up, w_down, g_post_mlp, loss_target, m_g_pre_mix, m_w_in, m_w_a_up, m_b_a_up, m_g_gla_norm, m_conv_w, m_conv_b, m_w_q_ml, m_w_k_ml, m_w_v_ml, m_w_if, m_b_if, m_ml_skip, m_g_ml_norm, m_w_pa, m_w_pb, m_w_o, m_g_post_mix, m_g_pre_mlp, m_w_up, m_w_down, m_g_post_mlp, v_g_pre_mix, v_w_in, v_w_a_up, v_b_a_up, v_g_gla_norm, v_conv_w, v_conv_b, v_w_q_ml, v_w_k_ml, v_w_v_ml, v_w_if, v_b_if, v_ml_skip, v_g_ml_norm, v_w_pa, v_w_pb, v_w_o, v_g_post_mix, v_g_pre_mlp, v_w_up, v_w_down, v_g_post_mlp):
    args = dict(locals())
    w = {n: args[n][0] for n in WEIGHTS}
    m = {n: args["m_" + n][0] for n in WEIGHTS}
    v = {n: args["v_" + n][0] for n in WEIGHTS}

    me_lin = _lin(_me())
    me_idx = jnp.reshape(me_lin, (1,)).astype(jnp.int32)

    def full_weight(n, g):
        if n in ("w_in", "w_up"):
            return g
        return _from_col_blocks(g) if n in BIG_COL_SHARDED else g.reshape(-1, g.shape[-1])

    def grad_parts(n, g):
        if n.partition("#")[0] in ("w_in", "w_up"):
            return g
        return (_col_blocks(g) if n in BIG_COL_SHARDED else g.reshape(N_DEV, -1, g.shape[-1])).astype(BF16)

    sharded_names = tuple(SMALL_SHARDED)
    narrow = {n: w[n].astype(BF16) for n in BIG}
    ready, pending, passing = {}, {}, {}
    first_state, _ = _copies_start(["gather"] * len(sharded_names) + ["gather_chips"],
                                   [_small_view(n, w[n]) for n in sharded_names] + [narrow["w_in"]], "allgather_start_first")

    def prefetch(group, after):
        state, token = _copies_start("gather_chips", [narrow[n] for n in group], "allgather_start_" + group[0], after)
        for n in group:
            pending[n] = (group, state)
        return token

    def pass_on(n, after):
        group, state = pending[n]
        shards, lands = _copies_wait(state, after, "allgather_wait_" + group[0])
        state, token = _copies_start("gather_pass", shards, "allgather_pass_" + group[0], lands=lands)
        for gn in group:
            passing[gn] = (group, state)
        return token

    def weight(n, after):
        if n not in ready:
            group, state = passing[n]
            shards, lands = _copies_wait(state, after, "allgather_passed_" + group[0])
            for gn, shard, land in zip(group, shards, lands):
                ready[gn] = full_weight(gn, lax.dynamic_update_slice(land, shard[None], (me_lin, 0, 0)))
        return ready[n]

    h = _first_norm(x[0], w["g_pre_mix"].reshape(1, -1))
    first_own, first_lands = _copies_wait(first_state, [h] + [narrow[n] for n in BIG if n != "w_in"], "allgather_wait_first")
    state, _ = _copies_start("gather_pass", first_own[-1:], "allgather_pass_w_in", lands=first_lands[-1:])
    passing["w_in"] = (("w_in",), state)
    ws = {n: (w[n].reshape(1, -1) if w[n].ndim == 1 else w[n]) for n in SMALL if n not in SMALL_SHARDED}
    for n, own, land in zip(sharded_names, first_own, first_lands):
        ws[n] = _small_unshard(n, lax.dynamic_update_slice(land, own[None], (me_lin, 0, 0)))

    sets, waiting_small = [], []

    def start_set(large, small):
        names = tuple(large)
        s_names, s_kinds, s_srcs = small if small else ((), [], [])
        state, token = _copies_start(s_kinds + ["exchange"] * len(names), s_srcs + [grad_parts(n, large[n]) for n in names],
                                     "exchange_start_" + (names + s_names)[0].replace("#", "_"))
        sets.append((names, s_names, s_kinds, state))
        return token

    def on_grads(grads):
        return start_set(grads, waiting_small.pop() if waiting_small else None)

    def on_small(small, loss):
        names = tuple(small)
        kinds = ["exchange" if n in SMALL_SHARDED else "gather" for n in names]
        srcs = [_small_shards(n, small[n]) if n in SMALL_SHARDED else _small_view(n, small[n]) for n in names]
        if loss is None:
            return start_set({}, (names, kinds, srcs))
        waiting_small.append((names, kinds + ["gather"], srcs + [loss]))
        return None

    grad_x, last_token = _local_step(x[0], h, loss_target[0], weight, ws, prefetch, pass_on, on_grads, on_small)

    out, chunks, sums = {}, {}, []

    def finish_set(names, s_names, s_kinds, state, after):
        own, lands = _copies_wait(state, after, "exchange_wait_" + (names + s_names)[0].replace("#", "_"))
        ns = len(s_kinds)
        if s_names:
            k = len(s_names)
            upd = _small_update("adamw_small_" + s_names[0], me_idx, s_kinds[:k], lands[:k], own[:k],
                                *[[_small_view(n, d[n]) for n in s_names] for d in (w, m, v)],
                                sums=list(zip(lands[k:ns], own[k:ns])))
            for i, n in enumerate(s_names):
                out[n] = tuple(_small_unview(n, a, w[n].shape) for a in upd[4 * i:4 * i + 4])
            sums.extend(upd[4 * k:])
        if names == MIX:
            upd = _small_update("adamw_mix", me_idx, ["exchange"] * len(names), lands[ns:], own[ns:],
                                *[[d[n] for n in names] for d in (w, m, v)])
            for i, n in enumerate(names):
                out[n] = tuple(upd[4 * i:4 * i + 4])
            return
        for name, part, land in zip(names, own[ns:], lands[ns:]):
            n, _, chunk = name.partition("#")
            chunks.setdefault(n, []).append((land, part))
            if chunk in ("", "1"):
                got_lands, got_parts = zip(*chunks[n])
                out[n] = _sum_adamw(got_lands, got_parts, me_idx, w[n], m[n], v[n], "adamw_" + n)

    for entry in sets:
        finish_set(*entry, [grad_x, last_token] + [out[n][1] for n in BIG if n in out])
    loss_sum = sums[0]

    shaped = lambda a, n: a.reshape(args[n].shape)
    return (loss_sum[0, 0], grad_x[None],
            *[shaped(out[n][0], n) for n in WEIGHTS], *[shaped(out[n][1], n) for n in WEIGHTS],
            *[shaped(out[n][2], n) for n in WEIGHTS], *[shaped(out[n][3], n) for n in WEIGHTS])
```

```python
import functools

import jax
import jax.numpy as jnp
from jax import lax
from jax.experimental import pallas as pl
from jax.experimental.pallas import tpu as pltpu

F32 = jnp.float32
BF16 = jnp.bfloat16
MESH = pl.DeviceIdType.MESH

N_DEV = 8
EPS = 1e-6
CHUNK = 64
CHUNKS_PER_STEP = 8
HEADS = 4
GLA_DK = 64
HEAD_W = 128
GLA_GATE_NORM = 16.0
LOWRANK = 16
CONV_K = 4
QKV_BLOCK = 4
LANES = 128
HALO = 8
IN_SPLITS = (256, 256, 512, 512, 16, 512, 512, 1024, 1024)

ADAM_LR = 0.001
ADAM_B1 = 0.9
ADAM_B2 = 0.999
ADAM_EPS = 1e-08
ADAM_WD = 0.01
ADAM_STEP = 10

VMEM_LIMIT = 56 * 1024 * 1024


def _cparams(*sem):
    return pltpu.CompilerParams(dimension_semantics=sem, vmem_limit_bytes=VMEM_LIMIT)


def _dims(mode, ndim):
    contract = {"nn": ((ndim - 1,), (ndim - 2,)), "nt": ((ndim - 1,), (ndim - 1,)), "tn": ((ndim - 2,), (ndim - 2,))}[mode]
    return contract, (((0,), (0,)) if ndim == 3 else ((), ()))


def _raw_dot(a, b, mode):
    return lax.dot_general(a.astype(BF16), b.astype(BF16), _dims(mode, a.ndim), preferred_element_type=F32)


@functools.partial(jax.custom_vjp, nondiff_argnums=(2,))
def _bdot(a, b, mode):
    return _raw_dot(a, b, mode)


def _bdot_fwd(a, b, mode):
    return _raw_dot(a, b, mode), (a, b)


def _bdot_bwd(mode, res, ct):
    a, b = res
    if mode == "nn":
        da, db = _raw_dot(ct, b, "nt"), _raw_dot(a, ct, "tn")
    elif mode == "nt":
        da, db = _raw_dot(ct, b, "nn"), _raw_dot(ct, a, "tn")
    else:
        da, db = _raw_dot(b, ct, "nt"), _raw_dot(a, ct, "nn")
    return da.astype(a.dtype), db.astype(b.dtype)


_bdot.defvjp(_bdot_fwd, _bdot_bwd)


def _split3(x):
    hi = x.astype(BF16)
    r1 = x - hi.astype(F32)
    mid = r1.astype(BF16)
    return hi, mid, (r1 - mid.astype(F32)).astype(BF16)


def _split_dot(tri, x):
    if x.ndim == 3:
        tri = jnp.broadcast_to(tri, (x.shape[0], *tri.shape))
    return sum(lax.dot_general(tri, t, _dims("nn", x.ndim), preferred_element_type=F32) for t in _split3(x))


def _tri(n, lower):
    r = lax.broadcasted_iota(jnp.int32, (n, n), 0)
    c = lax.broadcasted_iota(jnp.int32, (n, n), 1)
    return ((c <= r) if lower else (c >= r)).astype(BF16)


@jax.custom_vjp
def _cumsum_rows(x):
    return _split_dot(_tri(x.shape[-2], True), x)


def _cumsum_rows_fwd(x):
    return _cumsum_rows(x), None


def _cumsum_rows_bwd(_, ct):
    return (_split_dot(_tri(ct.shape[-2], False), ct),)


_cumsum_rows.defvjp(_cumsum_rows_fwd, _cumsum_rows_bwd)


def _abs(x):
    return jnp.where(x >= 0, x, -x)


def _sigmoid(x):
    return lax.logistic(x)


def _log_sigmoid(x):
    return jnp.minimum(x, 0.0) - jnp.log(1.0 + jnp.exp(-_abs(x)))


def _rms(x, g):
    return x * lax.rsqrt(jnp.mean(x * x, axis=-1, keepdims=True) + EPS) * g


def _head_slices(w):
    return [slice(h * w, (h + 1) * w) for h in range(HEADS)]


def _heads(ref, rows=slice(None)):
    return jnp.stack([ref[rows, hs] for hs in _head_slices(HEAD_W)])


def _put_heads(ref, val, rows=slice(None)):
    for h, hs in enumerate(_head_slices(HEAD_W)):
        ref[rows, hs] = val[h].astype(ref.dtype)


def _tile(dim, want):
    if dim <= want or dim % LANES:
        return dim
    t = want
    while dim % t:
        t -= LANES
    return t


def _mm(a, b, mode, out_dtype, name, tm=1024, tn=1024, tk=4096, epilogue=None, extra=(), deps=(), shards=None):
    if shards == "b":
        assert mode == "nn"
        ns = b.shape[2]
        (m, k), (k2, n) = a.shape, (b.shape[1], b.shape[0] * ns)
        tn = ns
    elif mode == "nn":
        (m, k), (k2, n) = a.shape, b.shape
    elif mode == "nt":
        (m, k), (n, k2) = a.shape, b.shape
    else:
        (k, m), (k2, n) = a.shape, b.shape
    assert k == k2, (name, a.shape, b.shape)
    tm, tn, tk = _tile(m, tm), _tile(n, tn), _tile(k, tk)
    nk = k // tk
    out_dtypes = out_dtype if epilogue else (out_dtype,)
    assert nk == 1 or (out_dtype == F32 and not epilogue), name
    n_in = 2 + len(extra)

    def body(*refs):
        p = _raw_dot(refs[0][...], refs[1][...], mode)
        if nk > 1:
            _accumulate(pl.program_id(2), [refs[n_in + len(deps)]], [p])
            return
        outs = epilogue(p, *[r[...] for r in refs[2:n_in]]) if epilogue else (p,)
        for ref, val in zip(refs[n_in + len(deps):], outs):
            ref[...] = val.astype(ref.dtype)

    a_spec = pl.BlockSpec((tk, tm), lambda i, j, kk: (kk, i)) if mode == "tn" else pl.BlockSpec((tm, tk), lambda i, j, kk: (i, kk))
    if shards == "b":
        b_spec = pl.BlockSpec((None, tk, tn), lambda i, j, kk: (j, kk, 0))
    elif mode == "nt":
        b_spec = pl.BlockSpec((tn, tk), lambda i, j, kk: (j, kk))
    else:
        b_spec = pl.BlockSpec((tk, tn), lambda i, j, kk: (kk, j))
    o_spec = pl.BlockSpec((tm, tn), lambda i, j, kk: (i, j))
    res = pl.pallas_call(
        body, name=name, grid=(m // tm, n // tn, nk),
        in_specs=[a_spec, b_spec] + [o_spec] * len(extra) + [ANY] * len(deps), out_specs=[o_spec] * len(out_dtypes),
        out_shape=[jax.ShapeDtypeStruct((m, n), dt) for dt in out_dtypes],
        compiler_params=_cparams("parallel", "parallel", "arbitrary"),
    )(a, b, *extra, *deps)
    return res if epilogue else res[0]


def _mm_tn_whole(pairs, name):
    def body(*refs):
        for i in range(len(pairs)):
            refs[2 * len(pairs) + i][...] = _raw_dot(refs[2 * i][...], refs[2 * i + 1][...], "tn").astype(BF16)

    return pl.pallas_call(
        body, name=name,
        out_shape=[jax.ShapeDtypeStruct((a.shape[1], b.shape[1]), BF16) for a, b in pairs],
        compiler_params=pltpu.CompilerParams(vmem_limit_bytes=VMEM_LIMIT),
    )(*[x for pair in pairs for x in pair])


def _shard_pieces(widths, n):
    bounds = [0]
    for wd in widths:
        bounds.append(bounds[-1] + wd)
    assert bounds[-1] == N_DEV * n
    return [[(i, max(s * n, b) - b, max(s * n, b) - s * n, min((s + 1) * n, b + wd) - max(s * n, b))
             for i, (b, wd) in enumerate(zip(bounds, widths)) if b < (s + 1) * n and b + wd > s * n]
            for s in range(N_DEV)]


def _mm_shard_cols(a, bs, widths, n, name, row_tile, tm, deps=()):
    t = a.shape[0]
    nb = len(bs)
    pieces = _shard_pieces(widths, n)

    def body(a_ref, *rest):
        b_refs = rest[:nb]
        o_ref, at_ref = rest[nb + len(deps):]
        j = pl.program_id(0)

        @pl.when(j == 0)
        def _():
            at_ref[...] = a_ref[...].astype(BF16).T

        for s in range(N_DEV):
            @pl.when(j == s)
            def _(s=s):
                for i, c_in, c_out, wd in pieces[s]:
                    cols = min(_round_up(wd, LANES), bs[i].shape[1] - c_in) if wd < LANES else wd
                    p = _raw_dot(at_ref[...], b_refs[i][:, c_in:c_in + cols], "nn")
                    o_ref[:, c_out:c_out + wd] = p[:, 0:wd].astype(BF16)

    return pl.pallas_call(
        body, name=name, grid=(N_DEV,),
        in_specs=[pl.BlockSpec((t, tm), lambda j: (0, row_tile))]
        + [pl.BlockSpec(b.shape, lambda j: (0, 0), pipeline_mode=pl.Buffered(1)) for b in bs] + [ANY] * len(deps),
        out_specs=pl.BlockSpec((None, tm, n), lambda j: (j, 0, 0)),
        out_shape=jax.ShapeDtypeStruct((N_DEV, tm, n), BF16),
        scratch_shapes=[pltpu.VMEM((tm, t), BF16)],
        compiler_params=_cparams("arbitrary"),
    )(a, *bs, *deps)


def _round_up(v, m):
    return -(-v // m) * m


def _rowwise(name, fn, rows, params, out_rows, out_accs=(), tile=256, deps=()):
    t = rows[0].shape[0]
    r = min(tile, t)
    assert t % r == 0
    n_in, n_or = len(rows) + len(params), len(out_rows)
    n_all = n_in + len(deps)
    params = list(params) + list(deps)

    def body(*refs):
        vals = [ref[...] for ref in refs[:n_in]]
        outs = refs[n_all:]
        ro, ao = fn(*vals)
        for ref, v in zip(outs[:n_or], ro):
            ref[...] = v.astype(ref.dtype)
        if out_accs:
            _accumulate(pl.program_id(0), outs[n_or:], ao)

    def full(shape, **kw):
        return pl.BlockSpec(shape, lambda i, nd=len(shape): (0,) * nd, **kw)

    return pl.pallas_call(
        body, name=name, grid=(t // r,),
        in_specs=[pl.BlockSpec((r, a.shape[1]), lambda i: (i, 0)) for a in rows]
        + [full(p.shape, pipeline_mode=pl.Buffered(1)) for p in params],
        out_specs=[pl.BlockSpec((r, w), lambda i: (i, 0)) for w, _ in out_rows] + [full(s) for s, _ in out_accs],
        out_shape=[jax.ShapeDtypeStruct((t, w), dt) for w, dt in out_rows] + [jax.ShapeDtypeStruct(s, dt) for s, dt in out_accs],
        compiler_params=_cparams("arbitrary"),
    )(*rows, *params)


def _accumulate(step, refs, vals):
    for ref, v in zip(refs, vals):
        @pl.when(step == 0)
        def _(ref=ref, v=v):
            ref[...] = v.astype(ref.dtype)

        @pl.when(step > 0)
        def _(ref=ref, v=v):
            ref[...] += v.astype(ref.dtype)


def _gla_chunk(q, k, v, la, st):
    c = q.shape[-2]
    row = lax.broadcasted_iota(jnp.int32, (c, c), 0)
    col = lax.broadcasted_iota(jnp.int32, (c, c), 1)
    cum = _cumsum_rows(la)
    cl = jnp.sum(la, axis=-2, keepdims=True)
    ep = jnp.exp(cum)
    en = jnp.exp(-cum)
    qs = q * (GLA_DK ** -0.5)
    qp = qs * ep
    a_f = _bdot(qp, k * en, "nt")
    a_b = _bdot(qs * en, k * ep, "nt")
    sc = jnp.where(row >= col, a_f, a_b)
    o = _bdot(sc, v, "nn") + _bdot(qp, st, "nt")
    kd = k * jnp.exp(cl - cum)
    st_new = st * jnp.exp(cl) + _bdot(v, kd, "tn")
    return o, st_new


def _gla_specs(nc, rev):
    nb = nc // CHUNKS_PER_STEP
    rows = CHUNKS_PER_STEP * CHUNK

    def blk(n):
        return (nb - 1 - n) if rev else n
    hm = pl.BlockSpec((HEADS, rows, GLA_DK), lambda n: (0, blk(n), 0))
    tm = pl.BlockSpec((rows, HEADS * HEAD_W), lambda n: (blk(n), 0))
    st = pl.BlockSpec((HEADS, CHUNKS_PER_STEP, HEAD_W, GLA_DK), lambda n: (0, blk(n), 0, 0))
    return nb, hm, tm, st


def _chunk_rows(c):
    return slice(c * CHUNK, (c + 1) * CHUNK)


def _ml_chunk(q, k, v, li_r, lf_r, cm, nv, m):
    c = q.shape[-2]
    row = lax.broadcasted_iota(jnp.int32, (c, c), 0)
    col = lax.broadcasted_iota(jnp.int32, (c, c), 1)
    eye = (row == col).astype(F32)
    li_c = jnp.sum(eye * li_r, axis=-1, keepdims=True)
    lf_c = jnp.sum(eye * lf_r, axis=-1, keepdims=True)
    fc_c = jnp.sum((col <= row).astype(F32) * lf_r, axis=-1, keepdims=True)
    fc_r = jnp.sum((row <= col).astype(F32) * lf_c, axis=-2, keepdims=True)
    f_last = jnp.sum(lf_r, axis=-1, keepdims=True)
    kc = k * (HEAD_W ** -0.5)
    a_c = f_last - fc_c + li_c
    m_loc = jnp.max(a_c, axis=-2, keepdims=True)
    kw = kc * jnp.exp(a_c - m_loc)
    c_chunk = _bdot(kw, v, "tn")
    n_chunk = jnp.sum(kw, axis=-2, keepdims=True)
    m_new = jnp.maximum(f_last + m, m_loc)
    sp = jnp.exp(f_last + m - m_new)
    sl = jnp.exp(m_loc - m_new)
    cm_new = sp * cm + sl * c_chunk
    nv_new = sp * nv + sl * n_chunk
    log_d = li_r - _abs(fc_c - fc_r)
    g_inter = fc_c + m
    m_t = jnp.maximum(g_inter, jnp.max(log_d, axis=-1, keepdims=True))
    s = _bdot(q, kc, "nt") * jnp.exp(log_d - m_t)
    sc = jnp.exp(g_inter - m_t)
    num = _bdot(s, v, "nn") + sc * _bdot(q, cm, "nn")
    den = jnp.sum(s, axis=-1, keepdims=True) + sc * jnp.sum(q * nv, axis=-1, keepdims=True)
    den = jnp.maximum(_abs(den), jnp.exp(-m_t))
    return num / den, cm_new, nv_new, m_new


def _ml_specs(nc, rev):
    nb = nc // CHUNKS_PER_STEP

    def blk(n):
        return (nb - 1 - n) if rev else n
    tm = pl.BlockSpec((CHUNKS_PER_STEP * CHUNK, HEADS * HEAD_W), lambda n: (blk(n), 0))
    gate = pl.BlockSpec((HEADS, CHUNKS_PER_STEP, 1, CHUNK), lambda n: (0, blk(n), 0, 0))
    cm = pl.BlockSpec((HEADS, CHUNKS_PER_STEP, HEAD_W, HEAD_W), lambda n: (0, blk(n), 0, 0))
    vec = pl.BlockSpec((HEADS, CHUNKS_PER_STEP, 1, HEAD_W), lambda n: (0, blk(n), 0, 0))
    return nb, tm, gate, cm, vec


_ML_STATE = [pltpu.VMEM((HEADS, HEAD_W, HEAD_W), F32), pltpu.VMEM((HEADS, 1, HEAD_W), F32), pltpu.VMEM((HEADS, 1, HEAD_W), F32)]


def _recurrences_fwd(q, k, v, la, qm, km, vm, li, lf, deps=()):
    t = v.shape[0]
    nc = t // CHUNK
    nb, hm, tm, st = _gla_specs(nc, False)
    _, _, gate, cm, vec = _ml_specs(nc, False)
    n_in = 9 + len(deps)

    def body(*refs):
        q_ref, k_ref, v_ref, la_ref, qm_ref, km_ref, vm_ref, li_ref, lf_ref = refs[:9]
        o_ref, sp_ref, hc_ref, cp_ref, np_ref, mp_ref, st_ref, c_ref, n_ref, m_ref = refs[n_in:]

        @pl.when(pl.program_id(0) == 0)
        def _():
            for ref in (st_ref, c_ref, n_ref, m_ref):
                ref[...] = jnp.zeros_like(ref)

        s, cs, ns, ms = st_ref[...], c_ref[...], n_ref[...], m_ref[...][:, :, 0:1]
        for c in range(CHUNKS_PER_STEP):
            r = _chunk_rows(c)
            sp_ref[:, c] = s
            o, s = _gla_chunk(q_ref[:, r], k_ref[:, r], _heads(v_ref, r), la_ref[:, r], s)
            _put_heads(o_ref, o, r)
            cp_ref[:, c] = cs
            np_ref[:, c] = ns
            mp_ref[:, c] = jnp.broadcast_to(ms, m_ref.shape)
            hc, cs, ns, ms = _ml_chunk(_heads(qm_ref, r), _heads(km_ref, r), _heads(vm_ref, r), li_ref[:, c], lf_ref[:, c],
                                       cs, ns, ms)
            _put_heads(hc_ref, hc, r)
        st_ref[...] = s
        c_ref[...] = cs
        n_ref[...] = ns
        m_ref[...] = jnp.broadcast_to(ms, m_ref.shape)

    tm_shape = jax.ShapeDtypeStruct((t, HEADS * HEAD_W), F32)
    vec_shape = jax.ShapeDtypeStruct((HEADS, nc, 1, HEAD_W), F32)
    return pl.pallas_call(
        body, name="recurrences_fwd", grid=(nb,),
        in_specs=[hm, hm, tm, hm, tm, tm, tm, gate, gate] + [ANY] * len(deps), out_specs=[tm, st, tm, cm, vec, vec],
        out_shape=[tm_shape, jax.ShapeDtypeStruct((HEADS, nc, HEAD_W, GLA_DK), F32),
                   tm_shape, jax.ShapeDtypeStruct((HEADS, nc, HEAD_W, HEAD_W), F32), vec_shape, vec_shape],
        scratch_shapes=[pltpu.VMEM((HEADS, HEAD_W, GLA_DK), F32)] + _ML_STATE,
        compiler_params=_cparams("arbitrary"),
    )(q, k, v, la, qm, km, vm, li, lf, *deps)


def _recurrences_bwd(q, k, v, la, sp, do, qm, km, vm, li, lf, cp, npv, mp, dhc, deps=()):
    t = v.shape[0]
    nc = t // CHUNK
    nb, hm, tm, st = _gla_specs(nc, True)
    _, _, gate, cm, vec = _ml_specs(nc, True)
    n_in = 15 + len(deps)

    def body(*refs):
        (q_ref, k_ref, v_ref, la_ref, sp_ref, do_ref,
         qm_ref, km_ref, vm_ref, li_ref, lf_ref, cp_ref, np_ref, mp_ref, dhc_ref) = refs[:15]
        (dq_ref, dk_ref, dv_ref, dla_ref, dqm_ref, dkm_ref, dvm_ref, dli_ref, dlf_ref,
         ds_ref, dc_ref, dn_ref, dm_ref) = refs[n_in:]

        @pl.when(pl.program_id(0) == 0)
        def _():
            for ref in (ds_ref, dc_ref, dn_ref, dm_ref):
                ref[...] = jnp.zeros_like(ref)

        ds, dc, dn, dm = ds_ref[...], dc_ref[...], dn_ref[...], dm_ref[...][:, :, 0:1]
        for c in reversed(range(CHUNKS_PER_STEP)):
            r = _chunk_rows(c)
            _, vjp = jax.vjp(_gla_chunk, q_ref[:, r], k_ref[:, r], _heads(v_ref, r), la_ref[:, r], sp_ref[:, c])
            dq, dk, dv, dla, ds = vjp((_heads(do_ref, r), ds))
            dq_ref[:, r] = dq.astype(dq_ref.dtype)
            dk_ref[:, r] = dk.astype(dk_ref.dtype)
            _put_heads(dv_ref, dv, r)
            dla_ref[:, r] = dla
            _, vjp = jax.vjp(_ml_chunk, _heads(qm_ref, r), _heads(km_ref, r), _heads(vm_ref, r), li_ref[:, c], lf_ref[:, c],
                             cp_ref[:, c], np_ref[:, c], mp_ref[:, c][:, :, 0:1])
            dqm, dkm, dvm, dli, dlf, dc, dn, dm = vjp((_heads(dhc_ref, r), dc, dn, dm))
            _put_heads(dqm_ref, dqm, r)
            _put_heads(dkm_ref, dkm, r)
            _put_heads(dvm_ref, dvm, r)
            dli_ref[:, c] = dli
            dlf_ref[:, c] = dlf
        ds_ref[...] = ds
        dc_ref[...] = dc
        dn_ref[...] = dn
        dm_ref[...] = jnp.broadcast_to(dm, dm_ref.shape)

    hm_shape = jax.ShapeDtypeStruct((HEADS, t, GLA_DK), BF16)
    tm_shape = jax.ShapeDtypeStruct((t, HEADS * HEAD_W), F32)
    gate_shape = jax.ShapeDtypeStruct((HEADS, nc, 1, CHUNK), F32)
    return pl.pallas_call(
        body, name="recurrences_bwd", grid=(nb,),
        in_specs=[hm, hm, tm, hm, st, tm, tm, tm, tm, gate, gate, cm, vec, vec, tm] + [ANY] * len(deps),
        out_specs=[hm, hm, tm, hm, tm, tm, tm, gate, gate],
        out_shape=[hm_shape, hm_shape, jax.ShapeDtypeStruct((t, HEADS * HEAD_W), BF16),
                   jax.ShapeDtypeStruct((HEADS, t, GLA_DK), F32), tm_shape, tm_shape, tm_shape, gate_shape, gate_shape],
        scratch_shapes=[pltpu.VMEM((HEADS, HEAD_W, GLA_DK), F32)] + _ML_STATE,
        compiler_params=_cparams("arbitrary"),
    )(q, k, v, la, sp, do, qm, km, vm, li, lf, cp, npv, mp, dhc, *deps)


@jax.custom_vjp
def _bdot_diag(x, w):
    b = w.shape[1]
    return jnp.concatenate([_raw_dot(x[:, :b], w[0], "nn"), _raw_dot(x[:, b:], w[1], "nn")], axis=1)


def _bdot_diag_fwd(x, w):
    return _bdot_diag(x, w), (x, w)


def _bdot_diag_bwd(res, ct):
    x, w = res
    b = w.shape[1]
    dx = jnp.concatenate([_raw_dot(ct[:, :b], w[0], "nt"), _raw_dot(ct[:, b:], w[1], "nt")], axis=1)
    dw = jnp.stack([_raw_dot(x[:, :b], ct[:, :b], "tn"), _raw_dot(x[:, b:], ct[:, b:], "tn")])
    return dx.astype(x.dtype), dw.astype(w.dtype)


_bdot_diag.defvjp(_bdot_diag_fwd, _bdot_diag_bwd)


def _ml_pre(s0, s1, s2, s3, cw0, cw1, cw2, cw3, cb, wq, wk, wv, wiq, wik, wiv, bif):
    pre = cb + cw0 * s0 + cw1 * s1 + cw2 * s2 + cw3 * s3
    xc = pre * _sigmoid(pre)
    q = _bdot_diag(xc, wq)
    k = _bdot_diag(xc, wk)
    v = _bdot_diag(s3, wv)
    gates = _bdot(q, wiq, "nn") + _bdot(k, wik, "nn") + _bdot(v, wiv, "nn") + bif
    lane = lax.broadcasted_iota(jnp.int32, gates.shape, 1)
    gl = jnp.where(lane < HEADS, gates, _log_sigmoid(gates))
    return xc, q, k, v, gl


def _delayed(xs_ref, x_ref, halo_ref, r):
    xs_ref[0:HALO, :] = halo_ref[...]
    xs_ref[HALO:HALO + r, :] = x_ref[...]
    return [xs_ref[pl.ds(HALO - (CONV_K - 1) + j, r), :] for j in range(CONV_K)]


def _full_spec(shape):
    return pl.BlockSpec(shape, lambda i, nd=len(shape): (0,) * nd)


def _ml_pre_fwd(x_m, x_pad, params, tile=256, deps=()):
    t, w = x_m.shape
    r = min(tile, t)

    def body(*refs):
        x_ref, halo_ref = refs[:2]
        p = [ref[...] for ref in refs[2:2 + len(params)]]
        outs = refs[2 + len(params) + len(deps):-1]
        res = _ml_pre(*_delayed(refs[-1], x_ref, halo_ref, r), *p)
        for ref, val in zip(outs, res):
            ref[...] = val

    row = pl.BlockSpec((r, w), lambda i: (i, 0))
    return pl.pallas_call(
        body, name="ml_pre_fwd", grid=(t // r,),
        in_specs=[row, pl.BlockSpec((HALO, w), lambda i: (i * (r // HALO), 0))] + [_full_spec(p.shape) for p in params]
        + [ANY] * len(deps),
        out_specs=[row] * 4 + [pl.BlockSpec((r, LANES), lambda i: (i, 0))],
        out_shape=[jax.ShapeDtypeStruct((t, w), F32)] * 4 + [jax.ShapeDtypeStruct((t, LANES), F32)],
        scratch_shapes=[pltpu.VMEM((r + HALO, w), F32)],
        compiler_params=_cparams("arbitrary"),
    )(x_m, x_pad, *params, *deps)


def _ml_pre_bwd(x_m, x_pad, params, cts, tile=256):
    t, w = x_m.shape
    r = min(tile, t)
    nt = t // r
    n_p = len(params)

    def body(*refs):
        x_ref, halo_ref = refs[:2]
        p = [ref[...] for ref in refs[2:2 + n_p]]
        ct = [ref[...] for ref in refs[2 + n_p:7 + n_p]]
        dx_ref = refs[7 + n_p]
        dp_refs = refs[8 + n_p:8 + 2 * n_p]
        xs_ref, ds_ref, carry_ref = refs[8 + 2 * n_p:]
        step = pl.program_id(0)

        @pl.when(step == 0)
        def _():
            ds_ref[...] = jnp.zeros_like(ds_ref)
            carry_ref[...] = jnp.zeros_like(carry_ref)

        _, vjp = jax.vjp(_ml_pre, *_delayed(xs_ref, x_ref, halo_ref, r), *p)
        grads = vjp(tuple(ct))
        for j in range(CONV_K):
            ds_ref[j, HALO:HALO + r, :] = grads[j]
        lead = HALO + CONV_K - 1
        d_tile = sum(ds_ref[j, pl.ds(lead - j, r), :] for j in range(CONV_K))
        d_halo = sum(ds_ref[j, pl.ds(CONV_K - 1 - j, HALO), :] for j in range(CONV_K))
        dx_ref[...] = jnp.concatenate([d_tile[:r - HALO], d_tile[r - HALO:] + carry_ref[...]], axis=0).astype(dx_ref.dtype)
        carry_ref[...] = d_halo
        _accumulate(step, dp_refs, grads[CONV_K:])

    row = pl.BlockSpec((r, w), lambda i: (nt - 1 - i, 0))
    return pl.pallas_call(
        body, name="ml_pre_bwd", grid=(nt,),
        in_specs=[row, pl.BlockSpec((HALO, w), lambda i: ((nt - 1 - i) * (r // HALO), 0))] + [_full_spec(p.shape) for p in params]
        + [row] * 4 + [pl.BlockSpec((r, LANES), lambda i: (nt - 1 - i, 0))],
        out_specs=[row] + [_full_spec(p.shape) for p in params],
        out_shape=[jax.ShapeDtypeStruct((t, w), BF16)] + [jax.ShapeDtypeStruct(p.shape, F32) for p in params],
        scratch_shapes=[pltpu.VMEM((r + HALO, w), F32), pltpu.VMEM((CONV_K, r + 2 * HALO, w), F32), pltpu.VMEM((HALO, w), F32)],
        compiler_params=_cparams("arbitrary"),
    )(x_m, x_pad, *params, *cts)


def _per_head(fn, row_vals, head_params, shared_params=()):
    return [fn(*[a[:, hs] for a in row_vals], *[p[:, hs] for p in head_params], *shared_params) for hs in _head_slices(HEAD_W)]


def _gla_out(o, g, gn):
    return _rms(o, gn) * (g * _sigmoid(g))


def _ml_out(hc, op, xc, g, sk):
    hcell = hc * _sigmoid(op)
    mu = jnp.mean(hcell, axis=-1, keepdims=True)
    d = hcell - mu
    var = jnp.mean(d * d, axis=-1, keepdims=True)
    return d * lax.rsqrt(var + EPS) * g + sk * xc


def _log_decay(al, w, b):
    return _log_sigmoid(_bdot(al, w, "nn") + b) * (1.0 / GLA_GATE_NORM)


def _merge(ga, gb, ya, yb):
    ga, gb, ya, yb = (a.astype(F32) for a in (ga, gb, ya, yb))
    return _sigmoid(ga) * ya + _sigmoid(gb) * yb


def _post_mix(x, z, gpm, gpl):
    x1 = x + _rms(z, gpm)
    return x1, _rms(x1, gpl)


def _loss_rows(x1, dn, tgt, g):
    e = x1 + _rms(dn, g) - tgt
    return 0.5 * jnp.sum(jnp.mean(e * e, axis=-1, keepdims=True), axis=0, keepdims=True)


def _lin(p):
    return 4 * p[0] + 2 * p[1] + p[2]


def _me():
    return lax.axis_index("x"), lax.axis_index("y"), lax.axis_index("c")


def _flip(p, k):
    return tuple((1 - v) if (k >> (2 - i)) & 1 else v for i, v in enumerate(p))


ANY = pl.BlockSpec(memory_space=pl.ANY)


HBM = pl.BlockSpec(memory_space=pltpu.HBM)
SEM = pl.BlockSpec(memory_space=pltpu.SEMAPHORE)
DATAFLOW = pltpu.SideEffectType.DATAFLOW_SIDE_EFFECTING


SIBLING = 1
OTHER_CHIPS = (2, 4, 6)


def _peer_copies(kinds, srcs, lands, send_sems, recv_sems):
    me = _me()
    copies = []
    for a, (kind, src, land) in enumerate(zip(kinds, srcs, lands)):
        masks = {"gather": range(1, N_DEV), "exchange": range(1, N_DEV), "gather_chips": (SIBLING, *OTHER_CHIPS),
                 "gather_pass": OTHER_CHIPS}[kind]
        for k in masks:
            peer = _flip(me, k)
            if kind == "gather_pass":
                block = land.at[_lin(peer)]
                src_ref, dst_ref, target = block, block, _flip(me, SIBLING)
            else:
                src_ref, dst_ref, target = (src.at[_lin(peer)] if kind == "exchange" else src), land.at[_lin(me)], peer
            copies.append(pltpu.make_async_remote_copy(
                src_ref=src_ref, dst_ref=dst_ref, send_sem=send_sems.at[a * 7 + k - 1], recv_sem=recv_sems.at[a * 7 + k - 1],
                device_id=target, device_id_type=MESH))
    return copies


def _copies_start(kind, srcs, name, after=None, lands=None):
    n = len(srcs)
    extra = [] if after is None else [after]
    kind = [kind] * n if isinstance(kind, str) else list(kind)
    land_shapes = [(s.shape if k == "exchange" else (N_DEV, *s.shape)) for k, s in zip(kind, srcs)]
    lands = [lax.empty(ls, s.dtype) for ls, s in zip(land_shapes, srcs)] if lands is None else lands

    def body(*refs):
        sems = refs[2 * n + len(extra):]
        for cp in _peer_copies(kind, refs[:n], refs[n:2 * n], sems[0], sems[1]):
            cp.start()
        refs[-1][...] = jnp.zeros_like(refs[-1])

    def hbm(a):
        return pltpu.with_memory_space_constraint(a, pltpu.HBM)

    out = pl.pallas_call(
        body, name=name,
        out_shape=(pltpu.SemaphoreType.DMA((7 * n,)), pltpu.SemaphoreType.DMA((7 * n,)),
                   *[pltpu.HBM(s.shape, s.dtype) for s in srcs],
                   *[pltpu.HBM(ls, s.dtype) for ls, s in zip(land_shapes, srcs)],
                   jax.ShapeDtypeStruct((8, LANES), F32)),
        in_specs=[HBM] * (2 * n) + [ANY] * len(extra),
        out_specs=(SEM, SEM, *[HBM] * (2 * n), pl.BlockSpec(memory_space=pltpu.VMEM)),
        input_output_aliases={i: 2 + i for i in range(2 * n)},
        compiler_params=pltpu.CompilerParams(has_side_effects=DATAFLOW),
    )(*[hbm(s) for s in srcs], *[hbm(a) for a in lands], *extra)
    return (kind, n, out[:-1]), out[-1]


def _copies_wait(state, after, name):
    kind, n, (send_sems, recv_sems, *thru) = state
    after = list(after) if isinstance(after, (list, tuple)) else [after]

    def body(*refs):
        for cp in _peer_copies(kind, refs[:n], refs[n:2 * n], refs[2 * n], refs[2 * n + 1]):
            cp.wait_send()
            cp.wait_recv()

    out = pl.pallas_call(
        body, name=name,
        out_shape=tuple(pltpu.HBM(t.shape, t.dtype) for t in thru),
        in_specs=[HBM] * (2 * n) + [SEM, SEM] + [ANY] * len(after), out_specs=tuple([HBM] * (2 * n)),
        input_output_aliases={i: i for i in range(2 * n)},
        compiler_params=pltpu.CompilerParams(has_side_effects=DATAFLOW),
    )(*thru, send_sems, recv_sems, *after)
    return out[:n], out[n:]


def _adamw(w, g, m, v):
    m2 = ADAM_B1 * m + (1.0 - ADAM_B1) * g
    v2 = ADAM_B2 * v + (1.0 - ADAM_B2) * (g * g)
    m_hat = m2 / (1.0 - ADAM_B1 ** ADAM_STEP)
    v_hat = v2 / (1.0 - ADAM_B2 ** ADAM_STEP)
    delta = -ADAM_LR * (m_hat / (jnp.sqrt(v_hat) + ADAM_EPS) + ADAM_WD * w)
    return delta, m2, v2


def _sum_adamw(lands, parts, me_idx, w, m, v, name, tile=256):
    r, c = w.shape
    nchunks = len(lands)
    tr = min(tile, r // nchunks)
    per_chunk = r // nchunks // tr
    per = 1 + N_DEV

    def body(me_ref, *refs):
        w_ref, m_ref, v_ref, g_ref, d_ref, m2_ref, v2_ref = refs[nchunks * per:]
        for k in range(nchunks):
            own_ref, slots = refs[k * per], refs[k * per + 1:(k + 1) * per]

            @pl.when(pl.program_id(0) // per_chunk == k)
            def _(own_ref=own_ref, slots=slots):
                own = own_ref[...].astype(F32)
                g = None
                for s in range(N_DEV):
                    term = jnp.where(me_ref[0] == s, own, slots[s][...].astype(F32))
                    g = term if g is None else g + term
                d, m2, v2 = _adamw(w_ref[...], g, m_ref[...], v_ref[...])
                g_ref[...] = g
                d_ref[...] = d
                m2_ref[...] = m2
                v2_ref[...] = v2

    def chunk_specs(k):
        def tile_of(i):
            return jnp.clip(i - k * per_chunk, 0, per_chunk - 1)

        def slot_spec(s):
            return pl.BlockSpec((None, tr, c), lambda i, me: (jnp.where(me[0] == s, (s + 1) % N_DEV, s), tile_of(i), 0))
        return [pl.BlockSpec((None, tr, c), lambda i, me: (me[0], tile_of(i), 0))] + [slot_spec(s) for s in range(N_DEV)]

    row = pl.BlockSpec((tr, c), lambda i, me: (i, 0))
    operands = [a for land, part in zip(lands, parts) for a in (part, *[land] * N_DEV)]
    return pl.pallas_call(
        body, name=name,
        grid_spec=pltpu.PrefetchScalarGridSpec(
            num_scalar_prefetch=1, grid=(r // tr,),
            in_specs=[s for k in range(nchunks) for s in chunk_specs(k)] + [row] * 3,
            out_specs=[row] * 4),
        out_shape=[jax.ShapeDtypeStruct((r, c), F32)] * 4,
        compiler_params=_cparams("parallel"),
    )(me_idx, *operands, w, m, v)


def _small_update(name, me_idx, kinds, lands, owns, ws, ms, vs, sums=()):
    n = len(ws)
    lands, owns = list(lands) + [s[0] for s in sums], list(owns) + [s[1] for s in sums]
    kinds = list(kinds) + ["gather"] * len(sums)
    nl = len(lands)

    def summed(me, land_ref, own):
        g = None
        for s in range(N_DEV):
            term = jnp.where(me == s, own, land_ref[s]).astype(F32)
            g = term if g is None else g + term
        return g

    def body(me_ref, *refs):
        land_refs, own_refs = refs[:nl], refs[nl:2 * nl]
        w_refs, m_refs, v_refs = (refs[2 * nl + i * n:2 * nl + (i + 1) * n] for i in range(3))
        outs = refs[2 * nl + 3 * n:]
        me = me_ref[0]
        for i in range(n):
            g = summed(me, land_refs[i], own_refs[i][...])
            d, m2, v2 = _adamw(w_refs[i][...], g, m_refs[i][...], v_refs[i][...])
            for ref, val in zip(outs[4 * i:4 * i + 4], (g, d, m2, v2)):
                ref[...] = val
        for i in range(n, nl):
            outs[4 * n + i - n][...] = summed(me, land_refs[i], own_refs[i][...])

    def whole(shape):
        return pl.BlockSpec(shape, lambda i, me, nd=len(shape): (0,) * nd)

    def own_spec(kind, own):
        if kind == "gather":
            return whole(own.shape)
        return pl.BlockSpec((None, *own.shape[1:]), lambda i, me: (me[0], 0, 0))

    shapes = [w.shape for w in ws]
    out_shapes = [s for s in shapes for _ in range(4)] + [s[1].shape for s in sums]
    return pl.pallas_call(
        body, name=name,
        grid_spec=pltpu.PrefetchScalarGridSpec(
            num_scalar_prefetch=1, grid=(1,),
            in_specs=[whole(a.shape) for a in lands] + [own_spec(k, o) for k, o in zip(kinds, owns)]
            + [whole(s) for s in shapes] * 3,
            out_specs=[whole(s) for s in out_shapes]),
        out_shape=[jax.ShapeDtypeStruct(s, F32) for s in out_shapes],
        compiler_params=_cparams("arbitrary"),
    )(me_idx, *lands, *owns, *ws, *ms, *vs)


def _small_view(n, a):
    if a.ndim == 1:
        return a.reshape(1, -1)
    if a.ndim == 3:
        return a.transpose(1, 2, 0).reshape(QKV_BLOCK * QKV_BLOCK, -1)
    return a.T if n == "w_if" else a


def _small_unview(n, a, shape):
    if len(shape) == 1:
        return a.reshape(shape)
    if len(shape) == 3:
        return a.reshape(QKV_BLOCK, QKV_BLOCK, -1).transpose(2, 0, 1)
    return a.T if n == "w_if" else a


def _small_shards(n, g):
    if n == "w_if":
        return g.reshape(N_DEV, -1, g.shape[1]).transpose(0, 2, 1)
    return g.reshape(g.shape[0], N_DEV, -1).transpose(1, 0, 2)


def _small_unshard(n, s):
    if n == "w_if":
        return s.transpose(0, 2, 1).reshape(-1, s.shape[1])
    return s.transpose(1, 0, 2).reshape(s.shape[1], -1)


def _to_hm(a, d):
    t = a.shape[0]
    return a.reshape(t, HEADS, d).transpose(1, 0, 2)


def _from_hm(a):
    h, t, d = a.shape
    return a.transpose(1, 0, 2).reshape(t, h * d)


def _gate_rows(g):
    t = g.shape[0]
    return g.T.reshape(HEADS, t // CHUNK, 1, CHUNK)


def _gate_cols(g):
    h, nc, _, c = g.shape
    return g.reshape(h, nc * c).T


def _blockdiag_dense(w):
    n = w.shape[0] * QKV_BLOCK // 2
    tiled = jnp.tile(w.reshape(2, n, QKV_BLOCK), (1, 1, n // QKV_BLOCK))
    r = lax.broadcasted_iota(jnp.int32, (2, n, n), 1)
    c = lax.broadcasted_iota(jnp.int32, (2, n, n), 2)
    return jnp.where(r // QKV_BLOCK == c // QKV_BLOCK, tiled, 0.0)


def _blockdiag_blocks(dense):
    _, n, _ = dense[0].shape
    k = len(dense)

    def body(*refs):
        r = lax.broadcasted_iota(jnp.int32, (n, n), 0)
        c = lax.broadcasted_iota(jnp.int32, (n, n), 1)
        fr = lax.broadcasted_iota(jnp.int32, (n, LANES), 0)
        fc = lax.broadcasted_iota(jnp.int32, (n, LANES), 1)
        fold = ((fr & (QKV_BLOCK - 1)) == fc).astype(BF16)
        for i in range(k):
            for half in range(2):
                kept = jnp.where((r >> 2) == (c >> 2), refs[i][half], 0.0)
                refs[k + i][half] = sum(lax.dot_general(t, fold, _dims("nn", 2), preferred_element_type=F32)
                                        for t in _split3(kept))

    out = pl.pallas_call(body, name="blockdiag_blocks", out_shape=[jax.ShapeDtypeStruct((2, n, LANES), F32)] * k)(*dense)
    return [o[:, :, 0:QKV_BLOCK].reshape(2 * n // QKV_BLOCK, QKV_BLOCK, QKV_BLOCK) for o in out]


def _col_blocks(w):
    k, n = w.shape
    return w.reshape(k, N_DEV, n // N_DEV).transpose(1, 0, 2)


def _from_col_blocks(g):
    d, k, n = g.shape
    return g.transpose(1, 0, 2).reshape(k, d * n)


def _first_norm(x, g):
    return _rowwise("pre_mix_norm", lambda xv, gv: ((_rms(xv, gv),), ()), [x], [g], [(x.shape[1], BF16)])[0]


def _local_step(x, h, tgt, weight, ws, prefetch, pass_on, on_grads, on_small):
    t, d = x.shape
    g1 = ws["g_pre_mix"]

    def dep(token):
        return () if token is None else (token,)

    w_in = weight("w_in", x)
    fetch_mix = prefetch(("w_pa", "w_pb", "w_o"), w_in)

    n_in = w_in.shape[2]

    offs = [0]
    for s in IN_SPLITS:
        offs.append(offs[-1] + s)

    w_a_up_p = jnp.pad(ws["w_a_up"], ((0, LANES - LOWRANK), (0, 0)))
    b_a_up = ws["b_a_up"]

    def proj_in_fwd(hv, w, wa, ba):
        proj = jnp.concatenate([_raw_dot(hv, w[j], "nn") for j in range(N_DEV)], axis=1)
        parts = [proj[:, offs[i]:offs[i + 1]] for i in range(len(IN_SPLITS))]
        parts[4] = jnp.concatenate([parts[4], jnp.zeros((parts[4].shape[0], LANES - LOWRANK), F32)], axis=1)
        return (*parts, _log_decay(parts[4], wa, ba)), ()

    widths = [LANES if s == LOWRANK else s for s in IN_SPLITS]
    q_a, k_a, v_a, g_a, a_low_p, x_m, o_pre, gate_a, gate_b, la = _rowwise(
        "proj_in", proj_in_fwd, [h], [w_in, w_a_up_p, b_a_up],
        [(wd, BF16 if i == 2 else F32) for i, wd in enumerate(widths)] + [(HEADS * GLA_DK, F32)],
        deps=dep(fetch_mix))

    fetch_up = prefetch(("w_up", "w_down"), la)
    q_hm, k_hm, la_hm = _to_hm(q_a, GLA_DK), _to_hm(k_a, GLA_DK), _to_hm(la, GLA_DK)
    gn = ws["g_gla_norm"]
    ml_w = HEADS * HEAD_W

    cw = ws["conv_w"]
    w_if_p = jnp.pad(ws["w_if"], ((0, 0), (0, LANES - 2 * HEADS)))
    pre_params = [cw[0:1], cw[1:2], cw[2:3], cw[3:4], ws["conv_b"],
                  _blockdiag_dense(ws["w_q_ml"]), _blockdiag_dense(ws["w_k_ml"]), _blockdiag_dense(ws["w_v_ml"]),
                  w_if_p[0:ml_w], w_if_p[ml_w:2 * ml_w], w_if_p[2 * ml_w:3 * ml_w],
                  jnp.pad(ws["b_if"], ((0, 0), (0, LANES - 2 * HEADS)))]
    x_pad = jnp.pad(x_m, ((HALO, 0), (0, 0)))
    xc, q_m, k_m, v_m, gl = _ml_pre_fwd(x_m, x_pad, pre_params, tile=512, deps=dep(fetch_up))
    pass_mix = pass_on("w_pa", xc)
    li, lf = _gate_rows(gl[:, 0:HEADS]), _gate_rows(gl[:, HEADS:2 * HEADS])
    o_gla, s_prev, hc, c_prev, n_prev, m_prev = _recurrences_fwd(q_hm, k_hm, v_a, la_hm, q_m, k_m, v_m, li, lf,
                                                                 deps=dep(pass_mix))
    g_ml, skip = ws["g_ml_norm"], ws["ml_skip"]

    def branches_out(o, g, a, b, c_, ga, gb, n_, wa, gm, s, wb):
        ya_in = jnp.concatenate(_per_head(_gla_out, [o, g], [], [n_]), axis=1)
        ya = _raw_dot(ya_in, wa, "nn")
        hb = jnp.concatenate(_per_head(_ml_out, [a, b, c_], [gm, s]), axis=1)
        yb = _raw_dot(hb, wb, "nn")
        return (ya_in, ya, hb, yb, _merge(ga, gb, ya, yb)), ()

    ya_in, y_a, h_b, y_b, merged = _rowwise(
        "branches_out", branches_out, [o_gla, g_a, hc, o_pre, xc, gate_a, gate_b],
        [gn, weight("w_pa", hc), g_ml, skip, weight("w_pb", hc)],
        [(ml_w, BF16), (d, BF16), (ml_w, BF16), (d, BF16), (d, BF16)], tile=512)

    gpm, gpl, gpo = ws["g_post_mix"], ws["g_pre_mlp"], ws["g_post_mlp"]

    def proj_o_fwd(mg, xv, w, a, b):
        zv = _raw_dot(mg, w, "nn")
        return (zv, *_post_mix(xv, zv, a, b)), ()

    pass_up = pass_on("w_up", merged)
    z, x1, h2 = _rowwise("proj_o", proj_o_fwd, [merged, x], [weight("w_o", merged), gpm, gpl],
                         [(d, F32), (d, F32), (d, BF16)], tile=512, deps=dep(pass_up))
    w_up = weight("w_up", h2)
    w_down = weight("w_down", h2)
    d_ff = w_down.shape[0]

    def mlp_loss(h2v, x1v, tgtv, wu, wd, g):
        upv = jnp.concatenate([_raw_dot(h2v, wu[j], "nn") for j in range(wu.shape[0])], axis=1)
        uv = jnp.square(jnp.maximum(upv, 0.0))
        dnv = _raw_dot(uv, wd, "nn")
        loss, vjp = jax.vjp(lambda a, b, c_: _loss_rows(a, b, tgtv, c_), x1v, dnv, g)
        dx1, ddn, dg = vjp(jnp.ones((1, 1), F32))
        return (upv, uv, dx1, ddn), (jnp.broadcast_to(loss, (1, LANES)), dg)

    up, u, dx1_y, d_dn, loss, d_gpo = _rowwise("mlp_loss", mlp_loss, [h2, x1, tgt], [w_up, w_down, gpo],
                                               [(d_ff, BF16), (d_ff, BF16), (d, F32), (d, BF16)],
                                               [((1, LANES), F32), ((1, d), F32)])

    dw_down = _mm(u, d_dn, "tn", BF16, "mlp_down_dw", tm=512)

    def mlp_dx(ddn, upv, xv, zv, dx1, wd, wu, a, b):
        dup = (_raw_dot(ddn, wd, "nt") * (2.0 * jnp.maximum(upv.astype(F32), 0.0))).astype(BF16)
        ns = wu.shape[2]
        dh2 = sum(_raw_dot(dup[:, j * ns:(j + 1) * ns], wu[j], "nt") for j in range(wu.shape[0]))
        _, vjp = jax.vjp(_post_mix, xv, zv, a, b)
        dx, dz, da, db = vjp((dx1, dh2))
        return (dup, dx, dz), (da, db)

    d_up, dx_res, d_z, d_gpm, d_gpl = _rowwise("mlp_dx", mlp_dx, [d_dn, up, x, z, dx1_y], [w_down, w_up, gpm, gpl],
                                               [(d_ff, BF16), (d, F32), (d, BF16)], [((1, d), F32), ((1, d), F32)])
    dw_up = _mm_shard_cols(h2, [d_up], [d_up.shape[1]], w_up.shape[2], "mlp_up_dw", 0, d)
    sent_mlp = on_grads(dict(w_down=dw_down, w_up=dw_up))

    def branches_out_bwd(dz, ga, gb, ya, yb, o, g, a, b, c_, wo, wa, n_, wb, gm, s):
        d_ga_, d_gb_, d_ya_, d_yb_ = jax.vjp(_merge, ga, gb, ya, yb)[1](_raw_dot(dz, wo, "nt"))
        ct_a, ct_b = _raw_dot(d_ya_, wa, "nt"), _raw_dot(d_yb_, wb, "nt")
        parts_a, parts_b = [], []
        for hs in _head_slices(HEAD_W):
            parts_a.append(jax.vjp(_gla_out, o[:, hs], g[:, hs], n_)[1](ct_a[:, hs]))
            parts_b.append(jax.vjp(_ml_out, a[:, hs], b[:, hs], c_[:, hs], gm[:, hs], s[:, hs])[1](ct_b[:, hs]))
        cat = lambda parts, i: jnp.concatenate([p[i] for p in parts], axis=1)
        return ((d_ga_, d_gb_, d_ya_, d_yb_, cat(parts_a, 0), cat(parts_a, 1), cat(parts_b, 0), cat(parts_b, 1), cat(parts_b, 2)),
                (sum(p[2] for p in parts_a), cat(parts_b, 3), cat(parts_b, 4)))

    d_ga, d_gb, d_ya, d_yb, d_o, d_g_a, d_hc, d_opre, d_xc, d_gn, d_gml, d_skip = _rowwise(
        "branches_out_bwd", branches_out_bwd, [d_z, gate_a, gate_b, y_a, y_b, o_gla, g_a, hc, o_pre, xc],
        [weight("w_o", merged), weight("w_pa", hc), gn, weight("w_pb", hc), g_ml, skip],
        [(d, BF16)] * 4 + [(ml_w, F32), (ml_w, BF16), (ml_w, F32), (ml_w, BF16), (ml_w, F32)],
        [((1, HEAD_W), F32), ((1, ml_w), F32), ((1, ml_w), F32)], deps=dep(sent_mlp))
    dw_o, dw_pa, dw_pb = _mm_tn_whole([(merged, d_z), (ya_in, d_ya), (h_b, d_yb)], "mix_dw")
    sent_mix = on_grads(dict(w_o=dw_o, w_pa=dw_pa, w_pb=dw_pb))

    dq_hm, dk_hm, d_va, dla_hm, d_qm, d_km, d_vm, d_li, d_lf = _recurrences_bwd(
        q_hm, k_hm, v_a, la_hm, s_prev, d_o, q_m, k_m, v_m, li, lf, c_prev, n_prev, m_prev, d_hc, deps=dep(sent_mix))
    d_gl = jnp.concatenate([_gate_cols(d_li), _gate_cols(d_lf), jnp.zeros((t, LANES - 2 * HEADS), F32)], axis=1)
    pre_grads = _ml_pre_bwd(x_m, x_pad, pre_params, [d_xc, d_qm, d_km, d_vm, d_gl], tile=512)
    d_xm = pre_grads[0]
    d_cw = jnp.concatenate(pre_grads[1:5], axis=0)
    d_cb = pre_grads[5]
    d_wq, d_wk, d_wv = _blockdiag_blocks(pre_grads[6:9])
    d_wif = jnp.concatenate(pre_grads[9:12], axis=0)[:, 0:2 * HEADS]
    d_bif = pre_grads[12][:, 0:2 * HEADS]

    def decay_bwd(al, ct, w, b):
        _, vjp = jax.vjp(_log_decay, al, w, b)
        dal, dw, db = vjp(ct)
        return (dal,), (dw, db)

    d_alow_p, d_wa_p, d_ba = _rowwise("gla_decay_bwd", decay_bwd, [a_low_p, _from_hm(dla_hm)], [w_a_up_p, b_a_up],
                                      [(LANES, BF16)], [(w_a_up_p.shape, F32), (b_a_up.shape, F32)])
    d_proj = [jnp.concatenate([_from_hm(dq_hm), _from_hm(dk_hm), d_va, d_g_a], axis=1), d_alow_p,
              jnp.concatenate([d_xm, d_opre, d_ga, d_gb], axis=1)]
    d_widths = [offs[4], LOWRANK, offs[9] - offs[5]]
    d_pieces = _shard_pieces(d_widths, n_in)
    small = dict(w_a_up=d_wa_p[0:LOWRANK], b_a_up=d_ba, g_gla_norm=d_gn, conv_w=d_cw, conv_b=d_cb,
                 w_q_ml=d_wq, w_k_ml=d_wk, w_v_ml=d_wv, w_if=d_wif, b_if=d_bif, ml_skip=d_skip, g_ml_norm=d_gml,
                 g_post_mix=d_gpm, g_pre_mlp=d_gpl, g_post_mlp=d_gpo)
    sent_small = on_small(small, loss)
    sent_in = sent_small
    for half in range(2):
        dw_half = _mm_shard_cols(h, d_proj, d_widths, n_in, "proj_in_dw_%d" % half, half, d // 2, deps=dep(sent_in))
        sent_in = on_grads({"w_in#%d" % half: dw_half})

    def proj_in_dx(dp_a, dp_low, dp_b, xv, dres, w, g):
        dh = 0.0
        for s in range(N_DEV):
            for i, c_in, c_w, wd in d_pieces[s]:
                src = (dp_a, dp_low, dp_b)[i]
                cols = src.shape[1] - c_in if wd < LANES else wd
                dh = dh + _raw_dot(src[:, c_in:c_in + cols], w[s][:, c_w:c_w + cols], "nt")
        _, vjp = jax.vjp(_rms, xv, g)
        dx, dg = vjp(dh)
        return (dx + dres,), (dg,)

    grad_x, d_g1 = _rowwise("proj_in_dx", proj_in_dx, [*d_proj, x, dx_res], [w_in, g1], [(d, F32)], [((1, d), F32)],
                            deps=dep(sent_in))
    return grad_x, on_small(dict(g_pre_mix=d_g1), None)


BIG = ("w_in", "w_pa", "w_pb", "w_o", "w_up", "w_down")
MIX = ("w_o", "w_pa", "w_pb")
BIG_COL_SHARDED = ("w_in", "w_pa", "w_pb", "w_up")
SMALL_SHARDED = ("w_a_up", "conv_w", "w_if")
SMALL = ("g_pre_mix", "w_a_up", "b_a_up", "g_gla_norm", "conv_w", "conv_b", "w_q_ml", "w_k_ml", "w_v_ml", "w_if", "b_if",
         "ml_skip", "g_ml_norm", "g_post_mix", "g_pre_mlp", "g_post_mlp")
WEIGHTS = ("g_pre_mix", "w_in", "w_a_up", "b_a_up", "g_gla_norm", "conv_w", "conv_b", "w_q_ml", "w_k_ml", "w_v_ml", "w_if", "b_if",
           "ml_skip", "g_ml_norm", "w_pa", "w_pb", "w_o", "g_post_mix", "g_pre_mlp", "w_up", "w_down", "g_post_mlp")


def kernel(x, g_pre_mix, w_in, w_a_up, b_a_up, g_gla_norm, conv_w, conv_b, w_q_ml, w_k_ml, w_v_ml, w_if, b_if, ml_skip, g_ml_norm, w_pa, w_pb, w_o, g_post_mix, g_pre_mlp, w_up, w_down, g_post_mlp, loss_target, m_g_pre_mix, m_w_in, m_w_a_up, m_b_a_up, m_g_gla_norm, m_conv_w, m_conv_b, m_w_q_ml, m_w_k_ml, m_w_v_ml, m_w_if, m_b_if, m_ml_skip, m_g_ml_norm, m_w_pa, m_w_pb, m_w_o, m_g_post_mix, m_g_pre_mlp, m_w_up, m_w_down, m_g_post_mlp, v_g_pre_mix, v_w_in, v_w_a_up, v_b_a_up, v_g_gla_norm, v_conv_w, v_conv_b, v_w_q_ml, v_w_k_ml, v_w_v_ml, v_w_if, v_b_if, v_ml_skip, v_g_ml_norm, v_w_pa, v_w_pb, v_w_o, v_g_post_mix, v_g_pre_mlp, v_w_up, v_w_down, v_g_post_mlp):
    args = dict(locals())
    w = {n: args[n][0] for n in WEIGHTS}
    m = {n: args["m_" + n][0] for n in WEIGHTS}
    v = {n: args["v_" + n][0] for n in WEIGHTS}

    me_lin = _lin(_me())
    me_idx = jnp.reshape(me_lin, (1,)).astype(jnp.int32)

    def full_weight(n, g):
        if n in ("w_in", "w_up"):
            return g
        return _from_col_blocks(g) if n in BIG_COL_SHARDED else g.reshape(-1, g.shape[-1])

    def grad_parts(n, g):
        if n.partition("#")[0] in ("w_in", "w_up"):
            return g
        return (_col_blocks(g) if n in BIG_COL_SHARDED else g.reshape(N_DEV, -1, g.shape[-1])).astype(BF16)

    sharded_names = tuple(SMALL_SHARDED)
    narrow = {n: w[n].astype(BF16) for n in BIG}
    ready, pending, passing = {}, {}, {}
    first_state, _ = _copies_start(["gather"] * len(sharded_names) + ["gather_chips"],
                                   [_small_view(n, w[n]) for n in sharded_names] + [narrow["w_in"]], "allgather_start_first")

    def prefetch(group, after):
        state, token = _copies_start("gather_chips", [narrow[n] for n in group], "allgather_start_" + group[0], after)
        for n in group:
            pending[n] = (group, state)
        return token

    def pass_on(n, after):
        group, state = pending[n]
        shards, lands = _copies_wait(state, after, "allgather_wait_" + group[0])
        state, token = _copies_start("gather_pass", shards, "allgather_pass_" + group[0], lands=lands)
        for gn in group:
            passing[gn] = (group, state)
        return token

    def weight(n, after):
        if n not in ready:
            group, state = passing[n]
            shards, lands = _copies_wait(state, after, "allgather_passed_" + group[0])
            for gn, shard, land in zip(group, shards, lands):
                ready[gn] = full_weight(gn, lax.dynamic_update_slice(land, shard[None], (me_lin, 0, 0)))
        return ready[n]

    h = _first_norm(x[0], w["g_pre_mix"].reshape(1, -1))
    first_own, first_lands = _copies_wait(first_state, [h] + [narrow[n] for n in BIG if n != "w_in"], "allgather_wait_first")
    state, _ = _copies_start("gather_pass", first_own[-1:], "allgather_pass_w_in", lands=first_lands[-1:])
    passing["w_in"] = (("w_in",), state)
    ws = {n: (w[n].reshape(1, -1) if w[n].ndim == 1 else w[n]) for n in SMALL if n not in SMALL_SHARDED}
    for n, own, land in zip(sharded_names, first_own, first_lands):
        ws[n] = _small_unshard(n, lax.dynamic_update_slice(land, own[None], (me_lin, 0, 0)))

    sets, waiting_small = [], []

    def start_set(large, small):
        names = tuple(large)
        s_names, s_kinds, s_srcs = small if small else ((), [], [])
        state, token = _copies_start(s_kinds + ["exchange"] * len(names), s_srcs + [grad_parts(n, large[n]) for n in names],
                                     "exchange_start_" + (names + s_names)[0].replace("#", "_"))
        sets.append((names, s_names, s_kinds, state))
        return token

    def on_grads(grads):
        return start_set(grads, waiting_small.pop() if waiting_small else None)

    def on_small(small, loss):
        names = tuple(small)
        kinds = ["exchange" if n in SMALL_SHARDED else "gather" for n in names]
        srcs = [_small_shards(n, small[n]) if n in SMALL_SHARDED else _small_view(n, small[n]) for n in names]
        if loss is None:
            return start_set({}, (names, kinds, srcs))
        waiting_small.append((names, kinds + ["gather"], srcs + [loss]))
        return None

    grad_x, last_token = _local_step(x[0], h, loss_target[0], weight, ws, prefetch, pass_on, on_grads, on_small)

    out, chunks, sums = {}, {}, []

    def finish_set(names, s_names, s_kinds, state, after):
        own, lands = _copies_wait(state, after, "exchange_wait_" + (names + s_names)[0].replace("#", "_"))
        ns = len(s_kinds)
        if s_names:
            k = len(s_names)
            upd = _small_update("adamw_small_" + s_names[0], me_idx, s_kinds[:k], lands[:k], own[:k],
                                *[[_small_view(n, d[n]) for n in s_names] for d in (w, m, v)],
                                sums=list(zip(lands[k:ns], own[k:ns])))
            for i, n in enumerate(s_names):
                out[n] = tuple(_small_unview(n, a, w[n].shape) for a in upd[4 * i:4 * i + 4])
            sums.extend(upd[4 * k:])
        if names == MIX:
            upd = _small_update("adamw_mix", me_idx, ["exchange"] * len(names), lands[ns:], own[ns:],
                                *[[d[n] for n in names] for d in (w, m, v)])
            for i, n in enumerate(names):
                out[n] = tuple(upd[4 * i:4 * i + 4])
            return
        for name, part, land in zip(names, own[ns:], lands[ns:]):
            n, _, chunk = name.partition("#")
            chunks.setdefault(n, []).append((land, part))
            if chunk in ("", "1"):
                got_lands, got_parts = zip(*chunks[n])
                out[n] = _sum_adamw(got_lands, got_parts, me_idx, w[n], m[n], v[n], "adamw_" + n)

    for entry in sets:
        finish_set(*entry, [grad_x, last_token] + [out[n][1] for n in BIG if n in out])
    loss_sum = sums[0]

    shaped = lambda a, n: a.reshape(args[n].shape)
    return (loss_sum[0, 0], grad_x[None],
            *[shaped(out[n][0], n) for n in WEIGHTS], *[shaped(out[n][1], n) for n in WEIGHTS],
            *[shaped(out[n][2], n) for n in WEIGHTS], *[shaped(out[n][3], n) for n in WEIGHTS])
```

```python
import functools

import jax
import jax.numpy as jnp
from jax import lax
from jax.experimental import pallas as pl
from jax.experimental.pallas import tpu as pltpu

F32 = jnp.float32
BF16 = jnp.bfloat16
MESH = pl.DeviceIdType.MESH

N_DEV = 8
EPS = 1e-6
CHUNK = 64
CHUNKS_PER_STEP = 8
HEADS = 4
GLA_DK = 64
HEAD_W = 128
GLA_GATE_NORM = 16.0
LOWRANK = 16
CONV_K = 4
QKV_BLOCK = 4
LANES = 128
HALO = 8
IN_SPLITS = (256, 256, 512, 512, 16, 512, 512, 1024, 1024)

ADAM_LR = 0.001
ADAM_B1 = 0.9
ADAM_B2 = 0.999
ADAM_EPS = 1e-08
ADAM_WD = 0.01
ADAM_STEP = 10

VMEM_LIMIT = 56 * 1024 * 1024


def _cparams(*sem):
    return pltpu.CompilerParams(dimension_semantics=sem, vmem_limit_bytes=VMEM_LIMIT)


def _dims(mode, ndim):
    contract = {"nn": ((ndim - 1,), (ndim - 2,)), "nt": ((ndim - 1,), (ndim - 1,)), "tn": ((ndim - 2,), (ndim - 2,))}[mode]
    return contract, (((0,), (0,)) if ndim == 3 else ((), ()))


def _raw_dot(a, b, mode):
    return lax.dot_general(a.astype(BF16), b.astype(BF16), _dims(mode, a.ndim), preferred_element_type=F32)


@functools.partial(jax.custom_vjp, nondiff_argnums=(2,))
def _bdot(a, b, mode):
    return _raw_dot(a, b, mode)


def _bdot_fwd(a, b, mode):
    return _raw_dot(a, b, mode), (a, b)


def _bdot_bwd(mode, res, ct):
    a, b = res
    if mode == "nn":
        da, db = _raw_dot(ct, b, "nt"), _raw_dot(a, ct, "tn")
    elif mode == "nt":
        da, db = _raw_dot(ct, b, "nn"), _raw_dot(ct, a, "tn")
    else:
        da, db = _raw_dot(b, ct, "nt"), _raw_dot(a, ct, "nn")
    return da.astype(a.dtype), db.astype(b.dtype)


_bdot.defvjp(_bdot_fwd, _bdot_bwd)


def _split3(x):
    hi = x.astype(BF16)
    r1 = x - hi.astype(F32)
    mid = r1.astype(BF16)
    return hi, mid, (r1 - mid.astype(F32)).astype(BF16)


def _split_dot(tri, x):
    if x.ndim == 3:
        tri = jnp.broadcast_to(tri, (x.shape[0], *tri.shape))
    return sum(lax.dot_general(tri, t, _dims("nn", x.ndim), preferred_element_type=F32) for t in _split3(x))


def _tri(n, lower):
    r = lax.broadcasted_iota(jnp.int32, (n, n), 0)
    c = lax.broadcasted_iota(jnp.int32, (n, n), 1)
    return ((c <= r) if lower else (c >= r)).astype(BF16)


@jax.custom_vjp
def _cumsum_rows(x):
    return _split_dot(_tri(x.shape[-2], True), x)


def _cumsum_rows_fwd(x):
    return _cumsum_rows(x), None


def _cumsum_rows_bwd(_, ct):
    return (_split_dot(_tri(ct.shape[-2], False), ct),)


_cumsum_rows.defvjp(_cumsum_rows_fwd, _cumsum_rows_bwd)


def _abs(x):
    return jnp.where(x >= 0, x, -x)


def _sigmoid(x):
    return lax.logistic(x)


def _log_sigmoid(x):
    return jnp.minimum(x, 0.0) - jnp.log(1.0 + jnp.exp(-_abs(x)))


def _rms(x, g):
    return x * lax.rsqrt(jnp.mean(x * x, axis=-1, keepdims=True) + EPS) * g


def _head_slices(w):
    return [slice(h * w, (h + 1) * w) for h in range(HEADS)]


def _heads(ref, rows=slice(None)):
    return jnp.stack([ref[rows, hs] for hs in _head_slices(HEAD_W)])


def _put_heads(ref, val, rows=slice(None)):
    for h, hs in enumerate(_head_slices(HEAD_W)):
        ref[rows, hs] = val[h].astype(ref.dtype)


def _tile(dim, want):
    if dim <= want or dim % LANES:
        return dim
    t = want
    while dim % t:
        t -= LANES
    return t


def _mm(a, b, mode, out_dtype, name, tm=1024, tn=1024, tk=4096, epilogue=None, extra=(), deps=(), shards=None):
    if shards == "b":
        assert mode == "nn"
        ns = b.shape[2]
        (m, k), (k2, n) = a.shape, (b.shape[1], b.shape[0] * ns)
        tn = ns
    elif mode == "nn":
        (m, k), (k2, n) = a.shape, b.shape
    elif mode == "nt":
        (m, k), (n, k2) = a.shape, b.shape
    else:
        (k, m), (k2, n) = a.shape, b.shape
    assert k == k2, (name, a.shape, b.shape)
    tm, tn, tk = _tile(m, tm), _tile(n, tn), _tile(k, tk)
    nk = k // tk
    out_dtypes = out_dtype if epilogue else (out_dtype,)
    assert nk == 1 or (out_dtype == F32 and not epilogue), name
    n_in = 2 + len(extra)

    def body(*refs):
        p = _raw_dot(refs[0][...], refs[1][...], mode)
        if nk > 1:
            _accumulate(pl.program_id(2), [refs[n_in + len(deps)]], [p])
            return
        outs = epilogue(p, *[r[...] for r in refs[2:n_in]]) if epilogue else (p,)
        for ref, val in zip(refs[n_in + len(deps):], outs):
            ref[...] = val.astype(ref.dtype)

    a_spec = pl.BlockSpec((tk, tm), lambda i, j, kk: (kk, i)) if mode == "tn" else pl.BlockSpec((tm, tk), lambda i, j, kk: (i, kk))
    if shards == "b":
        b_spec = pl.BlockSpec((None, tk, tn), lambda i, j, kk: (j, kk, 0))
    elif mode == "nt":
        b_spec = pl.BlockSpec((tn, tk), lambda i, j, kk: (j, kk))
    else:
        b_spec = pl.BlockSpec((tk, tn), lambda i, j, kk: (kk, j))
    o_spec = pl.BlockSpec((tm, tn), lambda i, j, kk: (i, j))
    res = pl.pallas_call(
        body, name=name, grid=(m // tm, n // tn, nk),
        in_specs=[a_spec, b_spec] + [o_spec] * len(extra) + [ANY] * len(deps), out_specs=[o_spec] * len(out_dtypes),
        out_shape=[jax.ShapeDtypeStruct((m, n), dt) for dt in out_dtypes],
        compiler_params=_cparams("parallel", "parallel", "arbitrary"),
    )(a, b, *extra, *deps)
    return res if epilogue else res[0]


def _mm_tn_whole(pairs, name):
    def body(*refs):
        for i in range(len(pairs)):
            refs[2 * len(pairs) + i][...] = _raw_dot(refs[2 * i][...], refs[2 * i + 1][...], "tn").astype(BF16)

    return pl.pallas_call(
        body, name=name,
        out_shape=[jax.ShapeDtypeStruct((a.shape[1], b.shape[1]), BF16) for a, b in pairs],
        compiler_params=pltpu.CompilerParams(vmem_limit_bytes=VMEM_LIMIT),
    )(*[x for pair in pairs for x in pair])


def _shard_pieces(widths, n):
    bounds = [0]
    for wd in widths:
        bounds.append(bounds[-1] + wd)
    assert bounds[-1] == N_DEV * n
    return [[(i, max(s * n, b) - b, max(s * n, b) - s * n, min((s + 1) * n, b + wd) - max(s * n, b))
             for i, (b, wd) in enumerate(zip(bounds, widths)) if b < (s + 1) * n and b + wd > s * n]
            for s in range(N_DEV)]


def _mm_shard_cols(a, bs, widths, n, name, row_tile, tm, deps=()):
    t = a.shape[0]
    nb = len(bs)
    pieces = _shard_pieces(widths, n)

    def body(a_ref, *rest):
        b_refs = rest[:nb]
        o_ref, at_ref = rest[nb + len(deps):]
        j = pl.program_id(0)

        @pl.when(j == 0)
        def _():
            at_ref[...] = a_ref[...].astype(BF16).T

        for s in range(N_DEV):
            @pl.when(j == s)
            def _(s=s):
                for i, c_in, c_out, wd in pieces[s]:
                    cols = min(_round_up(wd, LANES), bs[i].shape[1] - c_in) if wd < LANES else wd
                    p = _raw_dot(at_ref[...], b_refs[i][:, c_in:c_in + cols], "nn")
                    o_ref[:, c_out:c_out + wd] = p[:, 0:wd].astype(BF16)

    return pl.pallas_call(
        body, name=name, grid=(N_DEV,),
        in_specs=[pl.BlockSpec((t, tm), lambda j: (0, row_tile))]
        + [pl.BlockSpec(b.shape, lambda j: (0, 0), pipeline_mode=pl.Buffered(1)) for b in bs] + [ANY] * len(deps),
        out_specs=pl.BlockSpec((None, tm, n), lambda j: (j, 0, 0)),
        out_shape=jax.ShapeDtypeStruct((N_DEV, tm, n), BF16),
        scratch_shapes=[pltpu.VMEM((tm, t), BF16)],
        compiler_params=_cparams("arbitrary"),
    )(a, *bs, *deps)


def _round_up(v, m):
    return -(-v // m) * m


def _rowwise(name, fn, rows, params, out_rows, out_accs=(), tile=256, deps=()):
    t = rows[0].shape[0]
    r = min(tile, t)
    assert t % r == 0
    n_in, n_or = len(rows) + len(params), len(out_rows)
    n_all = n_in + len(deps)
    params = list(params) + list(deps)

    def body(*refs):
        vals = [ref[...] for ref in refs[:n_in]]
        outs = refs[n_all:]
        ro, ao = fn(*vals)
        for ref, v in zip(outs[:n_or], ro):
            ref[...] = v.astype(ref.dtype)
        if out_accs:
            _accumulate(pl.program_id(0), outs[n_or:], ao)

    def full(shape, **kw):
        return pl.BlockSpec(shape, lambda i, nd=len(shape): (0,) * nd, **kw)

    return pl.pallas_call(
        body, name=name, grid=(t // r,),
        in_specs=[pl.BlockSpec((r, a.shape[1]), lambda i: (i, 0)) for a in rows]
        + [full(p.shape, pipeline_mode=pl.Buffered(1)) for p in params],
        out_specs=[pl.BlockSpec((r, w), lambda i: (i, 0)) for w, _ in out_rows] + [full(s) for s, _ in out_accs],
        out_shape=[jax.ShapeDtypeStruct((t, w), dt) for w, dt in out_rows] + [jax.ShapeDtypeStruct(s, dt) for s, dt in out_accs],
        compiler_params=_cparams("arbitrary"),
    )(*rows, *params)


def _accumulate(step, refs, vals):
    for ref, v in zip(refs, vals):
        @pl.when(step == 0)
        def _(ref=ref, v=v):
            ref[...] = v.astype(ref.dtype)

        @pl.when(step > 0)
        def _(ref=ref, v=v):
            ref[...] += v.astype(ref.dtype)


def _gla_chunk(q, k, v, la, st):
    c = q.shape[-2]
    row = lax.broadcasted_iota(jnp.int32, (c, c), 0)
    col = lax.broadcasted_iota(jnp.int32, (c, c), 1)
    cum = _cumsum_rows(la)
    cl = jnp.sum(la, axis=-2, keepdims=True)
    ep = jnp.exp(cum)
    en = jnp.exp(-cum)
    qs = q * (GLA_DK ** -0.5)
    qp = qs * ep
    a_f = _bdot(qp, k * en, "nt")
    a_b = _bdot(qs * en, k * ep, "nt")
    sc = jnp.where(row >= col, a_f, a_b)
    o = _bdot(sc, v, "nn") + _bdot(qp, st, "nt")
    kd = k * jnp.exp(cl - cum)
    st_new = st * jnp.exp(cl) + _bdot(v, kd, "tn")
    return o, st_new


def _gla_specs(nc, rev):
    nb = nc // CHUNKS_PER_STEP
    rows = CHUNKS_PER_STEP * CHUNK

    def blk(n):
        return (nb - 1 - n) if rev else n
    hm = pl.BlockSpec((HEADS, rows, GLA_DK), lambda n: (0, blk(n), 0))
    tm = pl.BlockSpec((rows, HEADS * HEAD_W), lambda n: (blk(n), 0))
    st = pl.BlockSpec((HEADS, CHUNKS_PER_STEP, HEAD_W, GLA_DK), lambda n: (0, blk(n), 0, 0))
    return nb, hm, tm, st


def _chunk_rows(c):
    return slice(c * CHUNK, (c + 1) * CHUNK)


def _ml_chunk(q, k, v, li_r, lf_r, cm, nv, m):
    c = q.shape[-2]
    row = lax.broadcasted_iota(jnp.int32, (c, c), 0)
    col = lax.broadcasted_iota(jnp.int32, (c, c), 1)
    eye = (row == col).astype(F32)
    li_c = jnp.sum(eye * li_r, axis=-1, keepdims=True)
    lf_c = jnp.sum(eye * lf_r, axis=-1, keepdims=True)
    fc_c = jnp.sum((col <= row).astype(F32) * lf_r, axis=-1, keepdims=True)
    fc_r = jnp.sum((row <= col).astype(F32) * lf_c, axis=-2, keepdims=True)
    f_last = jnp.sum(lf_r, axis=-1, keepdims=True)
    kc = k * (HEAD_W ** -0.5)
    a_c = f_last - fc_c + li_c
    m_loc = jnp.max(a_c, axis=-2, keepdims=True)
    kw = kc * jnp.exp(a_c - m_loc)
    c_chunk = _bdot(kw, v, "tn")
    n_chunk = jnp.sum(kw, axis=-2, keepdims=True)
    m_new = jnp.maximum(f_last + m, m_loc)
    sp = jnp.exp(f_last + m - m_new)
    sl = jnp.exp(m_loc - m_new)
    cm_new = sp * cm + sl * c_chunk
    nv_new = sp * nv + sl * n_chunk
    log_d = li_r - _abs(fc_c - fc_r)
    g_inter = fc_c + m
    m_t = jnp.maximum(g_inter, jnp.max(log_d, axis=-1, keepdims=True))
    s = _bdot(q, kc, "nt") * jnp.exp(log_d - m_t)
    sc = jnp.exp(g_inter - m_t)
    num = _bdot(s, v, "nn") + sc * _bdot(q, cm, "nn")
    den = jnp.sum(s, axis=-1, keepdims=True) + sc * jnp.sum(q * nv, axis=-1, keepdims=True)
    den = jnp.maximum(_abs(den), jnp.exp(-m_t))
    return num / den, cm_new, nv_new, m_new


def _ml_specs(nc, rev):
    nb = nc // CHUNKS_PER_STEP

    def blk(n):
        return (nb - 1 - n) if rev else n
    tm = pl.BlockSpec((CHUNKS_PER_STEP * CHUNK, HEADS * HEAD_W), lambda n: (blk(n), 0))
    gate = pl.BlockSpec((HEADS, CHUNKS_PER_STEP, 1, CHUNK), lambda n: (0, blk(n), 0, 0))
    cm = pl.BlockSpec((HEADS, CHUNKS_PER_STEP, HEAD_W, HEAD_W), lambda n: (0, blk(n), 0, 0))
    vec = pl.BlockSpec((HEADS, CHUNKS_PER_STEP, 1, HEAD_W), lambda n: (0, blk(n), 0, 0))
    return nb, tm, gate, cm, vec


_ML_STATE = [pltpu.VMEM((HEADS, HEAD_W, HEAD_W), F32), pltpu.VMEM((HEADS, 1, HEAD_W), F32), pltpu.VMEM((HEADS, 1, HEAD_W), F32)]


def _recurrences_fwd(q, k, v, la, qm, km, vm, li, lf, deps=()):
    t = v.shape[0]
    nc = t // CHUNK
    nb, hm, tm, st = _gla_specs(nc, False)
    _, _, gate, cm, vec = _ml_specs(nc, False)
    n_in = 9 + len(deps)

    def body(*refs):
        q_ref, k_ref, v_ref, la_ref, qm_ref, km_ref, vm_ref, li_ref, lf_ref = refs[:9]
        o_ref, sp_ref, hc_ref, cp_ref, np_ref, mp_ref, st_ref, c_ref, n_ref, m_ref = refs[n_in:]

        @pl.when(pl.program_id(0) == 0)
        def _():
            for ref in (st_ref, c_ref, n_ref, m_ref):
                ref[...] = jnp.zeros_like(ref)

        s, cs, ns, ms = st_ref[...], c_ref[...], n_ref[...], m_ref[...][:, :, 0:1]
        for c in range(CHUNKS_PER_STEP):
            r = _chunk_rows(c)
            sp_ref[:, c] = s
            o, s = _gla_chunk(q_ref[:, r], k_ref[:, r], _heads(v_ref, r), la_ref[:, r], s)
            _put_heads(o_ref, o, r)
            cp_ref[:, c] = cs
            np_ref[:, c] = ns
            mp_ref[:, c] = jnp.broadcast_to(ms, m_ref.shape)
            hc, cs, ns, ms = _ml_chunk(_heads(qm_ref, r), _heads(km_ref, r), _heads(vm_ref, r), li_ref[:, c], lf_ref[:, c],
                                       cs, ns, ms)
            _put_heads(hc_ref, hc, r)
        st_ref[...] = s
        c_ref[...] = cs
        n_ref[...] = ns
        m_ref[...] = jnp.broadcast_to(ms, m_ref.shape)

    tm_shape = jax.ShapeDtypeStruct((t, HEADS * HEAD_W), F32)
    vec_shape = jax.ShapeDtypeStruct((HEADS, nc, 1, HEAD_W), F32)
    return pl.pallas_call(
        body, name="recurrences_fwd", grid=(nb,),
        in_specs=[hm, hm, tm, hm, tm, tm, tm, gate, gate] + [ANY] * len(deps), out_specs=[tm, st, tm, cm, vec, vec],
        out_shape=[tm_shape, jax.ShapeDtypeStruct((HEADS, nc, HEAD_W, GLA_DK), F32),
                   tm_shape, jax.ShapeDtypeStruct((HEADS, nc, HEAD_W, HEAD_W), F32), vec_shape, vec_shape],
        scratch_shapes=[pltpu.VMEM((HEADS, HEAD_W, GLA_DK), F32)] + _ML_STATE,
        compiler_params=_cparams("arbitrary"),
    )(q, k, v, la, qm, km, vm, li, lf, *deps)


def _recurrences_bwd(q, k, v, la, sp, do, qm, km, vm, li, lf, cp, npv, mp, dhc, deps=()):
    t = v.shape[0]
    nc = t // CHUNK
    nb, hm, tm, st = _gla_specs(nc, True)
    _, _, gate, cm, vec = _ml_specs(nc, True)
    n_in = 15 + len(deps)

    def body(*refs):
        (q_ref, k_ref, v_ref, la_ref, sp_ref, do_ref,
         qm_ref, km_ref, vm_ref, li_ref, lf_ref, cp_ref, np_ref, mp_ref, dhc_ref) = refs[:15]
        (dq_ref, dk_ref, dv_ref, dla_ref, dqm_ref, dkm_ref, dvm_ref, dli_ref, dlf_ref,
         ds_ref, dc_ref, dn_ref, dm_ref) = refs[n_in:]

        @pl.when(pl.program_id(0) == 0)
        def _():
            for ref in (ds_ref, dc_ref, dn_ref, dm_ref):
                ref[...] = jnp.zeros_like(ref)

        ds, dc, dn, dm = ds_ref[...], dc_ref[...], dn_ref[...], dm_ref[...][:, :, 0:1]
        for c in reversed(range(CHUNKS_PER_STEP)):
            r = _chunk_rows(c)
            _, vjp = jax.vjp(_gla_chunk, q_ref[:, r], k_ref[:, r], _heads(v_ref, r), la_ref[:, r], sp_ref[:, c])
            dq, dk, dv, dla, ds = vjp((_heads(do_ref, r), ds))
            dq_ref[:, r] = dq.astype(dq_ref.dtype)
            dk_ref[:, r] = dk.astype(dk_ref.dtype)
            _put_heads(dv_ref, dv, r)
            dla_ref[:, r] = dla
            _, vjp = jax.vjp(_ml_chunk, _heads(qm_ref, r), _heads(km_ref, r), _heads(vm_ref, r), li_ref[:, c], lf_ref[:, c],
                             cp_ref[:, c], np_ref[:, c], mp_ref[:, c][:, :, 0:1])
            dqm, dkm, dvm, dli, dlf, dc, dn, dm = vjp((_heads(dhc_ref, r), dc, dn, dm))
            _put_heads(dqm_ref, dqm, r)
            _put_heads(dkm_ref, dkm, r)
            _put_heads(dvm_ref, dvm, r)
            dli_ref[:, c] = dli
            dlf_ref[:, c] = dlf
        ds_ref[...] = ds
        dc_ref[...] = dc
        dn_ref[...] = dn
        dm_ref[...] = jnp.broadcast_to(dm, dm_ref.shape)

    hm_shape = jax.ShapeDtypeStruct((HEADS, t, GLA_DK), BF16)
    tm_shape = jax.ShapeDtypeStruct((t, HEADS * HEAD_W), F32)
    gate_shape = jax.ShapeDtypeStruct((HEADS, nc, 1, CHUNK), F32)
    return pl.pallas_call(
        body, name="recurrences_bwd", grid=(nb,),
        in_specs=[hm, hm, tm, hm, st, tm, tm, tm, tm, gate, gate, cm, vec, vec, tm] + [ANY] * len(deps),
        out_specs=[hm, hm, tm, hm, tm, tm, tm, gate, gate],
        out_shape=[hm_shape, hm_shape, jax.ShapeDtypeStruct((t, HEADS * HEAD_W), BF16),
                   jax.ShapeDtypeStruct((HEADS, t, GLA_DK), F32), tm_shape, tm_shape, tm_shape, gate_shape, gate_shape],
        scratch_shapes=[pltpu.VMEM((HEADS, HEAD_W, GLA_DK), F32)] + _ML_STATE,
        compiler_params=_cparams("arbitrary"),
    )(q, k, v, la, sp, do, qm, km, vm, li, lf, cp, npv, mp, dhc, *deps)


@jax.custom_vjp
def _bdot_diag(x, w):
    b = w.shape[1]
    return jnp.concatenate([_raw_dot(x[:, :b], w[0], "nn"), _raw_dot(x[:, b:], w[1], "nn")], axis=1)


def _bdot_diag_fwd(x, w):
    return _bdot_diag(x, w), (x, w)


def _bdot_diag_bwd(res, ct):
    x, w = res
    b = w.shape[1]
    dx = jnp.concatenate([_raw_dot(ct[:, :b], w[0], "nt"), _raw_dot(ct[:, b:], w[1], "nt")], axis=1)
    dw = jnp.stack([_raw_dot(x[:, :b], ct[:, :b], "tn"), _raw_dot(x[:, b:], ct[:, b:], "tn")])
    return dx.astype(x.dtype), dw.astype(w.dtype)


_bdot_diag.defvjp(_bdot_diag_fwd, _bdot_diag_bwd)


def _ml_pre(s0, s1, s2, s3, cw0, cw1, cw2, cw3, cb, wq, wk, wv, wiq, wik, wiv, bif):
    pre = cb + cw0 * s0 + cw1 * s1 + cw2 * s2 + cw3 * s3
    xc = pre * _sigmoid(pre)
    q = _bdot_diag(xc, wq)
    k = _bdot_diag(xc, wk)
    v = _bdot_diag(s3, wv)
    gates = _bdot(q, wiq, "nn") + _bdot(k, wik, "nn") + _bdot(v, wiv, "nn") + bif
    lane = lax.broadcasted_iota(jnp.int32, gates.shape, 1)
    gl = jnp.where(lane < HEADS, gates, _log_sigmoid(gates))
    return xc, q, k, v, gl


def _delayed(xs_ref, x_ref, halo_ref, r):
    xs_ref[0:HALO, :] = halo_ref[...]
    xs_ref[HALO:HALO + r, :] = x_ref[...]
    return [xs_ref[pl.ds(HALO - (CONV_K - 1) + j, r), :] for j in range(CONV_K)]


def _full_spec(shape):
    return pl.BlockSpec(shape, lambda i, nd=len(shape): (0,) * nd)


def _ml_pre_fwd(x_m, x_pad, params, tile=256, deps=()):
    t, w = x_m.shape
    r = min(tile, t)

    def body(*refs):
        x_ref, halo_ref = refs[:2]
        p = [ref[...] for ref in refs[2:2 + len(params)]]
        outs = refs[2 + len(params) + len(deps):-1]
        res = _ml_pre(*_delayed(refs[-1], x_ref, halo_ref, r), *p)
        for ref, val in zip(outs, res):
            ref[...] = val

    row = pl.BlockSpec((r, w), lambda i: (i, 0))
    return pl.pallas_call(
        body, name="ml_pre_fwd", grid=(t // r,),
        in_specs=[row, pl.BlockSpec((HALO, w), lambda i: (i * (r // HALO), 0))] + [_full_spec(p.shape) for p in params]
        + [ANY] * len(deps),
        out_specs=[row] * 4 + [pl.BlockSpec((r, LANES), lambda i: (i, 0))],
        out_shape=[jax.ShapeDtypeStruct((t, w), F32)] * 4 + [jax.ShapeDtypeStruct((t, LANES), F32)],
        scratch_shapes=[pltpu.VMEM((r + HALO, w), F32)],
        compiler_params=_cparams("arbitrary"),
    )(x_m, x_pad, *params, *deps)


def _ml_pre_bwd(x_m, x_pad, params, cts, tile=256):
    t, w = x_m.shape
    r = min(tile, t)
    nt = t // r
    n_p = len(params)

    def body(*refs):
        x_ref, halo_ref = refs[:2]
        p = [ref[...] for ref in refs[2:2 + n_p]]
        ct = [ref[...] for ref in refs[2 + n_p:7 + n_p]]
        dx_ref = refs[7 + n_p]
        dp_refs = refs[8 + n_p:8 + 2 * n_p]
        xs_ref, ds_ref, carry_ref = refs[8 + 2 * n_p:]
        step = pl.program_id(0)

        @pl.when(step == 0)
        def _():
            ds_ref[...] = jnp.zeros_like(ds_ref)
            carry_ref[...] = jnp.zeros_like(carry_ref)

        _, vjp = jax.vjp(_ml_pre, *_delayed(xs_ref, x_ref, halo_ref, r), *p)
        grads = vjp(tuple(ct))
        for j in range(CONV_K):
            ds_ref[j, HALO:HALO + r, :] = grads[j]
        lead = HALO + CONV_K - 1
        d_tile = sum(ds_ref[j, pl.ds(lead - j, r), :] for j in range(CONV_K))
        d_halo = sum(ds_ref[j, pl.ds(CONV_K - 1 - j, HALO), :] for j in range(CONV_K))
        dx_ref[...] = jnp.concatenate([d_tile[:r - HALO], d_tile[r - HALO:] + carry_ref[...]], axis=0).astype(dx_ref.dtype)
        carry_ref[...] = d_halo
        _accumulate(step, dp_refs, grads[CONV_K:])

    row = pl.BlockSpec((r, w), lambda i: (nt - 1 - i, 0))
    return pl.pallas_call(
        body, name="ml_pre_bwd", grid=(nt,),
        in_specs=[row, pl.BlockSpec((HALO, w), lambda i: ((nt - 1 - i) * (r // HALO), 0))] + [_full_spec(p.shape) for p in params]
        + [row] * 4 + [pl.BlockSpec((r, LANES), lambda i: (nt - 1 - i, 0))],
        out_specs=[row] + [_full_spec(p.shape) for p in params],
        out_shape=[jax.ShapeDtypeStruct((t, w), BF16)] + [jax.ShapeDtypeStruct(p.shape, F32) for p in params],
        scratch_shapes=[pltpu.VMEM((r + HALO, w), F32), pltpu.VMEM((CONV_K, r + 2 * HALO, w), F32), pltpu.VMEM((HALO, w), F32)],
        compiler_params=_cparams("arbitrary"),
    )(x_m, x_pad, *params, *cts)


def _per_head(fn, row_vals, head_params, shared_params=()):
    return [fn(*[a[:, hs] for a in row_vals], *[p[:, hs] for p in head_params], *shared_params) for hs in _head_slices(HEAD_W)]


def _gla_out(o, g, gn):
    return _rms(o, gn) * (g * _sigmoid(g))


def _ml_out(hc, op, xc, g, sk):
    hcell = hc * _sigmoid(op)
    mu = jnp.mean(hcell, axis=-1, keepdims=True)
    d = hcell - mu
    var = jnp.mean(d * d, axis=-1, keepdims=True)
    return d * lax.rsqrt(var + EPS) * g + sk * xc


def _log_decay(al, w, b):
    return _log_sigmoid(_bdot(al, w, "nn") + b) * (1.0 / GLA_GATE_NORM)


def _merge(ga, gb, ya, yb):
    ga, gb, ya, yb = (a.astype(F32) for a in (ga, gb, ya, yb))
    return _sigmoid(ga) * ya + _sigmoid(gb) * yb


def _post_mix(x, z, gpm, gpl):
    x1 = x + _rms(z, gpm)
    return x1, _rms(x1, gpl)


def _loss_rows(x1, dn, tgt, g):
    e = x1 + _rms(dn, g) - tgt
    return 0.5 * jnp.sum(jnp.mean(e * e, axis=-1, keepdims=True), axis=0, keepdims=True)


def _lin(p):
    return 4 * p[0] + 2 * p[1] + p[2]


def _me():
    return lax.axis_index("x"), lax.axis_index("y"), lax.axis_index("c")


def _flip(p, k):
    return tuple((1 - v) if (k >> (2 - i)) & 1 else v for i, v in enumerate(p))


ANY = pl.BlockSpec(memory_space=pl.ANY)


HBM = pl.BlockSpec(memory_space=pltpu.HBM)
SEM = pl.BlockSpec(memory_space=pltpu.SEMAPHORE)
DATAFLOW = pltpu.SideEffectType.DATAFLOW_SIDE_EFFECTING


SIBLING = 1
OTHER_CHIPS = (2, 4, 6)


def _peer_copies(kinds, srcs, lands, send_sems, recv_sems):
    me = _me()
    copies = []
    for a, (kind, src, land) in enumerate(zip(kinds, srcs, lands)):
        masks = {"gather": range(1, N_DEV), "exchange": range(1, N_DEV), "gather_chips": (SIBLING, *OTHER_CHIPS),
                 "gather_pass": OTHER_CHIPS}[kind]
        for k in masks:
            peer = _flip(me, k)
            if kind == "gather_pass":
                block = land.at[_lin(peer)]
                src_ref, dst_ref, target = block, block, _flip(me, SIBLING)
            else:
                src_ref, dst_ref, target = (src.at[_lin(peer)] if kind == "exchange" else src), land.at[_lin(me)], peer
            copies.append(pltpu.make_async_remote_copy(
                src_ref=src_ref, dst_ref=dst_ref, send_sem=send_sems.at[a * 7 + k - 1], recv_sem=recv_sems.at[a * 7 + k - 1],
                device_id=target, device_id_type=MESH))
    return copies


def _copies_start(kind, srcs, name, after=None, lands=None):
    n = len(srcs)
    extra = [] if after is None else [after]
    kind = [kind] * n if isinstance(kind, str) else list(kind)
    land_shapes = [(s.shape if k == "exchange" else (N_DEV, *s.shape)) for k, s in zip(kind, srcs)]
    lands = [lax.empty(ls, s.dtype) for ls, s in zip(land_shapes, srcs)] if lands is None else lands

    def body(*refs):
        sems = refs[2 * n + len(extra):]
        for cp in _peer_copies(kind, refs[:n], refs[n:2 * n], sems[0], sems[1]):
            cp.start()
        refs[-1][...] = jnp.zeros_like(refs[-1])

    def hbm(a):
        return pltpu.with_memory_space_constraint(a, pltpu.HBM)

    out = pl.pallas_call(
        body, name=name,
        out_shape=(pltpu.SemaphoreType.DMA((7 * n,)), pltpu.SemaphoreType.DMA((7 * n,)),
                   *[pltpu.HBM(s.shape, s.dtype) for s in srcs],
                   *[pltpu.HBM(ls, s.dtype) for ls, s in zip(land_shapes, srcs)],
                   jax.ShapeDtypeStruct((8, LANES), F32)),
        in_specs=[HBM] * (2 * n) + [ANY] * len(extra),
        out_specs=(SEM, SEM, *[HBM] * (2 * n), pl.BlockSpec(memory_space=pltpu.VMEM)),
        input_output_aliases={i: 2 + i for i in range(2 * n)},
        compiler_params=pltpu.CompilerParams(has_side_effects=DATAFLOW),
    )(*[hbm(s) for s in srcs], *[hbm(a) for a in lands], *extra)
    return (kind, n, out[:-1]), out[-1]


def _copies_wait(state, after, name):
    kind, n, (send_sems, recv_sems, *thru) = state
    after = list(after) if isinstance(after, (list, tuple)) else [after]

    def body(*refs):
        for cp in _peer_copies(kind, refs[:n], refs[n:2 * n], refs[2 * n], refs[2 * n + 1]):
            cp.wait_send()
            cp.wait_recv()

    out = pl.pallas_call(
        body, name=name,
        out_shape=tuple(pltpu.HBM(t.shape, t.dtype) for t in thru),
        in_specs=[HBM] * (2 * n) + [SEM, SEM] + [ANY] * len(after), out_specs=tuple([HBM] * (2 * n)),
        input_output_aliases={i: i for i in range(2 * n)},
        compiler_params=pltpu.CompilerParams(has_side_effects=DATAFLOW),
    )(*thru, send_sems, recv_sems, *after)
    return out[:n], out[n:]


def _adamw(w, g, m, v):
    m2 = ADAM_B1 * m + (1.0 - ADAM_B1) * g
    v2 = ADAM_B2 * v + (1.0 - ADAM_B2) * (g * g)
    m_hat = m2 / (1.0 - ADAM_B1 ** ADAM_STEP)
    v_hat = v2 / (1.0 - ADAM_B2 ** ADAM_STEP)
    delta = -ADAM_LR * (m_hat / (jnp.sqrt(v_hat) + ADAM_EPS) + ADAM_WD * w)
    return delta, m2, v2


def _sum_adamw(lands, parts, me_idx, w, m, v, name, tile=256):
    r, c = w.shape
    nchunks = len(lands)
    tr = min(tile, r // nchunks)
    per_chunk = r // nchunks // tr
    per = 1 + N_DEV

    def body(me_ref, *refs):
        w_ref, m_ref, v_ref, g_ref, d_ref, m2_ref, v2_ref = refs[nchunks * per:]
        for k in range(nchunks):
            own_ref, slots = refs[k * per], refs[k * per + 1:(k + 1) * per]

            @pl.when(pl.program_id(0) // per_chunk == k)
            def _(own_ref=own_ref, slots=slots):
                own = own_ref[...].astype(F32)
                g = None
                for s in range(N_DEV):
                    term = jnp.where(me_ref[0] == s, own, slots[s][...].astype(F32))
                    g = term if g is None else g + term
                d, m2, v2 = _adamw(w_ref[...], g, m_ref[...], v_ref[...])
                g_ref[...] = g
                d_ref[...] = d
                m2_ref[...] = m2
                v2_ref[...] = v2

    def chunk_specs(k):
        def tile_of(i):
            return jnp.clip(i - k * per_chunk, 0, per_chunk - 1)

        def slot_spec(s):
            return pl.BlockSpec((None, tr, c), lambda i, me: (jnp.where(me[0] == s, (s + 1) % N_DEV, s), tile_of(i), 0))
        return [pl.BlockSpec((None, tr, c), lambda i, me: (me[0], tile_of(i), 0))] + [slot_spec(s) for s in range(N_DEV)]

    row = pl.BlockSpec((tr, c), lambda i, me: (i, 0))
    operands = [a for land, part in zip(lands, parts) for a in (part, *[land] * N_DEV)]
    return pl.pallas_call(
        body, name=name,
        grid_spec=pltpu.PrefetchScalarGridSpec(
            num_scalar_prefetch=1, grid=(r // tr,),
            in_specs=[s for k in range(nchunks) for s in chunk_specs(k)] + [row] * 3,
            out_specs=[row] * 4),
        out_shape=[jax.ShapeDtypeStruct((r, c), F32)] * 4,
        compiler_params=_cparams("parallel"),
    )(me_idx, *operands, w, m, v)


def _small_update(name, me_idx, kinds, lands, owns, ws, ms, vs, sums=()):
    n = len(ws)
    lands, owns = list(lands) + [s[0] for s in sums], list(owns) + [s[1] for s in sums]
    kinds = list(kinds) + ["gather"] * len(sums)
    nl = len(lands)

    def summed(me, land_ref, own):
        g = None
        for s in range(N_DEV):
            term = jnp.where(me == s, own, land_ref[s]).astype(F32)
            g = term if g is None else g + term
        return g

    def body(me_ref, *refs):
        land_refs, own_refs = refs[:nl], refs[nl:2 * nl]
        w_refs, m_refs, v_refs = (refs[2 * nl + i * n:2 * nl + (i + 1) * n] for i in range(3))
        outs = refs[2 * nl + 3 * n:]
        me = me_ref[0]
        for i in range(n):
            g = summed(me, land_refs[i], own_refs[i][...])
            d, m2, v2 = _adamw(w_refs[i][...], g, m_refs[i][...], v_refs[i][...])
            for ref, val in zip(outs[4 * i:4 * i + 4], (g, d, m2, v2)):
                ref[...] = val
        for i in range(n, nl):
            outs[4 * n + i - n][...] = summed(me, land_refs[i], own_refs[i][...])

    def whole(shape):
        return pl.BlockSpec(shape, lambda i, me, nd=len(shape): (0,) * nd)

    def own_spec(kind, own):
        if kind == "gather":
            return whole(own.shape)
        return pl.BlockSpec((None, *own.shape[1:]), lambda i, me: (me[0], 0, 0))

    shapes = [w.shape for w in ws]
    out_shapes = [s for s in shapes for _ in range(4)] + [s[1].shape for s in sums]
    return pl.pallas_call(
        body, name=name,
        grid_spec=pltpu.PrefetchScalarGridSpec(
            num_scalar_prefetch=1, grid=(1,),
            in_specs=[whole(a.shape) for a in lands] + [own_spec(k, o) for k, o in zip(kinds, owns)]
            + [whole(s) for s in shapes] * 3,
            out_specs=[whole(s) for s in out_shapes]),
        out_shape=[jax.ShapeDtypeStruct(s, F32) for s in out_shapes],
        compiler_params=_cparams("arbitrary"),
    )(me_idx, *lands, *owns, *ws, *ms, *vs)


def _small_view(n, a):
    if a.ndim == 1:
        return a.reshape(1, -1)
    if a.ndim == 3:
        return a.transpose(1, 2, 0).reshape(QKV_BLOCK * QKV_BLOCK, -1)
    return a.T if n == "w_if" else a


def _small_unview(n, a, shape):
    if len(shape) == 1:
        return a.reshape(shape)
    if len(shape) == 3:
        return a.reshape(QKV_BLOCK, QKV_BLOCK, -1).transpose(2, 0, 1)
    return a.T if n == "w_if" else a


def _small_shards(n, g):
    if n == "w_if":
        return g.reshape(N_DEV, -1, g.shape[1]).transpose(0, 2, 1)
    return g.reshape(g.shape[0], N_DEV, -1).transpose(1, 0, 2)


def _small_unshard(n, s):
    if n == "w_if":
        return s.transpose(0, 2, 1).reshape(-1, s.shape[1])
    return s.transpose(1, 0, 2).reshape(s.shape[1], -1)


def _to_hm(a, d):
    t = a.shape[0]
    return a.reshape(t, HEADS, d).transpose(1, 0, 2)


def _from_hm(a):
    h, t, d = a.shape
    return a.transpose(1, 0, 2).reshape(t, h * d)


def _gate_rows(g):
    t = g.shape[0]
    return g.T.reshape(HEADS, t // CHUNK, 1, CHUNK)


def _gate_cols(g):
    h, nc, _, c = g.shape
    return g.reshape(h, nc * c).T


def _blockdiag_dense(w):
    n = w.shape[0] * QKV_BLOCK // 2
    tiled = jnp.tile(w.reshape(2, n, QKV_BLOCK), (1, 1, n // QKV_BLOCK))
    r = lax.broadcasted_iota(jnp.int32, (2, n, n), 1)
    c = lax.broadcasted_iota(jnp.int32, (2, n, n), 2)
    return jnp.where(r // QKV_BLOCK == c // QKV_BLOCK, tiled, 0.0)


def _blockdiag_blocks(dense):
    _, n, _ = dense[0].shape
    k = len(dense)

    def body(*refs):
        r = lax.broadcasted_iota(jnp.int32, (n, n), 0)
        c = lax.broadcasted_iota(jnp.int32, (n, n), 1)
        fr = lax.broadcasted_iota(jnp.int32, (n, LANES), 0)
        fc = lax.broadcasted_iota(jnp.int32, (n, LANES), 1)
        fold = ((fr & (QKV_BLOCK - 1)) == fc).astype(BF16)
        for i in range(k):
            for half in range(2):
                kept = jnp.where((r >> 2) == (c >> 2), refs[i][half], 0.0)
                refs[k + i][half] = sum(lax.dot_general(t, fold, _dims("nn", 2), preferred_element_type=F32)
                                        for t in _split3(kept))

    out = pl.pallas_call(body, name="blockdiag_blocks", out_shape=[jax.ShapeDtypeStruct((2, n, LANES), F32)] * k)(*dense)
    return [o[:, :, 0:QKV_BLOCK].reshape(2 * n // QKV_BLOCK, QKV_BLOCK, QKV_BLOCK) for o in out]


def _col_blocks(w):
    k, n = w.shape
    return w.reshape(k, N_DEV, n // N_DEV).transpose(1, 0, 2)


def _from_col_blocks(g):
    d, k, n = g.shape
    return g.transpose(1, 0, 2).reshape(k, d * n)


def _first_norm(x, g):
    return _rowwise("pre_mix_norm", lambda xv, gv: ((_rms(xv, gv),), ()), [x], [g], [(x.shape[1], BF16)])[0]


def _local_step(x, h, tgt, weight, ws, prefetch, pass_on, on_grads, on_small):
    t, d = x.shape
    g1 = ws["g_pre_mix"]

    def dep(token):
        return () if token is None else (token,)

    w_in = weight("w_in", x)
    fetch_mix = prefetch(("w_pa", "w_pb", "w_o"), w_in)
    fetch_up = prefetch(("w_up", "w_down"), fetch_mix)

    n_in = w_in.shape[2]

    offs = [0]
    for s in IN_SPLITS:
        offs.append(offs[-1] + s)

    w_a_up_p = jnp.pad(ws["w_a_up"], ((0, LANES - LOWRANK), (0, 0)))
    b_a_up = ws["b_a_up"]

    def proj_in_fwd(hv, w, wa, ba):
        proj = jnp.concatenate([_raw_dot(hv, w[j], "nn") for j in range(N_DEV)], axis=1)
        parts = [proj[:, offs[i]:offs[i + 1]] for i in range(len(IN_SPLITS))]
        parts[4] = jnp.concatenate([parts[4], jnp.zeros((parts[4].shape[0], LANES - LOWRANK), F32)], axis=1)
        return (*parts, _log_decay(parts[4], wa, ba)), ()

    widths = [LANES if s == LOWRANK else s for s in IN_SPLITS]
    q_a, k_a, v_a, g_a, a_low_p, x_m, o_pre, gate_a, gate_b, la = _rowwise(
        "proj_in", proj_in_fwd, [h], [w_in, w_a_up_p, b_a_up],
        [(wd, BF16 if i == 2 else F32) for i, wd in enumerate(widths)] + [(HEADS * GLA_DK, F32)],
        deps=dep(fetch_up))

    q_hm, k_hm, la_hm = _to_hm(q_a, GLA_DK), _to_hm(k_a, GLA_DK), _to_hm(la, GLA_DK)
    gn = ws["g_gla_norm"]
    ml_w = HEADS * HEAD_W

    cw = ws["conv_w"]
    w_if_p = jnp.pad(ws["w_if"], ((0, 0), (0, LANES - 2 * HEADS)))
    pre_params = [cw[0:1], cw[1:2], cw[2:3], cw[3:4], ws["conv_b"],
                  _blockdiag_dense(ws["w_q_ml"]), _blockdiag_dense(ws["w_k_ml"]), _blockdiag_dense(ws["w_v_ml"]),
                  w_if_p[0:ml_w], w_if_p[ml_w:2 * ml_w], w_if_p[2 * ml_w:3 * ml_w],
                  jnp.pad(ws["b_if"], ((0, 0), (0, LANES - 2 * HEADS)))]
    x_pad = jnp.pad(x_m, ((HALO, 0), (0, 0)))
    xc, q_m, k_m, v_m, gl = _ml_pre_fwd(x_m, x_pad, pre_params, tile=512)
    pass_mix = pass_on("w_pa", xc)
    li, lf = _gate_rows(gl[:, 0:HEADS]), _gate_rows(gl[:, HEADS:2 * HEADS])
    o_gla, s_prev, hc, c_prev, n_prev, m_prev = _recurrences_fwd(q_hm, k_hm, v_a, la_hm, q_m, k_m, v_m, li, lf,
                                                                 deps=dep(pass_mix))
    pass_up = pass_on("w_up", hc)
    g_ml, skip = ws["g_ml_norm"], ws["ml_skip"]

    def branches_out(o, g, a, b, c_, ga, gb, n_, wa, gm, s, wb):
        ya_in = jnp.concatenate(_per_head(_gla_out, [o, g], [], [n_]), axis=1)
        ya = _raw_dot(ya_in, wa, "nn")
        hb = jnp.concatenate(_per_head(_ml_out, [a, b, c_], [gm, s]), axis=1)
        yb = _raw_dot(hb, wb, "nn")
        return (ya_in, ya, hb, yb, _merge(ga, gb, ya, yb)), ()

    ya_in, y_a, h_b, y_b, merged = _rowwise(
        "branches_out", branches_out, [o_gla, g_a, hc, o_pre, xc, gate_a, gate_b],
        [gn, weight("w_pa", hc), g_ml, skip, weight("w_pb", hc)],
        [(ml_w, BF16), (d, BF16), (ml_w, BF16), (d, BF16), (d, BF16)], tile=512, deps=dep(pass_up))

    gpm, gpl, gpo = ws["g_post_mix"], ws["g_pre_mlp"], ws["g_post_mlp"]

    def proj_o_fwd(mg, xv, w, a, b):
        zv = _raw_dot(mg, w, "nn")
        return (zv, *_post_mix(xv, zv, a, b)), ()

    z, x1, h2 = _rowwise("proj_o", proj_o_fwd, [merged, x], [weight("w_o", merged), gpm, gpl],
                         [(d, F32), (d, F32), (d, BF16)], tile=512)
    w_up = weight("w_up", h2)
    w_down = weight("w_down", h2)
    d_ff = w_down.shape[0]

    def mlp_loss(h2v, x1v, tgtv, wu, wd, g):
        upv = jnp.concatenate([_raw_dot(h2v, wu[j], "nn") for j in range(wu.shape[0])], axis=1)
        uv = jnp.square(jnp.maximum(upv, 0.0))
        dnv = _raw_dot(uv, wd, "nn")
        loss, vjp = jax.vjp(lambda a, b, c_: _loss_rows(a, b, tgtv, c_), x1v, dnv, g)
        dx1, ddn, dg = vjp(jnp.ones((1, 1), F32))
        return (upv, uv, dx1, ddn), (jnp.broadcast_to(loss, (1, LANES)), dg)

    up, u, dx1_y, d_dn, loss, d_gpo = _rowwise("mlp_loss", mlp_loss, [h2, x1, tgt], [w_up, w_down, gpo],
                                               [(d_ff, BF16), (d_ff, BF16), (d, F32), (d, BF16)],
                                               [((1, LANES), F32), ((1, d), F32)])

    dw_down = _mm(u, d_dn, "tn", BF16, "mlp_down_dw", tm=512)

    def mlp_dx(ddn, upv, xv, zv, dx1, wd, wu, a, b):
        dup = (_raw_dot(ddn, wd, "nt") * (2.0 * jnp.maximum(upv.astype(F32), 0.0))).astype(BF16)
        ns = wu.shape[2]
        dh2 = sum(_raw_dot(dup[:, j * ns:(j + 1) * ns], wu[j], "nt") for j in range(wu.shape[0]))
        _, vjp = jax.vjp(_post_mix, xv, zv, a, b)
        dx, dz, da, db = vjp((dx1, dh2))
        return (dup, dx, dz), (da, db)

    d_up, dx_res, d_z, d_gpm, d_gpl = _rowwise("mlp_dx", mlp_dx, [d_dn, up, x, z, dx1_y], [w_down, w_up, gpm, gpl],
                                               [(d_ff, BF16), (d, F32), (d, BF16)], [((1, d), F32), ((1, d), F32)])
    dw_up = _mm_shard_cols(h2, [d_up], [d_up.shape[1]], w_up.shape[2], "mlp_up_dw", 0, d)
    sent_mlp = on_grads(dict(w_down=dw_down, w_up=dw_up))

    def branches_out_bwd(dz, ga, gb, ya, yb, o, g, a, b, c_, wo, wa, n_, wb, gm, s):
        d_ga_, d_gb_, d_ya_, d_yb_ = jax.vjp(_merge, ga, gb, ya, yb)[1](_raw_dot(dz, wo, "nt"))
        ct_a, ct_b = _raw_dot(d_ya_, wa, "nt"), _raw_dot(d_yb_, wb, "nt")
        parts_a, parts_b = [], []
        for hs in _head_slices(HEAD_W):
            parts_a.append(jax.vjp(_gla_out, o[:, hs], g[:, hs], n_)[1](ct_a[:, hs]))
            parts_b.append(jax.vjp(_ml_out, a[:, hs], b[:, hs], c_[:, hs], gm[:, hs], s[:, hs])[1](ct_b[:, hs]))
        cat = lambda parts, i: jnp.concatenate([p[i] for p in parts], axis=1)
        return ((d_ga_, d_gb_, d_ya_, d_yb_, cat(parts_a, 0), cat(parts_a, 1), cat(parts_b, 0), cat(parts_b, 1), cat(parts_b, 2)),
                (sum(p[2] for p in parts_a), cat(parts_b, 3), cat(parts_b, 4)))

    d_ga, d_gb, d_ya, d_yb, d_o, d_g_a, d_hc, d_opre, d_xc, d_gn, d_gml, d_skip = _rowwise(
        "branches_out_bwd", branches_out_bwd, [d_z, gate_a, gate_b, y_a, y_b, o_gla, g_a, hc, o_pre, xc],
        [weight("w_o", merged), weight("w_pa", hc), gn, weight("w_pb", hc), g_ml, skip],
        [(d, BF16)] * 4 + [(ml_w, F32), (ml_w, BF16), (ml_w, F32), (ml_w, BF16), (ml_w, F32)],
        [((1, HEAD_W), F32), ((1, ml_w), F32), ((1, ml_w), F32)], deps=dep(sent_mlp))
    dw_o, dw_pa, dw_pb = _mm_tn_whole([(merged, d_z), (ya_in, d_ya), (h_b, d_yb)], "mix_dw")
    sent_mix = on_grads(dict(w_o=dw_o, w_pa=dw_pa, w_pb=dw_pb))

    dq_hm, dk_hm, d_va, dla_hm, d_qm, d_km, d_vm, d_li, d_lf = _recurrences_bwd(
        q_hm, k_hm, v_a, la_hm, s_prev, d_o, q_m, k_m, v_m, li, lf, c_prev, n_prev, m_prev, d_hc, deps=dep(sent_mix))
    d_gl = jnp.concatenate([_gate_cols(d_li), _gate_cols(d_lf), jnp.zeros((t, LANES - 2 * HEADS), F32)], axis=1)
    pre_grads = _ml_pre_bwd(x_m, x_pad, pre_params, [d_xc, d_qm, d_km, d_vm, d_gl], tile=512)
    d_xm = pre_grads[0]
    d_cw = jnp.concatenate(pre_grads[1:5], axis=0)
    d_cb = pre_grads[5]
    d_wq, d_wk, d_wv = _blockdiag_blocks(pre_grads[6:9])
    d_wif = jnp.concatenate(pre_grads[9:12], axis=0)[:, 0:2 * HEADS]
    d_bif = pre_grads[12][:, 0:2 * HEADS]

    def decay_bwd(al, ct, w, b):
        _, vjp = jax.vjp(_log_decay, al, w, b)
        dal, dw, db = vjp(ct)
        return (dal,), (dw, db)

    d_alow_p, d_wa_p, d_ba = _rowwise("gla_decay_bwd", decay_bwd, [a_low_p, _from_hm(dla_hm)], [w_a_up_p, b_a_up],
                                      [(LANES, BF16)], [(w_a_up_p.shape, F32), (b_a_up.shape, F32)])
    d_proj = [jnp.concatenate([_from_hm(dq_hm), _from_hm(dk_hm), d_va, d_g_a], axis=1), d_alow_p,
              jnp.concatenate([d_xm, d_opre, d_ga, d_gb], axis=1)]
    d_widths = [offs[4], LOWRANK, offs[9] - offs[5]]
    d_pieces = _shard_pieces(d_widths, n_in)
    small = dict(w_a_up=d_wa_p[0:LOWRANK], b_a_up=d_ba, g_gla_norm=d_gn, conv_w=d_cw, conv_b=d_cb,
                 w_q_ml=d_wq, w_k_ml=d_wk, w_v_ml=d_wv, w_if=d_wif, b_if=d_bif, ml_skip=d_skip, g_ml_norm=d_gml,
                 g_post_mix=d_gpm, g_pre_mlp=d_gpl, g_post_mlp=d_gpo)
    sent_small = on_small(small, loss)
    sent_in = sent_small
    for half in range(2):
        dw_half = _mm_shard_cols(h, d_proj, d_widths, n_in, "proj_in_dw_%d" % half, half, d // 2, deps=dep(sent_in))
        sent_in = on_grads({"w_in#%d" % half: dw_half})

    def proj_in_dx(dp_a, dp_low, dp_b, xv, dres, w, g):
        dh = 0.0
        for s in range(N_DEV):
            for i, c_in, c_w, wd in d_pieces[s]:
                src = (dp_a, dp_low, dp_b)[i]
                cols = src.shape[1] - c_in if wd < LANES else wd
                dh = dh + _raw_dot(src[:, c_in:c_in + cols], w[s][:, c_w:c_w + cols], "nt")
        _, vjp = jax.vjp(_rms, xv, g)
        dx, dg = vjp(dh)
        return (dx + dres,), (dg,)

    grad_x, d_g1 = _rowwise("proj_in_dx", proj_in_dx, [*d_proj, x, dx_res], [w_in, g1], [(d, F32)], [((1, d), F32)],
                            deps=dep(sent_in))
    return grad_x, on_small(dict(g_pre_mix=d_g1), None)


BIG = ("w_in", "w_pa", "w_pb", "w_o", "w_up", "w_down")
MIX = ("w_o", "w_pa", "w_pb")
BIG_COL_SHARDED = ("w_in", "w_pa", "w_pb", "w_up")
SMALL_SHARDED = ("w_a_up", "conv_w", "w_if")
SMALL = ("g_pre_mix", "w_a_up", "b_a_up", "g_gla_norm", "conv_w", "conv_b", "w_q_ml", "w_k_ml", "w_v_ml", "w_if", "b_if",
         "ml_skip", "g_ml_norm", "g_post_mix", "g_pre_mlp", "g_post_mlp")
WEIGHTS = ("g_pre_mix", "w_in", "w_a_up", "b_a_up", "g_gla_norm", "conv_w", "conv_b", "w_q_ml", "w_k_ml", "w_v_ml", "w_if", "b_if",
           "ml_skip", "g_ml_norm", "w_pa", "w_pb", "w_o", "g_post_mix", "g_pre_mlp", "w_up", "w_down", "g_post_mlp")


def kernel(x, g_pre_mix, w_in, w_a_up, b_a_up, g_gla_norm, conv_w, conv_b, w_q_ml, w_k_ml, w_v_ml, w_if, b_if, ml_skip, g_ml_norm, w_pa, w_pb, w_o, g_post_mix, g_pre_mlp, w_up, w_down, g_post_mlp, loss_target, m_g_pre_mix, m_w_in, m_w_a_up, m_b_a_up, m_g_gla_norm, m_conv_w, m_conv_b, m_w_q_ml, m_w_k_ml, m_w_v_ml, m_w_if, m_b_if, m_ml_skip, m_g_ml_norm, m_w_pa, m_w_pb, m_w_o, m_g_post_mix, m_g_pre_mlp, m_w_up, m_w_down, m_g_post_mlp, v_g_pre_mix, v_w_in, v_w_a_up, v_b_a_up, v_g_gla_norm, v_conv_w, v_conv_b, v_w_q_ml, v_w_k_ml, v_w_v_ml, v_w_if, v_b_if, v_ml_skip, v_g_ml_norm, v_w_pa, v_w_pb, v_w_o, v_g_post_mix, v_g_pre_mlp, v_w_up, v_w_down, v_g_post_mlp):
    args = dict(locals())
    w = {n: args[n][0] for n in WEIGHTS}
    m = {n: args["m_" + n][0] for n in WEIGHTS}
    v = {n: args["v_" + n][0] for n in WEIGHTS}

    me_lin = _lin(_me())
    me_idx = jnp.reshape(me_lin, (1,)).astype(jnp.int32)

    def full_weight(n, g):
        if n in ("w_in", "w_up"):
            return g
        return _from_col_blocks(g) if n in BIG_COL_SHARDED else g.reshape(-1, g.shape[-1])

    def grad_parts(n, g):
        if n.partition("#")[0] in ("w_in", "w_up"):
            return g
        return (_col_blocks(g) if n in BIG_COL_SHARDED else g.reshape(N_DEV, -1, g.shape[-1])).astype(BF16)

    sharded_names = tuple(SMALL_SHARDED)
    narrow = {n: w[n].astype(BF16) for n in BIG}
    ready, pending, passing = {}, {}, {}
    first_state, _ = _copies_start(["gather"] * len(sharded_names) + ["gather_chips"],
                                   [_small_view(n, w[n]) for n in sharded_names] + [narrow["w_in"]], "allgather_start_first")

    def prefetch(group, after):
        state, token = _copies_start("gather_chips", [narrow[n] for n in group], "allgather_start_" + group[0], after)
        for n in group:
            pending[n] = (group, state)
        return token

    def pass_on(n, after):
        group, state = pending[n]
        shards, lands = _copies_wait(state, after, "allgather_wait_" + group[0])
        state, token = _copies_start("gather_pass", shards, "allgather_pass_" + group[0], lands=lands)
        for gn in group:
            passing[gn] = (group, state)
        return token

    def weight(n, after):
        if n not in ready:
            group, state = passing[n]
            shards, lands = _copies_wait(state, after, "allgather_passed_" + group[0])
            for gn, shard, land in zip(group, shards, lands):
                ready[gn] = full_weight(gn, lax.dynamic_update_slice(land, shard[None], (me_lin, 0, 0)))
        return ready[n]

    h = _first_norm(x[0], w["g_pre_mix"].reshape(1, -1))
    first_own, first_lands = _copies_wait(first_state, [h] + [narrow[n] for n in BIG if n != "w_in"], "allgather_wait_first")
    state, _ = _copies_start("gather_pass", first_own[-1:], "allgather_pass_w_in", lands=first_lands[-1:])
    passing["w_in"] = (("w_in",), state)
    ws = {n: (w[n].reshape(1, -1) if w[n].ndim == 1 else w[n]) for n in SMALL if n not in SMALL_SHARDED}
    for n, own, land in zip(sharded_names, first_own, first_lands):
        ws[n] = _small_unshard(n, lax.dynamic_update_slice(land, own[None], (me_lin, 0, 0)))

    sets, waiting_small = [], []

    def start_set(large, small):
        names = tuple(large)
        s_names, s_kinds, s_srcs = small if small else ((), [], [])
        state, token = _copies_start(s_kinds + ["exchange"] * len(names), s_srcs + [grad_parts(n, large[n]) for n in names],
                                     "exchange_start_" + (names + s_names)[0].replace("#", "_"))
        sets.append((names, s_names, s_kinds, state))
        return token

    def on_grads(grads):
        return start_set(grads, waiting_small.pop() if waiting_small else None)

    def on_small(small, loss):
        names = tuple(small)
        kinds = ["exchange" if n in SMALL_SHARDED else "gather" for n in names]
        srcs = [_small_shards(n, small[n]) if n in SMALL_SHARDED else _small_view(n, small[n]) for n in names]
        if loss is None:
            return start_set({}, (names, kinds, srcs))
        waiting_small.append((names, kinds + ["gather"], srcs + [loss]))
        return None

    grad_x, last_token = _local_step(x[0], h, loss_target[0], weight, ws, prefetch, pass_on, on_grads, on_small)

    out, chunks, sums = {}, {}, []

    def finish_set(names, s_names, s_kinds, state, after):
        own, lands = _copies_wait(state, after, "exchange_wait_" + (names + s_names)[0].replace("#", "_"))
        ns = len(s_kinds)
        if s_names:
            k = len(s_names)
            upd = _small_update("adamw_small_" + s_names[0], me_idx, s_kinds[:k], lands[:k], own[:k],
                                *[[_small_view(n, d[n]) for n in s_names] for d in (w, m, v)],
                                sums=list(zip(lands[k:ns], own[k:ns])))
            for i, n in enumerate(s_names):
                out[n] = tuple(_small_unview(n, a, w[n].shape) for a in upd[4 * i:4 * i + 4])
            sums.extend(upd[4 * k:])
        if names == MIX:
            upd = _small_update("adamw_mix", me_idx, ["exchange"] * len(names), lands[ns:], own[ns:],
                                *[[d[n] for n in names] for d in (w, m, v)])
            for i, n in enumerate(names):
                out[n] = tuple(upd[4 * i:4 * i + 4])
            return
        for name, part, land in zip(names, own[ns:], lands[ns:]):
            n, _, chunk = name.partition("#")
            chunks.setdefault(n, []).append((land, part))
            if chunk in ("", "1"):
                got_lands, got_parts = zip(*chunks[n])
                out[n] = _sum_adamw(got_lands, got_parts, me_idx, w[n], m[n], v[n], "adamw_" + n)

    for entry in sets:
        finish_set(*entry, [grad_x, last_token] + [out[n][1] for n in BIG if n in out])
    loss_sum = sums[0]

    shaped = lambda a, n: a.reshape(args[n].shape)
    return (loss_sum[0, 0], grad_x[None],
            *[shaped(out[n][0], n) for n in WEIGHTS], *[shaped(out[n][1], n) for n in WEIGHTS],
            *[shaped(out[n][2], n) for n in WEIGHTS], *[shaped(out[n][3], n) for n in WEIGHTS])
```

```python
import functools

import jax
import jax.numpy as jnp
from jax import lax
from jax.experimental import pallas as pl
from jax.experimental.pallas import tpu as pltpu

F32 = jnp.float32
BF16 = jnp.bfloat16
MESH = pl.DeviceIdType.MESH

N_DEV = 8
EPS = 1e-6
CHUNK = 64
CHUNKS_PER_STEP = 8
HEADS = 4
GLA_DK = 64
HEAD_W = 128
GLA_GATE_NORM = 16.0
LOWRANK = 16
CONV_K = 4
QKV_BLOCK = 4
LANES = 128
HALO = 8
IN_SPLITS = (256, 256, 512, 512, 16, 512, 512, 1024, 1024)

ADAM_LR = 0.001
ADAM_B1 = 0.9
ADAM_B2 = 0.999
ADAM_EPS = 1e-08
ADAM_WD = 0.01
ADAM_STEP = 10

VMEM_LIMIT = 56 * 1024 * 1024


def _cparams(*sem):
    return pltpu.CompilerParams(dimension_semantics=sem, vmem_limit_bytes=VMEM_LIMIT)


def _dims(mode, ndim):
    contract = {"nn": ((ndim - 1,), (ndim - 2,)), "nt": ((ndim - 1,), (ndim - 1,)), "tn": ((ndim - 2,), (ndim - 2,))}[mode]
    return contract, (((0,), (0,)) if ndim == 3 else ((), ()))


def _raw_dot(a, b, mode):
    return lax.dot_general(a.astype(BF16), b.astype(BF16), _dims(mode, a.ndim), preferred_element_type=F32)


@functools.partial(jax.custom_vjp, nondiff_argnums=(2,))
def _bdot(a, b, mode):
    return _raw_dot(a, b, mode)


def _bdot_fwd(a, b, mode):
    return _raw_dot(a, b, mode), (a, b)


def _bdot_bwd(mode, res, ct):
    a, b = res
    if mode == "nn":
        da, db = _raw_dot(ct, b, "nt"), _raw_dot(a, ct, "tn")
    elif mode == "nt":
        da, db = _raw_dot(ct, b, "nn"), _raw_dot(ct, a, "tn")
    else:
        da, db = _raw_dot(b, ct, "nt"), _raw_dot(a, ct, "nn")
    return da.astype(a.dtype), db.astype(b.dtype)


_bdot.defvjp(_bdot_fwd, _bdot_bwd)


def _split3(x):
    hi = x.astype(BF16)
    r1 = x - hi.astype(F32)
    mid = r1.astype(BF16)
    return hi, mid, (r1 - mid.astype(F32)).astype(BF16)


def _split_dot(tri, x):
    if x.ndim == 3:
        tri = jnp.broadcast_to(tri, (x.shape[0], *tri.shape))
    return sum(lax.dot_general(tri, t, _dims("nn", x.ndim), preferred_element_type=F32) for t in _split3(x))


def _tri(n, lower):
    r = lax.broadcasted_iota(jnp.int32, (n, n), 0)
    c = lax.broadcasted_iota(jnp.int32, (n, n), 1)
    return ((c <= r) if lower else (c >= r)).astype(BF16)


@jax.custom_vjp
def _cumsum_rows(x):
    return _split_dot(_tri(x.shape[-2], True), x)


def _cumsum_rows_fwd(x):
    return _cumsum_rows(x), None


def _cumsum_rows_bwd(_, ct):
    return (_split_dot(_tri(ct.shape[-2], False), ct),)


_cumsum_rows.defvjp(_cumsum_rows_fwd, _cumsum_rows_bwd)


def _abs(x):
    return jnp.where(x >= 0, x, -x)


def _sigmoid(x):
    return lax.logistic(x)


def _log_sigmoid(x):
    return jnp.minimum(x, 0.0) - jnp.log(1.0 + jnp.exp(-_abs(x)))


def _rms(x, g):
    return x * lax.rsqrt(jnp.mean(x * x, axis=-1, keepdims=True) + EPS) * g


def _head_slices(w):
    return [slice(h * w, (h + 1) * w) for h in range(HEADS)]


def _heads(ref, rows=slice(None)):
    return jnp.stack([ref[rows, hs] for hs in _head_slices(HEAD_W)])


def _put_heads(ref, val, rows=slice(None)):
    for h, hs in enumerate(_head_slices(HEAD_W)):
        ref[rows, hs] = val[h].astype(ref.dtype)


def _tile(dim, want):
    if dim <= want or dim % LANES:
        return dim
    t = want
    while dim % t:
        t -= LANES
    return t


def _mm(a, b, mode, out_dtype, name, tm=1024, tn=1024, tk=4096, epilogue=None, extra=(), deps=(), shards=None):
    if shards == "b":
        assert mode == "nn"
        ns = b.shape[2]
        (m, k), (k2, n) = a.shape, (b.shape[1], b.shape[0] * ns)
        tn = ns
    elif mode == "nn":
        (m, k), (k2, n) = a.shape, b.shape
    elif mode == "nt":
        (m, k), (n, k2) = a.shape, b.shape
    else:
        (k, m), (k2, n) = a.shape, b.shape
    assert k == k2, (name, a.shape, b.shape)
    tm, tn, tk = _tile(m, tm), _tile(n, tn), _tile(k, tk)
    nk = k // tk
    out_dtypes = out_dtype if epilogue else (out_dtype,)
    assert nk == 1 or (out_dtype == F32 and not epilogue), name
    n_in = 2 + len(extra)

    def body(*refs):
        p = _raw_dot(refs[0][...], refs[1][...], mode)
        if nk > 1:
            _accumulate(pl.program_id(2), [refs[n_in + len(deps)]], [p])
            return
        outs = epilogue(p, *[r[...] for r in refs[2:n_in]]) if epilogue else (p,)
        for ref, val in zip(refs[n_in + len(deps):], outs):
            ref[...] = val.astype(ref.dtype)

    a_spec = pl.BlockSpec((tk, tm), lambda i, j, kk: (kk, i)) if mode == "tn" else pl.BlockSpec((tm, tk), lambda i, j, kk: (i, kk))
    if shards == "b":
        b_spec = pl.BlockSpec((None, tk, tn), lambda i, j, kk: (j, kk, 0))
    elif mode == "nt":
        b_spec = pl.BlockSpec((tn, tk), lambda i, j, kk: (j, kk))
    else:
        b_spec = pl.BlockSpec((tk, tn), lambda i, j, kk: (kk, j))
    o_spec = pl.BlockSpec((tm, tn), lambda i, j, kk: (i, j))
    res = pl.pallas_call(
        body, name=name, grid=(m // tm, n // tn, nk),
        in_specs=[a_spec, b_spec] + [o_spec] * len(extra) + [ANY] * len(deps), out_specs=[o_spec] * len(out_dtypes),
        out_shape=[jax.ShapeDtypeStruct((m, n), dt) for dt in out_dtypes],
        compiler_params=_cparams("parallel", "parallel", "arbitrary"),
    )(a, b, *extra, *deps)
    return res if epilogue else res[0]


def _mm_tn_whole(pairs, name):
    def body(*refs):
        for i in range(len(pairs)):
            refs[2 * len(pairs) + i][...] = _raw_dot(refs[2 * i][...], refs[2 * i + 1][...], "tn").astype(BF16)

    return pl.pallas_call(
        body, name=name,
        out_shape=[jax.ShapeDtypeStruct((a.shape[1], b.shape[1]), BF16) for a, b in pairs],
        compiler_params=pltpu.CompilerParams(vmem_limit_bytes=VMEM_LIMIT),
    )(*[x for pair in pairs for x in pair])


def _shard_pieces(widths, n):
    bounds = [0]
    for wd in widths:
        bounds.append(bounds[-1] + wd)
    assert bounds[-1] == N_DEV * n
    return [[(i, max(s * n, b) - b, max(s * n, b) - s * n, min((s + 1) * n, b + wd) - max(s * n, b))
             for i, (b, wd) in enumerate(zip(bounds, widths)) if b < (s + 1) * n and b + wd > s * n]
            for s in range(N_DEV)]


def _mm_shard_cols(a, bs, widths, n, name, row_tile, tm, deps=()):
    t = a.shape[0]
    nb = len(bs)
    pieces = _shard_pieces(widths, n)

    def body(a_ref, *rest):
        b_refs = rest[:nb]
        o_ref, at_ref = rest[nb + len(deps):]
        j = pl.program_id(0)

        @pl.when(j == 0)
        def _():
            at_ref[...] = a_ref[...].astype(BF16).T

        for s in range(N_DEV):
            @pl.when(j == s)
            def _(s=s):
                for i, c_in, c_out, wd in pieces[s]:
                    cols = min(_round_up(wd, LANES), bs[i].shape[1] - c_in) if wd < LANES else wd
                    p = _raw_dot(at_ref[...], b_refs[i][:, c_in:c_in + cols], "nn")
                    o_ref[:, c_out:c_out + wd] = p[:, 0:wd].astype(BF16)

    return pl.pallas_call(
        body, name=name, grid=(N_DEV,),
        in_specs=[pl.BlockSpec((t, tm), lambda j: (0, row_tile))]
        + [pl.BlockSpec(b.shape, lambda j: (0, 0), pipeline_mode=pl.Buffered(1)) for b in bs] + [ANY] * len(deps),
        out_specs=pl.BlockSpec((None, tm, n), lambda j: (j, 0, 0)),
        out_shape=jax.ShapeDtypeStruct((N_DEV, tm, n), BF16),
        scratch_shapes=[pltpu.VMEM((tm, t), BF16)],
        compiler_params=_cparams("arbitrary"),
    )(a, *bs, *deps)


def _round_up(v, m):
    return -(-v // m) * m


def _rowwise(name, fn, rows, params, out_rows, out_accs=(), tile=256, deps=()):
    t = rows[0].shape[0]
    r = min(tile, t)
    assert t % r == 0
    n_in, n_or = len(rows) + len(params), len(out_rows)
    n_all = n_in + len(deps)
    params = list(params) + list(deps)

    def body(*refs):
        vals = [ref[...] for ref in refs[:n_in]]
        outs = refs[n_all:]
        ro, ao = fn(*vals)
        for ref, v in zip(outs[:n_or], ro):
            ref[...] = v.astype(ref.dtype)
        if out_accs:
            _accumulate(pl.program_id(0), outs[n_or:], ao)

    def full(shape, **kw):
        return pl.BlockSpec(shape, lambda i, nd=len(shape): (0,) * nd, **kw)

    return pl.pallas_call(
        body, name=name, grid=(t // r,),
        in_specs=[pl.BlockSpec((r, a.shape[1]), lambda i: (i, 0)) for a in rows]
        + [full(p.shape, pipeline_mode=pl.Buffered(1)) for p in params],
        out_specs=[pl.BlockSpec((r, w), lambda i: (i, 0)) for w, _ in out_rows] + [full(s) for s, _ in out_accs],
        out_shape=[jax.ShapeDtypeStruct((t, w), dt) for w, dt in out_rows] + [jax.ShapeDtypeStruct(s, dt) for s, dt in out_accs],
        compiler_params=_cparams("arbitrary"),
    )(*rows, *params)


def _accumulate(step, refs, vals):
    for ref, v in zip(refs, vals):
        @pl.when(step == 0)
        def _(ref=ref, v=v):
            ref[...] = v.astype(ref.dtype)

        @pl.when(step > 0)
        def _(ref=ref, v=v):
            ref[...] += v.astype(ref.dtype)


def _gla_chunk(q, k, v, la, st):
    c = q.shape[-2]
    row = lax.broadcasted_iota(jnp.int32, (c, c), 0)
    col = lax.broadcasted_iota(jnp.int32, (c, c), 1)
    cum = _cumsum_rows(la)
    cl = jnp.sum(la, axis=-2, keepdims=True)
    ep = jnp.exp(cum)
    en = jnp.exp(-cum)
    qs = q * (GLA_DK ** -0.5)
    qp = qs * ep
    a_f = _bdot(qp, k * en, "nt")
    a_b = _bdot(qs * en, k * ep, "nt")
    sc = jnp.where(row >= col, a_f, a_b)
    o = _bdot(sc, v, "nn") + _bdot(qp, st, "nt")
    kd = k * jnp.exp(cl - cum)
    st_new = st * jnp.exp(cl) + _bdot(v, kd, "tn")
    return o, st_new


def _gla_specs(nc, rev):
    nb = nc // CHUNKS_PER_STEP
    rows = CHUNKS_PER_STEP * CHUNK

    def blk(n):
        return (nb - 1 - n) if rev else n
    hm = pl.BlockSpec((HEADS, rows, GLA_DK), lambda n: (0, blk(n), 0))
    tm = pl.BlockSpec((rows, HEADS * HEAD_W), lambda n: (blk(n), 0))
    st = pl.BlockSpec((HEADS, CHUNKS_PER_STEP, HEAD_W, GLA_DK), lambda n: (0, blk(n), 0, 0))
    return nb, hm, tm, st


def _chunk_rows(c):
    return slice(c * CHUNK, (c + 1) * CHUNK)


def _ml_chunk(q, k, v, li_r, lf_r, cm, nv, m):
    c = q.shape[-2]
    row = lax.broadcasted_iota(jnp.int32, (c, c), 0)
    col = lax.broadcasted_iota(jnp.int32, (c, c), 1)
    eye = (row == col).astype(F32)
    li_c = jnp.sum(eye * li_r, axis=-1, keepdims=True)
    lf_c = jnp.sum(eye * lf_r, axis=-1, keepdims=True)
    fc_c = jnp.sum((col <= row).astype(F32) * lf_r, axis=-1, keepdims=True)
    fc_r = jnp.sum((row <= col).astype(F32) * lf_c, axis=-2, keepdims=True)
    f_last = jnp.sum(lf_r, axis=-1, keepdims=True)
    kc = k * (HEAD_W ** -0.5)
    a_c = f_last - fc_c + li_c
    m_loc = jnp.max(a_c, axis=-2, keepdims=True)
    kw = kc * jnp.exp(a_c - m_loc)
    c_chunk = _bdot(kw, v, "tn")
    n_chunk = jnp.sum(kw, axis=-2, keepdims=True)
    m_new = jnp.maximum(f_last + m, m_loc)
    sp = jnp.exp(f_last + m - m_new)
    sl = jnp.exp(m_loc - m_new)
    cm_new = sp * cm + sl * c_chunk
    nv_new = sp * nv + sl * n_chunk
    log_d = li_r - _abs(fc_c - fc_r)
    g_inter = fc_c + m
    m_t = jnp.maximum(g_inter, jnp.max(log_d, axis=-1, keepdims=True))
    s = _bdot(q, kc, "nt") * jnp.exp(log_d - m_t)
    sc = jnp.exp(g_inter - m_t)
    num = _bdot(s, v, "nn") + sc * _bdot(q, cm, "nn")
    den = jnp.sum(s, axis=-1, keepdims=True) + sc * jnp.sum(q * nv, axis=-1, keepdims=True)
    den = jnp.maximum(_abs(den), jnp.exp(-m_t))
    return num / den, cm_new, nv_new, m_new


def _ml_specs(nc, rev):
    nb = nc // CHUNKS_PER_STEP

    def blk(n):
        return (nb - 1 - n) if rev else n
    tm = pl.BlockSpec((CHUNKS_PER_STEP * CHUNK, HEADS * HEAD_W), lambda n: (blk(n), 0))
    gate = pl.BlockSpec((HEADS, CHUNKS_PER_STEP, 1, CHUNK), lambda n: (0, blk(n), 0, 0))
    cm = pl.BlockSpec((HEADS, CHUNKS_PER_STEP, HEAD_W, HEAD_W), lambda n: (0, blk(n), 0, 0))
    vec = pl.BlockSpec((HEADS, CHUNKS_PER_STEP, 1, HEAD_W), lambda n: (0, blk(n), 0, 0))
    return nb, tm, gate, cm, vec


_ML_STATE = [pltpu.VMEM((HEADS, HEAD_W, HEAD_W), F32), pltpu.VMEM((HEADS, 1, HEAD_W), F32), pltpu.VMEM((HEADS, 1, HEAD_W), F32)]


def _recurrences_fwd(q, k, v, la, qm, km, vm, li, lf, deps=()):
    t = v.shape[0]
    nc = t // CHUNK
    nb, hm, tm, st = _gla_specs(nc, False)
    _, _, gate, cm, vec = _ml_specs(nc, False)
    n_in = 9 + len(deps)

    def body(*refs):
        q_ref, k_ref, v_ref, la_ref, qm_ref, km_ref, vm_ref, li_ref, lf_ref = refs[:9]
        o_ref, sp_ref, hc_ref, cp_ref, np_ref, mp_ref, st_ref, c_ref, n_ref, m_ref = refs[n_in:]

        @pl.when(pl.program_id(0) == 0)
        def _():
            for ref in (st_ref, c_ref, n_ref, m_ref):
                ref[...] = jnp.zeros_like(ref)

        s, cs, ns, ms = st_ref[...], c_ref[...], n_ref[...], m_ref[...][:, :, 0:1]
        for c in range(CHUNKS_PER_STEP):
            r = _chunk_rows(c)
            sp_ref[:, c] = s
            o, s = _gla_chunk(q_ref[:, r], k_ref[:, r], _heads(v_ref, r), la_ref[:, r], s)
            _put_heads(o_ref, o, r)
            cp_ref[:, c] = cs
            np_ref[:, c] = ns
            mp_ref[:, c] = jnp.broadcast_to(ms, m_ref.shape)
            hc, cs, ns, ms = _ml_chunk(_heads(qm_ref, r), _heads(km_ref, r), _heads(vm_ref, r), li_ref[:, c], lf_ref[:, c],
                                       cs, ns, ms)
            _put_heads(hc_ref, hc, r)
        st_ref[...] = s
        c_ref[...] = cs
        n_ref[...] = ns
        m_ref[...] = jnp.broadcast_to(ms, m_ref.shape)

    tm_shape = jax.ShapeDtypeStruct((t, HEADS * HEAD_W), F32)
    vec_shape = jax.ShapeDtypeStruct((HEADS, nc, 1, HEAD_W), F32)
    return pl.pallas_call(
        body, name="recurrences_fwd", grid=(nb,),
        in_specs=[hm, hm, tm, hm, tm, tm, tm, gate, gate] + [ANY] * len(deps), out_specs=[tm, st, tm, cm, vec, vec],
        out_shape=[tm_shape, jax.ShapeDtypeStruct((HEADS, nc, HEAD_W, GLA_DK), F32),
                   tm_shape, jax.ShapeDtypeStruct((HEADS, nc, HEAD_W, HEAD_W), F32), vec_shape, vec_shape],
        scratch_shapes=[pltpu.VMEM((HEADS, HEAD_W, GLA_DK), F32)] + _ML_STATE,
        compiler_params=_cparams("arbitrary"),
    )(q, k, v, la, qm, km, vm, li, lf, *deps)


def _recurrences_bwd(q, k, v, la, sp, do, qm, km, vm, li, lf, cp, npv, mp, dhc, deps=()):
    t = v.shape[0]
    nc = t // CHUNK
    nb, hm, tm, st = _gla_specs(nc, True)
    _, _, gate, cm, vec = _ml_specs(nc, True)
    n_in = 15 + len(deps)

    def body(*refs):
        (q_ref, k_ref, v_ref, la_ref, sp_ref, do_ref,
         qm_ref, km_ref, vm_ref, li_ref, lf_ref, cp_ref, np_ref, mp_ref, dhc_ref) = refs[:15]
        (dq_ref, dk_ref, dv_ref, dla_ref, dqm_ref, dkm_ref, dvm_ref, dli_ref, dlf_ref,
         ds_ref, dc_ref, dn_ref, dm_ref) = refs[n_in:]

        @pl.when(pl.program_id(0) == 0)
        def _():
            for ref in (ds_ref, dc_ref, dn_ref, dm_ref):
                ref[...] = jnp.zeros_like(ref)

        ds, dc, dn, dm = ds_ref[...], dc_ref[...], dn_ref[...], dm_ref[...][:, :, 0:1]
        for c in reversed(range(CHUNKS_PER_STEP)):
            r = _chunk_rows(c)
            _, vjp = jax.vjp(_gla_chunk, q_ref[:, r], k_ref[:, r], _heads(v_ref, r), la_ref[:, r], sp_ref[:, c])
            dq, dk, dv, dla, ds = vjp((_heads(do_ref, r), ds))
            dq_ref[:, r] = dq.astype(dq_ref.dtype)
            dk_ref[:, r] = dk.astype(dk_ref.dtype)
            _put_heads(dv_ref, dv, r)
            dla_ref[:, r] = dla
            _, vjp = jax.vjp(_ml_chunk, _heads(qm_ref, r), _heads(km_ref, r), _heads(vm_ref, r), li_ref[:, c], lf_ref[:, c],
                             cp_ref[:, c], np_ref[:, c], mp_ref[:, c][:, :, 0:1])
            dqm, dkm, dvm, dli, dlf, dc, dn, dm = vjp((_heads(dhc_ref, r), dc, dn, dm))
            _put_heads(dqm_ref, dqm, r)
            _put_heads(dkm_ref, dkm, r)
            _put_heads(dvm_ref, dvm, r)
            dli_ref[:, c] = dli
            dlf_ref[:, c] = dlf
        ds_ref[...] = ds
        dc_ref[...] = dc
        dn_ref[...] = dn
        dm_ref[...] = jnp.broadcast_to(dm, dm_ref.shape)

    hm_shape = jax.ShapeDtypeStruct((HEADS, t, GLA_DK), BF16)
    tm_shape = jax.ShapeDtypeStruct((t, HEADS * HEAD_W), F32)
    gate_shape = jax.ShapeDtypeStruct((HEADS, nc, 1, CHUNK), F32)
    return pl.pallas_call(
        body, name="recurrences_bwd", grid=(nb,),
        in_specs=[hm, hm, tm, hm, st, tm, tm, tm, tm, gate, gate, cm, vec, vec, tm] + [ANY] * len(deps),
        out_specs=[hm, hm, tm, hm, tm, tm, tm, gate, gate],
        out_shape=[hm_shape, hm_shape, jax.ShapeDtypeStruct((t, HEADS * HEAD_W), BF16),
                   jax.ShapeDtypeStruct((HEADS, t, GLA_DK), F32), tm_shape, tm_shape, tm_shape, gate_shape, gate_shape],
        scratch_shapes=[pltpu.VMEM((HEADS, HEAD_W, GLA_DK), F32)] + _ML_STATE,
        compiler_params=_cparams("arbitrary"),
    )(q, k, v, la, sp, do, qm, km, vm, li, lf, cp, npv, mp, dhc, *deps)


@jax.custom_vjp
def _bdot_diag(x, w):
    b = w.shape[1]
    return jnp.concatenate([_raw_dot(x[:, :b], w[0], "nn"), _raw_dot(x[:, b:], w[1], "nn")], axis=1)


def _bdot_diag_fwd(x, w):
    return _bdot_diag(x, w), (x, w)


def _bdot_diag_bwd(res, ct):
    x, w = res
    b = w.shape[1]
    dx = jnp.concatenate([_raw_dot(ct[:, :b], w[0], "nt"), _raw_dot(ct[:, b:], w[1], "nt")], axis=1)
    dw = jnp.stack([_raw_dot(x[:, :b], ct[:, :b], "tn"), _raw_dot(x[:, b:], ct[:, b:], "tn")])
    return dx.astype(x.dtype), dw.astype(w.dtype)


_bdot_diag.defvjp(_bdot_diag_fwd, _bdot_diag_bwd)


def _ml_pre(s0, s1, s2, s3, cw0, cw1, cw2, cw3, cb, wq, wk, wv, wiq, wik, wiv, bif):
    pre = cb + cw0 * s0 + cw1 * s1 + cw2 * s2 + cw3 * s3
    xc = pre * _sigmoid(pre)
    q = _bdot_diag(xc, wq)
    k = _bdot_diag(xc, wk)
    v = _bdot_diag(s3, wv)
    gates = _bdot(q, wiq, "nn") + _bdot(k, wik, "nn") + _bdot(v, wiv, "nn") + bif
    lane = lax.broadcasted_iota(jnp.int32, gates.shape, 1)
    gl = jnp.where(lane < HEADS, gates, _log_sigmoid(gates))
    return xc, q, k, v, gl


def _delayed(xs_ref, x_ref, halo_ref, r):
    xs_ref[0:HALO, :] = halo_ref[...]
    xs_ref[HALO:HALO + r, :] = x_ref[...]
    return [xs_ref[pl.ds(HALO - (CONV_K - 1) + j, r), :] for j in range(CONV_K)]


def _full_spec(shape):
    return pl.BlockSpec(shape, lambda i, nd=len(shape): (0,) * nd)


def _ml_pre_fwd(x_m, x_pad, params, tile=256, deps=()):
    t, w = x_m.shape
    r = min(tile, t)

    def body(*refs):
        x_ref, halo_ref = refs[:2]
        p = [ref[...] for ref in refs[2:2 + len(params)]]
        outs = refs[2 + len(params) + len(deps):-1]
        res = _ml_pre(*_delayed(refs[-1], x_ref, halo_ref, r), *p)
        for ref, val in zip(outs, res):
            ref[...] = val

    row = pl.BlockSpec((r, w), lambda i: (i, 0))
    return pl.pallas_call(
        body, name="ml_pre_fwd", grid=(t // r,),
        in_specs=[row, pl.BlockSpec((HALO, w), lambda i: (i * (r // HALO), 0))] + [_full_spec(p.shape) for p in params]
        + [ANY] * len(deps),
        out_specs=[row] * 4 + [pl.BlockSpec((r, LANES), lambda i: (i, 0))],
        out_shape=[jax.ShapeDtypeStruct((t, w), F32)] * 4 + [jax.ShapeDtypeStruct((t, LANES), F32)],
        scratch_shapes=[pltpu.VMEM((r + HALO, w), F32)],
        compiler_params=_cparams("arbitrary"),
    )(x_m, x_pad, *params, *deps)


def _ml_pre_bwd(x_m, x_pad, params, cts, tile=256):
    t, w = x_m.shape
    r = min(tile, t)
    nt = t // r
    n_p = len(params)

    def body(*refs):
        x_ref, halo_ref = refs[:2]
        p = [ref[...] for ref in refs[2:2 + n_p]]
        ct = [ref[...] for ref in refs[2 + n_p:7 + n_p]]
        dx_ref = refs[7 + n_p]
        dp_refs = refs[8 + n_p:8 + 2 * n_p]
        xs_ref, ds_ref, carry_ref = refs[8 + 2 * n_p:]
        step = pl.program_id(0)

        @pl.when(step == 0)
        def _():
            ds_ref[...] = jnp.zeros_like(ds_ref)
            carry_ref[...] = jnp.zeros_like(carry_ref)

        _, vjp = jax.vjp(_ml_pre, *_delayed(xs_ref, x_ref, halo_ref, r), *p)
        grads = vjp(tuple(ct))
        for j in range(CONV_K):
            ds_ref[j, HALO:HALO + r, :] = grads[j]
        lead = HALO + CONV_K - 1
        d_tile = sum(ds_ref[j, pl.ds(lead - j, r), :] for j in range(CONV_K))
        d_halo = sum(ds_ref[j, pl.ds(CONV_K - 1 - j, HALO), :] for j in range(CONV_K))
        dx_ref[...] = jnp.concatenate([d_tile[:r - HALO], d_tile[r - HALO:] + carry_ref[...]], axis=0).astype(dx_ref.dtype)
        carry_ref[...] = d_halo
        _accumulate(step, dp_refs, grads[CONV_K:])

    row = pl.BlockSpec((r, w), lambda i: (nt - 1 - i, 0))
    return pl.pallas_call(
        body, name="ml_pre_bwd", grid=(nt,),
        in_specs=[row, pl.BlockSpec((HALO, w), lambda i: ((nt - 1 - i) * (r // HALO), 0))] + [_full_spec(p.shape) for p in params]
        + [row] * 4 + [pl.BlockSpec((r, LANES), lambda i: (nt - 1 - i, 0))],
        out_specs=[row] + [_full_spec(p.shape) for p in params],
        out_shape=[jax.ShapeDtypeStruct((t, w), BF16)] + [jax.ShapeDtypeStruct(p.shape, F32) for p in params],
        scratch_shapes=[pltpu.VMEM((r + HALO, w), F32), pltpu.VMEM((CONV_K, r + 2 * HALO, w), F32), pltpu.VMEM((HALO, w), F32)],
        compiler_params=_cparams("arbitrary"),
    )(x_m, x_pad, *params, *cts)


def _per_head(fn, row_vals, head_params, shared_params=()):
    return [fn(*[a[:, hs] for a in row_vals], *[p[:, hs] for p in head_params], *shared_params) for hs in _head_slices(HEAD_W)]


def _gla_out(o, g, gn):
    return _rms(o, gn) * (g * _sigmoid(g))


def _ml_out(hc, op, xc, g, sk):
    hcell = hc * _sigmoid(op)
    mu = jnp.mean(hcell, axis=-1, keepdims=True)
    d = hcell - mu
    var = jnp.mean(d * d, axis=-1, keepdims=True)
    return d * lax.rsqrt(var + EPS) * g + sk * xc


def _log_decay(al, w, b):
    return _log_sigmoid(_bdot(al, w, "nn") + b) * (1.0 / GLA_GATE_NORM)


def _merge(ga, gb, ya, yb):
    ga, gb, ya, yb = (a.astype(F32) for a in (ga, gb, ya, yb))
    return _sigmoid(ga) * ya + _sigmoid(gb) * yb


def _post_mix(x, z, gpm, gpl):
    x1 = x + _rms(z, gpm)
    return x1, _rms(x1, gpl)


def _loss_rows(x1, dn, tgt, g):
    e = x1 + _rms(dn, g) - tgt
    return 0.5 * jnp.sum(jnp.mean(e * e, axis=-1, keepdims=True), axis=0, keepdims=True)


def _lin(p):
    return 4 * p[0] + 2 * p[1] + p[2]


def _me():
    return lax.axis_index("x"), lax.axis_index("y"), lax.axis_index("c")


def _flip(p, k):
    return tuple((1 - v) if (k >> (2 - i)) & 1 else v for i, v in enumerate(p))


ANY = pl.BlockSpec(memory_space=pl.ANY)


HBM = pl.BlockSpec(memory_space=pltpu.HBM)
SEM = pl.BlockSpec(memory_space=pltpu.SEMAPHORE)
DATAFLOW = pltpu.SideEffectType.DATAFLOW_SIDE_EFFECTING


SIBLING = 1
OTHER_CHIPS = (2, 4, 6)


def _peer_copies(kinds, srcs, lands, send_sems, recv_sems):
    me = _me()
    copies = []
    for a, (kind, src, land) in enumerate(zip(kinds, srcs, lands)):
        masks = {"gather": range(1, N_DEV), "exchange": range(1, N_DEV), "gather_chips": (SIBLING, *OTHER_CHIPS),
                 "gather_pass": OTHER_CHIPS}[kind]
        for k in masks:
            peer = _flip(me, k)
            if kind == "gather_pass":
                block = land.at[_lin(peer)]
                src_ref, dst_ref, target = block, block, _flip(me, SIBLING)
            else:
                src_ref, dst_ref, target = (src.at[_lin(peer)] if kind == "exchange" else src), land.at[_lin(me)], peer
            copies.append(pltpu.make_async_remote_copy(
                src_ref=src_ref, dst_ref=dst_ref, send_sem=send_sems.at[a * 7 + k - 1], recv_sem=recv_sems.at[a * 7 + k - 1],
                device_id=target, device_id_type=MESH))
    return copies


def _own_copies(kinds, srcs, lands, own_sems):
    return [pltpu.make_async_copy(src, land.at[_lin(_me())], own_sems.at[a])
            for a, (kind, src, land) in enumerate(zip(kinds, srcs, lands)) if kind in ("gather", "gather_chips")]


def _copies_start(kind, srcs, name, after=None, lands=None):
    n = len(srcs)
    extra = [] if after is None else [after]
    kind = [kind] * n if isinstance(kind, str) else list(kind)
    land_shapes = [(s.shape if k == "exchange" else (N_DEV, *s.shape)) for k, s in zip(kind, srcs)]
    lands = [lax.empty(ls, s.dtype) for ls, s in zip(land_shapes, srcs)] if lands is None else lands

    def body(*refs):
        sems = refs[2 * n + len(extra):]
        for cp in _peer_copies(kind, refs[:n], refs[n:2 * n], sems[0], sems[1]) + _own_copies(kind, refs[:n], refs[n:2 * n], sems[2]):
            cp.start()
        refs[-1][...] = jnp.zeros_like(refs[-1])

    def hbm(a):
        return pltpu.with_memory_space_constraint(a, pltpu.HBM)

    out = pl.pallas_call(
        body, name=name,
        out_shape=(pltpu.SemaphoreType.DMA((7 * n,)), pltpu.SemaphoreType.DMA((7 * n,)), pltpu.SemaphoreType.DMA((n,)),
                   *[pltpu.HBM(s.shape, s.dtype) for s in srcs],
                   *[pltpu.HBM(ls, s.dtype) for ls, s in zip(land_shapes, srcs)],
                   jax.ShapeDtypeStruct((8, LANES), F32)),
        in_specs=[HBM] * (2 * n) + [ANY] * len(extra),
        out_specs=(SEM, SEM, SEM, *[HBM] * (2 * n), pl.BlockSpec(memory_space=pltpu.VMEM)),
        input_output_aliases={i: 3 + i for i in range(2 * n)},
        compiler_params=pltpu.CompilerParams(has_side_effects=DATAFLOW),
    )(*[hbm(s) for s in srcs], *[hbm(a) for a in lands], *extra)
    return (kind, n, out[:-1]), out[-1]


def _copies_wait(state, after, name):
    kind, n, (send_sems, recv_sems, own_sems, *thru) = state
    after = list(after) if isinstance(after, (list, tuple)) else [after]

    def body(*refs):
        for cp in _peer_copies(kind, refs[:n], refs[n:2 * n], refs[2 * n], refs[2 * n + 1]):
            cp.wait_send()
            cp.wait_recv()
        for cp in _own_copies(kind, refs[:n], refs[n:2 * n], refs[2 * n + 2]):
            cp.wait()

    out = pl.pallas_call(
        body, name=name,
        out_shape=tuple(pltpu.HBM(t.shape, t.dtype) for t in thru),
        in_specs=[HBM] * (2 * n) + [SEM, SEM, SEM] + [ANY] * len(after), out_specs=tuple([HBM] * (2 * n)),
        input_output_aliases={i: i for i in range(2 * n)},
        compiler_params=pltpu.CompilerParams(has_side_effects=DATAFLOW),
    )(*thru, send_sems, recv_sems, own_sems, *after)
    return out[:n], out[n:]


def _adamw(w, g, m, v):
    m2 = ADAM_B1 * m + (1.0 - ADAM_B1) * g
    v2 = ADAM_B2 * v + (1.0 - ADAM_B2) * (g * g)
    m_hat = m2 / (1.0 - ADAM_B1 ** ADAM_STEP)
    v_hat = v2 / (1.0 - ADAM_B2 ** ADAM_STEP)
    delta = -ADAM_LR * (m_hat / (jnp.sqrt(v_hat) + ADAM_EPS) + ADAM_WD * w)
    return delta, m2, v2


def _sum_adamw(lands, parts, me_idx, w, m, v, name, tile=256):
    r, c = w.shape
    nchunks = len(lands)
    tr = min(tile, r // nchunks)
    per_chunk = r // nchunks // tr
    per = 1 + N_DEV

    def body(me_ref, *refs):
        w_ref, m_ref, v_ref, g_ref, d_ref, m2_ref, v2_ref = refs[nchunks * per:]
        for k in range(nchunks):
            own_ref, slots = refs[k * per], refs[k * per + 1:(k + 1) * per]

            @pl.when(pl.program_id(0) // per_chunk == k)
            def _(own_ref=own_ref, slots=slots):
                own = own_ref[...].astype(F32)
                g = None
                for s in range(N_DEV):
                    term = jnp.where(me_ref[0] == s, own, slots[s][...].astype(F32))
                    g = term if g is None else g + term
                d, m2, v2 = _adamw(w_ref[...], g, m_ref[...], v_ref[...])
                g_ref[...] = g
                d_ref[...] = d
                m2_ref[...] = m2
                v2_ref[...] = v2

    def chunk_specs(k):
        def tile_of(i):
            return jnp.clip(i - k * per_chunk, 0, per_chunk - 1)

        def slot_spec(s):
            return pl.BlockSpec((None, tr, c), lambda i, me: (jnp.where(me[0] == s, (s + 1) % N_DEV, s), tile_of(i), 0))
        return [pl.BlockSpec((None, tr, c), lambda i, me: (me[0], tile_of(i), 0))] + [slot_spec(s) for s in range(N_DEV)]

    row = pl.BlockSpec((tr, c), lambda i, me: (i, 0))
    operands = [a for land, part in zip(lands, parts) for a in (part, *[land] * N_DEV)]
    return pl.pallas_call(
        body, name=name,
        grid_spec=pltpu.PrefetchScalarGridSpec(
            num_scalar_prefetch=1, grid=(r // tr,),
            in_specs=[s for k in range(nchunks) for s in chunk_specs(k)] + [row] * 3,
            out_specs=[row] * 4),
        out_shape=[jax.ShapeDtypeStruct((r, c), F32)] * 4,
        compiler_params=_cparams("parallel"),
    )(me_idx, *operands, w, m, v)


def _small_update(name, me_idx, kinds, lands, owns, ws, ms, vs, sums=()):
    n = len(ws)
    lands, owns = list(lands) + [s[0] for s in sums], list(owns) + [s[1] for s in sums]
    kinds = list(kinds) + ["gather"] * len(sums)
    nl = len(lands)

    def summed(me, land_ref, own):
        g = None
        for s in range(N_DEV):
            term = jnp.where(me == s, own, land_ref[s]).astype(F32)
            g = term if g is None else g + term
        return g

    def body(me_ref, *refs):
        land_refs, own_refs = refs[:nl], refs[nl:2 * nl]
        w_refs, m_refs, v_refs = (refs[2 * nl + i * n:2 * nl + (i + 1) * n] for i in range(3))
        outs = refs[2 * nl + 3 * n:]
        me = me_ref[0]
        for i in range(n):
            g = summed(me, land_refs[i], own_refs[i][...])
            d, m2, v2 = _adamw(w_refs[i][...], g, m_refs[i][...], v_refs[i][...])
            for ref, val in zip(outs[4 * i:4 * i + 4], (g, d, m2, v2)):
                ref[...] = val
        for i in range(n, nl):
            outs[4 * n + i - n][...] = summed(me, land_refs[i], own_refs[i][...])

    def whole(shape):
        return pl.BlockSpec(shape, lambda i, me, nd=len(shape): (0,) * nd)

    def own_spec(kind, own):
        if kind == "gather":
            return whole(own.shape)
        return pl.BlockSpec((None, *own.shape[1:]), lambda i, me: (me[0], 0, 0))

    shapes = [w.shape for w in ws]
    out_shapes = [s for s in shapes for _ in range(4)] + [s[1].shape for s in sums]
    return pl.pallas_call(
        body, name=name,
        grid_spec=pltpu.PrefetchScalarGridSpec(
            num_scalar_prefetch=1, grid=(1,),
            in_specs=[whole(a.shape) for a in lands] + [own_spec(k, o) for k, o in zip(kinds, owns)]
            + [whole(s) for s in shapes] * 3,
            out_specs=[whole(s) for s in out_shapes]),
        out_shape=[jax.ShapeDtypeStruct(s, F32) for s in out_shapes],
        compiler_params=_cparams("arbitrary"),
    )(me_idx, *lands, *owns, *ws, *ms, *vs)


def _small_view(n, a):
    if a.ndim == 1:
        return a.reshape(1, -1)
    if a.ndim == 3:
        return a.transpose(1, 2, 0).reshape(QKV_BLOCK * QKV_BLOCK, -1)
    return a.T if n == "w_if" else a


def _small_unview(n, a, shape):
    if len(shape) == 1:
        return a.reshape(shape)
    if len(shape) == 3:
        return a.reshape(QKV_BLOCK, QKV_BLOCK, -1).transpose(2, 0, 1)
    return a.T if n == "w_if" else a


def _small_shards(n, g):
    if n == "w_if":
        return g.reshape(N_DEV, -1, g.shape[1]).transpose(0, 2, 1)
    return g.reshape(g.shape[0], N_DEV, -1).transpose(1, 0, 2)


def _small_unshard(n, s):
    if n == "w_if":
        return s.transpose(0, 2, 1).reshape(-1, s.shape[1])
    return s.transpose(1, 0, 2).reshape(s.shape[1], -1)


def _to_hm(a, d):
    t = a.shape[0]
    return a.reshape(t, HEADS, d).transpose(1, 0, 2)


def _from_hm(a):
    h, t, d = a.shape
    return a.transpose(1, 0, 2).reshape(t, h * d)


def _gate_rows(g):
    t = g.shape[0]
    return g.T.reshape(HEADS, t // CHUNK, 1, CHUNK)


def _gate_cols(g):
    h, nc, _, c = g.shape
    return g.reshape(h, nc * c).T


def _blockdiag_dense(w):
    n = w.shape[0] * QKV_BLOCK // 2
    tiled = jnp.tile(w.reshape(2, n, QKV_BLOCK), (1, 1, n // QKV_BLOCK))
    r = lax.broadcasted_iota(jnp.int32, (2, n, n), 1)
    c = lax.broadcasted_iota(jnp.int32, (2, n, n), 2)
    return jnp.where(r // QKV_BLOCK == c // QKV_BLOCK, tiled, 0.0)


def _blockdiag_blocks(dense):
    _, n, _ = dense[0].shape
    k = len(dense)

    def body(*refs):
        r = lax.broadcasted_iota(jnp.int32, (n, n), 0)
        c = lax.broadcasted_iota(jnp.int32, (n, n), 1)
        fr = lax.broadcasted_iota(jnp.int32, (n, LANES), 0)
        fc = lax.broadcasted_iota(jnp.int32, (n, LANES), 1)
        fold = ((fr & (QKV_BLOCK - 1)) == fc).astype(BF16)
        for i in range(k):
            for half in range(2):
                kept = jnp.where((r >> 2) == (c >> 2), refs[i][half], 0.0)
                refs[k + i][half] = sum(lax.dot_general(t, fold, _dims("nn", 2), preferred_element_type=F32)
                                        for t in _split3(kept))

    out = pl.pallas_call(body, name="blockdiag_blocks", out_shape=[jax.ShapeDtypeStruct((2, n, LANES), F32)] * k)(*dense)
    return [o[:, :, 0:QKV_BLOCK].reshape(2 * n // QKV_BLOCK, QKV_BLOCK, QKV_BLOCK) for o in out]


def _col_blocks(w):
    k, n = w.shape
    return w.reshape(k, N_DEV, n // N_DEV).transpose(1, 0, 2)


def _from_col_blocks(g):
    d, k, n = g.shape
    return g.transpose(1, 0, 2).reshape(k, d * n)


def _first_norm(x, g):
    return _rowwise("pre_mix_norm", lambda xv, gv: ((_rms(xv, gv),), ()), [x], [g], [(x.shape[1], BF16)])[0]


def _local_step(x, h, tgt, weight, ws, prefetch, pass_on, on_grads, on_small):
    t, d = x.shape
    g1 = ws["g_pre_mix"]

    def dep(token):
        return () if token is None else (token,)

    w_in = weight("w_in", x)
    fetch_mix = prefetch(("w_pa", "w_pb", "w_o"), w_in)
    fetch_up = prefetch(("w_up", "w_down"), fetch_mix)

    n_in = w_in.shape[2]

    offs = [0]
    for s in IN_SPLITS:
        offs.append(offs[-1] + s)

    w_a_up_p = jnp.pad(ws["w_a_up"], ((0, LANES - LOWRANK), (0, 0)))
    b_a_up = ws["b_a_up"]

    def proj_in_fwd(hv, w, wa, ba):
        proj = jnp.concatenate([_raw_dot(hv, w[j], "nn") for j in range(N_DEV)], axis=1)
        parts = [proj[:, offs[i]:offs[i + 1]] for i in range(len(IN_SPLITS))]
        parts[4] = jnp.concatenate([parts[4], jnp.zeros((parts[4].shape[0], LANES - LOWRANK), F32)], axis=1)
        return (*parts, _log_decay(parts[4], wa, ba)), ()

    widths = [LANES if s == LOWRANK else s for s in IN_SPLITS]
    q_a, k_a, v_a, g_a, a_low_p, x_m, o_pre, gate_a, gate_b, la = _rowwise(
        "proj_in", proj_in_fwd, [h], [w_in, w_a_up_p, b_a_up],
        [(wd, BF16 if i == 2 else F32) for i, wd in enumerate(widths)] + [(HEADS * GLA_DK, F32)],
        deps=dep(fetch_up))

    q_hm, k_hm, la_hm = _to_hm(q_a, GLA_DK), _to_hm(k_a, GLA_DK), _to_hm(la, GLA_DK)
    gn = ws["g_gla_norm"]
    ml_w = HEADS * HEAD_W

    cw = ws["conv_w"]
    w_if_p = jnp.pad(ws["w_if"], ((0, 0), (0, LANES - 2 * HEADS)))
    pre_params = [cw[0:1], cw[1:2], cw[2:3], cw[3:4], ws["conv_b"],
                  _blockdiag_dense(ws["w_q_ml"]), _blockdiag_dense(ws["w_k_ml"]), _blockdiag_dense(ws["w_v_ml"]),
                  w_if_p[0:ml_w], w_if_p[ml_w:2 * ml_w], w_if_p[2 * ml_w:3 * ml_w],
                  jnp.pad(ws["b_if"], ((0, 0), (0, LANES - 2 * HEADS)))]
    x_pad = jnp.pad(x_m, ((HALO, 0), (0, 0)))
    xc, q_m, k_m, v_m, gl = _ml_pre_fwd(x_m, x_pad, pre_params, tile=512)
    pass_mix = pass_on("w_pa", xc)
    li, lf = _gate_rows(gl[:, 0:HEADS]), _gate_rows(gl[:, HEADS:2 * HEADS])
    o_gla, s_prev, hc, c_prev, n_prev, m_prev = _recurrences_fwd(q_hm, k_hm, v_a, la_hm, q_m, k_m, v_m, li, lf,
                                                                 deps=dep(pass_mix))
    pass_up = pass_on("w_up", hc)
    g_ml, skip = ws["g_ml_norm"], ws["ml_skip"]

    def branches_out(o, g, a, b, c_, ga, gb, n_, wa, gm, s, wb):
        ya_in = jnp.concatenate(_per_head(_gla_out, [o, g], [], [n_]), axis=1)
        ya = _raw_dot(ya_in, wa, "nn")
        hb = jnp.concatenate(_per_head(_ml_out, [a, b, c_], [gm, s]), axis=1)
        yb = _raw_dot(hb, wb, "nn")
        return (ya_in, ya, hb, yb, _merge(ga, gb, ya, yb)), ()

    ya_in, y_a, h_b, y_b, merged = _rowwise(
        "branches_out", branches_out, [o_gla, g_a, hc, o_pre, xc, gate_a, gate_b],
        [gn, weight("w_pa", hc), g_ml, skip, weight("w_pb", hc)],
        [(ml_w, BF16), (d, BF16), (ml_w, BF16), (d, BF16), (d, BF16)], tile=512, deps=dep(pass_up))

    gpm, gpl, gpo = ws["g_post_mix"], ws["g_pre_mlp"], ws["g_post_mlp"]

    def proj_o_fwd(mg, xv, w, a, b):
        zv = _raw_dot(mg, w, "nn")
        return (zv, *_post_mix(xv, zv, a, b)), ()

    z, x1, h2 = _rowwise("proj_o", proj_o_fwd, [merged, x], [weight("w_o", merged), gpm, gpl],
                         [(d, F32), (d, F32), (d, BF16)], tile=512)
    w_up = weight("w_up", h2)
    w_down = weight("w_down", h2)
    d_ff = w_down.shape[0]

    def mlp_loss(h2v, x1v, tgtv, wu, wd, g):
        upv = jnp.concatenate([_raw_dot(h2v, wu[j], "nn") for j in range(wu.shape[0])], axis=1)
        uv = jnp.square(jnp.maximum(upv, 0.0))
        dnv = _raw_dot(uv, wd, "nn")
        loss, vjp = jax.vjp(lambda a, b, c_: _loss_rows(a, b, tgtv, c_), x1v, dnv, g)
        dx1, ddn, dg = vjp(jnp.ones((1, 1), F32))
        return (upv, uv, dx1, ddn), (jnp.broadcast_to(loss, (1, LANES)), dg)

    up, u, dx1_y, d_dn, loss, d_gpo = _rowwise("mlp_loss", mlp_loss, [h2, x1, tgt], [w_up, w_down, gpo],
                                               [(d_ff, BF16), (d_ff, BF16), (d, F32), (d, BF16)],
                                               [((1, LANES), F32), ((1, d), F32)])

    dw_down = _mm(u, d_dn, "tn", BF16, "mlp_down_dw", tm=512)

    def mlp_dx(ddn, upv, xv, zv, dx1, wd, wu, a, b):
        dup = (_raw_dot(ddn, wd, "nt") * (2.0 * jnp.maximum(upv.astype(F32), 0.0))).astype(BF16)
        ns = wu.shape[2]
        dh2 = sum(_raw_dot(dup[:, j * ns:(j + 1) * ns], wu[j], "nt") for j in range(wu.shape[0]))
        _, vjp = jax.vjp(_post_mix, xv, zv, a, b)
        dx, dz, da, db = vjp((dx1, dh2))
        return (dup, dx, dz), (da, db)

    d_up, dx_res, d_z, d_gpm, d_gpl = _rowwise("mlp_dx", mlp_dx, [d_dn, up, x, z, dx1_y], [w_down, w_up, gpm, gpl],
                                               [(d_ff, BF16), (d, F32), (d, BF16)], [((1, d), F32), ((1, d), F32)])
    dw_up = _mm_shard_cols(h2, [d_up], [d_up.shape[1]], w_up.shape[2], "mlp_up_dw", 0, d)
    sent_mlp = on_grads(dict(w_down=dw_down, w_up=dw_up))

    def branches_out_bwd(dz, ga, gb, ya, yb, o, g, a, b, c_, wo, wa, n_, wb, gm, s):
        d_ga_, d_gb_, d_ya_, d_yb_ = jax.vjp(_merge, ga, gb, ya, yb)[1](_raw_dot(dz, wo, "nt"))
        ct_a, ct_b = _raw_dot(d_ya_, wa, "nt"), _raw_dot(d_yb_, wb, "nt")
        parts_a, parts_b = [], []
        for hs in _head_slices(HEAD_W):
            parts_a.append(jax.vjp(_gla_out, o[:, hs], g[:, hs], n_)[1](ct_a[:, hs]))
            parts_b.append(jax.vjp(_ml_out, a[:, hs], b[:, hs], c_[:, hs], gm[:, hs], s[:, hs])[1](ct_b[:, hs]))
        cat = lambda parts, i: jnp.concatenate([p[i] for p in parts], axis=1)
        return ((d_ga_, d_gb_, d_ya_, d_yb_, cat(parts_a, 0), cat(parts_a, 1), cat(parts_b, 0), cat(parts_b, 1), cat(parts_b, 2)),
                (sum(p[2] for p in parts_a), cat(parts_b, 3), cat(parts_b, 4)))

    d_ga, d_gb, d_ya, d_yb, d_o, d_g_a, d_hc, d_opre, d_xc, d_gn, d_gml, d_skip = _rowwise(
        "branches_out_bwd", branches_out_bwd, [d_z, gate_a, gate_b, y_a, y_b, o_gla, g_a, hc, o_pre, xc],
        [weight("w_o", merged), weight("w_pa", hc), gn, weight("w_pb", hc), g_ml, skip],
        [(d, BF16)] * 4 + [(ml_w, F32), (ml_w, BF16), (ml_w, F32), (ml_w, BF16), (ml_w, F32)],
        [((1, HEAD_W), F32), ((1, ml_w), F32), ((1, ml_w), F32)], deps=dep(sent_mlp))
    dw_o, dw_pa, dw_pb = _mm_tn_whole([(merged, d_z), (ya_in, d_ya), (h_b, d_yb)], "mix_dw")
    sent_mix = on_grads(dict(w_o=dw_o, w_pa=dw_pa, w_pb=dw_pb))

    dq_hm, dk_hm, d_va, dla_hm, d_qm, d_km, d_vm, d_li, d_lf = _recurrences_bwd(
        q_hm, k_hm, v_a, la_hm, s_prev, d_o, q_m, k_m, v_m, li, lf, c_prev, n_prev, m_prev, d_hc, deps=dep(sent_mix))
    d_gl = jnp.concatenate([_gate_cols(d_li), _gate_cols(d_lf), jnp.zeros((t, LANES - 2 * HEADS), F32)], axis=1)
    pre_grads = _ml_pre_bwd(x_m, x_pad, pre_params, [d_xc, d_qm, d_km, d_vm, d_gl], tile=512)
    d_xm = pre_grads[0]
    d_cw = jnp.concatenate(pre_grads[1:5], axis=0)
    d_cb = pre_grads[5]
    d_wq, d_wk, d_wv = _blockdiag_blocks(pre_grads[6:9])
    d_wif = jnp.concatenate(pre_grads[9:12], axis=0)[:, 0:2 * HEADS]
    d_bif = pre_grads[12][:, 0:2 * HEADS]

    def decay_bwd(al, ct, w, b):
        _, vjp = jax.vjp(_log_decay, al, w, b)
        dal, dw, db = vjp(ct)
        return (dal,), (dw, db)

    d_alow_p, d_wa_p, d_ba = _rowwise("gla_decay_bwd", decay_bwd, [a_low_p, _from_hm(dla_hm)], [w_a_up_p, b_a_up],
                                      [(LANES, BF16)], [(w_a_up_p.shape, F32), (b_a_up.shape, F32)])
    d_proj = [jnp.concatenate([_from_hm(dq_hm), _from_hm(dk_hm), d_va, d_g_a], axis=1), d_alow_p,
              jnp.concatenate([d_xm, d_opre, d_ga, d_gb], axis=1)]
    d_widths = [offs[4], LOWRANK, offs[9] - offs[5]]
    d_pieces = _shard_pieces(d_widths, n_in)
    small = dict(w_a_up=d_wa_p[0:LOWRANK], b_a_up=d_ba, g_gla_norm=d_gn, conv_w=d_cw, conv_b=d_cb,
                 w_q_ml=d_wq, w_k_ml=d_wk, w_v_ml=d_wv, w_if=d_wif, b_if=d_bif, ml_skip=d_skip, g_ml_norm=d_gml,
                 g_post_mix=d_gpm, g_pre_mlp=d_gpl, g_post_mlp=d_gpo)
    sent_small = on_small(small, loss)
    sent_in = sent_small
    for half in range(2):
        dw_half = _mm_shard_cols(h, d_proj, d_widths, n_in, "proj_in_dw_%d" % half, half, d // 2, deps=dep(sent_in))
        sent_in = on_grads({"w_in#%d" % half: dw_half})

    def proj_in_dx(dp_a, dp_low, dp_b, xv, dres, w, g):
        dh = 0.0
        for s in range(N_DEV):
            for i, c_in, c_w, wd in d_pieces[s]:
                src = (dp_a, dp_low, dp_b)[i]
                cols = src.shape[1] - c_in if wd < LANES else wd
                dh = dh + _raw_dot(src[:, c_in:c_in + cols], w[s][:, c_w:c_w + cols], "nt")
        _, vjp = jax.vjp(_rms, xv, g)
        dx, dg = vjp(dh)
        return (dx + dres,), (dg,)

    grad_x, d_g1 = _rowwise("proj_in_dx", proj_in_dx, [*d_proj, x, dx_res], [w_in, g1], [(d, F32)], [((1, d), F32)],
                            deps=dep(sent_in))
    return grad_x, on_small(dict(g_pre_mix=d_g1), None)


BIG = ("w_in", "w_pa", "w_pb", "w_o", "w_up", "w_down")
MIX = ("w_o", "w_pa", "w_pb")
BIG_COL_SHARDED = ("w_in", "w_pa", "w_pb", "w_up")
SMALL_SHARDED = ("w_a_up", "conv_w", "w_if")
SMALL = ("g_pre_mix", "w_a_up", "b_a_up", "g_gla_norm", "conv_w", "conv_b", "w_q_ml", "w_k_ml", "w_v_ml", "w_if", "b_if",
         "ml_skip", "g_ml_norm", "g_post_mix", "g_pre_mlp", "g_post_mlp")
WEIGHTS = ("g_pre_mix", "w_in", "w_a_up", "b_a_up", "g_gla_norm", "conv_w", "conv_b", "w_q_ml", "w_k_ml", "w_v_ml", "w_if", "b_if",
           "ml_skip", "g_ml_norm", "w_pa", "w_pb", "w_o", "g_post_mix", "g_pre_mlp", "w_up", "w_down", "g_post_mlp")


def kernel(x, g_pre_mix, w_in, w_a_up, b_a_up, g_gla_norm, conv_w, conv_b, w_q_ml, w_k_ml, w_v_ml, w_if, b_if, ml_skip, g_ml_norm, w_pa, w_pb, w_o, g_post_mix, g_pre_mlp, w_up, w_down, g_post_mlp, loss_target, m_g_pre_mix, m_w_in, m_w_a_up, m_b_a_up, m_g_gla_norm, m_conv_w, m_conv_b, m_w_q_ml, m_w_k_ml, m_w_v_ml, m_w_if, m_b_if, m_ml_skip, m_g_ml_norm, m_w_pa, m_w_pb, m_w_o, m_g_post_mix, m_g_pre_mlp, m_w_up, m_w_down, m_g_post_mlp, v_g_pre_mix, v_w_in, v_w_a_up, v_b_a_up, v_g_gla_norm, v_conv_w, v_conv_b, v_w_q_ml, v_w_k_ml, v_w_v_ml, v_w_if, v_b_if, v_ml_skip, v_g_ml_norm, v_w_pa, v_w_pb, v_w_o, v_g_post_mix, v_g_pre_mlp, v_w_up, v_w_down, v_g_post_mlp):
    args = dict(locals())
    w = {n: args[n][0] for n in WEIGHTS}
    m = {n: args["m_" + n][0] for n in WEIGHTS}
    v = {n: args["v_" + n][0] for n in WEIGHTS}

    me_lin = _lin(_me())
    me_idx = jnp.reshape(me_lin, (1,)).astype(jnp.int32)

    def full_weight(n, g):
        if n in ("w_in", "w_up"):
            return g
        return _from_col_blocks(g) if n in BIG_COL_SHARDED else g.reshape(-1, g.shape[-1])

    def grad_parts(n, g):
        if n.partition("#")[0] in ("w_in", "w_up"):
            return g
        return (_col_blocks(g) if n in BIG_COL_SHARDED else g.reshape(N_DEV, -1, g.shape[-1])).astype(BF16)

    sharded_names = tuple(SMALL_SHARDED)
    narrow = {n: w[n].astype(BF16) for n in BIG}
    ready, pending, passing = {}, {}, {}
    first_state, _ = _copies_start(["gather"] * len(sharded_names) + ["gather_chips"],
                                   [_small_view(n, w[n]) for n in sharded_names] + [narrow["w_in"]], "allgather_start_first")

    def prefetch(group, after):
        state, token = _copies_start("gather_chips", [narrow[n] for n in group], "allgather_start_" + group[0], after)
        for n in group:
            pending[n] = (group, state)
        return token

    def pass_on(n, after):
        group, state = pending[n]
        shards, lands = _copies_wait(state, after, "allgather_wait_" + group[0])
        state, token = _copies_start("gather_pass", shards, "allgather_pass_" + group[0], lands=lands)
        for gn in group:
            passing[gn] = (group, state)
        return token

    def weight(n, after):
        if n not in ready:
            group, state = passing[n]
            _, lands = _copies_wait(state, after, "allgather_passed_" + group[0])
            for gn, land in zip(group, lands):
                ready[gn] = full_weight(gn, land)
        return ready[n]

    h = _first_norm(x[0], w["g_pre_mix"].reshape(1, -1))
    first_own, first_lands = _copies_wait(first_state, [h] + [narrow[n] for n in BIG if n != "w_in"], "allgather_wait_first")
    state, _ = _copies_start("gather_pass", first_own[-1:], "allgather_pass_w_in", lands=first_lands[-1:])
    passing["w_in"] = (("w_in",), state)
    ws = {n: (w[n].reshape(1, -1) if w[n].ndim == 1 else w[n]) for n in SMALL if n not in SMALL_SHARDED}
    for n, land in zip(sharded_names, first_lands):
        ws[n] = _small_unshard(n, land)

    sets, waiting_small = [], []

    def start_set(large, small):
        names = tuple(large)
        s_names, s_kinds, s_srcs = small if small else ((), [], [])
        state, token = _copies_start(s_kinds + ["exchange"] * len(names), s_srcs + [grad_parts(n, large[n]) for n in names],
                                     "exchange_start_" + (names + s_names)[0].replace("#", "_"))
        sets.append((names, s_names, s_kinds, state))
        return token

    def on_grads(grads):
        return start_set(grads, waiting_small.pop() if waiting_small else None)

    def on_small(small, loss):
        names = tuple(small)
        kinds = ["exchange" if n in SMALL_SHARDED else "gather" for n in names]
        srcs = [_small_shards(n, small[n]) if n in SMALL_SHARDED else _small_view(n, small[n]) for n in names]
        if loss is None:
            return start_set({}, (names, kinds, srcs))
        waiting_small.append((names, kinds + ["gather"], srcs + [loss]))
        return None

    grad_x, last_token = _local_step(x[0], h, loss_target[0], weight, ws, prefetch, pass_on, on_grads, on_small)

    out, chunks, sums = {}, {}, []

    def finish_set(names, s_names, s_kinds, state, after):
        own, lands = _copies_wait(state, after, "exchange_wait_" + (names + s_names)[0].replace("#", "_"))
        ns = len(s_kinds)
        if s_names:
            k = len(s_names)
            upd = _small_update("adamw_small_" + s_names[0], me_idx, s_kinds[:k], lands[:k], own[:k],
                                *[[_small_view(n, d[n]) for n in s_names] for d in (w, m, v)],
                                sums=list(zip(lands[k:ns], own[k:ns])))
            for i, n in enumerate(s_names):
                out[n] = tuple(_small_unview(n, a, w[n].shape) for a in upd[4 * i:4 * i + 4])
            sums.extend(upd[4 * k:])
        if names == MIX:
            upd = _small_update("adamw_mix", me_idx, ["exchange"] * len(names), lands[ns:], own[ns:],
                                *[[d[n] for n in names] for d in (w, m, v)])
            for i, n in enumerate(names):
                out[n] = tuple(upd[4 * i:4 * i + 4])
            return
        for name, part, land in zip(names, own[ns:], lands[ns:]):
            n, _, chunk = name.partition("#")
            chunks.setdefault(n, []).append((land, part))
            if chunk in ("", "1"):
                got_lands, got_parts = zip(*chunks[n])
                out[n] = _sum_adamw(got_lands, got_parts, me_idx, w[n], m[n], v[n], "adamw_" + n)

    for entry in sets:
        finish_set(*entry, [grad_x, last_token] + [out[n][1] for n in BIG if n in out])
    loss_sum = sums[0]

    shaped = lambda a, n: a.reshape(args[n].shape)
    return (loss_sum[0, 0], grad_x[None],
            *[shaped(out[n][0], n) for n in WEIGHTS], *[shaped(out[n][1], n) for n in WEIGHTS],
            *[shaped(out[n][2], n) for n in WEIGHTS], *[shaped(out[n][3], n) for n in WEIGHTS])
```

```python
import functools

import jax
import jax.numpy as jnp
from jax import lax
from jax.experimental import pallas as pl
from jax.experimental.pallas import tpu as pltpu

F32 = jnp.float32
BF16 = jnp.bfloat16
MESH = pl.DeviceIdType.MESH

N_DEV = 8
EPS = 1e-6
CHUNK = 64
CHUNKS_PER_STEP = 8
HEADS = 4
GLA_DK = 64
HEAD_W = 128
GLA_GATE_NORM = 16.0
LOWRANK = 16
CONV_K = 4
QKV_BLOCK = 4
LANES = 128
HALO = 8
IN_SPLITS = (256, 256, 512, 512, 16, 512, 512, 1024, 1024)

ADAM_LR = 0.001
ADAM_B1 = 0.9
ADAM_B2 = 0.999
ADAM_EPS = 1e-08
ADAM_WD = 0.01
ADAM_STEP = 10

VMEM_LIMIT = 56 * 1024 * 1024


def _cparams(*sem):
    return pltpu.CompilerParams(dimension_semantics=sem, vmem_limit_bytes=VMEM_LIMIT)


def _dims(mode, ndim):
    contract = {"nn": ((ndim - 1,), (ndim - 2,)), "nt": ((ndim - 1,), (ndim - 1,)), "tn": ((ndim - 2,), (ndim - 2,))}[mode]
    return contract, (((0,), (0,)) if ndim == 3 else ((), ()))


def _raw_dot(a, b, mode):
    return lax.dot_general(a.astype(BF16), b.astype(BF16), _dims(mode, a.ndim), preferred_element_type=F32)


@functools.partial(jax.custom_vjp, nondiff_argnums=(2,))
def _bdot(a, b, mode):
    return _raw_dot(a, b, mode)


def _bdot_fwd(a, b, mode):
    return _raw_dot(a, b, mode), (a, b)


def _bdot_bwd(mode, res, ct):
    a, b = res
    if mode == "nn":
        da, db = _raw_dot(ct, b, "nt"), _raw_dot(a, ct, "tn")
    elif mode == "nt":
        da, db = _raw_dot(ct, b, "nn"), _raw_dot(ct, a, "tn")
    else:
        da, db = _raw_dot(b, ct, "nt"), _raw_dot(a, ct, "nn")
    return da.astype(a.dtype), db.astype(b.dtype)


_bdot.defvjp(_bdot_fwd, _bdot_bwd)


def _split3(x):
    hi = x.astype(BF16)
    r1 = x - hi.astype(F32)
    mid = r1.astype(BF16)
    return hi, mid, (r1 - mid.astype(F32)).astype(BF16)


def _split_dot(tri, x):
    if x.ndim == 3:
        tri = jnp.broadcast_to(tri, (x.shape[0], *tri.shape))
    return sum(lax.dot_general(tri, t, _dims("nn", x.ndim), preferred_element_type=F32) for t in _split3(x))


def _tri(n, lower):
    r = lax.broadcasted_iota(jnp.int32, (n, n), 0)
    c = lax.broadcasted_iota(jnp.int32, (n, n), 1)
    return ((c <= r) if lower else (c >= r)).astype(BF16)


@jax.custom_vjp
def _cumsum_rows(x):
    return _split_dot(_tri(x.shape[-2], True), x)


def _cumsum_rows_fwd(x):
    return _cumsum_rows(x), None


def _cumsum_rows_bwd(_, ct):
    return (_split_dot(_tri(ct.shape[-2], False), ct),)


_cumsum_rows.defvjp(_cumsum_rows_fwd, _cumsum_rows_bwd)


def _abs(x):
    return jnp.where(x >= 0, x, -x)


def _sigmoid(x):
    return lax.logistic(x)


def _log_sigmoid(x):
    return jnp.minimum(x, 0.0) - jnp.log(1.0 + jnp.exp(-_abs(x)))


def _rms(x, g):
    return x * lax.rsqrt(jnp.mean(x * x, axis=-1, keepdims=True) + EPS) * g


def _head_slices(w):
    return [slice(h * w, (h + 1) * w) for h in range(HEADS)]


def _heads(ref, rows=slice(None)):
    return jnp.stack([ref[rows, hs] for hs in _head_slices(HEAD_W)])


def _put_heads(ref, val, rows=slice(None)):
    for h, hs in enumerate(_head_slices(HEAD_W)):
        ref[rows, hs] = val[h].astype(ref.dtype)


def _tile(dim, want):
    if dim <= want or dim % LANES:
        return dim
    t = want
    while dim % t:
        t -= LANES
    return t


def _mm(a, b, mode, out_dtype, name, tm=1024, tn=1024, tk=4096, epilogue=None, extra=(), deps=(), shards=None):
    if shards == "b":
        assert mode == "nn"
        ns = b.shape[2]
        (m, k), (k2, n) = a.shape, (b.shape[1], b.shape[0] * ns)
        tn = ns
    elif mode == "nn":
        (m, k), (k2, n) = a.shape, b.shape
    elif mode == "nt":
        (m, k), (n, k2) = a.shape, b.shape
    else:
        (k, m), (k2, n) = a.shape, b.shape
    assert k == k2, (name, a.shape, b.shape)
    tm, tn, tk = _tile(m, tm), _tile(n, tn), _tile(k, tk)
    nk = k // tk
    out_dtypes = out_dtype if epilogue else (out_dtype,)
    assert nk == 1 or (out_dtype == F32 and not epilogue), name
    n_in = 2 + len(extra)

    def body(*refs):
        p = _raw_dot(refs[0][...], refs[1][...], mode)
        if nk > 1:
            _accumulate(pl.program_id(2), [refs[n_in + len(deps)]], [p])
            return
        outs = epilogue(p, *[r[...] for r in refs[2:n_in]]) if epilogue else (p,)
        for ref, val in zip(refs[n_in + len(deps):], outs):
            ref[...] = val.astype(ref.dtype)

    a_spec = pl.BlockSpec((tk, tm), lambda i, j, kk: (kk, i)) if mode == "tn" else pl.BlockSpec((tm, tk), lambda i, j, kk: (i, kk))
    if shards == "b":
        b_spec = pl.BlockSpec((None, tk, tn), lambda i, j, kk: (j, kk, 0))
    elif mode == "nt":
        b_spec = pl.BlockSpec((tn, tk), lambda i, j, kk: (j, kk))
    else:
        b_spec = pl.BlockSpec((tk, tn), lambda i, j, kk: (kk, j))
    o_spec = pl.BlockSpec((tm, tn), lambda i, j, kk: (i, j))
    res = pl.pallas_call(
        body, name=name, grid=(m // tm, n // tn, nk),
        in_specs=[a_spec, b_spec] + [o_spec] * len(extra) + [ANY] * len(deps), out_specs=[o_spec] * len(out_dtypes),
        out_shape=[jax.ShapeDtypeStruct((m, n), dt) for dt in out_dtypes],
        compiler_params=_cparams("parallel", "parallel", "arbitrary"),
    )(a, b, *extra, *deps)
    return res if epilogue else res[0]


def _mm_tn_whole(pairs, name):
    def body(*refs):
        for i in range(len(pairs)):
            refs[2 * len(pairs) + i][...] = _raw_dot(refs[2 * i][...], refs[2 * i + 1][...], "tn").astype(BF16)

    return pl.pallas_call(
        body, name=name,
        out_shape=[jax.ShapeDtypeStruct((a.shape[1], b.shape[1]), BF16) for a, b in pairs],
        compiler_params=pltpu.CompilerParams(vmem_limit_bytes=VMEM_LIMIT),
    )(*[x for pair in pairs for x in pair])


def _shard_pieces(widths, n):
    bounds = [0]
    for wd in widths:
        bounds.append(bounds[-1] + wd)
    assert bounds[-1] == N_DEV * n
    return [[(i, max(s * n, b) - b, max(s * n, b) - s * n, min((s + 1) * n, b + wd) - max(s * n, b))
             for i, (b, wd) in enumerate(zip(bounds, widths)) if b < (s + 1) * n and b + wd > s * n]
            for s in range(N_DEV)]


def _mm_shard_cols(a, bs, widths, n, name, row_tile, tm, deps=()):
    t = a.shape[0]
    nb = len(bs)
    pieces = _shard_pieces(widths, n)

    def body(a_ref, *rest):
        b_refs = rest[:nb]
        o_ref, at_ref = rest[nb + len(deps):]
        j = pl.program_id(0)

        @pl.when(j == 0)
        def _():
            at_ref[...] = a_ref[...].astype(BF16).T

        for s in range(N_DEV):
            @pl.when(j == s)
            def _(s=s):
                for i, c_in, c_out, wd in pieces[s]:
                    cols = min(_round_up(wd, LANES), bs[i].shape[1] - c_in) if wd < LANES else wd
                    p = _raw_dot(at_ref[...], b_refs[i][:, c_in:c_in + cols], "nn")
                    o_ref[:, c_out:c_out + wd] = p[:, 0:wd].astype(BF16)

    return pl.pallas_call(
        body, name=name, grid=(N_DEV,),
        in_specs=[pl.BlockSpec((t, tm), lambda j: (0, row_tile))]
        + [pl.BlockSpec(b.shape, lambda j: (0, 0), pipeline_mode=pl.Buffered(1)) for b in bs] + [ANY] * len(deps),
        out_specs=pl.BlockSpec((None, tm, n), lambda j: (j, 0, 0)),
        out_shape=jax.ShapeDtypeStruct((N_DEV, tm, n), BF16),
        scratch_shapes=[pltpu.VMEM((tm, t), BF16)],
        compiler_params=_cparams("arbitrary"),
    )(a, *bs, *deps)


def _round_up(v, m):
    return -(-v // m) * m


def _rowwise(name, fn, rows, params, out_rows, out_accs=(), tile=256, deps=()):
    t = rows[0].shape[0]
    r = min(tile, t)
    assert t % r == 0
    n_in, n_or = len(rows) + len(params), len(out_rows)
    n_all = n_in + len(deps)
    params = list(params) + list(deps)

    def body(*refs):
        vals = [ref[...] for ref in refs[:n_in]]
        outs = refs[n_all:]
        ro, ao = fn(*vals)
        for ref, v in zip(outs[:n_or], ro):
            ref[...] = v.astype(ref.dtype)
        if out_accs:
            _accumulate(pl.program_id(0), outs[n_or:], ao)

    def full(shape, **kw):
        return pl.BlockSpec(shape, lambda i, nd=len(shape): (0,) * nd, **kw)

    return pl.pallas_call(
        body, name=name, grid=(t // r,),
        in_specs=[pl.BlockSpec((r, a.shape[1]), lambda i: (i, 0)) for a in rows]
        + [full(p.shape, pipeline_mode=pl.Buffered(1)) for p in params],
        out_specs=[pl.BlockSpec((r, w), lambda i: (i, 0)) for w, _ in out_rows] + [full(s) for s, _ in out_accs],
        out_shape=[jax.ShapeDtypeStruct((t, w), dt) for w, dt in out_rows] + [jax.ShapeDtypeStruct(s, dt) for s, dt in out_accs],
        compiler_params=_cparams("arbitrary"),
    )(*rows, *params)


def _accumulate(step, refs, vals):
    for ref, v in zip(refs, vals):
        @pl.when(step == 0)
        def _(ref=ref, v=v):
            ref[...] = v.astype(ref.dtype)

        @pl.when(step > 0)
        def _(ref=ref, v=v):
            ref[...] += v.astype(ref.dtype)


def _gla_chunk(q, k, v, la, st):
    c = q.shape[-2]
    row = lax.broadcasted_iota(jnp.int32, (c, c), 0)
    col = lax.broadcasted_iota(jnp.int32, (c, c), 1)
    cum = _cumsum_rows(la)
    cl = jnp.sum(la, axis=-2, keepdims=True)
    ep = jnp.exp(cum)
    en = jnp.exp(-cum)
    qs = q * (GLA_DK ** -0.5)
    qp = qs * ep
    a_f = _bdot(qp, k * en, "nt")
    a_b = _bdot(qs * en, k * ep, "nt")
    sc = jnp.where(row >= col, a_f, a_b)
    o = _bdot(sc, v, "nn") + _bdot(qp, st, "nt")
    kd = k * jnp.exp(cl - cum)
    st_new = st * jnp.exp(cl) + _bdot(v, kd, "tn")
    return o, st_new


def _gla_specs(nc, rev):
    nb = nc // CHUNKS_PER_STEP
    rows = CHUNKS_PER_STEP * CHUNK

    def blk(n):
        return (nb - 1 - n) if rev else n
    hm = pl.BlockSpec((HEADS, rows, GLA_DK), lambda n: (0, blk(n), 0))
    tm = pl.BlockSpec((rows, HEADS * HEAD_W), lambda n: (blk(n), 0))
    st = pl.BlockSpec((HEADS, CHUNKS_PER_STEP, HEAD_W, GLA_DK), lambda n: (0, blk(n), 0, 0))
    return nb, hm, tm, st


def _chunk_rows(c):
    return slice(c * CHUNK, (c + 1) * CHUNK)


def _ml_chunk(q, k, v, li_r, lf_r, cm, nv, m):
    c = q.shape[-2]
    row = lax.broadcasted_iota(jnp.int32, (c, c), 0)
    col = lax.broadcasted_iota(jnp.int32, (c, c), 1)
    eye = (row == col).astype(F32)
    li_c = jnp.sum(eye * li_r, axis=-1, keepdims=True)
    lf_c = jnp.sum(eye * lf_r, axis=-1, keepdims=True)
    fc_c = jnp.sum((col <= row).astype(F32) * lf_r, axis=-1, keepdims=True)
    fc_r = jnp.sum((row <= col).astype(F32) * lf_c, axis=-2, keepdims=True)
    f_last = jnp.sum(lf_r, axis=-1, keepdims=True)
    kc = k * (HEAD_W ** -0.5)
    a_c = f_last - fc_c + li_c
    m_loc = jnp.max(a_c, axis=-2, keepdims=True)
    kw = kc * jnp.exp(a_c - m_loc)
    c_chunk = _bdot(kw, v, "tn")
    n_chunk = jnp.sum(kw, axis=-2, keepdims=True)
    m_new = jnp.maximum(f_last + m, m_loc)
    sp = jnp.exp(f_last + m - m_new)
    sl = jnp.exp(m_loc - m_new)
    cm_new = sp * cm + sl * c_chunk
    nv_new = sp * nv + sl * n_chunk
    log_d = li_r - _abs(fc_c - fc_r)
    g_inter = fc_c + m
    m_t = jnp.maximum(g_inter, jnp.max(log_d, axis=-1, keepdims=True))
    s = _bdot(q, kc, "nt") * jnp.exp(log_d - m_t)
    sc = jnp.exp(g_inter - m_t)
    num = _bdot(s, v, "nn") + sc * _bdot(q, cm, "nn")
    den = jnp.sum(s, axis=-1, keepdims=True) + sc * jnp.sum(q * nv, axis=-1, keepdims=True)
    den = jnp.maximum(_abs(den), jnp.exp(-m_t))
    return num / den, cm_new, nv_new, m_new


def _ml_specs(nc, rev):
    nb = nc // CHUNKS_PER_STEP

    def blk(n):
        return (nb - 1 - n) if rev else n
    tm = pl.BlockSpec((CHUNKS_PER_STEP * CHUNK, HEADS * HEAD_W), lambda n: (blk(n), 0))
    gate = pl.BlockSpec((HEADS, CHUNKS_PER_STEP, 1, CHUNK), lambda n: (0, blk(n), 0, 0))
    cm = pl.BlockSpec((HEADS, CHUNKS_PER_STEP, HEAD_W, HEAD_W), lambda n: (0, blk(n), 0, 0))
    vec = pl.BlockSpec((HEADS, CHUNKS_PER_STEP, 1, HEAD_W), lambda n: (0, blk(n), 0, 0))
    return nb, tm, gate, cm, vec


_ML_STATE = [pltpu.VMEM((HEADS, HEAD_W, HEAD_W), F32), pltpu.VMEM((HEADS, 1, HEAD_W), F32), pltpu.VMEM((HEADS, 1, HEAD_W), F32)]


def _recurrences_fwd(q, k, v, la, qm, km, vm, li, lf, deps=()):
    t = v.shape[0]
    nc = t // CHUNK
    nb, hm, tm, st = _gla_specs(nc, False)
    _, _, gate, cm, vec = _ml_specs(nc, False)
    n_in = 9 + len(deps)

    def body(*refs):
        q_ref, k_ref, v_ref, la_ref, qm_ref, km_ref, vm_ref, li_ref, lf_ref = refs[:9]
        o_ref, sp_ref, hc_ref, cp_ref, np_ref, mp_ref, st_ref, c_ref, n_ref, m_ref = refs[n_in:]

        @pl.when(pl.program_id(0) == 0)
        def _():
            for ref in (st_ref, c_ref, n_ref, m_ref):
                ref[...] = jnp.zeros_like(ref)

        s, cs, ns, ms = st_ref[...], c_ref[...], n_ref[...], m_ref[...][:, :, 0:1]
        for c in range(CHUNKS_PER_STEP):
            r = _chunk_rows(c)
            sp_ref[:, c] = s
            o, s = _gla_chunk(q_ref[:, r], k_ref[:, r], _heads(v_ref, r), la_ref[:, r], s)
            _put_heads(o_ref, o, r)
            cp_ref[:, c] = cs
            np_ref[:, c] = ns
            mp_ref[:, c] = jnp.broadcast_to(ms, m_ref.shape)
            hc, cs, ns, ms = _ml_chunk(_heads(qm_ref, r), _heads(km_ref, r), _heads(vm_ref, r), li_ref[:, c], lf_ref[:, c],
                                       cs, ns, ms)
            _put_heads(hc_ref, hc, r)
        st_ref[...] = s
        c_ref[...] = cs
        n_ref[...] = ns
        m_ref[...] = jnp.broadcast_to(ms, m_ref.shape)

    tm_shape = jax.ShapeDtypeStruct((t, HEADS * HEAD_W), F32)
    vec_shape = jax.ShapeDtypeStruct((HEADS, nc, 1, HEAD_W), F32)
    return pl.pallas_call(
        body, name="recurrences_fwd", grid=(nb,),
        in_specs=[hm, hm, tm, hm, tm, tm, tm, gate, gate] + [ANY] * len(deps), out_specs=[tm, st, tm, cm, vec, vec],
        out_shape=[tm_shape, jax.ShapeDtypeStruct((HEADS, nc, HEAD_W, GLA_DK), F32),
                   tm_shape, jax.ShapeDtypeStruct((HEADS, nc, HEAD_W, HEAD_W), F32), vec_shape, vec_shape],
        scratch_shapes=[pltpu.VMEM((HEADS, HEAD_W, GLA_DK), F32)] + _ML_STATE,
        compiler_params=_cparams("arbitrary"),
    )(q, k, v, la, qm, km, vm, li, lf, *deps)


def _recurrences_bwd(q, k, v, la, sp, do, qm, km, vm, li, lf, cp, npv, mp, dhc, deps=()):
    t = v.shape[0]
    nc = t // CHUNK
    nb, hm, tm, st = _gla_specs(nc, True)
    _, _, gate, cm, vec = _ml_specs(nc, True)
    n_in = 15 + len(deps)

    def body(*refs):
        (q_ref, k_ref, v_ref, la_ref, sp_ref, do_ref,
         qm_ref, km_ref, vm_ref, li_ref, lf_ref, cp_ref, np_ref, mp_ref, dhc_ref) = refs[:15]
        (dq_ref, dk_ref, dv_ref, dla_ref, dqm_ref, dkm_ref, dvm_ref, dli_ref, dlf_ref,
         ds_ref, dc_ref, dn_ref, dm_ref) = refs[n_in:]

        @pl.when(pl.program_id(0) == 0)
        def _():
            for ref in (ds_ref, dc_ref, dn_ref, dm_ref):
                ref[...] = jnp.zeros_like(ref)

        ds, dc, dn, dm = ds_ref[...], dc_ref[...], dn_ref[...], dm_ref[...][:, :, 0:1]
        for c in reversed(range(CHUNKS_PER_STEP)):
            r = _chunk_rows(c)
            _, vjp = jax.vjp(_gla_chunk, q_ref[:, r], k_ref[:, r], _heads(v_ref, r), la_ref[:, r], sp_ref[:, c])
            dq, dk, dv, dla, ds = vjp((_heads(do_ref, r), ds))
            dq_ref[:, r] = dq.astype(dq_ref.dtype)
            dk_ref[:, r] = dk.astype(dk_ref.dtype)
            _put_heads(dv_ref, dv, r)
            dla_ref[:, r] = dla
            _, vjp = jax.vjp(_ml_chunk, _heads(qm_ref, r), _heads(km_ref, r), _heads(vm_ref, r), li_ref[:, c], lf_ref[:, c],
                             cp_ref[:, c], np_ref[:, c], mp_ref[:, c][:, :, 0:1])
            dqm, dkm, dvm, dli, dlf, dc, dn, dm = vjp((_heads(dhc_ref, r), dc, dn, dm))
            _put_heads(dqm_ref, dqm, r)
            _put_heads(dkm_ref, dkm, r)
            _put_heads(dvm_ref, dvm, r)
            dli_ref[:, c] = dli
            dlf_ref[:, c] = dlf
        ds_ref[...] = ds
        dc_ref[...] = dc
        dn_ref[...] = dn
        dm_ref[...] = jnp.broadcast_to(dm, dm_ref.shape)

    hm_shape = jax.ShapeDtypeStruct((HEADS, t, GLA_DK), BF16)
    tm_shape = jax.ShapeDtypeStruct((t, HEADS * HEAD_W), F32)
    gate_shape = jax.ShapeDtypeStruct((HEADS, nc, 1, CHUNK), F32)
    return pl.pallas_call(
        body, name="recurrences_bwd", grid=(nb,),
        in_specs=[hm, hm, tm, hm, st, tm, tm, tm, tm, gate, gate, cm, vec, vec, tm] + [ANY] * len(deps),
        out_specs=[hm, hm, tm, hm, tm, tm, tm, gate, gate],
        out_shape=[hm_shape, hm_shape, jax.ShapeDtypeStruct((t, HEADS * HEAD_W), BF16),
                   jax.ShapeDtypeStruct((HEADS, t, GLA_DK), F32), tm_shape, tm_shape, tm_shape, gate_shape, gate_shape],
        scratch_shapes=[pltpu.VMEM((HEADS, HEAD_W, GLA_DK), F32)] + _ML_STATE,
        compiler_params=_cparams("arbitrary"),
    )(q, k, v, la, sp, do, qm, km, vm, li, lf, cp, npv, mp, dhc, *deps)


@jax.custom_vjp
def _bdot_diag(x, w):
    b = w.shape[1]
    return jnp.concatenate([_raw_dot(x[:, :b], w[0], "nn"), _raw_dot(x[:, b:], w[1], "nn")], axis=1)


def _bdot_diag_fwd(x, w):
    return _bdot_diag(x, w), (x, w)


def _bdot_diag_bwd(res, ct):
    x, w = res
    b = w.shape[1]
    dx = jnp.concatenate([_raw_dot(ct[:, :b], w[0], "nt"), _raw_dot(ct[:, b:], w[1], "nt")], axis=1)
    dw = jnp.stack([_raw_dot(x[:, :b], ct[:, :b], "tn"), _raw_dot(x[:, b:], ct[:, b:], "tn")])
    return dx.astype(x.dtype), dw.astype(w.dtype)


_bdot_diag.defvjp(_bdot_diag_fwd, _bdot_diag_bwd)


def _ml_pre(s0, s1, s2, s3, cw0, cw1, cw2, cw3, cb, wq, wk, wv, wiq, wik, wiv, bif):
    pre = cb + cw0 * s0 + cw1 * s1 + cw2 * s2 + cw3 * s3
    xc = pre * _sigmoid(pre)
    q = _bdot_diag(xc, wq)
    k = _bdot_diag(xc, wk)
    v = _bdot_diag(s3, wv)
    gates = _bdot(q, wiq, "nn") + _bdot(k, wik, "nn") + _bdot(v, wiv, "nn") + bif
    lane = lax.broadcasted_iota(jnp.int32, gates.shape, 1)
    gl = jnp.where(lane < HEADS, gates, _log_sigmoid(gates))
    return xc, q, k, v, gl


def _delayed(xs_ref, x_ref, halo_ref, r):
    xs_ref[0:HALO, :] = halo_ref[...]
    xs_ref[HALO:HALO + r, :] = x_ref[...]
    return [xs_ref[pl.ds(HALO - (CONV_K - 1) + j, r), :] for j in range(CONV_K)]


def _full_spec(shape):
    return pl.BlockSpec(shape, lambda i, nd=len(shape): (0,) * nd)


def _ml_pre_fwd(x_m, x_pad, params, tile=256, deps=()):
    t, w = x_m.shape
    r = min(tile, t)

    def body(*refs):
        x_ref, halo_ref = refs[:2]
        p = [ref[...] for ref in refs[2:2 + len(params)]]
        outs = refs[2 + len(params) + len(deps):-1]
        res = _ml_pre(*_delayed(refs[-1], x_ref, halo_ref, r), *p)
        for ref, val in zip(outs, res):
            ref[...] = val

    row = pl.BlockSpec((r, w), lambda i: (i, 0))
    return pl.pallas_call(
        body, name="ml_pre_fwd", grid=(t // r,),
        in_specs=[row, pl.BlockSpec((HALO, w), lambda i: (i * (r // HALO), 0))] + [_full_spec(p.shape) for p in params]
        + [ANY] * len(deps),
        out_specs=[row] * 4 + [pl.BlockSpec((r, LANES), lambda i: (i, 0))],
        out_shape=[jax.ShapeDtypeStruct((t, w), F32)] * 4 + [jax.ShapeDtypeStruct((t, LANES), F32)],
        scratch_shapes=[pltpu.VMEM((r + HALO, w), F32)],
        compiler_params=_cparams("arbitrary"),
    )(x_m, x_pad, *params, *deps)


def _ml_pre_bwd(x_m, x_pad, params, cts, tile=256):
    t, w = x_m.shape
    r = min(tile, t)
    nt = t // r
    n_p = len(params)

    def body(*refs):
        x_ref, halo_ref = refs[:2]
        p = [ref[...] for ref in refs[2:2 + n_p]]
        ct = [ref[...] for ref in refs[2 + n_p:7 + n_p]]
        dx_ref = refs[7 + n_p]
        dp_refs = refs[8 + n_p:8 + 2 * n_p]
        xs_ref, ds_ref, carry_ref = refs[8 + 2 * n_p:]
        step = pl.program_id(0)

        @pl.when(step == 0)
        def _():
            ds_ref[...] = jnp.zeros_like(ds_ref)
            carry_ref[...] = jnp.zeros_like(carry_ref)

        _, vjp = jax.vjp(_ml_pre, *_delayed(xs_ref, x_ref, halo_ref, r), *p)
        grads = vjp(tuple(ct))
        for j in range(CONV_K):
            ds_ref[j, HALO:HALO + r, :] = grads[j]
        lead = HALO + CONV_K - 1
        d_tile = sum(ds_ref[j, pl.ds(lead - j, r), :] for j in range(CONV_K))
        d_halo = sum(ds_ref[j, pl.ds(CONV_K - 1 - j, HALO), :] for j in range(CONV_K))
        dx_ref[...] = jnp.concatenate([d_tile[:r - HALO], d_tile[r - HALO:] + carry_ref[...]], axis=0).astype(dx_ref.dtype)
        carry_ref[...] = d_halo
        _accumulate(step, dp_refs, grads[CONV_K:])

    row = pl.BlockSpec((r, w), lambda i: (nt - 1 - i, 0))
    return pl.pallas_call(
        body, name="ml_pre_bwd", grid=(nt,),
        in_specs=[row, pl.BlockSpec((HALO, w), lambda i: ((nt - 1 - i) * (r // HALO), 0))] + [_full_spec(p.shape) for p in params]
        + [row] * 4 + [pl.BlockSpec((r, LANES), lambda i: (nt - 1 - i, 0))],
        out_specs=[row] + [_full_spec(p.shape) for p in params],
        out_shape=[jax.ShapeDtypeStruct((t, w), BF16)] + [jax.ShapeDtypeStruct(p.shape, F32) for p in params],
        scratch_shapes=[pltpu.VMEM((r + HALO, w), F32), pltpu.VMEM((CONV_K, r + 2 * HALO, w), F32), pltpu.VMEM((HALO, w), F32)],
        compiler_params=_cparams("arbitrary"),
    )(x_m, x_pad, *params, *cts)


def _per_head(fn, row_vals, head_params, shared_params=()):
    return [fn(*[a[:, hs] for a in row_vals], *[p[:, hs] for p in head_params], *shared_params) for hs in _head_slices(HEAD_W)]


def _gla_out(o, g, gn):
    return _rms(o, gn) * (g * _sigmoid(g))


def _ml_out(hc, op, xc, g, sk):
    hcell = hc * _sigmoid(op)
    mu = jnp.mean(hcell, axis=-1, keepdims=True)
    d = hcell - mu
    var = jnp.mean(d * d, axis=-1, keepdims=True)
    return d * lax.rsqrt(var + EPS) * g + sk * xc


def _log_decay(al, w, b):
    return _log_sigmoid(_bdot(al, w, "nn") + b) * (1.0 / GLA_GATE_NORM)


def _merge(ga, gb, ya, yb):
    ga, gb, ya, yb = (a.astype(F32) for a in (ga, gb, ya, yb))
    return _sigmoid(ga) * ya + _sigmoid(gb) * yb


def _post_mix(x, z, gpm, gpl):
    x1 = x + _rms(z, gpm)
    return x1, _rms(x1, gpl)


def _loss_rows(x1, dn, tgt, g):
    e = x1 + _rms(dn, g) - tgt
    return 0.5 * jnp.sum(jnp.mean(e * e, axis=-1, keepdims=True), axis=0, keepdims=True)


def _lin(p):
    return 4 * p[0] + 2 * p[1] + p[2]


def _me():
    return lax.axis_index("x"), lax.axis_index("y"), lax.axis_index("c")


def _flip(p, k):
    return tuple((1 - v) if (k >> (2 - i)) & 1 else v for i, v in enumerate(p))


ANY = pl.BlockSpec(memory_space=pl.ANY)


HBM = pl.BlockSpec(memory_space=pltpu.HBM)
SEM = pl.BlockSpec(memory_space=pltpu.SEMAPHORE)
DATAFLOW = pltpu.SideEffectType.DATAFLOW_SIDE_EFFECTING


SIBLING = 1
OTHER_CHIPS = (2, 4, 6)


def _peer_copies(kinds, srcs, lands, send_sems, recv_sems):
    me = _me()
    copies = []
    for a, (kind, src, land) in enumerate(zip(kinds, srcs, lands)):
        masks = {"gather": range(1, N_DEV), "exchange": range(1, N_DEV), "gather_chips": (SIBLING, *OTHER_CHIPS),
                 "gather_pass": OTHER_CHIPS}[kind]
        for k in masks:
            peer = _flip(me, k)
            if kind == "gather_pass":
                block = land.at[_lin(peer)]
                src_ref, dst_ref, target = block, block, _flip(me, SIBLING)
            else:
                src_ref, dst_ref, target = (src.at[_lin(peer)] if kind == "exchange" else src), land.at[_lin(me)], peer
            copies.append(pltpu.make_async_remote_copy(
                src_ref=src_ref, dst_ref=dst_ref, send_sem=send_sems.at[a * 7 + k - 1], recv_sem=recv_sems.at[a * 7 + k - 1],
                device_id=target, device_id_type=MESH))
    return copies


def _own_copies(kinds, srcs, lands, own_sems):
    return [pltpu.make_async_copy(src, land.at[_lin(_me())], own_sems.at[a])
            for a, (kind, src, land) in enumerate(zip(kinds, srcs, lands)) if kind in ("gather", "gather_chips")]


def _copies_start(kind, srcs, name, after=None, lands=None):
    n = len(srcs)
    extra = [] if after is None else [after]
    kind = [kind] * n if isinstance(kind, str) else list(kind)
    land_shapes = [(s.shape if k == "exchange" else (N_DEV, *s.shape)) for k, s in zip(kind, srcs)]
    lands = [lax.empty(ls, s.dtype) for ls, s in zip(land_shapes, srcs)] if lands is None else lands

    def body(*refs):
        sems = refs[2 * n + len(extra):]
        for cp in _peer_copies(kind, refs[:n], refs[n:2 * n], sems[0], sems[1]) + _own_copies(kind, refs[:n], refs[n:2 * n], sems[2]):
            cp.start()
        refs[-1][...] = jnp.zeros_like(refs[-1])

    def hbm(a):
        return pltpu.with_memory_space_constraint(a, pltpu.HBM)

    out = pl.pallas_call(
        body, name=name,
        out_shape=(pltpu.SemaphoreType.DMA((7 * n,)), pltpu.SemaphoreType.DMA((7 * n,)), pltpu.SemaphoreType.DMA((n,)),
                   *[pltpu.HBM(s.shape, s.dtype) for s in srcs],
                   *[pltpu.HBM(ls, s.dtype) for ls, s in zip(land_shapes, srcs)],
                   jax.ShapeDtypeStruct((8, LANES), F32)),
        in_specs=[HBM] * (2 * n) + [ANY] * len(extra),
        out_specs=(SEM, SEM, SEM, *[HBM] * (2 * n), pl.BlockSpec(memory_space=pltpu.VMEM)),
        input_output_aliases={i: 3 + i for i in range(2 * n)},
        compiler_params=pltpu.CompilerParams(has_side_effects=DATAFLOW),
    )(*[hbm(s) for s in srcs], *[hbm(a) for a in lands], *extra)
    return (kind, n, out[:-1]), out[-1]


def _copies_wait(state, after, name):
    kind, n, (send_sems, recv_sems, own_sems, *thru) = state
    after = list(after) if isinstance(after, (list, tuple)) else [after]

    def body(*refs):
        for cp in _peer_copies(kind, refs[:n], refs[n:2 * n], refs[2 * n], refs[2 * n + 1]):
            cp.wait_send()
            cp.wait_recv()
        for cp in _own_copies(kind, refs[:n], refs[n:2 * n], refs[2 * n + 2]):
            cp.wait()

    out = pl.pallas_call(
        body, name=name,
        out_shape=tuple(pltpu.HBM(t.shape, t.dtype) for t in thru),
        in_specs=[HBM] * (2 * n) + [SEM, SEM, SEM] + [ANY] * len(after), out_specs=tuple([HBM] * (2 * n)),
        input_output_aliases={i: i for i in range(2 * n)},
        compiler_params=pltpu.CompilerParams(has_side_effects=DATAFLOW),
    )(*thru, send_sems, recv_sems, own_sems, *after)
    return out[:n], out[n:]


def _adamw(w, g, m, v):
    m2 = ADAM_B1 * m + (1.0 - ADAM_B1) * g
    v2 = ADAM_B2 * v + (1.0 - ADAM_B2) * (g * g)
    m_hat = m2 / (1.0 - ADAM_B1 ** ADAM_STEP)
    v_hat = v2 / (1.0 - ADAM_B2 ** ADAM_STEP)
    delta = -ADAM_LR * (m_hat / (jnp.sqrt(v_hat) + ADAM_EPS) + ADAM_WD * w)
    return delta, m2, v2


def _sum_adamw(lands, parts, me_idx, w, m, v, name, tile=256):
    r, c = w.shape
    nchunks = len(lands)
    tr = min(tile, r // nchunks)
    per_chunk = r // nchunks // tr
    per = 1 + N_DEV

    def body(me_ref, *refs):
        w_ref, m_ref, v_ref, g_ref, d_ref, m2_ref, v2_ref = refs[nchunks * per:]
        for k in range(nchunks):
            own_ref, slots = refs[k * per], refs[k * per + 1:(k + 1) * per]

            @pl.when(pl.program_id(0) // per_chunk == k)
            def _(own_ref=own_ref, slots=slots):
                own = own_ref[...].astype(F32)
                g = None
                for s in range(N_DEV):
                    term = jnp.where(me_ref[0] == s, own, slots[s][...].astype(F32))
                    g = term if g is None else g + term
                d, m2, v2 = _adamw(w_ref[...], g, m_ref[...], v_ref[...])
                g_ref[...] = g
                d_ref[...] = d
                m2_ref[...] = m2
                v2_ref[...] = v2

    def chunk_specs(k):
        def tile_of(i):
            return jnp.clip(i - k * per_chunk, 0, per_chunk - 1)

        def slot_spec(s):
            return pl.BlockSpec((None, tr, c), lambda i, me: (jnp.where(me[0] == s, (s + 1) % N_DEV, s), tile_of(i), 0))
        return [pl.BlockSpec((None, tr, c), lambda i, me: (me[0], tile_of(i), 0))] + [slot_spec(s) for s in range(N_DEV)]

    row = pl.BlockSpec((tr, c), lambda i, me: (i, 0))
    operands = [a for land, part in zip(lands, parts) for a in (part, *[land] * N_DEV)]
    return pl.pallas_call(
        body, name=name,
        grid_spec=pltpu.PrefetchScalarGridSpec(
            num_scalar_prefetch=1, grid=(r // tr,),
            in_specs=[s for k in range(nchunks) for s in chunk_specs(k)] + [row] * 3,
            out_specs=[row] * 4),
        out_shape=[jax.ShapeDtypeStruct((r, c), F32)] * 4,
        compiler_params=_cparams("parallel"),
    )(me_idx, *operands, w, m, v)


def _cols_view(a):
    r, c = a.shape
    return jnp.transpose(a.reshape(r // LANES, LANES, c), (2, 0, 1))


def _from_cols_view(a, r, c):
    return jnp.transpose(a, (1, 2, 0)).reshape(r, c)


def _sum_adamw_cols(lands, parts, me_idx, w, m, v, name):
    nchunks = len(lands)
    rows_k, c = lands[0].shape[1:]
    r = nchunks * rows_k
    steps = r // LANES
    per_chunk = rows_k // LANES
    per = 1 + N_DEV
    c_pad = _round_up(c, LANES)
    assert steps == 8
    block = steps * max(n for n in range(1, 65) if c % n == 0)

    def body(me_ref, *refs):
        w_ref, m_ref, v_ref, g_ref, d_ref, m2_ref, v2_ref, gt_ref = refs[nchunks * per:]
        for i in range(steps):
            k = i // per_chunk
            own_ref, slots = refs[k * per], refs[k * per + 1:(k + 1) * per]

            @pl.when(pl.program_id(0) == i)
            def _(i=i, own_ref=own_ref, slots=slots):
                own = own_ref[...].astype(F32)
                g = None
                for s in range(N_DEV):
                    term = jnp.where(me_ref[0] == s, own, slots[s][...].astype(F32))
                    g = term if g is None else g + term
                g = jnp.concatenate([g, jnp.zeros((LANES, c_pad - c), F32)], axis=1)
                for j in range(c_pad // LANES):
                    cols = min(LANES, c - j * LANES)
                    gt_ref[pl.ds(steps * LANES * j + i, cols, stride=steps), :] = jnp.transpose(g[:, j * LANES:(j + 1) * LANES])[0:cols]

        @pl.when(pl.program_id(0) == steps - 1)
        def _():
            def update(b, carry):
                cols = pl.ds(b * (block // steps), block // steps)
                g = gt_ref[pl.ds(pl.multiple_of(b * block, steps), block), :].reshape(block // steps, steps, LANES)
                d, m2, v2 = _adamw(w_ref[cols], g, m_ref[cols], v_ref[cols])
                g_ref[cols] = g
                d_ref[cols] = d
                m2_ref[cols] = m2
                v2_ref[cols] = v2
                return carry
            lax.fori_loop(0, c * steps // block, update, 0)

    def chunk_specs(k):
        def tile_of(i):
            return jnp.clip(i - k * per_chunk, 0, per_chunk - 1)

        def slot_spec(s):
            return pl.BlockSpec((None, LANES, c), lambda i, me: (jnp.where(me[0] == s, (s + 1) % N_DEV, s), tile_of(i), 0))
        return [pl.BlockSpec((None, LANES, c), lambda i, me: (me[0], tile_of(i), 0))] + [slot_spec(s) for s in range(N_DEV)]

    whole = pl.BlockSpec((c, steps, LANES), lambda i, me: (0, 0, 0))
    operands = [a for land, part in zip(lands, parts) for a in (part, *[land] * N_DEV)]
    return pl.pallas_call(
        body, name=name,
        grid_spec=pltpu.PrefetchScalarGridSpec(
            num_scalar_prefetch=1, grid=(steps,),
            in_specs=[s for k in range(nchunks) for s in chunk_specs(k)]
            + [pl.BlockSpec((c, steps, LANES), lambda i, me: (0, 0, 0), pipeline_mode=pl.Buffered(1))] * 3,
            out_specs=[whole] * 4,
            scratch_shapes=[pltpu.VMEM((c * steps, LANES), F32)]),
        out_shape=[jax.ShapeDtypeStruct((c, steps, LANES), F32)] * 4,
        compiler_params=_cparams("arbitrary"),
    )(me_idx, *operands, w, m, v)


def _small_update(name, me_idx, kinds, lands, owns, ws, ms, vs, sums=()):
    n = len(ws)
    lands, owns = list(lands) + [s[0] for s in sums], list(owns) + [s[1] for s in sums]
    kinds = list(kinds) + ["gather"] * len(sums)
    nl = len(lands)

    def summed(me, land_ref, own):
        g = None
        for s in range(N_DEV):
            term = jnp.where(me == s, own, land_ref[s]).astype(F32)
            g = term if g is None else g + term
        return g

    def body(me_ref, *refs):
        land_refs, own_refs = refs[:nl], refs[nl:2 * nl]
        w_refs, m_refs, v_refs = (refs[2 * nl + i * n:2 * nl + (i + 1) * n] for i in range(3))
        outs = refs[2 * nl + 3 * n:]
        me = me_ref[0]
        for i in range(n):
            g = summed(me, land_refs[i], own_refs[i][...])
            d, m2, v2 = _adamw(w_refs[i][...], g, m_refs[i][...], v_refs[i][...])
            for ref, val in zip(outs[4 * i:4 * i + 4], (g, d, m2, v2)):
                ref[...] = val
        for i in range(n, nl):
            outs[4 * n + i - n][...] = summed(me, land_refs[i], own_refs[i][...])

    def whole(shape):
        return pl.BlockSpec(shape, lambda i, me, nd=len(shape): (0,) * nd)

    def own_spec(kind, own):
        if kind == "gather":
            return whole(own.shape)
        return pl.BlockSpec((None, *own.shape[1:]), lambda i, me: (me[0], 0, 0))

    shapes = [w.shape for w in ws]
    out_shapes = [s for s in shapes for _ in range(4)] + [s[1].shape for s in sums]
    return pl.pallas_call(
        body, name=name,
        grid_spec=pltpu.PrefetchScalarGridSpec(
            num_scalar_prefetch=1, grid=(1,),
            in_specs=[whole(a.shape) for a in lands] + [own_spec(k, o) for k, o in zip(kinds, owns)]
            + [whole(s) for s in shapes] * 3,
            out_specs=[whole(s) for s in out_shapes]),
        out_shape=[jax.ShapeDtypeStruct(s, F32) for s in out_shapes],
        compiler_params=_cparams("arbitrary"),
    )(me_idx, *lands, *owns, *ws, *ms, *vs)


def _small_view(n, a):
    if a.ndim == 1:
        return a.reshape(1, -1)
    if a.ndim == 3:
        return a.transpose(1, 2, 0).reshape(QKV_BLOCK * QKV_BLOCK, -1)
    return a.T if n == "w_if" else a


def _small_unview(n, a, shape):
    if len(shape) == 1:
        return a.reshape(shape)
    if len(shape) == 3:
        return a.reshape(QKV_BLOCK, QKV_BLOCK, -1).transpose(2, 0, 1)
    return a.T if n == "w_if" else a


def _small_shards(n, g):
    if n == "w_if":
        return g.reshape(N_DEV, -1, g.shape[1]).transpose(0, 2, 1)
    return g.reshape(g.shape[0], N_DEV, -1).transpose(1, 0, 2)


def _small_unshard(n, s):
    if n == "w_if":
        return s.transpose(0, 2, 1).reshape(-1, s.shape[1])
    return s.transpose(1, 0, 2).reshape(s.shape[1], -1)


def _to_hm(a, d):
    t = a.shape[0]
    return a.reshape(t, HEADS, d).transpose(1, 0, 2)


def _from_hm(a):
    h, t, d = a.shape
    return a.transpose(1, 0, 2).reshape(t, h * d)


def _gate_rows(g):
    t = g.shape[0]
    return g.T.reshape(HEADS, t // CHUNK, 1, CHUNK)


def _gate_cols(g):
    h, nc, _, c = g.shape
    return g.reshape(h, nc * c).T


def _blockdiag_dense(w):
    n = w.shape[0] * QKV_BLOCK // 2
    tiled = jnp.tile(w.reshape(2, n, QKV_BLOCK), (1, 1, n // QKV_BLOCK))
    r = lax.broadcasted_iota(jnp.int32, (2, n, n), 1)
    c = lax.broadcasted_iota(jnp.int32, (2, n, n), 2)
    return jnp.where(r // QKV_BLOCK == c // QKV_BLOCK, tiled, 0.0)


def _blockdiag_blocks(dense):
    _, n, _ = dense[0].shape
    k = len(dense)

    def body(*refs):
        r = lax.broadcasted_iota(jnp.int32, (n, n), 0)
        c = lax.broadcasted_iota(jnp.int32, (n, n), 1)
        fr = lax.broadcasted_iota(jnp.int32, (n, LANES), 0)
        fc = lax.broadcasted_iota(jnp.int32, (n, LANES), 1)
        fold = ((fr & (QKV_BLOCK - 1)) == fc).astype(BF16)
        for i in range(k):
            for half in range(2):
                kept = jnp.where((r >> 2) == (c >> 2), refs[i][half], 0.0)
                refs[k + i][half] = sum(lax.dot_general(t, fold, _dims("nn", 2), preferred_element_type=F32)
                                        for t in _split3(kept))

    out = pl.pallas_call(body, name="blockdiag_blocks", out_shape=[jax.ShapeDtypeStruct((2, n, LANES), F32)] * k)(*dense)
    return [o[:, :, 0:QKV_BLOCK].reshape(2 * n // QKV_BLOCK, QKV_BLOCK, QKV_BLOCK) for o in out]


def _col_blocks(w):
    k, n = w.shape
    return w.reshape(k, N_DEV, n // N_DEV).transpose(1, 0, 2)


def _from_col_blocks(g):
    d, k, n = g.shape
    return g.transpose(1, 0, 2).reshape(k, d * n)


def _first_norm(x, g):
    return _rowwise("pre_mix_norm", lambda xv, gv: ((_rms(xv, gv),), ()), [x], [g], [(x.shape[1], BF16)])[0]


def _local_step(x, h, tgt, weight, ws, prefetch, pass_on, on_grads, on_small):
    t, d = x.shape
    g1 = ws["g_pre_mix"]

    def dep(token):
        return () if token is None else (token,)

    w_in = weight("w_in", x)
    fetch_mix = prefetch(("w_pa", "w_pb", "w_o"), w_in)
    fetch_up = prefetch(("w_up", "w_down"), fetch_mix)

    n_in = w_in.shape[2]

    offs = [0]
    for s in IN_SPLITS:
        offs.append(offs[-1] + s)

    w_a_up_p = jnp.pad(ws["w_a_up"], ((0, LANES - LOWRANK), (0, 0)))
    b_a_up = ws["b_a_up"]

    def proj_in_fwd(hv, w, wa, ba):
        proj = jnp.concatenate([_raw_dot(hv, w[j], "nn") for j in range(N_DEV)], axis=1)
        parts = [proj[:, offs[i]:offs[i + 1]] for i in range(len(IN_SPLITS))]
        parts[4] = jnp.concatenate([parts[4], jnp.zeros((parts[4].shape[0], LANES - LOWRANK), F32)], axis=1)
        return (*parts, _log_decay(parts[4], wa, ba)), ()

    widths = [LANES if s == LOWRANK else s for s in IN_SPLITS]
    q_a, k_a, v_a, g_a, a_low_p, x_m, o_pre, gate_a, gate_b, la = _rowwise(
        "proj_in", proj_in_fwd, [h], [w_in, w_a_up_p, b_a_up],
        [(wd, BF16 if i == 2 else F32) for i, wd in enumerate(widths)] + [(HEADS * GLA_DK, F32)],
        deps=dep(fetch_up))

    q_hm, k_hm, la_hm = _to_hm(q_a, GLA_DK), _to_hm(k_a, GLA_DK), _to_hm(la, GLA_DK)
    gn = ws["g_gla_norm"]
    ml_w = HEADS * HEAD_W

    cw = ws["conv_w"]
    w_if_p = jnp.pad(ws["w_if"], ((0, 0), (0, LANES - 2 * HEADS)))
    pre_params = [cw[0:1], cw[1:2], cw[2:3], cw[3:4], ws["conv_b"],
                  _blockdiag_dense(ws["w_q_ml"]), _blockdiag_dense(ws["w_k_ml"]), _blockdiag_dense(ws["w_v_ml"]),
                  w_if_p[0:ml_w], w_if_p[ml_w:2 * ml_w], w_if_p[2 * ml_w:3 * ml_w],
                  jnp.pad(ws["b_if"], ((0, 0), (0, LANES - 2 * HEADS)))]
    x_pad = jnp.pad(x_m, ((HALO, 0), (0, 0)))
    xc, q_m, k_m, v_m, gl = _ml_pre_fwd(x_m, x_pad, pre_params, tile=512)
    pass_mix = pass_on("w_pa", xc)
    li, lf = _gate_rows(gl[:, 0:HEADS]), _gate_rows(gl[:, HEADS:2 * HEADS])
    o_gla, s_prev, hc, c_prev, n_prev, m_prev = _recurrences_fwd(q_hm, k_hm, v_a, la_hm, q_m, k_m, v_m, li, lf,
                                                                 deps=dep(pass_mix))
    pass_up = pass_on("w_up", hc)
    g_ml, skip = ws["g_ml_norm"], ws["ml_skip"]

    def branches_out(o, g, a, b, c_, ga, gb, n_, wa, gm, s, wb):
        ya_in = jnp.concatenate(_per_head(_gla_out, [o, g], [], [n_]), axis=1)
        ya = _raw_dot(ya_in, wa, "nn")
        hb = jnp.concatenate(_per_head(_ml_out, [a, b, c_], [gm, s]), axis=1)
        yb = _raw_dot(hb, wb, "nn")
        return (ya_in, ya, hb, yb, _merge(ga, gb, ya, yb)), ()

    ya_in, y_a, h_b, y_b, merged = _rowwise(
        "branches_out", branches_out, [o_gla, g_a, hc, o_pre, xc, gate_a, gate_b],
        [gn, weight("w_pa", hc), g_ml, skip, weight("w_pb", hc)],
        [(ml_w, BF16), (d, BF16), (ml_w, BF16), (d, BF16), (d, BF16)], tile=512, deps=dep(pass_up))

    gpm, gpl, gpo = ws["g_post_mix"], ws["g_pre_mlp"], ws["g_post_mlp"]

    def proj_o_fwd(mg, xv, w, a, b):
        zv = _raw_dot(mg, w, "nn")
        return (zv, *_post_mix(xv, zv, a, b)), ()

    z, x1, h2 = _rowwise("proj_o", proj_o_fwd, [merged, x], [weight("w_o", merged), gpm, gpl],
                         [(d, F32), (d, F32), (d, BF16)], tile=512)
    w_up = weight("w_up", h2)
    w_down = weight("w_down", h2)
    d_ff = w_down.shape[0]

    def mlp_loss(h2v, x1v, tgtv, wu, wd, g):
        upv = jnp.concatenate([_raw_dot(h2v, wu[j], "nn") for j in range(wu.shape[0])], axis=1)
        uv = jnp.square(jnp.maximum(upv, 0.0))
        dnv = _raw_dot(uv, wd, "nn")
        loss, vjp = jax.vjp(lambda a, b, c_: _loss_rows(a, b, tgtv, c_), x1v, dnv, g)
        dx1, ddn, dg = vjp(jnp.ones((1, 1), F32))
        return (upv, uv, dx1, ddn), (jnp.broadcast_to(loss, (1, LANES)), dg)

    up, u, dx1_y, d_dn, loss, d_gpo = _rowwise("mlp_loss", mlp_loss, [h2, x1, tgt], [w_up, w_down, gpo],
                                               [(d_ff, BF16), (d_ff, BF16), (d, F32), (d, BF16)],
                                               [((1, LANES), F32), ((1, d), F32)])

    dw_down = _mm(u, d_dn, "tn", BF16, "mlp_down_dw", tm=512)

    def mlp_dx(ddn, upv, xv, zv, dx1, wd, wu, a, b):
        dup = (_raw_dot(ddn, wd, "nt") * (2.0 * jnp.maximum(upv.astype(F32), 0.0))).astype(BF16)
        ns = wu.shape[2]
        dh2 = sum(_raw_dot(dup[:, j * ns:(j + 1) * ns], wu[j], "nt") for j in range(wu.shape[0]))
        _, vjp = jax.vjp(_post_mix, xv, zv, a, b)
        dx, dz, da, db = vjp((dx1, dh2))
        return (dup, dx, dz), (da, db)

    d_up, dx_res, d_z, d_gpm, d_gpl = _rowwise("mlp_dx", mlp_dx, [d_dn, up, x, z, dx1_y], [w_down, w_up, gpm, gpl],
                                               [(d_ff, BF16), (d, F32), (d, BF16)], [((1, d), F32), ((1, d), F32)])
    dw_up = _mm_shard_cols(h2, [d_up], [d_up.shape[1]], w_up.shape[2], "mlp_up_dw", 0, d)
    sent_mlp = on_grads(dict(w_down=dw_down, w_up=dw_up))

    def branches_out_bwd(dz, ga, gb, ya, yb, o, g, a, b, c_, wo, wa, n_, wb, gm, s):
        d_ga_, d_gb_, d_ya_, d_yb_ = jax.vjp(_merge, ga, gb, ya, yb)[1](_raw_dot(dz, wo, "nt"))
        ct_a, ct_b = _raw_dot(d_ya_, wa, "nt"), _raw_dot(d_yb_, wb, "nt")
        parts_a, parts_b = [], []
        for hs in _head_slices(HEAD_W):
            parts_a.append(jax.vjp(_gla_out, o[:, hs], g[:, hs], n_)[1](ct_a[:, hs]))
            parts_b.append(jax.vjp(_ml_out, a[:, hs], b[:, hs], c_[:, hs], gm[:, hs], s[:, hs])[1](ct_b[:, hs]))
        cat = lambda parts, i: jnp.concatenate([p[i] for p in parts], axis=1)
        return ((d_ga_, d_gb_, d_ya_, d_yb_, cat(parts_a, 0), cat(parts_a, 1), cat(parts_b, 0), cat(parts_b, 1), cat(parts_b, 2)),
                (sum(p[2] for p in parts_a), cat(parts_b, 3), cat(parts_b, 4)))

    d_ga, d_gb, d_ya, d_yb, d_o, d_g_a, d_hc, d_opre, d_xc, d_gn, d_gml, d_skip = _rowwise(
        "branches_out_bwd", branches_out_bwd, [d_z, gate_a, gate_b, y_a, y_b, o_gla, g_a, hc, o_pre, xc],
        [weight("w_o", merged), weight("w_pa", hc), gn, weight("w_pb", hc), g_ml, skip],
        [(d, BF16)] * 4 + [(ml_w, F32), (ml_w, BF16), (ml_w, F32), (ml_w, BF16), (ml_w, F32)],
        [((1, HEAD_W), F32), ((1, ml_w), F32), ((1, ml_w), F32)], deps=dep(sent_mlp))
    dw_o, dw_pa, dw_pb = _mm_tn_whole([(merged, d_z), (ya_in, d_ya), (h_b, d_yb)], "mix_dw")
    sent_mix = on_grads(dict(w_o=dw_o, w_pa=dw_pa, w_pb=dw_pb))

    dq_hm, dk_hm, d_va, dla_hm, d_qm, d_km, d_vm, d_li, d_lf = _recurrences_bwd(
        q_hm, k_hm, v_a, la_hm, s_prev, d_o, q_m, k_m, v_m, li, lf, c_prev, n_prev, m_prev, d_hc, deps=dep(sent_mix))
    d_gl = jnp.concatenate([_gate_cols(d_li), _gate_cols(d_lf), jnp.zeros((t, LANES - 2 * HEADS), F32)], axis=1)
    pre_grads = _ml_pre_bwd(x_m, x_pad, pre_params, [d_xc, d_qm, d_km, d_vm, d_gl], tile=512)
    d_xm = pre_grads[0]
    d_cw = jnp.concatenate(pre_grads[1:5], axis=0)
    d_cb = pre_grads[5]
    d_wq, d_wk, d_wv = _blockdiag_blocks(pre_grads[6:9])
    d_wif = jnp.concatenate(pre_grads[9:12], axis=0)[:, 0:2 * HEADS]
    d_bif = pre_grads[12][:, 0:2 * HEADS]

    def decay_bwd(al, ct, w, b):
        _, vjp = jax.vjp(_log_decay, al, w, b)
        dal, dw, db = vjp(ct)
        return (dal,), (dw, db)

    d_alow_p, d_wa_p, d_ba = _rowwise("gla_decay_bwd", decay_bwd, [a_low_p, _from_hm(dla_hm)], [w_a_up_p, b_a_up],
                                      [(LANES, BF16)], [(w_a_up_p.shape, F32), (b_a_up.shape, F32)])
    d_proj = [jnp.concatenate([_from_hm(dq_hm), _from_hm(dk_hm), d_va, d_g_a], axis=1), d_alow_p,
              jnp.concatenate([d_xm, d_opre, d_ga, d_gb], axis=1)]
    d_widths = [offs[4], LOWRANK, offs[9] - offs[5]]
    d_pieces = _shard_pieces(d_widths, n_in)
    small = dict(w_a_up=d_wa_p[0:LOWRANK], b_a_up=d_ba, g_gla_norm=d_gn, conv_w=d_cw, conv_b=d_cb,
                 w_q_ml=d_wq, w_k_ml=d_wk, w_v_ml=d_wv, w_if=d_wif, b_if=d_bif, ml_skip=d_skip, g_ml_norm=d_gml,
                 g_post_mix=d_gpm, g_pre_mlp=d_gpl, g_post_mlp=d_gpo)
    sent_small = on_small(small, loss)
    sent_in = sent_small
    for half in range(2):
        dw_half = _mm_shard_cols(h, d_proj, d_widths, n_in, "proj_in_dw_%d" % half, half, d // 2, deps=dep(sent_in))
        sent_in = on_grads({"w_in#%d" % half: dw_half})

    def proj_in_dx(dp_a, dp_low, dp_b, xv, dres, w, g):
        dh = 0.0
        for s in range(N_DEV):
            for i, c_in, c_w, wd in d_pieces[s]:
                src = (dp_a, dp_low, dp_b)[i]
                cols = src.shape[1] - c_in if wd < LANES else wd
                dh = dh + _raw_dot(src[:, c_in:c_in + cols], w[s][:, c_w:c_w + cols], "nt")
        _, vjp = jax.vjp(_rms, xv, g)
        dx, dg = vjp(dh)
        return (dx + dres,), (dg,)

    grad_x, d_g1 = _rowwise("proj_in_dx", proj_in_dx, [*d_proj, x, dx_res], [w_in, g1], [(d, F32)], [((1, d), F32)],
                            deps=dep(sent_in))
    return grad_x, on_small(dict(g_pre_mix=d_g1), None)


BIG = ("w_in", "w_pa", "w_pb", "w_o", "w_up", "w_down")
MIX = ("w_o", "w_pa", "w_pb")
BIG_COL_SHARDED = ("w_in", "w_pa", "w_pb", "w_up")
SMALL_SHARDED = ("w_a_up", "conv_w", "w_if")
SMALL = ("g_pre_mix", "w_a_up", "b_a_up", "g_gla_norm", "conv_w", "conv_b", "w_q_ml", "w_k_ml", "w_v_ml", "w_if", "b_if",
         "ml_skip", "g_ml_norm", "g_post_mix", "g_pre_mlp", "g_post_mlp")
WEIGHTS = ("g_pre_mix", "w_in", "w_a_up", "b_a_up", "g_gla_norm", "conv_w", "conv_b", "w_q_ml", "w_k_ml", "w_v_ml", "w_if", "b_if",
           "ml_skip", "g_ml_norm", "w_pa", "w_pb", "w_o", "g_post_mix", "g_pre_mlp", "w_up", "w_down", "g_post_mlp")


def kernel(x, g_pre_mix, w_in, w_a_up, b_a_up, g_gla_norm, conv_w, conv_b, w_q_ml, w_k_ml, w_v_ml, w_if, b_if, ml_skip, g_ml_norm, w_pa, w_pb, w_o, g_post_mix, g_pre_mlp, w_up, w_down, g_post_mlp, loss_target, m_g_pre_mix, m_w_in, m_w_a_up, m_b_a_up, m_g_gla_norm, m_conv_w, m_conv_b, m_w_q_ml, m_w_k_ml, m_w_v_ml, m_w_if, m_b_if, m_ml_skip, m_g_ml_norm, m_w_pa, m_w_pb, m_w_o, m_g_post_mix, m_g_pre_mlp, m_w_up, m_w_down, m_g_post_mlp, v_g_pre_mix, v_w_in, v_w_a_up, v_b_a_up, v_g_gla_norm, v_conv_w, v_conv_b, v_w_q_ml, v_w_k_ml, v_w_v_ml, v_w_if, v_b_if, v_ml_skip, v_g_ml_norm, v_w_pa, v_w_pb, v_w_o, v_g_post_mix, v_g_pre_mlp, v_w_up, v_w_down, v_g_post_mlp):
    args = dict(locals())
    w = {n: args[n][0] for n in WEIGHTS}
    m = {n: args["m_" + n][0] for n in WEIGHTS}
    v = {n: args["v_" + n][0] for n in WEIGHTS}

    me_lin = _lin(_me())
    me_idx = jnp.reshape(me_lin, (1,)).astype(jnp.int32)

    def full_weight(n, g):
        if n in ("w_in", "w_up"):
            return g
        return _from_col_blocks(g) if n in BIG_COL_SHARDED else g.reshape(-1, g.shape[-1])

    def grad_parts(n, g):
        if n.partition("#")[0] in ("w_in", "w_up"):
            return g
        return (_col_blocks(g) if n in BIG_COL_SHARDED else g.reshape(N_DEV, -1, g.shape[-1])).astype(BF16)

    sharded_names = tuple(SMALL_SHARDED)
    narrow = {n: w[n].astype(BF16) for n in BIG}
    ready, pending, passing = {}, {}, {}
    first_state, _ = _copies_start(["gather"] * len(sharded_names) + ["gather_chips"],
                                   [_small_view(n, w[n]) for n in sharded_names] + [narrow["w_in"]], "allgather_start_first")

    def prefetch(group, after):
        state, token = _copies_start("gather_chips", [narrow[n] for n in group], "allgather_start_" + group[0], after)
        for n in group:
            pending[n] = (group, state)
        return token

    def pass_on(n, after):
        group, state = pending[n]
        shards, lands = _copies_wait(state, after, "allgather_wait_" + group[0])
        state, token = _copies_start("gather_pass", shards, "allgather_pass_" + group[0], lands=lands)
        for gn in group:
            passing[gn] = (group, state)
        return token

    def weight(n, after):
        if n not in ready:
            group, state = passing[n]
            _, lands = _copies_wait(state, after, "allgather_passed_" + group[0])
            for gn, land in zip(group, lands):
                ready[gn] = full_weight(gn, land)
        return ready[n]

    h = _first_norm(x[0], w["g_pre_mix"].reshape(1, -1))
    first_own, first_lands = _copies_wait(first_state, [h] + [narrow[n] for n in BIG if n != "w_in"], "allgather_wait_first")
    state, _ = _copies_start("gather_pass", first_own[-1:], "allgather_pass_w_in", lands=first_lands[-1:])
    passing["w_in"] = (("w_in",), state)
    ws = {n: (w[n].reshape(1, -1) if w[n].ndim == 1 else w[n]) for n in SMALL if n not in SMALL_SHARDED}
    for n, land in zip(sharded_names, first_lands):
        ws[n] = _small_unshard(n, land)

    sets, waiting_small = [], []

    def start_set(large, small):
        names = tuple(large)
        s_names, s_kinds, s_srcs = small if small else ((), [], [])
        state, token = _copies_start(s_kinds + ["exchange"] * len(names), s_srcs + [grad_parts(n, large[n]) for n in names],
                                     "exchange_start_" + (names + s_names)[0].replace("#", "_"))
        sets.append((names, s_names, s_kinds, state))
        return token

    def on_grads(grads):
        return start_set(grads, waiting_small.pop() if waiting_small else None)

    def on_small(small, loss):
        names = tuple(small)
        kinds = ["exchange" if n in SMALL_SHARDED else "gather" for n in names]
        srcs = [_small_shards(n, small[n]) if n in SMALL_SHARDED else _small_view(n, small[n]) for n in names]
        if loss is None:
            return start_set({}, (names, kinds, srcs))
        waiting_small.append((names, kinds + ["gather"], srcs + [loss]))
        return None

    grad_x, last_token = _local_step(x[0], h, loss_target[0], weight, ws, prefetch, pass_on, on_grads, on_small)

    out, chunks, sums, updated = {}, {}, [], []

    def finish_set(names, s_names, s_kinds, state, after):
        own, lands = _copies_wait(state, after, "exchange_wait_" + (names + s_names)[0].replace("#", "_"))
        ns = len(s_kinds)
        if s_names:
            k = len(s_names)
            upd = _small_update("adamw_small_" + s_names[0], me_idx, s_kinds[:k], lands[:k], own[:k],
                                *[[_small_view(n, d[n]) for n in s_names] for d in (w, m, v)],
                                sums=list(zip(lands[k:ns], own[k:ns])))
            for i, n in enumerate(s_names):
                out[n] = tuple(_small_unview(n, a, w[n].shape) for a in upd[4 * i:4 * i + 4])
            sums.extend(upd[4 * k:])
        if names == MIX:
            upd = _small_update("adamw_mix", me_idx, ["exchange"] * len(names), lands[ns:], own[ns:],
                                *[[d[n] for n in names] for d in (w, m, v)])
            for i, n in enumerate(names):
                out[n] = tuple(upd[4 * i:4 * i + 4])
            updated.append(upd[1])
            return
        for name, part, land in zip(names, own[ns:], lands[ns:]):
            n, _, chunk = name.partition("#")
            chunks.setdefault(n, []).append((land, part))
            if chunk in ("", "1"):
                got_lands, got_parts = zip(*chunks[n])
                if n == "w_in":
                    upd = _sum_adamw_cols(got_lands, got_parts, me_idx, *[_cols_view(d[n]) for d in (w, m, v)], "adamw_" + n)
                    out[n] = tuple(_from_cols_view(a, *w[n].shape) for a in upd)
                else:
                    out[n] = upd = _sum_adamw(got_lands, got_parts, me_idx, w[n], m[n], v[n], "adamw_" + n)
                updated.append(upd[1])

    for entry in sets:
        finish_set(*entry, [grad_x, last_token] + updated)
    loss_sum = sums[0]

    shaped = lambda a, n: a.reshape(args[n].shape)
    return (loss_sum[0, 0], grad_x[None],
            *[shaped(out[n][0], n) for n in WEIGHTS], *[shaped(out[n][1], n) for n in WEIGHTS],
            *[shaped(out[n][2], n) for n in WEIGHTS], *[shaped(out[n][3], n) for n in WEIGHTS])
```

```python
import functools

import jax
import jax.numpy as jnp
from jax import lax
from jax.experimental import pallas as pl
from jax.experimental.pallas import tpu as pltpu

F32 = jnp.float32
BF16 = jnp.bfloat16
MESH = pl.DeviceIdType.MESH

N_DEV = 8
EPS = 1e-6
CHUNK = 64
CHUNKS_PER_STEP = 8
HEADS = 4
GLA_DK = 64
HEAD_W = 128
GLA_GATE_NORM = 16.0
LOWRANK = 16
CONV_K = 4
QKV_BLOCK = 4
LANES = 128
HALO = 8
IN_SPLITS = (256, 256, 512, 512, 16, 512, 512, 1024, 1024)

ADAM_LR = 0.001
ADAM_B1 = 0.9
ADAM_B2 = 0.999
ADAM_EPS = 1e-08
ADAM_WD = 0.01
ADAM_STEP = 10

VMEM_LIMIT = 56 * 1024 * 1024


def _cparams(*sem):
    return pltpu.CompilerParams(dimension_semantics=sem, vmem_limit_bytes=VMEM_LIMIT)


def _dims(mode, ndim):
    contract = {"nn": ((ndim - 1,), (ndim - 2,)), "nt": ((ndim - 1,), (ndim - 1,)), "tn": ((ndim - 2,), (ndim - 2,))}[mode]
    return contract, (((0,), (0,)) if ndim == 3 else ((), ()))


def _raw_dot(a, b, mode):
    return lax.dot_general(a.astype(BF16), b.astype(BF16), _dims(mode, a.ndim), preferred_element_type=F32)


@functools.partial(jax.custom_vjp, nondiff_argnums=(2,))
def _bdot(a, b, mode):
    return _raw_dot(a, b, mode)


def _bdot_fwd(a, b, mode):
    return _raw_dot(a, b, mode), (a, b)


def _bdot_bwd(mode, res, ct):
    a, b = res
    if mode == "nn":
        da, db = _raw_dot(ct, b, "nt"), _raw_dot(a, ct, "tn")
    elif mode == "nt":
        da, db = _raw_dot(ct, b, "nn"), _raw_dot(ct, a, "tn")
    else:
        da, db = _raw_dot(b, ct, "nt"), _raw_dot(a, ct, "nn")
    return da.astype(a.dtype), db.astype(b.dtype)


_bdot.defvjp(_bdot_fwd, _bdot_bwd)


def _split3(x):
    hi = x.astype(BF16)
    r1 = x - hi.astype(F32)
    mid = r1.astype(BF16)
    return hi, mid, (r1 - mid.astype(F32)).astype(BF16)


def _split_dot(tri, x):
    if x.ndim == 3:
        tri = jnp.broadcast_to(tri, (x.shape[0], *tri.shape))
    return sum(lax.dot_general(tri, t, _dims("nn", x.ndim), preferred_element_type=F32) for t in _split3(x))


def _tri(n, lower):
    r = lax.broadcasted_iota(jnp.int32, (n, n), 0)
    c = lax.broadcasted_iota(jnp.int32, (n, n), 1)
    return ((c <= r) if lower else (c >= r)).astype(BF16)


@jax.custom_vjp
def _cumsum_rows(x):
    return _split_dot(_tri(x.shape[-2], True), x)


def _cumsum_rows_fwd(x):
    return _cumsum_rows(x), None


def _cumsum_rows_bwd(_, ct):
    return (_split_dot(_tri(ct.shape[-2], False), ct),)


_cumsum_rows.defvjp(_cumsum_rows_fwd, _cumsum_rows_bwd)


def _abs(x):
    return jnp.where(x >= 0, x, -x)


def _sigmoid(x):
    return lax.logistic(x)


def _log_sigmoid(x):
    return jnp.minimum(x, 0.0) - jnp.log(1.0 + jnp.exp(-_abs(x)))


def _rms(x, g):
    return x * lax.rsqrt(jnp.mean(x * x, axis=-1, keepdims=True) + EPS) * g


def _head_slices(w):
    return [slice(h * w, (h + 1) * w) for h in range(HEADS)]


def _heads(ref, rows=slice(None)):
    return jnp.stack([ref[rows, hs] for hs in _head_slices(HEAD_W)])


def _put_heads(ref, val, rows=slice(None)):
    for h, hs in enumerate(_head_slices(HEAD_W)):
        ref[rows, hs] = val[h].astype(ref.dtype)


def _tile(dim, want):
    if dim <= want or dim % LANES:
        return dim
    t = want
    while dim % t:
        t -= LANES
    return t


def _mm(a, b, mode, out_dtype, name, tm=1024, tn=1024, tk=4096, epilogue=None, extra=(), deps=(), shards=None):
    if shards == "b":
        assert mode == "nn"
        ns = b.shape[2]
        (m, k), (k2, n) = a.shape, (b.shape[1], b.shape[0] * ns)
        tn = ns
    elif mode == "nn":
        (m, k), (k2, n) = a.shape, b.shape
    elif mode == "nt":
        (m, k), (n, k2) = a.shape, b.shape
    else:
        (k, m), (k2, n) = a.shape, b.shape
    assert k == k2, (name, a.shape, b.shape)
    tm, tn, tk = _tile(m, tm), _tile(n, tn), _tile(k, tk)
    nk = k // tk
    out_dtypes = out_dtype if epilogue else (out_dtype,)
    assert nk == 1 or (out_dtype == F32 and not epilogue), name
    n_in = 2 + len(extra)

    def body(*refs):
        p = _raw_dot(refs[0][...], refs[1][...], mode)
        if nk > 1:
            _accumulate(pl.program_id(2), [refs[n_in + len(deps)]], [p])
            return
        outs = epilogue(p, *[r[...] for r in refs[2:n_in]]) if epilogue else (p,)
        for ref, val in zip(refs[n_in + len(deps):], outs):
            ref[...] = val.astype(ref.dtype)

    a_spec = pl.BlockSpec((tk, tm), lambda i, j, kk: (kk, i)) if mode == "tn" else pl.BlockSpec((tm, tk), lambda i, j, kk: (i, kk))
    if shards == "b":
        b_spec = pl.BlockSpec((None, tk, tn), lambda i, j, kk: (j, kk, 0))
    elif mode == "nt":
        b_spec = pl.BlockSpec((tn, tk), lambda i, j, kk: (j, kk))
    else:
        b_spec = pl.BlockSpec((tk, tn), lambda i, j, kk: (kk, j))
    o_spec = pl.BlockSpec((tm, tn), lambda i, j, kk: (i, j))
    res = pl.pallas_call(
        body, name=name, grid=(m // tm, n // tn, nk),
        in_specs=[a_spec, b_spec] + [o_spec] * len(extra) + [ANY] * len(deps), out_specs=[o_spec] * len(out_dtypes),
        out_shape=[jax.ShapeDtypeStruct((m, n), dt) for dt in out_dtypes],
        compiler_params=_cparams("parallel", "parallel", "arbitrary"),
    )(a, b, *extra, *deps)
    return res if epilogue else res[0]


def _mm_tn_whole(pairs, name):
    def body(*refs):
        for i in range(len(pairs)):
            refs[2 * len(pairs) + i][...] = _raw_dot(refs[2 * i][...], refs[2 * i + 1][...], "tn").astype(BF16)

    return pl.pallas_call(
        body, name=name,
        out_shape=[jax.ShapeDtypeStruct((a.shape[1], b.shape[1]), BF16) for a, b in pairs],
        compiler_params=pltpu.CompilerParams(vmem_limit_bytes=VMEM_LIMIT),
    )(*[x for pair in pairs for x in pair])


def _shard_pieces(widths, n):
    bounds = [0]
    for wd in widths:
        bounds.append(bounds[-1] + wd)
    assert bounds[-1] == N_DEV * n
    return [[(i, max(s * n, b) - b, max(s * n, b) - s * n, min((s + 1) * n, b + wd) - max(s * n, b))
             for i, (b, wd) in enumerate(zip(bounds, widths)) if b < (s + 1) * n and b + wd > s * n]
            for s in range(N_DEV)]


def _mm_shard_cols(a, bs, widths, n, name, row_tile, tm, deps=()):
    t = a.shape[0]
    nb = len(bs)
    pieces = _shard_pieces(widths, n)

    def body(a_ref, *rest):
        b_refs = rest[:nb]
        o_ref, at_ref = rest[nb + len(deps):]
        j = pl.program_id(0)

        @pl.when(j == 0)
        def _():
            at_ref[...] = a_ref[...].astype(BF16).T

        for s in range(N_DEV):
            @pl.when(j == s)
            def _(s=s):
                for i, c_in, c_out, wd in pieces[s]:
                    cols = min(_round_up(wd, LANES), bs[i].shape[1] - c_in) if wd < LANES else wd
                    p = _raw_dot(at_ref[...], b_refs[i][:, c_in:c_in + cols], "nn")
                    o_ref[:, c_out:c_out + wd] = p[:, 0:wd].astype(BF16)

    return pl.pallas_call(
        body, name=name, grid=(N_DEV,),
        in_specs=[pl.BlockSpec((t, tm), lambda j: (0, row_tile))]
        + [pl.BlockSpec(b.shape, lambda j: (0, 0), pipeline_mode=pl.Buffered(1)) for b in bs] + [ANY] * len(deps),
        out_specs=pl.BlockSpec((None, tm, n), lambda j: (j, 0, 0)),
        out_shape=jax.ShapeDtypeStruct((N_DEV, tm, n), BF16),
        scratch_shapes=[pltpu.VMEM((tm, t), BF16)],
        compiler_params=_cparams("arbitrary"),
    )(a, *bs, *deps)


def _round_up(v, m):
    return -(-v // m) * m


def _rowwise(name, fn, rows, params, out_rows, out_accs=(), tile=256, deps=()):
    t = rows[0].shape[0]
    r = min(tile, t)
    assert t % r == 0
    n_in, n_or = len(rows) + len(params), len(out_rows)
    n_all = n_in + len(deps)
    params = list(params) + list(deps)

    def body(*refs):
        vals = [ref[...] for ref in refs[:n_in]]
        outs = refs[n_all:]
        ro, ao = fn(*vals)
        for ref, v in zip(outs[:n_or], ro):
            ref[...] = v.astype(ref.dtype)
        if out_accs:
            _accumulate(pl.program_id(0), outs[n_or:], ao)

    def full(shape, **kw):
        return pl.BlockSpec(shape, lambda i, nd=len(shape): (0,) * nd, **kw)

    return pl.pallas_call(
        body, name=name, grid=(t // r,),
        in_specs=[pl.BlockSpec((r, a.shape[1]), lambda i: (i, 0)) for a in rows]
        + [full(p.shape, pipeline_mode=pl.Buffered(1)) for p in params],
        out_specs=[pl.BlockSpec((r, w), lambda i: (i, 0)) for w, _ in out_rows] + [full(s) for s, _ in out_accs],
        out_shape=[jax.ShapeDtypeStruct((t, w), dt) for w, dt in out_rows] + [jax.ShapeDtypeStruct(s, dt) for s, dt in out_accs],
        compiler_params=_cparams("arbitrary"),
    )(*rows, *params)


def _accumulate(step, refs, vals):
    for ref, v in zip(refs, vals):
        @pl.when(step == 0)
        def _(ref=ref, v=v):
            ref[...] = v.astype(ref.dtype)

        @pl.when(step > 0)
        def _(ref=ref, v=v):
            ref[...] += v.astype(ref.dtype)


def _gla_chunk(q, k, v, la, st):
    c = q.shape[-2]
    row = lax.broadcasted_iota(jnp.int32, (c, c), 0)
    col = lax.broadcasted_iota(jnp.int32, (c, c), 1)
    cum = _cumsum_rows(la)
    cl = jnp.sum(la, axis=-2, keepdims=True)
    ep = jnp.exp(cum)
    en = jnp.exp(-cum)
    qs = q * (GLA_DK ** -0.5)
    qp = qs * ep
    a_f = _bdot(qp, k * en, "nt")
    a_b = _bdot(qs * en, k * ep, "nt")
    sc = jnp.where(row >= col, a_f, a_b)
    o = _bdot(sc, v, "nn") + _bdot(qp, st, "nt")
    kd = k * jnp.exp(cl - cum)
    st_new = st * jnp.exp(cl) + _bdot(v, kd, "tn")
    return o, st_new


def _gla_specs(nc, rev):
    nb = nc // CHUNKS_PER_STEP
    rows = CHUNKS_PER_STEP * CHUNK

    def blk(n):
        return (nb - 1 - n) if rev else n
    hm = pl.BlockSpec((HEADS, rows, GLA_DK), lambda n: (0, blk(n), 0))
    tm = pl.BlockSpec((rows, HEADS * HEAD_W), lambda n: (blk(n), 0))
    st = pl.BlockSpec((HEADS, CHUNKS_PER_STEP, HEAD_W, GLA_DK), lambda n: (0, blk(n), 0, 0))
    return nb, hm, tm, st


def _chunk_rows(c):
    return slice(c * CHUNK, (c + 1) * CHUNK)


def _ml_chunk(q, k, v, li_r, lf_r, cm, nv, m):
    c = q.shape[-2]
    row = lax.broadcasted_iota(jnp.int32, (c, c), 0)
    col = lax.broadcasted_iota(jnp.int32, (c, c), 1)
    eye = (row == col).astype(F32)
    li_c = jnp.sum(eye * li_r, axis=-1, keepdims=True)
    lf_c = jnp.sum(eye * lf_r, axis=-1, keepdims=True)
    fc_c = jnp.sum((col <= row).astype(F32) * lf_r, axis=-1, keepdims=True)
    fc_r = jnp.sum((row <= col).astype(F32) * lf_c, axis=-2, keepdims=True)
    f_last = jnp.sum(lf_r, axis=-1, keepdims=True)
    kc = k * (HEAD_W ** -0.5)
    a_c = f_last - fc_c + li_c
    m_loc = jnp.max(a_c, axis=-2, keepdims=True)
    kw = kc * jnp.exp(a_c - m_loc)
    c_chunk = _bdot(kw, v, "tn")
    n_chunk = jnp.sum(kw, axis=-2, keepdims=True)
    m_new = jnp.maximum(f_last + m, m_loc)
    sp = jnp.exp(f_last + m - m_new)
    sl = jnp.exp(m_loc - m_new)
    cm_new = sp * cm + sl * c_chunk
    nv_new = sp * nv + sl * n_chunk
    log_d = li_r - _abs(fc_c - fc_r)
    g_inter = fc_c + m
    m_t = jnp.maximum(g_inter, jnp.max(log_d, axis=-1, keepdims=True))
    s = _bdot(q, kc, "nt") * jnp.exp(log_d - m_t)
    sc = jnp.exp(g_inter - m_t)
    num = _bdot(s, v, "nn") + sc * _bdot(q, cm, "nn")
    den = jnp.sum(s, axis=-1, keepdims=True) + sc * jnp.sum(q * nv, axis=-1, keepdims=True)
    den = jnp.maximum(_abs(den), jnp.exp(-m_t))
    return num / den, cm_new, nv_new, m_new


def _ml_specs(nc, rev):
    nb = nc // CHUNKS_PER_STEP

    def blk(n):
        return (nb - 1 - n) if rev else n
    tm = pl.BlockSpec((CHUNKS_PER_STEP * CHUNK, HEADS * HEAD_W), lambda n: (blk(n), 0))
    gate = pl.BlockSpec((HEADS, CHUNKS_PER_STEP, 1, CHUNK), lambda n: (0, blk(n), 0, 0))
    cm = pl.BlockSpec((HEADS, CHUNKS_PER_STEP, HEAD_W, HEAD_W), lambda n: (0, blk(n), 0, 0))
    vec = pl.BlockSpec((HEADS, CHUNKS_PER_STEP, 1, HEAD_W), lambda n: (0, blk(n), 0, 0))
    return nb, tm, gate, cm, vec


_ML_STATE = [pltpu.VMEM((HEADS, HEAD_W, HEAD_W), F32), pltpu.VMEM((HEADS, 1, HEAD_W), F32), pltpu.VMEM((HEADS, 1, HEAD_W), F32)]


def _recurrences_fwd(q, k, v, la, qm, km, vm, li, lf, deps=()):
    t = v.shape[0]
    nc = t // CHUNK
    nb, hm, tm, st = _gla_specs(nc, False)
    _, _, gate, cm, vec = _ml_specs(nc, False)
    n_in = 9 + len(deps)

    def body(*refs):
        q_ref, k_ref, v_ref, la_ref, qm_ref, km_ref, vm_ref, li_ref, lf_ref = refs[:9]
        o_ref, sp_ref, hc_ref, cp_ref, np_ref, mp_ref, st_ref, c_ref, n_ref, m_ref = refs[n_in:]

        @pl.when(pl.program_id(0) == 0)
        def _():
            for ref in (st_ref, c_ref, n_ref, m_ref):
                ref[...] = jnp.zeros_like(ref)

        s, cs, ns, ms = st_ref[...], c_ref[...], n_ref[...], m_ref[...][:, :, 0:1]
        for c in range(CHUNKS_PER_STEP):
            r = _chunk_rows(c)
            sp_ref[:, c] = s
            o, s = _gla_chunk(q_ref[:, r], k_ref[:, r], _heads(v_ref, r), la_ref[:, r], s)
            _put_heads(o_ref, o, r)
            cp_ref[:, c] = cs
            np_ref[:, c] = ns
            mp_ref[:, c] = jnp.broadcast_to(ms, m_ref.shape)
            hc, cs, ns, ms = _ml_chunk(_heads(qm_ref, r), _heads(km_ref, r), _heads(vm_ref, r), li_ref[:, c], lf_ref[:, c],
                                       cs, ns, ms)
            _put_heads(hc_ref, hc, r)
        st_ref[...] = s
        c_ref[...] = cs
        n_ref[...] = ns
        m_ref[...] = jnp.broadcast_to(ms, m_ref.shape)

    tm_shape = jax.ShapeDtypeStruct((t, HEADS * HEAD_W), F32)
    vec_shape = jax.ShapeDtypeStruct((HEADS, nc, 1, HEAD_W), F32)
    return pl.pallas_call(
        body, name="recurrences_fwd", grid=(nb,),
        in_specs=[hm, hm, tm, hm, tm, tm, tm, gate, gate] + [ANY] * len(deps), out_specs=[tm, st, tm, cm, vec, vec],
        out_shape=[tm_shape, jax.ShapeDtypeStruct((HEADS, nc, HEAD_W, GLA_DK), F32),
                   tm_shape, jax.ShapeDtypeStruct((HEADS, nc, HEAD_W, HEAD_W), F32), vec_shape, vec_shape],
        scratch_shapes=[pltpu.VMEM((HEADS, HEAD_W, GLA_DK), F32)] + _ML_STATE,
        compiler_params=_cparams("arbitrary"),
    )(q, k, v, la, qm, km, vm, li, lf, *deps)


def _recurrences_bwd(q, k, v, la, sp, do, qm, km, vm, li, lf, cp, npv, mp, dhc, deps=()):
    t = v.shape[0]
    nc = t // CHUNK
    nb, hm, tm, st = _gla_specs(nc, True)
    _, _, gate, cm, vec = _ml_specs(nc, True)
    n_in = 15 + len(deps)

    def body(*refs):
        (q_ref, k_ref, v_ref, la_ref, sp_ref, do_ref,
         qm_ref, km_ref, vm_ref, li_ref, lf_ref, cp_ref, np_ref, mp_ref, dhc_ref) = refs[:15]
        (dq_ref, dk_ref, dv_ref, dla_ref, dqm_ref, dkm_ref, dvm_ref, dli_ref, dlf_ref,
         ds_ref, dc_ref, dn_ref, dm_ref) = refs[n_in:]

        @pl.when(pl.program_id(0) == 0)
        def _():
            for ref in (ds_ref, dc_ref, dn_ref, dm_ref):
                ref[...] = jnp.zeros_like(ref)

        ds, dc, dn, dm = ds_ref[...], dc_ref[...], dn_ref[...], dm_ref[...][:, :, 0:1]
        for c in reversed(range(CHUNKS_PER_STEP)):
            r = _chunk_rows(c)
            _, vjp = jax.vjp(_gla_chunk, q_ref[:, r], k_ref[:, r], _heads(v_ref, r), la_ref[:, r], sp_ref[:, c])
            dq, dk, dv, dla, ds = vjp((_heads(do_ref, r), ds))
            dq_ref[:, r] = dq.astype(dq_ref.dtype)
            dk_ref[:, r] = dk.astype(dk_ref.dtype)
            _put_heads(dv_ref, dv, r)
            dla_ref[:, r] = dla
            _, vjp = jax.vjp(_ml_chunk, _heads(qm_ref, r), _heads(km_ref, r), _heads(vm_ref, r), li_ref[:, c], lf_ref[:, c],
                             cp_ref[:, c], np_ref[:, c], mp_ref[:, c][:, :, 0:1])
            dqm, dkm, dvm, dli, dlf, dc, dn, dm = vjp((_heads(dhc_ref, r), dc, dn, dm))
            _put_heads(dqm_ref, dqm, r)
            _put_heads(dkm_ref, dkm, r)
            _put_heads(dvm_ref, dvm, r)
            dli_ref[:, c] = dli
            dlf_ref[:, c] = dlf
        ds_ref[...] = ds
        dc_ref[...] = dc
        dn_ref[...] = dn
        dm_ref[...] = jnp.broadcast_to(dm, dm_ref.shape)

    hm_shape = jax.ShapeDtypeStruct((HEADS, t, GLA_DK), BF16)
    tm_shape = jax.ShapeDtypeStruct((t, HEADS * HEAD_W), F32)
    gate_shape = jax.ShapeDtypeStruct((HEADS, nc, 1, CHUNK), F32)
    return pl.pallas_call(
        body, name="recurrences_bwd", grid=(nb,),
        in_specs=[hm, hm, tm, hm, st, tm, tm, tm, tm, gate, gate, cm, vec, vec, tm] + [ANY] * len(deps),
        out_specs=[hm, hm, tm, hm, tm, tm, tm, gate, gate],
        out_shape=[hm_shape, hm_shape, jax.ShapeDtypeStruct((t, HEADS * HEAD_W), BF16),
                   jax.ShapeDtypeStruct((HEADS, t, GLA_DK), F32), tm_shape, tm_shape, tm_shape, gate_shape, gate_shape],
        scratch_shapes=[pltpu.VMEM((HEADS, HEAD_W, GLA_DK), F32)] + _ML_STATE,
        compiler_params=_cparams("arbitrary"),
    )(q, k, v, la, sp, do, qm, km, vm, li, lf, cp, npv, mp, dhc, *deps)


@jax.custom_vjp
def _bdot_diag(x, w):
    b = w.shape[1]
    return jnp.concatenate([_raw_dot(x[:, :b], w[0], "nn"), _raw_dot(x[:, b:], w[1], "nn")], axis=1)


def _bdot_diag_fwd(x, w):
    return _bdot_diag(x, w), (x, w)


def _bdot_diag_bwd(res, ct):
    x, w = res
    b = w.shape[1]
    dx = jnp.concatenate([_raw_dot(ct[:, :b], w[0], "nt"), _raw_dot(ct[:, b:], w[1], "nt")], axis=1)
    dw = jnp.stack([_raw_dot(x[:, :b], ct[:, :b], "tn"), _raw_dot(x[:, b:], ct[:, b:], "tn")])
    return dx.astype(x.dtype), dw.astype(w.dtype)


_bdot_diag.defvjp(_bdot_diag_fwd, _bdot_diag_bwd)


def _ml_pre(s0, s1, s2, s3, cw0, cw1, cw2, cw3, cb, wq, wk, wv, wiq, wik, wiv, bif):
    pre = cb + cw0 * s0 + cw1 * s1 + cw2 * s2 + cw3 * s3
    xc = pre * _sigmoid(pre)
    q = _bdot_diag(xc, wq)
    k = _bdot_diag(xc, wk)
    v = _bdot_diag(s3, wv)
    gates = _bdot(q, wiq, "nn") + _bdot(k, wik, "nn") + _bdot(v, wiv, "nn") + bif
    lane = lax.broadcasted_iota(jnp.int32, gates.shape, 1)
    gl = jnp.where(lane < HEADS, gates, _log_sigmoid(gates))
    return xc, q, k, v, gl


def _delayed(xs_ref, x_ref, halo_ref, r, first):
    xs_ref[0:HALO, :] = jnp.where(first, 0.0, halo_ref[...])
    xs_ref[HALO:HALO + r, :] = x_ref[...]
    return [xs_ref[pl.ds(HALO - (CONV_K - 1) + j, r), :] for j in range(CONV_K)]


def _halo_spec(r, w, tile_of):
    return pl.BlockSpec((HALO, w), lambda i: (jnp.maximum(tile_of(i) * (r // HALO) - 1, 0), 0))


def _full_spec(shape):
    return pl.BlockSpec(shape, lambda i, nd=len(shape): (0,) * nd)


def _ml_pre_fwd(x_m, params, tile=256, deps=()):
    t, w = x_m.shape
    r = min(tile, t)

    def body(*refs):
        x_ref, halo_ref = refs[:2]
        p = [ref[...] for ref in refs[2:2 + len(params)]]
        outs = refs[2 + len(params) + len(deps):-1]
        res = _ml_pre(*_delayed(refs[-1], x_ref, halo_ref, r, pl.program_id(0) == 0), *p)
        for ref, val in zip(outs, res):
            ref[...] = val

    row = pl.BlockSpec((r, w), lambda i: (i, 0))
    return pl.pallas_call(
        body, name="ml_pre_fwd", grid=(t // r,),
        in_specs=[row, _halo_spec(r, w, lambda i: i)] + [_full_spec(p.shape) for p in params]
        + [ANY] * len(deps),
        out_specs=[row] * 4 + [pl.BlockSpec((r, LANES), lambda i: (i, 0))],
        out_shape=[jax.ShapeDtypeStruct((t, w), F32)] * 4 + [jax.ShapeDtypeStruct((t, LANES), F32)],
        scratch_shapes=[pltpu.VMEM((r + HALO, w), F32)],
        compiler_params=_cparams("arbitrary"),
    )(x_m, x_m, *params, *deps)


def _ml_pre_bwd(x_m, params, cts, tile=256):
    t, w = x_m.shape
    r = min(tile, t)
    nt = t // r
    n_p = len(params)

    def body(*refs):
        x_ref, halo_ref = refs[:2]
        p = [ref[...] for ref in refs[2:2 + n_p]]
        ct = [ref[...] for ref in refs[2 + n_p:7 + n_p]]
        dx_ref = refs[7 + n_p]
        dp_refs = refs[8 + n_p:8 + 2 * n_p]
        xs_ref, ds_ref, carry_ref = refs[8 + 2 * n_p:]
        step = pl.program_id(0)

        @pl.when(step == 0)
        def _():
            ds_ref[...] = jnp.zeros_like(ds_ref)
            carry_ref[...] = jnp.zeros_like(carry_ref)

        _, vjp = jax.vjp(_ml_pre, *_delayed(xs_ref, x_ref, halo_ref, r, step == nt - 1), *p)
        grads = vjp(tuple(ct))
        for j in range(CONV_K):
            ds_ref[j, HALO:HALO + r, :] = grads[j]
        lead = HALO + CONV_K - 1
        d_tile = sum(ds_ref[j, pl.ds(lead - j, r), :] for j in range(CONV_K))
        d_halo = sum(ds_ref[j, pl.ds(CONV_K - 1 - j, HALO), :] for j in range(CONV_K))
        dx_ref[...] = jnp.concatenate([d_tile[:r - HALO], d_tile[r - HALO:] + carry_ref[...]], axis=0).astype(dx_ref.dtype)
        carry_ref[...] = d_halo
        _accumulate(step, dp_refs, grads[CONV_K:])

    row = pl.BlockSpec((r, w), lambda i: (nt - 1 - i, 0))
    return pl.pallas_call(
        body, name="ml_pre_bwd", grid=(nt,),
        in_specs=[row, _halo_spec(r, w, lambda i: nt - 1 - i)] + [_full_spec(p.shape) for p in params]
        + [row] * 4 + [pl.BlockSpec((r, LANES), lambda i: (nt - 1 - i, 0))],
        out_specs=[row] + [_full_spec(p.shape) for p in params],
        out_shape=[jax.ShapeDtypeStruct((t, w), BF16)] + [jax.ShapeDtypeStruct(p.shape, F32) for p in params],
        scratch_shapes=[pltpu.VMEM((r + HALO, w), F32), pltpu.VMEM((CONV_K, r + 2 * HALO, w), F32), pltpu.VMEM((HALO, w), F32)],
        compiler_params=_cparams("arbitrary"),
    )(x_m, x_m, *params, *cts)


def _per_head(fn, row_vals, head_params, shared_params=()):
    return [fn(*[a[:, hs] for a in row_vals], *[p[:, hs] for p in head_params], *shared_params) for hs in _head_slices(HEAD_W)]


def _gla_out(o, g, gn):
    return _rms(o, gn) * (g * _sigmoid(g))


def _ml_out(hc, op, xc, g, sk):
    hcell = hc * _sigmoid(op)
    mu = jnp.mean(hcell, axis=-1, keepdims=True)
    d = hcell - mu
    var = jnp.mean(d * d, axis=-1, keepdims=True)
    return d * lax.rsqrt(var + EPS) * g + sk * xc


def _log_decay(al, w, b):
    return _log_sigmoid(_bdot(al, w, "nn") + b) * (1.0 / GLA_GATE_NORM)


def _merge(ga, gb, ya, yb):
    ga, gb, ya, yb = (a.astype(F32) for a in (ga, gb, ya, yb))
    return _sigmoid(ga) * ya + _sigmoid(gb) * yb


def _post_mix(x, z, gpm, gpl):
    x1 = x + _rms(z, gpm)
    return x1, _rms(x1, gpl)


def _loss_rows(x1, dn, tgt, g):
    e = x1 + _rms(dn, g) - tgt
    return 0.5 * jnp.sum(jnp.mean(e * e, axis=-1, keepdims=True), axis=0, keepdims=True)


def _lin(p):
    return 4 * p[0] + 2 * p[1] + p[2]


def _me():
    return lax.axis_index("x"), lax.axis_index("y"), lax.axis_index("c")


def _flip(p, k):
    return tuple((1 - v) if (k >> (2 - i)) & 1 else v for i, v in enumerate(p))


ANY = pl.BlockSpec(memory_space=pl.ANY)


HBM = pl.BlockSpec(memory_space=pltpu.HBM)
SEM = pl.BlockSpec(memory_space=pltpu.SEMAPHORE)
DATAFLOW = pltpu.SideEffectType.DATAFLOW_SIDE_EFFECTING


SIBLING = 1
OTHER_CHIPS = (2, 4, 6)


def _peer_copies(kinds, srcs, lands, send_sems, recv_sems):
    me = _me()
    copies = []
    for a, (kind, src, land) in enumerate(zip(kinds, srcs, lands)):
        masks = {"gather": range(1, N_DEV), "exchange": range(1, N_DEV), "gather_chips": (SIBLING, *OTHER_CHIPS),
                 "gather_pass": OTHER_CHIPS}[kind]
        for k in masks:
            peer = _flip(me, k)
            if kind == "gather_pass":
                block = land.at[_lin(peer)]
                src_ref, dst_ref, target = block, block, _flip(me, SIBLING)
            else:
                src_ref, dst_ref, target = (src.at[_lin(peer)] if kind == "exchange" else src), land.at[_lin(me)], peer
            copies.append(pltpu.make_async_remote_copy(
                src_ref=src_ref, dst_ref=dst_ref, send_sem=send_sems.at[a * 7 + k - 1], recv_sem=recv_sems.at[a * 7 + k - 1],
                device_id=target, device_id_type=MESH))
    return copies


def _own_copies(kinds, srcs, lands, own_sems):
    return [pltpu.make_async_copy(src, land.at[_lin(_me())], own_sems.at[a])
            for a, (kind, src, land) in enumerate(zip(kinds, srcs, lands)) if kind in ("gather", "gather_chips")]


def _copies_start(kind, srcs, name, after=None, lands=None):
    n = len(srcs)
    extra = [] if after is None else [after]
    kind = [kind] * n if isinstance(kind, str) else list(kind)
    land_shapes = [(s.shape if k == "exchange" else (N_DEV, *s.shape)) for k, s in zip(kind, srcs)]
    lands = [lax.empty(ls, s.dtype) for ls, s in zip(land_shapes, srcs)] if lands is None else lands

    def body(*refs):
        sems = refs[2 * n + len(extra):]
        for cp in _peer_copies(kind, refs[:n], refs[n:2 * n], sems[0], sems[1]) + _own_copies(kind, refs[:n], refs[n:2 * n], sems[2]):
            cp.start()
        refs[-1][...] = jnp.zeros_like(refs[-1])

    def hbm(a):
        return pltpu.with_memory_space_constraint(a, pltpu.HBM)

    out = pl.pallas_call(
        body, name=name,
        out_shape=(pltpu.SemaphoreType.DMA((7 * n,)), pltpu.SemaphoreType.DMA((7 * n,)), pltpu.SemaphoreType.DMA((n,)),
                   *[pltpu.HBM(s.shape, s.dtype) for s in srcs],
                   *[pltpu.HBM(ls, s.dtype) for ls, s in zip(land_shapes, srcs)],
                   jax.ShapeDtypeStruct((8, LANES), F32)),
        in_specs=[HBM] * (2 * n) + [ANY] * len(extra),
        out_specs=(SEM, SEM, SEM, *[HBM] * (2 * n), pl.BlockSpec(memory_space=pltpu.VMEM)),
        input_output_aliases={i: 3 + i for i in range(2 * n)},
        compiler_params=pltpu.CompilerParams(has_side_effects=DATAFLOW),
    )(*[hbm(s) for s in srcs], *[hbm(a) for a in lands], *extra)
    return (kind, n, out[:-1]), out[-1]


def _copies_wait(state, after, name):
    kind, n, (send_sems, recv_sems, own_sems, *thru) = state
    after = list(after) if isinstance(after, (list, tuple)) else [after]

    def body(*refs):
        for cp in _peer_copies(kind, refs[:n], refs[n:2 * n], refs[2 * n], refs[2 * n + 1]):
            cp.wait_send()
            cp.wait_recv()
        for cp in _own_copies(kind, refs[:n], refs[n:2 * n], refs[2 * n + 2]):
            cp.wait()

    out = pl.pallas_call(
        body, name=name,
        out_shape=tuple(pltpu.HBM(t.shape, t.dtype) for t in thru),
        in_specs=[HBM] * (2 * n) + [SEM, SEM, SEM] + [ANY] * len(after), out_specs=tuple([HBM] * (2 * n)),
        input_output_aliases={i: i for i in range(2 * n)},
        compiler_params=pltpu.CompilerParams(has_side_effects=DATAFLOW),
    )(*thru, send_sems, recv_sems, own_sems, *after)
    return out[:n], out[n:]


def _adamw(w, g, m, v):
    m2 = ADAM_B1 * m + (1.0 - ADAM_B1) * g
    v2 = ADAM_B2 * v + (1.0 - ADAM_B2) * (g * g)
    m_hat = m2 / (1.0 - ADAM_B1 ** ADAM_STEP)
    v_hat = v2 / (1.0 - ADAM_B2 ** ADAM_STEP)
    delta = -ADAM_LR * (m_hat / (jnp.sqrt(v_hat) + ADAM_EPS) + ADAM_WD * w)
    return delta, m2, v2


def _sum_adamw(lands, parts, me_idx, w, m, v, name, tile=256):
    r, c = w.shape
    nchunks = len(lands)
    tr = min(tile, r // nchunks)
    per_chunk = r // nchunks // tr
    per = 1 + N_DEV

    def body(me_ref, *refs):
        w_ref, m_ref, v_ref, g_ref, d_ref, m2_ref, v2_ref = refs[nchunks * per:]
        for k in range(nchunks):
            own_ref, slots = refs[k * per], refs[k * per + 1:(k + 1) * per]

            @pl.when(pl.program_id(0) // per_chunk == k)
            def _(own_ref=own_ref, slots=slots):
                own = own_ref[...].astype(F32)
                g = None
                for s in range(N_DEV):
                    term = jnp.where(me_ref[0] == s, own, slots[s][...].astype(F32))
                    g = term if g is None else g + term
                d, m2, v2 = _adamw(w_ref[...], g, m_ref[...], v_ref[...])
                g_ref[...] = g
                d_ref[...] = d
                m2_ref[...] = m2
                v2_ref[...] = v2

    def chunk_specs(k):
        def tile_of(i):
            return jnp.clip(i - k * per_chunk, 0, per_chunk - 1)

        def slot_spec(s):
            return pl.BlockSpec((None, tr, c), lambda i, me: (jnp.where(me[0] == s, (s + 1) % N_DEV, s), tile_of(i), 0))
        return [pl.BlockSpec((None, tr, c), lambda i, me: (me[0], tile_of(i), 0))] + [slot_spec(s) for s in range(N_DEV)]

    row = pl.BlockSpec((tr, c), lambda i, me: (i, 0))
    operands = [a for land, part in zip(lands, parts) for a in (part, *[land] * N_DEV)]
    return pl.pallas_call(
        body, name=name,
        grid_spec=pltpu.PrefetchScalarGridSpec(
            num_scalar_prefetch=1, grid=(r // tr,),
            in_specs=[s for k in range(nchunks) for s in chunk_specs(k)] + [row] * 3,
            out_specs=[row] * 4),
        out_shape=[jax.ShapeDtypeStruct((r, c), F32)] * 4,
        compiler_params=_cparams("parallel"),
    )(me_idx, *operands, w, m, v)


def _cols_view(a):
    r, c = a.shape
    return jnp.transpose(a.reshape(r // LANES, LANES, c), (2, 0, 1))


def _from_cols_view(a, r, c):
    return jnp.transpose(a, (1, 2, 0)).reshape(r, c)


def _sum_adamw_cols(lands, parts, me_idx, w, m, v, name):
    nchunks = len(lands)
    rows_k, c = lands[0].shape[1:]
    r = nchunks * rows_k
    steps = r // LANES
    per_chunk = rows_k // LANES
    per = 1 + N_DEV
    c_pad = _round_up(c, LANES)
    assert steps == 8
    block = steps * max(n for n in range(1, 65) if c % n == 0)

    def body(me_ref, *refs):
        w_ref, m_ref, v_ref, g_ref, d_ref, m2_ref, v2_ref, gt_ref = refs[nchunks * per:]
        for i in range(steps):
            k = i // per_chunk
            own_ref, slots = refs[k * per], refs[k * per + 1:(k + 1) * per]

            @pl.when(pl.program_id(0) == i)
            def _(i=i, own_ref=own_ref, slots=slots):
                own = own_ref[...].astype(F32)
                g = None
                for s in range(N_DEV):
                    term = jnp.where(me_ref[0] == s, own, slots[s][...].astype(F32))
                    g = term if g is None else g + term
                g = jnp.concatenate([g, jnp.zeros((LANES, c_pad - c), F32)], axis=1)
                for j in range(c_pad // LANES):
                    cols = min(LANES, c - j * LANES)
                    gt_ref[pl.ds(steps * LANES * j + i, cols, stride=steps), :] = jnp.transpose(g[:, j * LANES:(j + 1) * LANES])[0:cols]

        @pl.when(pl.program_id(0) == steps - 1)
        def _():
            def update(b, carry):
                cols = pl.ds(b * (block // steps), block // steps)
                g = gt_ref[pl.ds(pl.multiple_of(b * block, steps), block), :].reshape(block // steps, steps, LANES)
                d, m2, v2 = _adamw(w_ref[cols], g, m_ref[cols], v_ref[cols])
                g_ref[cols] = g
                d_ref[cols] = d
                m2_ref[cols] = m2
                v2_ref[cols] = v2
                return carry
            lax.fori_loop(0, c * steps // block, update, 0)

    def chunk_specs(k):
        def tile_of(i):
            return jnp.clip(i - k * per_chunk, 0, per_chunk - 1)

        def slot_spec(s):
            return pl.BlockSpec((None, LANES, c), lambda i, me: (jnp.where(me[0] == s, (s + 1) % N_DEV, s), tile_of(i), 0))
        return [pl.BlockSpec((None, LANES, c), lambda i, me: (me[0], tile_of(i), 0))] + [slot_spec(s) for s in range(N_DEV)]

    whole = pl.BlockSpec((c, steps, LANES), lambda i, me: (0, 0, 0))
    operands = [a for land, part in zip(lands, parts) for a in (part, *[land] * N_DEV)]
    return pl.pallas_call(
        body, name=name,
        grid_spec=pltpu.PrefetchScalarGridSpec(
            num_scalar_prefetch=1, grid=(steps,),
            in_specs=[s for k in range(nchunks) for s in chunk_specs(k)]
            + [pl.BlockSpec((c, steps, LANES), lambda i, me: (0, 0, 0), pipeline_mode=pl.Buffered(1))] * 3,
            out_specs=[whole] * 4,
            scratch_shapes=[pltpu.VMEM((c * steps, LANES), F32)]),
        out_shape=[jax.ShapeDtypeStruct((c, steps, LANES), F32)] * 4,
        compiler_params=_cparams("arbitrary"),
    )(me_idx, *operands, w, m, v)


def _small_update(name, me_idx, kinds, lands, owns, ws, ms, vs, sums=()):
    n = len(ws)
    lands, owns = list(lands) + [s[0] for s in sums], list(owns) + [s[1] for s in sums]
    kinds = list(kinds) + ["gather"] * len(sums)
    nl = len(lands)

    def summed(me, land_ref, own):
        g = None
        for s in range(N_DEV):
            term = jnp.where(me == s, own, land_ref[s]).astype(F32)
            g = term if g is None else g + term
        return g

    def body(me_ref, *refs):
        land_refs, own_refs = refs[:nl], refs[nl:2 * nl]
        w_refs, m_refs, v_refs = (refs[2 * nl + i * n:2 * nl + (i + 1) * n] for i in range(3))
        outs = refs[2 * nl + 3 * n:]
        me = me_ref[0]
        for i in range(n):
            g = summed(me, land_refs[i], own_refs[i][...])
            d, m2, v2 = _adamw(w_refs[i][...], g, m_refs[i][...], v_refs[i][...])
            for ref, val in zip(outs[4 * i:4 * i + 4], (g, d, m2, v2)):
                ref[...] = val
        for i in range(n, nl):
            outs[4 * n + i - n][...] = summed(me, land_refs[i], own_refs[i][...])

    def whole(shape):
        return pl.BlockSpec(shape, lambda i, me, nd=len(shape): (0,) * nd)

    def own_spec(kind, own):
        if kind == "gather":
            return whole(own.shape)
        return pl.BlockSpec((None, *own.shape[1:]), lambda i, me: (me[0], 0, 0))

    shapes = [w.shape for w in ws]
    out_shapes = [s for s in shapes for _ in range(4)] + [s[1].shape for s in sums]
    return pl.pallas_call(
        body, name=name,
        grid_spec=pltpu.PrefetchScalarGridSpec(
            num_scalar_prefetch=1, grid=(1,),
            in_specs=[whole(a.shape) for a in lands] + [own_spec(k, o) for k, o in zip(kinds, owns)]
            + [whole(s) for s in shapes] * 3,
            out_specs=[whole(s) for s in out_shapes]),
        out_shape=[jax.ShapeDtypeStruct(s, F32) for s in out_shapes],
        compiler_params=_cparams("arbitrary"),
    )(me_idx, *lands, *owns, *ws, *ms, *vs)


def _small_view(n, a):
    if a.ndim == 1:
        return a.reshape(1, -1)
    if a.ndim == 3:
        return a.transpose(1, 2, 0).reshape(QKV_BLOCK * QKV_BLOCK, -1)
    return a.T if n == "w_if" else a


def _small_unview(n, a, shape):
    if len(shape) == 1:
        return a.reshape(shape)
    if len(shape) == 3:
        return a.reshape(QKV_BLOCK, QKV_BLOCK, -1).transpose(2, 0, 1)
    return a.T if n == "w_if" else a


def _small_shards(n, g):
    if n == "w_if":
        return g.reshape(N_DEV, -1, g.shape[1]).transpose(0, 2, 1)
    return g.reshape(g.shape[0], N_DEV, -1).transpose(1, 0, 2)


def _small_unshard(n, s):
    if n == "w_if":
        return s.transpose(0, 2, 1).reshape(-1, s.shape[1])
    return s.transpose(1, 0, 2).reshape(s.shape[1], -1)


def _to_hm(a, d):
    t = a.shape[0]
    return a.reshape(t, HEADS, d).transpose(1, 0, 2)


def _from_hm(a):
    h, t, d = a.shape
    return a.transpose(1, 0, 2).reshape(t, h * d)


def _gate_rows(g):
    t = g.shape[0]
    return g.T.reshape(HEADS, t // CHUNK, 1, CHUNK)


def _gate_cols(g):
    h, nc, _, c = g.shape
    return g.reshape(h, nc * c).T


def _blockdiag_dense(w):
    n = w.shape[0] * QKV_BLOCK // 2
    tiled = jnp.tile(w.reshape(2, n, QKV_BLOCK), (1, 1, n // QKV_BLOCK))
    r = lax.broadcasted_iota(jnp.int32, (2, n, n), 1)
    c = lax.broadcasted_iota(jnp.int32, (2, n, n), 2)
    return jnp.where(r // QKV_BLOCK == c // QKV_BLOCK, tiled, 0.0)


def _blockdiag_blocks(dense):
    _, n, _ = dense[0].shape
    k = len(dense)

    def body(*refs):
        r = lax.broadcasted_iota(jnp.int32, (n, n), 0)
        c = lax.broadcasted_iota(jnp.int32, (n, n), 1)
        fr = lax.broadcasted_iota(jnp.int32, (n, LANES), 0)
        fc = lax.broadcasted_iota(jnp.int32, (n, LANES), 1)
        fold = ((fr & (QKV_BLOCK - 1)) == fc).astype(BF16)
        for i in range(k):
            for half in range(2):
                kept = jnp.where((r >> 2) == (c >> 2), refs[i][half], 0.0)
                refs[k + i][half] = sum(lax.dot_general(t, fold, _dims("nn", 2), preferred_element_type=F32)
                                        for t in _split3(kept))

    out = pl.pallas_call(body, name="blockdiag_blocks", out_shape=[jax.ShapeDtypeStruct((2, n, LANES), F32)] * k)(*dense)
    return [o[:, :, 0:QKV_BLOCK].reshape(2 * n // QKV_BLOCK, QKV_BLOCK, QKV_BLOCK) for o in out]


def _col_blocks(w):
    k, n = w.shape
    return w.reshape(k, N_DEV, n // N_DEV).transpose(1, 0, 2)


def _from_col_blocks(g):
    d, k, n = g.shape
    return g.transpose(1, 0, 2).reshape(k, d * n)


def _first_norm(x, g):
    return _rowwise("pre_mix_norm", lambda xv, gv: ((_rms(xv, gv),), ()), [x], [g], [(x.shape[1], BF16)])[0]


def _local_step(x, h, tgt, weight, ws, prefetch, pass_on, on_grads, on_small):
    t, d = x.shape
    g1 = ws["g_pre_mix"]

    def dep(token):
        return () if token is None else (token,)

    w_in = weight("w_in", x)
    fetch_mix = prefetch(("w_pa", "w_pb", "w_o"), w_in)
    fetch_up = prefetch(("w_up", "w_down"), fetch_mix)

    n_in = w_in.shape[2]

    offs = [0]
    for s in IN_SPLITS:
        offs.append(offs[-1] + s)

    w_a_up_p = jnp.pad(ws["w_a_up"], ((0, LANES - LOWRANK), (0, 0)))
    b_a_up = ws["b_a_up"]

    def proj_in_fwd(hv, w, wa, ba):
        proj = jnp.concatenate([_raw_dot(hv, w[j], "nn") for j in range(N_DEV)], axis=1)
        parts = [proj[:, offs[i]:offs[i + 1]] for i in range(len(IN_SPLITS))]
        parts[4] = jnp.concatenate([parts[4], jnp.zeros((parts[4].shape[0], LANES - LOWRANK), F32)], axis=1)
        return (*parts, _log_decay(parts[4], wa, ba)), ()

    widths = [LANES if s == LOWRANK else s for s in IN_SPLITS]
    q_a, k_a, v_a, g_a, a_low_p, x_m, o_pre, gate_a, gate_b, la = _rowwise(
        "proj_in", proj_in_fwd, [h], [w_in, w_a_up_p, b_a_up],
        [(wd, BF16 if i == 2 else F32) for i, wd in enumerate(widths)] + [(HEADS * GLA_DK, F32)],
        deps=dep(fetch_up))

    q_hm, k_hm, la_hm = _to_hm(q_a, GLA_DK), _to_hm(k_a, GLA_DK), _to_hm(la, GLA_DK)
    gn = ws["g_gla_norm"]
    ml_w = HEADS * HEAD_W

    cw = ws["conv_w"]
    w_if_p = jnp.pad(ws["w_if"], ((0, 0), (0, LANES - 2 * HEADS)))
    pre_params = [cw[0:1], cw[1:2], cw[2:3], cw[3:4], ws["conv_b"],
                  _blockdiag_dense(ws["w_q_ml"]), _blockdiag_dense(ws["w_k_ml"]), _blockdiag_dense(ws["w_v_ml"]),
                  w_if_p[0:ml_w], w_if_p[ml_w:2 * ml_w], w_if_p[2 * ml_w:3 * ml_w],
                  jnp.pad(ws["b_if"], ((0, 0), (0, LANES - 2 * HEADS)))]
    xc, q_m, k_m, v_m, gl = _ml_pre_fwd(x_m, pre_params, tile=512)
    pass_mix = pass_on("w_pa", xc)
    li, lf = _gate_rows(gl[:, 0:HEADS]), _gate_rows(gl[:, HEADS:2 * HEADS])
    o_gla, s_prev, hc, c_prev, n_prev, m_prev = _recurrences_fwd(q_hm, k_hm, v_a, la_hm, q_m, k_m, v_m, li, lf,
                                                                 deps=dep(pass_mix))
    pass_up = pass_on("w_up", hc)
    g_ml, skip = ws["g_ml_norm"], ws["ml_skip"]

    def branches_out(o, g, a, b, c_, ga, gb, n_, wa, gm, s, wb):
        ya_in = jnp.concatenate(_per_head(_gla_out, [o, g], [], [n_]), axis=1)
        ya = _raw_dot(ya_in, wa, "nn")
        hb = jnp.concatenate(_per_head(_ml_out, [a, b, c_], [gm, s]), axis=1)
        yb = _raw_dot(hb, wb, "nn")
        return (ya_in, ya, hb, yb, _merge(ga, gb, ya, yb)), ()

    ya_in, y_a, h_b, y_b, merged = _rowwise(
        "branches_out", branches_out, [o_gla, g_a, hc, o_pre, xc, gate_a, gate_b],
        [gn, weight("w_pa", hc), g_ml, skip, weight("w_pb", hc)],
        [(ml_w, BF16), (d, BF16), (ml_w, BF16), (d, BF16), (d, BF16)], tile=512, deps=dep(pass_up))

    gpm, gpl, gpo = ws["g_post_mix"], ws["g_pre_mlp"], ws["g_post_mlp"]

    def proj_o_fwd(mg, xv, w, a, b):
        zv = _raw_dot(mg, w, "nn")
        return (zv, *_post_mix(xv, zv, a, b)), ()

    z, x1, h2 = _rowwise("proj_o", proj_o_fwd, [merged, x], [weight("w_o", merged), gpm, gpl],
                         [(d, F32), (d, F32), (d, BF16)], tile=512)
    w_up = weight("w_up", h2)
    w_down = weight("w_down", h2)
    d_ff = w_down.shape[0]

    def mlp_loss(h2v, x1v, tgtv, wu, wd, g):
        upv = jnp.concatenate([_raw_dot(h2v, wu[j], "nn") for j in range(wu.shape[0])], axis=1)
        uv = jnp.square(jnp.maximum(upv, 0.0))
        dnv = _raw_dot(uv, wd, "nn")
        loss, vjp = jax.vjp(lambda a, b, c_: _loss_rows(a, b, tgtv, c_), x1v, dnv, g)
        dx1, ddn, dg = vjp(jnp.ones((1, 1), F32))
        return (upv, uv, dx1, ddn), (jnp.broadcast_to(loss, (1, LANES)), dg)

    up, u, dx1_y, d_dn, loss, d_gpo = _rowwise("mlp_loss", mlp_loss, [h2, x1, tgt], [w_up, w_down, gpo],
                                               [(d_ff, BF16), (d_ff, BF16), (d, F32), (d, BF16)],
                                               [((1, LANES), F32), ((1, d), F32)])

    dw_down = _mm(u, d_dn, "tn", BF16, "mlp_down_dw", tm=512)

    def mlp_dx(ddn, upv, xv, zv, dx1, wd, wu, a, b):
        dup = (_raw_dot(ddn, wd, "nt") * (2.0 * jnp.maximum(upv.astype(F32), 0.0))).astype(BF16)
        ns = wu.shape[2]
        dh2 = sum(_raw_dot(dup[:, j * ns:(j + 1) * ns], wu[j], "nt") for j in range(wu.shape[0]))
        _, vjp = jax.vjp(_post_mix, xv, zv, a, b)
        dx, dz, da, db = vjp((dx1, dh2))
        return (dup, dx, dz), (da, db)

    d_up, dx_res, d_z, d_gpm, d_gpl = _rowwise("mlp_dx", mlp_dx, [d_dn, up, x, z, dx1_y], [w_down, w_up, gpm, gpl],
                                               [(d_ff, BF16), (d, F32), (d, BF16)], [((1, d), F32), ((1, d), F32)])
    dw_up = _mm_shard_cols(h2, [d_up], [d_up.shape[1]], w_up.shape[2], "mlp_up_dw", 0, d)
    sent_mlp = on_grads(dict(w_down=dw_down, w_up=dw_up))

    def branches_out_bwd(dz, ga, gb, ya, yb, o, g, a, b, c_, wo, wa, n_, wb, gm, s):
        d_ga_, d_gb_, d_ya_, d_yb_ = jax.vjp(_merge, ga, gb, ya, yb)[1](_raw_dot(dz, wo, "nt"))
        ct_a, ct_b = _raw_dot(d_ya_, wa, "nt"), _raw_dot(d_yb_, wb, "nt")
        parts_a, parts_b = [], []
        for hs in _head_slices(HEAD_W):
            parts_a.append(jax.vjp(_gla_out, o[:, hs], g[:, hs], n_)[1](ct_a[:, hs]))
            parts_b.append(jax.vjp(_ml_out, a[:, hs], b[:, hs], c_[:, hs], gm[:, hs], s[:, hs])[1](ct_b[:, hs]))
        cat = lambda parts, i: jnp.concatenate([p[i] for p in parts], axis=1)
        return ((d_ga_, d_gb_, d_ya_, d_yb_, cat(parts_a, 0), cat(parts_a, 1), cat(parts_b, 0), cat(parts_b, 1), cat(parts_b, 2)),
                (sum(p[2] for p in parts_a), cat(parts_b, 3), cat(parts_b, 4)))

    d_ga, d_gb, d_ya, d_yb, d_o, d_g_a, d_hc, d_opre, d_xc, d_gn, d_gml, d_skip = _rowwise(
        "branches_out_bwd", branches_out_bwd, [d_z, gate_a, gate_b, y_a, y_b, o_gla, g_a, hc, o_pre, xc],
        [weight("w_o", merged), weight("w_pa", hc), gn, weight("w_pb", hc), g_ml, skip],
        [(d, BF16)] * 4 + [(ml_w, F32), (ml_w, BF16), (ml_w, F32), (ml_w, BF16), (ml_w, F32)],
        [((1, HEAD_W), F32), ((1, ml_w), F32), ((1, ml_w), F32)], deps=dep(sent_mlp))
    dw_o, dw_pa, dw_pb = _mm_tn_whole([(merged, d_z), (ya_in, d_ya), (h_b, d_yb)], "mix_dw")
    sent_mix = on_grads(dict(w_o=dw_o, w_pa=dw_pa, w_pb=dw_pb))

    dq_hm, dk_hm, d_va, dla_hm, d_qm, d_km, d_vm, d_li, d_lf = _recurrences_bwd(
        q_hm, k_hm, v_a, la_hm, s_prev, d_o, q_m, k_m, v_m, li, lf, c_prev, n_prev, m_prev, d_hc, deps=dep(sent_mix))
    d_gl = jnp.concatenate([_gate_cols(d_li), _gate_cols(d_lf), jnp.zeros((t, LANES - 2 * HEADS), F32)], axis=1)
    pre_grads = _ml_pre_bwd(x_m, pre_params, [d_xc, d_qm, d_km, d_vm, d_gl], tile=512)
    d_xm = pre_grads[0]
    d_cw = jnp.concatenate(pre_grads[1:5], axis=0)
    d_cb = pre_grads[5]
    d_wq, d_wk, d_wv = _blockdiag_blocks(pre_grads[6:9])
    d_wif = jnp.concatenate(pre_grads[9:12], axis=0)[:, 0:2 * HEADS]
    d_bif = pre_grads[12][:, 0:2 * HEADS]

    def decay_bwd(al, ct, w, b):
        _, vjp = jax.vjp(_log_decay, al, w, b)
        dal, dw, db = vjp(ct)
        return (dal,), (dw, db)

    d_alow_p, d_wa_p, d_ba = _rowwise("gla_decay_bwd", decay_bwd, [a_low_p, _from_hm(dla_hm)], [w_a_up_p, b_a_up],
                                      [(LANES, BF16)], [(w_a_up_p.shape, F32), (b_a_up.shape, F32)])
    d_proj = [jnp.concatenate([_from_hm(dq_hm), _from_hm(dk_hm), d_va, d_g_a], axis=1), d_alow_p,
              jnp.concatenate([d_xm, d_opre, d_ga, d_gb], axis=1)]
    d_widths = [offs[4], LOWRANK, offs[9] - offs[5]]
    d_pieces = _shard_pieces(d_widths, n_in)
    small = dict(w_a_up=d_wa_p[0:LOWRANK], b_a_up=d_ba, g_gla_norm=d_gn, conv_w=d_cw, conv_b=d_cb,
                 w_q_ml=d_wq, w_k_ml=d_wk, w_v_ml=d_wv, w_if=d_wif, b_if=d_bif, ml_skip=d_skip, g_ml_norm=d_gml,
                 g_post_mix=d_gpm, g_pre_mlp=d_gpl, g_post_mlp=d_gpo)
    sent_small = on_small(small, loss)
    sent_in = sent_small
    for half in range(2):
        dw_half = _mm_shard_cols(h, d_proj, d_widths, n_in, "proj_in_dw_%d" % half, half, d // 2, deps=dep(sent_in))
        sent_in = on_grads({"w_in#%d" % half: dw_half})

    def proj_in_dx(dp_a, dp_low, dp_b, xv, dres, w, g):
        dh = 0.0
        for s in range(N_DEV):
            for i, c_in, c_w, wd in d_pieces[s]:
                src = (dp_a, dp_low, dp_b)[i]
                cols = src.shape[1] - c_in if wd < LANES else wd
                dh = dh + _raw_dot(src[:, c_in:c_in + cols], w[s][:, c_w:c_w + cols], "nt")
        _, vjp = jax.vjp(_rms, xv, g)
        dx, dg = vjp(dh)
        return (dx + dres,), (dg,)

    grad_x, d_g1 = _rowwise("proj_in_dx", proj_in_dx, [*d_proj, x, dx_res], [w_in, g1], [(d, F32)], [((1, d), F32)],
                            deps=dep(sent_in))
    return grad_x, on_small(dict(g_pre_mix=d_g1), None)


BIG = ("w_in", "w_pa", "w_pb", "w_o", "w_up", "w_down")
MIX = ("w_o", "w_pa", "w_pb")
BIG_COL_SHARDED = ("w_in", "w_pa", "w_pb", "w_up")
SMALL_SHARDED = ("w_a_up", "conv_w", "w_if")
SMALL = ("g_pre_mix", "w_a_up", "b_a_up", "g_gla_norm", "conv_w", "conv_b", "w_q_ml", "w_k_ml", "w_v_ml", "w_if", "b_if",
         "ml_skip", "g_ml_norm", "g_post_mix", "g_pre_mlp", "g_post_mlp")
WEIGHTS = ("g_pre_mix", "w_in", "w_a_up", "b_a_up", "g_gla_norm", "conv_w", "conv_b", "w_q_ml", "w_k_ml", "w_v_ml", "w_if", "b_if",
           "ml_skip", "g_ml_norm", "w_pa", "w_pb", "w_o", "g_post_mix", "g_pre_mlp", "w_up", "w_down", "g_post_mlp")


def kernel(x, g_pre_mix, w_in, w_a_up, b_a_up, g_gla_norm, conv_w, conv_b, w_q_ml, w_k_ml, w_v_ml, w_if, b_if, ml_skip, g_ml_norm, w_pa, w_pb, w_o, g_post_mix, g_pre_mlp, w_up, w_down, g_post_mlp, loss_target, m_g_pre_mix, m_w_in, m_w_a_up, m_b_a_up, m_g_gla_norm, m_conv_w, m_conv_b, m_w_q_ml, m_w_k_ml, m_w_v_ml, m_w_if, m_b_if, m_ml_skip, m_g_ml_norm, m_w_pa, m_w_pb, m_w_o, m_g_post_mix, m_g_pre_mlp, m_w_up, m_w_down, m_g_post_mlp, v_g_pre_mix, v_w_in, v_w_a_up, v_b_a_up, v_g_gla_norm, v_conv_w, v_conv_b, v_w_q_ml, v_w_k_ml, v_w_v_ml, v_w_if, v_b_if, v_ml_skip, v_g_ml_norm, v_w_pa, v_w_pb, v_w_o, v_g_post_mix, v_g_pre_mlp, v_w_up, v_w_down, v_g_post_mlp):
    args = dict(locals())
    w = {n: args[n][0] for n in WEIGHTS}
    m = {n: args["m_" + n][0] for n in WEIGHTS}
    v = {n: args["v_" + n][0] for n in WEIGHTS}

    me_lin = _lin(_me())
    me_idx = jnp.reshape(me_lin, (1,)).astype(jnp.int32)

    def full_weight(n, g):
        if n in ("w_in", "w_up"):
            return g
        return _from_col_blocks(g) if n in BIG_COL_SHARDED else g.reshape(-1, g.shape[-1])

    def grad_parts(n, g):
        if n.partition("#")[0] in ("w_in", "w_up"):
            return g
        return (_col_blocks(g) if n in BIG_COL_SHARDED else g.reshape(N_DEV, -1, g.shape[-1])).astype(BF16)

    sharded_names = tuple(SMALL_SHARDED)
    narrow = {n: w[n].astype(BF16) for n in BIG}
    ready, pending, passing = {}, {}, {}
    first_state, _ = _copies_start(["gather"] * len(sharded_names) + ["gather_chips"],
                                   [_small_view(n, w[n]) for n in sharded_names] + [narrow["w_in"]], "allgather_start_first")

    def prefetch(group, after):
        state, token = _copies_start("gather_chips", [narrow[n] for n in group], "allgather_start_" + group[0], after)
        for n in group:
            pending[n] = (group, state)
        return token

    def pass_on(n, after):
        group, state = pending[n]
        shards, lands = _copies_wait(state, after, "allgather_wait_" + group[0])
        state, token = _copies_start("gather_pass", shards, "allgather_pass_" + group[0], lands=lands)
        for gn in group:
            passing[gn] = (group, state)
        return token

    def weight(n, after):
        if n not in ready:
            group, state = passing[n]
            _, lands = _copies_wait(state, after, "allgather_passed_" + group[0])
            for gn, land in zip(group, lands):
                ready[gn] = full_weight(gn, land)
        return ready[n]

    h = _first_norm(x[0], w["g_pre_mix"].reshape(1, -1))
    first_own, first_lands = _copies_wait(first_state, [h] + [narrow[n] for n in BIG if n != "w_in"], "allgather_wait_first")
    state, _ = _copies_start("gather_pass", first_own[-1:], "allgather_pass_w_in", lands=first_lands[-1:])
    passing["w_in"] = (("w_in",), state)
    ws = {n: (w[n].reshape(1, -1) if w[n].ndim == 1 else w[n]) for n in SMALL if n not in SMALL_SHARDED}
    for n, land in zip(sharded_names, first_lands):
        ws[n] = _small_unshard(n, land)

    sets, waiting_small = [], []

    def start_set(large, small):
        names = tuple(large)
        s_names, s_kinds, s_srcs = small if small else ((), [], [])
        state, token = _copies_start(s_kinds + ["exchange"] * len(names), s_srcs + [grad_parts(n, large[n]) for n in names],
                                     "exchange_start_" + (names + s_names)[0].replace("#", "_"))
        sets.append((names, s_names, s_kinds, state))
        return token

    def on_grads(grads):
        return start_set(grads, waiting_small.pop() if waiting_small else None)

    def on_small(small, loss):
        names = tuple(small)
        kinds = ["exchange" if n in SMALL_SHARDED else "gather" for n in names]
        srcs = [_small_shards(n, small[n]) if n in SMALL_SHARDED else _small_view(n, small[n]) for n in names]
        if loss is None:
            return start_set({}, (names, kinds, srcs))
        waiting_small.append((names, kinds + ["gather"], srcs + [loss]))
        return None

    grad_x, last_token = _local_step(x[0], h, loss_target[0], weight, ws, prefetch, pass_on, on_grads, on_small)

    out, chunks, sums, updated = {}, {}, [], []

    def finish_set(names, s_names, s_kinds, state, after):
        own, lands = _copies_wait(state, after, "exchange_wait_" + (names + s_names)[0].replace("#", "_"))
        ns = len(s_kinds)
        if s_names:
            k = len(s_names)
            upd = _small_update("adamw_small_" + s_names[0], me_idx, s_kinds[:k], lands[:k], own[:k],
                                *[[_small_view(n, d[n]) for n in s_names] for d in (w, m, v)],
                                sums=list(zip(lands[k:ns], own[k:ns])))
            for i, n in enumerate(s_names):
                out[n] = tuple(_small_unview(n, a, w[n].shape) for a in upd[4 * i:4 * i + 4])
            sums.extend(upd[4 * k:])
        if names == MIX:
            upd = _small_update("adamw_mix", me_idx, ["exchange"] * len(names), lands[ns:], own[ns:],
                                *[[d[n] for n in names] for d in (w, m, v)])
            for i, n in enumerate(names):
                out[n] = tuple(upd[4 * i:4 * i + 4])
            updated.append(upd[1])
            return
        for name, part, land in zip(names, own[ns:], lands[ns:]):
            n, _, chunk = name.partition("#")
            chunks.setdefault(n, []).append((land, part))
            if chunk in ("", "1"):
                got_lands, got_parts = zip(*chunks[n])
                if n == "w_in":
                    upd = _sum_adamw_cols(got_lands, got_parts, me_idx, *[_cols_view(d[n]) for d in (w, m, v)], "adamw_" + n)
                    out[n] = tuple(_from_cols_view(a, *w[n].shape) for a in upd)
                else:
                    out[n] = upd = _sum_adamw(got_lands, got_parts, me_idx, w[n], m[n], v[n], "adamw_" + n)
                updated.append(upd[1])

    for entry in sets:
        finish_set(*entry, [grad_x, last_token] + updated)
    loss_sum = sums[0]

    shaped = lambda a, n: a.reshape(args[n].shape)
    return (loss_sum[0, 0], grad_x[None],
            *[shaped(out[n][0], n) for n in WEIGHTS], *[shaped(out[n][1], n) for n in WEIGHTS],
            *[shaped(out[n][2], n) for n in WEIGHTS], *[shaped(out[n][3], n) for n in WEIGHTS])
```

```python
import functools

import jax
import jax.numpy as jnp
from jax import lax
from jax.experimental import pallas as pl
from jax.experimental.pallas import tpu as pltpu

F32 = jnp.float32
BF16 = jnp.bfloat16
MESH = pl.DeviceIdType.MESH

N_DEV = 8
EPS = 1e-6
CHUNK = 64
CHUNKS_PER_STEP = 8
HEADS = 4
GLA_DK = 64
HEAD_W = 128
GLA_GATE_NORM = 16.0
LOWRANK = 16
CONV_K = 4
QKV_BLOCK = 4
LANES = 128
HALO = 8
IN_SPLITS = (256, 256, 512, 512, 16, 512, 512, 1024, 1024)

ADAM_LR = 0.001
ADAM_B1 = 0.9
ADAM_B2 = 0.999
ADAM_EPS = 1e-08
ADAM_WD = 0.01
ADAM_STEP = 10

VMEM_LIMIT = 56 * 1024 * 1024


def _cparams(*sem):
    return pltpu.CompilerParams(dimension_semantics=sem, vmem_limit_bytes=VMEM_LIMIT)


def _dims(mode, ndim):
    contract = {"nn": ((ndim - 1,), (ndim - 2,)), "nt": ((ndim - 1,), (ndim - 1,)), "tn": ((ndim - 2,), (ndim - 2,))}[mode]
    return contract, (((0,), (0,)) if ndim == 3 else ((), ()))


def _raw_dot(a, b, mode):
    return lax.dot_general(a.astype(BF16), b.astype(BF16), _dims(mode, a.ndim), preferred_element_type=F32)


@functools.partial(jax.custom_vjp, nondiff_argnums=(2,))
def _bdot(a, b, mode):
    return _raw_dot(a, b, mode)


def _bdot_fwd(a, b, mode):
    return _raw_dot(a, b, mode), (a, b)


def _bdot_bwd(mode, res, ct):
    a, b = res
    if mode == "nn":
        da, db = _raw_dot(ct, b, "nt"), _raw_dot(a, ct, "tn")
    elif mode == "nt":
        da, db = _raw_dot(ct, b, "nn"), _raw_dot(ct, a, "tn")
    else:
        da, db = _raw_dot(b, ct, "nt"), _raw_dot(a, ct, "nn")
    return da.astype(a.dtype), db.astype(b.dtype)


_bdot.defvjp(_bdot_fwd, _bdot_bwd)


def _split3(x):
    hi = x.astype(BF16)
    r1 = x - hi.astype(F32)
    mid = r1.astype(BF16)
    return hi, mid, (r1 - mid.astype(F32)).astype(BF16)


def _split_dot(tri, x):
    if x.ndim == 3:
        tri = jnp.broadcast_to(tri, (x.shape[0], *tri.shape))
    return sum(lax.dot_general(tri, t, _dims("nn", x.ndim), preferred_element_type=F32) for t in _split3(x))


def _tri(n, lower):
    r = lax.broadcasted_iota(jnp.int32, (n, n), 0)
    c = lax.broadcasted_iota(jnp.int32, (n, n), 1)
    return ((c <= r) if lower else (c >= r)).astype(BF16)


@jax.custom_vjp
def _cumsum_rows(x):
    return _split_dot(_tri(x.shape[-2], True), x)


def _cumsum_rows_fwd(x):
    return _cumsum_rows(x), None


def _cumsum_rows_bwd(_, ct):
    return (_split_dot(_tri(ct.shape[-2], False), ct),)


_cumsum_rows.defvjp(_cumsum_rows_fwd, _cumsum_rows_bwd)


def _abs(x):
    return jnp.where(x >= 0, x, -x)


def _sigmoid(x):
    return lax.logistic(x)


def _log_sigmoid(x):
    return jnp.minimum(x, 0.0) - jnp.log(1.0 + jnp.exp(-_abs(x)))


def _rms(x, g):
    return x * lax.rsqrt(jnp.mean(x * x, axis=-1, keepdims=True) + EPS) * g


def _head_slices(w):
    return [slice(h * w, (h + 1) * w) for h in range(HEADS)]


def _heads(ref, rows=slice(None)):
    return jnp.stack([ref[rows, hs] for hs in _head_slices(HEAD_W)])


def _put_heads(ref, val, rows=slice(None)):
    for h, hs in enumerate(_head_slices(HEAD_W)):
        ref[rows, hs] = val[h].astype(ref.dtype)


def _tile(dim, want):
    if dim <= want or dim % LANES:
        return dim
    t = want
    while dim % t:
        t -= LANES
    return t


def _mm(a, b, mode, out_dtype, name, tm=1024, tn=1024, tk=4096, epilogue=None, extra=(), deps=(), shards=None):
    if shards == "b":
        assert mode == "nn"
        ns = b.shape[2]
        (m, k), (k2, n) = a.shape, (b.shape[1], b.shape[0] * ns)
        tn = ns
    elif mode == "nn":
        (m, k), (k2, n) = a.shape, b.shape
    elif mode == "nt":
        (m, k), (n, k2) = a.shape, b.shape
    else:
        (k, m), (k2, n) = a.shape, b.shape
    assert k == k2, (name, a.shape, b.shape)
    tm, tn, tk = _tile(m, tm), _tile(n, tn), _tile(k, tk)
    nk = k // tk
    out_dtypes = out_dtype if epilogue else (out_dtype,)
    assert nk == 1 or (out_dtype == F32 and not epilogue), name
    n_in = 2 + len(extra)

    def body(*refs):
        p = _raw_dot(refs[0][...], refs[1][...], mode)
        if nk > 1:
            _accumulate(pl.program_id(2), [refs[n_in + len(deps)]], [p])
            return
        outs = epilogue(p, *[r[...] for r in refs[2:n_in]]) if epilogue else (p,)
        for ref, val in zip(refs[n_in + len(deps):], outs):
            ref[...] = val.astype(ref.dtype)

    a_spec = pl.BlockSpec((tk, tm), lambda i, j, kk: (kk, i)) if mode == "tn" else pl.BlockSpec((tm, tk), lambda i, j, kk: (i, kk))
    if shards == "b":
        b_spec = pl.BlockSpec((None, tk, tn), lambda i, j, kk: (j, kk, 0))
    elif mode == "nt":
        b_spec = pl.BlockSpec((tn, tk), lambda i, j, kk: (j, kk))
    else:
        b_spec = pl.BlockSpec((tk, tn), lambda i, j, kk: (kk, j))
    o_spec = pl.BlockSpec((tm, tn), lambda i, j, kk: (i, j))
    res = pl.pallas_call(
        body, name=name, grid=(m // tm, n // tn, nk),
        in_specs=[a_spec, b_spec] + [o_spec] * len(extra) + [ANY] * len(deps), out_specs=[o_spec] * len(out_dtypes),
        out_shape=[jax.ShapeDtypeStruct((m, n), dt) for dt in out_dtypes],
        compiler_params=_cparams("parallel", "parallel", "arbitrary"),
    )(a, b, *extra, *deps)
    return res if epilogue else res[0]


def _mm_tn_whole(pairs, name):
    def body(*refs):
        for i in range(len(pairs)):
            refs[2 * len(pairs) + i][...] = _raw_dot(refs[2 * i][...], refs[2 * i + 1][...], "tn").astype(BF16)

    return pl.pallas_call(
        body, name=name,
        out_shape=[jax.ShapeDtypeStruct((a.shape[1], b.shape[1]), BF16) for a, b in pairs],
        compiler_params=pltpu.CompilerParams(vmem_limit_bytes=VMEM_LIMIT),
    )(*[x for pair in pairs for x in pair])


def _shard_pieces(widths, n):
    bounds = [0]
    for wd in widths:
        bounds.append(bounds[-1] + wd)
    assert bounds[-1] == N_DEV * n
    return [[(i, max(s * n, b) - b, max(s * n, b) - s * n, min((s + 1) * n, b + wd) - max(s * n, b))
             for i, (b, wd) in enumerate(zip(bounds, widths)) if b < (s + 1) * n and b + wd > s * n]
            for s in range(N_DEV)]


def _mm_shard_cols(a, bs, widths, n, name, row_tile, tm, deps=()):
    t = a.shape[0]
    nb = len(bs)
    pieces = _shard_pieces(widths, n)

    def body(a_ref, *rest):
        b_refs = rest[:nb]
        o_ref, at_ref = rest[nb + len(deps):]
        j = pl.program_id(0)

        @pl.when(j == 0)
        def _():
            at_ref[...] = a_ref[...].astype(BF16).T

        for s in range(N_DEV):
            @pl.when(j == s)
            def _(s=s):
                for i, c_in, c_out, wd in pieces[s]:
                    cols = min(_round_up(wd, LANES), bs[i].shape[1] - c_in) if wd < LANES else wd
                    p = _raw_dot(at_ref[...], b_refs[i][:, c_in:c_in + cols], "nn")
                    o_ref[:, c_out:c_out + wd] = p[:, 0:wd].astype(BF16)

    return pl.pallas_call(
        body, name=name, grid=(N_DEV,),
        in_specs=[pl.BlockSpec((t, tm), lambda j: (0, row_tile))]
        + [pl.BlockSpec(b.shape, lambda j: (0, 0), pipeline_mode=pl.Buffered(1)) for b in bs] + [ANY] * len(deps),
        out_specs=pl.BlockSpec((None, tm, n), lambda j: (j, 0, 0)),
        out_shape=jax.ShapeDtypeStruct((N_DEV, tm, n), BF16),
        scratch_shapes=[pltpu.VMEM((tm, t), BF16)],
        compiler_params=_cparams("arbitrary"),
    )(a, *bs, *deps)


def _round_up(v, m):
    return -(-v // m) * m


def _rowwise(name, fn, rows, params, out_rows, out_accs=(), tile=256, deps=()):
    t = rows[0].shape[0]
    r = min(tile, t)
    assert t % r == 0
    n_in, n_or = len(rows) + len(params), len(out_rows)
    n_all = n_in + len(deps)
    params = list(params) + list(deps)

    def body(*refs):
        vals = [ref[...] for ref in refs[:n_in]]
        outs = refs[n_all:]
        ro, ao = fn(*vals)
        for ref, v in zip(outs[:n_or], ro):
            ref[...] = v.astype(ref.dtype)
        if out_accs:
            _accumulate(pl.program_id(0), outs[n_or:], ao)

    def full(shape, **kw):
        return pl.BlockSpec(shape, lambda i, nd=len(shape): (0,) * nd, **kw)

    def tiled(w):
        if isinstance(w, tuple):
            return pl.BlockSpec((w[0], r, w[1]), lambda i: (0, i, 0))
        return pl.BlockSpec((r, w), lambda i: (i, 0))

    def whole(w):
        return (w[0], t, w[1]) if isinstance(w, tuple) else (t, w)

    return pl.pallas_call(
        body, name=name, grid=(t // r,),
        in_specs=[tiled(a.shape[1] if a.ndim == 2 else (a.shape[0], a.shape[2])) for a in rows]
        + [full(p.shape, pipeline_mode=pl.Buffered(1)) for p in params],
        out_specs=[tiled(w) for w, _ in out_rows] + [full(s) for s, _ in out_accs],
        out_shape=[jax.ShapeDtypeStruct(whole(w), dt) for w, dt in out_rows] + [jax.ShapeDtypeStruct(s, dt) for s, dt in out_accs],
        compiler_params=_cparams("arbitrary"),
    )(*rows, *params)


def _accumulate(step, refs, vals):
    for ref, v in zip(refs, vals):
        @pl.when(step == 0)
        def _(ref=ref, v=v):
            ref[...] = v.astype(ref.dtype)

        @pl.when(step > 0)
        def _(ref=ref, v=v):
            ref[...] += v.astype(ref.dtype)


def _gla_chunk(q, k, v, la, st):
    c = q.shape[-2]
    row = lax.broadcasted_iota(jnp.int32, (c, c), 0)
    col = lax.broadcasted_iota(jnp.int32, (c, c), 1)
    cum = _cumsum_rows(la)
    cl = jnp.sum(la, axis=-2, keepdims=True)
    ep = jnp.exp(cum)
    en = jnp.exp(-cum)
    qs = q * (GLA_DK ** -0.5)
    qp = qs * ep
    a_f = _bdot(qp, k * en, "nt")
    a_b = _bdot(qs * en, k * ep, "nt")
    sc = jnp.where(row >= col, a_f, a_b)
    o = _bdot(sc, v, "nn") + _bdot(qp, st, "nt")
    kd = k * jnp.exp(cl - cum)
    st_new = st * jnp.exp(cl) + _bdot(v, kd, "tn")
    return o, st_new


def _gla_specs(nc, rev):
    nb = nc // CHUNKS_PER_STEP
    rows = CHUNKS_PER_STEP * CHUNK

    def blk(n):
        return (nb - 1 - n) if rev else n
    hm = pl.BlockSpec((HEADS, rows, GLA_DK), lambda n: (0, blk(n), 0))
    tm = pl.BlockSpec((rows, HEADS * HEAD_W), lambda n: (blk(n), 0))
    st = pl.BlockSpec((HEADS, CHUNKS_PER_STEP, HEAD_W, GLA_DK), lambda n: (0, blk(n), 0, 0))
    return nb, hm, tm, st


def _chunk_rows(c):
    return slice(c * CHUNK, (c + 1) * CHUNK)


def _ml_chunk(q, k, v, li_r, lf_r, cm, nv, m):
    c = q.shape[-2]
    row = lax.broadcasted_iota(jnp.int32, (c, c), 0)
    col = lax.broadcasted_iota(jnp.int32, (c, c), 1)
    eye = (row == col).astype(F32)
    li_c = jnp.sum(eye * li_r, axis=-1, keepdims=True)
    lf_c = jnp.sum(eye * lf_r, axis=-1, keepdims=True)
    fc_c = jnp.sum((col <= row).astype(F32) * lf_r, axis=-1, keepdims=True)
    fc_r = jnp.sum((row <= col).astype(F32) * lf_c, axis=-2, keepdims=True)
    f_last = jnp.sum(lf_r, axis=-1, keepdims=True)
    kc = k * (HEAD_W ** -0.5)
    a_c = f_last - fc_c + li_c
    m_loc = jnp.max(a_c, axis=-2, keepdims=True)
    kw = kc * jnp.exp(a_c - m_loc)
    c_chunk = _bdot(kw, v, "tn")
    n_chunk = jnp.sum(kw, axis=-2, keepdims=True)
    m_new = jnp.maximum(f_last + m, m_loc)
    sp = jnp.exp(f_last + m - m_new)
    sl = jnp.exp(m_loc - m_new)
    cm_new = sp * cm + sl * c_chunk
    nv_new = sp * nv + sl * n_chunk
    log_d = li_r - _abs(fc_c - fc_r)
    g_inter = fc_c + m
    m_t = jnp.maximum(g_inter, jnp.max(log_d, axis=-1, keepdims=True))
    s = _bdot(q, kc, "nt") * jnp.exp(log_d - m_t)
    sc = jnp.exp(g_inter - m_t)
    num = _bdot(s, v, "nn") + sc * _bdot(q, cm, "nn")
    den = jnp.sum(s, axis=-1, keepdims=True) + sc * jnp.sum(q * nv, axis=-1, keepdims=True)
    den = jnp.maximum(_abs(den), jnp.exp(-m_t))
    return num / den, cm_new, nv_new, m_new


def _ml_specs(nc, rev):
    nb = nc // CHUNKS_PER_STEP

    def blk(n):
        return (nb - 1 - n) if rev else n
    tm = pl.BlockSpec((CHUNKS_PER_STEP * CHUNK, HEADS * HEAD_W), lambda n: (blk(n), 0))
    gate = pl.BlockSpec((HEADS, CHUNKS_PER_STEP, 1, CHUNK), lambda n: (0, blk(n), 0, 0))
    cm = pl.BlockSpec((HEADS, CHUNKS_PER_STEP, HEAD_W, HEAD_W), lambda n: (0, blk(n), 0, 0))
    vec = pl.BlockSpec((HEADS, CHUNKS_PER_STEP, 1, HEAD_W), lambda n: (0, blk(n), 0, 0))
    return nb, tm, gate, cm, vec


_ML_STATE = [pltpu.VMEM((HEADS, HEAD_W, HEAD_W), F32), pltpu.VMEM((HEADS, 1, HEAD_W), F32), pltpu.VMEM((HEADS, 1, HEAD_W), F32)]


def _recurrences_fwd(q, k, v, la, qm, km, vm, li, lf, deps=()):
    t = v.shape[0]
    nc = t // CHUNK
    nb, hm, tm, st = _gla_specs(nc, False)
    _, _, gate, cm, vec = _ml_specs(nc, False)
    n_in = 9 + len(deps)

    def body(*refs):
        q_ref, k_ref, v_ref, la_ref, qm_ref, km_ref, vm_ref, li_ref, lf_ref = refs[:9]
        o_ref, sp_ref, hc_ref, cp_ref, np_ref, mp_ref, st_ref, c_ref, n_ref, m_ref = refs[n_in:]

        @pl.when(pl.program_id(0) == 0)
        def _():
            for ref in (st_ref, c_ref, n_ref, m_ref):
                ref[...] = jnp.zeros_like(ref)

        s, cs, ns, ms = st_ref[...], c_ref[...], n_ref[...], m_ref[...][:, :, 0:1]
        for c in range(CHUNKS_PER_STEP):
            r = _chunk_rows(c)
            sp_ref[:, c] = s
            o, s = _gla_chunk(q_ref[:, r], k_ref[:, r], _heads(v_ref, r), la_ref[:, r], s)
            _put_heads(o_ref, o, r)
            cp_ref[:, c] = cs
            np_ref[:, c] = ns
            mp_ref[:, c] = jnp.broadcast_to(ms, m_ref.shape)
            hc, cs, ns, ms = _ml_chunk(_heads(qm_ref, r), _heads(km_ref, r), _heads(vm_ref, r), li_ref[:, c], lf_ref[:, c],
                                       cs, ns, ms)
            _put_heads(hc_ref, hc, r)
        st_ref[...] = s
        c_ref[...] = cs
        n_ref[...] = ns
        m_ref[...] = jnp.broadcast_to(ms, m_ref.shape)

    tm_shape = jax.ShapeDtypeStruct((t, HEADS * HEAD_W), F32)
    vec_shape = jax.ShapeDtypeStruct((HEADS, nc, 1, HEAD_W), F32)
    return pl.pallas_call(
        body, name="recurrences_fwd", grid=(nb,),
        in_specs=[hm, hm, tm, hm, tm, tm, tm, gate, gate] + [ANY] * len(deps), out_specs=[tm, st, tm, cm, vec, vec],
        out_shape=[tm_shape, jax.ShapeDtypeStruct((HEADS, nc, HEAD_W, GLA_DK), F32),
                   tm_shape, jax.ShapeDtypeStruct((HEADS, nc, HEAD_W, HEAD_W), F32), vec_shape, vec_shape],
        scratch_shapes=[pltpu.VMEM((HEADS, HEAD_W, GLA_DK), F32)] + _ML_STATE,
        compiler_params=_cparams("arbitrary"),
    )(q, k, v, la, qm, km, vm, li, lf, *deps)


def _recurrences_bwd(q, k, v, la, sp, do, qm, km, vm, li, lf, cp, npv, mp, dhc, deps=()):
    t = v.shape[0]
    nc = t // CHUNK
    nb, hm, tm, st = _gla_specs(nc, True)
    _, _, gate, cm, vec = _ml_specs(nc, True)
    n_in = 15 + len(deps)

    def body(*refs):
        (q_ref, k_ref, v_ref, la_ref, sp_ref, do_ref,
         qm_ref, km_ref, vm_ref, li_ref, lf_ref, cp_ref, np_ref, mp_ref, dhc_ref) = refs[:15]
        (dq_ref, dk_ref, dv_ref, dla_ref, dqm_ref, dkm_ref, dvm_ref, dli_ref, dlf_ref,
         ds_ref, dc_ref, dn_ref, dm_ref) = refs[n_in:]

        @pl.when(pl.program_id(0) == 0)
        def _():
            for ref in (ds_ref, dc_ref, dn_ref, dm_ref):
                ref[...] = jnp.zeros_like(ref)

        ds, dc, dn, dm = ds_ref[...], dc_ref[...], dn_ref[...], dm_ref[...][:, :, 0:1]
        for c in reversed(range(CHUNKS_PER_STEP)):
            r = _chunk_rows(c)
            _, vjp = jax.vjp(_gla_chunk, q_ref[:, r], k_ref[:, r], _heads(v_ref, r), la_ref[:, r], sp_ref[:, c])
            dq, dk, dv, dla, ds = vjp((_heads(do_ref, r), ds))
            dq_ref[:, r] = dq.astype(dq_ref.dtype)
            dk_ref[:, r] = dk.astype(dk_ref.dtype)
            _put_heads(dv_ref, dv, r)
            dla_ref[:, r] = dla
            _, vjp = jax.vjp(_ml_chunk, _heads(qm_ref, r), _heads(km_ref, r), _heads(vm_ref, r), li_ref[:, c], lf_ref[:, c],
                             cp_ref[:, c], np_ref[:, c], mp_ref[:, c][:, :, 0:1])
            dqm, dkm, dvm, dli, dlf, dc, dn, dm = vjp((_heads(dhc_ref, r), dc, dn, dm))
            _put_heads(dqm_ref, dqm, r)
            _put_heads(dkm_ref, dkm, r)
            _put_heads(dvm_ref, dvm, r)
            dli_ref[:, c] = dli
            dlf_ref[:, c] = dlf
        ds_ref[...] = ds
        dc_ref[...] = dc
        dn_ref[...] = dn
        dm_ref[...] = jnp.broadcast_to(dm, dm_ref.shape)

    hm_shape = jax.ShapeDtypeStruct((HEADS, t, GLA_DK), BF16)
    tm_shape = jax.ShapeDtypeStruct((t, HEADS * HEAD_W), F32)
    gate_shape = jax.ShapeDtypeStruct((HEADS, nc, 1, CHUNK), F32)
    return pl.pallas_call(
        body, name="recurrences_bwd", grid=(nb,),
        in_specs=[hm, hm, tm, hm, st, tm, tm, tm, tm, gate, gate, cm, vec, vec, tm] + [ANY] * len(deps),
        out_specs=[hm, hm, tm, hm, tm, tm, tm, gate, gate],
        out_shape=[hm_shape, hm_shape, jax.ShapeDtypeStruct((t, HEADS * HEAD_W), BF16),
                   jax.ShapeDtypeStruct((HEADS, t, GLA_DK), F32), tm_shape, tm_shape, tm_shape, gate_shape, gate_shape],
        scratch_shapes=[pltpu.VMEM((HEADS, HEAD_W, GLA_DK), F32)] + _ML_STATE,
        compiler_params=_cparams("arbitrary"),
    )(q, k, v, la, sp, do, qm, km, vm, li, lf, cp, npv, mp, dhc, *deps)


@jax.custom_vjp
def _bdot_diag(x, w):
    b = w.shape[1]
    return jnp.concatenate([_raw_dot(x[:, :b], w[0], "nn"), _raw_dot(x[:, b:], w[1], "nn")], axis=1)


def _bdot_diag_fwd(x, w):
    return _bdot_diag(x, w), (x, w)


def _bdot_diag_bwd(res, ct):
    x, w = res
    b = w.shape[1]
    dx = jnp.concatenate([_raw_dot(ct[:, :b], w[0], "nt"), _raw_dot(ct[:, b:], w[1], "nt")], axis=1)
    dw = jnp.stack([_raw_dot(x[:, :b], ct[:, :b], "tn"), _raw_dot(x[:, b:], ct[:, b:], "tn")])
    return dx.astype(x.dtype), dw.astype(w.dtype)


_bdot_diag.defvjp(_bdot_diag_fwd, _bdot_diag_bwd)


def _ml_pre(s0, s1, s2, s3, cw0, cw1, cw2, cw3, cb, wq, wk, wv, wiq, wik, wiv, bif):
    pre = cb + cw0 * s0 + cw1 * s1 + cw2 * s2 + cw3 * s3
    xc = pre * _sigmoid(pre)
    q = _bdot_diag(xc, wq)
    k = _bdot_diag(xc, wk)
    v = _bdot_diag(s3, wv)
    gates = _bdot(q, wiq, "nn") + _bdot(k, wik, "nn") + _bdot(v, wiv, "nn") + bif
    lane = lax.broadcasted_iota(jnp.int32, gates.shape, 1)
    gl = jnp.where(lane < HEADS, gates, _log_sigmoid(gates))
    return xc, q, k, v, gl


def _delayed(xs_ref, x_ref, halo_ref, r, first):
    xs_ref[0:HALO, :] = jnp.where(first, 0.0, halo_ref[...])
    xs_ref[HALO:HALO + r, :] = x_ref[...]
    return [xs_ref[pl.ds(HALO - (CONV_K - 1) + j, r), :] for j in range(CONV_K)]


def _halo_spec(r, w, tile_of):
    return pl.BlockSpec((HALO, w), lambda i: (jnp.maximum(tile_of(i) * (r // HALO) - 1, 0), 0))


def _full_spec(shape):
    return pl.BlockSpec(shape, lambda i, nd=len(shape): (0,) * nd)


def _ml_pre_fwd(x_m, params, tile=256, deps=()):
    t, w = x_m.shape
    r = min(tile, t)

    def body(*refs):
        x_ref, halo_ref = refs[:2]
        p = [ref[...] for ref in refs[2:2 + len(params)]]
        outs = refs[2 + len(params) + len(deps):-1]
        res = _ml_pre(*_delayed(refs[-1], x_ref, halo_ref, r, pl.program_id(0) == 0), *p)
        for ref, val in zip(outs, res):
            ref[...] = val

    row = pl.BlockSpec((r, w), lambda i: (i, 0))
    return pl.pallas_call(
        body, name="ml_pre_fwd", grid=(t // r,),
        in_specs=[row, _halo_spec(r, w, lambda i: i)] + [_full_spec(p.shape) for p in params]
        + [ANY] * len(deps),
        out_specs=[row] * 4 + [pl.BlockSpec((r, LANES), lambda i: (i, 0))],
        out_shape=[jax.ShapeDtypeStruct((t, w), F32)] * 4 + [jax.ShapeDtypeStruct((t, LANES), F32)],
        scratch_shapes=[pltpu.VMEM((r + HALO, w), F32)],
        compiler_params=_cparams("arbitrary"),
    )(x_m, x_m, *params, *deps)


def _ml_pre_bwd(x_m, params, cts, tile=256):
    t, w = x_m.shape
    r = min(tile, t)
    nt = t // r
    n_p = len(params)

    def body(*refs):
        x_ref, halo_ref = refs[:2]
        p = [ref[...] for ref in refs[2:2 + n_p]]
        ct = [ref[...] for ref in refs[2 + n_p:7 + n_p]]
        dx_ref = refs[7 + n_p]
        dp_refs = refs[8 + n_p:8 + 2 * n_p]
        xs_ref, ds_ref, carry_ref = refs[8 + 2 * n_p:]
        step = pl.program_id(0)

        @pl.when(step == 0)
        def _():
            ds_ref[...] = jnp.zeros_like(ds_ref)
            carry_ref[...] = jnp.zeros_like(carry_ref)

        _, vjp = jax.vjp(_ml_pre, *_delayed(xs_ref, x_ref, halo_ref, r, step == nt - 1), *p)
        grads = vjp(tuple(ct))
        for j in range(CONV_K):
            ds_ref[j, HALO:HALO + r, :] = grads[j]
        lead = HALO + CONV_K - 1
        d_tile = sum(ds_ref[j, pl.ds(lead - j, r), :] for j in range(CONV_K))
        d_halo = sum(ds_ref[j, pl.ds(CONV_K - 1 - j, HALO), :] for j in range(CONV_K))
        dx_ref[...] = jnp.concatenate([d_tile[:r - HALO], d_tile[r - HALO:] + carry_ref[...]], axis=0).astype(dx_ref.dtype)
        carry_ref[...] = d_halo
        _accumulate(step, dp_refs, grads[CONV_K:])

    row = pl.BlockSpec((r, w), lambda i: (nt - 1 - i, 0))
    return pl.pallas_call(
        body, name="ml_pre_bwd", grid=(nt,),
        in_specs=[row, _halo_spec(r, w, lambda i: nt - 1 - i)] + [_full_spec(p.shape) for p in params]
        + [row] * 4 + [pl.BlockSpec((r, LANES), lambda i: (nt - 1 - i, 0))],
        out_specs=[row] + [_full_spec(p.shape) for p in params],
        out_shape=[jax.ShapeDtypeStruct((t, w), BF16)] + [jax.ShapeDtypeStruct(p.shape, F32) for p in params],
        scratch_shapes=[pltpu.VMEM((r + HALO, w), F32), pltpu.VMEM((CONV_K, r + 2 * HALO, w), F32), pltpu.VMEM((HALO, w), F32)],
        compiler_params=_cparams("arbitrary"),
    )(x_m, x_m, *params, *cts)


def _per_head(fn, row_vals, head_params, shared_params=()):
    return [fn(*[a[:, hs] for a in row_vals], *[p[:, hs] for p in head_params], *shared_params) for hs in _head_slices(HEAD_W)]


def _gla_out(o, g, gn):
    return _rms(o, gn) * (g * _sigmoid(g))


def _ml_out(hc, op, xc, g, sk):
    hcell = hc * _sigmoid(op)
    mu = jnp.mean(hcell, axis=-1, keepdims=True)
    d = hcell - mu
    var = jnp.mean(d * d, axis=-1, keepdims=True)
    return d * lax.rsqrt(var + EPS) * g + sk * xc


def _log_decay(al, w, b):
    return _log_sigmoid(_bdot(al, w, "nn") + b) * (1.0 / GLA_GATE_NORM)


def _merge(ga, gb, ya, yb):
    ga, gb, ya, yb = (a.astype(F32) for a in (ga, gb, ya, yb))
    return _sigmoid(ga) * ya + _sigmoid(gb) * yb


def _post_mix(x, z, gpm, gpl):
    x1 = x + _rms(z, gpm)
    return x1, _rms(x1, gpl)


def _loss_rows(x1, dn, tgt, g):
    e = x1 + _rms(dn, g) - tgt
    return 0.5 * jnp.sum(jnp.mean(e * e, axis=-1, keepdims=True), axis=0, keepdims=True)


def _lin(p):
    return 4 * p[0] + 2 * p[1] + p[2]


def _me():
    return lax.axis_index("x"), lax.axis_index("y"), lax.axis_index("c")


def _flip(p, k):
    return tuple((1 - v) if (k >> (2 - i)) & 1 else v for i, v in enumerate(p))


ANY = pl.BlockSpec(memory_space=pl.ANY)


HBM = pl.BlockSpec(memory_space=pltpu.HBM)
SEM = pl.BlockSpec(memory_space=pltpu.SEMAPHORE)
DATAFLOW = pltpu.SideEffectType.DATAFLOW_SIDE_EFFECTING


SIBLING = 1
OTHER_CHIPS = (2, 4, 6)


def _peer_copies(kinds, srcs, lands, send_sems, recv_sems):
    me = _me()
    copies = []
    for a, (kind, src, land) in enumerate(zip(kinds, srcs, lands)):
        masks = {"gather": range(1, N_DEV), "exchange": range(1, N_DEV), "gather_chips": (SIBLING, *OTHER_CHIPS),
                 "gather_pass": OTHER_CHIPS}[kind]
        for k in masks:
            peer = _flip(me, k)
            if kind == "gather_pass":
                block = land.at[_lin(peer)]
                src_ref, dst_ref, target = block, block, _flip(me, SIBLING)
            else:
                src_ref, dst_ref, target = (src.at[_lin(peer)] if kind == "exchange" else src), land.at[_lin(me)], peer
            copies.append(pltpu.make_async_remote_copy(
                src_ref=src_ref, dst_ref=dst_ref, send_sem=send_sems.at[a * 7 + k - 1], recv_sem=recv_sems.at[a * 7 + k - 1],
                device_id=target, device_id_type=MESH))
    return copies


def _own_copies(kinds, srcs, lands, own_sems):
    return [pltpu.make_async_copy(src, land.at[_lin(_me())], own_sems.at[a])
            for a, (kind, src, land) in enumerate(zip(kinds, srcs, lands)) if kind in ("gather", "gather_chips")]


def _copies_start(kind, srcs, name, after=None, lands=None):
    n = len(srcs)
    extra = [] if after is None else [after]
    kind = [kind] * n if isinstance(kind, str) else list(kind)
    land_shapes = [(s.shape if k == "exchange" else (N_DEV, *s.shape)) for k, s in zip(kind, srcs)]
    lands = [lax.empty(ls, s.dtype) for ls, s in zip(land_shapes, srcs)] if lands is None else lands

    def body(*refs):
        sems = refs[2 * n + len(extra):]
        for cp in _peer_copies(kind, refs[:n], refs[n:2 * n], sems[0], sems[1]) + _own_copies(kind, refs[:n], refs[n:2 * n], sems[2]):
            cp.start()
        refs[-1][...] = jnp.zeros_like(refs[-1])

    def hbm(a):
        return pltpu.with_memory_space_constraint(a, pltpu.HBM)

    out = pl.pallas_call(
        body, name=name,
        out_shape=(pltpu.SemaphoreType.DMA((7 * n,)), pltpu.SemaphoreType.DMA((7 * n,)), pltpu.SemaphoreType.DMA((n,)),
                   *[pltpu.HBM(s.shape, s.dtype) for s in srcs],
                   *[pltpu.HBM(ls, s.dtype) for ls, s in zip(land_shapes, srcs)],
                   jax.ShapeDtypeStruct((8, LANES), F32)),
        in_specs=[HBM] * (2 * n) + [ANY] * len(extra),
        out_specs=(SEM, SEM, SEM, *[HBM] * (2 * n), pl.BlockSpec(memory_space=pltpu.VMEM)),
        input_output_aliases={i: 3 + i for i in range(2 * n)},
        compiler_params=pltpu.CompilerParams(has_side_effects=DATAFLOW),
    )(*[hbm(s) for s in srcs], *[hbm(a) for a in lands], *extra)
    return (kind, n, out[:-1]), out[-1]


def _copies_wait(state, after, name):
    kind, n, (send_sems, recv_sems, own_sems, *thru) = state
    after = list(after) if isinstance(after, (list, tuple)) else [after]

    def body(*refs):
        for cp in _peer_copies(kind, refs[:n], refs[n:2 * n], refs[2 * n], refs[2 * n + 1]):
            cp.wait_send()
            cp.wait_recv()
        for cp in _own_copies(kind, refs[:n], refs[n:2 * n], refs[2 * n + 2]):
            cp.wait()

    out = pl.pallas_call(
        body, name=name,
        out_shape=tuple(pltpu.HBM(t.shape, t.dtype) for t in thru),
        in_specs=[HBM] * (2 * n) + [SEM, SEM, SEM] + [ANY] * len(after), out_specs=tuple([HBM] * (2 * n)),
        input_output_aliases={i: i for i in range(2 * n)},
        compiler_params=pltpu.CompilerParams(has_side_effects=DATAFLOW),
    )(*thru, send_sems, recv_sems, own_sems, *after)
    return out[:n], out[n:]


def _adamw(w, g, m, v):
    m2 = ADAM_B1 * m + (1.0 - ADAM_B1) * g
    v2 = ADAM_B2 * v + (1.0 - ADAM_B2) * (g * g)
    m_hat = m2 / (1.0 - ADAM_B1 ** ADAM_STEP)
    v_hat = v2 / (1.0 - ADAM_B2 ** ADAM_STEP)
    delta = -ADAM_LR * (m_hat / (jnp.sqrt(v_hat) + ADAM_EPS) + ADAM_WD * w)
    return delta, m2, v2


def _sum_adamw(lands, parts, me_idx, w, m, v, name, tile=256):
    r, c = w.shape
    nchunks = len(lands)
    tr = min(tile, r // nchunks)
    per_chunk = r // nchunks // tr
    per = 1 + N_DEV

    def body(me_ref, *refs):
        w_ref, m_ref, v_ref, g_ref, d_ref, m2_ref, v2_ref = refs[nchunks * per:]
        for k in range(nchunks):
            own_ref, slots = refs[k * per], refs[k * per + 1:(k + 1) * per]

            @pl.when(pl.program_id(0) // per_chunk == k)
            def _(own_ref=own_ref, slots=slots):
                own = own_ref[...].astype(F32)
                g = None
                for s in range(N_DEV):
                    term = jnp.where(me_ref[0] == s, own, slots[s][...].astype(F32))
                    g = term if g is None else g + term
                d, m2, v2 = _adamw(w_ref[...], g, m_ref[...], v_ref[...])
                g_ref[...] = g
                d_ref[...] = d
                m2_ref[...] = m2
                v2_ref[...] = v2

    def chunk_specs(k):
        def tile_of(i):
            return jnp.clip(i - k * per_chunk, 0, per_chunk - 1)

        def slot_spec(s):
            return pl.BlockSpec((None, tr, c), lambda i, me: (jnp.where(me[0] == s, (s + 1) % N_DEV, s), tile_of(i), 0))
        return [pl.BlockSpec((None, tr, c), lambda i, me: (me[0], tile_of(i), 0))] + [slot_spec(s) for s in range(N_DEV)]

    row = pl.BlockSpec((tr, c), lambda i, me: (i, 0))
    operands = [a for land, part in zip(lands, parts) for a in (part, *[land] * N_DEV)]
    return pl.pallas_call(
        body, name=name,
        grid_spec=pltpu.PrefetchScalarGridSpec(
            num_scalar_prefetch=1, grid=(r // tr,),
            in_specs=[s for k in range(nchunks) for s in chunk_specs(k)] + [row] * 3,
            out_specs=[row] * 4),
        out_shape=[jax.ShapeDtypeStruct((r, c), F32)] * 4,
        compiler_params=_cparams("parallel"),
    )(me_idx, *operands, w, m, v)


def _cols_view(a):
    r, c = a.shape
    return jnp.transpose(a.reshape(r // LANES, LANES, c), (2, 0, 1))


def _from_cols_view(a, r, c):
    return jnp.transpose(a, (1, 2, 0)).reshape(r, c)


def _sum_adamw_cols(lands, parts, me_idx, w, m, v, name):
    nchunks = len(lands)
    rows_k, c = lands[0].shape[1:]
    r = nchunks * rows_k
    steps = r // LANES
    per_chunk = rows_k // LANES
    per = 1 + N_DEV
    c_pad = _round_up(c, LANES)
    assert steps == 8
    block = steps * max(n for n in range(1, 65) if c % n == 0)

    def body(me_ref, *refs):
        w_ref, m_ref, v_ref, g_ref, d_ref, m2_ref, v2_ref, gt_ref = refs[nchunks * per:]
        for i in range(steps):
            k = i // per_chunk
            own_ref, slots = refs[k * per], refs[k * per + 1:(k + 1) * per]

            @pl.when(pl.program_id(0) == i)
            def _(i=i, own_ref=own_ref, slots=slots):
                own = own_ref[...].astype(F32)
                g = None
                for s in range(N_DEV):
                    term = jnp.where(me_ref[0] == s, own, slots[s][...].astype(F32))
                    g = term if g is None else g + term
                g = jnp.concatenate([g, jnp.zeros((LANES, c_pad - c), F32)], axis=1)
                for j in range(c_pad // LANES):
                    cols = min(LANES, c - j * LANES)
                    gt_ref[pl.ds(steps * LANES * j + i, cols, stride=steps), :] = jnp.transpose(g[:, j * LANES:(j + 1) * LANES])[0:cols]

        @pl.when(pl.program_id(0) == steps - 1)
        def _():
            def update(b, carry):
                cols = pl.ds(b * (block // steps), block // steps)
                g = gt_ref[pl.ds(pl.multiple_of(b * block, steps), block), :].reshape(block // steps, steps, LANES)
                d, m2, v2 = _adamw(w_ref[cols], g, m_ref[cols], v_ref[cols])
                g_ref[cols] = g
                d_ref[cols] = d
                m2_ref[cols] = m2
                v2_ref[cols] = v2
                return carry
            lax.fori_loop(0, c * steps // block, update, 0)

    def chunk_specs(k):
        def tile_of(i):
            return jnp.clip(i - k * per_chunk, 0, per_chunk - 1)

        def slot_spec(s):
            return pl.BlockSpec((None, LANES, c), lambda i, me: (jnp.where(me[0] == s, (s + 1) % N_DEV, s), tile_of(i), 0))
        return [pl.BlockSpec((None, LANES, c), lambda i, me: (me[0], tile_of(i), 0))] + [slot_spec(s) for s in range(N_DEV)]

    whole = pl.BlockSpec((c, steps, LANES), lambda i, me: (0, 0, 0))
    operands = [a for land, part in zip(lands, parts) for a in (part, *[land] * N_DEV)]
    return pl.pallas_call(
        body, name=name,
        grid_spec=pltpu.PrefetchScalarGridSpec(
            num_scalar_prefetch=1, grid=(steps,),
            in_specs=[s for k in range(nchunks) for s in chunk_specs(k)]
            + [pl.BlockSpec((c, steps, LANES), lambda i, me: (0, 0, 0), pipeline_mode=pl.Buffered(1))] * 3,
            out_specs=[whole] * 4,
            scratch_shapes=[pltpu.VMEM((c * steps, LANES), F32)]),
        out_shape=[jax.ShapeDtypeStruct((c, steps, LANES), F32)] * 4,
        compiler_params=_cparams("arbitrary"),
    )(me_idx, *operands, w, m, v)


def _small_update(name, me_idx, kinds, lands, owns, ws, ms, vs, sums=()):
    n = len(ws)
    lands, owns = list(lands) + [s[0] for s in sums], list(owns) + [s[1] for s in sums]
    kinds = list(kinds) + ["gather"] * len(sums)
    nl = len(lands)

    def summed(me, land_ref, own):
        g = None
        for s in range(N_DEV):
            term = jnp.where(me == s, own, land_ref[s]).astype(F32)
            g = term if g is None else g + term
        return g

    def body(me_ref, *refs):
        land_refs, own_refs = refs[:nl], refs[nl:2 * nl]
        w_refs, m_refs, v_refs = (refs[2 * nl + i * n:2 * nl + (i + 1) * n] for i in range(3))
        outs = refs[2 * nl + 3 * n:]
        me = me_ref[0]
        for i in range(n):
            g = summed(me, land_refs[i], own_refs[i][...])
            d, m2, v2 = _adamw(w_refs[i][...], g, m_refs[i][...], v_refs[i][...])
            for ref, val in zip(outs[4 * i:4 * i + 4], (g, d, m2, v2)):
                ref[...] = val
        for i in range(n, nl):
            outs[4 * n + i - n][...] = summed(me, land_refs[i], own_refs[i][...])

    def whole(shape):
        return pl.BlockSpec(shape, lambda i, me, nd=len(shape): (0,) * nd)

    def own_spec(kind, own):
        if kind == "gather":
            return whole(own.shape)
        return pl.BlockSpec((None, *own.shape[1:]), lambda i, me: (me[0], 0, 0))

    shapes = [w.shape for w in ws]
    out_shapes = [s for s in shapes for _ in range(4)] + [s[1].shape for s in sums]
    return pl.pallas_call(
        body, name=name,
        grid_spec=pltpu.PrefetchScalarGridSpec(
            num_scalar_prefetch=1, grid=(1,),
            in_specs=[whole(a.shape) for a in lands] + [own_spec(k, o) for k, o in zip(kinds, owns)]
            + [whole(s) for s in shapes] * 3,
            out_specs=[whole(s) for s in out_shapes]),
        out_shape=[jax.ShapeDtypeStruct(s, F32) for s in out_shapes],
        compiler_params=_cparams("arbitrary"),
    )(me_idx, *lands, *owns, *ws, *ms, *vs)


def _small_view(n, a):
    if a.ndim == 1:
        return a.reshape(1, -1)
    if a.ndim == 3:
        return a.transpose(1, 2, 0).reshape(QKV_BLOCK * QKV_BLOCK, -1)
    return a.T if n == "w_if" else a


def _small_unview(n, a, shape):
    if len(shape) == 1:
        return a.reshape(shape)
    if len(shape) == 3:
        return a.reshape(QKV_BLOCK, QKV_BLOCK, -1).transpose(2, 0, 1)
    return a.T if n == "w_if" else a


def _small_shards(n, g):
    if n == "w_if":
        return g.reshape(N_DEV, -1, g.shape[1]).transpose(0, 2, 1)
    return g.reshape(g.shape[0], N_DEV, -1).transpose(1, 0, 2)


def _small_unshard(n, s):
    if n == "w_if":
        return s.transpose(0, 2, 1).reshape(-1, s.shape[1])
    return s.transpose(1, 0, 2).reshape(s.shape[1], -1)


def _from_hm(a):
    h, t, d = a.shape
    return a.transpose(1, 0, 2).reshape(t, h * d)


def _gate_rows(g):
    t = g.shape[0]
    return g.T.reshape(HEADS, t // CHUNK, 1, CHUNK)


def _gate_cols(g):
    h, nc, _, c = g.shape
    return g.reshape(h, nc * c).T


def _blockdiag_dense(w):
    n = w.shape[0] * QKV_BLOCK // 2
    tiled = jnp.tile(w.reshape(2, n, QKV_BLOCK), (1, 1, n // QKV_BLOCK))
    r = lax.broadcasted_iota(jnp.int32, (2, n, n), 1)
    c = lax.broadcasted_iota(jnp.int32, (2, n, n), 2)
    return jnp.where(r // QKV_BLOCK == c // QKV_BLOCK, tiled, 0.0)


def _blockdiag_blocks(dense):
    _, n, _ = dense[0].shape
    k = len(dense)

    def body(*refs):
        r = lax.broadcasted_iota(jnp.int32, (n, n), 0)
        c = lax.broadcasted_iota(jnp.int32, (n, n), 1)
        fr = lax.broadcasted_iota(jnp.int32, (n, LANES), 0)
        fc = lax.broadcasted_iota(jnp.int32, (n, LANES), 1)
        fold = ((fr & (QKV_BLOCK - 1)) == fc).astype(BF16)
        for i in range(k):
            for half in range(2):
                kept = jnp.where((r >> 2) == (c >> 2), refs[i][half], 0.0)
                refs[k + i][half] = sum(lax.dot_general(t, fold, _dims("nn", 2), preferred_element_type=F32)
                                        for t in _split3(kept))

    out = pl.pallas_call(body, name="blockdiag_blocks", out_shape=[jax.ShapeDtypeStruct((2, n, LANES), F32)] * k)(*dense)
    return [o[:, :, 0:QKV_BLOCK].reshape(2 * n // QKV_BLOCK, QKV_BLOCK, QKV_BLOCK) for o in out]


def _col_blocks(w):
    k, n = w.shape
    return w.reshape(k, N_DEV, n // N_DEV).transpose(1, 0, 2)


def _from_col_blocks(g):
    d, k, n = g.shape
    return g.transpose(1, 0, 2).reshape(k, d * n)


def _first_norm(x, g):
    return _rowwise("pre_mix_norm", lambda xv, gv: ((_rms(xv, gv),), ()), [x], [g], [(x.shape[1], BF16)])[0]


def _local_step(x, h, tgt, weight, ws, prefetch, pass_on, on_grads, on_small):
    t, d = x.shape
    g1 = ws["g_pre_mix"]

    def dep(token):
        return () if token is None else (token,)

    w_in = weight("w_in", x)
    fetch_mix = prefetch(("w_pa", "w_pb", "w_o"), w_in)
    fetch_up = prefetch(("w_up", "w_down"), fetch_mix)

    n_in = w_in.shape[2]

    offs = [0]
    for s in IN_SPLITS:
        offs.append(offs[-1] + s)

    w_a_up_p = jnp.pad(ws["w_a_up"], ((0, LANES - LOWRANK), (0, 0)))
    b_a_up = ws["b_a_up"]

    def proj_in_fwd(hv, w, wa, ba):
        proj = jnp.concatenate([_raw_dot(hv, w[j], "nn") for j in range(N_DEV)], axis=1)
        parts = [proj[:, offs[i]:offs[i + 1]] for i in range(len(IN_SPLITS))]
        parts[4] = jnp.concatenate([parts[4], jnp.zeros((parts[4].shape[0], LANES - LOWRANK), F32)], axis=1)
        head_major = lambda a: jnp.stack([a[:, hs] for hs in _head_slices(GLA_DK)])
        return (head_major(parts[0]), head_major(parts[1]), *parts[2:], head_major(_log_decay(parts[4], wa, ba))), ()

    widths = [LANES if s == LOWRANK else s for s in IN_SPLITS]
    gla_hm = ((HEADS, GLA_DK), F32)
    q_hm, k_hm, v_a, g_a, a_low_p, x_m, o_pre, gate_a, gate_b, la_hm = _rowwise(
        "proj_in", proj_in_fwd, [h], [w_in, w_a_up_p, b_a_up],
        [gla_hm, gla_hm] + [(wd, BF16 if i == 2 else F32) for i, wd in enumerate(widths)][2:] + [gla_hm],
        deps=dep(fetch_up))
    gn = ws["g_gla_norm"]
    ml_w = HEADS * HEAD_W

    cw = ws["conv_w"]
    w_if_p = jnp.pad(ws["w_if"], ((0, 0), (0, LANES - 2 * HEADS)))
    pre_params = [cw[0:1], cw[1:2], cw[2:3], cw[3:4], ws["conv_b"],
                  _blockdiag_dense(ws["w_q_ml"]), _blockdiag_dense(ws["w_k_ml"]), _blockdiag_dense(ws["w_v_ml"]),
                  w_if_p[0:ml_w], w_if_p[ml_w:2 * ml_w], w_if_p[2 * ml_w:3 * ml_w],
                  jnp.pad(ws["b_if"], ((0, 0), (0, LANES - 2 * HEADS)))]
    xc, q_m, k_m, v_m, gl = _ml_pre_fwd(x_m, pre_params, tile=512)
    pass_mix = pass_on("w_pa", xc)
    li, lf = _gate_rows(gl[:, 0:HEADS]), _gate_rows(gl[:, HEADS:2 * HEADS])
    o_gla, s_prev, hc, c_prev, n_prev, m_prev = _recurrences_fwd(q_hm, k_hm, v_a, la_hm, q_m, k_m, v_m, li, lf,
                                                                 deps=dep(pass_mix))
    pass_up = pass_on("w_up", hc)
    g_ml, skip = ws["g_ml_norm"], ws["ml_skip"]

    def branches_out(o, g, a, b, c_, ga, gb, n_, wa, gm, s, wb):
        ya_in = jnp.concatenate(_per_head(_gla_out, [o, g], [], [n_]), axis=1)
        ya = _raw_dot(ya_in, wa, "nn")
        hb = jnp.concatenate(_per_head(_ml_out, [a, b, c_], [gm, s]), axis=1)
        yb = _raw_dot(hb, wb, "nn")
        return (ya_in, ya, hb, yb, _merge(ga, gb, ya, yb)), ()

    ya_in, y_a, h_b, y_b, merged = _rowwise(
        "branches_out", branches_out, [o_gla, g_a, hc, o_pre, xc, gate_a, gate_b],
        [gn, weight("w_pa", hc), g_ml, skip, weight("w_pb", hc)],
        [(ml_w, BF16), (d, BF16), (ml_w, BF16), (d, BF16), (d, BF16)], tile=512, deps=dep(pass_up))

    gpm, gpl, gpo = ws["g_post_mix"], ws["g_pre_mlp"], ws["g_post_mlp"]

    def proj_o_fwd(mg, xv, w, a, b):
        zv = _raw_dot(mg, w, "nn")
        return (zv, *_post_mix(xv, zv, a, b)), ()

    z, x1, h2 = _rowwise("proj_o", proj_o_fwd, [merged, x], [weight("w_o", merged), gpm, gpl],
                         [(d, F32), (d, F32), (d, BF16)], tile=512)
    w_up = weight("w_up", h2)
    w_down = weight("w_down", h2)
    d_ff = w_down.shape[0]

    def mlp_loss(h2v, x1v, tgtv, wu, wd, g):
        upv = jnp.concatenate([_raw_dot(h2v, wu[j], "nn") for j in range(wu.shape[0])], axis=1)
        uv = jnp.square(jnp.maximum(upv, 0.0))
        dnv = _raw_dot(uv, wd, "nn")
        loss, vjp = jax.vjp(lambda a, b, c_: _loss_rows(a, b, tgtv, c_), x1v, dnv, g)
        dx1, ddn, dg = vjp(jnp.ones((1, 1), F32))
        return (upv, uv, dx1, ddn), (jnp.broadcast_to(loss, (1, LANES)), dg)

    up, u, dx1_y, d_dn, loss, d_gpo = _rowwise("mlp_loss", mlp_loss, [h2, x1, tgt], [w_up, w_down, gpo],
                                               [(d_ff, BF16), (d_ff, BF16), (d, F32), (d, BF16)],
                                               [((1, LANES), F32), ((1, d), F32)])

    dw_down = _mm(u, d_dn, "tn", BF16, "mlp_down_dw", tm=512)

    def mlp_dx(ddn, upv, xv, zv, dx1, wd, wu, a, b):
        dup = (_raw_dot(ddn, wd, "nt") * (2.0 * jnp.maximum(upv.astype(F32), 0.0))).astype(BF16)
        ns = wu.shape[2]
        dh2 = sum(_raw_dot(dup[:, j * ns:(j + 1) * ns], wu[j], "nt") for j in range(wu.shape[0]))
        _, vjp = jax.vjp(_post_mix, xv, zv, a, b)
        dx, dz, da, db = vjp((dx1, dh2))
        return (dup, dx, dz), (da, db)

    d_up, dx_res, d_z, d_gpm, d_gpl = _rowwise("mlp_dx", mlp_dx, [d_dn, up, x, z, dx1_y], [w_down, w_up, gpm, gpl],
                                               [(d_ff, BF16), (d, F32), (d, BF16)], [((1, d), F32), ((1, d), F32)])
    dw_up = _mm_shard_cols(h2, [d_up], [d_up.shape[1]], w_up.shape[2], "mlp_up_dw", 0, d)
    sent_mlp = on_grads(dict(w_down=dw_down, w_up=dw_up))

    def branches_out_bwd(dz, ga, gb, ya, yb, o, g, a, b, c_, wo, wa, n_, wb, gm, s):
        d_ga_, d_gb_, d_ya_, d_yb_ = jax.vjp(_merge, ga, gb, ya, yb)[1](_raw_dot(dz, wo, "nt"))
        ct_a, ct_b = _raw_dot(d_ya_, wa, "nt"), _raw_dot(d_yb_, wb, "nt")
        parts_a, parts_b = [], []
        for hs in _head_slices(HEAD_W):
            parts_a.append(jax.vjp(_gla_out, o[:, hs], g[:, hs], n_)[1](ct_a[:, hs]))
            parts_b.append(jax.vjp(_ml_out, a[:, hs], b[:, hs], c_[:, hs], gm[:, hs], s[:, hs])[1](ct_b[:, hs]))
        cat = lambda parts, i: jnp.concatenate([p[i] for p in parts], axis=1)
        return ((d_ga_, d_gb_, d_ya_, d_yb_, cat(parts_a, 0), cat(parts_a, 1), cat(parts_b, 0), cat(parts_b, 1), cat(parts_b, 2)),
                (sum(p[2] for p in parts_a), cat(parts_b, 3), cat(parts_b, 4)))

    d_ga, d_gb, d_ya, d_yb, d_o, d_g_a, d_hc, d_opre, d_xc, d_gn, d_gml, d_skip = _rowwise(
        "branches_out_bwd", branches_out_bwd, [d_z, gate_a, gate_b, y_a, y_b, o_gla, g_a, hc, o_pre, xc],
        [weight("w_o", merged), weight("w_pa", hc), gn, weight("w_pb", hc), g_ml, skip],
        [(d, BF16)] * 4 + [(ml_w, F32), (ml_w, BF16), (ml_w, F32), (ml_w, BF16), (ml_w, F32)],
        [((1, HEAD_W), F32), ((1, ml_w), F32), ((1, ml_w), F32)], deps=dep(sent_mlp))
    dw_o, dw_pa, dw_pb = _mm_tn_whole([(merged, d_z), (ya_in, d_ya), (h_b, d_yb)], "mix_dw")
    sent_mix = on_grads(dict(w_o=dw_o, w_pa=dw_pa, w_pb=dw_pb))

    dq_hm, dk_hm, d_va, dla_hm, d_qm, d_km, d_vm, d_li, d_lf = _recurrences_bwd(
        q_hm, k_hm, v_a, la_hm, s_prev, d_o, q_m, k_m, v_m, li, lf, c_prev, n_prev, m_prev, d_hc, deps=dep(sent_mix))
    d_gl = jnp.concatenate([_gate_cols(d_li), _gate_cols(d_lf), jnp.zeros((t, LANES - 2 * HEADS), F32)], axis=1)
    pre_grads = _ml_pre_bwd(x_m, pre_params, [d_xc, d_qm, d_km, d_vm, d_gl], tile=512)
    d_xm = pre_grads[0]
    d_cw = jnp.concatenate(pre_grads[1:5], axis=0)
    d_cb = pre_grads[5]
    d_wq, d_wk, d_wv = _blockdiag_blocks(pre_grads[6:9])
    d_wif = jnp.concatenate(pre_grads[9:12], axis=0)[:, 0:2 * HEADS]
    d_bif = pre_grads[12][:, 0:2 * HEADS]

    def decay_bwd(al, ct, w, b):
        _, vjp = jax.vjp(_log_decay, al, w, b)
        dal, dw, db = vjp(jnp.concatenate([ct[hd] for hd in range(HEADS)], axis=1))
        return (dal,), (dw, db)

    d_alow_p, d_wa_p, d_ba = _rowwise("gla_decay_bwd", decay_bwd, [a_low_p, dla_hm], [w_a_up_p, b_a_up],
                                      [(LANES, BF16)], [(w_a_up_p.shape, F32), (b_a_up.shape, F32)])
    d_proj = [jnp.concatenate([_from_hm(dq_hm), _from_hm(dk_hm), d_va, d_g_a], axis=1), d_alow_p,
              jnp.concatenate([d_xm, d_opre, d_ga, d_gb], axis=1)]
    d_widths = [offs[4], LOWRANK, offs[9] - offs[5]]
    d_pieces = _shard_pieces(d_widths, n_in)
    small = dict(w_a_up=d_wa_p[0:LOWRANK], b_a_up=d_ba, g_gla_norm=d_gn, conv_w=d_cw, conv_b=d_cb,
                 w_q_ml=d_wq, w_k_ml=d_wk, w_v_ml=d_wv, w_if=d_wif, b_if=d_bif, ml_skip=d_skip, g_ml_norm=d_gml,
                 g_post_mix=d_gpm, g_pre_mlp=d_gpl, g_post_mlp=d_gpo)
    sent_small = on_small(small, loss)
    sent_in = sent_small
    for half in range(2):
        dw_half = _mm_shard_cols(h, d_proj, d_widths, n_in, "proj_in_dw_%d" % half, half, d // 2, deps=dep(sent_in))
        sent_in = on_grads({"w_in#%d" % half: dw_half})

    def proj_in_dx(dp_a, dp_low, dp_b, xv, dres, w, g):
        dh = 0.0
        for s in range(N_DEV):
            for i, c_in, c_w, wd in d_pieces[s]:
                src = (dp_a, dp_low, dp_b)[i]
                cols = src.shape[1] - c_in if wd < LANES else wd
                dh = dh + _raw_dot(src[:, c_in:c_in + cols], w[s][:, c_w:c_w + cols], "nt")
        _, vjp = jax.vjp(_rms, xv, g)
        dx, dg = vjp(dh)
        return (dx + dres,), (dg,)

    grad_x, d_g1 = _rowwise("proj_in_dx", proj_in_dx, [*d_proj, x, dx_res], [w_in, g1], [(d, F32)], [((1, d), F32)],
                            deps=dep(sent_in))
    return grad_x, on_small(dict(g_pre_mix=d_g1), None)


BIG = ("w_in", "w_pa", "w_pb", "w_o", "w_up", "w_down")
MIX = ("w_o", "w_pa", "w_pb")
BIG_COL_SHARDED = ("w_in", "w_pa", "w_pb", "w_up")
SMALL_SHARDED = ("w_a_up", "conv_w", "w_if")
SMALL = ("g_pre_mix", "w_a_up", "b_a_up", "g_gla_norm", "conv_w", "conv_b", "w_q_ml", "w_k_ml", "w_v_ml", "w_if", "b_if",
         "ml_skip", "g_ml_norm", "g_post_mix", "g_pre_mlp", "g_post_mlp")
WEIGHTS = ("g_pre_mix", "w_in", "w_a_up", "b_a_up", "g_gla_norm", "conv_w", "conv_b", "w_q_ml", "w_k_ml", "w_v_ml", "w_if", "b_if",
           "ml_skip", "g_ml_norm", "w_pa", "w_pb", "w_o", "g_post_mix", "g_pre_mlp", "w_up", "w_down", "g_post_mlp")


def kernel(x, g_pre_mix, w_in, w_a_up, b_a_up, g_gla_norm, conv_w, conv_b, w_q_ml, w_k_ml, w_v_ml, w_if, b_if, ml_skip, g_ml_norm, w_pa, w_pb, w_o, g_post_mix, g_pre_mlp, w_up, w_down, g_post_mlp, loss_target, m_g_pre_mix, m_w_in, m_w_a_up, m_b_a_up, m_g_gla_norm, m_conv_w, m_conv_b, m_w_q_ml, m_w_k_ml, m_w_v_ml, m_w_if, m_b_if, m_ml_skip, m_g_ml_norm, m_w_pa, m_w_pb, m_w_o, m_g_post_mix, m_g_pre_mlp, m_w_up, m_w_down, m_g_post_mlp, v_g_pre_mix, v_w_in, v_w_a_up, v_b_a_up, v_g_gla_norm, v_conv_w, v_conv_b, v_w_q_ml, v_w_k_ml, v_w_v_ml, v_w_if, v_b_if, v_ml_skip, v_g_ml_norm, v_w_pa, v_w_pb, v_w_o, v_g_post_mix, v_g_pre_mlp, v_w_up, v_w_down, v_g_post_mlp):
    args = dict(locals())
    w = {n: args[n][0] for n in WEIGHTS}
    m = {n: args["m_" + n][0] for n in WEIGHTS}
    v = {n: args["v_" + n][0] for n in WEIGHTS}

    me_lin = _lin(_me())
    me_idx = jnp.reshape(me_lin, (1,)).astype(jnp.int32)

    def full_weight(n, g):
        if n in ("w_in", "w_up"):
            return g
        return _from_col_blocks(g) if n in BIG_COL_SHARDED else g.reshape(-1, g.shape[-1])

    def grad_parts(n, g):
        if n.partition("#")[0] in ("w_in", "w_up"):
            return g
        return (_col_blocks(g) if n in BIG_COL_SHARDED else g.reshape(N_DEV, -1, g.shape[-1])).astype(BF16)

    sharded_names = tuple(SMALL_SHARDED)
    narrow = {n: w[n].astype(BF16) for n in BIG}
    ready, pending, passing = {}, {}, {}
    first_state, _ = _copies_start(["gather"] * len(sharded_names) + ["gather_chips"],
                                   [_small_view(n, w[n]) for n in sharded_names] + [narrow["w_in"]], "allgather_start_first")

    def prefetch(group, after):
        state, token = _copies_start("gather_chips", [narrow[n] for n in group], "allgather_start_" + group[0], after)
        for n in group:
            pending[n] = (group, state)
        return token

    def pass_on(n, after):
        group, state = pending[n]
        shards, lands = _copies_wait(state, after, "allgather_wait_" + group[0])
        state, token = _copies_start("gather_pass", shards, "allgather_pass_" + group[0], lands=lands)
        for gn in group:
            passing[gn] = (group, state)
        return token

    def weight(n, after):
        if n not in ready:
            group, state = passing[n]
            _, lands = _copies_wait(state, after, "allgather_passed_" + group[0])
            for gn, land in zip(group, lands):
                ready[gn] = full_weight(gn, land)
        return ready[n]

    h = _first_norm(x[0], w["g_pre_mix"].reshape(1, -1))
    first_own, first_lands = _copies_wait(first_state, [h] + [narrow[n] for n in BIG if n != "w_in"], "allgather_wait_first")
    state, _ = _copies_start("gather_pass", first_own[-1:], "allgather_pass_w_in", lands=first_lands[-1:])
    passing["w_in"] = (("w_in",), state)
    ws = {n: (w[n].reshape(1, -1) if w[n].ndim == 1 else w[n]) for n in SMALL if n not in SMALL_SHARDED}
    for n, land in zip(sharded_names, first_lands):
        ws[n] = _small_unshard(n, land)

    sets, waiting_small = [], []

    def start_set(large, small):
        names = tuple(large)
        s_names, s_kinds, s_srcs = small if small else ((), [], [])
        state, token = _copies_start(s_kinds + ["exchange"] * len(names), s_srcs + [grad_parts(n, large[n]) for n in names],
                                     "exchange_start_" + (names + s_names)[0].replace("#", "_"))
        sets.append((names, s_names, s_kinds, state))
        return token

    def on_grads(grads):
        return start_set(grads, waiting_small.pop() if waiting_small else None)

    def on_small(small, loss):
        names = tuple(small)
        kinds = ["exchange" if n in SMALL_SHARDED else "gather" for n in names]
        srcs = [_small_shards(n, small[n]) if n in SMALL_SHARDED else _small_view(n, small[n]) for n in names]
        if loss is None:
            return start_set({}, (names, kinds, srcs))
        waiting_small.append((names, kinds + ["gather"], srcs + [loss]))
        return None

    grad_x, last_token = _local_step(x[0], h, loss_target[0], weight, ws, prefetch, pass_on, on_grads, on_small)

    out, chunks, sums, updated = {}, {}, [], []

    def finish_set(names, s_names, s_kinds, state, after):
        own, lands = _copies_wait(state, after, "exchange_wait_" + (names + s_names)[0].replace("#", "_"))
        ns = len(s_kinds)
        if s_names:
            k = len(s_names)
            upd = _small_update("adamw_small_" + s_names[0], me_idx, s_kinds[:k], lands[:k], own[:k],
                                *[[_small_view(n, d[n]) for n in s_names] for d in (w, m, v)],
                                sums=list(zip(lands[k:ns], own[k:ns])))
            for i, n in enumerate(s_names):
                out[n] = tuple(_small_unview(n, a, w[n].shape) for a in upd[4 * i:4 * i + 4])
            sums.extend(upd[4 * k:])
        if names == MIX:
            upd = _small_update("adamw_mix", me_idx, ["exchange"] * len(names), lands[ns:], own[ns:],
                                *[[d[n] for n in names] for d in (w, m, v)])
            for i, n in enumerate(names):
                out[n] = tuple(upd[4 * i:4 * i + 4])
            updated.append(upd[1])
            return
        for name, part, land in zip(names, own[ns:], lands[ns:]):
            n, _, chunk = name.partition("#")
            chunks.setdefault(n, []).append((land, part))
            if chunk in ("", "1"):
                got_lands, got_parts = zip(*chunks[n])
                if n == "w_in":
                    upd = _sum_adamw_cols(got_lands, got_parts, me_idx, *[_cols_view(d[n]) for d in (w, m, v)], "adamw_" + n)
                    out[n] = tuple(_from_cols_view(a, *w[n].shape) for a in upd)
                else:
                    out[n] = upd = _sum_adamw(got_lands, got_parts, me_idx, w[n], m[n], v[n], "adamw_" + n)
                updated.append(upd[1])

    for entry in sets:
        finish_set(*entry, [grad_x, last_token] + updated)
    loss_sum = sums[0]

    shaped = lambda a, n: a.reshape(args[n].shape)
    return (loss_sum[0, 0], grad_x[None],
            *[shaped(out[n][0], n) for n in WEIGHTS], *[shaped(out[n][1], n) for n in WEIGHTS],
            *[shaped(out[n][2], n) for n in WEIGHTS], *[shaped(out[n][3], n) for n in WEIGHTS])
```

```python
import functools

import jax
import jax.numpy as jnp
from jax import lax
from jax.experimental import pallas as pl
from jax.experimental.pallas import tpu as pltpu

F32 = jnp.float32
BF16 = jnp.bfloat16
MESH = pl.DeviceIdType.MESH

N_DEV = 8
EPS = 1e-6
CHUNK = 64
CHUNKS_PER_STEP = 8
HEADS = 4
GLA_DK = 64
HEAD_W = 128
GLA_GATE_NORM = 16.0
LOWRANK = 16
CONV_K = 4
QKV_BLOCK = 4
LANES = 128
HALO = 8
IN_SPLITS = (256, 256, 512, 512, 16, 512, 512, 1024, 1024)

ADAM_LR = 0.001
ADAM_B1 = 0.9
ADAM_B2 = 0.999
ADAM_EPS = 1e-08
ADAM_WD = 0.01
ADAM_STEP = 10

VMEM_LIMIT = 56 * 1024 * 1024


def _cparams(*sem):
    return pltpu.CompilerParams(dimension_semantics=sem, vmem_limit_bytes=VMEM_LIMIT)


def _dims(mode, ndim):
    contract = {"nn": ((ndim - 1,), (ndim - 2,)), "nt": ((ndim - 1,), (ndim - 1,)), "tn": ((ndim - 2,), (ndim - 2,))}[mode]
    return contract, (((0,), (0,)) if ndim == 3 else ((), ()))


def _raw_dot(a, b, mode):
    return lax.dot_general(a.astype(BF16), b.astype(BF16), _dims(mode, a.ndim), preferred_element_type=F32)


@functools.partial(jax.custom_vjp, nondiff_argnums=(2,))
def _bdot(a, b, mode):
    return _raw_dot(a, b, mode)


def _bdot_fwd(a, b, mode):
    return _raw_dot(a, b, mode), (a, b)


def _bdot_bwd(mode, res, ct):
    a, b = res
    if mode == "nn":
        da, db = _raw_dot(ct, b, "nt"), _raw_dot(a, ct, "tn")
    elif mode == "nt":
        da, db = _raw_dot(ct, b, "nn"), _raw_dot(ct, a, "tn")
    else:
        da, db = _raw_dot(b, ct, "nt"), _raw_dot(a, ct, "nn")
    return da.astype(a.dtype), db.astype(b.dtype)


_bdot.defvjp(_bdot_fwd, _bdot_bwd)


def _split3(x):
    hi = x.astype(BF16)
    r1 = x - hi.astype(F32)
    mid = r1.astype(BF16)
    return hi, mid, (r1 - mid.astype(F32)).astype(BF16)


def _split_dot(tri, x):
    if x.ndim == 3:
        tri = jnp.broadcast_to(tri, (x.shape[0], *tri.shape))
    return sum(lax.dot_general(tri, t, _dims("nn", x.ndim), preferred_element_type=F32) for t in _split3(x))


def _tri(n, lower):
    r = lax.broadcasted_iota(jnp.int32, (n, n), 0)
    c = lax.broadcasted_iota(jnp.int32, (n, n), 1)
    return ((c <= r) if lower else (c >= r)).astype(BF16)


@jax.custom_vjp
def _cumsum_rows(x):
    return _split_dot(_tri(x.shape[-2], True), x)


def _cumsum_rows_fwd(x):
    return _cumsum_rows(x), None


def _cumsum_rows_bwd(_, ct):
    return (_split_dot(_tri(ct.shape[-2], False), ct),)


_cumsum_rows.defvjp(_cumsum_rows_fwd, _cumsum_rows_bwd)


def _abs(x):
    return jnp.where(x >= 0, x, -x)


def _sigmoid(x):
    return lax.logistic(x)


def _log_sigmoid(x):
    return jnp.minimum(x, 0.0) - jnp.log(1.0 + jnp.exp(-_abs(x)))


def _rms(x, g):
    return x * lax.rsqrt(jnp.mean(x * x, axis=-1, keepdims=True) + EPS) * g


def _head_slices(w):
    return [slice(h * w, (h + 1) * w) for h in range(HEADS)]


def _heads(ref, rows=slice(None)):
    return jnp.stack([ref[rows, hs] for hs in _head_slices(HEAD_W)])


def _put_heads(ref, val, rows=slice(None)):
    for h, hs in enumerate(_head_slices(HEAD_W)):
        ref[rows, hs] = val[h].astype(ref.dtype)


def _tile(dim, want):
    if dim <= want or dim % LANES:
        return dim
    t = want
    while dim % t:
        t -= LANES
    return t


def _mm(a, b, mode, out_dtype, name, tm=1024, tn=1024, tk=4096, epilogue=None, extra=(), deps=(), shards=None):
    if shards == "b":
        assert mode == "nn"
        ns = b.shape[2]
        (m, k), (k2, n) = a.shape, (b.shape[1], b.shape[0] * ns)
        tn = ns
    elif mode == "nn":
        (m, k), (k2, n) = a.shape, b.shape
    elif mode == "nt":
        (m, k), (n, k2) = a.shape, b.shape
    else:
        (k, m), (k2, n) = a.shape, b.shape
    assert k == k2, (name, a.shape, b.shape)
    tm, tn, tk = _tile(m, tm), _tile(n, tn), _tile(k, tk)
    nk = k // tk
    out_dtypes = out_dtype if epilogue else (out_dtype,)
    assert nk == 1 or (out_dtype == F32 and not epilogue), name
    n_in = 2 + len(extra)

    def body(*refs):
        p = _raw_dot(refs[0][...], refs[1][...], mode)
        if nk > 1:
            _accumulate(pl.program_id(2), [refs[n_in + len(deps)]], [p])
            return
        outs = epilogue(p, *[r[...] for r in refs[2:n_in]]) if epilogue else (p,)
        for ref, val in zip(refs[n_in + len(deps):], outs):
            ref[...] = val.astype(ref.dtype)

    a_spec = pl.BlockSpec((tk, tm), lambda i, j, kk: (kk, i)) if mode == "tn" else pl.BlockSpec((tm, tk), lambda i, j, kk: (i, kk))
    if shards == "b":
        b_spec = pl.BlockSpec((None, tk, tn), lambda i, j, kk: (j, kk, 0))
    elif mode == "nt":
        b_spec = pl.BlockSpec((tn, tk), lambda i, j, kk: (j, kk))
    else:
        b_spec = pl.BlockSpec((tk, tn), lambda i, j, kk: (kk, j))
    o_spec = pl.BlockSpec((tm, tn), lambda i, j, kk: (i, j))
    res = pl.pallas_call(
        body, name=name, grid=(m // tm, n // tn, nk),
        in_specs=[a_spec, b_spec] + [o_spec] * len(extra) + [ANY] * len(deps), out_specs=[o_spec] * len(out_dtypes),
        out_shape=[jax.ShapeDtypeStruct((m, n), dt) for dt in out_dtypes],
        compiler_params=_cparams("parallel", "parallel", "arbitrary"),
    )(a, b, *extra, *deps)
    return res if epilogue else res[0]


def _mm_tn_whole(pairs, name):
    def body(*refs):
        for i in range(len(pairs)):
            refs[2 * len(pairs) + i][...] = _raw_dot(refs[2 * i][...], refs[2 * i + 1][...], "tn").astype(BF16)

    return pl.pallas_call(
        body, name=name,
        out_shape=[jax.ShapeDtypeStruct((a.shape[1], b.shape[1]), BF16) for a, b in pairs],
        compiler_params=pltpu.CompilerParams(vmem_limit_bytes=VMEM_LIMIT),
    )(*[x for pair in pairs for x in pair])


def _shard_pieces(widths, n):
    bounds = [0]
    for wd in widths:
        bounds.append(bounds[-1] + wd)
    assert bounds[-1] == N_DEV * n
    return [[(i, max(s * n, b) - b, max(s * n, b) - s * n, min((s + 1) * n, b + wd) - max(s * n, b))
             for i, (b, wd) in enumerate(zip(bounds, widths)) if b < (s + 1) * n and b + wd > s * n]
            for s in range(N_DEV)]


def _mm_shard_cols(a, bs, widths, n, name, row_tile, tm, deps=()):
    t = a.shape[0]
    nb = len(bs)
    pieces = _shard_pieces(widths, n)

    def body(a_ref, *rest):
        b_refs = rest[:nb]
        o_ref, at_ref = rest[nb + len(deps):]
        j = pl.program_id(0)

        @pl.when(j == 0)
        def _():
            at_ref[...] = a_ref[...].astype(BF16).T

        for s in range(N_DEV):
            @pl.when(j == s)
            def _(s=s):
                for i, c_in, c_out, wd in pieces[s]:
                    cols = min(_round_up(wd, LANES), bs[i].shape[1] - c_in) if wd < LANES else wd
                    p = _raw_dot(at_ref[...], b_refs[i][:, c_in:c_in + cols], "nn")
                    o_ref[:, c_out:c_out + wd] = p[:, 0:wd].astype(BF16)

    return pl.pallas_call(
        body, name=name, grid=(N_DEV,),
        in_specs=[pl.BlockSpec((t, tm), lambda j: (0, row_tile))]
        + [pl.BlockSpec(b.shape, lambda j: (0, 0), pipeline_mode=pl.Buffered(1)) for b in bs] + [ANY] * len(deps),
        out_specs=pl.BlockSpec((None, tm, n), lambda j: (j, 0, 0)),
        out_shape=jax.ShapeDtypeStruct((N_DEV, tm, n), BF16),
        scratch_shapes=[pltpu.VMEM((tm, t), BF16)],
        compiler_params=_cparams("arbitrary"),
    )(a, *bs, *deps)


def _round_up(v, m):
    return -(-v // m) * m


def _rowwise(name, fn, rows, params, out_rows, out_accs=(), tile=256, deps=()):
    t = rows[0].shape[0]
    r = min(tile, t)
    assert t % r == 0
    n_in, n_or = len(rows) + len(params), len(out_rows)
    n_all = n_in + len(deps)
    params = list(params) + list(deps)

    def body(*refs):
        vals = [ref[...] for ref in refs[:n_in]]
        outs = refs[n_all:]
        ro, ao = fn(*vals)
        for ref, v in zip(outs[:n_or], ro):
            ref[...] = v.astype(ref.dtype)
        if out_accs:
            _accumulate(pl.program_id(0), outs[n_or:], ao)

    def full(shape, **kw):
        return pl.BlockSpec(shape, lambda i, nd=len(shape): (0,) * nd, **kw)

    def tiled(w):
        if isinstance(w, tuple):
            return pl.BlockSpec((w[0], r, w[1]), lambda i: (0, i, 0))
        return pl.BlockSpec((r, w), lambda i: (i, 0))

    def whole(w):
        return (w[0], t, w[1]) if isinstance(w, tuple) else (t, w)

    return pl.pallas_call(
        body, name=name, grid=(t // r,),
        in_specs=[tiled(a.shape[1] if a.ndim == 2 else (a.shape[0], a.shape[2])) for a in rows]
        + [full(p.shape, pipeline_mode=pl.Buffered(1)) for p in params],
        out_specs=[tiled(w) for w, _ in out_rows] + [full(s) for s, _ in out_accs],
        out_shape=[jax.ShapeDtypeStruct(whole(w), dt) for w, dt in out_rows] + [jax.ShapeDtypeStruct(s, dt) for s, dt in out_accs],
        compiler_params=_cparams("arbitrary"),
    )(*rows, *params)


def _accumulate(step, refs, vals):
    for ref, v in zip(refs, vals):
        @pl.when(step == 0)
        def _(ref=ref, v=v):
            ref[...] = v.astype(ref.dtype)

        @pl.when(step > 0)
        def _(ref=ref, v=v):
            ref[...] += v.astype(ref.dtype)


def _gla_chunk(q, k, v, la, st):
    c = q.shape[-2]
    row = lax.broadcasted_iota(jnp.int32, (c, c), 0)
    col = lax.broadcasted_iota(jnp.int32, (c, c), 1)
    cum = _cumsum_rows(la)
    cl = jnp.sum(la, axis=-2, keepdims=True)
    ep = jnp.exp(cum)
    en = jnp.exp(-cum)
    qs = q * (GLA_DK ** -0.5)
    qp = qs * ep
    a_f = _bdot(qp, k * en, "nt")
    a_b = _bdot(qs * en, k * ep, "nt")
    sc = jnp.where(row >= col, a_f, a_b)
    o = _bdot(sc, v, "nn") + _bdot(qp, st, "nt")
    kd = k * jnp.exp(cl - cum)
    st_new = st * jnp.exp(cl) + _bdot(v, kd, "tn")
    return o, st_new


def _gla_specs(nc, rev):
    nb = nc // CHUNKS_PER_STEP
    rows = CHUNKS_PER_STEP * CHUNK

    def blk(n):
        return (nb - 1 - n) if rev else n
    hm = pl.BlockSpec((HEADS, rows, GLA_DK), lambda n: (0, blk(n), 0))
    tm = pl.BlockSpec((rows, HEADS * HEAD_W), lambda n: (blk(n), 0))
    st = pl.BlockSpec((HEADS, CHUNKS_PER_STEP, HEAD_W, GLA_DK), lambda n: (0, blk(n), 0, 0))
    return nb, hm, tm, st


def _chunk_rows(c):
    return slice(c * CHUNK, (c + 1) * CHUNK)


def _ml_chunk(q, k, v, li_r, lf_r, cm, nv, m):
    c = q.shape[-2]
    row = lax.broadcasted_iota(jnp.int32, (c, c), 0)
    col = lax.broadcasted_iota(jnp.int32, (c, c), 1)
    eye = (row == col).astype(F32)
    li_c = jnp.sum(eye * li_r, axis=-1, keepdims=True)
    lf_c = jnp.sum(eye * lf_r, axis=-1, keepdims=True)
    fc_c = jnp.sum((col <= row).astype(F32) * lf_r, axis=-1, keepdims=True)
    fc_r = jnp.sum((row <= col).astype(F32) * lf_c, axis=-2, keepdims=True)
    f_last = jnp.sum(lf_r, axis=-1, keepdims=True)
    kc = k * (HEAD_W ** -0.5)
    a_c = f_last - fc_c + li_c
    m_loc = jnp.max(a_c, axis=-2, keepdims=True)
    kw = kc * jnp.exp(a_c - m_loc)
    c_chunk = _bdot(kw, v, "tn")
    n_chunk = jnp.sum(kw, axis=-2, keepdims=True)
    m_new = jnp.maximum(f_last + m, m_loc)
    sp = jnp.exp(f_last + m - m_new)
    sl = jnp.exp(m_loc - m_new)
    cm_new = sp * cm + sl * c_chunk
    nv_new = sp * nv + sl * n_chunk
    log_d = li_r - _abs(fc_c - fc_r)
    g_inter = fc_c + m
    m_t = jnp.maximum(g_inter, jnp.max(log_d, axis=-1, keepdims=True))
    s = _bdot(q, kc, "nt") * jnp.exp(log_d - m_t)
    sc = jnp.exp(g_inter - m_t)
    num = _bdot(s, v, "nn") + sc * _bdot(q, cm, "nn")
    den = jnp.sum(s, axis=-1, keepdims=True) + sc * jnp.sum(q * nv, axis=-1, keepdims=True)
    den = jnp.maximum(_abs(den), jnp.exp(-m_t))
    return num / den, cm_new, nv_new, m_new


def _ml_specs(nc, rev):
    nb = nc // CHUNKS_PER_STEP

    def blk(n):
        return (nb - 1 - n) if rev else n
    tm = pl.BlockSpec((CHUNKS_PER_STEP * CHUNK, HEADS * HEAD_W), lambda n: (blk(n), 0))
    gate = pl.BlockSpec((HEADS, CHUNKS_PER_STEP, 1, CHUNK), lambda n: (0, blk(n), 0, 0))
    cm = pl.BlockSpec((HEADS, CHUNKS_PER_STEP, HEAD_W, HEAD_W), lambda n: (0, blk(n), 0, 0))
    vec = pl.BlockSpec((HEADS, CHUNKS_PER_STEP, 1, HEAD_W), lambda n: (0, blk(n), 0, 0))
    return nb, tm, gate, cm, vec


_ML_STATE = [pltpu.VMEM((HEADS, HEAD_W, HEAD_W), F32), pltpu.VMEM((HEADS, 1, HEAD_W), F32), pltpu.VMEM((HEADS, 1, HEAD_W), F32)]


def _recurrences_fwd(q, k, v, la, qm, km, vm, li, lf, deps=()):
    t = v.shape[0]
    nc = t // CHUNK
    nb, hm, tm, st = _gla_specs(nc, False)
    _, _, gate, cm, vec = _ml_specs(nc, False)
    n_in = 9 + len(deps)

    def body(*refs):
        q_ref, k_ref, v_ref, la_ref, qm_ref, km_ref, vm_ref, li_ref, lf_ref = refs[:9]
        o_ref, sp_ref, hc_ref, cp_ref, np_ref, mp_ref, st_ref, c_ref, n_ref, m_ref = refs[n_in:]

        @pl.when(pl.program_id(0) == 0)
        def _():
            for ref in (st_ref, c_ref, n_ref, m_ref):
                ref[...] = jnp.zeros_like(ref)

        s, cs, ns, ms = st_ref[...], c_ref[...], n_ref[...], m_ref[...][:, :, 0:1]
        for c in range(CHUNKS_PER_STEP):
            r = _chunk_rows(c)
            sp_ref[:, c] = s
            o, s = _gla_chunk(q_ref[:, r], k_ref[:, r], _heads(v_ref, r), la_ref[:, r], s)
            _put_heads(o_ref, o, r)
            cp_ref[:, c] = cs
            np_ref[:, c] = ns
            mp_ref[:, c] = jnp.broadcast_to(ms, m_ref.shape)
            hc, cs, ns, ms = _ml_chunk(_heads(qm_ref, r), _heads(km_ref, r), _heads(vm_ref, r), li_ref[:, c], lf_ref[:, c],
                                       cs, ns, ms)
            _put_heads(hc_ref, hc, r)
        st_ref[...] = s
        c_ref[...] = cs
        n_ref[...] = ns
        m_ref[...] = jnp.broadcast_to(ms, m_ref.shape)

    tm_shape = jax.ShapeDtypeStruct((t, HEADS * HEAD_W), F32)
    vec_shape = jax.ShapeDtypeStruct((HEADS, nc, 1, HEAD_W), F32)
    return pl.pallas_call(
        body, name="recurrences_fwd", grid=(nb,),
        in_specs=[hm, hm, tm, hm, tm, tm, tm, gate, gate] + [ANY] * len(deps), out_specs=[tm, st, tm, cm, vec, vec],
        out_shape=[tm_shape, jax.ShapeDtypeStruct((HEADS, nc, HEAD_W, GLA_DK), F32),
                   tm_shape, jax.ShapeDtypeStruct((HEADS, nc, HEAD_W, HEAD_W), F32), vec_shape, vec_shape],
        scratch_shapes=[pltpu.VMEM((HEADS, HEAD_W, GLA_DK), F32)] + _ML_STATE,
        compiler_params=_cparams("arbitrary"),
    )(q, k, v, la, qm, km, vm, li, lf, *deps)


def _recurrences_bwd(q, k, v, la, sp, do, qm, km, vm, li, lf, cp, npv, mp, dhc, deps=()):
    t = v.shape[0]
    nc = t // CHUNK
    nb, hm, tm, st = _gla_specs(nc, True)
    _, _, gate, cm, vec = _ml_specs(nc, True)
    n_in = 15 + len(deps)

    def body(*refs):
        (q_ref, k_ref, v_ref, la_ref, sp_ref, do_ref,
         qm_ref, km_ref, vm_ref, li_ref, lf_ref, cp_ref, np_ref, mp_ref, dhc_ref) = refs[:15]
        (dq_ref, dk_ref, dv_ref, dla_ref, dqm_ref, dkm_ref, dvm_ref, dli_ref, dlf_ref,
         ds_ref, dc_ref, dn_ref, dm_ref) = refs[n_in:]

        @pl.when(pl.program_id(0) == 0)
        def _():
            for ref in (ds_ref, dc_ref, dn_ref, dm_ref):
                ref[...] = jnp.zeros_like(ref)

        ds, dc, dn, dm = ds_ref[...], dc_ref[...], dn_ref[...], dm_ref[...][:, :, 0:1]
        for c in reversed(range(CHUNKS_PER_STEP)):
            r = _chunk_rows(c)
            _, vjp = jax.vjp(_gla_chunk, q_ref[:, r], k_ref[:, r], _heads(v_ref, r), la_ref[:, r], sp_ref[:, c])
            dq, dk, dv, dla, ds = vjp((_heads(do_ref, r), ds))
            dq_ref[:, r] = dq.astype(dq_ref.dtype)
            dk_ref[:, r] = dk.astype(dk_ref.dtype)
            _put_heads(dv_ref, dv, r)
            dla_ref[:, r] = dla
            _, vjp = jax.vjp(_ml_chunk, _heads(qm_ref, r), _heads(km_ref, r), _heads(vm_ref, r), li_ref[:, c], lf_ref[:, c],
                             cp_ref[:, c], np_ref[:, c], mp_ref[:, c][:, :, 0:1])
            dqm, dkm, dvm, dli, dlf, dc, dn, dm = vjp((_heads(dhc_ref, r), dc, dn, dm))
            _put_heads(dqm_ref, dqm, r)
            _put_heads(dkm_ref, dkm, r)
            _put_heads(dvm_ref, dvm, r)
            dli_ref[:, c] = dli
            dlf_ref[:, c] = dlf
        ds_ref[...] = ds
        dc_ref[...] = dc
        dn_ref[...] = dn
        dm_ref[...] = jnp.broadcast_to(dm, dm_ref.shape)

    hm_shape = jax.ShapeDtypeStruct((HEADS, t, GLA_DK), BF16)
    tm_shape = jax.ShapeDtypeStruct((t, HEADS * HEAD_W), F32)
    gate_shape = jax.ShapeDtypeStruct((HEADS, nc, 1, CHUNK), F32)
    return pl.pallas_call(
        body, name="recurrences_bwd", grid=(nb,),
        in_specs=[hm, hm, tm, hm, st, tm, tm, tm, tm, gate, gate, cm, vec, vec, tm] + [ANY] * len(deps),
        out_specs=[hm, hm, tm, hm, tm, tm, tm, gate, gate],
        out_shape=[hm_shape, hm_shape, jax.ShapeDtypeStruct((t, HEADS * HEAD_W), BF16),
                   jax.ShapeDtypeStruct((HEADS, t, GLA_DK), F32), tm_shape, tm_shape, tm_shape, gate_shape, gate_shape],
        scratch_shapes=[pltpu.VMEM((HEADS, HEAD_W, GLA_DK), F32)] + _ML_STATE,
        compiler_params=_cparams("arbitrary"),
    )(q, k, v, la, sp, do, qm, km, vm, li, lf, cp, npv, mp, dhc, *deps)


@jax.custom_vjp
def _bdot_diag(x, w):
    b = w.shape[1]
    return jnp.concatenate([_raw_dot(x[:, :b], w[0], "nn"), _raw_dot(x[:, b:], w[1], "nn")], axis=1)


def _bdot_diag_fwd(x, w):
    return _bdot_diag(x, w), (x, w)


def _bdot_diag_bwd(res, ct):
    x, w = res
    b = w.shape[1]
    dx = jnp.concatenate([_raw_dot(ct[:, :b], w[0], "nt"), _raw_dot(ct[:, b:], w[1], "nt")], axis=1)
    dw = jnp.stack([_raw_dot(x[:, :b], ct[:, :b], "tn"), _raw_dot(x[:, b:], ct[:, b:], "tn")])
    return dx.astype(x.dtype), dw.astype(w.dtype)


_bdot_diag.defvjp(_bdot_diag_fwd, _bdot_diag_bwd)


def _ml_pre(s0, s1, s2, s3, cw0, cw1, cw2, cw3, cb, wq, wk, wv, wiq, wik, wiv, bif):
    pre = cb + cw0 * s0 + cw1 * s1 + cw2 * s2 + cw3 * s3
    xc = pre * _sigmoid(pre)
    q = _bdot_diag(xc, wq)
    k = _bdot_diag(xc, wk)
    v = _bdot_diag(s3, wv)
    gates = _bdot(q, wiq, "nn") + _bdot(k, wik, "nn") + _bdot(v, wiv, "nn") + bif
    lane = lax.broadcasted_iota(jnp.int32, gates.shape, 1)
    gl = jnp.where(lane < HEADS, gates, _log_sigmoid(gates))
    return xc, q, k, v, gl


def _delayed(xs_ref, x_ref, halo_ref, r, first):
    xs_ref[0:HALO, :] = jnp.where(first, 0.0, halo_ref[...])
    xs_ref[HALO:HALO + r, :] = x_ref[...]
    return [xs_ref[pl.ds(HALO - (CONV_K - 1) + j, r), :] for j in range(CONV_K)]


def _halo_spec(r, w, tile_of):
    return pl.BlockSpec((HALO, w), lambda i: (jnp.maximum(tile_of(i) * (r // HALO) - 1, 0), 0))


def _full_spec(shape):
    return pl.BlockSpec(shape, lambda i, nd=len(shape): (0,) * nd)


def _ml_pre_fwd(x_m, params, tile=256, deps=()):
    t, w = x_m.shape
    r = min(tile, t)

    def body(*refs):
        x_ref, halo_ref = refs[:2]
        p = [ref[...] for ref in refs[2:2 + len(params)]]
        outs = refs[2 + len(params) + len(deps):-1]
        res = _ml_pre(*_delayed(refs[-1], x_ref, halo_ref, r, pl.program_id(0) == 0), *p)
        for ref, val in zip(outs, res):
            ref[...] = val

    row = pl.BlockSpec((r, w), lambda i: (i, 0))
    return pl.pallas_call(
        body, name="ml_pre_fwd", grid=(t // r,),
        in_specs=[row, _halo_spec(r, w, lambda i: i)] + [_full_spec(p.shape) for p in params]
        + [ANY] * len(deps),
        out_specs=[row] * 4 + [pl.BlockSpec((r, LANES), lambda i: (i, 0))],
        out_shape=[jax.ShapeDtypeStruct((t, w), F32)] * 4 + [jax.ShapeDtypeStruct((t, LANES), F32)],
        scratch_shapes=[pltpu.VMEM((r + HALO, w), F32)],
        compiler_params=_cparams("arbitrary"),
    )(x_m, x_m, *params, *deps)


def _ml_pre_bwd(x_m, params, cts, tile=256):
    t, w = x_m.shape
    r = min(tile, t)
    nt = t // r
    n_p = len(params)

    def body(*refs):
        x_ref, halo_ref = refs[:2]
        p = [ref[...] for ref in refs[2:2 + n_p]]
        ct = [ref[...] for ref in refs[2 + n_p:7 + n_p]]
        dx_ref = refs[7 + n_p]
        dp_refs = refs[8 + n_p:8 + 2 * n_p]
        xs_ref, ds_ref, carry_ref = refs[8 + 2 * n_p:]
        step = pl.program_id(0)

        @pl.when(step == 0)
        def _():
            ds_ref[...] = jnp.zeros_like(ds_ref)
            carry_ref[...] = jnp.zeros_like(carry_ref)

        _, vjp = jax.vjp(_ml_pre, *_delayed(xs_ref, x_ref, halo_ref, r, step == nt - 1), *p)
        grads = vjp(tuple(ct))
        for j in range(CONV_K):
            ds_ref[j, HALO:HALO + r, :] = grads[j]
        lead = HALO + CONV_K - 1
        d_tile = sum(ds_ref[j, pl.ds(lead - j, r), :] for j in range(CONV_K))
        d_halo = sum(ds_ref[j, pl.ds(CONV_K - 1 - j, HALO), :] for j in range(CONV_K))
        dx_ref[...] = jnp.concatenate([d_tile[:r - HALO], d_tile[r - HALO:] + carry_ref[...]], axis=0).astype(dx_ref.dtype)
        carry_ref[...] = d_halo
        _accumulate(step, dp_refs, grads[CONV_K:])

    row = pl.BlockSpec((r, w), lambda i: (nt - 1 - i, 0))
    return pl.pallas_call(
        body, name="ml_pre_bwd", grid=(nt,),
        in_specs=[row, _halo_spec(r, w, lambda i: nt - 1 - i)] + [_full_spec(p.shape) for p in params]
        + [row] * 4 + [pl.BlockSpec((r, LANES), lambda i: (nt - 1 - i, 0))],
        out_specs=[row] + [_full_spec(p.shape) for p in params],
        out_shape=[jax.ShapeDtypeStruct((t, w), BF16)] + [jax.ShapeDtypeStruct(p.shape, F32) for p in params],
        scratch_shapes=[pltpu.VMEM((r + HALO, w), F32), pltpu.VMEM((CONV_K, r + 2 * HALO, w), F32), pltpu.VMEM((HALO, w), F32)],
        compiler_params=_cparams("arbitrary"),
    )(x_m, x_m, *params, *cts)


def _per_head(fn, row_vals, head_params, shared_params=()):
    return [fn(*[a[:, hs] for a in row_vals], *[p[:, hs] for p in head_params], *shared_params) for hs in _head_slices(HEAD_W)]


def _gla_out(o, g, gn):
    return _rms(o, gn) * (g * _sigmoid(g))


def _ml_out(hc, op, xc, g, sk):
    hcell = hc * _sigmoid(op)
    mu = jnp.mean(hcell, axis=-1, keepdims=True)
    d = hcell - mu
    var = jnp.mean(d * d, axis=-1, keepdims=True)
    return d * lax.rsqrt(var + EPS) * g + sk * xc


def _log_decay(al, w, b):
    return _log_sigmoid(_bdot(al, w, "nn") + b) * (1.0 / GLA_GATE_NORM)


def _merge(ga, gb, ya, yb):
    ga, gb, ya, yb = (a.astype(F32) for a in (ga, gb, ya, yb))
    return _sigmoid(ga) * ya + _sigmoid(gb) * yb


def _post_mix(x, z, gpm, gpl):
    x1 = x + _rms(z, gpm)
    return x1, _rms(x1, gpl)


def _loss_rows(x1, dn, tgt, g):
    e = x1 + _rms(dn, g) - tgt
    return 0.5 * jnp.sum(jnp.mean(e * e, axis=-1, keepdims=True), axis=0, keepdims=True)


def _lin(p):
    return 4 * p[0] + 2 * p[1] + p[2]


def _me():
    return lax.axis_index("x"), lax.axis_index("y"), lax.axis_index("c")


def _flip(p, k):
    return tuple((1 - v) if (k >> (2 - i)) & 1 else v for i, v in enumerate(p))


ANY = pl.BlockSpec(memory_space=pl.ANY)


HBM = pl.BlockSpec(memory_space=pltpu.HBM)
SEM = pl.BlockSpec(memory_space=pltpu.SEMAPHORE)
DATAFLOW = pltpu.SideEffectType.DATAFLOW_SIDE_EFFECTING


SIBLING = 1
OTHER_CHIPS = (2, 4, 6)


def _peer_copies(kinds, srcs, lands, send_sems, recv_sems):
    me = _me()
    copies = []
    for a, (kind, src, land) in enumerate(zip(kinds, srcs, lands)):
        masks = {"gather": range(1, N_DEV), "exchange": range(1, N_DEV), "gather_chips": (SIBLING, *OTHER_CHIPS),
                 "gather_pass": OTHER_CHIPS}[kind]
        for k in masks:
            peer = _flip(me, k)
            if kind == "gather_pass":
                block = land.at[_lin(peer)]
                src_ref, dst_ref, target = block, block, _flip(me, SIBLING)
            else:
                src_ref, dst_ref, target = (src.at[_lin(peer)] if kind == "exchange" else src), land.at[_lin(me)], peer
            copies.append(pltpu.make_async_remote_copy(
                src_ref=src_ref, dst_ref=dst_ref, send_sem=send_sems.at[a * 7 + k - 1], recv_sem=recv_sems.at[a * 7 + k - 1],
                device_id=target, device_id_type=MESH))
    return copies


def _own_copies(kinds, srcs, lands, own_sems):
    return [pltpu.make_async_copy(src, land.at[_lin(_me())], own_sems.at[a])
            for a, (kind, src, land) in enumerate(zip(kinds, srcs, lands)) if kind in ("gather", "gather_chips")]


def _copies_start(kind, srcs, name, after=None, lands=None):
    n = len(srcs)
    extra = [] if after is None else [after]
    kind = [kind] * n if isinstance(kind, str) else list(kind)
    land_shapes = [(s.shape if k == "exchange" else (N_DEV, *s.shape)) for k, s in zip(kind, srcs)]
    lands = [lax.empty(ls, s.dtype) for ls, s in zip(land_shapes, srcs)] if lands is None else lands

    def body(*refs):
        sems = refs[2 * n + len(extra):]
        for cp in _peer_copies(kind, refs[:n], refs[n:2 * n], sems[0], sems[1]) + _own_copies(kind, refs[:n], refs[n:2 * n], sems[2]):
            cp.start()
        refs[-1][...] = jnp.zeros_like(refs[-1])

    def hbm(a):
        return pltpu.with_memory_space_constraint(a, pltpu.HBM)

    out = pl.pallas_call(
        body, name=name,
        out_shape=(pltpu.SemaphoreType.DMA((7 * n,)), pltpu.SemaphoreType.DMA((7 * n,)), pltpu.SemaphoreType.DMA((n,)),
                   *[pltpu.HBM(s.shape, s.dtype) for s in srcs],
                   *[pltpu.HBM(ls, s.dtype) for ls, s in zip(land_shapes, srcs)],
                   jax.ShapeDtypeStruct((8, LANES), F32)),
        in_specs=[HBM] * (2 * n) + [ANY] * len(extra),
        out_specs=(SEM, SEM, SEM, *[HBM] * (2 * n), pl.BlockSpec(memory_space=pltpu.VMEM)),
        input_output_aliases={i: 3 + i for i in range(2 * n)},
        compiler_params=pltpu.CompilerParams(has_side_effects=DATAFLOW),
    )(*[hbm(s) for s in srcs], *[hbm(a) for a in lands], *extra)
    return (kind, n, out[:-1]), out[-1]


def _copies_wait(state, after, name):
    kind, n, (send_sems, recv_sems, own_sems, *thru) = state
    after = list(after) if isinstance(after, (list, tuple)) else [after]

    def body(*refs):
        for cp in _peer_copies(kind, refs[:n], refs[n:2 * n], refs[2 * n], refs[2 * n + 1]):
            cp.wait_send()
            cp.wait_recv()
        for cp in _own_copies(kind, refs[:n], refs[n:2 * n], refs[2 * n + 2]):
            cp.wait()

    out = pl.pallas_call(
        body, name=name,
        out_shape=tuple(pltpu.HBM(t.shape, t.dtype) for t in thru),
        in_specs=[HBM] * (2 * n) + [SEM, SEM, SEM] + [ANY] * len(after), out_specs=tuple([HBM] * (2 * n)),
        input_output_aliases={i: i for i in range(2 * n)},
        compiler_params=pltpu.CompilerParams(has_side_effects=DATAFLOW),
    )(*thru, send_sems, recv_sems, own_sems, *after)
    return out[:n], out[n:]


def _adamw(w, g, m, v):
    m2 = ADAM_B1 * m + (1.0 - ADAM_B1) * g
    v2 = ADAM_B2 * v + (1.0 - ADAM_B2) * (g * g)
    m_hat = m2 / (1.0 - ADAM_B1 ** ADAM_STEP)
    v_hat = v2 / (1.0 - ADAM_B2 ** ADAM_STEP)
    delta = -ADAM_LR * (m_hat / (jnp.sqrt(v_hat) + ADAM_EPS) + ADAM_WD * w)
    return delta, m2, v2


def _sum_adamw(lands, parts, me_idx, w, m, v, name, tile=256):
    r, c = w.shape
    nchunks = len(lands)
    tr = min(tile, r // nchunks)
    per_chunk = r // nchunks // tr
    per = 1 + N_DEV

    def body(me_ref, *refs):
        w_ref, m_ref, v_ref, g_ref, d_ref, m2_ref, v2_ref = refs[nchunks * per:]
        for k in range(nchunks):
            own_ref, slots = refs[k * per], refs[k * per + 1:(k + 1) * per]

            @pl.when(pl.program_id(0) // per_chunk == k)
            def _(own_ref=own_ref, slots=slots):
                own = own_ref[...].astype(F32)
                g = None
                for s in range(N_DEV):
                    term = jnp.where(me_ref[0] == s, own, slots[s][...].astype(F32))
                    g = term if g is None else g + term
                d, m2, v2 = _adamw(w_ref[...], g, m_ref[...], v_ref[...])
                g_ref[...] = g
                d_ref[...] = d
                m2_ref[...] = m2
                v2_ref[...] = v2

    def chunk_specs(k):
        def tile_of(i):
            return jnp.clip(i - k * per_chunk, 0, per_chunk - 1)

        def slot_spec(s):
            return pl.BlockSpec((None, tr, c), lambda i, me: (jnp.where(me[0] == s, (s + 1) % N_DEV, s), tile_of(i), 0))
        return [pl.BlockSpec((None, tr, c), lambda i, me: (me[0], tile_of(i), 0))] + [slot_spec(s) for s in range(N_DEV)]

    row = pl.BlockSpec((tr, c), lambda i, me: (i, 0))
    operands = [a for land, part in zip(lands, parts) for a in (part, *[land] * N_DEV)]
    return pl.pallas_call(
        body, name=name,
        grid_spec=pltpu.PrefetchScalarGridSpec(
            num_scalar_prefetch=1, grid=(r // tr,),
            in_specs=[s for k in range(nchunks) for s in chunk_specs(k)] + [row] * 3,
            out_specs=[row] * 4),
        out_shape=[jax.ShapeDtypeStruct((r, c), F32)] * 4,
        compiler_params=_cparams("parallel"),
    )(me_idx, *operands, w, m, v)


def _cols_view(a):
    r, c = a.shape
    return jnp.transpose(a.reshape(r // LANES, LANES, c), (2, 0, 1))


def _from_cols_view(a, r, c):
    return jnp.transpose(a, (1, 2, 0)).reshape(r, c)


def _sum_adamw_cols(lands, parts, me_idx, w, m, v, name):
    nchunks = len(lands)
    rows_k, c = lands[0].shape[1:]
    r = nchunks * rows_k
    steps = r // LANES
    per_chunk = rows_k // LANES
    per = 1 + N_DEV
    c_pad = _round_up(c, LANES)
    assert steps == 8
    block = steps * max(n for n in range(1, 65) if c % n == 0)

    def body(me_ref, *refs):
        w_ref, m_ref, v_ref, g_ref, d_ref, m2_ref, v2_ref, gt_ref = refs[nchunks * per:]
        for i in range(steps):
            k = i // per_chunk
            own_ref, slots = refs[k * per], refs[k * per + 1:(k + 1) * per]

            @pl.when(pl.program_id(0) == i)
            def _(i=i, own_ref=own_ref, slots=slots):
                own = own_ref[...].astype(F32)
                g = None
                for s in range(N_DEV):
                    term = jnp.where(me_ref[0] == s, own, slots[s][...].astype(F32))
                    g = term if g is None else g + term
                g = jnp.concatenate([g, jnp.zeros((LANES, c_pad - c), F32)], axis=1)
                for j in range(c_pad // LANES):
                    cols = min(LANES, c - j * LANES)
                    gt_ref[pl.ds(steps * LANES * j + i, cols, stride=steps), :] = jnp.transpose(g[:, j * LANES:(j + 1) * LANES])[0:cols]

        @pl.when(pl.program_id(0) == steps - 1)
        def _():
            def update(b, carry):
                cols = pl.ds(b * (block // steps), block // steps)
                g = gt_ref[pl.ds(pl.multiple_of(b * block, steps), block), :].reshape(block // steps, steps, LANES)
                d, m2, v2 = _adamw(w_ref[cols], g, m_ref[cols], v_ref[cols])
                g_ref[cols] = g
                d_ref[cols] = d
                m2_ref[cols] = m2
                v2_ref[cols] = v2
                return carry
            lax.fori_loop(0, c * steps // block, update, 0)

    def chunk_specs(k):
        def tile_of(i):
            return jnp.clip(i - k * per_chunk, 0, per_chunk - 1)

        def slot_spec(s):
            return pl.BlockSpec((None, LANES, c), lambda i, me: (jnp.where(me[0] == s, (s + 1) % N_DEV, s), tile_of(i), 0))
        return [pl.BlockSpec((None, LANES, c), lambda i, me: (me[0], tile_of(i), 0))] + [slot_spec(s) for s in range(N_DEV)]

    whole = pl.BlockSpec((c, steps, LANES), lambda i, me: (0, 0, 0))
    operands = [a for land, part in zip(lands, parts) for a in (part, *[land] * N_DEV)]
    return pl.pallas_call(
        body, name=name,
        grid_spec=pltpu.PrefetchScalarGridSpec(
            num_scalar_prefetch=1, grid=(steps,),
            in_specs=[s for k in range(nchunks) for s in chunk_specs(k)]
            + [pl.BlockSpec((c, steps, LANES), lambda i, me: (0, 0, 0), pipeline_mode=pl.Buffered(1))] * 3,
            out_specs=[whole] * 4,
            scratch_shapes=[pltpu.VMEM((c * steps, LANES), F32)]),
        out_shape=[jax.ShapeDtypeStruct((c, steps, LANES), F32)] * 4,
        compiler_params=_cparams("arbitrary"),
    )(me_idx, *operands, w, m, v)


def _small_update(name, me_idx, kinds, lands, owns, ws, ms, vs, sums=()):
    n = len(ws)
    lands, owns = list(lands) + [s[0] for s in sums], list(owns) + [s[1] for s in sums]
    kinds = list(kinds) + ["gather"] * len(sums)
    nl = len(lands)

    def summed(me, land_ref, own):
        g = None
        for s in range(N_DEV):
            term = jnp.where(me == s, own, land_ref[s]).astype(F32)
            g = term if g is None else g + term
        return g

    def body(me_ref, *refs):
        land_refs, own_refs = refs[:nl], refs[nl:2 * nl]
        w_refs, m_refs, v_refs = (refs[2 * nl + i * n:2 * nl + (i + 1) * n] for i in range(3))
        outs = refs[2 * nl + 3 * n:]
        me = me_ref[0]
        for i in range(n):
            g = summed(me, land_refs[i], own_refs[i][...])
            d, m2, v2 = _adamw(w_refs[i][...], g, m_refs[i][...], v_refs[i][...])
            for ref, val in zip(outs[4 * i:4 * i + 4], (g, d, m2, v2)):
                ref[...] = val
        for i in range(n, nl):
            outs[4 * n + i - n][...] = summed(me, land_refs[i], own_refs[i][...])

    def whole(shape):
        return pl.BlockSpec(shape, lambda i, me, nd=len(shape): (0,) * nd)

    def own_spec(kind, own):
        if kind == "gather":
            return whole(own.shape)
        return pl.BlockSpec((None, *own.shape[1:]), lambda i, me: (me[0], 0, 0))

    shapes = [w.shape for w in ws]
    out_shapes = [s for s in shapes for _ in range(4)] + [s[1].shape for s in sums]
    return pl.pallas_call(
        body, name=name,
        grid_spec=pltpu.PrefetchScalarGridSpec(
            num_scalar_prefetch=1, grid=(1,),
            in_specs=[whole(a.shape) for a in lands] + [own_spec(k, o) for k, o in zip(kinds, owns)]
            + [whole(s) for s in shapes] * 3,
            out_specs=[whole(s) for s in out_shapes]),
        out_shape=[jax.ShapeDtypeStruct(s, F32) for s in out_shapes],
        compiler_params=_cparams("arbitrary"),
    )(me_idx, *lands, *owns, *ws, *ms, *vs)


def _small_view(n, a):
    if a.ndim == 1:
        return a.reshape(1, -1)
    if a.ndim == 3:
        return a.transpose(1, 2, 0).reshape(QKV_BLOCK * QKV_BLOCK, -1)
    return a.T if n == "w_if" else a


def _small_unview(n, a, shape):
    if len(shape) == 1:
        return a.reshape(shape)
    if len(shape) == 3:
        return a.reshape(QKV_BLOCK, QKV_BLOCK, -1).transpose(2, 0, 1)
    return a.T if n == "w_if" else a


def _small_shards(n, g):
    if n == "w_if":
        return g.reshape(N_DEV, -1, g.shape[1]).transpose(0, 2, 1)
    return g.reshape(g.shape[0], N_DEV, -1).transpose(1, 0, 2)


def _small_unshard(n, s):
    if n == "w_if":
        return s.transpose(0, 2, 1).reshape(-1, s.shape[1])
    return s.transpose(1, 0, 2).reshape(s.shape[1], -1)


def _from_hm(a):
    h, t, d = a.shape
    return a.transpose(1, 0, 2).reshape(t, h * d)


def _gate_rows(g):
    t = g.shape[0]
    return g.T.reshape(HEADS, t // CHUNK, 1, CHUNK)


def _gate_cols(g):
    h, nc, _, c = g.shape
    return g.reshape(h, nc * c).T


def _blockdiag_dense(w):
    n = w.shape[0] * QKV_BLOCK // 2
    tiled = jnp.tile(w.reshape(2, n, QKV_BLOCK), (1, 1, n // QKV_BLOCK))
    r = lax.broadcasted_iota(jnp.int32, (2, n, n), 1)
    c = lax.broadcasted_iota(jnp.int32, (2, n, n), 2)
    return jnp.where(r // QKV_BLOCK == c // QKV_BLOCK, tiled, 0.0)


def _blockdiag_blocks(dense):
    _, n, _ = dense[0].shape
    k = len(dense)

    def body(*refs):
        r = lax.broadcasted_iota(jnp.int32, (n, n), 0)
        c = lax.broadcasted_iota(jnp.int32, (n, n), 1)
        fr = lax.broadcasted_iota(jnp.int32, (n, LANES), 0)
        fc = lax.broadcasted_iota(jnp.int32, (n, LANES), 1)
        fold = ((fr & (QKV_BLOCK - 1)) == fc).astype(BF16)
        for i in range(k):
            for half in range(2):
                kept = jnp.where((r >> 2) == (c >> 2), refs[i][half], 0.0)
                refs[k + i][half] = sum(lax.dot_general(t, fold, _dims("nn", 2), preferred_element_type=F32)
                                        for t in _split3(kept))

    out = pl.pallas_call(body, name="blockdiag_blocks", out_shape=[jax.ShapeDtypeStruct((2, n, LANES), F32)] * k)(*dense)
    return [o[:, :, 0:QKV_BLOCK].reshape(2 * n // QKV_BLOCK, QKV_BLOCK, QKV_BLOCK) for o in out]


def _col_blocks(w):
    k, n = w.shape
    return w.reshape(k, N_DEV, n // N_DEV).transpose(1, 0, 2)


def _from_col_blocks(g):
    d, k, n = g.shape
    return g.transpose(1, 0, 2).reshape(k, d * n)


def _first_norm(x, g):
    return _rowwise("pre_mix_norm", lambda xv, gv: ((_rms(xv, gv),), ()), [x], [g], [(x.shape[1], BF16)])[0]


def _local_step(x, h, tgt, weight, ws, prefetch, pass_on, on_grads, on_small):
    t, d = x.shape
    g1 = ws["g_pre_mix"]

    def dep(token):
        return () if token is None else (token,)

    w_in = weight("w_in", x)
    fetch_mix = prefetch(("w_pa", "w_pb", "w_o"), w_in)
    fetch_up = prefetch(("w_up", "w_down"), fetch_mix)

    n_in = w_in.shape[2]

    offs = [0]
    for s in IN_SPLITS:
        offs.append(offs[-1] + s)

    w_a_up_p = jnp.pad(ws["w_a_up"], ((0, LANES - LOWRANK), (0, 0)))
    b_a_up = ws["b_a_up"]

    def proj_in_fwd(hv, w, wa, ba):
        proj = jnp.concatenate([_raw_dot(hv, w[j], "nn") for j in range(N_DEV)], axis=1)
        parts = [proj[:, offs[i]:offs[i + 1]] for i in range(len(IN_SPLITS))]
        parts[4] = jnp.concatenate([parts[4], jnp.zeros((parts[4].shape[0], LANES - LOWRANK), F32)], axis=1)
        head_major = lambda a: jnp.stack([a[:, hs] for hs in _head_slices(GLA_DK)])
        return (head_major(parts[0]), head_major(parts[1]), *parts[2:], head_major(_log_decay(parts[4], wa, ba))), ()

    widths = [LANES if s == LOWRANK else s for s in IN_SPLITS]
    gla_hm = ((HEADS, GLA_DK), F32)
    q_hm, k_hm, v_a, g_a, a_low_p, x_m, o_pre, gate_a, gate_b, la_hm = _rowwise(
        "proj_in", proj_in_fwd, [h], [w_in, w_a_up_p, b_a_up],
        [gla_hm, gla_hm] + [(wd, BF16 if i == 2 else F32) for i, wd in enumerate(widths)][2:] + [gla_hm],
        deps=dep(fetch_up))
    gn = ws["g_gla_norm"]
    ml_w = HEADS * HEAD_W

    cw = ws["conv_w"]
    w_if_p = jnp.pad(ws["w_if"], ((0, 0), (0, LANES - 2 * HEADS)))
    pre_params = [cw[0:1], cw[1:2], cw[2:3], cw[3:4], ws["conv_b"],
                  _blockdiag_dense(ws["w_q_ml"]), _blockdiag_dense(ws["w_k_ml"]), _blockdiag_dense(ws["w_v_ml"]),
                  w_if_p[0:ml_w], w_if_p[ml_w:2 * ml_w], w_if_p[2 * ml_w:3 * ml_w],
                  jnp.pad(ws["b_if"], ((0, 0), (0, LANES - 2 * HEADS)))]
    xc, q_m, k_m, v_m, gl = _ml_pre_fwd(x_m, pre_params, tile=512)
    pass_mix = pass_on("w_pa", xc)
    li, lf = _gate_rows(gl[:, 0:HEADS]), _gate_rows(gl[:, HEADS:2 * HEADS])
    o_gla, s_prev, hc, c_prev, n_prev, m_prev = _recurrences_fwd(q_hm, k_hm, v_a, la_hm, q_m, k_m, v_m, li, lf,
                                                                 deps=dep(pass_mix))
    g_ml, skip = ws["g_ml_norm"], ws["ml_skip"]

    def branches_out(o, g, a, b, c_, ga, gb, n_, wa, gm, s, wb):
        ya_in = jnp.concatenate(_per_head(_gla_out, [o, g], [], [n_]), axis=1)
        ya = _raw_dot(ya_in, wa, "nn")
        hb = jnp.concatenate(_per_head(_ml_out, [a, b, c_], [gm, s]), axis=1)
        yb = _raw_dot(hb, wb, "nn")
        return (ya_in, ya, hb, yb, _merge(ga, gb, ya, yb)), ()

    ya_in, y_a, h_b, y_b, merged = _rowwise(
        "branches_out", branches_out, [o_gla, g_a, hc, o_pre, xc, gate_a, gate_b],
        [gn, weight("w_pa", hc), g_ml, skip, weight("w_pb", hc)],
        [(ml_w, BF16), (d, BF16), (ml_w, BF16), (d, BF16), (d, BF16)], tile=512)

    gpm, gpl, gpo = ws["g_post_mix"], ws["g_pre_mlp"], ws["g_post_mlp"]

    def proj_o_fwd(mg, xv, w, a, b):
        zv = _raw_dot(mg, w, "nn")
        return (zv, *_post_mix(xv, zv, a, b)), ()

    pass_up = pass_on("w_up", merged)
    z, x1, h2 = _rowwise("proj_o", proj_o_fwd, [merged, x], [weight("w_o", merged), gpm, gpl],
                         [(d, F32), (d, F32), (d, BF16)], tile=512, deps=dep(pass_up))
    w_up = weight("w_up", h2)
    w_down = weight("w_down", h2)
    d_ff = w_down.shape[0]

    def mlp_loss(h2v, x1v, tgtv, wu, wd, g):
        upv = jnp.concatenate([_raw_dot(h2v, wu[j], "nn") for j in range(wu.shape[0])], axis=1)
        uv = jnp.square(jnp.maximum(upv, 0.0))
        dnv = _raw_dot(uv, wd, "nn")
        loss, vjp = jax.vjp(lambda a, b, c_: _loss_rows(a, b, tgtv, c_), x1v, dnv, g)
        dx1, ddn, dg = vjp(jnp.ones((1, 1), F32))
        return (upv, uv, dx1, ddn), (jnp.broadcast_to(loss, (1, LANES)), dg)

    up, u, dx1_y, d_dn, loss, d_gpo = _rowwise("mlp_loss", mlp_loss, [h2, x1, tgt], [w_up, w_down, gpo],
                                               [(d_ff, BF16), (d_ff, BF16), (d, F32), (d, BF16)],
                                               [((1, LANES), F32), ((1, d), F32)])

    dw_down = _mm(u, d_dn, "tn", BF16, "mlp_down_dw", tm=512)

    def mlp_dx(ddn, upv, xv, zv, dx1, wd, wu, a, b):
        dup = (_raw_dot(ddn, wd, "nt") * (2.0 * jnp.maximum(upv.astype(F32), 0.0))).astype(BF16)
        ns = wu.shape[2]
        dh2 = sum(_raw_dot(dup[:, j * ns:(j + 1) * ns], wu[j], "nt") for j in range(wu.shape[0]))
        _, vjp = jax.vjp(_post_mix, xv, zv, a, b)
        dx, dz, da, db = vjp((dx1, dh2))
        return (dup, dx, dz), (da, db)

    d_up, dx_res, d_z, d_gpm, d_gpl = _rowwise("mlp_dx", mlp_dx, [d_dn, up, x, z, dx1_y], [w_down, w_up, gpm, gpl],
                                               [(d_ff, BF16), (d, F32), (d, BF16)], [((1, d), F32), ((1, d), F32)])
    dw_up = _mm_shard_cols(h2, [d_up], [d_up.shape[1]], w_up.shape[2], "mlp_up_dw", 0, d)
    sent_mlp = on_grads(dict(w_down=dw_down, w_up=dw_up))

    def branches_out_bwd(dz, ga, gb, ya, yb, o, g, a, b, c_, wo, wa, n_, wb, gm, s):
        d_ga_, d_gb_, d_ya_, d_yb_ = jax.vjp(_merge, ga, gb, ya, yb)[1](_raw_dot(dz, wo, "nt"))
        ct_a, ct_b = _raw_dot(d_ya_, wa, "nt"), _raw_dot(d_yb_, wb, "nt")
        parts_a, parts_b = [], []
        for hs in _head_slices(HEAD_W):
            parts_a.append(jax.vjp(_gla_out, o[:, hs], g[:, hs], n_)[1](ct_a[:, hs]))
            parts_b.append(jax.vjp(_ml_out, a[:, hs], b[:, hs], c_[:, hs], gm[:, hs], s[:, hs])[1](ct_b[:, hs]))
        cat = lambda parts, i: jnp.concatenate([p[i] for p in parts], axis=1)
        return ((d_ga_, d_gb_, d_ya_, d_yb_, cat(parts_a, 0), cat(parts_a, 1), cat(parts_b, 0), cat(parts_b, 1), cat(parts_b, 2)),
                (sum(p[2] for p in parts_a), cat(parts_b, 3), cat(parts_b, 4)))

    d_ga, d_gb, d_ya, d_yb, d_o, d_g_a, d_hc, d_opre, d_xc, d_gn, d_gml, d_skip = _rowwise(
        "branches_out_bwd", branches_out_bwd, [d_z, gate_a, gate_b, y_a, y_b, o_gla, g_a, hc, o_pre, xc],
        [weight("w_o", merged), weight("w_pa", hc), gn, weight("w_pb", hc), g_ml, skip],
        [(d, BF16)] * 4 + [(ml_w, F32), (ml_w, BF16), (ml_w, F32), (ml_w, BF16), (ml_w, F32)],
        [((1, HEAD_W), F32), ((1, ml_w), F32), ((1, ml_w), F32)], deps=dep(sent_mlp))
    dw_o, dw_pa, dw_pb = _mm_tn_whole([(merged, d_z), (ya_in, d_ya), (h_b, d_yb)], "mix_dw")
    sent_mix = on_grads(dict(w_o=dw_o, w_pa=dw_pa, w_pb=dw_pb))

    dq_hm, dk_hm, d_va, dla_hm, d_qm, d_km, d_vm, d_li, d_lf = _recurrences_bwd(
        q_hm, k_hm, v_a, la_hm, s_prev, d_o, q_m, k_m, v_m, li, lf, c_prev, n_prev, m_prev, d_hc, deps=dep(sent_mix))
    d_gl = jnp.concatenate([_gate_cols(d_li), _gate_cols(d_lf), jnp.zeros((t, LANES - 2 * HEADS), F32)], axis=1)
    pre_grads = _ml_pre_bwd(x_m, pre_params, [d_xc, d_qm, d_km, d_vm, d_gl], tile=512)
    d_xm = pre_grads[0]
    d_cw = jnp.concatenate(pre_grads[1:5], axis=0)
    d_cb = pre_grads[5]
    d_wq, d_wk, d_wv = _blockdiag_blocks(pre_grads[6:9])
    d_wif = jnp.concatenate(pre_grads[9:12], axis=0)[:, 0:2 * HEADS]
    d_bif = pre_grads[12][:, 0:2 * HEADS]

    def decay_bwd(al, ct, w, b):
        _, vjp = jax.vjp(_log_decay, al, w, b)
        dal, dw, db = vjp(jnp.concatenate([ct[hd] for hd in range(HEADS)], axis=1))
        return (dal,), (dw, db)

    d_alow_p, d_wa_p, d_ba = _rowwise("gla_decay_bwd", decay_bwd, [a_low_p, dla_hm], [w_a_up_p, b_a_up],
                                      [(LANES, BF16)], [(w_a_up_p.shape, F32), (b_a_up.shape, F32)])
    d_proj = [jnp.concatenate([_from_hm(dq_hm), _from_hm(dk_hm), d_va, d_g_a], axis=1), d_alow_p,
              jnp.concatenate([d_xm, d_opre, d_ga, d_gb], axis=1)]
    d_widths = [offs[4], LOWRANK, offs[9] - offs[5]]
    d_pieces = _shard_pieces(d_widths, n_in)
    small = dict(w_a_up=d_wa_p[0:LOWRANK], b_a_up=d_ba, g_gla_norm=d_gn, conv_w=d_cw, conv_b=d_cb,
                 w_q_ml=d_wq, w_k_ml=d_wk, w_v_ml=d_wv, w_if=d_wif, b_if=d_bif, ml_skip=d_skip, g_ml_norm=d_gml,
                 g_post_mix=d_gpm, g_pre_mlp=d_gpl, g_post_mlp=d_gpo)
    sent_small = on_small(small, loss)
    sent_in = sent_small
    for half in range(2):
        dw_half = _mm_shard_cols(h, d_proj, d_widths, n_in, "proj_in_dw_%d" % half, half, d // 2, deps=dep(sent_in))
        sent_in = on_grads({"w_in#%d" % half: dw_half})

    def proj_in_dx(dp_a, dp_low, dp_b, xv, dres, w, g):
        dh = 0.0
        for s in range(N_DEV):
            for i, c_in, c_w, wd in d_pieces[s]:
                src = (dp_a, dp_low, dp_b)[i]
                cols = src.shape[1] - c_in if wd < LANES else wd
                dh = dh + _raw_dot(src[:, c_in:c_in + cols], w[s][:, c_w:c_w + cols], "nt")
        _, vjp = jax.vjp(_rms, xv, g)
        dx, dg = vjp(dh)
        return (dx + dres,), (dg,)

    grad_x, d_g1 = _rowwise("proj_in_dx", proj_in_dx, [*d_proj, x, dx_res], [w_in, g1], [(d, F32)], [((1, d), F32)],
                            deps=dep(sent_in))
    return grad_x, on_small(dict(g_pre_mix=d_g1), None)


BIG = ("w_in", "w_pa", "w_pb", "w_o", "w_up", "w_down")
MIX = ("w_o", "w_pa", "w_pb")
BIG_COL_SHARDED = ("w_in", "w_pa", "w_pb", "w_up")
SMALL_SHARDED = ("w_a_up", "conv_w", "w_if")
SMALL = ("g_pre_mix", "w_a_up", "b_a_up", "g_gla_norm", "conv_w", "conv_b", "w_q_ml", "w_k_ml", "w_v_ml", "w_if", "b_if",
         "ml_skip", "g_ml_norm", "g_post_mix", "g_pre_mlp", "g_post_mlp")
WEIGHTS = ("g_pre_mix", "w_in", "w_a_up", "b_a_up", "g_gla_norm", "conv_w", "conv_b", "w_q_ml", "w_k_ml", "w_v_ml", "w_if", "b_if",
           "ml_skip", "g_ml_norm", "w_pa", "w_pb", "w_o", "g_post_mix", "g_pre_mlp", "w_up", "w_down", "g_post_mlp")


def kernel(x, g_pre_mix, w_in, w_a_up, b_a_up, g_gla_norm, conv_w, conv_b, w_q_ml, w_k_ml, w_v_ml, w_if, b_if, ml_skip, g_ml_norm, w_pa, w_pb, w_o, g_post_mix, g_pre_mlp, w_up, w_down, g_post_mlp, loss_target, m_g_pre_mix, m_w_in, m_w_a_up, m_b_a_up, m_g_gla_norm, m_conv_w, m_conv_b, m_w_q_ml, m_w_k_ml, m_w_v_ml, m_w_if, m_b_if, m_ml_skip, m_g_ml_norm, m_w_pa, m_w_pb, m_w_o, m_g_post_mix, m_g_pre_mlp, m_w_up, m_w_down, m_g_post_mlp, v_g_pre_mix, v_w_in, v_w_a_up, v_b_a_up, v_g_gla_norm, v_conv_w, v_conv_b, v_w_q_ml, v_w_k_ml, v_w_v_ml, v_w_if, v_b_if, v_ml_skip, v_g_ml_norm, v_w_pa, v_w_pb, v_w_o, v_g_post_mix, v_g_pre_mlp, v_w_up, v_w_down, v_g_post_mlp):
    args = dict(locals())
    w = {n: args[n][0] for n in WEIGHTS}
    m = {n: args["m_" + n][0] for n in WEIGHTS}
    v = {n: args["v_" + n][0] for n in WEIGHTS}

    me_lin = _lin(_me())
    me_idx = jnp.reshape(me_lin, (1,)).astype(jnp.int32)

    def full_weight(n, g):
        if n in ("w_in", "w_up"):
            return g
        return _from_col_blocks(g) if n in BIG_COL_SHARDED else g.reshape(-1, g.shape[-1])

    def grad_parts(n, g):
        if n.partition("#")[0] in ("w_in", "w_up"):
            return g
        return (_col_blocks(g) if n in BIG_COL_SHARDED else g.reshape(N_DEV, -1, g.shape[-1])).astype(BF16)

    sharded_names = tuple(SMALL_SHARDED)
    narrow = {n: w[n].astype(BF16) for n in BIG}
    ready, pending, passing = {}, {}, {}
    first_state, _ = _copies_start(["gather"] * len(sharded_names) + ["gather_chips"],
                                   [_small_view(n, w[n]) for n in sharded_names] + [narrow["w_in"]], "allgather_start_first")

    def prefetch(group, after):
        state, token = _copies_start("gather_chips", [narrow[n] for n in group], "allgather_start_" + group[0], after)
        for n in group:
            pending[n] = (group, state)
        return token

    def pass_on(n, after):
        group, state = pending[n]
        shards, lands = _copies_wait(state, after, "allgather_wait_" + group[0])
        state, token = _copies_start("gather_pass", shards, "allgather_pass_" + group[0], lands=lands)
        for gn in group:
            passing[gn] = (group, state)
        return token

    def weight(n, after):
        if n not in ready:
            group, state = passing[n]
            _, lands = _copies_wait(state, after, "allgather_passed_" + group[0])
            for gn, land in zip(group, lands):
                ready[gn] = full_weight(gn, land)
        return ready[n]

    h = _first_norm(x[0], w["g_pre_mix"].reshape(1, -1))
    first_own, first_lands = _copies_wait(first_state, [h] + [narrow[n] for n in BIG if n != "w_in"], "allgather_wait_first")
    state, _ = _copies_start("gather_pass", first_own[-1:], "allgather_pass_w_in", lands=first_lands[-1:])
    passing["w_in"] = (("w_in",), state)
    ws = {n: (w[n].reshape(1, -1) if w[n].ndim == 1 else w[n]) for n in SMALL if n not in SMALL_SHARDED}
    for n, land in zip(sharded_names, first_lands):
        ws[n] = _small_unshard(n, land)

    sets, waiting_small = [], []

    def start_set(large, small):
        names = tuple(large)
        s_names, s_kinds, s_srcs = small if small else ((), [], [])
        state, token = _copies_start(s_kinds + ["exchange"] * len(names), s_srcs + [grad_parts(n, large[n]) for n in names],
                                     "exchange_start_" + (names + s_names)[0].replace("#", "_"))
        sets.append((names, s_names, s_kinds, state))
        return token

    def on_grads(grads):
        return start_set(grads, waiting_small.pop() if waiting_small else None)

    def on_small(small, loss):
        names = tuple(small)
        kinds = ["exchange" if n in SMALL_SHARDED else "gather" for n in names]
        srcs = [_small_shards(n, small[n]) if n in SMALL_SHARDED else _small_view(n, small[n]) for n in names]
        if loss is None:
            return start_set({}, (names, kinds, srcs))
        waiting_small.append((names, kinds + ["gather"], srcs + [loss]))
        return None

    grad_x, last_token = _local_step(x[0], h, loss_target[0], weight, ws, prefetch, pass_on, on_grads, on_small)

    out, chunks, sums, updated = {}, {}, [], []

    def finish_set(names, s_names, s_kinds, state, after):
        own, lands = _copies_wait(state, after, "exchange_wait_" + (names + s_names)[0].replace("#", "_"))
        ns = len(s_kinds)
        if s_names:
            k = len(s_names)
            upd = _small_update("adamw_small_" + s_names[0], me_idx, s_kinds[:k], lands[:k], own[:k],
                                *[[_small_view(n, d[n]) for n in s_names] for d in (w, m, v)],
                                sums=list(zip(lands[k:ns], own[k:ns])))
            for i, n in enumerate(s_names):
                out[n] = tuple(_small_unview(n, a, w[n].shape) for a in upd[4 * i:4 * i + 4])
            sums.extend(upd[4 * k:])
        if names == MIX:
            upd = _small_update("adamw_mix", me_idx, ["exchange"] * len(names), lands[ns:], own[ns:],
                                *[[d[n] for n in names] for d in (w, m, v)])
            for i, n in enumerate(names):
                out[n] = tuple(upd[4 * i:4 * i + 4])
            updated.append(upd[1])
            return
        for name, part, land in zip(names, own[ns:], lands[ns:]):
            n, _, chunk = name.partition("#")
            chunks.setdefault(n, []).append((land, part))
            if chunk in ("", "1"):
                got_lands, got_parts = zip(*chunks[n])
                if n == "w_in":
                    upd = _sum_adamw_cols(got_lands, got_parts, me_idx, *[_cols_view(d[n]) for d in (w, m, v)], "adamw_" + n)
                    out[n] = tuple(_from_cols_view(a, *w[n].shape) for a in upd)
                else:
                    out[n] = upd = _sum_adamw(got_lands, got_parts, me_idx, w[n], m[n], v[n], "adamw_" + n)
                updated.append(upd[1])

    for entry in sets:
        finish_set(*entry, [grad_x, last_token] + updated)
    loss_sum = sums[0]

    shaped = lambda a, n: a.reshape(args[n].shape)
    return (loss_sum[0, 0], grad_x[None],
            *[shaped(out[n][0], n) for n in WEIGHTS], *[shaped(out[n][1], n) for n in WEIGHTS],
            *[shaped(out[n][2], n) for n in WEIGHTS], *[shaped(out[n][3], n) for n in WEIGHTS])
```

```python
import functools

import jax
import jax.numpy as jnp
from jax import lax
from jax.experimental import pallas as pl
from jax.experimental.pallas import tpu as pltpu

F32 = jnp.float32
BF16 = jnp.bfloat16
MESH = pl.DeviceIdType.MESH

N_DEV = 8
EPS = 1e-6
CHUNK = 64
CHUNKS_PER_STEP = 8
HEADS = 4
GLA_DK = 64
HEAD_W = 128
GLA_GATE_NORM = 16.0
LOWRANK = 16
CONV_K = 4
QKV_BLOCK = 4
LANES = 128
HALO = 8
IN_SPLITS = (256, 256, 512, 512, 16, 512, 512, 1024, 1024)

ADAM_LR = 0.001
ADAM_B1 = 0.9
ADAM_B2 = 0.999
ADAM_EPS = 1e-08
ADAM_WD = 0.01
ADAM_STEP = 10

VMEM_LIMIT = 56 * 1024 * 1024
UPDATE_COLS_MAX = 64


def _cparams(*sem):
    return pltpu.CompilerParams(dimension_semantics=sem, vmem_limit_bytes=VMEM_LIMIT)


def _dims(mode, ndim):
    contract = {"nn": ((ndim - 1,), (ndim - 2,)), "nt": ((ndim - 1,), (ndim - 1,)), "tn": ((ndim - 2,), (ndim - 2,))}[mode]
    return contract, (((0,), (0,)) if ndim == 3 else ((), ()))


def _raw_dot(a, b, mode):
    return lax.dot_general(a.astype(BF16), b.astype(BF16), _dims(mode, a.ndim), preferred_element_type=F32)


@functools.partial(jax.custom_vjp, nondiff_argnums=(2,))
def _bdot(a, b, mode):
    return _raw_dot(a, b, mode)


def _bdot_fwd(a, b, mode):
    return _raw_dot(a, b, mode), (a, b)


def _bdot_bwd(mode, res, ct):
    a, b = res
    if mode == "nn":
        da, db = _raw_dot(ct, b, "nt"), _raw_dot(a, ct, "tn")
    elif mode == "nt":
        da, db = _raw_dot(ct, b, "nn"), _raw_dot(ct, a, "tn")
    else:
        da, db = _raw_dot(b, ct, "nt"), _raw_dot(a, ct, "nn")
    return da.astype(a.dtype), db.astype(b.dtype)


_bdot.defvjp(_bdot_fwd, _bdot_bwd)


def _split3(x):
    hi = x.astype(BF16)
    r1 = x - hi.astype(F32)
    mid = r1.astype(BF16)
    return hi, mid, (r1 - mid.astype(F32)).astype(BF16)


def _split_dot(tri, x):
    if x.ndim == 3:
        tri = jnp.broadcast_to(tri, (x.shape[0], *tri.shape))
    return sum(lax.dot_general(tri, t, _dims("nn", x.ndim), preferred_element_type=F32) for t in _split3(x))


def _tri(n, lower):
    r = lax.broadcasted_iota(jnp.int32, (n, n), 0)
    c = lax.broadcasted_iota(jnp.int32, (n, n), 1)
    return ((c <= r) if lower else (c >= r)).astype(BF16)


@jax.custom_vjp
def _cumsum_rows(x):
    return _split_dot(_tri(x.shape[-2], True), x)


def _cumsum_rows_fwd(x):
    return _cumsum_rows(x), None


def _cumsum_rows_bwd(_, ct):
    return (_split_dot(_tri(ct.shape[-2], False), ct),)


_cumsum_rows.defvjp(_cumsum_rows_fwd, _cumsum_rows_bwd)


def _abs(x):
    return jnp.where(x >= 0, x, -x)


def _sigmoid(x):
    return lax.logistic(x)


def _log_sigmoid(x):
    return jnp.minimum(x, 0.0) - jnp.log(1.0 + jnp.exp(-_abs(x)))


def _rms(x, g):
    return x * lax.rsqrt(jnp.mean(x * x, axis=-1, keepdims=True) + EPS) * g


def _head_slices(w):
    return [slice(h * w, (h + 1) * w) for h in range(HEADS)]


def _heads(ref, rows=slice(None)):
    return jnp.stack([ref[rows, hs] for hs in _head_slices(HEAD_W)])


def _put_heads(ref, val, rows=slice(None)):
    for h, hs in enumerate(_head_slices(HEAD_W)):
        ref[rows, hs] = val[h].astype(ref.dtype)


def _tile(dim, want):
    if dim <= want or dim % LANES:
        return dim
    t = want
    while dim % t:
        t -= LANES
    return t


def _mm(a, b, mode, out_dtype, name, tm=1024, tn=1024, tk=4096, epilogue=None, extra=(), deps=(), shards=None):
    if shards == "b":
        assert mode == "nn"
        ns = b.shape[2]
        (m, k), (k2, n) = a.shape, (b.shape[1], b.shape[0] * ns)
        tn = ns
    elif mode == "nn":
        (m, k), (k2, n) = a.shape, b.shape
    elif mode == "nt":
        (m, k), (n, k2) = a.shape, b.shape
    else:
        (k, m), (k2, n) = a.shape, b.shape
    assert k == k2, (name, a.shape, b.shape)
    tm, tn, tk = _tile(m, tm), _tile(n, tn), _tile(k, tk)
    nk = k // tk
    out_dtypes = out_dtype if epilogue else (out_dtype,)
    assert nk == 1 or (out_dtype == F32 and not epilogue), name
    n_in = 2 + len(extra)

    def body(*refs):
        p = _raw_dot(refs[0][...], refs[1][...], mode)
        if nk > 1:
            _accumulate(pl.program_id(2), [refs[n_in + len(deps)]], [p])
            return
        outs = epilogue(p, *[r[...] for r in refs[2:n_in]]) if epilogue else (p,)
        for ref, val in zip(refs[n_in + len(deps):], outs):
            ref[...] = val.astype(ref.dtype)

    a_spec = pl.BlockSpec((tk, tm), lambda i, j, kk: (kk, i)) if mode == "tn" else pl.BlockSpec((tm, tk), lambda i, j, kk: (i, kk))
    if shards == "b":
        b_spec = pl.BlockSpec((None, tk, tn), lambda i, j, kk: (j, kk, 0))
    elif mode == "nt":
        b_spec = pl.BlockSpec((tn, tk), lambda i, j, kk: (j, kk))
    else:
        b_spec = pl.BlockSpec((tk, tn), lambda i, j, kk: (kk, j))
    o_spec = pl.BlockSpec((tm, tn), lambda i, j, kk: (i, j))
    res = pl.pallas_call(
        body, name=name, grid=(m // tm, n // tn, nk),
        in_specs=[a_spec, b_spec] + [o_spec] * len(extra) + [ANY] * len(deps), out_specs=[o_spec] * len(out_dtypes),
        out_shape=[jax.ShapeDtypeStruct((m, n), dt) for dt in out_dtypes],
        compiler_params=_cparams("parallel", "parallel", "arbitrary"),
    )(a, b, *extra, *deps)
    return res if epilogue else res[0]


def _mm_tn_whole(pairs, name):
    def body(*refs):
        for i in range(len(pairs)):
            refs[2 * len(pairs) + i][...] = _raw_dot(refs[2 * i][...], refs[2 * i + 1][...], "tn").astype(BF16)

    return pl.pallas_call(
        body, name=name,
        out_shape=[jax.ShapeDtypeStruct((a.shape[1], b.shape[1]), BF16) for a, b in pairs],
        compiler_params=pltpu.CompilerParams(vmem_limit_bytes=VMEM_LIMIT),
    )(*[x for pair in pairs for x in pair])


def _shard_pieces(widths, n):
    bounds = [0]
    for wd in widths:
        bounds.append(bounds[-1] + wd)
    assert bounds[-1] == N_DEV * n
    return [[(i, max(s * n, b) - b, max(s * n, b) - s * n, min((s + 1) * n, b + wd) - max(s * n, b))
             for i, (b, wd) in enumerate(zip(bounds, widths)) if b < (s + 1) * n and b + wd > s * n]
            for s in range(N_DEV)]


def _mm_shard_cols(a, bs, widths, n, name, row_tile, tm, deps=()):
    t = a.shape[0]
    nb = len(bs)
    pieces = _shard_pieces(widths, n)

    def body(a_ref, *rest):
        b_refs = rest[:nb]
        o_ref, at_ref = rest[nb + len(deps):]
        j = pl.program_id(0)

        @pl.when(j == 0)
        def _():
            at_ref[...] = a_ref[...].astype(BF16).T

        for s in range(N_DEV):
            @pl.when(j == s)
            def _(s=s):
                for i, c_in, c_out, wd in pieces[s]:
                    cols = min(_round_up(wd, LANES), bs[i].shape[1] - c_in) if wd < LANES else wd
                    p = _raw_dot(at_ref[...], b_refs[i][:, c_in:c_in + cols], "nn")
                    o_ref[:, c_out:c_out + wd] = p[:, 0:wd].astype(BF16)

    return pl.pallas_call(
        body, name=name, grid=(N_DEV,),
        in_specs=[pl.BlockSpec((t, tm), lambda j: (0, row_tile))]
        + [pl.BlockSpec(b.shape, lambda j: (0, 0), pipeline_mode=pl.Buffered(1)) for b in bs] + [ANY] * len(deps),
        out_specs=pl.BlockSpec((None, tm, n), lambda j: (j, 0, 0)),
        out_shape=jax.ShapeDtypeStruct((N_DEV, tm, n), BF16),
        scratch_shapes=[pltpu.VMEM((tm, t), BF16)],
        compiler_params=_cparams("arbitrary"),
    )(a, *bs, *deps)


def _round_up(v, m):
    return -(-v // m) * m


def _rowwise(name, fn, rows, params, out_rows, out_accs=(), tile=256, deps=()):
    t = rows[0].shape[0]
    r = min(tile, t)
    assert t % r == 0
    n_in, n_or = len(rows) + len(params), len(out_rows)
    n_all = n_in + len(deps)
    params = list(params) + list(deps)

    def body(*refs):
        vals = [ref[...] for ref in refs[:n_in]]
        outs = refs[n_all:]
        ro, ao = fn(*vals)
        for ref, v in zip(outs[:n_or], ro):
            ref[...] = v.astype(ref.dtype)
        if out_accs:
            _accumulate(pl.program_id(0), outs[n_or:], ao)

    def full(shape, **kw):
        return pl.BlockSpec(shape, lambda i, nd=len(shape): (0,) * nd, **kw)

    def tiled(w):
        if isinstance(w, tuple):
            return pl.BlockSpec((w[0], r, w[1]), lambda i: (0, i, 0))
        return pl.BlockSpec((r, w), lambda i: (i, 0))

    def whole(w):
        return (w[0], t, w[1]) if isinstance(w, tuple) else (t, w)

    return pl.pallas_call(
        body, name=name, grid=(t // r,),
        in_specs=[tiled(a.shape[1] if a.ndim == 2 else (a.shape[0], a.shape[2])) for a in rows]
        + [full(p.shape, pipeline_mode=pl.Buffered(1)) for p in params],
        out_specs=[tiled(w) for w, _ in out_rows] + [full(s) for s, _ in out_accs],
        out_shape=[jax.ShapeDtypeStruct(whole(w), dt) for w, dt in out_rows] + [jax.ShapeDtypeStruct(s, dt) for s, dt in out_accs],
        compiler_params=_cparams("arbitrary"),
    )(*rows, *params)


def _accumulate(step, refs, vals):
    for ref, v in zip(refs, vals):
        @pl.when(step == 0)
        def _(ref=ref, v=v):
            ref[...] = v.astype(ref.dtype)

        @pl.when(step > 0)
        def _(ref=ref, v=v):
            ref[...] += v.astype(ref.dtype)


def _gla_chunk(q, k, v, la, st):
    c = q.shape[-2]
    row = lax.broadcasted_iota(jnp.int32, (c, c), 0)
    col = lax.broadcasted_iota(jnp.int32, (c, c), 1)
    cum = _cumsum_rows(la)
    cl = jnp.sum(la, axis=-2, keepdims=True)
    ep = jnp.exp(cum)
    en = jnp.exp(-cum)
    qs = q * (GLA_DK ** -0.5)
    qp = qs * ep
    a_f = _bdot(qp, k * en, "nt")
    a_b = _bdot(qs * en, k * ep, "nt")
    sc = jnp.where(row >= col, a_f, a_b)
    o = _bdot(sc, v, "nn") + _bdot(qp, st, "nt")
    kd = k * jnp.exp(cl - cum)
    st_new = st * jnp.exp(cl) + _bdot(v, kd, "tn")
    return o, st_new


def _gla_specs(nc, rev):
    nb = nc // CHUNKS_PER_STEP
    rows = CHUNKS_PER_STEP * CHUNK

    def blk(n):
        return (nb - 1 - n) if rev else n
    hm = pl.BlockSpec((HEADS, rows, GLA_DK), lambda n: (0, blk(n), 0))
    tm = pl.BlockSpec((rows, HEADS * HEAD_W), lambda n: (blk(n), 0))
    st = pl.BlockSpec((HEADS, CHUNKS_PER_STEP, HEAD_W, GLA_DK), lambda n: (0, blk(n), 0, 0))
    return nb, hm, tm, st


def _chunk_rows(c):
    return slice(c * CHUNK, (c + 1) * CHUNK)


def _ml_chunk(q, k, v, li_r, lf_r, cm, nv, m):
    c = q.shape[-2]
    row = lax.broadcasted_iota(jnp.int32, (c, c), 0)
    col = lax.broadcasted_iota(jnp.int32, (c, c), 1)
    eye = (row == col).astype(F32)
    li_c = jnp.sum(eye * li_r, axis=-1, keepdims=True)
    lf_c = jnp.sum(eye * lf_r, axis=-1, keepdims=True)
    fc_c = jnp.sum((col <= row).astype(F32) * lf_r, axis=-1, keepdims=True)
    fc_r = jnp.sum((row <= col).astype(F32) * lf_c, axis=-2, keepdims=True)
    f_last = jnp.sum(lf_r, axis=-1, keepdims=True)
    kc = k * (HEAD_W ** -0.5)
    a_c = f_last - fc_c + li_c
    m_loc = jnp.max(a_c, axis=-2, keepdims=True)
    kw = kc * jnp.exp(a_c - m_loc)
    c_chunk = _bdot(kw, v, "tn")
    n_chunk = jnp.sum(kw, axis=-2, keepdims=True)
    m_new = jnp.maximum(f_last + m, m_loc)
    sp = jnp.exp(f_last + m - m_new)
    sl = jnp.exp(m_loc - m_new)
    cm_new = sp * cm + sl * c_chunk
    nv_new = sp * nv + sl * n_chunk
    log_d = li_r - _abs(fc_c - fc_r)
    g_inter = fc_c + m
    m_t = jnp.maximum(g_inter, jnp.max(log_d, axis=-1, keepdims=True))
    s = _bdot(q, kc, "nt") * jnp.exp(log_d - m_t)
    sc = jnp.exp(g_inter - m_t)
    num = _bdot(s, v, "nn") + sc * _bdot(q, cm, "nn")
    den = jnp.sum(s, axis=-1, keepdims=True) + sc * jnp.sum(q * nv, axis=-1, keepdims=True)
    den = jnp.maximum(_abs(den), jnp.exp(-m_t))
    return num / den, cm_new, nv_new, m_new


def _ml_specs(nc, rev):
    nb = nc // CHUNKS_PER_STEP

    def blk(n):
        return (nb - 1 - n) if rev else n
    tm = pl.BlockSpec((CHUNKS_PER_STEP * CHUNK, HEADS * HEAD_W), lambda n: (blk(n), 0))
    gate = pl.BlockSpec((HEADS, CHUNKS_PER_STEP, 1, CHUNK), lambda n: (0, blk(n), 0, 0))
    cm = pl.BlockSpec((HEADS, CHUNKS_PER_STEP, HEAD_W, HEAD_W), lambda n: (0, blk(n), 0, 0))
    vec = pl.BlockSpec((HEADS, CHUNKS_PER_STEP, 1, HEAD_W), lambda n: (0, blk(n), 0, 0))
    return nb, tm, gate, cm, vec


_ML_STATE = [pltpu.VMEM((HEADS, HEAD_W, HEAD_W), F32), pltpu.VMEM((HEADS, 1, HEAD_W), F32), pltpu.VMEM((HEADS, 1, HEAD_W), F32)]


def _recurrences_fwd(q, k, v, la, qm, km, vm, li, lf, deps=()):
    t = v.shape[0]
    nc = t // CHUNK
    nb, hm, tm, st = _gla_specs(nc, False)
    _, _, gate, cm, vec = _ml_specs(nc, False)
    n_in = 9 + len(deps)

    def body(*refs):
        q_ref, k_ref, v_ref, la_ref, qm_ref, km_ref, vm_ref, li_ref, lf_ref = refs[:9]
        o_ref, sp_ref, hc_ref, cp_ref, np_ref, mp_ref, st_ref, c_ref, n_ref, m_ref = refs[n_in:]

        @pl.when(pl.program_id(0) == 0)
        def _():
            for ref in (st_ref, c_ref, n_ref, m_ref):
                ref[...] = jnp.zeros_like(ref)

        s, cs, ns, ms = st_ref[...], c_ref[...], n_ref[...], m_ref[...][:, :, 0:1]
        for c in range(CHUNKS_PER_STEP):
            r = _chunk_rows(c)
            sp_ref[:, c] = s
            o, s = _gla_chunk(q_ref[:, r], k_ref[:, r], _heads(v_ref, r), la_ref[:, r], s)
            _put_heads(o_ref, o, r)
            cp_ref[:, c] = cs
            np_ref[:, c] = ns
            mp_ref[:, c] = jnp.broadcast_to(ms, m_ref.shape)
            hc, cs, ns, ms = _ml_chunk(_heads(qm_ref, r), _heads(km_ref, r), _heads(vm_ref, r), li_ref[:, c], lf_ref[:, c],
                                       cs, ns, ms)
            _put_heads(hc_ref, hc, r)
        st_ref[...] = s
        c_ref[...] = cs
        n_ref[...] = ns
        m_ref[...] = jnp.broadcast_to(ms, m_ref.shape)

    tm_shape = jax.ShapeDtypeStruct((t, HEADS * HEAD_W), F32)
    vec_shape = jax.ShapeDtypeStruct((HEADS, nc, 1, HEAD_W), F32)
    return pl.pallas_call(
        body, name="recurrences_fwd", grid=(nb,),
        in_specs=[hm, hm, tm, hm, tm, tm, tm, gate, gate] + [ANY] * len(deps), out_specs=[tm, st, tm, cm, vec, vec],
        out_shape=[tm_shape, jax.ShapeDtypeStruct((HEADS, nc, HEAD_W, GLA_DK), F32),
                   tm_shape, jax.ShapeDtypeStruct((HEADS, nc, HEAD_W, HEAD_W), F32), vec_shape, vec_shape],
        scratch_shapes=[pltpu.VMEM((HEADS, HEAD_W, GLA_DK), F32)] + _ML_STATE,
        compiler_params=_cparams("arbitrary"),
    )(q, k, v, la, qm, km, vm, li, lf, *deps)


def _recurrences_bwd(q, k, v, la, sp, do, qm, km, vm, li, lf, cp, npv, mp, dhc, deps=()):
    t = v.shape[0]
    nc = t // CHUNK
    nb, hm, tm, st = _gla_specs(nc, True)
    _, _, gate, cm, vec = _ml_specs(nc, True)
    n_in = 15 + len(deps)

    def body(*refs):
        (q_ref, k_ref, v_ref, la_ref, sp_ref, do_ref,
         qm_ref, km_ref, vm_ref, li_ref, lf_ref, cp_ref, np_ref, mp_ref, dhc_ref) = refs[:15]
        (dq_ref, dk_ref, dv_ref, dla_ref, dqm_ref, dkm_ref, dvm_ref, dli_ref, dlf_ref,
         ds_ref, dc_ref, dn_ref, dm_ref) = refs[n_in:]

        @pl.when(pl.program_id(0) == 0)
        def _():
            for ref in (ds_ref, dc_ref, dn_ref, dm_ref):
                ref[...] = jnp.zeros_like(ref)

        ds, dc, dn, dm = ds_ref[...], dc_ref[...], dn_ref[...], dm_ref[...][:, :, 0:1]
        for c in reversed(range(CHUNKS_PER_STEP)):
            r = _chunk_rows(c)
            _, vjp = jax.vjp(_gla_chunk, q_ref[:, r], k_ref[:, r], _heads(v_ref, r), la_ref[:, r], sp_ref[:, c])
            dq, dk, dv, dla, ds = vjp((_heads(do_ref, r), ds))
            dq_ref[:, r] = dq.astype(dq_ref.dtype)
            dk_ref[:, r] = dk.astype(dk_ref.dtype)
            _put_heads(dv_ref, dv, r)
            dla_ref[:, r] = dla
            _, vjp = jax.vjp(_ml_chunk, _heads(qm_ref, r), _heads(km_ref, r), _heads(vm_ref, r), li_ref[:, c], lf_ref[:, c],
                             cp_ref[:, c], np_ref[:, c], mp_ref[:, c][:, :, 0:1])
            dqm, dkm, dvm, dli, dlf, dc, dn, dm = vjp((_heads(dhc_ref, r), dc, dn, dm))
            _put_heads(dqm_ref, dqm, r)
            _put_heads(dkm_ref, dkm, r)
            _put_heads(dvm_ref, dvm, r)
            dli_ref[:, c] = dli
            dlf_ref[:, c] = dlf
        ds_ref[...] = ds
        dc_ref[...] = dc
        dn_ref[...] = dn
        dm_ref[...] = jnp.broadcast_to(dm, dm_ref.shape)

    hm_shape = jax.ShapeDtypeStruct((HEADS, t, GLA_DK), BF16)
    tm_shape = jax.ShapeDtypeStruct((t, HEADS * HEAD_W), F32)
    gate_shape = jax.ShapeDtypeStruct((HEADS, nc, 1, CHUNK), F32)
    return pl.pallas_call(
        body, name="recurrences_bwd", grid=(nb,),
        in_specs=[hm, hm, tm, hm, st, tm, tm, tm, tm, gate, gate, cm, vec, vec, tm] + [ANY] * len(deps),
        out_specs=[hm, hm, tm, hm, tm, tm, tm, gate, gate],
        out_shape=[hm_shape, hm_shape, jax.ShapeDtypeStruct((t, HEADS * HEAD_W), BF16),
                   jax.ShapeDtypeStruct((HEADS, t, GLA_DK), F32), tm_shape, tm_shape, tm_shape, gate_shape, gate_shape],
        scratch_shapes=[pltpu.VMEM((HEADS, HEAD_W, GLA_DK), F32)] + _ML_STATE,
        compiler_params=_cparams("arbitrary"),
    )(q, k, v, la, sp, do, qm, km, vm, li, lf, cp, npv, mp, dhc, *deps)


@jax.custom_vjp
def _bdot_diag(x, w):
    b = w.shape[1]
    return jnp.concatenate([_raw_dot(x[:, :b], w[0], "nn"), _raw_dot(x[:, b:], w[1], "nn")], axis=1)


def _bdot_diag_fwd(x, w):
    return _bdot_diag(x, w), (x, w)


def _bdot_diag_bwd(res, ct):
    x, w = res
    b = w.shape[1]
    dx = jnp.concatenate([_raw_dot(ct[:, :b], w[0], "nt"), _raw_dot(ct[:, b:], w[1], "nt")], axis=1)
    dw = jnp.stack([_raw_dot(x[:, :b], ct[:, :b], "tn"), _raw_dot(x[:, b:], ct[:, b:], "tn")])
    return dx.astype(x.dtype), dw.astype(w.dtype)


_bdot_diag.defvjp(_bdot_diag_fwd, _bdot_diag_bwd)


def _ml_pre(s0, s1, s2, s3, cw0, cw1, cw2, cw3, cb, wq, wk, wv, wiq, wik, wiv, bif):
    pre = cb + cw0 * s0 + cw1 * s1 + cw2 * s2 + cw3 * s3
    xc = pre * _sigmoid(pre)
    q = _bdot_diag(xc, wq)
    k = _bdot_diag(xc, wk)
    v = _bdot_diag(s3, wv)
    gates = _bdot(q, wiq, "nn") + _bdot(k, wik, "nn") + _bdot(v, wiv, "nn") + bif
    lane = lax.broadcasted_iota(jnp.int32, gates.shape, 1)
    gl = jnp.where(lane < HEADS, gates, _log_sigmoid(gates))
    return xc, q, k, v, gl


def _delayed(xs_ref, x_ref, halo_ref, r, first):
    xs_ref[0:HALO, :] = jnp.where(first, 0.0, halo_ref[...])
    xs_ref[HALO:HALO + r, :] = x_ref[...]
    return [xs_ref[pl.ds(HALO - (CONV_K - 1) + j, r), :] for j in range(CONV_K)]


def _halo_spec(r, w, tile_of):
    return pl.BlockSpec((HALO, w), lambda i: (jnp.maximum(tile_of(i) * (r // HALO) - 1, 0), 0))


def _full_spec(shape):
    return pl.BlockSpec(shape, lambda i, nd=len(shape): (0,) * nd)


def _ml_pre_fwd(x_m, params, tile=256, deps=()):
    t, w = x_m.shape
    r = min(tile, t)

    def body(*refs):
        x_ref, halo_ref = refs[:2]
        p = [ref[...] for ref in refs[2:2 + len(params)]]
        outs = refs[2 + len(params) + len(deps):-1]
        res = _ml_pre(*_delayed(refs[-1], x_ref, halo_ref, r, pl.program_id(0) == 0), *p)
        for ref, val in zip(outs, res):
            ref[...] = val

    row = pl.BlockSpec((r, w), lambda i: (i, 0))
    return pl.pallas_call(
        body, name="ml_pre_fwd", grid=(t // r,),
        in_specs=[row, _halo_spec(r, w, lambda i: i)] + [_full_spec(p.shape) for p in params]
        + [ANY] * len(deps),
        out_specs=[row] * 4 + [pl.BlockSpec((r, LANES), lambda i: (i, 0))],
        out_shape=[jax.ShapeDtypeStruct((t, w), F32)] * 4 + [jax.ShapeDtypeStruct((t, LANES), F32)],
        scratch_shapes=[pltpu.VMEM((r + HALO, w), F32)],
        compiler_params=_cparams("arbitrary"),
    )(x_m, x_m, *params, *deps)


def _ml_pre_bwd(x_m, params, cts, tile=256):
    t, w = x_m.shape
    r = min(tile, t)
    nt = t // r
    n_p = len(params)

    def body(*refs):
        x_ref, halo_ref = refs[:2]
        p = [ref[...] for ref in refs[2:2 + n_p]]
        ct = [ref[...] for ref in refs[2 + n_p:7 + n_p]]
        dx_ref = refs[7 + n_p]
        dp_refs = refs[8 + n_p:8 + 2 * n_p]
        xs_ref, ds_ref, carry_ref = refs[8 + 2 * n_p:]
        step = pl.program_id(0)

        @pl.when(step == 0)
        def _():
            ds_ref[...] = jnp.zeros_like(ds_ref)
            carry_ref[...] = jnp.zeros_like(carry_ref)

        _, vjp = jax.vjp(_ml_pre, *_delayed(xs_ref, x_ref, halo_ref, r, step == nt - 1), *p)
        grads = vjp(tuple(ct))
        for j in range(CONV_K):
            ds_ref[j, HALO:HALO + r, :] = grads[j]
        lead = HALO + CONV_K - 1
        d_tile = sum(ds_ref[j, pl.ds(lead - j, r), :] for j in range(CONV_K))
        d_halo = sum(ds_ref[j, pl.ds(CONV_K - 1 - j, HALO), :] for j in range(CONV_K))
        dx_ref[...] = jnp.concatenate([d_tile[:r - HALO], d_tile[r - HALO:] + carry_ref[...]], axis=0).astype(dx_ref.dtype)
        carry_ref[...] = d_halo
        _accumulate(step, dp_refs, grads[CONV_K:])

    row = pl.BlockSpec((r, w), lambda i: (nt - 1 - i, 0))
    return pl.pallas_call(
        body, name="ml_pre_bwd", grid=(nt,),
        in_specs=[row, _halo_spec(r, w, lambda i: nt - 1 - i)] + [_full_spec(p.shape) for p in params]
        + [row] * 4 + [pl.BlockSpec((r, LANES), lambda i: (nt - 1 - i, 0))],
        out_specs=[row] + [_full_spec(p.shape) for p in params],
        out_shape=[jax.ShapeDtypeStruct((t, w), BF16)] + [jax.ShapeDtypeStruct(p.shape, F32) for p in params],
        scratch_shapes=[pltpu.VMEM((r + HALO, w), F32), pltpu.VMEM((CONV_K, r + 2 * HALO, w), F32), pltpu.VMEM((HALO, w), F32)],
        compiler_params=_cparams("arbitrary"),
    )(x_m, x_m, *params, *cts)


def _per_head(fn, row_vals, head_params, shared_params=()):
    return [fn(*[a[:, hs] for a in row_vals], *[p[:, hs] for p in head_params], *shared_params) for hs in _head_slices(HEAD_W)]


def _gla_out(o, g, gn):
    return _rms(o, gn) * (g * _sigmoid(g))


def _ml_out(hc, op, xc, g, sk):
    hcell = hc * _sigmoid(op)
    mu = jnp.mean(hcell, axis=-1, keepdims=True)
    d = hcell - mu
    var = jnp.mean(d * d, axis=-1, keepdims=True)
    return d * lax.rsqrt(var + EPS) * g + sk * xc


def _log_decay(al, w, b):
    return _log_sigmoid(_bdot(al, w, "nn") + b) * (1.0 / GLA_GATE_NORM)


def _merge(ga, gb, ya, yb):
    ga, gb, ya, yb = (a.astype(F32) for a in (ga, gb, ya, yb))
    return _sigmoid(ga) * ya + _sigmoid(gb) * yb


def _post_mix(x, z, gpm, gpl):
    x1 = x + _rms(z, gpm)
    return x1, _rms(x1, gpl)


def _loss_rows(x1, dn, tgt, g):
    e = x1 + _rms(dn, g) - tgt
    return 0.5 * jnp.sum(jnp.mean(e * e, axis=-1, keepdims=True), axis=0, keepdims=True)


def _lin(p):
    return 4 * p[0] + 2 * p[1] + p[2]


def _me():
    return lax.axis_index("x"), lax.axis_index("y"), lax.axis_index("c")


def _flip(p, k):
    return tuple((1 - v) if (k >> (2 - i)) & 1 else v for i, v in enumerate(p))


ANY = pl.BlockSpec(memory_space=pl.ANY)


HBM = pl.BlockSpec(memory_space=pltpu.HBM)
SEM = pl.BlockSpec(memory_space=pltpu.SEMAPHORE)
DATAFLOW = pltpu.SideEffectType.DATAFLOW_SIDE_EFFECTING


SIBLING = 1
OTHER_CHIPS = (2, 4, 6)


def _peer_copies(kinds, srcs, lands, send_sems, recv_sems):
    me = _me()
    copies = []
    for a, (kind, src, land) in enumerate(zip(kinds, srcs, lands)):
        masks = {"gather": range(1, N_DEV), "exchange": range(1, N_DEV), "gather_chips": (SIBLING, *OTHER_CHIPS),
                 "gather_pass": OTHER_CHIPS}[kind]
        for k in masks:
            peer = _flip(me, k)
            if kind == "gather_pass":
                block = land.at[_lin(peer)]
                src_ref, dst_ref, target = block, block, _flip(me, SIBLING)
            else:
                src_ref, dst_ref, target = (src.at[_lin(peer)] if kind == "exchange" else src), land.at[_lin(me)], peer
            copies.append(pltpu.make_async_remote_copy(
                src_ref=src_ref, dst_ref=dst_ref, send_sem=send_sems.at[a * 7 + k - 1], recv_sem=recv_sems.at[a * 7 + k - 1],
                device_id=target, device_id_type=MESH))
    return copies


def _own_copies(kinds, srcs, lands, own_sems):
    return [pltpu.make_async_copy(src, land.at[_lin(_me())], own_sems.at[a])
            for a, (kind, src, land) in enumerate(zip(kinds, srcs, lands)) if kind in ("gather", "gather_chips")]


def _copies_start(kind, srcs, name, after=None, lands=None):
    n = len(srcs)
    extra = [] if after is None else [after]
    kind = [kind] * n if isinstance(kind, str) else list(kind)
    land_shapes = [(s.shape if k == "exchange" else (N_DEV, *s.shape)) for k, s in zip(kind, srcs)]
    lands = [lax.empty(ls, s.dtype) for ls, s in zip(land_shapes, srcs)] if lands is None else lands

    def body(*refs):
        sems = refs[2 * n + len(extra):]
        for cp in _peer_copies(kind, refs[:n], refs[n:2 * n], sems[0], sems[1]) + _own_copies(kind, refs[:n], refs[n:2 * n], sems[2]):
            cp.start()
        refs[-1][...] = jnp.zeros_like(refs[-1])

    def hbm(a):
        return pltpu.with_memory_space_constraint(a, pltpu.HBM)

    out = pl.pallas_call(
        body, name=name,
        out_shape=(pltpu.SemaphoreType.DMA((7 * n,)), pltpu.SemaphoreType.DMA((7 * n,)), pltpu.SemaphoreType.DMA((n,)),
                   *[pltpu.HBM(s.shape, s.dtype) for s in srcs],
                   *[pltpu.HBM(ls, s.dtype) for ls, s in zip(land_shapes, srcs)],
                   jax.ShapeDtypeStruct((8, LANES), F32)),
        in_specs=[HBM] * (2 * n) + [ANY] * len(extra),
        out_specs=(SEM, SEM, SEM, *[HBM] * (2 * n), pl.BlockSpec(memory_space=pltpu.VMEM)),
        input_output_aliases={i: 3 + i for i in range(2 * n)},
        compiler_params=pltpu.CompilerParams(has_side_effects=DATAFLOW),
    )(*[hbm(s) for s in srcs], *[hbm(a) for a in lands], *extra)
    return (kind, n, out[:-1]), out[-1]


def _copies_wait(state, after, name):
    kind, n, (send_sems, recv_sems, own_sems, *thru) = state
    after = list(after) if isinstance(after, (list, tuple)) else [after]

    def body(*refs):
        for cp in _peer_copies(kind, refs[:n], refs[n:2 * n], refs[2 * n], refs[2 * n + 1]):
            cp.wait_send()
            cp.wait_recv()
        for cp in _own_copies(kind, refs[:n], refs[n:2 * n], refs[2 * n + 2]):
            cp.wait()

    out = pl.pallas_call(
        body, name=name,
        out_shape=tuple(pltpu.HBM(t.shape, t.dtype) for t in thru),
        in_specs=[HBM] * (2 * n) + [SEM, SEM, SEM] + [ANY] * len(after), out_specs=tuple([HBM] * (2 * n)),
        input_output_aliases={i: i for i in range(2 * n)},
        compiler_params=pltpu.CompilerParams(has_side_effects=DATAFLOW),
    )(*thru, send_sems, recv_sems, own_sems, *after)
    return out[:n], out[n:]


def _adamw(w, g, m, v):
    m2 = ADAM_B1 * m + (1.0 - ADAM_B1) * g
    v2 = ADAM_B2 * v + (1.0 - ADAM_B2) * (g * g)
    m_hat = m2 / (1.0 - ADAM_B1 ** ADAM_STEP)
    v_hat = v2 / (1.0 - ADAM_B2 ** ADAM_STEP)
    delta = -ADAM_LR * (m_hat / (jnp.sqrt(v_hat) + ADAM_EPS) + ADAM_WD * w)
    return delta, m2, v2


def _sum_adamw(lands, parts, me_idx, w, m, v, name, tile=256):
    r, c = w.shape
    nchunks = len(lands)
    tr = min(tile, r // nchunks)
    per_chunk = r // nchunks // tr
    per = 1 + N_DEV

    def body(me_ref, *refs):
        w_ref, m_ref, v_ref, g_ref, d_ref, m2_ref, v2_ref = refs[nchunks * per:]
        for k in range(nchunks):
            own_ref, slots = refs[k * per], refs[k * per + 1:(k + 1) * per]

            @pl.when(pl.program_id(0) // per_chunk == k)
            def _(own_ref=own_ref, slots=slots):
                own = own_ref[...].astype(F32)
                g = None
                for s in range(N_DEV):
                    term = jnp.where(me_ref[0] == s, own, slots[s][...].astype(F32))
                    g = term if g is None else g + term
                d, m2, v2 = _adamw(w_ref[...], g, m_ref[...], v_ref[...])
                g_ref[...] = g
                d_ref[...] = d
                m2_ref[...] = m2
                v2_ref[...] = v2

    def chunk_specs(k):
        def tile_of(i):
            return jnp.clip(i - k * per_chunk, 0, per_chunk - 1)

        def slot_spec(s):
            return pl.BlockSpec((None, tr, c), lambda i, me: (jnp.where(me[0] == s, (s + 1) % N_DEV, s), tile_of(i), 0))
        return [pl.BlockSpec((None, tr, c), lambda i, me: (me[0], tile_of(i), 0))] + [slot_spec(s) for s in range(N_DEV)]

    row = pl.BlockSpec((tr, c), lambda i, me: (i, 0))
    operands = [a for land, part in zip(lands, parts) for a in (part, *[land] * N_DEV)]
    return pl.pallas_call(
        body, name=name,
        grid_spec=pltpu.PrefetchScalarGridSpec(
            num_scalar_prefetch=1, grid=(r // tr,),
            in_specs=[s for k in range(nchunks) for s in chunk_specs(k)] + [row] * 3,
            out_specs=[row] * 4),
        out_shape=[jax.ShapeDtypeStruct((r, c), F32)] * 4,
        compiler_params=_cparams("parallel"),
    )(me_idx, *operands, w, m, v)


def _cols_view(a):
    r, c = a.shape
    return jnp.transpose(a.reshape(r // LANES, LANES, c), (2, 0, 1))


def _from_cols_view(a, r, c):
    return jnp.transpose(a, (1, 2, 0)).reshape(r, c)


def _sum_adamw_cols(lands, parts, me_idx, w, m, v, name):
    nchunks = len(lands)
    rows_k, c = lands[0].shape[1:]
    r = nchunks * rows_k
    steps = r // LANES
    per_chunk = rows_k // LANES
    per = 1 + N_DEV
    c_pad = _round_up(c, LANES)
    assert steps == 8
    block = steps * max(n for n in range(1, UPDATE_COLS_MAX + 1) if c % n == 0)

    def body(me_ref, *refs):
        w_ref, m_ref, v_ref, g_ref, d_ref, m2_ref, v2_ref, gt_ref = refs[nchunks * per:]
        for i in range(steps):
            k = i // per_chunk
            own_ref, slots = refs[k * per], refs[k * per + 1:(k + 1) * per]

            @pl.when(pl.program_id(0) == i)
            def _(i=i, own_ref=own_ref, slots=slots):
                own = own_ref[...].astype(F32)
                g = None
                for s in range(N_DEV):
                    term = jnp.where(me_ref[0] == s, own, slots[s][...].astype(F32))
                    g = term if g is None else g + term
                g = jnp.concatenate([g, jnp.zeros((LANES, c_pad - c), F32)], axis=1)
                for j in range(c_pad // LANES):
                    cols = min(LANES, c - j * LANES)
                    gt_ref[pl.ds(steps * LANES * j + i, cols, stride=steps), :] = jnp.transpose(g[:, j * LANES:(j + 1) * LANES])[0:cols]

        @pl.when(pl.program_id(0) == steps - 1)
        def _():
            def update(b, carry):
                cols = pl.ds(b * (block // steps), block // steps)
                g = gt_ref[pl.ds(pl.multiple_of(b * block, steps), block), :].reshape(block // steps, steps, LANES)
                d, m2, v2 = _adamw(w_ref[cols], g, m_ref[cols], v_ref[cols])
                g_ref[cols] = g
                d_ref[cols] = d
                m2_ref[cols] = m2
                v2_ref[cols] = v2
                return carry
            lax.fori_loop(0, c * steps // block, update, 0)

    def chunk_specs(k):
        def tile_of(i):
            return jnp.clip(i - k * per_chunk, 0, per_chunk - 1)

        def slot_spec(s):
            return pl.BlockSpec((None, LANES, c), lambda i, me: (jnp.where(me[0] == s, (s + 1) % N_DEV, s), tile_of(i), 0))
        return [pl.BlockSpec((None, LANES, c), lambda i, me: (me[0], tile_of(i), 0))] + [slot_spec(s) for s in range(N_DEV)]

    whole = pl.BlockSpec((c, steps, LANES), lambda i, me: (0, 0, 0))
    operands = [a for land, part in zip(lands, parts) for a in (part, *[land] * N_DEV)]
    return pl.pallas_call(
        body, name=name,
        grid_spec=pltpu.PrefetchScalarGridSpec(
            num_scalar_prefetch=1, grid=(steps,),
            in_specs=[s for k in range(nchunks) for s in chunk_specs(k)]
            + [pl.BlockSpec((c, steps, LANES), lambda i, me: (0, 0, 0), pipeline_mode=pl.Buffered(1))] * 3,
            out_specs=[whole] * 4,
            scratch_shapes=[pltpu.VMEM((c * steps, LANES), F32)]),
        out_shape=[jax.ShapeDtypeStruct((c, steps, LANES), F32)] * 4,
        compiler_params=_cparams("arbitrary"),
    )(me_idx, *operands, w, m, v)


def _small_update(name, me_idx, kinds, lands, owns, ws, ms, vs, sums=()):
    n = len(ws)
    lands, owns = list(lands) + [s[0] for s in sums], list(owns) + [s[1] for s in sums]
    kinds = list(kinds) + ["gather"] * len(sums)
    nl = len(lands)

    def summed(me, land_ref, own):
        g = None
        for s in range(N_DEV):
            term = jnp.where(me == s, own, land_ref[s]).astype(F32)
            g = term if g is None else g + term
        return g

    def body(me_ref, *refs):
        land_refs, own_refs = refs[:nl], refs[nl:2 * nl]
        w_refs, m_refs, v_refs = (refs[2 * nl + i * n:2 * nl + (i + 1) * n] for i in range(3))
        outs = refs[2 * nl + 3 * n:]
        me = me_ref[0]
        for i in range(n):
            g = summed(me, land_refs[i], own_refs[i][...])
            d, m2, v2 = _adamw(w_refs[i][...], g, m_refs[i][...], v_refs[i][...])
            for ref, val in zip(outs[4 * i:4 * i + 4], (g, d, m2, v2)):
                ref[...] = val
        for i in range(n, nl):
            outs[4 * n + i - n][...] = summed(me, land_refs[i], own_refs[i][...])

    def whole(shape):
        return pl.BlockSpec(shape, lambda i, me, nd=len(shape): (0,) * nd)

    def own_spec(kind, own):
        if kind == "gather":
            return whole(own.shape)
        return pl.BlockSpec((None, *own.shape[1:]), lambda i, me: (me[0], 0, 0))

    shapes = [w.shape for w in ws]
    out_shapes = [s for s in shapes for _ in range(4)] + [s[1].shape for s in sums]
    return pl.pallas_call(
        body, name=name,
        grid_spec=pltpu.PrefetchScalarGridSpec(
            num_scalar_prefetch=1, grid=(1,),
            in_specs=[whole(a.shape) for a in lands] + [own_spec(k, o) for k, o in zip(kinds, owns)]
            + [whole(s) for s in shapes] * 3,
            out_specs=[whole(s) for s in out_shapes]),
        out_shape=[jax.ShapeDtypeStruct(s, F32) for s in out_shapes],
        compiler_params=_cparams("arbitrary"),
    )(me_idx, *lands, *owns, *ws, *ms, *vs)


def _small_view(n, a):
    if a.ndim == 1:
        return a.reshape(1, -1)
    if a.ndim == 3:
        return a.transpose(1, 2, 0).reshape(QKV_BLOCK * QKV_BLOCK, -1)
    return a.T if n == "w_if" else a


def _small_unview(n, a, shape):
    if len(shape) == 1:
        return a.reshape(shape)
    if len(shape) == 3:
        return a.reshape(QKV_BLOCK, QKV_BLOCK, -1).transpose(2, 0, 1)
    return a.T if n == "w_if" else a


def _small_shards(n, g):
    if n == "w_if":
        return g.reshape(N_DEV, -1, g.shape[1]).transpose(0, 2, 1)
    return g.reshape(g.shape[0], N_DEV, -1).transpose(1, 0, 2)


def _small_unshard(n, s):
    if n == "w_if":
        return s.transpose(0, 2, 1).reshape(-1, s.shape[1])
    return s.transpose(1, 0, 2).reshape(s.shape[1], -1)


def _from_hm(a):
    h, t, d = a.shape
    return a.transpose(1, 0, 2).reshape(t, h * d)


def _gate_rows(g):
    t = g.shape[0]
    return g.T.reshape(HEADS, t // CHUNK, 1, CHUNK)


def _gate_cols(g):
    h, nc, _, c = g.shape
    return g.reshape(h, nc * c).T


def _blockdiag_dense(w):
    n = w.shape[0] * QKV_BLOCK // 2
    tiled = jnp.tile(w.reshape(2, n, QKV_BLOCK), (1, 1, n // QKV_BLOCK))
    r = lax.broadcasted_iota(jnp.int32, (2, n, n), 1)
    c = lax.broadcasted_iota(jnp.int32, (2, n, n), 2)
    return jnp.where(r // QKV_BLOCK == c // QKV_BLOCK, tiled, 0.0)


def _blockdiag_blocks(dense):
    _, n, _ = dense[0].shape
    k = len(dense)

    def body(*refs):
        r = lax.broadcasted_iota(jnp.int32, (n, n), 0)
        c = lax.broadcasted_iota(jnp.int32, (n, n), 1)
        fr = lax.broadcasted_iota(jnp.int32, (n, LANES), 0)
        fc = lax.broadcasted_iota(jnp.int32, (n, LANES), 1)
        fold = ((fr & (QKV_BLOCK - 1)) == fc).astype(BF16)
        for i in range(k):
            for half in range(2):
                kept = jnp.where((r >> 2) == (c >> 2), refs[i][half], 0.0)
                refs[k + i][half] = sum(lax.dot_general(t, fold, _dims("nn", 2), preferred_element_type=F32)
                                        for t in _split3(kept))

    out = pl.pallas_call(body, name="blockdiag_blocks", out_shape=[jax.ShapeDtypeStruct((2, n, LANES), F32)] * k)(*dense)
    return [o[:, :, 0:QKV_BLOCK].reshape(2 * n // QKV_BLOCK, QKV_BLOCK, QKV_BLOCK) for o in out]


def _col_blocks(w):
    k, n = w.shape
    return w.reshape(k, N_DEV, n // N_DEV).transpose(1, 0, 2)


def _from_col_blocks(g):
    d, k, n = g.shape
    return g.transpose(1, 0, 2).reshape(k, d * n)


def _first_norm(x, g):
    return _rowwise("pre_mix_norm", lambda xv, gv: ((_rms(xv, gv),), ()), [x], [g], [(x.shape[1], BF16)])[0]


def _local_step(x, h, tgt, weight, ws, prefetch, pass_on, on_grads, on_small):
    t, d = x.shape
    g1 = ws["g_pre_mix"]

    def dep(token):
        return () if token is None else (token,)

    w_in = weight("w_in", x)
    fetch_mix = prefetch(("w_pa", "w_pb", "w_o"), w_in)
    fetch_up = prefetch(("w_up", "w_down"), fetch_mix)

    n_in = w_in.shape[2]

    offs = [0]
    for s in IN_SPLITS:
        offs.append(offs[-1] + s)

    w_a_up_p = jnp.pad(ws["w_a_up"], ((0, LANES - LOWRANK), (0, 0)))
    b_a_up = ws["b_a_up"]

    def proj_in_fwd(hv, w, wa, ba):
        proj = jnp.concatenate([_raw_dot(hv, w[j], "nn") for j in range(N_DEV)], axis=1)
        parts = [proj[:, offs[i]:offs[i + 1]] for i in range(len(IN_SPLITS))]
        parts[4] = jnp.concatenate([parts[4], jnp.zeros((parts[4].shape[0], LANES - LOWRANK), F32)], axis=1)
        head_major = lambda a: jnp.stack([a[:, hs] for hs in _head_slices(GLA_DK)])
        return (head_major(parts[0]), head_major(parts[1]), *parts[2:], head_major(_log_decay(parts[4], wa, ba))), ()

    widths = [LANES if s == LOWRANK else s for s in IN_SPLITS]
    gla_hm = ((HEADS, GLA_DK), F32)
    q_hm, k_hm, v_a, g_a, a_low_p, x_m, o_pre, gate_a, gate_b, la_hm = _rowwise(
        "proj_in", proj_in_fwd, [h], [w_in, w_a_up_p, b_a_up],
        [gla_hm, gla_hm] + [(wd, BF16 if i == 2 else F32) for i, wd in enumerate(widths)][2:] + [gla_hm],
        deps=dep(fetch_up))
    gn = ws["g_gla_norm"]
    ml_w = HEADS * HEAD_W

    cw = ws["conv_w"]
    w_if_p = jnp.pad(ws["w_if"], ((0, 0), (0, LANES - 2 * HEADS)))
    pre_params = [cw[0:1], cw[1:2], cw[2:3], cw[3:4], ws["conv_b"],
                  _blockdiag_dense(ws["w_q_ml"]), _blockdiag_dense(ws["w_k_ml"]), _blockdiag_dense(ws["w_v_ml"]),
                  w_if_p[0:ml_w], w_if_p[ml_w:2 * ml_w], w_if_p[2 * ml_w:3 * ml_w],
                  jnp.pad(ws["b_if"], ((0, 0), (0, LANES - 2 * HEADS)))]
    xc, q_m, k_m, v_m, gl = _ml_pre_fwd(x_m, pre_params, tile=512)
    pass_mix = pass_on("w_pa", xc)
    li, lf = _gate_rows(gl[:, 0:HEADS]), _gate_rows(gl[:, HEADS:2 * HEADS])
    o_gla, s_prev, hc, c_prev, n_prev, m_prev = _recurrences_fwd(q_hm, k_hm, v_a, la_hm, q_m, k_m, v_m, li, lf,
                                                                 deps=dep(pass_mix))
    g_ml, skip = ws["g_ml_norm"], ws["ml_skip"]

    def branches_out(o, g, a, b, c_, ga, gb, n_, wa, gm, s, wb):
        ya_in = jnp.concatenate(_per_head(_gla_out, [o, g], [], [n_]), axis=1)
        ya = _raw_dot(ya_in, wa, "nn")
        hb = jnp.concatenate(_per_head(_ml_out, [a, b, c_], [gm, s]), axis=1)
        yb = _raw_dot(hb, wb, "nn")
        return (ya_in, ya, hb, yb, _merge(ga, gb, ya, yb)), ()

    ya_in, y_a, h_b, y_b, merged = _rowwise(
        "branches_out", branches_out, [o_gla, g_a, hc, o_pre, xc, gate_a, gate_b],
        [gn, weight("w_pa", hc), g_ml, skip, weight("w_pb", hc)],
        [(ml_w, BF16), (d, BF16), (ml_w, BF16), (d, BF16), (d, BF16)], tile=512)

    gpm, gpl, gpo = ws["g_post_mix"], ws["g_pre_mlp"], ws["g_post_mlp"]

    def proj_o_fwd(mg, xv, w, a, b):
        zv = _raw_dot(mg, w, "nn")
        return (zv, *_post_mix(xv, zv, a, b)), ()

    pass_up = pass_on("w_up", merged)
    z, x1, h2 = _rowwise("proj_o", proj_o_fwd, [merged, x], [weight("w_o", merged), gpm, gpl],
                         [(d, F32), (d, F32), (d, BF16)], tile=512, deps=dep(pass_up))
    w_up = weight("w_up", h2)
    w_down = weight("w_down", h2)
    d_ff = w_down.shape[0]

    def mlp_loss(h2v, x1v, tgtv, wu, wd, g):
        upv = jnp.concatenate([_raw_dot(h2v, wu[j], "nn") for j in range(wu.shape[0])], axis=1)
        uv = jnp.square(jnp.maximum(upv, 0.0))
        dnv = _raw_dot(uv, wd, "nn")
        loss, vjp = jax.vjp(lambda a, b, c_: _loss_rows(a, b, tgtv, c_), x1v, dnv, g)
        dx1, ddn, dg = vjp(jnp.ones((1, 1), F32))
        return (upv, uv, dx1, ddn), (jnp.broadcast_to(loss, (1, LANES)), dg)

    up, u, dx1_y, d_dn, loss, d_gpo = _rowwise("mlp_loss", mlp_loss, [h2, x1, tgt], [w_up, w_down, gpo],
                                               [(d_ff, BF16), (d_ff, BF16), (d, F32), (d, BF16)],
                                               [((1, LANES), F32), ((1, d), F32)])

    dw_down = _mm(u, d_dn, "tn", BF16, "mlp_down_dw", tm=512)

    def mlp_dx(ddn, upv, xv, zv, dx1, wd, wu, a, b):
        dup = (_raw_dot(ddn, wd, "nt") * (2.0 * jnp.maximum(upv.astype(F32), 0.0))).astype(BF16)
        ns = wu.shape[2]
        dh2 = sum(_raw_dot(dup[:, j * ns:(j + 1) * ns], wu[j], "nt") for j in range(wu.shape[0]))
        _, vjp = jax.vjp(_post_mix, xv, zv, a, b)
        dx, dz, da, db = vjp((dx1, dh2))
        return (dup, dx, dz), (da, db)

    d_up, dx_res, d_z, d_gpm, d_gpl = _rowwise("mlp_dx", mlp_dx, [d_dn, up, x, z, dx1_y], [w_down, w_up, gpm, gpl],
                                               [(d_ff, BF16), (d, F32), (d, BF16)], [((1, d), F32), ((1, d), F32)])
    dw_up = _mm_shard_cols(h2, [d_up], [d_up.shape[1]], w_up.shape[2], "mlp_up_dw", 0, d)
    sent_mlp = on_grads(dict(w_down=dw_down, w_up=dw_up))

    def branches_out_bwd(dz, ga, gb, ya, yb, o, g, a, b, c_, wo, wa, n_, wb, gm, s):
        d_ga_, d_gb_, d_ya_, d_yb_ = jax.vjp(_merge, ga, gb, ya, yb)[1](_raw_dot(dz, wo, "nt"))
        ct_a, ct_b = _raw_dot(d_ya_, wa, "nt"), _raw_dot(d_yb_, wb, "nt")
        parts_a, parts_b = [], []
        for hs in _head_slices(HEAD_W):
            parts_a.append(jax.vjp(_gla_out, o[:, hs], g[:, hs], n_)[1](ct_a[:, hs]))
            parts_b.append(jax.vjp(_ml_out, a[:, hs], b[:, hs], c_[:, hs], gm[:, hs], s[:, hs])[1](ct_b[:, hs]))
        cat = lambda parts, i: jnp.concatenate([p[i] for p in parts], axis=1)
        return ((d_ga_, d_gb_, d_ya_, d_yb_, cat(parts_a, 0), cat(parts_a, 1), cat(parts_b, 0), cat(parts_b, 1), cat(parts_b, 2)),
                (sum(p[2] for p in parts_a), cat(parts_b, 3), cat(parts_b, 4)))

    d_ga, d_gb, d_ya, d_yb, d_o, d_g_a, d_hc, d_opre, d_xc, d_gn, d_gml, d_skip = _rowwise(
        "branches_out_bwd", branches_out_bwd, [d_z, gate_a, gate_b, y_a, y_b, o_gla, g_a, hc, o_pre, xc],
        [weight("w_o", merged), weight("w_pa", hc), gn, weight("w_pb", hc), g_ml, skip],
        [(d, BF16)] * 4 + [(ml_w, F32), (ml_w, BF16), (ml_w, F32), (ml_w, BF16), (ml_w, F32)],
        [((1, HEAD_W), F32), ((1, ml_w), F32), ((1, ml_w), F32)], deps=dep(sent_mlp))
    dw_o, dw_pa, dw_pb = _mm_tn_whole([(merged, d_z), (ya_in, d_ya), (h_b, d_yb)], "mix_dw")
    sent_mix = on_grads(dict(w_o=dw_o, w_pa=dw_pa, w_pb=dw_pb))

    dq_hm, dk_hm, d_va, dla_hm, d_qm, d_km, d_vm, d_li, d_lf = _recurrences_bwd(
        q_hm, k_hm, v_a, la_hm, s_prev, d_o, q_m, k_m, v_m, li, lf, c_prev, n_prev, m_prev, d_hc, deps=dep(sent_mix))
    d_gl = jnp.concatenate([_gate_cols(d_li), _gate_cols(d_lf), jnp.zeros((t, LANES - 2 * HEADS), F32)], axis=1)
    pre_grads = _ml_pre_bwd(x_m, pre_params, [d_xc, d_qm, d_km, d_vm, d_gl], tile=512)
    d_xm = pre_grads[0]
    d_cw = jnp.concatenate(pre_grads[1:5], axis=0)
    d_cb = pre_grads[5]
    d_wq, d_wk, d_wv = _blockdiag_blocks(pre_grads[6:9])
    d_wif = jnp.concatenate(pre_grads[9:12], axis=0)[:, 0:2 * HEADS]
    d_bif = pre_grads[12][:, 0:2 * HEADS]

    def decay_bwd(al, ct, w, b):
        _, vjp = jax.vjp(_log_decay, al, w, b)
        dal, dw, db = vjp(jnp.concatenate([ct[hd] for hd in range(HEADS)], axis=1))
        return (dal,), (dw, db)

    d_alow_p, d_wa_p, d_ba = _rowwise("gla_decay_bwd", decay_bwd, [a_low_p, dla_hm], [w_a_up_p, b_a_up],
                                      [(LANES, BF16)], [(w_a_up_p.shape, F32), (b_a_up.shape, F32)])
    d_proj = [jnp.concatenate([_from_hm(dq_hm), _from_hm(dk_hm), d_va, d_g_a], axis=1), d_alow_p,
              jnp.concatenate([d_xm, d_opre, d_ga, d_gb], axis=1)]
    d_widths = [offs[4], LOWRANK, offs[9] - offs[5]]
    d_pieces = _shard_pieces(d_widths, n_in)
    small = dict(w_a_up=d_wa_p[0:LOWRANK], b_a_up=d_ba, g_gla_norm=d_gn, conv_w=d_cw, conv_b=d_cb,
                 w_q_ml=d_wq, w_k_ml=d_wk, w_v_ml=d_wv, w_if=d_wif, b_if=d_bif, ml_skip=d_skip, g_ml_norm=d_gml,
                 g_post_mix=d_gpm, g_pre_mlp=d_gpl, g_post_mlp=d_gpo)
    sent_small = on_small(small, loss)
    sent_in = sent_small
    for half in range(2):
        dw_half = _mm_shard_cols(h, d_proj, d_widths, n_in, "proj_in_dw_%d" % half, half, d // 2, deps=dep(sent_in))
        sent_in = on_grads({"w_in#%d" % half: dw_half})

    def proj_in_dx(dp_a, dp_low, dp_b, xv, dres, w, g):
        dh = 0.0
        for s in range(N_DEV):
            for i, c_in, c_w, wd in d_pieces[s]:
                src = (dp_a, dp_low, dp_b)[i]
                cols = src.shape[1] - c_in if wd < LANES else wd
                dh = dh + _raw_dot(src[:, c_in:c_in + cols], w[s][:, c_w:c_w + cols], "nt")
        _, vjp = jax.vjp(_rms, xv, g)
        dx, dg = vjp(dh)
        return (dx + dres,), (dg,)

    grad_x, d_g1 = _rowwise("proj_in_dx", proj_in_dx, [*d_proj, x, dx_res], [w_in, g1], [(d, F32)], [((1, d), F32)],
                            deps=dep(sent_in))
    return grad_x, on_small(dict(g_pre_mix=d_g1), None)


BIG = ("w_in", "w_pa", "w_pb", "w_o", "w_up", "w_down")
MIX = ("w_o", "w_pa", "w_pb")
BIG_COL_SHARDED = ("w_in", "w_pa", "w_pb", "w_up")
SMALL_SHARDED = ("w_a_up", "conv_w", "w_if")
SMALL = ("g_pre_mix", "w_a_up", "b_a_up", "g_gla_norm", "conv_w", "conv_b", "w_q_ml", "w_k_ml", "w_v_ml", "w_if", "b_if",
         "ml_skip", "g_ml_norm", "g_post_mix", "g_pre_mlp", "g_post_mlp")
WEIGHTS = ("g_pre_mix", "w_in", "w_a_up", "b_a_up", "g_gla_norm", "conv_w", "conv_b", "w_q_ml", "w_k_ml", "w_v_ml", "w_if", "b_if",
           "ml_skip", "g_ml_norm", "w_pa", "w_pb", "w_o", "g_post_mix", "g_pre_mlp", "w_up", "w_down", "g_post_mlp")


def kernel(x, g_pre_mix, w_in, w_a_up, b_a_up, g_gla_norm, conv_w, conv_b, w_q_ml, w_k_ml, w_v_ml, w_if, b_if, ml_skip, g_ml_norm, w_pa, w_pb, w_o, g_post_mix, g_pre_mlp, w_up, w_down, g_post_mlp, loss_target, m_g_pre_mix, m_w_in, m_w_a_up, m_b_a_up, m_g_gla_norm, m_conv_w, m_conv_b, m_w_q_ml, m_w_k_ml, m_w_v_ml, m_w_if, m_b_if, m_ml_skip, m_g_ml_norm, m_w_pa, m_w_pb, m_w_o, m_g_post_mix, m_g_pre_mlp, m_w_up, m_w_down, m_g_post_mlp, v_g_pre_mix, v_w_in, v_w_a_up, v_b_a_up, v_g_gla_norm, v_conv_w, v_conv_b, v_w_q_ml, v_w_k_ml, v_w_v_ml, v_w_if, v_b_if, v_ml_skip, v_g_ml_norm, v_w_pa, v_w_pb, v_w_o, v_g_post_mix, v_g_pre_mlp, v_w_up, v_w_down, v_g_post_mlp):
    args = dict(locals())
    w = {n: args[n][0] for n in WEIGHTS}
    m = {n: args["m_" + n][0] for n in WEIGHTS}
    v = {n: args["v_" + n][0] for n in WEIGHTS}

    me_lin = _lin(_me())
    me_idx = jnp.reshape(me_lin, (1,)).astype(jnp.int32)

    def full_weight(n, g):
        if n in ("w_in", "w_up"):
            return g
        return _from_col_blocks(g) if n in BIG_COL_SHARDED else g.reshape(-1, g.shape[-1])

    def grad_parts(n, g):
        if n.partition("#")[0] in ("w_in", "w_up"):
            return g
        return (_col_blocks(g) if n in BIG_COL_SHARDED else g.reshape(N_DEV, -1, g.shape[-1])).astype(BF16)

    sharded_names = tuple(SMALL_SHARDED)
    narrow = {n: w[n].astype(BF16) for n in BIG}
    ready, pending, passing = {}, {}, {}
    first_state, _ = _copies_start(["gather"] * len(sharded_names) + ["gather_chips"],
                                   [_small_view(n, w[n]) for n in sharded_names] + [narrow["w_in"]], "allgather_start_first")

    def prefetch(group, after):
        state, token = _copies_start("gather_chips", [narrow[n] for n in group], "allgather_start_" + group[0], after)
        for n in group:
            pending[n] = (group, state)
        return token

    def pass_on(n, after):
        group, state = pending[n]
        shards, lands = _copies_wait(state, after, "allgather_wait_" + group[0])
        state, token = _copies_start("gather_pass", shards, "allgather_pass_" + group[0], lands=lands)
        for gn in group:
            passing[gn] = (group, state)
        return token

    def weight(n, after):
        if n not in ready:
            group, state = passing[n]
            _, lands = _copies_wait(state, after, "allgather_passed_" + group[0])
            for gn, land in zip(group, lands):
                ready[gn] = full_weight(gn, land)
        return ready[n]

    h = _first_norm(x[0], w["g_pre_mix"].reshape(1, -1))
    first_own, first_lands = _copies_wait(first_state, [h] + [narrow[n] for n in BIG if n != "w_in"], "allgather_wait_first")
    state, _ = _copies_start("gather_pass", first_own[-1:], "allgather_pass_w_in", lands=first_lands[-1:])
    passing["w_in"] = (("w_in",), state)
    ws = {n: (w[n].reshape(1, -1) if w[n].ndim == 1 else w[n]) for n in SMALL if n not in SMALL_SHARDED}
    for n, land in zip(sharded_names, first_lands):
        ws[n] = _small_unshard(n, land)

    sets, waiting_small = [], []

    def start_set(large, small):
        names = tuple(large)
        s_names, s_kinds, s_srcs = small if small else ((), [], [])
        state, token = _copies_start(s_kinds + ["exchange"] * len(names), s_srcs + [grad_parts(n, large[n]) for n in names],
                                     "exchange_start_" + (names + s_names)[0].replace("#", "_"))
        sets.append((names, s_names, s_kinds, state))
        return token

    def on_grads(grads):
        return start_set(grads, waiting_small.pop() if waiting_small else None)

    def on_small(small, loss):
        names = tuple(small)
        kinds = ["exchange" if n in SMALL_SHARDED else "gather" for n in names]
        srcs = [_small_shards(n, small[n]) if n in SMALL_SHARDED else _small_view(n, small[n]) for n in names]
        if loss is None:
            return start_set({}, (names, kinds, srcs))
        waiting_small.append((names, kinds + ["gather"], srcs + [loss]))
        return None

    grad_x, last_token = _local_step(x[0], h, loss_target[0], weight, ws, prefetch, pass_on, on_grads, on_small)

    out, chunks, sums, updated = {}, {}, [], []

    def finish_set(names, s_names, s_kinds, state, after):
        own, lands = _copies_wait(state, after, "exchange_wait_" + (names + s_names)[0].replace("#", "_"))
        ns = len(s_kinds)
        if s_names:
            k = len(s_names)
            upd = _small_update("adamw_small_" + s_names[0], me_idx, s_kinds[:k], lands[:k], own[:k],
                                *[[_small_view(n, d[n]) for n in s_names] for d in (w, m, v)],
                                sums=list(zip(lands[k:ns], own[k:ns])))
            for i, n in enumerate(s_names):
                out[n] = tuple(_small_unview(n, a, w[n].shape) for a in upd[4 * i:4 * i + 4])
            sums.extend(upd[4 * k:])
            updated.append(upd[1])
        if names == MIX:
            upd = _small_update("adamw_mix", me_idx, ["exchange"] * len(names), lands[ns:], own[ns:],
                                *[[d[n] for n in names] for d in (w, m, v)])
            for i, n in enumerate(names):
                out[n] = tuple(upd[4 * i:4 * i + 4])
            updated.append(upd[1])
            return
        for name, part, land in zip(names, own[ns:], lands[ns:]):
            n, _, chunk = name.partition("#")
            chunks.setdefault(n, []).append((land, part))
            if chunk in ("", "1"):
                got_lands, got_parts = zip(*chunks[n])
                if n == "w_in":
                    upd = _sum_adamw_cols(got_lands, got_parts, me_idx, *[_cols_view(d[n]) for d in (w, m, v)], "adamw_" + n)
                    out[n] = tuple(_from_cols_view(a, *w[n].shape) for a in upd)
                else:
                    out[n] = upd = _sum_adamw(got_lands, got_parts, me_idx, w[n], m[n], v[n], "adamw_" + n)
                updated.append(upd[1])

    for entry in sets[:-2] + [sets[-1], sets[-2]]:
        finish_set(*entry, [grad_x, last_token] + updated)
    loss_sum = sums[0]

    shaped = lambda a, n: a.reshape(args[n].shape)
    return (loss_sum[0, 0], grad_x[None],
            *[shaped(out[n][0], n) for n in WEIGHTS], *[shaped(out[n][1], n) for n in WEIGHTS],
            *[shaped(out[n][2], n) for n in WEIGHTS], *[shaped(out[n][3], n) for n in WEIGHTS])
```

```python
import functools

import jax
import jax.numpy as jnp
from jax import lax
from jax.experimental import pallas as pl
from jax.experimental.pallas import tpu as pltpu

F32 = jnp.float32
BF16 = jnp.bfloat16
MESH = pl.DeviceIdType.MESH

N_DEV = 8
EPS = 1e-6
CHUNK = 64
CHUNKS_PER_STEP = 8
HEADS = 4
GLA_DK = 64
HEAD_W = 128
GLA_GATE_NORM = 16.0
LOWRANK = 16
CONV_K = 4
QKV_BLOCK = 4
LANES = 128
HALO = 8
IN_SPLITS = (256, 256, 512, 512, 16, 512, 512, 1024, 1024)

ADAM_LR = 0.001
ADAM_B1 = 0.9
ADAM_B2 = 0.999
ADAM_EPS = 1e-08
ADAM_WD = 0.01
ADAM_STEP = 10

VMEM_LIMIT = 56 * 1024 * 1024
UPDATE_COLS_MAX = 64


def _cparams(*sem):
    return pltpu.CompilerParams(dimension_semantics=sem, vmem_limit_bytes=VMEM_LIMIT)


def _dims(mode, ndim):
    contract = {"nn": ((ndim - 1,), (ndim - 2,)), "nt": ((ndim - 1,), (ndim - 1,)), "tn": ((ndim - 2,), (ndim - 2,))}[mode]
    return contract, (((0,), (0,)) if ndim == 3 else ((), ()))


def _raw_dot(a, b, mode):
    return lax.dot_general(a.astype(BF16), b.astype(BF16), _dims(mode, a.ndim), preferred_element_type=F32)


@functools.partial(jax.custom_vjp, nondiff_argnums=(2,))
def _bdot(a, b, mode):
    return _raw_dot(a, b, mode)


def _bdot_fwd(a, b, mode):
    return _raw_dot(a, b, mode), (a, b)


def _bdot_bwd(mode, res, ct):
    a, b = res
    if mode == "nn":
        da, db = _raw_dot(ct, b, "nt"), _raw_dot(a, ct, "tn")
    elif mode == "nt":
        da, db = _raw_dot(ct, b, "nn"), _raw_dot(ct, a, "tn")
    else:
        da, db = _raw_dot(b, ct, "nt"), _raw_dot(a, ct, "nn")
    return da.astype(a.dtype), db.astype(b.dtype)


_bdot.defvjp(_bdot_fwd, _bdot_bwd)


def _split3(x):
    hi = x.astype(BF16)
    r1 = x - hi.astype(F32)
    mid = r1.astype(BF16)
    return hi, mid, (r1 - mid.astype(F32)).astype(BF16)


def _split_dot(tri, x):
    if x.ndim == 3:
        tri = jnp.broadcast_to(tri, (x.shape[0], *tri.shape))
    return sum(lax.dot_general(tri, t, _dims("nn", x.ndim), preferred_element_type=F32) for t in _split3(x))


def _tri(n, lower):
    r = lax.broadcasted_iota(jnp.int32, (n, n), 0)
    c = lax.broadcasted_iota(jnp.int32, (n, n), 1)
    return ((c <= r) if lower else (c >= r)).astype(BF16)


@jax.custom_vjp
def _cumsum_rows(x):
    return _split_dot(_tri(x.shape[-2], True), x)


def _cumsum_rows_fwd(x):
    return _cumsum_rows(x), None


def _cumsum_rows_bwd(_, ct):
    return (_split_dot(_tri(ct.shape[-2], False), ct),)


_cumsum_rows.defvjp(_cumsum_rows_fwd, _cumsum_rows_bwd)


def _abs(x):
    return jnp.where(x >= 0, x, -x)


def _sigmoid(x):
    return lax.logistic(x)


def _log_sigmoid(x):
    return jnp.minimum(x, 0.0) - jnp.log(1.0 + jnp.exp(-_abs(x)))


def _rms(x, g):
    return x * lax.rsqrt(jnp.mean(x * x, axis=-1, keepdims=True) + EPS) * g


def _head_slices(w):
    return [slice(h * w, (h + 1) * w) for h in range(HEADS)]


def _heads(ref, rows=slice(None)):
    return jnp.stack([ref[rows, hs] for hs in _head_slices(HEAD_W)])


def _put_heads(ref, val, rows=slice(None)):
    for h, hs in enumerate(_head_slices(HEAD_W)):
        ref[rows, hs] = val[h].astype(ref.dtype)


def _tile(dim, want):
    if dim <= want or dim % LANES:
        return dim
    t = want
    while dim % t:
        t -= LANES
    return t


def _mm(a, b, mode, out_dtype, name, tm=1024, tn=1024, tk=4096, epilogue=None, extra=(), deps=(), shards=None):
    if shards == "b":
        assert mode == "nn"
        ns = b.shape[2]
        (m, k), (k2, n) = a.shape, (b.shape[1], b.shape[0] * ns)
        tn = ns
    elif mode == "nn":
        (m, k), (k2, n) = a.shape, b.shape
    elif mode == "nt":
        (m, k), (n, k2) = a.shape, b.shape
    else:
        (k, m), (k2, n) = a.shape, b.shape
    assert k == k2, (name, a.shape, b.shape)
    tm, tn, tk = _tile(m, tm), _tile(n, tn), _tile(k, tk)
    nk = k // tk
    out_dtypes = out_dtype if epilogue else (out_dtype,)
    assert nk == 1 or (out_dtype == F32 and not epilogue), name
    n_in = 2 + len(extra)

    def body(*refs):
        p = _raw_dot(refs[0][...], refs[1][...], mode)
        if nk > 1:
            _accumulate(pl.program_id(2), [refs[n_in + len(deps)]], [p])
            return
        outs = epilogue(p, *[r[...] for r in refs[2:n_in]]) if epilogue else (p,)
        for ref, val in zip(refs[n_in + len(deps):], outs):
            ref[...] = val.astype(ref.dtype)

    a_spec = pl.BlockSpec((tk, tm), lambda i, j, kk: (kk, i)) if mode == "tn" else pl.BlockSpec((tm, tk), lambda i, j, kk: (i, kk))
    if shards == "b":
        b_spec = pl.BlockSpec((None, tk, tn), lambda i, j, kk: (j, kk, 0))
    elif mode == "nt":
        b_spec = pl.BlockSpec((tn, tk), lambda i, j, kk: (j, kk))
    else:
        b_spec = pl.BlockSpec((tk, tn), lambda i, j, kk: (kk, j))
    o_spec = pl.BlockSpec((tm, tn), lambda i, j, kk: (i, j))
    res = pl.pallas_call(
        body, name=name, grid=(m // tm, n // tn, nk),
        in_specs=[a_spec, b_spec] + [o_spec] * len(extra) + [ANY] * len(deps), out_specs=[o_spec] * len(out_dtypes),
        out_shape=[jax.ShapeDtypeStruct((m, n), dt) for dt in out_dtypes],
        compiler_params=_cparams("parallel", "parallel", "arbitrary"),
    )(a, b, *extra, *deps)
    return res if epilogue else res[0]


def _mm_tn_whole(pairs, name):
    def body(*refs):
        for i in range(len(pairs)):
            refs[2 * len(pairs) + i][...] = _raw_dot(refs[2 * i][...], refs[2 * i + 1][...], "tn").astype(BF16)

    return pl.pallas_call(
        body, name=name,
        out_shape=[jax.ShapeDtypeStruct((a.shape[1], b.shape[1]), BF16) for a, b in pairs],
        compiler_params=pltpu.CompilerParams(vmem_limit_bytes=VMEM_LIMIT),
    )(*[x for pair in pairs for x in pair])


def _shard_pieces(widths, n):
    bounds = [0]
    for wd in widths:
        bounds.append(bounds[-1] + wd)
    assert bounds[-1] == N_DEV * n
    return [[(i, max(s * n, b) - b, max(s * n, b) - s * n, min((s + 1) * n, b + wd) - max(s * n, b))
             for i, (b, wd) in enumerate(zip(bounds, widths)) if b < (s + 1) * n and b + wd > s * n]
            for s in range(N_DEV)]


def _mm_shard_cols(a, bs, widths, n, name, row_tile, tm, deps=()):
    t = a.shape[0]
    nb = len(bs)
    pieces = _shard_pieces(widths, n)

    first_use = [min(s for s in range(N_DEV) if any(p[0] == i for p in pieces[s])) for i in range(nb)]

    def body(a_ref, *rest):
        b_hbm = rest[:nb]
        o_ref, at_ref = rest[nb + len(deps):nb + len(deps) + 2]
        b_refs, sems = rest[nb + len(deps) + 2:-1], rest[-1]
        loads = [pltpu.make_async_copy(b_hbm[i], b_refs[i], sems.at[i]) for i in range(nb)]
        j = pl.program_id(0)

        @pl.when(j == 0)
        def _():
            for load in loads:
                load.start()
            at_ref[...] = a_ref[...].astype(BF16).T

        for s in range(N_DEV):
            @pl.when(j == s)
            def _(s=s):
                for i in range(nb):
                    if first_use[i] == s:
                        loads[i].wait()
                for i, c_in, c_out, wd in pieces[s]:
                    cols = min(_round_up(wd, LANES), bs[i].shape[1] - c_in) if wd < LANES else wd
                    p = _raw_dot(at_ref[...], b_refs[i][:, c_in:c_in + cols], "nn")
                    o_ref[:, c_out:c_out + wd] = p[:, 0:wd].astype(BF16)

    return pl.pallas_call(
        body, name=name, grid=(N_DEV,),
        in_specs=[pl.BlockSpec((t, tm), lambda j: (0, row_tile))] + [ANY] * (nb + len(deps)),
        out_specs=pl.BlockSpec((None, tm, n), lambda j: (j, 0, 0)),
        out_shape=jax.ShapeDtypeStruct((N_DEV, tm, n), BF16),
        scratch_shapes=[pltpu.VMEM((tm, t), BF16)] + [pltpu.VMEM(b.shape, b.dtype) for b in bs]
        + [pltpu.SemaphoreType.DMA((nb,))],
        compiler_params=_cparams("arbitrary"),
    )(a, *bs, *deps)


def _round_up(v, m):
    return -(-v // m) * m


def _rowwise(name, fn, rows, params, out_rows, out_accs=(), tile=256, deps=()):
    t = rows[0].shape[0]
    r = min(tile, t)
    assert t % r == 0
    n_in, n_or = len(rows) + len(params), len(out_rows)
    n_all = n_in + len(deps)
    params = list(params) + list(deps)

    def body(*refs):
        vals = [ref[...] for ref in refs[:n_in]]
        outs = refs[n_all:]
        ro, ao = fn(*vals)
        for ref, v in zip(outs[:n_or], ro):
            ref[...] = v.astype(ref.dtype)
        if out_accs:
            _accumulate(pl.program_id(0), outs[n_or:], ao)

    def full(shape, **kw):
        return pl.BlockSpec(shape, lambda i, nd=len(shape): (0,) * nd, **kw)

    def tiled(w):
        if isinstance(w, tuple):
            return pl.BlockSpec((w[0], r, w[1]), lambda i: (0, i, 0))
        return pl.BlockSpec((r, w), lambda i: (i, 0))

    def whole(w):
        return (w[0], t, w[1]) if isinstance(w, tuple) else (t, w)

    return pl.pallas_call(
        body, name=name, grid=(t // r,),
        in_specs=[tiled(a.shape[1] if a.ndim == 2 else (a.shape[0], a.shape[2])) for a in rows]
        + [full(p.shape, pipeline_mode=pl.Buffered(1)) for p in params],
        out_specs=[tiled(w) for w, _ in out_rows] + [full(s) for s, _ in out_accs],
        out_shape=[jax.ShapeDtypeStruct(whole(w), dt) for w, dt in out_rows] + [jax.ShapeDtypeStruct(s, dt) for s, dt in out_accs],
        compiler_params=_cparams("arbitrary"),
    )(*rows, *params)


def _accumulate(step, refs, vals):
    for ref, v in zip(refs, vals):
        @pl.when(step == 0)
        def _(ref=ref, v=v):
            ref[...] = v.astype(ref.dtype)

        @pl.when(step > 0)
        def _(ref=ref, v=v):
            ref[...] += v.astype(ref.dtype)


def _gla_chunk(q, k, v, la, st):
    c = q.shape[-2]
    row = lax.broadcasted_iota(jnp.int32, (c, c), 0)
    col = lax.broadcasted_iota(jnp.int32, (c, c), 1)
    cum = _cumsum_rows(la)
    cl = jnp.sum(la, axis=-2, keepdims=True)
    ep = jnp.exp(cum)
    en = jnp.exp(-cum)
    qs = q * (GLA_DK ** -0.5)
    qp = qs * ep
    a_f = _bdot(qp, k * en, "nt")
    a_b = _bdot(qs * en, k * ep, "nt")
    sc = jnp.where(row >= col, a_f, a_b)
    o = _bdot(sc, v, "nn") + _bdot(qp, st, "nt")
    kd = k * jnp.exp(cl - cum)
    st_new = st * jnp.exp(cl) + _bdot(v, kd, "tn")
    return o, st_new


def _gla_specs(nc, rev):
    nb = nc // CHUNKS_PER_STEP
    rows = CHUNKS_PER_STEP * CHUNK

    def blk(n):
        return (nb - 1 - n) if rev else n
    hm = pl.BlockSpec((HEADS, rows, GLA_DK), lambda n: (0, blk(n), 0))
    tm = pl.BlockSpec((rows, HEADS * HEAD_W), lambda n: (blk(n), 0))
    st = pl.BlockSpec((HEADS, CHUNKS_PER_STEP, HEAD_W, GLA_DK), lambda n: (0, blk(n), 0, 0))
    return nb, hm, tm, st


def _chunk_rows(c):
    return slice(c * CHUNK, (c + 1) * CHUNK)


def _ml_chunk(q, k, v, li_r, lf_r, cm, nv, m):
    c = q.shape[-2]
    row = lax.broadcasted_iota(jnp.int32, (c, c), 0)
    col = lax.broadcasted_iota(jnp.int32, (c, c), 1)
    eye = (row == col).astype(F32)
    li_c = jnp.sum(eye * li_r, axis=-1, keepdims=True)
    lf_c = jnp.sum(eye * lf_r, axis=-1, keepdims=True)
    fc_c = jnp.sum((col <= row).astype(F32) * lf_r, axis=-1, keepdims=True)
    fc_r = jnp.sum((row <= col).astype(F32) * lf_c, axis=-2, keepdims=True)
    f_last = jnp.sum(lf_r, axis=-1, keepdims=True)
    kc = k * (HEAD_W ** -0.5)
    a_c = f_last - fc_c + li_c
    m_loc = jnp.max(a_c, axis=-2, keepdims=True)
    kw = kc * jnp.exp(a_c - m_loc)
    c_chunk = _bdot(kw, v, "tn")
    n_chunk = jnp.sum(kw, axis=-2, keepdims=True)
    m_new = jnp.maximum(f_last + m, m_loc)
    sp = jnp.exp(f_last + m - m_new)
    sl = jnp.exp(m_loc - m_new)
    cm_new = sp * cm + sl * c_chunk
    nv_new = sp * nv + sl * n_chunk
    log_d = li_r - _abs(fc_c - fc_r)
    g_inter = fc_c + m
    m_t = jnp.maximum(g_inter, jnp.max(log_d, axis=-1, keepdims=True))
    s = _bdot(q, kc, "nt") * jnp.exp(log_d - m_t)
    sc = jnp.exp(g_inter - m_t)
    num = _bdot(s, v, "nn") + sc * _bdot(q, cm, "nn")
    den = jnp.sum(s, axis=-1, keepdims=True) + sc * jnp.sum(q * nv, axis=-1, keepdims=True)
    den = jnp.maximum(_abs(den), jnp.exp(-m_t))
    return num / den, cm_new, nv_new, m_new


def _ml_specs(nc, rev):
    nb = nc // CHUNKS_PER_STEP

    def blk(n):
        return (nb - 1 - n) if rev else n
    tm = pl.BlockSpec((CHUNKS_PER_STEP * CHUNK, HEADS * HEAD_W), lambda n: (blk(n), 0))
    gate = pl.BlockSpec((HEADS, CHUNKS_PER_STEP, 1, CHUNK), lambda n: (0, blk(n), 0, 0))
    cm = pl.BlockSpec((HEADS, CHUNKS_PER_STEP, HEAD_W, HEAD_W), lambda n: (0, blk(n), 0, 0))
    vec = pl.BlockSpec((HEADS, CHUNKS_PER_STEP, 1, HEAD_W), lambda n: (0, blk(n), 0, 0))
    return nb, tm, gate, cm, vec


_ML_STATE = [pltpu.VMEM((HEADS, HEAD_W, HEAD_W), F32), pltpu.VMEM((HEADS, 1, HEAD_W), F32), pltpu.VMEM((HEADS, 1, HEAD_W), F32)]


def _recurrences_fwd(q, k, v, la, qm, km, vm, li, lf, deps=()):
    t = v.shape[0]
    nc = t // CHUNK
    nb, hm, tm, st = _gla_specs(nc, False)
    _, _, gate, cm, vec = _ml_specs(nc, False)
    n_in = 9 + len(deps)

    def body(*refs):
        q_ref, k_ref, v_ref, la_ref, qm_ref, km_ref, vm_ref, li_ref, lf_ref = refs[:9]
        o_ref, sp_ref, hc_ref, cp_ref, np_ref, mp_ref, st_ref, c_ref, n_ref, m_ref = refs[n_in:]

        @pl.when(pl.program_id(0) == 0)
        def _():
            for ref in (st_ref, c_ref, n_ref, m_ref):
                ref[...] = jnp.zeros_like(ref)

        s, cs, ns, ms = st_ref[...], c_ref[...], n_ref[...], m_ref[...][:, :, 0:1]
        for c in range(CHUNKS_PER_STEP):
            r = _chunk_rows(c)
            sp_ref[:, c] = s
            o, s = _gla_chunk(q_ref[:, r], k_ref[:, r], _heads(v_ref, r), la_ref[:, r], s)
            _put_heads(o_ref, o, r)
            cp_ref[:, c] = cs
            np_ref[:, c] = ns
            mp_ref[:, c] = jnp.broadcast_to(ms, m_ref.shape)
            hc, cs, ns, ms = _ml_chunk(_heads(qm_ref, r), _heads(km_ref, r), _heads(vm_ref, r), li_ref[:, c], lf_ref[:, c],
                                       cs, ns, ms)
            _put_heads(hc_ref, hc, r)
        st_ref[...] = s
        c_ref[...] = cs
        n_ref[...] = ns
        m_ref[...] = jnp.broadcast_to(ms, m_ref.shape)

    tm_shape = jax.ShapeDtypeStruct((t, HEADS * HEAD_W), F32)
    vec_shape = jax.ShapeDtypeStruct((HEADS, nc, 1, HEAD_W), F32)
    return pl.pallas_call(
        body, name="recurrences_fwd", grid=(nb,),
        in_specs=[hm, hm, tm, hm, tm, tm, tm, gate, gate] + [ANY] * len(deps), out_specs=[tm, st, tm, cm, vec, vec],
        out_shape=[tm_shape, jax.ShapeDtypeStruct((HEADS, nc, HEAD_W, GLA_DK), F32),
                   tm_shape, jax.ShapeDtypeStruct((HEADS, nc, HEAD_W, HEAD_W), F32), vec_shape, vec_shape],
        scratch_shapes=[pltpu.VMEM((HEADS, HEAD_W, GLA_DK), F32)] + _ML_STATE,
        compiler_params=_cparams("arbitrary"),
    )(q, k, v, la, qm, km, vm, li, lf, *deps)


def _recurrences_bwd(q, k, v, la, sp, do, qm, km, vm, li, lf, cp, npv, mp, dhc, deps=()):
    t = v.shape[0]
    nc = t // CHUNK
    nb, hm, tm, st = _gla_specs(nc, True)
    _, _, gate, cm, vec = _ml_specs(nc, True)
    n_in = 15 + len(deps)

    def body(*refs):
        (q_ref, k_ref, v_ref, la_ref, sp_ref, do_ref,
         qm_ref, km_ref, vm_ref, li_ref, lf_ref, cp_ref, np_ref, mp_ref, dhc_ref) = refs[:15]
        (dq_ref, dk_ref, dv_ref, dla_ref, dqm_ref, dkm_ref, dvm_ref, dli_ref, dlf_ref,
         ds_ref, dc_ref, dn_ref, dm_ref) = refs[n_in:]

        @pl.when(pl.program_id(0) == 0)
        def _():
            for ref in (ds_ref, dc_ref, dn_ref, dm_ref):
                ref[...] = jnp.zeros_like(ref)

        ds, dc, dn, dm = ds_ref[...], dc_ref[...], dn_ref[...], dm_ref[...][:, :, 0:1]
        for c in reversed(range(CHUNKS_PER_STEP)):
            r = _chunk_rows(c)
            _, vjp = jax.vjp(_gla_chunk, q_ref[:, r], k_ref[:, r], _heads(v_ref, r), la_ref[:, r], sp_ref[:, c])
            dq, dk, dv, dla, ds = vjp((_heads(do_ref, r), ds))
            dq_ref[:, r] = dq.astype(dq_ref.dtype)
            dk_ref[:, r] = dk.astype(dk_ref.dtype)
            _put_heads(dv_ref, dv, r)
            dla_ref[:, r] = dla
            _, vjp = jax.vjp(_ml_chunk, _heads(qm_ref, r), _heads(km_ref, r), _heads(vm_ref, r), li_ref[:, c], lf_ref[:, c],
                             cp_ref[:, c], np_ref[:, c], mp_ref[:, c][:, :, 0:1])
            dqm, dkm, dvm, dli, dlf, dc, dn, dm = vjp((_heads(dhc_ref, r), dc, dn, dm))
            _put_heads(dqm_ref, dqm, r)
            _put_heads(dkm_ref, dkm, r)
            _put_heads(dvm_ref, dvm, r)
            dli_ref[:, c] = dli
            dlf_ref[:, c] = dlf
        ds_ref[...] = ds
        dc_ref[...] = dc
        dn_ref[...] = dn
        dm_ref[...] = jnp.broadcast_to(dm, dm_ref.shape)

    hm_shape = jax.ShapeDtypeStruct((HEADS, t, GLA_DK), BF16)
    tm_shape = jax.ShapeDtypeStruct((t, HEADS * HEAD_W), F32)
    gate_shape = jax.ShapeDtypeStruct((HEADS, nc, 1, CHUNK), F32)
    return pl.pallas_call(
        body, name="recurrences_bwd", grid=(nb,),
        in_specs=[hm, hm, tm, hm, st, tm, tm, tm, tm, gate, gate, cm, vec, vec, tm] + [ANY] * len(deps),
        out_specs=[hm, hm, tm, hm, tm, tm, tm, gate, gate],
        out_shape=[hm_shape, hm_shape, jax.ShapeDtypeStruct((t, HEADS * HEAD_W), BF16),
                   jax.ShapeDtypeStruct((HEADS, t, GLA_DK), F32), tm_shape, tm_shape, tm_shape, gate_shape, gate_shape],
        scratch_shapes=[pltpu.VMEM((HEADS, HEAD_W, GLA_DK), F32)] + _ML_STATE,
        compiler_params=_cparams("arbitrary"),
    )(q, k, v, la, sp, do, qm, km, vm, li, lf, cp, npv, mp, dhc, *deps)


@jax.custom_vjp
def _bdot_diag(x, w):
    b = w.shape[1]
    return jnp.concatenate([_raw_dot(x[:, :b], w[0], "nn"), _raw_dot(x[:, b:], w[1], "nn")], axis=1)


def _bdot_diag_fwd(x, w):
    return _bdot_diag(x, w), (x, w)


def _bdot_diag_bwd(res, ct):
    x, w = res
    b = w.shape[1]
    dx = jnp.concatenate([_raw_dot(ct[:, :b], w[0], "nt"), _raw_dot(ct[:, b:], w[1], "nt")], axis=1)
    dw = jnp.stack([_raw_dot(x[:, :b], ct[:, :b], "tn"), _raw_dot(x[:, b:], ct[:, b:], "tn")])
    return dx.astype(x.dtype), dw.astype(w.dtype)


_bdot_diag.defvjp(_bdot_diag_fwd, _bdot_diag_bwd)


def _ml_pre(s0, s1, s2, s3, cw0, cw1, cw2, cw3, cb, wq, wk, wv, wiq, wik, wiv, bif):
    pre = cb + cw0 * s0 + cw1 * s1 + cw2 * s2 + cw3 * s3
    xc = pre * _sigmoid(pre)
    q = _bdot_diag(xc, wq)
    k = _bdot_diag(xc, wk)
    v = _bdot_diag(s3, wv)
    gates = _bdot(q, wiq, "nn") + _bdot(k, wik, "nn") + _bdot(v, wiv, "nn") + bif
    lane = lax.broadcasted_iota(jnp.int32, gates.shape, 1)
    gl = jnp.where(lane < HEADS, gates, _log_sigmoid(gates))
    return xc, q, k, v, gl


def _delayed(xs_ref, x_ref, halo_ref, r, first):
    xs_ref[0:HALO, :] = jnp.where(first, 0.0, halo_ref[...])
    xs_ref[HALO:HALO + r, :] = x_ref[...]
    return [xs_ref[pl.ds(HALO - (CONV_K - 1) + j, r), :] for j in range(CONV_K)]


def _halo_spec(r, w, tile_of):
    return pl.BlockSpec((HALO, w), lambda i: (jnp.maximum(tile_of(i) * (r // HALO) - 1, 0), 0))


def _full_spec(shape):
    return pl.BlockSpec(shape, lambda i, nd=len(shape): (0,) * nd)


def _ml_pre_fwd(x_m, params, tile=256, deps=()):
    t, w = x_m.shape
    r = min(tile, t)

    def body(*refs):
        x_ref, halo_ref = refs[:2]
        p = [ref[...] for ref in refs[2:2 + len(params)]]
        outs = refs[2 + len(params) + len(deps):-1]
        res = _ml_pre(*_delayed(refs[-1], x_ref, halo_ref, r, pl.program_id(0) == 0), *p)
        for ref, val in zip(outs, res):
            ref[...] = val

    row = pl.BlockSpec((r, w), lambda i: (i, 0))
    return pl.pallas_call(
        body, name="ml_pre_fwd", grid=(t // r,),
        in_specs=[row, _halo_spec(r, w, lambda i: i)] + [_full_spec(p.shape) for p in params]
        + [ANY] * len(deps),
        out_specs=[row] * 4 + [pl.BlockSpec((r, LANES), lambda i: (i, 0))],
        out_shape=[jax.ShapeDtypeStruct((t, w), F32)] * 4 + [jax.ShapeDtypeStruct((t, LANES), F32)],
        scratch_shapes=[pltpu.VMEM((r + HALO, w), F32)],
        compiler_params=_cparams("arbitrary"),
    )(x_m, x_m, *params, *deps)


def _ml_pre_bwd(x_m, params, cts, tile=256):
    t, w = x_m.shape
    r = min(tile, t)
    nt = t // r
    n_p = len(params)

    def body(*refs):
        x_ref, halo_ref = refs[:2]
        p = [ref[...] for ref in refs[2:2 + n_p]]
        ct = [ref[...] for ref in refs[2 + n_p:7 + n_p]]
        dx_ref = refs[7 + n_p]
        dp_refs = refs[8 + n_p:8 + 2 * n_p]
        xs_ref, ds_ref, carry_ref = refs[8 + 2 * n_p:]
        step = pl.program_id(0)

        @pl.when(step == 0)
        def _():
            ds_ref[...] = jnp.zeros_like(ds_ref)
            carry_ref[...] = jnp.zeros_like(carry_ref)

        _, vjp = jax.vjp(_ml_pre, *_delayed(xs_ref, x_ref, halo_ref, r, step == nt - 1), *p)
        grads = vjp(tuple(ct))
        for j in range(CONV_K):
            ds_ref[j, HALO:HALO + r, :] = grads[j]
        lead = HALO + CONV_K - 1
        d_tile = sum(ds_ref[j, pl.ds(lead - j, r), :] for j in range(CONV_K))
        d_halo = sum(ds_ref[j, pl.ds(CONV_K - 1 - j, HALO), :] for j in range(CONV_K))
        dx_ref[...] = jnp.concatenate([d_tile[:r - HALO], d_tile[r - HALO:] + carry_ref[...]], axis=0).astype(dx_ref.dtype)
        carry_ref[...] = d_halo
        _accumulate(step, dp_refs, grads[CONV_K:])

    row = pl.BlockSpec((r, w), lambda i: (nt - 1 - i, 0))
    return pl.pallas_call(
        body, name="ml_pre_bwd", grid=(nt,),
        in_specs=[row, _halo_spec(r, w, lambda i: nt - 1 - i)] + [_full_spec(p.shape) for p in params]
        + [row] * 4 + [pl.BlockSpec((r, LANES), lambda i: (nt - 1 - i, 0))],
        out_specs=[row] + [_full_spec(p.shape) for p in params],
        out_shape=[jax.ShapeDtypeStruct((t, w), BF16)] + [jax.ShapeDtypeStruct(p.shape, F32) for p in params],
        scratch_shapes=[pltpu.VMEM((r + HALO, w), F32), pltpu.VMEM((CONV_K, r + 2 * HALO, w), F32), pltpu.VMEM((HALO, w), F32)],
        compiler_params=_cparams("arbitrary"),
    )(x_m, x_m, *params, *cts)


def _per_head(fn, row_vals, head_params, shared_params=()):
    return [fn(*[a[:, hs] for a in row_vals], *[p[:, hs] for p in head_params], *shared_params) for hs in _head_slices(HEAD_W)]


def _gla_out(o, g, gn):
    return _rms(o, gn) * (g * _sigmoid(g))


def _ml_out(hc, op, xc, g, sk):
    hcell = hc * _sigmoid(op)
    mu = jnp.mean(hcell, axis=-1, keepdims=True)
    d = hcell - mu
    var = jnp.mean(d * d, axis=-1, keepdims=True)
    return d * lax.rsqrt(var + EPS) * g + sk * xc


def _log_decay(al, w, b):
    return _log_sigmoid(_bdot(al, w, "nn") + b) * (1.0 / GLA_GATE_NORM)


def _merge(ga, gb, ya, yb):
    ga, gb, ya, yb = (a.astype(F32) for a in (ga, gb, ya, yb))
    return _sigmoid(ga) * ya + _sigmoid(gb) * yb


def _post_mix(x, z, gpm, gpl):
    x1 = x + _rms(z, gpm)
    return x1, _rms(x1, gpl)


def _loss_rows(x1, dn, tgt, g):
    e = x1 + _rms(dn, g) - tgt
    return 0.5 * jnp.sum(jnp.mean(e * e, axis=-1, keepdims=True), axis=0, keepdims=True)


def _lin(p):
    return 4 * p[0] + 2 * p[1] + p[2]


def _me():
    return lax.axis_index("x"), lax.axis_index("y"), lax.axis_index("c")


def _flip(p, k):
    return tuple((1 - v) if (k >> (2 - i)) & 1 else v for i, v in enumerate(p))


ANY = pl.BlockSpec(memory_space=pl.ANY)


HBM = pl.BlockSpec(memory_space=pltpu.HBM)
SEM = pl.BlockSpec(memory_space=pltpu.SEMAPHORE)
DATAFLOW = pltpu.SideEffectType.DATAFLOW_SIDE_EFFECTING


SIBLING = 1
OTHER_CHIPS = (2, 4, 6)


def _peer_copies(kinds, srcs, lands, send_sems, recv_sems):
    me = _me()
    copies = []
    for a, (kind, src, land) in enumerate(zip(kinds, srcs, lands)):
        masks = {"gather": range(1, N_DEV), "exchange": range(1, N_DEV), "gather_chips": (SIBLING, *OTHER_CHIPS),
                 "gather_pass": OTHER_CHIPS}[kind]
        for k in masks:
            peer = _flip(me, k)
            if kind == "gather_pass":
                block = land.at[_lin(peer)]
                src_ref, dst_ref, target = block, block, _flip(me, SIBLING)
            else:
                src_ref, dst_ref, target = (src.at[_lin(peer)] if kind == "exchange" else src), land.at[_lin(me)], peer
            copies.append(pltpu.make_async_remote_copy(
                src_ref=src_ref, dst_ref=dst_ref, send_sem=send_sems.at[a * 7 + k - 1], recv_sem=recv_sems.at[a * 7 + k - 1],
                device_id=target, device_id_type=MESH))
    return copies


def _own_copies(kinds, srcs, lands, own_sems):
    return [pltpu.make_async_copy(src, land.at[_lin(_me())], own_sems.at[a])
            for a, (kind, src, land) in enumerate(zip(kinds, srcs, lands)) if kind in ("gather", "gather_chips")]


def _copies_start(kind, srcs, name, after=None, lands=None):
    n = len(srcs)
    extra = [] if after is None else [after]
    kind = [kind] * n if isinstance(kind, str) else list(kind)
    land_shapes = [(s.shape if k == "exchange" else (N_DEV, *s.shape)) for k, s in zip(kind, srcs)]
    lands = [lax.empty(ls, s.dtype) for ls, s in zip(land_shapes, srcs)] if lands is None else lands

    def body(*refs):
        sems = refs[2 * n + len(extra):]
        for cp in _peer_copies(kind, refs[:n], refs[n:2 * n], sems[0], sems[1]) + _own_copies(kind, refs[:n], refs[n:2 * n], sems[2]):
            cp.start()
        refs[-1][...] = jnp.zeros_like(refs[-1])

    def hbm(a):
        return pltpu.with_memory_space_constraint(a, pltpu.HBM)

    out = pl.pallas_call(
        body, name=name,
        out_shape=(pltpu.SemaphoreType.DMA((7 * n,)), pltpu.SemaphoreType.DMA((7 * n,)), pltpu.SemaphoreType.DMA((n,)),
                   *[pltpu.HBM(s.shape, s.dtype) for s in srcs],
                   *[pltpu.HBM(ls, s.dtype) for ls, s in zip(land_shapes, srcs)],
                   jax.ShapeDtypeStruct((8, LANES), F32)),
        in_specs=[HBM] * (2 * n) + [ANY] * len(extra),
        out_specs=(SEM, SEM, SEM, *[HBM] * (2 * n), pl.BlockSpec(memory_space=pltpu.VMEM)),
        input_output_aliases={i: 3 + i for i in range(2 * n)},
        compiler_params=pltpu.CompilerParams(has_side_effects=DATAFLOW),
    )(*[hbm(s) for s in srcs], *[hbm(a) for a in lands], *extra)
    return (kind, n, out[:-1]), out[-1]


def _copies_wait(state, after, name):
    kind, n, (send_sems, recv_sems, own_sems, *thru) = state
    after = list(after) if isinstance(after, (list, tuple)) else [after]

    def body(*refs):
        for cp in _peer_copies(kind, refs[:n], refs[n:2 * n], refs[2 * n], refs[2 * n + 1]):
            cp.wait_send()
            cp.wait_recv()
        for cp in _own_copies(kind, refs[:n], refs[n:2 * n], refs[2 * n + 2]):
            cp.wait()

    out = pl.pallas_call(
        body, name=name,
        out_shape=tuple(pltpu.HBM(t.shape, t.dtype) for t in thru),
        in_specs=[HBM] * (2 * n) + [SEM, SEM, SEM] + [ANY] * len(after), out_specs=tuple([HBM] * (2 * n)),
        input_output_aliases={i: i for i in range(2 * n)},
        compiler_params=pltpu.CompilerParams(has_side_effects=DATAFLOW),
    )(*thru, send_sems, recv_sems, own_sems, *after)
    return out[:n], out[n:]


def _adamw(w, g, m, v):
    m2 = ADAM_B1 * m + (1.0 - ADAM_B1) * g
    v2 = ADAM_B2 * v + (1.0 - ADAM_B2) * (g * g)
    m_hat = m2 / (1.0 - ADAM_B1 ** ADAM_STEP)
    v_hat = v2 / (1.0 - ADAM_B2 ** ADAM_STEP)
    delta = -ADAM_LR * (m_hat / (jnp.sqrt(v_hat) + ADAM_EPS) + ADAM_WD * w)
    return delta, m2, v2


def _sum_adamw(lands, parts, me_idx, w, m, v, name, tile=256):
    r, c = w.shape
    nchunks = len(lands)
    tr = min(tile, r // nchunks)
    per_chunk = r // nchunks // tr
    per = 1 + N_DEV

    def body(me_ref, *refs):
        w_ref, m_ref, v_ref, g_ref, d_ref, m2_ref, v2_ref = refs[nchunks * per:]
        for k in range(nchunks):
            own_ref, slots = refs[k * per], refs[k * per + 1:(k + 1) * per]

            @pl.when(pl.program_id(0) // per_chunk == k)
            def _(own_ref=own_ref, slots=slots):
                own = own_ref[...].astype(F32)
                g = None
                for s in range(N_DEV):
                    term = jnp.where(me_ref[0] == s, own, slots[s][...].astype(F32))
                    g = term if g is None else g + term
                d, m2, v2 = _adamw(w_ref[...], g, m_ref[...], v_ref[...])
                g_ref[...] = g
                d_ref[...] = d
                m2_ref[...] = m2
                v2_ref[...] = v2

    def chunk_specs(k):
        def tile_of(i):
            return jnp.clip(i - k * per_chunk, 0, per_chunk - 1)

        def slot_spec(s):
            return pl.BlockSpec((None, tr, c), lambda i, me: (jnp.where(me[0] == s, (s + 1) % N_DEV, s), tile_of(i), 0))
        return [pl.BlockSpec((None, tr, c), lambda i, me: (me[0], tile_of(i), 0))] + [slot_spec(s) for s in range(N_DEV)]

    row = pl.BlockSpec((tr, c), lambda i, me: (i, 0))
    operands = [a for land, part in zip(lands, parts) for a in (part, *[land] * N_DEV)]
    return pl.pallas_call(
        body, name=name,
        grid_spec=pltpu.PrefetchScalarGridSpec(
            num_scalar_prefetch=1, grid=(r // tr,),
            in_specs=[s for k in range(nchunks) for s in chunk_specs(k)] + [row] * 3,
            out_specs=[row] * 4),
        out_shape=[jax.ShapeDtypeStruct((r, c), F32)] * 4,
        compiler_params=_cparams("parallel"),
    )(me_idx, *operands, w, m, v)


def _cols_view(a):
    r, c = a.shape
    return jnp.transpose(a.reshape(r // LANES, LANES, c), (2, 0, 1))


def _from_cols_view(a, r, c):
    return jnp.transpose(a, (1, 2, 0)).reshape(r, c)


def _sum_adamw_cols(lands, parts, me_idx, w, m, v, name):
    nchunks = len(lands)
    rows_k, c = lands[0].shape[1:]
    r = nchunks * rows_k
    steps = r // LANES
    per_chunk = rows_k // LANES
    per = 1 + N_DEV
    c_pad = _round_up(c, LANES)
    assert steps == 8
    block = steps * max(n for n in range(1, UPDATE_COLS_MAX + 1) if c % n == 0)

    def body(me_ref, *refs):
        w_ref, m_ref, v_ref, g_ref, d_ref, m2_ref, v2_ref, gt_ref = refs[nchunks * per:]
        for i in range(steps):
            k = i // per_chunk
            own_ref, slots = refs[k * per], refs[k * per + 1:(k + 1) * per]

            @pl.when(pl.program_id(0) == i)
            def _(i=i, own_ref=own_ref, slots=slots):
                own = own_ref[...].astype(F32)
                g = None
                for s in range(N_DEV):
                    term = jnp.where(me_ref[0] == s, own, slots[s][...].astype(F32))
                    g = term if g is None else g + term
                g = jnp.concatenate([g, jnp.zeros((LANES, c_pad - c), F32)], axis=1)
                for j in range(c_pad // LANES):
                    cols = min(LANES, c - j * LANES)
                    gt_ref[pl.ds(steps * LANES * j + i, cols, stride=steps), :] = jnp.transpose(g[:, j * LANES:(j + 1) * LANES])[0:cols]

        @pl.when(pl.program_id(0) == steps - 1)
        def _():
            def update(b, carry):
                cols = pl.ds(b * (block // steps), block // steps)
                g = gt_ref[pl.ds(pl.multiple_of(b * block, steps), block), :].reshape(block // steps, steps, LANES)
                d, m2, v2 = _adamw(w_ref[cols], g, m_ref[cols], v_ref[cols])
                g_ref[cols] = g
                d_ref[cols] = d
                m2_ref[cols] = m2
                v2_ref[cols] = v2
                return carry
            lax.fori_loop(0, c * steps // block, update, 0)

    def chunk_specs(k):
        def tile_of(i):
            return jnp.clip(i - k * per_chunk, 0, per_chunk - 1)

        def slot_spec(s):
            return pl.BlockSpec((None, LANES, c), lambda i, me: (jnp.where(me[0] == s, (s + 1) % N_DEV, s), tile_of(i), 0))
        return [pl.BlockSpec((None, LANES, c), lambda i, me: (me[0], tile_of(i), 0))] + [slot_spec(s) for s in range(N_DEV)]

    whole = pl.BlockSpec((c, steps, LANES), lambda i, me: (0, 0, 0))
    operands = [a for land, part in zip(lands, parts) for a in (part, *[land] * N_DEV)]
    return pl.pallas_call(
        body, name=name,
        grid_spec=pltpu.PrefetchScalarGridSpec(
            num_scalar_prefetch=1, grid=(steps,),
            in_specs=[s for k in range(nchunks) for s in chunk_specs(k)]
            + [pl.BlockSpec((c, steps, LANES), lambda i, me: (0, 0, 0), pipeline_mode=pl.Buffered(1))] * 3,
            out_specs=[whole] * 4,
            scratch_shapes=[pltpu.VMEM((c * steps, LANES), F32)]),
        out_shape=[jax.ShapeDtypeStruct((c, steps, LANES), F32)] * 4,
        compiler_params=_cparams("arbitrary"),
    )(me_idx, *operands, w, m, v)


def _small_update(name, me_idx, kinds, lands, owns, ws, ms, vs, sums=()):
    n = len(ws)
    lands, owns = list(lands) + [s[0] for s in sums], list(owns) + [s[1] for s in sums]
    kinds = list(kinds) + ["gather"] * len(sums)
    nl = len(lands)

    def summed(me, land_ref, own):
        g = None
        for s in range(N_DEV):
            term = jnp.where(me == s, own, land_ref[s]).astype(F32)
            g = term if g is None else g + term
        return g

    def body(me_ref, *refs):
        land_refs, own_refs = refs[:nl], refs[nl:2 * nl]
        w_refs, m_refs, v_refs = (refs[2 * nl + i * n:2 * nl + (i + 1) * n] for i in range(3))
        outs = refs[2 * nl + 3 * n:]
        me = me_ref[0]
        for i in range(n):
            g = summed(me, land_refs[i], own_refs[i][...])
            d, m2, v2 = _adamw(w_refs[i][...], g, m_refs[i][...], v_refs[i][...])
            for ref, val in zip(outs[4 * i:4 * i + 4], (g, d, m2, v2)):
                ref[...] = val
        for i in range(n, nl):
            outs[4 * n + i - n][...] = summed(me, land_refs[i], own_refs[i][...])

    def whole(shape):
        return pl.BlockSpec(shape, lambda i, me, nd=len(shape): (0,) * nd)

    def own_spec(kind, own):
        if kind == "gather":
            return whole(own.shape)
        return pl.BlockSpec((None, *own.shape[1:]), lambda i, me: (me[0], 0, 0))

    shapes = [w.shape for w in ws]
    out_shapes = [s for s in shapes for _ in range(4)] + [s[1].shape for s in sums]
    return pl.pallas_call(
        body, name=name,
        grid_spec=pltpu.PrefetchScalarGridSpec(
            num_scalar_prefetch=1, grid=(1,),
            in_specs=[whole(a.shape) for a in lands] + [own_spec(k, o) for k, o in zip(kinds, owns)]
            + [whole(s) for s in shapes] * 3,
            out_specs=[whole(s) for s in out_shapes]),
        out_shape=[jax.ShapeDtypeStruct(s, F32) for s in out_shapes],
        compiler_params=_cparams("arbitrary"),
    )(me_idx, *lands, *owns, *ws, *ms, *vs)


def _small_view(n, a):
    if a.ndim == 1:
        return a.reshape(1, -1)
    if a.ndim == 3:
        return a.transpose(1, 2, 0).reshape(QKV_BLOCK * QKV_BLOCK, -1)
    return a.T if n == "w_if" else a


def _small_unview(n, a, shape):
    if len(shape) == 1:
        return a.reshape(shape)
    if len(shape) == 3:
        return a.reshape(QKV_BLOCK, QKV_BLOCK, -1).transpose(2, 0, 1)
    return a.T if n == "w_if" else a


def _small_shards(n, g):
    if n == "w_if":
        return g.reshape(N_DEV, -1, g.shape[1]).transpose(0, 2, 1)
    return g.reshape(g.shape[0], N_DEV, -1).transpose(1, 0, 2)


def _small_unshard(n, s):
    if n == "w_if":
        return s.transpose(0, 2, 1).reshape(-1, s.shape[1])
    return s.transpose(1, 0, 2).reshape(s.shape[1], -1)


def _from_hm(a):
    h, t, d = a.shape
    return a.transpose(1, 0, 2).reshape(t, h * d)


def _gate_rows(g):
    t = g.shape[0]
    return g.T.reshape(HEADS, t // CHUNK, 1, CHUNK)


def _gate_cols(g):
    h, nc, _, c = g.shape
    return g.reshape(h, nc * c).T


def _blockdiag_dense(w):
    n = w.shape[0] * QKV_BLOCK // 2
    tiled = jnp.tile(w.reshape(2, n, QKV_BLOCK), (1, 1, n // QKV_BLOCK))
    r = lax.broadcasted_iota(jnp.int32, (2, n, n), 1)
    c = lax.broadcasted_iota(jnp.int32, (2, n, n), 2)
    return jnp.where(r // QKV_BLOCK == c // QKV_BLOCK, tiled, 0.0)


def _blockdiag_blocks(dense):
    _, n, _ = dense[0].shape
    k = len(dense)

    def body(*refs):
        r = lax.broadcasted_iota(jnp.int32, (n, n), 0)
        c = lax.broadcasted_iota(jnp.int32, (n, n), 1)
        fr = lax.broadcasted_iota(jnp.int32, (n, LANES), 0)
        fc = lax.broadcasted_iota(jnp.int32, (n, LANES), 1)
        fold = ((fr & (QKV_BLOCK - 1)) == fc).astype(BF16)
        for i in range(k):
            for half in range(2):
                kept = jnp.where((r >> 2) == (c >> 2), refs[i][half], 0.0)
                refs[k + i][half] = sum(lax.dot_general(t, fold, _dims("nn", 2), preferred_element_type=F32)
                                        for t in _split3(kept))

    out = pl.pallas_call(body, name="blockdiag_blocks", out_shape=[jax.ShapeDtypeStruct((2, n, LANES), F32)] * k)(*dense)
    return [o[:, :, 0:QKV_BLOCK].reshape(2 * n // QKV_BLOCK, QKV_BLOCK, QKV_BLOCK) for o in out]


def _col_blocks(w):
    k, n = w.shape
    return w.reshape(k, N_DEV, n // N_DEV).transpose(1, 0, 2)


def _from_col_blocks(g):
    d, k, n = g.shape
    return g.transpose(1, 0, 2).reshape(k, d * n)


def _first_norm(x, g):
    return _rowwise("pre_mix_norm", lambda xv, gv: ((_rms(xv, gv),), ()), [x], [g], [(x.shape[1], BF16)])[0]


def _local_step(x, h, tgt, weight, ws, prefetch, pass_on, on_grads, on_small):
    t, d = x.shape
    g1 = ws["g_pre_mix"]

    def dep(token):
        return () if token is None else (token,)

    w_in = weight("w_in", x)
    fetch_mix = prefetch(("w_pa", "w_pb", "w_o"), w_in)
    fetch_up = prefetch(("w_up", "w_down"), fetch_mix)

    n_in = w_in.shape[2]

    offs = [0]
    for s in IN_SPLITS:
        offs.append(offs[-1] + s)

    w_a_up_p = jnp.pad(ws["w_a_up"], ((0, LANES - LOWRANK), (0, 0)))
    b_a_up = ws["b_a_up"]

    def proj_in_fwd(hv, w, wa, ba):
        proj = jnp.concatenate([_raw_dot(hv, w[j], "nn") for j in range(N_DEV)], axis=1)
        parts = [proj[:, offs[i]:offs[i + 1]] for i in range(len(IN_SPLITS))]
        parts[4] = jnp.concatenate([parts[4], jnp.zeros((parts[4].shape[0], LANES - LOWRANK), F32)], axis=1)
        head_major = lambda a: jnp.stack([a[:, hs] for hs in _head_slices(GLA_DK)])
        return (head_major(parts[0]), head_major(parts[1]), *parts[2:], head_major(_log_decay(parts[4], wa, ba))), ()

    widths = [LANES if s == LOWRANK else s for s in IN_SPLITS]
    gla_hm = ((HEADS, GLA_DK), F32)
    q_hm, k_hm, v_a, g_a, a_low_p, x_m, o_pre, gate_a, gate_b, la_hm = _rowwise(
        "proj_in", proj_in_fwd, [h], [w_in, w_a_up_p, b_a_up],
        [gla_hm, gla_hm] + [(wd, BF16 if i == 2 else F32) for i, wd in enumerate(widths)][2:] + [gla_hm],
        deps=dep(fetch_up))
    gn = ws["g_gla_norm"]
    ml_w = HEADS * HEAD_W

    cw = ws["conv_w"]
    w_if_p = jnp.pad(ws["w_if"], ((0, 0), (0, LANES - 2 * HEADS)))
    pre_params = [cw[0:1], cw[1:2], cw[2:3], cw[3:4], ws["conv_b"],
                  _blockdiag_dense(ws["w_q_ml"]), _blockdiag_dense(ws["w_k_ml"]), _blockdiag_dense(ws["w_v_ml"]),
                  w_if_p[0:ml_w], w_if_p[ml_w:2 * ml_w], w_if_p[2 * ml_w:3 * ml_w],
                  jnp.pad(ws["b_if"], ((0, 0), (0, LANES - 2 * HEADS)))]
    xc, q_m, k_m, v_m, gl = _ml_pre_fwd(x_m, pre_params, tile=512)
    pass_mix = pass_on("w_pa", xc)
    li, lf = _gate_rows(gl[:, 0:HEADS]), _gate_rows(gl[:, HEADS:2 * HEADS])
    o_gla, s_prev, hc, c_prev, n_prev, m_prev = _recurrences_fwd(q_hm, k_hm, v_a, la_hm, q_m, k_m, v_m, li, lf,
                                                                 deps=dep(pass_mix))
    g_ml, skip = ws["g_ml_norm"], ws["ml_skip"]

    def branches_out(o, g, a, b, c_, ga, gb, n_, wa, gm, s, wb):
        ya_in = jnp.concatenate(_per_head(_gla_out, [o, g], [], [n_]), axis=1)
        ya = _raw_dot(ya_in, wa, "nn")
        hb = jnp.concatenate(_per_head(_ml_out, [a, b, c_], [gm, s]), axis=1)
        yb = _raw_dot(hb, wb, "nn")
        return (ya_in, ya, hb, yb, _merge(ga, gb, ya, yb)), ()

    ya_in, y_a, h_b, y_b, merged = _rowwise(
        "branches_out", branches_out, [o_gla, g_a, hc, o_pre, xc, gate_a, gate_b],
        [gn, weight("w_pa", hc), g_ml, skip, weight("w_pb", hc)],
        [(ml_w, BF16), (d, BF16), (ml_w, BF16), (d, BF16), (d, BF16)], tile=512)

    gpm, gpl, gpo = ws["g_post_mix"], ws["g_pre_mlp"], ws["g_post_mlp"]

    def proj_o_fwd(mg, xv, w, a, b):
        zv = _raw_dot(mg, w, "nn")
        return (zv, *_post_mix(xv, zv, a, b)), ()

    pass_up = pass_on("w_up", merged)
    z, x1, h2 = _rowwise("proj_o", proj_o_fwd, [merged, x], [weight("w_o", merged), gpm, gpl],
                         [(d, F32), (d, F32), (d, BF16)], tile=512, deps=dep(pass_up))
    w_up = weight("w_up", h2)
    w_down = weight("w_down", h2)
    d_ff = w_down.shape[0]

    def mlp_loss(h2v, x1v, tgtv, wu, wd, g):
        upv = jnp.concatenate([_raw_dot(h2v, wu[j], "nn") for j in range(wu.shape[0])], axis=1)
        uv = jnp.square(jnp.maximum(upv, 0.0))
        dnv = _raw_dot(uv, wd, "nn")
        loss, vjp = jax.vjp(lambda a, b, c_: _loss_rows(a, b, tgtv, c_), x1v, dnv, g)
        dx1, ddn, dg = vjp(jnp.ones((1, 1), F32))
        return (upv, uv, dx1, ddn), (jnp.broadcast_to(loss, (1, LANES)), dg)

    up, u, dx1_y, d_dn, loss, d_gpo = _rowwise("mlp_loss", mlp_loss, [h2, x1, tgt], [w_up, w_down, gpo],
                                               [(d_ff, BF16), (d_ff, BF16), (d, F32), (d, BF16)],
                                               [((1, LANES), F32), ((1, d), F32)])

    dw_down = _mm(u, d_dn, "tn", BF16, "mlp_down_dw", tm=512)

    def mlp_dx(ddn, upv, xv, zv, dx1, wd, wu, a, b):
        dup = (_raw_dot(ddn, wd, "nt") * (2.0 * jnp.maximum(upv.astype(F32), 0.0))).astype(BF16)
        ns = wu.shape[2]
        dh2 = sum(_raw_dot(dup[:, j * ns:(j + 1) * ns], wu[j], "nt") for j in range(wu.shape[0]))
        _, vjp = jax.vjp(_post_mix, xv, zv, a, b)
        dx, dz, da, db = vjp((dx1, dh2))
        return (dup, dx, dz), (da, db)

    d_up, dx_res, d_z, d_gpm, d_gpl = _rowwise("mlp_dx", mlp_dx, [d_dn, up, x, z, dx1_y], [w_down, w_up, gpm, gpl],
                                               [(d_ff, BF16), (d, F32), (d, BF16)], [((1, d), F32), ((1, d), F32)])
    dw_up = _mm_shard_cols(h2, [d_up], [d_up.shape[1]], w_up.shape[2], "mlp_up_dw", 0, d)
    sent_mlp = on_grads(dict(w_down=dw_down, w_up=dw_up))

    def branches_out_bwd(dz, ga, gb, ya, yb, o, g, a, b, c_, wo, wa, n_, wb, gm, s):
        d_ga_, d_gb_, d_ya_, d_yb_ = jax.vjp(_merge, ga, gb, ya, yb)[1](_raw_dot(dz, wo, "nt"))
        ct_a, ct_b = _raw_dot(d_ya_, wa, "nt"), _raw_dot(d_yb_, wb, "nt")
        parts_a, parts_b = [], []
        for hs in _head_slices(HEAD_W):
            parts_a.append(jax.vjp(_gla_out, o[:, hs], g[:, hs], n_)[1](ct_a[:, hs]))
            parts_b.append(jax.vjp(_ml_out, a[:, hs], b[:, hs], c_[:, hs], gm[:, hs], s[:, hs])[1](ct_b[:, hs]))
        cat = lambda parts, i: jnp.concatenate([p[i] for p in parts], axis=1)
        return ((d_ga_, d_gb_, d_ya_, d_yb_, cat(parts_a, 0), cat(parts_a, 1), cat(parts_b, 0), cat(parts_b, 1), cat(parts_b, 2)),
                (sum(p[2] for p in parts_a), cat(parts_b, 3), cat(parts_b, 4)))

    d_ga, d_gb, d_ya, d_yb, d_o, d_g_a, d_hc, d_opre, d_xc, d_gn, d_gml, d_skip = _rowwise(
        "branches_out_bwd", branches_out_bwd, [d_z, gate_a, gate_b, y_a, y_b, o_gla, g_a, hc, o_pre, xc],
        [weight("w_o", merged), weight("w_pa", hc), gn, weight("w_pb", hc), g_ml, skip],
        [(d, BF16)] * 4 + [(ml_w, F32), (ml_w, BF16), (ml_w, F32), (ml_w, BF16), (ml_w, F32)],
        [((1, HEAD_W), F32), ((1, ml_w), F32), ((1, ml_w), F32)], deps=dep(sent_mlp))
    dw_o, dw_pa, dw_pb = _mm_tn_whole([(merged, d_z), (ya_in, d_ya), (h_b, d_yb)], "mix_dw")
    sent_mix = on_grads(dict(w_o=dw_o, w_pa=dw_pa, w_pb=dw_pb))

    dq_hm, dk_hm, d_va, dla_hm, d_qm, d_km, d_vm, d_li, d_lf = _recurrences_bwd(
        q_hm, k_hm, v_a, la_hm, s_prev, d_o, q_m, k_m, v_m, li, lf, c_prev, n_prev, m_prev, d_hc, deps=dep(sent_mix))
    d_gl = jnp.concatenate([_gate_cols(d_li), _gate_cols(d_lf), jnp.zeros((t, LANES - 2 * HEADS), F32)], axis=1)
    pre_grads = _ml_pre_bwd(x_m, pre_params, [d_xc, d_qm, d_km, d_vm, d_gl], tile=512)
    d_xm = pre_grads[0]
    d_cw = jnp.concatenate(pre_grads[1:5], axis=0)
    d_cb = pre_grads[5]
    d_wq, d_wk, d_wv = _blockdiag_blocks(pre_grads[6:9])
    d_wif = jnp.concatenate(pre_grads[9:12], axis=0)[:, 0:2 * HEADS]
    d_bif = pre_grads[12][:, 0:2 * HEADS]

    def decay_bwd(al, ct, w, b):
        _, vjp = jax.vjp(_log_decay, al, w, b)
        dal, dw, db = vjp(jnp.concatenate([ct[hd] for hd in range(HEADS)], axis=1))
        return (dal,), (dw, db)

    d_alow_p, d_wa_p, d_ba = _rowwise("gla_decay_bwd", decay_bwd, [a_low_p, dla_hm], [w_a_up_p, b_a_up],
                                      [(LANES, BF16)], [(w_a_up_p.shape, F32), (b_a_up.shape, F32)])
    d_proj = [jnp.concatenate([_from_hm(dq_hm), _from_hm(dk_hm), d_va, d_g_a], axis=1), d_alow_p,
              jnp.concatenate([d_xm, d_opre, d_ga, d_gb], axis=1)]
    d_widths = [offs[4], LOWRANK, offs[9] - offs[5]]
    d_pieces = _shard_pieces(d_widths, n_in)
    small = dict(w_a_up=d_wa_p[0:LOWRANK], b_a_up=d_ba, g_gla_norm=d_gn, conv_w=d_cw, conv_b=d_cb,
                 w_q_ml=d_wq, w_k_ml=d_wk, w_v_ml=d_wv, w_if=d_wif, b_if=d_bif, ml_skip=d_skip, g_ml_norm=d_gml,
                 g_post_mix=d_gpm, g_pre_mlp=d_gpl, g_post_mlp=d_gpo)
    sent_small = on_small(small, loss)
    sent_in = sent_small
    for half in range(2):
        dw_half = _mm_shard_cols(h, d_proj, d_widths, n_in, "proj_in_dw_%d" % half, half, d // 2, deps=dep(sent_in))
        sent_in = on_grads({"w_in#%d" % half: dw_half})

    def proj_in_dx(dp_a, dp_low, dp_b, xv, dres, w, g):
        dh = 0.0
        for s in range(N_DEV):
            for i, c_in, c_w, wd in d_pieces[s]:
                src = (dp_a, dp_low, dp_b)[i]
                cols = src.shape[1] - c_in if wd < LANES else wd
                dh = dh + _raw_dot(src[:, c_in:c_in + cols], w[s][:, c_w:c_w + cols], "nt")
        _, vjp = jax.vjp(_rms, xv, g)
        dx, dg = vjp(dh)
        return (dx + dres,), (dg,)

    grad_x, d_g1 = _rowwise("proj_in_dx", proj_in_dx, [*d_proj, x, dx_res], [w_in, g1], [(d, F32)], [((1, d), F32)],
                            deps=dep(sent_in))
    return grad_x, on_small(dict(g_pre_mix=d_g1), None)


BIG = ("w_in", "w_pa", "w_pb", "w_o", "w_up", "w_down")
MIX = ("w_o", "w_pa", "w_pb")
BIG_COL_SHARDED = ("w_in", "w_pa", "w_pb", "w_up")
SMALL_SHARDED = ("w_a_up", "conv_w", "w_if")
SMALL = ("g_pre_mix", "w_a_up", "b_a_up", "g_gla_norm", "conv_w", "conv_b", "w_q_ml", "w_k_ml", "w_v_ml", "w_if", "b_if",
         "ml_skip", "g_ml_norm", "g_post_mix", "g_pre_mlp", "g_post_mlp")
WEIGHTS = ("g_pre_mix", "w_in", "w_a_up", "b_a_up", "g_gla_norm", "conv_w", "conv_b", "w_q_ml", "w_k_ml", "w_v_ml", "w_if", "b_if",
           "ml_skip", "g_ml_norm", "w_pa", "w_pb", "w_o", "g_post_mix", "g_pre_mlp", "w_up", "w_down", "g_post_mlp")


def kernel(x, g_pre_mix, w_in, w_a_up, b_a_up, g_gla_norm, conv_w, conv_b, w_q_ml, w_k_ml, w_v_ml, w_if, b_if, ml_skip, g_ml_norm, w_pa, w_pb, w_o, g_post_mix, g_pre_mlp, w_up, w_down, g_post_mlp, loss_target, m_g_pre_mix, m_w_in, m_w_a_up, m_b_a_up, m_g_gla_norm, m_conv_w, m_conv_b, m_w_q_ml, m_w_k_ml, m_w_v_ml, m_w_if, m_b_if, m_ml_skip, m_g_ml_norm, m_w_pa, m_w_pb, m_w_o, m_g_post_mix, m_g_pre_mlp, m_w_up, m_w_down, m_g_post_mlp, v_g_pre_mix, v_w_in, v_w_a_up, v_b_a_up, v_g_gla_norm, v_conv_w, v_conv_b, v_w_q_ml, v_w_k_ml, v_w_v_ml, v_w_if, v_b_if, v_ml_skip, v_g_ml_norm, v_w_pa, v_w_pb, v_w_o, v_g_post_mix, v_g_pre_mlp, v_w_up, v_w_down, v_g_post_mlp):
    args = dict(locals())
    w = {n: args[n][0] for n in WEIGHTS}
    m = {n: args["m_" + n][0] for n in WEIGHTS}
    v = {n: args["v_" + n][0] for n in WEIGHTS}

    me_lin = _lin(_me())
    me_idx = jnp.reshape(me_lin, (1,)).astype(jnp.int32)

    def full_weight(n, g):
        if n in ("w_in", "w_up"):
            return g
        return _from_col_blocks(g) if n in BIG_COL_SHARDED else g.reshape(-1, g.shape[-1])

    def grad_parts(n, g):
        if n.partition("#")[0] in ("w_in", "w_up"):
            return g
        return (_col_blocks(g) if n in BIG_COL_SHARDED else g.reshape(N_DEV, -1, g.shape[-1])).astype(BF16)

    sharded_names = tuple(SMALL_SHARDED)
    narrow = {n: w[n].astype(BF16) for n in BIG}
    ready, pending, passing = {}, {}, {}
    first_state, _ = _copies_start(["gather"] * len(sharded_names) + ["gather_chips"],
                                   [_small_view(n, w[n]) for n in sharded_names] + [narrow["w_in"]], "allgather_start_first")

    def prefetch(group, after):
        state, token = _copies_start("gather_chips", [narrow[n] for n in group], "allgather_start_" + group[0], after)
        for n in group:
            pending[n] = (group, state)
        return token

    def pass_on(n, after):
        group, state = pending[n]
        shards, lands = _copies_wait(state, after, "allgather_wait_" + group[0])
        state, token = _copies_start("gather_pass", shards, "allgather_pass_" + group[0], lands=lands)
        for gn in group:
            passing[gn] = (group, state)
        return token

    def weight(n, after):
        if n not in ready:
            group, state = passing[n]
            _, lands = _copies_wait(state, after, "allgather_passed_" + group[0])
            for gn, land in zip(group, lands):
                ready[gn] = full_weight(gn, land)
        return ready[n]

    h = _first_norm(x[0], w["g_pre_mix"].reshape(1, -1))
    first_own, first_lands = _copies_wait(first_state, [h] + [narrow[n] for n in BIG if n != "w_in"], "allgather_wait_first")
    state, _ = _copies_start("gather_pass", first_own[-1:], "allgather_pass_w_in", lands=first_lands[-1:])
    passing["w_in"] = (("w_in",), state)
    ws = {n: (w[n].reshape(1, -1) if w[n].ndim == 1 else w[n]) for n in SMALL if n not in SMALL_SHARDED}
    for n, land in zip(sharded_names, first_lands):
        ws[n] = _small_unshard(n, land)

    sets, waiting_small = [], []

    def start_set(large, small):
        names = tuple(large)
        s_names, s_kinds, s_srcs = small if small else ((), [], [])
        state, token = _copies_start(s_kinds + ["exchange"] * len(names), s_srcs + [grad_parts(n, large[n]) for n in names],
                                     "exchange_start_" + (names + s_names)[0].replace("#", "_"))
        sets.append((names, s_names, s_kinds, state))
        return token

    def on_grads(grads):
        return start_set(grads, waiting_small.pop() if waiting_small else None)

    def on_small(small, loss):
        names = tuple(small)
        kinds = ["exchange" if n in SMALL_SHARDED else "gather" for n in names]
        srcs = [_small_shards(n, small[n]) if n in SMALL_SHARDED else _small_view(n, small[n]) for n in names]
        if loss is None:
            return start_set({}, (names, kinds, srcs))
        waiting_small.append((names, kinds + ["gather"], srcs + [loss]))
        return None

    grad_x, last_token = _local_step(x[0], h, loss_target[0], weight, ws, prefetch, pass_on, on_grads, on_small)

    out, chunks, sums, updated = {}, {}, [], []

    def finish_set(names, s_names, s_kinds, state, after):
        own, lands = _copies_wait(state, after, "exchange_wait_" + (names + s_names)[0].replace("#", "_"))
        ns = len(s_kinds)
        if s_names:
            k = len(s_names)
            upd = _small_update("adamw_small_" + s_names[0], me_idx, s_kinds[:k], lands[:k], own[:k],
                                *[[_small_view(n, d[n]) for n in s_names] for d in (w, m, v)],
                                sums=list(zip(lands[k:ns], own[k:ns])))
            for i, n in enumerate(s_names):
                out[n] = tuple(_small_unview(n, a, w[n].shape) for a in upd[4 * i:4 * i + 4])
            sums.extend(upd[4 * k:])
            updated.append(upd[1])
        if names == MIX:
            upd = _small_update("adamw_mix", me_idx, ["exchange"] * len(names), lands[ns:], own[ns:],
                                *[[d[n] for n in names] for d in (w, m, v)])
            for i, n in enumerate(names):
                out[n] = tuple(upd[4 * i:4 * i + 4])
            updated.append(upd[1])
            return
        for name, part, land in zip(names, own[ns:], lands[ns:]):
            n, _, chunk = name.partition("#")
            chunks.setdefault(n, []).append((land, part))
            if chunk in ("", "1"):
                got_lands, got_parts = zip(*chunks[n])
                if n == "w_in":
                    upd = _sum_adamw_cols(got_lands, got_parts, me_idx, *[_cols_view(d[n]) for d in (w, m, v)], "adamw_" + n)
                    out[n] = tuple(_from_cols_view(a, *w[n].shape) for a in upd)
                else:
                    out[n] = upd = _sum_adamw(got_lands, got_parts, me_idx, w[n], m[n], v[n], "adamw_" + n)
                updated.append(upd[1])

    for entry in sets[:-2] + [sets[-1], sets[-2]]:
        finish_set(*entry, [grad_x, last_token] + updated)
    loss_sum = sums[0]

    shaped = lambda a, n: a.reshape(args[n].shape)
    return (loss_sum[0, 0], grad_x[None],
            *[shaped(out[n][0], n) for n in WEIGHTS], *[shaped(out[n][1], n) for n in WEIGHTS],
            *[shaped(out[n][2], n) for n in WEIGHTS], *[shaped(out[n][3], n) for n in WEIGHTS])
```

```python
import functools

import jax
import jax.numpy as jnp
from jax import lax
from jax.experimental import pallas as pl
from jax.experimental.pallas import tpu as pltpu

F32 = jnp.float32
BF16 = jnp.bfloat16
MESH = pl.DeviceIdType.MESH

N_DEV = 8
EPS = 1e-6
CHUNK = 64
CHUNKS_PER_STEP = 8
HEADS = 4
GLA_DK = 64
HEAD_W = 128
GLA_GATE_NORM = 16.0
LOWRANK = 16
CONV_K = 4
QKV_BLOCK = 4
LANES = 128
HALO = 8
IN_SPLITS = (256, 256, 512, 512, 16, 512, 512, 1024, 1024)

ADAM_LR = 0.001
ADAM_B1 = 0.9
ADAM_B2 = 0.999
ADAM_EPS = 1e-08
ADAM_WD = 0.01
ADAM_STEP = 10

VMEM_LIMIT = 56 * 1024 * 1024
W_IN_PARTS = 4
UPDATE_COLS_MAX = 64


def _cparams(*sem):
    return pltpu.CompilerParams(dimension_semantics=sem, vmem_limit_bytes=VMEM_LIMIT)


def _dims(mode, ndim):
    contract = {"nn": ((ndim - 1,), (ndim - 2,)), "nt": ((ndim - 1,), (ndim - 1,)), "tn": ((ndim - 2,), (ndim - 2,))}[mode]
    return contract, (((0,), (0,)) if ndim == 3 else ((), ()))


def _raw_dot(a, b, mode):
    return lax.dot_general(a.astype(BF16), b.astype(BF16), _dims(mode, a.ndim), preferred_element_type=F32)


@functools.partial(jax.custom_vjp, nondiff_argnums=(2,))
def _bdot(a, b, mode):
    return _raw_dot(a, b, mode)


def _bdot_fwd(a, b, mode):
    return _raw_dot(a, b, mode), (a, b)


def _bdot_bwd(mode, res, ct):
    a, b = res
    if mode == "nn":
        da, db = _raw_dot(ct, b, "nt"), _raw_dot(a, ct, "tn")
    elif mode == "nt":
        da, db = _raw_dot(ct, b, "nn"), _raw_dot(ct, a, "tn")
    else:
        da, db = _raw_dot(b, ct, "nt"), _raw_dot(a, ct, "nn")
    return da.astype(a.dtype), db.astype(b.dtype)


_bdot.defvjp(_bdot_fwd, _bdot_bwd)


def _split3(x):
    hi = x.astype(BF16)
    r1 = x - hi.astype(F32)
    mid = r1.astype(BF16)
    return hi, mid, (r1 - mid.astype(F32)).astype(BF16)


def _split_dot(tri, x):
    if x.ndim == 3:
        tri = jnp.broadcast_to(tri, (x.shape[0], *tri.shape))
    return sum(lax.dot_general(tri, t, _dims("nn", x.ndim), preferred_element_type=F32) for t in _split3(x))


def _tri(n, lower):
    r = lax.broadcasted_iota(jnp.int32, (n, n), 0)
    c = lax.broadcasted_iota(jnp.int32, (n, n), 1)
    return ((c <= r) if lower else (c >= r)).astype(BF16)


@jax.custom_vjp
def _cumsum_rows(x):
    return _split_dot(_tri(x.shape[-2], True), x)


def _cumsum_rows_fwd(x):
    return _cumsum_rows(x), None


def _cumsum_rows_bwd(_, ct):
    return (_split_dot(_tri(ct.shape[-2], False), ct),)


_cumsum_rows.defvjp(_cumsum_rows_fwd, _cumsum_rows_bwd)


def _abs(x):
    return jnp.where(x >= 0, x, -x)


def _sigmoid(x):
    return lax.logistic(x)


def _log_sigmoid(x):
    return jnp.minimum(x, 0.0) - jnp.log(1.0 + jnp.exp(-_abs(x)))


def _rms(x, g):
    return x * lax.rsqrt(jnp.mean(x * x, axis=-1, keepdims=True) + EPS) * g


def _head_slices(w):
    return [slice(h * w, (h + 1) * w) for h in range(HEADS)]


def _heads(ref, rows=slice(None)):
    return jnp.stack([ref[rows, hs] for hs in _head_slices(HEAD_W)])


def _put_heads(ref, val, rows=slice(None)):
    for h, hs in enumerate(_head_slices(HEAD_W)):
        ref[rows, hs] = val[h].astype(ref.dtype)


def _tile(dim, want):
    if dim <= want or dim % LANES:
        return dim
    t = want
    while dim % t:
        t -= LANES
    return t


def _mm(a, b, mode, out_dtype, name, tm=1024, tn=1024, tk=4096, epilogue=None, extra=(), deps=(), shards=None):
    if shards == "b":
        assert mode == "nn"
        ns = b.shape[2]
        (m, k), (k2, n) = a.shape, (b.shape[1], b.shape[0] * ns)
        tn = ns
    elif mode == "nn":
        (m, k), (k2, n) = a.shape, b.shape
    elif mode == "nt":
        (m, k), (n, k2) = a.shape, b.shape
    else:
        (k, m), (k2, n) = a.shape, b.shape
    assert k == k2, (name, a.shape, b.shape)
    tm, tn, tk = _tile(m, tm), _tile(n, tn), _tile(k, tk)
    nk = k // tk
    out_dtypes = out_dtype if epilogue else (out_dtype,)
    assert nk == 1 or (out_dtype == F32 and not epilogue), name
    n_in = 2 + len(extra)

    def body(*refs):
        p = _raw_dot(refs[0][...], refs[1][...], mode)
        if nk > 1:
            _accumulate(pl.program_id(2), [refs[n_in + len(deps)]], [p])
            return
        outs = epilogue(p, *[r[...] for r in refs[2:n_in]]) if epilogue else (p,)
        for ref, val in zip(refs[n_in + len(deps):], outs):
            ref[...] = val.astype(ref.dtype)

    a_spec = pl.BlockSpec((tk, tm), lambda i, j, kk: (kk, i)) if mode == "tn" else pl.BlockSpec((tm, tk), lambda i, j, kk: (i, kk))
    if shards == "b":
        b_spec = pl.BlockSpec((None, tk, tn), lambda i, j, kk: (j, kk, 0))
    elif mode == "nt":
        b_spec = pl.BlockSpec((tn, tk), lambda i, j, kk: (j, kk))
    else:
        b_spec = pl.BlockSpec((tk, tn), lambda i, j, kk: (kk, j))
    o_spec = pl.BlockSpec((tm, tn), lambda i, j, kk: (i, j))
    res = pl.pallas_call(
        body, name=name, grid=(m // tm, n // tn, nk),
        in_specs=[a_spec, b_spec] + [o_spec] * len(extra) + [ANY] * len(deps), out_specs=[o_spec] * len(out_dtypes),
        out_shape=[jax.ShapeDtypeStruct((m, n), dt) for dt in out_dtypes],
        compiler_params=_cparams("parallel", "parallel", "arbitrary"),
    )(a, b, *extra, *deps)
    return res if epilogue else res[0]


def _mm_tn_whole(pairs, name):
    def body(*refs):
        for i in range(len(pairs)):
            refs[2 * len(pairs) + i][...] = _raw_dot(refs[2 * i][...], refs[2 * i + 1][...], "tn").astype(BF16)

    return pl.pallas_call(
        body, name=name,
        out_shape=[jax.ShapeDtypeStruct((a.shape[1], b.shape[1]), BF16) for a, b in pairs],
        compiler_params=pltpu.CompilerParams(vmem_limit_bytes=VMEM_LIMIT),
    )(*[x for pair in pairs for x in pair])


def _shard_pieces(widths, n):
    bounds = [0]
    for wd in widths:
        bounds.append(bounds[-1] + wd)
    assert bounds[-1] == N_DEV * n
    return [[(i, max(s * n, b) - b, max(s * n, b) - s * n, min((s + 1) * n, b + wd) - max(s * n, b))
             for i, (b, wd) in enumerate(zip(bounds, widths)) if b < (s + 1) * n and b + wd > s * n]
            for s in range(N_DEV)]


def _mm_shard_cols(a, bs, widths, n, name, row_tile, tm, deps=()):
    t = a.shape[0]
    nb = len(bs)
    pieces = _shard_pieces(widths, n)

    first_use = [min(s for s in range(N_DEV) if any(p[0] == i for p in pieces[s])) for i in range(nb)]

    def body(a_ref, *rest):
        b_hbm = rest[:nb]
        o_ref, at_ref = rest[nb + len(deps):nb + len(deps) + 2]
        b_refs, sems = rest[nb + len(deps) + 2:-1], rest[-1]
        loads = [pltpu.make_async_copy(b_hbm[i], b_refs[i], sems.at[i]) for i in range(nb)]
        j = pl.program_id(0)

        @pl.when(j == 0)
        def _():
            for load in loads:
                load.start()
            at_ref[...] = a_ref[...].astype(BF16).T

        for s in range(N_DEV):
            @pl.when(j == s)
            def _(s=s):
                for i in range(nb):
                    if first_use[i] == s:
                        loads[i].wait()
                for i, c_in, c_out, wd in pieces[s]:
                    cols = min(_round_up(wd, LANES), bs[i].shape[1] - c_in) if wd < LANES else wd
                    p = _raw_dot(at_ref[...], b_refs[i][:, c_in:c_in + cols], "nn")
                    o_ref[:, c_out:c_out + wd] = p[:, 0:wd].astype(BF16)

    return pl.pallas_call(
        body, name=name, grid=(N_DEV,),
        in_specs=[pl.BlockSpec((t, tm), lambda j: (0, row_tile))] + [ANY] * (nb + len(deps)),
        out_specs=pl.BlockSpec((None, tm, n), lambda j: (j, 0, 0)),
        out_shape=jax.ShapeDtypeStruct((N_DEV, tm, n), BF16),
        scratch_shapes=[pltpu.VMEM((tm, t), BF16)] + [pltpu.VMEM(b.shape, b.dtype) for b in bs]
        + [pltpu.SemaphoreType.DMA((nb,))],
        compiler_params=_cparams("arbitrary"),
    )(a, *bs, *deps)


def _round_up(v, m):
    return -(-v // m) * m


def _rowwise(name, fn, rows, params, out_rows, out_accs=(), tile=256, deps=()):
    t = rows[0].shape[0]
    r = min(tile, t)
    assert t % r == 0
    n_in, n_or = len(rows) + len(params), len(out_rows)
    n_all = n_in + len(deps)
    params = list(params) + list(deps)

    def body(*refs):
        vals = [ref[...] for ref in refs[:n_in]]
        outs = refs[n_all:]
        ro, ao = fn(*vals)
        for ref, v in zip(outs[:n_or], ro):
            ref[...] = v.astype(ref.dtype)
        if out_accs:
            _accumulate(pl.program_id(0), outs[n_or:], ao)

    def full(shape, **kw):
        return pl.BlockSpec(shape, lambda i, nd=len(shape): (0,) * nd, **kw)

    def tiled(w):
        if isinstance(w, tuple):
            return pl.BlockSpec((w[0], r, w[1]), lambda i: (0, i, 0))
        return pl.BlockSpec((r, w), lambda i: (i, 0))

    def whole(w):
        return (w[0], t, w[1]) if isinstance(w, tuple) else (t, w)

    return pl.pallas_call(
        body, name=name, grid=(t // r,),
        in_specs=[tiled(a.shape[1] if a.ndim == 2 else (a.shape[0], a.shape[2])) for a in rows]
        + [full(p.shape, pipeline_mode=pl.Buffered(1)) for p in params],
        out_specs=[tiled(w) for w, _ in out_rows] + [full(s) for s, _ in out_accs],
        out_shape=[jax.ShapeDtypeStruct(whole(w), dt) for w, dt in out_rows] + [jax.ShapeDtypeStruct(s, dt) for s, dt in out_accs],
        compiler_params=_cparams("arbitrary"),
    )(*rows, *params)


def _accumulate(step, refs, vals):
    for ref, v in zip(refs, vals):
        @pl.when(step == 0)
        def _(ref=ref, v=v):
            ref[...] = v.astype(ref.dtype)

        @pl.when(step > 0)
        def _(ref=ref, v=v):
            ref[...] += v.astype(ref.dtype)


def _gla_chunk(q, k, v, la, st):
    c = q.shape[-2]
    row = lax.broadcasted_iota(jnp.int32, (c, c), 0)
    col = lax.broadcasted_iota(jnp.int32, (c, c), 1)
    cum = _cumsum_rows(la)
    cl = jnp.sum(la, axis=-2, keepdims=True)
    ep = jnp.exp(cum)
    en = jnp.exp(-cum)
    qs = q * (GLA_DK ** -0.5)
    qp = qs * ep
    a_f = _bdot(qp, k * en, "nt")
    a_b = _bdot(qs * en, k * ep, "nt")
    sc = jnp.where(row >= col, a_f, a_b)
    o = _bdot(sc, v, "nn") + _bdot(qp, st, "nt")
    kd = k * jnp.exp(cl - cum)
    st_new = st * jnp.exp(cl) + _bdot(v, kd, "tn")
    return o, st_new


def _gla_specs(nc, rev):
    nb = nc // CHUNKS_PER_STEP
    rows = CHUNKS_PER_STEP * CHUNK

    def blk(n):
        return (nb - 1 - n) if rev else n
    hm = pl.BlockSpec((HEADS, rows, GLA_DK), lambda n: (0, blk(n), 0))
    tm = pl.BlockSpec((rows, HEADS * HEAD_W), lambda n: (blk(n), 0))
    st = pl.BlockSpec((HEADS, CHUNKS_PER_STEP, HEAD_W, GLA_DK), lambda n: (0, blk(n), 0, 0))
    return nb, hm, tm, st


def _chunk_rows(c):
    return slice(c * CHUNK, (c + 1) * CHUNK)


def _ml_chunk(q, k, v, li_r, lf_r, cm, nv, m):
    c = q.shape[-2]
    row = lax.broadcasted_iota(jnp.int32, (c, c), 0)
    col = lax.broadcasted_iota(jnp.int32, (c, c), 1)
    eye = (row == col).astype(F32)
    li_c = jnp.sum(eye * li_r, axis=-1, keepdims=True)
    lf_c = jnp.sum(eye * lf_r, axis=-1, keepdims=True)
    fc_c = jnp.sum((col <= row).astype(F32) * lf_r, axis=-1, keepdims=True)
    fc_r = jnp.sum((row <= col).astype(F32) * lf_c, axis=-2, keepdims=True)
    f_last = jnp.sum(lf_r, axis=-1, keepdims=True)
    kc = k * (HEAD_W ** -0.5)
    a_c = f_last - fc_c + li_c
    m_loc = jnp.max(a_c, axis=-2, keepdims=True)
    kw = kc * jnp.exp(a_c - m_loc)
    c_chunk = _bdot(kw, v, "tn")
    n_chunk = jnp.sum(kw, axis=-2, keepdims=True)
    m_new = jnp.maximum(f_last + m, m_loc)
    sp = jnp.exp(f_last + m - m_new)
    sl = jnp.exp(m_loc - m_new)
    cm_new = sp * cm + sl * c_chunk
    nv_new = sp * nv + sl * n_chunk
    log_d = li_r - _abs(fc_c - fc_r)
    g_inter = fc_c + m
    m_t = jnp.maximum(g_inter, jnp.max(log_d, axis=-1, keepdims=True))
    s = _bdot(q, kc, "nt") * jnp.exp(log_d - m_t)
    sc = jnp.exp(g_inter - m_t)
    num = _bdot(s, v, "nn") + sc * _bdot(q, cm, "nn")
    den = jnp.sum(s, axis=-1, keepdims=True) + sc * jnp.sum(q * nv, axis=-1, keepdims=True)
    den = jnp.maximum(_abs(den), jnp.exp(-m_t))
    return num / den, cm_new, nv_new, m_new


def _ml_specs(nc, rev):
    nb = nc // CHUNKS_PER_STEP

    def blk(n):
        return (nb - 1 - n) if rev else n
    tm = pl.BlockSpec((CHUNKS_PER_STEP * CHUNK, HEADS * HEAD_W), lambda n: (blk(n), 0))
    gate = pl.BlockSpec((HEADS, CHUNKS_PER_STEP, 1, CHUNK), lambda n: (0, blk(n), 0, 0))
    cm = pl.BlockSpec((HEADS, CHUNKS_PER_STEP, HEAD_W, HEAD_W), lambda n: (0, blk(n), 0, 0))
    vec = pl.BlockSpec((HEADS, CHUNKS_PER_STEP, 1, HEAD_W), lambda n: (0, blk(n), 0, 0))
    return nb, tm, gate, cm, vec


_ML_STATE = [pltpu.VMEM((HEADS, HEAD_W, HEAD_W), F32), pltpu.VMEM((HEADS, 1, HEAD_W), F32), pltpu.VMEM((HEADS, 1, HEAD_W), F32)]


def _recurrences_fwd(q, k, v, la, qm, km, vm, li, lf, deps=()):
    t = v.shape[0]
    nc = t // CHUNK
    nb, hm, tm, st = _gla_specs(nc, False)
    _, _, gate, cm, vec = _ml_specs(nc, False)
    n_in = 9 + len(deps)

    def body(*refs):
        q_ref, k_ref, v_ref, la_ref, qm_ref, km_ref, vm_ref, li_ref, lf_ref = refs[:9]
        o_ref, sp_ref, hc_ref, cp_ref, np_ref, mp_ref, st_ref, c_ref, n_ref, m_ref = refs[n_in:]

        @pl.when(pl.program_id(0) == 0)
        def _():
            for ref in (st_ref, c_ref, n_ref, m_ref):
                ref[...] = jnp.zeros_like(ref)

        s, cs, ns, ms = st_ref[...], c_ref[...], n_ref[...], m_ref[...][:, :, 0:1]
        for c in range(CHUNKS_PER_STEP):
            r = _chunk_rows(c)
            sp_ref[:, c] = s
            o, s = _gla_chunk(q_ref[:, r], k_ref[:, r], _heads(v_ref, r), la_ref[:, r], s)
            _put_heads(o_ref, o, r)
            cp_ref[:, c] = cs
            np_ref[:, c] = ns
            mp_ref[:, c] = jnp.broadcast_to(ms, m_ref.shape)
            hc, cs, ns, ms = _ml_chunk(_heads(qm_ref, r), _heads(km_ref, r), _heads(vm_ref, r), li_ref[:, c], lf_ref[:, c],
                                       cs, ns, ms)
            _put_heads(hc_ref, hc, r)
        st_ref[...] = s
        c_ref[...] = cs
        n_ref[...] = ns
        m_ref[...] = jnp.broadcast_to(ms, m_ref.shape)

    tm_shape = jax.ShapeDtypeStruct((t, HEADS * HEAD_W), F32)
    vec_shape = jax.ShapeDtypeStruct((HEADS, nc, 1, HEAD_W), F32)
    return pl.pallas_call(
        body, name="recurrences_fwd", grid=(nb,),
        in_specs=[hm, hm, tm, hm, tm, tm, tm, gate, gate] + [ANY] * len(deps), out_specs=[tm, st, tm, cm, vec, vec],
        out_shape=[tm_shape, jax.ShapeDtypeStruct((HEADS, nc, HEAD_W, GLA_DK), F32),
                   tm_shape, jax.ShapeDtypeStruct((HEADS, nc, HEAD_W, HEAD_W), F32), vec_shape, vec_shape],
        scratch_shapes=[pltpu.VMEM((HEADS, HEAD_W, GLA_DK), F32)] + _ML_STATE,
        compiler_params=_cparams("arbitrary"),
    )(q, k, v, la, qm, km, vm, li, lf, *deps)


def _recurrences_bwd(q, k, v, la, sp, do, qm, km, vm, li, lf, cp, npv, mp, dhc, deps=()):
    t = v.shape[0]
    nc = t // CHUNK
    nb, hm, tm, st = _gla_specs(nc, True)
    _, _, gate, cm, vec = _ml_specs(nc, True)
    n_in = 15 + len(deps)

    def body(*refs):
        (q_ref, k_ref, v_ref, la_ref, sp_ref, do_ref,
         qm_ref, km_ref, vm_ref, li_ref, lf_ref, cp_ref, np_ref, mp_ref, dhc_ref) = refs[:15]
        (dq_ref, dk_ref, dv_ref, dla_ref, dqm_ref, dkm_ref, dvm_ref, dli_ref, dlf_ref,
         ds_ref, dc_ref, dn_ref, dm_ref) = refs[n_in:]

        @pl.when(pl.program_id(0) == 0)
        def _():
            for ref in (ds_ref, dc_ref, dn_ref, dm_ref):
                ref[...] = jnp.zeros_like(ref)

        ds, dc, dn, dm = ds_ref[...], dc_ref[...], dn_ref[...], dm_ref[...][:, :, 0:1]
        for c in reversed(range(CHUNKS_PER_STEP)):
            r = _chunk_rows(c)
            _, vjp = jax.vjp(_gla_chunk, q_ref[:, r], k_ref[:, r], _heads(v_ref, r), la_ref[:, r], sp_ref[:, c])
            dq, dk, dv, dla, ds = vjp((_heads(do_ref, r), ds))
            dq_ref[:, r] = dq.astype(dq_ref.dtype)
            dk_ref[:, r] = dk.astype(dk_ref.dtype)
            _put_heads(dv_ref, dv, r)
            dla_ref[:, r] = dla
            _, vjp = jax.vjp(_ml_chunk, _heads(qm_ref, r), _heads(km_ref, r), _heads(vm_ref, r), li_ref[:, c], lf_ref[:, c],
                             cp_ref[:, c], np_ref[:, c], mp_ref[:, c][:, :, 0:1])
            dqm, dkm, dvm, dli, dlf, dc, dn, dm = vjp((_heads(dhc_ref, r), dc, dn, dm))
            _put_heads(dqm_ref, dqm, r)
            _put_heads(dkm_ref, dkm, r)
            _put_heads(dvm_ref, dvm, r)
            dli_ref[:, c] = dli
            dlf_ref[:, c] = dlf
        ds_ref[...] = ds
        dc_ref[...] = dc
        dn_ref[...] = dn
        dm_ref[...] = jnp.broadcast_to(dm, dm_ref.shape)

    hm_shape = jax.ShapeDtypeStruct((HEADS, t, GLA_DK), BF16)
    tm_shape = jax.ShapeDtypeStruct((t, HEADS * HEAD_W), F32)
    gate_shape = jax.ShapeDtypeStruct((HEADS, nc, 1, CHUNK), F32)
    return pl.pallas_call(
        body, name="recurrences_bwd", grid=(nb,),
        in_specs=[hm, hm, tm, hm, st, tm, tm, tm, tm, gate, gate, cm, vec, vec, tm] + [ANY] * len(deps),
        out_specs=[hm, hm, tm, hm, tm, tm, tm, gate, gate],
        out_shape=[hm_shape, hm_shape, jax.ShapeDtypeStruct((t, HEADS * HEAD_W), BF16),
                   jax.ShapeDtypeStruct((HEADS, t, GLA_DK), F32), tm_shape, tm_shape, tm_shape, gate_shape, gate_shape],
        scratch_shapes=[pltpu.VMEM((HEADS, HEAD_W, GLA_DK), F32)] + _ML_STATE,
        compiler_params=_cparams("arbitrary"),
    )(q, k, v, la, sp, do, qm, km, vm, li, lf, cp, npv, mp, dhc, *deps)


@jax.custom_vjp
def _bdot_diag(x, w):
    b = w.shape[1]
    return jnp.concatenate([_raw_dot(x[:, :b], w[0], "nn"), _raw_dot(x[:, b:], w[1], "nn")], axis=1)


def _bdot_diag_fwd(x, w):
    return _bdot_diag(x, w), (x, w)


def _bdot_diag_bwd(res, ct):
    x, w = res
    b = w.shape[1]
    dx = jnp.concatenate([_raw_dot(ct[:, :b], w[0], "nt"), _raw_dot(ct[:, b:], w[1], "nt")], axis=1)
    dw = jnp.stack([_raw_dot(x[:, :b], ct[:, :b], "tn"), _raw_dot(x[:, b:], ct[:, b:], "tn")])
    return dx.astype(x.dtype), dw.astype(w.dtype)


_bdot_diag.defvjp(_bdot_diag_fwd, _bdot_diag_bwd)


def _ml_pre(s0, s1, s2, s3, cw0, cw1, cw2, cw3, cb, wq, wk, wv, wiq, wik, wiv, bif):
    pre = cb + cw0 * s0 + cw1 * s1 + cw2 * s2 + cw3 * s3
    xc = pre * _sigmoid(pre)
    q = _bdot_diag(xc, wq)
    k = _bdot_diag(xc, wk)
    v = _bdot_diag(s3, wv)
    gates = _bdot(q, wiq, "nn") + _bdot(k, wik, "nn") + _bdot(v, wiv, "nn") + bif
    lane = lax.broadcasted_iota(jnp.int32, gates.shape, 1)
    gl = jnp.where(lane < HEADS, gates, _log_sigmoid(gates))
    return xc, q, k, v, gl


def _delayed(xs_ref, x_ref, halo_ref, r, first):
    xs_ref[0:HALO, :] = jnp.where(first, 0.0, halo_ref[...])
    xs_ref[HALO:HALO + r, :] = x_ref[...]
    return [xs_ref[pl.ds(HALO - (CONV_K - 1) + j, r), :] for j in range(CONV_K)]


def _halo_spec(r, w, tile_of):
    return pl.BlockSpec((HALO, w), lambda i: (jnp.maximum(tile_of(i) * (r // HALO) - 1, 0), 0))


def _full_spec(shape):
    return pl.BlockSpec(shape, lambda i, nd=len(shape): (0,) * nd)


def _ml_pre_fwd(x_m, params, tile=256, deps=()):
    t, w = x_m.shape
    r = min(tile, t)

    def body(*refs):
        x_ref, halo_ref = refs[:2]
        p = [ref[...] for ref in refs[2:2 + len(params)]]
        outs = refs[2 + len(params) + len(deps):-1]
        res = _ml_pre(*_delayed(refs[-1], x_ref, halo_ref, r, pl.program_id(0) == 0), *p)
        for ref, val in zip(outs, res):
            ref[...] = val

    row = pl.BlockSpec((r, w), lambda i: (i, 0))
    return pl.pallas_call(
        body, name="ml_pre_fwd", grid=(t // r,),
        in_specs=[row, _halo_spec(r, w, lambda i: i)] + [_full_spec(p.shape) for p in params]
        + [ANY] * len(deps),
        out_specs=[row] * 4 + [pl.BlockSpec((r, LANES), lambda i: (i, 0))],
        out_shape=[jax.ShapeDtypeStruct((t, w), F32)] * 4 + [jax.ShapeDtypeStruct((t, LANES), F32)],
        scratch_shapes=[pltpu.VMEM((r + HALO, w), F32)],
        compiler_params=_cparams("arbitrary"),
    )(x_m, x_m, *params, *deps)


def _ml_pre_bwd(x_m, params, cts, tile=256):
    t, w = x_m.shape
    r = min(tile, t)
    nt = t // r
    n_p = len(params)

    def body(*refs):
        x_ref, halo_ref = refs[:2]
        p = [ref[...] for ref in refs[2:2 + n_p]]
        ct = [ref[...] for ref in refs[2 + n_p:7 + n_p]]
        dx_ref = refs[7 + n_p]
        dp_refs = refs[8 + n_p:8 + 2 * n_p]
        xs_ref, ds_ref, carry_ref = refs[8 + 2 * n_p:]
        step = pl.program_id(0)

        @pl.when(step == 0)
        def _():
            ds_ref[...] = jnp.zeros_like(ds_ref)
            carry_ref[...] = jnp.zeros_like(carry_ref)

        _, vjp = jax.vjp(_ml_pre, *_delayed(xs_ref, x_ref, halo_ref, r, step == nt - 1), *p)
        grads = vjp(tuple(ct))
        for j in range(CONV_K):
            ds_ref[j, HALO:HALO + r, :] = grads[j]
        lead = HALO + CONV_K - 1
        d_tile = sum(ds_ref[j, pl.ds(lead - j, r), :] for j in range(CONV_K))
        d_halo = sum(ds_ref[j, pl.ds(CONV_K - 1 - j, HALO), :] for j in range(CONV_K))
        dx_ref[...] = jnp.concatenate([d_tile[:r - HALO], d_tile[r - HALO:] + carry_ref[...]], axis=0).astype(dx_ref.dtype)
        carry_ref[...] = d_halo
        _accumulate(step, dp_refs, grads[CONV_K:])

    row = pl.BlockSpec((r, w), lambda i: (nt - 1 - i, 0))
    return pl.pallas_call(
        body, name="ml_pre_bwd", grid=(nt,),
        in_specs=[row, _halo_spec(r, w, lambda i: nt - 1 - i)] + [_full_spec(p.shape) for p in params]
        + [row] * 4 + [pl.BlockSpec((r, LANES), lambda i: (nt - 1 - i, 0))],
        out_specs=[row] + [_full_spec(p.shape) for p in params],
        out_shape=[jax.ShapeDtypeStruct((t, w), BF16)] + [jax.ShapeDtypeStruct(p.shape, F32) for p in params],
        scratch_shapes=[pltpu.VMEM((r + HALO, w), F32), pltpu.VMEM((CONV_K, r + 2 * HALO, w), F32), pltpu.VMEM((HALO, w), F32)],
        compiler_params=_cparams("arbitrary"),
    )(x_m, x_m, *params, *cts)


def _per_head(fn, row_vals, head_params, shared_params=()):
    return [fn(*[a[:, hs] for a in row_vals], *[p[:, hs] for p in head_params], *shared_params) for hs in _head_slices(HEAD_W)]


def _gla_out(o, g, gn):
    return _rms(o, gn) * (g * _sigmoid(g))


def _ml_out(hc, op, xc, g, sk):
    hcell = hc * _sigmoid(op)
    mu = jnp.mean(hcell, axis=-1, keepdims=True)
    d = hcell - mu
    var = jnp.mean(d * d, axis=-1, keepdims=True)
    return d * lax.rsqrt(var + EPS) * g + sk * xc


def _log_decay(al, w, b):
    return _log_sigmoid(_bdot(al, w, "nn") + b) * (1.0 / GLA_GATE_NORM)


def _merge(ga, gb, ya, yb):
    ga, gb, ya, yb = (a.astype(F32) for a in (ga, gb, ya, yb))
    return _sigmoid(ga) * ya + _sigmoid(gb) * yb


def _post_mix(x, z, gpm, gpl):
    x1 = x + _rms(z, gpm)
    return x1, _rms(x1, gpl)


def _loss_rows(x1, dn, tgt, g):
    e = x1 + _rms(dn, g) - tgt
    return 0.5 * jnp.sum(jnp.mean(e * e, axis=-1, keepdims=True), axis=0, keepdims=True)


def _lin(p):
    return 4 * p[0] + 2 * p[1] + p[2]


def _me():
    return lax.axis_index("x"), lax.axis_index("y"), lax.axis_index("c")


def _flip(p, k):
    return tuple((1 - v) if (k >> (2 - i)) & 1 else v for i, v in enumerate(p))


ANY = pl.BlockSpec(memory_space=pl.ANY)


HBM = pl.BlockSpec(memory_space=pltpu.HBM)
SEM = pl.BlockSpec(memory_space=pltpu.SEMAPHORE)
DATAFLOW = pltpu.SideEffectType.DATAFLOW_SIDE_EFFECTING


SIBLING = 1
OTHER_CHIPS = (2, 4, 6)


def _peer_copies(kinds, srcs, lands, send_sems, recv_sems):
    me = _me()
    copies = []
    for a, (kind, src, land) in enumerate(zip(kinds, srcs, lands)):
        masks = {"gather": range(1, N_DEV), "exchange": range(1, N_DEV), "gather_chips": (SIBLING, *OTHER_CHIPS),
                 "gather_pass": OTHER_CHIPS}[kind]
        for k in masks:
            peer = _flip(me, k)
            if kind == "gather_pass":
                block = land.at[_lin(peer)]
                src_ref, dst_ref, target = block, block, _flip(me, SIBLING)
            else:
                src_ref, dst_ref, target = (src.at[_lin(peer)] if kind == "exchange" else src), land.at[_lin(me)], peer
            copies.append(pltpu.make_async_remote_copy(
                src_ref=src_ref, dst_ref=dst_ref, send_sem=send_sems.at[a * 7 + k - 1], recv_sem=recv_sems.at[a * 7 + k - 1],
                device_id=target, device_id_type=MESH))
    return copies


def _own_copies(kinds, srcs, lands, own_sems):
    return [pltpu.make_async_copy(src, land.at[_lin(_me())], own_sems.at[a])
            for a, (kind, src, land) in enumerate(zip(kinds, srcs, lands)) if kind in ("gather", "gather_chips")]


def _copies_start(kind, srcs, name, after=None, lands=None):
    n = len(srcs)
    extra = [] if after is None else [after]
    kind = [kind] * n if isinstance(kind, str) else list(kind)
    land_shapes = [(s.shape if k == "exchange" else (N_DEV, *s.shape)) for k, s in zip(kind, srcs)]
    lands = [lax.empty(ls, s.dtype) for ls, s in zip(land_shapes, srcs)] if lands is None else lands

    def body(*refs):
        sems = refs[2 * n + len(extra):]
        for cp in _peer_copies(kind, refs[:n], refs[n:2 * n], sems[0], sems[1]) + _own_copies(kind, refs[:n], refs[n:2 * n], sems[2]):
            cp.start()
        refs[-1][...] = jnp.zeros_like(refs[-1])

    def hbm(a):
        return pltpu.with_memory_space_constraint(a, pltpu.HBM)

    out = pl.pallas_call(
        body, name=name,
        out_shape=(pltpu.SemaphoreType.DMA((7 * n,)), pltpu.SemaphoreType.DMA((7 * n,)), pltpu.SemaphoreType.DMA((n,)),
                   *[pltpu.HBM(s.shape, s.dtype) for s in srcs],
                   *[pltpu.HBM(ls, s.dtype) for ls, s in zip(land_shapes, srcs)],
                   jax.ShapeDtypeStruct((8, LANES), F32)),
        in_specs=[HBM] * (2 * n) + [ANY] * len(extra),
        out_specs=(SEM, SEM, SEM, *[HBM] * (2 * n), pl.BlockSpec(memory_space=pltpu.VMEM)),
        input_output_aliases={i: 3 + i for i in range(2 * n)},
        compiler_params=pltpu.CompilerParams(has_side_effects=DATAFLOW),
    )(*[hbm(s) for s in srcs], *[hbm(a) for a in lands], *extra)
    return (kind, n, out[:-1]), out[-1]


def _copies_wait(state, after, name):
    kind, n, (send_sems, recv_sems, own_sems, *thru) = state
    after = list(after) if isinstance(after, (list, tuple)) else [after]

    def body(*refs):
        for cp in _peer_copies(kind, refs[:n], refs[n:2 * n], refs[2 * n], refs[2 * n + 1]):
            cp.wait_send()
            cp.wait_recv()
        for cp in _own_copies(kind, refs[:n], refs[n:2 * n], refs[2 * n + 2]):
            cp.wait()

    out = pl.pallas_call(
        body, name=name,
        out_shape=tuple(pltpu.HBM(t.shape, t.dtype) for t in thru),
        in_specs=[HBM] * (2 * n) + [SEM, SEM, SEM] + [ANY] * len(after), out_specs=tuple([HBM] * (2 * n)),
        input_output_aliases={i: i for i in range(2 * n)},
        compiler_params=pltpu.CompilerParams(has_side_effects=DATAFLOW),
    )(*thru, send_sems, recv_sems, own_sems, *after)
    return out[:n], out[n:]


def _adamw(w, g, m, v):
    m2 = ADAM_B1 * m + (1.0 - ADAM_B1) * g
    v2 = ADAM_B2 * v + (1.0 - ADAM_B2) * (g * g)
    m_hat = m2 / (1.0 - ADAM_B1 ** ADAM_STEP)
    v_hat = v2 / (1.0 - ADAM_B2 ** ADAM_STEP)
    delta = -ADAM_LR * (m_hat / (jnp.sqrt(v_hat) + ADAM_EPS) + ADAM_WD * w)
    return delta, m2, v2


def _sum_adamw(lands, parts, me_idx, w, m, v, name, tile=256):
    r, c = w.shape
    nchunks = len(lands)
    tr = min(tile, r // nchunks)
    per_chunk = r // nchunks // tr
    per = 1 + N_DEV

    def body(me_ref, *refs):
        w_ref, m_ref, v_ref, g_ref, d_ref, m2_ref, v2_ref = refs[nchunks * per:]
        for k in range(nchunks):
            own_ref, slots = refs[k * per], refs[k * per + 1:(k + 1) * per]

            @pl.when(pl.program_id(0) // per_chunk == k)
            def _(own_ref=own_ref, slots=slots):
                own = own_ref[...].astype(F32)
                g = None
                for s in range(N_DEV):
                    term = jnp.where(me_ref[0] == s, own, slots[s][...].astype(F32))
                    g = term if g is None else g + term
                d, m2, v2 = _adamw(w_ref[...], g, m_ref[...], v_ref[...])
                g_ref[...] = g
                d_ref[...] = d
                m2_ref[...] = m2
                v2_ref[...] = v2

    def chunk_specs(k):
        def tile_of(i):
            return jnp.clip(i - k * per_chunk, 0, per_chunk - 1)

        def slot_spec(s):
            return pl.BlockSpec((None, tr, c), lambda i, me: (jnp.where(me[0] == s, (s + 1) % N_DEV, s), tile_of(i), 0))
        return [pl.BlockSpec((None, tr, c), lambda i, me: (me[0], tile_of(i), 0))] + [slot_spec(s) for s in range(N_DEV)]

    row = pl.BlockSpec((tr, c), lambda i, me: (i, 0))
    operands = [a for land, part in zip(lands, parts) for a in (part, *[land] * N_DEV)]
    return pl.pallas_call(
        body, name=name,
        grid_spec=pltpu.PrefetchScalarGridSpec(
            num_scalar_prefetch=1, grid=(r // tr,),
            in_specs=[s for k in range(nchunks) for s in chunk_specs(k)] + [row] * 3,
            out_specs=[row] * 4),
        out_shape=[jax.ShapeDtypeStruct((r, c), F32)] * 4,
        compiler_params=_cparams("parallel"),
    )(me_idx, *operands, w, m, v)


def _cols_view(a):
    r, c = a.shape
    return jnp.transpose(a.reshape(r // LANES, LANES, c), (2, 0, 1))


def _from_cols_view(a, r, c):
    return jnp.transpose(a, (1, 2, 0)).reshape(r, c)


def _sum_adamw_cols(lands, parts, me_idx, w, m, v, name):
    nchunks = len(lands)
    rows_k, c = lands[0].shape[1:]
    r = nchunks * rows_k
    steps = r // LANES
    per_chunk = rows_k // LANES
    per = 1 + N_DEV
    c_pad = _round_up(c, LANES)
    assert steps == 8
    block = steps * max(n for n in range(1, UPDATE_COLS_MAX + 1) if c % n == 0)

    def body(me_ref, *refs):
        w_ref, m_ref, v_ref, g_ref, d_ref, m2_ref, v2_ref, gt_ref = refs[nchunks * per:]
        for i in range(steps):
            k = i // per_chunk
            own_ref, slots = refs[k * per], refs[k * per + 1:(k + 1) * per]

            @pl.when(pl.program_id(0) == i)
            def _(i=i, own_ref=own_ref, slots=slots):
                own = own_ref[...].astype(F32)
                g = None
                for s in range(N_DEV):
                    term = jnp.where(me_ref[0] == s, own, slots[s][...].astype(F32))
                    g = term if g is None else g + term
                g = jnp.concatenate([g, jnp.zeros((LANES, c_pad - c), F32)], axis=1)
                for j in range(c_pad // LANES):
                    cols = min(LANES, c - j * LANES)
                    gt_ref[pl.ds(steps * LANES * j + i, cols, stride=steps), :] = jnp.transpose(g[:, j * LANES:(j + 1) * LANES])[0:cols]

        @pl.when(pl.program_id(0) == steps - 1)
        def _():
            def update(b, carry):
                cols = pl.ds(b * (block // steps), block // steps)
                g = gt_ref[pl.ds(pl.multiple_of(b * block, steps), block), :].reshape(block // steps, steps, LANES)
                d, m2, v2 = _adamw(w_ref[cols], g, m_ref[cols], v_ref[cols])
                g_ref[cols] = g
                d_ref[cols] = d
                m2_ref[cols] = m2
                v2_ref[cols] = v2
                return carry
            lax.fori_loop(0, c * steps // block, update, 0)

    def chunk_specs(k):
        def tile_of(i):
            return jnp.clip(i - k * per_chunk, 0, per_chunk - 1)

        def slot_spec(s):
            return pl.BlockSpec((None, LANES, c), lambda i, me: (jnp.where(me[0] == s, (s + 1) % N_DEV, s), tile_of(i), 0))
        return [pl.BlockSpec((None, LANES, c), lambda i, me: (me[0], tile_of(i), 0))] + [slot_spec(s) for s in range(N_DEV)]

    whole = pl.BlockSpec((c, steps, LANES), lambda i, me: (0, 0, 0))
    operands = [a for land, part in zip(lands, parts) for a in (part, *[land] * N_DEV)]
    return pl.pallas_call(
        body, name=name,
        grid_spec=pltpu.PrefetchScalarGridSpec(
            num_scalar_prefetch=1, grid=(steps,),
            in_specs=[s for k in range(nchunks) for s in chunk_specs(k)]
            + [pl.BlockSpec((c, steps, LANES), lambda i, me: (0, 0, 0), pipeline_mode=pl.Buffered(1))] * 3,
            out_specs=[whole] * 4,
            scratch_shapes=[pltpu.VMEM((c * steps, LANES), F32)]),
        out_shape=[jax.ShapeDtypeStruct((c, steps, LANES), F32)] * 4,
        compiler_params=_cparams("arbitrary"),
    )(me_idx, *operands, w, m, v)


def _small_update(name, me_idx, kinds, lands, owns, ws, ms, vs, sums=()):
    n = len(ws)
    lands, owns = list(lands) + [s[0] for s in sums], list(owns) + [s[1] for s in sums]
    kinds = list(kinds) + ["gather"] * len(sums)
    nl = len(lands)

    def summed(me, land_ref, own):
        g = None
        for s in range(N_DEV):
            term = jnp.where(me == s, own, land_ref[s]).astype(F32)
            g = term if g is None else g + term
        return g

    def body(me_ref, *refs):
        land_refs, own_refs = refs[:nl], refs[nl:2 * nl]
        w_refs, m_refs, v_refs = (refs[2 * nl + i * n:2 * nl + (i + 1) * n] for i in range(3))
        outs = refs[2 * nl + 3 * n:]
        me = me_ref[0]
        for i in range(n):
            g = summed(me, land_refs[i], own_refs[i][...])
            d, m2, v2 = _adamw(w_refs[i][...], g, m_refs[i][...], v_refs[i][...])
            for ref, val in zip(outs[4 * i:4 * i + 4], (g, d, m2, v2)):
                ref[...] = val
        for i in range(n, nl):
            outs[4 * n + i - n][...] = summed(me, land_refs[i], own_refs[i][...])

    def whole(shape):
        return pl.BlockSpec(shape, lambda i, me, nd=len(shape): (0,) * nd)

    def own_spec(kind, own):
        if kind == "gather":
            return whole(own.shape)
        return pl.BlockSpec((None, *own.shape[1:]), lambda i, me: (me[0], 0, 0))

    shapes = [w.shape for w in ws]
    out_shapes = [s for s in shapes for _ in range(4)] + [s[1].shape for s in sums]
    return pl.pallas_call(
        body, name=name,
        grid_spec=pltpu.PrefetchScalarGridSpec(
            num_scalar_prefetch=1, grid=(1,),
            in_specs=[whole(a.shape) for a in lands] + [own_spec(k, o) for k, o in zip(kinds, owns)]
            + [whole(s) for s in shapes] * 3,
            out_specs=[whole(s) for s in out_shapes]),
        out_shape=[jax.ShapeDtypeStruct(s, F32) for s in out_shapes],
        compiler_params=_cparams("arbitrary"),
    )(me_idx, *lands, *owns, *ws, *ms, *vs)


def _small_view(n, a):
    if a.ndim == 1:
        return a.reshape(1, -1)
    if a.ndim == 3:
        return a.transpose(1, 2, 0).reshape(QKV_BLOCK * QKV_BLOCK, -1)
    return a.T if n == "w_if" else a


def _small_unview(n, a, shape):
    if len(shape) == 1:
        return a.reshape(shape)
    if len(shape) == 3:
        return a.reshape(QKV_BLOCK, QKV_BLOCK, -1).transpose(2, 0, 1)
    return a.T if n == "w_if" else a


def _small_shards(n, g):
    if n == "w_if":
        return g.reshape(N_DEV, -1, g.shape[1]).transpose(0, 2, 1)
    return g.reshape(g.shape[0], N_DEV, -1).transpose(1, 0, 2)


def _small_unshard(n, s):
    if n == "w_if":
        return s.transpose(0, 2, 1).reshape(-1, s.shape[1])
    return s.transpose(1, 0, 2).reshape(s.shape[1], -1)


def _from_hm(a):
    h, t, d = a.shape
    return a.transpose(1, 0, 2).reshape(t, h * d)


def _gate_rows(g):
    t = g.shape[0]
    return g.T.reshape(HEADS, t // CHUNK, 1, CHUNK)


def _gate_cols(g):
    h, nc, _, c = g.shape
    return g.reshape(h, nc * c).T


def _blockdiag_dense(w):
    n = w.shape[0] * QKV_BLOCK // 2
    tiled = jnp.tile(w.reshape(2, n, QKV_BLOCK), (1, 1, n // QKV_BLOCK))
    r = lax.broadcasted_iota(jnp.int32, (2, n, n), 1)
    c = lax.broadcasted_iota(jnp.int32, (2, n, n), 2)
    return jnp.where(r // QKV_BLOCK == c // QKV_BLOCK, tiled, 0.0)


def _blockdiag_blocks(dense):
    _, n, _ = dense[0].shape
    k = len(dense)

    def body(*refs):
        r = lax.broadcasted_iota(jnp.int32, (n, n), 0)
        c = lax.broadcasted_iota(jnp.int32, (n, n), 1)
        fr = lax.broadcasted_iota(jnp.int32, (n, LANES), 0)
        fc = lax.broadcasted_iota(jnp.int32, (n, LANES), 1)
        fold = ((fr & (QKV_BLOCK - 1)) == fc).astype(BF16)
        for i in range(k):
            for half in range(2):
                kept = jnp.where((r >> 2) == (c >> 2), refs[i][half], 0.0)
                refs[k + i][half] = sum(lax.dot_general(t, fold, _dims("nn", 2), preferred_element_type=F32)
                                        for t in _split3(kept))

    out = pl.pallas_call(body, name="blockdiag_blocks", out_shape=[jax.ShapeDtypeStruct((2, n, LANES), F32)] * k)(*dense)
    return [o[:, :, 0:QKV_BLOCK].reshape(2 * n // QKV_BLOCK, QKV_BLOCK, QKV_BLOCK) for o in out]


def _col_blocks(w):
    k, n = w.shape
    return w.reshape(k, N_DEV, n // N_DEV).transpose(1, 0, 2)


def _from_col_blocks(g):
    d, k, n = g.shape
    return g.transpose(1, 0, 2).reshape(k, d * n)


def _first_norm(x, g):
    return _rowwise("pre_mix_norm", lambda xv, gv: ((_rms(xv, gv),), ()), [x], [g], [(x.shape[1], BF16)])[0]


def _local_step(x, h, tgt, weight, ws, prefetch, pass_on, on_grads, on_small):
    t, d = x.shape
    g1 = ws["g_pre_mix"]

    def dep(token):
        return () if token is None else (token,)

    w_in = weight("w_in", x)
    fetch_mix = prefetch(("w_pa", "w_pb", "w_o"), w_in)
    fetch_up = prefetch(("w_up", "w_down"), fetch_mix)

    n_in = w_in.shape[2]

    offs = [0]
    for s in IN_SPLITS:
        offs.append(offs[-1] + s)

    w_a_up_p = jnp.pad(ws["w_a_up"], ((0, LANES - LOWRANK), (0, 0)))
    b_a_up = ws["b_a_up"]

    def proj_in_fwd(hv, w, wa, ba):
        proj = jnp.concatenate([_raw_dot(hv, w[j], "nn") for j in range(N_DEV)], axis=1)
        parts = [proj[:, offs[i]:offs[i + 1]] for i in range(len(IN_SPLITS))]
        parts[4] = jnp.concatenate([parts[4], jnp.zeros((parts[4].shape[0], LANES - LOWRANK), F32)], axis=1)
        head_major = lambda a: jnp.stack([a[:, hs] for hs in _head_slices(GLA_DK)])
        return (head_major(parts[0]), head_major(parts[1]), *parts[2:], head_major(_log_decay(parts[4], wa, ba))), ()

    widths = [LANES if s == LOWRANK else s for s in IN_SPLITS]
    gla_hm = ((HEADS, GLA_DK), F32)
    q_hm, k_hm, v_a, g_a, a_low_p, x_m, o_pre, gate_a, gate_b, la_hm = _rowwise(
        "proj_in", proj_in_fwd, [h], [w_in, w_a_up_p, b_a_up],
        [gla_hm, gla_hm] + [(wd, BF16 if i == 2 else F32) for i, wd in enumerate(widths)][2:] + [gla_hm],
        deps=dep(fetch_up))
    gn = ws["g_gla_norm"]
    ml_w = HEADS * HEAD_W

    cw = ws["conv_w"]
    w_if_p = jnp.pad(ws["w_if"], ((0, 0), (0, LANES - 2 * HEADS)))
    pre_params = [cw[0:1], cw[1:2], cw[2:3], cw[3:4], ws["conv_b"],
                  _blockdiag_dense(ws["w_q_ml"]), _blockdiag_dense(ws["w_k_ml"]), _blockdiag_dense(ws["w_v_ml"]),
                  w_if_p[0:ml_w], w_if_p[ml_w:2 * ml_w], w_if_p[2 * ml_w:3 * ml_w],
                  jnp.pad(ws["b_if"], ((0, 0), (0, LANES - 2 * HEADS)))]
    xc, q_m, k_m, v_m, gl = _ml_pre_fwd(x_m, pre_params, tile=512)
    pass_mix = pass_on("w_pa", xc)
    li, lf = _gate_rows(gl[:, 0:HEADS]), _gate_rows(gl[:, HEADS:2 * HEADS])
    o_gla, s_prev, hc, c_prev, n_prev, m_prev = _recurrences_fwd(q_hm, k_hm, v_a, la_hm, q_m, k_m, v_m, li, lf,
                                                                 deps=dep(pass_mix))
    g_ml, skip = ws["g_ml_norm"], ws["ml_skip"]

    def branches_out(o, g, a, b, c_, ga, gb, n_, wa, gm, s, wb):
        ya_in = jnp.concatenate(_per_head(_gla_out, [o, g], [], [n_]), axis=1)
        ya = _raw_dot(ya_in, wa, "nn")
        hb = jnp.concatenate(_per_head(_ml_out, [a, b, c_], [gm, s]), axis=1)
        yb = _raw_dot(hb, wb, "nn")
        return (ya_in, ya, hb, yb, _merge(ga, gb, ya, yb)), ()

    ya_in, y_a, h_b, y_b, merged = _rowwise(
        "branches_out", branches_out, [o_gla, g_a, hc, o_pre, xc, gate_a, gate_b],
        [gn, weight("w_pa", hc), g_ml, skip, weight("w_pb", hc)],
        [(ml_w, BF16), (d, BF16), (ml_w, BF16), (d, BF16), (d, BF16)], tile=512)

    gpm, gpl, gpo = ws["g_post_mix"], ws["g_pre_mlp"], ws["g_post_mlp"]

    def proj_o_fwd(mg, xv, w, a, b):
        zv = _raw_dot(mg, w, "nn")
        return (zv, *_post_mix(xv, zv, a, b)), ()

    pass_up = pass_on("w_up", merged)
    z, x1, h2 = _rowwise("proj_o", proj_o_fwd, [merged, x], [weight("w_o", merged), gpm, gpl],
                         [(d, F32), (d, F32), (d, BF16)], tile=512, deps=dep(pass_up))
    w_up = weight("w_up", h2)
    w_down = weight("w_down", h2)
    d_ff = w_down.shape[0]

    def mlp_loss(h2v, x1v, tgtv, wu, wd, g):
        upv = jnp.concatenate([_raw_dot(h2v, wu[j], "nn") for j in range(wu.shape[0])], axis=1)
        uv = jnp.square(jnp.maximum(upv, 0.0))
        dnv = _raw_dot(uv, wd, "nn")
        loss, vjp = jax.vjp(lambda a, b, c_: _loss_rows(a, b, tgtv, c_), x1v, dnv, g)
        dx1, ddn, dg = vjp(jnp.ones((1, 1), F32))
        return (upv, uv, dx1, ddn), (jnp.broadcast_to(loss, (1, LANES)), dg)

    up, u, dx1_y, d_dn, loss, d_gpo = _rowwise("mlp_loss", mlp_loss, [h2, x1, tgt], [w_up, w_down, gpo],
                                               [(d_ff, BF16), (d_ff, BF16), (d, F32), (d, BF16)],
                                               [((1, LANES), F32), ((1, d), F32)])

    dw_down = _mm(u, d_dn, "tn", BF16, "mlp_down_dw", tm=512)

    def mlp_dx(ddn, upv, xv, zv, dx1, wd, wu, a, b):
        dup = (_raw_dot(ddn, wd, "nt") * (2.0 * jnp.maximum(upv.astype(F32), 0.0))).astype(BF16)
        ns = wu.shape[2]
        dh2 = sum(_raw_dot(dup[:, j * ns:(j + 1) * ns], wu[j], "nt") for j in range(wu.shape[0]))
        _, vjp = jax.vjp(_post_mix, xv, zv, a, b)
        dx, dz, da, db = vjp((dx1, dh2))
        return (dup, dx, dz), (da, db)

    d_up, dx_res, d_z, d_gpm, d_gpl = _rowwise("mlp_dx", mlp_dx, [d_dn, up, x, z, dx1_y], [w_down, w_up, gpm, gpl],
                                               [(d_ff, BF16), (d, F32), (d, BF16)], [((1, d), F32), ((1, d), F32)])
    dw_up = _mm_shard_cols(h2, [d_up], [d_up.shape[1]], w_up.shape[2], "mlp_up_dw", 0, d)
    sent_mlp = on_grads(dict(w_down=dw_down, w_up=dw_up))

    def branches_out_bwd(dz, ga, gb, ya, yb, o, g, a, b, c_, wo, wa, n_, wb, gm, s):
        d_ga_, d_gb_, d_ya_, d_yb_ = jax.vjp(_merge, ga, gb, ya, yb)[1](_raw_dot(dz, wo, "nt"))
        ct_a, ct_b = _raw_dot(d_ya_, wa, "nt"), _raw_dot(d_yb_, wb, "nt")
        parts_a, parts_b = [], []
        for hs in _head_slices(HEAD_W):
            parts_a.append(jax.vjp(_gla_out, o[:, hs], g[:, hs], n_)[1](ct_a[:, hs]))
            parts_b.append(jax.vjp(_ml_out, a[:, hs], b[:, hs], c_[:, hs], gm[:, hs], s[:, hs])[1](ct_b[:, hs]))
        cat = lambda parts, i: jnp.concatenate([p[i] for p in parts], axis=1)
        return ((d_ga_, d_gb_, d_ya_, d_yb_, cat(parts_a, 0), cat(parts_a, 1), cat(parts_b, 0), cat(parts_b, 1), cat(parts_b, 2)),
                (sum(p[2] for p in parts_a), cat(parts_b, 3), cat(parts_b, 4)))

    d_ga, d_gb, d_ya, d_yb, d_o, d_g_a, d_hc, d_opre, d_xc, d_gn, d_gml, d_skip = _rowwise(
        "branches_out_bwd", branches_out_bwd, [d_z, gate_a, gate_b, y_a, y_b, o_gla, g_a, hc, o_pre, xc],
        [weight("w_o", merged), weight("w_pa", hc), gn, weight("w_pb", hc), g_ml, skip],
        [(d, BF16)] * 4 + [(ml_w, F32), (ml_w, BF16), (ml_w, F32), (ml_w, BF16), (ml_w, F32)],
        [((1, HEAD_W), F32), ((1, ml_w), F32), ((1, ml_w), F32)], deps=dep(sent_mlp))
    dw_o, dw_pa, dw_pb = _mm_tn_whole([(merged, d_z), (ya_in, d_ya), (h_b, d_yb)], "mix_dw")
    sent_mix = on_grads(dict(w_o=dw_o, w_pa=dw_pa, w_pb=dw_pb))

    dq_hm, dk_hm, d_va, dla_hm, d_qm, d_km, d_vm, d_li, d_lf = _recurrences_bwd(
        q_hm, k_hm, v_a, la_hm, s_prev, d_o, q_m, k_m, v_m, li, lf, c_prev, n_prev, m_prev, d_hc, deps=dep(sent_mix))
    d_gl = jnp.concatenate([_gate_cols(d_li), _gate_cols(d_lf), jnp.zeros((t, LANES - 2 * HEADS), F32)], axis=1)
    pre_grads = _ml_pre_bwd(x_m, pre_params, [d_xc, d_qm, d_km, d_vm, d_gl], tile=512)
    d_xm = pre_grads[0]
    d_cw = jnp.concatenate(pre_grads[1:5], axis=0)
    d_cb = pre_grads[5]
    d_wq, d_wk, d_wv = _blockdiag_blocks(pre_grads[6:9])
    d_wif = jnp.concatenate(pre_grads[9:12], axis=0)[:, 0:2 * HEADS]
    d_bif = pre_grads[12][:, 0:2 * HEADS]

    def decay_bwd(al, ct, w, b):
        _, vjp = jax.vjp(_log_decay, al, w, b)
        dal, dw, db = vjp(jnp.concatenate([ct[hd] for hd in range(HEADS)], axis=1))
        return (dal,), (dw, db)

    d_alow_p, d_wa_p, d_ba = _rowwise("gla_decay_bwd", decay_bwd, [a_low_p, dla_hm], [w_a_up_p, b_a_up],
                                      [(LANES, BF16)], [(w_a_up_p.shape, F32), (b_a_up.shape, F32)])
    d_proj = [jnp.concatenate([_from_hm(dq_hm), _from_hm(dk_hm), d_va, d_g_a], axis=1), d_alow_p,
              jnp.concatenate([d_xm, d_opre, d_ga, d_gb], axis=1)]
    d_widths = [offs[4], LOWRANK, offs[9] - offs[5]]
    d_pieces = _shard_pieces(d_widths, n_in)
    small = dict(w_a_up=d_wa_p[0:LOWRANK], b_a_up=d_ba, g_gla_norm=d_gn, conv_w=d_cw, conv_b=d_cb,
                 w_q_ml=d_wq, w_k_ml=d_wk, w_v_ml=d_wv, w_if=d_wif, b_if=d_bif, ml_skip=d_skip, g_ml_norm=d_gml,
                 g_post_mix=d_gpm, g_pre_mlp=d_gpl, g_post_mlp=d_gpo)
    sent_small = on_small(small, loss)
    sent_in = sent_small
    for part in range(W_IN_PARTS):
        dw_part = _mm_shard_cols(h, d_proj, d_widths, n_in, "proj_in_dw_%d" % part, part, d // W_IN_PARTS, deps=dep(sent_in))
        sent_in = on_grads({"w_in#%d" % part: dw_part})

    def proj_in_dx(dp_a, dp_low, dp_b, xv, dres, w, g):
        dh = 0.0
        for s in range(N_DEV):
            for i, c_in, c_w, wd in d_pieces[s]:
                src = (dp_a, dp_low, dp_b)[i]
                cols = src.shape[1] - c_in if wd < LANES else wd
                dh = dh + _raw_dot(src[:, c_in:c_in + cols], w[s][:, c_w:c_w + cols], "nt")
        _, vjp = jax.vjp(_rms, xv, g)
        dx, dg = vjp(dh)
        return (dx + dres,), (dg,)

    grad_x, d_g1 = _rowwise("proj_in_dx", proj_in_dx, [*d_proj, x, dx_res], [w_in, g1], [(d, F32)], [((1, d), F32)],
                            deps=dep(sent_in))
    return grad_x, on_small(dict(g_pre_mix=d_g1), None)


BIG = ("w_in", "w_pa", "w_pb", "w_o", "w_up", "w_down")
MIX = ("w_o", "w_pa", "w_pb")
BIG_COL_SHARDED = ("w_in", "w_pa", "w_pb", "w_up")
SMALL_SHARDED = ("w_a_up", "conv_w", "w_if")
SMALL = ("g_pre_mix", "w_a_up", "b_a_up", "g_gla_norm", "conv_w", "conv_b", "w_q_ml", "w_k_ml", "w_v_ml", "w_if", "b_if",
         "ml_skip", "g_ml_norm", "g_post_mix", "g_pre_mlp", "g_post_mlp")
WEIGHTS = ("g_pre_mix", "w_in", "w_a_up", "b_a_up", "g_gla_norm", "conv_w", "conv_b", "w_q_ml", "w_k_ml", "w_v_ml", "w_if", "b_if",
           "ml_skip", "g_ml_norm", "w_pa", "w_pb", "w_o", "g_post_mix", "g_pre_mlp", "w_up", "w_down", "g_post_mlp")


def kernel(x, g_pre_mix, w_in, w_a_up, b_a_up, g_gla_norm, conv_w, conv_b, w_q_ml, w_k_ml, w_v_ml, w_if, b_if, ml_skip, g_ml_norm, w_pa, w_pb, w_o, g_post_mix, g_pre_mlp, w_up, w_down, g_post_mlp, loss_target, m_g_pre_mix, m_w_in, m_w_a_up, m_b_a_up, m_g_gla_norm, m_conv_w, m_conv_b, m_w_q_ml, m_w_k_ml, m_w_v_ml, m_w_if, m_b_if, m_ml_skip, m_g_ml_norm, m_w_pa, m_w_pb, m_w_o, m_g_post_mix, m_g_pre_mlp, m_w_up, m_w_down, m_g_post_mlp, v_g_pre_mix, v_w_in, v_w_a_up, v_b_a_up, v_g_gla_norm, v_conv_w, v_conv_b, v_w_q_ml, v_w_k_ml, v_w_v_ml, v_w_if, v_b_if, v_ml_skip, v_g_ml_norm, v_w_pa, v_w_pb, v_w_o, v_g_post_mix, v_g_pre_mlp, v_w_up, v_w_down, v_g_post_mlp):
    args = dict(locals())
    w = {n: args[n][0] for n in WEIGHTS}
    m = {n: args["m_" + n][0] for n in WEIGHTS}
    v = {n: args["v_" + n][0] for n in WEIGHTS}

    me_lin = _lin(_me())
    me_idx = jnp.reshape(me_lin, (1,)).astype(jnp.int32)

    def full_weight(n, g):
        if n in ("w_in", "w_up"):
            return g
        return _from_col_blocks(g) if n in BIG_COL_SHARDED else g.reshape(-1, g.shape[-1])

    def grad_parts(n, g):
        if n.partition("#")[0] in ("w_in", "w_up"):
            return g
        return (_col_blocks(g) if n in BIG_COL_SHARDED else g.reshape(N_DEV, -1, g.shape[-1])).astype(BF16)

    sharded_names = tuple(SMALL_SHARDED)
    narrow = {n: w[n].astype(BF16) for n in BIG}
    ready, pending, passing = {}, {}, {}
    first_state, _ = _copies_start(["gather"] * len(sharded_names) + ["gather_chips"],
                                   [_small_view(n, w[n]) for n in sharded_names] + [narrow["w_in"]], "allgather_start_first")

    def prefetch(group, after):
        state, token = _copies_start("gather_chips", [narrow[n] for n in group], "allgather_start_" + group[0], after)
        for n in group:
            pending[n] = (group, state)
        return token

    def pass_on(n, after):
        group, state = pending[n]
        shards, lands = _copies_wait(state, after, "allgather_wait_" + group[0])
        state, token = _copies_start("gather_pass", shards, "allgather_pass_" + group[0], lands=lands)
        for gn in group:
            passing[gn] = (group, state)
        return token

    def weight(n, after):
        if n not in ready:
            group, state = passing[n]
            _, lands = _copies_wait(state, after, "allgather_passed_" + group[0])
            for gn, land in zip(group, lands):
                ready[gn] = full_weight(gn, land)
        return ready[n]

    h = _first_norm(x[0], w["g_pre_mix"].reshape(1, -1))
    first_own, first_lands = _copies_wait(first_state, [h] + [narrow[n] for n in BIG if n != "w_in"], "allgather_wait_first")
    state, _ = _copies_start("gather_pass", first_own[-1:], "allgather_pass_w_in", lands=first_lands[-1:])
    passing["w_in"] = (("w_in",), state)
    ws = {n: (w[n].reshape(1, -1) if w[n].ndim == 1 else w[n]) for n in SMALL if n not in SMALL_SHARDED}
    for n, land in zip(sharded_names, first_lands):
        ws[n] = _small_unshard(n, land)

    sets, waiting_small = [], []

    def start_set(large, small):
        names = tuple(large)
        s_names, s_kinds, s_srcs = small if small else ((), [], [])
        state, token = _copies_start(s_kinds + ["exchange"] * len(names), s_srcs + [grad_parts(n, large[n]) for n in names],
                                     "exchange_start_" + (names + s_names)[0].replace("#", "_"))
        sets.append((names, s_names, s_kinds, state))
        return token

    def on_grads(grads):
        return start_set(grads, waiting_small.pop() if waiting_small else None)

    def on_small(small, loss):
        names = tuple(small)
        kinds = ["exchange" if n in SMALL_SHARDED else "gather" for n in names]
        srcs = [_small_shards(n, small[n]) if n in SMALL_SHARDED else _small_view(n, small[n]) for n in names]
        if loss is None:
            return start_set({}, (names, kinds, srcs))
        waiting_small.append((names, kinds + ["gather"], srcs + [loss]))
        return None

    grad_x, last_token = _local_step(x[0], h, loss_target[0], weight, ws, prefetch, pass_on, on_grads, on_small)

    out, chunks, sums, updated = {}, {}, [], []

    def finish_set(names, s_names, s_kinds, state, after):
        own, lands = _copies_wait(state, after, "exchange_wait_" + (names + s_names)[0].replace("#", "_"))
        ns = len(s_kinds)
        if s_names:
            k = len(s_names)
            upd = _small_update("adamw_small_" + s_names[0], me_idx, s_kinds[:k], lands[:k], own[:k],
                                *[[_small_view(n, d[n]) for n in s_names] for d in (w, m, v)],
                                sums=list(zip(lands[k:ns], own[k:ns])))
            for i, n in enumerate(s_names):
                out[n] = tuple(_small_unview(n, a, w[n].shape) for a in upd[4 * i:4 * i + 4])
            sums.extend(upd[4 * k:])
            updated.append(upd[1])
        if names == MIX:
            upd = _small_update("adamw_mix", me_idx, ["exchange"] * len(names), lands[ns:], own[ns:],
                                *[[d[n] for n in names] for d in (w, m, v)])
            for i, n in enumerate(names):
                out[n] = tuple(upd[4 * i:4 * i + 4])
            updated.append(upd[1])
            return
        for name, part, land in zip(names, own[ns:], lands[ns:]):
            n, _, chunk = name.partition("#")
            chunks.setdefault(n, []).append((land, part))
            if chunk in ("", str(W_IN_PARTS - 1)):
                got_lands, got_parts = zip(*chunks[n])
                if n == "w_in":
                    upd = _sum_adamw_cols(got_lands, got_parts, me_idx, *[_cols_view(d[n]) for d in (w, m, v)], "adamw_" + n)
                    out[n] = tuple(_from_cols_view(a, *w[n].shape) for a in upd)
                else:
                    out[n] = upd = _sum_adamw(got_lands, got_parts, me_idx, w[n], m[n], v[n], "adamw_" + n)
                updated.append(upd[1])

    for entry in sets[:-2] + [sets[-1], sets[-2]]:
        finish_set(*entry, [grad_x, last_token] + updated)
    loss_sum = sums[0]

    shaped = lambda a, n: a.reshape(args[n].shape)
    return (loss_sum[0, 0], grad_x[None],
            *[shaped(out[n][0], n) for n in WEIGHTS], *[shaped(out[n][1], n) for n in WEIGHTS],
            *[shaped(out[n][2], n) for n in WEIGHTS], *[shaped(out[n][3], n) for n in WEIGHTS])
```
